```python
import math
import jax, jax.numpy as jnp
from jax import lax
import numpy as np

D_MODEL = 2048
BATCH = 8
SEQ = 4096
DEPTH = 1

CHUNK = 128
A_GROUP_DIM = 128
A_GROUPS = (D_MODEL // 2) // A_GROUP_DIM
A_WIDTH = A_GROUPS * A_GROUP_DIM
HEAD_DIM = 64
B_HEADS = (D_MODEL // 2) // HEAD_DIM
B_KV_HEADS = 2
Q_PER_KV = B_HEADS // B_KV_HEADS
B_WIDTH = B_HEADS * HEAD_DIM
KV_WIDTH = B_KV_HEADS * HEAD_DIM
WINDOW = 128
BLOCK = 128
N_BUCKETS = 32
MAX_DISTANCE = 128
MIX_WIDTH = A_WIDTH + B_WIDTH
PROJ_WIDTH = 2 * A_WIDTH + B_WIDTH + 2 * KV_WIDTH
SPLITS = [A_WIDTH, 2 * A_WIDTH, 2 * A_WIDTH + B_WIDTH, 2 * A_WIDTH + B_WIDTH + KV_WIDTH]
D_FF = 4 * D_MODEL
EPS = 1e-5
NEG = -1e30

kernel_name = "hymba_gmlp_swa_sink_t5_sqrelu"


def rms_norm(x, g):
    xf = x.astype(jnp.float32)
    y = xf * lax.rsqrt(jnp.mean(xf * xf, axis=-1, keepdims=True) + EPS)
    return (y * g.astype(jnp.float32)).astype(x.dtype)


def layer_norm(x, g, b):
    xf = x.astype(jnp.float32)
    mu = jnp.mean(xf, axis=-1, keepdims=True)
    xc = xf - mu
    var = jnp.mean(xc * xc, axis=-1, keepdims=True)
    y = xc * lax.rsqrt(var + EPS) * g.astype(jnp.float32) + b.astype(jnp.float32)
    return y.astype(x.dtype)


def t5_relative_bias(table):
    i = jnp.arange(BLOCK)[:, None]
    j = jnp.arange(2 * BLOCK)[None, :]
    rel = jnp.maximum(i + BLOCK - j, 0)
    n_exact = N_BUCKETS // 2
    relf = jnp.maximum(rel, n_exact).astype(jnp.float32)
    large = n_exact + (jnp.log(relf / n_exact) / math.log(MAX_DISTANCE / n_exact)
                       * (N_BUCKETS - n_exact)).astype(jnp.int32)
    large = jnp.minimum(large, N_BUCKETS - 1)
    bucket = jnp.where(rel < n_exact, rel, large)
    return jnp.transpose(table.astype(jnp.float32)[bucket], (2, 0, 1))


def spatial_gating(u, v, ln_g, ln_b, w_s, b_s):
    Bsz, S = u.shape[:2]
    u = jax.nn.gelu(u)
    v = layer_norm(jax.nn.gelu(v), ln_g, ln_b)
    v = v.reshape(Bsz, S // CHUNK, CHUNK, A_GROUPS, A_GROUP_DIM)
    causal = jnp.tril(jnp.ones((CHUNK, CHUNK), dtype=bool))
    w = jnp.where(causal[None], w_s, jnp.zeros_like(w_s))
    mixed = jnp.einsum('gts,bcsgd->bctgd', w, v) + jnp.transpose(b_s)[None, None, :, :, None]
    return u * mixed.reshape(Bsz, S, A_GROUPS, A_GROUP_DIM)


def sliding_window_attention(q, k, v, sinks, rel_bias):
    Bsz, S = q.shape[:2]
    nb = S // BLOCK
    qb = q.reshape(Bsz, nb, BLOCK, B_KV_HEADS, Q_PER_KV, HEAD_DIM)

    def band(t):
        tp = jnp.pad(t, ((0, 0), (BLOCK, 0), (0, 0), (0, 0)))
        tp = tp.reshape(Bsz, nb + 1, BLOCK, B_KV_HEADS, HEAD_DIM)
        return jnp.concatenate([tp[:, :-1], tp[:, 1:]], axis=2)

    kb, vb = band(k), band(v)
    s = jnp.einsum('bnikgd,bnjkd->bnkgij', qb, kb).astype(jnp.float32) * (HEAD_DIM ** -0.5)
    s = s + rel_bias.reshape(B_KV_HEADS, Q_PER_KV, BLOCK, 2 * BLOCK)
    i = jnp.arange(BLOCK)[:, None]
    j = jnp.arange(2 * BLOCK)[None, :]
    rel = i + BLOCK - j
    in_window = (rel >= 0) & (rel < WINDOW)
    key_exists = (jnp.arange(nb)[:, None] > 0) | (jnp.arange(2 * BLOCK)[None, :] >= BLOCK)
    mask = in_window[None] & key_exists[:, None, :]
    s = jnp.where(mask[None, :, None, None], s, NEG)
    sink = sinks.astype(jnp.float32).reshape(B_KV_HEADS, Q_PER_KV)[None, None, :, :, None, None]
    m = jnp.maximum(jnp.max(s, axis=-1, keepdims=True), sink)
    p = jnp.exp(s - m)
    denom = jnp.sum(p, axis=-1, keepdims=True) + jnp.exp(sink - m)
    o = jnp.einsum('bnkgij,bnjkd->bnikgd', (p / denom).astype(v.dtype), vb)
    return o.reshape(Bsz, S, B_WIDTH)


def _fwd_setup_inputs(seed: int = 0) -> dict:
    key = jax.random.key(seed)
    ks = jax.random.split(key, 17)
    f32 = jnp.float32
    nrm = lambda k, shape, scale: jax.random.normal(k, shape, f32) * scale
    return {
        'x': nrm(ks[0], (BATCH, SEQ, D_MODEL), 1.0),
        'rel_bias_table': nrm(ks[1], (N_BUCKETS, B_HEADS), 0.5),
        'mix_norm_g': 1.0 + nrm(ks[2], (DEPTH, D_MODEL), 0.02),
        'w_in': nrm(ks[3], (DEPTH, D_MODEL, PROJ_WIDTH), D_MODEL ** -0.5),
        'gate_norm_g': 1.0 + nrm(ks[4], (DEPTH, A_GROUPS, A_GROUP_DIM), 0.02),
        'gate_norm_b': nrm(ks[5], (DEPTH, A_GROUPS, A_GROUP_DIM), 0.02),
        'w_spatial': nrm(ks[6], (DEPTH, A_GROUPS, CHUNK, CHUNK), CHUNK ** -0.5),
        'b_spatial': 1.0 + nrm(ks[7], (DEPTH, A_GROUPS, CHUNK), 0.1),
        'attn_sinks': nrm(ks[8], (DEPTH, B_HEADS), 0.5),
        'out_norm_a_g': 1.0 + nrm(ks[9], (DEPTH, A_WIDTH), 0.02),
        'out_norm_b_g': 1.0 + nrm(ks[10], (DEPTH, B_WIDTH), 0.02),
        'w_out': nrm(ks[11], (DEPTH, MIX_WIDTH, D_MODEL), MIX_WIDTH ** -0.5),
        'ffn_norm_g': 1.0 + nrm(ks[12], (DEPTH, D_MODEL), 0.02),
        'w_up': nrm(ks[13], (DEPTH, D_MODEL, D_FF), D_MODEL ** -0.5),
        'w_down': nrm(ks[14], (DEPTH, D_FF, D_MODEL), D_FF ** -0.5),
        'final_norm_g': 1.0 + nrm(ks[15], (D_MODEL,), 0.02),
    }


def _fwd_reference(x, rel_bias_table, mix_norm_g, w_in, gate_norm_g, gate_norm_b, w_spatial,
              b_spatial, attn_sinks, out_norm_a_g, out_norm_b_g, w_out, ffn_norm_g,
              w_up, w_down, final_norm_g):
    Bsz, S, _ = x.shape
    rel_bias = t5_relative_bias(rel_bias_table)
    h = x
    for layer in range(DEPTH):
        n = rms_norm(h, mix_norm_g[layer])
        proj = n @ w_in[layer]
        u, v, q, k, va = jnp.split(proj, SPLITS, axis=-1)
        a_out = spatial_gating(
            u.reshape(Bsz, S, A_GROUPS, A_GROUP_DIM),
            v.reshape(Bsz, S, A_GROUPS, A_GROUP_DIM),
            gate_norm_g[layer], gate_norm_b[layer], w_spatial[layer], b_spatial[layer],
        ).reshape(Bsz, S, A_WIDTH)
        b_out = sliding_window_attention(
            q.reshape(Bsz, S, B_HEADS, HEAD_DIM),
            k.reshape(Bsz, S, B_KV_HEADS, HEAD_DIM),
            va.reshape(Bsz, S, B_KV_HEADS, HEAD_DIM),
            attn_sinks[layer], rel_bias,
        )
        mixed = jnp.concatenate(
            [rms_norm(a_out, out_norm_a_g[layer]), rms_norm(b_out, out_norm_b_g[layer])], axis=-1)
        h = h + mixed @ w_out[layer]
        z = jax.nn.relu(rms_norm(h, ffn_norm_g[layer]) @ w_up[layer])
        h = h + (z * z) @ w_down[layer]
    return rms_norm(h, final_norm_g)


import jax as _jax
import jax.numpy as _jnp

TWIN_FORMAT = 'train_step'
FWD_PARAMS = ['x', 'rel_bias_table', 'mix_norm_g', 'w_in', 'gate_norm_g', 'gate_norm_b', 'w_spatial', 'b_spatial', 'attn_sinks', 'out_norm_a_g', 'out_norm_b_g', 'w_out', 'ffn_norm_g', 'w_up', 'w_down', 'final_norm_g']
TWIN_WEIGHTS = ['rel_bias_table', 'mix_norm_g', 'w_in', 'gate_norm_g', 'gate_norm_b', 'w_spatial', 'b_spatial', 'attn_sinks', 'out_norm_a_g', 'out_norm_b_g', 'w_out', 'ffn_norm_g', 'w_up', 'w_down', 'final_norm_g']
TWIN_DIFF_INPUT = 'x'
TWIN_INPUTS = ['x', 'rel_bias_table', 'mix_norm_g', 'w_in', 'gate_norm_g', 'gate_norm_b', 'w_spatial', 'b_spatial', 'attn_sinks', 'out_norm_a_g', 'out_norm_b_g', 'w_out', 'ffn_norm_g', 'w_up', 'w_down', 'final_norm_g', 'loss_target', 'm_rel_bias_table', 'm_mix_norm_g', 'm_w_in', 'm_gate_norm_g', 'm_gate_norm_b', 'm_w_spatial', 'm_b_spatial', 'm_attn_sinks', 'm_out_norm_a_g', 'm_out_norm_b_g', 'm_w_out', 'm_ffn_norm_g', 'm_w_up', 'm_w_down', 'm_final_norm_g', 'v_rel_bias_table', 'v_mix_norm_g', 'v_w_in', 'v_gate_norm_g', 'v_gate_norm_b', 'v_w_spatial', 'v_b_spatial', 'v_attn_sinks', 'v_out_norm_a_g', 'v_out_norm_b_g', 'v_w_out', 'v_ffn_norm_g', 'v_w_up', 'v_w_down', 'v_final_norm_g']
TWIN_OUTPUTS = ['loss', 'grad_x', 'grad_rel_bias_table', 'grad_mix_norm_g', 'grad_w_in', 'grad_gate_norm_g', 'grad_gate_norm_b', 'grad_w_spatial', 'grad_b_spatial', 'grad_attn_sinks', 'grad_out_norm_a_g', 'grad_out_norm_b_g', 'grad_w_out', 'grad_ffn_norm_g', 'grad_w_up', 'grad_w_down', 'grad_final_norm_g', 'delta_rel_bias_table', 'delta_mix_norm_g', 'delta_w_in', 'delta_gate_norm_g', 'delta_gate_norm_b', 'delta_w_spatial', 'delta_b_spatial', 'delta_attn_sinks', 'delta_out_norm_a_g', 'delta_out_norm_b_g', 'delta_w_out', 'delta_ffn_norm_g', 'delta_w_up', 'delta_w_down', 'delta_final_norm_g', 'new_m_rel_bias_table', 'new_m_mix_norm_g', 'new_m_w_in', 'new_m_gate_norm_g', 'new_m_gate_norm_b', 'new_m_w_spatial', 'new_m_b_spatial', 'new_m_attn_sinks', 'new_m_out_norm_a_g', 'new_m_out_norm_b_g', 'new_m_w_out', 'new_m_ffn_norm_g', 'new_m_w_up', 'new_m_w_down', 'new_m_final_norm_g', 'new_v_rel_bias_table', 'new_v_mix_norm_g', 'new_v_w_in', 'new_v_gate_norm_g', 'new_v_gate_norm_b', 'new_v_w_spatial', 'new_v_b_spatial', 'new_v_attn_sinks', 'new_v_out_norm_a_g', 'new_v_out_norm_b_g', 'new_v_w_out', 'new_v_ffn_norm_g', 'new_v_w_up', 'new_v_w_down', 'new_v_final_norm_g']
TWIN_LEAF_KINDS = {'loss': 'loss', 'grad_x': 'grad_x', 'grad_rel_bias_table': 'grad_w', 'grad_mix_norm_g': 'grad_w', 'grad_w_in': 'grad_w', 'grad_gate_norm_g': 'grad_w', 'grad_gate_norm_b': 'grad_w', 'grad_w_spatial': 'grad_w', 'grad_b_spatial': 'grad_w', 'grad_attn_sinks': 'grad_w', 'grad_out_norm_a_g': 'grad_w', 'grad_out_norm_b_g': 'grad_w', 'grad_w_out': 'grad_w', 'grad_ffn_norm_g': 'grad_w', 'grad_w_up': 'grad_w', 'grad_w_down': 'grad_w', 'grad_final_norm_g': 'grad_w', 'delta_rel_bias_table': 'delta_w', 'delta_mix_norm_g': 'delta_w', 'delta_w_in': 'delta_w', 'delta_gate_norm_g': 'delta_w', 'delta_gate_norm_b': 'delta_w', 'delta_w_spatial': 'delta_w', 'delta_b_spatial': 'delta_w', 'delta_attn_sinks': 'delta_w', 'delta_out_norm_a_g': 'delta_w', 'delta_out_norm_b_g': 'delta_w', 'delta_w_out': 'delta_w', 'delta_ffn_norm_g': 'delta_w', 'delta_w_up': 'delta_w', 'delta_w_down': 'delta_w', 'delta_final_norm_g': 'delta_w', 'new_m_rel_bias_table': 'new_m', 'new_m_mix_norm_g': 'new_m', 'new_m_w_in': 'new_m', 'new_m_gate_norm_g': 'new_m', 'new_m_gate_norm_b': 'new_m', 'new_m_w_spatial': 'new_m', 'new_m_b_spatial': 'new_m', 'new_m_attn_sinks': 'new_m', 'new_m_out_norm_a_g': 'new_m', 'new_m_out_norm_b_g': 'new_m', 'new_m_w_out': 'new_m', 'new_m_ffn_norm_g': 'new_m', 'new_m_w_up': 'new_m', 'new_m_w_down': 'new_m', 'new_m_final_norm_g': 'new_m', 'new_v_rel_bias_table': 'new_v', 'new_v_mix_norm_g': 'new_v', 'new_v_w_in': 'new_v', 'new_v_gate_norm_g': 'new_v', 'new_v_gate_norm_b': 'new_v', 'new_v_w_spatial': 'new_v', 'new_v_b_spatial': 'new_v', 'new_v_attn_sinks': 'new_v', 'new_v_out_norm_a_g': 'new_v', 'new_v_out_norm_b_g': 'new_v', 'new_v_w_out': 'new_v', 'new_v_ffn_norm_g': 'new_v', 'new_v_w_up': 'new_v', 'new_v_w_down': 'new_v', 'new_v_final_norm_g': 'new_v'}


def _forward(args):
    return _fwd_reference(*[args[k] for k in FWD_PARAMS])


def _output_shape():
    def fwd():
        inp = _fwd_setup_inputs(0)
        return _fwd_reference(*[inp[k] for k in FWD_PARAMS])
    out = _jax.eval_shape(fwd)
    return out.shape, out.dtype

N_MICROBATCH = 1
ADAM_LR = 0.001
ADAM_B1 = 0.9
ADAM_B2 = 0.999
ADAM_EPS = 1e-08
ADAM_WD = 0.01
ADAM_STEP = 10
PER_EXAMPLE_BATCH_AXIS = {'x': 0, 'loss_target': 0}
SHARED_INPUTS = []
_WEIGHT_DTYPES = {'rel_bias_table': _jnp.float32, 'mix_norm_g': _jnp.float32, 'w_in': _jnp.float32, 'gate_norm_g': _jnp.float32, 'gate_norm_b': _jnp.float32, 'w_spatial': _jnp.float32, 'b_spatial': _jnp.float32, 'attn_sinks': _jnp.float32, 'out_norm_a_g': _jnp.float32, 'out_norm_b_g': _jnp.float32, 'w_out': _jnp.float32, 'ffn_norm_g': _jnp.float32, 'w_up': _jnp.float32, 'w_down': _jnp.float32, 'final_norm_g': _jnp.float32}
MOMENT_SCALE = {'rel_bias_table': 8.696848e-02, 'mix_norm_g': 1.021674e-01, 'w_in': 7.756319e-02, 'gate_norm_g': 3.757252e-02, 'gate_norm_b': 3.966960e-02, 'w_spatial': 3.755358e-02, 'b_spatial': 5.542218e-02, 'attn_sinks': 1.604979e-02, 'out_norm_a_g': 7.229820e-02, 'out_norm_b_g': 6.703237e-02, 'w_out': 6.995028e-02, 'ffn_norm_g': 6.891638e-02, 'w_up': 3.400064e-02, 'w_down': 6.873756e-02, 'final_norm_g': 1.611873e+01}


def _to_microbatches(a, axis):
    t = _jnp.moveaxis(a, axis, 0)
    t = t.reshape((N_MICROBATCH, t.shape[0] // N_MICROBATCH) + t.shape[1:])
    return _jnp.moveaxis(t, 1, axis + 1)


def setup_inputs(seed: int = 0) -> dict:
    inp = _fwd_setup_inputs(seed)
    key = _jax.random.fold_in(_jax.random.key(seed), 7919)
    shape, _ = _output_shape()
    out = dict(inp)
    out["loss_target"] = _jax.random.normal(_jax.random.fold_in(key, 0), shape, _jnp.float32)
    for i, name in enumerate(TWIN_WEIGHTS):
        w = inp[name].astype(_jnp.float32)
        if MOMENT_SCALE is None:
            s = _jnp.sqrt(_jnp.mean(_jnp.square(w)) + 1e-30)
        else:
            s = MOMENT_SCALE[name]
        km, kv = _jax.random.split(_jax.random.fold_in(key, i + 1))
        out[name] = w
        out["m_" + name] = s * _jax.random.normal(km, w.shape, _jnp.float32)
        out["v_" + name] = (s * s) * _jax.random.uniform(kv, w.shape, _jnp.float32, 0.5, 1.5)
    if N_MICROBATCH > 1:
        for name, axis in PER_EXAMPLE_BATCH_AXIS.items():
            out[name] = _to_microbatches(out[name], axis)
    return {'x': out['x'], 'rel_bias_table': out['rel_bias_table'], 'mix_norm_g': out['mix_norm_g'], 'w_in': out['w_in'], 'gate_norm_g': out['gate_norm_g'], 'gate_norm_b': out['gate_norm_b'], 'w_spatial': out['w_spatial'], 'b_spatial': out['b_spatial'], 'attn_sinks': out['attn_sinks'], 'out_norm_a_g': out['out_norm_a_g'], 'out_norm_b_g': out['out_norm_b_g'], 'w_out': out['w_out'], 'ffn_norm_g': out['ffn_norm_g'], 'w_up': out['w_up'], 'w_down': out['w_down'], 'final_norm_g': out['final_norm_g'], 'loss_target': out['loss_target'], 'm_rel_bias_table': out['m_rel_bias_table'], 'm_mix_norm_g': out['m_mix_norm_g'], 'm_w_in': out['m_w_in'], 'm_gate_norm_g': out['m_gate_norm_g'], 'm_gate_norm_b': out['m_gate_norm_b'], 'm_w_spatial': out['m_w_spatial'], 'm_b_spatial': out['m_b_spatial'], 'm_attn_sinks': out['m_attn_sinks'], 'm_out_norm_a_g': out['m_out_norm_a_g'], 'm_out_norm_b_g': out['m_out_norm_b_g'], 'm_w_out': out['m_w_out'], 'm_ffn_norm_g': out['m_ffn_norm_g'], 'm_w_up': out['m_w_up'], 'm_w_down': out['m_w_down'], 'm_final_norm_g': out['m_final_norm_g'], 'v_rel_bias_table': out['v_rel_bias_table'], 'v_mix_norm_g': out['v_mix_norm_g'], 'v_w_in': out['v_w_in'], 'v_gate_norm_g': out['v_gate_norm_g'], 'v_gate_norm_b': out['v_gate_norm_b'], 'v_w_spatial': out['v_w_spatial'], 'v_b_spatial': out['v_b_spatial'], 'v_attn_sinks': out['v_attn_sinks'], 'v_out_norm_a_g': out['v_out_norm_a_g'], 'v_out_norm_b_g': out['v_out_norm_b_g'], 'v_w_out': out['v_w_out'], 'v_ffn_norm_g': out['v_ffn_norm_g'], 'v_w_up': out['v_w_up'], 'v_w_down': out['v_w_down'], 'v_final_norm_g': out['v_final_norm_g']}


def _loss(weights, diff, rest, loss_target):
    with _jax.named_scope("forward"):
        args = {**rest, TWIN_DIFF_INPUT: diff, **{k: w.astype(_WEIGHT_DTYPES[k]) for k, w in weights.items()}}
        y = _forward(args)
    with _jax.named_scope("loss_head"):
        err = _jnp.square(y.astype(_jnp.float32) - loss_target)
        return 0.5 * _jnp.sum(_jnp.mean(err, axis=-1)) if err.ndim else 0.5 * err


def _adamw(w, g, m, v):
    m = ADAM_B1 * m + (1.0 - ADAM_B1) * g
    v = ADAM_B2 * v + (1.0 - ADAM_B2) * _jnp.square(g)
    m_hat = m / (1.0 - ADAM_B1 ** ADAM_STEP)
    v_hat = v / (1.0 - ADAM_B2 ** ADAM_STEP)
    delta = -ADAM_LR * (m_hat / (_jnp.sqrt(v_hat) + ADAM_EPS) + ADAM_WD * w)
    return delta, m, v


def reference(x, rel_bias_table, mix_norm_g, w_in, gate_norm_g, gate_norm_b, w_spatial, b_spatial, attn_sinks, out_norm_a_g, out_norm_b_g, w_out, ffn_norm_g, w_up, w_down, final_norm_g, loss_target, m_rel_bias_table, m_mix_norm_g, m_w_in, m_gate_norm_g, m_gate_norm_b, m_w_spatial, m_b_spatial, m_attn_sinks, m_out_norm_a_g, m_out_norm_b_g, m_w_out, m_ffn_norm_g, m_w_up, m_w_down, m_final_norm_g, v_rel_bias_table, v_mix_norm_g, v_w_in, v_gate_norm_g, v_gate_norm_b, v_w_spatial, v_b_spatial, v_attn_sinks, v_out_norm_a_g, v_out_norm_b_g, v_w_out, v_ffn_norm_g, v_w_up, v_w_down, v_final_norm_g):
    given = dict(x=x, rel_bias_table=rel_bias_table, mix_norm_g=mix_norm_g, w_in=w_in, gate_norm_g=gate_norm_g, gate_norm_b=gate_norm_b, w_spatial=w_spatial, b_spatial=b_spatial, attn_sinks=attn_sinks, out_norm_a_g=out_norm_a_g, out_norm_b_g=out_norm_b_g, w_out=w_out, ffn_norm_g=ffn_norm_g, w_up=w_up, w_down=w_down, final_norm_g=final_norm_g, loss_target=loss_target, m_rel_bias_table=m_rel_bias_table, m_mix_norm_g=m_mix_norm_g, m_w_in=m_w_in, m_gate_norm_g=m_gate_norm_g, m_gate_norm_b=m_gate_norm_b, m_w_spatial=m_w_spatial, m_b_spatial=m_b_spatial, m_attn_sinks=m_attn_sinks, m_out_norm_a_g=m_out_norm_a_g, m_out_norm_b_g=m_out_norm_b_g, m_w_out=m_w_out, m_ffn_norm_g=m_ffn_norm_g, m_w_up=m_w_up, m_w_down=m_w_down, m_final_norm_g=m_final_norm_g, v_rel_bias_table=v_rel_bias_table, v_mix_norm_g=v_mix_norm_g, v_w_in=v_w_in, v_gate_norm_g=v_gate_norm_g, v_gate_norm_b=v_gate_norm_b, v_w_spatial=v_w_spatial, v_b_spatial=v_b_spatial, v_attn_sinks=v_attn_sinks, v_out_norm_a_g=v_out_norm_a_g, v_out_norm_b_g=v_out_norm_b_g, v_w_out=v_w_out, v_ffn_norm_g=v_ffn_norm_g, v_w_up=v_w_up, v_w_down=v_w_down, v_final_norm_g=v_final_norm_g)
    weights = {n: given[n] for n in TWIN_WEIGHTS}
    shared = {n: given[n] for n in SHARED_INPUTS}
    per_example = {n: given[n] for n in ['x']}
    grad_fn = _jax.value_and_grad(_loss, argnums=(0, 1))

    def one_microbatch(ex, loss_target):
        ex = dict(ex)
        diff = ex.pop(TWIN_DIFF_INPUT)
        return grad_fn(weights, diff, {**shared, **ex}, loss_target)

    if N_MICROBATCH == 1:
        loss, (grad_w, grad_x) = one_microbatch(per_example, given["loss_target"])
    else:
        def body(carry, xs):
            loss_sum, grad_sum = carry
            l_k, (gw_k, gx_k) = one_microbatch(xs[0], xs[1])
            with _jax.named_scope("update"):
                return (loss_sum + l_k, _jax.tree.map(_jnp.add, grad_sum, gw_k)), gx_k

        init = (_jnp.zeros((), _jnp.float32), _jax.tree.map(_jnp.zeros_like, weights))
        (loss, grad_w), grad_x = _jax.lax.scan(body, init, (per_example, given["loss_target"]))
    with _jax.named_scope("update"):
        delta_w, new_m, new_v = {}, {}, {}
        for n in TWIN_WEIGHTS:
            delta_w[n], new_m[n], new_v[n] = _adamw(weights[n], grad_w[n], given["m_" + n], given["v_" + n])
    return (loss, grad_x, *[grad_w[n] for n in TWIN_WEIGHTS], *[delta_w[n] for n in TWIN_WEIGHTS],
            *[new_m[n] for n in TWIN_WEIGHTS], *[new_v[n] for n in TWIN_WEIGHTS])
```

```python
import functools
import math

import numpy as np
import jax
import jax.numpy as jnp
from jax import lax
from jax.experimental import pallas as pl
from jax.experimental.pallas import tpu as pltpu

F32 = jnp.float32
BF16 = jnp.bfloat16

D_MODEL = 2048
CHUNK = 128
A_GROUPS = 8
A_WIDTH = 1024
HEAD_DIM = 64
B_HEADS = 16
Q_PER_KV = 8
B_WIDTH = 1024
KV_WIDTH = 128
PROJ_WIDTH = 3328
D_FF = 8192
N_BUCKETS = 32
EPS = 1e-5
NEG = -1e30
SCALE = HEAD_DIM ** -0.5
N_CHIPS = 4
N_DEV = 8

ADAM_LR = 0.001
ADAM_B1 = 0.9
ADAM_B2 = 0.999
ADAM_EPS = 1e-08
ADAM_WD = 0.01
ADAM_STEP = 10

VMEM_LIMIT = 56 * 1024 * 1024
MESH = pl.DeviceIdType.MESH


def _bucket_thresholds():
    d = np.arange(CHUNK)
    n_exact = N_BUCKETS // 2
    relf = np.maximum(d, n_exact).astype(np.float64)
    large = n_exact + (np.log(relf / n_exact) / math.log(CHUNK / n_exact) * (N_BUCKETS - n_exact)).astype(np.int32)
    bucket = np.where(d < n_exact, d, np.minimum(large, N_BUCKETS - 1))
    return [int(np.min(d[bucket >= b])) for b in range(1, N_BUCKETS)]


BUCKET_THR = _bucket_thresholds()


def _params(sem=None):
    return pltpu.CompilerParams(dimension_semantics=sem, vmem_limit_bytes=VMEM_LIMIT)


def _gelu(x):
    c = math.sqrt(2.0 / math.pi)
    return 0.5 * x * (1.0 + jnp.tanh(c * (x + 0.044715 * (x * x * x))))


def _gelu_and_grad(x):
    c = math.sqrt(2.0 / math.pi)
    x2 = x * x
    t = jnp.tanh(c * (x + 0.044715 * (x2 * x)))
    g = 0.5 * x * (1.0 + t)
    dg = 0.5 * (1.0 + t) + 0.5 * x * (1.0 - t * t) * (c * (1.0 + 3.0 * 0.044715 * x2))
    return g, dg


def _dot(a, b):
    return jnp.dot(a, b, preferred_element_type=F32)


def _dot_nt(a, b):
    return lax.dot_general(a, b, (((1,), (1,)), ((), ())), preferred_element_type=F32)


def _dot_tn(a, b):
    return lax.dot_general(a, b, (((0,), (0,)), ((), ())), preferred_element_type=F32)


def _rms_bwd(dn, h, g):
    r = lax.rsqrt(jnp.mean(h * h, axis=-1, keepdims=True) + EPS)
    w = dn * g
    dh = r * w - h * ((r * r * r) * jnp.mean(w * h, axis=-1, keepdims=True))
    return dh, r


def _norm_matmul(a, g, b, *, tm, tn, name):
    T, K = a.shape
    if b.ndim == 3:
        per = b.shape[2] // tn
        N = b.shape[0] * b.shape[2]
        b_spec = pl.BlockSpec((None, K, tn), lambda i, j: (j // per, 0, j % per))
    else:
        N = b.shape[1]
        b_spec = pl.BlockSpec((K, tn), lambda i, j: (0, j))

    def body(a_ref, g_ref, b_ref, n_ref, o_ref):
        @pl.when(pl.program_id(1) == 0)
        def _():
            xf = a_ref[...]
            r = lax.rsqrt(jnp.mean(xf * xf, axis=-1, keepdims=True) + EPS)
            n_ref[...] = ((xf * r) * g_ref[...]).astype(BF16)
        o_ref[...] = _dot(n_ref[...], b_ref[...])

    return pl.pallas_call(
        body, name=name, grid=(T // tm, N // tn),
        in_specs=[pl.BlockSpec((tm, K), lambda i, j: (i, 0)), pl.BlockSpec((1, K), lambda i, j: (0, 0)), b_spec],
        out_specs=[pl.BlockSpec((tm, K), lambda i, j: (i, 0)), pl.BlockSpec((tm, tn), lambda i, j: (i, j))],
        out_shape=[jax.ShapeDtypeStruct((T, K), BF16), jax.ShapeDtypeStruct((T, N), F32)],
        compiler_params=_params(("parallel", "arbitrary")),
    )(a, g, b)


def _sq_relu_bf16(z):
    z = jnp.maximum(z, 0.0)
    return (z * z).astype(BF16)


def _to_bf16(v):
    return v.astype(BF16)


def _matmul_res(a, b, res, *, tm, tn, tk, prologue, name):
    T, K = a.shape
    N = b.shape[1]

    def body(a_ref, b_ref, res_ref, o_ref):
        k = pl.program_id(2)
        p = _dot(prologue(a_ref[...]), b_ref[...])

        @pl.when(k == 0)
        def _():
            o_ref[...] = res_ref[...] + p

        @pl.when(k > 0)
        def _():
            o_ref[...] += p

    return pl.pallas_call(
        body, name=name, grid=(T // tm, N // tn, K // tk),
        in_specs=[pl.BlockSpec((tm, tk), lambda i, j, k: (i, k)), pl.BlockSpec((tk, tn), lambda i, j, k: (k, j)),
                  pl.BlockSpec((tm, tn), lambda i, j, k: (i, j))],
        out_specs=pl.BlockSpec((tm, tn), lambda i, j, k: (i, j)),
        out_shape=jax.ShapeDtypeStruct((T, N), F32),
        compiler_params=_params(("parallel", "parallel", "arbitrary")),
    )(a, b, res)


def _matmul_nt(a, b, *, tm, tn, tk, name, extra=None, epilogue=None, out_dtype=F32):
    T, K = a.shape
    if b.ndim == 3:
        per = b.shape[2] // tk
        N = b.shape[1]
        b_spec = pl.BlockSpec((None, tn, tk), lambda i, j, k: (k // per, j, k % per))
    else:
        N = b.shape[0]
        b_spec = pl.BlockSpec((tn, tk), lambda i, j, k: (j, k))
    nk = K // tk
    assert out_dtype == F32 or nk == 1
    in_specs = [pl.BlockSpec((tm, tk), lambda i, j, k: (i, k)), b_spec]
    args = [a, b]
    if extra is not None:
        in_specs.append(pl.BlockSpec((tm, tn), lambda i, j, k: (i, j)))
        args.append(extra)

    def body(*refs):
        a_ref, b_ref = refs[0], refs[1]
        o_ref = refs[-1]
        p = _dot_nt(a_ref[...].astype(BF16), b_ref[...])
        if nk == 1:
            if epilogue is not None:
                p = epilogue(p, refs[2][...])
            o_ref[...] = p.astype(out_dtype)
        else:
            k = pl.program_id(2)

            @pl.when(k == 0)
            def _():
                o_ref[...] = p

            @pl.when(k > 0)
            def _():
                o_ref[...] += p

    return pl.pallas_call(
        body, name=name, grid=(T // tm, N // tn, nk),
        in_specs=in_specs,
        out_specs=pl.BlockSpec((tm, tn), lambda i, j, k: (i, j)),
        out_shape=jax.ShapeDtypeStruct((T, N), out_dtype),
        compiler_params=_params(("parallel", "parallel", "arbitrary")),
    )(*args)


def _matmul_tn(a, b, *, tmo, tn, tk, name, a_prologue=_to_bf16, shards=1):
    T, M = a.shape
    N = b.shape[1]
    if shards > 1:
        per = (N // shards) // tn
        out_spec = pl.BlockSpec((None, tmo, tn), lambda i, j, k: (j // per, i, j % per))
        out_shape = jax.ShapeDtypeStruct((shards, M, N // shards), F32)
    else:
        out_spec = pl.BlockSpec((tmo, tn), lambda i, j, k: (i, j))
        out_shape = jax.ShapeDtypeStruct((M, N), F32)

    def body(a_ref, b_ref, o_ref):
        k = pl.program_id(2)
        p = _dot_tn(a_prologue(a_ref[...]), b_ref[...].astype(BF16))

        @pl.when(k == 0)
        def _():
            o_ref[...] = p

        @pl.when(k > 0)
        def _():
            o_ref[...] += p

    return pl.pallas_call(
        body, name=name, grid=(M // tmo, N // tn, T // tk),
        in_specs=[pl.BlockSpec((tk, tmo), lambda i, j, k: (k, i)), pl.BlockSpec((tk, tn), lambda i, j, k: (k, j))],
        out_specs=out_spec, out_shape=out_shape,
        compiler_params=_params(("parallel", "parallel", "arbitrary")),
    )(a, b)


def _loss_bwd(h2, tgt, g, *, tm):
    T, D = h2.shape

    def body(h_ref, t_ref, g_ref, dh_ref, dg_ref, loss_ref):
        @pl.when(pl.program_id(0) == 0)
        def _():
            dg_ref[...] = jnp.zeros_like(dg_ref)
            loss_ref[...] = jnp.zeros_like(loss_ref)
        h = h_ref[...]
        gg = g_ref[...]
        r = lax.rsqrt(jnp.mean(h * h, axis=-1, keepdims=True) + EPS)
        hn = h * r
        err = hn * gg - t_ref[...]
        loss_ref[...] += 0.5 * jnp.sum(jnp.mean(err * err, axis=-1, keepdims=True), axis=0, keepdims=True)
        dy = err * (1.0 / D)
        dg_ref[...] += jnp.sum(dy * hn, axis=0, keepdims=True)
        w = dy * gg
        dh_ref[...] = r * w - h * ((r * r * r) * jnp.mean(w * h, axis=-1, keepdims=True))

    return pl.pallas_call(
        body, name="loss_bwd", grid=(T // tm,),
        in_specs=[pl.BlockSpec((tm, D), lambda i: (i, 0)), pl.BlockSpec((tm, D), lambda i: (i, 0)),
                  pl.BlockSpec((1, D), lambda i: (0, 0))],
        out_specs=[pl.BlockSpec((tm, D), lambda i: (i, 0)), pl.BlockSpec((1, D), lambda i: (0, 0)),
                   pl.BlockSpec((1, 1), lambda i: (0, 0))],
        out_shape=[jax.ShapeDtypeStruct((T, D), F32), jax.ShapeDtypeStruct((1, D), F32),
                   jax.ShapeDtypeStruct((1, 1), F32)],
        compiler_params=_params(("arbitrary",)),
    )(h2, tgt, g)


def _rms_bwd_res(dn, h, g, dres, *, tm, name):
    T, D = h.shape

    def body(dn_ref, h_ref, g_ref, dres_ref, dh_ref, dg_ref):
        @pl.when(pl.program_id(0) == 0)
        def _():
            dg_ref[...] = jnp.zeros_like(dg_ref)
        h_ = h_ref[...]
        dn_ = dn_ref[...]
        dh, r = _rms_bwd(dn_, h_, g_ref[...])
        dg_ref[...] += jnp.sum(dn_ * (h_ * r), axis=0, keepdims=True)
        dh_ref[...] = dres_ref[...] + dh

    return pl.pallas_call(
        body, name=name, grid=(T // tm,),
        in_specs=[pl.BlockSpec((tm, D), lambda i: (i, 0)), pl.BlockSpec((tm, D), lambda i: (i, 0)),
                  pl.BlockSpec((1, D), lambda i: (0, 0)), pl.BlockSpec((tm, D), lambda i: (i, 0))],
        out_specs=[pl.BlockSpec((tm, D), lambda i: (i, 0)), pl.BlockSpec((1, D), lambda i: (0, 0))],
        out_shape=[jax.ShapeDtypeStruct((T, D), F32), jax.ShapeDtypeStruct((1, D), F32)],
        compiler_params=_params(("arbitrary",)),
    )(dn, h, g, dres)


def _rel_distance():
    i = lax.broadcasted_iota(jnp.int32, (CHUNK, 2 * CHUNK), 0)
    j = lax.broadcasted_iota(jnp.int32, (CHUNK, 2 * CHUNK), 1)
    return i + CHUNK - j


def _bias_build(table):
    def body(tab_ref, o_ref):
        rel = _rel_distance()
        ge = [rel >= t for t in BUCKET_THR]
        for h in range(B_HEADS):
            cur = jnp.full((CHUNK, 2 * CHUNK), tab_ref[0, h], F32)
            for b in range(1, N_BUCKETS):
                cur = jnp.where(ge[b - 1], tab_ref[b, h], cur)
            o_ref[h] = cur

    return pl.pallas_call(
        body, name="bias_build",
        in_specs=[pl.BlockSpec(memory_space=pltpu.SMEM)],
        out_specs=pl.BlockSpec(memory_space=pltpu.VMEM),
        out_shape=jax.ShapeDtypeStruct((B_HEADS, CHUNK, 2 * CHUNK), F32),
    )(table)


def _bias_grad(dbias):
    def body(db_ref, o_ref, acc_ref):
        rel = _rel_distance()
        lo = [0] + BUCKET_THR
        hi = BUCKET_THR + [CHUNK]
        for b in range(N_BUCKETS):
            m = (rel >= lo[b]) & (rel < hi[b])
            for h in range(B_HEADS):
                row = b * B_HEADS + h
                acc_ref[row:row + 1, :] = jnp.sum(jnp.where(m, db_ref[h], 0.0), axis=0, keepdims=True)
        o_ref[...] = jnp.sum(acc_ref[...], axis=1, keepdims=True)

    return pl.pallas_call(
        body, name="bias_grad",
        in_specs=[pl.BlockSpec(memory_space=pltpu.VMEM)],
        out_specs=pl.BlockSpec(memory_space=pltpu.VMEM),
        out_shape=jax.ShapeDtypeStruct((N_BUCKETS * B_HEADS, 1), F32),
        scratch_shapes=[pltpu.VMEM((N_BUCKETS * B_HEADS, 2 * CHUNK), F32)],
    )(dbias)


def _causal_mask():
    t = lax.broadcasted_iota(jnp.int32, (CHUNK, CHUNK), 0)
    s = lax.broadcasted_iota(jnp.int32, (CHUNK, CHUNK), 1)
    return s <= t


def _band_mask(n):
    rel = _rel_distance()
    j = lax.broadcasted_iota(jnp.int32, (CHUNK, 2 * CHUNK), 1)
    return (rel >= 0) & (rel < CHUNK) & ((n > 0) | (j >= CHUNK))


def _gate_forward(u, v, lg, lb, wc, bs):
    ug = _gelu(u)
    vg = _gelu(v)
    mu = jnp.mean(vg, axis=-1, keepdims=True)
    xc = vg - mu
    rstd = lax.rsqrt(jnp.mean(xc * xc, axis=-1, keepdims=True) + EPS)
    xhat = xc * rstd
    vl = (xhat * lg + lb).astype(BF16)
    mixed = _dot(wc, vl) + bs
    return ug, xhat, rstd, vl, mixed


def _softmax_band(q, kband, bias, mask, sink):
    s = _dot_nt(q, kband) * SCALE + bias
    s = jnp.where(mask, s, NEG)
    m = jnp.maximum(jnp.max(s, axis=-1, keepdims=True), sink)
    p = jnp.exp(s - m)
    e_sink = jnp.exp(sink - m)
    inv = 1.0 / (jnp.sum(p, axis=-1, keepdims=True) + e_sink)
    return p * inv, e_sink * inv


def _mixer_fwd(proj, lg, lb, wsp, bs_col, sinks, bias, ga, gb):
    T = proj.shape[0]
    nb = T // CHUNK

    def body(u_ref, v_ref, q_ref, kvc_ref, kvp_ref, lg_ref, lb_ref, w_ref, bs_ref, sink_ref, bias_ref,
             ga_ref, gb_ref, mixed_ref, ab_ref):
        n = pl.program_id(0)
        causal = _causal_mask()
        ssq = jnp.zeros((CHUNK, 1), F32)
        for g in range(A_GROUPS):
            cols = slice(g * CHUNK, (g + 1) * CHUNK)
            wc = jnp.where(causal, w_ref[g], 0.0).astype(BF16)
            ug, _, _, _, mixed = _gate_forward(u_ref[:, cols], v_ref[:, cols], lg_ref[g:g + 1, :], lb_ref[g:g + 1, :],
                                               wc, bs_ref[g])
            a = ug * mixed
            ab_ref[:, cols] = a
            ssq = ssq + jnp.sum(a * a, axis=-1, keepdims=True)
        ra = lax.rsqrt(ssq * (1.0 / A_WIDTH) + EPS)
        mixed_ref[:, :A_WIDTH] = ((ab_ref[:, :A_WIDTH] * ra) * ga_ref[...]).astype(BF16)

        mask = _band_mask(n)
        kvc = kvc_ref[...].astype(BF16)
        kvp = kvp_ref[...].astype(BF16)
        band = jnp.concatenate([kvp, kvc], axis=0)
        ssq = jnp.zeros((CHUNK, 1), F32)
        for h in range(B_HEADS):
            kv = h // Q_PER_KV
            q = q_ref[:, h * HEAD_DIM:(h + 1) * HEAD_DIM].astype(BF16)
            kband = band[:, kv * HEAD_DIM:(kv + 1) * HEAD_DIM]
            vband = band[:, KV_WIDTH + kv * HEAD_DIM:KV_WIDTH + (kv + 1) * HEAD_DIM]
            p, _ = _softmax_band(q, kband, bias_ref[h], mask, sink_ref[0, h])
            o = _dot(p.astype(BF16), vband)
            ab_ref[:, A_WIDTH + h * HEAD_DIM:A_WIDTH + (h + 1) * HEAD_DIM] = o
            ssq = ssq + jnp.sum(o * o, axis=-1, keepdims=True)
        rb = lax.rsqrt(ssq * (1.0 / B_WIDTH) + EPS)
        mixed_ref[:, A_WIDTH:] = ((ab_ref[:, A_WIDTH:] * rb) * gb_ref[...]).astype(BF16)

    full = lambda *shape: pl.BlockSpec(shape, lambda n: (0,) * len(shape))
    return pl.pallas_call(
        body, name="mixer_fwd", grid=(nb,),
        in_specs=[pl.BlockSpec((CHUNK, A_WIDTH), lambda n: (n, 0)),
                  pl.BlockSpec((CHUNK, A_WIDTH), lambda n: (n, 1)),
                  pl.BlockSpec((CHUNK, B_WIDTH), lambda n: (n, 2)),
                  pl.BlockSpec((CHUNK, 2 * KV_WIDTH), lambda n: (n, 12)),
                  pl.BlockSpec((CHUNK, 2 * KV_WIDTH), lambda n: (jnp.maximum(n - 1, 0), 12)),
                  full(A_GROUPS, CHUNK), full(A_GROUPS, CHUNK), full(A_GROUPS, CHUNK, CHUNK), full(A_GROUPS, CHUNK, 1),
                  pl.BlockSpec(memory_space=pltpu.SMEM), full(B_HEADS, CHUNK, 2 * CHUNK),
                  full(1, A_WIDTH), full(1, B_WIDTH)],
        out_specs=[pl.BlockSpec((CHUNK, D_MODEL), lambda n: (n, 0)), pl.BlockSpec((CHUNK, D_MODEL), lambda n: (n, 0))],
        out_shape=[jax.ShapeDtypeStruct((T, D_MODEL), BF16), jax.ShapeDtypeStruct((T, D_MODEL), F32)],
        compiler_params=_params(("parallel",)),
    )(proj, proj, proj, proj, proj, lg, lb, wsp, bs_col, sinks, bias, ga, gb)


def _gmlp_bwd(proj, ab, dmixed, ga, lg, lb, wsp, bs_col):
    T = proj.shape[0]
    nb = T // CHUNK

    def body(u_ref, v_ref, a_ref, dna_ref, ga_ref, lg_ref, lb_ref, w_ref, bs_ref,
             dp_ref, dga_ref, dw_ref, dbs_ref, dlg_ref, dlb_ref):
        @pl.when(pl.program_id(0) == 0)
        def _():
            for r in (dga_ref, dw_ref, dbs_ref, dlg_ref, dlb_ref):
                r[...] = jnp.zeros_like(r)
        causal = _causal_mask()
        a_all = a_ref[...]
        dna = dna_ref[...]
        da_all, ra = _rms_bwd(dna, a_all, ga_ref[...])
        dga_ref[...] += jnp.sum(dna * (a_all * ra), axis=0, keepdims=True)
        for g in range(A_GROUPS):
            cols = slice(g * CHUNK, (g + 1) * CHUNK)
            wc = jnp.where(causal, w_ref[g], 0.0).astype(BF16)
            lgg = lg_ref[g:g + 1, :]
            u = u_ref[:, cols]
            v = v_ref[:, cols]
            ug, xhat, rstd, vl, mixed = _gate_forward(u, v, lgg, lb_ref[g:g + 1, :], wc, bs_ref[g])
            da = da_all[:, cols]
            dug = da * mixed
            dmg = da * ug
            dmg_b = dmg.astype(BF16)
            dbs_ref[g] += jnp.sum(dmg, axis=-1, keepdims=True)
            dw_ref[g] += jnp.where(causal, _dot_nt(dmg_b, vl), 0.0)
            dvl = _dot_tn(wc, dmg_b)
            dlg_ref[g:g + 1, :] += jnp.sum(dvl * xhat, axis=0, keepdims=True)
            dlb_ref[g:g + 1, :] += jnp.sum(dvl, axis=0, keepdims=True)
            dxh = dvl * lgg
            dvg = rstd * (dxh - jnp.mean(dxh, axis=-1, keepdims=True)
                          - xhat * jnp.mean(dxh * xhat, axis=-1, keepdims=True))
            _, gu = _gelu_and_grad(u)
            _, gv = _gelu_and_grad(v)
            dp_ref[:, cols] = (dug * gu).astype(BF16)
            dp_ref[:, A_WIDTH + g * CHUNK:A_WIDTH + (g + 1) * CHUNK] = (dvg * gv).astype(BF16)

    full = lambda *shape: pl.BlockSpec(shape, lambda n: (0,) * len(shape))
    return pl.pallas_call(
        body, name="gmlp_bwd", grid=(nb,),
        in_specs=[pl.BlockSpec((CHUNK, A_WIDTH), lambda n: (n, 0)),
                  pl.BlockSpec((CHUNK, A_WIDTH), lambda n: (n, 1)),
                  pl.BlockSpec((CHUNK, A_WIDTH), lambda n: (n, 0)),
                  pl.BlockSpec((CHUNK, A_WIDTH), lambda n: (n, 0)),
                  full(1, A_WIDTH), full(A_GROUPS, CHUNK), full(A_GROUPS, CHUNK), full(A_GROUPS, CHUNK, CHUNK),
                  full(A_GROUPS, CHUNK, 1)],
        out_specs=[pl.BlockSpec((CHUNK, 2 * A_WIDTH), lambda n: (n, 0)),
                   full(1, A_WIDTH), full(A_GROUPS, CHUNK, CHUNK), full(A_GROUPS, CHUNK, 1),
                   full(A_GROUPS, CHUNK), full(A_GROUPS, CHUNK)],
        out_shape=[jax.ShapeDtypeStruct((T, 2 * A_WIDTH), BF16),
                   jax.ShapeDtypeStruct((1, A_WIDTH), F32), jax.ShapeDtypeStruct((A_GROUPS, CHUNK, CHUNK), F32),
                   jax.ShapeDtypeStruct((A_GROUPS, CHUNK, 1), F32), jax.ShapeDtypeStruct((A_GROUPS, CHUNK), F32),
                   jax.ShapeDtypeStruct((A_GROUPS, CHUNK), F32)],
        compiler_params=_params(("arbitrary",)),
    )(proj, proj, ab, dmixed, ga, lg, lb, wsp, bs_col)


def _attn_bwd(proj, ab, dmixed, gb, sinks, bias):
    T = proj.shape[0]
    nb = T // CHUNK
    qn = lambda n: jnp.minimum(n, nb - 1)

    def body(q_ref, kvc_ref, kvp_ref, o_ref, dnb_ref, gb_ref, sink_ref, bias_ref,
             dq_ref, dkv_ref, dgb_ref, dsink_ref, dbias_ref, carry_ref, sacc_ref):
        n = pl.program_id(0)

        @pl.when(n == 0)
        def _():
            carry_ref[...] = jnp.zeros_like(carry_ref)
            sacc_ref[...] = jnp.zeros_like(sacc_ref)
            dgb_ref[...] = jnp.zeros_like(dgb_ref)
            dbias_ref[...] = jnp.zeros_like(dbias_ref)

        @pl.when(n < nb)
        def _():
            mask = _band_mask(n)
            o_all = o_ref[...]
            dnb = dnb_ref[...]
            do_all, rb = _rms_bwd(dnb, o_all, gb_ref[...])
            dgb_ref[...] += jnp.sum(dnb * (o_all * rb), axis=0, keepdims=True)
            band = jnp.concatenate([kvp_ref[...].astype(BF16), kvc_ref[...].astype(BF16)], axis=0)
            dkv_parts = []
            for kv in range(B_HEADS // Q_PER_KV):
                kband = band[:, kv * HEAD_DIM:(kv + 1) * HEAD_DIM]
                vband = band[:, KV_WIDTH + kv * HEAD_DIM:KV_WIDTH + (kv + 1) * HEAD_DIM]
                dk = jnp.zeros((2 * CHUNK, HEAD_DIM), F32)
                dv = jnp.zeros((2 * CHUNK, HEAD_DIM), F32)
                for hq in range(Q_PER_KV):
                    h = kv * Q_PER_KV + hq
                    hc = slice(h * HEAD_DIM, (h + 1) * HEAD_DIM)
                    q = q_ref[:, hc].astype(BF16)
                    p, p_sink = _softmax_band(q, kband, bias_ref[h], mask, sink_ref[0, h])
                    do = do_all[:, hc].astype(BF16)
                    dp = _dot_nt(do, vband)
                    delta = jnp.sum(p * dp, axis=-1, keepdims=True)
                    ds = p * (dp - delta)
                    sacc_ref[:, h:h + 1] += -(p_sink * delta)
                    dbias_ref[h] += ds
                    dsr = (ds * SCALE).astype(BF16)
                    dq_ref[:, hc] = _dot(dsr, kband).astype(BF16)
                    dk = dk + _dot_tn(dsr, q)
                    dv = dv + _dot_tn(p.astype(BF16), do)
                dkv_parts.append((dk, dv))
            dband = jnp.concatenate([dkv_parts[0][0], dkv_parts[1][0], dkv_parts[0][1], dkv_parts[1][1]], axis=1)
            dkv_ref[...] = (carry_ref[...] + dband[:CHUNK]).astype(BF16)
            carry_ref[...] = dband[CHUNK:]

        @pl.when(n == nb)
        def _():
            dkv_ref[...] = carry_ref[...].astype(BF16)
            dsink_ref[...] = jnp.sum(sacc_ref[...], axis=0, keepdims=True)

    full = lambda *shape: pl.BlockSpec(shape, lambda n: (0,) * len(shape))
    return pl.pallas_call(
        body, name="attn_bwd", grid=(nb + 1,),
        in_specs=[pl.BlockSpec((CHUNK, B_WIDTH), lambda n: (qn(n), 2)),
                  pl.BlockSpec((CHUNK, 2 * KV_WIDTH), lambda n: (qn(n), 12)),
                  pl.BlockSpec((CHUNK, 2 * KV_WIDTH), lambda n: (jnp.maximum(qn(n) - 1, 0), 12)),
                  pl.BlockSpec((CHUNK, B_WIDTH), lambda n: (qn(n), 1)),
                  pl.BlockSpec((CHUNK, B_WIDTH), lambda n: (qn(n), 1)),
                  full(1, B_WIDTH), pl.BlockSpec(memory_space=pltpu.SMEM), full(B_HEADS, CHUNK, 2 * CHUNK)],
        out_specs=[pl.BlockSpec((CHUNK, B_WIDTH), lambda n: (qn(n), 0)),
                   pl.BlockSpec((CHUNK, 2 * KV_WIDTH), lambda n: (jnp.maximum(n - 1, 0), 0)),
                   full(1, B_WIDTH), full(1, B_HEADS), full(B_HEADS, CHUNK, 2 * CHUNK)],
        out_shape=[jax.ShapeDtypeStruct((T, B_WIDTH), BF16), jax.ShapeDtypeStruct((T, 2 * KV_WIDTH), BF16),
                   jax.ShapeDtypeStruct((1, B_WIDTH), F32), jax.ShapeDtypeStruct((1, B_HEADS), F32),
                   jax.ShapeDtypeStruct((B_HEADS, CHUNK, 2 * CHUNK), F32)],
        scratch_shapes=[pltpu.VMEM((CHUNK, 2 * KV_WIDTH), F32), pltpu.VMEM((CHUNK, B_HEADS), F32)],
        compiler_params=_params(("arbitrary",)),
    )(proj, proj, proj, ab, dmixed, gb, sinks, bias)


def _sq_relu_grad(acc, z):
    return acc * (2.0 * jnp.maximum(z, 0.0))


def _local_step(x, tgt, sp, win, wo, wu, wd):
    T = x.shape[0]
    tm = min(512, T)
    tk = min(512, T)
    lg = sp["gate_norm_g"].reshape(A_GROUPS, CHUNK)
    lb = sp["gate_norm_b"].reshape(A_GROUPS, CHUNK)
    wsp = sp["w_spatial"].reshape(A_GROUPS, CHUNK, CHUNK)
    bs_col = sp["b_spatial"].reshape(A_GROUPS, CHUNK, 1)
    sinks = sp["attn_sinks"].reshape(1, B_HEADS)
    ga = sp["out_norm_a_g"].reshape(1, A_WIDTH)
    gb = sp["out_norm_b_g"].reshape(1, B_WIDTH)
    g1 = sp["mix_norm_g"].reshape(1, D_MODEL)
    g2 = sp["ffn_norm_g"].reshape(1, D_MODEL)
    gf = sp["final_norm_g"].reshape(1, D_MODEL)

    bias = _bias_build(sp["rel_bias_table"])
    n1, proj = _norm_matmul(x, g1, win, tm=tm, tn=PROJ_WIDTH // 2, name="in_proj")
    mixed, ab = _mixer_fwd(proj, lg, lb, wsp, bs_col, sinks, bias, ga, gb)
    h1 = _matmul_res(mixed, wo, x, tm=tm, tn=1024, tk=D_MODEL, prologue=_to_bf16, name="out_proj")
    n2, zp = _norm_matmul(h1, g2, wu, tm=tm, tn=1024, name="up_proj")
    h2 = _matmul_res(zp, wd, h1, tm=tm, tn=1024, tk=2048, prologue=_sq_relu_bf16, name="down_proj")

    dh2, dgf, loss = _loss_bwd(h2, tgt, gf, tm=tm)
    dzp = _matmul_nt(dh2, wd, tm=tm, tn=1024, tk=D_MODEL, name="bwd_dz", extra=zp, epilogue=_sq_relu_grad,
                     out_dtype=BF16)
    dwd = _matmul_tn(zp, dh2, tmo=1024, tn=1024, tk=tk, name="grad_w_down", a_prologue=_sq_relu_bf16)
    dwu = _matmul_tn(n2, dzp, tmo=1024, tn=1024, tk=tk, name="grad_w_up", shards=N_CHIPS)
    dn2 = _matmul_nt(dzp, wu, tm=tm, tn=1024, tk=2048, name="bwd_dn2")
    dh1, dg2 = _rms_bwd_res(dn2, h1, g2, dh2, tm=tm, name="ffn_norm_bwd")
    dwo = _matmul_tn(mixed, dh1, tmo=1024, tn=1024, tk=tk, name="grad_w_out")
    dmixed = _matmul_nt(dh1, wo, tm=tm, tn=1024, tk=D_MODEL, name="bwd_dmixed")
    duv, dga, dwsp, dbs, dlg, dlb = _gmlp_bwd(proj, ab, dmixed, ga, lg, lb, wsp, bs_col)
    dq, dkv, dgb, dsinks, dbias = _attn_bwd(proj, ab, dmixed, gb, sinks, bias)
    dtable = _bias_grad(dbias)
    dproj = jnp.concatenate([duv, dq, dkv], axis=1)
    dwin = _matmul_tn(n1, dproj, tmo=1024, tn=PROJ_WIDTH // 2, tk=tk, name="grad_w_in")
    dn1 = _matmul_nt(dproj, win, tm=tm, tn=1024, tk=PROJ_WIDTH, name="bwd_dn1")
    dx, dg1 = _rms_bwd_res(dn1, x, g1, dh1, tm=tm, name="mix_norm_bwd")

    small = {
        "rel_bias_table": dtable.reshape(N_BUCKETS, B_HEADS), "mix_norm_g": dg1, "gate_norm_g": dlg, "gate_norm_b": dlb,
        "w_spatial": dwsp, "b_spatial": dbs, "attn_sinks": dsinks, "out_norm_a_g": dga, "out_norm_b_g": dgb,
        "ffn_norm_g": dg2, "final_norm_g": dgf,
    }
    return loss, dx, (dwin, dwo, dwu, dwd), small


def _place():
    x, y, c = lax.axis_index("x"), lax.axis_index("y"), lax.axis_index("c")
    chips = [(1 - x, y), (x, 1 - y), (1 - x, 1 - y)]
    return x, y, c, chips


def _remote(src, dst, send_sem, recv_sem, to):
    return pltpu.make_async_remote_copy(src_ref=src, dst_ref=dst, send_sem=send_sem, recv_sem=recv_sem,
                                        device_id=to, device_id_type=MESH)


def _cast_bf16(w, *, tm, name):
    R, C = w.shape

    def body(w_ref, o_ref):
        o_ref[...] = w_ref[...].astype(BF16)

    return pl.pallas_call(
        body, name=name, grid=(R // tm,),
        in_specs=[pl.BlockSpec((tm, C), lambda i: (i, 0))], out_specs=pl.BlockSpec((tm, C), lambda i: (i, 0)),
        out_shape=jax.ShapeDtypeStruct((R, C), BF16), compiler_params=_params(("parallel",)),
    )(w)


def _gather_weights(shards):
    nw = len(shards)

    def body(*refs):
        srcs, fulls = refs[:nw], refs[nw:2 * nw]
        send_sems, recv_sems, local_sems = refs[2 * nw:]
        x, y, c, chips = _place()
        me = 2 * x + y
        pending = []
        for w in range(nw):
            cp = pltpu.make_async_copy(srcs[w], fulls[w].at[me], local_sems.at[w])
            cp.start()
            pending.append(cp)
        sends = []
        for w in range(nw):
            hr = srcs[w].shape[0] // 2
            rows = pl.ds(c * hr, hr)
            for j, chip in enumerate(chips):
                cp = _remote(srcs[w].at[rows, :], fulls[w].at[me, rows, :], send_sems.at[6 * w + j],
                             recv_sems.at[6 * w + j], (*chip, c))
                cp.start()
                sends.append(cp)
        for w in range(nw):
            hr = srcs[w].shape[0] // 2
            rows = pl.ds(c * hr, hr)
            for j, chip in enumerate(chips):
                landed = fulls[w].at[2 * chip[0] + chip[1], rows, :]
                _remote(landed, landed, send_sems.at[6 * w + j], recv_sems.at[6 * w + j], (x, y, c)).wait_recv()
                cp = _remote(landed, landed, send_sems.at[6 * w + 3 + j], recv_sems.at[6 * w + 3 + j], (x, y, 1 - c))
                cp.start()
                sends.append(cp)
        for w in range(nw):
            hr = srcs[w].shape[0] // 2
            rows = pl.ds((1 - c) * hr, hr)
            for j, chip in enumerate(chips):
                other = fulls[w].at[2 * chip[0] + chip[1], rows, :]
                _remote(other, other, send_sems.at[6 * w + 3 + j], recv_sems.at[6 * w + 3 + j], (x, y, c)).wait_recv()
        for cp in sends:
            cp.wait_send()
        for cp in pending:
            cp.wait()

    any_spec = pl.BlockSpec(memory_space=pl.ANY)
    return pl.pallas_call(
        body, name="gather_weights",
        in_specs=[any_spec] * nw, out_specs=[any_spec] * nw,
        out_shape=[jax.ShapeDtypeStruct((N_CHIPS,) + s.shape, s.dtype) for s in shards],
        scratch_shapes=[pltpu.SemaphoreType.DMA((6 * nw,)), pltpu.SemaphoreType.DMA((6 * nw,)),
                        pltpu.SemaphoreType.DMA((nw,))],
    )(*shards)


def _sibling_halves(grads):
    nw = len(grads)

    def body(*refs):
        gs, outs = refs[:nw], refs[nw:2 * nw]
        send_sems, recv_sems = refs[2 * nw:]
        x, y, c, _ = _place()
        cps = []
        for w in range(nw):
            hr = gs[w].shape[1] // 2
            cp = _remote(gs[w].at[:, pl.ds((1 - c) * hr, hr), :], outs[w], send_sems.at[w], recv_sems.at[w],
                         (x, y, 1 - c))
            cp.start()
            cps.append(cp)
        for cp in cps:
            cp.wait()

    any_spec = pl.BlockSpec(memory_space=pl.ANY)
    return pl.pallas_call(
        body, name="rs_sibling_halves",
        in_specs=[any_spec] * nw, out_specs=[any_spec] * nw,
        out_shape=[jax.ShapeDtypeStruct((g.shape[0], g.shape[1] // 2, g.shape[2]), g.dtype) for g in grads],
        scratch_shapes=[pltpu.SemaphoreType.DMA((nw,)), pltpu.SemaphoreType.DMA((nw,))],
    )(*grads)


def _core_index():
    return lax.axis_index("c").astype(jnp.int32).reshape(1)


def _chip_index():
    return (2 * lax.axis_index("x") + lax.axis_index("y")).astype(jnp.int32).reshape(1)


def _pair_sum_bf16(g, got, *, tm, name):
    S, R, C = g.shape
    hr = R // 2
    nt = hr // tm

    def body(c_ref, g_ref, got_ref, o_ref):
        del c_ref
        o_ref[...] = (g_ref[...] + got_ref[...]).astype(BF16)

    return pl.pallas_call(
        body, name=name,
        grid_spec=pltpu.PrefetchScalarGridSpec(
            num_scalar_prefetch=1, grid=(S, nt),
            in_specs=[pl.BlockSpec((None, tm, C), lambda s, i, c: (s, c[0] * nt + i, 0)),
                      pl.BlockSpec((None, tm, C), lambda s, i, c: (s, i, 0))],
            out_specs=pl.BlockSpec((None, tm, C), lambda s, i, c: (s, i, 0))),
        out_shape=jax.ShapeDtypeStruct((S, hr, C), BF16),
        compiler_params=_params(("parallel", "parallel")),
    )(_core_index(), g, got)


def _scatter_to_owners(pairs):
    nw = len(pairs)

    def body(*refs):
        qs, outs = refs[:nw], refs[nw:2 * nw]
        send_sems, recv_sems = refs[2 * nw:]
        x, y, c, chips = _place()
        cps = []
        for w in range(nw):
            for j, chip in enumerate(chips):
                cp = _remote(qs[w].at[2 * chip[0] + chip[1]], outs[w].at[j], send_sems.at[3 * w + j],
                             recv_sems.at[3 * w + j], (*chip, c))
                cp.start()
                cps.append(cp)
        for cp in cps:
            cp.wait()

    any_spec = pl.BlockSpec(memory_space=pl.ANY)
    return pl.pallas_call(
        body, name="rs_scatter_to_owners",
        in_specs=[any_spec] * nw, out_specs=[any_spec] * nw,
        out_shape=[jax.ShapeDtypeStruct((3,) + q.shape[1:], q.dtype) for q in pairs],
        scratch_shapes=[pltpu.SemaphoreType.DMA((3 * nw,)), pltpu.SemaphoreType.DMA((3 * nw,))],
    )(*pairs)


def _owner_sum(g, got, others, *, tm, name):
    S, R, C = g.shape
    hr = R // 2
    nt = hr // tm

    def body(idx_ref, g_ref, got_ref, o_ref_in, out_ref):
        del idx_ref
        acc = g_ref[...] + got_ref[...]
        for j in range(3):
            acc = acc + o_ref_in[j].astype(F32)
        out_ref[...] = acc

    return pl.pallas_call(
        body, name=name,
        grid_spec=pltpu.PrefetchScalarGridSpec(
            num_scalar_prefetch=1, grid=(nt,),
            in_specs=[pl.BlockSpec((None, tm, C), lambda i, p: (p[1], p[0] * nt + i, 0)),
                      pl.BlockSpec((None, tm, C), lambda i, p: (p[1], i, 0)),
                      pl.BlockSpec((3, tm, C), lambda i, p: (0, i, 0))],
            out_specs=pl.BlockSpec((tm, C), lambda i, p: (i, 0))),
        out_shape=jax.ShapeDtypeStruct((hr, C), F32),
        compiler_params=_params(("parallel",)),
    )(jnp.concatenate([_core_index(), _chip_index()]), g, got, others)


def _join_halves(halves):
    nw = len(halves)

    def body(*refs):
        hs, outs = refs[:nw], refs[nw:2 * nw]
        send_sems, recv_sems, local_sems = refs[2 * nw:]
        x, y, c, _ = _place()
        cps = []
        for w in range(nw):
            hr = hs[w].shape[0]
            mine = outs[w].at[pl.ds(c * hr, hr), :]
            lc = pltpu.make_async_copy(hs[w], mine, local_sems.at[w])
            lc.start()
            cp = _remote(hs[w], mine, send_sems.at[w], recv_sems.at[w], (x, y, 1 - c))
            cp.start()
            cps.append((lc, cp))
        for w, (lc, cp) in enumerate(cps):
            hr = hs[w].shape[0]
            theirs = outs[w].at[pl.ds((1 - c) * hr, hr), :]
            _remote(theirs, theirs, send_sems.at[w], recv_sems.at[w], (x, y, c)).wait_recv()
            cp.wait_send()
            lc.wait()

    any_spec = pl.BlockSpec(memory_space=pl.ANY)
    return pl.pallas_call(
        body, name="rs_join_halves",
        in_specs=[any_spec] * nw, out_specs=[any_spec] * nw,
        out_shape=[jax.ShapeDtypeStruct((2 * h.shape[0], h.shape[1]), h.dtype) for h in halves],
        scratch_shapes=[pltpu.SemaphoreType.DMA((nw,)), pltpu.SemaphoreType.DMA((nw,)), pltpu.SemaphoreType.DMA((nw,))],
    )(*halves)


def _all_reduce_small(packed):
    R, C = packed.shape

    def body(in_ref, out_ref, slots, send_sems, recv_sems):
        x, y, c, _ = _place()
        me = 4 * x + 2 * y + c
        cps = []
        for k in range(1, N_DEV):
            p = (me + k) % N_DEV
            cp = _remote(in_ref, slots.at[me], send_sems.at[k - 1], recv_sems.at[k - 1], (p // 4, (p // 2) % 2, p % 2))
            cp.start()
            cps.append(cp)
        slots[me] = in_ref[...]
        for k in range(1, N_DEV):
            src = (me + N_DEV - k) % N_DEV
            _remote(in_ref, slots.at[src], send_sems.at[k - 1], recv_sems.at[k - 1], (x, y, c)).wait_recv()
        for cp in cps:
            cp.wait_send()
        acc = slots[0]
        for d in range(1, N_DEV):
            acc = acc + slots[d]
        out_ref[...] = acc

    vmem = pl.BlockSpec(memory_space=pltpu.VMEM)
    return pl.pallas_call(
        body, name="all_reduce_small", in_specs=[vmem], out_specs=vmem,
        out_shape=jax.ShapeDtypeStruct((R, C), F32),
        scratch_shapes=[pltpu.VMEM((N_DEV, R, C), F32), pltpu.SemaphoreType.DMA((N_DEV - 1,)),
                        pltpu.SemaphoreType.DMA((N_DEV - 1,))],
        compiler_params=_params(),
    )(packed)


def _adamw_math(w, g, m, v):
    m = ADAM_B1 * m + (1.0 - ADAM_B1) * g
    v = ADAM_B2 * v + (1.0 - ADAM_B2) * (g * g)
    m_hat = m / (1.0 - ADAM_B1 ** ADAM_STEP)
    v_hat = v / (1.0 - ADAM_B2 ** ADAM_STEP)
    delta = -ADAM_LR * (m_hat / (jnp.sqrt(v_hat) + ADAM_EPS) + ADAM_WD * w)
    return delta, m, v


def _adamw(w, g, m, v, *, tm, name):
    R, C = w.shape

    def body(w_ref, g_ref, m_ref, v_ref, d_ref, nm_ref, nv_ref):
        d_ref[...], nm_ref[...], nv_ref[...] = _adamw_math(w_ref[...], g_ref[...], m_ref[...], v_ref[...])

    spec = pl.BlockSpec((tm, C), lambda i: (i, 0))
    return pl.pallas_call(
        body, name=name, grid=(R // tm,), in_specs=[spec] * 4, out_specs=[spec] * 3,
        out_shape=[jax.ShapeDtypeStruct((R, C), F32)] * 3, compiler_params=_params(("parallel",)),
    )(w, g, m, v)


SMALL = ["rel_bias_table", "mix_norm_g", "gate_norm_g", "gate_norm_b", "w_spatial", "b_spatial", "attn_sinks",
         "out_norm_a_g", "out_norm_b_g", "ffn_norm_g", "final_norm_g"]
LARGE = ["w_in", "w_out", "w_up", "w_down"]
WEIGHTS = ["rel_bias_table", "mix_norm_g", "w_in", "gate_norm_g", "gate_norm_b", "w_spatial", "b_spatial", "attn_sinks",
           "out_norm_a_g", "out_norm_b_g", "w_out", "ffn_norm_g", "w_up", "w_down", "final_norm_g"]
PACK_UNIT = 8 * 128


def _pack(parts):
    rows = []
    for p in parts:
        flat = p.reshape(-1)
        pad = (-flat.shape[0]) % PACK_UNIT
        rows.append(jnp.pad(flat, (0, pad)).reshape(-1, 128))
    return jnp.concatenate(rows, axis=0)


def _unpack(packed, like):
    out, row = [], 0
    for p in like:
        n = math.prod(p.shape)
        nrows = (n + PACK_UNIT - 1) // PACK_UNIT * 8
        out.append(packed[row:row + nrows].reshape(-1)[:n].reshape(p.shape))
        row += nrows
    return out


def kernel(x, rel_bias_table, mix_norm_g, w_in, gate_norm_g, gate_norm_b, w_spatial, b_spatial, attn_sinks, out_norm_a_g, out_norm_b_g, w_out, ffn_norm_g, w_up, w_down, final_norm_g, loss_target, m_rel_bias_table, m_mix_norm_g, m_w_in, m_gate_norm_g, m_gate_norm_b, m_w_spatial, m_b_spatial, m_attn_sinks, m_out_norm_a_g, m_out_norm_b_g, m_w_out, m_ffn_norm_g, m_w_up, m_w_down, m_final_norm_g, v_rel_bias_table, v_mix_norm_g, v_w_in, v_gate_norm_g, v_gate_norm_b, v_w_spatial, v_b_spatial, v_attn_sinks, v_out_norm_a_g, v_out_norm_b_g, v_w_out, v_ffn_norm_g, v_w_up, v_w_down, v_final_norm_g):
    args = dict(locals())
    wts = {n: args[n] for n in WEIGHTS}
    mom = {n: args["m_" + n] for n in WEIGHTS}
    var = {n: args["v_" + n] for n in WEIGHTS}
    sp = {n: wts[n] for n in SMALL}

    own = [wts[n].reshape(wts[n].shape[1:]) for n in LARGE]
    own_bf16 = [_cast_bf16(w, tm=256, name="cast_" + n) for n, w in zip(LARGE, own)]
    g_in, g_out, g_up, g_down = _gather_weights(own_bf16)
    win = g_in.transpose(1, 0, 2).reshape(D_MODEL, PROJ_WIDTH)
    wo = g_out.reshape(A_WIDTH + B_WIDTH, D_MODEL)
    wd = g_down.reshape(D_FF, D_MODEL)

    loss, dx, (dwin, dwo, dwu, dwd), small = _local_step(x[0], loss_target[0], sp, win, wo, g_up, wd)

    grads = [dwin.reshape(D_MODEL, N_CHIPS, PROJ_WIDTH // N_CHIPS).transpose(1, 0, 2),
             dwo.reshape(N_CHIPS, (A_WIDTH + B_WIDTH) // N_CHIPS, D_MODEL), dwu,
             dwd.reshape(N_CHIPS, D_FF // N_CHIPS, D_MODEL)]
    got = _sibling_halves(grads)
    pairs = [_pair_sum_bf16(g, r, tm=256, name="rs_pair_sum_" + n) for n, g, r in zip(LARGE, grads, got)]
    others = _scatter_to_owners(pairs)
    halves = [_owner_sum(g, r, o, tm=256, name="rs_owner_sum_" + n) for n, g, r, o in zip(LARGE, grads, got, others)]
    large_grads = _join_halves(halves)

    out_g, out_d, out_m, out_v = {}, {}, {}, {}
    for n, w, g in zip(LARGE, own, large_grads):
        shape = wts[n].shape
        d, nm, nv = _adamw(w, g, mom[n].reshape(w.shape), var[n].reshape(w.shape), tm=256, name="adamw_" + n)
        out_g[n], out_d[n], out_m[n], out_v[n] = g.reshape(shape), d.reshape(shape), nm.reshape(shape), nv.reshape(shape)

    like = [wts[n] for n in SMALL]
    g_small = _all_reduce_small(_pack([small[n] for n in SMALL]))
    d_s, m_s, v_s = _adamw(_pack(like), g_small, _pack([mom[n] for n in SMALL]), _pack([var[n] for n in SMALL]),
                           tm=g_small.shape[0], name="adamw_small")
    for n, g, d, nm, nv in zip(SMALL, _unpack(g_small, like), _unpack(d_s, like), _unpack(m_s, like),
                               _unpack(v_s, like)):
        out_g[n], out_d[n], out_m[n], out_v[n] = g, d, nm, nv

    total = lax.psum(loss[0, 0], ("x", "y", "c"))
    return (total, dx[None], *[out_g[n] for n in WEIGHTS], *[out_d[n] for n in WEIGHTS],
            *[out_m[n] for n in WEIGHTS], *[out_v[n] for n in WEIGHTS])
```

```python
import functools
import math

import numpy as np
import jax
import jax.numpy as jnp
from jax import lax
from jax.experimental import pallas as pl
from jax.experimental.pallas import tpu as pltpu

F32 = jnp.float32
BF16 = jnp.bfloat16

D_MODEL = 2048
CHUNK = 128
A_GROUPS = 8
A_WIDTH = 1024
HEAD_DIM = 64
B_HEADS = 16
Q_PER_KV = 8
B_WIDTH = 1024
KV_WIDTH = 128
PROJ_WIDTH = 3328
D_FF = 8192
N_BUCKETS = 32
EPS = 1e-5
NEG = -1e30
SCALE = HEAD_DIM ** -0.5
N_CHIPS = 4
N_DEV = 8

ADAM_LR = 0.001
ADAM_B1 = 0.9
ADAM_B2 = 0.999
ADAM_EPS = 1e-08
ADAM_WD = 0.01
ADAM_STEP = 10

VMEM_LIMIT = 56 * 1024 * 1024
MESH = pl.DeviceIdType.MESH


def _bucket_thresholds():
    d = np.arange(CHUNK)
    n_exact = N_BUCKETS // 2
    relf = np.maximum(d, n_exact).astype(np.float64)
    large = n_exact + (np.log(relf / n_exact) / math.log(CHUNK / n_exact) * (N_BUCKETS - n_exact)).astype(np.int32)
    bucket = np.where(d < n_exact, d, np.minimum(large, N_BUCKETS - 1))
    return [int(np.min(d[bucket >= b])) for b in range(1, N_BUCKETS)]


BUCKET_THR = _bucket_thresholds()


def _params(sem=None):
    return pltpu.CompilerParams(dimension_semantics=sem, vmem_limit_bytes=VMEM_LIMIT)


def _gelu(x):
    c = math.sqrt(2.0 / math.pi)
    return 0.5 * x * (1.0 + jnp.tanh(c * (x + 0.044715 * (x * x * x))))


def _gelu_and_grad(x):
    c = math.sqrt(2.0 / math.pi)
    x2 = x * x
    t = jnp.tanh(c * (x + 0.044715 * (x2 * x)))
    g = 0.5 * x * (1.0 + t)
    dg = 0.5 * (1.0 + t) + 0.5 * x * (1.0 - t * t) * (c * (1.0 + 3.0 * 0.044715 * x2))
    return g, dg


def _dot(a, b):
    return jnp.dot(a, b, preferred_element_type=F32)


def _dot_nt(a, b):
    return lax.dot_general(a, b, (((1,), (1,)), ((), ())), preferred_element_type=F32)


def _dot_tn(a, b):
    return lax.dot_general(a, b, (((0,), (0,)), ((), ())), preferred_element_type=F32)


def _rms_bwd(dn, h, g):
    r = lax.rsqrt(jnp.mean(h * h, axis=-1, keepdims=True) + EPS)
    w = dn * g
    dh = r * w - h * ((r * r * r) * jnp.mean(w * h, axis=-1, keepdims=True))
    return dh, r


def _norm_matmul(a, g, b, *, tm, tn, name):
    T, K = a.shape
    if b.ndim == 3:
        per = b.shape[2] // tn
        N = b.shape[0] * b.shape[2]
        b_spec = pl.BlockSpec((None, K, tn), lambda i, j: (j // per, 0, j % per))
    else:
        N = b.shape[1]
        b_spec = pl.BlockSpec((K, tn), lambda i, j: (0, j))

    def body(a_ref, g_ref, b_ref, n_ref, o_ref):
        @pl.when(pl.program_id(1) == 0)
        def _():
            xf = a_ref[...]
            r = lax.rsqrt(jnp.mean(xf * xf, axis=-1, keepdims=True) + EPS)
            n_ref[...] = ((xf * r) * g_ref[...]).astype(BF16)
        o_ref[...] = _dot(n_ref[...], b_ref[...])

    return pl.pallas_call(
        body, name=name, grid=(T // tm, N // tn),
        in_specs=[pl.BlockSpec((tm, K), lambda i, j: (i, 0)), pl.BlockSpec((1, K), lambda i, j: (0, 0)), b_spec],
        out_specs=[pl.BlockSpec((tm, K), lambda i, j: (i, 0)), pl.BlockSpec((tm, tn), lambda i, j: (i, j))],
        out_shape=[jax.ShapeDtypeStruct((T, K), BF16), jax.ShapeDtypeStruct((T, N), F32)],
        compiler_params=_params(("parallel", "arbitrary")),
    )(a, g, b)


def _sq_relu_bf16(z):
    z = jnp.maximum(z, 0.0)
    return (z * z).astype(BF16)


def _to_bf16(v):
    return v.astype(BF16)


def _matmul_res(a, b, res, *, tm, tn, tk, prologue, name):
    T, K = a.shape
    N = b.shape[1]

    def body(a_ref, b_ref, res_ref, o_ref):
        k = pl.program_id(2)
        p = _dot(prologue(a_ref[...]), b_ref[...])

        @pl.when(k == 0)
        def _():
            o_ref[...] = res_ref[...] + p

        @pl.when(k > 0)
        def _():
            o_ref[...] += p

    return pl.pallas_call(
        body, name=name, grid=(T // tm, N // tn, K // tk),
        in_specs=[pl.BlockSpec((tm, tk), lambda i, j, k: (i, k)), pl.BlockSpec((tk, tn), lambda i, j, k: (k, j)),
                  pl.BlockSpec((tm, tn), lambda i, j, k: (i, j))],
        out_specs=pl.BlockSpec((tm, tn), lambda i, j, k: (i, j)),
        out_shape=jax.ShapeDtypeStruct((T, N), F32),
        compiler_params=_params(("parallel", "parallel", "arbitrary")),
    )(a, b, res)


def _matmul_nt(a, b, *, tm, tn, tk, name, extra=None, epilogue=None, out_dtype=F32):
    T, K = a.shape
    if b.ndim == 3:
        per = b.shape[2] // tk
        N = b.shape[1]
        b_spec = pl.BlockSpec((None, tn, tk), lambda i, j, k: (k // per, j, k % per))
    else:
        N = b.shape[0]
        b_spec = pl.BlockSpec((tn, tk), lambda i, j, k: (j, k))
    nk = K // tk
    assert out_dtype == F32 or nk == 1
    in_specs = [pl.BlockSpec((tm, tk), lambda i, j, k: (i, k)), b_spec]
    args = [a, b]
    if extra is not None:
        in_specs.append(pl.BlockSpec((tm, tn), lambda i, j, k: (i, j)))
        args.append(extra)

    def body(*refs):
        a_ref, b_ref = refs[0], refs[1]
        o_ref = refs[-1]
        p = _dot_nt(a_ref[...].astype(BF16), b_ref[...])
        if nk == 1:
            if epilogue is not None:
                p = epilogue(p, refs[2][...])
            o_ref[...] = p.astype(out_dtype)
        else:
            k = pl.program_id(2)

            @pl.when(k == 0)
            def _():
                o_ref[...] = p

            @pl.when(k > 0)
            def _():
                o_ref[...] += p

    return pl.pallas_call(
        body, name=name, grid=(T // tm, N // tn, nk),
        in_specs=in_specs,
        out_specs=pl.BlockSpec((tm, tn), lambda i, j, k: (i, j)),
        out_shape=jax.ShapeDtypeStruct((T, N), out_dtype),
        compiler_params=_params(("parallel", "parallel", "arbitrary")),
    )(*args)


def _matmul_tn(a, b, *, tmo, tn, tk, name, a_prologue=_to_bf16, shards=1):
    T, M = a.shape
    N = b.shape[1]
    if shards > 1:
        per = (N // shards) // tn
        out_spec = pl.BlockSpec((None, tmo, tn), lambda i, j, k: (j // per, i, j % per))
        out_shape = jax.ShapeDtypeStruct((shards, M, N // shards), F32)
    else:
        out_spec = pl.BlockSpec((tmo, tn), lambda i, j, k: (i, j))
        out_shape = jax.ShapeDtypeStruct((M, N), F32)

    def body(a_ref, b_ref, o_ref):
        k = pl.program_id(2)
        p = _dot_tn(a_prologue(a_ref[...]), b_ref[...].astype(BF16))

        @pl.when(k == 0)
        def _():
            o_ref[...] = p

        @pl.when(k > 0)
        def _():
            o_ref[...] += p

    return pl.pallas_call(
        body, name=name, grid=(M // tmo, N // tn, T // tk),
        in_specs=[pl.BlockSpec((tk, tmo), lambda i, j, k: (k, i)), pl.BlockSpec((tk, tn), lambda i, j, k: (k, j))],
        out_specs=out_spec, out_shape=out_shape,
        compiler_params=_params(("parallel", "parallel", "arbitrary")),
    )(a, b)


def _loss_bwd(h2, tgt, g, *, tm):
    T, D = h2.shape

    def body(h_ref, t_ref, g_ref, dh_ref, dg_ref, loss_ref):
        @pl.when(pl.program_id(0) == 0)
        def _():
            dg_ref[...] = jnp.zeros_like(dg_ref)
            loss_ref[...] = jnp.zeros_like(loss_ref)
        h = h_ref[...]
        gg = g_ref[...]
        r = lax.rsqrt(jnp.mean(h * h, axis=-1, keepdims=True) + EPS)
        hn = h * r
        err = hn * gg - t_ref[...]
        loss_ref[...] += 0.5 * jnp.sum(jnp.mean(err * err, axis=-1, keepdims=True), axis=0, keepdims=True)
        dy = err * (1.0 / D)
        dg_ref[...] += jnp.sum(dy * hn, axis=0, keepdims=True)
        w = dy * gg
        dh_ref[...] = r * w - h * ((r * r * r) * jnp.mean(w * h, axis=-1, keepdims=True))

    return pl.pallas_call(
        body, name="loss_bwd", grid=(T // tm,),
        in_specs=[pl.BlockSpec((tm, D), lambda i: (i, 0)), pl.BlockSpec((tm, D), lambda i: (i, 0)),
                  pl.BlockSpec((1, D), lambda i: (0, 0))],
        out_specs=[pl.BlockSpec((tm, D), lambda i: (i, 0)), pl.BlockSpec((1, D), lambda i: (0, 0)),
                   pl.BlockSpec((1, 1), lambda i: (0, 0))],
        out_shape=[jax.ShapeDtypeStruct((T, D), F32), jax.ShapeDtypeStruct((1, D), F32),
                   jax.ShapeDtypeStruct((1, 1), F32)],
        compiler_params=_params(("arbitrary",)),
    )(h2, tgt, g)


def _rms_bwd_res(dn, h, g, dres, *, tm, name):
    T, D = h.shape

    def body(dn_ref, h_ref, g_ref, dres_ref, dh_ref, dg_ref):
        @pl.when(pl.program_id(0) == 0)
        def _():
            dg_ref[...] = jnp.zeros_like(dg_ref)
        h_ = h_ref[...]
        dn_ = dn_ref[...]
        dh, r = _rms_bwd(dn_, h_, g_ref[...])
        dg_ref[...] += jnp.sum(dn_ * (h_ * r), axis=0, keepdims=True)
        dh_ref[...] = dres_ref[...] + dh

    return pl.pallas_call(
        body, name=name, grid=(T // tm,),
        in_specs=[pl.BlockSpec((tm, D), lambda i: (i, 0)), pl.BlockSpec((tm, D), lambda i: (i, 0)),
                  pl.BlockSpec((1, D), lambda i: (0, 0)), pl.BlockSpec((tm, D), lambda i: (i, 0))],
        out_specs=[pl.BlockSpec((tm, D), lambda i: (i, 0)), pl.BlockSpec((1, D), lambda i: (0, 0))],
        out_shape=[jax.ShapeDtypeStruct((T, D), F32), jax.ShapeDtypeStruct((1, D), F32)],
        compiler_params=_params(("arbitrary",)),
    )(dn, h, g, dres)


def _rel_distance():
    i = lax.broadcasted_iota(jnp.int32, (CHUNK, 2 * CHUNK), 0)
    j = lax.broadcasted_iota(jnp.int32, (CHUNK, 2 * CHUNK), 1)
    return i + CHUNK - j


def _bias_build(table):
    def body(tab_ref, o_ref):
        rel = _rel_distance()
        ge = [rel >= t for t in BUCKET_THR]
        for h in range(B_HEADS):
            cur = jnp.full((CHUNK, 2 * CHUNK), tab_ref[0, h], F32)
            for b in range(1, N_BUCKETS):
                cur = jnp.where(ge[b - 1], tab_ref[b, h], cur)
            o_ref[h] = cur

    return pl.pallas_call(
        body, name="bias_build",
        in_specs=[pl.BlockSpec(memory_space=pltpu.SMEM)],
        out_specs=pl.BlockSpec(memory_space=pltpu.VMEM),
        out_shape=jax.ShapeDtypeStruct((B_HEADS, CHUNK, 2 * CHUNK), F32),
    )(table)


def _bias_grad(dbias):
    def body(db_ref, o_ref, acc_ref):
        rel = _rel_distance()
        lo = [0] + BUCKET_THR
        hi = BUCKET_THR + [CHUNK]
        for b in range(N_BUCKETS):
            m = (rel >= lo[b]) & (rel < hi[b])
            for h in range(B_HEADS):
                row = b * B_HEADS + h
                acc_ref[row:row + 1, :] = jnp.sum(jnp.where(m, db_ref[h], 0.0), axis=0, keepdims=True)
        o_ref[...] = jnp.sum(acc_ref[...], axis=1, keepdims=True)

    return pl.pallas_call(
        body, name="bias_grad",
        in_specs=[pl.BlockSpec(memory_space=pltpu.VMEM)],
        out_specs=pl.BlockSpec(memory_space=pltpu.VMEM),
        out_shape=jax.ShapeDtypeStruct((N_BUCKETS * B_HEADS, 1), F32),
        scratch_shapes=[pltpu.VMEM((N_BUCKETS * B_HEADS, 2 * CHUNK), F32)],
    )(dbias)


def _causal_mask():
    t = lax.broadcasted_iota(jnp.int32, (CHUNK, CHUNK), 0)
    s = lax.broadcasted_iota(jnp.int32, (CHUNK, CHUNK), 1)
    return s <= t


def _band_mask(n):
    rel = _rel_distance()
    j = lax.broadcasted_iota(jnp.int32, (CHUNK, 2 * CHUNK), 1)
    return (rel >= 0) & (rel < CHUNK) & ((n > 0) | (j >= CHUNK))


def _gate_forward(u, v, lg, lb, wc, bs):
    ug = _gelu(u)
    vg = _gelu(v)
    mu = jnp.mean(vg, axis=-1, keepdims=True)
    xc = vg - mu
    rstd = lax.rsqrt(jnp.mean(xc * xc, axis=-1, keepdims=True) + EPS)
    xhat = xc * rstd
    vl = (xhat * lg + lb).astype(BF16)
    mixed = _dot(wc, vl) + bs
    return ug, xhat, rstd, vl, mixed


def _softmax_band(q, kband, bias, mask, sink):
    s = _dot_nt(q, kband) * SCALE + bias
    s = jnp.where(mask, s, NEG)
    m = jnp.maximum(jnp.max(s, axis=-1, keepdims=True), sink)
    p = jnp.exp(s - m)
    e_sink = jnp.exp(sink - m)
    inv = 1.0 / (jnp.sum(p, axis=-1, keepdims=True) + e_sink)
    return p * inv, e_sink * inv


def _mixer_fwd(proj, lg, lb, wsp, bs_col, sinks, bias, ga, gb):
    T = proj.shape[0]
    nb = T // CHUNK

    def body(u_ref, v_ref, q_ref, kvc_ref, kvp_ref, lg_ref, lb_ref, w_ref, bs_ref, sink_ref, bias_ref,
             ga_ref, gb_ref, mixed_ref, ab_ref):
        n = pl.program_id(0)
        causal = _causal_mask()
        ssq = jnp.zeros((CHUNK, 1), F32)
        for g in range(A_GROUPS):
            cols = slice(g * CHUNK, (g + 1) * CHUNK)
            wc = jnp.where(causal, w_ref[g], 0.0).astype(BF16)
            ug, _, _, _, mixed = _gate_forward(u_ref[:, cols], v_ref[:, cols], lg_ref[g:g + 1, :], lb_ref[g:g + 1, :],
                                               wc, bs_ref[g])
            a = ug * mixed
            ab_ref[:, cols] = a
            ssq = ssq + jnp.sum(a * a, axis=-1, keepdims=True)
        ra = lax.rsqrt(ssq * (1.0 / A_WIDTH) + EPS)
        mixed_ref[:, :A_WIDTH] = ((ab_ref[:, :A_WIDTH] * ra) * ga_ref[...]).astype(BF16)

        mask = _band_mask(n)
        kvc = kvc_ref[...].astype(BF16)
        kvp = kvp_ref[...].astype(BF16)
        band = jnp.concatenate([kvp, kvc], axis=0)
        ssq = jnp.zeros((CHUNK, 1), F32)
        for h in range(B_HEADS):
            kv = h // Q_PER_KV
            q = q_ref[:, h * HEAD_DIM:(h + 1) * HEAD_DIM].astype(BF16)
            kband = band[:, kv * HEAD_DIM:(kv + 1) * HEAD_DIM]
            vband = band[:, KV_WIDTH + kv * HEAD_DIM:KV_WIDTH + (kv + 1) * HEAD_DIM]
            p, _ = _softmax_band(q, kband, bias_ref[h], mask, sink_ref[0, h])
            o = _dot(p.astype(BF16), vband)
            ab_ref[:, A_WIDTH + h * HEAD_DIM:A_WIDTH + (h + 1) * HEAD_DIM] = o
            ssq = ssq + jnp.sum(o * o, axis=-1, keepdims=True)
        rb = lax.rsqrt(ssq * (1.0 / B_WIDTH) + EPS)
        mixed_ref[:, A_WIDTH:] = ((ab_ref[:, A_WIDTH:] * rb) * gb_ref[...]).astype(BF16)

    full = lambda *shape: pl.BlockSpec(shape, lambda n: (0,) * len(shape))
    return pl.pallas_call(
        body, name="mixer_fwd", grid=(nb,),
        in_specs=[pl.BlockSpec((CHUNK, A_WIDTH), lambda n: (n, 0)),
                  pl.BlockSpec((CHUNK, A_WIDTH), lambda n: (n, 1)),
                  pl.BlockSpec((CHUNK, B_WIDTH), lambda n: (n, 2)),
                  pl.BlockSpec((CHUNK, 2 * KV_WIDTH), lambda n: (n, 12)),
                  pl.BlockSpec((CHUNK, 2 * KV_WIDTH), lambda n: (jnp.maximum(n - 1, 0), 12)),
                  full(A_GROUPS, CHUNK), full(A_GROUPS, CHUNK), full(A_GROUPS, CHUNK, CHUNK), full(A_GROUPS, CHUNK, 1),
                  pl.BlockSpec(memory_space=pltpu.SMEM), full(B_HEADS, CHUNK, 2 * CHUNK),
                  full(1, A_WIDTH), full(1, B_WIDTH)],
        out_specs=[pl.BlockSpec((CHUNK, D_MODEL), lambda n: (n, 0)), pl.BlockSpec((CHUNK, D_MODEL), lambda n: (n, 0))],
        out_shape=[jax.ShapeDtypeStruct((T, D_MODEL), BF16), jax.ShapeDtypeStruct((T, D_MODEL), F32)],
        compiler_params=_params(("parallel",)),
    )(proj, proj, proj, proj, proj, lg, lb, wsp, bs_col, sinks, bias, ga, gb)


def _gmlp_bwd(proj, ab, dmixed, ga, lg, lb, wsp, bs_col):
    T = proj.shape[0]
    nb = T // CHUNK

    def body(u_ref, v_ref, a_ref, dna_ref, ga_ref, lg_ref, lb_ref, w_ref, bs_ref,
             dp_ref, dga_ref, dw_ref, dbs_ref, dlg_ref, dlb_ref):
        @pl.when(pl.program_id(0) == 0)
        def _():
            for r in (dga_ref, dw_ref, dbs_ref, dlg_ref, dlb_ref):
                r[...] = jnp.zeros_like(r)
        causal = _causal_mask()
        a_all = a_ref[...]
        dna = dna_ref[...]
        da_all, ra = _rms_bwd(dna, a_all, ga_ref[...])
        dga_ref[...] += jnp.sum(dna * (a_all * ra), axis=0, keepdims=True)
        for g in range(A_GROUPS):
            cols = slice(g * CHUNK, (g + 1) * CHUNK)
            wc = jnp.where(causal, w_ref[g], 0.0).astype(BF16)
            lgg = lg_ref[g:g + 1, :]
            u = u_ref[:, cols]
            v = v_ref[:, cols]
            ug, xhat, rstd, vl, mixed = _gate_forward(u, v, lgg, lb_ref[g:g + 1, :], wc, bs_ref[g])
            da = da_all[:, cols]
            dug = da * mixed
            dmg = da * ug
            dmg_b = dmg.astype(BF16)
            dbs_ref[g] += jnp.sum(dmg, axis=-1, keepdims=True)
            dw_ref[g] += jnp.where(causal, _dot_nt(dmg_b, vl), 0.0)
            dvl = _dot_tn(wc, dmg_b)
            dlg_ref[g:g + 1, :] += jnp.sum(dvl * xhat, axis=0, keepdims=True)
            dlb_ref[g:g + 1, :] += jnp.sum(dvl, axis=0, keepdims=True)
            dxh = dvl * lgg
            dvg = rstd * (dxh - jnp.mean(dxh, axis=-1, keepdims=True)
                          - xhat * jnp.mean(dxh * xhat, axis=-1, keepdims=True))
            _, gu = _gelu_and_grad(u)
            _, gv = _gelu_and_grad(v)
            dp_ref[:, cols] = (dug * gu).astype(BF16)
            dp_ref[:, A_WIDTH + g * CHUNK:A_WIDTH + (g + 1) * CHUNK] = (dvg * gv).astype(BF16)

    full = lambda *shape: pl.BlockSpec(shape, lambda n: (0,) * len(shape))
    return pl.pallas_call(
        body, name="gmlp_bwd", grid=(nb,),
        in_specs=[pl.BlockSpec((CHUNK, A_WIDTH), lambda n: (n, 0)),
                  pl.BlockSpec((CHUNK, A_WIDTH), lambda n: (n, 1)),
                  pl.BlockSpec((CHUNK, A_WIDTH), lambda n: (n, 0)),
                  pl.BlockSpec((CHUNK, A_WIDTH), lambda n: (n, 0)),
                  full(1, A_WIDTH), full(A_GROUPS, CHUNK), full(A_GROUPS, CHUNK), full(A_GROUPS, CHUNK, CHUNK),
                  full(A_GROUPS, CHUNK, 1)],
        out_specs=[pl.BlockSpec((CHUNK, 2 * A_WIDTH), lambda n: (n, 0)),
                   full(1, A_WIDTH), full(A_GROUPS, CHUNK, CHUNK), full(A_GROUPS, CHUNK, 1),
                   full(A_GROUPS, CHUNK), full(A_GROUPS, CHUNK)],
        out_shape=[jax.ShapeDtypeStruct((T, 2 * A_WIDTH), BF16),
                   jax.ShapeDtypeStruct((1, A_WIDTH), F32), jax.ShapeDtypeStruct((A_GROUPS, CHUNK, CHUNK), F32),
                   jax.ShapeDtypeStruct((A_GROUPS, CHUNK, 1), F32), jax.ShapeDtypeStruct((A_GROUPS, CHUNK), F32),
                   jax.ShapeDtypeStruct((A_GROUPS, CHUNK), F32)],
        compiler_params=_params(("arbitrary",)),
    )(proj, proj, ab, dmixed, ga, lg, lb, wsp, bs_col)


def _attn_bwd(proj, ab, dmixed, gb, sinks, bias):
    T = proj.shape[0]
    nb = T // CHUNK
    qn = lambda n: jnp.minimum(n, nb - 1)

    def body(q_ref, kvc_ref, kvp_ref, o_ref, dnb_ref, gb_ref, sink_ref, bias_ref,
             dq_ref, dkv_ref, dgb_ref, dsink_ref, dbias_ref, carry_ref, sacc_ref):
        n = pl.program_id(0)

        @pl.when(n == 0)
        def _():
            carry_ref[...] = jnp.zeros_like(carry_ref)
            sacc_ref[...] = jnp.zeros_like(sacc_ref)
            dgb_ref[...] = jnp.zeros_like(dgb_ref)
            dbias_ref[...] = jnp.zeros_like(dbias_ref)

        @pl.when(n < nb)
        def _():
            mask = _band_mask(n)
            o_all = o_ref[...]
            dnb = dnb_ref[...]
            do_all, rb = _rms_bwd(dnb, o_all, gb_ref[...])
            dgb_ref[...] += jnp.sum(dnb * (o_all * rb), axis=0, keepdims=True)
            band = jnp.concatenate([kvp_ref[...].astype(BF16), kvc_ref[...].astype(BF16)], axis=0)
            dkv_parts = []
            for kv in range(B_HEADS // Q_PER_KV):
                kband = band[:, kv * HEAD_DIM:(kv + 1) * HEAD_DIM]
                vband = band[:, KV_WIDTH + kv * HEAD_DIM:KV_WIDTH + (kv + 1) * HEAD_DIM]
                dk = jnp.zeros((2 * CHUNK, HEAD_DIM), F32)
                dv = jnp.zeros((2 * CHUNK, HEAD_DIM), F32)
                for hq in range(Q_PER_KV):
                    h = kv * Q_PER_KV + hq
                    hc = slice(h * HEAD_DIM, (h + 1) * HEAD_DIM)
                    q = q_ref[:, hc].astype(BF16)
                    p, p_sink = _softmax_band(q, kband, bias_ref[h], mask, sink_ref[0, h])
                    do = do_all[:, hc].astype(BF16)
                    dp = _dot_nt(do, vband)
                    delta = jnp.sum(p * dp, axis=-1, keepdims=True)
                    ds = p * (dp - delta)
                    sacc_ref[:, h:h + 1] += -(p_sink * delta)
                    dbias_ref[h] += ds
                    dsr = (ds * SCALE).astype(BF16)
                    dq_ref[:, hc] = _dot(dsr, kband).astype(BF16)
                    dk = dk + _dot_tn(dsr, q)
                    dv = dv + _dot_tn(p.astype(BF16), do)
                dkv_parts.append((dk, dv))
            dband = jnp.concatenate([dkv_parts[0][0], dkv_parts[1][0], dkv_parts[0][1], dkv_parts[1][1]], axis=1)
            dkv_ref[...] = (carry_ref[...] + dband[:CHUNK]).astype(BF16)
            carry_ref[...] = dband[CHUNK:]

        @pl.when(n == nb)
        def _():
            dkv_ref[...] = carry_ref[...].astype(BF16)
            dsink_ref[...] = jnp.sum(sacc_ref[...], axis=0, keepdims=True)

    full = lambda *shape: pl.BlockSpec(shape, lambda n: (0,) * len(shape))
    return pl.pallas_call(
        body, name="attn_bwd", grid=(nb + 1,),
        in_specs=[pl.BlockSpec((CHUNK, B_WIDTH), lambda n: (qn(n), 2)),
                  pl.BlockSpec((CHUNK, 2 * KV_WIDTH), lambda n: (qn(n), 12)),
                  pl.BlockSpec((CHUNK, 2 * KV_WIDTH), lambda n: (jnp.maximum(qn(n) - 1, 0), 12)),
                  pl.BlockSpec((CHUNK, B_WIDTH), lambda n: (qn(n), 1)),
                  pl.BlockSpec((CHUNK, B_WIDTH), lambda n: (qn(n), 1)),
                  full(1, B_WIDTH), pl.BlockSpec(memory_space=pltpu.SMEM), full(B_HEADS, CHUNK, 2 * CHUNK)],
        out_specs=[pl.BlockSpec((CHUNK, B_WIDTH), lambda n: (qn(n), 0)),
                   pl.BlockSpec((CHUNK, 2 * KV_WIDTH), lambda n: (jnp.maximum(n - 1, 0), 0)),
                   full(1, B_WIDTH), full(1, B_HEADS), full(B_HEADS, CHUNK, 2 * CHUNK)],
        out_shape=[jax.ShapeDtypeStruct((T, B_WIDTH), BF16), jax.ShapeDtypeStruct((T, 2 * KV_WIDTH), BF16),
                   jax.ShapeDtypeStruct((1, B_WIDTH), F32), jax.ShapeDtypeStruct((1, B_HEADS), F32),
                   jax.ShapeDtypeStruct((B_HEADS, CHUNK, 2 * CHUNK), F32)],
        scratch_shapes=[pltpu.VMEM((CHUNK, 2 * KV_WIDTH), F32), pltpu.VMEM((CHUNK, B_HEADS), F32)],
        compiler_params=_params(("arbitrary",)),
    )(proj, proj, proj, ab, dmixed, gb, sinks, bias)


def _sq_relu_grad(acc, z):
    return acc * (2.0 * jnp.maximum(z, 0.0))


def _local_step(x, tgt, sp, win, wo, wu, wd):
    T = x.shape[0]
    tm = min(512, T)
    tk = min(512, T)
    lg = sp["gate_norm_g"].reshape(A_GROUPS, CHUNK)
    lb = sp["gate_norm_b"].reshape(A_GROUPS, CHUNK)
    wsp = sp["w_spatial"].reshape(A_GROUPS, CHUNK, CHUNK)
    bs_col = sp["b_spatial"].reshape(A_GROUPS, CHUNK, 1)
    sinks = sp["attn_sinks"].reshape(1, B_HEADS)
    ga = sp["out_norm_a_g"].reshape(1, A_WIDTH)
    gb = sp["out_norm_b_g"].reshape(1, B_WIDTH)
    g1 = sp["mix_norm_g"].reshape(1, D_MODEL)
    g2 = sp["ffn_norm_g"].reshape(1, D_MODEL)
    gf = sp["final_norm_g"].reshape(1, D_MODEL)

    bias = _bias_build(sp["rel_bias_table"])
    n1, proj = _norm_matmul(x, g1, win, tm=tm, tn=PROJ_WIDTH // 2, name="in_proj")
    mixed, ab = _mixer_fwd(proj, lg, lb, wsp, bs_col, sinks, bias, ga, gb)
    h1 = _matmul_res(mixed, wo, x, tm=tm, tn=1024, tk=D_MODEL, prologue=_to_bf16, name="out_proj")
    n2, zp = _norm_matmul(h1, g2, wu, tm=tm, tn=1024, name="up_proj")
    h2 = _matmul_res(zp, wd, h1, tm=tm, tn=1024, tk=2048, prologue=_sq_relu_bf16, name="down_proj")

    dh2, dgf, loss = _loss_bwd(h2, tgt, gf, tm=tm)
    dzp = _matmul_nt(dh2, wd, tm=tm, tn=1024, tk=D_MODEL, name="bwd_dz", extra=zp, epilogue=_sq_relu_grad,
                     out_dtype=BF16)
    dwd = _matmul_tn(zp, dh2, tmo=1024, tn=1024, tk=tk, name="grad_w_down", a_prologue=_sq_relu_bf16)
    dwu = _matmul_tn(n2, dzp, tmo=1024, tn=1024, tk=tk, name="grad_w_up", shards=N_CHIPS)
    dn2 = _matmul_nt(dzp, wu, tm=tm, tn=1024, tk=2048, name="bwd_dn2")
    dh1, dg2 = _rms_bwd_res(dn2, h1, g2, dh2, tm=tm, name="ffn_norm_bwd")
    dwo = _matmul_tn(mixed, dh1, tmo=1024, tn=1024, tk=tk, name="grad_w_out")
    dmixed = _matmul_nt(dh1, wo, tm=tm, tn=1024, tk=D_MODEL, name="bwd_dmixed")
    duv, dga, dwsp, dbs, dlg, dlb = _gmlp_bwd(proj, ab, dmixed, ga, lg, lb, wsp, bs_col)
    dq, dkv, dgb, dsinks, dbias = _attn_bwd(proj, ab, dmixed, gb, sinks, bias)
    dtable = _bias_grad(dbias)
    dproj = jnp.concatenate([duv, dq, dkv], axis=1)
    dwin = _matmul_tn(n1, dproj, tmo=1024, tn=PROJ_WIDTH // 2, tk=tk, name="grad_w_in")
    dn1 = _matmul_nt(dproj, win, tm=tm, tn=1024, tk=PROJ_WIDTH, name="bwd_dn1")
    dx, dg1 = _rms_bwd_res(dn1, x, g1, dh1, tm=tm, name="mix_norm_bwd")

    small = {
        "rel_bias_table": dtable.reshape(N_BUCKETS, B_HEADS), "mix_norm_g": dg1, "gate_norm_g": dlg, "gate_norm_b": dlb,
        "w_spatial": dwsp, "b_spatial": dbs, "attn_sinks": dsinks, "out_norm_a_g": dga, "out_norm_b_g": dgb,
        "ffn_norm_g": dg2, "final_norm_g": dgf,
    }
    return loss, dx, (dwin, dwo, dwu, dwd), small


def _place():
    x, y, c = lax.axis_index("x"), lax.axis_index("y"), lax.axis_index("c")
    chips = [(1 - x, y), (x, 1 - y), (1 - x, 1 - y)]
    return x, y, c, chips


def _remote(src, dst, send_sem, recv_sem, to):
    return pltpu.make_async_remote_copy(src_ref=src, dst_ref=dst, send_sem=send_sem, recv_sem=recv_sem,
                                        device_id=to, device_id_type=MESH)


def _core_index():
    return lax.axis_index("c").astype(jnp.int32).reshape(1)


def _chip_index():
    return (2 * lax.axis_index("x") + lax.axis_index("y")).astype(jnp.int32).reshape(1)


def _cast_into_slot(w, *, tm, name):
    R, C = w.shape

    def body(me_ref, w_ref, o_ref):
        del me_ref
        o_ref[...] = w_ref[...].astype(BF16)

    return pl.pallas_call(
        body, name=name,
        grid_spec=pltpu.PrefetchScalarGridSpec(
            num_scalar_prefetch=1, grid=(R // tm,),
            in_specs=[pl.BlockSpec((tm, C), lambda i, me: (i, 0))],
            out_specs=pl.BlockSpec((None, tm, C), lambda i, me: (me[0], i, 0))),
        out_shape=jax.ShapeDtypeStruct((N_CHIPS, R, C), BF16), compiler_params=_params(("parallel",)),
    )(_chip_index(), w)


def _gather_weights(slots):
    nw = len(slots)

    def body(*refs):
        fulls = refs[nw:2 * nw]
        send_sems, recv_sems = refs[2 * nw:]
        x, y, c, chips = _place()
        me = 2 * x + y
        sends = []
        for w in range(nw):
            hr = fulls[w].shape[1] // 2
            rows = pl.ds(c * hr, hr)
            for j, chip in enumerate(chips):
                mine = fulls[w].at[me, rows, :]
                cp = _remote(mine, mine, send_sems.at[6 * w + j], recv_sems.at[6 * w + j], (*chip, c))
                cp.start()
                sends.append(cp)
        for w in range(nw):
            hr = fulls[w].shape[1] // 2
            rows = pl.ds(c * hr, hr)
            for j, chip in enumerate(chips):
                landed = fulls[w].at[2 * chip[0] + chip[1], rows, :]
                _remote(landed, landed, send_sems.at[6 * w + j], recv_sems.at[6 * w + j], (x, y, c)).wait_recv()
                cp = _remote(landed, landed, send_sems.at[6 * w + 3 + j], recv_sems.at[6 * w + 3 + j], (x, y, 1 - c))
                cp.start()
                sends.append(cp)
        for w in range(nw):
            hr = fulls[w].shape[1] // 2
            rows = pl.ds((1 - c) * hr, hr)
            for j, chip in enumerate(chips):
                other = fulls[w].at[2 * chip[0] + chip[1], rows, :]
                _remote(other, other, send_sems.at[6 * w + 3 + j], recv_sems.at[6 * w + 3 + j], (x, y, c)).wait_recv()
        for cp in sends:
            cp.wait_send()

    any_spec = pl.BlockSpec(memory_space=pl.ANY)
    return pl.pallas_call(
        body, name="gather_weights",
        in_specs=[any_spec] * nw, out_specs=[any_spec] * nw,
        out_shape=[jax.ShapeDtypeStruct(s.shape, s.dtype) for s in slots],
        scratch_shapes=[pltpu.SemaphoreType.DMA((6 * nw,)), pltpu.SemaphoreType.DMA((6 * nw,))],
        input_output_aliases={w: w for w in range(nw)},
    )(*slots)


def _sibling_halves(grads):
    nw = len(grads)

    def body(*refs):
        gs, outs = refs[:nw], refs[nw:2 * nw]
        send_sems, recv_sems = refs[2 * nw:]
        x, y, c, _ = _place()
        cps = []
        for w in range(nw):
            hr = gs[w].shape[1] // 2
            cp = _remote(gs[w].at[:, pl.ds((1 - c) * hr, hr), :], outs[w], send_sems.at[w], recv_sems.at[w],
                         (x, y, 1 - c))
            cp.start()
            cps.append(cp)
        for cp in cps:
            cp.wait()

    any_spec = pl.BlockSpec(memory_space=pl.ANY)
    return pl.pallas_call(
        body, name="rs_sibling_halves",
        in_specs=[any_spec] * nw, out_specs=[any_spec] * nw,
        out_shape=[jax.ShapeDtypeStruct((g.shape[0], g.shape[1] // 2, g.shape[2]), g.dtype) for g in grads],
        scratch_shapes=[pltpu.SemaphoreType.DMA((nw,)), pltpu.SemaphoreType.DMA((nw,))],
    )(*grads)


def _pair_sum_bf16(g, got, *, tm, name):
    S, R, C = g.shape
    hr = R // 2
    nt = hr // tm

    def body(c_ref, g_ref, got_ref, o_ref):
        del c_ref
        o_ref[...] = (g_ref[...] + got_ref[...]).astype(BF16)

    return pl.pallas_call(
        body, name=name,
        grid_spec=pltpu.PrefetchScalarGridSpec(
            num_scalar_prefetch=1, grid=(S, nt),
            in_specs=[pl.BlockSpec((None, tm, C), lambda s, i, c: (s, c[0] * nt + i, 0)),
                      pl.BlockSpec((None, tm, C), lambda s, i, c: (s, i, 0))],
            out_specs=pl.BlockSpec((None, tm, C), lambda s, i, c: (s, i, 0))),
        out_shape=jax.ShapeDtypeStruct((S, hr, C), BF16),
        compiler_params=_params(("parallel", "parallel")),
    )(_core_index(), g, got)


def _scatter_to_owners(pairs):
    nw = len(pairs)

    def body(*refs):
        qs, outs = refs[:nw], refs[nw:2 * nw]
        send_sems, recv_sems = refs[2 * nw:]
        x, y, c, chips = _place()
        cps = []
        for w in range(nw):
            for j, chip in enumerate(chips):
                cp = _remote(qs[w].at[2 * chip[0] + chip[1]], outs[w].at[j], send_sems.at[3 * w + j],
                             recv_sems.at[3 * w + j], (*chip, c))
                cp.start()
                cps.append(cp)
        for cp in cps:
            cp.wait()

    any_spec = pl.BlockSpec(memory_space=pl.ANY)
    return pl.pallas_call(
        body, name="rs_scatter_to_owners",
        in_specs=[any_spec] * nw, out_specs=[any_spec] * nw,
        out_shape=[jax.ShapeDtypeStruct((3,) + q.shape[1:], q.dtype) for q in pairs],
        scratch_shapes=[pltpu.SemaphoreType.DMA((3 * nw,)), pltpu.SemaphoreType.DMA((3 * nw,))],
    )(*pairs)


def _owner_sum(g, got, others, *, tm, name):
    S, R, C = g.shape
    hr = R // 2
    nt = hr // tm

    def body(idx_ref, g_ref, got_ref, o_ref_in, out_ref):
        del idx_ref
        acc = g_ref[...] + got_ref[...]
        for j in range(3):
            acc = acc + o_ref_in[j].astype(F32)
        out_ref[...] = acc

    return pl.pallas_call(
        body, name=name,
        grid_spec=pltpu.PrefetchScalarGridSpec(
            num_scalar_prefetch=1, grid=(nt,),
            in_specs=[pl.BlockSpec((None, tm, C), lambda i, p: (p[1], p[0] * nt + i, 0)),
                      pl.BlockSpec((None, tm, C), lambda i, p: (p[1], i, 0)),
                      pl.BlockSpec((3, tm, C), lambda i, p: (0, i, 0))],
            out_specs=pl.BlockSpec((tm, C), lambda i, p: (i, 0))),
        out_shape=jax.ShapeDtypeStruct((hr, C), F32),
        compiler_params=_params(("parallel",)),
    )(jnp.concatenate([_core_index(), _chip_index()]), g, got, others)


def _swap_halves(halves):
    nw = len(halves)

    def body(*refs):
        hs, outs = refs[:nw], refs[nw:2 * nw]
        send_sems, recv_sems = refs[2 * nw:]
        x, y, c, _ = _place()
        cps = []
        for w in range(nw):
            cp = _remote(hs[w], outs[w], send_sems.at[w], recv_sems.at[w], (x, y, 1 - c))
            cp.start()
            cps.append(cp)
        for cp in cps:
            cp.wait()

    any_spec = pl.BlockSpec(memory_space=pl.ANY)
    return pl.pallas_call(
        body, name="rs_swap_halves",
        in_specs=[any_spec] * nw, out_specs=[any_spec] * nw,
        out_shape=[jax.ShapeDtypeStruct(h.shape, h.dtype) for h in halves],
        scratch_shapes=[pltpu.SemaphoreType.DMA((nw,)), pltpu.SemaphoreType.DMA((nw,))],
    )(*halves)


def _all_reduce_small(packed):
    R, C = packed.shape

    def body(in_ref, out_ref, slots, send_sems, recv_sems):
        x, y, c, _ = _place()
        me = 4 * x + 2 * y + c
        cps = []
        for k in range(1, N_DEV):
            p = (me + k) % N_DEV
            cp = _remote(in_ref, slots.at[me], send_sems.at[k - 1], recv_sems.at[k - 1], (p // 4, (p // 2) % 2, p % 2))
            cp.start()
            cps.append(cp)
        slots[me] = in_ref[...]
        for k in range(1, N_DEV):
            src = (me + N_DEV - k) % N_DEV
            _remote(in_ref, slots.at[src], send_sems.at[k - 1], recv_sems.at[k - 1], (x, y, c)).wait_recv()
        for cp in cps:
            cp.wait_send()
        acc = slots[0]
        for d in range(1, N_DEV):
            acc = acc + slots[d]
        out_ref[...] = acc

    vmem = pl.BlockSpec(memory_space=pltpu.VMEM)
    return pl.pallas_call(
        body, name="all_reduce_small", in_specs=[vmem], out_specs=vmem,
        out_shape=jax.ShapeDtypeStruct((R, C), F32),
        scratch_shapes=[pltpu.VMEM((N_DEV, R, C), F32), pltpu.SemaphoreType.DMA((N_DEV - 1,)),
                        pltpu.SemaphoreType.DMA((N_DEV - 1,))],
        compiler_params=_params(),
    )(packed)


def _adamw_math(w, g, m, v):
    m = ADAM_B1 * m + (1.0 - ADAM_B1) * g
    v = ADAM_B2 * v + (1.0 - ADAM_B2) * (g * g)
    m_hat = m / (1.0 - ADAM_B1 ** ADAM_STEP)
    v_hat = v / (1.0 - ADAM_B2 ** ADAM_STEP)
    delta = -ADAM_LR * (m_hat / (jnp.sqrt(v_hat) + ADAM_EPS) + ADAM_WD * w)
    return delta, m, v


def _adamw(w, g, m, v, *, tm, name):
    R, C = w.shape

    def body(w_ref, g_ref, m_ref, v_ref, d_ref, nm_ref, nv_ref):
        d_ref[...], nm_ref[...], nv_ref[...] = _adamw_math(w_ref[...], g_ref[...], m_ref[...], v_ref[...])

    spec = pl.BlockSpec((tm, C), lambda i: (i, 0))
    return pl.pallas_call(
        body, name=name, grid=(R // tm,), in_specs=[spec] * 4, out_specs=[spec] * 3,
        out_shape=[jax.ShapeDtypeStruct((R, C), F32)] * 3, compiler_params=_params(("parallel",)),
    )(w, g, m, v)


def _adamw_halves(w, own, got, m, v, *, tm, name):
    R, C = w.shape
    nt = (R // 2) // tm

    def body(c_ref, w_ref, own_ref, got_ref, m_ref, v_ref, g_ref, d_ref, nm_ref, nv_ref):
        g = jnp.where(pl.program_id(0) == c_ref[0], own_ref[...], got_ref[...])
        g_ref[...] = g
        d_ref[...], nm_ref[...], nv_ref[...] = _adamw_math(w_ref[...], g, m_ref[...], v_ref[...])

    whole = pl.BlockSpec((tm, C), lambda h, i, c: (h * nt + i, 0))
    half = pl.BlockSpec((tm, C), lambda h, i, c: (i, 0))
    return pl.pallas_call(
        body, name=name,
        grid_spec=pltpu.PrefetchScalarGridSpec(
            num_scalar_prefetch=1, grid=(2, nt),
            in_specs=[whole, half, half, whole, whole], out_specs=[whole] * 4),
        out_shape=[jax.ShapeDtypeStruct((R, C), F32)] * 4, compiler_params=_params(("parallel", "parallel")),
    )(_core_index(), w, own, got, m, v)


SMALL = ["rel_bias_table", "mix_norm_g", "gate_norm_g", "gate_norm_b", "w_spatial", "b_spatial", "attn_sinks",
         "out_norm_a_g", "out_norm_b_g", "ffn_norm_g", "final_norm_g"]
LARGE = ["w_in", "w_out", "w_up", "w_down"]
WEIGHTS = ["rel_bias_table", "mix_norm_g", "w_in", "gate_norm_g", "gate_norm_b", "w_spatial", "b_spatial", "attn_sinks",
           "out_norm_a_g", "out_norm_b_g", "w_out", "ffn_norm_g", "w_up", "w_down", "final_norm_g"]
PACK_UNIT = 8 * 128


def _pack(parts):
    rows = []
    for p in parts:
        flat = p.reshape(-1)
        pad = (-flat.shape[0]) % PACK_UNIT
        rows.append(jnp.pad(flat, (0, pad)).reshape(-1, 128))
    return jnp.concatenate(rows, axis=0)


def _unpack(packed, like):
    out, row = [], 0
    for p in like:
        n = math.prod(p.shape)
        nrows = (n + PACK_UNIT - 1) // PACK_UNIT * 8
        out.append(packed[row:row + nrows].reshape(-1)[:n].reshape(p.shape))
        row += nrows
    return out


def kernel(x, rel_bias_table, mix_norm_g, w_in, gate_norm_g, gate_norm_b, w_spatial, b_spatial, attn_sinks, out_norm_a_g, out_norm_b_g, w_out, ffn_norm_g, w_up, w_down, final_norm_g, loss_target, m_rel_bias_table, m_mix_norm_g, m_w_in, m_gate_norm_g, m_gate_norm_b, m_w_spatial, m_b_spatial, m_attn_sinks, m_out_norm_a_g, m_out_norm_b_g, m_w_out, m_ffn_norm_g, m_w_up, m_w_down, m_final_norm_g, v_rel_bias_table, v_mix_norm_g, v_w_in, v_gate_norm_g, v_gate_norm_b, v_w_spatial, v_b_spatial, v_attn_sinks, v_out_norm_a_g, v_out_norm_b_g, v_w_out, v_ffn_norm_g, v_w_up, v_w_down, v_final_norm_g):
    args = dict(locals())
    wts = {n: args[n] for n in WEIGHTS}
    mom = {n: args["m_" + n] for n in WEIGHTS}
    var = {n: args["v_" + n] for n in WEIGHTS}
    sp = {n: wts[n] for n in SMALL}

    own = [wts[n].reshape(wts[n].shape[1:]) for n in LARGE]
    slots = [_cast_into_slot(w, tm=256, name="cast_" + n) for n, w in zip(LARGE, own)]
    g_in, g_out, g_up, g_down = _gather_weights(slots)
    win = g_in.transpose(1, 0, 2).reshape(D_MODEL, PROJ_WIDTH)
    wo = g_out.reshape(A_WIDTH + B_WIDTH, D_MODEL)
    wd = g_down.reshape(D_FF, D_MODEL)

    loss, dx, (dwin, dwo, dwu, dwd), small = _local_step(x[0], loss_target[0], sp, win, wo, g_up, wd)

    grads = [dwin.reshape(D_MODEL, N_CHIPS, PROJ_WIDTH // N_CHIPS).transpose(1, 0, 2),
             dwo.reshape(N_CHIPS, (A_WIDTH + B_WIDTH) // N_CHIPS, D_MODEL), dwu,
             dwd.reshape(N_CHIPS, D_FF // N_CHIPS, D_MODEL)]
    got = _sibling_halves(grads)
    pairs = [_pair_sum_bf16(g, r, tm=256, name="rs_pair_sum_" + n) for n, g, r in zip(LARGE, grads, got)]
    others = _scatter_to_owners(pairs)
    halves = [_owner_sum(g, r, o, tm=256, name="rs_owner_sum_" + n) for n, g, r, o in zip(LARGE, grads, got, others)]
    swapped = _swap_halves(halves)

    out_g, out_d, out_m, out_v = {}, {}, {}, {}
    for n, w, h, s in zip(LARGE, own, halves, swapped):
        shape = wts[n].shape
        g, d, nm, nv = _adamw_halves(w, h, s, mom[n].reshape(w.shape), var[n].reshape(w.shape), tm=256,
                                     name="adamw_" + n)
        out_g[n], out_d[n], out_m[n], out_v[n] = g.reshape(shape), d.reshape(shape), nm.reshape(shape), nv.reshape(shape)

    like = [wts[n] for n in SMALL]
    g_small = _all_reduce_small(_pack([small[n] for n in SMALL]))
    d_s, m_s, v_s = _adamw(_pack(like), g_small, _pack([mom[n] for n in SMALL]), _pack([var[n] for n in SMALL]),
                           tm=g_small.shape[0], name="adamw_small")
    for n, g, d, nm, nv in zip(SMALL, _unpack(g_small, like), _unpack(d_s, like), _unpack(m_s, like),
                               _unpack(v_s, like)):
        out_g[n], out_d[n], out_m[n], out_v[n] = g, d, nm, nv

    total = lax.psum(loss[0, 0], ("x", "y", "c"))
    return (total, dx[None], *[out_g[n] for n in WEIGHTS], *[out_d[n] for n in WEIGHTS],
            *[out_m[n] for n in WEIGHTS], *[out_v[n] for n in WEIGHTS])
```

```python
import functools
import math

import numpy as np
import jax
import jax.numpy as jnp
from jax import lax
from jax.experimental import pallas as pl
from jax.experimental.pallas import tpu as pltpu

F32 = jnp.float32
BF16 = jnp.bfloat16

D_MODEL = 2048
CHUNK = 128
A_GROUPS = 8
A_WIDTH = 1024
HEAD_DIM = 64
B_HEADS = 16
Q_PER_KV = 8
B_WIDTH = 1024
KV_WIDTH = 128
PROJ_WIDTH = 3328
D_FF = 8192
N_BUCKETS = 32
EPS = 1e-5
NEG = -1e30
SCALE = HEAD_DIM ** -0.5
N_CHIPS = 4
N_DEV = 8

ADAM_LR = 0.001
ADAM_B1 = 0.9
ADAM_B2 = 0.999
ADAM_EPS = 1e-08
ADAM_WD = 0.01
ADAM_STEP = 10

VMEM_LIMIT = 56 * 1024 * 1024
MESH = pl.DeviceIdType.MESH


def _bucket_thresholds():
    d = np.arange(CHUNK)
    n_exact = N_BUCKETS // 2
    relf = np.maximum(d, n_exact).astype(np.float64)
    large = n_exact + (np.log(relf / n_exact) / math.log(CHUNK / n_exact) * (N_BUCKETS - n_exact)).astype(np.int32)
    bucket = np.where(d < n_exact, d, np.minimum(large, N_BUCKETS - 1))
    return [int(np.min(d[bucket >= b])) for b in range(1, N_BUCKETS)]


BUCKET_THR = _bucket_thresholds()


def _params(sem=None):
    return pltpu.CompilerParams(dimension_semantics=sem, vmem_limit_bytes=VMEM_LIMIT)


def _gelu(x):
    c = math.sqrt(2.0 / math.pi)
    return 0.5 * x * (1.0 + jnp.tanh(c * (x + 0.044715 * (x * x * x))))


def _gelu_and_grad(x):
    c = math.sqrt(2.0 / math.pi)
    x2 = x * x
    t = jnp.tanh(c * (x + 0.044715 * (x2 * x)))
    g = 0.5 * x * (1.0 + t)
    dg = 0.5 * (1.0 + t) + 0.5 * x * (1.0 - t * t) * (c * (1.0 + 3.0 * 0.044715 * x2))
    return g, dg


def _dot(a, b):
    return jnp.dot(a, b, preferred_element_type=F32)


def _dot_nt(a, b):
    return lax.dot_general(a, b, (((1,), (1,)), ((), ())), preferred_element_type=F32)


def _dot_tn(a, b):
    return lax.dot_general(a, b, (((0,), (0,)), ((), ())), preferred_element_type=F32)


def _rms_bwd(dn, h, g):
    r = lax.rsqrt(jnp.mean(h * h, axis=-1, keepdims=True) + EPS)
    w = dn * g
    dh = r * w - h * ((r * r * r) * jnp.mean(w * h, axis=-1, keepdims=True))
    return dh, r


def _place():
    x, y, c = lax.axis_index("x"), lax.axis_index("y"), lax.axis_index("c")
    chips = [(1 - x, y), (x, 1 - y), (1 - x, 1 - y)]
    return x, y, c, chips


def _remote(src, dst, send_sem, recv_sem, to):
    return pltpu.make_async_remote_copy(src_ref=src, dst_ref=dst, send_sem=send_sem, recv_sem=recv_sem,
                                        device_id=to, device_id_type=MESH)


class _Ride:
    def __init__(self, args, out_shape, n_sem, start, finish, mid=None, mid_frac=0.8, aliases=None):
        self.args, self.out_shape, self.n_sem = list(args), list(out_shape), n_sem
        self.start, self.mid, self.finish, self.mid_frac = start, mid, finish, mid_frac
        self.aliases = dict(aliases or {})


def _call(body, *, name, grid, in_specs, out_specs, out_shape, scratch_shapes=(), sem=None, rides=()):
    single = not isinstance(out_shape, (list, tuple))
    out_specs = [out_specs] if single else list(out_specs)
    out_shape = [out_shape] if single else list(out_shape)
    n_in, n_out, n_scr = len(in_specs), len(out_shape), len(scratch_shapes)
    r_in = [len(r.args) for r in rides]
    r_out = [len(r.out_shape) for r in rides]
    any_spec = pl.BlockSpec(memory_space=pl.ANY)
    aliases, off_i, off_o = {}, n_in, n_out
    for r in rides:
        for i, o in r.aliases.items():
            aliases[off_i + i] = off_o + o
        off_i += len(r.args)
        off_o += len(r.out_shape)
    steps = math.prod(grid)

    def wrapped(*refs):
        p = 0
        ins = refs[p:p + n_in]; p += n_in
        rins = refs[p:p + sum(r_in)]; p += sum(r_in)
        outs = refs[p:p + n_out]; p += n_out
        routs = refs[p:p + sum(r_out)]; p += sum(r_out)
        scr = refs[p:p + n_scr]; p += n_scr
        sems = refs[p:]
        parts, pi, po = [], 0, 0
        for k, r in enumerate(rides):
            parts.append((rins[pi:pi + r_in[k]], routs[po:po + r_out[k]], sems[2 * k], sems[2 * k + 1]))
            pi += r_in[k]
            po += r_out[k]
        lin = 0
        for d in range(len(grid)):
            lin = lin * grid[d] + pl.program_id(d)
        if rides:
            @pl.when(lin == 0)
            def _():
                for r, part in zip(rides, parts):
                    r.start(*part)
        body(*ins, *outs, *scr)
        for r, part in zip(rides, parts):
            if r.mid is not None:
                @pl.when(lin == min(steps - 1, int(r.mid_frac * steps)))
                def _(r=r, part=part):
                    r.mid(*part)
        if rides:
            @pl.when(lin == steps - 1)
            def _():
                for r, part in zip(rides, parts):
                    r.finish(*part)

    scratch = list(scratch_shapes)
    for r in rides:
        scratch += [pltpu.SemaphoreType.DMA((r.n_sem,)), pltpu.SemaphoreType.DMA((r.n_sem,))]
    if rides:
        sem = ("arbitrary",) * len(grid)
    res = pl.pallas_call(
        wrapped, name=name, grid=grid,
        in_specs=list(in_specs) + [any_spec] * sum(r_in),
        out_specs=out_specs + [any_spec] * sum(r_out),
        out_shape=out_shape + [s for r in rides for s in r.out_shape],
        scratch_shapes=scratch, input_output_aliases=aliases,
        compiler_params=_params(sem),
    )

    def run(*args):
        got = res(*args, *[a for r in rides for a in r.args])
        mine = got[0] if single else list(got[:n_out])
        if not rides:
            return mine
        rest, out = list(got[n_out:]), []
        for k in range(len(rides)):
            out.append(rest[:r_out[k]])
            rest = rest[r_out[k]:]
        return mine, out

    return run


def _ride_gather(slot, mid_frac=0.8):
    hr = slot.shape[1] // 2

    def start(ins, outs, ss, rs):
        x, y, c, chips = _place()
        mine = outs[0].at[2 * x + y, pl.ds(c * hr, hr), :]
        for j, chip in enumerate(chips):
            _remote(mine, mine, ss.at[j], rs.at[j], (*chip, c)).start()

    def mid(ins, outs, ss, rs):
        x, y, c, chips = _place()
        for j, chip in enumerate(chips):
            landed = outs[0].at[2 * chip[0] + chip[1], pl.ds(c * hr, hr), :]
            _remote(landed, landed, ss.at[j], rs.at[j], (x, y, c)).wait_recv()
            _remote(landed, landed, ss.at[3 + j], rs.at[3 + j], (x, y, 1 - c)).start()

    def finish(ins, outs, ss, rs):
        x, y, c, chips = _place()
        for j, chip in enumerate(chips):
            other = outs[0].at[2 * chip[0] + chip[1], pl.ds((1 - c) * hr, hr), :]
            _remote(other, other, ss.at[3 + j], rs.at[3 + j], (x, y, c)).wait_recv()
        for j in range(6):
            part = outs[0].at[0, pl.ds(0, hr), :]
            _remote(part, part, ss.at[j], rs.at[j], (x, y, c)).wait_send()

    return _Ride([slot], [jax.ShapeDtypeStruct(slot.shape, slot.dtype)], 6, start, finish, mid=mid,
                 mid_frac=mid_frac, aliases={0: 0})


def _ride_sibling_halves(g):
    S, R, C = g.shape
    hr = R // 2

    def copy(ins, outs, ss, rs):
        x, y, c, _ = _place()
        return _remote(ins[0].at[:, pl.ds((1 - c) * hr, hr), :], outs[0], ss.at[0], rs.at[0], (x, y, 1 - c))

    return _Ride([g], [jax.ShapeDtypeStruct((S, hr, C), g.dtype)], 1,
                 lambda *a: copy(*a).start(), lambda *a: copy(*a).wait())


def _ride_scatter(q):
    def copies(ins, outs, ss, rs):
        x, y, c, chips = _place()
        return [_remote(ins[0].at[2 * chip[0] + chip[1]], outs[0].at[j], ss.at[j], rs.at[j], (*chip, c))
                for j, chip in enumerate(chips)]

    def start(*a):
        for cp in copies(*a):
            cp.start()

    def finish(*a):
        for cp in copies(*a):
            cp.wait()

    return _Ride([q], [jax.ShapeDtypeStruct((3,) + q.shape[1:], q.dtype)], 3, start, finish)


def _ride_swap(h):
    def copy(ins, outs, ss, rs):
        x, y, c, _ = _place()
        return _remote(ins[0], outs[0], ss.at[0], rs.at[0], (x, y, 1 - c))

    return _Ride([h], [jax.ShapeDtypeStruct(h.shape, h.dtype)], 1,
                 lambda *a: copy(*a).start(), lambda *a: copy(*a).wait())


def _carrier(rides, *, name):
    _, outs = _call(lambda: None, name=name, grid=(1,), in_specs=[], out_specs=[], out_shape=[], rides=rides)()
    return outs


def _norm_matmul(a, g, b, *, tm, tn, name, rides=()):
    T, K = a.shape
    if b.ndim == 3:
        per = b.shape[2] // tn
        N = b.shape[0] * b.shape[2]
        b_spec = pl.BlockSpec((None, K, tn), lambda i, j: (j // per, 0, j % per))
    else:
        N = b.shape[1]
        b_spec = pl.BlockSpec((K, tn), lambda i, j: (0, j))

    def body(a_ref, g_ref, b_ref, n_ref, o_ref):
        @pl.when(pl.program_id(1) == 0)
        def _():
            xf = a_ref[...]
            r = lax.rsqrt(jnp.mean(xf * xf, axis=-1, keepdims=True) + EPS)
            n_ref[...] = ((xf * r) * g_ref[...]).astype(BF16)
        o_ref[...] = _dot(n_ref[...], b_ref[...])

    return _call(
        body, name=name, grid=(T // tm, N // tn),
        in_specs=[pl.BlockSpec((tm, K), lambda i, j: (i, 0)), pl.BlockSpec((1, K), lambda i, j: (0, 0)), b_spec],
        out_specs=[pl.BlockSpec((tm, K), lambda i, j: (i, 0)), pl.BlockSpec((tm, tn), lambda i, j: (i, j))],
        out_shape=[jax.ShapeDtypeStruct((T, K), BF16), jax.ShapeDtypeStruct((T, N), F32)],
        sem=("parallel", "arbitrary"), rides=rides,
    )(a, g, b)


def _sq_relu_bf16(z):
    z = jnp.maximum(z, 0.0)
    return (z * z).astype(BF16)


def _to_bf16(v):
    return v.astype(BF16)


def _matmul_res(a, b, res, *, tm, tn, tk, prologue, name):
    T, K = a.shape
    N = b.shape[1]

    def body(a_ref, b_ref, res_ref, o_ref):
        k = pl.program_id(2)
        p = _dot(prologue(a_ref[...]), b_ref[...])

        @pl.when(k == 0)
        def _():
            o_ref[...] = res_ref[...] + p

        @pl.when(k > 0)
        def _():
            o_ref[...] += p

    return pl.pallas_call(
        body, name=name, grid=(T // tm, N // tn, K // tk),
        in_specs=[pl.BlockSpec((tm, tk), lambda i, j, k: (i, k)), pl.BlockSpec((tk, tn), lambda i, j, k: (k, j)),
                  pl.BlockSpec((tm, tn), lambda i, j, k: (i, j))],
        out_specs=pl.BlockSpec((tm, tn), lambda i, j, k: (i, j)),
        out_shape=jax.ShapeDtypeStruct((T, N), F32),
        compiler_params=_params(("parallel", "parallel", "arbitrary")),
    )(a, b, res)


def _matmul_nt(a, b, *, tm, tn, tk, name, extra=None, epilogue=None, out_dtype=F32, rides=()):
    T, K = a.shape
    if b.ndim == 3:
        per = b.shape[2] // tk
        N = b.shape[1]
        b_spec = pl.BlockSpec((None, tn, tk), lambda i, j, k: (k // per, j, k % per))
    else:
        N = b.shape[0]
        b_spec = pl.BlockSpec((tn, tk), lambda i, j, k: (j, k))
    nk = K // tk
    assert out_dtype == F32 or nk == 1
    in_specs = [pl.BlockSpec((tm, tk), lambda i, j, k: (i, k)), b_spec]
    args = [a, b]
    if extra is not None:
        in_specs.append(pl.BlockSpec((tm, tn), lambda i, j, k: (i, j)))
        args.append(extra)

    def body(*refs):
        a_ref, b_ref = refs[0], refs[1]
        o_ref = refs[-1]
        p = _dot_nt(a_ref[...].astype(BF16), b_ref[...])
        if nk == 1:
            if epilogue is not None:
                p = epilogue(p, refs[2][...])
            o_ref[...] = p.astype(out_dtype)
        else:
            k = pl.program_id(2)

            @pl.when(k == 0)
            def _():
                o_ref[...] = p

            @pl.when(k > 0)
            def _():
                o_ref[...] += p

    return _call(
        body, name=name, grid=(T // tm, N // tn, nk),
        in_specs=in_specs,
        out_specs=pl.BlockSpec((tm, tn), lambda i, j, k: (i, j)),
        out_shape=jax.ShapeDtypeStruct((T, N), out_dtype),
        sem=("parallel", "parallel", "arbitrary"), rides=rides,
    )(*args)


def _matmul_tn(a, b, *, tmo, tn, tk, name, a_prologue=_to_bf16, shards=1, rides=()):
    T, M = a.shape
    N = b.shape[1]
    if shards > 1:
        per = (N // shards) // tn
        out_spec = pl.BlockSpec((None, tmo, tn), lambda i, j, k: (j // per, i, j % per))
        out_shape = jax.ShapeDtypeStruct((shards, M, N // shards), F32)
    else:
        out_spec = pl.BlockSpec((tmo, tn), lambda i, j, k: (i, j))
        out_shape = jax.ShapeDtypeStruct((M, N), F32)

    def body(a_ref, b_ref, o_ref):
        k = pl.program_id(2)
        p = _dot_tn(a_prologue(a_ref[...]), b_ref[...].astype(BF16))

        @pl.when(k == 0)
        def _():
            o_ref[...] = p

        @pl.when(k > 0)
        def _():
            o_ref[...] += p

    return _call(
        body, name=name, grid=(M // tmo, N // tn, T // tk),
        in_specs=[pl.BlockSpec((tk, tmo), lambda i, j, k: (k, i)), pl.BlockSpec((tk, tn), lambda i, j, k: (k, j))],
        out_specs=out_spec, out_shape=out_shape,
        sem=("parallel", "parallel", "arbitrary"), rides=rides,
    )(a, b)


def _loss_bwd(h2, tgt, g, *, tm):
    T, D = h2.shape

    def body(h_ref, t_ref, g_ref, dh_ref, dg_ref, loss_ref):
        @pl.when(pl.program_id(0) == 0)
        def _():
            dg_ref[...] = jnp.zeros_like(dg_ref)
            loss_ref[...] = jnp.zeros_like(loss_ref)
        h = h_ref[...]
        gg = g_ref[...]
        r = lax.rsqrt(jnp.mean(h * h, axis=-1, keepdims=True) + EPS)
        hn = h * r
        err = hn * gg - t_ref[...]
        loss_ref[...] += 0.5 * jnp.sum(jnp.mean(err * err, axis=-1, keepdims=True), axis=0, keepdims=True)
        dy = err * (1.0 / D)
        dg_ref[...] += jnp.sum(dy * hn, axis=0, keepdims=True)
        w = dy * gg
        dh_ref[...] = r * w - h * ((r * r * r) * jnp.mean(w * h, axis=-1, keepdims=True))

    return pl.pallas_call(
        body, name="loss_bwd", grid=(T // tm,),
        in_specs=[pl.BlockSpec((tm, D), lambda i: (i, 0)), pl.BlockSpec((tm, D), lambda i: (i, 0)),
                  pl.BlockSpec((1, D), lambda i: (0, 0))],
        out_specs=[pl.BlockSpec((tm, D), lambda i: (i, 0)), pl.BlockSpec((1, D), lambda i: (0, 0)),
                   pl.BlockSpec((1, 1), lambda i: (0, 0))],
        out_shape=[jax.ShapeDtypeStruct((T, D), F32), jax.ShapeDtypeStruct((1, D), F32),
                   jax.ShapeDtypeStruct((1, 1), F32)],
        compiler_params=_params(("arbitrary",)),
    )(h2, tgt, g)


def _rms_bwd_res(dn, h, g, dres, *, tm, name, rides=()):
    T, D = h.shape

    def body(dn_ref, h_ref, g_ref, dres_ref, dh_ref, dg_ref):
        @pl.when(pl.program_id(0) == 0)
        def _():
            dg_ref[...] = jnp.zeros_like(dg_ref)
        h_ = h_ref[...]
        dn_ = dn_ref[...]
        dh, r = _rms_bwd(dn_, h_, g_ref[...])
        dg_ref[...] += jnp.sum(dn_ * (h_ * r), axis=0, keepdims=True)
        dh_ref[...] = dres_ref[...] + dh

    return _call(
        body, name=name, grid=(T // tm,),
        in_specs=[pl.BlockSpec((tm, D), lambda i: (i, 0)), pl.BlockSpec((tm, D), lambda i: (i, 0)),
                  pl.BlockSpec((1, D), lambda i: (0, 0)), pl.BlockSpec((tm, D), lambda i: (i, 0))],
        out_specs=[pl.BlockSpec((tm, D), lambda i: (i, 0)), pl.BlockSpec((1, D), lambda i: (0, 0))],
        out_shape=[jax.ShapeDtypeStruct((T, D), F32), jax.ShapeDtypeStruct((1, D), F32)],
        sem=("arbitrary",), rides=rides,
    )(dn, h, g, dres)


def _rel_distance():
    i = lax.broadcasted_iota(jnp.int32, (CHUNK, 2 * CHUNK), 0)
    j = lax.broadcasted_iota(jnp.int32, (CHUNK, 2 * CHUNK), 1)
    return i + CHUNK - j


def _bias_build(table):
    def body(tab_ref, o_ref):
        rel = _rel_distance()
        ge = [rel >= t for t in BUCKET_THR]
        for h in range(B_HEADS):
            cur = jnp.full((CHUNK, 2 * CHUNK), tab_ref[0, h], F32)
            for b in range(1, N_BUCKETS):
                cur = jnp.where(ge[b - 1], tab_ref[b, h], cur)
            o_ref[h] = cur

    return pl.pallas_call(
        body, name="bias_build",
        in_specs=[pl.BlockSpec(memory_space=pltpu.SMEM)],
        out_specs=pl.BlockSpec(memory_space=pltpu.VMEM),
        out_shape=jax.ShapeDtypeStruct((B_HEADS, CHUNK, 2 * CHUNK), F32),
    )(table)


def _bias_grad(dbias):
    def body(db_ref, o_ref, acc_ref):
        rel = _rel_distance()
        lo = [0] + BUCKET_THR
        hi = BUCKET_THR + [CHUNK]
        for b in range(N_BUCKETS):
            m = (rel >= lo[b]) & (rel < hi[b])
            for h in range(B_HEADS):
                row = b * B_HEADS + h
                acc_ref[row:row + 1, :] = jnp.sum(jnp.where(m, db_ref[h], 0.0), axis=0, keepdims=True)
        o_ref[...] = jnp.sum(acc_ref[...], axis=1, keepdims=True)

    return pl.pallas_call(
        body, name="bias_grad",
        in_specs=[pl.BlockSpec(memory_space=pltpu.VMEM)],
        out_specs=pl.BlockSpec(memory_space=pltpu.VMEM),
        out_shape=jax.ShapeDtypeStruct((N_BUCKETS * B_HEADS, 1), F32),
        scratch_shapes=[pltpu.VMEM((N_BUCKETS * B_HEADS, 2 * CHUNK), F32)],
    )(dbias)


def _causal_mask():
    t = lax.broadcasted_iota(jnp.int32, (CHUNK, CHUNK), 0)
    s = lax.broadcasted_iota(jnp.int32, (CHUNK, CHUNK), 1)
    return s <= t


def _band_mask(n):
    rel = _rel_distance()
    j = lax.broadcasted_iota(jnp.int32, (CHUNK, 2 * CHUNK), 1)
    return (rel >= 0) & (rel < CHUNK) & ((n > 0) | (j >= CHUNK))


def _gate_forward(u, v, lg, lb, wc, bs):
    ug = _gelu(u)
    vg = _gelu(v)
    mu = jnp.mean(vg, axis=-1, keepdims=True)
    xc = vg - mu
    rstd = lax.rsqrt(jnp.mean(xc * xc, axis=-1, keepdims=True) + EPS)
    xhat = xc * rstd
    vl = (xhat * lg + lb).astype(BF16)
    mixed = _dot(wc, vl) + bs
    return ug, xhat, rstd, vl, mixed


def _softmax_band(q, kband, bias, mask, sink):
    s = _dot_nt(q, kband) * SCALE + bias
    s = jnp.where(mask, s, NEG)
    m = jnp.maximum(jnp.max(s, axis=-1, keepdims=True), sink)
    p = jnp.exp(s - m)
    e_sink = jnp.exp(sink - m)
    inv = 1.0 / (jnp.sum(p, axis=-1, keepdims=True) + e_sink)
    return p * inv, e_sink * inv


def _mixer_fwd(proj, lg, lb, wsp, bs_col, sinks, bias, ga, gb, rides=()):
    T = proj.shape[0]
    nb = T // CHUNK

    def body(u_ref, v_ref, q_ref, kvc_ref, kvp_ref, lg_ref, lb_ref, w_ref, bs_ref, sink_ref, bias_ref,
             ga_ref, gb_ref, mixed_ref, ab_ref):
        n = pl.program_id(0)
        causal = _causal_mask()
        ssq = jnp.zeros((CHUNK, 1), F32)
        for g in range(A_GROUPS):
            cols = slice(g * CHUNK, (g + 1) * CHUNK)
            wc = jnp.where(causal, w_ref[g], 0.0).astype(BF16)
            ug, _, _, _, mixed = _gate_forward(u_ref[:, cols], v_ref[:, cols], lg_ref[g:g + 1, :], lb_ref[g:g + 1, :],
                                               wc, bs_ref[g])
            a = ug * mixed
            ab_ref[:, cols] = a
            ssq = ssq + jnp.sum(a * a, axis=-1, keepdims=True)
        ra = lax.rsqrt(ssq * (1.0 / A_WIDTH) + EPS)
        mixed_ref[:, :A_WIDTH] = ((ab_ref[:, :A_WIDTH] * ra) * ga_ref[...]).astype(BF16)

        mask = _band_mask(n)
        kvc = kvc_ref[...].astype(BF16)
        kvp = kvp_ref[...].astype(BF16)
        band = jnp.concatenate([kvp, kvc], axis=0)
        ssq = jnp.zeros((CHUNK, 1), F32)
        for h in range(B_HEADS):
            kv = h // Q_PER_KV
            q = q_ref[:, h * HEAD_DIM:(h + 1) * HEAD_DIM].astype(BF16)
            kband = band[:, kv * HEAD_DIM:(kv + 1) * HEAD_DIM]
            vband = band[:, KV_WIDTH + kv * HEAD_DIM:KV_WIDTH + (kv + 1) * HEAD_DIM]
            p, _ = _softmax_band(q, kband, bias_ref[h], mask, sink_ref[0, h])
            o = _dot(p.astype(BF16), vband)
            ab_ref[:, A_WIDTH + h * HEAD_DIM:A_WIDTH + (h + 1) * HEAD_DIM] = o
            ssq = ssq + jnp.sum(o * o, axis=-1, keepdims=True)
        rb = lax.rsqrt(ssq * (1.0 / B_WIDTH) + EPS)
        mixed_ref[:, A_WIDTH:] = ((ab_ref[:, A_WIDTH:] * rb) * gb_ref[...]).astype(BF16)

    full = lambda *shape: pl.BlockSpec(shape, lambda n: (0,) * len(shape))
    return _call(
        body, name="mixer_fwd", grid=(nb,),
        in_specs=[pl.BlockSpec((CHUNK, A_WIDTH), lambda n: (n, 0)),
                  pl.BlockSpec((CHUNK, A_WIDTH), lambda n: (n, 1)),
                  pl.BlockSpec((CHUNK, B_WIDTH), lambda n: (n, 2)),
                  pl.BlockSpec((CHUNK, 2 * KV_WIDTH), lambda n: (n, 12)),
                  pl.BlockSpec((CHUNK, 2 * KV_WIDTH), lambda n: (jnp.maximum(n - 1, 0), 12)),
                  full(A_GROUPS, CHUNK), full(A_GROUPS, CHUNK), full(A_GROUPS, CHUNK, CHUNK), full(A_GROUPS, CHUNK, 1),
                  pl.BlockSpec(memory_space=pltpu.SMEM), full(B_HEADS, CHUNK, 2 * CHUNK),
                  full(1, A_WIDTH), full(1, B_WIDTH)],
        out_specs=[pl.BlockSpec((CHUNK, D_MODEL), lambda n: (n, 0)), pl.BlockSpec((CHUNK, D_MODEL), lambda n: (n, 0))],
        out_shape=[jax.ShapeDtypeStruct((T, D_MODEL), BF16), jax.ShapeDtypeStruct((T, D_MODEL), F32)],
        sem=("parallel",), rides=rides,
    )(proj, proj, proj, proj, proj, lg, lb, wsp, bs_col, sinks, bias, ga, gb)


def _gmlp_bwd(proj, ab, dmixed, ga, lg, lb, wsp, bs_col, rides=()):
    T = proj.shape[0]
    nb = T // CHUNK

    def body(u_ref, v_ref, a_ref, dna_ref, ga_ref, lg_ref, lb_ref, w_ref, bs_ref,
             dp_ref, dga_ref, dw_ref, dbs_ref, dlg_ref, dlb_ref):
        @pl.when(pl.program_id(0) == 0)
        def _():
            for r in (dga_ref, dw_ref, dbs_ref, dlg_ref, dlb_ref):
                r[...] = jnp.zeros_like(r)
        causal = _causal_mask()
        a_all = a_ref[...]
        dna = dna_ref[...]
        da_all, ra = _rms_bwd(dna, a_all, ga_ref[...])
        dga_ref[...] += jnp.sum(dna * (a_all * ra), axis=0, keepdims=True)
        for g in range(A_GROUPS):
            cols = slice(g * CHUNK, (g + 1) * CHUNK)
            wc = jnp.where(causal, w_ref[g], 0.0).astype(BF16)
            lgg = lg_ref[g:g + 1, :]
            u = u_ref[:, cols]
            v = v_ref[:, cols]
            ug, xhat, rstd, vl, mixed = _gate_forward(u, v, lgg, lb_ref[g:g + 1, :], wc, bs_ref[g])
            da = da_all[:, cols]
            dug = da * mixed
            dmg = da * ug
            dmg_b = dmg.astype(BF16)
            dbs_ref[g] += jnp.sum(dmg, axis=-1, keepdims=True)
            dw_ref[g] += jnp.where(causal, _dot_nt(dmg_b, vl), 0.0)
            dvl = _dot_tn(wc, dmg_b)
            dlg_ref[g:g + 1, :] += jnp.sum(dvl * xhat, axis=0, keepdims=True)
            dlb_ref[g:g + 1, :] += jnp.sum(dvl, axis=0, keepdims=True)
            dxh = dvl * lgg
            dvg = rstd * (dxh - jnp.mean(dxh, axis=-1, keepdims=True)
                          - xhat * jnp.mean(dxh * xhat, axis=-1, keepdims=True))
            _, gu = _gelu_and_grad(u)
            _, gv = _gelu_and_grad(v)
            dp_ref[:, cols] = (dug * gu).astype(BF16)
            dp_ref[:, A_WIDTH + g * CHUNK:A_WIDTH + (g + 1) * CHUNK] = (dvg * gv).astype(BF16)

    full = lambda *shape: pl.BlockSpec(shape, lambda n: (0,) * len(shape))
    return _call(
        body, name="gmlp_bwd", grid=(nb,),
        in_specs=[pl.BlockSpec((CHUNK, A_WIDTH), lambda n: (n, 0)),
                  pl.BlockSpec((CHUNK, A_WIDTH), lambda n: (n, 1)),
                  pl.BlockSpec((CHUNK, A_WIDTH), lambda n: (n, 0)),
                  pl.BlockSpec((CHUNK, A_WIDTH), lambda n: (n, 0)),
                  full(1, A_WIDTH), full(A_GROUPS, CHUNK), full(A_GROUPS, CHUNK), full(A_GROUPS, CHUNK, CHUNK),
                  full(A_GROUPS, CHUNK, 1)],
        out_specs=[pl.BlockSpec((CHUNK, 2 * A_WIDTH), lambda n: (n, 0)),
                   full(1, A_WIDTH), full(A_GROUPS, CHUNK, CHUNK), full(A_GROUPS, CHUNK, 1),
                   full(A_GROUPS, CHUNK), full(A_GROUPS, CHUNK)],
        out_shape=[jax.ShapeDtypeStruct((T, 2 * A_WIDTH), BF16),
                   jax.ShapeDtypeStruct((1, A_WIDTH), F32), jax.ShapeDtypeStruct((A_GROUPS, CHUNK, CHUNK), F32),
                   jax.ShapeDtypeStruct((A_GROUPS, CHUNK, 1), F32), jax.ShapeDtypeStruct((A_GROUPS, CHUNK), F32),
                   jax.ShapeDtypeStruct((A_GROUPS, CHUNK), F32)],
        sem=("arbitrary",), rides=rides,
    )(proj, proj, ab, dmixed, ga, lg, lb, wsp, bs_col)


def _attn_bwd(proj, ab, dmixed, gb, sinks, bias, rides=()):
    T = proj.shape[0]
    nb = T // CHUNK
    qn = lambda n: jnp.minimum(n, nb - 1)

    def body(q_ref, kvc_ref, kvp_ref, o_ref, dnb_ref, gb_ref, sink_ref, bias_ref,
             dq_ref, dkv_ref, dgb_ref, dsink_ref, dbias_ref, carry_ref, sacc_ref):
        n = pl.program_id(0)

        @pl.when(n == 0)
        def _():
            carry_ref[...] = jnp.zeros_like(carry_ref)
            sacc_ref[...] = jnp.zeros_like(sacc_ref)
            dgb_ref[...] = jnp.zeros_like(dgb_ref)
            dbias_ref[...] = jnp.zeros_like(dbias_ref)

        @pl.when(n < nb)
        def _():
            mask = _band_mask(n)
            o_all = o_ref[...]
            dnb = dnb_ref[...]
            do_all, rb = _rms_bwd(dnb, o_all, gb_ref[...])
            dgb_ref[...] += jnp.sum(dnb * (o_all * rb), axis=0, keepdims=True)
            band = jnp.concatenate([kvp_ref[...].astype(BF16), kvc_ref[...].astype(BF16)], axis=0)
            dkv_parts = []
            for kv in range(B_HEADS // Q_PER_KV):
                kband = band[:, kv * HEAD_DIM:(kv + 1) * HEAD_DIM]
                vband = band[:, KV_WIDTH + kv * HEAD_DIM:KV_WIDTH + (kv + 1) * HEAD_DIM]
                dk = jnp.zeros((2 * CHUNK, HEAD_DIM), F32)
                dv = jnp.zeros((2 * CHUNK, HEAD_DIM), F32)
                for hq in range(Q_PER_KV):
                    h = kv * Q_PER_KV + hq
                    hc = slice(h * HEAD_DIM, (h + 1) * HEAD_DIM)
                    q = q_ref[:, hc].astype(BF16)
                    p, p_sink = _softmax_band(q, kband, bias_ref[h], mask, sink_ref[0, h])
                    do = do_all[:, hc].astype(BF16)
                    dp = _dot_nt(do, vband)
                    delta = jnp.sum(p * dp, axis=-1, keepdims=True)
                    ds = p * (dp - delta)
                    sacc_ref[:, h:h + 1] += -(p_sink * delta)
                    dbias_ref[h] += ds
                    dsr = (ds * SCALE).astype(BF16)
                    dq_ref[:, hc] = _dot(dsr, kband).astype(BF16)
                    dk = dk + _dot_tn(dsr, q)
                    dv = dv + _dot_tn(p.astype(BF16), do)
                dkv_parts.append((dk, dv))
            dband = jnp.concatenate([dkv_parts[0][0], dkv_parts[1][0], dkv_parts[0][1], dkv_parts[1][1]], axis=1)
            dkv_ref[...] = (carry_ref[...] + dband[:CHUNK]).astype(BF16)
            carry_ref[...] = dband[CHUNK:]

        @pl.when(n == nb)
        def _():
            dkv_ref[...] = carry_ref[...].astype(BF16)
            dsink_ref[...] = jnp.sum(sacc_ref[...], axis=0, keepdims=True)

    full = lambda *shape: pl.BlockSpec(shape, lambda n: (0,) * len(shape))
    return _call(
        body, name="attn_bwd", grid=(nb + 1,),
        in_specs=[pl.BlockSpec((CHUNK, B_WIDTH), lambda n: (qn(n), 2)),
                  pl.BlockSpec((CHUNK, 2 * KV_WIDTH), lambda n: (qn(n), 12)),
                  pl.BlockSpec((CHUNK, 2 * KV_WIDTH), lambda n: (jnp.maximum(qn(n) - 1, 0), 12)),
                  pl.BlockSpec((CHUNK, B_WIDTH), lambda n: (qn(n), 1)),
                  pl.BlockSpec((CHUNK, B_WIDTH), lambda n: (qn(n), 1)),
                  full(1, B_WIDTH), pl.BlockSpec(memory_space=pltpu.SMEM), full(B_HEADS, CHUNK, 2 * CHUNK)],
        out_specs=[pl.BlockSpec((CHUNK, B_WIDTH), lambda n: (qn(n), 0)),
                   pl.BlockSpec((CHUNK, 2 * KV_WIDTH), lambda n: (jnp.maximum(n - 1, 0), 0)),
                   full(1, B_WIDTH), full(1, B_HEADS), full(B_HEADS, CHUNK, 2 * CHUNK)],
        out_shape=[jax.ShapeDtypeStruct((T, B_WIDTH), BF16), jax.ShapeDtypeStruct((T, 2 * KV_WIDTH), BF16),
                   jax.ShapeDtypeStruct((1, B_WIDTH), F32), jax.ShapeDtypeStruct((1, B_HEADS), F32),
                   jax.ShapeDtypeStruct((B_HEADS, CHUNK, 2 * CHUNK), F32)],
        scratch_shapes=[pltpu.VMEM((CHUNK, 2 * KV_WIDTH), F32), pltpu.VMEM((CHUNK, B_HEADS), F32)],
        sem=("arbitrary",), rides=rides,
    )(proj, proj, proj, ab, dmixed, gb, sinks, bias)


def _sq_relu_grad(acc, z):
    return acc * (2.0 * jnp.maximum(z, 0.0))


def _local_step(x, tgt, sp, win, wo, wu, wd):
    T = x.shape[0]
    tm = min(512, T)
    tk = min(512, T)
    lg = sp["gate_norm_g"].reshape(A_GROUPS, CHUNK)
    lb = sp["gate_norm_b"].reshape(A_GROUPS, CHUNK)
    wsp = sp["w_spatial"].reshape(A_GROUPS, CHUNK, CHUNK)
    bs_col = sp["b_spatial"].reshape(A_GROUPS, CHUNK, 1)
    sinks = sp["attn_sinks"].reshape(1, B_HEADS)
    ga = sp["out_norm_a_g"].reshape(1, A_WIDTH)
    gb = sp["out_norm_b_g"].reshape(1, B_WIDTH)
    g1 = sp["mix_norm_g"].reshape(1, D_MODEL)
    g2 = sp["ffn_norm_g"].reshape(1, D_MODEL)
    gf = sp["final_norm_g"].reshape(1, D_MODEL)

    bias = _bias_build(sp["rel_bias_table"])
    n1, proj = _norm_matmul(x, g1, win, tm=tm, tn=PROJ_WIDTH // 2, name="in_proj")
    mixed, ab = _mixer_fwd(proj, lg, lb, wsp, bs_col, sinks, bias, ga, gb)
    h1 = _matmul_res(mixed, wo, x, tm=tm, tn=1024, tk=D_MODEL, prologue=_to_bf16, name="out_proj")
    n2, zp = _norm_matmul(h1, g2, wu, tm=tm, tn=1024, name="up_proj")
    h2 = _matmul_res(zp, wd, h1, tm=tm, tn=1024, tk=2048, prologue=_sq_relu_bf16, name="down_proj")

    dh2, dgf, loss = _loss_bwd(h2, tgt, gf, tm=tm)
    dzp = _matmul_nt(dh2, wd, tm=tm, tn=1024, tk=D_MODEL, name="bwd_dz", extra=zp, epilogue=_sq_relu_grad,
                     out_dtype=BF16)
    dwd = _matmul_tn(zp, dh2, tmo=1024, tn=1024, tk=tk, name="grad_w_down", a_prologue=_sq_relu_bf16)
    dwu = _matmul_tn(n2, dzp, tmo=1024, tn=1024, tk=tk, name="grad_w_up", shards=N_CHIPS)
    dn2 = _matmul_nt(dzp, wu, tm=tm, tn=1024, tk=2048, name="bwd_dn2")
    dh1, dg2 = _rms_bwd_res(dn2, h1, g2, dh2, tm=tm, name="ffn_norm_bwd")
    dwo = _matmul_tn(mixed, dh1, tmo=1024, tn=1024, tk=tk, name="grad_w_out")
    dmixed = _matmul_nt(dh1, wo, tm=tm, tn=1024, tk=D_MODEL, name="bwd_dmixed")
    duv, dga, dwsp, dbs, dlg, dlb = _gmlp_bwd(proj, ab, dmixed, ga, lg, lb, wsp, bs_col)
    dq, dkv, dgb, dsinks, dbias = _attn_bwd(proj, ab, dmixed, gb, sinks, bias)
    dtable = _bias_grad(dbias)
    dproj = jnp.concatenate([duv, dq, dkv], axis=1)
    dwin = _matmul_tn(n1, dproj, tmo=1024, tn=PROJ_WIDTH // 2, tk=tk, name="grad_w_in")
    dn1 = _matmul_nt(dproj, win, tm=tm, tn=1024, tk=PROJ_WIDTH, name="bwd_dn1")
    dx, dg1 = _rms_bwd_res(dn1, x, g1, dh1, tm=tm, name="mix_norm_bwd")

    small = {
        "rel_bias_table": dtable.reshape(N_BUCKETS, B_HEADS), "mix_norm_g": dg1, "gate_norm_g": dlg, "gate_norm_b": dlb,
        "w_spatial": dwsp, "b_spatial": dbs, "attn_sinks": dsinks, "out_norm_a_g": dga, "out_norm_b_g": dgb,
        "ffn_norm_g": dg2, "final_norm_g": dgf,
    }
    return loss, dx, (dwin, dwo, dwu, dwd), small


def _place():
    x, y, c = lax.axis_index("x"), lax.axis_index("y"), lax.axis_index("c")
    chips = [(1 - x, y), (x, 1 - y), (1 - x, 1 - y)]
    return x, y, c, chips


def _remote(src, dst, send_sem, recv_sem, to):
    return pltpu.make_async_remote_copy(src_ref=src, dst_ref=dst, send_sem=send_sem, recv_sem=recv_sem,
                                        device_id=to, device_id_type=MESH)


def _core_index():
    return lax.axis_index("c").astype(jnp.int32).reshape(1)


def _chip_index():
    return (2 * lax.axis_index("x") + lax.axis_index("y")).astype(jnp.int32).reshape(1)


def _cast_into_slot(w, *, tm, name):
    R, C = w.shape

    def body(me_ref, w_ref, o_ref):
        del me_ref
        o_ref[...] = w_ref[...].astype(BF16)

    return pl.pallas_call(
        body, name=name,
        grid_spec=pltpu.PrefetchScalarGridSpec(
            num_scalar_prefetch=1, grid=(R // tm,),
            in_specs=[pl.BlockSpec((tm, C), lambda i, me: (i, 0))],
            out_specs=pl.BlockSpec((None, tm, C), lambda i, me: (me[0], i, 0))),
        out_shape=jax.ShapeDtypeStruct((N_CHIPS, R, C), BF16), compiler_params=_params(("parallel",)),
    )(_chip_index(), w)


def _gather_weights(slots):
    nw = len(slots)

    def body(*refs):
        fulls = refs[nw:2 * nw]
        send_sems, recv_sems = refs[2 * nw:]
        x, y, c, chips = _place()
        me = 2 * x + y
        sends = []
        for w in range(nw):
            hr = fulls[w].shape[1] // 2
            rows = pl.ds(c * hr, hr)
            for j, chip in enumerate(chips):
                mine = fulls[w].at[me, rows, :]
                cp = _remote(mine, mine, send_sems.at[6 * w + j], recv_sems.at[6 * w + j], (*chip, c))
                cp.start()
                sends.append(cp)
        for w in range(nw):
            hr = fulls[w].shape[1] // 2
            rows = pl.ds(c * hr, hr)
            for j, chip in enumerate(chips):
                landed = fulls[w].at[2 * chip[0] + chip[1], rows, :]
                _remote(landed, landed, send_sems.at[6 * w + j], recv_sems.at[6 * w + j], (x, y, c)).wait_recv()
                cp = _remote(landed, landed, send_sems.at[6 * w + 3 + j], recv_sems.at[6 * w + 3 + j], (x, y, 1 - c))
                cp.start()
                sends.append(cp)
        for w in range(nw):
            hr = fulls[w].shape[1] // 2
            rows = pl.ds((1 - c) * hr, hr)
            for j, chip in enumerate(chips):
                other = fulls[w].at[2 * chip[0] + chip[1], rows, :]
                _remote(other, other, send_sems.at[6 * w + 3 + j], recv_sems.at[6 * w + 3 + j], (x, y, c)).wait_recv()
        for cp in sends:
            cp.wait_send()

    any_spec = pl.BlockSpec(memory_space=pl.ANY)
    return pl.pallas_call(
        body, name="gather_weights",
        in_specs=[any_spec] * nw, out_specs=[any_spec] * nw,
        out_shape=[jax.ShapeDtypeStruct(s.shape, s.dtype) for s in slots],
        scratch_shapes=[pltpu.SemaphoreType.DMA((6 * nw,)), pltpu.SemaphoreType.DMA((6 * nw,))],
        input_output_aliases={w: w for w in range(nw)},
    )(*slots)


def _sibling_halves(grads):
    nw = len(grads)

    def body(*refs):
        gs, outs = refs[:nw], refs[nw:2 * nw]
        send_sems, recv_sems = refs[2 * nw:]
        x, y, c, _ = _place()
        cps = []
        for w in range(nw):
            hr = gs[w].shape[1] // 2
            cp = _remote(gs[w].at[:, pl.ds((1 - c) * hr, hr), :], outs[w], send_sems.at[w], recv_sems.at[w],
                         (x, y, 1 - c))
            cp.start()
            cps.append(cp)
        for cp in cps:
            cp.wait()

    any_spec = pl.BlockSpec(memory_space=pl.ANY)
    return pl.pallas_call(
        body, name="rs_sibling_halves",
        in_specs=[any_spec] * nw, out_specs=[any_spec] * nw,
        out_shape=[jax.ShapeDtypeStruct((g.shape[0], g.shape[1] // 2, g.shape[2]), g.dtype) for g in grads],
        scratch_shapes=[pltpu.SemaphoreType.DMA((nw,)), pltpu.SemaphoreType.DMA((nw,))],
    )(*grads)


def _pair_sum_bf16(g, got, *, tm, name):
    S, R, C = g.shape
    hr = R // 2
    nt = hr // tm

    def body(c_ref, g_ref, got_ref, o_ref):
        del c_ref
        o_ref[...] = (g_ref[...] + got_ref[...]).astype(BF16)

    return pl.pallas_call(
        body, name=name,
        grid_spec=pltpu.PrefetchScalarGridSpec(
            num_scalar_prefetch=1, grid=(S, nt),
            in_specs=[pl.BlockSpec((None, tm, C), lambda s, i, c: (s, c[0] * nt + i, 0)),
                      pl.BlockSpec((None, tm, C), lambda s, i, c: (s, i, 0))],
            out_specs=pl.BlockSpec((None, tm, C), lambda s, i, c: (s, i, 0))),
        out_shape=jax.ShapeDtypeStruct((S, hr, C), BF16),
        compiler_params=_params(("parallel", "parallel")),
    )(_core_index(), g, got)


def _scatter_to_owners(pairs):
    nw = len(pairs)

    def body(*refs):
        qs, outs = refs[:nw], refs[nw:2 * nw]
        send_sems, recv_sems = refs[2 * nw:]
        x, y, c, chips = _place()
        cps = []
        for w in range(nw):
            for j, chip in enumerate(chips):
                cp = _remote(qs[w].at[2 * chip[0] + chip[1]], outs[w].at[j], send_sems.at[3 * w + j],
                             recv_sems.at[3 * w + j], (*chip, c))
                cp.start()
                cps.append(cp)
        for cp in cps:
            cp.wait()

    any_spec = pl.BlockSpec(memory_space=pl.ANY)
    return pl.pallas_call(
        body, name="rs_scatter_to_owners",
        in_specs=[any_spec] * nw, out_specs=[any_spec] * nw,
        out_shape=[jax.ShapeDtypeStruct((3,) + q.shape[1:], q.dtype) for q in pairs],
        scratch_shapes=[pltpu.SemaphoreType.DMA((3 * nw,)), pltpu.SemaphoreType.DMA((3 * nw,))],
    )(*pairs)


def _owner_sum(g, got, others, *, tm, name):
    S, R, C = g.shape
    hr = R // 2
    nt = hr // tm

    def body(idx_ref, g_ref, got_ref, o_ref_in, out_ref):
        del idx_ref
        acc = g_ref[...] + got_ref[...]
        for j in range(3):
            acc = acc + o_ref_in[j].astype(F32)
        out_ref[...] = acc

    return pl.pallas_call(
        body, name=name,
        grid_spec=pltpu.PrefetchScalarGridSpec(
            num_scalar_prefetch=1, grid=(nt,),
            in_specs=[pl.BlockSpec((None, tm, C), lambda i, p: (p[1], p[0] * nt + i, 0)),
                      pl.BlockSpec((None, tm, C), lambda i, p: (p[1], i, 0)),
                      pl.BlockSpec((3, tm, C), lambda i, p: (0, i, 0))],
            out_specs=pl.BlockSpec((tm, C), lambda i, p: (i, 0))),
        out_shape=jax.ShapeDtypeStruct((hr, C), F32),
        compiler_params=_params(("parallel",)),
    )(jnp.concatenate([_core_index(), _chip_index()]), g, got, others)


def _swap_halves(halves):
    nw = len(halves)

    def body(*refs):
        hs, outs = refs[:nw], refs[nw:2 * nw]
        send_sems, recv_sems = refs[2 * nw:]
        x, y, c, _ = _place()
        cps = []
        for w in range(nw):
            cp = _remote(hs[w], outs[w], send_sems.at[w], recv_sems.at[w], (x, y, 1 - c))
            cp.start()
            cps.append(cp)
        for cp in cps:
            cp.wait()

    any_spec = pl.BlockSpec(memory_space=pl.ANY)
    return pl.pallas_call(
        body, name="rs_swap_halves",
        in_specs=[any_spec] * nw, out_specs=[any_spec] * nw,
        out_shape=[jax.ShapeDtypeStruct(h.shape, h.dtype) for h in halves],
        scratch_shapes=[pltpu.SemaphoreType.DMA((nw,)), pltpu.SemaphoreType.DMA((nw,))],
    )(*halves)


def _all_reduce_small(packed):
    R, C = packed.shape

    def body(in_ref, out_ref, slots, send_sems, recv_sems):
        x, y, c, _ = _place()
        me = 4 * x + 2 * y + c
        cps = []
        for k in range(1, N_DEV):
            p = (me + k) % N_DEV
            cp = _remote(in_ref, slots.at[me], send_sems.at[k - 1], recv_sems.at[k - 1], (p // 4, (p // 2) % 2, p % 2))
            cp.start()
            cps.append(cp)
        slots[me] = in_ref[...]
        for k in range(1, N_DEV):
            src = (me + N_DEV - k) % N_DEV
            _remote(in_ref, slots.at[src], send_sems.at[k - 1], recv_sems.at[k - 1], (x, y, c)).wait_recv()
        for cp in cps:
            cp.wait_send()
        acc = slots[0]
        for d in range(1, N_DEV):
            acc = acc + slots[d]
        out_ref[...] = acc

    vmem = pl.BlockSpec(memory_space=pltpu.VMEM)
    return pl.pallas_call(
        body, name="all_reduce_small", in_specs=[vmem], out_specs=vmem,
        out_shape=jax.ShapeDtypeStruct((R, C), F32),
        scratch_shapes=[pltpu.VMEM((N_DEV, R, C), F32), pltpu.SemaphoreType.DMA((N_DEV - 1,)),
                        pltpu.SemaphoreType.DMA((N_DEV - 1,))],
        compiler_params=_params(),
    )(packed)


def _adamw_math(w, g, m, v):
    m = ADAM_B1 * m + (1.0 - ADAM_B1) * g
    v = ADAM_B2 * v + (1.0 - ADAM_B2) * (g * g)
    m_hat = m / (1.0 - ADAM_B1 ** ADAM_STEP)
    v_hat = v / (1.0 - ADAM_B2 ** ADAM_STEP)
    delta = -ADAM_LR * (m_hat / (jnp.sqrt(v_hat) + ADAM_EPS) + ADAM_WD * w)
    return delta, m, v


def _adamw(w, g, m, v, *, tm, name):
    R, C = w.shape

    def body(w_ref, g_ref, m_ref, v_ref, d_ref, nm_ref, nv_ref):
        d_ref[...], nm_ref[...], nv_ref[...] = _adamw_math(w_ref[...], g_ref[...], m_ref[...], v_ref[...])

    spec = pl.BlockSpec((tm, C), lambda i: (i, 0))
    return pl.pallas_call(
        body, name=name, grid=(R // tm,), in_specs=[spec] * 4, out_specs=[spec] * 3,
        out_shape=[jax.ShapeDtypeStruct((R, C), F32)] * 3, compiler_params=_params(("parallel",)),
    )(w, g, m, v)


def _adamw_halves(w, own, got, m, v, *, tm, name):
    R, C = w.shape
    nt = (R // 2) // tm

    def body(c_ref, w_ref, own_ref, got_ref, m_ref, v_ref, g_ref, d_ref, nm_ref, nv_ref):
        g = jnp.where(pl.program_id(0) == c_ref[0], own_ref[...], got_ref[...])
        g_ref[...] = g
        d_ref[...], nm_ref[...], nv_ref[...] = _adamw_math(w_ref[...], g, m_ref[...], v_ref[...])

    whole = pl.BlockSpec((tm, C), lambda h, i, c: (h * nt + i, 0))
    half = pl.BlockSpec((tm, C), lambda h, i, c: (i, 0))
    return pl.pallas_call(
        body, name=name,
        grid_spec=pltpu.PrefetchScalarGridSpec(
            num_scalar_prefetch=1, grid=(2, nt),
            in_specs=[whole, half, half, whole, whole], out_specs=[whole] * 4),
        out_shape=[jax.ShapeDtypeStruct((R, C), F32)] * 4, compiler_params=_params(("parallel", "parallel")),
    )(_core_index(), w, own, got, m, v)


SMALL = ["rel_bias_table", "mix_norm_g", "gate_norm_g", "gate_norm_b", "w_spatial", "b_spatial", "attn_sinks",
         "out_norm_a_g", "out_norm_b_g", "ffn_norm_g", "final_norm_g"]
LARGE = ["w_in", "w_out", "w_up", "w_down"]
WEIGHTS = ["rel_bias_table", "mix_norm_g", "w_in", "gate_norm_g", "gate_norm_b", "w_spatial", "b_spatial", "attn_sinks",
           "out_norm_a_g", "out_norm_b_g", "w_out", "ffn_norm_g", "w_up", "w_down", "final_norm_g"]
PACK_UNIT = 8 * 128


def _pack(parts):
    rows = []
    for p in parts:
        flat = p.reshape(-1)
        pad = (-flat.shape[0]) % PACK_UNIT
        rows.append(jnp.pad(flat, (0, pad)).reshape(-1, 128))
    return jnp.concatenate(rows, axis=0)


def _unpack(packed, like):
    out, row = [], 0
    for p in like:
        n = math.prod(p.shape)
        nrows = (n + PACK_UNIT - 1) // PACK_UNIT * 8
        out.append(packed[row:row + nrows].reshape(-1)[:n].reshape(p.shape))
        row += nrows
    return out


def kernel(x, rel_bias_table, mix_norm_g, w_in, gate_norm_g, gate_norm_b, w_spatial, b_spatial, attn_sinks, out_norm_a_g, out_norm_b_g, w_out, ffn_norm_g, w_up, w_down, final_norm_g, loss_target, m_rel_bias_table, m_mix_norm_g, m_w_in, m_gate_norm_g, m_gate_norm_b, m_w_spatial, m_b_spatial, m_attn_sinks, m_out_norm_a_g, m_out_norm_b_g, m_w_out, m_ffn_norm_g, m_w_up, m_w_down, m_final_norm_g, v_rel_bias_table, v_mix_norm_g, v_w_in, v_gate_norm_g, v_gate_norm_b, v_w_spatial, v_b_spatial, v_attn_sinks, v_out_norm_a_g, v_out_norm_b_g, v_w_out, v_ffn_norm_g, v_w_up, v_w_down, v_final_norm_g):
    args = dict(locals())
    wts = {n: args[n] for n in WEIGHTS}
    mom = {n: args["m_" + n] for n in WEIGHTS}
    var = {n: args["v_" + n] for n in WEIGHTS}
    sp = {n: wts[n] for n in SMALL}
    x2, tgt = x[0], loss_target[0]
    T = x2.shape[0]
    tm = min(512, T)
    tk = min(512, T)
    lg = sp["gate_norm_g"].reshape(A_GROUPS, CHUNK)
    lb = sp["gate_norm_b"].reshape(A_GROUPS, CHUNK)
    wsp = sp["w_spatial"].reshape(A_GROUPS, CHUNK, CHUNK)
    bs_col = sp["b_spatial"].reshape(A_GROUPS, CHUNK, 1)
    sinks = sp["attn_sinks"].reshape(1, B_HEADS)
    ga = sp["out_norm_a_g"].reshape(1, A_WIDTH)
    gb = sp["out_norm_b_g"].reshape(1, B_WIDTH)
    g1 = sp["mix_norm_g"].reshape(1, D_MODEL)
    g2 = sp["ffn_norm_g"].reshape(1, D_MODEL)
    gf = sp["final_norm_g"].reshape(1, D_MODEL)

    def pair_sum(n, g, got):
        return _pair_sum_bf16(g, got, tm=256, name="rs_pair_sum_" + n)

    def owner_sum(n, g, got, others):
        return _owner_sum(g, got, others, tm=256, name="rs_owner_sum_" + n)

    own = [wts[n].reshape(wts[n].shape[1:]) for n in LARGE]
    s_in, s_out, s_up, s_down = [_cast_into_slot(w, tm=256, name="cast_" + n) for n, w in zip(LARGE, own)]
    ((g_in,),) = _carrier([_ride_gather(s_in)], name="gather_w_in")
    win = g_in.transpose(1, 0, 2).reshape(D_MODEL, PROJ_WIDTH)
    bias = _bias_build(sp["rel_bias_table"])
    (n1, proj), ((g_out,),) = _norm_matmul(x2, g1, win, tm=tm, tn=PROJ_WIDTH // 2, name="in_proj",
                                           rides=[_ride_gather(s_out)])
    wo = g_out.reshape(A_WIDTH + B_WIDTH, D_MODEL)
    (mixed, ab), ((wu,),) = _mixer_fwd(proj, lg, lb, wsp, bs_col, sinks, bias, ga, gb, rides=[_ride_gather(s_up)])
    h1 = _matmul_res(mixed, wo, x2, tm=tm, tn=1024, tk=D_MODEL, prologue=_to_bf16, name="out_proj")
    (n2, zp), ((g_down,),) = _norm_matmul(h1, g2, wu, tm=tm, tn=1024, name="up_proj", rides=[_ride_gather(s_down)])
    wd = g_down.reshape(D_FF, D_MODEL)
    h2 = _matmul_res(zp, wd, h1, tm=tm, tn=1024, tk=2048, prologue=_sq_relu_bf16, name="down_proj")

    dh2, dgf, loss = _loss_bwd(h2, tgt, gf, tm=tm)
    dzp = _matmul_nt(dh2, wd, tm=tm, tn=1024, tk=D_MODEL, name="bwd_dz", extra=zp, epilogue=_sq_relu_grad,
                     out_dtype=BF16)
    dwd = _matmul_tn(zp, dh2, tmo=1024, tn=1024, tk=tk, name="grad_w_down", a_prologue=_sq_relu_bf16)
    dwd = dwd.reshape(N_CHIPS, D_FF // N_CHIPS, D_MODEL)
    dwu, ((r_d,),) = _matmul_tn(n2, dzp, tmo=1024, tn=1024, tk=tk, name="grad_w_up", shards=N_CHIPS,
                                rides=[_ride_sibling_halves(dwd)])
    q_d = pair_sum("w_down", dwd, r_d)
    dn2, ((o_d,), (r_u,)) = _matmul_nt(dzp, wu, tm=tm, tn=1024, tk=2048, name="bwd_dn2",
                                       rides=[_ride_scatter(q_d), _ride_sibling_halves(dwu)])
    h_d = owner_sum("w_down", dwd, r_d, o_d)
    q_u = pair_sum("w_up", dwu, r_u)
    dh1, dg2 = _rms_bwd_res(dn2, h1, g2, dh2, tm=tm, name="ffn_norm_bwd")
    dwo, ((w_d,),) = _matmul_tn(mixed, dh1, tmo=1024, tn=1024, tk=tk, name="grad_w_out", rides=[_ride_swap(h_d)])
    dwo = dwo.reshape(N_CHIPS, (A_WIDTH + B_WIDTH) // N_CHIPS, D_MODEL)
    dmixed, ((r_o,),) = _matmul_nt(dh1, wo, tm=tm, tn=1024, tk=D_MODEL, name="bwd_dmixed",
                                   rides=[_ride_sibling_halves(dwo)])
    q_o = pair_sum("w_out", dwo, r_o)
    (duv, dga, dwsp, dbs, dlg, dlb), ((o_o,),) = _gmlp_bwd(proj, ab, dmixed, ga, lg, lb, wsp, bs_col,
                                                           rides=[_ride_scatter(q_o)])
    (dq, dkv, dgb, dsinks, dbias), ((o_u,),) = _attn_bwd(proj, ab, dmixed, gb, sinks, bias, rides=[_ride_scatter(q_u)])
    h_o = owner_sum("w_out", dwo, r_o, o_o)
    h_u = owner_sum("w_up", dwu, r_u, o_u)
    dtable = _bias_grad(dbias)
    dproj = jnp.concatenate([duv, dq, dkv], axis=1)
    dwin, ((w_o,), (w_u,)) = _matmul_tn(n1, dproj, tmo=1024, tn=PROJ_WIDTH // 2, tk=tk, name="grad_w_in",
                                        rides=[_ride_swap(h_o), _ride_swap(h_u)])
    dwin = dwin.reshape(D_MODEL, N_CHIPS, PROJ_WIDTH // N_CHIPS).transpose(1, 0, 2)
    dn1, ((r_i,),) = _matmul_nt(dproj, win, tm=tm, tn=1024, tk=PROJ_WIDTH, name="bwd_dn1",
                                rides=[_ride_sibling_halves(dwin)])
    q_i = pair_sum("w_in", dwin, r_i)
    (dx, dg1), ((o_i,),) = _rms_bwd_res(dn1, x2, g1, dh1, tm=tm, name="mix_norm_bwd", rides=[_ride_scatter(q_i)])
    h_i = owner_sum("w_in", dwin, r_i, o_i)
    ((w_i,),) = _carrier([_ride_swap(h_i)], name="swap_w_in")
    halves = [h_i, h_o, h_u, h_d]
    swapped = [w_i, w_o, w_u, w_d]
    small = {
        "rel_bias_table": dtable.reshape(N_BUCKETS, B_HEADS), "mix_norm_g": dg1, "gate_norm_g": dlg, "gate_norm_b": dlb,
        "w_spatial": dwsp, "b_spatial": dbs, "attn_sinks": dsinks, "out_norm_a_g": dga, "out_norm_b_g": dgb,
        "ffn_norm_g": dg2, "final_norm_g": dgf,
    }

    out_g, out_d, out_m, out_v = {}, {}, {}, {}
    for n, w, h, s in zip(LARGE, own, halves, swapped):
        shape = wts[n].shape
        g, d, nm, nv = _adamw_halves(w, h, s, mom[n].reshape(w.shape), var[n].reshape(w.shape), tm=256,
                                     name="adamw_" + n)
        out_g[n], out_d[n], out_m[n], out_v[n] = g.reshape(shape), d.reshape(shape), nm.reshape(shape), nv.reshape(shape)

    like = [wts[n] for n in SMALL]
    g_small = _all_reduce_small(_pack([small[n] for n in SMALL]))
    d_s, m_s, v_s = _adamw(_pack(like), g_small, _pack([mom[n] for n in SMALL]), _pack([var[n] for n in SMALL]),
                           tm=g_small.shape[0], name="adamw_small")
    for n, g, d, nm, nv in zip(SMALL, _unpack(g_small, like), _unpack(d_s, like), _unpack(m_s, like),
                               _unpack(v_s, like)):
        out_g[n], out_d[n], out_m[n], out_v[n] = g, d, nm, nv

    total = lax.psum(loss[0, 0], ("x", "y", "c"))
    return (total, dx[None], *[out_g[n] for n in WEIGHTS], *[out_d[n] for n in WEIGHTS],
            *[out_m[n] for n in WEIGHTS], *[out_v[n] for n in WEIGHTS])
```

```python
import functools
import math

import numpy as np
import jax
import jax.numpy as jnp
from jax import lax
from jax.experimental import pallas as pl
from jax.experimental.pallas import tpu as pltpu

F32 = jnp.float32
BF16 = jnp.bfloat16

D_MODEL = 2048
CHUNK = 128
A_GROUPS = 8
A_WIDTH = 1024
HEAD_DIM = 64
B_HEADS = 16
Q_PER_KV = 8
B_WIDTH = 1024
KV_WIDTH = 128
PROJ_WIDTH = 3328
D_FF = 8192
N_BUCKETS = 32
EPS = 1e-5
NEG = -1e30
SCALE = HEAD_DIM ** -0.5
N_CHIPS = 4
N_DEV = 8

ADAM_LR = 0.001
ADAM_B1 = 0.9
ADAM_B2 = 0.999
ADAM_EPS = 1e-08
ADAM_WD = 0.01
ADAM_STEP = 10

VMEM_LIMIT = 56 * 1024 * 1024
MESH = pl.DeviceIdType.MESH


def _bucket_thresholds():
    d = np.arange(CHUNK)
    n_exact = N_BUCKETS // 2
    relf = np.maximum(d, n_exact).astype(np.float64)
    large = n_exact + (np.log(relf / n_exact) / math.log(CHUNK / n_exact) * (N_BUCKETS - n_exact)).astype(np.int32)
    bucket = np.where(d < n_exact, d, np.minimum(large, N_BUCKETS - 1))
    return [int(np.min(d[bucket >= b])) for b in range(1, N_BUCKETS)]


BUCKET_THR = _bucket_thresholds()


def _params(sem=None):
    return pltpu.CompilerParams(dimension_semantics=sem, vmem_limit_bytes=VMEM_LIMIT)


def _gelu(x):
    c = math.sqrt(2.0 / math.pi)
    return 0.5 * x * (1.0 + jnp.tanh(c * (x + 0.044715 * (x * x * x))))


def _gelu_and_grad(x):
    c = math.sqrt(2.0 / math.pi)
    x2 = x * x
    t = jnp.tanh(c * (x + 0.044715 * (x2 * x)))
    g = 0.5 * x * (1.0 + t)
    dg = 0.5 * (1.0 + t) + 0.5 * x * (1.0 - t * t) * (c * (1.0 + 3.0 * 0.044715 * x2))
    return g, dg


def _dot(a, b):
    return jnp.dot(a, b, preferred_element_type=F32)


def _dot_nt(a, b):
    return lax.dot_general(a, b, (((1,), (1,)), ((), ())), preferred_element_type=F32)


def _dot_tn(a, b):
    return lax.dot_general(a, b, (((0,), (0,)), ((), ())), preferred_element_type=F32)


def _rms_bwd(dn, h, g):
    r = lax.rsqrt(jnp.mean(h * h, axis=-1, keepdims=True) + EPS)
    w = dn * g
    dh = r * w - h * ((r * r * r) * jnp.mean(w * h, axis=-1, keepdims=True))
    return dh, r


def _place():
    x, y, c = lax.axis_index("x"), lax.axis_index("y"), lax.axis_index("c")
    chips = [(1 - x, y), (x, 1 - y), (1 - x, 1 - y)]
    return x, y, c, chips


def _remote(src, dst, send_sem, recv_sem, to):
    return pltpu.make_async_remote_copy(src_ref=src, dst_ref=dst, send_sem=send_sem, recv_sem=recv_sem,
                                        device_id=to, device_id_type=MESH)


class _Ride:
    def __init__(self, args, out_shape, n_sem, start, finish, mid=None, mid_frac=0.8, aliases=None):
        self.args, self.out_shape, self.n_sem = list(args), list(out_shape), n_sem
        self.start, self.mid, self.finish, self.mid_frac = start, mid, finish, mid_frac
        self.aliases = dict(aliases or {})


def _call(body, *, name, grid, in_specs, out_specs, out_shape, scratch_shapes=(), sem=None, rides=()):
    single = not isinstance(out_shape, (list, tuple))
    out_specs = [out_specs] if single else list(out_specs)
    out_shape = [out_shape] if single else list(out_shape)
    n_in, n_out, n_scr = len(in_specs), len(out_shape), len(scratch_shapes)
    r_in = [len(r.args) for r in rides]
    r_out = [len(r.out_shape) for r in rides]
    any_spec = pl.BlockSpec(memory_space=pl.ANY)
    aliases, off_i, off_o = {}, n_in, n_out
    for r in rides:
        for i, o in r.aliases.items():
            aliases[off_i + i] = off_o + o
        off_i += len(r.args)
        off_o += len(r.out_shape)
    steps = math.prod(grid)

    def wrapped(*refs):
        p = 0
        ins = refs[p:p + n_in]; p += n_in
        rins = refs[p:p + sum(r_in)]; p += sum(r_in)
        outs = refs[p:p + n_out]; p += n_out
        routs = refs[p:p + sum(r_out)]; p += sum(r_out)
        scr = refs[p:p + n_scr]; p += n_scr
        sems = refs[p:]
        parts, pi, po = [], 0, 0
        for k, r in enumerate(rides):
            parts.append((rins[pi:pi + r_in[k]], routs[po:po + r_out[k]], sems[2 * k], sems[2 * k + 1]))
            pi += r_in[k]
            po += r_out[k]
        lin = 0
        for d in range(len(grid)):
            lin = lin * grid[d] + pl.program_id(d)
        if rides:
            @pl.when(lin == 0)
            def _():
                for r, part in zip(rides, parts):
                    r.start(*part)
        body(*ins, *outs, *scr)
        for r, part in zip(rides, parts):
            if r.mid is not None:
                @pl.when(lin == min(steps - 1, int(r.mid_frac * steps)))
                def _(r=r, part=part):
                    r.mid(*part)
        if rides:
            @pl.when(lin == steps - 1)
            def _():
                for r, part in zip(rides, parts):
                    r.finish(*part)

    scratch = list(scratch_shapes)
    for r in rides:
        scratch += [pltpu.SemaphoreType.DMA((r.n_sem,)), pltpu.SemaphoreType.DMA((r.n_sem,))]
    if rides:
        sem = ("arbitrary",) * len(grid)
    res = pl.pallas_call(
        wrapped, name=name, grid=grid,
        in_specs=list(in_specs) + [any_spec] * sum(r_in),
        out_specs=out_specs + [any_spec] * sum(r_out),
        out_shape=out_shape + [s for r in rides for s in r.out_shape],
        scratch_shapes=scratch, input_output_aliases=aliases,
        compiler_params=_params(sem),
    )

    def run(*args):
        got = res(*args, *[a for r in rides for a in r.args])
        mine = got[0] if single else list(got[:n_out])
        if not rides:
            return mine
        rest, out = list(got[n_out:]), []
        for k in range(len(rides)):
            out.append(rest[:r_out[k]])
            rest = rest[r_out[k]:]
        return mine, out

    return run


def _ride_gather(slot, mid_frac=0.8):
    hr = slot.shape[1] // 2

    def start(ins, outs, ss, rs):
        x, y, c, chips = _place()
        mine = outs[0].at[2 * x + y, pl.ds(c * hr, hr), :]
        for j, chip in enumerate(chips):
            _remote(mine, mine, ss.at[j], rs.at[j], (*chip, c)).start()

    def mid(ins, outs, ss, rs):
        x, y, c, chips = _place()
        for j, chip in enumerate(chips):
            landed = outs[0].at[2 * chip[0] + chip[1], pl.ds(c * hr, hr), :]
            _remote(landed, landed, ss.at[j], rs.at[j], (x, y, c)).wait_recv()
            _remote(landed, landed, ss.at[3 + j], rs.at[3 + j], (x, y, 1 - c)).start()

    def finish(ins, outs, ss, rs):
        x, y, c, chips = _place()
        for j, chip in enumerate(chips):
            other = outs[0].at[2 * chip[0] + chip[1], pl.ds((1 - c) * hr, hr), :]
            _remote(other, other, ss.at[3 + j], rs.at[3 + j], (x, y, c)).wait_recv()
        for j in range(6):
            part = outs[0].at[0, pl.ds(0, hr), :]
            _remote(part, part, ss.at[j], rs.at[j], (x, y, c)).wait_send()

    return _Ride([slot], [jax.ShapeDtypeStruct(slot.shape, slot.dtype)], 6, start, finish, mid=mid,
                 mid_frac=mid_frac, aliases={0: 0})


def _ride_sibling_halves(g):
    S, R, C = g.shape
    hr = R // 2

    def copy(ins, outs, ss, rs):
        x, y, c, _ = _place()
        return _remote(ins[0].at[:, pl.ds((1 - c) * hr, hr), :], outs[0], ss.at[0], rs.at[0], (x, y, 1 - c))

    return _Ride([g], [jax.ShapeDtypeStruct((S, hr, C), g.dtype)], 1,
                 lambda *a: copy(*a).start(), lambda *a: copy(*a).wait())


def _ride_scatter(q):
    def copies(ins, outs, ss, rs):
        x, y, c, chips = _place()
        return [_remote(ins[0].at[2 * chip[0] + chip[1]], outs[0].at[j], ss.at[j], rs.at[j], (*chip, c))
                for j, chip in enumerate(chips)]

    def start(*a):
        for cp in copies(*a):
            cp.start()

    def finish(*a):
        for cp in copies(*a):
            cp.wait()

    return _Ride([q], [jax.ShapeDtypeStruct((3,) + q.shape[1:], q.dtype)], 3, start, finish)


def _ride_swap(h):
    def copy(ins, outs, ss, rs):
        x, y, c, _ = _place()
        return _remote(ins[0], outs[0], ss.at[0], rs.at[0], (x, y, 1 - c))

    return _Ride([h], [jax.ShapeDtypeStruct(h.shape, h.dtype)], 1,
                 lambda *a: copy(*a).start(), lambda *a: copy(*a).wait())


def _carrier(rides, *, name):
    _, outs = _call(lambda: None, name=name, grid=(1,), in_specs=[], out_specs=[], out_shape=[], rides=rides)()
    return outs


def _sq_relu_bf16(z):
    z = jnp.maximum(z, 0.0)
    return (z * z).astype(BF16)


def _norm_bf16(a_ref, g_ref):
    xf = a_ref[...]
    r = lax.rsqrt(jnp.mean(xf * xf, axis=-1, keepdims=True) + EPS)
    return ((xf * r) * g_ref[...]).astype(BF16)


def _norm_matmul_wide(a, g, b, *, tm, tn, name, rides=()):
    T, K = a.shape
    N = b.shape[1]

    def body(a_ref, g_ref, b_ref, nt_ref, o_ref):
        n = _norm_bf16(a_ref, g_ref)
        nt_ref[...] = n.T
        o_ref[...] = _dot(n, b_ref[...])

    return _call(
        body, name=name, grid=(N // tn, T // tm),
        in_specs=[pl.BlockSpec((tm, K), lambda j, i: (i, 0)), pl.BlockSpec((1, K), lambda j, i: (0, 0)),
                  pl.BlockSpec((K, tn), lambda j, i: (0, j))],
        out_specs=[pl.BlockSpec((None, K, tm), lambda j, i: (j, 0, i)), pl.BlockSpec((tm, tn), lambda j, i: (i, j))],
        out_shape=[jax.ShapeDtypeStruct((N // tn, K, T), BF16), jax.ShapeDtypeStruct((T, N), F32)],
        sem=("arbitrary", "arbitrary"), rides=rides,
    )(a, g, b)


def _norm_matmul_sq(a, g, b, *, tm, tn, name, rides=()):
    T, K = a.shape
    per = b.shape[2] // tn
    N = b.shape[0] * b.shape[2]

    def body(a_ref, g_ref, b_ref, nt_ref, o_ref, z_ref, zt_ref, n_scr):
        @pl.when(pl.program_id(1) == 0)
        def _():
            n = _norm_bf16(a_ref, g_ref)
            n_scr[...] = n
            nt_ref[...] = n.T
        p = _dot(n_scr[...], b_ref[...])
        o_ref[...] = p
        z = _sq_relu_bf16(p)
        z_ref[...] = z
        zt_ref[...] = z.T

    return _call(
        body, name=name, grid=(T // tm, N // tn),
        in_specs=[pl.BlockSpec((tm, K), lambda i, j: (i, 0)), pl.BlockSpec((1, K), lambda i, j: (0, 0)),
                  pl.BlockSpec((None, K, tn), lambda i, j: (j // per, 0, j % per))],
        out_specs=[pl.BlockSpec((K, tm), lambda i, j: (0, i)), pl.BlockSpec((tm, tn), lambda i, j: (i, j)),
                   pl.BlockSpec((tm, tn), lambda i, j: (i, j)), pl.BlockSpec((tn, tm), lambda i, j: (j, i))],
        out_shape=[jax.ShapeDtypeStruct((K, T), BF16), jax.ShapeDtypeStruct((T, N), F32),
                   jax.ShapeDtypeStruct((T, N), BF16), jax.ShapeDtypeStruct((N, T), BF16)],
        scratch_shapes=[pltpu.VMEM((tm, K), BF16)],
        sem=("parallel", "arbitrary"), rides=rides,
    )(a, g, b)


def _matmul_nn(at, b, *, tmo, tn, tk, name, shards=1, rides=()):
    M, T = at.shape[-2:]
    N = b.shape[1]
    if at.ndim == 3:
        a_spec = pl.BlockSpec((None, tmo, tk), lambda i, j, k: (0, i, k))
    else:
        a_spec = pl.BlockSpec((tmo, tk), lambda i, j, k: (i, k))
    if shards > 1:
        per = (N // shards) // tn
        out_spec = pl.BlockSpec((None, tmo, tn), lambda i, j, k: (j // per, i, j % per))
        out_shape = jax.ShapeDtypeStruct((shards, M, N // shards), F32)
    else:
        out_spec = pl.BlockSpec((tmo, tn), lambda i, j, k: (i, j))
        out_shape = jax.ShapeDtypeStruct((M, N), F32)

    def body(a_ref, b_ref, o_ref):
        k = pl.program_id(2)
        p = _dot(a_ref[...], b_ref[...])

        @pl.when(k == 0)
        def _():
            o_ref[...] = p

        @pl.when(k > 0)
        def _():
            o_ref[...] += p

    return _call(
        body, name=name, grid=(M // tmo, N // tn, T // tk),
        in_specs=[a_spec, pl.BlockSpec((tk, tn), lambda i, j, k: (k, j))],
        out_specs=out_spec, out_shape=out_shape,
        sem=("parallel", "parallel", "arbitrary"), rides=rides,
    )(at, b)


def _to_bf16(v):
    return v.astype(BF16)


def _matmul_res(a, b, res, *, tm, tn, tk, prologue, name):
    T, K = a.shape
    N = b.shape[1]

    def body(a_ref, b_ref, res_ref, o_ref):
        k = pl.program_id(2)
        p = _dot(prologue(a_ref[...]), b_ref[...])

        @pl.when(k == 0)
        def _():
            o_ref[...] = res_ref[...] + p

        @pl.when(k > 0)
        def _():
            o_ref[...] += p

    return pl.pallas_call(
        body, name=name, grid=(T // tm, N // tn, K // tk),
        in_specs=[pl.BlockSpec((tm, tk), lambda i, j, k: (i, k)), pl.BlockSpec((tk, tn), lambda i, j, k: (k, j)),
                  pl.BlockSpec((tm, tn), lambda i, j, k: (i, j))],
        out_specs=pl.BlockSpec((tm, tn), lambda i, j, k: (i, j)),
        out_shape=jax.ShapeDtypeStruct((T, N), F32),
        compiler_params=_params(("parallel", "parallel", "arbitrary")),
    )(a, b, res)


def _matmul_nt(a, b, *, tm, tn, tk, name, extra=None, epilogue=None, out_dtype=F32, rides=()):
    T, K = a.shape
    if b.ndim == 3:
        per = b.shape[2] // tk
        N = b.shape[1]
        b_spec = pl.BlockSpec((None, tn, tk), lambda i, j, k: (k // per, j, k % per))
    else:
        N = b.shape[0]
        b_spec = pl.BlockSpec((tn, tk), lambda i, j, k: (j, k))
    nk = K // tk
    assert out_dtype == F32 or nk == 1
    in_specs = [pl.BlockSpec((tm, tk), lambda i, j, k: (i, k)), b_spec]
    args = [a, b]
    if extra is not None:
        in_specs.append(pl.BlockSpec((tm, tn), lambda i, j, k: (i, j)))
        args.append(extra)

    def body(*refs):
        a_ref, b_ref = refs[0], refs[1]
        o_ref = refs[-1]
        p = _dot_nt(a_ref[...].astype(BF16), b_ref[...])
        if nk == 1:
            if epilogue is not None:
                p = epilogue(p, refs[2][...])
            o_ref[...] = p.astype(out_dtype)
        else:
            k = pl.program_id(2)

            @pl.when(k == 0)
            def _():
                o_ref[...] = p

            @pl.when(k > 0)
            def _():
                o_ref[...] += p

    return _call(
        body, name=name, grid=(T // tm, N // tn, nk),
        in_specs=in_specs,
        out_specs=pl.BlockSpec((tm, tn), lambda i, j, k: (i, j)),
        out_shape=jax.ShapeDtypeStruct((T, N), out_dtype),
        sem=("parallel", "parallel", "arbitrary"), rides=rides,
    )(*args)


def _matmul_tn(a, b, *, tmo, tn, tk, name, a_prologue=_to_bf16, shards=1, rides=()):
    T, M = a.shape
    N = b.shape[1]
    if shards > 1:
        per = (N // shards) // tn
        out_spec = pl.BlockSpec((None, tmo, tn), lambda i, j, k: (j // per, i, j % per))
        out_shape = jax.ShapeDtypeStruct((shards, M, N // shards), F32)
    else:
        out_spec = pl.BlockSpec((tmo, tn), lambda i, j, k: (i, j))
        out_shape = jax.ShapeDtypeStruct((M, N), F32)

    def body(a_ref, b_ref, o_ref):
        k = pl.program_id(2)
        p = _dot_tn(a_prologue(a_ref[...]), b_ref[...].astype(BF16))

        @pl.when(k == 0)
        def _():
            o_ref[...] = p

        @pl.when(k > 0)
        def _():
            o_ref[...] += p

    return _call(
        body, name=name, grid=(M // tmo, N // tn, T // tk),
        in_specs=[pl.BlockSpec((tk, tmo), lambda i, j, k: (k, i)), pl.BlockSpec((tk, tn), lambda i, j, k: (k, j))],
        out_specs=out_spec, out_shape=out_shape,
        sem=("parallel", "parallel", "arbitrary"), rides=rides,
    )(a, b)


def _loss_bwd(h2, tgt, g, *, tm):
    T, D = h2.shape

    def body(h_ref, t_ref, g_ref, dh_ref, dhb_ref, dg_ref, loss_ref):
        @pl.when(pl.program_id(0) == 0)
        def _():
            dg_ref[...] = jnp.zeros_like(dg_ref)
            loss_ref[...] = jnp.zeros_like(loss_ref)
        h = h_ref[...]
        gg = g_ref[...]
        r = lax.rsqrt(jnp.mean(h * h, axis=-1, keepdims=True) + EPS)
        hn = h * r
        err = hn * gg - t_ref[...]
        loss_ref[...] += 0.5 * jnp.sum(jnp.mean(err * err, axis=-1, keepdims=True), axis=0, keepdims=True)
        dy = err * (1.0 / D)
        dg_ref[...] += jnp.sum(dy * hn, axis=0, keepdims=True)
        w = dy * gg
        dh = r * w - h * ((r * r * r) * jnp.mean(w * h, axis=-1, keepdims=True))
        dh_ref[...] = dh
        dhb_ref[...] = dh.astype(BF16)

    tile = pl.BlockSpec((tm, D), lambda i: (i, 0))
    return pl.pallas_call(
        body, name="loss_bwd", grid=(T // tm,),
        in_specs=[tile, tile, pl.BlockSpec((1, D), lambda i: (0, 0))],
        out_specs=[tile, tile, pl.BlockSpec((1, D), lambda i: (0, 0)), pl.BlockSpec((1, 1), lambda i: (0, 0))],
        out_shape=[jax.ShapeDtypeStruct((T, D), F32), jax.ShapeDtypeStruct((T, D), BF16),
                   jax.ShapeDtypeStruct((1, D), F32), jax.ShapeDtypeStruct((1, 1), F32)],
        compiler_params=_params(("arbitrary",)),
    )(h2, tgt, g)


def _rms_bwd_res(dn, h, g, dres, *, tm, name, rides=()):
    T, D = h.shape

    def body(dn_ref, h_ref, g_ref, dres_ref, dh_ref, dhb_ref, dg_ref):
        @pl.when(pl.program_id(0) == 0)
        def _():
            dg_ref[...] = jnp.zeros_like(dg_ref)
        h_ = h_ref[...]
        dn_ = dn_ref[...]
        dh, r = _rms_bwd(dn_, h_, g_ref[...])
        dg_ref[...] += jnp.sum(dn_ * (h_ * r), axis=0, keepdims=True)
        dh = dres_ref[...] + dh
        dh_ref[...] = dh
        dhb_ref[...] = dh.astype(BF16)

    tile = pl.BlockSpec((tm, D), lambda i: (i, 0))
    return _call(
        body, name=name, grid=(T // tm,),
        in_specs=[tile, tile, pl.BlockSpec((1, D), lambda i: (0, 0)), tile],
        out_specs=[tile, tile, pl.BlockSpec((1, D), lambda i: (0, 0))],
        out_shape=[jax.ShapeDtypeStruct((T, D), F32), jax.ShapeDtypeStruct((T, D), BF16),
                   jax.ShapeDtypeStruct((1, D), F32)],
        sem=("arbitrary",), rides=rides,
    )(dn, h, g, dres)


def _rel_distance():
    i = lax.broadcasted_iota(jnp.int32, (CHUNK, 2 * CHUNK), 0)
    j = lax.broadcasted_iota(jnp.int32, (CHUNK, 2 * CHUNK), 1)
    return i + CHUNK - j


def _bias_build(table):
    def body(tab_ref, o_ref):
        rel = _rel_distance()
        ge = [rel >= t for t in BUCKET_THR]
        for h in range(B_HEADS):
            cur = jnp.full((CHUNK, 2 * CHUNK), tab_ref[0, h], F32)
            for b in range(1, N_BUCKETS):
                cur = jnp.where(ge[b - 1], tab_ref[b, h], cur)
            o_ref[h] = cur

    return pl.pallas_call(
        body, name="bias_build",
        in_specs=[pl.BlockSpec(memory_space=pltpu.SMEM)],
        out_specs=pl.BlockSpec(memory_space=pltpu.VMEM),
        out_shape=jax.ShapeDtypeStruct((B_HEADS, CHUNK, 2 * CHUNK), F32),
    )(table)


def _bias_grad(dbias):
    def body(db_ref, o_ref, acc_ref):
        rel = _rel_distance()
        lo = [0] + BUCKET_THR
        hi = BUCKET_THR + [CHUNK]
        for b in range(N_BUCKETS):
            m = (rel >= lo[b]) & (rel < hi[b])
            for h in range(B_HEADS):
                row = b * B_HEADS + h
                acc_ref[row:row + 1, :] = jnp.sum(jnp.where(m, db_ref[h], 0.0), axis=0, keepdims=True)
        o_ref[...] = jnp.sum(acc_ref[...], axis=1, keepdims=True)

    return pl.pallas_call(
        body, name="bias_grad",
        in_specs=[pl.BlockSpec(memory_space=pltpu.VMEM)],
        out_specs=pl.BlockSpec(memory_space=pltpu.VMEM),
        out_shape=jax.ShapeDtypeStruct((N_BUCKETS * B_HEADS, 1), F32),
        scratch_shapes=[pltpu.VMEM((N_BUCKETS * B_HEADS, 2 * CHUNK), F32)],
    )(dbias)


def _causal_mask():
    t = lax.broadcasted_iota(jnp.int32, (CHUNK, CHUNK), 0)
    s = lax.broadcasted_iota(jnp.int32, (CHUNK, CHUNK), 1)
    return s <= t


def _band_mask(n):
    rel = _rel_distance()
    j = lax.broadcasted_iota(jnp.int32, (CHUNK, 2 * CHUNK), 1)
    return (rel >= 0) & (rel < CHUNK) & ((n > 0) | (j >= CHUNK))


def _gate_forward(u, v, lg, lb, wc, bs):
    ug = _gelu(u)
    vg = _gelu(v)
    mu = jnp.mean(vg, axis=-1, keepdims=True)
    xc = vg - mu
    rstd = lax.rsqrt(jnp.mean(xc * xc, axis=-1, keepdims=True) + EPS)
    xhat = xc * rstd
    vl = (xhat * lg + lb).astype(BF16)
    mixed = _dot(wc, vl) + bs
    return ug, xhat, rstd, vl, mixed


def _softmax_band(q, kband, bias, mask, sink):
    s = _dot_nt(q, kband) * SCALE + bias
    s = jnp.where(mask, s, NEG)
    m = jnp.maximum(jnp.max(s, axis=-1, keepdims=True), sink)
    p = jnp.exp(s - m)
    e_sink = jnp.exp(sink - m)
    inv = 1.0 / (jnp.sum(p, axis=-1, keepdims=True) + e_sink)
    return p * inv, e_sink * inv


def _mixer_fwd(proj, lg, lb, wsp, bs_col, sinks, bias, ga, gb, rides=()):
    T = proj.shape[0]
    nb = T // CHUNK

    def body(u_ref, v_ref, q_ref, kvc_ref, kvp_ref, lg_ref, lb_ref, w_ref, bs_ref, sink_ref, bias_ref,
             ga_ref, gb_ref, mixed_ref, mixed_t_ref, ab_ref):
        n = pl.program_id(0)
        causal = _causal_mask()
        ssq = jnp.zeros((CHUNK, 1), F32)
        for g in range(A_GROUPS):
            cols = slice(g * CHUNK, (g + 1) * CHUNK)
            wc = jnp.where(causal, w_ref[g], 0.0).astype(BF16)
            ug, _, _, _, mixed = _gate_forward(u_ref[:, cols], v_ref[:, cols], lg_ref[g:g + 1, :], lb_ref[g:g + 1, :],
                                               wc, bs_ref[g])
            a = ug * mixed
            ab_ref[:, cols] = a
            ssq = ssq + jnp.sum(a * a, axis=-1, keepdims=True)
        ra = lax.rsqrt(ssq * (1.0 / A_WIDTH) + EPS)
        mixed_ref[:, :A_WIDTH] = ((ab_ref[:, :A_WIDTH] * ra) * ga_ref[...]).astype(BF16)

        mask = _band_mask(n)
        kvc = kvc_ref[...].astype(BF16)
        kvp = kvp_ref[...].astype(BF16)
        band = jnp.concatenate([kvp, kvc], axis=0)
        ssq = jnp.zeros((CHUNK, 1), F32)
        for h in range(B_HEADS):
            kv = h // Q_PER_KV
            q = q_ref[:, h * HEAD_DIM:(h + 1) * HEAD_DIM].astype(BF16)
            kband = band[:, kv * HEAD_DIM:(kv + 1) * HEAD_DIM]
            vband = band[:, KV_WIDTH + kv * HEAD_DIM:KV_WIDTH + (kv + 1) * HEAD_DIM]
            p, _ = _softmax_band(q, kband, bias_ref[h], mask, sink_ref[0, h])
            o = _dot(p.astype(BF16), vband)
            ab_ref[:, A_WIDTH + h * HEAD_DIM:A_WIDTH + (h + 1) * HEAD_DIM] = o
            ssq = ssq + jnp.sum(o * o, axis=-1, keepdims=True)
        rb = lax.rsqrt(ssq * (1.0 / B_WIDTH) + EPS)
        mixed_ref[:, A_WIDTH:] = ((ab_ref[:, A_WIDTH:] * rb) * gb_ref[...]).astype(BF16)
        mixed_t_ref[...] = mixed_ref[...].T

    full = lambda *shape: pl.BlockSpec(shape, lambda n: (0,) * len(shape))
    return _call(
        body, name="mixer_fwd", grid=(nb,),
        in_specs=[pl.BlockSpec((CHUNK, A_WIDTH), lambda n: (n, 0)),
                  pl.BlockSpec((CHUNK, A_WIDTH), lambda n: (n, 1)),
                  pl.BlockSpec((CHUNK, B_WIDTH), lambda n: (n, 2)),
                  pl.BlockSpec((CHUNK, 2 * KV_WIDTH), lambda n: (n, 12)),
                  pl.BlockSpec((CHUNK, 2 * KV_WIDTH), lambda n: (jnp.maximum(n - 1, 0), 12)),
                  full(A_GROUPS, CHUNK), full(A_GROUPS, CHUNK), full(A_GROUPS, CHUNK, CHUNK), full(A_GROUPS, CHUNK, 1),
                  pl.BlockSpec(memory_space=pltpu.SMEM), full(B_HEADS, CHUNK, 2 * CHUNK),
                  full(1, A_WIDTH), full(1, B_WIDTH)],
        out_specs=[pl.BlockSpec((CHUNK, D_MODEL), lambda n: (n, 0)), pl.BlockSpec((D_MODEL, CHUNK), lambda n: (0, n)),
                   pl.BlockSpec((CHUNK, D_MODEL), lambda n: (n, 0))],
        out_shape=[jax.ShapeDtypeStruct((T, D_MODEL), BF16), jax.ShapeDtypeStruct((D_MODEL, T), BF16),
                   jax.ShapeDtypeStruct((T, D_MODEL), F32)],
        sem=("parallel",), rides=rides,
    )(proj, proj, proj, proj, proj, lg, lb, wsp, bs_col, sinks, bias, ga, gb)


def _gmlp_bwd(proj, ab, dmixed, ga, lg, lb, wsp, bs_col, rides=()):
    T = proj.shape[0]
    nb = T // CHUNK

    def body(u_ref, v_ref, a_ref, dna_ref, ga_ref, lg_ref, lb_ref, w_ref, bs_ref,
             dp_ref, dga_ref, dw_ref, dbs_ref, dlg_ref, dlb_ref):
        @pl.when(pl.program_id(0) == 0)
        def _():
            for r in (dga_ref, dw_ref, dbs_ref, dlg_ref, dlb_ref):
                r[...] = jnp.zeros_like(r)
        causal = _causal_mask()
        a_all = a_ref[...]
        dna = dna_ref[...]
        da_all, ra = _rms_bwd(dna, a_all, ga_ref[...])
        dga_ref[...] += jnp.sum(dna * (a_all * ra), axis=0, keepdims=True)
        for g in range(A_GROUPS):
            cols = slice(g * CHUNK, (g + 1) * CHUNK)
            wc = jnp.where(causal, w_ref[g], 0.0).astype(BF16)
            lgg = lg_ref[g:g + 1, :]
            u = u_ref[:, cols]
            v = v_ref[:, cols]
            ug, xhat, rstd, vl, mixed = _gate_forward(u, v, lgg, lb_ref[g:g + 1, :], wc, bs_ref[g])
            da = da_all[:, cols]
            dug = da * mixed
            dmg = da * ug
            dmg_b = dmg.astype(BF16)
            dbs_ref[g] += jnp.sum(dmg, axis=-1, keepdims=True)
            dw_ref[g] += jnp.where(causal, _dot_nt(dmg_b, vl), 0.0)
            dvl = _dot_tn(wc, dmg_b)
            dlg_ref[g:g + 1, :] += jnp.sum(dvl * xhat, axis=0, keepdims=True)
            dlb_ref[g:g + 1, :] += jnp.sum(dvl, axis=0, keepdims=True)
            dxh = dvl * lgg
            dvg = rstd * (dxh - jnp.mean(dxh, axis=-1, keepdims=True)
                          - xhat * jnp.mean(dxh * xhat, axis=-1, keepdims=True))
            _, gu = _gelu_and_grad(u)
            _, gv = _gelu_and_grad(v)
            dp_ref[:, cols] = (dug * gu).astype(BF16)
            dp_ref[:, A_WIDTH + g * CHUNK:A_WIDTH + (g + 1) * CHUNK] = (dvg * gv).astype(BF16)

    full = lambda *shape: pl.BlockSpec(shape, lambda n: (0,) * len(shape))
    return _call(
        body, name="gmlp_bwd", grid=(nb,),
        in_specs=[pl.BlockSpec((CHUNK, A_WIDTH), lambda n: (n, 0)),
                  pl.BlockSpec((CHUNK, A_WIDTH), lambda n: (n, 1)),
                  pl.BlockSpec((CHUNK, A_WIDTH), lambda n: (n, 0)),
                  pl.BlockSpec((CHUNK, A_WIDTH), lambda n: (n, 0)),
                  full(1, A_WIDTH), full(A_GROUPS, CHUNK), full(A_GROUPS, CHUNK), full(A_GROUPS, CHUNK, CHUNK),
                  full(A_GROUPS, CHUNK, 1)],
        out_specs=[pl.BlockSpec((CHUNK, 2 * A_WIDTH), lambda n: (n, 0)),
                   full(1, A_WIDTH), full(A_GROUPS, CHUNK, CHUNK), full(A_GROUPS, CHUNK, 1),
                   full(A_GROUPS, CHUNK), full(A_GROUPS, CHUNK)],
        out_shape=[jax.ShapeDtypeStruct((T, 2 * A_WIDTH), BF16),
                   jax.ShapeDtypeStruct((1, A_WIDTH), F32), jax.ShapeDtypeStruct((A_GROUPS, CHUNK, CHUNK), F32),
                   jax.ShapeDtypeStruct((A_GROUPS, CHUNK, 1), F32), jax.ShapeDtypeStruct((A_GROUPS, CHUNK), F32),
                   jax.ShapeDtypeStruct((A_GROUPS, CHUNK), F32)],
        sem=("arbitrary",), rides=rides,
    )(proj, proj, ab, dmixed, ga, lg, lb, wsp, bs_col)


def _attn_bwd(proj, ab, dmixed, gb, sinks, bias, rides=()):
    T = proj.shape[0]
    nb = T // CHUNK
    qn = lambda n: jnp.minimum(n, nb - 1)

    def body(q_ref, kvc_ref, kvp_ref, o_ref, dnb_ref, gb_ref, sink_ref, bias_ref,
             dq_ref, dkv_ref, dgb_ref, dsink_ref, dbias_ref, carry_ref, sacc_ref):
        n = pl.program_id(0)

        @pl.when(n == 0)
        def _():
            carry_ref[...] = jnp.zeros_like(carry_ref)
            sacc_ref[...] = jnp.zeros_like(sacc_ref)
            dgb_ref[...] = jnp.zeros_like(dgb_ref)
            dbias_ref[...] = jnp.zeros_like(dbias_ref)

        @pl.when(n < nb)
        def _():
            mask = _band_mask(n)
            o_all = o_ref[...]
            dnb = dnb_ref[...]
            do_all, rb = _rms_bwd(dnb, o_all, gb_ref[...])
            dgb_ref[...] += jnp.sum(dnb * (o_all * rb), axis=0, keepdims=True)
            band = jnp.concatenate([kvp_ref[...].astype(BF16), kvc_ref[...].astype(BF16)], axis=0)
            dkv_parts = []
            for kv in range(B_HEADS // Q_PER_KV):
                kband = band[:, kv * HEAD_DIM:(kv + 1) * HEAD_DIM]
                vband = band[:, KV_WIDTH + kv * HEAD_DIM:KV_WIDTH + (kv + 1) * HEAD_DIM]
                dk = jnp.zeros((2 * CHUNK, HEAD_DIM), F32)
                dv = jnp.zeros((2 * CHUNK, HEAD_DIM), F32)
                for hq in range(Q_PER_KV):
                    h = kv * Q_PER_KV + hq
                    hc = slice(h * HEAD_DIM, (h + 1) * HEAD_DIM)
                    q = q_ref[:, hc].astype(BF16)
                    p, p_sink = _softmax_band(q, kband, bias_ref[h], mask, sink_ref[0, h])
                    do = do_all[:, hc].astype(BF16)
                    dp = _dot_nt(do, vband)
                    delta = jnp.sum(p * dp, axis=-1, keepdims=True)
                    ds = p * (dp - delta)
                    sacc_ref[:, h:h + 1] += -(p_sink * delta)
                    dbias_ref[h] += ds
                    dsr = (ds * SCALE).astype(BF16)
                    dq_ref[:, hc] = _dot(dsr, kband).astype(BF16)
                    dk = dk + _dot_tn(dsr, q)
                    dv = dv + _dot_tn(p.astype(BF16), do)
                dkv_parts.append((dk, dv))
            dband = jnp.concatenate([dkv_parts[0][0], dkv_parts[1][0], dkv_parts[0][1], dkv_parts[1][1]], axis=1)
            dkv_ref[...] = (carry_ref[...] + dband[:CHUNK]).astype(BF16)
            carry_ref[...] = dband[CHUNK:]

        @pl.when(n == nb)
        def _():
            dkv_ref[...] = carry_ref[...].astype(BF16)
            dsink_ref[...] = jnp.sum(sacc_ref[...], axis=0, keepdims=True)

    full = lambda *shape: pl.BlockSpec(shape, lambda n: (0,) * len(shape))
    return _call(
        body, name="attn_bwd", grid=(nb + 1,),
        in_specs=[pl.BlockSpec((CHUNK, B_WIDTH), lambda n: (qn(n), 2)),
                  pl.BlockSpec((CHUNK, 2 * KV_WIDTH), lambda n: (qn(n), 12)),
                  pl.BlockSpec((CHUNK, 2 * KV_WIDTH), lambda n: (jnp.maximum(qn(n) - 1, 0), 12)),
                  pl.BlockSpec((CHUNK, B_WIDTH), lambda n: (qn(n), 1)),
                  pl.BlockSpec((CHUNK, B_WIDTH), lambda n: (qn(n), 1)),
                  full(1, B_WIDTH), pl.BlockSpec(memory_space=pltpu.SMEM), full(B_HEADS, CHUNK, 2 * CHUNK)],
        out_specs=[pl.BlockSpec((CHUNK, B_WIDTH), lambda n: (qn(n), 0)),
                   pl.BlockSpec((CHUNK, 2 * KV_WIDTH), lambda n: (jnp.maximum(n - 1, 0), 0)),
                   full(1, B_WIDTH), full(1, B_HEADS), full(B_HEADS, CHUNK, 2 * CHUNK)],
        out_shape=[jax.ShapeDtypeStruct((T, B_WIDTH), BF16), jax.ShapeDtypeStruct((T, 2 * KV_WIDTH), BF16),
                   jax.ShapeDtypeStruct((1, B_WIDTH), F32), jax.ShapeDtypeStruct((1, B_HEADS), F32),
                   jax.ShapeDtypeStruct((B_HEADS, CHUNK, 2 * CHUNK), F32)],
        scratch_shapes=[pltpu.VMEM((CHUNK, 2 * KV_WIDTH), F32), pltpu.VMEM((CHUNK, B_HEADS), F32)],
        sem=("arbitrary",), rides=rides,
    )(proj, proj, proj, ab, dmixed, gb, sinks, bias)


def _sq_relu_grad(acc, z):
    return acc * (2.0 * jnp.maximum(z, 0.0))


def _local_step(x, tgt, sp, win, wo, wu, wd):
    T = x.shape[0]
    tm = min(512, T)
    tk = min(512, T)
    lg = sp["gate_norm_g"].reshape(A_GROUPS, CHUNK)
    lb = sp["gate_norm_b"].reshape(A_GROUPS, CHUNK)
    wsp = sp["w_spatial"].reshape(A_GROUPS, CHUNK, CHUNK)
    bs_col = sp["b_spatial"].reshape(A_GROUPS, CHUNK, 1)
    sinks = sp["attn_sinks"].reshape(1, B_HEADS)
    ga = sp["out_norm_a_g"].reshape(1, A_WIDTH)
    gb = sp["out_norm_b_g"].reshape(1, B_WIDTH)
    g1 = sp["mix_norm_g"].reshape(1, D_MODEL)
    g2 = sp["ffn_norm_g"].reshape(1, D_MODEL)
    gf = sp["final_norm_g"].reshape(1, D_MODEL)

    bias = _bias_build(sp["rel_bias_table"])
    n1, proj = _norm_matmul(x, g1, win, tm=tm, tn=PROJ_WIDTH // 2, name="in_proj")
    mixed, ab = _mixer_fwd(proj, lg, lb, wsp, bs_col, sinks, bias, ga, gb)
    h1 = _matmul_res(mixed, wo, x, tm=tm, tn=1024, tk=D_MODEL, prologue=_to_bf16, name="out_proj")
    n2, zp = _norm_matmul(h1, g2, wu, tm=tm, tn=1024, name="up_proj")
    h2 = _matmul_res(zp, wd, h1, tm=tm, tn=1024, tk=2048, prologue=_sq_relu_bf16, name="down_proj")

    dh2, dgf, loss = _loss_bwd(h2, tgt, gf, tm=tm)
    dzp = _matmul_nt(dh2, wd, tm=tm, tn=1024, tk=D_MODEL, name="bwd_dz", extra=zp, epilogue=_sq_relu_grad,
                     out_dtype=BF16)
    dwd = _matmul_tn(zp, dh2, tmo=1024, tn=1024, tk=tk, name="grad_w_down", a_prologue=_sq_relu_bf16)
    dwu = _matmul_tn(n2, dzp, tmo=1024, tn=1024, tk=tk, name="grad_w_up", shards=N_CHIPS)
    dn2 = _matmul_nt(dzp, wu, tm=tm, tn=1024, tk=2048, name="bwd_dn2")
    dh1, dg2 = _rms_bwd_res(dn2, h1, g2, dh2, tm=tm, name="ffn_norm_bwd")
    dwo = _matmul_tn(mixed, dh1, tmo=1024, tn=1024, tk=tk, name="grad_w_out")
    dmixed = _matmul_nt(dh1, wo, tm=tm, tn=1024, tk=D_MODEL, name="bwd_dmixed")
    duv, dga, dwsp, dbs, dlg, dlb = _gmlp_bwd(proj, ab, dmixed, ga, lg, lb, wsp, bs_col)
    dq, dkv, dgb, dsinks, dbias = _attn_bwd(proj, ab, dmixed, gb, sinks, bias)
    dtable = _bias_grad(dbias)
    dproj = jnp.concatenate([duv, dq, dkv], axis=1)
    dwin = _matmul_tn(n1, dproj, tmo=1024, tn=PROJ_WIDTH // 2, tk=tk, name="grad_w_in")
    dn1 = _matmul_nt(dproj, win, tm=tm, tn=1024, tk=PROJ_WIDTH, name="bwd_dn1")
    dx, dg1 = _rms_bwd_res(dn1, x, g1, dh1, tm=tm, name="mix_norm_bwd")

    small = {
        "rel_bias_table": dtable.reshape(N_BUCKETS, B_HEADS), "mix_norm_g": dg1, "gate_norm_g": dlg, "gate_norm_b": dlb,
        "w_spatial": dwsp, "b_spatial": dbs, "attn_sinks": dsinks, "out_norm_a_g": dga, "out_norm_b_g": dgb,
        "ffn_norm_g": dg2, "final_norm_g": dgf,
    }
    return loss, dx, (dwin, dwo, dwu, dwd), small


def _place():
    x, y, c = lax.axis_index("x"), lax.axis_index("y"), lax.axis_index("c")
    chips = [(1 - x, y), (x, 1 - y), (1 - x, 1 - y)]
    return x, y, c, chips


def _remote(src, dst, send_sem, recv_sem, to):
    return pltpu.make_async_remote_copy(src_ref=src, dst_ref=dst, send_sem=send_sem, recv_sem=recv_sem,
                                        device_id=to, device_id_type=MESH)


def _core_index():
    return lax.axis_index("c").astype(jnp.int32).reshape(1)


def _chip_index():
    return (2 * lax.axis_index("x") + lax.axis_index("y")).astype(jnp.int32).reshape(1)


def _cast_into_slot(w, *, tm, name):
    R, C = w.shape

    def body(me_ref, w_ref, o_ref):
        del me_ref
        o_ref[...] = w_ref[...].astype(BF16)

    return pl.pallas_call(
        body, name=name,
        grid_spec=pltpu.PrefetchScalarGridSpec(
            num_scalar_prefetch=1, grid=(R // tm,),
            in_specs=[pl.BlockSpec((tm, C), lambda i, me: (i, 0))],
            out_specs=pl.BlockSpec((None, tm, C), lambda i, me: (me[0], i, 0))),
        out_shape=jax.ShapeDtypeStruct((N_CHIPS, R, C), BF16), compiler_params=_params(("parallel",)),
    )(_chip_index(), w)


def _gather_weights(slots):
    nw = len(slots)

    def body(*refs):
        fulls = refs[nw:2 * nw]
        send_sems, recv_sems = refs[2 * nw:]
        x, y, c, chips = _place()
        me = 2 * x + y
        sends = []
        for w in range(nw):
            hr = fulls[w].shape[1] // 2
            rows = pl.ds(c * hr, hr)
            for j, chip in enumerate(chips):
                mine = fulls[w].at[me, rows, :]
                cp = _remote(mine, mine, send_sems.at[6 * w + j], recv_sems.at[6 * w + j], (*chip, c))
                cp.start()
                sends.append(cp)
        for w in range(nw):
            hr = fulls[w].shape[1] // 2
            rows = pl.ds(c * hr, hr)
            for j, chip in enumerate(chips):
                landed = fulls[w].at[2 * chip[0] + chip[1], rows, :]
                _remote(landed, landed, send_sems.at[6 * w + j], recv_sems.at[6 * w + j], (x, y, c)).wait_recv()
                cp = _remote(landed, landed, send_sems.at[6 * w + 3 + j], recv_sems.at[6 * w + 3 + j], (x, y, 1 - c))
                cp.start()
                sends.append(cp)
        for w in range(nw):
            hr = fulls[w].shape[1] // 2
            rows = pl.ds((1 - c) * hr, hr)
            for j, chip in enumerate(chips):
                other = fulls[w].at[2 * chip[0] + chip[1], rows, :]
                _remote(other, other, send_sems.at[6 * w + 3 + j], recv_sems.at[6 * w + 3 + j], (x, y, c)).wait_recv()
        for cp in sends:
            cp.wait_send()

    any_spec = pl.BlockSpec(memory_space=pl.ANY)
    return pl.pallas_call(
        body, name="gather_weights",
        in_specs=[any_spec] * nw, out_specs=[any_spec] * nw,
        out_shape=[jax.ShapeDtypeStruct(s.shape, s.dtype) for s in slots],
        scratch_shapes=[pltpu.SemaphoreType.DMA((6 * nw,)), pltpu.SemaphoreType.DMA((6 * nw,))],
        input_output_aliases={w: w for w in range(nw)},
    )(*slots)


def _sibling_halves(grads):
    nw = len(grads)

    def body(*refs):
        gs, outs = refs[:nw], refs[nw:2 * nw]
        send_sems, recv_sems = refs[2 * nw:]
        x, y, c, _ = _place()
        cps = []
        for w in range(nw):
            hr = gs[w].shape[1] // 2
            cp = _remote(gs[w].at[:, pl.ds((1 - c) * hr, hr), :], outs[w], send_sems.at[w], recv_sems.at[w],
                         (x, y, 1 - c))
            cp.start()
            cps.append(cp)
        for cp in cps:
            cp.wait()

    any_spec = pl.BlockSpec(memory_space=pl.ANY)
    return pl.pallas_call(
        body, name="rs_sibling_halves",
        in_specs=[any_spec] * nw, out_specs=[any_spec] * nw,
        out_shape=[jax.ShapeDtypeStruct((g.shape[0], g.shape[1] // 2, g.shape[2]), g.dtype) for g in grads],
        scratch_shapes=[pltpu.SemaphoreType.DMA((nw,)), pltpu.SemaphoreType.DMA((nw,))],
    )(*grads)


def _pair_sum_bf16(g, got, *, tm, name):
    S, R, C = g.shape
    hr = R // 2
    nt = hr // tm

    def body(c_ref, g_ref, got_ref, o_ref):
        del c_ref
        o_ref[...] = (g_ref[...] + got_ref[...]).astype(BF16)

    return pl.pallas_call(
        body, name=name,
        grid_spec=pltpu.PrefetchScalarGridSpec(
            num_scalar_prefetch=1, grid=(S, nt),
            in_specs=[pl.BlockSpec((None, tm, C), lambda s, i, c: (s, c[0] * nt + i, 0)),
                      pl.BlockSpec((None, tm, C), lambda s, i, c: (s, i, 0))],
            out_specs=pl.BlockSpec((None, tm, C), lambda s, i, c: (s, i, 0))),
        out_shape=jax.ShapeDtypeStruct((S, hr, C), BF16),
        compiler_params=_params(("parallel", "parallel")),
    )(_core_index(), g, got)


def _scatter_to_owners(pairs):
    nw = len(pairs)

    def body(*refs):
        qs, outs = refs[:nw], refs[nw:2 * nw]
        send_sems, recv_sems = refs[2 * nw:]
        x, y, c, chips = _place()
        cps = []
        for w in range(nw):
            for j, chip in enumerate(chips):
                cp = _remote(qs[w].at[2 * chip[0] + chip[1]], outs[w].at[j], send_sems.at[3 * w + j],
                             recv_sems.at[3 * w + j], (*chip, c))
                cp.start()
                cps.append(cp)
        for cp in cps:
            cp.wait()

    any_spec = pl.BlockSpec(memory_space=pl.ANY)
    return pl.pallas_call(
        body, name="rs_scatter_to_owners",
        in_specs=[any_spec] * nw, out_specs=[any_spec] * nw,
        out_shape=[jax.ShapeDtypeStruct((3,) + q.shape[1:], q.dtype) for q in pairs],
        scratch_shapes=[pltpu.SemaphoreType.DMA((3 * nw,)), pltpu.SemaphoreType.DMA((3 * nw,))],
    )(*pairs)


def _owner_sum(g, got, others, *, tm, name):
    S, R, C = g.shape
    hr = R // 2
    nt = hr // tm

    def body(idx_ref, g_ref, got_ref, o_ref_in, out_ref):
        del idx_ref
        acc = g_ref[...] + got_ref[...]
        for j in range(3):
            acc = acc + o_ref_in[j].astype(F32)
        out_ref[...] = acc

    return pl.pallas_call(
        body, name=name,
        grid_spec=pltpu.PrefetchScalarGridSpec(
            num_scalar_prefetch=1, grid=(nt,),
            in_specs=[pl.BlockSpec((None, tm, C), lambda i, p: (p[1], p[0] * nt + i, 0)),
                      pl.BlockSpec((None, tm, C), lambda i, p: (p[1], i, 0)),
                      pl.BlockSpec((3, tm, C), lambda i, p: (0, i, 0))],
            out_specs=pl.BlockSpec((tm, C), lambda i, p: (i, 0))),
        out_shape=jax.ShapeDtypeStruct((hr, C), F32),
        compiler_params=_params(("parallel",)),
    )(jnp.concatenate([_core_index(), _chip_index()]), g, got, others)


def _swap_halves(halves):
    nw = len(halves)

    def body(*refs):
        hs, outs = refs[:nw], refs[nw:2 * nw]
        send_sems, recv_sems = refs[2 * nw:]
        x, y, c, _ = _place()
        cps = []
        for w in range(nw):
            cp = _remote(hs[w], outs[w], send_sems.at[w], recv_sems.at[w], (x, y, 1 - c))
            cp.start()
            cps.append(cp)
        for cp in cps:
            cp.wait()

    any_spec = pl.BlockSpec(memory_space=pl.ANY)
    return pl.pallas_call(
        body, name="rs_swap_halves",
        in_specs=[any_spec] * nw, out_specs=[any_spec] * nw,
        out_shape=[jax.ShapeDtypeStruct(h.shape, h.dtype) for h in halves],
        scratch_shapes=[pltpu.SemaphoreType.DMA((nw,)), pltpu.SemaphoreType.DMA((nw,))],
    )(*halves)


def _all_reduce_small(packed):
    R, C = packed.shape

    def body(in_ref, out_ref, slots, send_sems, recv_sems):
        x, y, c, _ = _place()
        me = 4 * x + 2 * y + c
        cps = []
        for k in range(1, N_DEV):
            p = (me + k) % N_DEV
            cp = _remote(in_ref, slots.at[me], send_sems.at[k - 1], recv_sems.at[k - 1], (p // 4, (p // 2) % 2, p % 2))
            cp.start()
            cps.append(cp)
        slots[me] = in_ref[...]
        for k in range(1, N_DEV):
            src = (me + N_DEV - k) % N_DEV
            _remote(in_ref, slots.at[src], send_sems.at[k - 1], recv_sems.at[k - 1], (x, y, c)).wait_recv()
        for cp in cps:
            cp.wait_send()
        acc = slots[0]
        for d in range(1, N_DEV):
            acc = acc + slots[d]
        out_ref[...] = acc

    vmem = pl.BlockSpec(memory_space=pltpu.VMEM)
    return pl.pallas_call(
        body, name="all_reduce_small", in_specs=[vmem], out_specs=vmem,
        out_shape=jax.ShapeDtypeStruct((R, C), F32),
        scratch_shapes=[pltpu.VMEM((N_DEV, R, C), F32), pltpu.SemaphoreType.DMA((N_DEV - 1,)),
                        pltpu.SemaphoreType.DMA((N_DEV - 1,))],
        compiler_params=_params(),
    )(packed)


def _adamw_math(w, g, m, v):
    m = ADAM_B1 * m + (1.0 - ADAM_B1) * g
    v = ADAM_B2 * v + (1.0 - ADAM_B2) * (g * g)
    m_hat = m / (1.0 - ADAM_B1 ** ADAM_STEP)
    v_hat = v / (1.0 - ADAM_B2 ** ADAM_STEP)
    delta = -ADAM_LR * (m_hat / (jnp.sqrt(v_hat) + ADAM_EPS) + ADAM_WD * w)
    return delta, m, v


def _adamw(w, g, m, v, *, tm, name):
    R, C = w.shape

    def body(w_ref, g_ref, m_ref, v_ref, d_ref, nm_ref, nv_ref):
        d_ref[...], nm_ref[...], nv_ref[...] = _adamw_math(w_ref[...], g_ref[...], m_ref[...], v_ref[...])

    spec = pl.BlockSpec((tm, C), lambda i: (i, 0))
    return pl.pallas_call(
        body, name=name, grid=(R // tm,), in_specs=[spec] * 4, out_specs=[spec] * 3,
        out_shape=[jax.ShapeDtypeStruct((R, C), F32)] * 3, compiler_params=_params(("parallel",)),
    )(w, g, m, v)


def _adamw_halves(w, own, got, m, v, *, tm, name):
    R, C = w.shape
    nt = (R // 2) // tm

    def body(c_ref, w_ref, own_ref, got_ref, m_ref, v_ref, g_ref, d_ref, nm_ref, nv_ref):
        g = jnp.where(pl.program_id(0) == c_ref[0], own_ref[...], got_ref[...])
        g_ref[...] = g
        d_ref[...], nm_ref[...], nv_ref[...] = _adamw_math(w_ref[...], g, m_ref[...], v_ref[...])

    whole = pl.BlockSpec((tm, C), lambda h, i, c: (h * nt + i, 0))
    half = pl.BlockSpec((tm, C), lambda h, i, c: (i, 0))
    return pl.pallas_call(
        body, name=name,
        grid_spec=pltpu.PrefetchScalarGridSpec(
            num_scalar_prefetch=1, grid=(2, nt),
            in_specs=[whole, half, half, whole, whole], out_specs=[whole] * 4),
        out_shape=[jax.ShapeDtypeStruct((R, C), F32)] * 4, compiler_params=_params(("parallel", "parallel")),
    )(_core_index(), w, own, got, m, v)


SMALL = ["rel_bias_table", "mix_norm_g", "gate_norm_g", "gate_norm_b", "w_spatial", "b_spatial", "attn_sinks",
         "out_norm_a_g", "out_norm_b_g", "ffn_norm_g", "final_norm_g"]
LARGE = ["w_in", "w_out", "w_up", "w_down"]
WEIGHTS = ["rel_bias_table", "mix_norm_g", "w_in", "gate_norm_g", "gate_norm_b", "w_spatial", "b_spatial", "attn_sinks",
           "out_norm_a_g", "out_norm_b_g", "w_out", "ffn_norm_g", "w_up", "w_down", "final_norm_g"]
PACK_UNIT = 8 * 128


def _pack(parts):
    rows = []
    for p in parts:
        flat = p.reshape(-1)
        pad = (-flat.shape[0]) % PACK_UNIT
        rows.append(jnp.pad(flat, (0, pad)).reshape(-1, 128))
    return jnp.concatenate(rows, axis=0)


def _unpack(packed, like):
    out, row = [], 0
    for p in like:
        n = math.prod(p.shape)
        nrows = (n + PACK_UNIT - 1) // PACK_UNIT * 8
        out.append(packed[row:row + nrows].reshape(-1)[:n].reshape(p.shape))
        row += nrows
    return out


def kernel(x, rel_bias_table, mix_norm_g, w_in, gate_norm_g, gate_norm_b, w_spatial, b_spatial, attn_sinks, out_norm_a_g, out_norm_b_g, w_out, ffn_norm_g, w_up, w_down, final_norm_g, loss_target, m_rel_bias_table, m_mix_norm_g, m_w_in, m_gate_norm_g, m_gate_norm_b, m_w_spatial, m_b_spatial, m_attn_sinks, m_out_norm_a_g, m_out_norm_b_g, m_w_out, m_ffn_norm_g, m_w_up, m_w_down, m_final_norm_g, v_rel_bias_table, v_mix_norm_g, v_w_in, v_gate_norm_g, v_gate_norm_b, v_w_spatial, v_b_spatial, v_attn_sinks, v_out_norm_a_g, v_out_norm_b_g, v_w_out, v_ffn_norm_g, v_w_up, v_w_down, v_final_norm_g):
    args = dict(locals())
    wts = {n: args[n] for n in WEIGHTS}
    mom = {n: args["m_" + n] for n in WEIGHTS}
    var = {n: args["v_" + n] for n in WEIGHTS}
    sp = {n: wts[n] for n in SMALL}
    x2, tgt = x[0], loss_target[0]
    T = x2.shape[0]
    tm = min(512, T)
    tl = min(1024, T)
    lg = sp["gate_norm_g"].reshape(A_GROUPS, CHUNK)
    lb = sp["gate_norm_b"].reshape(A_GROUPS, CHUNK)
    wsp = sp["w_spatial"].reshape(A_GROUPS, CHUNK, CHUNK)
    bs_col = sp["b_spatial"].reshape(A_GROUPS, CHUNK, 1)
    sinks = sp["attn_sinks"].reshape(1, B_HEADS)
    ga = sp["out_norm_a_g"].reshape(1, A_WIDTH)
    gb = sp["out_norm_b_g"].reshape(1, B_WIDTH)
    g1 = sp["mix_norm_g"].reshape(1, D_MODEL)
    g2 = sp["ffn_norm_g"].reshape(1, D_MODEL)
    gf = sp["final_norm_g"].reshape(1, D_MODEL)

    def pair_sum(n, g, got):
        return _pair_sum_bf16(g, got, tm=256, name="rs_pair_sum_" + n)

    def owner_sum(n, g, got, others):
        return _owner_sum(g, got, others, tm=256, name="rs_owner_sum_" + n)

    own = [wts[n].reshape(wts[n].shape[1:]) for n in LARGE]
    s_in, s_out, s_up, s_down = [_cast_into_slot(w, tm=256, name="cast_" + n) for n, w in zip(LARGE, own)]
    ((g_in,),) = _carrier([_ride_gather(s_in)], name="gather_w_in")
    win = g_in.transpose(1, 0, 2).reshape(D_MODEL, PROJ_WIDTH)
    bias = _bias_build(sp["rel_bias_table"])
    (n1t, proj), ((g_out,),) = _norm_matmul_wide(x2, g1, win, tm=tm, tn=PROJ_WIDTH // 2, name="in_proj",
                                                 rides=[_ride_gather(s_out)])
    wo = g_out.reshape(A_WIDTH + B_WIDTH, D_MODEL)
    (mixed, mixed_t, ab), ((wu,),) = _mixer_fwd(proj, lg, lb, wsp, bs_col, sinks, bias, ga, gb,
                                                rides=[_ride_gather(s_up)])
    h1 = _matmul_res(mixed, wo, x2, tm=tl, tn=1024, tk=D_MODEL, prologue=_to_bf16, name="out_proj")
    (n2t, zp, z2, z2t), ((g_down,),) = _norm_matmul_sq(h1, g2, wu, tm=tl, tn=512, name="up_proj",
                                                       rides=[_ride_gather(s_down)])
    wd = g_down.reshape(D_FF, D_MODEL)
    h2 = _matmul_res(z2, wd, h1, tm=tl, tn=1024, tk=2048, prologue=_to_bf16, name="down_proj")

    dh2, dh2b, dgf, loss = _loss_bwd(h2, tgt, gf, tm=tm)
    dzp = _matmul_nt(dh2b, wd, tm=tl, tn=1024, tk=D_MODEL, name="bwd_dz", extra=zp, epilogue=_sq_relu_grad,
                     out_dtype=BF16)
    dwd = _matmul_nn(z2t, dh2b, tmo=1024, tn=2048, tk=tl, name="grad_w_down")
    dwd = dwd.reshape(N_CHIPS, D_FF // N_CHIPS, D_MODEL)
    dwu, ((r_d,),) = _matmul_nn(n2t, dzp, tmo=1024, tn=2048, tk=tl, name="grad_w_up", shards=N_CHIPS,
                                rides=[_ride_sibling_halves(dwd)])
    q_d = pair_sum("w_down", dwd, r_d)
    dn2, ((o_d,), (r_u,)) = _matmul_nt(dzp, wu, tm=tl, tn=1024, tk=2048, name="bwd_dn2",
                                       rides=[_ride_scatter(q_d), _ride_sibling_halves(dwu)])
    h_d = owner_sum("w_down", dwd, r_d, o_d)
    q_u = pair_sum("w_up", dwu, r_u)
    dh1, dh1b, dg2 = _rms_bwd_res(dn2, h1, g2, dh2, tm=tm, name="ffn_norm_bwd")
    dwo, ((w_d,),) = _matmul_nn(mixed_t, dh1b, tmo=1024, tn=2048, tk=tl, name="grad_w_out", rides=[_ride_swap(h_d)])
    dwo = dwo.reshape(N_CHIPS, (A_WIDTH + B_WIDTH) // N_CHIPS, D_MODEL)
    dmixed, ((r_o,),) = _matmul_nt(dh1b, wo, tm=tl, tn=1024, tk=D_MODEL, name="bwd_dmixed",
                                   rides=[_ride_sibling_halves(dwo)])
    q_o = pair_sum("w_out", dwo, r_o)
    (duv, dga, dwsp, dbs, dlg, dlb), ((o_o,),) = _gmlp_bwd(proj, ab, dmixed, ga, lg, lb, wsp, bs_col,
                                                           rides=[_ride_scatter(q_o)])
    (dq, dkv, dgb, dsinks, dbias), ((o_u,),) = _attn_bwd(proj, ab, dmixed, gb, sinks, bias, rides=[_ride_scatter(q_u)])
    h_o = owner_sum("w_out", dwo, r_o, o_o)
    h_u = owner_sum("w_up", dwu, r_u, o_u)
    dtable = _bias_grad(dbias)
    dproj = jnp.concatenate([duv, dq, dkv], axis=1)
    dwin, ((w_o,), (w_u,)) = _matmul_nn(n1t, dproj, tmo=1024, tn=PROJ_WIDTH // 2, tk=tl, name="grad_w_in",
                                        rides=[_ride_swap(h_o), _ride_swap(h_u)])
    dwin = dwin.reshape(D_MODEL, N_CHIPS, PROJ_WIDTH // N_CHIPS).transpose(1, 0, 2)
    dn1, ((r_i,),) = _matmul_nt(dproj, win, tm=tl, tn=1024, tk=PROJ_WIDTH, name="bwd_dn1",
                                rides=[_ride_sibling_halves(dwin)])
    q_i = pair_sum("w_in", dwin, r_i)
    (dx, _, dg1), ((o_i,),) = _rms_bwd_res(dn1, x2, g1, dh1, tm=tm, name="mix_norm_bwd", rides=[_ride_scatter(q_i)])
    h_i = owner_sum("w_in", dwin, r_i, o_i)
    ((w_i,),) = _carrier([_ride_swap(h_i)], name="swap_w_in")
    halves = [h_i, h_o, h_u, h_d]
    swapped = [w_i, w_o, w_u, w_d]
    small = {
        "rel_bias_table": dtable.reshape(N_BUCKETS, B_HEADS), "mix_norm_g": dg1, "gate_norm_g": dlg, "gate_norm_b": dlb,
        "w_spatial": dwsp, "b_spatial": dbs, "attn_sinks": dsinks, "out_norm_a_g": dga, "out_norm_b_g": dgb,
        "ffn_norm_g": dg2, "final_norm_g": dgf,
    }

    out_g, out_d, out_m, out_v = {}, {}, {}, {}
    for n, w, h, s in zip(LARGE, own, halves, swapped):
        shape = wts[n].shape
        g, d, nm, nv = _adamw_halves(w, h, s, mom[n].reshape(w.shape), var[n].reshape(w.shape), tm=256,
                                     name="adamw_" + n)
        out_g[n], out_d[n], out_m[n], out_v[n] = g.reshape(shape), d.reshape(shape), nm.reshape(shape), nv.reshape(shape)

    like = [wts[n] for n in SMALL]
    g_small = _all_reduce_small(_pack([small[n] for n in SMALL]))
    d_s, m_s, v_s = _adamw(_pack(like), g_small, _pack([mom[n] for n in SMALL]), _pack([var[n] for n in SMALL]),
                           tm=g_small.shape[0], name="adamw_small")
    for n, g, d, nm, nv in zip(SMALL, _unpack(g_small, like), _unpack(d_s, like), _unpack(m_s, like),
                               _unpack(v_s, like)):
        out_g[n], out_d[n], out_m[n], out_v[n] = g, d, nm, nv

    total = lax.psum(loss[0, 0], ("x", "y", "c"))
    return (total, dx[None], *[out_g[n] for n in WEIGHTS], *[out_d[n] for n in WEIGHTS],
            *[out_m[n] for n in WEIGHTS], *[out_v[n] for n in WEIGHTS])
```

```python
import functools
import math

import numpy as np
import jax
import jax.numpy as jnp
from jax import lax
from jax.experimental import pallas as pl
from jax.experimental.pallas import tpu as pltpu

F32 = jnp.float32
BF16 = jnp.bfloat16

D_MODEL = 2048
CHUNK = 128
A_GROUPS = 8
A_WIDTH = 1024
HEAD_DIM = 64
B_HEADS = 16
Q_PER_KV = 8
B_WIDTH = 1024
KV_WIDTH = 128
PROJ_WIDTH = 3328
D_FF = 8192
N_BUCKETS = 32
EPS = 1e-5
NEG = -1e30
SCALE = HEAD_DIM ** -0.5
N_CHIPS = 4
N_DEV = 8

ADAM_LR = 0.001
ADAM_B1 = 0.9
ADAM_B2 = 0.999
ADAM_EPS = 1e-08
ADAM_WD = 0.01
ADAM_STEP = 10

VMEM_LIMIT = 56 * 1024 * 1024
MESH = pl.DeviceIdType.MESH


def _bucket_thresholds():
    d = np.arange(CHUNK)
    n_exact = N_BUCKETS // 2
    relf = np.maximum(d, n_exact).astype(np.float64)
    large = n_exact + (np.log(relf / n_exact) / math.log(CHUNK / n_exact) * (N_BUCKETS - n_exact)).astype(np.int32)
    bucket = np.where(d < n_exact, d, np.minimum(large, N_BUCKETS - 1))
    return [int(np.min(d[bucket >= b])) for b in range(1, N_BUCKETS)]


BUCKET_THR = _bucket_thresholds()


def _params(sem=None):
    return pltpu.CompilerParams(dimension_semantics=sem, vmem_limit_bytes=VMEM_LIMIT)


def _gelu(x):
    c = math.sqrt(2.0 / math.pi)
    return 0.5 * x * (1.0 + jnp.tanh(c * (x + 0.044715 * (x * x * x))))


def _gelu_and_grad(x):
    c = math.sqrt(2.0 / math.pi)
    x2 = x * x
    t = jnp.tanh(c * (x + 0.044715 * (x2 * x)))
    g = 0.5 * x * (1.0 + t)
    dg = 0.5 * (1.0 + t) + 0.5 * x * (1.0 - t * t) * (c * (1.0 + 3.0 * 0.044715 * x2))
    return g, dg


def _dot(a, b):
    return jnp.dot(a, b, preferred_element_type=F32)


def _dot_nt(a, b):
    return lax.dot_general(a, b, (((1,), (1,)), ((), ())), preferred_element_type=F32)


def _dot_tn(a, b):
    return lax.dot_general(a, b, (((0,), (0,)), ((), ())), preferred_element_type=F32)


def _rms_bwd(dn, h, g):
    r = lax.rsqrt(jnp.mean(h * h, axis=-1, keepdims=True) + EPS)
    w = dn * g
    dh = r * w - h * ((r * r * r) * jnp.mean(w * h, axis=-1, keepdims=True))
    return dh, r


def _place():
    x, y, c = lax.axis_index("x"), lax.axis_index("y"), lax.axis_index("c")
    chips = [(1 - x, y), (x, 1 - y), (1 - x, 1 - y)]
    return x, y, c, chips


def _remote(src, dst, send_sem, recv_sem, to):
    return pltpu.make_async_remote_copy(src_ref=src, dst_ref=dst, send_sem=send_sem, recv_sem=recv_sem,
                                        device_id=to, device_id_type=MESH)


class _Ride:
    def __init__(self, args, out_shape, n_sem, start, finish, mid=None, mid_frac=0.8, aliases=None):
        self.args, self.out_shape, self.n_sem = list(args), list(out_shape), n_sem
        self.start, self.mid, self.finish, self.mid_frac = start, mid, finish, mid_frac
        self.aliases = dict(aliases or {})


def _call(body, *, name, grid, in_specs, out_specs, out_shape, scratch_shapes=(), sem=None, rides=()):
    single = not isinstance(out_shape, (list, tuple))
    out_specs = [out_specs] if single else list(out_specs)
    out_shape = [out_shape] if single else list(out_shape)
    n_in, n_out, n_scr = len(in_specs), len(out_shape), len(scratch_shapes)
    r_in = [len(r.args) for r in rides]
    r_out = [len(r.out_shape) for r in rides]
    any_spec = pl.BlockSpec(memory_space=pl.ANY)
    aliases, off_i, off_o = {}, n_in, n_out
    for r in rides:
        for i, o in r.aliases.items():
            aliases[off_i + i] = off_o + o
        off_i += len(r.args)
        off_o += len(r.out_shape)
    steps = math.prod(grid)

    def wrapped(*refs):
        p = 0
        ins = refs[p:p + n_in]; p += n_in
        rins = refs[p:p + sum(r_in)]; p += sum(r_in)
        outs = refs[p:p + n_out]; p += n_out
        routs = refs[p:p + sum(r_out)]; p += sum(r_out)
        scr = refs[p:p + n_scr]; p += n_scr
        sems = refs[p:]
        parts, pi, po = [], 0, 0
        for k, r in enumerate(rides):
            parts.append((rins[pi:pi + r_in[k]], routs[po:po + r_out[k]], sems[2 * k], sems[2 * k + 1]))
            pi += r_in[k]
            po += r_out[k]
        lin = 0
        for d in range(len(grid)):
            lin = lin * grid[d] + pl.program_id(d)
        if rides:
            @pl.when(lin == 0)
            def _():
                for r, part in zip(rides, parts):
                    r.start(*part)
        body(*ins, *outs, *scr)
        for r, part in zip(rides, parts):
            if r.mid is not None:
                @pl.when(lin == min(steps - 1, int(r.mid_frac * steps)))
                def _(r=r, part=part):
                    r.mid(*part)
        if rides:
            @pl.when(lin == steps - 1)
            def _():
                for r, part in zip(rides, parts):
                    r.finish(*part)

    scratch = list(scratch_shapes)
    for r in rides:
        scratch += [pltpu.SemaphoreType.DMA((r.n_sem,)), pltpu.SemaphoreType.DMA((r.n_sem,))]
    if rides:
        sem = ("arbitrary",) * len(grid)
    res = pl.pallas_call(
        wrapped, name=name, grid=grid,
        in_specs=list(in_specs) + [any_spec] * sum(r_in),
        out_specs=out_specs + [any_spec] * sum(r_out),
        out_shape=out_shape + [s for r in rides for s in r.out_shape],
        scratch_shapes=scratch, input_output_aliases=aliases,
        compiler_params=_params(sem),
    )

    def run(*args):
        got = res(*args, *[a for r in rides for a in r.args])
        mine = got[0] if single else list(got[:n_out])
        if not rides:
            return mine
        rest, out = list(got[n_out:]), []
        for k in range(len(rides)):
            out.append(rest[:r_out[k]])
            rest = rest[r_out[k]:]
        return mine, out

    return run


def _ride_gather(slot, mid_frac=0.8):
    hr = slot.shape[1] // 2

    def start(ins, outs, ss, rs):
        x, y, c, chips = _place()
        mine = outs[0].at[2 * x + y, pl.ds(c * hr, hr), :]
        for j, chip in enumerate(chips):
            _remote(mine, mine, ss.at[j], rs.at[j], (*chip, c)).start()

    def mid(ins, outs, ss, rs):
        x, y, c, chips = _place()
        for j, chip in enumerate(chips):
            landed = outs[0].at[2 * chip[0] + chip[1], pl.ds(c * hr, hr), :]
            _remote(landed, landed, ss.at[j], rs.at[j], (x, y, c)).wait_recv()
            _remote(landed, landed, ss.at[3 + j], rs.at[3 + j], (x, y, 1 - c)).start()

    def finish(ins, outs, ss, rs):
        x, y, c, chips = _place()
        for j, chip in enumerate(chips):
            other = outs[0].at[2 * chip[0] + chip[1], pl.ds((1 - c) * hr, hr), :]
            _remote(other, other, ss.at[3 + j], rs.at[3 + j], (x, y, c)).wait_recv()
        for j in range(6):
            part = outs[0].at[0, pl.ds(0, hr), :]
            _remote(part, part, ss.at[j], rs.at[j], (x, y, c)).wait_send()

    return _Ride([slot], [jax.ShapeDtypeStruct(slot.shape, slot.dtype)], 6, start, finish, mid=mid,
                 mid_frac=mid_frac, aliases={0: 0})


def _ride_sibling_halves(g):
    S, R, C = g.shape
    hr = R // 2

    def copy(ins, outs, ss, rs):
        x, y, c, _ = _place()
        return _remote(ins[0].at[:, pl.ds((1 - c) * hr, hr), :], outs[0], ss.at[0], rs.at[0], (x, y, 1 - c))

    return _Ride([g], [jax.ShapeDtypeStruct((S, hr, C), g.dtype)], 1,
                 lambda *a: copy(*a).start(), lambda *a: copy(*a).wait())


def _ride_scatter(q):
    def copies(ins, outs, ss, rs):
        x, y, c, chips = _place()
        return [_remote(ins[0].at[2 * chip[0] + chip[1]], outs[0].at[j], ss.at[j], rs.at[j], (*chip, c))
                for j, chip in enumerate(chips)]

    def start(*a):
        for cp in copies(*a):
            cp.start()

    def finish(*a):
        for cp in copies(*a):
            cp.wait()

    return _Ride([q], [jax.ShapeDtypeStruct((3,) + q.shape[1:], q.dtype)], 3, start, finish)


def _ride_swap(h):
    def copy(ins, outs, ss, rs):
        x, y, c, _ = _place()
        return _remote(ins[0], outs[0], ss.at[0], rs.at[0], (x, y, 1 - c))

    return _Ride([h], [jax.ShapeDtypeStruct(h.shape, h.dtype)], 1,
                 lambda *a: copy(*a).start(), lambda *a: copy(*a).wait())


def _carrier(rides, *, name):
    _, outs = _call(lambda: None, name=name, grid=(1,), in_specs=[], out_specs=[], out_shape=[], rides=rides)()
    return outs


def _sq_relu_bf16(z):
    z = jnp.maximum(z, 0.0)
    return (z * z).astype(BF16)


def _norm_bf16(a_ref, g_ref):
    xf = a_ref[...]
    r = lax.rsqrt(jnp.mean(xf * xf, axis=-1, keepdims=True) + EPS)
    return ((xf * r) * g_ref[...]).astype(BF16)


def _norm_matmul_wide(a, g, b, *, tm, tn, name, rides=()):
    T, K = a.shape
    N = b.shape[1]

    def body(a_ref, g_ref, b_ref, nt_ref, o_ref):
        n = _norm_bf16(a_ref, g_ref)
        nt_ref[...] = n.T
        o_ref[...] = _dot(n, b_ref[...])

    return _call(
        body, name=name, grid=(N // tn, T // tm),
        in_specs=[pl.BlockSpec((tm, K), lambda j, i: (i, 0)), pl.BlockSpec((1, K), lambda j, i: (0, 0)),
                  pl.BlockSpec((K, tn), lambda j, i: (0, j))],
        out_specs=[pl.BlockSpec((None, K, tm), lambda j, i: (j, 0, i)), pl.BlockSpec((tm, tn), lambda j, i: (i, j))],
        out_shape=[jax.ShapeDtypeStruct((N // tn, K, T), BF16), jax.ShapeDtypeStruct((T, N), F32)],
        sem=("arbitrary", "arbitrary"), rides=rides,
    )(a, g, b)


def _norm_matmul_sq(a, g, b, *, tm, tn, name, rides=()):
    T, K = a.shape
    per = b.shape[2] // tn
    N = b.shape[0] * b.shape[2]

    def body(a_ref, g_ref, b_ref, nt_ref, o_ref, z_ref, zt_ref, n_scr):
        @pl.when(pl.program_id(1) == 0)
        def _():
            n = _norm_bf16(a_ref, g_ref)
            n_scr[...] = n
            nt_ref[...] = n.T
        p = _dot(n_scr[...], b_ref[...])
        o_ref[...] = p
        z = _sq_relu_bf16(p)
        z_ref[...] = z
        zt_ref[...] = z.T

    return _call(
        body, name=name, grid=(T // tm, N // tn),
        in_specs=[pl.BlockSpec((tm, K), lambda i, j: (i, 0)), pl.BlockSpec((1, K), lambda i, j: (0, 0)),
                  pl.BlockSpec((None, K, tn), lambda i, j: (j // per, 0, j % per))],
        out_specs=[pl.BlockSpec((K, tm), lambda i, j: (0, i)), pl.BlockSpec((tm, tn), lambda i, j: (i, j)),
                   pl.BlockSpec((tm, tn), lambda i, j: (i, j)), pl.BlockSpec((tn, tm), lambda i, j: (j, i))],
        out_shape=[jax.ShapeDtypeStruct((K, T), BF16), jax.ShapeDtypeStruct((T, N), F32),
                   jax.ShapeDtypeStruct((T, N), BF16), jax.ShapeDtypeStruct((N, T), BF16)],
        scratch_shapes=[pltpu.VMEM((tm, K), BF16)],
        sem=("parallel", "arbitrary"), rides=rides,
    )(a, g, b)


def _matmul_nn(at, b, *, tmo, tn, tk, name, shards=1, rides=()):
    M, T = at.shape[-2:]
    N = b.shape[1]
    if at.ndim == 3:
        a_spec = pl.BlockSpec((None, tmo, tk), lambda i, j, k: (0, i, k))
    else:
        a_spec = pl.BlockSpec((tmo, tk), lambda i, j, k: (i, k))
    if shards > 1:
        per = (N // shards) // tn
        out_spec = pl.BlockSpec((None, tmo, tn), lambda i, j, k: (j // per, i, j % per))
        out_shape = jax.ShapeDtypeStruct((shards, M, N // shards), F32)
    else:
        out_spec = pl.BlockSpec((tmo, tn), lambda i, j, k: (i, j))
        out_shape = jax.ShapeDtypeStruct((M, N), F32)

    def body(a_ref, b_ref, o_ref):
        k = pl.program_id(2)
        p = _dot(a_ref[...], b_ref[...])

        @pl.when(k == 0)
        def _():
            o_ref[...] = p

        @pl.when(k > 0)
        def _():
            o_ref[...] += p

    return _call(
        body, name=name, grid=(M // tmo, N // tn, T // tk),
        in_specs=[a_spec, pl.BlockSpec((tk, tn), lambda i, j, k: (k, j))],
        out_specs=out_spec, out_shape=out_shape,
        sem=("parallel", "parallel", "arbitrary"), rides=rides,
    )(at, b)


def _to_bf16(v):
    return v.astype(BF16)


def _matmul_res(a, b, res, *, tm, tn, tk, prologue, name):
    T, K = a.shape
    N = b.shape[1]

    def body(a_ref, b_ref, res_ref, o_ref):
        k = pl.program_id(2)
        p = _dot(prologue(a_ref[...]), b_ref[...])

        @pl.when(k == 0)
        def _():
            o_ref[...] = res_ref[...] + p

        @pl.when(k > 0)
        def _():
            o_ref[...] += p

    return pl.pallas_call(
        body, name=name, grid=(T // tm, N // tn, K // tk),
        in_specs=[pl.BlockSpec((tm, tk), lambda i, j, k: (i, k)), pl.BlockSpec((tk, tn), lambda i, j, k: (k, j)),
                  pl.BlockSpec((tm, tn), lambda i, j, k: (i, j))],
        out_specs=pl.BlockSpec((tm, tn), lambda i, j, k: (i, j)),
        out_shape=jax.ShapeDtypeStruct((T, N), F32),
        compiler_params=_params(("parallel", "parallel", "arbitrary")),
    )(a, b, res)


def _matmul_nt(a, b, *, tm, tn, tk, name, extra=None, epilogue=None, out_dtype=F32, rides=()):
    T, K = a.shape
    if b.ndim == 3:
        per = b.shape[2] // tk
        N = b.shape[1]
        b_spec = pl.BlockSpec((None, tn, tk), lambda i, j, k: (k // per, j, k % per))
    else:
        N = b.shape[0]
        b_spec = pl.BlockSpec((tn, tk), lambda i, j, k: (j, k))
    nk = K // tk
    assert out_dtype == F32 or nk == 1
    in_specs = [pl.BlockSpec((tm, tk), lambda i, j, k: (i, k)), b_spec]
    args = [a, b]
    if extra is not None:
        in_specs.append(pl.BlockSpec((tm, tn), lambda i, j, k: (i, j)))
        args.append(extra)

    def body(*refs):
        a_ref, b_ref = refs[0], refs[1]
        o_ref = refs[-1]
        p = _dot_nt(a_ref[...].astype(BF16), b_ref[...])
        if nk == 1:
            if epilogue is not None:
                p = epilogue(p, refs[2][...])
            o_ref[...] = p.astype(out_dtype)
        else:
            k = pl.program_id(2)

            @pl.when(k == 0)
            def _():
                o_ref[...] = p

            @pl.when(k > 0)
            def _():
                o_ref[...] += p

    return _call(
        body, name=name, grid=(T // tm, N // tn, nk),
        in_specs=in_specs,
        out_specs=pl.BlockSpec((tm, tn), lambda i, j, k: (i, j)),
        out_shape=jax.ShapeDtypeStruct((T, N), out_dtype),
        sem=("parallel", "parallel", "arbitrary"), rides=rides,
    )(*args)


def _matmul_tn(a, b, *, tmo, tn, tk, name, a_prologue=_to_bf16, shards=1, rides=()):
    T, M = a.shape
    N = b.shape[1]
    if shards > 1:
        per = (N // shards) // tn
        out_spec = pl.BlockSpec((None, tmo, tn), lambda i, j, k: (j // per, i, j % per))
        out_shape = jax.ShapeDtypeStruct((shards, M, N // shards), F32)
    else:
        out_spec = pl.BlockSpec((tmo, tn), lambda i, j, k: (i, j))
        out_shape = jax.ShapeDtypeStruct((M, N), F32)

    def body(a_ref, b_ref, o_ref):
        k = pl.program_id(2)
        p = _dot_tn(a_prologue(a_ref[...]), b_ref[...].astype(BF16))

        @pl.when(k == 0)
        def _():
            o_ref[...] = p

        @pl.when(k > 0)
        def _():
            o_ref[...] += p

    return _call(
        body, name=name, grid=(M // tmo, N // tn, T // tk),
        in_specs=[pl.BlockSpec((tk, tmo), lambda i, j, k: (k, i)), pl.BlockSpec((tk, tn), lambda i, j, k: (k, j))],
        out_specs=out_spec, out_shape=out_shape,
        sem=("parallel", "parallel", "arbitrary"), rides=rides,
    )(a, b)


def _loss_bwd(h2, tgt, g, *, tm):
    T, D = h2.shape

    def body(h_ref, t_ref, g_ref, dh_ref, dhb_ref, dg_ref, loss_ref):
        @pl.when(pl.program_id(0) == 0)
        def _():
            dg_ref[...] = jnp.zeros_like(dg_ref)
            loss_ref[...] = jnp.zeros_like(loss_ref)
        h = h_ref[...]
        gg = g_ref[...]
        r = lax.rsqrt(jnp.mean(h * h, axis=-1, keepdims=True) + EPS)
        hn = h * r
        err = hn * gg - t_ref[...]
        loss_ref[...] += 0.5 * jnp.sum(jnp.mean(err * err, axis=-1, keepdims=True), axis=0, keepdims=True)
        dy = err * (1.0 / D)
        dg_ref[...] += jnp.sum(dy * hn, axis=0, keepdims=True)
        w = dy * gg
        dh = r * w - h * ((r * r * r) * jnp.mean(w * h, axis=-1, keepdims=True))
        dh_ref[...] = dh
        dhb_ref[...] = dh.astype(BF16)

    tile = pl.BlockSpec((tm, D), lambda i: (i, 0))
    return pl.pallas_call(
        body, name="loss_bwd", grid=(T // tm,),
        in_specs=[tile, tile, pl.BlockSpec((1, D), lambda i: (0, 0))],
        out_specs=[tile, tile, pl.BlockSpec((1, D), lambda i: (0, 0)), pl.BlockSpec((1, 1), lambda i: (0, 0))],
        out_shape=[jax.ShapeDtypeStruct((T, D), F32), jax.ShapeDtypeStruct((T, D), BF16),
                   jax.ShapeDtypeStruct((1, D), F32), jax.ShapeDtypeStruct((1, 1), F32)],
        compiler_params=_params(("arbitrary",)),
    )(h2, tgt, g)


def _rms_bwd_res(dn, h, g, dres, *, tm, name, rides=()):
    T, D = h.shape

    def body(dn_ref, h_ref, g_ref, dres_ref, dh_ref, dhb_ref, dg_ref):
        @pl.when(pl.program_id(0) == 0)
        def _():
            dg_ref[...] = jnp.zeros_like(dg_ref)
        h_ = h_ref[...]
        dn_ = dn_ref[...]
        dh, r = _rms_bwd(dn_, h_, g_ref[...])
        dg_ref[...] += jnp.sum(dn_ * (h_ * r), axis=0, keepdims=True)
        dh = dres_ref[...] + dh
        dh_ref[...] = dh
        dhb_ref[...] = dh.astype(BF16)

    tile = pl.BlockSpec((tm, D), lambda i: (i, 0))
    return _call(
        body, name=name, grid=(T // tm,),
        in_specs=[tile, tile, pl.BlockSpec((1, D), lambda i: (0, 0)), tile],
        out_specs=[tile, tile, pl.BlockSpec((1, D), lambda i: (0, 0))],
        out_shape=[jax.ShapeDtypeStruct((T, D), F32), jax.ShapeDtypeStruct((T, D), BF16),
                   jax.ShapeDtypeStruct((1, D), F32)],
        sem=("arbitrary",), rides=rides,
    )(dn, h, g, dres)


def _rel_distance():
    i = lax.broadcasted_iota(jnp.int32, (CHUNK, 2 * CHUNK), 0)
    j = lax.broadcasted_iota(jnp.int32, (CHUNK, 2 * CHUNK), 1)
    return i + CHUNK - j


def _bias_build(table):
    def body(tab_ref, o_ref):
        rel = _rel_distance()
        ge = [rel >= t for t in BUCKET_THR]
        for h in range(B_HEADS):
            cur = jnp.full((CHUNK, 2 * CHUNK), tab_ref[0, h], F32)
            for b in range(1, N_BUCKETS):
                cur = jnp.where(ge[b - 1], tab_ref[b, h], cur)
            o_ref[h] = cur

    return pl.pallas_call(
        body, name="bias_build",
        in_specs=[pl.BlockSpec(memory_space=pltpu.SMEM)],
        out_specs=pl.BlockSpec(memory_space=pltpu.VMEM),
        out_shape=jax.ShapeDtypeStruct((B_HEADS, CHUNK, 2 * CHUNK), F32),
    )(table)


def _bias_grad(dbias):
    def body(db_ref, o_ref, acc_ref):
        rel = _rel_distance()
        lo = [0] + BUCKET_THR
        hi = BUCKET_THR + [CHUNK]
        for b in range(N_BUCKETS):
            m = (rel >= lo[b]) & (rel < hi[b])
            for h in range(B_HEADS):
                row = b * B_HEADS + h
                acc_ref[row:row + 1, :] = jnp.sum(jnp.where(m, db_ref[h], 0.0), axis=0, keepdims=True)
        o_ref[...] = jnp.sum(acc_ref[...], axis=1, keepdims=True)

    return pl.pallas_call(
        body, name="bias_grad",
        in_specs=[pl.BlockSpec(memory_space=pltpu.VMEM)],
        out_specs=pl.BlockSpec(memory_space=pltpu.VMEM),
        out_shape=jax.ShapeDtypeStruct((N_BUCKETS * B_HEADS, 1), F32),
        scratch_shapes=[pltpu.VMEM((N_BUCKETS * B_HEADS, 2 * CHUNK), F32)],
    )(dbias)


def _causal_mask():
    t = lax.broadcasted_iota(jnp.int32, (CHUNK, CHUNK), 0)
    s = lax.broadcasted_iota(jnp.int32, (CHUNK, CHUNK), 1)
    return s <= t


def _band_mask(n):
    rel = _rel_distance()
    j = lax.broadcasted_iota(jnp.int32, (CHUNK, 2 * CHUNK), 1)
    return (rel >= 0) & (rel < CHUNK) & ((n > 0) | (j >= CHUNK))


def _gate_forward(u, v, lg, lb, wc, bs):
    ug = _gelu(u)
    vg = _gelu(v)
    mu = jnp.mean(vg, axis=-1, keepdims=True)
    xc = vg - mu
    rstd = lax.rsqrt(jnp.mean(xc * xc, axis=-1, keepdims=True) + EPS)
    xhat = xc * rstd
    vl = (xhat * lg + lb).astype(BF16)
    mixed = _dot(wc, vl) + bs
    return ug, xhat, rstd, vl, mixed


def _softmax_scores(qk, bias, mask, sink):
    s = qk * SCALE + bias
    s = jnp.where(mask, s, NEG)
    m = jnp.maximum(jnp.max(s, axis=-1, keepdims=True), sink)
    p = jnp.exp(s - m)
    e_sink = jnp.exp(sink - m)
    inv = 1.0 / (jnp.sum(p, axis=-1, keepdims=True) + e_sink)
    return p * inv, e_sink * inv


PAIRS = Q_PER_KV // 2


def _head(g, pr, e):
    return g * Q_PER_KV + 2 * pr + e


def _stack_pairs(ref, g, col0=0):
    w = 2 * HEAD_DIM
    return jnp.concatenate([ref[:, col0 + (g * PAIRS + pr) * w:col0 + (g * PAIRS + pr + 1) * w] for pr in range(PAIRS)],
                           axis=0)


def _low_lanes():
    return lax.broadcasted_iota(jnp.int32, (2 * CHUNK, 2 * HEAD_DIM), 1) < HEAD_DIM


def _band_operands(kv_prev, kv_cur):
    band = jnp.concatenate([kv_prev, kv_cur], axis=0)
    low = _low_lanes()
    ops = []
    for cat in (band[:, :KV_WIDTH], band[:, KV_WIDTH:]):
        rol = pltpu.roll(cat, HEAD_DIM, 1)
        ops.append([[jnp.where(low if e == 0 else ~low, cat if g == e else rol, 0.0).astype(BF16) for e in range(2)]
                    for g in range(2)])
    return ops


def _mixer_fwd(proj, lg, lb, wsp, bs_col, sinks, bias, ga, gb, rides=()):
    T = proj.shape[0]
    nb = T // CHUNK

    def body(u_ref, v_ref, q_ref, kvc_ref, kvp_ref, lg_ref, lb_ref, w_ref, bs_ref, sink_ref, bias_ref,
             ga_ref, gb_ref, mixed_ref, mixed_t_ref, ab_ref):
        n = pl.program_id(0)
        causal = _causal_mask()
        ssq = jnp.zeros((CHUNK, 1), F32)
        for g in range(A_GROUPS):
            cols = slice(g * CHUNK, (g + 1) * CHUNK)
            wc = jnp.where(causal, w_ref[g], 0.0).astype(BF16)
            ug, _, _, _, mixed = _gate_forward(u_ref[:, cols], v_ref[:, cols], lg_ref[g:g + 1, :], lb_ref[g:g + 1, :],
                                               wc, bs_ref[g])
            a = ug * mixed
            ab_ref[:, cols] = a
            ssq = ssq + jnp.sum(a * a, axis=-1, keepdims=True)
        ra = lax.rsqrt(ssq * (1.0 / A_WIDTH) + EPS)
        mixed_ref[:, :A_WIDTH] = ((ab_ref[:, :A_WIDTH] * ra) * ga_ref[...]).astype(BF16)

        mask = _band_mask(n)
        kops, vops = _band_operands(kvp_ref[...], kvc_ref[...])
        ssq = jnp.zeros((CHUNK, 1), F32)
        for g in range(B_HEADS // Q_PER_KV):
            qst = _stack_pairs(q_ref, g).astype(BF16)
            o_st = jnp.zeros((PAIRS * CHUNK, 2 * HEAD_DIM), F32)
            for e in range(2):
                s_all = _dot_nt(qst, kops[g][e])
                ps = []
                for pr in range(PAIRS):
                    h = _head(g, pr, e)
                    p, _ = _softmax_scores(s_all[pr * CHUNK:(pr + 1) * CHUNK], bias_ref[h], mask, sink_ref[0, h])
                    ps.append(p.astype(BF16))
                o_st = o_st + _dot(jnp.concatenate(ps, axis=0), vops[g][e])
            for pr in range(PAIRS):
                o = o_st[pr * CHUNK:(pr + 1) * CHUNK]
                c0 = A_WIDTH + (g * PAIRS + pr) * 2 * HEAD_DIM
                ab_ref[:, c0:c0 + 2 * HEAD_DIM] = o
                ssq = ssq + jnp.sum(o * o, axis=-1, keepdims=True)
        rb = lax.rsqrt(ssq * (1.0 / B_WIDTH) + EPS)
        mixed_ref[:, A_WIDTH:] = ((ab_ref[:, A_WIDTH:] * rb) * gb_ref[...]).astype(BF16)
        mixed_t_ref[...] = mixed_ref[...].T

    full = lambda *shape: pl.BlockSpec(shape, lambda n: (0,) * len(shape))
    return _call(
        body, name="mixer_fwd", grid=(nb,),
        in_specs=[pl.BlockSpec((CHUNK, A_WIDTH), lambda n: (n, 0)),
                  pl.BlockSpec((CHUNK, A_WIDTH), lambda n: (n, 1)),
                  pl.BlockSpec((CHUNK, B_WIDTH), lambda n: (n, 2)),
                  pl.BlockSpec((CHUNK, 2 * KV_WIDTH), lambda n: (n, 12)),
                  pl.BlockSpec((CHUNK, 2 * KV_WIDTH), lambda n: (jnp.maximum(n - 1, 0), 12)),
                  full(A_GROUPS, CHUNK), full(A_GROUPS, CHUNK), full(A_GROUPS, CHUNK, CHUNK), full(A_GROUPS, CHUNK, 1),
                  pl.BlockSpec(memory_space=pltpu.SMEM), full(B_HEADS, CHUNK, 2 * CHUNK),
                  full(1, A_WIDTH), full(1, B_WIDTH)],
        out_specs=[pl.BlockSpec((CHUNK, D_MODEL), lambda n: (n, 0)), pl.BlockSpec((D_MODEL, CHUNK), lambda n: (0, n)),
                   pl.BlockSpec((CHUNK, D_MODEL), lambda n: (n, 0))],
        out_shape=[jax.ShapeDtypeStruct((T, D_MODEL), BF16), jax.ShapeDtypeStruct((D_MODEL, T), BF16),
                   jax.ShapeDtypeStruct((T, D_MODEL), F32)],
        sem=("parallel",), rides=rides,
    )(proj, proj, proj, proj, proj, lg, lb, wsp, bs_col, sinks, bias, ga, gb)


def _gmlp_bwd(proj, ab, dmixed, ga, lg, lb, wsp, bs_col, rides=()):
    T = proj.shape[0]
    nb = T // CHUNK

    def body(u_ref, v_ref, a_ref, dna_ref, ga_ref, lg_ref, lb_ref, w_ref, bs_ref,
             dp_ref, dga_ref, dw_ref, dbs_ref, dlg_ref, dlb_ref):
        @pl.when(pl.program_id(0) == 0)
        def _():
            for r in (dga_ref, dw_ref, dbs_ref, dlg_ref, dlb_ref):
                r[...] = jnp.zeros_like(r)
        causal = _causal_mask()
        a_all = a_ref[...]
        dna = dna_ref[...]
        da_all, ra = _rms_bwd(dna, a_all, ga_ref[...])
        dga_ref[...] += jnp.sum(dna * (a_all * ra), axis=0, keepdims=True)
        for g in range(A_GROUPS):
            cols = slice(g * CHUNK, (g + 1) * CHUNK)
            wc = jnp.where(causal, w_ref[g], 0.0).astype(BF16)
            lgg = lg_ref[g:g + 1, :]
            u = u_ref[:, cols]
            v = v_ref[:, cols]
            ug, xhat, rstd, vl, mixed = _gate_forward(u, v, lgg, lb_ref[g:g + 1, :], wc, bs_ref[g])
            da = da_all[:, cols]
            dug = da * mixed
            dmg = da * ug
            dmg_b = dmg.astype(BF16)
            dbs_ref[g] += jnp.sum(dmg, axis=-1, keepdims=True)
            dw_ref[g] += jnp.where(causal, _dot_nt(dmg_b, vl), 0.0)
            dvl = _dot_tn(wc, dmg_b)
            dlg_ref[g:g + 1, :] += jnp.sum(dvl * xhat, axis=0, keepdims=True)
            dlb_ref[g:g + 1, :] += jnp.sum(dvl, axis=0, keepdims=True)
            dxh = dvl * lgg
            dvg = rstd * (dxh - jnp.mean(dxh, axis=-1, keepdims=True)
                          - xhat * jnp.mean(dxh * xhat, axis=-1, keepdims=True))
            _, gu = _gelu_and_grad(u)
            _, gv = _gelu_and_grad(v)
            dp_ref[:, cols] = (dug * gu).astype(BF16)
            dp_ref[:, A_WIDTH + g * CHUNK:A_WIDTH + (g + 1) * CHUNK] = (dvg * gv).astype(BF16)

    full = lambda *shape: pl.BlockSpec(shape, lambda n: (0,) * len(shape))
    return _call(
        body, name="gmlp_bwd", grid=(nb,),
        in_specs=[pl.BlockSpec((CHUNK, A_WIDTH), lambda n: (n, 0)),
                  pl.BlockSpec((CHUNK, A_WIDTH), lambda n: (n, 1)),
                  pl.BlockSpec((CHUNK, A_WIDTH), lambda n: (n, 0)),
                  pl.BlockSpec((CHUNK, A_WIDTH), lambda n: (n, 0)),
                  full(1, A_WIDTH), full(A_GROUPS, CHUNK), full(A_GROUPS, CHUNK), full(A_GROUPS, CHUNK, CHUNK),
                  full(A_GROUPS, CHUNK, 1)],
        out_specs=[pl.BlockSpec((CHUNK, 2 * A_WIDTH), lambda n: (n, 0)),
                   full(1, A_WIDTH), full(A_GROUPS, CHUNK, CHUNK), full(A_GROUPS, CHUNK, 1),
                   full(A_GROUPS, CHUNK), full(A_GROUPS, CHUNK)],
        out_shape=[jax.ShapeDtypeStruct((T, 2 * A_WIDTH), BF16),
                   jax.ShapeDtypeStruct((1, A_WIDTH), F32), jax.ShapeDtypeStruct((A_GROUPS, CHUNK, CHUNK), F32),
                   jax.ShapeDtypeStruct((A_GROUPS, CHUNK, 1), F32), jax.ShapeDtypeStruct((A_GROUPS, CHUNK), F32),
                   jax.ShapeDtypeStruct((A_GROUPS, CHUNK), F32)],
        sem=("arbitrary",), rides=rides,
    )(proj, proj, ab, dmixed, ga, lg, lb, wsp, bs_col)


def _attn_bwd(proj, ab, dmixed, gb, sinks, bias, rides=()):
    T = proj.shape[0]
    nb = T // CHUNK
    qn = lambda n: jnp.minimum(n, nb - 1)

    def body(q_ref, kvc_ref, kvp_ref, o_ref, dnb_ref, gb_ref, sink_ref, bias_ref,
             dq_ref, dkv_ref, dgb_ref, dsink_ref, dbias_ref, carry_ref, sacc_ref):
        n = pl.program_id(0)

        @pl.when(n == 0)
        def _():
            carry_ref[...] = jnp.zeros_like(carry_ref)
            sacc_ref[...] = jnp.zeros_like(sacc_ref)
            dgb_ref[...] = jnp.zeros_like(dgb_ref)
            dbias_ref[...] = jnp.zeros_like(dbias_ref)

        @pl.when(n < nb)
        def _():
            mask = _band_mask(n)
            o_all = o_ref[...]
            dnb = dnb_ref[...]
            do_all, rb = _rms_bwd(dnb, o_all, gb_ref[...])
            dgb_ref[...] += jnp.sum(dnb * (o_all * rb), axis=0, keepdims=True)
            kops, vops = _band_operands(kvp_ref[...], kvc_ref[...])
            low = _low_lanes()
            halves = []
            for g in range(B_HEADS // Q_PER_KV):
                qst = _stack_pairs(q_ref, g).astype(BF16)
                dost = _stack_pairs(do_all, g).astype(BF16)
                dq_st = jnp.zeros((PAIRS * CHUNK, 2 * HEAD_DIM), F32)
                dk_e, dv_e = [], []
                for e in range(2):
                    s_all = _dot_nt(qst, kops[g][e])
                    dp_all = _dot_nt(dost, vops[g][e])
                    ps, dsrs = [], []
                    for pr in range(PAIRS):
                        h = _head(g, pr, e)
                        rows = slice(pr * CHUNK, (pr + 1) * CHUNK)
                        p, p_sink = _softmax_scores(s_all[rows], bias_ref[h], mask, sink_ref[0, h])
                        dp = dp_all[rows]
                        delta = jnp.sum(p * dp, axis=-1, keepdims=True)
                        ds = p * (dp - delta)
                        sacc_ref[:, h:h + 1] += -(p_sink * delta)
                        dbias_ref[h] += ds
                        ps.append(p.astype(BF16))
                        dsrs.append((ds * SCALE).astype(BF16))
                    dsr_all = jnp.concatenate(dsrs, axis=0)
                    dq_st = dq_st + _dot(dsr_all, kops[g][e])
                    dk_e.append(_dot_tn(dsr_all, qst))
                    dv_e.append(_dot_tn(jnp.concatenate(ps, axis=0), dost))
                for pr in range(PAIRS):
                    c0 = (g * PAIRS + pr) * 2 * HEAD_DIM
                    dq_ref[:, c0:c0 + 2 * HEAD_DIM] = dq_st[pr * CHUNK:(pr + 1) * CHUNK].astype(BF16)
                halves.append((dk_e, dv_e))
            tiles = []
            for t in range(2):
                g0, g1 = halves[0][t], halves[1][t]
                tiles.append(jnp.where(low, g0[0] + pltpu.roll(g0[1], HEAD_DIM, 1), pltpu.roll(g1[0], HEAD_DIM, 1) + g1[1]))
            dband = jnp.concatenate(tiles, axis=1)
            dkv_ref[...] = (carry_ref[...] + dband[:CHUNK]).astype(BF16)
            carry_ref[...] = dband[CHUNK:]

        @pl.when(n == nb)
        def _():
            dkv_ref[...] = carry_ref[...].astype(BF16)
            dsink_ref[...] = jnp.sum(sacc_ref[...], axis=0, keepdims=True)

    full = lambda *shape: pl.BlockSpec(shape, lambda n: (0,) * len(shape))
    return _call(
        body, name="attn_bwd", grid=(nb + 1,),
        in_specs=[pl.BlockSpec((CHUNK, B_WIDTH), lambda n: (qn(n), 2)),
                  pl.BlockSpec((CHUNK, 2 * KV_WIDTH), lambda n: (qn(n), 12)),
                  pl.BlockSpec((CHUNK, 2 * KV_WIDTH), lambda n: (jnp.maximum(qn(n) - 1, 0), 12)),
                  pl.BlockSpec((CHUNK, B_WIDTH), lambda n: (qn(n), 1)),
                  pl.BlockSpec((CHUNK, B_WIDTH), lambda n: (qn(n), 1)),
                  full(1, B_WIDTH), pl.BlockSpec(memory_space=pltpu.SMEM), full(B_HEADS, CHUNK, 2 * CHUNK)],
        out_specs=[pl.BlockSpec((CHUNK, B_WIDTH), lambda n: (qn(n), 0)),
                   pl.BlockSpec((CHUNK, 2 * KV_WIDTH), lambda n: (jnp.maximum(n - 1, 0), 0)),
                   full(1, B_WIDTH), full(1, B_HEADS), full(B_HEADS, CHUNK, 2 * CHUNK)],
        out_shape=[jax.ShapeDtypeStruct((T, B_WIDTH), BF16), jax.ShapeDtypeStruct((T, 2 * KV_WIDTH), BF16),
                   jax.ShapeDtypeStruct((1, B_WIDTH), F32), jax.ShapeDtypeStruct((1, B_HEADS), F32),
                   jax.ShapeDtypeStruct((B_HEADS, CHUNK, 2 * CHUNK), F32)],
        scratch_shapes=[pltpu.VMEM((CHUNK, 2 * KV_WIDTH), F32), pltpu.VMEM((CHUNK, B_HEADS), F32)],
        sem=("arbitrary",), rides=rides,
    )(proj, proj, proj, ab, dmixed, gb, sinks, bias)


def _sq_relu_grad(acc, z):
    return acc * (2.0 * jnp.maximum(z, 0.0))


def _local_step(x, tgt, sp, win, wo, wu, wd):
    T = x.shape[0]
    tm = min(512, T)
    tk = min(512, T)
    lg = sp["gate_norm_g"].reshape(A_GROUPS, CHUNK)
    lb = sp["gate_norm_b"].reshape(A_GROUPS, CHUNK)
    wsp = sp["w_spatial"].reshape(A_GROUPS, CHUNK, CHUNK)
    bs_col = sp["b_spatial"].reshape(A_GROUPS, CHUNK, 1)
    sinks = sp["attn_sinks"].reshape(1, B_HEADS)
    ga = sp["out_norm_a_g"].reshape(1, A_WIDTH)
    gb = sp["out_norm_b_g"].reshape(1, B_WIDTH)
    g1 = sp["mix_norm_g"].reshape(1, D_MODEL)
    g2 = sp["ffn_norm_g"].reshape(1, D_MODEL)
    gf = sp["final_norm_g"].reshape(1, D_MODEL)

    bias = _bias_build(sp["rel_bias_table"])
    n1, proj = _norm_matmul(x, g1, win, tm=tm, tn=PROJ_WIDTH // 2, name="in_proj")
    mixed, ab = _mixer_fwd(proj, lg, lb, wsp, bs_col, sinks, bias, ga, gb)
    h1 = _matmul_res(mixed, wo, x, tm=tm, tn=1024, tk=D_MODEL, prologue=_to_bf16, name="out_proj")
    n2, zp = _norm_matmul(h1, g2, wu, tm=tm, tn=1024, name="up_proj")
    h2 = _matmul_res(zp, wd, h1, tm=tm, tn=1024, tk=2048, prologue=_sq_relu_bf16, name="down_proj")

    dh2, dgf, loss = _loss_bwd(h2, tgt, gf, tm=tm)
    dzp = _matmul_nt(dh2, wd, tm=tm, tn=1024, tk=D_MODEL, name="bwd_dz", extra=zp, epilogue=_sq_relu_grad,
                     out_dtype=BF16)
    dwd = _matmul_tn(zp, dh2, tmo=1024, tn=1024, tk=tk, name="grad_w_down", a_prologue=_sq_relu_bf16)
    dwu = _matmul_tn(n2, dzp, tmo=1024, tn=1024, tk=tk, name="grad_w_up", shards=N_CHIPS)
    dn2 = _matmul_nt(dzp, wu, tm=tm, tn=1024, tk=2048, name="bwd_dn2")
    dh1, dg2 = _rms_bwd_res(dn2, h1, g2, dh2, tm=tm, name="ffn_norm_bwd")
    dwo = _matmul_tn(mixed, dh1, tmo=1024, tn=1024, tk=tk, name="grad_w_out")
    dmixed = _matmul_nt(dh1, wo, tm=tm, tn=1024, tk=D_MODEL, name="bwd_dmixed")
    duv, dga, dwsp, dbs, dlg, dlb = _gmlp_bwd(proj, ab, dmixed, ga, lg, lb, wsp, bs_col)
    dq, dkv, dgb, dsinks, dbias = _attn_bwd(proj, ab, dmixed, gb, sinks, bias)
    dtable = _bias_grad(dbias)
    dproj = jnp.concatenate([duv, dq, dkv], axis=1)
    dwin = _matmul_tn(n1, dproj, tmo=1024, tn=PROJ_WIDTH // 2, tk=tk, name="grad_w_in")
    dn1 = _matmul_nt(dproj, win, tm=tm, tn=1024, tk=PROJ_WIDTH, name="bwd_dn1")
    dx, dg1 = _rms_bwd_res(dn1, x, g1, dh1, tm=tm, name="mix_norm_bwd")

    small = {
        "rel_bias_table": dtable.reshape(N_BUCKETS, B_HEADS), "mix_norm_g": dg1, "gate_norm_g": dlg, "gate_norm_b": dlb,
        "w_spatial": dwsp, "b_spatial": dbs, "attn_sinks": dsinks, "out_norm_a_g": dga, "out_norm_b_g": dgb,
        "ffn_norm_g": dg2, "final_norm_g": dgf,
    }
    return loss, dx, (dwin, dwo, dwu, dwd), small


def _place():
    x, y, c = lax.axis_index("x"), lax.axis_index("y"), lax.axis_index("c")
    chips = [(1 - x, y), (x, 1 - y), (1 - x, 1 - y)]
    return x, y, c, chips


def _remote(src, dst, send_sem, recv_sem, to):
    return pltpu.make_async_remote_copy(src_ref=src, dst_ref=dst, send_sem=send_sem, recv_sem=recv_sem,
                                        device_id=to, device_id_type=MESH)


def _core_index():
    return lax.axis_index("c").astype(jnp.int32).reshape(1)


def _chip_index():
    return (2 * lax.axis_index("x") + lax.axis_index("y")).astype(jnp.int32).reshape(1)


def _cast_into_slot(w, *, tm, name):
    R, C = w.shape

    def body(me_ref, w_ref, o_ref):
        del me_ref
        o_ref[...] = w_ref[...].astype(BF16)

    return pl.pallas_call(
        body, name=name,
        grid_spec=pltpu.PrefetchScalarGridSpec(
            num_scalar_prefetch=1, grid=(R // tm,),
            in_specs=[pl.BlockSpec((tm, C), lambda i, me: (i, 0))],
            out_specs=pl.BlockSpec((None, tm, C), lambda i, me: (me[0], i, 0))),
        out_shape=jax.ShapeDtypeStruct((N_CHIPS, R, C), BF16), compiler_params=_params(("parallel",)),
    )(_chip_index(), w)


def _gather_weights(slots):
    nw = len(slots)

    def body(*refs):
        fulls = refs[nw:2 * nw]
        send_sems, recv_sems = refs[2 * nw:]
        x, y, c, chips = _place()
        me = 2 * x + y
        sends = []
        for w in range(nw):
            hr = fulls[w].shape[1] // 2
            rows = pl.ds(c * hr, hr)
            for j, chip in enumerate(chips):
                mine = fulls[w].at[me, rows, :]
                cp = _remote(mine, mine, send_sems.at[6 * w + j], recv_sems.at[6 * w + j], (*chip, c))
                cp.start()
                sends.append(cp)
        for w in range(nw):
            hr = fulls[w].shape[1] // 2
            rows = pl.ds(c * hr, hr)
            for j, chip in enumerate(chips):
                landed = fulls[w].at[2 * chip[0] + chip[1], rows, :]
                _remote(landed, landed, send_sems.at[6 * w + j], recv_sems.at[6 * w + j], (x, y, c)).wait_recv()
                cp = _remote(landed, landed, send_sems.at[6 * w + 3 + j], recv_sems.at[6 * w + 3 + j], (x, y, 1 - c))
                cp.start()
                sends.append(cp)
        for w in range(nw):
            hr = fulls[w].shape[1] // 2
            rows = pl.ds((1 - c) * hr, hr)
            for j, chip in enumerate(chips):
                other = fulls[w].at[2 * chip[0] + chip[1], rows, :]
                _remote(other, other, send_sems.at[6 * w + 3 + j], recv_sems.at[6 * w + 3 + j], (x, y, c)).wait_recv()
        for cp in sends:
            cp.wait_send()

    any_spec = pl.BlockSpec(memory_space=pl.ANY)
    return pl.pallas_call(
        body, name="gather_weights",
        in_specs=[any_spec] * nw, out_specs=[any_spec] * nw,
        out_shape=[jax.ShapeDtypeStruct(s.shape, s.dtype) for s in slots],
        scratch_shapes=[pltpu.SemaphoreType.DMA((6 * nw,)), pltpu.SemaphoreType.DMA((6 * nw,))],
        input_output_aliases={w: w for w in range(nw)},
    )(*slots)


def _sibling_halves(grads):
    nw = len(grads)

    def body(*refs):
        gs, outs = refs[:nw], refs[nw:2 * nw]
        send_sems, recv_sems = refs[2 * nw:]
        x, y, c, _ = _place()
        cps = []
        for w in range(nw):
            hr = gs[w].shape[1] // 2
            cp = _remote(gs[w].at[:, pl.ds((1 - c) * hr, hr), :], outs[w], send_sems.at[w], recv_sems.at[w],
                         (x, y, 1 - c))
            cp.start()
            cps.append(cp)
        for cp in cps:
            cp.wait()

    any_spec = pl.BlockSpec(memory_space=pl.ANY)
    return pl.pallas_call(
        body, name="rs_sibling_halves",
        in_specs=[any_spec] * nw, out_specs=[any_spec] * nw,
        out_shape=[jax.ShapeDtypeStruct((g.shape[0], g.shape[1] // 2, g.shape[2]), g.dtype) for g in grads],
        scratch_shapes=[pltpu.SemaphoreType.DMA((nw,)), pltpu.SemaphoreType.DMA((nw,))],
    )(*grads)


def _pair_sum_bf16(g, got, *, tm, name):
    S, R, C = g.shape
    hr = R // 2
    nt = hr // tm

    def body(c_ref, g_ref, got_ref, o_ref):
        del c_ref
        o_ref[...] = (g_ref[...] + got_ref[...]).astype(BF16)

    return pl.pallas_call(
        body, name=name,
        grid_spec=pltpu.PrefetchScalarGridSpec(
            num_scalar_prefetch=1, grid=(S, nt),
            in_specs=[pl.BlockSpec((None, tm, C), lambda s, i, c: (s, c[0] * nt + i, 0)),
                      pl.BlockSpec((None, tm, C), lambda s, i, c: (s, i, 0))],
            out_specs=pl.BlockSpec((None, tm, C), lambda s, i, c: (s, i, 0))),
        out_shape=jax.ShapeDtypeStruct((S, hr, C), BF16),
        compiler_params=_params(("parallel", "parallel")),
    )(_core_index(), g, got)


def _scatter_to_owners(pairs):
    nw = len(pairs)

    def body(*refs):
        qs, outs = refs[:nw], refs[nw:2 * nw]
        send_sems, recv_sems = refs[2 * nw:]
        x, y, c, chips = _place()
        cps = []
        for w in range(nw):
            for j, chip in enumerate(chips):
                cp = _remote(qs[w].at[2 * chip[0] + chip[1]], outs[w].at[j], send_sems.at[3 * w + j],
                             recv_sems.at[3 * w + j], (*chip, c))
                cp.start()
                cps.append(cp)
        for cp in cps:
            cp.wait()

    any_spec = pl.BlockSpec(memory_space=pl.ANY)
    return pl.pallas_call(
        body, name="rs_scatter_to_owners",
        in_specs=[any_spec] * nw, out_specs=[any_spec] * nw,
        out_shape=[jax.ShapeDtypeStruct((3,) + q.shape[1:], q.dtype) for q in pairs],
        scratch_shapes=[pltpu.SemaphoreType.DMA((3 * nw,)), pltpu.SemaphoreType.DMA((3 * nw,))],
    )(*pairs)


def _owner_sum(g, got, others, *, tm, name):
    S, R, C = g.shape
    hr = R // 2
    nt = hr // tm

    def body(idx_ref, g_ref, got_ref, o_ref_in, out_ref):
        del idx_ref
        acc = g_ref[...] + got_ref[...]
        for j in range(3):
            acc = acc + o_ref_in[j].astype(F32)
        out_ref[...] = acc

    return pl.pallas_call(
        body, name=name,
        grid_spec=pltpu.PrefetchScalarGridSpec(
            num_scalar_prefetch=1, grid=(nt,),
            in_specs=[pl.BlockSpec((None, tm, C), lambda i, p: (p[1], p[0] * nt + i, 0)),
                      pl.BlockSpec((None, tm, C), lambda i, p: (p[1], i, 0)),
                      pl.BlockSpec((3, tm, C), lambda i, p: (0, i, 0))],
            out_specs=pl.BlockSpec((tm, C), lambda i, p: (i, 0))),
        out_shape=jax.ShapeDtypeStruct((hr, C), F32),
        compiler_params=_params(("parallel",)),
    )(jnp.concatenate([_core_index(), _chip_index()]), g, got, others)


def _swap_halves(halves):
    nw = len(halves)

    def body(*refs):
        hs, outs = refs[:nw], refs[nw:2 * nw]
        send_sems, recv_sems = refs[2 * nw:]
        x, y, c, _ = _place()
        cps = []
        for w in range(nw):
            cp = _remote(hs[w], outs[w], send_sems.at[w], recv_sems.at[w], (x, y, 1 - c))
            cp.start()
            cps.append(cp)
        for cp in cps:
            cp.wait()

    any_spec = pl.BlockSpec(memory_space=pl.ANY)
    return pl.pallas_call(
        body, name="rs_swap_halves",
        in_specs=[any_spec] * nw, out_specs=[any_spec] * nw,
        out_shape=[jax.ShapeDtypeStruct(h.shape, h.dtype) for h in halves],
        scratch_shapes=[pltpu.SemaphoreType.DMA((nw,)), pltpu.SemaphoreType.DMA((nw,))],
    )(*halves)


def _all_reduce_small(packed):
    R, C = packed.shape

    def body(in_ref, out_ref, slots, send_sems, recv_sems):
        x, y, c, _ = _place()
        me = 4 * x + 2 * y + c
        cps = []
        for k in range(1, N_DEV):
            p = (me + k) % N_DEV
            cp = _remote(in_ref, slots.at[me], send_sems.at[k - 1], recv_sems.at[k - 1], (p // 4, (p // 2) % 2, p % 2))
            cp.start()
            cps.append(cp)
        slots[me] = in_ref[...]
        for k in range(1, N_DEV):
            src = (me + N_DEV - k) % N_DEV
            _remote(in_ref, slots.at[src], send_sems.at[k - 1], recv_sems.at[k - 1], (x, y, c)).wait_recv()
        for cp in cps:
            cp.wait_send()
        acc = slots[0]
        for d in range(1, N_DEV):
            acc = acc + slots[d]
        out_ref[...] = acc

    vmem = pl.BlockSpec(memory_space=pltpu.VMEM)
    return pl.pallas_call(
        body, name="all_reduce_small", in_specs=[vmem], out_specs=vmem,
        out_shape=jax.ShapeDtypeStruct((R, C), F32),
        scratch_shapes=[pltpu.VMEM((N_DEV, R, C), F32), pltpu.SemaphoreType.DMA((N_DEV - 1,)),
                        pltpu.SemaphoreType.DMA((N_DEV - 1,))],
        compiler_params=_params(),
    )(packed)


def _adamw_math(w, g, m, v):
    m = ADAM_B1 * m + (1.0 - ADAM_B1) * g
    v = ADAM_B2 * v + (1.0 - ADAM_B2) * (g * g)
    m_hat = m / (1.0 - ADAM_B1 ** ADAM_STEP)
    v_hat = v / (1.0 - ADAM_B2 ** ADAM_STEP)
    delta = -ADAM_LR * (m_hat / (jnp.sqrt(v_hat) + ADAM_EPS) + ADAM_WD * w)
    return delta, m, v


def _adamw(w, g, m, v, *, tm, name):
    R, C = w.shape

    def body(w_ref, g_ref, m_ref, v_ref, d_ref, nm_ref, nv_ref):
        d_ref[...], nm_ref[...], nv_ref[...] = _adamw_math(w_ref[...], g_ref[...], m_ref[...], v_ref[...])

    spec = pl.BlockSpec((tm, C), lambda i: (i, 0))
    return pl.pallas_call(
        body, name=name, grid=(R // tm,), in_specs=[spec] * 4, out_specs=[spec] * 3,
        out_shape=[jax.ShapeDtypeStruct((R, C), F32)] * 3, compiler_params=_params(("parallel",)),
    )(w, g, m, v)


def _adamw_halves(w, own, got, m, v, *, tm, name):
    R, C = w.shape
    nt = (R // 2) // tm

    def body(c_ref, w_ref, own_ref, got_ref, m_ref, v_ref, g_ref, d_ref, nm_ref, nv_ref):
        g = jnp.where(pl.program_id(0) == c_ref[0], own_ref[...], got_ref[...])
        g_ref[...] = g
        d_ref[...], nm_ref[...], nv_ref[...] = _adamw_math(w_ref[...], g, m_ref[...], v_ref[...])

    whole = pl.BlockSpec((tm, C), lambda h, i, c: (h * nt + i, 0))
    half = pl.BlockSpec((tm, C), lambda h, i, c: (i, 0))
    return pl.pallas_call(
        body, name=name,
        grid_spec=pltpu.PrefetchScalarGridSpec(
            num_scalar_prefetch=1, grid=(2, nt),
            in_specs=[whole, half, half, whole, whole], out_specs=[whole] * 4),
        out_shape=[jax.ShapeDtypeStruct((R, C), F32)] * 4, compiler_params=_params(("parallel", "parallel")),
    )(_core_index(), w, own, got, m, v)


SMALL = ["rel_bias_table", "mix_norm_g", "gate_norm_g", "gate_norm_b", "w_spatial", "b_spatial", "attn_sinks",
         "out_norm_a_g", "out_norm_b_g", "ffn_norm_g", "final_norm_g"]
LARGE = ["w_in", "w_out", "w_up", "w_down"]
WEIGHTS = ["rel_bias_table", "mix_norm_g", "w_in", "gate_norm_g", "gate_norm_b", "w_spatial", "b_spatial", "attn_sinks",
           "out_norm_a_g", "out_norm_b_g", "w_out", "ffn_norm_g", "w_up", "w_down", "final_norm_g"]
PACK_UNIT = 8 * 128


def _pack(parts):
    rows = []
    for p in parts:
        flat = p.reshape(-1)
        pad = (-flat.shape[0]) % PACK_UNIT
        rows.append(jnp.pad(flat, (0, pad)).reshape(-1, 128))
    return jnp.concatenate(rows, axis=0)


def _unpack(packed, like):
    out, row = [], 0
    for p in like:
        n = math.prod(p.shape)
        nrows = (n + PACK_UNIT - 1) // PACK_UNIT * 8
        out.append(packed[row:row + nrows].reshape(-1)[:n].reshape(p.shape))
        row += nrows
    return out


def kernel(x, rel_bias_table, mix_norm_g, w_in, gate_norm_g, gate_norm_b, w_spatial, b_spatial, attn_sinks, out_norm_a_g, out_norm_b_g, w_out, ffn_norm_g, w_up, w_down, final_norm_g, loss_target, m_rel_bias_table, m_mix_norm_g, m_w_in, m_gate_norm_g, m_gate_norm_b, m_w_spatial, m_b_spatial, m_attn_sinks, m_out_norm_a_g, m_out_norm_b_g, m_w_out, m_ffn_norm_g, m_w_up, m_w_down, m_final_norm_g, v_rel_bias_table, v_mix_norm_g, v_w_in, v_gate_norm_g, v_gate_norm_b, v_w_spatial, v_b_spatial, v_attn_sinks, v_out_norm_a_g, v_out_norm_b_g, v_w_out, v_ffn_norm_g, v_w_up, v_w_down, v_final_norm_g):
    args = dict(locals())
    wts = {n: args[n] for n in WEIGHTS}
    mom = {n: args["m_" + n] for n in WEIGHTS}
    var = {n: args["v_" + n] for n in WEIGHTS}
    sp = {n: wts[n] for n in SMALL}
    x2, tgt = x[0], loss_target[0]
    T = x2.shape[0]
    tm = min(512, T)
    tl = min(1024, T)
    lg = sp["gate_norm_g"].reshape(A_GROUPS, CHUNK)
    lb = sp["gate_norm_b"].reshape(A_GROUPS, CHUNK)
    wsp = sp["w_spatial"].reshape(A_GROUPS, CHUNK, CHUNK)
    bs_col = sp["b_spatial"].reshape(A_GROUPS, CHUNK, 1)
    sinks = sp["attn_sinks"].reshape(1, B_HEADS)
    ga = sp["out_norm_a_g"].reshape(1, A_WIDTH)
    gb = sp["out_norm_b_g"].reshape(1, B_WIDTH)
    g1 = sp["mix_norm_g"].reshape(1, D_MODEL)
    g2 = sp["ffn_norm_g"].reshape(1, D_MODEL)
    gf = sp["final_norm_g"].reshape(1, D_MODEL)

    def pair_sum(n, g, got):
        return _pair_sum_bf16(g, got, tm=256, name="rs_pair_sum_" + n)

    def owner_sum(n, g, got, others):
        return _owner_sum(g, got, others, tm=256, name="rs_owner_sum_" + n)

    own = [wts[n].reshape(wts[n].shape[1:]) for n in LARGE]
    s_in, s_out, s_up, s_down = [_cast_into_slot(w, tm=256, name="cast_" + n) for n, w in zip(LARGE, own)]
    ((g_in,),) = _carrier([_ride_gather(s_in)], name="gather_w_in")
    win = g_in.transpose(1, 0, 2).reshape(D_MODEL, PROJ_WIDTH)
    bias = _bias_build(sp["rel_bias_table"])
    (n1t, proj), ((g_out,),) = _norm_matmul_wide(x2, g1, win, tm=tm, tn=PROJ_WIDTH // 2, name="in_proj",
                                                 rides=[_ride_gather(s_out)])
    wo = g_out.reshape(A_WIDTH + B_WIDTH, D_MODEL)
    (mixed, mixed_t, ab), ((wu,),) = _mixer_fwd(proj, lg, lb, wsp, bs_col, sinks, bias, ga, gb,
                                                rides=[_ride_gather(s_up)])
    h1 = _matmul_res(mixed, wo, x2, tm=tl, tn=1024, tk=D_MODEL, prologue=_to_bf16, name="out_proj")
    (n2t, zp, z2, z2t), ((g_down,),) = _norm_matmul_sq(h1, g2, wu, tm=tl, tn=512, name="up_proj",
                                                       rides=[_ride_gather(s_down)])
    wd = g_down.reshape(D_FF, D_MODEL)
    h2 = _matmul_res(z2, wd, h1, tm=tl, tn=1024, tk=2048, prologue=_to_bf16, name="down_proj")

    dh2, dh2b, dgf, loss = _loss_bwd(h2, tgt, gf, tm=tm)
    dzp = _matmul_nt(dh2b, wd, tm=tl, tn=1024, tk=D_MODEL, name="bwd_dz", extra=zp, epilogue=_sq_relu_grad,
                     out_dtype=BF16)
    dwd = _matmul_nn(z2t, dh2b, tmo=1024, tn=2048, tk=tl, name="grad_w_down")
    dwd = dwd.reshape(N_CHIPS, D_FF // N_CHIPS, D_MODEL)
    dwu, ((r_d,),) = _matmul_nn(n2t, dzp, tmo=1024, tn=2048, tk=tl, name="grad_w_up", shards=N_CHIPS,
                                rides=[_ride_sibling_halves(dwd)])
    q_d = pair_sum("w_down", dwd, r_d)
    dn2, ((o_d,), (r_u,)) = _matmul_nt(dzp, wu, tm=tl, tn=1024, tk=2048, name="bwd_dn2",
                                       rides=[_ride_scatter(q_d), _ride_sibling_halves(dwu)])
    h_d = owner_sum("w_down", dwd, r_d, o_d)
    q_u = pair_sum("w_up", dwu, r_u)
    dh1, dh1b, dg2 = _rms_bwd_res(dn2, h1, g2, dh2, tm=tm, name="ffn_norm_bwd")
    dwo, ((w_d,),) = _matmul_nn(mixed_t, dh1b, tmo=1024, tn=2048, tk=tl, name="grad_w_out", rides=[_ride_swap(h_d)])
    dwo = dwo.reshape(N_CHIPS, (A_WIDTH + B_WIDTH) // N_CHIPS, D_MODEL)
    dmixed, ((r_o,),) = _matmul_nt(dh1b, wo, tm=tl, tn=1024, tk=D_MODEL, name="bwd_dmixed",
                                   rides=[_ride_sibling_halves(dwo)])
    q_o = pair_sum("w_out", dwo, r_o)
    (duv, dga, dwsp, dbs, dlg, dlb), ((o_o,),) = _gmlp_bwd(proj, ab, dmixed, ga, lg, lb, wsp, bs_col,
                                                           rides=[_ride_scatter(q_o)])
    (dq, dkv, dgb, dsinks, dbias), ((o_u,),) = _attn_bwd(proj, ab, dmixed, gb, sinks, bias, rides=[_ride_scatter(q_u)])
    h_o = owner_sum("w_out", dwo, r_o, o_o)
    h_u = owner_sum("w_up", dwu, r_u, o_u)
    dtable = _bias_grad(dbias)
    dproj = jnp.concatenate([duv, dq, dkv], axis=1)
    dwin, ((w_o,), (w_u,)) = _matmul_nn(n1t, dproj, tmo=1024, tn=PROJ_WIDTH // 2, tk=tl, name="grad_w_in",
                                        rides=[_ride_swap(h_o), _ride_swap(h_u)])
    dwin = dwin.reshape(D_MODEL, N_CHIPS, PROJ_WIDTH // N_CHIPS).transpose(1, 0, 2)
    dn1, ((r_i,),) = _matmul_nt(dproj, win, tm=tl, tn=1024, tk=PROJ_WIDTH, name="bwd_dn1",
                                rides=[_ride_sibling_halves(dwin)])
    q_i = pair_sum("w_in", dwin, r_i)
    (dx, _, dg1), ((o_i,),) = _rms_bwd_res(dn1, x2, g1, dh1, tm=tm, name="mix_norm_bwd", rides=[_ride_scatter(q_i)])
    h_i = owner_sum("w_in", dwin, r_i, o_i)
    ((w_i,),) = _carrier([_ride_swap(h_i)], name="swap_w_in")
    halves = [h_i, h_o, h_u, h_d]
    swapped = [w_i, w_o, w_u, w_d]
    small = {
        "rel_bias_table": dtable.reshape(N_BUCKETS, B_HEADS), "mix_norm_g": dg1, "gate_norm_g": dlg, "gate_norm_b": dlb,
        "w_spatial": dwsp, "b_spatial": dbs, "attn_sinks": dsinks, "out_norm_a_g": dga, "out_norm_b_g": dgb,
        "ffn_norm_g": dg2, "final_norm_g": dgf,
    }

    out_g, out_d, out_m, out_v = {}, {}, {}, {}
    for n, w, h, s in zip(LARGE, own, halves, swapped):
        shape = wts[n].shape
        g, d, nm, nv = _adamw_halves(w, h, s, mom[n].reshape(w.shape), var[n].reshape(w.shape), tm=256,
                                     name="adamw_" + n)
        out_g[n], out_d[n], out_m[n], out_v[n] = g.reshape(shape), d.reshape(shape), nm.reshape(shape), nv.reshape(shape)

    like = [wts[n] for n in SMALL]
    g_small = _all_reduce_small(_pack([small[n] for n in SMALL]))
    d_s, m_s, v_s = _adamw(_pack(like), g_small, _pack([mom[n] for n in SMALL]), _pack([var[n] for n in SMALL]),
                           tm=g_small.shape[0], name="adamw_small")
    for n, g, d, nm, nv in zip(SMALL, _unpack(g_small, like), _unpack(d_s, like), _unpack(m_s, like),
                               _unpack(v_s, like)):
        out_g[n], out_d[n], out_m[n], out_v[n] = g, d, nm, nv

    total = lax.psum(loss[0, 0], ("x", "y", "c"))
    return (total, dx[None], *[out_g[n] for n in WEIGHTS], *[out_d[n] for n in WEIGHTS],
            *[out_m[n] for n in WEIGHTS], *[out_v[n] for n in WEIGHTS])
```

```python
import functools
import math

import numpy as np
import jax
import jax.numpy as jnp
from jax import lax
from jax.experimental import pallas as pl
from jax.experimental.pallas import tpu as pltpu

F32 = jnp.float32
BF16 = jnp.bfloat16

D_MODEL = 2048
CHUNK = 128
A_GROUPS = 8
A_WIDTH = 1024
HEAD_DIM = 64
B_HEADS = 16
Q_PER_KV = 8
B_WIDTH = 1024
KV_WIDTH = 128
PROJ_WIDTH = 3328
D_FF = 8192
N_BUCKETS = 32
EPS = 1e-5
NEG = -1e30
SCALE = HEAD_DIM ** -0.5
N_CHIPS = 4
N_DEV = 8

ADAM_LR = 0.001
ADAM_B1 = 0.9
ADAM_B2 = 0.999
ADAM_EPS = 1e-08
ADAM_WD = 0.01
ADAM_STEP = 10

VMEM_LIMIT = 56 * 1024 * 1024
MESH = pl.DeviceIdType.MESH


def _bucket_thresholds():
    d = np.arange(CHUNK)
    n_exact = N_BUCKETS // 2
    relf = np.maximum(d, n_exact).astype(np.float64)
    large = n_exact + (np.log(relf / n_exact) / math.log(CHUNK / n_exact) * (N_BUCKETS - n_exact)).astype(np.int32)
    bucket = np.where(d < n_exact, d, np.minimum(large, N_BUCKETS - 1))
    return [int(np.min(d[bucket >= b])) for b in range(1, N_BUCKETS)]


BUCKET_THR = _bucket_thresholds()


def _params(sem=None):
    return pltpu.CompilerParams(dimension_semantics=sem, vmem_limit_bytes=VMEM_LIMIT)


def _gelu(x):
    c = math.sqrt(2.0 / math.pi)
    return 0.5 * x * (1.0 + jnp.tanh(c * (x + 0.044715 * (x * x * x))))


def _gelu_and_grad(x):
    c = math.sqrt(2.0 / math.pi)
    x2 = x * x
    t = jnp.tanh(c * (x + 0.044715 * (x2 * x)))
    g = 0.5 * x * (1.0 + t)
    dg = 0.5 * (1.0 + t) + 0.5 * x * (1.0 - t * t) * (c * (1.0 + 3.0 * 0.044715 * x2))
    return g, dg


def _dot(a, b):
    return jnp.dot(a, b, preferred_element_type=F32)


def _dot_nt(a, b):
    return lax.dot_general(a, b, (((1,), (1,)), ((), ())), preferred_element_type=F32)


def _dot_tn(a, b):
    return lax.dot_general(a, b, (((0,), (0,)), ((), ())), preferred_element_type=F32)


def _rms_bwd(dn, h, g):
    r = lax.rsqrt(jnp.mean(h * h, axis=-1, keepdims=True) + EPS)
    w = dn * g
    dh = r * w - h * ((r * r * r) * jnp.mean(w * h, axis=-1, keepdims=True))
    return dh, r


def _place():
    x, y, c = lax.axis_index("x"), lax.axis_index("y"), lax.axis_index("c")
    chips = [(1 - x, y), (x, 1 - y), (1 - x, 1 - y)]
    return x, y, c, chips


def _remote(src, dst, send_sem, recv_sem, to):
    return pltpu.make_async_remote_copy(src_ref=src, dst_ref=dst, send_sem=send_sem, recv_sem=recv_sem,
                                        device_id=to, device_id_type=MESH)


class _Ride:
    def __init__(self, args, out_shape, n_sem, start, finish, mid=None, mid_frac=0.8, aliases=None):
        self.args, self.out_shape, self.n_sem = list(args), list(out_shape), n_sem
        self.start, self.mid, self.finish, self.mid_frac = start, mid, finish, mid_frac
        self.aliases = dict(aliases or {})


def _call(body, *, name, grid, in_specs, out_specs, out_shape, scratch_shapes=(), sem=None, rides=()):
    single = not isinstance(out_shape, (list, tuple))
    out_specs = [out_specs] if single else list(out_specs)
    out_shape = [out_shape] if single else list(out_shape)
    n_in, n_out, n_scr = len(in_specs), len(out_shape), len(scratch_shapes)
    r_in = [len(r.args) for r in rides]
    r_out = [len(r.out_shape) for r in rides]
    any_spec = pl.BlockSpec(memory_space=pl.ANY)
    aliases, off_i, off_o = {}, n_in, n_out
    for r in rides:
        for i, o in r.aliases.items():
            aliases[off_i + i] = off_o + o
        off_i += len(r.args)
        off_o += len(r.out_shape)
    steps = math.prod(grid)

    def wrapped(*refs):
        p = 0
        ins = refs[p:p + n_in]; p += n_in
        rins = refs[p:p + sum(r_in)]; p += sum(r_in)
        outs = refs[p:p + n_out]; p += n_out
        routs = refs[p:p + sum(r_out)]; p += sum(r_out)
        scr = refs[p:p + n_scr]; p += n_scr
        sems = refs[p:]
        parts, pi, po = [], 0, 0
        for k, r in enumerate(rides):
            parts.append((rins[pi:pi + r_in[k]], routs[po:po + r_out[k]], sems[2 * k], sems[2 * k + 1]))
            pi += r_in[k]
            po += r_out[k]
        lin = 0
        for d in range(len(grid)):
            lin = lin * grid[d] + pl.program_id(d)
        if rides:
            @pl.when(lin == 0)
            def _():
                for r, part in zip(rides, parts):
                    r.start(*part)
        body(*ins, *outs, *scr)
        for r, part in zip(rides, parts):
            if r.mid is not None:
                @pl.when(lin == min(steps - 1, int(r.mid_frac * steps)))
                def _(r=r, part=part):
                    r.mid(*part)
        if rides:
            @pl.when(lin == steps - 1)
            def _():
                for r, part in zip(rides, parts):
                    r.finish(*part)

    scratch = list(scratch_shapes)
    for r in rides:
        scratch += [pltpu.SemaphoreType.DMA((r.n_sem,)), pltpu.SemaphoreType.DMA((r.n_sem,))]
    if rides:
        sem = ("arbitrary",) * len(grid)
    res = pl.pallas_call(
        wrapped, name=name, grid=grid,
        in_specs=list(in_specs) + [any_spec] * sum(r_in),
        out_specs=out_specs + [any_spec] * sum(r_out),
        out_shape=out_shape + [s for r in rides for s in r.out_shape],
        scratch_shapes=scratch, input_output_aliases=aliases,
        compiler_params=_params(sem),
    )

    def run(*args):
        got = res(*args, *[a for r in rides for a in r.args])
        mine = got[0] if single else list(got[:n_out])
        if not rides:
            return mine
        rest, out = list(got[n_out:]), []
        for k in range(len(rides)):
            out.append(rest[:r_out[k]])
            rest = rest[r_out[k]:]
        return mine, out

    return run


def _ride_gather(slot, part=(0, 1), mid_frac=0.8):
    k0, k1, n = part if len(part) == 3 else (part[0], part[0] + 1, part[1])
    rows_n = (k1 - k0) * (slot.shape[1] // 2 // n)
    off = lambda c: c * (slot.shape[1] // 2) + k0 * (slot.shape[1] // 2 // n)

    def start(ins, outs, ss, rs):
        x, y, c, chips = _place()
        mine = outs[0].at[2 * x + y, pl.ds(off(c), rows_n), :]
        for j, chip in enumerate(chips):
            _remote(mine, mine, ss.at[j], rs.at[j], (*chip, c)).start()

    def mid(ins, outs, ss, rs):
        x, y, c, chips = _place()
        for j, chip in enumerate(chips):
            landed = outs[0].at[2 * chip[0] + chip[1], pl.ds(off(c), rows_n), :]
            _remote(landed, landed, ss.at[j], rs.at[j], (x, y, c)).wait_recv()
            _remote(landed, landed, ss.at[3 + j], rs.at[3 + j], (x, y, 1 - c)).start()

    def finish(ins, outs, ss, rs):
        x, y, c, chips = _place()
        for j, chip in enumerate(chips):
            other = outs[0].at[2 * chip[0] + chip[1], pl.ds(off(1 - c), rows_n), :]
            _remote(other, other, ss.at[3 + j], rs.at[3 + j], (x, y, c)).wait_recv()
        for j in range(6):
            piece = outs[0].at[0, pl.ds(0, rows_n), :]
            _remote(piece, piece, ss.at[j], rs.at[j], (x, y, c)).wait_send()

    return _Ride([slot], [jax.ShapeDtypeStruct(slot.shape, slot.dtype)], 6, start, finish, mid=mid,
                 mid_frac=mid_frac, aliases={0: 0})


def _ride_sibling_halves(g):
    S, R, C = g.shape
    hr = R // 2

    def copy(ins, outs, ss, rs):
        x, y, c, _ = _place()
        return _remote(ins[0].at[:, pl.ds((1 - c) * hr, hr), :], outs[0], ss.at[0], rs.at[0], (x, y, 1 - c))

    return _Ride([g], [jax.ShapeDtypeStruct((S, hr, C), g.dtype)], 1,
                 lambda *a: copy(*a).start(), lambda *a: copy(*a).wait())


def _ride_scatter(q, land=None, part=(0, 1)):
    k0, k1, n = part if len(part) == 3 else (part[0], part[0] + 1, part[1])
    rows_n = q.shape[1] // n
    rows = pl.ds(k0 * rows_n, (k1 - k0) * rows_n)

    def copies(ins, outs, ss, rs):
        x, y, c, chips = _place()
        return [_remote(ins[0].at[2 * chip[0] + chip[1], rows, :], outs[0].at[j, rows, :], ss.at[j], rs.at[j], (*chip, c))
                for j, chip in enumerate(chips)]

    def start(*a):
        for cp in copies(*a):
            cp.start()

    def finish(*a):
        for cp in copies(*a):
            cp.wait()

    shape = jax.ShapeDtypeStruct((3,) + q.shape[1:], q.dtype)
    if land is None:
        return _Ride([q], [shape], 3, start, finish)
    return _Ride([q, land], [shape], 3, start, finish, aliases={1: 0})


def _ride_swap(h):
    def copy(ins, outs, ss, rs):
        x, y, c, _ = _place()
        return _remote(ins[0], outs[0], ss.at[0], rs.at[0], (x, y, 1 - c))

    return _Ride([h], [jax.ShapeDtypeStruct(h.shape, h.dtype)], 1,
                 lambda *a: copy(*a).start(), lambda *a: copy(*a).wait())


def _mesh_place(p):
    return (p // 4, (p // 2) % 2, p % 2)


def _ride_small_to_all(packed):
    def copies(ins, outs, ss, rs):
        x, y, c, _ = _place()
        me = 4 * x + 2 * y + c
        return [_remote(ins[0], outs[0].at[me], ss.at[k - 1], rs.at[k - 1], _mesh_place((me + k) % N_DEV))
                for k in range(1, N_DEV)]

    def own(ins, outs, ss, rs):
        x, y, c, _ = _place()
        return pltpu.make_async_copy(ins[0], outs[0].at[4 * x + 2 * y + c], ss.at[N_DEV - 1])

    def start(*a):
        own(*a).start()
        for cp in copies(*a):
            cp.start()

    def finish(ins, outs, ss, rs):
        x, y, c, _ = _place()
        me = 4 * x + 2 * y + c
        for k in range(1, N_DEV):
            _remote(ins[0], outs[0].at[(me + N_DEV - k) % N_DEV], ss.at[k - 1], rs.at[k - 1], (x, y, c)).wait_recv()
        for cp in copies(ins, outs, ss, rs):
            cp.wait_send()
        own(ins, outs, ss, rs).wait()

    return _Ride([packed], [jax.ShapeDtypeStruct((N_DEV,) + packed.shape, packed.dtype)], N_DEV, start, finish)


def _carrier(rides, *, name):
    _, outs = _call(lambda: None, name=name, grid=(1,), in_specs=[], out_specs=[], out_shape=[], rides=rides)()
    return outs


def _sq_relu_bf16(z):
    z = jnp.maximum(z, 0.0)
    return (z * z).astype(BF16)


def _norm_bf16(a_ref, g_ref):
    xf = a_ref[...]
    r = lax.rsqrt(jnp.mean(xf * xf, axis=-1, keepdims=True) + EPS)
    return ((xf * r) * g_ref[...]).astype(BF16)


def _norm_matmul_wide(a, g, b, *, tm, tn, name, rides=()):
    T, K = a.shape
    N = b.shape[1]

    def body(a_ref, g_ref, b_ref, nt_ref, o_ref):
        n = _norm_bf16(a_ref, g_ref)
        nt_ref[...] = n.T
        o_ref[...] = _dot(n, b_ref[...])

    return _call(
        body, name=name, grid=(N // tn, T // tm),
        in_specs=[pl.BlockSpec((tm, K), lambda j, i: (i, 0)), pl.BlockSpec((1, K), lambda j, i: (0, 0)),
                  pl.BlockSpec((K, tn), lambda j, i: (0, j))],
        out_specs=[pl.BlockSpec((None, K, tm), lambda j, i: (j, 0, i)), pl.BlockSpec((tm, tn), lambda j, i: (i, j))],
        out_shape=[jax.ShapeDtypeStruct((N // tn, K, T), BF16), jax.ShapeDtypeStruct((T, N), F32)],
        sem=("arbitrary", "arbitrary"), rides=rides,
    )(a, g, b)


def _norm_matmul_sq(a, g, b, *, tm, tn, name, rides=()):
    T, K = a.shape
    per = b.shape[2] // tn
    N = b.shape[0] * b.shape[2]

    def body(a_ref, g_ref, b_ref, nt_ref, o_ref, z_ref, zt_ref, n_scr):
        @pl.when(pl.program_id(1) == 0)
        def _():
            n = _norm_bf16(a_ref, g_ref)
            n_scr[...] = n
            nt_ref[...] = n.T
        p = _dot(n_scr[...], b_ref[...])
        o_ref[...] = p
        z = _sq_relu_bf16(p)
        z_ref[...] = z
        zt_ref[...] = z.T

    return _call(
        body, name=name, grid=(T // tm, N // tn),
        in_specs=[pl.BlockSpec((tm, K), lambda i, j: (i, 0)), pl.BlockSpec((1, K), lambda i, j: (0, 0)),
                  pl.BlockSpec((None, K, tn), lambda i, j: (j // per, 0, j % per))],
        out_specs=[pl.BlockSpec((K, tm), lambda i, j: (0, i)), pl.BlockSpec((tm, tn), lambda i, j: (i, j)),
                   pl.BlockSpec((tm, tn), lambda i, j: (i, j)), pl.BlockSpec((tn, tm), lambda i, j: (j, i))],
        out_shape=[jax.ShapeDtypeStruct((K, T), BF16), jax.ShapeDtypeStruct((T, N), F32),
                   jax.ShapeDtypeStruct((T, N), BF16), jax.ShapeDtypeStruct((N, T), BF16)],
        scratch_shapes=[pltpu.VMEM((tm, K), BF16)],
        sem=("parallel", "arbitrary"), rides=rides,
    )(a, g, b)


def _matmul_nn(at, b, *, tmo, tn, tk, name, shards=1, rides=()):
    M, T = at.shape[-2:]
    N = b.shape[1]
    if at.ndim == 3:
        a_spec = pl.BlockSpec((None, tmo, tk), lambda i, j, k: (0, i, k))
    else:
        a_spec = pl.BlockSpec((tmo, tk), lambda i, j, k: (i, k))
    if shards > 1:
        per = (N // shards) // tn
        out_spec = pl.BlockSpec((None, tmo, tn), lambda i, j, k: (j // per, i, j % per))
        out_shape = jax.ShapeDtypeStruct((shards, M, N // shards), F32)
    else:
        out_spec = pl.BlockSpec((tmo, tn), lambda i, j, k: (i, j))
        out_shape = jax.ShapeDtypeStruct((M, N), F32)

    def body(a_ref, b_ref, o_ref):
        k = pl.program_id(2)
        p = _dot(a_ref[...], b_ref[...])

        @pl.when(k == 0)
        def _():
            o_ref[...] = p

        @pl.when(k > 0)
        def _():
            o_ref[...] += p

    return _call(
        body, name=name, grid=(M // tmo, N // tn, T // tk),
        in_specs=[a_spec, pl.BlockSpec((tk, tn), lambda i, j, k: (k, j))],
        out_specs=out_spec, out_shape=out_shape,
        sem=("parallel", "parallel", "arbitrary"), rides=rides,
    )(at, b)


def _to_bf16(v):
    return v.astype(BF16)


def _matmul_res(a, b, res, *, tm, tn, tk, prologue, name, rides=()):
    T, K = a.shape
    N = b.shape[1]

    def body(a_ref, b_ref, res_ref, o_ref):
        k = pl.program_id(2)
        p = _dot(prologue(a_ref[...]), b_ref[...])

        @pl.when(k == 0)
        def _():
            o_ref[...] = res_ref[...] + p

        @pl.when(k > 0)
        def _():
            o_ref[...] += p

    return _call(
        body, name=name, grid=(T // tm, N // tn, K // tk),
        in_specs=[pl.BlockSpec((tm, tk), lambda i, j, k: (i, k)), pl.BlockSpec((tk, tn), lambda i, j, k: (k, j)),
                  pl.BlockSpec((tm, tn), lambda i, j, k: (i, j))],
        out_specs=pl.BlockSpec((tm, tn), lambda i, j, k: (i, j)),
        out_shape=jax.ShapeDtypeStruct((T, N), F32),
        sem=("parallel", "parallel", "arbitrary"), rides=rides,
    )(a, b, res)


def _matmul_nt(a, b, *, tm, tn, tk, name, extra=None, epilogue=None, out_dtype=F32, rides=()):
    T, K = a.shape
    if b.ndim == 3:
        per = b.shape[2] // tk
        N = b.shape[1]
        b_spec = pl.BlockSpec((None, tn, tk), lambda i, j, k: (k // per, j, k % per))
    else:
        N = b.shape[0]
        b_spec = pl.BlockSpec((tn, tk), lambda i, j, k: (j, k))
    nk = K // tk
    assert out_dtype == F32 or nk == 1
    in_specs = [pl.BlockSpec((tm, tk), lambda i, j, k: (i, k)), b_spec]
    args = [a, b]
    if extra is not None:
        in_specs.append(pl.BlockSpec((tm, tn), lambda i, j, k: (i, j)))
        args.append(extra)

    def body(*refs):
        a_ref, b_ref = refs[0], refs[1]
        o_ref = refs[-1]
        p = _dot_nt(a_ref[...].astype(BF16), b_ref[...])
        if nk == 1:
            if epilogue is not None:
                p = epilogue(p, refs[2][...])
            o_ref[...] = p.astype(out_dtype)
        else:
            k = pl.program_id(2)

            @pl.when(k == 0)
            def _():
                o_ref[...] = p

            @pl.when(k > 0)
            def _():
                o_ref[...] += p

    return _call(
        body, name=name, grid=(T // tm, N // tn, nk),
        in_specs=in_specs,
        out_specs=pl.BlockSpec((tm, tn), lambda i, j, k: (i, j)),
        out_shape=jax.ShapeDtypeStruct((T, N), out_dtype),
        sem=("parallel", "parallel", "arbitrary"), rides=rides,
    )(*args)


def _matmul_tn(a, b, *, tmo, tn, tk, name, a_prologue=_to_bf16, shards=1, rides=()):
    T, M = a.shape
    N = b.shape[1]
    if shards > 1:
        per = (N // shards) // tn
        out_spec = pl.BlockSpec((None, tmo, tn), lambda i, j, k: (j // per, i, j % per))
        out_shape = jax.ShapeDtypeStruct((shards, M, N // shards), F32)
    else:
        out_spec = pl.BlockSpec((tmo, tn), lambda i, j, k: (i, j))
        out_shape = jax.ShapeDtypeStruct((M, N), F32)

    def body(a_ref, b_ref, o_ref):
        k = pl.program_id(2)
        p = _dot_tn(a_prologue(a_ref[...]), b_ref[...].astype(BF16))

        @pl.when(k == 0)
        def _():
            o_ref[...] = p

        @pl.when(k > 0)
        def _():
            o_ref[...] += p

    return _call(
        body, name=name, grid=(M // tmo, N // tn, T // tk),
        in_specs=[pl.BlockSpec((tk, tmo), lambda i, j, k: (k, i)), pl.BlockSpec((tk, tn), lambda i, j, k: (k, j))],
        out_specs=out_spec, out_shape=out_shape,
        sem=("parallel", "parallel", "arbitrary"), rides=rides,
    )(a, b)


def _loss_bwd(h2, tgt, g, *, tm):
    T, D = h2.shape

    def body(h_ref, t_ref, g_ref, dh_ref, dhb_ref, dg_ref, loss_ref):
        @pl.when(pl.program_id(0) == 0)
        def _():
            dg_ref[...] = jnp.zeros_like(dg_ref)
            loss_ref[...] = jnp.zeros_like(loss_ref)
        h = h_ref[...]
        gg = g_ref[...]
        r = lax.rsqrt(jnp.mean(h * h, axis=-1, keepdims=True) + EPS)
        hn = h * r
        err = hn * gg - t_ref[...]
        loss_ref[...] += 0.5 * jnp.sum(jnp.mean(err * err, axis=-1, keepdims=True), axis=0, keepdims=True)
        dy = err * (1.0 / D)
        dg_ref[...] += jnp.sum(dy * hn, axis=0, keepdims=True)
        w = dy * gg
        dh = r * w - h * ((r * r * r) * jnp.mean(w * h, axis=-1, keepdims=True))
        dh_ref[...] = dh
        dhb_ref[...] = dh.astype(BF16)

    tile = pl.BlockSpec((tm, D), lambda i: (i, 0))
    return pl.pallas_call(
        body, name="loss_bwd", grid=(T // tm,),
        in_specs=[tile, tile, pl.BlockSpec((1, D), lambda i: (0, 0))],
        out_specs=[tile, tile, pl.BlockSpec((1, D), lambda i: (0, 0)), pl.BlockSpec((1, 1), lambda i: (0, 0))],
        out_shape=[jax.ShapeDtypeStruct((T, D), F32), jax.ShapeDtypeStruct((T, D), BF16),
                   jax.ShapeDtypeStruct((1, D), F32), jax.ShapeDtypeStruct((1, 1), F32)],
        compiler_params=_params(("arbitrary",)),
    )(h2, tgt, g)


def _rms_bwd_res(dn, h, g, dres, *, tm, name, rides=()):
    T, D = h.shape

    def body(dn_ref, h_ref, g_ref, dres_ref, dh_ref, dhb_ref, dg_ref):
        @pl.when(pl.program_id(0) == 0)
        def _():
            dg_ref[...] = jnp.zeros_like(dg_ref)
        h_ = h_ref[...]
        dn_ = dn_ref[...]
        dh, r = _rms_bwd(dn_, h_, g_ref[...])
        dg_ref[...] += jnp.sum(dn_ * (h_ * r), axis=0, keepdims=True)
        dh = dres_ref[...] + dh
        dh_ref[...] = dh
        dhb_ref[...] = dh.astype(BF16)

    tile = pl.BlockSpec((tm, D), lambda i: (i, 0))
    return _call(
        body, name=name, grid=(T // tm,),
        in_specs=[tile, tile, pl.BlockSpec((1, D), lambda i: (0, 0)), tile],
        out_specs=[tile, tile, pl.BlockSpec((1, D), lambda i: (0, 0))],
        out_shape=[jax.ShapeDtypeStruct((T, D), F32), jax.ShapeDtypeStruct((T, D), BF16),
                   jax.ShapeDtypeStruct((1, D), F32)],
        sem=("arbitrary",), rides=rides,
    )(dn, h, g, dres)


def _rel_distance():
    i = lax.broadcasted_iota(jnp.int32, (CHUNK, 2 * CHUNK), 0)
    j = lax.broadcasted_iota(jnp.int32, (CHUNK, 2 * CHUNK), 1)
    return i + CHUNK - j


def _bias_build(table):
    def body(tab_ref, o_ref):
        rel = _rel_distance()
        ge = [rel >= t for t in BUCKET_THR]
        for h in range(B_HEADS):
            cur = jnp.full((CHUNK, 2 * CHUNK), tab_ref[0, h], F32)
            for b in range(1, N_BUCKETS):
                cur = jnp.where(ge[b - 1], tab_ref[b, h], cur)
            o_ref[h] = cur

    return pl.pallas_call(
        body, name="bias_build",
        in_specs=[pl.BlockSpec(memory_space=pltpu.SMEM)],
        out_specs=pl.BlockSpec(memory_space=pltpu.VMEM),
        out_shape=jax.ShapeDtypeStruct((B_HEADS, CHUNK, 2 * CHUNK), F32),
    )(table)


def _bias_grad(dbias):
    def body(db_ref, o_ref, acc_ref):
        rel = _rel_distance()
        lo = [0] + BUCKET_THR
        hi = BUCKET_THR + [CHUNK]
        for b in range(N_BUCKETS):
            m = (rel >= lo[b]) & (rel < hi[b])
            for h in range(B_HEADS):
                row = b * B_HEADS + h
                acc_ref[row:row + 1, :] = jnp.sum(jnp.where(m, db_ref[h], 0.0), axis=0, keepdims=True)
        o_ref[...] = jnp.sum(acc_ref[...], axis=1, keepdims=True)

    return pl.pallas_call(
        body, name="bias_grad",
        in_specs=[pl.BlockSpec(memory_space=pltpu.VMEM)],
        out_specs=pl.BlockSpec(memory_space=pltpu.VMEM),
        out_shape=jax.ShapeDtypeStruct((N_BUCKETS * B_HEADS, 1), F32),
        scratch_shapes=[pltpu.VMEM((N_BUCKETS * B_HEADS, 2 * CHUNK), F32)],
    )(dbias)


def _causal_mask():
    t = lax.broadcasted_iota(jnp.int32, (CHUNK, CHUNK), 0)
    s = lax.broadcasted_iota(jnp.int32, (CHUNK, CHUNK), 1)
    return s <= t


def _band_mask(n):
    rel = _rel_distance()
    j = lax.broadcasted_iota(jnp.int32, (CHUNK, 2 * CHUNK), 1)
    return (rel >= 0) & (rel < CHUNK) & ((n > 0) | (j >= CHUNK))


def _gate_forward(u, v, lg, lb, wc, bs):
    ug = _gelu(u)
    vg = _gelu(v)
    mu = jnp.mean(vg, axis=-1, keepdims=True)
    xc = vg - mu
    rstd = lax.rsqrt(jnp.mean(xc * xc, axis=-1, keepdims=True) + EPS)
    xhat = xc * rstd
    vl = (xhat * lg + lb).astype(BF16)
    mixed = _dot(wc, vl) + bs
    return ug, xhat, rstd, vl, mixed


def _softmax_scores(qk, bias, mask, sink):
    s = qk * SCALE + bias
    s = jnp.where(mask, s, NEG)
    m = jnp.maximum(jnp.max(s, axis=-1, keepdims=True), sink)
    p = jnp.exp(s - m)
    e_sink = jnp.exp(sink - m)
    inv = 1.0 / (jnp.sum(p, axis=-1, keepdims=True) + e_sink)
    return p * inv, e_sink * inv


PAIRS = Q_PER_KV // 2


def _head(g, pr, e):
    return g * Q_PER_KV + 2 * pr + e


def _stack_pairs(ref, g, col0=0):
    w = 2 * HEAD_DIM
    return jnp.concatenate([ref[:, col0 + (g * PAIRS + pr) * w:col0 + (g * PAIRS + pr + 1) * w] for pr in range(PAIRS)],
                           axis=0)


def _low_lanes():
    return lax.broadcasted_iota(jnp.int32, (2 * CHUNK, 2 * HEAD_DIM), 1) < HEAD_DIM


def _band_operands(kv_prev, kv_cur):
    band = jnp.concatenate([kv_prev, kv_cur], axis=0)
    low = _low_lanes()
    ops = []
    for cat in (band[:, :KV_WIDTH], band[:, KV_WIDTH:]):
        rol = pltpu.roll(cat, HEAD_DIM, 1)
        ops.append([[jnp.where(low if e == 0 else ~low, cat if g == e else rol, 0.0).astype(BF16) for e in range(2)]
                    for g in range(2)])
    return ops


def _mixer_fwd(proj, lg, lb, wsp, bs_col, sinks, bias, ga, gb, rides=()):
    T = proj.shape[0]
    nb = T // CHUNK

    def body(u_ref, v_ref, q_ref, kvc_ref, kvp_ref, lg_ref, lb_ref, w_ref, bs_ref, sink_ref, bias_ref,
             ga_ref, gb_ref, mixed_ref, mixed_t_ref, ab_ref):
        n = pl.program_id(0)
        causal = _causal_mask()
        ssq = jnp.zeros((CHUNK, 1), F32)
        for g in range(A_GROUPS):
            cols = slice(g * CHUNK, (g + 1) * CHUNK)
            wc = jnp.where(causal, w_ref[g], 0.0).astype(BF16)
            ug, _, _, _, mixed = _gate_forward(u_ref[:, cols], v_ref[:, cols], lg_ref[g:g + 1, :], lb_ref[g:g + 1, :],
                                               wc, bs_ref[g])
            a = ug * mixed
            ab_ref[:, cols] = a
            ssq = ssq + jnp.sum(a * a, axis=-1, keepdims=True)
        ra = lax.rsqrt(ssq * (1.0 / A_WIDTH) + EPS)
        mixed_ref[:, :A_WIDTH] = ((ab_ref[:, :A_WIDTH] * ra) * ga_ref[...]).astype(BF16)

        mask = _band_mask(n)
        kops, vops = _band_operands(kvp_ref[...], kvc_ref[...])
        ssq = jnp.zeros((CHUNK, 1), F32)
        for g in range(B_HEADS // Q_PER_KV):
            qst = _stack_pairs(q_ref, g).astype(BF16)
            o_st = jnp.zeros((PAIRS * CHUNK, 2 * HEAD_DIM), F32)
            for e in range(2):
                s_all = _dot_nt(qst, kops[g][e])
                ps = []
                for pr in range(PAIRS):
                    h = _head(g, pr, e)
                    p, _ = _softmax_scores(s_all[pr * CHUNK:(pr + 1) * CHUNK], bias_ref[h], mask, sink_ref[0, h])
                    ps.append(p.astype(BF16))
                o_st = o_st + _dot(jnp.concatenate(ps, axis=0), vops[g][e])
            for pr in range(PAIRS):
                o = o_st[pr * CHUNK:(pr + 1) * CHUNK]
                c0 = A_WIDTH + (g * PAIRS + pr) * 2 * HEAD_DIM
                ab_ref[:, c0:c0 + 2 * HEAD_DIM] = o
                ssq = ssq + jnp.sum(o * o, axis=-1, keepdims=True)
        rb = lax.rsqrt(ssq * (1.0 / B_WIDTH) + EPS)
        mixed_ref[:, A_WIDTH:] = ((ab_ref[:, A_WIDTH:] * rb) * gb_ref[...]).astype(BF16)
        mixed_t_ref[...] = mixed_ref[...].T

    full = lambda *shape: pl.BlockSpec(shape, lambda n: (0,) * len(shape))
    return _call(
        body, name="mixer_fwd", grid=(nb,),
        in_specs=[pl.BlockSpec((CHUNK, A_WIDTH), lambda n: (n, 0)),
                  pl.BlockSpec((CHUNK, A_WIDTH), lambda n: (n, 1)),
                  pl.BlockSpec((CHUNK, B_WIDTH), lambda n: (n, 2)),
                  pl.BlockSpec((CHUNK, 2 * KV_WIDTH), lambda n: (n, 12)),
                  pl.BlockSpec((CHUNK, 2 * KV_WIDTH), lambda n: (jnp.maximum(n - 1, 0), 12)),
                  full(A_GROUPS, CHUNK), full(A_GROUPS, CHUNK), full(A_GROUPS, CHUNK, CHUNK), full(A_GROUPS, CHUNK, 1),
                  pl.BlockSpec(memory_space=pltpu.SMEM), full(B_HEADS, CHUNK, 2 * CHUNK),
                  full(1, A_WIDTH), full(1, B_WIDTH)],
        out_specs=[pl.BlockSpec((CHUNK, D_MODEL), lambda n: (n, 0)), pl.BlockSpec((D_MODEL, CHUNK), lambda n: (0, n)),
                   pl.BlockSpec((CHUNK, D_MODEL), lambda n: (n, 0))],
        out_shape=[jax.ShapeDtypeStruct((T, D_MODEL), BF16), jax.ShapeDtypeStruct((D_MODEL, T), BF16),
                   jax.ShapeDtypeStruct((T, D_MODEL), F32)],
        sem=("parallel",), rides=rides,
    )(proj, proj, proj, proj, proj, lg, lb, wsp, bs_col, sinks, bias, ga, gb)


def _gmlp_bwd(proj, ab, dmixed, ga, lg, lb, wsp, bs_col, rides=()):
    T = proj.shape[0]
    nb = T // CHUNK

    def body(u_ref, v_ref, a_ref, dna_ref, ga_ref, lg_ref, lb_ref, w_ref, bs_ref,
             dp_ref, dga_ref, dw_ref, dbs_ref, dlg_ref, dlb_ref):
        @pl.when(pl.program_id(0) == 0)
        def _():
            for r in (dga_ref, dw_ref, dbs_ref, dlg_ref, dlb_ref):
                r[...] = jnp.zeros_like(r)
        causal = _causal_mask()
        a_all = a_ref[...]
        dna = dna_ref[...]
        da_all, ra = _rms_bwd(dna, a_all, ga_ref[...])
        dga_ref[...] += jnp.sum(dna * (a_all * ra), axis=0, keepdims=True)
        for g in range(A_GROUPS):
            cols = slice(g * CHUNK, (g + 1) * CHUNK)
            wc = jnp.where(causal, w_ref[g], 0.0).astype(BF16)
            lgg = lg_ref[g:g + 1, :]
            u = u_ref[:, cols]
            v = v_ref[:, cols]
            ug, xhat, rstd, vl, mixed = _gate_forward(u, v, lgg, lb_ref[g:g + 1, :], wc, bs_ref[g])
            da = da_all[:, cols]
            dug = da * mixed
            dmg = da * ug
            dmg_b = dmg.astype(BF16)
            dbs_ref[g] += jnp.sum(dmg, axis=-1, keepdims=True)
            dw_ref[g] += jnp.where(causal, _dot_nt(dmg_b, vl), 0.0)
            dvl = _dot_tn(wc, dmg_b)
            dlg_ref[g:g + 1, :] += jnp.sum(dvl * xhat, axis=0, keepdims=True)
            dlb_ref[g:g + 1, :] += jnp.sum(dvl, axis=0, keepdims=True)
            dxh = dvl * lgg
            dvg = rstd * (dxh - jnp.mean(dxh, axis=-1, keepdims=True)
                          - xhat * jnp.mean(dxh * xhat, axis=-1, keepdims=True))
            _, gu = _gelu_and_grad(u)
            _, gv = _gelu_and_grad(v)
            dp_ref[:, cols] = (dug * gu).astype(BF16)
            dp_ref[:, A_WIDTH + g * CHUNK:A_WIDTH + (g + 1) * CHUNK] = (dvg * gv).astype(BF16)

    full = lambda *shape: pl.BlockSpec(shape, lambda n: (0,) * len(shape))
    return _call(
        body, name="gmlp_bwd", grid=(nb,),
        in_specs=[pl.BlockSpec((CHUNK, A_WIDTH), lambda n: (n, 0)),
                  pl.BlockSpec((CHUNK, A_WIDTH), lambda n: (n, 1)),
                  pl.BlockSpec((CHUNK, A_WIDTH), lambda n: (n, 0)),
                  pl.BlockSpec((CHUNK, A_WIDTH), lambda n: (n, 0)),
                  full(1, A_WIDTH), full(A_GROUPS, CHUNK), full(A_GROUPS, CHUNK), full(A_GROUPS, CHUNK, CHUNK),
                  full(A_GROUPS, CHUNK, 1)],
        out_specs=[pl.BlockSpec((CHUNK, 2 * A_WIDTH), lambda n: (n, 0)),
                   full(1, A_WIDTH), full(A_GROUPS, CHUNK, CHUNK), full(A_GROUPS, CHUNK, 1),
                   full(A_GROUPS, CHUNK), full(A_GROUPS, CHUNK)],
        out_shape=[jax.ShapeDtypeStruct((T, 2 * A_WIDTH), BF16),
                   jax.ShapeDtypeStruct((1, A_WIDTH), F32), jax.ShapeDtypeStruct((A_GROUPS, CHUNK, CHUNK), F32),
                   jax.ShapeDtypeStruct((A_GROUPS, CHUNK, 1), F32), jax.ShapeDtypeStruct((A_GROUPS, CHUNK), F32),
                   jax.ShapeDtypeStruct((A_GROUPS, CHUNK), F32)],
        sem=("arbitrary",), rides=rides,
    )(proj, proj, ab, dmixed, ga, lg, lb, wsp, bs_col)


def _attn_bwd(proj, ab, dmixed, gb, sinks, bias, rides=()):
    T = proj.shape[0]
    nb = T // CHUNK
    qn = lambda n: jnp.minimum(n, nb - 1)

    def body(q_ref, kvc_ref, kvp_ref, o_ref, dnb_ref, gb_ref, sink_ref, bias_ref,
             dq_ref, dkv_ref, dgb_ref, dsink_ref, dbias_ref, carry_ref, sacc_ref):
        n = pl.program_id(0)

        @pl.when(n == 0)
        def _():
            carry_ref[...] = jnp.zeros_like(carry_ref)
            sacc_ref[...] = jnp.zeros_like(sacc_ref)
            dgb_ref[...] = jnp.zeros_like(dgb_ref)
            dbias_ref[...] = jnp.zeros_like(dbias_ref)

        @pl.when(n < nb)
        def _():
            mask = _band_mask(n)
            o_all = o_ref[...]
            dnb = dnb_ref[...]
            do_all, rb = _rms_bwd(dnb, o_all, gb_ref[...])
            dgb_ref[...] += jnp.sum(dnb * (o_all * rb), axis=0, keepdims=True)
            kops, vops = _band_operands(kvp_ref[...], kvc_ref[...])
            low = _low_lanes()
            halves = []
            for g in range(B_HEADS // Q_PER_KV):
                qst = _stack_pairs(q_ref, g).astype(BF16)
                dost = _stack_pairs(do_all, g).astype(BF16)
                dq_st = jnp.zeros((PAIRS * CHUNK, 2 * HEAD_DIM), F32)
                dk_e, dv_e = [], []
                for e in range(2):
                    s_all = _dot_nt(qst, kops[g][e])
                    dp_all = _dot_nt(dost, vops[g][e])
                    ps, dsrs = [], []
                    for pr in range(PAIRS):
                        h = _head(g, pr, e)
                        rows = slice(pr * CHUNK, (pr + 1) * CHUNK)
                        p, p_sink = _softmax_scores(s_all[rows], bias_ref[h], mask, sink_ref[0, h])
                        dp = dp_all[rows]
                        delta = jnp.sum(p * dp, axis=-1, keepdims=True)
                        ds = p * (dp - delta)
                        sacc_ref[:, h:h + 1] += -(p_sink * delta)
                        dbias_ref[h] += ds
                        ps.append(p.astype(BF16))
                        dsrs.append((ds * SCALE).astype(BF16))
                    dsr_all = jnp.concatenate(dsrs, axis=0)
                    dq_st = dq_st + _dot(dsr_all, kops[g][e])
                    dk_e.append(_dot_tn(dsr_all, qst))
                    dv_e.append(_dot_tn(jnp.concatenate(ps, axis=0), dost))
                for pr in range(PAIRS):
                    c0 = (g * PAIRS + pr) * 2 * HEAD_DIM
                    dq_ref[:, c0:c0 + 2 * HEAD_DIM] = dq_st[pr * CHUNK:(pr + 1) * CHUNK].astype(BF16)
                halves.append((dk_e, dv_e))
            tiles = []
            for t in range(2):
                g0, g1 = halves[0][t], halves[1][t]
                tiles.append(jnp.where(low, g0[0] + pltpu.roll(g0[1], HEAD_DIM, 1), pltpu.roll(g1[0], HEAD_DIM, 1) + g1[1]))
            dband = jnp.concatenate(tiles, axis=1)
            dkv_ref[...] = (carry_ref[...] + dband[:CHUNK]).astype(BF16)
            carry_ref[...] = dband[CHUNK:]

        @pl.when(n == nb)
        def _():
            dkv_ref[...] = carry_ref[...].astype(BF16)
            dsink_ref[...] = jnp.sum(sacc_ref[...], axis=0, keepdims=True)

    full = lambda *shape: pl.BlockSpec(shape, lambda n: (0,) * len(shape))
    return _call(
        body, name="attn_bwd", grid=(nb + 1,),
        in_specs=[pl.BlockSpec((CHUNK, B_WIDTH), lambda n: (qn(n), 2)),
                  pl.BlockSpec((CHUNK, 2 * KV_WIDTH), lambda n: (qn(n), 12)),
                  pl.BlockSpec((CHUNK, 2 * KV_WIDTH), lambda n: (jnp.maximum(qn(n) - 1, 0), 12)),
                  pl.BlockSpec((CHUNK, B_WIDTH), lambda n: (qn(n), 1)),
                  pl.BlockSpec((CHUNK, B_WIDTH), lambda n: (qn(n), 1)),
                  full(1, B_WIDTH), pl.BlockSpec(memory_space=pltpu.SMEM), full(B_HEADS, CHUNK, 2 * CHUNK)],
        out_specs=[pl.BlockSpec((CHUNK, B_WIDTH), lambda n: (qn(n), 0)),
                   pl.BlockSpec((CHUNK, 2 * KV_WIDTH), lambda n: (jnp.maximum(n - 1, 0), 0)),
                   full(1, B_WIDTH), full(1, B_HEADS), full(B_HEADS, CHUNK, 2 * CHUNK)],
        out_shape=[jax.ShapeDtypeStruct((T, B_WIDTH), BF16), jax.ShapeDtypeStruct((T, 2 * KV_WIDTH), BF16),
                   jax.ShapeDtypeStruct((1, B_WIDTH), F32), jax.ShapeDtypeStruct((1, B_HEADS), F32),
                   jax.ShapeDtypeStruct((B_HEADS, CHUNK, 2 * CHUNK), F32)],
        scratch_shapes=[pltpu.VMEM((CHUNK, 2 * KV_WIDTH), F32), pltpu.VMEM((CHUNK, B_HEADS), F32)],
        sem=("arbitrary",), rides=rides,
    )(proj, proj, proj, ab, dmixed, gb, sinks, bias)


def _sq_relu_grad(acc, z):
    return acc * (2.0 * jnp.maximum(z, 0.0))


def _local_step(x, tgt, sp, win, wo, wu, wd):
    T = x.shape[0]
    tm = min(512, T)
    tk = min(512, T)
    lg = sp["gate_norm_g"].reshape(A_GROUPS, CHUNK)
    lb = sp["gate_norm_b"].reshape(A_GROUPS, CHUNK)
    wsp = sp["w_spatial"].reshape(A_GROUPS, CHUNK, CHUNK)
    bs_col = sp["b_spatial"].reshape(A_GROUPS, CHUNK, 1)
    sinks = sp["attn_sinks"].reshape(1, B_HEADS)
    ga = sp["out_norm_a_g"].reshape(1, A_WIDTH)
    gb = sp["out_norm_b_g"].reshape(1, B_WIDTH)
    g1 = sp["mix_norm_g"].reshape(1, D_MODEL)
    g2 = sp["ffn_norm_g"].reshape(1, D_MODEL)
    gf = sp["final_norm_g"].reshape(1, D_MODEL)

    bias = _bias_build(sp["rel_bias_table"])
    n1, proj = _norm_matmul(x, g1, win, tm=tm, tn=PROJ_WIDTH // 2, name="in_proj")
    mixed, ab = _mixer_fwd(proj, lg, lb, wsp, bs_col, sinks, bias, ga, gb)
    h1 = _matmul_res(mixed, wo, x, tm=tm, tn=1024, tk=D_MODEL, prologue=_to_bf16, name="out_proj")
    n2, zp = _norm_matmul(h1, g2, wu, tm=tm, tn=1024, name="up_proj")
    h2 = _matmul_res(zp, wd, h1, tm=tm, tn=1024, tk=2048, prologue=_sq_relu_bf16, name="down_proj")

    dh2, dgf, loss = _loss_bwd(h2, tgt, gf, tm=tm)
    dzp = _matmul_nt(dh2, wd, tm=tm, tn=1024, tk=D_MODEL, name="bwd_dz", extra=zp, epilogue=_sq_relu_grad,
                     out_dtype=BF16)
    dwd = _matmul_tn(zp, dh2, tmo=1024, tn=1024, tk=tk, name="grad_w_down", a_prologue=_sq_relu_bf16)
    dwu = _matmul_tn(n2, dzp, tmo=1024, tn=1024, tk=tk, name="grad_w_up", shards=N_CHIPS)
    dn2 = _matmul_nt(dzp, wu, tm=tm, tn=1024, tk=2048, name="bwd_dn2")
    dh1, dg2 = _rms_bwd_res(dn2, h1, g2, dh2, tm=tm, name="ffn_norm_bwd")
    dwo = _matmul_tn(mixed, dh1, tmo=1024, tn=1024, tk=tk, name="grad_w_out")
    dmixed = _matmul_nt(dh1, wo, tm=tm, tn=1024, tk=D_MODEL, name="bwd_dmixed")
    duv, dga, dwsp, dbs, dlg, dlb = _gmlp_bwd(proj, ab, dmixed, ga, lg, lb, wsp, bs_col)
    dq, dkv, dgb, dsinks, dbias = _attn_bwd(proj, ab, dmixed, gb, sinks, bias)
    dtable = _bias_grad(dbias)
    dproj = jnp.concatenate([duv, dq, dkv], axis=1)
    dwin = _matmul_tn(n1, dproj, tmo=1024, tn=PROJ_WIDTH // 2, tk=tk, name="grad_w_in")
    dn1 = _matmul_nt(dproj, win, tm=tm, tn=1024, tk=PROJ_WIDTH, name="bwd_dn1")
    dx, dg1 = _rms_bwd_res(dn1, x, g1, dh1, tm=tm, name="mix_norm_bwd")

    small = {
        "rel_bias_table": dtable.reshape(N_BUCKETS, B_HEADS), "mix_norm_g": dg1, "gate_norm_g": dlg, "gate_norm_b": dlb,
        "w_spatial": dwsp, "b_spatial": dbs, "attn_sinks": dsinks, "out_norm_a_g": dga, "out_norm_b_g": dgb,
        "ffn_norm_g": dg2, "final_norm_g": dgf,
    }
    return loss, dx, (dwin, dwo, dwu, dwd), small


def _place():
    x, y, c = lax.axis_index("x"), lax.axis_index("y"), lax.axis_index("c")
    chips = [(1 - x, y), (x, 1 - y), (1 - x, 1 - y)]
    return x, y, c, chips


def _remote(src, dst, send_sem, recv_sem, to):
    return pltpu.make_async_remote_copy(src_ref=src, dst_ref=dst, send_sem=send_sem, recv_sem=recv_sem,
                                        device_id=to, device_id_type=MESH)


def _core_index():
    return lax.axis_index("c").astype(jnp.int32).reshape(1)


def _chip_index():
    return (2 * lax.axis_index("x") + lax.axis_index("y")).astype(jnp.int32).reshape(1)


def _cast_into_slot(w, *, tm, name):
    R, C = w.shape

    def body(me_ref, w_ref, o_ref):
        del me_ref
        o_ref[...] = w_ref[...].astype(BF16)

    return pl.pallas_call(
        body, name=name,
        grid_spec=pltpu.PrefetchScalarGridSpec(
            num_scalar_prefetch=1, grid=(R // tm,),
            in_specs=[pl.BlockSpec((tm, C), lambda i, me: (i, 0))],
            out_specs=pl.BlockSpec((None, tm, C), lambda i, me: (me[0], i, 0))),
        out_shape=jax.ShapeDtypeStruct((N_CHIPS, R, C), BF16), compiler_params=_params(("parallel",)),
    )(_chip_index(), w)


def _gather_weights(slots):
    nw = len(slots)

    def body(*refs):
        fulls = refs[nw:2 * nw]
        send_sems, recv_sems = refs[2 * nw:]
        x, y, c, chips = _place()
        me = 2 * x + y
        sends = []
        for w in range(nw):
            hr = fulls[w].shape[1] // 2
            rows = pl.ds(c * hr, hr)
            for j, chip in enumerate(chips):
                mine = fulls[w].at[me, rows, :]
                cp = _remote(mine, mine, send_sems.at[6 * w + j], recv_sems.at[6 * w + j], (*chip, c))
                cp.start()
                sends.append(cp)
        for w in range(nw):
            hr = fulls[w].shape[1] // 2
            rows = pl.ds(c * hr, hr)
            for j, chip in enumerate(chips):
                landed = fulls[w].at[2 * chip[0] + chip[1], rows, :]
                _remote(landed, landed, send_sems.at[6 * w + j], recv_sems.at[6 * w + j], (x, y, c)).wait_recv()
                cp = _remote(landed, landed, send_sems.at[6 * w + 3 + j], recv_sems.at[6 * w + 3 + j], (x, y, 1 - c))
                cp.start()
                sends.append(cp)
        for w in range(nw):
            hr = fulls[w].shape[1] // 2
            rows = pl.ds((1 - c) * hr, hr)
            for j, chip in enumerate(chips):
                other = fulls[w].at[2 * chip[0] + chip[1], rows, :]
                _remote(other, other, send_sems.at[6 * w + 3 + j], recv_sems.at[6 * w + 3 + j], (x, y, c)).wait_recv()
        for cp in sends:
            cp.wait_send()

    any_spec = pl.BlockSpec(memory_space=pl.ANY)
    return pl.pallas_call(
        body, name="gather_weights",
        in_specs=[any_spec] * nw, out_specs=[any_spec] * nw,
        out_shape=[jax.ShapeDtypeStruct(s.shape, s.dtype) for s in slots],
        scratch_shapes=[pltpu.SemaphoreType.DMA((6 * nw,)), pltpu.SemaphoreType.DMA((6 * nw,))],
        input_output_aliases={w: w for w in range(nw)},
    )(*slots)


def _sibling_halves(grads):
    nw = len(grads)

    def body(*refs):
        gs, outs = refs[:nw], refs[nw:2 * nw]
        send_sems, recv_sems = refs[2 * nw:]
        x, y, c, _ = _place()
        cps = []
        for w in range(nw):
            hr = gs[w].shape[1] // 2
            cp = _remote(gs[w].at[:, pl.ds((1 - c) * hr, hr), :], outs[w], send_sems.at[w], recv_sems.at[w],
                         (x, y, 1 - c))
            cp.start()
            cps.append(cp)
        for cp in cps:
            cp.wait()

    any_spec = pl.BlockSpec(memory_space=pl.ANY)
    return pl.pallas_call(
        body, name="rs_sibling_halves",
        in_specs=[any_spec] * nw, out_specs=[any_spec] * nw,
        out_shape=[jax.ShapeDtypeStruct((g.shape[0], g.shape[1] // 2, g.shape[2]), g.dtype) for g in grads],
        scratch_shapes=[pltpu.SemaphoreType.DMA((nw,)), pltpu.SemaphoreType.DMA((nw,))],
    )(*grads)


def _pair_sum_bf16(g, got, *, tm, name):
    S, R, C = g.shape
    hr = R // 2
    nt = hr // tm

    def body(c_ref, g_ref, got_ref, o_ref):
        del c_ref
        o_ref[...] = (g_ref[...] + got_ref[...]).astype(BF16)

    return pl.pallas_call(
        body, name=name,
        grid_spec=pltpu.PrefetchScalarGridSpec(
            num_scalar_prefetch=1, grid=(S, nt),
            in_specs=[pl.BlockSpec((None, tm, C), lambda s, i, c: (s, c[0] * nt + i, 0)),
                      pl.BlockSpec((None, tm, C), lambda s, i, c: (s, i, 0))],
            out_specs=pl.BlockSpec((None, tm, C), lambda s, i, c: (s, i, 0))),
        out_shape=jax.ShapeDtypeStruct((S, hr, C), BF16),
        compiler_params=_params(("parallel", "parallel")),
    )(_core_index(), g, got)


def _scatter_to_owners(pairs):
    nw = len(pairs)

    def body(*refs):
        qs, outs = refs[:nw], refs[nw:2 * nw]
        send_sems, recv_sems = refs[2 * nw:]
        x, y, c, chips = _place()
        cps = []
        for w in range(nw):
            for j, chip in enumerate(chips):
                cp = _remote(qs[w].at[2 * chip[0] + chip[1]], outs[w].at[j], send_sems.at[3 * w + j],
                             recv_sems.at[3 * w + j], (*chip, c))
                cp.start()
                cps.append(cp)
        for cp in cps:
            cp.wait()

    any_spec = pl.BlockSpec(memory_space=pl.ANY)
    return pl.pallas_call(
        body, name="rs_scatter_to_owners",
        in_specs=[any_spec] * nw, out_specs=[any_spec] * nw,
        out_shape=[jax.ShapeDtypeStruct((3,) + q.shape[1:], q.dtype) for q in pairs],
        scratch_shapes=[pltpu.SemaphoreType.DMA((3 * nw,)), pltpu.SemaphoreType.DMA((3 * nw,))],
    )(*pairs)


def _owner_sum(g, got, others, *, tm, name):
    S, R, C = g.shape
    hr = R // 2
    nt = hr // tm

    def body(idx_ref, g_ref, got_ref, o_ref_in, out_ref):
        del idx_ref
        acc = g_ref[...] + got_ref[...]
        for j in range(3):
            acc = acc + o_ref_in[j].astype(F32)
        out_ref[...] = acc

    return pl.pallas_call(
        body, name=name,
        grid_spec=pltpu.PrefetchScalarGridSpec(
            num_scalar_prefetch=1, grid=(nt,),
            in_specs=[pl.BlockSpec((None, tm, C), lambda i, p: (p[1], p[0] * nt + i, 0)),
                      pl.BlockSpec((None, tm, C), lambda i, p: (p[1], i, 0)),
                      pl.BlockSpec((3, tm, C), lambda i, p: (0, i, 0))],
            out_specs=pl.BlockSpec((tm, C), lambda i, p: (i, 0))),
        out_shape=jax.ShapeDtypeStruct((hr, C), F32),
        compiler_params=_params(("parallel",)),
    )(jnp.concatenate([_core_index(), _chip_index()]), g, got, others)


def _swap_halves(halves):
    nw = len(halves)

    def body(*refs):
        hs, outs = refs[:nw], refs[nw:2 * nw]
        send_sems, recv_sems = refs[2 * nw:]
        x, y, c, _ = _place()
        cps = []
        for w in range(nw):
            cp = _remote(hs[w], outs[w], send_sems.at[w], recv_sems.at[w], (x, y, 1 - c))
            cp.start()
            cps.append(cp)
        for cp in cps:
            cp.wait()

    any_spec = pl.BlockSpec(memory_space=pl.ANY)
    return pl.pallas_call(
        body, name="rs_swap_halves",
        in_specs=[any_spec] * nw, out_specs=[any_spec] * nw,
        out_shape=[jax.ShapeDtypeStruct(h.shape, h.dtype) for h in halves],
        scratch_shapes=[pltpu.SemaphoreType.DMA((nw,)), pltpu.SemaphoreType.DMA((nw,))],
    )(*halves)


def _all_reduce_small(packed):
    R, C = packed.shape

    def body(in_ref, out_ref, slots, send_sems, recv_sems):
        x, y, c, _ = _place()
        me = 4 * x + 2 * y + c
        cps = []
        for k in range(1, N_DEV):
            p = (me + k) % N_DEV
            cp = _remote(in_ref, slots.at[me], send_sems.at[k - 1], recv_sems.at[k - 1], (p // 4, (p // 2) % 2, p % 2))
            cp.start()
            cps.append(cp)
        slots[me] = in_ref[...]
        for k in range(1, N_DEV):
            src = (me + N_DEV - k) % N_DEV
            _remote(in_ref, slots.at[src], send_sems.at[k - 1], recv_sems.at[k - 1], (x, y, c)).wait_recv()
        for cp in cps:
            cp.wait_send()
        acc = slots[0]
        for d in range(1, N_DEV):
            acc = acc + slots[d]
        out_ref[...] = acc

    vmem = pl.BlockSpec(memory_space=pltpu.VMEM)
    return pl.pallas_call(
        body, name="all_reduce_small", in_specs=[vmem], out_specs=vmem,
        out_shape=jax.ShapeDtypeStruct((R, C), F32),
        scratch_shapes=[pltpu.VMEM((N_DEV, R, C), F32), pltpu.SemaphoreType.DMA((N_DEV - 1,)),
                        pltpu.SemaphoreType.DMA((N_DEV - 1,))],
        compiler_params=_params(),
    )(packed)


def _adamw_math(w, g, m, v):
    m = ADAM_B1 * m + (1.0 - ADAM_B1) * g
    v = ADAM_B2 * v + (1.0 - ADAM_B2) * (g * g)
    m_hat = m / (1.0 - ADAM_B1 ** ADAM_STEP)
    v_hat = v / (1.0 - ADAM_B2 ** ADAM_STEP)
    delta = -ADAM_LR * (m_hat / (jnp.sqrt(v_hat) + ADAM_EPS) + ADAM_WD * w)
    return delta, m, v


def _adamw(w, g, m, v, *, tm, name):
    R, C = w.shape

    def body(w_ref, g_ref, m_ref, v_ref, d_ref, nm_ref, nv_ref):
        d_ref[...], nm_ref[...], nv_ref[...] = _adamw_math(w_ref[...], g_ref[...], m_ref[...], v_ref[...])

    spec = pl.BlockSpec((tm, C), lambda i: (i, 0))
    return pl.pallas_call(
        body, name=name, grid=(R // tm,), in_specs=[spec] * 4, out_specs=[spec] * 3,
        out_shape=[jax.ShapeDtypeStruct((R, C), F32)] * 3, compiler_params=_params(("parallel",)),
    )(w, g, m, v)


def _adamw_halves(w, own, got, m, v, *, tm, name, rides=()):
    R, C = w.shape
    nt = (R // 2) // tm

    def body(w_ref, own_ref, got_ref, m_ref, v_ref, g_ref, d_ref, nm_ref, nv_ref):
        g = jnp.where(pl.program_id(0) == lax.axis_index("c"), own_ref[...], got_ref[...])
        g_ref[...] = g
        d_ref[...], nm_ref[...], nv_ref[...] = _adamw_math(w_ref[...], g, m_ref[...], v_ref[...])

    whole = pl.BlockSpec((tm, C), lambda h, i: (h * nt + i, 0))
    half = pl.BlockSpec((tm, C), lambda h, i: (i, 0))
    return _call(
        body, name=name, grid=(2, nt), in_specs=[whole, half, half, whole, whole], out_specs=[whole] * 4,
        out_shape=[jax.ShapeDtypeStruct((R, C), F32)] * 4, sem=("parallel", "parallel"), rides=rides,
    )(w, own, got, m, v)


def _adamw_small(w, slots, m, v):
    def body(w_ref, slots_ref, m_ref, v_ref, g_ref, d_ref, nm_ref, nv_ref):
        g = slots_ref[0]
        for d in range(1, N_DEV):
            g = g + slots_ref[d]
        g_ref[...] = g
        d_ref[...], nm_ref[...], nv_ref[...] = _adamw_math(w_ref[...], g, m_ref[...], v_ref[...])

    vmem = pl.BlockSpec(memory_space=pltpu.VMEM)
    return pl.pallas_call(
        body, name="adamw_small", in_specs=[vmem] * 4, out_specs=[vmem] * 4,
        out_shape=[jax.ShapeDtypeStruct(w.shape, F32)] * 4, compiler_params=_params(),
    )(w, slots, m, v)


SMALL = ["rel_bias_table", "mix_norm_g", "gate_norm_g", "gate_norm_b", "w_spatial", "b_spatial", "attn_sinks",
         "out_norm_a_g", "out_norm_b_g", "ffn_norm_g", "final_norm_g"]
LARGE = ["w_in", "w_out", "w_up", "w_down"]
WEIGHTS = ["rel_bias_table", "mix_norm_g", "w_in", "gate_norm_g", "gate_norm_b", "w_spatial", "b_spatial", "attn_sinks",
           "out_norm_a_g", "out_norm_b_g", "w_out", "ffn_norm_g", "w_up", "w_down", "final_norm_g"]
PACK_UNIT = 8 * 128


def _pack(parts):
    rows = []
    for p in parts:
        flat = p.reshape(-1)
        pad = (-flat.shape[0]) % PACK_UNIT
        rows.append(jnp.pad(flat, (0, pad)).reshape(-1, 128))
    return jnp.concatenate(rows, axis=0)


def _unpack(packed, like):
    out, row = [], 0
    for p in like:
        n = math.prod(p.shape)
        nrows = (n + PACK_UNIT - 1) // PACK_UNIT * 8
        out.append(packed[row:row + nrows].reshape(-1)[:n].reshape(p.shape))
        row += nrows
    return out


def kernel(x, rel_bias_table, mix_norm_g, w_in, gate_norm_g, gate_norm_b, w_spatial, b_spatial, attn_sinks, out_norm_a_g, out_norm_b_g, w_out, ffn_norm_g, w_up, w_down, final_norm_g, loss_target, m_rel_bias_table, m_mix_norm_g, m_w_in, m_gate_norm_g, m_gate_norm_b, m_w_spatial, m_b_spatial, m_attn_sinks, m_out_norm_a_g, m_out_norm_b_g, m_w_out, m_ffn_norm_g, m_w_up, m_w_down, m_final_norm_g, v_rel_bias_table, v_mix_norm_g, v_w_in, v_gate_norm_g, v_gate_norm_b, v_w_spatial, v_b_spatial, v_attn_sinks, v_out_norm_a_g, v_out_norm_b_g, v_w_out, v_ffn_norm_g, v_w_up, v_w_down, v_final_norm_g):
    args = dict(locals())
    wts = {n: args[n] for n in WEIGHTS}
    mom = {n: args["m_" + n] for n in WEIGHTS}
    var = {n: args["v_" + n] for n in WEIGHTS}
    sp = {n: wts[n] for n in SMALL}
    x2, tgt = x[0], loss_target[0]
    T = x2.shape[0]
    tm = min(512, T)
    tl = min(1024, T)
    lg = sp["gate_norm_g"].reshape(A_GROUPS, CHUNK)
    lb = sp["gate_norm_b"].reshape(A_GROUPS, CHUNK)
    wsp = sp["w_spatial"].reshape(A_GROUPS, CHUNK, CHUNK)
    bs_col = sp["b_spatial"].reshape(A_GROUPS, CHUNK, 1)
    sinks = sp["attn_sinks"].reshape(1, B_HEADS)
    ga = sp["out_norm_a_g"].reshape(1, A_WIDTH)
    gb = sp["out_norm_b_g"].reshape(1, B_WIDTH)
    g1 = sp["mix_norm_g"].reshape(1, D_MODEL)
    g2 = sp["ffn_norm_g"].reshape(1, D_MODEL)
    gf = sp["final_norm_g"].reshape(1, D_MODEL)

    def pair_sum(n, g, got):
        return _pair_sum_bf16(g, got, tm=256, name="rs_pair_sum_" + n)

    def owner_sum(n, g, got, others):
        return _owner_sum(g, got, others, tm=256, name="rs_owner_sum_" + n)

    own = [wts[n].reshape(wts[n].shape[1:]) for n in LARGE]
    s_in, s_out, s_up, s_down = [_cast_into_slot(w, tm=256, name="cast_" + n) for n, w in zip(LARGE, own)]
    ((g_in,),) = _carrier([_ride_gather(s_in)], name="gather_w_in")
    win = g_in.transpose(1, 0, 2).reshape(D_MODEL, PROJ_WIDTH)
    bias = _bias_build(sp["rel_bias_table"])
    (n1t, proj), ((g_out,), (s_up,)) = _norm_matmul_wide(
        x2, g1, win, tm=tm, tn=PROJ_WIDTH // 2, name="in_proj", rides=[_ride_gather(s_out), _ride_gather(s_up, (0, 2, 8))])
    wo = g_out.reshape(A_WIDTH + B_WIDTH, D_MODEL)
    (mixed, mixed_t, ab), ((s_up,),) = _mixer_fwd(proj, lg, lb, wsp, bs_col, sinks, bias, ga, gb,
                                                  rides=[_ride_gather(s_up, (2, 7, 8))])
    h1, ((wu,), (s_down,)) = _matmul_res(mixed, wo, x2, tm=tl, tn=1024, tk=D_MODEL, prologue=_to_bf16, name="out_proj",
                                         rides=[_ride_gather(s_up, (7, 8, 8)), _ride_gather(s_down, (0, 1, 8))])
    (n2t, zp, z2, z2t), ((g_down,),) = _norm_matmul_sq(h1, g2, wu, tm=tl, tn=512, name="up_proj",
                                                       rides=[_ride_gather(s_down, (1, 8, 8))])
    wd = g_down.reshape(D_FF, D_MODEL)
    h2 = _matmul_res(z2, wd, h1, tm=tl, tn=1024, tk=2048, prologue=_to_bf16, name="down_proj")

    dh2, dh2b, dgf, loss = _loss_bwd(h2, tgt, gf, tm=tm)
    dzp = _matmul_nt(dh2b, wd, tm=tl, tn=1024, tk=D_MODEL, name="bwd_dz", extra=zp, epilogue=_sq_relu_grad,
                     out_dtype=BF16)
    dwd = _matmul_nn(z2t, dh2b, tmo=1024, tn=2048, tk=tl, name="grad_w_down")
    dwd = dwd.reshape(N_CHIPS, D_FF // N_CHIPS, D_MODEL)
    dwu, ((r_d,),) = _matmul_nn(n2t, dzp, tmo=1024, tn=2048, tk=tl, name="grad_w_up", shards=N_CHIPS,
                                rides=[_ride_sibling_halves(dwd)])
    q_d = pair_sum("w_down", dwd, r_d)
    dn2, ((o_d,), (r_u,)) = _matmul_nt(dzp, wu, tm=tl, tn=1024, tk=2048, name="bwd_dn2",
                                       rides=[_ride_scatter(q_d, None, (0, 6, 8)), _ride_sibling_halves(dwu)])
    q_u = pair_sum("w_up", dwu, r_u)
    (dh1, dh1b, dg2), ((o_d,),) = _rms_bwd_res(dn2, h1, g2, dh2, tm=tm, name="ffn_norm_bwd",
                                               rides=[_ride_scatter(q_d, o_d, (6, 8, 8))])
    h_d = owner_sum("w_down", dwd, r_d, o_d)
    dwo, ((o_u,),) = _matmul_nn(mixed_t, dh1b, tmo=1024, tn=2048, tk=tl, name="grad_w_out",
                                rides=[_ride_scatter(q_u, None, (0, 2, 8))])
    dwo = dwo.reshape(N_CHIPS, (A_WIDTH + B_WIDTH) // N_CHIPS, D_MODEL)
    dmixed, ((o_u,), (r_o,)) = _matmul_nt(dh1b, wo, tm=tl, tn=1024, tk=D_MODEL, name="bwd_dmixed",
                                          rides=[_ride_scatter(q_u, o_u, (2, 4, 8)), _ride_sibling_halves(dwo)])
    q_o = pair_sum("w_out", dwo, r_o)
    (duv, dga, dwsp, dbs, dlg, dlb), ((o_u,),) = _gmlp_bwd(proj, ab, dmixed, ga, lg, lb, wsp, bs_col,
                                                           rides=[_ride_scatter(q_u, o_u, (4, 8, 8))])
    h_u = owner_sum("w_up", dwu, r_u, o_u)
    (dq, dkv, dgb, dsinks, dbias), ((o_o,), (w_d,)) = _attn_bwd(proj, ab, dmixed, gb, sinks, bias,
                                                                rides=[_ride_scatter(q_o), _ride_swap(h_d)])
    h_o = owner_sum("w_out", dwo, r_o, o_o)
    dtable = _bias_grad(dbias)
    dproj = jnp.concatenate([duv, dq, dkv], axis=1)
    dwin, ((w_o,), (w_u,)) = _matmul_nn(n1t, dproj, tmo=1024, tn=PROJ_WIDTH // 2, tk=tl, name="grad_w_in",
                                        rides=[_ride_swap(h_o), _ride_swap(h_u)])
    dwin = dwin.reshape(D_MODEL, N_CHIPS, PROJ_WIDTH // N_CHIPS).transpose(1, 0, 2)
    dn1, ((r_i,),) = _matmul_nt(dproj, win, tm=tl, tn=1024, tk=PROJ_WIDTH, name="bwd_dn1",
                                rides=[_ride_sibling_halves(dwin)])
    q_i = pair_sum("w_in", dwin, r_i)
    (dx, _, dg1), ((o_i,),) = _rms_bwd_res(dn1, x2, g1, dh1, tm=tm, name="mix_norm_bwd",
                                           rides=[_ride_scatter(q_i, None, (0, 4, 8))])
    small = {
        "rel_bias_table": dtable.reshape(N_BUCKETS, B_HEADS), "mix_norm_g": dg1, "gate_norm_g": dlg, "gate_norm_b": dlb,
        "w_spatial": dwsp, "b_spatial": dbs, "attn_sinks": dsinks, "out_norm_a_g": dga, "out_norm_b_g": dgb,
        "ffn_norm_g": dg2, "final_norm_g": dgf,
    }

    out_g, out_d, out_m, out_v = {}, {}, {}, {}

    def adamw_large(n, h, s, rides=()):
        w = own[LARGE.index(n)]
        res = _adamw_halves(w, h, s, mom[n].reshape(w.shape), var[n].reshape(w.shape), tm=256, name="adamw_" + n,
                            rides=rides)
        outs, carried = res if rides else (res, None)
        for store, val in zip((out_g, out_d, out_m, out_v), outs):
            store[n] = val.reshape(wts[n].shape)
        return carried

    like = [wts[n] for n in SMALL]
    (o_i,), (slots,) = adamw_large("w_down", h_d, w_d, rides=[_ride_scatter(q_i, o_i, (4, 8, 8)),
                                                              _ride_small_to_all(_pack([small[n] for n in SMALL]))])
    h_i = owner_sum("w_in", dwin, r_i, o_i)
    ((w_i,),) = _carrier([_ride_swap(h_i)], name="swap_w_in")
    adamw_large("w_up", h_u, w_u)
    adamw_large("w_out", h_o, w_o)
    adamw_large("w_in", h_i, w_i)
    g_s, d_s, m_s, v_s = _adamw_small(_pack(like), slots, _pack([mom[n] for n in SMALL]), _pack([var[n] for n in SMALL]))
    for n, g, d, nm, nv in zip(SMALL, _unpack(g_s, like), _unpack(d_s, like), _unpack(m_s, like), _unpack(v_s, like)):
        out_g[n], out_d[n], out_m[n], out_v[n] = g, d, nm, nv

    total = lax.psum(loss[0, 0], ("x", "y", "c"))
    return (total, dx[None], *[out_g[n] for n in WEIGHTS], *[out_d[n] for n in WEIGHTS],
            *[out_m[n] for n in WEIGHTS], *[out_v[n] for n in WEIGHTS])
```

```python
import functools
import math

import numpy as np
import jax
import jax.numpy as jnp
from jax import lax
from jax.experimental import pallas as pl
from jax.experimental.pallas import tpu as pltpu

F32 = jnp.float32
BF16 = jnp.bfloat16

D_MODEL = 2048
CHUNK = 128
A_GROUPS = 8
A_WIDTH = 1024
HEAD_DIM = 64
B_HEADS = 16
Q_PER_KV = 8
B_WIDTH = 1024
KV_WIDTH = 128
PROJ_WIDTH = 3328
D_FF = 8192
N_BUCKETS = 32
EPS = 1e-5
NEG = -1e30
SCALE = HEAD_DIM ** -0.5
N_CHIPS = 4
N_DEV = 8

ADAM_LR = 0.001
ADAM_B1 = 0.9
ADAM_B2 = 0.999
ADAM_EPS = 1e-08
ADAM_WD = 0.01
ADAM_STEP = 10

VMEM_LIMIT = 56 * 1024 * 1024
MESH = pl.DeviceIdType.MESH


def _bucket_thresholds():
    d = np.arange(CHUNK)
    n_exact = N_BUCKETS // 2
    relf = np.maximum(d, n_exact).astype(np.float64)
    large = n_exact + (np.log(relf / n_exact) / math.log(CHUNK / n_exact) * (N_BUCKETS - n_exact)).astype(np.int32)
    bucket = np.where(d < n_exact, d, np.minimum(large, N_BUCKETS - 1))
    return [int(np.min(d[bucket >= b])) for b in range(1, N_BUCKETS)]


BUCKET_THR = _bucket_thresholds()


def _params(sem=None):
    return pltpu.CompilerParams(dimension_semantics=sem, vmem_limit_bytes=VMEM_LIMIT)


def _gelu(x):
    c = math.sqrt(2.0 / math.pi)
    return 0.5 * x * (1.0 + jnp.tanh(c * (x + 0.044715 * (x * x * x))))


def _gelu_and_grad(x):
    c = math.sqrt(2.0 / math.pi)
    x2 = x * x
    t = jnp.tanh(c * (x + 0.044715 * (x2 * x)))
    g = 0.5 * x * (1.0 + t)
    dg = 0.5 * (1.0 + t) + 0.5 * x * (1.0 - t * t) * (c * (1.0 + 3.0 * 0.044715 * x2))
    return g, dg


def _dot(a, b):
    return jnp.dot(a, b, preferred_element_type=F32)


def _dot_nt(a, b):
    return lax.dot_general(a, b, (((1,), (1,)), ((), ())), preferred_element_type=F32)


def _dot_tn(a, b):
    return lax.dot_general(a, b, (((0,), (0,)), ((), ())), preferred_element_type=F32)


def _rms_bwd(dn, h, g):
    r = lax.rsqrt(jnp.mean(h * h, axis=-1, keepdims=True) + EPS)
    w = dn * g
    dh = r * w - h * ((r * r * r) * jnp.mean(w * h, axis=-1, keepdims=True))
    return dh, r


def _place():
    x, y, c = lax.axis_index("x"), lax.axis_index("y"), lax.axis_index("c")
    chips = [(1 - x, y), (x, 1 - y), (1 - x, 1 - y)]
    return x, y, c, chips


def _remote(src, dst, send_sem, recv_sem, to):
    return pltpu.make_async_remote_copy(src_ref=src, dst_ref=dst, send_sem=send_sem, recv_sem=recv_sem,
                                        device_id=to, device_id_type=MESH)


class _Ride:
    def __init__(self, args, out_shape, n_sem, start, finish, mid=None, mid_frac=0.8, aliases=None):
        self.args, self.out_shape, self.n_sem = list(args), list(out_shape), n_sem
        self.start, self.mid, self.finish, self.mid_frac = start, mid, finish, mid_frac
        self.aliases = dict(aliases or {})


def _call(body, *, name, grid, in_specs, out_specs, out_shape, scratch_shapes=(), sem=None, rides=()):
    single = not isinstance(out_shape, (list, tuple))
    out_specs = [out_specs] if single else list(out_specs)
    out_shape = [out_shape] if single else list(out_shape)
    n_in, n_out, n_scr = len(in_specs), len(out_shape), len(scratch_shapes)
    r_in = [len(r.args) for r in rides]
    r_out = [len(r.out_shape) for r in rides]
    any_spec = pl.BlockSpec(memory_space=pl.ANY)
    aliases, off_i, off_o = {}, n_in, n_out
    for r in rides:
        for i, o in r.aliases.items():
            aliases[off_i + i] = off_o + o
        off_i += len(r.args)
        off_o += len(r.out_shape)
    steps = math.prod(grid)

    def wrapped(*refs):
        p = 0
        ins = refs[p:p + n_in]; p += n_in
        rins = refs[p:p + sum(r_in)]; p += sum(r_in)
        outs = refs[p:p + n_out]; p += n_out
        routs = refs[p:p + sum(r_out)]; p += sum(r_out)
        scr = refs[p:p + n_scr]; p += n_scr
        sems = refs[p:]
        parts, pi, po = [], 0, 0
        for k, r in enumerate(rides):
            parts.append((rins[pi:pi + r_in[k]], routs[po:po + r_out[k]], sems[2 * k], sems[2 * k + 1]))
            pi += r_in[k]
            po += r_out[k]
        lin = 0
        for d in range(len(grid)):
            lin = lin * grid[d] + pl.program_id(d)
        if rides:
            @pl.when(lin == 0)
            def _():
                for r, part in zip(rides, parts):
                    r.start(*part)
        body(*ins, *outs, *scr)
        for r, part in zip(rides, parts):
            if r.mid is not None:
                @pl.when(lin == min(steps - 1, int(r.mid_frac * steps)))
                def _(r=r, part=part):
                    r.mid(*part)
        if rides:
            @pl.when(lin == steps - 1)
            def _():
                for r, part in zip(rides, parts):
                    r.finish(*part)

    scratch = list(scratch_shapes)
    for r in rides:
        scratch += [pltpu.SemaphoreType.DMA((r.n_sem,)), pltpu.SemaphoreType.DMA((r.n_sem,))]
    if rides:
        sem = ("arbitrary",) * len(grid)
    res = pl.pallas_call(
        wrapped, name=name, grid=grid,
        in_specs=list(in_specs) + [any_spec] * sum(r_in),
        out_specs=out_specs + [any_spec] * sum(r_out),
        out_shape=out_shape + [s for r in rides for s in r.out_shape],
        scratch_shapes=scratch, input_output_aliases=aliases,
        compiler_params=_params(sem),
    )

    def run(*args):
        got = res(*args, *[a for r in rides for a in r.args])
        mine = got[0] if single else list(got[:n_out])
        if not rides:
            return mine
        rest, out = list(got[n_out:]), []
        for k in range(len(rides)):
            out.append(rest[:r_out[k]])
            rest = rest[r_out[k]:]
        return mine, out

    return run


def _ride_gather(slot, part=(0, 1), mid_frac=0.8):
    k0, k1, n = part if len(part) == 3 else (part[0], part[0] + 1, part[1])
    rows_n = (k1 - k0) * (slot.shape[1] // 2 // n)
    off = lambda c: c * (slot.shape[1] // 2) + k0 * (slot.shape[1] // 2 // n)

    def start(ins, outs, ss, rs):
        x, y, c, chips = _place()
        mine = outs[0].at[2 * x + y, pl.ds(off(c), rows_n), :]
        for j, chip in enumerate(chips):
            _remote(mine, mine, ss.at[j], rs.at[j], (*chip, c)).start()

    def mid(ins, outs, ss, rs):
        x, y, c, chips = _place()
        for j, chip in enumerate(chips):
            landed = outs[0].at[2 * chip[0] + chip[1], pl.ds(off(c), rows_n), :]
            _remote(landed, landed, ss.at[j], rs.at[j], (x, y, c)).wait_recv()
            _remote(landed, landed, ss.at[3 + j], rs.at[3 + j], (x, y, 1 - c)).start()

    def finish(ins, outs, ss, rs):
        x, y, c, chips = _place()
        for j, chip in enumerate(chips):
            other = outs[0].at[2 * chip[0] + chip[1], pl.ds(off(1 - c), rows_n), :]
            _remote(other, other, ss.at[3 + j], rs.at[3 + j], (x, y, c)).wait_recv()
        for j in range(6):
            piece = outs[0].at[0, pl.ds(0, rows_n), :]
            _remote(piece, piece, ss.at[j], rs.at[j], (x, y, c)).wait_send()

    return _Ride([slot], [jax.ShapeDtypeStruct(slot.shape, slot.dtype)], 6, start, finish, mid=mid,
                 mid_frac=mid_frac, aliases={0: 0})


def _ride_sibling_halves(g):
    S, R, C = g.shape
    hr = R // 2

    def copy(ins, outs, ss, rs):
        x, y, c, _ = _place()
        return _remote(ins[0].at[:, pl.ds((1 - c) * hr, hr), :], outs[0], ss.at[0], rs.at[0], (x, y, 1 - c))

    return _Ride([g], [jax.ShapeDtypeStruct((S, hr, C), g.dtype)], 1,
                 lambda *a: copy(*a).start(), lambda *a: copy(*a).wait())


def _ride_scatter(q, land=None, part=(0, 1)):
    k0, k1, n = part if len(part) == 3 else (part[0], part[0] + 1, part[1])
    rows_n = q.shape[1] // n
    rows = pl.ds(k0 * rows_n, (k1 - k0) * rows_n)

    def copies(ins, outs, ss, rs):
        x, y, c, chips = _place()
        return [_remote(ins[0].at[2 * chip[0] + chip[1], rows, :], outs[0].at[j, rows, :], ss.at[j], rs.at[j], (*chip, c))
                for j, chip in enumerate(chips)]

    def start(*a):
        for cp in copies(*a):
            cp.start()

    def finish(*a):
        for cp in copies(*a):
            cp.wait()

    shape = jax.ShapeDtypeStruct((3,) + q.shape[1:], q.dtype)
    if land is None:
        return _Ride([q], [shape], 3, start, finish)
    return _Ride([q, land], [shape], 3, start, finish, aliases={1: 0})


def _ride_swap(h):
    def copy(ins, outs, ss, rs):
        x, y, c, _ = _place()
        return _remote(ins[0], outs[0], ss.at[0], rs.at[0], (x, y, 1 - c))

    return _Ride([h], [jax.ShapeDtypeStruct(h.shape, h.dtype)], 1,
                 lambda *a: copy(*a).start(), lambda *a: copy(*a).wait())


def _mesh_place(p):
    return (p // 4, (p // 2) % 2, p % 2)


def _ride_small_to_all(packed):
    def copies(ins, outs, ss, rs):
        x, y, c, _ = _place()
        me = 4 * x + 2 * y + c
        return [_remote(ins[0], outs[0].at[me], ss.at[k - 1], rs.at[k - 1], _mesh_place((me + k) % N_DEV))
                for k in range(1, N_DEV)]

    def own(ins, outs, ss, rs):
        x, y, c, _ = _place()
        return pltpu.make_async_copy(ins[0], outs[0].at[4 * x + 2 * y + c], ss.at[N_DEV - 1])

    def start(*a):
        own(*a).start()
        for cp in copies(*a):
            cp.start()

    def finish(ins, outs, ss, rs):
        x, y, c, _ = _place()
        me = 4 * x + 2 * y + c
        for k in range(1, N_DEV):
            _remote(ins[0], outs[0].at[(me + N_DEV - k) % N_DEV], ss.at[k - 1], rs.at[k - 1], (x, y, c)).wait_recv()
        for cp in copies(ins, outs, ss, rs):
            cp.wait_send()
        own(ins, outs, ss, rs).wait()

    return _Ride([packed], [jax.ShapeDtypeStruct((N_DEV,) + packed.shape, packed.dtype)], N_DEV, start, finish)


def _carrier(rides, *, name):
    _, outs = _call(lambda: None, name=name, grid=(1,), in_specs=[], out_specs=[], out_shape=[], rides=rides)()
    return outs


def _sq_relu_bf16(z):
    z = jnp.maximum(z, 0.0)
    return (z * z).astype(BF16)


def _norm_bf16(a_ref, g_ref):
    xf = a_ref[...]
    r = lax.rsqrt(jnp.mean(xf * xf, axis=-1, keepdims=True) + EPS)
    return ((xf * r) * g_ref[...]).astype(BF16)


def _norm_matmul_wide(a, g, b, *, tm, tn, name, rides=()):
    T, K = a.shape
    N = b.shape[1]

    def body(a_ref, g_ref, b_ref, nt_ref, o_ref):
        n = _norm_bf16(a_ref, g_ref)
        nt_ref[...] = n.T
        o_ref[...] = _dot(n, b_ref[...])

    return _call(
        body, name=name, grid=(N // tn, T // tm),
        in_specs=[pl.BlockSpec((tm, K), lambda j, i: (i, 0)), pl.BlockSpec((1, K), lambda j, i: (0, 0)),
                  pl.BlockSpec((K, tn), lambda j, i: (0, j))],
        out_specs=[pl.BlockSpec((None, K, tm), lambda j, i: (j, 0, i)), pl.BlockSpec((tm, tn), lambda j, i: (i, j))],
        out_shape=[jax.ShapeDtypeStruct((N // tn, K, T), BF16), jax.ShapeDtypeStruct((T, N), F32)],
        sem=("arbitrary", "arbitrary"), rides=rides,
    )(a, g, b)


def _norm_matmul_sq(a, g, b, *, tm, tn, name, rides=()):
    T, K = a.shape
    per = b.shape[2] // tn
    N = b.shape[0] * b.shape[2]

    def body(a_ref, g_ref, b_ref, nt_ref, o_ref, z_ref, zt_ref, n_scr):
        @pl.when(pl.program_id(1) == 0)
        def _():
            n = _norm_bf16(a_ref, g_ref)
            n_scr[...] = n
            nt_ref[...] = n.T
        p = _dot(n_scr[...], b_ref[...])
        o_ref[...] = p
        z = _sq_relu_bf16(p)
        z_ref[...] = z
        zt_ref[...] = z.T

    return _call(
        body, name=name, grid=(T // tm, N // tn),
        in_specs=[pl.BlockSpec((tm, K), lambda i, j: (i, 0)), pl.BlockSpec((1, K), lambda i, j: (0, 0)),
                  pl.BlockSpec((None, K, tn), lambda i, j: (j // per, 0, j % per))],
        out_specs=[pl.BlockSpec((K, tm), lambda i, j: (0, i)), pl.BlockSpec((tm, tn), lambda i, j: (i, j)),
                   pl.BlockSpec((tm, tn), lambda i, j: (i, j)), pl.BlockSpec((tn, tm), lambda i, j: (j, i))],
        out_shape=[jax.ShapeDtypeStruct((K, T), BF16), jax.ShapeDtypeStruct((T, N), F32),
                   jax.ShapeDtypeStruct((T, N), BF16), jax.ShapeDtypeStruct((N, T), BF16)],
        scratch_shapes=[pltpu.VMEM((tm, K), BF16)],
        sem=("parallel", "arbitrary"), rides=rides,
    )(a, g, b)


def _matmul_nn(at, b, *, tmo, tn, tk, name, shards=1, rides=()):
    M, T = at.shape[-2:]
    N = b.shape[1]
    if at.ndim == 3:
        a_spec = pl.BlockSpec((None, tmo, tk), lambda i, j, k: (0, i, k))
    else:
        a_spec = pl.BlockSpec((tmo, tk), lambda i, j, k: (i, k))
    if shards > 1:
        per = (N // shards) // tn
        out_spec = pl.BlockSpec((None, tmo, tn), lambda i, j, k: (j // per, i, j % per))
        out_shape = jax.ShapeDtypeStruct((shards, M, N // shards), F32)
    else:
        out_spec = pl.BlockSpec((tmo, tn), lambda i, j, k: (i, j))
        out_shape = jax.ShapeDtypeStruct((M, N), F32)

    def body(a_ref, b_ref, o_ref):
        k = pl.program_id(2)
        p = _dot(a_ref[...], b_ref[...])

        @pl.when(k == 0)
        def _():
            o_ref[...] = p

        @pl.when(k > 0)
        def _():
            o_ref[...] += p

    return _call(
        body, name=name, grid=(M // tmo, N // tn, T // tk),
        in_specs=[a_spec, pl.BlockSpec((tk, tn), lambda i, j, k: (k, j))],
        out_specs=out_spec, out_shape=out_shape,
        sem=("parallel", "parallel", "arbitrary"), rides=rides,
    )(at, b)


def _to_bf16(v):
    return v.astype(BF16)


def _matmul_res(a, b, res, *, tm, tn, tk, prologue, name, rides=()):
    T, K = a.shape
    N = b.shape[1]

    def body(a_ref, b_ref, res_ref, o_ref):
        k = pl.program_id(2)
        p = _dot(prologue(a_ref[...]), b_ref[...])

        @pl.when(k == 0)
        def _():
            o_ref[...] = res_ref[...] + p

        @pl.when(k > 0)
        def _():
            o_ref[...] += p

    return _call(
        body, name=name, grid=(T // tm, N // tn, K // tk),
        in_specs=[pl.BlockSpec((tm, tk), lambda i, j, k: (i, k)), pl.BlockSpec((tk, tn), lambda i, j, k: (k, j)),
                  pl.BlockSpec((tm, tn), lambda i, j, k: (i, j))],
        out_specs=pl.BlockSpec((tm, tn), lambda i, j, k: (i, j)),
        out_shape=jax.ShapeDtypeStruct((T, N), F32),
        sem=("parallel", "parallel", "arbitrary"), rides=rides,
    )(a, b, res)


def _matmul_nt(a, b, *, tm, tn, tk, name, extra=None, epilogue=None, out_dtype=F32, rides=()):
    T, K = a.shape
    if b.ndim == 3:
        per = b.shape[2] // tk
        N = b.shape[1]
        b_spec = pl.BlockSpec((None, tn, tk), lambda i, j, k: (k // per, j, k % per))
    else:
        N = b.shape[0]
        b_spec = pl.BlockSpec((tn, tk), lambda i, j, k: (j, k))
    nk = K // tk
    assert out_dtype == F32 or nk == 1
    in_specs = [pl.BlockSpec((tm, tk), lambda i, j, k: (i, k)), b_spec]
    args = [a, b]
    if extra is not None:
        in_specs.append(pl.BlockSpec((tm, tn), lambda i, j, k: (i, j)))
        args.append(extra)

    def body(*refs):
        a_ref, b_ref = refs[0], refs[1]
        o_ref = refs[-1]
        p = _dot_nt(a_ref[...].astype(BF16), b_ref[...])
        if nk == 1:
            if epilogue is not None:
                p = epilogue(p, refs[2][...])
            o_ref[...] = p.astype(out_dtype)
        else:
            k = pl.program_id(2)

            @pl.when(k == 0)
            def _():
                o_ref[...] = p

            @pl.when(k > 0)
            def _():
                o_ref[...] += p

    return _call(
        body, name=name, grid=(T // tm, N // tn, nk),
        in_specs=in_specs,
        out_specs=pl.BlockSpec((tm, tn), lambda i, j, k: (i, j)),
        out_shape=jax.ShapeDtypeStruct((T, N), out_dtype),
        sem=("parallel", "parallel", "arbitrary"), rides=rides,
    )(*args)


def _matmul_tn(a, b, *, tmo, tn, tk, name, a_prologue=_to_bf16, shards=1, rides=()):
    T, M = a.shape
    N = b.shape[1]
    if shards > 1:
        per = (N // shards) // tn
        out_spec = pl.BlockSpec((None, tmo, tn), lambda i, j, k: (j // per, i, j % per))
        out_shape = jax.ShapeDtypeStruct((shards, M, N // shards), F32)
    else:
        out_spec = pl.BlockSpec((tmo, tn), lambda i, j, k: (i, j))
        out_shape = jax.ShapeDtypeStruct((M, N), F32)

    def body(a_ref, b_ref, o_ref):
        k = pl.program_id(2)
        p = _dot_tn(a_prologue(a_ref[...]), b_ref[...].astype(BF16))

        @pl.when(k == 0)
        def _():
            o_ref[...] = p

        @pl.when(k > 0)
        def _():
            o_ref[...] += p

    return _call(
        body, name=name, grid=(M // tmo, N // tn, T // tk),
        in_specs=[pl.BlockSpec((tk, tmo), lambda i, j, k: (k, i)), pl.BlockSpec((tk, tn), lambda i, j, k: (k, j))],
        out_specs=out_spec, out_shape=out_shape,
        sem=("parallel", "parallel", "arbitrary"), rides=rides,
    )(a, b)


def _loss_bwd(h2, tgt, g, *, tm):
    T, D = h2.shape

    def body(h_ref, t_ref, g_ref, dh_ref, dhb_ref, dg_ref, loss_ref):
        @pl.when(pl.program_id(0) == 0)
        def _():
            dg_ref[...] = jnp.zeros_like(dg_ref)
            loss_ref[...] = jnp.zeros_like(loss_ref)
        h = h_ref[...]
        gg = g_ref[...]
        r = lax.rsqrt(jnp.mean(h * h, axis=-1, keepdims=True) + EPS)
        hn = h * r
        err = hn * gg - t_ref[...]
        loss_ref[...] += 0.5 * jnp.sum(jnp.mean(err * err, axis=-1, keepdims=True), axis=0, keepdims=True)
        dy = err * (1.0 / D)
        dg_ref[...] += jnp.sum(dy * hn, axis=0, keepdims=True)
        w = dy * gg
        dh = r * w - h * ((r * r * r) * jnp.mean(w * h, axis=-1, keepdims=True))
        dh_ref[...] = dh
        dhb_ref[...] = dh.astype(BF16)

    tile = pl.BlockSpec((tm, D), lambda i: (i, 0))
    return pl.pallas_call(
        body, name="loss_bwd", grid=(T // tm,),
        in_specs=[tile, tile, pl.BlockSpec((1, D), lambda i: (0, 0))],
        out_specs=[tile, tile, pl.BlockSpec((1, D), lambda i: (0, 0)), pl.BlockSpec((1, 1), lambda i: (0, 0))],
        out_shape=[jax.ShapeDtypeStruct((T, D), F32), jax.ShapeDtypeStruct((T, D), BF16),
                   jax.ShapeDtypeStruct((1, D), F32), jax.ShapeDtypeStruct((1, 1), F32)],
        compiler_params=_params(("arbitrary",)),
    )(h2, tgt, g)


def _rms_bwd_res(dn, h, g, dres, *, tm, name, rides=()):
    T, D = h.shape

    def body(dn_ref, h_ref, g_ref, dres_ref, dh_ref, dhb_ref, dg_ref):
        @pl.when(pl.program_id(0) == 0)
        def _():
            dg_ref[...] = jnp.zeros_like(dg_ref)
        h_ = h_ref[...]
        dn_ = dn_ref[...]
        dh, r = _rms_bwd(dn_, h_, g_ref[...])
        dg_ref[...] += jnp.sum(dn_ * (h_ * r), axis=0, keepdims=True)
        dh = dres_ref[...] + dh
        dh_ref[...] = dh
        dhb_ref[...] = dh.astype(BF16)

    tile = pl.BlockSpec((tm, D), lambda i: (i, 0))
    return _call(
        body, name=name, grid=(T // tm,),
        in_specs=[tile, tile, pl.BlockSpec((1, D), lambda i: (0, 0)), tile],
        out_specs=[tile, tile, pl.BlockSpec((1, D), lambda i: (0, 0))],
        out_shape=[jax.ShapeDtypeStruct((T, D), F32), jax.ShapeDtypeStruct((T, D), BF16),
                   jax.ShapeDtypeStruct((1, D), F32)],
        sem=("arbitrary",), rides=rides,
    )(dn, h, g, dres)


def _rel_distance():
    i = lax.broadcasted_iota(jnp.int32, (CHUNK, 2 * CHUNK), 0)
    j = lax.broadcasted_iota(jnp.int32, (CHUNK, 2 * CHUNK), 1)
    return i + CHUNK - j


def _bias_build(table):
    def body(tab_ref, o_ref):
        rel = _rel_distance()
        ge = [rel >= t for t in BUCKET_THR]
        for h in range(B_HEADS):
            cur = jnp.full((CHUNK, 2 * CHUNK), tab_ref[0, h], F32)
            for b in range(1, N_BUCKETS):
                cur = jnp.where(ge[b - 1], tab_ref[b, h], cur)
            o_ref[h] = cur

    return pl.pallas_call(
        body, name="bias_build",
        in_specs=[pl.BlockSpec(memory_space=pltpu.SMEM)],
        out_specs=pl.BlockSpec(memory_space=pltpu.VMEM),
        out_shape=jax.ShapeDtypeStruct((B_HEADS, CHUNK, 2 * CHUNK), F32),
    )(table)


def _bias_grad(dbias):
    def body(db_ref, o_ref, acc_ref):
        rel = _rel_distance()
        lo = [0] + BUCKET_THR
        hi = BUCKET_THR + [CHUNK]
        for b in range(N_BUCKETS):
            m = (rel >= lo[b]) & (rel < hi[b])
            for h in range(B_HEADS):
                row = b * B_HEADS + h
                acc_ref[row:row + 1, :] = jnp.sum(jnp.where(m, db_ref[h], 0.0), axis=0, keepdims=True)
        o_ref[...] = jnp.sum(acc_ref[...], axis=1, keepdims=True)

    return pl.pallas_call(
        body, name="bias_grad",
        in_specs=[pl.BlockSpec(memory_space=pltpu.VMEM)],
        out_specs=pl.BlockSpec(memory_space=pltpu.VMEM),
        out_shape=jax.ShapeDtypeStruct((N_BUCKETS * B_HEADS, 1), F32),
        scratch_shapes=[pltpu.VMEM((N_BUCKETS * B_HEADS, 2 * CHUNK), F32)],
    )(dbias)


def _causal_mask():
    t = lax.broadcasted_iota(jnp.int32, (CHUNK, CHUNK), 0)
    s = lax.broadcasted_iota(jnp.int32, (CHUNK, CHUNK), 1)
    return s <= t


def _band_mask(n):
    rel = _rel_distance()
    j = lax.broadcasted_iota(jnp.int32, (CHUNK, 2 * CHUNK), 1)
    return (rel >= 0) & (rel < CHUNK) & ((n > 0) | (j >= CHUNK))


def _gate_forward(u, v, lg, lb, wc, bs):
    ug = _gelu(u)
    vg = _gelu(v)
    mu = jnp.mean(vg, axis=-1, keepdims=True)
    xc = vg - mu
    rstd = lax.rsqrt(jnp.mean(xc * xc, axis=-1, keepdims=True) + EPS)
    xhat = xc * rstd
    vl = (xhat * lg + lb).astype(BF16)
    mixed = _dot(wc, vl) + bs
    return ug, xhat, rstd, vl, mixed


def _softmax_scores(qk, bias, mask, sink):
    s = qk * SCALE + bias
    s = jnp.where(mask, s, NEG)
    m = jnp.maximum(jnp.max(s, axis=-1, keepdims=True), sink)
    p = jnp.exp(s - m)
    e_sink = jnp.exp(sink - m)
    inv = 1.0 / (jnp.sum(p, axis=-1, keepdims=True) + e_sink)
    return p * inv, e_sink * inv


PAIRS = Q_PER_KV // 2


def _head(g, pr, e):
    return g * Q_PER_KV + 2 * pr + e


def _stack_pairs(ref, g, col0=0):
    w = 2 * HEAD_DIM
    return jnp.concatenate([ref[:, col0 + (g * PAIRS + pr) * w:col0 + (g * PAIRS + pr + 1) * w] for pr in range(PAIRS)],
                           axis=0)


def _low_lanes():
    return lax.broadcasted_iota(jnp.int32, (2 * CHUNK, 2 * HEAD_DIM), 1) < HEAD_DIM


def _band_operands(kv_prev, kv_cur):
    band = jnp.concatenate([kv_prev, kv_cur], axis=0)
    low = _low_lanes()
    ops = []
    for cat in (band[:, :KV_WIDTH], band[:, KV_WIDTH:]):
        rol = pltpu.roll(cat, HEAD_DIM, 1)
        ops.append([[jnp.where(low if e == 0 else ~low, cat if g == e else rol, 0.0).astype(BF16) for e in range(2)]
                    for g in range(2)])
    return ops


def _mixer_fwd(proj, lg, lb, wsp, bs_col, sinks, bias, ga, gb, rides=()):
    T = proj.shape[0]
    nb = T // CHUNK

    def body(u_ref, v_ref, q_ref, kvc_ref, kvp_ref, lg_ref, lb_ref, w_ref, bs_ref, sink_ref, bias_ref,
             ga_ref, gb_ref, mixed_ref, mixed_t_ref, ab_ref):
        n = pl.program_id(0)
        causal = _causal_mask()
        ssq = jnp.zeros((CHUNK, 1), F32)
        for g in range(A_GROUPS):
            cols = slice(g * CHUNK, (g + 1) * CHUNK)
            wc = jnp.where(causal, w_ref[g], 0.0).astype(BF16)
            ug, _, _, _, mixed = _gate_forward(u_ref[:, cols], v_ref[:, cols], lg_ref[g:g + 1, :], lb_ref[g:g + 1, :],
                                               wc, bs_ref[g])
            a = ug * mixed
            ab_ref[:, cols] = a
            ssq = ssq + jnp.sum(a * a, axis=-1, keepdims=True)
        ra = lax.rsqrt(ssq * (1.0 / A_WIDTH) + EPS)
        mixed_ref[:, :A_WIDTH] = ((ab_ref[:, :A_WIDTH] * ra) * ga_ref[...]).astype(BF16)

        mask = _band_mask(n)
        kops, vops = _band_operands(kvp_ref[...], kvc_ref[...])
        ssq = jnp.zeros((CHUNK, 1), F32)
        for g in range(B_HEADS // Q_PER_KV):
            qst = _stack_pairs(q_ref, g).astype(BF16)
            o_st = jnp.zeros((PAIRS * CHUNK, 2 * HEAD_DIM), F32)
            for e in range(2):
                s_all = _dot_nt(qst, kops[g][e])
                ps = []
                for pr in range(PAIRS):
                    h = _head(g, pr, e)
                    p, _ = _softmax_scores(s_all[pr * CHUNK:(pr + 1) * CHUNK], bias_ref[h], mask, sink_ref[0, h])
                    ps.append(p.astype(BF16))
                o_st = o_st + _dot(jnp.concatenate(ps, axis=0), vops[g][e])
            for pr in range(PAIRS):
                o = o_st[pr * CHUNK:(pr + 1) * CHUNK]
                c0 = A_WIDTH + (g * PAIRS + pr) * 2 * HEAD_DIM
                ab_ref[:, c0:c0 + 2 * HEAD_DIM] = o
                ssq = ssq + jnp.sum(o * o, axis=-1, keepdims=True)
        rb = lax.rsqrt(ssq * (1.0 / B_WIDTH) + EPS)
        mixed_ref[:, A_WIDTH:] = ((ab_ref[:, A_WIDTH:] * rb) * gb_ref[...]).astype(BF16)
        mixed_t_ref[...] = mixed_ref[...].T

    full = lambda *shape: pl.BlockSpec(shape, lambda n: (0,) * len(shape))
    return _call(
        body, name="mixer_fwd", grid=(nb,),
        in_specs=[pl.BlockSpec((CHUNK, A_WIDTH), lambda n: (n, 0)),
                  pl.BlockSpec((CHUNK, A_WIDTH), lambda n: (n, 1)),
                  pl.BlockSpec((CHUNK, B_WIDTH), lambda n: (n, 2)),
                  pl.BlockSpec((CHUNK, 2 * KV_WIDTH), lambda n: (n, 12)),
                  pl.BlockSpec((CHUNK, 2 * KV_WIDTH), lambda n: (jnp.maximum(n - 1, 0), 12)),
                  full(A_GROUPS, CHUNK), full(A_GROUPS, CHUNK), full(A_GROUPS, CHUNK, CHUNK), full(A_GROUPS, CHUNK, 1),
                  pl.BlockSpec(memory_space=pltpu.SMEM), full(B_HEADS, CHUNK, 2 * CHUNK),
                  full(1, A_WIDTH), full(1, B_WIDTH)],
        out_specs=[pl.BlockSpec((CHUNK, D_MODEL), lambda n: (n, 0)), pl.BlockSpec((D_MODEL, CHUNK), lambda n: (0, n)),
                   pl.BlockSpec((CHUNK, D_MODEL), lambda n: (n, 0))],
        out_shape=[jax.ShapeDtypeStruct((T, D_MODEL), BF16), jax.ShapeDtypeStruct((D_MODEL, T), BF16),
                   jax.ShapeDtypeStruct((T, D_MODEL), F32)],
        sem=("parallel",), rides=rides,
    )(proj, proj, proj, proj, proj, lg, lb, wsp, bs_col, sinks, bias, ga, gb)


def _gmlp_bwd(proj, ab, dmixed, ga, lg, lb, wsp, bs_col, rides=()):
    T = proj.shape[0]
    nb = T // CHUNK

    def body(u_ref, v_ref, a_ref, dna_ref, ga_ref, lg_ref, lb_ref, w_ref, bs_ref,
             dp_ref, dga_ref, dw_ref, dbs_ref, dlg_ref, dlb_ref):
        @pl.when(pl.program_id(0) == 0)
        def _():
            for r in (dga_ref, dw_ref, dbs_ref, dlg_ref, dlb_ref):
                r[...] = jnp.zeros_like(r)
        causal = _causal_mask()
        a_all = a_ref[...]
        dna = dna_ref[...]
        da_all, ra = _rms_bwd(dna, a_all, ga_ref[...])
        dga_ref[...] += jnp.sum(dna * (a_all * ra), axis=0, keepdims=True)
        for g in range(A_GROUPS):
            cols = slice(g * CHUNK, (g + 1) * CHUNK)
            wc = jnp.where(causal, w_ref[g], 0.0).astype(BF16)
            lgg = lg_ref[g:g + 1, :]
            u = u_ref[:, cols]
            v = v_ref[:, cols]
            ug, xhat, rstd, vl, mixed = _gate_forward(u, v, lgg, lb_ref[g:g + 1, :], wc, bs_ref[g])
            da = da_all[:, cols]
            dug = da * mixed
            dmg = da * ug
            dmg_b = dmg.astype(BF16)
            dbs_ref[g] += jnp.sum(dmg, axis=-1, keepdims=True)
            dw_ref[g] += jnp.where(causal, _dot_nt(dmg_b, vl), 0.0)
            dvl = _dot_tn(wc, dmg_b)
            dlg_ref[g:g + 1, :] += jnp.sum(dvl * xhat, axis=0, keepdims=True)
            dlb_ref[g:g + 1, :] += jnp.sum(dvl, axis=0, keepdims=True)
            dxh = dvl * lgg
            dvg = rstd * (dxh - jnp.mean(dxh, axis=-1, keepdims=True)
                          - xhat * jnp.mean(dxh * xhat, axis=-1, keepdims=True))
            _, gu = _gelu_and_grad(u)
            _, gv = _gelu_and_grad(v)
            dp_ref[:, cols] = (dug * gu).astype(BF16)
            dp_ref[:, A_WIDTH + g * CHUNK:A_WIDTH + (g + 1) * CHUNK] = (dvg * gv).astype(BF16)

    full = lambda *shape: pl.BlockSpec(shape, lambda n: (0,) * len(shape))
    return _call(
        body, name="gmlp_bwd", grid=(nb,),
        in_specs=[pl.BlockSpec((CHUNK, A_WIDTH), lambda n: (n, 0)),
                  pl.BlockSpec((CHUNK, A_WIDTH), lambda n: (n, 1)),
                  pl.BlockSpec((CHUNK, A_WIDTH), lambda n: (n, 0)),
                  pl.BlockSpec((CHUNK, A_WIDTH), lambda n: (n, 0)),
                  full(1, A_WIDTH), full(A_GROUPS, CHUNK), full(A_GROUPS, CHUNK), full(A_GROUPS, CHUNK, CHUNK),
                  full(A_GROUPS, CHUNK, 1)],
        out_specs=[pl.BlockSpec((CHUNK, 2 * A_WIDTH), lambda n: (n, 0)),
                   full(1, A_WIDTH), full(A_GROUPS, CHUNK, CHUNK), full(A_GROUPS, CHUNK, 1),
                   full(A_GROUPS, CHUNK), full(A_GROUPS, CHUNK)],
        out_shape=[jax.ShapeDtypeStruct((T, 2 * A_WIDTH), BF16),
                   jax.ShapeDtypeStruct((1, A_WIDTH), F32), jax.ShapeDtypeStruct((A_GROUPS, CHUNK, CHUNK), F32),
                   jax.ShapeDtypeStruct((A_GROUPS, CHUNK, 1), F32), jax.ShapeDtypeStruct((A_GROUPS, CHUNK), F32),
                   jax.ShapeDtypeStruct((A_GROUPS, CHUNK), F32)],
        sem=("arbitrary",), rides=rides,
    )(proj, proj, ab, dmixed, ga, lg, lb, wsp, bs_col)


def _attn_bwd(proj, ab, dmixed, gb, sinks, bias, rides=()):
    T = proj.shape[0]
    nb = T // CHUNK
    qn = lambda n: jnp.minimum(n, nb - 1)

    def body(q_ref, kvc_ref, kvp_ref, o_ref, dnb_ref, gb_ref, sink_ref, bias_ref,
             dq_ref, dkv_ref, dgb_ref, dsink_ref, dbias_ref, carry_ref, sacc_ref):
        n = pl.program_id(0)

        @pl.when(n == 0)
        def _():
            carry_ref[...] = jnp.zeros_like(carry_ref)
            sacc_ref[...] = jnp.zeros_like(sacc_ref)
            dgb_ref[...] = jnp.zeros_like(dgb_ref)
            dbias_ref[...] = jnp.zeros_like(dbias_ref)

        @pl.when(n < nb)
        def _():
            mask = _band_mask(n)
            o_all = o_ref[...]
            dnb = dnb_ref[...]
            do_all, rb = _rms_bwd(dnb, o_all, gb_ref[...])
            dgb_ref[...] += jnp.sum(dnb * (o_all * rb), axis=0, keepdims=True)
            kops, vops = _band_operands(kvp_ref[...], kvc_ref[...])
            low = _low_lanes()
            halves = []
            for g in range(B_HEADS // Q_PER_KV):
                qst = _stack_pairs(q_ref, g).astype(BF16)
                dost = _stack_pairs(do_all, g).astype(BF16)
                dq_st = jnp.zeros((PAIRS * CHUNK, 2 * HEAD_DIM), F32)
                dk_e, dv_e = [], []
                for e in range(2):
                    s_all = _dot_nt(qst, kops[g][e])
                    dp_all = _dot_nt(dost, vops[g][e])
                    ps, dsrs = [], []
                    for pr in range(PAIRS):
                        h = _head(g, pr, e)
                        rows = slice(pr * CHUNK, (pr + 1) * CHUNK)
                        p, p_sink = _softmax_scores(s_all[rows], bias_ref[h], mask, sink_ref[0, h])
                        dp = dp_all[rows]
                        delta = jnp.sum(p * dp, axis=-1, keepdims=True)
                        ds = p * (dp - delta)
                        sacc_ref[:, h:h + 1] += -(p_sink * delta)
                        dbias_ref[h] += ds
                        ps.append(p.astype(BF16))
                        dsrs.append((ds * SCALE).astype(BF16))
                    dsr_all = jnp.concatenate(dsrs, axis=0)
                    dq_st = dq_st + _dot(dsr_all, kops[g][e])
                    dk_e.append(_dot_tn(dsr_all, qst))
                    dv_e.append(_dot_tn(jnp.concatenate(ps, axis=0), dost))
                for pr in range(PAIRS):
                    c0 = (g * PAIRS + pr) * 2 * HEAD_DIM
                    dq_ref[:, c0:c0 + 2 * HEAD_DIM] = dq_st[pr * CHUNK:(pr + 1) * CHUNK].astype(BF16)
                halves.append((dk_e, dv_e))
            tiles = []
            for t in range(2):
                g0, g1 = halves[0][t], halves[1][t]
                tiles.append(jnp.where(low, g0[0] + pltpu.roll(g0[1], HEAD_DIM, 1), pltpu.roll(g1[0], HEAD_DIM, 1) + g1[1]))
            dband = jnp.concatenate(tiles, axis=1)
            dkv_ref[...] = (carry_ref[...] + dband[:CHUNK]).astype(BF16)
            carry_ref[...] = dband[CHUNK:]

        @pl.when(n == nb)
        def _():
            dkv_ref[...] = carry_ref[...].astype(BF16)
            dsink_ref[...] = jnp.sum(sacc_ref[...], axis=0, keepdims=True)

    full = lambda *shape: pl.BlockSpec(shape, lambda n: (0,) * len(shape))
    return _call(
        body, name="attn_bwd", grid=(nb + 1,),
        in_specs=[pl.BlockSpec((CHUNK, B_WIDTH), lambda n: (qn(n), 2)),
                  pl.BlockSpec((CHUNK, 2 * KV_WIDTH), lambda n: (qn(n), 12)),
                  pl.BlockSpec((CHUNK, 2 * KV_WIDTH), lambda n: (jnp.maximum(qn(n) - 1, 0), 12)),
                  pl.BlockSpec((CHUNK, B_WIDTH), lambda n: (qn(n), 1)),
                  pl.BlockSpec((CHUNK, B_WIDTH), lambda n: (qn(n), 1)),
                  full(1, B_WIDTH), pl.BlockSpec(memory_space=pltpu.SMEM), full(B_HEADS, CHUNK, 2 * CHUNK)],
        out_specs=[pl.BlockSpec((CHUNK, B_WIDTH), lambda n: (qn(n), 0)),
                   pl.BlockSpec((CHUNK, 2 * KV_WIDTH), lambda n: (jnp.maximum(n - 1, 0), 0)),
                   full(1, B_WIDTH), full(1, B_HEADS), full(B_HEADS, CHUNK, 2 * CHUNK)],
        out_shape=[jax.ShapeDtypeStruct((T, B_WIDTH), BF16), jax.ShapeDtypeStruct((T, 2 * KV_WIDTH), BF16),
                   jax.ShapeDtypeStruct((1, B_WIDTH), F32), jax.ShapeDtypeStruct((1, B_HEADS), F32),
                   jax.ShapeDtypeStruct((B_HEADS, CHUNK, 2 * CHUNK), F32)],
        scratch_shapes=[pltpu.VMEM((CHUNK, 2 * KV_WIDTH), F32), pltpu.VMEM((CHUNK, B_HEADS), F32)],
        sem=("arbitrary",), rides=rides,
    )(proj, proj, proj, ab, dmixed, gb, sinks, bias)


def _sq_relu_grad(acc, z):
    return acc * (2.0 * jnp.maximum(z, 0.0))


def _local_step(x, tgt, sp, win, wo, wu, wd):
    T = x.shape[0]
    tm = min(512, T)
    tk = min(512, T)
    lg = sp["gate_norm_g"].reshape(A_GROUPS, CHUNK)
    lb = sp["gate_norm_b"].reshape(A_GROUPS, CHUNK)
    wsp = sp["w_spatial"].reshape(A_GROUPS, CHUNK, CHUNK)
    bs_col = sp["b_spatial"].reshape(A_GROUPS, CHUNK, 1)
    sinks = sp["attn_sinks"].reshape(1, B_HEADS)
    ga = sp["out_norm_a_g"].reshape(1, A_WIDTH)
    gb = sp["out_norm_b_g"].reshape(1, B_WIDTH)
    g1 = sp["mix_norm_g"].reshape(1, D_MODEL)
    g2 = sp["ffn_norm_g"].reshape(1, D_MODEL)
    gf = sp["final_norm_g"].reshape(1, D_MODEL)

    bias = _bias_build(sp["rel_bias_table"])
    n1, proj = _norm_matmul(x, g1, win, tm=tm, tn=PROJ_WIDTH // 2, name="in_proj")
    mixed, ab = _mixer_fwd(proj, lg, lb, wsp, bs_col, sinks, bias, ga, gb)
    h1 = _matmul_res(mixed, wo, x, tm=tm, tn=1024, tk=D_MODEL, prologue=_to_bf16, name="out_proj")
    n2, zp = _norm_matmul(h1, g2, wu, tm=tm, tn=1024, name="up_proj")
    h2 = _matmul_res(zp, wd, h1, tm=tm, tn=1024, tk=2048, prologue=_sq_relu_bf16, name="down_proj")

    dh2, dgf, loss = _loss_bwd(h2, tgt, gf, tm=tm)
    dzp = _matmul_nt(dh2, wd, tm=tm, tn=1024, tk=D_MODEL, name="bwd_dz", extra=zp, epilogue=_sq_relu_grad,
                     out_dtype=BF16)
    dwd = _matmul_tn(zp, dh2, tmo=1024, tn=1024, tk=tk, name="grad_w_down", a_prologue=_sq_relu_bf16)
    dwu = _matmul_tn(n2, dzp, tmo=1024, tn=1024, tk=tk, name="grad_w_up", shards=N_CHIPS)
    dn2 = _matmul_nt(dzp, wu, tm=tm, tn=1024, tk=2048, name="bwd_dn2")
    dh1, dg2 = _rms_bwd_res(dn2, h1, g2, dh2, tm=tm, name="ffn_norm_bwd")
    dwo = _matmul_tn(mixed, dh1, tmo=1024, tn=1024, tk=tk, name="grad_w_out")
    dmixed = _matmul_nt(dh1, wo, tm=tm, tn=1024, tk=D_MODEL, name="bwd_dmixed")
    duv, dga, dwsp, dbs, dlg, dlb = _gmlp_bwd(proj, ab, dmixed, ga, lg, lb, wsp, bs_col)
    dq, dkv, dgb, dsinks, dbias = _attn_bwd(proj, ab, dmixed, gb, sinks, bias)
    dtable = _bias_grad(dbias)
    dproj = jnp.concatenate([duv, dq, dkv], axis=1)
    dwin = _matmul_tn(n1, dproj, tmo=1024, tn=PROJ_WIDTH // 2, tk=tk, name="grad_w_in")
    dn1 = _matmul_nt(dproj, win, tm=tm, tn=1024, tk=PROJ_WIDTH, name="bwd_dn1")
    dx, dg1 = _rms_bwd_res(dn1, x, g1, dh1, tm=tm, name="mix_norm_bwd")

    small = {
        "rel_bias_table": dtable.reshape(N_BUCKETS, B_HEADS), "mix_norm_g": dg1, "gate_norm_g": dlg, "gate_norm_b": dlb,
        "w_spatial": dwsp, "b_spatial": dbs, "attn_sinks": dsinks, "out_norm_a_g": dga, "out_norm_b_g": dgb,
        "ffn_norm_g": dg2, "final_norm_g": dgf,
    }
    return loss, dx, (dwin, dwo, dwu, dwd), small


def _place():
    x, y, c = lax.axis_index("x"), lax.axis_index("y"), lax.axis_index("c")
    chips = [(1 - x, y), (x, 1 - y), (1 - x, 1 - y)]
    return x, y, c, chips


def _remote(src, dst, send_sem, recv_sem, to):
    return pltpu.make_async_remote_copy(src_ref=src, dst_ref=dst, send_sem=send_sem, recv_sem=recv_sem,
                                        device_id=to, device_id_type=MESH)


def _core_index():
    return lax.axis_index("c").astype(jnp.int32).reshape(1)


def _chip_index():
    return (2 * lax.axis_index("x") + lax.axis_index("y")).astype(jnp.int32).reshape(1)


def _cast_into_slot(w, *, tm, name):
    _, R, C = w.shape

    def body(me_ref, w_ref, o_ref):
        del me_ref
        o_ref[...] = w_ref[...].astype(BF16)

    return pl.pallas_call(
        body, name=name,
        grid_spec=pltpu.PrefetchScalarGridSpec(
            num_scalar_prefetch=1, grid=(R // tm,),
            in_specs=[pl.BlockSpec((None, tm, C), lambda i, me: (0, i, 0))],
            out_specs=pl.BlockSpec((None, tm, C), lambda i, me: (me[0], i, 0))),
        out_shape=jax.ShapeDtypeStruct((N_CHIPS, R, C), BF16), compiler_params=_params(("parallel",)),
    )(_chip_index(), w)


def _gather_weights(slots):
    nw = len(slots)

    def body(*refs):
        fulls = refs[nw:2 * nw]
        send_sems, recv_sems = refs[2 * nw:]
        x, y, c, chips = _place()
        me = 2 * x + y
        sends = []
        for w in range(nw):
            hr = fulls[w].shape[1] // 2
            rows = pl.ds(c * hr, hr)
            for j, chip in enumerate(chips):
                mine = fulls[w].at[me, rows, :]
                cp = _remote(mine, mine, send_sems.at[6 * w + j], recv_sems.at[6 * w + j], (*chip, c))
                cp.start()
                sends.append(cp)
        for w in range(nw):
            hr = fulls[w].shape[1] // 2
            rows = pl.ds(c * hr, hr)
            for j, chip in enumerate(chips):
                landed = fulls[w].at[2 * chip[0] + chip[1], rows, :]
                _remote(landed, landed, send_sems.at[6 * w + j], recv_sems.at[6 * w + j], (x, y, c)).wait_recv()
                cp = _remote(landed, landed, send_sems.at[6 * w + 3 + j], recv_sems.at[6 * w + 3 + j], (x, y, 1 - c))
                cp.start()
                sends.append(cp)
        for w in range(nw):
            hr = fulls[w].shape[1] // 2
            rows = pl.ds((1 - c) * hr, hr)
            for j, chip in enumerate(chips):
                other = fulls[w].at[2 * chip[0] + chip[1], rows, :]
                _remote(other, other, send_sems.at[6 * w + 3 + j], recv_sems.at[6 * w + 3 + j], (x, y, c)).wait_recv()
        for cp in sends:
            cp.wait_send()

    any_spec = pl.BlockSpec(memory_space=pl.ANY)
    return pl.pallas_call(
        body, name="gather_weights",
        in_specs=[any_spec] * nw, out_specs=[any_spec] * nw,
        out_shape=[jax.ShapeDtypeStruct(s.shape, s.dtype) for s in slots],
        scratch_shapes=[pltpu.SemaphoreType.DMA((6 * nw,)), pltpu.SemaphoreType.DMA((6 * nw,))],
        input_output_aliases={w: w for w in range(nw)},
    )(*slots)


def _sibling_halves(grads):
    nw = len(grads)

    def body(*refs):
        gs, outs = refs[:nw], refs[nw:2 * nw]
        send_sems, recv_sems = refs[2 * nw:]
        x, y, c, _ = _place()
        cps = []
        for w in range(nw):
            hr = gs[w].shape[1] // 2
            cp = _remote(gs[w].at[:, pl.ds((1 - c) * hr, hr), :], outs[w], send_sems.at[w], recv_sems.at[w],
                         (x, y, 1 - c))
            cp.start()
            cps.append(cp)
        for cp in cps:
            cp.wait()

    any_spec = pl.BlockSpec(memory_space=pl.ANY)
    return pl.pallas_call(
        body, name="rs_sibling_halves",
        in_specs=[any_spec] * nw, out_specs=[any_spec] * nw,
        out_shape=[jax.ShapeDtypeStruct((g.shape[0], g.shape[1] // 2, g.shape[2]), g.dtype) for g in grads],
        scratch_shapes=[pltpu.SemaphoreType.DMA((nw,)), pltpu.SemaphoreType.DMA((nw,))],
    )(*grads)


def _pair_sum_bf16(g, got, *, tm, name):
    S, R, C = g.shape
    hr = R // 2
    nt = hr // tm

    def body(c_ref, g_ref, got_ref, o_ref):
        del c_ref
        o_ref[...] = (g_ref[...] + got_ref[...]).astype(BF16)

    return pl.pallas_call(
        body, name=name,
        grid_spec=pltpu.PrefetchScalarGridSpec(
            num_scalar_prefetch=1, grid=(S, nt),
            in_specs=[pl.BlockSpec((None, tm, C), lambda s, i, c: (s, c[0] * nt + i, 0)),
                      pl.BlockSpec((None, tm, C), lambda s, i, c: (s, i, 0))],
            out_specs=pl.BlockSpec((None, tm, C), lambda s, i, c: (s, i, 0))),
        out_shape=jax.ShapeDtypeStruct((S, hr, C), BF16),
        compiler_params=_params(("parallel", "parallel")),
    )(_core_index(), g, got)


def _scatter_to_owners(pairs):
    nw = len(pairs)

    def body(*refs):
        qs, outs = refs[:nw], refs[nw:2 * nw]
        send_sems, recv_sems = refs[2 * nw:]
        x, y, c, chips = _place()
        cps = []
        for w in range(nw):
            for j, chip in enumerate(chips):
                cp = _remote(qs[w].at[2 * chip[0] + chip[1]], outs[w].at[j], send_sems.at[3 * w + j],
                             recv_sems.at[3 * w + j], (*chip, c))
                cp.start()
                cps.append(cp)
        for cp in cps:
            cp.wait()

    any_spec = pl.BlockSpec(memory_space=pl.ANY)
    return pl.pallas_call(
        body, name="rs_scatter_to_owners",
        in_specs=[any_spec] * nw, out_specs=[any_spec] * nw,
        out_shape=[jax.ShapeDtypeStruct((3,) + q.shape[1:], q.dtype) for q in pairs],
        scratch_shapes=[pltpu.SemaphoreType.DMA((3 * nw,)), pltpu.SemaphoreType.DMA((3 * nw,))],
    )(*pairs)


def _owner_sum(g, got, others, *, tm, name):
    S, R, C = g.shape
    hr = R // 2
    nt = hr // tm

    def body(idx_ref, g_ref, got_ref, o_ref_in, out_ref):
        del idx_ref
        acc = g_ref[...] + got_ref[...]
        for j in range(3):
            acc = acc + o_ref_in[j].astype(F32)
        out_ref[...] = acc

    return pl.pallas_call(
        body, name=name,
        grid_spec=pltpu.PrefetchScalarGridSpec(
            num_scalar_prefetch=1, grid=(nt,),
            in_specs=[pl.BlockSpec((None, tm, C), lambda i, p: (p[1], p[0] * nt + i, 0)),
                      pl.BlockSpec((None, tm, C), lambda i, p: (p[1], i, 0)),
                      pl.BlockSpec((3, tm, C), lambda i, p: (0, i, 0))],
            out_specs=pl.BlockSpec((tm, C), lambda i, p: (i, 0))),
        out_shape=jax.ShapeDtypeStruct((hr, C), F32),
        compiler_params=_params(("parallel",)),
    )(jnp.concatenate([_core_index(), _chip_index()]), g, got, others)


def _swap_halves(halves):
    nw = len(halves)

    def body(*refs):
        hs, outs = refs[:nw], refs[nw:2 * nw]
        send_sems, recv_sems = refs[2 * nw:]
        x, y, c, _ = _place()
        cps = []
        for w in range(nw):
            cp = _remote(hs[w], outs[w], send_sems.at[w], recv_sems.at[w], (x, y, 1 - c))
            cp.start()
            cps.append(cp)
        for cp in cps:
            cp.wait()

    any_spec = pl.BlockSpec(memory_space=pl.ANY)
    return pl.pallas_call(
        body, name="rs_swap_halves",
        in_specs=[any_spec] * nw, out_specs=[any_spec] * nw,
        out_shape=[jax.ShapeDtypeStruct(h.shape, h.dtype) for h in halves],
        scratch_shapes=[pltpu.SemaphoreType.DMA((nw,)), pltpu.SemaphoreType.DMA((nw,))],
    )(*halves)


def _all_reduce_small(packed):
    R, C = packed.shape

    def body(in_ref, out_ref, slots, send_sems, recv_sems):
        x, y, c, _ = _place()
        me = 4 * x + 2 * y + c
        cps = []
        for k in range(1, N_DEV):
            p = (me + k) % N_DEV
            cp = _remote(in_ref, slots.at[me], send_sems.at[k - 1], recv_sems.at[k - 1], (p // 4, (p // 2) % 2, p % 2))
            cp.start()
            cps.append(cp)
        slots[me] = in_ref[...]
        for k in range(1, N_DEV):
            src = (me + N_DEV - k) % N_DEV
            _remote(in_ref, slots.at[src], send_sems.at[k - 1], recv_sems.at[k - 1], (x, y, c)).wait_recv()
        for cp in cps:
            cp.wait_send()
        acc = slots[0]
        for d in range(1, N_DEV):
            acc = acc + slots[d]
        out_ref[...] = acc

    vmem = pl.BlockSpec(memory_space=pltpu.VMEM)
    return pl.pallas_call(
        body, name="all_reduce_small", in_specs=[vmem], out_specs=vmem,
        out_shape=jax.ShapeDtypeStruct((R, C), F32),
        scratch_shapes=[pltpu.VMEM((N_DEV, R, C), F32), pltpu.SemaphoreType.DMA((N_DEV - 1,)),
                        pltpu.SemaphoreType.DMA((N_DEV - 1,))],
        compiler_params=_params(),
    )(packed)


def _adamw_math(w, g, m, v):
    m = ADAM_B1 * m + (1.0 - ADAM_B1) * g
    v = ADAM_B2 * v + (1.0 - ADAM_B2) * (g * g)
    m_hat = m / (1.0 - ADAM_B1 ** ADAM_STEP)
    v_hat = v / (1.0 - ADAM_B2 ** ADAM_STEP)
    delta = -ADAM_LR * (m_hat / (jnp.sqrt(v_hat) + ADAM_EPS) + ADAM_WD * w)
    return delta, m, v


def _adamw(w, g, m, v, *, tm, name):
    R, C = w.shape

    def body(w_ref, g_ref, m_ref, v_ref, d_ref, nm_ref, nv_ref):
        d_ref[...], nm_ref[...], nv_ref[...] = _adamw_math(w_ref[...], g_ref[...], m_ref[...], v_ref[...])

    spec = pl.BlockSpec((tm, C), lambda i: (i, 0))
    return pl.pallas_call(
        body, name=name, grid=(R // tm,), in_specs=[spec] * 4, out_specs=[spec] * 3,
        out_shape=[jax.ShapeDtypeStruct((R, C), F32)] * 3, compiler_params=_params(("parallel",)),
    )(w, g, m, v)


def _adamw_halves(w, own, got, m, v, *, tm, name, rides=()):
    _, R, C = w.shape
    nt = (R // 2) // tm

    def body(w_ref, own_ref, got_ref, m_ref, v_ref, g_ref, d_ref, nm_ref, nv_ref):
        g = jnp.where(pl.program_id(0) == lax.axis_index("c"), own_ref[...], got_ref[...])
        g_ref[...] = g
        d_ref[...], nm_ref[...], nv_ref[...] = _adamw_math(w_ref[...], g, m_ref[...], v_ref[...])

    whole = pl.BlockSpec((None, tm, C), lambda h, i: (0, h * nt + i, 0))
    half = pl.BlockSpec((tm, C), lambda h, i: (i, 0))
    return _call(
        body, name=name, grid=(2, nt), in_specs=[whole, half, half, whole, whole], out_specs=[whole] * 4,
        out_shape=[jax.ShapeDtypeStruct((1, R, C), F32)] * 4, sem=("parallel", "parallel"), rides=rides,
    )(w, own, got, m, v)


def _adamw_small(w, slots, m, v, *, name):
    def body(w_ref, slots_ref, m_ref, v_ref, g_ref, d_ref, nm_ref, nv_ref):
        g = slots_ref[0]
        for d in range(1, N_DEV):
            g = g + slots_ref[d]
        g_ref[...] = g
        d_ref[...], nm_ref[...], nv_ref[...] = _adamw_math(w_ref[...], g, m_ref[...], v_ref[...])

    vmem = pl.BlockSpec(memory_space=pltpu.VMEM)
    return pl.pallas_call(
        body, name=name, in_specs=[vmem] * 4, out_specs=[vmem] * 4,
        out_shape=[jax.ShapeDtypeStruct(w.shape, F32)] * 4, compiler_params=_params(),
    )(w, slots, m, v)


SMALL = ["rel_bias_table", "mix_norm_g", "gate_norm_g", "gate_norm_b", "w_spatial", "b_spatial", "attn_sinks",
         "out_norm_a_g", "out_norm_b_g", "ffn_norm_g", "final_norm_g"]
SMALL_A = ["gate_norm_g", "gate_norm_b", "w_spatial", "b_spatial", "out_norm_a_g"]
SMALL_B = ["rel_bias_table", "mix_norm_g", "attn_sinks", "out_norm_b_g", "ffn_norm_g", "final_norm_g"]
LARGE = ["w_in", "w_out", "w_up", "w_down"]
WEIGHTS = ["rel_bias_table", "mix_norm_g", "w_in", "gate_norm_g", "gate_norm_b", "w_spatial", "b_spatial", "attn_sinks",
           "out_norm_a_g", "out_norm_b_g", "w_out", "ffn_norm_g", "w_up", "w_down", "final_norm_g"]
PACK_UNIT = 8 * 128


def _pack(parts):
    rows = []
    for p in parts:
        flat = p.reshape(-1)
        pad = (-flat.shape[0]) % PACK_UNIT
        rows.append(jnp.pad(flat, (0, pad)).reshape(-1, 128))
    return jnp.concatenate(rows, axis=0)


def _unpack(packed, like):
    out, row = [], 0
    for p in like:
        n = math.prod(p.shape)
        nrows = (n + PACK_UNIT - 1) // PACK_UNIT * 8
        out.append(packed[row:row + nrows].reshape(-1)[:n].reshape(p.shape))
        row += nrows
    return out


def kernel(x, rel_bias_table, mix_norm_g, w_in, gate_norm_g, gate_norm_b, w_spatial, b_spatial, attn_sinks, out_norm_a_g, out_norm_b_g, w_out, ffn_norm_g, w_up, w_down, final_norm_g, loss_target, m_rel_bias_table, m_mix_norm_g, m_w_in, m_gate_norm_g, m_gate_norm_b, m_w_spatial, m_b_spatial, m_attn_sinks, m_out_norm_a_g, m_out_norm_b_g, m_w_out, m_ffn_norm_g, m_w_up, m_w_down, m_final_norm_g, v_rel_bias_table, v_mix_norm_g, v_w_in, v_gate_norm_g, v_gate_norm_b, v_w_spatial, v_b_spatial, v_attn_sinks, v_out_norm_a_g, v_out_norm_b_g, v_w_out, v_ffn_norm_g, v_w_up, v_w_down, v_final_norm_g):
    args = dict(locals())
    wts = {n: args[n] for n in WEIGHTS}
    mom = {n: args["m_" + n] for n in WEIGHTS}
    var = {n: args["v_" + n] for n in WEIGHTS}
    sp = {n: wts[n] for n in SMALL}
    x2, tgt = x[0], loss_target[0]
    T = x2.shape[0]
    tm = min(512, T)
    tl = min(1024, T)
    lg = sp["gate_norm_g"].reshape(A_GROUPS, CHUNK)
    lb = sp["gate_norm_b"].reshape(A_GROUPS, CHUNK)
    wsp = sp["w_spatial"].reshape(A_GROUPS, CHUNK, CHUNK)
    bs_col = sp["b_spatial"].reshape(A_GROUPS, CHUNK, 1)
    sinks = sp["attn_sinks"].reshape(1, B_HEADS)
    ga = sp["out_norm_a_g"].reshape(1, A_WIDTH)
    gb = sp["out_norm_b_g"].reshape(1, B_WIDTH)
    g1 = sp["mix_norm_g"].reshape(1, D_MODEL)
    g2 = sp["ffn_norm_g"].reshape(1, D_MODEL)
    gf = sp["final_norm_g"].reshape(1, D_MODEL)

    def pair_sum(n, g, got):
        return _pair_sum_bf16(g, got, tm=256, name="rs_pair_sum_" + n)

    def owner_sum(n, g, got, others):
        return _owner_sum(g, got, others, tm=256, name="rs_owner_sum_" + n)

    s_in, s_out, s_up, s_down = [_cast_into_slot(wts[n], tm=256, name="cast_" + n) for n in LARGE]
    ((g_in,),) = _carrier([_ride_gather(s_in)], name="gather_w_in")
    win = g_in.transpose(1, 0, 2).reshape(D_MODEL, PROJ_WIDTH)
    bias = _bias_build(sp["rel_bias_table"])
    (n1t, proj), ((g_out,), (s_up,)) = _norm_matmul_wide(
        x2, g1, win, tm=tm, tn=PROJ_WIDTH // 2, name="in_proj", rides=[_ride_gather(s_out), _ride_gather(s_up, (0, 2, 8))])
    wo = g_out.reshape(A_WIDTH + B_WIDTH, D_MODEL)
    (mixed, mixed_t, ab), ((s_up,),) = _mixer_fwd(proj, lg, lb, wsp, bs_col, sinks, bias, ga, gb,
                                                  rides=[_ride_gather(s_up, (2, 7, 8))])
    h1, ((wu,), (s_down,)) = _matmul_res(mixed, wo, x2, tm=tl, tn=1024, tk=D_MODEL, prologue=_to_bf16, name="out_proj",
                                         rides=[_ride_gather(s_up, (7, 8, 8)), _ride_gather(s_down, (0, 1, 8))])
    (n2t, zp, z2, z2t), ((g_down,),) = _norm_matmul_sq(h1, g2, wu, tm=tl, tn=512, name="up_proj",
                                                       rides=[_ride_gather(s_down, (1, 8, 8))])
    wd = g_down.reshape(D_FF, D_MODEL)
    h2 = _matmul_res(z2, wd, h1, tm=tl, tn=1024, tk=2048, prologue=_to_bf16, name="down_proj")

    dh2, dh2b, dgf, loss = _loss_bwd(h2, tgt, gf, tm=tm)
    dzp = _matmul_nt(dh2b, wd, tm=tl, tn=1024, tk=D_MODEL, name="bwd_dz", extra=zp, epilogue=_sq_relu_grad,
                     out_dtype=BF16)
    dwd = _matmul_nn(z2t, dh2b, tmo=1024, tn=2048, tk=tl, name="grad_w_down")
    dwd = dwd.reshape(N_CHIPS, D_FF // N_CHIPS, D_MODEL)
    dwu, ((r_d,),) = _matmul_nn(n2t, dzp, tmo=1024, tn=2048, tk=tl, name="grad_w_up", shards=N_CHIPS,
                                rides=[_ride_sibling_halves(dwd)])
    q_d = pair_sum("w_down", dwd, r_d)
    dn2, ((o_d,), (r_u,)) = _matmul_nt(dzp, wu, tm=tl, tn=1024, tk=2048, name="bwd_dn2",
                                       rides=[_ride_scatter(q_d, None, (0, 6, 8)), _ride_sibling_halves(dwu)])
    q_u = pair_sum("w_up", dwu, r_u)
    (dh1, dh1b, dg2), ((o_d,),) = _rms_bwd_res(dn2, h1, g2, dh2, tm=tm, name="ffn_norm_bwd",
                                               rides=[_ride_scatter(q_d, o_d, (6, 8, 8))])
    h_d = owner_sum("w_down", dwd, r_d, o_d)
    dwo, ((o_u,),) = _matmul_nn(mixed_t, dh1b, tmo=1024, tn=2048, tk=tl, name="grad_w_out",
                                rides=[_ride_scatter(q_u, None, (0, 2, 8))])
    dwo = dwo.reshape(N_CHIPS, (A_WIDTH + B_WIDTH) // N_CHIPS, D_MODEL)
    dmixed, ((o_u,), (r_o,)) = _matmul_nt(dh1b, wo, tm=tl, tn=1024, tk=D_MODEL, name="bwd_dmixed",
                                          rides=[_ride_scatter(q_u, o_u, (2, 4, 8)), _ride_sibling_halves(dwo)])
    q_o = pair_sum("w_out", dwo, r_o)
    (duv, dga, dwsp, dbs, dlg, dlb), ((o_u,),) = _gmlp_bwd(proj, ab, dmixed, ga, lg, lb, wsp, bs_col,
                                                           rides=[_ride_scatter(q_u, o_u, (4, 8, 8))])
    h_u = owner_sum("w_up", dwu, r_u, o_u)
    (dq, dkv, dgb, dsinks, dbias), ((o_o,), (w_d,)) = _attn_bwd(proj, ab, dmixed, gb, sinks, bias,
                                                                rides=[_ride_scatter(q_o), _ride_swap(h_d)])
    h_o = owner_sum("w_out", dwo, r_o, o_o)
    dtable = _bias_grad(dbias)
    dproj = jnp.concatenate([duv, dq, dkv], axis=1)
    small = {"gate_norm_g": dlg, "gate_norm_b": dlb, "w_spatial": dwsp, "b_spatial": dbs, "out_norm_a_g": dga}
    dwin, ((w_o,), (w_u,), (slots_a,)) = _matmul_nn(
        n1t, dproj, tmo=1024, tn=PROJ_WIDTH // 2, tk=tl, name="grad_w_in",
        rides=[_ride_swap(h_o), _ride_swap(h_u), _ride_small_to_all(_pack([small[n] for n in SMALL_A]))])
    dwin = dwin.reshape(D_MODEL, N_CHIPS, PROJ_WIDTH // N_CHIPS).transpose(1, 0, 2)
    dn1, ((r_i,),) = _matmul_nt(dproj, win, tm=tl, tn=1024, tk=PROJ_WIDTH, name="bwd_dn1",
                                rides=[_ride_sibling_halves(dwin)])
    q_i = pair_sum("w_in", dwin, r_i)
    (dx, _, dg1), ((o_i,),) = _rms_bwd_res(dn1, x2, g1, dh1, tm=tm, name="mix_norm_bwd", rides=[_ride_scatter(q_i)])
    h_i = owner_sum("w_in", dwin, r_i, o_i)
    small.update({"rel_bias_table": dtable.reshape(N_BUCKETS, B_HEADS), "mix_norm_g": dg1, "attn_sinks": dsinks,
                  "out_norm_b_g": dgb, "ffn_norm_g": dg2, "final_norm_g": dgf})
    (w_i,), (slots_b,) = _carrier([_ride_swap(h_i), _ride_small_to_all(_pack([small[n] for n in SMALL_B]))],
                                  name="swap_w_in")

    out_g, out_d, out_m, out_v = {}, {}, {}, {}
    for n, h, s in zip(LARGE, [h_i, h_o, h_u, h_d], [w_i, w_o, w_u, w_d]):
        res = _adamw_halves(wts[n], h, s, mom[n], var[n], tm=256, name="adamw_" + n)
        out_g[n], out_d[n], out_m[n], out_v[n] = res
    for names, slots, tag in ((SMALL_A, slots_a, "a"), (SMALL_B, slots_b, "b")):
        like = [wts[n] for n in names]
        res = _adamw_small(_pack(like), slots, _pack([mom[n] for n in names]), _pack([var[n] for n in names]),
                           name="adamw_small_" + tag)
        for store, packed in zip((out_g, out_d, out_m, out_v), res):
            for n, val in zip(names, _unpack(packed, like)):
                store[n] = val

    total = lax.psum(loss[0, 0], ("x", "y", "c"))
    return (total, dx[None], *[out_g[n] for n in WEIGHTS], *[out_d[n] for n in WEIGHTS],
            *[out_m[n] for n in WEIGHTS], *[out_v[n] for n in WEIGHTS])
```

```python
import functools
import math

import numpy as np
import jax
import jax.numpy as jnp
from jax import lax
from jax.experimental import pallas as pl
from jax.experimental.pallas import tpu as pltpu

F32 = jnp.float32
BF16 = jnp.bfloat16

D_MODEL = 2048
CHUNK = 128
A_GROUPS = 8
A_WIDTH = 1024
HEAD_DIM = 64
B_HEADS = 16
Q_PER_KV = 8
B_WIDTH = 1024
KV_WIDTH = 128
PROJ_WIDTH = 3328
D_FF = 8192
N_BUCKETS = 32
EPS = 1e-5
NEG = -1e30
SCALE = HEAD_DIM ** -0.5
N_CHIPS = 4
N_DEV = 8

ADAM_LR = 0.001
ADAM_B1 = 0.9
ADAM_B2 = 0.999
ADAM_EPS = 1e-08
ADAM_WD = 0.01
ADAM_STEP = 10

VMEM_LIMIT = 56 * 1024 * 1024
MESH = pl.DeviceIdType.MESH


def _bucket_thresholds():
    d = np.arange(CHUNK)
    n_exact = N_BUCKETS // 2
    relf = np.maximum(d, n_exact).astype(np.float64)
    large = n_exact + (np.log(relf / n_exact) / math.log(CHUNK / n_exact) * (N_BUCKETS - n_exact)).astype(np.int32)
    bucket = np.where(d < n_exact, d, np.minimum(large, N_BUCKETS - 1))
    return [int(np.min(d[bucket >= b])) for b in range(1, N_BUCKETS)]


BUCKET_THR = _bucket_thresholds()


def _params(sem=None):
    return pltpu.CompilerParams(dimension_semantics=sem, vmem_limit_bytes=VMEM_LIMIT)


def _gelu(x):
    c = math.sqrt(2.0 / math.pi)
    return 0.5 * x * (1.0 + jnp.tanh(c * (x + 0.044715 * (x * x * x))))


def _gelu_and_grad(x):
    c = math.sqrt(2.0 / math.pi)
    x2 = x * x
    t = jnp.tanh(c * (x + 0.044715 * (x2 * x)))
    g = 0.5 * x * (1.0 + t)
    dg = 0.5 * (1.0 + t) + 0.5 * x * (1.0 - t * t) * (c * (1.0 + 3.0 * 0.044715 * x2))
    return g, dg


def _dot(a, b):
    return jnp.dot(a, b, preferred_element_type=F32)


def _dot_nt(a, b):
    return lax.dot_general(a, b, (((1,), (1,)), ((), ())), preferred_element_type=F32)


def _dot_tn(a, b):
    return lax.dot_general(a, b, (((0,), (0,)), ((), ())), preferred_element_type=F32)


def _rms_bwd(dn, h, g):
    r = lax.rsqrt(jnp.mean(h * h, axis=-1, keepdims=True) + EPS)
    w = dn * g
    dh = r * w - h * ((r * r * r) * jnp.mean(w * h, axis=-1, keepdims=True))
    return dh, r


def _place():
    x, y, c = lax.axis_index("x"), lax.axis_index("y"), lax.axis_index("c")
    chips = [(1 - x, y), (x, 1 - y), (1 - x, 1 - y)]
    return x, y, c, chips


def _remote(src, dst, send_sem, recv_sem, to):
    return pltpu.make_async_remote_copy(src_ref=src, dst_ref=dst, send_sem=send_sem, recv_sem=recv_sem,
                                        device_id=to, device_id_type=MESH)


class _Ride:
    def __init__(self, args, out_shape, n_sem, start, finish, mid=None, mid_frac=0.8, aliases=None):
        self.args, self.out_shape, self.n_sem = list(args), list(out_shape), n_sem
        self.start, self.mid, self.finish, self.mid_frac = start, mid, finish, mid_frac
        self.aliases = dict(aliases or {})


def _call(body, *, name, grid, in_specs, out_specs, out_shape, scratch_shapes=(), sem=None, rides=()):
    single = not isinstance(out_shape, (list, tuple))
    out_specs = [out_specs] if single else list(out_specs)
    out_shape = [out_shape] if single else list(out_shape)
    n_in, n_out, n_scr = len(in_specs), len(out_shape), len(scratch_shapes)
    r_in = [len(r.args) for r in rides]
    r_out = [len(r.out_shape) for r in rides]
    any_spec = pl.BlockSpec(memory_space=pl.ANY)
    aliases, off_i, off_o = {}, n_in, n_out
    for r in rides:
        for i, o in r.aliases.items():
            aliases[off_i + i] = off_o + o
        off_i += len(r.args)
        off_o += len(r.out_shape)
    steps = math.prod(grid)

    def wrapped(*refs):
        p = 0
        ins = refs[p:p + n_in]; p += n_in
        rins = refs[p:p + sum(r_in)]; p += sum(r_in)
        outs = refs[p:p + n_out]; p += n_out
        routs = refs[p:p + sum(r_out)]; p += sum(r_out)
        scr = refs[p:p + n_scr]; p += n_scr
        sems = refs[p:]
        parts, pi, po = [], 0, 0
        for k, r in enumerate(rides):
            parts.append((rins[pi:pi + r_in[k]], routs[po:po + r_out[k]], sems[2 * k], sems[2 * k + 1]))
            pi += r_in[k]
            po += r_out[k]
        lin = 0
        for d in range(len(grid)):
            lin = lin * grid[d] + pl.program_id(d)
        if rides:
            @pl.when(lin == 0)
            def _():
                for r, part in zip(rides, parts):
                    r.start(*part)
        body(*ins, *outs, *scr)
        for r, part in zip(rides, parts):
            if r.mid is not None:
                @pl.when(lin == min(steps - 1, int(r.mid_frac * steps)))
                def _(r=r, part=part):
                    r.mid(*part)
        if rides:
            @pl.when(lin == steps - 1)
            def _():
                for r, part in zip(rides, parts):
                    r.finish(*part)

    scratch = list(scratch_shapes)
    for r in rides:
        scratch += [pltpu.SemaphoreType.DMA((r.n_sem,)), pltpu.SemaphoreType.DMA((r.n_sem,))]
    if rides:
        sem = ("arbitrary",) * len(grid)
    res = pl.pallas_call(
        wrapped, name=name, grid=grid,
        in_specs=list(in_specs) + [any_spec] * sum(r_in),
        out_specs=out_specs + [any_spec] * sum(r_out),
        out_shape=out_shape + [s for r in rides for s in r.out_shape],
        scratch_shapes=scratch, input_output_aliases=aliases,
        compiler_params=_params(sem),
    )

    def run(*args):
        got = res(*args, *[a for r in rides for a in r.args])
        mine = got[0] if single else list(got[:n_out])
        if not rides:
            return mine
        rest, out = list(got[n_out:]), []
        for k in range(len(rides)):
            out.append(rest[:r_out[k]])
            rest = rest[r_out[k]:]
        return mine, out

    return run


def _ride_gather(slot, part=(0, 1), mid_frac=0.8):
    k0, k1, n = part if len(part) == 3 else (part[0], part[0] + 1, part[1])
    rows_n = (k1 - k0) * (slot.shape[1] // 2 // n)
    off = lambda c: c * (slot.shape[1] // 2) + k0 * (slot.shape[1] // 2 // n)

    def start(ins, outs, ss, rs):
        x, y, c, chips = _place()
        mine = outs[0].at[2 * x + y, pl.ds(off(c), rows_n), :]
        for j, chip in enumerate(chips):
            _remote(mine, mine, ss.at[j], rs.at[j], (*chip, c)).start()

    def mid(ins, outs, ss, rs):
        x, y, c, chips = _place()
        for j, chip in enumerate(chips):
            landed = outs[0].at[2 * chip[0] + chip[1], pl.ds(off(c), rows_n), :]
            _remote(landed, landed, ss.at[j], rs.at[j], (x, y, c)).wait_recv()
            _remote(landed, landed, ss.at[3 + j], rs.at[3 + j], (x, y, 1 - c)).start()

    def finish(ins, outs, ss, rs):
        x, y, c, chips = _place()
        for j, chip in enumerate(chips):
            other = outs[0].at[2 * chip[0] + chip[1], pl.ds(off(1 - c), rows_n), :]
            _remote(other, other, ss.at[3 + j], rs.at[3 + j], (x, y, c)).wait_recv()
        for j in range(6):
            piece = outs[0].at[0, pl.ds(0, rows_n), :]
            _remote(piece, piece, ss.at[j], rs.at[j], (x, y, c)).wait_send()

    return _Ride([slot], [jax.ShapeDtypeStruct(slot.shape, slot.dtype)], 6, start, finish, mid=mid,
                 mid_frac=mid_frac, aliases={0: 0})


def _ride_sibling_halves(g):
    S, R, C = g.shape
    hr = R // 2

    def copy(ins, outs, ss, rs):
        x, y, c, _ = _place()
        return _remote(ins[0].at[:, pl.ds((1 - c) * hr, hr), :], outs[0], ss.at[0], rs.at[0], (x, y, 1 - c))

    return _Ride([g], [jax.ShapeDtypeStruct((S, hr, C), g.dtype)], 1,
                 lambda *a: copy(*a).start(), lambda *a: copy(*a).wait())


def _ride_scatter(q, land=None, part=(0, 1)):
    k0, k1, n = part if len(part) == 3 else (part[0], part[0] + 1, part[1])
    rows_n = q.shape[1] // n
    rows = pl.ds(k0 * rows_n, (k1 - k0) * rows_n)

    def copies(ins, outs, ss, rs):
        x, y, c, chips = _place()
        return [_remote(ins[0].at[2 * chip[0] + chip[1], rows, :], outs[0].at[j, rows, :], ss.at[j], rs.at[j], (*chip, c))
                for j, chip in enumerate(chips)]

    def start(*a):
        for cp in copies(*a):
            cp.start()

    def finish(*a):
        for cp in copies(*a):
            cp.wait()

    shape = jax.ShapeDtypeStruct((3,) + q.shape[1:], q.dtype)
    if land is None:
        return _Ride([q], [shape], 3, start, finish)
    return _Ride([q, land], [shape], 3, start, finish, aliases={1: 0})


def _ride_swap(h):
    def copy(ins, outs, ss, rs):
        x, y, c, _ = _place()
        return _remote(ins[0], outs[0], ss.at[0], rs.at[0], (x, y, 1 - c))

    return _Ride([h], [jax.ShapeDtypeStruct(h.shape, h.dtype)], 1,
                 lambda *a: copy(*a).start(), lambda *a: copy(*a).wait())


def _mesh_place(p):
    return (p // 4, (p // 2) % 2, p % 2)


def _ride_small_to_all(packed):
    def copies(ins, outs, ss, rs):
        x, y, c, _ = _place()
        me = 4 * x + 2 * y + c
        return [_remote(ins[0], outs[0].at[me], ss.at[k - 1], rs.at[k - 1], _mesh_place((me + k) % N_DEV))
                for k in range(1, N_DEV)]

    def own(ins, outs, ss, rs):
        x, y, c, _ = _place()
        return pltpu.make_async_copy(ins[0], outs[0].at[4 * x + 2 * y + c], ss.at[N_DEV - 1])

    def start(*a):
        own(*a).start()
        for cp in copies(*a):
            cp.start()

    def finish(ins, outs, ss, rs):
        x, y, c, _ = _place()
        me = 4 * x + 2 * y + c
        for k in range(1, N_DEV):
            _remote(ins[0], outs[0].at[(me + N_DEV - k) % N_DEV], ss.at[k - 1], rs.at[k - 1], (x, y, c)).wait_recv()
        for cp in copies(ins, outs, ss, rs):
            cp.wait_send()
        own(ins, outs, ss, rs).wait()

    return _Ride([packed], [jax.ShapeDtypeStruct((N_DEV,) + packed.shape, packed.dtype)], N_DEV, start, finish)


def _carrier(rides, *, name):
    _, outs = _call(lambda: None, name=name, grid=(1,), in_specs=[], out_specs=[], out_shape=[], rides=rides)()
    return outs


def _sq_relu_bf16(z):
    z = jnp.maximum(z, 0.0)
    return (z * z).astype(BF16)


def _norm_bf16(a_ref, g_ref):
    xf = a_ref[...]
    r = lax.rsqrt(jnp.mean(xf * xf, axis=-1, keepdims=True) + EPS)
    return ((xf * r) * g_ref[...]).astype(BF16)


def _norm_matmul_wide(a, g, b, *, tm, tn, name, rides=()):
    T, K = a.shape
    N = b.shape[0]

    def body(a_ref, g_ref, b_ref, n_ref, o_ref):
        n = _norm_bf16(a_ref, g_ref)
        n_ref[...] = n
        o_ref[...] = _dot_nt(n, b_ref[...])

    return _call(
        body, name=name, grid=(N // tn, T // tm),
        in_specs=[pl.BlockSpec((tm, K), lambda j, i: (i, 0)), pl.BlockSpec((1, K), lambda j, i: (0, 0)),
                  pl.BlockSpec((tn, K), lambda j, i: (j, 0))],
        out_specs=[pl.BlockSpec((None, tm, K), lambda j, i: (j, i, 0)), pl.BlockSpec((tm, tn), lambda j, i: (i, j))],
        out_shape=[jax.ShapeDtypeStruct((N // tn, T, K), BF16), jax.ShapeDtypeStruct((T, N), F32)],
        sem=("arbitrary", "arbitrary"), rides=rides,
    )(a, g, b)


def _norm_matmul_sq(a, g, b, *, tm, tn, name, rides=()):
    T, K = a.shape
    per = b.shape[2] // tn
    N = b.shape[0] * b.shape[2]

    def body(a_ref, g_ref, b_ref, nt_ref, o_ref, z_ref, zt_ref, n_scr):
        @pl.when(pl.program_id(1) == 0)
        def _():
            n = _norm_bf16(a_ref, g_ref)
            n_scr[...] = n
            nt_ref[...] = n.T
        p = _dot(n_scr[...], b_ref[...])
        o_ref[...] = p
        z = _sq_relu_bf16(p)
        z_ref[...] = z
        zt_ref[...] = z.T

    return _call(
        body, name=name, grid=(T // tm, N // tn),
        in_specs=[pl.BlockSpec((tm, K), lambda i, j: (i, 0)), pl.BlockSpec((1, K), lambda i, j: (0, 0)),
                  pl.BlockSpec((None, K, tn), lambda i, j: (j // per, 0, j % per))],
        out_specs=[pl.BlockSpec((K, tm), lambda i, j: (0, i)), pl.BlockSpec((tm, tn), lambda i, j: (i, j)),
                   pl.BlockSpec((tm, tn), lambda i, j: (i, j)), pl.BlockSpec((tn, tm), lambda i, j: (j, i))],
        out_shape=[jax.ShapeDtypeStruct((K, T), BF16), jax.ShapeDtypeStruct((T, N), F32),
                   jax.ShapeDtypeStruct((T, N), BF16), jax.ShapeDtypeStruct((N, T), BF16)],
        scratch_shapes=[pltpu.VMEM((tm, K), BF16)],
        sem=("parallel", "arbitrary"), rides=rides,
    )(a, g, b)


def _matmul_nn(at, b, *, tmo, tn, tk, name, shards=1, rides=()):
    M, T = at.shape[-2:]
    N = b.shape[-1]
    if at.ndim == 3:
        a_spec = pl.BlockSpec((None, tmo, tk), lambda i, j, k: (0, i, k))
    else:
        a_spec = pl.BlockSpec((tmo, tk), lambda i, j, k: (i, k))
    if b.ndim == 3:
        b_spec = pl.BlockSpec((None, tk, tn), lambda i, j, k: (0, k, j))
    else:
        b_spec = pl.BlockSpec((tk, tn), lambda i, j, k: (k, j))
    if shards > 1:
        per = (N // shards) // tn
        out_spec = pl.BlockSpec((None, tmo, tn), lambda i, j, k: (j // per, i, j % per))
        out_shape = jax.ShapeDtypeStruct((shards, M, N // shards), F32)
    else:
        out_spec = pl.BlockSpec((tmo, tn), lambda i, j, k: (i, j))
        out_shape = jax.ShapeDtypeStruct((M, N), F32)

    def body(a_ref, b_ref, o_ref):
        k = pl.program_id(2)
        p = _dot(a_ref[...], b_ref[...])

        @pl.when(k == 0)
        def _():
            o_ref[...] = p

        @pl.when(k > 0)
        def _():
            o_ref[...] += p

    return _call(
        body, name=name, grid=(M // tmo, N // tn, T // tk),
        in_specs=[a_spec, b_spec],
        out_specs=out_spec, out_shape=out_shape,
        sem=("parallel", "parallel", "arbitrary"), rides=rides,
    )(at, b)


def _to_bf16(v):
    return v.astype(BF16)


def _matmul_res(a, b, res, *, tm, tn, tk, prologue, name, rides=()):
    T, K = a.shape
    N = b.shape[1]

    def body(a_ref, b_ref, res_ref, o_ref):
        k = pl.program_id(2)
        p = _dot(prologue(a_ref[...]), b_ref[...])

        @pl.when(k == 0)
        def _():
            o_ref[...] = res_ref[...] + p

        @pl.when(k > 0)
        def _():
            o_ref[...] += p

    return _call(
        body, name=name, grid=(T // tm, N // tn, K // tk),
        in_specs=[pl.BlockSpec((tm, tk), lambda i, j, k: (i, k)), pl.BlockSpec((tk, tn), lambda i, j, k: (k, j)),
                  pl.BlockSpec((tm, tn), lambda i, j, k: (i, j))],
        out_specs=pl.BlockSpec((tm, tn), lambda i, j, k: (i, j)),
        out_shape=jax.ShapeDtypeStruct((T, N), F32),
        sem=("parallel", "parallel", "arbitrary"), rides=rides,
    )(a, b, res)


def _matmul_nt(a, b, *, tm, tn, tk, name, extra=None, epilogue=None, out_dtype=F32, rides=()):
    T, K = a.shape
    if b.ndim == 3:
        per = b.shape[2] // tk
        N = b.shape[1]
        b_spec = pl.BlockSpec((None, tn, tk), lambda i, j, k: (k // per, j, k % per))
    else:
        N = b.shape[0]
        b_spec = pl.BlockSpec((tn, tk), lambda i, j, k: (j, k))
    nk = K // tk
    assert out_dtype == F32 or nk == 1
    in_specs = [pl.BlockSpec((tm, tk), lambda i, j, k: (i, k)), b_spec]
    args = [a, b]
    if extra is not None:
        in_specs.append(pl.BlockSpec((tm, tn), lambda i, j, k: (i, j)))
        args.append(extra)

    def body(*refs):
        a_ref, b_ref = refs[0], refs[1]
        o_ref = refs[-1]
        p = _dot_nt(a_ref[...].astype(BF16), b_ref[...])
        if nk == 1:
            if epilogue is not None:
                p = epilogue(p, refs[2][...])
            o_ref[...] = p.astype(out_dtype)
        else:
            k = pl.program_id(2)

            @pl.when(k == 0)
            def _():
                o_ref[...] = p

            @pl.when(k > 0)
            def _():
                o_ref[...] += p

    return _call(
        body, name=name, grid=(T // tm, N // tn, nk),
        in_specs=in_specs,
        out_specs=pl.BlockSpec((tm, tn), lambda i, j, k: (i, j)),
        out_shape=jax.ShapeDtypeStruct((T, N), out_dtype),
        sem=("parallel", "parallel", "arbitrary"), rides=rides,
    )(*args)


def _matmul_tn(a, b, *, tmo, tn, tk, name, a_prologue=_to_bf16, shards=1, rides=()):
    T, M = a.shape
    N = b.shape[1]
    if shards > 1:
        per = (N // shards) // tn
        out_spec = pl.BlockSpec((None, tmo, tn), lambda i, j, k: (j // per, i, j % per))
        out_shape = jax.ShapeDtypeStruct((shards, M, N // shards), F32)
    else:
        out_spec = pl.BlockSpec((tmo, tn), lambda i, j, k: (i, j))
        out_shape = jax.ShapeDtypeStruct((M, N), F32)

    def body(a_ref, b_ref, o_ref):
        k = pl.program_id(2)
        p = _dot_tn(a_prologue(a_ref[...]), b_ref[...].astype(BF16))

        @pl.when(k == 0)
        def _():
            o_ref[...] = p

        @pl.when(k > 0)
        def _():
            o_ref[...] += p

    return _call(
        body, name=name, grid=(M // tmo, N // tn, T // tk),
        in_specs=[pl.BlockSpec((tk, tmo), lambda i, j, k: (k, i)), pl.BlockSpec((tk, tn), lambda i, j, k: (k, j))],
        out_specs=out_spec, out_shape=out_shape,
        sem=("parallel", "parallel", "arbitrary"), rides=rides,
    )(a, b)


def _loss_bwd(h2, tgt, g, *, tm):
    T, D = h2.shape

    def body(h_ref, t_ref, g_ref, dh_ref, dhb_ref, dg_ref, loss_ref):
        @pl.when(pl.program_id(0) == 0)
        def _():
            dg_ref[...] = jnp.zeros_like(dg_ref)
            loss_ref[...] = jnp.zeros_like(loss_ref)
        h = h_ref[...]
        gg = g_ref[...]
        r = lax.rsqrt(jnp.mean(h * h, axis=-1, keepdims=True) + EPS)
        hn = h * r
        err = hn * gg - t_ref[...]
        loss_ref[...] += 0.5 * jnp.sum(jnp.mean(err * err, axis=-1, keepdims=True), axis=0, keepdims=True)
        dy = err * (1.0 / D)
        dg_ref[...] += jnp.sum(dy * hn, axis=0, keepdims=True)
        w = dy * gg
        dh = r * w - h * ((r * r * r) * jnp.mean(w * h, axis=-1, keepdims=True))
        dh_ref[...] = dh
        dhb_ref[...] = dh.astype(BF16)

    tile = pl.BlockSpec((tm, D), lambda i: (i, 0))
    return pl.pallas_call(
        body, name="loss_bwd", grid=(T // tm,),
        in_specs=[tile, tile, pl.BlockSpec((1, D), lambda i: (0, 0))],
        out_specs=[tile, tile, pl.BlockSpec((1, D), lambda i: (0, 0)), pl.BlockSpec((1, 1), lambda i: (0, 0))],
        out_shape=[jax.ShapeDtypeStruct((T, D), F32), jax.ShapeDtypeStruct((T, D), BF16),
                   jax.ShapeDtypeStruct((1, D), F32), jax.ShapeDtypeStruct((1, 1), F32)],
        compiler_params=_params(("arbitrary",)),
    )(h2, tgt, g)


def _rms_bwd_res(dn, h, g, dres, *, tm, name, rides=()):
    T, D = h.shape

    def body(dn_ref, h_ref, g_ref, dres_ref, dh_ref, dhb_ref, dg_ref):
        @pl.when(pl.program_id(0) == 0)
        def _():
            dg_ref[...] = jnp.zeros_like(dg_ref)
        h_ = h_ref[...]
        dn_ = dn_ref[...]
        dh, r = _rms_bwd(dn_, h_, g_ref[...])
        dg_ref[...] += jnp.sum(dn_ * (h_ * r), axis=0, keepdims=True)
        dh = dres_ref[...] + dh
        dh_ref[...] = dh
        dhb_ref[...] = dh.astype(BF16)

    tile = pl.BlockSpec((tm, D), lambda i: (i, 0))
    return _call(
        body, name=name, grid=(T // tm,),
        in_specs=[tile, tile, pl.BlockSpec((1, D), lambda i: (0, 0)), tile],
        out_specs=[tile, tile, pl.BlockSpec((1, D), lambda i: (0, 0))],
        out_shape=[jax.ShapeDtypeStruct((T, D), F32), jax.ShapeDtypeStruct((T, D), BF16),
                   jax.ShapeDtypeStruct((1, D), F32)],
        sem=("arbitrary",), rides=rides,
    )(dn, h, g, dres)


def _rel_distance():
    i = lax.broadcasted_iota(jnp.int32, (CHUNK, 2 * CHUNK), 0)
    j = lax.broadcasted_iota(jnp.int32, (CHUNK, 2 * CHUNK), 1)
    return i + CHUNK - j


def _bias_build(table):
    def body(tab_ref, o_ref):
        rel = _rel_distance()
        ge = [rel >= t for t in BUCKET_THR]
        for h in range(B_HEADS):
            cur = jnp.full((CHUNK, 2 * CHUNK), tab_ref[0, h], F32)
            for b in range(1, N_BUCKETS):
                cur = jnp.where(ge[b - 1], tab_ref[b, h], cur)
            o_ref[h] = cur

    return pl.pallas_call(
        body, name="bias_build",
        in_specs=[pl.BlockSpec(memory_space=pltpu.SMEM)],
        out_specs=pl.BlockSpec(memory_space=pltpu.VMEM),
        out_shape=jax.ShapeDtypeStruct((B_HEADS, CHUNK, 2 * CHUNK), F32),
    )(table)


def _bias_grad(dbias):
    def body(db_ref, o_ref, acc_ref):
        rel = _rel_distance()
        lo = [0] + BUCKET_THR
        hi = BUCKET_THR + [CHUNK]
        for b in range(N_BUCKETS):
            m = (rel >= lo[b]) & (rel < hi[b])
            for h in range(B_HEADS):
                row = b * B_HEADS + h
                acc_ref[row:row + 1, :] = jnp.sum(jnp.where(m, db_ref[h], 0.0), axis=0, keepdims=True)
        o_ref[...] = jnp.sum(acc_ref[...], axis=1, keepdims=True)

    return pl.pallas_call(
        body, name="bias_grad",
        in_specs=[pl.BlockSpec(memory_space=pltpu.VMEM)],
        out_specs=pl.BlockSpec(memory_space=pltpu.VMEM),
        out_shape=jax.ShapeDtypeStruct((N_BUCKETS * B_HEADS, 1), F32),
        scratch_shapes=[pltpu.VMEM((N_BUCKETS * B_HEADS, 2 * CHUNK), F32)],
    )(dbias)


def _causal_mask():
    t = lax.broadcasted_iota(jnp.int32, (CHUNK, CHUNK), 0)
    s = lax.broadcasted_iota(jnp.int32, (CHUNK, CHUNK), 1)
    return s <= t


def _band_mask(n):
    rel = _rel_distance()
    j = lax.broadcasted_iota(jnp.int32, (CHUNK, 2 * CHUNK), 1)
    return (rel >= 0) & (rel < CHUNK) & ((n > 0) | (j >= CHUNK))


def _gate_forward(u, v, lg, lb, wc, bs):
    ug = _gelu(u)
    vg = _gelu(v)
    mu = jnp.mean(vg, axis=-1, keepdims=True)
    xc = vg - mu
    rstd = lax.rsqrt(jnp.mean(xc * xc, axis=-1, keepdims=True) + EPS)
    xhat = xc * rstd
    vl = (xhat * lg + lb).astype(BF16)
    mixed = _dot(wc, vl) + bs
    return ug, xhat, rstd, vl, mixed


def _softmax_scores(qk, bias, mask, sink):
    s = qk * SCALE + bias
    s = jnp.where(mask, s, NEG)
    m = jnp.maximum(jnp.max(s, axis=-1, keepdims=True), sink)
    p = jnp.exp(s - m)
    e_sink = jnp.exp(sink - m)
    inv = 1.0 / (jnp.sum(p, axis=-1, keepdims=True) + e_sink)
    return p * inv, e_sink * inv


PAIRS = Q_PER_KV // 2


def _head(g, pr, e):
    return g * Q_PER_KV + 2 * pr + e


def _stack_pairs(ref, g, col0=0):
    w = 2 * HEAD_DIM
    return jnp.concatenate([ref[:, col0 + (g * PAIRS + pr) * w:col0 + (g * PAIRS + pr + 1) * w] for pr in range(PAIRS)],
                           axis=0)


def _low_lanes():
    return lax.broadcasted_iota(jnp.int32, (2 * CHUNK, 2 * HEAD_DIM), 1) < HEAD_DIM


def _band_operands(kv_prev, kv_cur):
    band = jnp.concatenate([kv_prev, kv_cur], axis=0)
    low = _low_lanes()
    ops = []
    for cat in (band[:, :KV_WIDTH], band[:, KV_WIDTH:]):
        rol = pltpu.roll(cat, HEAD_DIM, 1)
        ops.append([[jnp.where(low if e == 0 else ~low, cat if g == e else rol, 0.0).astype(BF16) for e in range(2)]
                    for g in range(2)])
    return ops


def _mixer_fwd(proj, lg, lb, wsp, bs_col, sinks, bias, ga, gb, rides=()):
    T = proj.shape[0]
    nb = T // CHUNK

    def body(u_ref, v_ref, q_ref, kvc_ref, kvp_ref, lg_ref, lb_ref, w_ref, bs_ref, sink_ref, bias_ref,
             ga_ref, gb_ref, mixed_ref, mixed_t_ref, ab_ref):
        n = pl.program_id(0)
        causal = _causal_mask()
        ssq = jnp.zeros((CHUNK, 1), F32)
        for g in range(A_GROUPS):
            cols = slice(g * CHUNK, (g + 1) * CHUNK)
            wc = jnp.where(causal, w_ref[g], 0.0).astype(BF16)
            ug, _, _, _, mixed = _gate_forward(u_ref[:, cols], v_ref[:, cols], lg_ref[g:g + 1, :], lb_ref[g:g + 1, :],
                                               wc, bs_ref[g])
            a = ug * mixed
            ab_ref[:, cols] = a
            ssq = ssq + jnp.sum(a * a, axis=-1, keepdims=True)
        ra = lax.rsqrt(ssq * (1.0 / A_WIDTH) + EPS)
        mixed_ref[:, :A_WIDTH] = ((ab_ref[:, :A_WIDTH] * ra) * ga_ref[...]).astype(BF16)

        mask = _band_mask(n)
        kops, vops = _band_operands(kvp_ref[...], kvc_ref[...])
        ssq = jnp.zeros((CHUNK, 1), F32)
        for g in range(B_HEADS // Q_PER_KV):
            qst = _stack_pairs(q_ref, g).astype(BF16)
            o_st = jnp.zeros((PAIRS * CHUNK, 2 * HEAD_DIM), F32)
            for e in range(2):
                s_all = _dot_nt(qst, kops[g][e])
                ps = []
                for pr in range(PAIRS):
                    h = _head(g, pr, e)
                    p, _ = _softmax_scores(s_all[pr * CHUNK:(pr + 1) * CHUNK], bias_ref[h], mask, sink_ref[0, h])
                    ps.append(p.astype(BF16))
                o_st = o_st + _dot(jnp.concatenate(ps, axis=0), vops[g][e])
            for pr in range(PAIRS):
                o = o_st[pr * CHUNK:(pr + 1) * CHUNK]
                c0 = A_WIDTH + (g * PAIRS + pr) * 2 * HEAD_DIM
                ab_ref[:, c0:c0 + 2 * HEAD_DIM] = o
                ssq = ssq + jnp.sum(o * o, axis=-1, keepdims=True)
        rb = lax.rsqrt(ssq * (1.0 / B_WIDTH) + EPS)
        mixed_ref[:, A_WIDTH:] = ((ab_ref[:, A_WIDTH:] * rb) * gb_ref[...]).astype(BF16)
        mixed_t_ref[...] = mixed_ref[...].T

    full = lambda *shape: pl.BlockSpec(shape, lambda n: (0,) * len(shape))
    return _call(
        body, name="mixer_fwd", grid=(nb,),
        in_specs=[pl.BlockSpec((CHUNK, A_WIDTH), lambda n: (n, 0)),
                  pl.BlockSpec((CHUNK, A_WIDTH), lambda n: (n, 1)),
                  pl.BlockSpec((CHUNK, B_WIDTH), lambda n: (n, 2)),
                  pl.BlockSpec((CHUNK, 2 * KV_WIDTH), lambda n: (n, 12)),
                  pl.BlockSpec((CHUNK, 2 * KV_WIDTH), lambda n: (jnp.maximum(n - 1, 0), 12)),
                  full(A_GROUPS, CHUNK), full(A_GROUPS, CHUNK), full(A_GROUPS, CHUNK, CHUNK), full(A_GROUPS, CHUNK, 1),
                  pl.BlockSpec(memory_space=pltpu.SMEM), full(B_HEADS, CHUNK, 2 * CHUNK),
                  full(1, A_WIDTH), full(1, B_WIDTH)],
        out_specs=[pl.BlockSpec((CHUNK, D_MODEL), lambda n: (n, 0)), pl.BlockSpec((D_MODEL, CHUNK), lambda n: (0, n)),
                   pl.BlockSpec((CHUNK, D_MODEL), lambda n: (n, 0))],
        out_shape=[jax.ShapeDtypeStruct((T, D_MODEL), BF16), jax.ShapeDtypeStruct((D_MODEL, T), BF16),
                   jax.ShapeDtypeStruct((T, D_MODEL), F32)],
        sem=("parallel",), rides=rides,
    )(proj, proj, proj, proj, proj, lg, lb, wsp, bs_col, sinks, bias, ga, gb)


def _gmlp_bwd(proj, ab, dmixed, ga, lg, lb, wsp, bs_col, rides=()):
    T = proj.shape[0]
    nb = T // CHUNK

    def body(u_ref, v_ref, a_ref, dna_ref, ga_ref, lg_ref, lb_ref, w_ref, bs_ref,
             dp_ref, dpt_ref, dga_ref, dw_ref, dbs_ref, dlg_ref, dlb_ref):
        @pl.when(pl.program_id(0) == 0)
        def _():
            for r in (dga_ref, dw_ref, dbs_ref, dlg_ref, dlb_ref):
                r[...] = jnp.zeros_like(r)
        causal = _causal_mask()
        a_all = a_ref[...]
        dna = dna_ref[...]
        da_all, ra = _rms_bwd(dna, a_all, ga_ref[...])
        dga_ref[...] += jnp.sum(dna * (a_all * ra), axis=0, keepdims=True)
        for g in range(A_GROUPS):
            cols = slice(g * CHUNK, (g + 1) * CHUNK)
            wc = jnp.where(causal, w_ref[g], 0.0).astype(BF16)
            lgg = lg_ref[g:g + 1, :]
            u = u_ref[:, cols]
            v = v_ref[:, cols]
            ug, xhat, rstd, vl, mixed = _gate_forward(u, v, lgg, lb_ref[g:g + 1, :], wc, bs_ref[g])
            da = da_all[:, cols]
            dug = da * mixed
            dmg = da * ug
            dmg_b = dmg.astype(BF16)
            dbs_ref[g] += jnp.sum(dmg, axis=-1, keepdims=True)
            dw_ref[g] += jnp.where(causal, _dot_nt(dmg_b, vl), 0.0)
            dvl = _dot_tn(wc, dmg_b)
            dlg_ref[g:g + 1, :] += jnp.sum(dvl * xhat, axis=0, keepdims=True)
            dlb_ref[g:g + 1, :] += jnp.sum(dvl, axis=0, keepdims=True)
            dxh = dvl * lgg
            dvg = rstd * (dxh - jnp.mean(dxh, axis=-1, keepdims=True)
                          - xhat * jnp.mean(dxh * xhat, axis=-1, keepdims=True))
            _, gu = _gelu_and_grad(u)
            _, gv = _gelu_and_grad(v)
            dp_ref[:, cols] = (dug * gu).astype(BF16)
            dp_ref[:, A_WIDTH + g * CHUNK:A_WIDTH + (g + 1) * CHUNK] = (dvg * gv).astype(BF16)
        dpt_ref[...] = dp_ref[...].T

    full = lambda *shape: pl.BlockSpec(shape, lambda n: (0,) * len(shape))
    return _call(
        body, name="gmlp_bwd", grid=(nb,),
        in_specs=[pl.BlockSpec((CHUNK, A_WIDTH), lambda n: (n, 0)),
                  pl.BlockSpec((CHUNK, A_WIDTH), lambda n: (n, 1)),
                  pl.BlockSpec((CHUNK, A_WIDTH), lambda n: (n, 0)),
                  pl.BlockSpec((CHUNK, A_WIDTH), lambda n: (n, 0)),
                  full(1, A_WIDTH), full(A_GROUPS, CHUNK), full(A_GROUPS, CHUNK), full(A_GROUPS, CHUNK, CHUNK),
                  full(A_GROUPS, CHUNK, 1)],
        out_specs=[pl.BlockSpec((CHUNK, 2 * A_WIDTH), lambda n: (n, 0)), pl.BlockSpec((2 * A_WIDTH, CHUNK), lambda n: (0, n)),
                   full(1, A_WIDTH), full(A_GROUPS, CHUNK, CHUNK), full(A_GROUPS, CHUNK, 1),
                   full(A_GROUPS, CHUNK), full(A_GROUPS, CHUNK)],
        out_shape=[jax.ShapeDtypeStruct((T, 2 * A_WIDTH), BF16), jax.ShapeDtypeStruct((2 * A_WIDTH, T), BF16),
                   jax.ShapeDtypeStruct((1, A_WIDTH), F32), jax.ShapeDtypeStruct((A_GROUPS, CHUNK, CHUNK), F32),
                   jax.ShapeDtypeStruct((A_GROUPS, CHUNK, 1), F32), jax.ShapeDtypeStruct((A_GROUPS, CHUNK), F32),
                   jax.ShapeDtypeStruct((A_GROUPS, CHUNK), F32)],
        sem=("arbitrary",), rides=rides,
    )(proj, proj, ab, dmixed, ga, lg, lb, wsp, bs_col)


def _attn_bwd(proj, ab, dmixed, gb, sinks, bias, rides=()):
    T = proj.shape[0]
    nb = T // CHUNK
    qn = lambda n: jnp.minimum(n, nb - 1)

    def body(q_ref, kvc_ref, kvp_ref, o_ref, dnb_ref, gb_ref, sink_ref, bias_ref,
             dq_ref, dkv_ref, dqt_ref, dkvt_ref, dgb_ref, dsink_ref, dbias_ref, carry_ref, sacc_ref):
        n = pl.program_id(0)

        @pl.when(n == 0)
        def _():
            carry_ref[...] = jnp.zeros_like(carry_ref)
            sacc_ref[...] = jnp.zeros_like(sacc_ref)
            dgb_ref[...] = jnp.zeros_like(dgb_ref)
            dbias_ref[...] = jnp.zeros_like(dbias_ref)

        @pl.when(n < nb)
        def _():
            mask = _band_mask(n)
            o_all = o_ref[...]
            dnb = dnb_ref[...]
            do_all, rb = _rms_bwd(dnb, o_all, gb_ref[...])
            dgb_ref[...] += jnp.sum(dnb * (o_all * rb), axis=0, keepdims=True)
            kops, vops = _band_operands(kvp_ref[...], kvc_ref[...])
            low = _low_lanes()
            halves = []
            for g in range(B_HEADS // Q_PER_KV):
                qst = _stack_pairs(q_ref, g).astype(BF16)
                dost = _stack_pairs(do_all, g).astype(BF16)
                dq_st = jnp.zeros((PAIRS * CHUNK, 2 * HEAD_DIM), F32)
                dk_e, dv_e = [], []
                for e in range(2):
                    s_all = _dot_nt(qst, kops[g][e])
                    dp_all = _dot_nt(dost, vops[g][e])
                    ps, dsrs = [], []
                    for pr in range(PAIRS):
                        h = _head(g, pr, e)
                        rows = slice(pr * CHUNK, (pr + 1) * CHUNK)
                        p, p_sink = _softmax_scores(s_all[rows], bias_ref[h], mask, sink_ref[0, h])
                        dp = dp_all[rows]
                        delta = jnp.sum(p * dp, axis=-1, keepdims=True)
                        ds = p * (dp - delta)
                        sacc_ref[:, h:h + 1] += -(p_sink * delta)
                        dbias_ref[h] += ds
                        ps.append(p.astype(BF16))
                        dsrs.append((ds * SCALE).astype(BF16))
                    dsr_all = jnp.concatenate(dsrs, axis=0)
                    dq_st = dq_st + _dot(dsr_all, kops[g][e])
                    dk_e.append(_dot_tn(dsr_all, qst))
                    dv_e.append(_dot_tn(jnp.concatenate(ps, axis=0), dost))
                for pr in range(PAIRS):
                    c0 = (g * PAIRS + pr) * 2 * HEAD_DIM
                    dq_ref[:, c0:c0 + 2 * HEAD_DIM] = dq_st[pr * CHUNK:(pr + 1) * CHUNK].astype(BF16)
                halves.append((dk_e, dv_e))
            tiles = []
            for t in range(2):
                g0, g1 = halves[0][t], halves[1][t]
                tiles.append(jnp.where(low, g0[0] + pltpu.roll(g0[1], HEAD_DIM, 1), pltpu.roll(g1[0], HEAD_DIM, 1) + g1[1]))
            dband = jnp.concatenate(tiles, axis=1)
            dkv = (carry_ref[...] + dband[:CHUNK]).astype(BF16)
            dkv_ref[...] = dkv
            dkvt_ref[...] = dkv.T
            dqt_ref[...] = dq_ref[...].T
            carry_ref[...] = dband[CHUNK:]

        @pl.when(n == nb)
        def _():
            dkv = carry_ref[...].astype(BF16)
            dkv_ref[...] = dkv
            dkvt_ref[...] = dkv.T
            dsink_ref[...] = jnp.sum(sacc_ref[...], axis=0, keepdims=True)

    full = lambda *shape: pl.BlockSpec(shape, lambda n: (0,) * len(shape))
    return _call(
        body, name="attn_bwd", grid=(nb + 1,),
        in_specs=[pl.BlockSpec((CHUNK, B_WIDTH), lambda n: (qn(n), 2)),
                  pl.BlockSpec((CHUNK, 2 * KV_WIDTH), lambda n: (qn(n), 12)),
                  pl.BlockSpec((CHUNK, 2 * KV_WIDTH), lambda n: (jnp.maximum(qn(n) - 1, 0), 12)),
                  pl.BlockSpec((CHUNK, B_WIDTH), lambda n: (qn(n), 1)),
                  pl.BlockSpec((CHUNK, B_WIDTH), lambda n: (qn(n), 1)),
                  full(1, B_WIDTH), pl.BlockSpec(memory_space=pltpu.SMEM), full(B_HEADS, CHUNK, 2 * CHUNK)],
        out_specs=[pl.BlockSpec((CHUNK, B_WIDTH), lambda n: (qn(n), 0)),
                   pl.BlockSpec((CHUNK, 2 * KV_WIDTH), lambda n: (jnp.maximum(n - 1, 0), 0)),
                   pl.BlockSpec((B_WIDTH, CHUNK), lambda n: (0, qn(n))),
                   pl.BlockSpec((2 * KV_WIDTH, CHUNK), lambda n: (0, jnp.maximum(n - 1, 0))),
                   full(1, B_WIDTH), full(1, B_HEADS), full(B_HEADS, CHUNK, 2 * CHUNK)],
        out_shape=[jax.ShapeDtypeStruct((T, B_WIDTH), BF16), jax.ShapeDtypeStruct((T, 2 * KV_WIDTH), BF16),
                   jax.ShapeDtypeStruct((B_WIDTH, T), BF16), jax.ShapeDtypeStruct((2 * KV_WIDTH, T), BF16),
                   jax.ShapeDtypeStruct((1, B_WIDTH), F32), jax.ShapeDtypeStruct((1, B_HEADS), F32),
                   jax.ShapeDtypeStruct((B_HEADS, CHUNK, 2 * CHUNK), F32)],
        scratch_shapes=[pltpu.VMEM((CHUNK, 2 * KV_WIDTH), F32), pltpu.VMEM((CHUNK, B_HEADS), F32)],
        sem=("arbitrary",), rides=rides,
    )(proj, proj, proj, ab, dmixed, gb, sinks, bias)


def _sq_relu_grad(acc, z):
    return acc * (2.0 * jnp.maximum(z, 0.0))


def _local_step(x, tgt, sp, win, wo, wu, wd):
    T = x.shape[0]
    tm = min(512, T)
    tk = min(512, T)
    lg = sp["gate_norm_g"].reshape(A_GROUPS, CHUNK)
    lb = sp["gate_norm_b"].reshape(A_GROUPS, CHUNK)
    wsp = sp["w_spatial"].reshape(A_GROUPS, CHUNK, CHUNK)
    bs_col = sp["b_spatial"].reshape(A_GROUPS, CHUNK, 1)
    sinks = sp["attn_sinks"].reshape(1, B_HEADS)
    ga = sp["out_norm_a_g"].reshape(1, A_WIDTH)
    gb = sp["out_norm_b_g"].reshape(1, B_WIDTH)
    g1 = sp["mix_norm_g"].reshape(1, D_MODEL)
    g2 = sp["ffn_norm_g"].reshape(1, D_MODEL)
    gf = sp["final_norm_g"].reshape(1, D_MODEL)

    bias = _bias_build(sp["rel_bias_table"])
    n1, proj = _norm_matmul(x, g1, win, tm=tm, tn=PROJ_WIDTH // 2, name="in_proj")
    mixed, ab = _mixer_fwd(proj, lg, lb, wsp, bs_col, sinks, bias, ga, gb)
    h1 = _matmul_res(mixed, wo, x, tm=tm, tn=1024, tk=D_MODEL, prologue=_to_bf16, name="out_proj")
    n2, zp = _norm_matmul(h1, g2, wu, tm=tm, tn=1024, name="up_proj")
    h2 = _matmul_res(zp, wd, h1, tm=tm, tn=1024, tk=2048, prologue=_sq_relu_bf16, name="down_proj")

    dh2, dgf, loss = _loss_bwd(h2, tgt, gf, tm=tm)
    dzp = _matmul_nt(dh2, wd, tm=tm, tn=1024, tk=D_MODEL, name="bwd_dz", extra=zp, epilogue=_sq_relu_grad,
                     out_dtype=BF16)
    dwd = _matmul_tn(zp, dh2, tmo=1024, tn=1024, tk=tk, name="grad_w_down", a_prologue=_sq_relu_bf16)
    dwu = _matmul_tn(n2, dzp, tmo=1024, tn=1024, tk=tk, name="grad_w_up", shards=N_CHIPS)
    dn2 = _matmul_nt(dzp, wu, tm=tm, tn=1024, tk=2048, name="bwd_dn2")
    dh1, dg2 = _rms_bwd_res(dn2, h1, g2, dh2, tm=tm, name="ffn_norm_bwd")
    dwo = _matmul_tn(mixed, dh1, tmo=1024, tn=1024, tk=tk, name="grad_w_out")
    dmixed = _matmul_nt(dh1, wo, tm=tm, tn=1024, tk=D_MODEL, name="bwd_dmixed")
    duv, dga, dwsp, dbs, dlg, dlb = _gmlp_bwd(proj, ab, dmixed, ga, lg, lb, wsp, bs_col)
    dq, dkv, dgb, dsinks, dbias = _attn_bwd(proj, ab, dmixed, gb, sinks, bias)
    dtable = _bias_grad(dbias)
    dproj = jnp.concatenate([duv, dq, dkv], axis=1)
    dwin = _matmul_tn(n1, dproj, tmo=1024, tn=PROJ_WIDTH // 2, tk=tk, name="grad_w_in")
    dn1 = _matmul_nt(dproj, win, tm=tm, tn=1024, tk=PROJ_WIDTH, name="bwd_dn1")
    dx, dg1 = _rms_bwd_res(dn1, x, g1, dh1, tm=tm, name="mix_norm_bwd")

    small = {
        "rel_bias_table": dtable.reshape(N_BUCKETS, B_HEADS), "mix_norm_g": dg1, "gate_norm_g": dlg, "gate_norm_b": dlb,
        "w_spatial": dwsp, "b_spatial": dbs, "attn_sinks": dsinks, "out_norm_a_g": dga, "out_norm_b_g": dgb,
        "ffn_norm_g": dg2, "final_norm_g": dgf,
    }
    return loss, dx, (dwin, dwo, dwu, dwd), small


def _place():
    x, y, c = lax.axis_index("x"), lax.axis_index("y"), lax.axis_index("c")
    chips = [(1 - x, y), (x, 1 - y), (1 - x, 1 - y)]
    return x, y, c, chips


def _remote(src, dst, send_sem, recv_sem, to):
    return pltpu.make_async_remote_copy(src_ref=src, dst_ref=dst, send_sem=send_sem, recv_sem=recv_sem,
                                        device_id=to, device_id_type=MESH)


def _core_index():
    return lax.axis_index("c").astype(jnp.int32).reshape(1)


def _chip_index():
    return (2 * lax.axis_index("x") + lax.axis_index("y")).astype(jnp.int32).reshape(1)


def _cast_into_slot(w, *, tm, name):
    _, R, C = w.shape

    def body(me_ref, w_ref, o_ref):
        del me_ref
        o_ref[...] = w_ref[...].astype(BF16)

    return pl.pallas_call(
        body, name=name,
        grid_spec=pltpu.PrefetchScalarGridSpec(
            num_scalar_prefetch=1, grid=(R // tm,),
            in_specs=[pl.BlockSpec((None, tm, C), lambda i, me: (0, i, 0))],
            out_specs=pl.BlockSpec((None, tm, C), lambda i, me: (me[0], i, 0))),
        out_shape=jax.ShapeDtypeStruct((N_CHIPS, R, C), BF16), compiler_params=_params(("parallel",)),
    )(_chip_index(), w)


def _gather_weights(slots):
    nw = len(slots)

    def body(*refs):
        fulls = refs[nw:2 * nw]
        send_sems, recv_sems = refs[2 * nw:]
        x, y, c, chips = _place()
        me = 2 * x + y
        sends = []
        for w in range(nw):
            hr = fulls[w].shape[1] // 2
            rows = pl.ds(c * hr, hr)
            for j, chip in enumerate(chips):
                mine = fulls[w].at[me, rows, :]
                cp = _remote(mine, mine, send_sems.at[6 * w + j], recv_sems.at[6 * w + j], (*chip, c))
                cp.start()
                sends.append(cp)
        for w in range(nw):
            hr = fulls[w].shape[1] // 2
            rows = pl.ds(c * hr, hr)
            for j, chip in enumerate(chips):
                landed = fulls[w].at[2 * chip[0] + chip[1], rows, :]
                _remote(landed, landed, send_sems.at[6 * w + j], recv_sems.at[6 * w + j], (x, y, c)).wait_recv()
                cp = _remote(landed, landed, send_sems.at[6 * w + 3 + j], recv_sems.at[6 * w + 3 + j], (x, y, 1 - c))
                cp.start()
                sends.append(cp)
        for w in range(nw):
            hr = fulls[w].shape[1] // 2
            rows = pl.ds((1 - c) * hr, hr)
            for j, chip in enumerate(chips):
                other = fulls[w].at[2 * chip[0] + chip[1], rows, :]
                _remote(other, other, send_sems.at[6 * w + 3 + j], recv_sems.at[6 * w + 3 + j], (x, y, c)).wait_recv()
        for cp in sends:
            cp.wait_send()

    any_spec = pl.BlockSpec(memory_space=pl.ANY)
    return pl.pallas_call(
        body, name="gather_weights",
        in_specs=[any_spec] * nw, out_specs=[any_spec] * nw,
        out_shape=[jax.ShapeDtypeStruct(s.shape, s.dtype) for s in slots],
        scratch_shapes=[pltpu.SemaphoreType.DMA((6 * nw,)), pltpu.SemaphoreType.DMA((6 * nw,))],
        input_output_aliases={w: w for w in range(nw)},
    )(*slots)


def _sibling_halves(grads):
    nw = len(grads)

    def body(*refs):
        gs, outs = refs[:nw], refs[nw:2 * nw]
        send_sems, recv_sems = refs[2 * nw:]
        x, y, c, _ = _place()
        cps = []
        for w in range(nw):
            hr = gs[w].shape[1] // 2
            cp = _remote(gs[w].at[:, pl.ds((1 - c) * hr, hr), :], outs[w], send_sems.at[w], recv_sems.at[w],
                         (x, y, 1 - c))
            cp.start()
            cps.append(cp)
        for cp in cps:
            cp.wait()

    any_spec = pl.BlockSpec(memory_space=pl.ANY)
    return pl.pallas_call(
        body, name="rs_sibling_halves",
        in_specs=[any_spec] * nw, out_specs=[any_spec] * nw,
        out_shape=[jax.ShapeDtypeStruct((g.shape[0], g.shape[1] // 2, g.shape[2]), g.dtype) for g in grads],
        scratch_shapes=[pltpu.SemaphoreType.DMA((nw,)), pltpu.SemaphoreType.DMA((nw,))],
    )(*grads)


def _pair_sum_bf16(g, got, *, tm, name):
    S, R, C = g.shape
    hr = R // 2
    nt = hr // tm

    def body(c_ref, g_ref, got_ref, o_ref):
        del c_ref
        o_ref[...] = (g_ref[...] + got_ref[...]).astype(BF16)

    return pl.pallas_call(
        body, name=name,
        grid_spec=pltpu.PrefetchScalarGridSpec(
            num_scalar_prefetch=1, grid=(S, nt),
            in_specs=[pl.BlockSpec((None, tm, C), lambda s, i, c: (s, c[0] * nt + i, 0)),
                      pl.BlockSpec((None, tm, C), lambda s, i, c: (s, i, 0))],
            out_specs=pl.BlockSpec((None, tm, C), lambda s, i, c: (s, i, 0))),
        out_shape=jax.ShapeDtypeStruct((S, hr, C), BF16),
        compiler_params=_params(("parallel", "parallel")),
    )(_core_index(), g, got)


def _scatter_to_owners(pairs):
    nw = len(pairs)

    def body(*refs):
        qs, outs = refs[:nw], refs[nw:2 * nw]
        send_sems, recv_sems = refs[2 * nw:]
        x, y, c, chips = _place()
        cps = []
        for w in range(nw):
            for j, chip in enumerate(chips):
                cp = _remote(qs[w].at[2 * chip[0] + chip[1]], outs[w].at[j], send_sems.at[3 * w + j],
                             recv_sems.at[3 * w + j], (*chip, c))
                cp.start()
                cps.append(cp)
        for cp in cps:
            cp.wait()

    any_spec = pl.BlockSpec(memory_space=pl.ANY)
    return pl.pallas_call(
        body, name="rs_scatter_to_owners",
        in_specs=[any_spec] * nw, out_specs=[any_spec] * nw,
        out_shape=[jax.ShapeDtypeStruct((3,) + q.shape[1:], q.dtype) for q in pairs],
        scratch_shapes=[pltpu.SemaphoreType.DMA((3 * nw,)), pltpu.SemaphoreType.DMA((3 * nw,))],
    )(*pairs)


def _owner_sum(g, got, others, *, tm, name):
    S, R, C = g.shape
    hr = R // 2
    nt = hr // tm

    def body(idx_ref, g_ref, got_ref, o_ref_in, out_ref):
        del idx_ref
        acc = g_ref[...] + got_ref[...]
        for j in range(3):
            acc = acc + o_ref_in[j].astype(F32)
        out_ref[...] = acc

    return pl.pallas_call(
        body, name=name,
        grid_spec=pltpu.PrefetchScalarGridSpec(
            num_scalar_prefetch=1, grid=(nt,),
            in_specs=[pl.BlockSpec((None, tm, C), lambda i, p: (p[1], p[0] * nt + i, 0)),
                      pl.BlockSpec((None, tm, C), lambda i, p: (p[1], i, 0)),
                      pl.BlockSpec((3, tm, C), lambda i, p: (0, i, 0))],
            out_specs=pl.BlockSpec((tm, C), lambda i, p: (i, 0))),
        out_shape=jax.ShapeDtypeStruct((hr, C), F32),
        compiler_params=_params(("parallel",)),
    )(jnp.concatenate([_core_index(), _chip_index()]), g, got, others)


def _swap_halves(halves):
    nw = len(halves)

    def body(*refs):
        hs, outs = refs[:nw], refs[nw:2 * nw]
        send_sems, recv_sems = refs[2 * nw:]
        x, y, c, _ = _place()
        cps = []
        for w in range(nw):
            cp = _remote(hs[w], outs[w], send_sems.at[w], recv_sems.at[w], (x, y, 1 - c))
            cp.start()
            cps.append(cp)
        for cp in cps:
            cp.wait()

    any_spec = pl.BlockSpec(memory_space=pl.ANY)
    return pl.pallas_call(
        body, name="rs_swap_halves",
        in_specs=[any_spec] * nw, out_specs=[any_spec] * nw,
        out_shape=[jax.ShapeDtypeStruct(h.shape, h.dtype) for h in halves],
        scratch_shapes=[pltpu.SemaphoreType.DMA((nw,)), pltpu.SemaphoreType.DMA((nw,))],
    )(*halves)


def _all_reduce_small(packed):
    R, C = packed.shape

    def body(in_ref, out_ref, slots, send_sems, recv_sems):
        x, y, c, _ = _place()
        me = 4 * x + 2 * y + c
        cps = []
        for k in range(1, N_DEV):
            p = (me + k) % N_DEV
            cp = _remote(in_ref, slots.at[me], send_sems.at[k - 1], recv_sems.at[k - 1], (p // 4, (p // 2) % 2, p % 2))
            cp.start()
            cps.append(cp)
        slots[me] = in_ref[...]
        for k in range(1, N_DEV):
            src = (me + N_DEV - k) % N_DEV
            _remote(in_ref, slots.at[src], send_sems.at[k - 1], recv_sems.at[k - 1], (x, y, c)).wait_recv()
        for cp in cps:
            cp.wait_send()
        acc = slots[0]
        for d in range(1, N_DEV):
            acc = acc + slots[d]
        out_ref[...] = acc

    vmem = pl.BlockSpec(memory_space=pltpu.VMEM)
    return pl.pallas_call(
        body, name="all_reduce_small", in_specs=[vmem], out_specs=vmem,
        out_shape=jax.ShapeDtypeStruct((R, C), F32),
        scratch_shapes=[pltpu.VMEM((N_DEV, R, C), F32), pltpu.SemaphoreType.DMA((N_DEV - 1,)),
                        pltpu.SemaphoreType.DMA((N_DEV - 1,))],
        compiler_params=_params(),
    )(packed)


def _adamw_math(w, g, m, v):
    m = ADAM_B1 * m + (1.0 - ADAM_B1) * g
    v = ADAM_B2 * v + (1.0 - ADAM_B2) * (g * g)
    m_hat = m / (1.0 - ADAM_B1 ** ADAM_STEP)
    v_hat = v / (1.0 - ADAM_B2 ** ADAM_STEP)
    delta = -ADAM_LR * (m_hat / (jnp.sqrt(v_hat) + ADAM_EPS) + ADAM_WD * w)
    return delta, m, v


def _adamw(w, g, m, v, *, tm, name):
    R, C = w.shape

    def body(w_ref, g_ref, m_ref, v_ref, d_ref, nm_ref, nv_ref):
        d_ref[...], nm_ref[...], nv_ref[...] = _adamw_math(w_ref[...], g_ref[...], m_ref[...], v_ref[...])

    spec = pl.BlockSpec((tm, C), lambda i: (i, 0))
    return pl.pallas_call(
        body, name=name, grid=(R // tm,), in_specs=[spec] * 4, out_specs=[spec] * 3,
        out_shape=[jax.ShapeDtypeStruct((R, C), F32)] * 3, compiler_params=_params(("parallel",)),
    )(w, g, m, v)


def _adamw_halves(w, own, got, m, v, *, tm, name, rides=()):
    _, R, C = w.shape
    nt = (R // 2) // tm

    def body(w_ref, own_ref, got_ref, m_ref, v_ref, g_ref, d_ref, nm_ref, nv_ref):
        g = jnp.where(pl.program_id(0) == lax.axis_index("c"), own_ref[...], got_ref[...])
        g_ref[...] = g
        d_ref[...], nm_ref[...], nv_ref[...] = _adamw_math(w_ref[...], g, m_ref[...], v_ref[...])

    whole = pl.BlockSpec((None, tm, C), lambda h, i: (0, h * nt + i, 0))
    half = pl.BlockSpec((tm, C), lambda h, i: (i, 0))
    return _call(
        body, name=name, grid=(2, nt), in_specs=[whole, half, half, whole, whole], out_specs=[whole] * 4,
        out_shape=[jax.ShapeDtypeStruct((1, R, C), F32)] * 4, sem=("parallel", "parallel"), rides=rides,
    )(w, own, got, m, v)


def _adamw_small(w, slots, m, v, *, name):
    def body(w_ref, slots_ref, m_ref, v_ref, g_ref, d_ref, nm_ref, nv_ref):
        g = slots_ref[0]
        for d in range(1, N_DEV):
            g = g + slots_ref[d]
        g_ref[...] = g
        d_ref[...], nm_ref[...], nv_ref[...] = _adamw_math(w_ref[...], g, m_ref[...], v_ref[...])

    vmem = pl.BlockSpec(memory_space=pltpu.VMEM)
    return pl.pallas_call(
        body, name=name, in_specs=[vmem] * 4, out_specs=[vmem] * 4,
        out_shape=[jax.ShapeDtypeStruct(w.shape, F32)] * 4, compiler_params=_params(),
    )(w, slots, m, v)


SMALL = ["rel_bias_table", "mix_norm_g", "gate_norm_g", "gate_norm_b", "w_spatial", "b_spatial", "attn_sinks",
         "out_norm_a_g", "out_norm_b_g", "ffn_norm_g", "final_norm_g"]
SMALL_A = ["gate_norm_g", "gate_norm_b", "w_spatial", "b_spatial", "out_norm_a_g"]
SMALL_B = ["rel_bias_table", "mix_norm_g", "attn_sinks", "out_norm_b_g", "ffn_norm_g", "final_norm_g"]
LARGE = ["w_in", "w_out", "w_up", "w_down"]
ROW_TILE = {"w_in": 208, "w_out": 256, "w_up": 256, "w_down": 256}
WEIGHTS = ["rel_bias_table", "mix_norm_g", "w_in", "gate_norm_g", "gate_norm_b", "w_spatial", "b_spatial", "attn_sinks",
           "out_norm_a_g", "out_norm_b_g", "w_out", "ffn_norm_g", "w_up", "w_down", "final_norm_g"]
PACK_UNIT = 8 * 128


def _pack(parts):
    rows = []
    for p in parts:
        flat = p.reshape(-1)
        pad = (-flat.shape[0]) % PACK_UNIT
        rows.append(jnp.pad(flat, (0, pad)).reshape(-1, 128))
    return jnp.concatenate(rows, axis=0)


def _unpack(packed, like):
    out, row = [], 0
    for p in like:
        n = math.prod(p.shape)
        nrows = (n + PACK_UNIT - 1) // PACK_UNIT * 8
        out.append(packed[row:row + nrows].reshape(-1)[:n].reshape(p.shape))
        row += nrows
    return out


def kernel(x, rel_bias_table, mix_norm_g, w_in, gate_norm_g, gate_norm_b, w_spatial, b_spatial, attn_sinks, out_norm_a_g, out_norm_b_g, w_out, ffn_norm_g, w_up, w_down, final_norm_g, loss_target, m_rel_bias_table, m_mix_norm_g, m_w_in, m_gate_norm_g, m_gate_norm_b, m_w_spatial, m_b_spatial, m_attn_sinks, m_out_norm_a_g, m_out_norm_b_g, m_w_out, m_ffn_norm_g, m_w_up, m_w_down, m_final_norm_g, v_rel_bias_table, v_mix_norm_g, v_w_in, v_gate_norm_g, v_gate_norm_b, v_w_spatial, v_b_spatial, v_attn_sinks, v_out_norm_a_g, v_out_norm_b_g, v_w_out, v_ffn_norm_g, v_w_up, v_w_down, v_final_norm_g):
    args = dict(locals())
    wts = {n: args[n] for n in WEIGHTS}
    mom = {n: args["m_" + n] for n in WEIGHTS}
    var = {n: args["v_" + n] for n in WEIGHTS}
    sp = {n: wts[n] for n in SMALL}
    x2, tgt = x[0], loss_target[0]
    T = x2.shape[0]
    tm = min(512, T)
    tl = min(1024, T)
    tg = min(2048, T)
    lg = sp["gate_norm_g"].reshape(A_GROUPS, CHUNK)
    lb = sp["gate_norm_b"].reshape(A_GROUPS, CHUNK)
    wsp = sp["w_spatial"].reshape(A_GROUPS, CHUNK, CHUNK)
    bs_col = sp["b_spatial"].reshape(A_GROUPS, CHUNK, 1)
    sinks = sp["attn_sinks"].reshape(1, B_HEADS)
    ga = sp["out_norm_a_g"].reshape(1, A_WIDTH)
    gb = sp["out_norm_b_g"].reshape(1, B_WIDTH)
    g1 = sp["mix_norm_g"].reshape(1, D_MODEL)
    g2 = sp["ffn_norm_g"].reshape(1, D_MODEL)
    gf = sp["final_norm_g"].reshape(1, D_MODEL)

    def pair_sum(n, g, got):
        return _pair_sum_bf16(g, got, tm=ROW_TILE[n], name="rs_pair_sum_" + n)

    def owner_sum(n, g, got, others):
        return _owner_sum(g, got, others, tm=ROW_TILE[n], name="rs_owner_sum_" + n)

    for d in (wts, mom, var):
        d["w_in"] = jnp.swapaxes(d["w_in"], 1, 2)

    s_in, s_out, s_up, s_down = [_cast_into_slot(wts[n], tm=ROW_TILE[n], name="cast_" + n) for n in LARGE]
    ((g_in,),) = _carrier([_ride_gather(s_in)], name="gather_w_in")
    win_t = g_in.reshape(PROJ_WIDTH, D_MODEL)
    bias = _bias_build(sp["rel_bias_table"])
    (n1, proj), ((g_out,), (s_up,)) = _norm_matmul_wide(
        x2, g1, win_t, tm=tm, tn=PROJ_WIDTH // 2, name="in_proj", rides=[_ride_gather(s_out), _ride_gather(s_up, (0, 2, 8))])
    wo = g_out.reshape(A_WIDTH + B_WIDTH, D_MODEL)
    (mixed, mixed_t, ab), ((s_up,),) = _mixer_fwd(proj, lg, lb, wsp, bs_col, sinks, bias, ga, gb,
                                                  rides=[_ride_gather(s_up, (2, 7, 8))])
    h1, ((wu,), (s_down,)) = _matmul_res(mixed, wo, x2, tm=tl, tn=1024, tk=D_MODEL, prologue=_to_bf16, name="out_proj",
                                         rides=[_ride_gather(s_up, (7, 8, 8)), _ride_gather(s_down, (0, 1, 8))])
    (n2t, zp, z2, z2t), ((g_down,),) = _norm_matmul_sq(h1, g2, wu, tm=tl, tn=512, name="up_proj",
                                                       rides=[_ride_gather(s_down, (1, 8, 8))])
    wd = g_down.reshape(D_FF, D_MODEL)
    h2 = _matmul_res(z2, wd, h1, tm=tl, tn=1024, tk=2048, prologue=_to_bf16, name="down_proj")

    dh2, dh2b, dgf, loss = _loss_bwd(h2, tgt, gf, tm=tm)
    dzp = _matmul_nt(dh2b, wd, tm=tl, tn=1024, tk=D_MODEL, name="bwd_dz", extra=zp, epilogue=_sq_relu_grad,
                     out_dtype=BF16)
    dwd = _matmul_nn(z2t, dh2b, tmo=1024, tn=2048, tk=tg, name="grad_w_down")
    dwd = dwd.reshape(N_CHIPS, D_FF // N_CHIPS, D_MODEL)
    dwu, ((r_d,),) = _matmul_nn(n2t, dzp, tmo=1024, tn=2048, tk=tg, name="grad_w_up", shards=N_CHIPS,
                                rides=[_ride_sibling_halves(dwd)])
    q_d = pair_sum("w_down", dwd, r_d)
    dn2, ((o_d,), (r_u,)) = _matmul_nt(dzp, wu, tm=tl, tn=1024, tk=2048, name="bwd_dn2",
                                       rides=[_ride_scatter(q_d, None, (0, 6, 8)), _ride_sibling_halves(dwu)])
    q_u = pair_sum("w_up", dwu, r_u)
    (dh1, dh1b, dg2), ((o_d,),) = _rms_bwd_res(dn2, h1, g2, dh2, tm=tm, name="ffn_norm_bwd",
                                               rides=[_ride_scatter(q_d, o_d, (6, 8, 8))])
    h_d = owner_sum("w_down", dwd, r_d, o_d)
    dwo, ((o_u,),) = _matmul_nn(mixed_t, dh1b, tmo=1024, tn=2048, tk=tg, name="grad_w_out",
                                rides=[_ride_scatter(q_u, None, (0, 2, 8))])
    dwo = dwo.reshape(N_CHIPS, (A_WIDTH + B_WIDTH) // N_CHIPS, D_MODEL)
    dmixed, ((o_u,), (r_o,)) = _matmul_nt(dh1b, wo, tm=tl, tn=1024, tk=D_MODEL, name="bwd_dmixed",
                                          rides=[_ride_scatter(q_u, o_u, (2, 4, 8)), _ride_sibling_halves(dwo)])
    q_o = pair_sum("w_out", dwo, r_o)
    (duv, duv_t, dga, dwsp, dbs, dlg, dlb), ((o_u,),) = _gmlp_bwd(proj, ab, dmixed, ga, lg, lb, wsp, bs_col,
                                                                  rides=[_ride_scatter(q_u, o_u, (4, 8, 8))])
    h_u = owner_sum("w_up", dwu, r_u, o_u)
    (dq, dkv, dq_t, dkv_t, dgb, dsinks, dbias), ((o_o,), (w_d,)) = _attn_bwd(
        proj, ab, dmixed, gb, sinks, bias, rides=[_ride_scatter(q_o), _ride_swap(h_d)])
    h_o = owner_sum("w_out", dwo, r_o, o_o)
    dtable = _bias_grad(dbias)
    dproj = jnp.concatenate([duv, dq, dkv], axis=1)
    dproj_t = jnp.concatenate([duv_t, dq_t, dkv_t], axis=0)
    small = {"gate_norm_g": dlg, "gate_norm_b": dlb, "w_spatial": dwsp, "b_spatial": dbs, "out_norm_a_g": dga}
    dwin, ((w_o,), (w_u,), (slots_a,)) = _matmul_nn(
        dproj_t, n1, tmo=PROJ_WIDTH // 2, tn=2048, tk=tl, name="grad_w_in",
        rides=[_ride_swap(h_o), _ride_swap(h_u), _ride_small_to_all(_pack([small[n] for n in SMALL_A]))])
    dwin = dwin.reshape(N_CHIPS, PROJ_WIDTH // N_CHIPS, D_MODEL)
    dn1, ((r_i,),) = _matmul_nn(dproj, win_t, tmo=tl, tn=1024, tk=PROJ_WIDTH, name="bwd_dn1",
                                rides=[_ride_sibling_halves(dwin)])
    q_i = pair_sum("w_in", dwin, r_i)
    (dx, _, dg1), ((o_i,),) = _rms_bwd_res(dn1, x2, g1, dh1, tm=tm, name="mix_norm_bwd", rides=[_ride_scatter(q_i)])
    h_i = owner_sum("w_in", dwin, r_i, o_i)
    small.update({"rel_bias_table": dtable.reshape(N_BUCKETS, B_HEADS), "mix_norm_g": dg1, "attn_sinks": dsinks,
                  "out_norm_b_g": dgb, "ffn_norm_g": dg2, "final_norm_g": dgf})
    (w_i,), (slots_b,) = _carrier([_ride_swap(h_i), _ride_small_to_all(_pack([small[n] for n in SMALL_B]))],
                                  name="swap_w_in")

    out_g, out_d, out_m, out_v = {}, {}, {}, {}
    for n, h, s in zip(LARGE, [h_i, h_o, h_u, h_d], [w_i, w_o, w_u, w_d]):
        res = _adamw_halves(wts[n], h, s, mom[n], var[n], tm=ROW_TILE[n], name="adamw_" + n)
        if n == "w_in":
            res = [jnp.swapaxes(r, 1, 2) for r in res]
        out_g[n], out_d[n], out_m[n], out_v[n] = res
    for names, slots, tag in ((SMALL_A, slots_a, "a"), (SMALL_B, slots_b, "b")):
        like = [wts[n] for n in names]
        res = _adamw_small(_pack(like), slots, _pack([mom[n] for n in names]), _pack([var[n] for n in names]),
                           name="adamw_small_" + tag)
        for store, packed in zip((out_g, out_d, out_m, out_v), res):
            for n, val in zip(names, _unpack(packed, like)):
                store[n] = val

    total = lax.psum(loss[0, 0], ("x", "y", "c"))
    return (total, dx[None], *[out_g[n] for n in WEIGHTS], *[out_d[n] for n in WEIGHTS],
            *[out_m[n] for n in WEIGHTS], *[out_v[n] for n in WEIGHTS])
```

```python
import functools
import math

import numpy as np
import jax
import jax.numpy as jnp
from jax import lax
from jax.experimental import pallas as pl
from jax.experimental.pallas import tpu as pltpu

F32 = jnp.float32
BF16 = jnp.bfloat16

D_MODEL = 2048
CHUNK = 128
A_GROUPS = 8
A_WIDTH = 1024
HEAD_DIM = 64
B_HEADS = 16
Q_PER_KV = 8
B_WIDTH = 1024
KV_WIDTH = 128
PROJ_WIDTH = 3328
D_FF = 8192
N_BUCKETS = 32
EPS = 1e-5
NEG = -1e30
SCALE = HEAD_DIM ** -0.5
N_CHIPS = 4
N_DEV = 8

ADAM_LR = 0.001
ADAM_B1 = 0.9
ADAM_B2 = 0.999
ADAM_EPS = 1e-08
ADAM_WD = 0.01
ADAM_STEP = 10

VMEM_LIMIT = 56 * 1024 * 1024
MESH = pl.DeviceIdType.MESH


def _bucket_thresholds():
    d = np.arange(CHUNK)
    n_exact = N_BUCKETS // 2
    relf = np.maximum(d, n_exact).astype(np.float64)
    large = n_exact + (np.log(relf / n_exact) / math.log(CHUNK / n_exact) * (N_BUCKETS - n_exact)).astype(np.int32)
    bucket = np.where(d < n_exact, d, np.minimum(large, N_BUCKETS - 1))
    return [int(np.min(d[bucket >= b])) for b in range(1, N_BUCKETS)]


BUCKET_THR = _bucket_thresholds()


def _params(sem=None):
    return pltpu.CompilerParams(dimension_semantics=sem, vmem_limit_bytes=VMEM_LIMIT)


def _gelu(x):
    c = math.sqrt(2.0 / math.pi)
    return 0.5 * x * (1.0 + jnp.tanh(c * (x + 0.044715 * (x * x * x))))


def _gelu_and_grad(x):
    c = math.sqrt(2.0 / math.pi)
    x2 = x * x
    t = jnp.tanh(c * (x + 0.044715 * (x2 * x)))
    g = 0.5 * x * (1.0 + t)
    dg = 0.5 * (1.0 + t) + 0.5 * x * (1.0 - t * t) * (c * (1.0 + 3.0 * 0.044715 * x2))
    return g, dg


def _dot(a, b):
    return jnp.dot(a, b, preferred_element_type=F32)


def _dot_nt(a, b):
    return lax.dot_general(a, b, (((1,), (1,)), ((), ())), preferred_element_type=F32)


def _dot_tn(a, b):
    return lax.dot_general(a, b, (((0,), (0,)), ((), ())), preferred_element_type=F32)


def _rms_bwd(dn, h, g):
    r = lax.rsqrt(jnp.mean(h * h, axis=-1, keepdims=True) + EPS)
    w = dn * g
    dh = r * w - h * ((r * r * r) * jnp.mean(w * h, axis=-1, keepdims=True))
    return dh, r


def _place():
    x, y, c = lax.axis_index("x"), lax.axis_index("y"), lax.axis_index("c")
    chips = [(1 - x, y), (x, 1 - y), (1 - x, 1 - y)]
    return x, y, c, chips


def _remote(src, dst, send_sem, recv_sem, to):
    return pltpu.make_async_remote_copy(src_ref=src, dst_ref=dst, send_sem=send_sem, recv_sem=recv_sem,
                                        device_id=to, device_id_type=MESH)


class _Ride:
    def __init__(self, args, out_shape, n_sem, start, finish, mid=None, mid_frac=0.8, aliases=None):
        self.args, self.out_shape, self.n_sem = list(args), list(out_shape), n_sem
        self.start, self.mid, self.finish, self.mid_frac = start, mid, finish, mid_frac
        self.aliases = dict(aliases or {})


def _call(body, *, name, grid, in_specs, out_specs, out_shape, scratch_shapes=(), sem=None, rides=()):
    single = not isinstance(out_shape, (list, tuple))
    out_specs = [out_specs] if single else list(out_specs)
    out_shape = [out_shape] if single else list(out_shape)
    n_in, n_out, n_scr = len(in_specs), len(out_shape), len(scratch_shapes)
    r_in = [len(r.args) for r in rides]
    r_out = [len(r.out_shape) for r in rides]
    any_spec = pl.BlockSpec(memory_space=pl.ANY)
    aliases, off_i, off_o = {}, n_in, n_out
    for r in rides:
        for i, o in r.aliases.items():
            aliases[off_i + i] = off_o + o
        off_i += len(r.args)
        off_o += len(r.out_shape)
    steps = math.prod(grid)

    def wrapped(*refs):
        p = 0
        ins = refs[p:p + n_in]; p += n_in
        rins = refs[p:p + sum(r_in)]; p += sum(r_in)
        outs = refs[p:p + n_out]; p += n_out
        routs = refs[p:p + sum(r_out)]; p += sum(r_out)
        scr = refs[p:p + n_scr]; p += n_scr
        sems = refs[p:]
        parts, pi, po = [], 0, 0
        for k, r in enumerate(rides):
            parts.append((rins[pi:pi + r_in[k]], routs[po:po + r_out[k]], sems[2 * k], sems[2 * k + 1]))
            pi += r_in[k]
            po += r_out[k]
        lin = 0
        for d in range(len(grid)):
            lin = lin * grid[d] + pl.program_id(d)
        if rides:
            @pl.when(lin == 0)
            def _():
                for r, part in zip(rides, parts):
                    r.start(*part)
        body(*ins, *outs, *scr)
        for r, part in zip(rides, parts):
            if r.mid is not None:
                @pl.when(lin == min(steps - 1, int(r.mid_frac * steps)))
                def _(r=r, part=part):
                    r.mid(*part)
        if rides:
            @pl.when(lin == steps - 1)
            def _():
                for r, part in zip(rides, parts):
                    r.finish(*part)

    scratch = list(scratch_shapes)
    for r in rides:
        scratch += [pltpu.SemaphoreType.DMA((r.n_sem,)), pltpu.SemaphoreType.DMA((r.n_sem,))]
    if rides:
        sem = ("arbitrary",) * len(grid)
    res = pl.pallas_call(
        wrapped, name=name, grid=grid,
        in_specs=list(in_specs) + [any_spec] * sum(r_in),
        out_specs=out_specs + [any_spec] * sum(r_out),
        out_shape=out_shape + [s for r in rides for s in r.out_shape],
        scratch_shapes=scratch, input_output_aliases=aliases,
        compiler_params=_params(sem),
    )

    def run(*args):
        got = res(*args, *[a for r in rides for a in r.args])
        mine = got[0] if single else list(got[:n_out])
        if not rides:
            return mine
        rest, out = list(got[n_out:]), []
        for k in range(len(rides)):
            out.append(rest[:r_out[k]])
            rest = rest[r_out[k]:]
        return mine, out

    return run


def _ride_gather(slot, part=(0, 1), mid_frac=0.8):
    k0, k1, n = part if len(part) == 3 else (part[0], part[0] + 1, part[1])
    rows_n = (k1 - k0) * (slot.shape[1] // 2 // n)
    off = lambda c: c * (slot.shape[1] // 2) + k0 * (slot.shape[1] // 2 // n)

    def start(ins, outs, ss, rs):
        x, y, c, chips = _place()
        mine = outs[0].at[2 * x + y, pl.ds(off(c), rows_n), :]
        for j, chip in enumerate(chips):
            _remote(mine, mine, ss.at[j], rs.at[j], (*chip, c)).start()

    def mid(ins, outs, ss, rs):
        x, y, c, chips = _place()
        for j, chip in enumerate(chips):
            landed = outs[0].at[2 * chip[0] + chip[1], pl.ds(off(c), rows_n), :]
            _remote(landed, landed, ss.at[j], rs.at[j], (x, y, c)).wait_recv()
            _remote(landed, landed, ss.at[3 + j], rs.at[3 + j], (x, y, 1 - c)).start()

    def finish(ins, outs, ss, rs):
        x, y, c, chips = _place()
        for j, chip in enumerate(chips):
            other = outs[0].at[2 * chip[0] + chip[1], pl.ds(off(1 - c), rows_n), :]
            _remote(other, other, ss.at[3 + j], rs.at[3 + j], (x, y, c)).wait_recv()
        for j in range(6):
            piece = outs[0].at[0, pl.ds(0, rows_n), :]
            _remote(piece, piece, ss.at[j], rs.at[j], (x, y, c)).wait_send()

    return _Ride([slot], [jax.ShapeDtypeStruct(slot.shape, slot.dtype)], 6, start, finish, mid=mid,
                 mid_frac=mid_frac, aliases={0: 0})


def _ride_sibling_halves(g):
    S, R, C = g.shape
    hr = R // 2

    def copy(ins, outs, ss, rs):
        x, y, c, _ = _place()
        return _remote(ins[0].at[:, pl.ds((1 - c) * hr, hr), :], outs[0], ss.at[0], rs.at[0], (x, y, 1 - c))

    return _Ride([g], [jax.ShapeDtypeStruct((S, hr, C), g.dtype)], 1,
                 lambda *a: copy(*a).start(), lambda *a: copy(*a).wait())


def _ride_scatter(q, land=None, part=(0, 1)):
    k0, k1, n = part if len(part) == 3 else (part[0], part[0] + 1, part[1])
    rows_n = q.shape[1] // n
    rows = pl.ds(k0 * rows_n, (k1 - k0) * rows_n)

    def copies(ins, outs, ss, rs):
        x, y, c, chips = _place()
        return [_remote(ins[0].at[2 * chip[0] + chip[1], rows, :], outs[0].at[j, rows, :], ss.at[j], rs.at[j], (*chip, c))
                for j, chip in enumerate(chips)]

    def start(*a):
        for cp in copies(*a):
            cp.start()

    def finish(*a):
        for cp in copies(*a):
            cp.wait()

    shape = jax.ShapeDtypeStruct((3,) + q.shape[1:], q.dtype)
    if land is None:
        return _Ride([q], [shape], 3, start, finish)
    return _Ride([q, land], [shape], 3, start, finish, aliases={1: 0})


def _ride_to_sibling(a, halves=False, first=False):
    def copy(ins, outs, ss, rs):
        x, y, c, _ = _place()
        src = ins[0].at[:, 1 - c] if halves else (ins[0].at[0] if first else ins[0])
        return _remote(src, outs[0], ss.at[0], rs.at[0], (x, y, 1 - c))

    shape = (a.shape[0],) + a.shape[2:] if halves else (a.shape[1:] if first else a.shape)
    return _Ride([a], [jax.ShapeDtypeStruct(shape, a.dtype)], 1, lambda *a_: copy(*a_).start(), lambda *a_: copy(*a_).wait())


def _ride_swap(h):
    def copy(ins, outs, ss, rs):
        x, y, c, _ = _place()
        return _remote(ins[0], outs[0], ss.at[0], rs.at[0], (x, y, 1 - c))

    return _Ride([h], [jax.ShapeDtypeStruct(h.shape, h.dtype)], 1,
                 lambda *a: copy(*a).start(), lambda *a: copy(*a).wait())


def _mesh_place(p):
    return (p // 4, (p // 2) % 2, p % 2)


def _ride_small_to_all(packed):
    def copies(ins, outs, ss, rs):
        x, y, c, _ = _place()
        me = 4 * x + 2 * y + c
        return [_remote(ins[0], outs[0].at[me], ss.at[k - 1], rs.at[k - 1], _mesh_place((me + k) % N_DEV))
                for k in range(1, N_DEV)]

    def own(ins, outs, ss, rs):
        x, y, c, _ = _place()
        return pltpu.make_async_copy(ins[0], outs[0].at[4 * x + 2 * y + c], ss.at[N_DEV - 1])

    def start(*a):
        own(*a).start()
        for cp in copies(*a):
            cp.start()

    def finish(ins, outs, ss, rs):
        x, y, c, _ = _place()
        me = 4 * x + 2 * y + c
        for k in range(1, N_DEV):
            _remote(ins[0], outs[0].at[(me + N_DEV - k) % N_DEV], ss.at[k - 1], rs.at[k - 1], (x, y, c)).wait_recv()
        for cp in copies(ins, outs, ss, rs):
            cp.wait_send()
        own(ins, outs, ss, rs).wait()

    return _Ride([packed], [jax.ShapeDtypeStruct((N_DEV,) + packed.shape, packed.dtype)], N_DEV, start, finish)


def _carrier(rides, *, name):
    _, outs = _call(lambda: None, name=name, grid=(1,), in_specs=[], out_specs=[], out_shape=[], rides=rides)()
    return outs


def _sq_relu_bf16(z):
    z = jnp.maximum(z, 0.0)
    return (z * z).astype(BF16)


def _norm_bf16(a_ref, g_ref):
    xf = a_ref[...]
    r = lax.rsqrt(jnp.mean(xf * xf, axis=-1, keepdims=True) + EPS)
    return ((xf * r) * g_ref[...]).astype(BF16)


def _norm_matmul_wide(a, g, b, *, tm, tn, name, rides=()):
    T, K = a.shape
    N = b.shape[0]

    def body(a_ref, g_ref, b_ref, n_ref, o_ref):
        n = _norm_bf16(a_ref, g_ref)
        n_ref[...] = n
        o_ref[...] = _dot_nt(n, b_ref[...])

    return _call(
        body, name=name, grid=(N // tn, T // tm),
        in_specs=[pl.BlockSpec((tm, K), lambda j, i: (i, 0)), pl.BlockSpec((1, K), lambda j, i: (0, 0)),
                  pl.BlockSpec((tn, K), lambda j, i: (j, 0))],
        out_specs=[pl.BlockSpec((None, tm, K), lambda j, i: (j, i, 0)), pl.BlockSpec((tm, tn), lambda j, i: (i, j))],
        out_shape=[jax.ShapeDtypeStruct((N // tn, T, K), BF16), jax.ShapeDtypeStruct((T, N), F32)],
        sem=("arbitrary", "arbitrary"), rides=rides,
    )(a, g, b)


def _norm_matmul_sq(a, g, b, *, tm, tn, name, rides=()):
    T, K = a.shape
    per = b.shape[2] // tn
    N = b.shape[0] * b.shape[2]

    def body(a_ref, g_ref, b_ref, nt_ref, o_ref, z_ref, zt_ref, n_scr):
        @pl.when(pl.program_id(1) == 0)
        def _():
            n = _norm_bf16(a_ref, g_ref)
            n_scr[...] = n
            nt_ref[...] = n.T
        p = _dot(n_scr[...], b_ref[...])
        o_ref[...] = p
        z = _sq_relu_bf16(p)
        z_ref[...] = z
        zt_ref[...] = z.T

    return _call(
        body, name=name, grid=(T // tm, N // tn),
        in_specs=[pl.BlockSpec((tm, K), lambda i, j: (i, 0)), pl.BlockSpec((1, K), lambda i, j: (0, 0)),
                  pl.BlockSpec((None, K, tn), lambda i, j: (j // per, 0, j % per))],
        out_specs=[pl.BlockSpec((K, tm), lambda i, j: (0, i)), pl.BlockSpec((tm, tn), lambda i, j: (i, j)),
                   pl.BlockSpec((tm, tn), lambda i, j: (i, j)), pl.BlockSpec((tn, tm), lambda i, j: (j, i))],
        out_shape=[jax.ShapeDtypeStruct((K, T), BF16), jax.ShapeDtypeStruct((T, N), F32),
                   jax.ShapeDtypeStruct((T, N), BF16), jax.ShapeDtypeStruct((N, T), BF16)],
        scratch_shapes=[pltpu.VMEM((tm, K), BF16)],
        sem=("parallel", "arbitrary"), rides=rides,
    )(a, g, b)


def _grad_pair(at, at_sib, b, b_sib, *, cols_sharded, tmo, tk, name, rides=()):
    S, _, hr, T = at.shape
    C = b.shape[-1] // N_CHIPS if cols_sharded else b.shape[-1]
    nk = T // tk
    a_sel = (lambda s: 0) if cols_sharded else (lambda s: s)
    b_sel = (lambda s: s) if cols_sharded else (lambda s: 0)
    if b.ndim == 3:
        b_spec = pl.BlockSpec((None, tk, C), lambda s, i, k: (0, k, b_sel(s)))
    else:
        b_spec = pl.BlockSpec((tk, C), lambda s, i, k: (k, b_sel(s)))

    def body(a_ref, as_ref, b_ref, bs_ref, o_ref, ob_ref):
        k = pl.program_id(2)
        p = _dot(a_ref[...], b_ref[...]) + _dot(as_ref[...], bs_ref[...])

        @pl.when(k == 0)
        def _():
            o_ref[...] = p

        @pl.when(k > 0)
        def _():
            o_ref[...] += p

        @pl.when(k == nk - 1)
        def _():
            ob_ref[...] = o_ref[...].astype(BF16)

    out = pl.BlockSpec((None, tmo, C), lambda s, i, k: (s, i, 0))
    return _call(
        body, name=name, grid=(N_CHIPS, hr // tmo, nk),
        in_specs=[pl.BlockSpec((None, None, tmo, tk), lambda s, i, k: (a_sel(s), lax.axis_index("c"), i, k)),
                  pl.BlockSpec((None, tmo, tk), lambda s, i, k: (a_sel(s), i, k)),
                  b_spec, pl.BlockSpec((tk, C), lambda s, i, k: (k, b_sel(s)))],
        out_specs=[out, out],
        out_shape=[jax.ShapeDtypeStruct((N_CHIPS, hr, C), F32), jax.ShapeDtypeStruct((N_CHIPS, hr, C), BF16)],
        sem=("parallel", "parallel", "arbitrary"), rides=rides,
    )(at, at_sib, b, b_sib)


def _matmul_nn(at, b, *, tmo, tn, tk, name, shards=1, rides=()):
    M, T = at.shape[-2:]
    N = b.shape[-1]
    if at.ndim == 3:
        a_spec = pl.BlockSpec((None, tmo, tk), lambda i, j, k: (0, i, k))
    else:
        a_spec = pl.BlockSpec((tmo, tk), lambda i, j, k: (i, k))
    if b.ndim == 3:
        b_spec = pl.BlockSpec((None, tk, tn), lambda i, j, k: (0, k, j))
    else:
        b_spec = pl.BlockSpec((tk, tn), lambda i, j, k: (k, j))
    if shards > 1:
        per = (N // shards) // tn
        out_spec = pl.BlockSpec((None, tmo, tn), lambda i, j, k: (j // per, i, j % per))
        out_shape = jax.ShapeDtypeStruct((shards, M, N // shards), F32)
    else:
        out_spec = pl.BlockSpec((tmo, tn), lambda i, j, k: (i, j))
        out_shape = jax.ShapeDtypeStruct((M, N), F32)

    def body(a_ref, b_ref, o_ref):
        k = pl.program_id(2)
        p = _dot(a_ref[...], b_ref[...])

        @pl.when(k == 0)
        def _():
            o_ref[...] = p

        @pl.when(k > 0)
        def _():
            o_ref[...] += p

    return _call(
        body, name=name, grid=(M // tmo, N // tn, T // tk),
        in_specs=[a_spec, b_spec],
        out_specs=out_spec, out_shape=out_shape,
        sem=("parallel", "parallel", "arbitrary"), rides=rides,
    )(at, b)


def _to_bf16(v):
    return v.astype(BF16)


def _matmul_res(a, b, res, *, tm, tn, tk, prologue, name, rides=()):
    T, K = a.shape
    N = b.shape[1]

    def body(a_ref, b_ref, res_ref, o_ref):
        k = pl.program_id(2)
        p = _dot(prologue(a_ref[...]), b_ref[...])

        @pl.when(k == 0)
        def _():
            o_ref[...] = res_ref[...] + p

        @pl.when(k > 0)
        def _():
            o_ref[...] += p

    return _call(
        body, name=name, grid=(T // tm, N // tn, K // tk),
        in_specs=[pl.BlockSpec((tm, tk), lambda i, j, k: (i, k)), pl.BlockSpec((tk, tn), lambda i, j, k: (k, j)),
                  pl.BlockSpec((tm, tn), lambda i, j, k: (i, j))],
        out_specs=pl.BlockSpec((tm, tn), lambda i, j, k: (i, j)),
        out_shape=jax.ShapeDtypeStruct((T, N), F32),
        sem=("parallel", "parallel", "arbitrary"), rides=rides,
    )(a, b, res)


def _matmul_nt(a, b, *, tm, tn, tk, name, extra=None, epilogue=None, out_dtype=F32, rides=()):
    T, K = a.shape
    if b.ndim == 3:
        per = b.shape[2] // tk
        N = b.shape[1]
        b_spec = pl.BlockSpec((None, tn, tk), lambda i, j, k: (k // per, j, k % per))
    else:
        N = b.shape[0]
        b_spec = pl.BlockSpec((tn, tk), lambda i, j, k: (j, k))
    nk = K // tk
    assert out_dtype == F32 or nk == 1
    in_specs = [pl.BlockSpec((tm, tk), lambda i, j, k: (i, k)), b_spec]
    args = [a, b]
    if extra is not None:
        in_specs.append(pl.BlockSpec((tm, tn), lambda i, j, k: (i, j)))
        args.append(extra)

    def body(*refs):
        a_ref, b_ref = refs[0], refs[1]
        o_ref = refs[-1]
        p = _dot_nt(a_ref[...].astype(BF16), b_ref[...])
        if nk == 1:
            if epilogue is not None:
                p = epilogue(p, refs[2][...])
            o_ref[...] = p.astype(out_dtype)
        else:
            k = pl.program_id(2)

            @pl.when(k == 0)
            def _():
                o_ref[...] = p

            @pl.when(k > 0)
            def _():
                o_ref[...] += p

    return _call(
        body, name=name, grid=(T // tm, N // tn, nk),
        in_specs=in_specs,
        out_specs=pl.BlockSpec((tm, tn), lambda i, j, k: (i, j)),
        out_shape=jax.ShapeDtypeStruct((T, N), out_dtype),
        sem=("parallel", "parallel", "arbitrary"), rides=rides,
    )(*args)


def _matmul_tn(a, b, *, tmo, tn, tk, name, a_prologue=_to_bf16, shards=1, rides=()):
    T, M = a.shape
    N = b.shape[1]
    if shards > 1:
        per = (N // shards) // tn
        out_spec = pl.BlockSpec((None, tmo, tn), lambda i, j, k: (j // per, i, j % per))
        out_shape = jax.ShapeDtypeStruct((shards, M, N // shards), F32)
    else:
        out_spec = pl.BlockSpec((tmo, tn), lambda i, j, k: (i, j))
        out_shape = jax.ShapeDtypeStruct((M, N), F32)

    def body(a_ref, b_ref, o_ref):
        k = pl.program_id(2)
        p = _dot_tn(a_prologue(a_ref[...]), b_ref[...].astype(BF16))

        @pl.when(k == 0)
        def _():
            o_ref[...] = p

        @pl.when(k > 0)
        def _():
            o_ref[...] += p

    return _call(
        body, name=name, grid=(M // tmo, N // tn, T // tk),
        in_specs=[pl.BlockSpec((tk, tmo), lambda i, j, k: (k, i)), pl.BlockSpec((tk, tn), lambda i, j, k: (k, j))],
        out_specs=out_spec, out_shape=out_shape,
        sem=("parallel", "parallel", "arbitrary"), rides=rides,
    )(a, b)


def _loss_bwd(h2, tgt, g, *, tm):
    T, D = h2.shape

    def body(h_ref, t_ref, g_ref, dh_ref, dhb_ref, dg_ref, loss_ref):
        @pl.when(pl.program_id(0) == 0)
        def _():
            dg_ref[...] = jnp.zeros_like(dg_ref)
            loss_ref[...] = jnp.zeros_like(loss_ref)
        h = h_ref[...]
        gg = g_ref[...]
        r = lax.rsqrt(jnp.mean(h * h, axis=-1, keepdims=True) + EPS)
        hn = h * r
        err = hn * gg - t_ref[...]
        loss_ref[...] += 0.5 * jnp.sum(jnp.mean(err * err, axis=-1, keepdims=True), axis=0, keepdims=True)
        dy = err * (1.0 / D)
        dg_ref[...] += jnp.sum(dy * hn, axis=0, keepdims=True)
        w = dy * gg
        dh = r * w - h * ((r * r * r) * jnp.mean(w * h, axis=-1, keepdims=True))
        dh_ref[...] = dh
        dhb_ref[...] = dh.astype(BF16)

    tile = pl.BlockSpec((tm, D), lambda i: (i, 0))
    return pl.pallas_call(
        body, name="loss_bwd", grid=(T // tm,),
        in_specs=[tile, tile, pl.BlockSpec((1, D), lambda i: (0, 0))],
        out_specs=[tile, tile, pl.BlockSpec((1, D), lambda i: (0, 0)), pl.BlockSpec((1, 1), lambda i: (0, 0))],
        out_shape=[jax.ShapeDtypeStruct((T, D), F32), jax.ShapeDtypeStruct((T, D), BF16),
                   jax.ShapeDtypeStruct((1, D), F32), jax.ShapeDtypeStruct((1, 1), F32)],
        compiler_params=_params(("arbitrary",)),
    )(h2, tgt, g)


def _rms_bwd_res(dn, h, g, dres, *, tm, name, rides=()):
    T, D = h.shape

    def body(dn_ref, h_ref, g_ref, dres_ref, dh_ref, dhb_ref, dg_ref):
        @pl.when(pl.program_id(0) == 0)
        def _():
            dg_ref[...] = jnp.zeros_like(dg_ref)
        h_ = h_ref[...]
        dn_ = dn_ref[...]
        dh, r = _rms_bwd(dn_, h_, g_ref[...])
        dg_ref[...] += jnp.sum(dn_ * (h_ * r), axis=0, keepdims=True)
        dh = dres_ref[...] + dh
        dh_ref[...] = dh
        dhb_ref[...] = dh.astype(BF16)

    tile = pl.BlockSpec((tm, D), lambda i: (i, 0))
    return _call(
        body, name=name, grid=(T // tm,),
        in_specs=[tile, tile, pl.BlockSpec((1, D), lambda i: (0, 0)), tile],
        out_specs=[tile, tile, pl.BlockSpec((1, D), lambda i: (0, 0))],
        out_shape=[jax.ShapeDtypeStruct((T, D), F32), jax.ShapeDtypeStruct((T, D), BF16),
                   jax.ShapeDtypeStruct((1, D), F32)],
        sem=("arbitrary",), rides=rides,
    )(dn, h, g, dres)


def _rel_distance():
    i = lax.broadcasted_iota(jnp.int32, (CHUNK, 2 * CHUNK), 0)
    j = lax.broadcasted_iota(jnp.int32, (CHUNK, 2 * CHUNK), 1)
    return i + CHUNK - j


def _bias_build(table):
    def body(tab_ref, o_ref):
        rel = _rel_distance()
        ge = [rel >= t for t in BUCKET_THR]
        for h in range(B_HEADS):
            cur = jnp.full((CHUNK, 2 * CHUNK), tab_ref[0, h], F32)
            for b in range(1, N_BUCKETS):
                cur = jnp.where(ge[b - 1], tab_ref[b, h], cur)
            o_ref[h] = cur

    return pl.pallas_call(
        body, name="bias_build",
        in_specs=[pl.BlockSpec(memory_space=pltpu.SMEM)],
        out_specs=pl.BlockSpec(memory_space=pltpu.VMEM),
        out_shape=jax.ShapeDtypeStruct((B_HEADS, CHUNK, 2 * CHUNK), F32),
    )(table)


def _bias_grad(dbias):
    def body(db_ref, o_ref, acc_ref):
        rel = _rel_distance()
        lo = [0] + BUCKET_THR
        hi = BUCKET_THR + [CHUNK]
        for b in range(N_BUCKETS):
            m = (rel >= lo[b]) & (rel < hi[b])
            for h in range(B_HEADS):
                row = b * B_HEADS + h
                acc_ref[row:row + 1, :] = jnp.sum(jnp.where(m, db_ref[h], 0.0), axis=0, keepdims=True)
        o_ref[...] = jnp.sum(acc_ref[...], axis=1, keepdims=True)

    return pl.pallas_call(
        body, name="bias_grad",
        in_specs=[pl.BlockSpec(memory_space=pltpu.VMEM)],
        out_specs=pl.BlockSpec(memory_space=pltpu.VMEM),
        out_shape=jax.ShapeDtypeStruct((N_BUCKETS * B_HEADS, 1), F32),
        scratch_shapes=[pltpu.VMEM((N_BUCKETS * B_HEADS, 2 * CHUNK), F32)],
    )(dbias)


def _causal_mask():
    t = lax.broadcasted_iota(jnp.int32, (CHUNK, CHUNK), 0)
    s = lax.broadcasted_iota(jnp.int32, (CHUNK, CHUNK), 1)
    return s <= t


def _band_mask(n):
    rel = _rel_distance()
    j = lax.broadcasted_iota(jnp.int32, (CHUNK, 2 * CHUNK), 1)
    return (rel >= 0) & (rel < CHUNK) & ((n > 0) | (j >= CHUNK))


def _gate_forward(u, v, lg, lb, wc, bs):
    ug = _gelu(u)
    vg = _gelu(v)
    mu = jnp.mean(vg, axis=-1, keepdims=True)
    xc = vg - mu
    rstd = lax.rsqrt(jnp.mean(xc * xc, axis=-1, keepdims=True) + EPS)
    xhat = xc * rstd
    vl = (xhat * lg + lb).astype(BF16)
    mixed = _dot(wc, vl) + bs
    return ug, xhat, rstd, vl, mixed


def _softmax_scores(qk, bias, mask, sink):
    s = qk * SCALE + bias
    s = jnp.where(mask, s, NEG)
    m = jnp.maximum(jnp.max(s, axis=-1, keepdims=True), sink)
    p = jnp.exp(s - m)
    e_sink = jnp.exp(sink - m)
    inv = 1.0 / (jnp.sum(p, axis=-1, keepdims=True) + e_sink)
    return p * inv, e_sink * inv


PAIRS = Q_PER_KV // 2


def _head(g, pr, e):
    return g * Q_PER_KV + 2 * pr + e


def _stack_pairs(ref, g, col0=0):
    w = 2 * HEAD_DIM
    return jnp.concatenate([ref[:, col0 + (g * PAIRS + pr) * w:col0 + (g * PAIRS + pr + 1) * w] for pr in range(PAIRS)],
                           axis=0)


def _low_lanes():
    return lax.broadcasted_iota(jnp.int32, (2 * CHUNK, 2 * HEAD_DIM), 1) < HEAD_DIM


def _band_operands(kv_prev, kv_cur):
    band = jnp.concatenate([kv_prev, kv_cur], axis=0)
    low = _low_lanes()
    ops = []
    for cat in (band[:, :KV_WIDTH], band[:, KV_WIDTH:]):
        rol = pltpu.roll(cat, HEAD_DIM, 1)
        ops.append([[jnp.where(low if e == 0 else ~low, cat if g == e else rol, 0.0).astype(BF16) for e in range(2)]
                    for g in range(2)])
    return ops


def _mixer_fwd(proj, lg, lb, wsp, bs_col, sinks, bias, ga, gb, rides=()):
    T = proj.shape[0]
    nb = T // CHUNK

    def body(u_ref, v_ref, q_ref, kvc_ref, kvp_ref, lg_ref, lb_ref, w_ref, bs_ref, sink_ref, bias_ref,
             ga_ref, gb_ref, mixed_ref, mixed_t_ref, ab_ref):
        n = pl.program_id(0)
        causal = _causal_mask()
        ssq = jnp.zeros((CHUNK, 1), F32)
        for g in range(A_GROUPS):
            cols = slice(g * CHUNK, (g + 1) * CHUNK)
            wc = jnp.where(causal, w_ref[g], 0.0).astype(BF16)
            ug, _, _, _, mixed = _gate_forward(u_ref[:, cols], v_ref[:, cols], lg_ref[g:g + 1, :], lb_ref[g:g + 1, :],
                                               wc, bs_ref[g])
            a = ug * mixed
            ab_ref[:, cols] = a
            ssq = ssq + jnp.sum(a * a, axis=-1, keepdims=True)
        ra = lax.rsqrt(ssq * (1.0 / A_WIDTH) + EPS)
        mixed_ref[:, :A_WIDTH] = ((ab_ref[:, :A_WIDTH] * ra) * ga_ref[...]).astype(BF16)

        mask = _band_mask(n)
        kops, vops = _band_operands(kvp_ref[...], kvc_ref[...])
        ssq = jnp.zeros((CHUNK, 1), F32)
        for g in range(B_HEADS // Q_PER_KV):
            qst = _stack_pairs(q_ref, g).astype(BF16)
            o_st = jnp.zeros((PAIRS * CHUNK, 2 * HEAD_DIM), F32)
            for e in range(2):
                s_all = _dot_nt(qst, kops[g][e])
                ps = []
                for pr in range(PAIRS):
                    h = _head(g, pr, e)
                    p, _ = _softmax_scores(s_all[pr * CHUNK:(pr + 1) * CHUNK], bias_ref[h], mask, sink_ref[0, h])
                    ps.append(p.astype(BF16))
                o_st = o_st + _dot(jnp.concatenate(ps, axis=0), vops[g][e])
            for pr in range(PAIRS):
                o = o_st[pr * CHUNK:(pr + 1) * CHUNK]
                c0 = A_WIDTH + (g * PAIRS + pr) * 2 * HEAD_DIM
                ab_ref[:, c0:c0 + 2 * HEAD_DIM] = o
                ssq = ssq + jnp.sum(o * o, axis=-1, keepdims=True)
        rb = lax.rsqrt(ssq * (1.0 / B_WIDTH) + EPS)
        mixed_ref[:, A_WIDTH:] = ((ab_ref[:, A_WIDTH:] * rb) * gb_ref[...]).astype(BF16)
        mixed_t_ref[...] = mixed_ref[...].T

    full = lambda *shape: pl.BlockSpec(shape, lambda n: (0,) * len(shape))
    return _call(
        body, name="mixer_fwd", grid=(nb,),
        in_specs=[pl.BlockSpec((CHUNK, A_WIDTH), lambda n: (n, 0)),
                  pl.BlockSpec((CHUNK, A_WIDTH), lambda n: (n, 1)),
                  pl.BlockSpec((CHUNK, B_WIDTH), lambda n: (n, 2)),
                  pl.BlockSpec((CHUNK, 2 * KV_WIDTH), lambda n: (n, 12)),
                  pl.BlockSpec((CHUNK, 2 * KV_WIDTH), lambda n: (jnp.maximum(n - 1, 0), 12)),
                  full(A_GROUPS, CHUNK), full(A_GROUPS, CHUNK), full(A_GROUPS, CHUNK, CHUNK), full(A_GROUPS, CHUNK, 1),
                  pl.BlockSpec(memory_space=pltpu.SMEM), full(B_HEADS, CHUNK, 2 * CHUNK),
                  full(1, A_WIDTH), full(1, B_WIDTH)],
        out_specs=[pl.BlockSpec((CHUNK, D_MODEL), lambda n: (n, 0)), pl.BlockSpec((D_MODEL, CHUNK), lambda n: (0, n)),
                   pl.BlockSpec((CHUNK, D_MODEL), lambda n: (n, 0))],
        out_shape=[jax.ShapeDtypeStruct((T, D_MODEL), BF16), jax.ShapeDtypeStruct((D_MODEL, T), BF16),
                   jax.ShapeDtypeStruct((T, D_MODEL), F32)],
        sem=("parallel",), rides=rides,
    )(proj, proj, proj, proj, proj, lg, lb, wsp, bs_col, sinks, bias, ga, gb)


def _gmlp_bwd(proj, ab, dmixed, ga, lg, lb, wsp, bs_col, rides=()):
    T = proj.shape[0]
    nb = T // CHUNK

    def body(u_ref, v_ref, a_ref, dna_ref, ga_ref, lg_ref, lb_ref, w_ref, bs_ref,
             dp_ref, dpt_ref, dga_ref, dw_ref, dbs_ref, dlg_ref, dlb_ref):
        @pl.when(pl.program_id(0) == 0)
        def _():
            for r in (dga_ref, dw_ref, dbs_ref, dlg_ref, dlb_ref):
                r[...] = jnp.zeros_like(r)
        causal = _causal_mask()
        a_all = a_ref[...]
        dna = dna_ref[...]
        da_all, ra = _rms_bwd(dna, a_all, ga_ref[...])
        dga_ref[...] += jnp.sum(dna * (a_all * ra), axis=0, keepdims=True)
        for g in range(A_GROUPS):
            cols = slice(g * CHUNK, (g + 1) * CHUNK)
            wc = jnp.where(causal, w_ref[g], 0.0).astype(BF16)
            lgg = lg_ref[g:g + 1, :]
            u = u_ref[:, cols]
            v = v_ref[:, cols]
            ug, xhat, rstd, vl, mixed = _gate_forward(u, v, lgg, lb_ref[g:g + 1, :], wc, bs_ref[g])
            da = da_all[:, cols]
            dug = da * mixed
            dmg = da * ug
            dmg_b = dmg.astype(BF16)
            dbs_ref[g] += jnp.sum(dmg, axis=-1, keepdims=True)
            dw_ref[g] += jnp.where(causal, _dot_nt(dmg_b, vl), 0.0)
            dvl = _dot_tn(wc, dmg_b)
            dlg_ref[g:g + 1, :] += jnp.sum(dvl * xhat, axis=0, keepdims=True)
            dlb_ref[g:g + 1, :] += jnp.sum(dvl, axis=0, keepdims=True)
            dxh = dvl * lgg
            dvg = rstd * (dxh - jnp.mean(dxh, axis=-1, keepdims=True)
                          - xhat * jnp.mean(dxh * xhat, axis=-1, keepdims=True))
            _, gu = _gelu_and_grad(u)
            _, gv = _gelu_and_grad(v)
            dp_ref[:, cols] = (dug * gu).astype(BF16)
            dp_ref[:, A_WIDTH + g * CHUNK:A_WIDTH + (g + 1) * CHUNK] = (dvg * gv).astype(BF16)
        dpt_ref[...] = dp_ref[...].T

    full = lambda *shape: pl.BlockSpec(shape, lambda n: (0,) * len(shape))
    return _call(
        body, name="gmlp_bwd", grid=(nb,),
        in_specs=[pl.BlockSpec((CHUNK, A_WIDTH), lambda n: (n, 0)),
                  pl.BlockSpec((CHUNK, A_WIDTH), lambda n: (n, 1)),
                  pl.BlockSpec((CHUNK, A_WIDTH), lambda n: (n, 0)),
                  pl.BlockSpec((CHUNK, A_WIDTH), lambda n: (n, 0)),
                  full(1, A_WIDTH), full(A_GROUPS, CHUNK), full(A_GROUPS, CHUNK), full(A_GROUPS, CHUNK, CHUNK),
                  full(A_GROUPS, CHUNK, 1)],
        out_specs=[pl.BlockSpec((CHUNK, 2 * A_WIDTH), lambda n: (n, 0)), pl.BlockSpec((2 * A_WIDTH, CHUNK), lambda n: (0, n)),
                   full(1, A_WIDTH), full(A_GROUPS, CHUNK, CHUNK), full(A_GROUPS, CHUNK, 1),
                   full(A_GROUPS, CHUNK), full(A_GROUPS, CHUNK)],
        out_shape=[jax.ShapeDtypeStruct((T, 2 * A_WIDTH), BF16), jax.ShapeDtypeStruct((2 * A_WIDTH, T), BF16),
                   jax.ShapeDtypeStruct((1, A_WIDTH), F32), jax.ShapeDtypeStruct((A_GROUPS, CHUNK, CHUNK), F32),
                   jax.ShapeDtypeStruct((A_GROUPS, CHUNK, 1), F32), jax.ShapeDtypeStruct((A_GROUPS, CHUNK), F32),
                   jax.ShapeDtypeStruct((A_GROUPS, CHUNK), F32)],
        sem=("arbitrary",), rides=rides,
    )(proj, proj, ab, dmixed, ga, lg, lb, wsp, bs_col)


def _attn_bwd(proj, ab, dmixed, gb, sinks, bias, rides=()):
    T = proj.shape[0]
    nb = T // CHUNK
    qn = lambda n: jnp.minimum(n, nb - 1)

    def body(q_ref, kvc_ref, kvp_ref, o_ref, dnb_ref, gb_ref, sink_ref, bias_ref,
             dq_ref, dkv_ref, dqt_ref, dkvt_ref, dgb_ref, dsink_ref, dbias_ref, carry_ref, sacc_ref):
        n = pl.program_id(0)

        @pl.when(n == 0)
        def _():
            carry_ref[...] = jnp.zeros_like(carry_ref)
            sacc_ref[...] = jnp.zeros_like(sacc_ref)
            dgb_ref[...] = jnp.zeros_like(dgb_ref)
            dbias_ref[...] = jnp.zeros_like(dbias_ref)

        @pl.when(n < nb)
        def _():
            mask = _band_mask(n)
            o_all = o_ref[...]
            dnb = dnb_ref[...]
            do_all, rb = _rms_bwd(dnb, o_all, gb_ref[...])
            dgb_ref[...] += jnp.sum(dnb * (o_all * rb), axis=0, keepdims=True)
            kops, vops = _band_operands(kvp_ref[...], kvc_ref[...])
            low = _low_lanes()
            halves = []
            for g in range(B_HEADS // Q_PER_KV):
                qst = _stack_pairs(q_ref, g).astype(BF16)
                dost = _stack_pairs(do_all, g).astype(BF16)
                dq_st = jnp.zeros((PAIRS * CHUNK, 2 * HEAD_DIM), F32)
                dk_e, dv_e = [], []
                for e in range(2):
                    s_all = _dot_nt(qst, kops[g][e])
                    dp_all = _dot_nt(dost, vops[g][e])
                    ps, dsrs = [], []
                    for pr in range(PAIRS):
                        h = _head(g, pr, e)
                        rows = slice(pr * CHUNK, (pr + 1) * CHUNK)
                        p, p_sink = _softmax_scores(s_all[rows], bias_ref[h], mask, sink_ref[0, h])
                        dp = dp_all[rows]
                        delta = jnp.sum(p * dp, axis=-1, keepdims=True)
                        ds = p * (dp - delta)
                        sacc_ref[:, h:h + 1] += -(p_sink * delta)
                        dbias_ref[h] += ds
                        ps.append(p.astype(BF16))
                        dsrs.append((ds * SCALE).astype(BF16))
                    dsr_all = jnp.concatenate(dsrs, axis=0)
                    dq_st = dq_st + _dot(dsr_all, kops[g][e])
                    dk_e.append(_dot_tn(dsr_all, qst))
                    dv_e.append(_dot_tn(jnp.concatenate(ps, axis=0), dost))
                for pr in range(PAIRS):
                    c0 = (g * PAIRS + pr) * 2 * HEAD_DIM
                    dq_ref[:, c0:c0 + 2 * HEAD_DIM] = dq_st[pr * CHUNK:(pr + 1) * CHUNK].astype(BF16)
                halves.append((dk_e, dv_e))
            tiles = []
            for t in range(2):
                g0, g1 = halves[0][t], halves[1][t]
                tiles.append(jnp.where(low, g0[0] + pltpu.roll(g0[1], HEAD_DIM, 1), pltpu.roll(g1[0], HEAD_DIM, 1) + g1[1]))
            dband = jnp.concatenate(tiles, axis=1)
            dkv = (carry_ref[...] + dband[:CHUNK]).astype(BF16)
            dkv_ref[...] = dkv
            dkvt_ref[...] = dkv.T
            dqt_ref[...] = dq_ref[...].T
            carry_ref[...] = dband[CHUNK:]

        @pl.when(n == nb)
        def _():
            dkv = carry_ref[...].astype(BF16)
            dkv_ref[...] = dkv
            dkvt_ref[...] = dkv.T
            dsink_ref[...] = jnp.sum(sacc_ref[...], axis=0, keepdims=True)

    full = lambda *shape: pl.BlockSpec(shape, lambda n: (0,) * len(shape))
    return _call(
        body, name="attn_bwd", grid=(nb + 1,),
        in_specs=[pl.BlockSpec((CHUNK, B_WIDTH), lambda n: (qn(n), 2)),
                  pl.BlockSpec((CHUNK, 2 * KV_WIDTH), lambda n: (qn(n), 12)),
                  pl.BlockSpec((CHUNK, 2 * KV_WIDTH), lambda n: (jnp.maximum(qn(n) - 1, 0), 12)),
                  pl.BlockSpec((CHUNK, B_WIDTH), lambda n: (qn(n), 1)),
                  pl.BlockSpec((CHUNK, B_WIDTH), lambda n: (qn(n), 1)),
                  full(1, B_WIDTH), pl.BlockSpec(memory_space=pltpu.SMEM), full(B_HEADS, CHUNK, 2 * CHUNK)],
        out_specs=[pl.BlockSpec((CHUNK, B_WIDTH), lambda n: (qn(n), 0)),
                   pl.BlockSpec((CHUNK, 2 * KV_WIDTH), lambda n: (jnp.maximum(n - 1, 0), 0)),
                   pl.BlockSpec((B_WIDTH, CHUNK), lambda n: (0, qn(n))),
                   pl.BlockSpec((2 * KV_WIDTH, CHUNK), lambda n: (0, jnp.maximum(n - 1, 0))),
                   full(1, B_WIDTH), full(1, B_HEADS), full(B_HEADS, CHUNK, 2 * CHUNK)],
        out_shape=[jax.ShapeDtypeStruct((T, B_WIDTH), BF16), jax.ShapeDtypeStruct((T, 2 * KV_WIDTH), BF16),
                   jax.ShapeDtypeStruct((B_WIDTH, T), BF16), jax.ShapeDtypeStruct((2 * KV_WIDTH, T), BF16),
                   jax.ShapeDtypeStruct((1, B_WIDTH), F32), jax.ShapeDtypeStruct((1, B_HEADS), F32),
                   jax.ShapeDtypeStruct((B_HEADS, CHUNK, 2 * CHUNK), F32)],
        scratch_shapes=[pltpu.VMEM((CHUNK, 2 * KV_WIDTH), F32), pltpu.VMEM((CHUNK, B_HEADS), F32)],
        sem=("arbitrary",), rides=rides,
    )(proj, proj, proj, ab, dmixed, gb, sinks, bias)


def _sq_relu_grad(acc, z):
    return acc * (2.0 * jnp.maximum(z, 0.0))


def _local_step(x, tgt, sp, win, wo, wu, wd):
    T = x.shape[0]
    tm = min(512, T)
    tk = min(512, T)
    lg = sp["gate_norm_g"].reshape(A_GROUPS, CHUNK)
    lb = sp["gate_norm_b"].reshape(A_GROUPS, CHUNK)
    wsp = sp["w_spatial"].reshape(A_GROUPS, CHUNK, CHUNK)
    bs_col = sp["b_spatial"].reshape(A_GROUPS, CHUNK, 1)
    sinks = sp["attn_sinks"].reshape(1, B_HEADS)
    ga = sp["out_norm_a_g"].reshape(1, A_WIDTH)
    gb = sp["out_norm_b_g"].reshape(1, B_WIDTH)
    g1 = sp["mix_norm_g"].reshape(1, D_MODEL)
    g2 = sp["ffn_norm_g"].reshape(1, D_MODEL)
    gf = sp["final_norm_g"].reshape(1, D_MODEL)

    bias = _bias_build(sp["rel_bias_table"])
    n1, proj = _norm_matmul(x, g1, win, tm=tm, tn=PROJ_WIDTH // 2, name="in_proj")
    mixed, ab = _mixer_fwd(proj, lg, lb, wsp, bs_col, sinks, bias, ga, gb)
    h1 = _matmul_res(mixed, wo, x, tm=tm, tn=1024, tk=D_MODEL, prologue=_to_bf16, name="out_proj")
    n2, zp = _norm_matmul(h1, g2, wu, tm=tm, tn=1024, name="up_proj")
    h2 = _matmul_res(zp, wd, h1, tm=tm, tn=1024, tk=2048, prologue=_sq_relu_bf16, name="down_proj")

    dh2, dgf, loss = _loss_bwd(h2, tgt, gf, tm=tm)
    dzp = _matmul_nt(dh2, wd, tm=tm, tn=1024, tk=D_MODEL, name="bwd_dz", extra=zp, epilogue=_sq_relu_grad,
                     out_dtype=BF16)
    dwd = _matmul_tn(zp, dh2, tmo=1024, tn=1024, tk=tk, name="grad_w_down", a_prologue=_sq_relu_bf16)
    dwu = _matmul_tn(n2, dzp, tmo=1024, tn=1024, tk=tk, name="grad_w_up", shards=N_CHIPS)
    dn2 = _matmul_nt(dzp, wu, tm=tm, tn=1024, tk=2048, name="bwd_dn2")
    dh1, dg2 = _rms_bwd_res(dn2, h1, g2, dh2, tm=tm, name="ffn_norm_bwd")
    dwo = _matmul_tn(mixed, dh1, tmo=1024, tn=1024, tk=tk, name="grad_w_out")
    dmixed = _matmul_nt(dh1, wo, tm=tm, tn=1024, tk=D_MODEL, name="bwd_dmixed")
    duv, dga, dwsp, dbs, dlg, dlb = _gmlp_bwd(proj, ab, dmixed, ga, lg, lb, wsp, bs_col)
    dq, dkv, dgb, dsinks, dbias = _attn_bwd(proj, ab, dmixed, gb, sinks, bias)
    dtable = _bias_grad(dbias)
    dproj = jnp.concatenate([duv, dq, dkv], axis=1)
    dwin = _matmul_tn(n1, dproj, tmo=1024, tn=PROJ_WIDTH // 2, tk=tk, name="grad_w_in")
    dn1 = _matmul_nt(dproj, win, tm=tm, tn=1024, tk=PROJ_WIDTH, name="bwd_dn1")
    dx, dg1 = _rms_bwd_res(dn1, x, g1, dh1, tm=tm, name="mix_norm_bwd")

    small = {
        "rel_bias_table": dtable.reshape(N_BUCKETS, B_HEADS), "mix_norm_g": dg1, "gate_norm_g": dlg, "gate_norm_b": dlb,
        "w_spatial": dwsp, "b_spatial": dbs, "attn_sinks": dsinks, "out_norm_a_g": dga, "out_norm_b_g": dgb,
        "ffn_norm_g": dg2, "final_norm_g": dgf,
    }
    return loss, dx, (dwin, dwo, dwu, dwd), small


def _place():
    x, y, c = lax.axis_index("x"), lax.axis_index("y"), lax.axis_index("c")
    chips = [(1 - x, y), (x, 1 - y), (1 - x, 1 - y)]
    return x, y, c, chips


def _remote(src, dst, send_sem, recv_sem, to):
    return pltpu.make_async_remote_copy(src_ref=src, dst_ref=dst, send_sem=send_sem, recv_sem=recv_sem,
                                        device_id=to, device_id_type=MESH)


def _core_index():
    return lax.axis_index("c").astype(jnp.int32).reshape(1)


def _chip_index():
    return (2 * lax.axis_index("x") + lax.axis_index("y")).astype(jnp.int32).reshape(1)


def _cast_into_slot(w, *, tm, name):
    _, R, C = w.shape

    def body(me_ref, w_ref, o_ref):
        del me_ref
        o_ref[...] = w_ref[...].astype(BF16)

    return pl.pallas_call(
        body, name=name,
        grid_spec=pltpu.PrefetchScalarGridSpec(
            num_scalar_prefetch=1, grid=(R // tm,),
            in_specs=[pl.BlockSpec((None, tm, C), lambda i, me: (0, i, 0))],
            out_specs=pl.BlockSpec((None, tm, C), lambda i, me: (me[0], i, 0))),
        out_shape=jax.ShapeDtypeStruct((N_CHIPS, R, C), BF16), compiler_params=_params(("parallel",)),
    )(_chip_index(), w)


def _gather_weights(slots):
    nw = len(slots)

    def body(*refs):
        fulls = refs[nw:2 * nw]
        send_sems, recv_sems = refs[2 * nw:]
        x, y, c, chips = _place()
        me = 2 * x + y
        sends = []
        for w in range(nw):
            hr = fulls[w].shape[1] // 2
            rows = pl.ds(c * hr, hr)
            for j, chip in enumerate(chips):
                mine = fulls[w].at[me, rows, :]
                cp = _remote(mine, mine, send_sems.at[6 * w + j], recv_sems.at[6 * w + j], (*chip, c))
                cp.start()
                sends.append(cp)
        for w in range(nw):
            hr = fulls[w].shape[1] // 2
            rows = pl.ds(c * hr, hr)
            for j, chip in enumerate(chips):
                landed = fulls[w].at[2 * chip[0] + chip[1], rows, :]
                _remote(landed, landed, send_sems.at[6 * w + j], recv_sems.at[6 * w + j], (x, y, c)).wait_recv()
                cp = _remote(landed, landed, send_sems.at[6 * w + 3 + j], recv_sems.at[6 * w + 3 + j], (x, y, 1 - c))
                cp.start()
                sends.append(cp)
        for w in range(nw):
            hr = fulls[w].shape[1] // 2
            rows = pl.ds((1 - c) * hr, hr)
            for j, chip in enumerate(chips):
                other = fulls[w].at[2 * chip[0] + chip[1], rows, :]
                _remote(other, other, send_sems.at[6 * w + 3 + j], recv_sems.at[6 * w + 3 + j], (x, y, c)).wait_recv()
        for cp in sends:
            cp.wait_send()

    any_spec = pl.BlockSpec(memory_space=pl.ANY)
    return pl.pallas_call(
        body, name="gather_weights",
        in_specs=[any_spec] * nw, out_specs=[any_spec] * nw,
        out_shape=[jax.ShapeDtypeStruct(s.shape, s.dtype) for s in slots],
        scratch_shapes=[pltpu.SemaphoreType.DMA((6 * nw,)), pltpu.SemaphoreType.DMA((6 * nw,))],
        input_output_aliases={w: w for w in range(nw)},
    )(*slots)


def _sibling_halves(grads):
    nw = len(grads)

    def body(*refs):
        gs, outs = refs[:nw], refs[nw:2 * nw]
        send_sems, recv_sems = refs[2 * nw:]
        x, y, c, _ = _place()
        cps = []
        for w in range(nw):
            hr = gs[w].shape[1] // 2
            cp = _remote(gs[w].at[:, pl.ds((1 - c) * hr, hr), :], outs[w], send_sems.at[w], recv_sems.at[w],
                         (x, y, 1 - c))
            cp.start()
            cps.append(cp)
        for cp in cps:
            cp.wait()

    any_spec = pl.BlockSpec(memory_space=pl.ANY)
    return pl.pallas_call(
        body, name="rs_sibling_halves",
        in_specs=[any_spec] * nw, out_specs=[any_spec] * nw,
        out_shape=[jax.ShapeDtypeStruct((g.shape[0], g.shape[1] // 2, g.shape[2]), g.dtype) for g in grads],
        scratch_shapes=[pltpu.SemaphoreType.DMA((nw,)), pltpu.SemaphoreType.DMA((nw,))],
    )(*grads)


def _pair_sum_bf16(g, got, *, tm, name):
    S, R, C = g.shape
    hr = R // 2
    nt = hr // tm

    def body(c_ref, g_ref, got_ref, o_ref):
        del c_ref
        o_ref[...] = (g_ref[...] + got_ref[...]).astype(BF16)

    return pl.pallas_call(
        body, name=name,
        grid_spec=pltpu.PrefetchScalarGridSpec(
            num_scalar_prefetch=1, grid=(S, nt),
            in_specs=[pl.BlockSpec((None, tm, C), lambda s, i, c: (s, c[0] * nt + i, 0)),
                      pl.BlockSpec((None, tm, C), lambda s, i, c: (s, i, 0))],
            out_specs=pl.BlockSpec((None, tm, C), lambda s, i, c: (s, i, 0))),
        out_shape=jax.ShapeDtypeStruct((S, hr, C), BF16),
        compiler_params=_params(("parallel", "parallel")),
    )(_core_index(), g, got)


def _scatter_to_owners(pairs):
    nw = len(pairs)

    def body(*refs):
        qs, outs = refs[:nw], refs[nw:2 * nw]
        send_sems, recv_sems = refs[2 * nw:]
        x, y, c, chips = _place()
        cps = []
        for w in range(nw):
            for j, chip in enumerate(chips):
                cp = _remote(qs[w].at[2 * chip[0] + chip[1]], outs[w].at[j], send_sems.at[3 * w + j],
                             recv_sems.at[3 * w + j], (*chip, c))
                cp.start()
                cps.append(cp)
        for cp in cps:
            cp.wait()

    any_spec = pl.BlockSpec(memory_space=pl.ANY)
    return pl.pallas_call(
        body, name="rs_scatter_to_owners",
        in_specs=[any_spec] * nw, out_specs=[any_spec] * nw,
        out_shape=[jax.ShapeDtypeStruct((3,) + q.shape[1:], q.dtype) for q in pairs],
        scratch_shapes=[pltpu.SemaphoreType.DMA((3 * nw,)), pltpu.SemaphoreType.DMA((3 * nw,))],
    )(*pairs)


def _owner_total(gh, others, *, tm, name):
    _, hr, C = gh.shape

    def body(me_ref, g_ref, o_ref_in, out_ref):
        del me_ref
        acc = g_ref[...]
        for j in range(3):
            acc = acc + o_ref_in[j].astype(F32)
        out_ref[...] = acc

    return pl.pallas_call(
        body, name=name,
        grid_spec=pltpu.PrefetchScalarGridSpec(
            num_scalar_prefetch=1, grid=(hr // tm,),
            in_specs=[pl.BlockSpec((None, tm, C), lambda i, me: (me[0], i, 0)),
                      pl.BlockSpec((3, tm, C), lambda i, me: (0, i, 0))],
            out_specs=pl.BlockSpec((tm, C), lambda i, me: (i, 0))),
        out_shape=jax.ShapeDtypeStruct((hr, C), F32),
        compiler_params=_params(("parallel",)),
    )(_chip_index(), gh, others)


def _owner_sum(g, got, others, *, tm, name):
    S, R, C = g.shape
    hr = R // 2
    nt = hr // tm

    def body(idx_ref, g_ref, got_ref, o_ref_in, out_ref):
        del idx_ref
        acc = g_ref[...] + got_ref[...]
        for j in range(3):
            acc = acc + o_ref_in[j].astype(F32)
        out_ref[...] = acc

    return pl.pallas_call(
        body, name=name,
        grid_spec=pltpu.PrefetchScalarGridSpec(
            num_scalar_prefetch=1, grid=(nt,),
            in_specs=[pl.BlockSpec((None, tm, C), lambda i, p: (p[1], p[0] * nt + i, 0)),
                      pl.BlockSpec((None, tm, C), lambda i, p: (p[1], i, 0)),
                      pl.BlockSpec((3, tm, C), lambda i, p: (0, i, 0))],
            out_specs=pl.BlockSpec((tm, C), lambda i, p: (i, 0))),
        out_shape=jax.ShapeDtypeStruct((hr, C), F32),
        compiler_params=_params(("parallel",)),
    )(jnp.concatenate([_core_index(), _chip_index()]), g, got, others)


def _swap_halves(halves):
    nw = len(halves)

    def body(*refs):
        hs, outs = refs[:nw], refs[nw:2 * nw]
        send_sems, recv_sems = refs[2 * nw:]
        x, y, c, _ = _place()
        cps = []
        for w in range(nw):
            cp = _remote(hs[w], outs[w], send_sems.at[w], recv_sems.at[w], (x, y, 1 - c))
            cp.start()
            cps.append(cp)
        for cp in cps:
            cp.wait()

    any_spec = pl.BlockSpec(memory_space=pl.ANY)
    return pl.pallas_call(
        body, name="rs_swap_halves",
        in_specs=[any_spec] * nw, out_specs=[any_spec] * nw,
        out_shape=[jax.ShapeDtypeStruct(h.shape, h.dtype) for h in halves],
        scratch_shapes=[pltpu.SemaphoreType.DMA((nw,)), pltpu.SemaphoreType.DMA((nw,))],
    )(*halves)


def _all_reduce_small(packed):
    R, C = packed.shape

    def body(in_ref, out_ref, slots, send_sems, recv_sems):
        x, y, c, _ = _place()
        me = 4 * x + 2 * y + c
        cps = []
        for k in range(1, N_DEV):
            p = (me + k) % N_DEV
            cp = _remote(in_ref, slots.at[me], send_sems.at[k - 1], recv_sems.at[k - 1], (p // 4, (p // 2) % 2, p % 2))
            cp.start()
            cps.append(cp)
        slots[me] = in_ref[...]
        for k in range(1, N_DEV):
            src = (me + N_DEV - k) % N_DEV
            _remote(in_ref, slots.at[src], send_sems.at[k - 1], recv_sems.at[k - 1], (x, y, c)).wait_recv()
        for cp in cps:
            cp.wait_send()
        acc = slots[0]
        for d in range(1, N_DEV):
            acc = acc + slots[d]
        out_ref[...] = acc

    vmem = pl.BlockSpec(memory_space=pltpu.VMEM)
    return pl.pallas_call(
        body, name="all_reduce_small", in_specs=[vmem], out_specs=vmem,
        out_shape=jax.ShapeDtypeStruct((R, C), F32),
        scratch_shapes=[pltpu.VMEM((N_DEV, R, C), F32), pltpu.SemaphoreType.DMA((N_DEV - 1,)),
                        pltpu.SemaphoreType.DMA((N_DEV - 1,))],
        compiler_params=_params(),
    )(packed)


def _adamw_math(w, g, m, v):
    m = ADAM_B1 * m + (1.0 - ADAM_B1) * g
    v = ADAM_B2 * v + (1.0 - ADAM_B2) * (g * g)
    m_hat = m / (1.0 - ADAM_B1 ** ADAM_STEP)
    v_hat = v / (1.0 - ADAM_B2 ** ADAM_STEP)
    delta = -ADAM_LR * (m_hat / (jnp.sqrt(v_hat) + ADAM_EPS) + ADAM_WD * w)
    return delta, m, v


def _adamw(w, g, m, v, *, tm, name):
    R, C = w.shape

    def body(w_ref, g_ref, m_ref, v_ref, d_ref, nm_ref, nv_ref):
        d_ref[...], nm_ref[...], nv_ref[...] = _adamw_math(w_ref[...], g_ref[...], m_ref[...], v_ref[...])

    spec = pl.BlockSpec((tm, C), lambda i: (i, 0))
    return pl.pallas_call(
        body, name=name, grid=(R // tm,), in_specs=[spec] * 4, out_specs=[spec] * 3,
        out_shape=[jax.ShapeDtypeStruct((R, C), F32)] * 3, compiler_params=_params(("parallel",)),
    )(w, g, m, v)


def _adamw_halves(w, own, got, m, v, *, tm, name, rides=()):
    _, R, C = w.shape
    nt = (R // 2) // tm

    def body(w_ref, own_ref, got_ref, m_ref, v_ref, g_ref, d_ref, nm_ref, nv_ref):
        g = jnp.where(pl.program_id(0) == lax.axis_index("c"), own_ref[...], got_ref[...])
        g_ref[...] = g
        d_ref[...], nm_ref[...], nv_ref[...] = _adamw_math(w_ref[...], g, m_ref[...], v_ref[...])

    whole = pl.BlockSpec((None, tm, C), lambda h, i: (0, h * nt + i, 0))
    half = pl.BlockSpec((tm, C), lambda h, i: (i, 0))
    return _call(
        body, name=name, grid=(2, nt), in_specs=[whole, half, half, whole, whole], out_specs=[whole] * 4,
        out_shape=[jax.ShapeDtypeStruct((1, R, C), F32)] * 4, sem=("parallel", "parallel"), rides=rides,
    )(w, own, got, m, v)


def _adamw_small(w, slots, m, v, *, name):
    def body(w_ref, slots_ref, m_ref, v_ref, g_ref, d_ref, nm_ref, nv_ref):
        g = slots_ref[0]
        for d in range(1, N_DEV):
            g = g + slots_ref[d]
        g_ref[...] = g
        d_ref[...], nm_ref[...], nv_ref[...] = _adamw_math(w_ref[...], g, m_ref[...], v_ref[...])

    vmem = pl.BlockSpec(memory_space=pltpu.VMEM)
    return pl.pallas_call(
        body, name=name, in_specs=[vmem] * 4, out_specs=[vmem] * 4,
        out_shape=[jax.ShapeDtypeStruct(w.shape, F32)] * 4, compiler_params=_params(),
    )(w, slots, m, v)


SMALL = ["rel_bias_table", "mix_norm_g", "gate_norm_g", "gate_norm_b", "w_spatial", "b_spatial", "attn_sinks",
         "out_norm_a_g", "out_norm_b_g", "ffn_norm_g", "final_norm_g"]
SMALL_A = ["gate_norm_g", "gate_norm_b", "w_spatial", "b_spatial", "out_norm_a_g"]
SMALL_B = ["rel_bias_table", "mix_norm_g", "attn_sinks", "out_norm_b_g", "ffn_norm_g", "final_norm_g"]
LARGE = ["w_in", "w_out", "w_up", "w_down"]
ROW_TILE = {"w_in": 208, "w_out": 256, "w_up": 256, "w_down": 256}
WEIGHTS = ["rel_bias_table", "mix_norm_g", "w_in", "gate_norm_g", "gate_norm_b", "w_spatial", "b_spatial", "attn_sinks",
           "out_norm_a_g", "out_norm_b_g", "w_out", "ffn_norm_g", "w_up", "w_down", "final_norm_g"]
PACK_UNIT = 8 * 128


def _pack(parts):
    rows = []
    for p in parts:
        flat = p.reshape(-1)
        pad = (-flat.shape[0]) % PACK_UNIT
        rows.append(jnp.pad(flat, (0, pad)).reshape(-1, 128))
    return jnp.concatenate(rows, axis=0)


def _unpack(packed, like):
    out, row = [], 0
    for p in like:
        n = math.prod(p.shape)
        nrows = (n + PACK_UNIT - 1) // PACK_UNIT * 8
        out.append(packed[row:row + nrows].reshape(-1)[:n].reshape(p.shape))
        row += nrows
    return out


def kernel(x, rel_bias_table, mix_norm_g, w_in, gate_norm_g, gate_norm_b, w_spatial, b_spatial, attn_sinks, out_norm_a_g, out_norm_b_g, w_out, ffn_norm_g, w_up, w_down, final_norm_g, loss_target, m_rel_bias_table, m_mix_norm_g, m_w_in, m_gate_norm_g, m_gate_norm_b, m_w_spatial, m_b_spatial, m_attn_sinks, m_out_norm_a_g, m_out_norm_b_g, m_w_out, m_ffn_norm_g, m_w_up, m_w_down, m_final_norm_g, v_rel_bias_table, v_mix_norm_g, v_w_in, v_gate_norm_g, v_gate_norm_b, v_w_spatial, v_b_spatial, v_attn_sinks, v_out_norm_a_g, v_out_norm_b_g, v_w_out, v_ffn_norm_g, v_w_up, v_w_down, v_final_norm_g):
    args = dict(locals())
    wts = {n: args[n] for n in WEIGHTS}
    mom = {n: args["m_" + n] for n in WEIGHTS}
    var = {n: args["v_" + n] for n in WEIGHTS}
    sp = {n: wts[n] for n in SMALL}
    x2, tgt = x[0], loss_target[0]
    T = x2.shape[0]
    tm = min(512, T)
    tl = min(1024, T)
    tg = min(2048, T)
    lg = sp["gate_norm_g"].reshape(A_GROUPS, CHUNK)
    lb = sp["gate_norm_b"].reshape(A_GROUPS, CHUNK)
    wsp = sp["w_spatial"].reshape(A_GROUPS, CHUNK, CHUNK)
    bs_col = sp["b_spatial"].reshape(A_GROUPS, CHUNK, 1)
    sinks = sp["attn_sinks"].reshape(1, B_HEADS)
    ga = sp["out_norm_a_g"].reshape(1, A_WIDTH)
    gb = sp["out_norm_b_g"].reshape(1, B_WIDTH)
    g1 = sp["mix_norm_g"].reshape(1, D_MODEL)
    g2 = sp["ffn_norm_g"].reshape(1, D_MODEL)
    gf = sp["final_norm_g"].reshape(1, D_MODEL)

    def owner_total(n, gh, others):
        return _owner_total(gh, others, tm=ROW_TILE[n], name="rs_owner_total_" + n)

    def halves_view(at, shards):
        return at.reshape(shards, 2, at.shape[0] // shards // 2, at.shape[1])

    for d in (wts, mom, var):
        d["w_in"] = jnp.swapaxes(d["w_in"], 1, 2)

    s_in, s_out, s_up, s_down = [_cast_into_slot(wts[n], tm=ROW_TILE[n], name="cast_" + n) for n in LARGE]
    ((g_in,),) = _carrier([_ride_gather(s_in)], name="gather_w_in")
    win_t = g_in.reshape(PROJ_WIDTH, D_MODEL)
    bias = _bias_build(sp["rel_bias_table"])
    (n1, proj), ((g_out,), (s_up,)) = _norm_matmul_wide(
        x2, g1, win_t, tm=tm, tn=PROJ_WIDTH // 2, name="in_proj", rides=[_ride_gather(s_out), _ride_gather(s_up, (0, 2, 8))])
    wo = g_out.reshape(A_WIDTH + B_WIDTH, D_MODEL)
    (mixed, mixed_t, ab), ((s_up,), (n1_sib,)) = _mixer_fwd(
        proj, lg, lb, wsp, bs_col, sinks, bias, ga, gb, rides=[_ride_gather(s_up, (2, 7, 8)), _ride_to_sibling(n1, first=True)])
    mixed_t = halves_view(mixed_t, N_CHIPS)
    h1, ((wu,), (s_down,), (mixed_t_sib,)) = _matmul_res(
        mixed, wo, x2, tm=tl, tn=1024, tk=D_MODEL, prologue=_to_bf16, name="out_proj",
        rides=[_ride_gather(s_up, (7, 8, 8)), _ride_gather(s_down, (0, 1, 8)), _ride_to_sibling(mixed_t, halves=True)])
    (n2t, zp, z2, z2t), ((g_down,),) = _norm_matmul_sq(h1, g2, wu, tm=tl, tn=512, name="up_proj",
                                                       rides=[_ride_gather(s_down, (1, 8, 8))])
    wd = g_down.reshape(D_FF, D_MODEL)
    n2t, z2t = halves_view(n2t, 1), halves_view(z2t, N_CHIPS)
    h2, ((n2t_sib,), (z2t_sib,)) = _matmul_res(
        z2, wd, h1, tm=tl, tn=1024, tk=2048, prologue=_to_bf16, name="down_proj",
        rides=[_ride_to_sibling(n2t, halves=True), _ride_to_sibling(z2t, halves=True)])

    dh2, dh2b, dgf, loss = _loss_bwd(h2, tgt, gf, tm=tm)
    dzp, ((dh2b_sib,),) = _matmul_nt(dh2b, wd, tm=tl, tn=1024, tk=D_MODEL, name="bwd_dz", extra=zp,
                                     epilogue=_sq_relu_grad, out_dtype=BF16, rides=[_ride_to_sibling(dh2b)])
    (gd, gdb), ((dzp_sib,),) = _grad_pair(z2t, z2t_sib, dh2b, dh2b_sib, cols_sharded=False, tmo=512, tk=tl,
                                          name="grad_w_down", rides=[_ride_to_sibling(dzp)])
    (gu, gub), ((o_d,),) = _grad_pair(n2t, n2t_sib, dzp, dzp_sib, cols_sharded=True, tmo=512, tk=tl,
                                      name="grad_w_up", rides=[_ride_scatter(gdb, None, (0, 7, 8))])
    dn2, ((o_d,), (o_u,)) = _matmul_nt(dzp, wu, tm=tl, tn=1024, tk=2048, name="bwd_dn2",
                                       rides=[_ride_scatter(gdb, o_d, (7, 8, 8)), _ride_scatter(gub, None, (0, 6, 8))])
    h_d = owner_total("w_down", gd, o_d)
    (dh1, dh1b, dg2), ((o_u,),) = _rms_bwd_res(dn2, h1, g2, dh2, tm=tm, name="ffn_norm_bwd",
                                               rides=[_ride_scatter(gub, o_u, (6, 8, 8))])
    h_u = owner_total("w_up", gu, o_u)
    dmixed, ((dh1b_sib,), (w_d,)) = _matmul_nt(dh1b, wo, tm=tl, tn=1024, tk=D_MODEL, name="bwd_dmixed",
                                               rides=[_ride_to_sibling(dh1b), _ride_swap(h_d)])
    (go, gob), ((w_u,),) = _grad_pair(mixed_t, mixed_t_sib, dh1b, dh1b_sib, cols_sharded=False, tmo=256, tk=tl,
                                      name="grad_w_out", rides=[_ride_swap(h_u)])
    (duv, duv_t, dga, dwsp, dbs, dlg, dlb), ((o_o,),) = _gmlp_bwd(proj, ab, dmixed, ga, lg, lb, wsp, bs_col,
                                                                  rides=[_ride_scatter(gob)])
    h_o = owner_total("w_out", go, o_o)
    small = {"gate_norm_g": dlg, "gate_norm_b": dlb, "w_spatial": dwsp, "b_spatial": dbs, "out_norm_a_g": dga}
    (dq, dkv, dq_t, dkv_t, dgb, dsinks, dbias), ((w_o,),) = _attn_bwd(proj, ab, dmixed, gb, sinks, bias,
                                                                      rides=[_ride_swap(h_o)])
    dtable = _bias_grad(dbias)
    dproj = jnp.concatenate([duv, dq, dkv], axis=1)
    dproj_t = halves_view(jnp.concatenate([duv_t, dq_t, dkv_t], axis=0), N_CHIPS)
    ((dproj_t_sib,),) = _carrier([_ride_to_sibling(dproj_t, halves=True)], name="trade_dproj_t")
    (gi, gib), ((slots_a,),) = _grad_pair(
        dproj_t, dproj_t_sib, n1, n1_sib, cols_sharded=False, tmo=PROJ_WIDTH // N_CHIPS // 2, tk=tl, name="grad_w_in",
        rides=[_ride_small_to_all(_pack([small[n] for n in SMALL_A]))])
    dn1, ((o_i,),) = _matmul_nn(dproj, win_t, tmo=tl, tn=1024, tk=PROJ_WIDTH, name="bwd_dn1", rides=[_ride_scatter(gib)])
    h_i = owner_total("w_in", gi, o_i)
    dx, _, dg1 = _rms_bwd_res(dn1, x2, g1, dh1, tm=tm, name="mix_norm_bwd")
    small.update({"rel_bias_table": dtable.reshape(N_BUCKETS, B_HEADS), "mix_norm_g": dg1, "attn_sinks": dsinks,
                  "out_norm_b_g": dgb, "ffn_norm_g": dg2, "final_norm_g": dgf})
    (w_i,), (slots_b,) = _carrier([_ride_swap(h_i), _ride_small_to_all(_pack([small[n] for n in SMALL_B]))],
                                  name="swap_w_in")

    out_g, out_d, out_m, out_v = {}, {}, {}, {}
    for n, h, s in zip(LARGE, [h_i, h_o, h_u, h_d], [w_i, w_o, w_u, w_d]):
        res = _adamw_halves(wts[n], h, s, mom[n], var[n], tm=ROW_TILE[n], name="adamw_" + n)
        if n == "w_in":
            res = [jnp.swapaxes(r, 1, 2) for r in res]
        out_g[n], out_d[n], out_m[n], out_v[n] = res
    for names, slots, tag in ((SMALL_A, slots_a, "a"), (SMALL_B, slots_b, "b")):
        like = [wts[n] for n in names]
        res = _adamw_small(_pack(like), slots, _pack([mom[n] for n in names]), _pack([var[n] for n in names]),
                           name="adamw_small_" + tag)
        for store, packed in zip((out_g, out_d, out_m, out_v), res):
            for n, val in zip(names, _unpack(packed, like)):
                store[n] = val

    total = lax.psum(loss[0, 0], ("x", "y", "c"))
    return (total, dx[None], *[out_g[n] for n in WEIGHTS], *[out_d[n] for n in WEIGHTS],
            *[out_m[n] for n in WEIGHTS], *[out_v[n] for n in WEIGHTS])
```

```python
import functools
import math

import numpy as np
import jax
import jax.numpy as jnp
from jax import lax
from jax.experimental import pallas as pl
from jax.experimental.pallas import tpu as pltpu

F32 = jnp.float32
BF16 = jnp.bfloat16

D_MODEL = 2048
CHUNK = 128
A_GROUPS = 8
A_WIDTH = 1024
HEAD_DIM = 64
B_HEADS = 16
Q_PER_KV = 8
B_WIDTH = 1024
KV_WIDTH = 128
PROJ_WIDTH = 3328
D_FF = 8192
N_BUCKETS = 32
EPS = 1e-5
NEG = -1e30
SCALE = HEAD_DIM ** -0.5
N_CHIPS = 4
N_DEV = 8

ADAM_LR = 0.001
ADAM_B1 = 0.9
ADAM_B2 = 0.999
ADAM_EPS = 1e-08
ADAM_WD = 0.01
ADAM_STEP = 10

VMEM_LIMIT = 60 * 1024 * 1024
MESH = pl.DeviceIdType.MESH


def _bucket_thresholds():
    d = np.arange(CHUNK)
    n_exact = N_BUCKETS // 2
    relf = np.maximum(d, n_exact).astype(np.float64)
    large = n_exact + (np.log(relf / n_exact) / math.log(CHUNK / n_exact) * (N_BUCKETS - n_exact)).astype(np.int32)
    bucket = np.where(d < n_exact, d, np.minimum(large, N_BUCKETS - 1))
    return [int(np.min(d[bucket >= b])) for b in range(1, N_BUCKETS)]


BUCKET_THR = _bucket_thresholds()


def _params(sem=None):
    return pltpu.CompilerParams(dimension_semantics=sem, vmem_limit_bytes=VMEM_LIMIT)


def _gelu(x):
    c = math.sqrt(2.0 / math.pi)
    return 0.5 * x * (1.0 + jnp.tanh(c * (x + 0.044715 * (x * x * x))))


def _gelu_and_grad(x):
    c = math.sqrt(2.0 / math.pi)
    x2 = x * x
    t = jnp.tanh(c * (x + 0.044715 * (x2 * x)))
    g = 0.5 * x * (1.0 + t)
    dg = 0.5 * (1.0 + t) + 0.5 * x * (1.0 - t * t) * (c * (1.0 + 3.0 * 0.044715 * x2))
    return g, dg


def _dot(a, b):
    return jnp.dot(a, b, preferred_element_type=F32)


def _dot_nt(a, b):
    return lax.dot_general(a, b, (((1,), (1,)), ((), ())), preferred_element_type=F32)


def _dot_tn(a, b):
    return lax.dot_general(a, b, (((0,), (0,)), ((), ())), preferred_element_type=F32)


def _rms_bwd(dn, h, g):
    r = lax.rsqrt(jnp.mean(h * h, axis=-1, keepdims=True) + EPS)
    w = dn * g
    dh = r * w - h * ((r * r * r) * jnp.mean(w * h, axis=-1, keepdims=True))
    return dh, r


def _place():
    x, y, c = lax.axis_index("x"), lax.axis_index("y"), lax.axis_index("c")
    chips = [(1 - x, y), (x, 1 - y), (1 - x, 1 - y)]
    return x, y, c, chips


def _remote(src, dst, send_sem, recv_sem, to):
    return pltpu.make_async_remote_copy(src_ref=src, dst_ref=dst, send_sem=send_sem, recv_sem=recv_sem,
                                        device_id=to, device_id_type=MESH)


class _Ride:
    def __init__(self, args, out_shape, n_sem, start, finish, mid=None, mid_frac=0.8, aliases=None):
        self.args, self.out_shape, self.n_sem = list(args), list(out_shape), n_sem
        self.start, self.mid, self.finish, self.mid_frac = start, mid, finish, mid_frac
        self.aliases = dict(aliases or {})


def _call(body, *, name, grid, in_specs, out_specs, out_shape, scratch_shapes=(), sem=None, rides=()):
    single = not isinstance(out_shape, (list, tuple))
    out_specs = [out_specs] if single else list(out_specs)
    out_shape = [out_shape] if single else list(out_shape)
    n_in, n_out, n_scr = len(in_specs), len(out_shape), len(scratch_shapes)
    r_in = [len(r.args) for r in rides]
    r_out = [len(r.out_shape) for r in rides]
    any_spec = pl.BlockSpec(memory_space=pl.ANY)
    aliases, off_i, off_o = {}, n_in, n_out
    for r in rides:
        for i, o in r.aliases.items():
            aliases[off_i + i] = off_o + o
        off_i += len(r.args)
        off_o += len(r.out_shape)
    steps = math.prod(grid)

    def wrapped(*refs):
        p = 0
        ins = refs[p:p + n_in]; p += n_in
        rins = refs[p:p + sum(r_in)]; p += sum(r_in)
        outs = refs[p:p + n_out]; p += n_out
        routs = refs[p:p + sum(r_out)]; p += sum(r_out)
        scr = refs[p:p + n_scr]; p += n_scr
        sems = refs[p:]
        parts, pi, po = [], 0, 0
        for k, r in enumerate(rides):
            parts.append((rins[pi:pi + r_in[k]], routs[po:po + r_out[k]], sems[2 * k], sems[2 * k + 1]))
            pi += r_in[k]
            po += r_out[k]
        lin = 0
        for d in range(len(grid)):
            lin = lin * grid[d] + pl.program_id(d)
        if rides:
            @pl.when(lin == 0)
            def _():
                for r, part in zip(rides, parts):
                    r.start(*part)
        body(*ins, *outs, *scr)
        for r, part in zip(rides, parts):
            if r.mid is not None:
                @pl.when(lin == min(steps - 1, int(r.mid_frac * steps)))
                def _(r=r, part=part):
                    r.mid(*part)
        if rides:
            @pl.when(lin == steps - 1)
            def _():
                for r, part in zip(rides, parts):
                    r.finish(*part)

    scratch = list(scratch_shapes)
    for r in rides:
        scratch += [pltpu.SemaphoreType.DMA((r.n_sem,)), pltpu.SemaphoreType.DMA((r.n_sem,))]
    if rides:
        sem = ("arbitrary",) * len(grid)
    res = pl.pallas_call(
        wrapped, name=name, grid=grid,
        in_specs=list(in_specs) + [any_spec] * sum(r_in),
        out_specs=out_specs + [any_spec] * sum(r_out),
        out_shape=out_shape + [s for r in rides for s in r.out_shape],
        scratch_shapes=scratch, input_output_aliases=aliases,
        compiler_params=_params(sem),
    )

    def run(*args):
        got = res(*args, *[a for r in rides for a in r.args])
        mine = got[0] if single else list(got[:n_out])
        if not rides:
            return mine
        rest, out = list(got[n_out:]), []
        for k in range(len(rides)):
            out.append(rest[:r_out[k]])
            rest = rest[r_out[k]:]
        return mine, out

    return run


def _ride_gather(slot, ici=None, d2d=None, both=None, mid_frac=0.8):
    half = slot.shape[1] // 2

    def rows(part, c):
        k0, k1, n = part
        return pl.ds(c * half + k0 * (half // n), (k1 - k0) * (half // n))

    def ici_copies(outs, ss, rs, part, base):
        x, y, c, chips = _place()
        mine = outs[0].at[2 * x + y, rows(part, c), :]
        return [_remote(mine, mine, ss.at[base + j], rs.at[base + j], (*chip, c)) for j, chip in enumerate(chips)]

    def d2d_copies(outs, ss, rs, part, base):
        x, y, c, chips = _place()
        return [_remote(outs[0].at[2 * chip[0] + chip[1], rows(part, c), :], outs[0].at[2 * chip[0] + chip[1], rows(part, c), :],
                        ss.at[base + j], rs.at[base + j], (x, y, 1 - c)) for j, chip in enumerate(chips)]

    def arrivals(outs, ss, rs, part, base, from_sibling):
        x, y, c, chips = _place()
        for j, chip in enumerate(chips):
            dst = outs[0].at[2 * chip[0] + chip[1], rows(part, 1 - c if from_sibling else c), :]
            _remote(dst, dst, ss.at[base + j], rs.at[base + j], (x, y, c)).wait_recv()

    def start(ins, outs, ss, rs):
        for part, base in ((ici, 0), (both, 6)):
            if part is not None:
                for cp in ici_copies(outs, ss, rs, part, base):
                    cp.start()
        if d2d is not None:
            for cp in d2d_copies(outs, ss, rs, d2d, 3):
                cp.start()

    def mid(ins, outs, ss, rs):
        arrivals(outs, ss, rs, both, 6, False)
        for cp in d2d_copies(outs, ss, rs, both, 9):
            cp.start()

    def finish(ins, outs, ss, rs):
        if ici is not None:
            arrivals(outs, ss, rs, ici, 0, False)
        if d2d is not None:
            arrivals(outs, ss, rs, d2d, 3, True)
        if both is not None:
            arrivals(outs, ss, rs, both, 9, True)
        for part, base, fn in ((ici, 0, ici_copies), (d2d, 3, d2d_copies), (both, 6, ici_copies), (both, 9, d2d_copies)):
            if part is not None:
                for cp in fn(outs, ss, rs, part, base):
                    cp.wait_send()

    return _Ride([slot], [jax.ShapeDtypeStruct(slot.shape, slot.dtype)], 12, start, finish,
                 mid=mid if both is not None else None, mid_frac=mid_frac, aliases={0: 0})


def _ride_sibling_halves(g):
    S, R, C = g.shape
    hr = R // 2

    def copy(ins, outs, ss, rs):
        x, y, c, _ = _place()
        return _remote(ins[0].at[:, pl.ds((1 - c) * hr, hr), :], outs[0], ss.at[0], rs.at[0], (x, y, 1 - c))

    return _Ride([g], [jax.ShapeDtypeStruct((S, hr, C), g.dtype)], 1,
                 lambda *a: copy(*a).start(), lambda *a: copy(*a).wait())


def _ride_scatter(q, land=None, part=(0, 1)):
    k0, k1, n = part if len(part) == 3 else (part[0], part[0] + 1, part[1])
    rows_n = q.shape[1] // n
    rows = pl.ds(k0 * rows_n, (k1 - k0) * rows_n)

    def copies(ins, outs, ss, rs):
        x, y, c, chips = _place()
        return [_remote(ins[0].at[2 * chip[0] + chip[1], rows, :], outs[0].at[j, rows, :], ss.at[j], rs.at[j], (*chip, c))
                for j, chip in enumerate(chips)]

    def start(*a):
        for cp in copies(*a):
            cp.start()

    def finish(*a):
        for cp in copies(*a):
            cp.wait()

    shape = jax.ShapeDtypeStruct((3,) + q.shape[1:], q.dtype)
    if land is None:
        return _Ride([q], [shape], 3, start, finish)
    return _Ride([q, land], [shape], 3, start, finish, aliases={1: 0})


def _ride_to_sibling(a, halves=False, first=False):
    def copy(ins, outs, ss, rs):
        x, y, c, _ = _place()
        src = ins[0].at[:, 1 - c] if halves else (ins[0].at[0] if first else ins[0])
        return _remote(src, outs[0], ss.at[0], rs.at[0], (x, y, 1 - c))

    shape = (a.shape[0],) + a.shape[2:] if halves else (a.shape[1:] if first else a.shape)
    return _Ride([a], [jax.ShapeDtypeStruct(shape, a.dtype)], 1, lambda *a_: copy(*a_).start(), lambda *a_: copy(*a_).wait())


def _ride_swap(h):
    def copy(ins, outs, ss, rs):
        x, y, c, _ = _place()
        return _remote(ins[0], outs[0], ss.at[0], rs.at[0], (x, y, 1 - c))

    return _Ride([h], [jax.ShapeDtypeStruct(h.shape, h.dtype)], 1,
                 lambda *a: copy(*a).start(), lambda *a: copy(*a).wait())


def _mesh_place(p):
    return (p // 4, (p // 2) % 2, p % 2)


def _ride_small_to_all(packed):
    def copies(ins, outs, ss, rs):
        x, y, c, _ = _place()
        me = 4 * x + 2 * y + c
        return [_remote(ins[0], outs[0].at[me], ss.at[k - 1], rs.at[k - 1], _mesh_place((me + k) % N_DEV))
                for k in range(1, N_DEV)]

    def own(ins, outs, ss, rs):
        x, y, c, _ = _place()
        return pltpu.make_async_copy(ins[0], outs[0].at[4 * x + 2 * y + c], ss.at[N_DEV - 1])

    def start(*a):
        own(*a).start()
        for cp in copies(*a):
            cp.start()

    def finish(ins, outs, ss, rs):
        x, y, c, _ = _place()
        me = 4 * x + 2 * y + c
        for k in range(1, N_DEV):
            _remote(ins[0], outs[0].at[(me + N_DEV - k) % N_DEV], ss.at[k - 1], rs.at[k - 1], (x, y, c)).wait_recv()
        for cp in copies(ins, outs, ss, rs):
            cp.wait_send()
        own(ins, outs, ss, rs).wait()

    return _Ride([packed], [jax.ShapeDtypeStruct((N_DEV,) + packed.shape, packed.dtype)], N_DEV, start, finish)


def _carrier(rides, *, name):
    _, outs = _call(lambda: None, name=name, grid=(1,), in_specs=[], out_specs=[], out_shape=[], rides=rides)()
    return outs


def _sq_relu_bf16(z):
    z = jnp.maximum(z, 0.0)
    return (z * z).astype(BF16)


def _norm_bf16(a_ref, g_ref):
    xf = a_ref[...]
    r = lax.rsqrt(jnp.mean(xf * xf, axis=-1, keepdims=True) + EPS)
    return ((xf * r) * g_ref[...]).astype(BF16)


def _norm_matmul_wide(a, g, b, *, tm, tn, name, rides=()):
    T, K = a.shape
    N = b.shape[0]

    def body(a_ref, g_ref, b_ref, n_ref, o_ref):
        n = _norm_bf16(a_ref, g_ref)
        n_ref[...] = n
        o_ref[...] = _dot_nt(n, b_ref[...])

    return _call(
        body, name=name, grid=(N // tn, T // tm),
        in_specs=[pl.BlockSpec((tm, K), lambda j, i: (i, 0)), pl.BlockSpec((1, K), lambda j, i: (0, 0)),
                  pl.BlockSpec((tn, K), lambda j, i: (j, 0))],
        out_specs=[pl.BlockSpec((None, tm, K), lambda j, i: (j, i, 0)), pl.BlockSpec((tm, tn), lambda j, i: (i, j))],
        out_shape=[jax.ShapeDtypeStruct((N // tn, T, K), BF16), jax.ShapeDtypeStruct((T, N), F32)],
        sem=("arbitrary", "arbitrary"), rides=rides,
    )(a, g, b)


def _norm_matmul_sq(a, g, b, *, tm, tn, name, rides=()):
    T, K = a.shape
    per = b.shape[2] // tn
    N = b.shape[0] * b.shape[2]

    def body(a_ref, g_ref, b_ref, nt_ref, o_ref, z_ref, zt_ref, n_scr):
        @pl.when(pl.program_id(1) == 0)
        def _():
            n = _norm_bf16(a_ref, g_ref)
            n_scr[...] = n
            nt_ref[...] = n.T
        p = _dot(n_scr[...], b_ref[...])
        o_ref[...] = p
        z = _sq_relu_bf16(p)
        z_ref[...] = z
        zt_ref[...] = z.T

    return _call(
        body, name=name, grid=(T // tm, N // tn),
        in_specs=[pl.BlockSpec((tm, K), lambda i, j: (i, 0)), pl.BlockSpec((1, K), lambda i, j: (0, 0)),
                  pl.BlockSpec((None, K, tn), lambda i, j: (j // per, 0, j % per))],
        out_specs=[pl.BlockSpec((K, tm), lambda i, j: (0, i)), pl.BlockSpec((tm, tn), lambda i, j: (i, j)),
                   pl.BlockSpec((tm, tn), lambda i, j: (i, j)), pl.BlockSpec((tn, tm), lambda i, j: (j, i))],
        out_shape=[jax.ShapeDtypeStruct((K, T), BF16), jax.ShapeDtypeStruct((T, N), F32),
                   jax.ShapeDtypeStruct((T, N), BF16), jax.ShapeDtypeStruct((N, T), BF16)],
        scratch_shapes=[pltpu.VMEM((tm, K), BF16)],
        sem=("parallel", "arbitrary"), rides=rides,
    )(a, g, b)


def _grad_pair(at, at_sib, b, b_sib, *, cols_sharded, tmo, tk, name, rides=()):
    S, _, hr, T = at.shape
    C = b.shape[-1] // N_CHIPS if cols_sharded else b.shape[-1]
    nk = T // tk
    a_sel = (lambda s: 0) if cols_sharded else (lambda s: s)
    b_sel = (lambda s: s) if cols_sharded else (lambda s: 0)
    if b.ndim == 3:
        b_spec = pl.BlockSpec((None, tk, C), lambda s, i, k: (0, k, b_sel(s)))
    else:
        b_spec = pl.BlockSpec((tk, C), lambda s, i, k: (k, b_sel(s)))

    def body(a_ref, as_ref, b_ref, bs_ref, o_ref, ob_ref):
        k = pl.program_id(2)
        p = _dot(a_ref[...], b_ref[...]) + _dot(as_ref[...], bs_ref[...])

        @pl.when(k == 0)
        def _():
            o_ref[...] = p

        @pl.when(k > 0)
        def _():
            o_ref[...] += p

        @pl.when(k == nk - 1)
        def _():
            ob_ref[...] = o_ref[...].astype(BF16)

    out = pl.BlockSpec((None, tmo, C), lambda s, i, k: (s, i, 0))
    return _call(
        body, name=name, grid=(N_CHIPS, hr // tmo, nk),
        in_specs=[pl.BlockSpec((None, None, tmo, tk), lambda s, i, k: (a_sel(s), lax.axis_index("c"), i, k)),
                  pl.BlockSpec((None, tmo, tk), lambda s, i, k: (a_sel(s), i, k)),
                  b_spec, pl.BlockSpec((tk, C), lambda s, i, k: (k, b_sel(s)))],
        out_specs=[out, out],
        out_shape=[jax.ShapeDtypeStruct((N_CHIPS, hr, C), F32), jax.ShapeDtypeStruct((N_CHIPS, hr, C), BF16)],
        sem=("parallel", "parallel", "arbitrary"), rides=rides,
    )(at, at_sib, b, b_sib)


def _matmul_nn(at, b, *, tmo, tn, tk, name, shards=1, rides=()):
    M, T = at.shape[-2:]
    N = b.shape[-1]
    if at.ndim == 3:
        a_spec = pl.BlockSpec((None, tmo, tk), lambda i, j, k: (0, i, k))
    else:
        a_spec = pl.BlockSpec((tmo, tk), lambda i, j, k: (i, k))
    if b.ndim == 3:
        b_spec = pl.BlockSpec((None, tk, tn), lambda i, j, k: (0, k, j))
    else:
        b_spec = pl.BlockSpec((tk, tn), lambda i, j, k: (k, j))
    if shards > 1:
        per = (N // shards) // tn
        out_spec = pl.BlockSpec((None, tmo, tn), lambda i, j, k: (j // per, i, j % per))
        out_shape = jax.ShapeDtypeStruct((shards, M, N // shards), F32)
    else:
        out_spec = pl.BlockSpec((tmo, tn), lambda i, j, k: (i, j))
        out_shape = jax.ShapeDtypeStruct((M, N), F32)

    def body(a_ref, b_ref, o_ref):
        k = pl.program_id(2)
        p = _dot(a_ref[...], b_ref[...])

        @pl.when(k == 0)
        def _():
            o_ref[...] = p

        @pl.when(k > 0)
        def _():
            o_ref[...] += p

    return _call(
        body, name=name, grid=(M // tmo, N // tn, T // tk),
        in_specs=[a_spec, b_spec],
        out_specs=out_spec, out_shape=out_shape,
        sem=("parallel", "parallel", "arbitrary"), rides=rides,
    )(at, b)


def _to_bf16(v):
    return v.astype(BF16)


def _matmul_res(a, b, res, *, tm, tn, tk, prologue, name, rides=()):
    T, K = a.shape
    N = b.shape[1]

    def body(a_ref, b_ref, res_ref, o_ref):
        k = pl.program_id(2)
        p = _dot(prologue(a_ref[...]), b_ref[...])

        @pl.when(k == 0)
        def _():
            o_ref[...] = res_ref[...] + p

        @pl.when(k > 0)
        def _():
            o_ref[...] += p

    return _call(
        body, name=name, grid=(T // tm, N // tn, K // tk),
        in_specs=[pl.BlockSpec((tm, tk), lambda i, j, k: (i, k)), pl.BlockSpec((tk, tn), lambda i, j, k: (k, j)),
                  pl.BlockSpec((tm, tn), lambda i, j, k: (i, j))],
        out_specs=pl.BlockSpec((tm, tn), lambda i, j, k: (i, j)),
        out_shape=jax.ShapeDtypeStruct((T, N), F32),
        sem=("parallel", "parallel", "arbitrary"), rides=rides,
    )(a, b, res)


def _matmul_nt(a, b, *, tm, tn, tk, name, extra=None, epilogue=None, out_dtype=F32, rides=()):
    T, K = a.shape
    if b.ndim == 3:
        per = b.shape[2] // tk
        N = b.shape[1]
        b_spec = pl.BlockSpec((None, tn, tk), lambda i, j, k: (k // per, j, k % per))
    else:
        N = b.shape[0]
        b_spec = pl.BlockSpec((tn, tk), lambda i, j, k: (j, k))
    nk = K // tk
    assert out_dtype == F32 or nk == 1
    in_specs = [pl.BlockSpec((tm, tk), lambda i, j, k: (i, k)), b_spec]
    args = [a, b]
    if extra is not None:
        in_specs.append(pl.BlockSpec((tm, tn), lambda i, j, k: (i, j)))
        args.append(extra)

    def body(*refs):
        a_ref, b_ref = refs[0], refs[1]
        o_ref = refs[-1]
        p = _dot_nt(a_ref[...].astype(BF16), b_ref[...])
        if nk == 1:
            if epilogue is not None:
                p = epilogue(p, refs[2][...])
            o_ref[...] = p.astype(out_dtype)
        else:
            k = pl.program_id(2)

            @pl.when(k == 0)
            def _():
                o_ref[...] = p

            @pl.when(k > 0)
            def _():
                o_ref[...] += p

    return _call(
        body, name=name, grid=(T // tm, N // tn, nk),
        in_specs=in_specs,
        out_specs=pl.BlockSpec((tm, tn), lambda i, j, k: (i, j)),
        out_shape=jax.ShapeDtypeStruct((T, N), out_dtype),
        sem=("parallel", "parallel", "arbitrary"), rides=rides,
    )(*args)


def _matmul_tn(a, b, *, tmo, tn, tk, name, a_prologue=_to_bf16, shards=1, rides=()):
    T, M = a.shape
    N = b.shape[1]
    if shards > 1:
        per = (N // shards) // tn
        out_spec = pl.BlockSpec((None, tmo, tn), lambda i, j, k: (j // per, i, j % per))
        out_shape = jax.ShapeDtypeStruct((shards, M, N // shards), F32)
    else:
        out_spec = pl.BlockSpec((tmo, tn), lambda i, j, k: (i, j))
        out_shape = jax.ShapeDtypeStruct((M, N), F32)

    def body(a_ref, b_ref, o_ref):
        k = pl.program_id(2)
        p = _dot_tn(a_prologue(a_ref[...]), b_ref[...].astype(BF16))

        @pl.when(k == 0)
        def _():
            o_ref[...] = p

        @pl.when(k > 0)
        def _():
            o_ref[...] += p

    return _call(
        body, name=name, grid=(M // tmo, N // tn, T // tk),
        in_specs=[pl.BlockSpec((tk, tmo), lambda i, j, k: (k, i)), pl.BlockSpec((tk, tn), lambda i, j, k: (k, j))],
        out_specs=out_spec, out_shape=out_shape,
        sem=("parallel", "parallel", "arbitrary"), rides=rides,
    )(a, b)


def _loss_bwd(h2, tgt, g, *, tm):
    T, D = h2.shape

    def body(h_ref, t_ref, g_ref, dh_ref, dhb_ref, dg_ref, loss_ref):
        @pl.when(pl.program_id(0) == 0)
        def _():
            dg_ref[...] = jnp.zeros_like(dg_ref)
            loss_ref[...] = jnp.zeros_like(loss_ref)
        h = h_ref[...]
        gg = g_ref[...]
        r = lax.rsqrt(jnp.mean(h * h, axis=-1, keepdims=True) + EPS)
        hn = h * r
        err = hn * gg - t_ref[...]
        loss_ref[...] += 0.5 * jnp.sum(jnp.mean(err * err, axis=-1, keepdims=True), axis=0, keepdims=True)
        dy = err * (1.0 / D)
        dg_ref[...] += jnp.sum(dy * hn, axis=0, keepdims=True)
        w = dy * gg
        dh = r * w - h * ((r * r * r) * jnp.mean(w * h, axis=-1, keepdims=True))
        dh_ref[...] = dh
        dhb_ref[...] = dh.astype(BF16)

    tile = pl.BlockSpec((tm, D), lambda i: (i, 0))
    return pl.pallas_call(
        body, name="loss_bwd", grid=(T // tm,),
        in_specs=[tile, tile, pl.BlockSpec((1, D), lambda i: (0, 0))],
        out_specs=[tile, tile, pl.BlockSpec((1, D), lambda i: (0, 0)), pl.BlockSpec((1, 1), lambda i: (0, 0))],
        out_shape=[jax.ShapeDtypeStruct((T, D), F32), jax.ShapeDtypeStruct((T, D), BF16),
                   jax.ShapeDtypeStruct((1, D), F32), jax.ShapeDtypeStruct((1, 1), F32)],
        compiler_params=_params(("arbitrary",)),
    )(h2, tgt, g)


def _rms_bwd_res(dn, h, g, dres, *, tm, name, rides=()):
    T, D = h.shape

    def body(dn_ref, h_ref, g_ref, dres_ref, dh_ref, dhb_ref, dg_ref):
        @pl.when(pl.program_id(0) == 0)
        def _():
            dg_ref[...] = jnp.zeros_like(dg_ref)
        h_ = h_ref[...]
        dn_ = dn_ref[...]
        dh, r = _rms_bwd(dn_, h_, g_ref[...])
        dg_ref[...] += jnp.sum(dn_ * (h_ * r), axis=0, keepdims=True)
        dh = dres_ref[...] + dh
        dh_ref[...] = dh
        dhb_ref[...] = dh.astype(BF16)

    tile = pl.BlockSpec((tm, D), lambda i: (i, 0))
    return _call(
        body, name=name, grid=(T // tm,),
        in_specs=[tile, tile, pl.BlockSpec((1, D), lambda i: (0, 0)), tile],
        out_specs=[tile, tile, pl.BlockSpec((1, D), lambda i: (0, 0))],
        out_shape=[jax.ShapeDtypeStruct((T, D), F32), jax.ShapeDtypeStruct((T, D), BF16),
                   jax.ShapeDtypeStruct((1, D), F32)],
        sem=("arbitrary",), rides=rides,
    )(dn, h, g, dres)


def _rel_distance():
    i = lax.broadcasted_iota(jnp.int32, (CHUNK, 2 * CHUNK), 0)
    j = lax.broadcasted_iota(jnp.int32, (CHUNK, 2 * CHUNK), 1)
    return i + CHUNK - j


def _bias_build(table):
    def body(tab_ref, o_ref):
        rel = _rel_distance()
        ge = [rel >= t for t in BUCKET_THR]
        for h in range(B_HEADS):
            cur = jnp.full((CHUNK, 2 * CHUNK), tab_ref[0, h], F32)
            for b in range(1, N_BUCKETS):
                cur = jnp.where(ge[b - 1], tab_ref[b, h], cur)
            o_ref[h] = cur

    return pl.pallas_call(
        body, name="bias_build",
        in_specs=[pl.BlockSpec(memory_space=pltpu.SMEM)],
        out_specs=pl.BlockSpec(memory_space=pltpu.VMEM),
        out_shape=jax.ShapeDtypeStruct((B_HEADS, CHUNK, 2 * CHUNK), F32),
    )(table)


def _bias_grad(dbias):
    def body(db_ref, o_ref, acc_ref):
        rel = _rel_distance()
        lo = [0] + BUCKET_THR
        hi = BUCKET_THR + [CHUNK]
        for b in range(N_BUCKETS):
            m = (rel >= lo[b]) & (rel < hi[b])
            for h in range(B_HEADS):
                row = b * B_HEADS + h
                acc_ref[row:row + 1, :] = jnp.sum(jnp.where(m, db_ref[h], 0.0), axis=0, keepdims=True)
        o_ref[...] = jnp.sum(acc_ref[...], axis=1, keepdims=True)

    return pl.pallas_call(
        body, name="bias_grad",
        in_specs=[pl.BlockSpec(memory_space=pltpu.VMEM)],
        out_specs=pl.BlockSpec(memory_space=pltpu.VMEM),
        out_shape=jax.ShapeDtypeStruct((N_BUCKETS * B_HEADS, 1), F32),
        scratch_shapes=[pltpu.VMEM((N_BUCKETS * B_HEADS, 2 * CHUNK), F32)],
    )(dbias)


def _causal_mask():
    t = lax.broadcasted_iota(jnp.int32, (CHUNK, CHUNK), 0)
    s = lax.broadcasted_iota(jnp.int32, (CHUNK, CHUNK), 1)
    return s <= t


def _band_mask(n):
    rel = _rel_distance()
    j = lax.broadcasted_iota(jnp.int32, (CHUNK, 2 * CHUNK), 1)
    return (rel >= 0) & (rel < CHUNK) & ((n > 0) | (j >= CHUNK))


def _gate_forward(u, v, lg, lb, wc, bs):
    ug = _gelu(u)
    vg = _gelu(v)
    mu = jnp.mean(vg, axis=-1, keepdims=True)
    xc = vg - mu
    rstd = lax.rsqrt(jnp.mean(xc * xc, axis=-1, keepdims=True) + EPS)
    xhat = xc * rstd
    vl = (xhat * lg + lb).astype(BF16)
    mixed = _dot(wc, vl) + bs
    return ug, xhat, rstd, vl, mixed


def _softmax_scores(qk, bias, mask, sink):
    s = qk * SCALE + bias
    s = jnp.where(mask, s, NEG)
    m = jnp.maximum(jnp.max(s, axis=-1, keepdims=True), sink)
    p = jnp.exp(s - m)
    e_sink = jnp.exp(sink - m)
    inv = 1.0 / (jnp.sum(p, axis=-1, keepdims=True) + e_sink)
    return p * inv, e_sink * inv


PAIRS = Q_PER_KV // 2


def _head(g, pr, e):
    return g * Q_PER_KV + 2 * pr + e


def _stack_pairs(ref, g, col0=0):
    w = 2 * HEAD_DIM
    return jnp.concatenate([ref[:, col0 + (g * PAIRS + pr) * w:col0 + (g * PAIRS + pr + 1) * w] for pr in range(PAIRS)],
                           axis=0)


def _low_lanes():
    return lax.broadcasted_iota(jnp.int32, (2 * CHUNK, 2 * HEAD_DIM), 1) < HEAD_DIM


def _band_operands(kv_prev, kv_cur):
    band = jnp.concatenate([kv_prev, kv_cur], axis=0)
    low = _low_lanes()
    ops = []
    for cat in (band[:, :KV_WIDTH], band[:, KV_WIDTH:]):
        rol = pltpu.roll(cat, HEAD_DIM, 1)
        ops.append([[jnp.where(low if e == 0 else ~low, cat if g == e else rol, 0.0).astype(BF16) for e in range(2)]
                    for g in range(2)])
    return ops


def _mixer_fwd(proj, lg, lb, wsp, bs_col, sinks, bias, ga, gb, rides=()):
    T = proj.shape[0]
    nb = T // CHUNK

    def body(u_ref, v_ref, q_ref, kvc_ref, kvp_ref, lg_ref, lb_ref, w_ref, bs_ref, sink_ref, bias_ref,
             ga_ref, gb_ref, mixed_ref, mixed_t_ref, ab_ref):
        n = pl.program_id(0)
        causal = _causal_mask()
        ssq = jnp.zeros((CHUNK, 1), F32)
        for g in range(A_GROUPS):
            cols = slice(g * CHUNK, (g + 1) * CHUNK)
            wc = jnp.where(causal, w_ref[g], 0.0).astype(BF16)
            ug, _, _, _, mixed = _gate_forward(u_ref[:, cols], v_ref[:, cols], lg_ref[g:g + 1, :], lb_ref[g:g + 1, :],
                                               wc, bs_ref[g])
            a = ug * mixed
            ab_ref[:, cols] = a
            ssq = ssq + jnp.sum(a * a, axis=-1, keepdims=True)
        ra = lax.rsqrt(ssq * (1.0 / A_WIDTH) + EPS)
        mixed_ref[:, :A_WIDTH] = ((ab_ref[:, :A_WIDTH] * ra) * ga_ref[...]).astype(BF16)

        mask = _band_mask(n)
        kops, vops = _band_operands(kvp_ref[...], kvc_ref[...])
        ssq = jnp.zeros((CHUNK, 1), F32)
        for g in range(B_HEADS // Q_PER_KV):
            qst = _stack_pairs(q_ref, g).astype(BF16)
            o_st = jnp.zeros((PAIRS * CHUNK, 2 * HEAD_DIM), F32)
            for e in range(2):
                s_all = _dot_nt(qst, kops[g][e])
                ps = []
                for pr in range(PAIRS):
                    h = _head(g, pr, e)
                    p, _ = _softmax_scores(s_all[pr * CHUNK:(pr + 1) * CHUNK], bias_ref[h], mask, sink_ref[0, h])
                    ps.append(p.astype(BF16))
                o_st = o_st + _dot(jnp.concatenate(ps, axis=0), vops[g][e])
            for pr in range(PAIRS):
                o = o_st[pr * CHUNK:(pr + 1) * CHUNK]
                c0 = A_WIDTH + (g * PAIRS + pr) * 2 * HEAD_DIM
                ab_ref[:, c0:c0 + 2 * HEAD_DIM] = o
                ssq = ssq + jnp.sum(o * o, axis=-1, keepdims=True)
        rb = lax.rsqrt(ssq * (1.0 / B_WIDTH) + EPS)
        mixed_ref[:, A_WIDTH:] = ((ab_ref[:, A_WIDTH:] * rb) * gb_ref[...]).astype(BF16)
        mixed_t_ref[...] = mixed_ref[...].T

    full = lambda *shape: pl.BlockSpec(shape, lambda n: (0,) * len(shape))
    return _call(
        body, name="mixer_fwd", grid=(nb,),
        in_specs=[pl.BlockSpec((CHUNK, A_WIDTH), lambda n: (n, 0)),
                  pl.BlockSpec((CHUNK, A_WIDTH), lambda n: (n, 1)),
                  pl.BlockSpec((CHUNK, B_WIDTH), lambda n: (n, 2)),
                  pl.BlockSpec((CHUNK, 2 * KV_WIDTH), lambda n: (n, 12)),
                  pl.BlockSpec((CHUNK, 2 * KV_WIDTH), lambda n: (jnp.maximum(n - 1, 0), 12)),
                  full(A_GROUPS, CHUNK), full(A_GROUPS, CHUNK), full(A_GROUPS, CHUNK, CHUNK), full(A_GROUPS, CHUNK, 1),
                  pl.BlockSpec(memory_space=pltpu.SMEM), full(B_HEADS, CHUNK, 2 * CHUNK),
                  full(1, A_WIDTH), full(1, B_WIDTH)],
        out_specs=[pl.BlockSpec((CHUNK, D_MODEL), lambda n: (n, 0)), pl.BlockSpec((D_MODEL, CHUNK), lambda n: (0, n)),
                   pl.BlockSpec((CHUNK, D_MODEL), lambda n: (n, 0))],
        out_shape=[jax.ShapeDtypeStruct((T, D_MODEL), BF16), jax.ShapeDtypeStruct((D_MODEL, T), BF16),
                   jax.ShapeDtypeStruct((T, D_MODEL), F32)],
        sem=("parallel",), rides=rides,
    )(proj, proj, proj, proj, proj, lg, lb, wsp, bs_col, sinks, bias, ga, gb)


def _gmlp_bwd(proj, ab, dmixed, ga, lg, lb, wsp, bs_col, rides=()):
    T = proj.shape[0]
    nb = T // CHUNK

    def body(u_ref, v_ref, a_ref, dna_ref, ga_ref, lg_ref, lb_ref, w_ref, bs_ref,
             dp_ref, dpt_ref, dga_ref, dw_ref, dbs_ref, dlg_ref, dlb_ref):
        @pl.when(pl.program_id(0) == 0)
        def _():
            for r in (dga_ref, dw_ref, dbs_ref, dlg_ref, dlb_ref):
                r[...] = jnp.zeros_like(r)
        causal = _causal_mask()
        a_all = a_ref[...]
        dna = dna_ref[...]
        da_all, ra = _rms_bwd(dna, a_all, ga_ref[...])
        dga_ref[...] += jnp.sum(dna * (a_all * ra), axis=0, keepdims=True)
        for g in range(A_GROUPS):
            cols = slice(g * CHUNK, (g + 1) * CHUNK)
            wc = jnp.where(causal, w_ref[g], 0.0).astype(BF16)
            lgg = lg_ref[g:g + 1, :]
            u = u_ref[:, cols]
            v = v_ref[:, cols]
            ug, xhat, rstd, vl, mixed = _gate_forward(u, v, lgg, lb_ref[g:g + 1, :], wc, bs_ref[g])
            da = da_all[:, cols]
            dug = da * mixed
            dmg = da * ug
            dmg_b = dmg.astype(BF16)
            dbs_ref[g] += jnp.sum(dmg, axis=-1, keepdims=True)
            dw_ref[g] += jnp.where(causal, _dot_nt(dmg_b, vl), 0.0)
            dvl = _dot_tn(wc, dmg_b)
            dlg_ref[g:g + 1, :] += jnp.sum(dvl * xhat, axis=0, keepdims=True)
            dlb_ref[g:g + 1, :] += jnp.sum(dvl, axis=0, keepdims=True)
            dxh = dvl * lgg
            dvg = rstd * (dxh - jnp.mean(dxh, axis=-1, keepdims=True)
                          - xhat * jnp.mean(dxh * xhat, axis=-1, keepdims=True))
            _, gu = _gelu_and_grad(u)
            _, gv = _gelu_and_grad(v)
            dp_ref[:, cols] = (dug * gu).astype(BF16)
            dp_ref[:, A_WIDTH + g * CHUNK:A_WIDTH + (g + 1) * CHUNK] = (dvg * gv).astype(BF16)
        dpt_ref[...] = dp_ref[...].T

    full = lambda *shape: pl.BlockSpec(shape, lambda n: (0,) * len(shape))
    return _call(
        body, name="gmlp_bwd", grid=(nb,),
        in_specs=[pl.BlockSpec((CHUNK, A_WIDTH), lambda n: (n, 0)),
                  pl.BlockSpec((CHUNK, A_WIDTH), lambda n: (n, 1)),
                  pl.BlockSpec((CHUNK, A_WIDTH), lambda n: (n, 0)),
                  pl.BlockSpec((CHUNK, A_WIDTH), lambda n: (n, 0)),
                  full(1, A_WIDTH), full(A_GROUPS, CHUNK), full(A_GROUPS, CHUNK), full(A_GROUPS, CHUNK, CHUNK),
                  full(A_GROUPS, CHUNK, 1)],
        out_specs=[pl.BlockSpec((CHUNK, 2 * A_WIDTH), lambda n: (n, 0)), pl.BlockSpec((2 * A_WIDTH, CHUNK), lambda n: (0, n)),
                   full(1, A_WIDTH), full(A_GROUPS, CHUNK, CHUNK), full(A_GROUPS, CHUNK, 1),
                   full(A_GROUPS, CHUNK), full(A_GROUPS, CHUNK)],
        out_shape=[jax.ShapeDtypeStruct((T, 2 * A_WIDTH), BF16), jax.ShapeDtypeStruct((2 * A_WIDTH, T), BF16),
                   jax.ShapeDtypeStruct((1, A_WIDTH), F32), jax.ShapeDtypeStruct((A_GROUPS, CHUNK, CHUNK), F32),
                   jax.ShapeDtypeStruct((A_GROUPS, CHUNK, 1), F32), jax.ShapeDtypeStruct((A_GROUPS, CHUNK), F32),
                   jax.ShapeDtypeStruct((A_GROUPS, CHUNK), F32)],
        sem=("arbitrary",), rides=rides,
    )(proj, proj, ab, dmixed, ga, lg, lb, wsp, bs_col)


def _attn_bwd(proj, ab, dmixed, gb, sinks, bias, rides=()):
    T = proj.shape[0]
    nb = T // CHUNK
    qn = lambda n: jnp.minimum(n, nb - 1)

    def body(q_ref, kvc_ref, kvp_ref, o_ref, dnb_ref, gb_ref, sink_ref, bias_ref,
             dq_ref, dkv_ref, dqt_ref, dkvt_ref, dgb_ref, dsink_ref, dbias_ref, carry_ref, sacc_ref):
        n = pl.program_id(0)

        @pl.when(n == 0)
        def _():
            carry_ref[...] = jnp.zeros_like(carry_ref)
            sacc_ref[...] = jnp.zeros_like(sacc_ref)
            dgb_ref[...] = jnp.zeros_like(dgb_ref)
            dbias_ref[...] = jnp.zeros_like(dbias_ref)

        @pl.when(n < nb)
        def _():
            mask = _band_mask(n)
            o_all = o_ref[...]
            dnb = dnb_ref[...]
            do_all, rb = _rms_bwd(dnb, o_all, gb_ref[...])
            dgb_ref[...] += jnp.sum(dnb * (o_all * rb), axis=0, keepdims=True)
            kops, vops = _band_operands(kvp_ref[...], kvc_ref[...])
            low = _low_lanes()
            halves = []
            for g in range(B_HEADS // Q_PER_KV):
                qst = _stack_pairs(q_ref, g).astype(BF16)
                dost = _stack_pairs(do_all, g).astype(BF16)
                dq_st = jnp.zeros((PAIRS * CHUNK, 2 * HEAD_DIM), F32)
                dk_e, dv_e = [], []
                for e in range(2):
                    s_all = _dot_nt(qst, kops[g][e])
                    dp_all = _dot_nt(dost, vops[g][e])
                    ps, dsrs = [], []
                    for pr in range(PAIRS):
                        h = _head(g, pr, e)
                        rows = slice(pr * CHUNK, (pr + 1) * CHUNK)
                        p, p_sink = _softmax_scores(s_all[rows], bias_ref[h], mask, sink_ref[0, h])
                        dp = dp_all[rows]
                        delta = jnp.sum(p * dp, axis=-1, keepdims=True)
                        ds = p * (dp - delta)
                        sacc_ref[:, h:h + 1] += -(p_sink * delta)
                        dbias_ref[h] += ds
                        ps.append(p.astype(BF16))
                        dsrs.append((ds * SCALE).astype(BF16))
                    dsr_all = jnp.concatenate(dsrs, axis=0)
                    dq_st = dq_st + _dot(dsr_all, kops[g][e])
                    dk_e.append(_dot_tn(dsr_all, qst))
                    dv_e.append(_dot_tn(jnp.concatenate(ps, axis=0), dost))
                for pr in range(PAIRS):
                    c0 = (g * PAIRS + pr) * 2 * HEAD_DIM
                    dq_ref[:, c0:c0 + 2 * HEAD_DIM] = dq_st[pr * CHUNK:(pr + 1) * CHUNK].astype(BF16)
                halves.append((dk_e, dv_e))
            tiles = []
            for t in range(2):
                g0, g1 = halves[0][t], halves[1][t]
                tiles.append(jnp.where(low, g0[0] + pltpu.roll(g0[1], HEAD_DIM, 1), pltpu.roll(g1[0], HEAD_DIM, 1) + g1[1]))
            dband = jnp.concatenate(tiles, axis=1)
            dkv = (carry_ref[...] + dband[:CHUNK]).astype(BF16)
            dkv_ref[...] = dkv
            dkvt_ref[...] = dkv.T
            dqt_ref[...] = dq_ref[...].T
            carry_ref[...] = dband[CHUNK:]

        @pl.when(n == nb)
        def _():
            dkv = carry_ref[...].astype(BF16)
            dkv_ref[...] = dkv
            dkvt_ref[...] = dkv.T
            dsink_ref[...] = jnp.sum(sacc_ref[...], axis=0, keepdims=True)

    full = lambda *shape: pl.BlockSpec(shape, lambda n: (0,) * len(shape))
    return _call(
        body, name="attn_bwd", grid=(nb + 1,),
        in_specs=[pl.BlockSpec((CHUNK, B_WIDTH), lambda n: (qn(n), 2)),
                  pl.BlockSpec((CHUNK, 2 * KV_WIDTH), lambda n: (qn(n), 12)),
                  pl.BlockSpec((CHUNK, 2 * KV_WIDTH), lambda n: (jnp.maximum(qn(n) - 1, 0), 12)),
                  pl.BlockSpec((CHUNK, B_WIDTH), lambda n: (qn(n), 1)),
                  pl.BlockSpec((CHUNK, B_WIDTH), lambda n: (qn(n), 1)),
                  full(1, B_WIDTH), pl.BlockSpec(memory_space=pltpu.SMEM), full(B_HEADS, CHUNK, 2 * CHUNK)],
        out_specs=[pl.BlockSpec((CHUNK, B_WIDTH), lambda n: (qn(n), 0)),
                   pl.BlockSpec((CHUNK, 2 * KV_WIDTH), lambda n: (jnp.maximum(n - 1, 0), 0)),
                   pl.BlockSpec((B_WIDTH, CHUNK), lambda n: (0, qn(n))),
                   pl.BlockSpec((2 * KV_WIDTH, CHUNK), lambda n: (0, jnp.maximum(n - 1, 0))),
                   full(1, B_WIDTH), full(1, B_HEADS), full(B_HEADS, CHUNK, 2 * CHUNK)],
        out_shape=[jax.ShapeDtypeStruct((T, B_WIDTH), BF16), jax.ShapeDtypeStruct((T, 2 * KV_WIDTH), BF16),
                   jax.ShapeDtypeStruct((B_WIDTH, T), BF16), jax.ShapeDtypeStruct((2 * KV_WIDTH, T), BF16),
                   jax.ShapeDtypeStruct((1, B_WIDTH), F32), jax.ShapeDtypeStruct((1, B_HEADS), F32),
                   jax.ShapeDtypeStruct((B_HEADS, CHUNK, 2 * CHUNK), F32)],
        scratch_shapes=[pltpu.VMEM((CHUNK, 2 * KV_WIDTH), F32), pltpu.VMEM((CHUNK, B_HEADS), F32)],
        sem=("arbitrary",), rides=rides,
    )(proj, proj, proj, ab, dmixed, gb, sinks, bias)


def _sq_relu_grad(acc, z):
    return acc * (2.0 * jnp.maximum(z, 0.0))


def _local_step(x, tgt, sp, win, wo, wu, wd):
    T = x.shape[0]
    tm = min(512, T)
    tk = min(512, T)
    lg = sp["gate_norm_g"].reshape(A_GROUPS, CHUNK)
    lb = sp["gate_norm_b"].reshape(A_GROUPS, CHUNK)
    wsp = sp["w_spatial"].reshape(A_GROUPS, CHUNK, CHUNK)
    bs_col = sp["b_spatial"].reshape(A_GROUPS, CHUNK, 1)
    sinks = sp["attn_sinks"].reshape(1, B_HEADS)
    ga = sp["out_norm_a_g"].reshape(1, A_WIDTH)
    gb = sp["out_norm_b_g"].reshape(1, B_WIDTH)
    g1 = sp["mix_norm_g"].reshape(1, D_MODEL)
    g2 = sp["ffn_norm_g"].reshape(1, D_MODEL)
    gf = sp["final_norm_g"].reshape(1, D_MODEL)

    bias = _bias_build(sp["rel_bias_table"])
    n1, proj = _norm_matmul(x, g1, win, tm=tm, tn=PROJ_WIDTH // 2, name="in_proj")
    mixed, ab = _mixer_fwd(proj, lg, lb, wsp, bs_col, sinks, bias, ga, gb)
    h1 = _matmul_res(mixed, wo, x, tm=tm, tn=1024, tk=D_MODEL, prologue=_to_bf16, name="out_proj")
    n2, zp = _norm_matmul(h1, g2, wu, tm=tm, tn=1024, name="up_proj")
    h2 = _matmul_res(zp, wd, h1, tm=tm, tn=1024, tk=2048, prologue=_sq_relu_bf16, name="down_proj")

    dh2, dgf, loss = _loss_bwd(h2, tgt, gf, tm=tm)
    dzp = _matmul_nt(dh2, wd, tm=tm, tn=1024, tk=D_MODEL, name="bwd_dz", extra=zp, epilogue=_sq_relu_grad,
                     out_dtype=BF16)
    dwd = _matmul_tn(zp, dh2, tmo=1024, tn=1024, tk=tk, name="grad_w_down", a_prologue=_sq_relu_bf16)
    dwu = _matmul_tn(n2, dzp, tmo=1024, tn=1024, tk=tk, name="grad_w_up", shards=N_CHIPS)
    dn2 = _matmul_nt(dzp, wu, tm=tm, tn=1024, tk=2048, name="bwd_dn2")
    dh1, dg2 = _rms_bwd_res(dn2, h1, g2, dh2, tm=tm, name="ffn_norm_bwd")
    dwo = _matmul_tn(mixed, dh1, tmo=1024, tn=1024, tk=tk, name="grad_w_out")
    dmixed = _matmul_nt(dh1, wo, tm=tm, tn=1024, tk=D_MODEL, name="bwd_dmixed")
    duv, dga, dwsp, dbs, dlg, dlb = _gmlp_bwd(proj, ab, dmixed, ga, lg, lb, wsp, bs_col)
    dq, dkv, dgb, dsinks, dbias = _attn_bwd(proj, ab, dmixed, gb, sinks, bias)
    dtable = _bias_grad(dbias)
    dproj = jnp.concatenate([duv, dq, dkv], axis=1)
    dwin = _matmul_tn(n1, dproj, tmo=1024, tn=PROJ_WIDTH // 2, tk=tk, name="grad_w_in")
    dn1 = _matmul_nt(dproj, win, tm=tm, tn=1024, tk=PROJ_WIDTH, name="bwd_dn1")
    dx, dg1 = _rms_bwd_res(dn1, x, g1, dh1, tm=tm, name="mix_norm_bwd")

    small = {
        "rel_bias_table": dtable.reshape(N_BUCKETS, B_HEADS), "mix_norm_g": dg1, "gate_norm_g": dlg, "gate_norm_b": dlb,
        "w_spatial": dwsp, "b_spatial": dbs, "attn_sinks": dsinks, "out_norm_a_g": dga, "out_norm_b_g": dgb,
        "ffn_norm_g": dg2, "final_norm_g": dgf,
    }
    return loss, dx, (dwin, dwo, dwu, dwd), small


def _place():
    x, y, c = lax.axis_index("x"), lax.axis_index("y"), lax.axis_index("c")
    chips = [(1 - x, y), (x, 1 - y), (1 - x, 1 - y)]
    return x, y, c, chips


def _remote(src, dst, send_sem, recv_sem, to):
    return pltpu.make_async_remote_copy(src_ref=src, dst_ref=dst, send_sem=send_sem, recv_sem=recv_sem,
                                        device_id=to, device_id_type=MESH)


def _core_index():
    return lax.axis_index("c").astype(jnp.int32).reshape(1)


def _chip_index():
    return (2 * lax.axis_index("x") + lax.axis_index("y")).astype(jnp.int32).reshape(1)


def _cast_into_slot(w, *, tm, name):
    _, R, C = w.shape

    def body(me_ref, w_ref, o_ref):
        del me_ref
        o_ref[...] = w_ref[...].astype(BF16)

    return pl.pallas_call(
        body, name=name,
        grid_spec=pltpu.PrefetchScalarGridSpec(
            num_scalar_prefetch=1, grid=(R // tm,),
            in_specs=[pl.BlockSpec((None, tm, C), lambda i, me: (0, i, 0))],
            out_specs=pl.BlockSpec((None, tm, C), lambda i, me: (me[0], i, 0))),
        out_shape=jax.ShapeDtypeStruct((N_CHIPS, R, C), BF16), compiler_params=_params(("parallel",)),
    )(_chip_index(), w)


def _cast_into_slot_carrying(w, *, tm, name, rides):
    _, R, C = w.shape

    def body(w_ref, o_ref):
        o_ref[...] = w_ref[...].astype(BF16)

    return _call(
        body, name=name, grid=(R // tm,),
        in_specs=[pl.BlockSpec((None, tm, C), lambda i: (0, i, 0))],
        out_specs=pl.BlockSpec((None, tm, C), lambda i: (2 * lax.axis_index("x") + lax.axis_index("y"), i, 0)),
        out_shape=jax.ShapeDtypeStruct((N_CHIPS, R, C), BF16), sem=("arbitrary",), rides=rides,
    )(w)


def _gather_weights(slots):
    nw = len(slots)

    def body(*refs):
        fulls = refs[nw:2 * nw]
        send_sems, recv_sems = refs[2 * nw:]
        x, y, c, chips = _place()
        me = 2 * x + y
        sends = []
        for w in range(nw):
            hr = fulls[w].shape[1] // 2
            rows = pl.ds(c * hr, hr)
            for j, chip in enumerate(chips):
                mine = fulls[w].at[me, rows, :]
                cp = _remote(mine, mine, send_sems.at[6 * w + j], recv_sems.at[6 * w + j], (*chip, c))
                cp.start()
                sends.append(cp)
        for w in range(nw):
            hr = fulls[w].shape[1] // 2
            rows = pl.ds(c * hr, hr)
            for j, chip in enumerate(chips):
                landed = fulls[w].at[2 * chip[0] + chip[1], rows, :]
                _remote(landed, landed, send_sems.at[6 * w + j], recv_sems.at[6 * w + j], (x, y, c)).wait_recv()
                cp = _remote(landed, landed, send_sems.at[6 * w + 3 + j], recv_sems.at[6 * w + 3 + j], (x, y, 1 - c))
                cp.start()
                sends.append(cp)
        for w in range(nw):
            hr = fulls[w].shape[1] // 2
            rows = pl.ds((1 - c) * hr, hr)
            for j, chip in enumerate(chips):
                other = fulls[w].at[2 * chip[0] + chip[1], rows, :]
                _remote(other, other, send_sems.at[6 * w + 3 + j], recv_sems.at[6 * w + 3 + j], (x, y, c)).wait_recv()
        for cp in sends:
            cp.wait_send()

    any_spec = pl.BlockSpec(memory_space=pl.ANY)
    return pl.pallas_call(
        body, name="gather_weights",
        in_specs=[any_spec] * nw, out_specs=[any_spec] * nw,
        out_shape=[jax.ShapeDtypeStruct(s.shape, s.dtype) for s in slots],
        scratch_shapes=[pltpu.SemaphoreType.DMA((6 * nw,)), pltpu.SemaphoreType.DMA((6 * nw,))],
        input_output_aliases={w: w for w in range(nw)},
    )(*slots)


def _sibling_halves(grads):
    nw = len(grads)

    def body(*refs):
        gs, outs = refs[:nw], refs[nw:2 * nw]
        send_sems, recv_sems = refs[2 * nw:]
        x, y, c, _ = _place()
        cps = []
        for w in range(nw):
            hr = gs[w].shape[1] // 2
            cp = _remote(gs[w].at[:, pl.ds((1 - c) * hr, hr), :], outs[w], send_sems.at[w], recv_sems.at[w],
                         (x, y, 1 - c))
            cp.start()
            cps.append(cp)
        for cp in cps:
            cp.wait()

    any_spec = pl.BlockSpec(memory_space=pl.ANY)
    return pl.pallas_call(
        body, name="rs_sibling_halves",
        in_specs=[any_spec] * nw, out_specs=[any_spec] * nw,
        out_shape=[jax.ShapeDtypeStruct((g.shape[0], g.shape[1] // 2, g.shape[2]), g.dtype) for g in grads],
        scratch_shapes=[pltpu.SemaphoreType.DMA((nw,)), pltpu.SemaphoreType.DMA((nw,))],
    )(*grads)


def _pair_sum_bf16(g, got, *, tm, name):
    S, R, C = g.shape
    hr = R // 2
    nt = hr // tm

    def body(c_ref, g_ref, got_ref, o_ref):
        del c_ref
        o_ref[...] = (g_ref[...] + got_ref[...]).astype(BF16)

    return pl.pallas_call(
        body, name=name,
        grid_spec=pltpu.PrefetchScalarGridSpec(
            num_scalar_prefetch=1, grid=(S, nt),
            in_specs=[pl.BlockSpec((None, tm, C), lambda s, i, c: (s, c[0] * nt + i, 0)),
                      pl.BlockSpec((None, tm, C), lambda s, i, c: (s, i, 0))],
            out_specs=pl.BlockSpec((None, tm, C), lambda s, i, c: (s, i, 0))),
        out_shape=jax.ShapeDtypeStruct((S, hr, C), BF16),
        compiler_params=_params(("parallel", "parallel")),
    )(_core_index(), g, got)


def _scatter_to_owners(pairs):
    nw = len(pairs)

    def body(*refs):
        qs, outs = refs[:nw], refs[nw:2 * nw]
        send_sems, recv_sems = refs[2 * nw:]
        x, y, c, chips = _place()
        cps = []
        for w in range(nw):
            for j, chip in enumerate(chips):
                cp = _remote(qs[w].at[2 * chip[0] + chip[1]], outs[w].at[j], send_sems.at[3 * w + j],
                             recv_sems.at[3 * w + j], (*chip, c))
                cp.start()
                cps.append(cp)
        for cp in cps:
            cp.wait()

    any_spec = pl.BlockSpec(memory_space=pl.ANY)
    return pl.pallas_call(
        body, name="rs_scatter_to_owners",
        in_specs=[any_spec] * nw, out_specs=[any_spec] * nw,
        out_shape=[jax.ShapeDtypeStruct((3,) + q.shape[1:], q.dtype) for q in pairs],
        scratch_shapes=[pltpu.SemaphoreType.DMA((3 * nw,)), pltpu.SemaphoreType.DMA((3 * nw,))],
    )(*pairs)


def _owner_total(gh, others, *, tm, name):
    _, hr, C = gh.shape

    def body(me_ref, g_ref, o_ref_in, out_ref):
        del me_ref
        acc = g_ref[...]
        for j in range(3):
            acc = acc + o_ref_in[j].astype(F32)
        out_ref[...] = acc

    return pl.pallas_call(
        body, name=name,
        grid_spec=pltpu.PrefetchScalarGridSpec(
            num_scalar_prefetch=1, grid=(hr // tm,),
            in_specs=[pl.BlockSpec((None, tm, C), lambda i, me: (me[0], i, 0)),
                      pl.BlockSpec((3, tm, C), lambda i, me: (0, i, 0))],
            out_specs=pl.BlockSpec((tm, C), lambda i, me: (i, 0))),
        out_shape=jax.ShapeDtypeStruct((hr, C), F32),
        compiler_params=_params(("parallel",)),
    )(_chip_index(), gh, others)


def _owner_sum(g, got, others, *, tm, name):
    S, R, C = g.shape
    hr = R // 2
    nt = hr // tm

    def body(idx_ref, g_ref, got_ref, o_ref_in, out_ref):
        del idx_ref
        acc = g_ref[...] + got_ref[...]
        for j in range(3):
            acc = acc + o_ref_in[j].astype(F32)
        out_ref[...] = acc

    return pl.pallas_call(
        body, name=name,
        grid_spec=pltpu.PrefetchScalarGridSpec(
            num_scalar_prefetch=1, grid=(nt,),
            in_specs=[pl.BlockSpec((None, tm, C), lambda i, p: (p[1], p[0] * nt + i, 0)),
                      pl.BlockSpec((None, tm, C), lambda i, p: (p[1], i, 0)),
                      pl.BlockSpec((3, tm, C), lambda i, p: (0, i, 0))],
            out_specs=pl.BlockSpec((tm, C), lambda i, p: (i, 0))),
        out_shape=jax.ShapeDtypeStruct((hr, C), F32),
        compiler_params=_params(("parallel",)),
    )(jnp.concatenate([_core_index(), _chip_index()]), g, got, others)


def _swap_halves(halves):
    nw = len(halves)

    def body(*refs):
        hs, outs = refs[:nw], refs[nw:2 * nw]
        send_sems, recv_sems = refs[2 * nw:]
        x, y, c, _ = _place()
        cps = []
        for w in range(nw):
            cp = _remote(hs[w], outs[w], send_sems.at[w], recv_sems.at[w], (x, y, 1 - c))
            cp.start()
            cps.append(cp)
        for cp in cps:
            cp.wait()

    any_spec = pl.BlockSpec(memory_space=pl.ANY)
    return pl.pallas_call(
        body, name="rs_swap_halves",
        in_specs=[any_spec] * nw, out_specs=[any_spec] * nw,
        out_shape=[jax.ShapeDtypeStruct(h.shape, h.dtype) for h in halves],
        scratch_shapes=[pltpu.SemaphoreType.DMA((nw,)), pltpu.SemaphoreType.DMA((nw,))],
    )(*halves)


def _all_reduce_small(packed):
    R, C = packed.shape

    def body(in_ref, out_ref, slots, send_sems, recv_sems):
        x, y, c, _ = _place()
        me = 4 * x + 2 * y + c
        cps = []
        for k in range(1, N_DEV):
            p = (me + k) % N_DEV
            cp = _remote(in_ref, slots.at[me], send_sems.at[k - 1], recv_sems.at[k - 1], (p // 4, (p // 2) % 2, p % 2))
            cp.start()
            cps.append(cp)
        slots[me] = in_ref[...]
        for k in range(1, N_DEV):
            src = (me + N_DEV - k) % N_DEV
            _remote(in_ref, slots.at[src], send_sems.at[k - 1], recv_sems.at[k - 1], (x, y, c)).wait_recv()
        for cp in cps:
            cp.wait_send()
        acc = slots[0]
        for d in range(1, N_DEV):
            acc = acc + slots[d]
        out_ref[...] = acc

    vmem = pl.BlockSpec(memory_space=pltpu.VMEM)
    return pl.pallas_call(
        body, name="all_reduce_small", in_specs=[vmem], out_specs=vmem,
        out_shape=jax.ShapeDtypeStruct((R, C), F32),
        scratch_shapes=[pltpu.VMEM((N_DEV, R, C), F32), pltpu.SemaphoreType.DMA((N_DEV - 1,)),
                        pltpu.SemaphoreType.DMA((N_DEV - 1,))],
        compiler_params=_params(),
    )(packed)


def _adamw_math(w, g, m, v):
    m = ADAM_B1 * m + (1.0 - ADAM_B1) * g
    v = ADAM_B2 * v + (1.0 - ADAM_B2) * (g * g)
    m_hat = m / (1.0 - ADAM_B1 ** ADAM_STEP)
    v_hat = v / (1.0 - ADAM_B2 ** ADAM_STEP)
    delta = -ADAM_LR * (m_hat / (jnp.sqrt(v_hat) + ADAM_EPS) + ADAM_WD * w)
    return delta, m, v


def _adamw(w, g, m, v, *, tm, name):
    R, C = w.shape

    def body(w_ref, g_ref, m_ref, v_ref, d_ref, nm_ref, nv_ref):
        d_ref[...], nm_ref[...], nv_ref[...] = _adamw_math(w_ref[...], g_ref[...], m_ref[...], v_ref[...])

    spec = pl.BlockSpec((tm, C), lambda i: (i, 0))
    return pl.pallas_call(
        body, name=name, grid=(R // tm,), in_specs=[spec] * 4, out_specs=[spec] * 3,
        out_shape=[jax.ShapeDtypeStruct((R, C), F32)] * 3, compiler_params=_params(("parallel",)),
    )(w, g, m, v)


def _adamw_halves(w, own, got, m, v, *, tm, name, rides=()):
    _, R, C = w.shape
    nt = (R // 2) // tm

    def body(w_ref, own_ref, got_ref, m_ref, v_ref, g_ref, d_ref, nm_ref, nv_ref):
        g = jnp.where(pl.program_id(0) == lax.axis_index("c"), own_ref[...], got_ref[...])
        g_ref[...] = g
        d_ref[...], nm_ref[...], nv_ref[...] = _adamw_math(w_ref[...], g, m_ref[...], v_ref[...])

    whole = pl.BlockSpec((None, tm, C), lambda h, i: (0, h * nt + i, 0))
    half = pl.BlockSpec((tm, C), lambda h, i: (i, 0))
    return _call(
        body, name=name, grid=(2, nt), in_specs=[whole, half, half, whole, whole], out_specs=[whole] * 4,
        out_shape=[jax.ShapeDtypeStruct((1, R, C), F32)] * 4, sem=("parallel", "parallel"), rides=rides,
    )(w, own, got, m, v)


def _adamw_small(w, slots, m, v, *, name):
    def body(w_ref, slots_ref, m_ref, v_ref, g_ref, d_ref, nm_ref, nv_ref):
        g = slots_ref[0]
        for d in range(1, N_DEV):
            g = g + slots_ref[d]
        g_ref[...] = g
        d_ref[...], nm_ref[...], nv_ref[...] = _adamw_math(w_ref[...], g, m_ref[...], v_ref[...])

    vmem = pl.BlockSpec(memory_space=pltpu.VMEM)
    return pl.pallas_call(
        body, name=name, in_specs=[vmem] * 4, out_specs=[vmem] * 4,
        out_shape=[jax.ShapeDtypeStruct(w.shape, F32)] * 4, compiler_params=_params(),
    )(w, slots, m, v)


SMALL = ["rel_bias_table", "mix_norm_g", "gate_norm_g", "gate_norm_b", "w_spatial", "b_spatial", "attn_sinks",
         "out_norm_a_g", "out_norm_b_g", "ffn_norm_g", "final_norm_g"]
SMALL_A = ["gate_norm_g", "gate_norm_b", "w_spatial", "b_spatial", "out_norm_a_g"]
SMALL_B = ["rel_bias_table", "mix_norm_g", "attn_sinks", "out_norm_b_g", "ffn_norm_g", "final_norm_g"]
LARGE = ["w_in", "w_out", "w_up", "w_down"]
ROW_TILE = {"w_in": 208, "w_out": 256, "w_up": 256, "w_down": 256}
WEIGHTS = ["rel_bias_table", "mix_norm_g", "w_in", "gate_norm_g", "gate_norm_b", "w_spatial", "b_spatial", "attn_sinks",
           "out_norm_a_g", "out_norm_b_g", "w_out", "ffn_norm_g", "w_up", "w_down", "final_norm_g"]
PACK_UNIT = 8 * 128


def _pack(parts):
    rows = []
    for p in parts:
        flat = p.reshape(-1)
        pad = (-flat.shape[0]) % PACK_UNIT
        rows.append(jnp.pad(flat, (0, pad)).reshape(-1, 128))
    return jnp.concatenate(rows, axis=0)


def _unpack(packed, like):
    out, row = [], 0
    for p in like:
        n = math.prod(p.shape)
        nrows = (n + PACK_UNIT - 1) // PACK_UNIT * 8
        out.append(packed[row:row + nrows].reshape(-1)[:n].reshape(p.shape))
        row += nrows
    return out


def kernel(x, rel_bias_table, mix_norm_g, w_in, gate_norm_g, gate_norm_b, w_spatial, b_spatial, attn_sinks, out_norm_a_g, out_norm_b_g, w_out, ffn_norm_g, w_up, w_down, final_norm_g, loss_target, m_rel_bias_table, m_mix_norm_g, m_w_in, m_gate_norm_g, m_gate_norm_b, m_w_spatial, m_b_spatial, m_attn_sinks, m_out_norm_a_g, m_out_norm_b_g, m_w_out, m_ffn_norm_g, m_w_up, m_w_down, m_final_norm_g, v_rel_bias_table, v_mix_norm_g, v_w_in, v_gate_norm_g, v_gate_norm_b, v_w_spatial, v_b_spatial, v_attn_sinks, v_out_norm_a_g, v_out_norm_b_g, v_w_out, v_ffn_norm_g, v_w_up, v_w_down, v_final_norm_g):
    args = dict(locals())
    wts = {n: args[n] for n in WEIGHTS}
    mom = {n: args["m_" + n] for n in WEIGHTS}
    var = {n: args["v_" + n] for n in WEIGHTS}
    sp = {n: wts[n] for n in SMALL}
    x2, tgt = x[0], loss_target[0]
    T = x2.shape[0]
    tm = min(512, T)
    tl = min(1024, T)
    tg = min(2048, T)
    lg = sp["gate_norm_g"].reshape(A_GROUPS, CHUNK)
    lb = sp["gate_norm_b"].reshape(A_GROUPS, CHUNK)
    wsp = sp["w_spatial"].reshape(A_GROUPS, CHUNK, CHUNK)
    bs_col = sp["b_spatial"].reshape(A_GROUPS, CHUNK, 1)
    sinks = sp["attn_sinks"].reshape(1, B_HEADS)
    ga = sp["out_norm_a_g"].reshape(1, A_WIDTH)
    gb = sp["out_norm_b_g"].reshape(1, B_WIDTH)
    g1 = sp["mix_norm_g"].reshape(1, D_MODEL)
    g2 = sp["ffn_norm_g"].reshape(1, D_MODEL)
    gf = sp["final_norm_g"].reshape(1, D_MODEL)

    def owner_total(n, gh, others):
        return _owner_total(gh, others, tm=ROW_TILE[n], name="rs_owner_total_" + n)

    def halves_view(at, shards):
        return at.reshape(shards, 2, at.shape[0] // shards // 2, at.shape[1])

    for d in (wts, mom, var):
        d["w_in"] = jnp.swapaxes(d["w_in"], 1, 2)

    s_in = _cast_into_slot(wts["w_in"], tm=ROW_TILE["w_in"], name="cast_w_in")
    s_up, ((s_in,),) = _cast_into_slot_carrying(wts["w_up"], tm=256, name="cast_w_up",
                                                rides=[_ride_gather(s_in, ici=(0, 1, 2))])
    s_down, ((s_in,),) = _cast_into_slot_carrying(wts["w_down"], tm=256, name="cast_w_down",
                                                  rides=[_ride_gather(s_in, ici=(1, 2, 2), d2d=(0, 1, 2))])
    s_out, ((g_in,),) = _cast_into_slot_carrying(wts["w_out"], tm=256, name="cast_w_out",
                                                 rides=[_ride_gather(s_in, d2d=(1, 2, 2))])
    win_t = g_in.reshape(PROJ_WIDTH, D_MODEL)
    bias = _bias_build(sp["rel_bias_table"])
    (n1, proj), ((s_out,), (s_up,)) = _norm_matmul_wide(
        x2, g1, win_t, tm=tm, tn=PROJ_WIDTH // 2, name="in_proj",
        rides=[_ride_gather(s_out, ici=(0, 1, 1)), _ride_gather(s_up, ici=(0, 2, 8))])
    (mixed, mixed_t, ab), ((g_out,), (s_up,), (n1_sib,)) = _mixer_fwd(
        proj, lg, lb, wsp, bs_col, sinks, bias, ga, gb,
        rides=[_ride_gather(s_out, d2d=(0, 1, 1)), _ride_gather(s_up, d2d=(0, 2, 8), ici=(2, 8, 8)),
               _ride_to_sibling(n1, first=True)])
    wo = g_out.reshape(A_WIDTH + B_WIDTH, D_MODEL)
    mixed_t = halves_view(mixed_t, N_CHIPS)
    h1, ((wu,), (s_down,), (mixed_t_sib,)) = _matmul_res(
        mixed, wo, x2, tm=tl, tn=1024, tk=D_MODEL, prologue=_to_bf16, name="out_proj",
        rides=[_ride_gather(s_up, d2d=(2, 8, 8)), _ride_gather(s_down, ici=(0, 2, 8)), _ride_to_sibling(mixed_t, halves=True)])
    (n2t, zp, z2, z2t), ((g_down,),) = _norm_matmul_sq(
        h1, g2, wu, tm=tl, tn=512, name="up_proj", rides=[_ride_gather(s_down, d2d=(0, 2, 8), both=(2, 8, 8), mid_frac=0.75)])
    wd = g_down.reshape(D_FF, D_MODEL)
    n2t, z2t = halves_view(n2t, 1), halves_view(z2t, N_CHIPS)
    h2, ((n2t_sib,), (z2t_sib,)) = _matmul_res(
        z2, wd, h1, tm=tl, tn=1024, tk=2048, prologue=_to_bf16, name="down_proj",
        rides=[_ride_to_sibling(n2t, halves=True), _ride_to_sibling(z2t, halves=True)])

    dh2, dh2b, dgf, loss = _loss_bwd(h2, tgt, gf, tm=tm)
    dzp, ((dh2b_sib,),) = _matmul_nt(dh2b, wd, tm=tl, tn=1024, tk=D_MODEL, name="bwd_dz", extra=zp,
                                     epilogue=_sq_relu_grad, out_dtype=BF16, rides=[_ride_to_sibling(dh2b)])
    (gd, gdb), ((dzp_sib,),) = _grad_pair(z2t, z2t_sib, dh2b, dh2b_sib, cols_sharded=False, tmo=1024, tk=tl,
                                          name="grad_w_down", rides=[_ride_to_sibling(dzp)])
    (gu, gub), ((o_d,),) = _grad_pair(n2t, n2t_sib, dzp, dzp_sib, cols_sharded=True, tmo=1024, tk=tl,
                                      name="grad_w_up", rides=[_ride_scatter(gdb, None, (0, 7, 8))])
    dn2, ((o_d,), (o_u,)) = _matmul_nt(dzp, wu, tm=tl, tn=1024, tk=2048, name="bwd_dn2",
                                       rides=[_ride_scatter(gdb, o_d, (7, 8, 8)), _ride_scatter(gub, None, (0, 6, 8))])
    h_d = owner_total("w_down", gd, o_d)
    (dh1, dh1b, dg2), ((o_u,),) = _rms_bwd_res(dn2, h1, g2, dh2, tm=tm, name="ffn_norm_bwd",
                                               rides=[_ride_scatter(gub, o_u, (6, 8, 8))])
    h_u = owner_total("w_up", gu, o_u)
    dmixed, ((dh1b_sib,), (w_d,)) = _matmul_nt(dh1b, wo, tm=tl, tn=1024, tk=D_MODEL, name="bwd_dmixed",
                                               rides=[_ride_to_sibling(dh1b), _ride_swap(h_d)])
    (go, gob), ((w_u,),) = _grad_pair(mixed_t, mixed_t_sib, dh1b, dh1b_sib, cols_sharded=False, tmo=256, tk=tl,
                                      name="grad_w_out", rides=[_ride_swap(h_u)])
    (duv, duv_t, dga, dwsp, dbs, dlg, dlb), ((o_o,),) = _gmlp_bwd(proj, ab, dmixed, ga, lg, lb, wsp, bs_col,
                                                                  rides=[_ride_scatter(gob)])
    h_o = owner_total("w_out", go, o_o)
    small = {"gate_norm_g": dlg, "gate_norm_b": dlb, "w_spatial": dwsp, "b_spatial": dbs, "out_norm_a_g": dga}
    (dq, dkv, dq_t, dkv_t, dgb, dsinks, dbias), ((w_o,),) = _attn_bwd(proj, ab, dmixed, gb, sinks, bias,
                                                                      rides=[_ride_swap(h_o)])
    dtable = _bias_grad(dbias)
    dproj = jnp.concatenate([duv, dq, dkv], axis=1)
    dproj_t = halves_view(jnp.concatenate([duv_t, dq_t, dkv_t], axis=0), N_CHIPS)
    ((dproj_t_sib,),) = _carrier([_ride_to_sibling(dproj_t, halves=True)], name="trade_dproj_t")
    (gi, gib), ((slots_a,),) = _grad_pair(
        dproj_t, dproj_t_sib, n1, n1_sib, cols_sharded=False, tmo=PROJ_WIDTH // N_CHIPS // 2, tk=tl, name="grad_w_in",
        rides=[_ride_small_to_all(_pack([small[n] for n in SMALL_A]))])
    dn1, ((o_i,),) = _matmul_nn(dproj, win_t, tmo=tl, tn=1024, tk=PROJ_WIDTH, name="bwd_dn1", rides=[_ride_scatter(gib)])
    h_i = owner_total("w_in", gi, o_i)
    dx, _, dg1 = _rms_bwd_res(dn1, x2, g1, dh1, tm=tm, name="mix_norm_bwd")
    small.update({"rel_bias_table": dtable.reshape(N_BUCKETS, B_HEADS), "mix_norm_g": dg1, "attn_sinks": dsinks,
                  "out_norm_b_g": dgb, "ffn_norm_g": dg2, "final_norm_g": dgf})
    (w_i,), (slots_b,) = _carrier([_ride_swap(h_i), _ride_small_to_all(_pack([small[n] for n in SMALL_B]))],
                                  name="swap_w_in")

    out_g, out_d, out_m, out_v = {}, {}, {}, {}
    for n, h, s in zip(LARGE, [h_i, h_o, h_u, h_d], [w_i, w_o, w_u, w_d]):
        res = _adamw_halves(wts[n], h, s, mom[n], var[n], tm=ROW_TILE[n], name="adamw_" + n)
        if n == "w_in":
            res = [jnp.swapaxes(r, 1, 2) for r in res]
        out_g[n], out_d[n], out_m[n], out_v[n] = res
    for names, slots, tag in ((SMALL_A, slots_a, "a"), (SMALL_B, slots_b, "b")):
        like = [wts[n] for n in names]
        res = _adamw_small(_pack(like), slots, _pack([mom[n] for n in names]), _pack([var[n] for n in names]),
                           name="adamw_small_" + tag)
        for store, packed in zip((out_g, out_d, out_m, out_v), res):
            for n, val in zip(names, _unpack(packed, like)):
                store[n] = val

    total = lax.psum(loss[0, 0], ("x", "y", "c"))
    return (total, dx[None], *[out_g[n] for n in WEIGHTS], *[out_d[n] for n in WEIGHTS],
            *[out_m[n] for n in WEIGHTS], *[out_v[n] for n in WEIGHTS])
```

```python
import functools
import math

import numpy as np
import jax
import jax.numpy as jnp
from jax import lax
from jax.experimental import pallas as pl
from jax.experimental.pallas import tpu as pltpu

F32 = jnp.float32
BF16 = jnp.bfloat16

D_MODEL = 2048
CHUNK = 128
A_GROUPS = 8
A_WIDTH = 1024
HEAD_DIM = 64
B_HEADS = 16
Q_PER_KV = 8
B_WIDTH = 1024
KV_WIDTH = 128
PROJ_WIDTH = 3328
D_FF = 8192
N_BUCKETS = 32
EPS = 1e-5
NEG = -1e30
SCALE = HEAD_DIM ** -0.5
N_CHIPS = 4
N_DEV = 8

ADAM_LR = 0.001
ADAM_B1 = 0.9
ADAM_B2 = 0.999
ADAM_EPS = 1e-08
ADAM_WD = 0.01
ADAM_STEP = 10

VMEM_LIMIT = 60 * 1024 * 1024
MESH = pl.DeviceIdType.MESH


def _bucket_thresholds():
    d = np.arange(CHUNK)
    n_exact = N_BUCKETS // 2
    relf = np.maximum(d, n_exact).astype(np.float64)
    large = n_exact + (np.log(relf / n_exact) / math.log(CHUNK / n_exact) * (N_BUCKETS - n_exact)).astype(np.int32)
    bucket = np.where(d < n_exact, d, np.minimum(large, N_BUCKETS - 1))
    return [int(np.min(d[bucket >= b])) for b in range(1, N_BUCKETS)]


BUCKET_THR = _bucket_thresholds()


def _params(sem=None):
    return pltpu.CompilerParams(dimension_semantics=sem, vmem_limit_bytes=VMEM_LIMIT)


def _gelu(x):
    c = math.sqrt(2.0 / math.pi)
    return 0.5 * x * (1.0 + jnp.tanh(c * (x + 0.044715 * (x * x * x))))


def _gelu_and_grad(x):
    c = math.sqrt(2.0 / math.pi)
    x2 = x * x
    t = jnp.tanh(c * (x + 0.044715 * (x2 * x)))
    g = 0.5 * x * (1.0 + t)
    dg = 0.5 * (1.0 + t) + 0.5 * x * (1.0 - t * t) * (c * (1.0 + 3.0 * 0.044715 * x2))
    return g, dg


def _dot(a, b):
    return jnp.dot(a, b, preferred_element_type=F32)


def _dot_nt(a, b):
    return lax.dot_general(a, b, (((1,), (1,)), ((), ())), preferred_element_type=F32)


def _dot_tn(a, b):
    return lax.dot_general(a, b, (((0,), (0,)), ((), ())), preferred_element_type=F32)


def _rms_bwd(dn, h, g):
    r = lax.rsqrt(jnp.mean(h * h, axis=-1, keepdims=True) + EPS)
    w = dn * g
    dh = r * w - h * ((r * r * r) * jnp.mean(w * h, axis=-1, keepdims=True))
    return dh, r


def _place():
    x, y, c = lax.axis_index("x"), lax.axis_index("y"), lax.axis_index("c")
    chips = [(1 - x, y), (x, 1 - y), (1 - x, 1 - y)]
    return x, y, c, chips


def _remote(src, dst, send_sem, recv_sem, to):
    return pltpu.make_async_remote_copy(src_ref=src, dst_ref=dst, send_sem=send_sem, recv_sem=recv_sem,
                                        device_id=to, device_id_type=MESH)


class _Ride:
    def __init__(self, args, out_shape, n_sem, start, finish, mid=None, mid_frac=0.8, aliases=None):
        self.args, self.out_shape, self.n_sem = list(args), list(out_shape), n_sem
        self.start, self.mid, self.finish, self.mid_frac = start, mid, finish, mid_frac
        self.aliases = dict(aliases or {})


def _call(body, *, name, grid, in_specs, out_specs, out_shape, scratch_shapes=(), sem=None, rides=()):
    single = not isinstance(out_shape, (list, tuple))
    out_specs = [out_specs] if single else list(out_specs)
    out_shape = [out_shape] if single else list(out_shape)
    n_in, n_out, n_scr = len(in_specs), len(out_shape), len(scratch_shapes)
    r_in = [len(r.args) for r in rides]
    r_out = [len(r.out_shape) for r in rides]
    any_spec = pl.BlockSpec(memory_space=pl.ANY)
    aliases, off_i, off_o = {}, n_in, n_out
    for r in rides:
        for i, o in r.aliases.items():
            aliases[off_i + i] = off_o + o
        off_i += len(r.args)
        off_o += len(r.out_shape)
    steps = math.prod(grid)

    def wrapped(*refs):
        p = 0
        ins = refs[p:p + n_in]; p += n_in
        rins = refs[p:p + sum(r_in)]; p += sum(r_in)
        outs = refs[p:p + n_out]; p += n_out
        routs = refs[p:p + sum(r_out)]; p += sum(r_out)
        scr = refs[p:p + n_scr]; p += n_scr
        sems = refs[p:]
        parts, pi, po = [], 0, 0
        for k, r in enumerate(rides):
            parts.append((rins[pi:pi + r_in[k]], routs[po:po + r_out[k]], sems[2 * k], sems[2 * k + 1]))
            pi += r_in[k]
            po += r_out[k]
        lin = 0
        for d in range(len(grid)):
            lin = lin * grid[d] + pl.program_id(d)
        if rides:
            @pl.when(lin == 0)
            def _():
                for r, part in zip(rides, parts):
                    r.start(*part)
        body(*ins, *outs, *scr)
        for r, part in zip(rides, parts):
            if r.mid is not None:
                @pl.when(lin == min(steps - 1, int(r.mid_frac * steps)))
                def _(r=r, part=part):
                    r.mid(*part)
        if rides:
            @pl.when(lin == steps - 1)
            def _():
                for r, part in zip(rides, parts):
                    r.finish(*part)

    scratch = list(scratch_shapes)
    for r in rides:
        scratch += [pltpu.SemaphoreType.DMA((r.n_sem,)), pltpu.SemaphoreType.DMA((r.n_sem,))]
    if rides:
        sem = ("arbitrary",) * len(grid)
    res = pl.pallas_call(
        wrapped, name=name, grid=grid,
        in_specs=list(in_specs) + [any_spec] * sum(r_in),
        out_specs=out_specs + [any_spec] * sum(r_out),
        out_shape=out_shape + [s for r in rides for s in r.out_shape],
        scratch_shapes=scratch, input_output_aliases=aliases,
        compiler_params=_params(sem),
    )

    def run(*args):
        got = res(*args, *[a for r in rides for a in r.args])
        mine = got[0] if single else list(got[:n_out])
        if not rides:
            return mine
        rest, out = list(got[n_out:]), []
        for k in range(len(rides)):
            out.append(rest[:r_out[k]])
            rest = rest[r_out[k]:]
        return mine, out

    return run


def _ride_gather(slot, ici=None, d2d=None, both=None, mid_frac=0.8):
    half = slot.shape[1] // 2

    def rows(part, c):
        k0, k1, n = part
        return pl.ds(c * half + k0 * (half // n), (k1 - k0) * (half // n))

    def ici_copies(outs, ss, rs, part, base):
        x, y, c, chips = _place()
        mine = outs[0].at[2 * x + y, rows(part, c), :]
        return [_remote(mine, mine, ss.at[base + j], rs.at[base + j], (*chip, c)) for j, chip in enumerate(chips)]

    def d2d_copies(outs, ss, rs, part, base):
        x, y, c, chips = _place()
        return [_remote(outs[0].at[2 * chip[0] + chip[1], rows(part, c), :], outs[0].at[2 * chip[0] + chip[1], rows(part, c), :],
                        ss.at[base + j], rs.at[base + j], (x, y, 1 - c)) for j, chip in enumerate(chips)]

    def arrivals(outs, ss, rs, part, base, from_sibling):
        x, y, c, chips = _place()
        for j, chip in enumerate(chips):
            dst = outs[0].at[2 * chip[0] + chip[1], rows(part, 1 - c if from_sibling else c), :]
            _remote(dst, dst, ss.at[base + j], rs.at[base + j], (x, y, c)).wait_recv()

    def start(ins, outs, ss, rs):
        for part, base in ((ici, 0), (both, 6)):
            if part is not None:
                for cp in ici_copies(outs, ss, rs, part, base):
                    cp.start()
        if d2d is not None:
            for cp in d2d_copies(outs, ss, rs, d2d, 3):
                cp.start()

    def mid(ins, outs, ss, rs):
        arrivals(outs, ss, rs, both, 6, False)
        for cp in d2d_copies(outs, ss, rs, both, 9):
            cp.start()

    def finish(ins, outs, ss, rs):
        if ici is not None:
            arrivals(outs, ss, rs, ici, 0, False)
        if d2d is not None:
            arrivals(outs, ss, rs, d2d, 3, True)
        if both is not None:
            arrivals(outs, ss, rs, both, 9, True)
        for part, base, fn in ((ici, 0, ici_copies), (d2d, 3, d2d_copies), (both, 6, ici_copies), (both, 9, d2d_copies)):
            if part is not None:
                for cp in fn(outs, ss, rs, part, base):
                    cp.wait_send()

    return _Ride([slot], [jax.ShapeDtypeStruct(slot.shape, slot.dtype)], 12, start, finish,
                 mid=mid if both is not None else None, mid_frac=mid_frac, aliases={0: 0})


def _ride_sibling_halves(g):
    S, R, C = g.shape
    hr = R // 2

    def copy(ins, outs, ss, rs):
        x, y, c, _ = _place()
        return _remote(ins[0].at[:, pl.ds((1 - c) * hr, hr), :], outs[0], ss.at[0], rs.at[0], (x, y, 1 - c))

    return _Ride([g], [jax.ShapeDtypeStruct((S, hr, C), g.dtype)], 1,
                 lambda *a: copy(*a).start(), lambda *a: copy(*a).wait())


def _ride_scatter(q, land=None, part=(0, 1)):
    k0, k1, n = part if len(part) == 3 else (part[0], part[0] + 1, part[1])
    rows_n = q.shape[1] // n
    rows = pl.ds(k0 * rows_n, (k1 - k0) * rows_n)

    def copies(ins, outs, ss, rs):
        x, y, c, chips = _place()
        return [_remote(ins[0].at[2 * chip[0] + chip[1], rows, :], outs[0].at[j, rows, :], ss.at[j], rs.at[j], (*chip, c))
                for j, chip in enumerate(chips)]

    def start(*a):
        for cp in copies(*a):
            cp.start()

    def finish(*a):
        for cp in copies(*a):
            cp.wait()

    shape = jax.ShapeDtypeStruct((3,) + q.shape[1:], q.dtype)
    if land is None:
        return _Ride([q], [shape], 3, start, finish)
    return _Ride([q, land], [shape], 3, start, finish, aliases={1: 0})


def _ride_to_sibling(a, halves=False, first=False):
    def copy(ins, outs, ss, rs):
        x, y, c, _ = _place()
        src = ins[0].at[:, 1 - c] if halves else (ins[0].at[0] if first else ins[0])
        return _remote(src, outs[0], ss.at[0], rs.at[0], (x, y, 1 - c))

    shape = (a.shape[0],) + a.shape[2:] if halves else (a.shape[1:] if first else a.shape)
    return _Ride([a], [jax.ShapeDtypeStruct(shape, a.dtype)], 1, lambda *a_: copy(*a_).start(), lambda *a_: copy(*a_).wait())


def _ride_swap(h):
    def copy(ins, outs, ss, rs):
        x, y, c, _ = _place()
        return _remote(ins[0], outs[0], ss.at[0], rs.at[0], (x, y, 1 - c))

    return _Ride([h], [jax.ShapeDtypeStruct(h.shape, h.dtype)], 1,
                 lambda *a: copy(*a).start(), lambda *a: copy(*a).wait())


def _mesh_place(p):
    return (p // 4, (p // 2) % 2, p % 2)


def _ride_small_to_all(packed):
    def copies(ins, outs, ss, rs):
        x, y, c, _ = _place()
        me = 4 * x + 2 * y + c
        return [_remote(ins[0], outs[0].at[me], ss.at[k - 1], rs.at[k - 1], _mesh_place((me + k) % N_DEV))
                for k in range(1, N_DEV)]

    def own(ins, outs, ss, rs):
        x, y, c, _ = _place()
        return pltpu.make_async_copy(ins[0], outs[0].at[4 * x + 2 * y + c], ss.at[N_DEV - 1])

    def start(*a):
        own(*a).start()
        for cp in copies(*a):
            cp.start()

    def finish(ins, outs, ss, rs):
        x, y, c, _ = _place()
        me = 4 * x + 2 * y + c
        for k in range(1, N_DEV):
            _remote(ins[0], outs[0].at[(me + N_DEV - k) % N_DEV], ss.at[k - 1], rs.at[k - 1], (x, y, c)).wait_recv()
        for cp in copies(ins, outs, ss, rs):
            cp.wait_send()
        own(ins, outs, ss, rs).wait()

    return _Ride([packed], [jax.ShapeDtypeStruct((N_DEV,) + packed.shape, packed.dtype)], N_DEV, start, finish)


def _carrier(rides, *, name):
    _, outs = _call(lambda: None, name=name, grid=(1,), in_specs=[], out_specs=[], out_shape=[], rides=rides)()
    return outs


def _sq_relu_bf16(z):
    z = jnp.maximum(z, 0.0)
    return (z * z).astype(BF16)


def _norm_bf16(a_ref, g_ref):
    xf = a_ref[...]
    r = lax.rsqrt(jnp.mean(xf * xf, axis=-1, keepdims=True) + EPS)
    return ((xf * r) * g_ref[...]).astype(BF16)


def _norm_matmul_wide(a, g, b, *, tm, tn, name, rides=()):
    T, K = a.shape
    N = b.shape[0]

    def body(a_ref, g_ref, b_ref, n_ref, o_ref):
        n = _norm_bf16(a_ref, g_ref)
        n_ref[...] = n
        o_ref[...] = _dot_nt(n, b_ref[...])

    return _call(
        body, name=name, grid=(N // tn, T // tm),
        in_specs=[pl.BlockSpec((tm, K), lambda j, i: (i, 0)), pl.BlockSpec((1, K), lambda j, i: (0, 0)),
                  pl.BlockSpec((tn, K), lambda j, i: (j, 0))],
        out_specs=[pl.BlockSpec((None, tm, K), lambda j, i: (j, i, 0)), pl.BlockSpec((tm, tn), lambda j, i: (i, j))],
        out_shape=[jax.ShapeDtypeStruct((N // tn, T, K), BF16), jax.ShapeDtypeStruct((T, N), F32)],
        sem=("arbitrary", "arbitrary"), rides=rides,
    )(a, g, b)


def _norm_matmul_sq(a, g, b, *, tm, tn, name, rides=()):
    T, K = a.shape
    per = b.shape[2] // tn
    N = b.shape[0] * b.shape[2]

    def body(a_ref, g_ref, b_ref, nt_ref, o_ref, z_ref, zt_ref, n_scr):
        @pl.when(pl.program_id(1) == 0)
        def _():
            n = _norm_bf16(a_ref, g_ref)
            n_scr[...] = n
            nt_ref[...] = n.T
        r = jnp.maximum(_dot(n_scr[...], b_ref[...]), 0.0)
        o_ref[...] = r.astype(BF16)
        z = (r * r).astype(BF16)
        z_ref[...] = z
        zt_ref[...] = z.T

    return _call(
        body, name=name, grid=(T // tm, N // tn),
        in_specs=[pl.BlockSpec((tm, K), lambda i, j: (i, 0)), pl.BlockSpec((1, K), lambda i, j: (0, 0)),
                  pl.BlockSpec((None, K, tn), lambda i, j: (j // per, 0, j % per))],
        out_specs=[pl.BlockSpec((K, tm), lambda i, j: (0, i)), pl.BlockSpec((tm, tn), lambda i, j: (i, j)),
                   pl.BlockSpec((tm, tn), lambda i, j: (i, j)), pl.BlockSpec((tn, tm), lambda i, j: (j, i))],
        out_shape=[jax.ShapeDtypeStruct((K, T), BF16), jax.ShapeDtypeStruct((T, N), BF16),
                   jax.ShapeDtypeStruct((T, N), BF16), jax.ShapeDtypeStruct((N, T), BF16)],
        scratch_shapes=[pltpu.VMEM((tm, K), BF16)],
        sem=("parallel", "arbitrary"), rides=rides,
    )(a, g, b)


def _grad_pair(at, at_sib, b, b_sib, *, cols_sharded, tmo, tk, name, rides=()):
    S, _, hr, T = at.shape
    C = b.shape[-1] // N_CHIPS if cols_sharded else b.shape[-1]
    nk = T // tk
    a_sel = (lambda s: 0) if cols_sharded else (lambda s: s)
    b_sel = (lambda s: s) if cols_sharded else (lambda s: 0)
    if b.ndim == 3:
        b_spec = pl.BlockSpec((None, tk, C), lambda s, i, k: (0, k, b_sel(s)))
    else:
        b_spec = pl.BlockSpec((tk, C), lambda s, i, k: (k, b_sel(s)))

    def body(a_ref, as_ref, b_ref, bs_ref, o_ref, ob_ref):
        k = pl.program_id(2)
        p = _dot(a_ref[...], b_ref[...]) + _dot(as_ref[...], bs_ref[...])

        @pl.when(k == 0)
        def _():
            o_ref[...] = p

        @pl.when(k > 0)
        def _():
            o_ref[...] += p

        @pl.when(k == nk - 1)
        def _():
            ob_ref[...] = o_ref[...].astype(BF16)

    out = pl.BlockSpec((None, tmo, C), lambda s, i, k: (s, i, 0))
    return _call(
        body, name=name, grid=(N_CHIPS, hr // tmo, nk),
        in_specs=[pl.BlockSpec((None, None, tmo, tk), lambda s, i, k: (a_sel(s), lax.axis_index("c"), i, k)),
                  pl.BlockSpec((None, tmo, tk), lambda s, i, k: (a_sel(s), i, k)),
                  b_spec, pl.BlockSpec((tk, C), lambda s, i, k: (k, b_sel(s)))],
        out_specs=[out, out],
        out_shape=[jax.ShapeDtypeStruct((N_CHIPS, hr, C), F32), jax.ShapeDtypeStruct((N_CHIPS, hr, C), BF16)],
        sem=("parallel", "parallel", "arbitrary"), rides=rides,
    )(at, at_sib, b, b_sib)


def _matmul_nn(at, b, *, tmo, tn, tk, name, shards=1, rides=()):
    M, T = at.shape[-2:]
    N = b.shape[-1]
    if at.ndim == 3:
        a_spec = pl.BlockSpec((None, tmo, tk), lambda i, j, k: (0, i, k))
    else:
        a_spec = pl.BlockSpec((tmo, tk), lambda i, j, k: (i, k))
    if b.ndim == 3:
        b_spec = pl.BlockSpec((None, tk, tn), lambda i, j, k: (0, k, j))
    else:
        b_spec = pl.BlockSpec((tk, tn), lambda i, j, k: (k, j))
    if shards > 1:
        per = (N // shards) // tn
        out_spec = pl.BlockSpec((None, tmo, tn), lambda i, j, k: (j // per, i, j % per))
        out_shape = jax.ShapeDtypeStruct((shards, M, N // shards), F32)
    else:
        out_spec = pl.BlockSpec((tmo, tn), lambda i, j, k: (i, j))
        out_shape = jax.ShapeDtypeStruct((M, N), F32)

    def body(a_ref, b_ref, o_ref):
        k = pl.program_id(2)
        p = _dot(a_ref[...], b_ref[...])

        @pl.when(k == 0)
        def _():
            o_ref[...] = p

        @pl.when(k > 0)
        def _():
            o_ref[...] += p

    return _call(
        body, name=name, grid=(M // tmo, N // tn, T // tk),
        in_specs=[a_spec, b_spec],
        out_specs=out_spec, out_shape=out_shape,
        sem=("parallel", "parallel", "arbitrary"), rides=rides,
    )(at, b)


def _to_bf16(v):
    return v.astype(BF16)


def _matmul_res(a, b, res, *, tm, tn, tk, prologue, name, rides=()):
    T, K = a.shape
    N = b.shape[1]

    def body(a_ref, b_ref, res_ref, o_ref):
        k = pl.program_id(2)
        p = _dot(prologue(a_ref[...]), b_ref[...])

        @pl.when(k == 0)
        def _():
            o_ref[...] = res_ref[...] + p

        @pl.when(k > 0)
        def _():
            o_ref[...] += p

    return _call(
        body, name=name, grid=(T // tm, N // tn, K // tk),
        in_specs=[pl.BlockSpec((tm, tk), lambda i, j, k: (i, k)), pl.BlockSpec((tk, tn), lambda i, j, k: (k, j)),
                  pl.BlockSpec((tm, tn), lambda i, j, k: (i, j))],
        out_specs=pl.BlockSpec((tm, tn), lambda i, j, k: (i, j)),
        out_shape=jax.ShapeDtypeStruct((T, N), F32),
        sem=("parallel", "parallel", "arbitrary"), rides=rides,
    )(a, b, res)


def _matmul_nt(a, b, *, tm, tn, tk, name, extra=None, epilogue=None, out_dtype=F32, rides=()):
    T, K = a.shape
    if b.ndim == 3:
        per = b.shape[2] // tk
        N = b.shape[1]
        b_spec = pl.BlockSpec((None, tn, tk), lambda i, j, k: (k // per, j, k % per))
    else:
        N = b.shape[0]
        b_spec = pl.BlockSpec((tn, tk), lambda i, j, k: (j, k))
    nk = K // tk
    assert out_dtype == F32 or nk == 1
    in_specs = [pl.BlockSpec((tm, tk), lambda i, j, k: (i, k)), b_spec]
    args = [a, b]
    if extra is not None:
        in_specs.append(pl.BlockSpec((tm, tn), lambda i, j, k: (i, j)))
        args.append(extra)

    def body(*refs):
        a_ref, b_ref = refs[0], refs[1]
        o_ref = refs[-1]
        p = _dot_nt(a_ref[...].astype(BF16), b_ref[...])
        if nk == 1:
            if epilogue is not None:
                p = epilogue(p, refs[2][...])
            o_ref[...] = p.astype(out_dtype)
        else:
            k = pl.program_id(2)

            @pl.when(k == 0)
            def _():
                o_ref[...] = p

            @pl.when(k > 0)
            def _():
                o_ref[...] += p

    return _call(
        body, name=name, grid=(T // tm, N // tn, nk),
        in_specs=in_specs,
        out_specs=pl.BlockSpec((tm, tn), lambda i, j, k: (i, j)),
        out_shape=jax.ShapeDtypeStruct((T, N), out_dtype),
        sem=("parallel", "parallel", "arbitrary"), rides=rides,
    )(*args)


def _matmul_tn(a, b, *, tmo, tn, tk, name, a_prologue=_to_bf16, shards=1, rides=()):
    T, M = a.shape
    N = b.shape[1]
    if shards > 1:
        per = (N // shards) // tn
        out_spec = pl.BlockSpec((None, tmo, tn), lambda i, j, k: (j // per, i, j % per))
        out_shape = jax.ShapeDtypeStruct((shards, M, N // shards), F32)
    else:
        out_spec = pl.BlockSpec((tmo, tn), lambda i, j, k: (i, j))
        out_shape = jax.ShapeDtypeStruct((M, N), F32)

    def body(a_ref, b_ref, o_ref):
        k = pl.program_id(2)
        p = _dot_tn(a_prologue(a_ref[...]), b_ref[...].astype(BF16))

        @pl.when(k == 0)
        def _():
            o_ref[...] = p

        @pl.when(k > 0)
        def _():
            o_ref[...] += p

    return _call(
        body, name=name, grid=(M // tmo, N // tn, T // tk),
        in_specs=[pl.BlockSpec((tk, tmo), lambda i, j, k: (k, i)), pl.BlockSpec((tk, tn), lambda i, j, k: (k, j))],
        out_specs=out_spec, out_shape=out_shape,
        sem=("parallel", "parallel", "arbitrary"), rides=rides,
    )(a, b)


def _loss_bwd(h2, tgt, g, *, tm):
    T, D = h2.shape

    def body(h_ref, t_ref, g_ref, dh_ref, dhb_ref, dg_ref, loss_ref):
        @pl.when(pl.program_id(0) == 0)
        def _():
            dg_ref[...] = jnp.zeros_like(dg_ref)
            loss_ref[...] = jnp.zeros_like(loss_ref)
        h = h_ref[...]
        gg = g_ref[...]
        r = lax.rsqrt(jnp.mean(h * h, axis=-1, keepdims=True) + EPS)
        hn = h * r
        err = hn * gg - t_ref[...]
        loss_ref[...] += 0.5 * jnp.sum(jnp.mean(err * err, axis=-1, keepdims=True), axis=0, keepdims=True)
        dy = err * (1.0 / D)
        dg_ref[...] += jnp.sum(dy * hn, axis=0, keepdims=True)
        w = dy * gg
        dh = r * w - h * ((r * r * r) * jnp.mean(w * h, axis=-1, keepdims=True))
        dh_ref[...] = dh
        dhb_ref[...] = dh.astype(BF16)

    tile = pl.BlockSpec((tm, D), lambda i: (i, 0))
    return pl.pallas_call(
        body, name="loss_bwd", grid=(T // tm,),
        in_specs=[tile, tile, pl.BlockSpec((1, D), lambda i: (0, 0))],
        out_specs=[tile, tile, pl.BlockSpec((1, D), lambda i: (0, 0)), pl.BlockSpec((1, 1), lambda i: (0, 0))],
        out_shape=[jax.ShapeDtypeStruct((T, D), F32), jax.ShapeDtypeStruct((T, D), BF16),
                   jax.ShapeDtypeStruct((1, D), F32), jax.ShapeDtypeStruct((1, 1), F32)],
        compiler_params=_params(("arbitrary",)),
    )(h2, tgt, g)


def _rms_bwd_res(dn, h, g, dres, *, tm, name, rides=()):
    T, D = h.shape

    def body(dn_ref, h_ref, g_ref, dres_ref, dh_ref, dhb_ref, dg_ref):
        @pl.when(pl.program_id(0) == 0)
        def _():
            dg_ref[...] = jnp.zeros_like(dg_ref)
        h_ = h_ref[...]
        dn_ = dn_ref[...]
        dh, r = _rms_bwd(dn_, h_, g_ref[...])
        dg_ref[...] += jnp.sum(dn_ * (h_ * r), axis=0, keepdims=True)
        dh = dres_ref[...] + dh
        dh_ref[...] = dh
        dhb_ref[...] = dh.astype(BF16)

    tile = pl.BlockSpec((tm, D), lambda i: (i, 0))
    return _call(
        body, name=name, grid=(T // tm,),
        in_specs=[tile, tile, pl.BlockSpec((1, D), lambda i: (0, 0)), tile],
        out_specs=[tile, tile, pl.BlockSpec((1, D), lambda i: (0, 0))],
        out_shape=[jax.ShapeDtypeStruct((T, D), F32), jax.ShapeDtypeStruct((T, D), BF16),
                   jax.ShapeDtypeStruct((1, D), F32)],
        sem=("arbitrary",), rides=rides,
    )(dn, h, g, dres)


def _rel_distance():
    i = lax.broadcasted_iota(jnp.int32, (CHUNK, 2 * CHUNK), 0)
    j = lax.broadcasted_iota(jnp.int32, (CHUNK, 2 * CHUNK), 1)
    return i + CHUNK - j


def _bias_build(table):
    def body(tab_ref, o_ref):
        rel = _rel_distance()
        ge = [rel >= t for t in BUCKET_THR]
        for h in range(B_HEADS):
            cur = jnp.full((CHUNK, 2 * CHUNK), tab_ref[0, h], F32)
            for b in range(1, N_BUCKETS):
                cur = jnp.where(ge[b - 1], tab_ref[b, h], cur)
            o_ref[h] = cur

    return pl.pallas_call(
        body, name="bias_build",
        in_specs=[pl.BlockSpec(memory_space=pltpu.SMEM)],
        out_specs=pl.BlockSpec(memory_space=pltpu.VMEM),
        out_shape=jax.ShapeDtypeStruct((B_HEADS, CHUNK, 2 * CHUNK), F32),
    )(table)


def _bias_grad(dbias):
    def body(db_ref, o_ref, acc_ref):
        rel = _rel_distance()
        lo = [0] + BUCKET_THR
        hi = BUCKET_THR + [CHUNK]
        for b in range(N_BUCKETS):
            m = (rel >= lo[b]) & (rel < hi[b])
            for h in range(B_HEADS):
                row = b * B_HEADS + h
                acc_ref[row:row + 1, :] = jnp.sum(jnp.where(m, db_ref[h], 0.0), axis=0, keepdims=True)
        o_ref[...] = jnp.sum(acc_ref[...], axis=1, keepdims=True)

    return pl.pallas_call(
        body, name="bias_grad",
        in_specs=[pl.BlockSpec(memory_space=pltpu.VMEM)],
        out_specs=pl.BlockSpec(memory_space=pltpu.VMEM),
        out_shape=jax.ShapeDtypeStruct((N_BUCKETS * B_HEADS, 1), F32),
        scratch_shapes=[pltpu.VMEM((N_BUCKETS * B_HEADS, 2 * CHUNK), F32)],
    )(dbias)


def _causal_mask():
    t = lax.broadcasted_iota(jnp.int32, (CHUNK, CHUNK), 0)
    s = lax.broadcasted_iota(jnp.int32, (CHUNK, CHUNK), 1)
    return s <= t


def _band_mask(n):
    rel = _rel_distance()
    j = lax.broadcasted_iota(jnp.int32, (CHUNK, 2 * CHUNK), 1)
    return (rel >= 0) & (rel < CHUNK) & ((n > 0) | (j >= CHUNK))


def _gate_forward(u, v, lg, lb, wc, bs):
    ug = _gelu(u)
    vg = _gelu(v)
    mu = jnp.mean(vg, axis=-1, keepdims=True)
    xc = vg - mu
    rstd = lax.rsqrt(jnp.mean(xc * xc, axis=-1, keepdims=True) + EPS)
    xhat = xc * rstd
    vl = (xhat * lg + lb).astype(BF16)
    mixed = _dot(wc, vl) + bs
    return ug, xhat, rstd, vl, mixed


def _softmax_scores(qk, bias, mask, sink):
    s = qk * SCALE + bias
    s = jnp.where(mask, s, NEG)
    m = jnp.maximum(jnp.max(s, axis=-1, keepdims=True), sink)
    p = jnp.exp(s - m)
    e_sink = jnp.exp(sink - m)
    inv = 1.0 / (jnp.sum(p, axis=-1, keepdims=True) + e_sink)
    return p * inv, e_sink * inv


PAIRS = Q_PER_KV // 2


def _head(g, pr, e):
    return g * Q_PER_KV + 2 * pr + e


def _stack_pairs(ref, g, col0=0):
    w = 2 * HEAD_DIM
    return jnp.concatenate([ref[:, col0 + (g * PAIRS + pr) * w:col0 + (g * PAIRS + pr + 1) * w] for pr in range(PAIRS)],
                           axis=0)


def _low_lanes():
    return lax.broadcasted_iota(jnp.int32, (2 * CHUNK, 2 * HEAD_DIM), 1) < HEAD_DIM


def _band_operands(kv_prev, kv_cur):
    band = jnp.concatenate([kv_prev, kv_cur], axis=0)
    low = _low_lanes()
    ops = []
    for cat in (band[:, :KV_WIDTH], band[:, KV_WIDTH:]):
        rol = pltpu.roll(cat, HEAD_DIM, 1)
        ops.append([[jnp.where(low if e == 0 else ~low, cat if g == e else rol, 0.0).astype(BF16) for e in range(2)]
                    for g in range(2)])
    return ops


def _mixer_fwd(proj, lg, lb, wsp, bs_col, sinks, bias, ga, gb, rides=()):
    T = proj.shape[0]
    nb = T // CHUNK

    def body(u_ref, v_ref, q_ref, kvc_ref, kvp_ref, lg_ref, lb_ref, w_ref, bs_ref, sink_ref, bias_ref,
             ga_ref, gb_ref, mixed_ref, mixed_t_ref, ab_ref):
        n = pl.program_id(0)
        causal = _causal_mask()
        ssq = jnp.zeros((CHUNK, 1), F32)
        for g in range(A_GROUPS):
            cols = slice(g * CHUNK, (g + 1) * CHUNK)
            wc = jnp.where(causal, w_ref[g], 0.0).astype(BF16)
            ug, _, _, _, mixed = _gate_forward(u_ref[:, cols], v_ref[:, cols], lg_ref[g:g + 1, :], lb_ref[g:g + 1, :],
                                               wc, bs_ref[g])
            a = ug * mixed
            ab_ref[:, cols] = a
            ssq = ssq + jnp.sum(a * a, axis=-1, keepdims=True)
        ra = lax.rsqrt(ssq * (1.0 / A_WIDTH) + EPS)
        mixed_ref[:, :A_WIDTH] = ((ab_ref[:, :A_WIDTH] * ra) * ga_ref[...]).astype(BF16)

        mask = _band_mask(n)
        kops, vops = _band_operands(kvp_ref[...], kvc_ref[...])
        ssq = jnp.zeros((CHUNK, 1), F32)
        for g in range(B_HEADS // Q_PER_KV):
            qst = _stack_pairs(q_ref, g).astype(BF16)
            o_st = jnp.zeros((PAIRS * CHUNK, 2 * HEAD_DIM), F32)
            for e in range(2):
                s_all = _dot_nt(qst, kops[g][e])
                ps = []
                for pr in range(PAIRS):
                    h = _head(g, pr, e)
                    p, _ = _softmax_scores(s_all[pr * CHUNK:(pr + 1) * CHUNK], bias_ref[h], mask, sink_ref[0, h])
                    ps.append(p.astype(BF16))
                o_st = o_st + _dot(jnp.concatenate(ps, axis=0), vops[g][e])
            for pr in range(PAIRS):
                o = o_st[pr * CHUNK:(pr + 1) * CHUNK]
                c0 = A_WIDTH + (g * PAIRS + pr) * 2 * HEAD_DIM
                ab_ref[:, c0:c0 + 2 * HEAD_DIM] = o
                ssq = ssq + jnp.sum(o * o, axis=-1, keepdims=True)
        rb = lax.rsqrt(ssq * (1.0 / B_WIDTH) + EPS)
        mixed_ref[:, A_WIDTH:] = ((ab_ref[:, A_WIDTH:] * rb) * gb_ref[...]).astype(BF16)
        mixed_t_ref[...] = mixed_ref[...].T

    full = lambda *shape: pl.BlockSpec(shape, lambda n: (0,) * len(shape))
    return _call(
        body, name="mixer_fwd", grid=(nb,),
        in_specs=[pl.BlockSpec((CHUNK, A_WIDTH), lambda n: (n, 0)),
                  pl.BlockSpec((CHUNK, A_WIDTH), lambda n: (n, 1)),
                  pl.BlockSpec((CHUNK, B_WIDTH), lambda n: (n, 2)),
                  pl.BlockSpec((CHUNK, 2 * KV_WIDTH), lambda n: (n, 12)),
                  pl.BlockSpec((CHUNK, 2 * KV_WIDTH), lambda n: (jnp.maximum(n - 1, 0), 12)),
                  full(A_GROUPS, CHUNK), full(A_GROUPS, CHUNK), full(A_GROUPS, CHUNK, CHUNK), full(A_GROUPS, CHUNK, 1),
                  pl.BlockSpec(memory_space=pltpu.SMEM), full(B_HEADS, CHUNK, 2 * CHUNK),
                  full(1, A_WIDTH), full(1, B_WIDTH)],
        out_specs=[pl.BlockSpec((CHUNK, D_MODEL), lambda n: (n, 0)), pl.BlockSpec((D_MODEL, CHUNK), lambda n: (0, n)),
                   pl.BlockSpec((CHUNK, D_MODEL), lambda n: (n, 0))],
        out_shape=[jax.ShapeDtypeStruct((T, D_MODEL), BF16), jax.ShapeDtypeStruct((D_MODEL, T), BF16),
                   jax.ShapeDtypeStruct((T, D_MODEL), F32)],
        sem=("parallel",), rides=rides,
    )(proj, proj, proj, proj, proj, lg, lb, wsp, bs_col, sinks, bias, ga, gb)


def _gmlp_bwd(proj, ab, dmixed, ga, lg, lb, wsp, bs_col, rides=()):
    T = proj.shape[0]
    nb = T // CHUNK

    def body(u_ref, v_ref, a_ref, dna_ref, ga_ref, lg_ref, lb_ref, w_ref, bs_ref,
             dp_ref, dpt_ref, dga_ref, dw_ref, dbs_ref, dlg_ref, dlb_ref):
        @pl.when(pl.program_id(0) == 0)
        def _():
            for r in (dga_ref, dw_ref, dbs_ref, dlg_ref, dlb_ref):
                r[...] = jnp.zeros_like(r)
        causal = _causal_mask()
        a_all = a_ref[...]
        dna = dna_ref[...]
        da_all, ra = _rms_bwd(dna, a_all, ga_ref[...])
        dga_ref[...] += jnp.sum(dna * (a_all * ra), axis=0, keepdims=True)
        for g in range(A_GROUPS):
            cols = slice(g * CHUNK, (g + 1) * CHUNK)
            wc = jnp.where(causal, w_ref[g], 0.0).astype(BF16)
            lgg = lg_ref[g:g + 1, :]
            u = u_ref[:, cols]
            v = v_ref[:, cols]
            ug, xhat, rstd, vl, mixed = _gate_forward(u, v, lgg, lb_ref[g:g + 1, :], wc, bs_ref[g])
            da = da_all[:, cols]
            dug = da * mixed
            dmg = da * ug
            dmg_b = dmg.astype(BF16)
            dbs_ref[g] += jnp.sum(dmg, axis=-1, keepdims=True)
            dw_ref[g] += jnp.where(causal, _dot_nt(dmg_b, vl), 0.0)
            dvl = _dot_tn(wc, dmg_b)
            dlg_ref[g:g + 1, :] += jnp.sum(dvl * xhat, axis=0, keepdims=True)
            dlb_ref[g:g + 1, :] += jnp.sum(dvl, axis=0, keepdims=True)
            dxh = dvl * lgg
            dvg = rstd * (dxh - jnp.mean(dxh, axis=-1, keepdims=True)
                          - xhat * jnp.mean(dxh * xhat, axis=-1, keepdims=True))
            _, gu = _gelu_and_grad(u)
            _, gv = _gelu_and_grad(v)
            dp_ref[:, cols] = (dug * gu).astype(BF16)
            dp_ref[:, A_WIDTH + g * CHUNK:A_WIDTH + (g + 1) * CHUNK] = (dvg * gv).astype(BF16)
        dpt_ref[...] = dp_ref[...].T

    full = lambda *shape: pl.BlockSpec(shape, lambda n: (0,) * len(shape))
    return _call(
        body, name="gmlp_bwd", grid=(nb,),
        in_specs=[pl.BlockSpec((CHUNK, A_WIDTH), lambda n: (n, 0)),
                  pl.BlockSpec((CHUNK, A_WIDTH), lambda n: (n, 1)),
                  pl.BlockSpec((CHUNK, A_WIDTH), lambda n: (n, 0)),
                  pl.BlockSpec((CHUNK, A_WIDTH), lambda n: (n, 0)),
                  full(1, A_WIDTH), full(A_GROUPS, CHUNK), full(A_GROUPS, CHUNK), full(A_GROUPS, CHUNK, CHUNK),
                  full(A_GROUPS, CHUNK, 1)],
        out_specs=[pl.BlockSpec((CHUNK, 2 * A_WIDTH), lambda n: (n, 0)), pl.BlockSpec((2 * A_WIDTH, CHUNK), lambda n: (0, n)),
                   full(1, A_WIDTH), full(A_GROUPS, CHUNK, CHUNK), full(A_GROUPS, CHUNK, 1),
                   full(A_GROUPS, CHUNK), full(A_GROUPS, CHUNK)],
        out_shape=[jax.ShapeDtypeStruct((T, 2 * A_WIDTH), BF16), jax.ShapeDtypeStruct((2 * A_WIDTH, T), BF16),
                   jax.ShapeDtypeStruct((1, A_WIDTH), F32), jax.ShapeDtypeStruct((A_GROUPS, CHUNK, CHUNK), F32),
                   jax.ShapeDtypeStruct((A_GROUPS, CHUNK, 1), F32), jax.ShapeDtypeStruct((A_GROUPS, CHUNK), F32),
                   jax.ShapeDtypeStruct((A_GROUPS, CHUNK), F32)],
        sem=("arbitrary",), rides=rides,
    )(proj, proj, ab, dmixed, ga, lg, lb, wsp, bs_col)


def _attn_bwd(proj, ab, dmixed, gb, sinks, bias, rides=()):
    T = proj.shape[0]
    nb = T // CHUNK
    qn = lambda n: jnp.minimum(n, nb - 1)

    def body(q_ref, kvc_ref, kvp_ref, o_ref, dnb_ref, gb_ref, sink_ref, bias_ref,
             dq_ref, dkv_ref, dqt_ref, dkvt_ref, dgb_ref, dsink_ref, dbias_ref, carry_ref, sacc_ref):
        n = pl.program_id(0)

        @pl.when(n == 0)
        def _():
            carry_ref[...] = jnp.zeros_like(carry_ref)
            sacc_ref[...] = jnp.zeros_like(sacc_ref)
            dgb_ref[...] = jnp.zeros_like(dgb_ref)
            dbias_ref[...] = jnp.zeros_like(dbias_ref)

        @pl.when(n < nb)
        def _():
            mask = _band_mask(n)
            o_all = o_ref[...]
            dnb = dnb_ref[...]
            do_all, rb = _rms_bwd(dnb, o_all, gb_ref[...])
            dgb_ref[...] += jnp.sum(dnb * (o_all * rb), axis=0, keepdims=True)
            kops, vops = _band_operands(kvp_ref[...], kvc_ref[...])
            low = _low_lanes()
            halves = []
            for g in range(B_HEADS // Q_PER_KV):
                qst = _stack_pairs(q_ref, g).astype(BF16)
                dost = _stack_pairs(do_all, g).astype(BF16)
                dq_st = jnp.zeros((PAIRS * CHUNK, 2 * HEAD_DIM), F32)
                dk_e, dv_e = [], []
                for e in range(2):
                    s_all = _dot_nt(qst, kops[g][e])
                    dp_all = _dot_nt(dost, vops[g][e])
                    ps, dsrs = [], []
                    for pr in range(PAIRS):
                        h = _head(g, pr, e)
                        rows = slice(pr * CHUNK, (pr + 1) * CHUNK)
                        p, p_sink = _softmax_scores(s_all[rows], bias_ref[h], mask, sink_ref[0, h])
                        dp = dp_all[rows]
                        delta = jnp.sum(p * dp, axis=-1, keepdims=True)
                        ds = p * (dp - delta)
                        sacc_ref[:, h:h + 1] += -(p_sink * delta)
                        dbias_ref[h] += ds
                        ps.append(p.astype(BF16))
                        dsrs.append((ds * SCALE).astype(BF16))
                    dsr_all = jnp.concatenate(dsrs, axis=0)
                    dq_st = dq_st + _dot(dsr_all, kops[g][e])
                    dk_e.append(_dot_tn(dsr_all, qst))
                    dv_e.append(_dot_tn(jnp.concatenate(ps, axis=0), dost))
                for pr in range(PAIRS):
                    c0 = (g * PAIRS + pr) * 2 * HEAD_DIM
                    dq_ref[:, c0:c0 + 2 * HEAD_DIM] = dq_st[pr * CHUNK:(pr + 1) * CHUNK].astype(BF16)
                halves.append((dk_e, dv_e))
            tiles = []
            for t in range(2):
                g0, g1 = halves[0][t], halves[1][t]
                tiles.append(jnp.where(low, g0[0] + pltpu.roll(g0[1], HEAD_DIM, 1), pltpu.roll(g1[0], HEAD_DIM, 1) + g1[1]))
            dband = jnp.concatenate(tiles, axis=1)
            dkv = (carry_ref[...] + dband[:CHUNK]).astype(BF16)
            dkv_ref[...] = dkv
            dkvt_ref[...] = dkv.T
            dqt_ref[...] = dq_ref[...].T
            carry_ref[...] = dband[CHUNK:]

        @pl.when(n == nb)
        def _():
            dkv = carry_ref[...].astype(BF16)
            dkv_ref[...] = dkv
            dkvt_ref[...] = dkv.T
            dsink_ref[...] = jnp.sum(sacc_ref[...], axis=0, keepdims=True)

    full = lambda *shape: pl.BlockSpec(shape, lambda n: (0,) * len(shape))
    return _call(
        body, name="attn_bwd", grid=(nb + 1,),
        in_specs=[pl.BlockSpec((CHUNK, B_WIDTH), lambda n: (qn(n), 2)),
                  pl.BlockSpec((CHUNK, 2 * KV_WIDTH), lambda n: (qn(n), 12)),
                  pl.BlockSpec((CHUNK, 2 * KV_WIDTH), lambda n: (jnp.maximum(qn(n) - 1, 0), 12)),
                  pl.BlockSpec((CHUNK, B_WIDTH), lambda n: (qn(n), 1)),
                  pl.BlockSpec((CHUNK, B_WIDTH), lambda n: (qn(n), 1)),
                  full(1, B_WIDTH), pl.BlockSpec(memory_space=pltpu.SMEM), full(B_HEADS, CHUNK, 2 * CHUNK)],
        out_specs=[pl.BlockSpec((CHUNK, B_WIDTH), lambda n: (qn(n), 0)),
                   pl.BlockSpec((CHUNK, 2 * KV_WIDTH), lambda n: (jnp.maximum(n - 1, 0), 0)),
                   pl.BlockSpec((B_WIDTH, CHUNK), lambda n: (0, qn(n))),
                   pl.BlockSpec((2 * KV_WIDTH, CHUNK), lambda n: (0, jnp.maximum(n - 1, 0))),
                   full(1, B_WIDTH), full(1, B_HEADS), full(B_HEADS, CHUNK, 2 * CHUNK)],
        out_shape=[jax.ShapeDtypeStruct((T, B_WIDTH), BF16), jax.ShapeDtypeStruct((T, 2 * KV_WIDTH), BF16),
                   jax.ShapeDtypeStruct((B_WIDTH, T), BF16), jax.ShapeDtypeStruct((2 * KV_WIDTH, T), BF16),
                   jax.ShapeDtypeStruct((1, B_WIDTH), F32), jax.ShapeDtypeStruct((1, B_HEADS), F32),
                   jax.ShapeDtypeStruct((B_HEADS, CHUNK, 2 * CHUNK), F32)],
        scratch_shapes=[pltpu.VMEM((CHUNK, 2 * KV_WIDTH), F32), pltpu.VMEM((CHUNK, B_HEADS), F32)],
        sem=("arbitrary",), rides=rides,
    )(proj, proj, proj, ab, dmixed, gb, sinks, bias)


def _sq_relu_grad(acc, r):
    return acc * (2.0 * r.astype(F32))


def _local_step(x, tgt, sp, win, wo, wu, wd):
    T = x.shape[0]
    tm = min(512, T)
    tk = min(512, T)
    lg = sp["gate_norm_g"].reshape(A_GROUPS, CHUNK)
    lb = sp["gate_norm_b"].reshape(A_GROUPS, CHUNK)
    wsp = sp["w_spatial"].reshape(A_GROUPS, CHUNK, CHUNK)
    bs_col = sp["b_spatial"].reshape(A_GROUPS, CHUNK, 1)
    sinks = sp["attn_sinks"].reshape(1, B_HEADS)
    ga = sp["out_norm_a_g"].reshape(1, A_WIDTH)
    gb = sp["out_norm_b_g"].reshape(1, B_WIDTH)
    g1 = sp["mix_norm_g"].reshape(1, D_MODEL)
    g2 = sp["ffn_norm_g"].reshape(1, D_MODEL)
    gf = sp["final_norm_g"].reshape(1, D_MODEL)

    bias = _bias_build(sp["rel_bias_table"])
    n1, proj = _norm_matmul(x, g1, win, tm=tm, tn=PROJ_WIDTH // 2, name="in_proj")
    mixed, ab = _mixer_fwd(proj, lg, lb, wsp, bs_col, sinks, bias, ga, gb)
    h1 = _matmul_res(mixed, wo, x, tm=tm, tn=1024, tk=D_MODEL, prologue=_to_bf16, name="out_proj")
    n2, zp = _norm_matmul(h1, g2, wu, tm=tm, tn=1024, name="up_proj")
    h2 = _matmul_res(zp, wd, h1, tm=tm, tn=1024, tk=2048, prologue=_sq_relu_bf16, name="down_proj")

    dh2, dgf, loss = _loss_bwd(h2, tgt, gf, tm=tm)
    dzp = _matmul_nt(dh2, wd, tm=tm, tn=1024, tk=D_MODEL, name="bwd_dz", extra=zp, epilogue=_sq_relu_grad,
                     out_dtype=BF16)
    dwd = _matmul_tn(zp, dh2, tmo=1024, tn=1024, tk=tk, name="grad_w_down", a_prologue=_sq_relu_bf16)
    dwu = _matmul_tn(n2, dzp, tmo=1024, tn=1024, tk=tk, name="grad_w_up", shards=N_CHIPS)
    dn2 = _matmul_nt(dzp, wu, tm=tm, tn=1024, tk=2048, name="bwd_dn2")
    dh1, dg2 = _rms_bwd_res(dn2, h1, g2, dh2, tm=tm, name="ffn_norm_bwd")
    dwo = _matmul_tn(mixed, dh1, tmo=1024, tn=1024, tk=tk, name="grad_w_out")
    dmixed = _matmul_nt(dh1, wo, tm=tm, tn=1024, tk=D_MODEL, name="bwd_dmixed")
    duv, dga, dwsp, dbs, dlg, dlb = _gmlp_bwd(proj, ab, dmixed, ga, lg, lb, wsp, bs_col)
    dq, dkv, dgb, dsinks, dbias = _attn_bwd(proj, ab, dmixed, gb, sinks, bias)
    dtable = _bias_grad(dbias)
    dproj = jnp.concatenate([duv, dq, dkv], axis=1)
    dwin = _matmul_tn(n1, dproj, tmo=1024, tn=PROJ_WIDTH // 2, tk=tk, name="grad_w_in")
    dn1 = _matmul_nt(dproj, win, tm=tm, tn=1024, tk=PROJ_WIDTH, name="bwd_dn1")
    dx, dg1 = _rms_bwd_res(dn1, x, g1, dh1, tm=tm, name="mix_norm_bwd")

    small = {
        "rel_bias_table": dtable.reshape(N_BUCKETS, B_HEADS), "mix_norm_g": dg1, "gate_norm_g": dlg, "gate_norm_b": dlb,
        "w_spatial": dwsp, "b_spatial": dbs, "attn_sinks": dsinks, "out_norm_a_g": dga, "out_norm_b_g": dgb,
        "ffn_norm_g": dg2, "final_norm_g": dgf,
    }
    return loss, dx, (dwin, dwo, dwu, dwd), small


def _place():
    x, y, c = lax.axis_index("x"), lax.axis_index("y"), lax.axis_index("c")
    chips = [(1 - x, y), (x, 1 - y), (1 - x, 1 - y)]
    return x, y, c, chips


def _remote(src, dst, send_sem, recv_sem, to):
    return pltpu.make_async_remote_copy(src_ref=src, dst_ref=dst, send_sem=send_sem, recv_sem=recv_sem,
                                        device_id=to, device_id_type=MESH)


def _core_index():
    return lax.axis_index("c").astype(jnp.int32).reshape(1)


def _chip_index():
    return (2 * lax.axis_index("x") + lax.axis_index("y")).astype(jnp.int32).reshape(1)


def _cast_into_slot(w, *, tm, name):
    _, R, C = w.shape

    def body(me_ref, w_ref, o_ref):
        del me_ref
        o_ref[...] = w_ref[...].astype(BF16)

    return pl.pallas_call(
        body, name=name,
        grid_spec=pltpu.PrefetchScalarGridSpec(
            num_scalar_prefetch=1, grid=(R // tm,),
            in_specs=[pl.BlockSpec((None, tm, C), lambda i, me: (0, i, 0))],
            out_specs=pl.BlockSpec((None, tm, C), lambda i, me: (me[0], i, 0))),
        out_shape=jax.ShapeDtypeStruct((N_CHIPS, R, C), BF16), compiler_params=_params(("parallel",)),
    )(_chip_index(), w)


def _cast_into_slot_carrying(w, *, tm, name, rides):
    _, R, C = w.shape

    def body(w_ref, o_ref):
        o_ref[...] = w_ref[...].astype(BF16)

    return _call(
        body, name=name, grid=(R // tm,),
        in_specs=[pl.BlockSpec((None, tm, C), lambda i: (0, i, 0))],
        out_specs=pl.BlockSpec((None, tm, C), lambda i: (2 * lax.axis_index("x") + lax.axis_index("y"), i, 0)),
        out_shape=jax.ShapeDtypeStruct((N_CHIPS, R, C), BF16), sem=("arbitrary",), rides=rides,
    )(w)


def _gather_weights(slots):
    nw = len(slots)

    def body(*refs):
        fulls = refs[nw:2 * nw]
        send_sems, recv_sems = refs[2 * nw:]
        x, y, c, chips = _place()
        me = 2 * x + y
        sends = []
        for w in range(nw):
            hr = fulls[w].shape[1] // 2
            rows = pl.ds(c * hr, hr)
            for j, chip in enumerate(chips):
                mine = fulls[w].at[me, rows, :]
                cp = _remote(mine, mine, send_sems.at[6 * w + j], recv_sems.at[6 * w + j], (*chip, c))
                cp.start()
                sends.append(cp)
        for w in range(nw):
            hr = fulls[w].shape[1] // 2
            rows = pl.ds(c * hr, hr)
            for j, chip in enumerate(chips):
                landed = fulls[w].at[2 * chip[0] + chip[1], rows, :]
                _remote(landed, landed, send_sems.at[6 * w + j], recv_sems.at[6 * w + j], (x, y, c)).wait_recv()
                cp = _remote(landed, landed, send_sems.at[6 * w + 3 + j], recv_sems.at[6 * w + 3 + j], (x, y, 1 - c))
                cp.start()
                sends.append(cp)
        for w in range(nw):
            hr = fulls[w].shape[1] // 2
            rows = pl.ds((1 - c) * hr, hr)
            for j, chip in enumerate(chips):
                other = fulls[w].at[2 * chip[0] + chip[1], rows, :]
                _remote(other, other, send_sems.at[6 * w + 3 + j], recv_sems.at[6 * w + 3 + j], (x, y, c)).wait_recv()
        for cp in sends:
            cp.wait_send()

    any_spec = pl.BlockSpec(memory_space=pl.ANY)
    return pl.pallas_call(
        body, name="gather_weights",
        in_specs=[any_spec] * nw, out_specs=[any_spec] * nw,
        out_shape=[jax.ShapeDtypeStruct(s.shape, s.dtype) for s in slots],
        scratch_shapes=[pltpu.SemaphoreType.DMA((6 * nw,)), pltpu.SemaphoreType.DMA((6 * nw,))],
        input_output_aliases={w: w for w in range(nw)},
    )(*slots)


def _sibling_halves(grads):
    nw = len(grads)

    def body(*refs):
        gs, outs = refs[:nw], refs[nw:2 * nw]
        send_sems, recv_sems = refs[2 * nw:]
        x, y, c, _ = _place()
        cps = []
        for w in range(nw):
            hr = gs[w].shape[1] // 2
            cp = _remote(gs[w].at[:, pl.ds((1 - c) * hr, hr), :], outs[w], send_sems.at[w], recv_sems.at[w],
                         (x, y, 1 - c))
            cp.start()
            cps.append(cp)
        for cp in cps:
            cp.wait()

    any_spec = pl.BlockSpec(memory_space=pl.ANY)
    return pl.pallas_call(
        body, name="rs_sibling_halves",
        in_specs=[any_spec] * nw, out_specs=[any_spec] * nw,
        out_shape=[jax.ShapeDtypeStruct((g.shape[0], g.shape[1] // 2, g.shape[2]), g.dtype) for g in grads],
        scratch_shapes=[pltpu.SemaphoreType.DMA((nw,)), pltpu.SemaphoreType.DMA((nw,))],
    )(*grads)


def _pair_sum_bf16(g, got, *, tm, name):
    S, R, C = g.shape
    hr = R // 2
    nt = hr // tm

    def body(c_ref, g_ref, got_ref, o_ref):
        del c_ref
        o_ref[...] = (g_ref[...] + got_ref[...]).astype(BF16)

    return pl.pallas_call(
        body, name=name,
        grid_spec=pltpu.PrefetchScalarGridSpec(
            num_scalar_prefetch=1, grid=(S, nt),
            in_specs=[pl.BlockSpec((None, tm, C), lambda s, i, c: (s, c[0] * nt + i, 0)),
                      pl.BlockSpec((None, tm, C), lambda s, i, c: (s, i, 0))],
            out_specs=pl.BlockSpec((None, tm, C), lambda s, i, c: (s, i, 0))),
        out_shape=jax.ShapeDtypeStruct((S, hr, C), BF16),
        compiler_params=_params(("parallel", "parallel")),
    )(_core_index(), g, got)


def _scatter_to_owners(pairs):
    nw = len(pairs)

    def body(*refs):
        qs, outs = refs[:nw], refs[nw:2 * nw]
        send_sems, recv_sems = refs[2 * nw:]
        x, y, c, chips = _place()
        cps = []
        for w in range(nw):
            for j, chip in enumerate(chips):
                cp = _remote(qs[w].at[2 * chip[0] + chip[1]], outs[w].at[j], send_sems.at[3 * w + j],
                             recv_sems.at[3 * w + j], (*chip, c))
                cp.start()
                cps.append(cp)
        for cp in cps:
            cp.wait()

    any_spec = pl.BlockSpec(memory_space=pl.ANY)
    return pl.pallas_call(
        body, name="rs_scatter_to_owners",
        in_specs=[any_spec] * nw, out_specs=[any_spec] * nw,
        out_shape=[jax.ShapeDtypeStruct((3,) + q.shape[1:], q.dtype) for q in pairs],
        scratch_shapes=[pltpu.SemaphoreType.DMA((3 * nw,)), pltpu.SemaphoreType.DMA((3 * nw,))],
    )(*pairs)


def _owner_total(gh, others, *, tm, name):
    _, hr, C = gh.shape

    def body(me_ref, g_ref, o_ref_in, out_ref):
        del me_ref
        acc = g_ref[...]
        for j in range(3):
            acc = acc + o_ref_in[j].astype(F32)
        out_ref[...] = acc

    return pl.pallas_call(
        body, name=name,
        grid_spec=pltpu.PrefetchScalarGridSpec(
            num_scalar_prefetch=1, grid=(hr // tm,),
            in_specs=[pl.BlockSpec((None, tm, C), lambda i, me: (me[0], i, 0)),
                      pl.BlockSpec((3, tm, C), lambda i, me: (0, i, 0))],
            out_specs=pl.BlockSpec((tm, C), lambda i, me: (i, 0))),
        out_shape=jax.ShapeDtypeStruct((hr, C), F32),
        compiler_params=_params(("parallel",)),
    )(_chip_index(), gh, others)


def _owner_sum(g, got, others, *, tm, name):
    S, R, C = g.shape
    hr = R // 2
    nt = hr // tm

    def body(idx_ref, g_ref, got_ref, o_ref_in, out_ref):
        del idx_ref
        acc = g_ref[...] + got_ref[...]
        for j in range(3):
            acc = acc + o_ref_in[j].astype(F32)
        out_ref[...] = acc

    return pl.pallas_call(
        body, name=name,
        grid_spec=pltpu.PrefetchScalarGridSpec(
            num_scalar_prefetch=1, grid=(nt,),
            in_specs=[pl.BlockSpec((None, tm, C), lambda i, p: (p[1], p[0] * nt + i, 0)),
                      pl.BlockSpec((None, tm, C), lambda i, p: (p[1], i, 0)),
                      pl.BlockSpec((3, tm, C), lambda i, p: (0, i, 0))],
            out_specs=pl.BlockSpec((tm, C), lambda i, p: (i, 0))),
        out_shape=jax.ShapeDtypeStruct((hr, C), F32),
        compiler_params=_params(("parallel",)),
    )(jnp.concatenate([_core_index(), _chip_index()]), g, got, others)


def _swap_halves(halves):
    nw = len(halves)

    def body(*refs):
        hs, outs = refs[:nw], refs[nw:2 * nw]
        send_sems, recv_sems = refs[2 * nw:]
        x, y, c, _ = _place()
        cps = []
        for w in range(nw):
            cp = _remote(hs[w], outs[w], send_sems.at[w], recv_sems.at[w], (x, y, 1 - c))
            cp.start()
            cps.append(cp)
        for cp in cps:
            cp.wait()

    any_spec = pl.BlockSpec(memory_space=pl.ANY)
    return pl.pallas_call(
        body, name="rs_swap_halves",
        in_specs=[any_spec] * nw, out_specs=[any_spec] * nw,
        out_shape=[jax.ShapeDtypeStruct(h.shape, h.dtype) for h in halves],
        scratch_shapes=[pltpu.SemaphoreType.DMA((nw,)), pltpu.SemaphoreType.DMA((nw,))],
    )(*halves)


def _all_reduce_small(packed):
    R, C = packed.shape

    def body(in_ref, out_ref, slots, send_sems, recv_sems):
        x, y, c, _ = _place()
        me = 4 * x + 2 * y + c
        cps = []
        for k in range(1, N_DEV):
            p = (me + k) % N_DEV
            cp = _remote(in_ref, slots.at[me], send_sems.at[k - 1], recv_sems.at[k - 1], (p // 4, (p // 2) % 2, p % 2))
            cp.start()
            cps.append(cp)
        slots[me] = in_ref[...]
        for k in range(1, N_DEV):
            src = (me + N_DEV - k) % N_DEV
            _remote(in_ref, slots.at[src], send_sems.at[k - 1], recv_sems.at[k - 1], (x, y, c)).wait_recv()
        for cp in cps:
            cp.wait_send()
        acc = slots[0]
        for d in range(1, N_DEV):
            acc = acc + slots[d]
        out_ref[...] = acc

    vmem = pl.BlockSpec(memory_space=pltpu.VMEM)
    return pl.pallas_call(
        body, name="all_reduce_small", in_specs=[vmem], out_specs=vmem,
        out_shape=jax.ShapeDtypeStruct((R, C), F32),
        scratch_shapes=[pltpu.VMEM((N_DEV, R, C), F32), pltpu.SemaphoreType.DMA((N_DEV - 1,)),
                        pltpu.SemaphoreType.DMA((N_DEV - 1,))],
        compiler_params=_params(),
    )(packed)


def _adamw_math(w, g, m, v):
    m = ADAM_B1 * m + (1.0 - ADAM_B1) * g
    v = ADAM_B2 * v + (1.0 - ADAM_B2) * (g * g)
    m_hat = m / (1.0 - ADAM_B1 ** ADAM_STEP)
    v_hat = v / (1.0 - ADAM_B2 ** ADAM_STEP)
    delta = -ADAM_LR * (m_hat / (jnp.sqrt(v_hat) + ADAM_EPS) + ADAM_WD * w)
    return delta, m, v


def _adamw(w, g, m, v, *, tm, name):
    R, C = w.shape

    def body(w_ref, g_ref, m_ref, v_ref, d_ref, nm_ref, nv_ref):
        d_ref[...], nm_ref[...], nv_ref[...] = _adamw_math(w_ref[...], g_ref[...], m_ref[...], v_ref[...])

    spec = pl.BlockSpec((tm, C), lambda i: (i, 0))
    return pl.pallas_call(
        body, name=name, grid=(R // tm,), in_specs=[spec] * 4, out_specs=[spec] * 3,
        out_shape=[jax.ShapeDtypeStruct((R, C), F32)] * 3, compiler_params=_params(("parallel",)),
    )(w, g, m, v)


def _adamw_halves(w, own, got, m, v, *, tm, name, rides=()):
    _, R, C = w.shape
    nt = (R // 2) // tm

    def body(w_ref, own_ref, got_ref, m_ref, v_ref, g_ref, d_ref, nm_ref, nv_ref):
        g = jnp.where(pl.program_id(0) == lax.axis_index("c"), own_ref[...], got_ref[...])
        g_ref[...] = g
        d_ref[...], nm_ref[...], nv_ref[...] = _adamw_math(w_ref[...], g, m_ref[...], v_ref[...])

    whole = pl.BlockSpec((None, tm, C), lambda h, i: (0, h * nt + i, 0))
    half = pl.BlockSpec((tm, C), lambda h, i: (i, 0))
    return _call(
        body, name=name, grid=(2, nt), in_specs=[whole, half, half, whole, whole], out_specs=[whole] * 4,
        out_shape=[jax.ShapeDtypeStruct((1, R, C), F32)] * 4, sem=("parallel", "parallel"), rides=rides,
    )(w, own, got, m, v)


def _adamw_small(w, slots, m, v, *, name):
    def body(w_ref, slots_ref, m_ref, v_ref, g_ref, d_ref, nm_ref, nv_ref):
        g = slots_ref[0]
        for d in range(1, N_DEV):
            g = g + slots_ref[d]
        g_ref[...] = g
        d_ref[...], nm_ref[...], nv_ref[...] = _adamw_math(w_ref[...], g, m_ref[...], v_ref[...])

    vmem = pl.BlockSpec(memory_space=pltpu.VMEM)
    return pl.pallas_call(
        body, name=name, in_specs=[vmem] * 4, out_specs=[vmem] * 4,
        out_shape=[jax.ShapeDtypeStruct(w.shape, F32)] * 4, compiler_params=_params(),
    )(w, slots, m, v)


SMALL = ["rel_bias_table", "mix_norm_g", "gate_norm_g", "gate_norm_b", "w_spatial", "b_spatial", "attn_sinks",
         "out_norm_a_g", "out_norm_b_g", "ffn_norm_g", "final_norm_g"]
SMALL_A = ["gate_norm_g", "gate_norm_b", "w_spatial", "b_spatial", "out_norm_a_g"]
SMALL_B = ["rel_bias_table", "mix_norm_g", "attn_sinks", "out_norm_b_g", "ffn_norm_g", "final_norm_g"]
LARGE = ["w_in", "w_out", "w_up", "w_down"]
ROW_TILE = {"w_in": 208, "w_out": 256, "w_up": 256, "w_down": 256}
WEIGHTS = ["rel_bias_table", "mix_norm_g", "w_in", "gate_norm_g", "gate_norm_b", "w_spatial", "b_spatial", "attn_sinks",
           "out_norm_a_g", "out_norm_b_g", "w_out", "ffn_norm_g", "w_up", "w_down", "final_norm_g"]
PACK_UNIT = 8 * 128


def _pack(parts):
    rows = []
    for p in parts:
        flat = p.reshape(-1)
        pad = (-flat.shape[0]) % PACK_UNIT
        rows.append(jnp.pad(flat, (0, pad)).reshape(-1, 128))
    return jnp.concatenate(rows, axis=0)


def _unpack(packed, like):
    out, row = [], 0
    for p in like:
        n = math.prod(p.shape)
        nrows = (n + PACK_UNIT - 1) // PACK_UNIT * 8
        out.append(packed[row:row + nrows].reshape(-1)[:n].reshape(p.shape))
        row += nrows
    return out


def kernel(x, rel_bias_table, mix_norm_g, w_in, gate_norm_g, gate_norm_b, w_spatial, b_spatial, attn_sinks, out_norm_a_g, out_norm_b_g, w_out, ffn_norm_g, w_up, w_down, final_norm_g, loss_target, m_rel_bias_table, m_mix_norm_g, m_w_in, m_gate_norm_g, m_gate_norm_b, m_w_spatial, m_b_spatial, m_attn_sinks, m_out_norm_a_g, m_out_norm_b_g, m_w_out, m_ffn_norm_g, m_w_up, m_w_down, m_final_norm_g, v_rel_bias_table, v_mix_norm_g, v_w_in, v_gate_norm_g, v_gate_norm_b, v_w_spatial, v_b_spatial, v_attn_sinks, v_out_norm_a_g, v_out_norm_b_g, v_w_out, v_ffn_norm_g, v_w_up, v_w_down, v_final_norm_g):
    args = dict(locals())
    wts = {n: args[n] for n in WEIGHTS}
    mom = {n: args["m_" + n] for n in WEIGHTS}
    var = {n: args["v_" + n] for n in WEIGHTS}
    sp = {n: wts[n] for n in SMALL}
    x2, tgt = x[0], loss_target[0]
    T = x2.shape[0]
    tm = min(512, T)
    tl = min(1024, T)
    tg = min(2048, T)
    lg = sp["gate_norm_g"].reshape(A_GROUPS, CHUNK)
    lb = sp["gate_norm_b"].reshape(A_GROUPS, CHUNK)
    wsp = sp["w_spatial"].reshape(A_GROUPS, CHUNK, CHUNK)
    bs_col = sp["b_spatial"].reshape(A_GROUPS, CHUNK, 1)
    sinks = sp["attn_sinks"].reshape(1, B_HEADS)
    ga = sp["out_norm_a_g"].reshape(1, A_WIDTH)
    gb = sp["out_norm_b_g"].reshape(1, B_WIDTH)
    g1 = sp["mix_norm_g"].reshape(1, D_MODEL)
    g2 = sp["ffn_norm_g"].reshape(1, D_MODEL)
    gf = sp["final_norm_g"].reshape(1, D_MODEL)

    def owner_total(n, gh, others):
        return _owner_total(gh, others, tm=ROW_TILE[n], name="rs_owner_total_" + n)

    def halves_view(at, shards):
        return at.reshape(shards, 2, at.shape[0] // shards // 2, at.shape[1])

    for d in (wts, mom, var):
        d["w_in"] = jnp.swapaxes(d["w_in"], 1, 2)

    s_in = _cast_into_slot(wts["w_in"], tm=ROW_TILE["w_in"], name="cast_w_in")
    s_up, ((s_in,),) = _cast_into_slot_carrying(wts["w_up"], tm=256, name="cast_w_up",
                                                rides=[_ride_gather(s_in, ici=(0, 1, 2))])
    s_down, ((s_in,),) = _cast_into_slot_carrying(wts["w_down"], tm=256, name="cast_w_down",
                                                  rides=[_ride_gather(s_in, ici=(1, 2, 2), d2d=(0, 1, 2))])
    s_out, ((g_in,),) = _cast_into_slot_carrying(wts["w_out"], tm=256, name="cast_w_out",
                                                 rides=[_ride_gather(s_in, d2d=(1, 2, 2))])
    win_t = g_in.reshape(PROJ_WIDTH, D_MODEL)
    bias = _bias_build(sp["rel_bias_table"])
    (n1, proj), ((s_out,), (s_up,)) = _norm_matmul_wide(
        x2, g1, win_t, tm=tm, tn=PROJ_WIDTH // 2, name="in_proj",
        rides=[_ride_gather(s_out, ici=(0, 1, 1)), _ride_gather(s_up, ici=(0, 2, 8))])
    (mixed, mixed_t, ab), ((g_out,), (s_up,), (n1_sib,)) = _mixer_fwd(
        proj, lg, lb, wsp, bs_col, sinks, bias, ga, gb,
        rides=[_ride_gather(s_out, d2d=(0, 1, 1)), _ride_gather(s_up, d2d=(0, 2, 8), ici=(2, 8, 8)),
               _ride_to_sibling(n1, first=True)])
    wo = g_out.reshape(A_WIDTH + B_WIDTH, D_MODEL)
    mixed_t = halves_view(mixed_t, N_CHIPS)
    h1, ((wu,), (s_down,), (mixed_t_sib,)) = _matmul_res(
        mixed, wo, x2, tm=tl, tn=1024, tk=D_MODEL, prologue=_to_bf16, name="out_proj",
        rides=[_ride_gather(s_up, d2d=(2, 8, 8)), _ride_gather(s_down, ici=(0, 2, 8)), _ride_to_sibling(mixed_t, halves=True)])
    (n2t, zp, z2, z2t), ((g_down,),) = _norm_matmul_sq(
        h1, g2, wu, tm=tl, tn=1024, name="up_proj", rides=[_ride_gather(s_down, d2d=(0, 2, 8), both=(2, 8, 8), mid_frac=0.75)])
    wd = g_down.reshape(D_FF, D_MODEL)
    n2t, z2t = halves_view(n2t, 1), halves_view(z2t, N_CHIPS)
    h2, ((n2t_sib,), (z2t_sib,)) = _matmul_res(
        z2, wd, h1, tm=tl, tn=1024, tk=2048, prologue=_to_bf16, name="down_proj",
        rides=[_ride_to_sibling(n2t, halves=True), _ride_to_sibling(z2t, halves=True)])

    dh2, dh2b, dgf, loss = _loss_bwd(h2, tgt, gf, tm=tm)
    dzp, ((dh2b_sib,),) = _matmul_nt(dh2b, wd, tm=tl, tn=1024, tk=D_MODEL, name="bwd_dz", extra=zp,
                                     epilogue=_sq_relu_grad, out_dtype=BF16, rides=[_ride_to_sibling(dh2b)])
    (gd, gdb), ((dzp_sib,),) = _grad_pair(z2t, z2t_sib, dh2b, dh2b_sib, cols_sharded=False, tmo=1024, tk=tl,
                                          name="grad_w_down", rides=[_ride_to_sibling(dzp)])
    (gu, gub), ((o_d,),) = _grad_pair(n2t, n2t_sib, dzp, dzp_sib, cols_sharded=True, tmo=1024, tk=tl,
                                      name="grad_w_up", rides=[_ride_scatter(gdb, None, (0, 7, 8))])
    dn2, ((o_d,), (o_u,)) = _matmul_nt(dzp, wu, tm=tl, tn=1024, tk=2048, name="bwd_dn2",
                                       rides=[_ride_scatter(gdb, o_d, (7, 8, 8)), _ride_scatter(gub, None, (0, 6, 8))])
    h_d = owner_total("w_down", gd, o_d)
    (dh1, dh1b, dg2), ((o_u,),) = _rms_bwd_res(dn2, h1, g2, dh2, tm=tm, name="ffn_norm_bwd",
                                               rides=[_ride_scatter(gub, o_u, (6, 8, 8))])
    h_u = owner_total("w_up", gu, o_u)
    dmixed, ((dh1b_sib,), (w_d,)) = _matmul_nt(dh1b, wo, tm=tl, tn=1024, tk=D_MODEL, name="bwd_dmixed",
                                               rides=[_ride_to_sibling(dh1b), _ride_swap(h_d)])
    (go, gob), ((w_u,),) = _grad_pair(mixed_t, mixed_t_sib, dh1b, dh1b_sib, cols_sharded=False, tmo=256, tk=tl,
                                      name="grad_w_out", rides=[_ride_swap(h_u)])
    (duv, duv_t, dga, dwsp, dbs, dlg, dlb), ((o_o,),) = _gmlp_bwd(proj, ab, dmixed, ga, lg, lb, wsp, bs_col,
                                                                  rides=[_ride_scatter(gob)])
    h_o = owner_total("w_out", go, o_o)
    small = {"gate_norm_g": dlg, "gate_norm_b": dlb, "w_spatial": dwsp, "b_spatial": dbs, "out_norm_a_g": dga}
    (dq, dkv, dq_t, dkv_t, dgb, dsinks, dbias), ((w_o,),) = _attn_bwd(proj, ab, dmixed, gb, sinks, bias,
                                                                      rides=[_ride_swap(h_o)])
    dtable = _bias_grad(dbias)
    dproj = jnp.concatenate([duv, dq, dkv], axis=1)
    dproj_t = halves_view(jnp.concatenate([duv_t, dq_t, dkv_t], axis=0), N_CHIPS)
    ((dproj_t_sib,),) = _carrier([_ride_to_sibling(dproj_t, halves=True)], name="trade_dproj_t")
    (gi, gib), ((slots_a,),) = _grad_pair(
        dproj_t, dproj_t_sib, n1, n1_sib, cols_sharded=False, tmo=PROJ_WIDTH // N_CHIPS // 2, tk=tl, name="grad_w_in",
        rides=[_ride_small_to_all(_pack([small[n] for n in SMALL_A]))])
    dn1, ((o_i,),) = _matmul_nn(dproj, win_t, tmo=tl, tn=1024, tk=PROJ_WIDTH, name="bwd_dn1", rides=[_ride_scatter(gib)])
    h_i = owner_total("w_in", gi, o_i)
    dx, _, dg1 = _rms_bwd_res(dn1, x2, g1, dh1, tm=tm, name="mix_norm_bwd")
    small.update({"rel_bias_table": dtable.reshape(N_BUCKETS, B_HEADS), "mix_norm_g": dg1, "attn_sinks": dsinks,
                  "out_norm_b_g": dgb, "ffn_norm_g": dg2, "final_norm_g": dgf})
    (w_i,), (slots_b,) = _carrier([_ride_swap(h_i), _ride_small_to_all(_pack([small[n] for n in SMALL_B] + [loss]))],
                                  name="swap_w_in")

    out_g, out_d, out_m, out_v = {}, {}, {}, {}
    for n, h, s in zip(LARGE, [h_i, h_o, h_u, h_d], [w_i, w_o, w_u, w_d]):
        res = _adamw_halves(wts[n], h, s, mom[n], var[n], tm=ROW_TILE[n], name="adamw_" + n)
        if n == "w_in":
            res = [jnp.swapaxes(r, 1, 2) for r in res]
        out_g[n], out_d[n], out_m[n], out_v[n] = res
    for names, slots, tag in ((SMALL_A, slots_a, "a"), (SMALL_B, slots_b, "b")):
        extra = [jnp.zeros((1, 1), F32)] if tag == "b" else []
        like = [wts[n] for n in names] + extra
        res = _adamw_small(_pack(like), slots, _pack([mom[n] for n in names] + extra),
                           _pack([var[n] for n in names] + extra), name="adamw_small_" + tag)
        for store, packed in zip((out_g, out_d, out_m, out_v), res):
            for n, val in zip(names + ["loss"], _unpack(packed, like)):
                store[n] = val

    total = out_g["loss"][0, 0]
    return (total, dx[None], *[out_g[n] for n in WEIGHTS], *[out_d[n] for n in WEIGHTS],
            *[out_m[n] for n in WEIGHTS], *[out_v[n] for n in WEIGHTS])
```

```python
import functools
import math

import numpy as np
import jax
import jax.numpy as jnp
from jax import lax
from jax.experimental import pallas as pl
from jax.experimental.pallas import tpu as pltpu

F32 = jnp.float32
BF16 = jnp.bfloat16

D_MODEL = 2048
CHUNK = 128
A_GROUPS = 8
A_WIDTH = 1024
HEAD_DIM = 64
B_HEADS = 16
Q_PER_KV = 8
B_WIDTH = 1024
KV_WIDTH = 128
PROJ_WIDTH = 3328
D_FF = 8192
N_BUCKETS = 32
EPS = 1e-5
NEG = -1e30
SCALE = HEAD_DIM ** -0.5
N_CHIPS = 4
N_DEV = 8

ADAM_LR = 0.001
ADAM_B1 = 0.9
ADAM_B2 = 0.999
ADAM_EPS = 1e-08
ADAM_WD = 0.01
ADAM_STEP = 10

VMEM_LIMIT = 60 * 1024 * 1024
MESH = pl.DeviceIdType.MESH


def _bucket_thresholds():
    d = np.arange(CHUNK)
    n_exact = N_BUCKETS // 2
    relf = np.maximum(d, n_exact).astype(np.float64)
    large = n_exact + (np.log(relf / n_exact) / math.log(CHUNK / n_exact) * (N_BUCKETS - n_exact)).astype(np.int32)
    bucket = np.where(d < n_exact, d, np.minimum(large, N_BUCKETS - 1))
    return [int(np.min(d[bucket >= b])) for b in range(1, N_BUCKETS)]


BUCKET_THR = _bucket_thresholds()


def _params(sem=None):
    return pltpu.CompilerParams(dimension_semantics=sem, vmem_limit_bytes=VMEM_LIMIT)


def _gelu(x):
    c = math.sqrt(2.0 / math.pi)
    return 0.5 * x * (1.0 + jnp.tanh(c * (x + 0.044715 * (x * x * x))))


def _gelu_and_grad(x):
    c = math.sqrt(2.0 / math.pi)
    x2 = x * x
    t = jnp.tanh(c * (x + 0.044715 * (x2 * x)))
    g = 0.5 * x * (1.0 + t)
    dg = 0.5 * (1.0 + t) + 0.5 * x * (1.0 - t * t) * (c * (1.0 + 3.0 * 0.044715 * x2))
    return g, dg


def _dot(a, b):
    return jnp.dot(a, b, preferred_element_type=F32)


def _dot_nt(a, b):
    return lax.dot_general(a, b, (((1,), (1,)), ((), ())), preferred_element_type=F32)


def _dot_tn(a, b):
    return lax.dot_general(a, b, (((0,), (0,)), ((), ())), preferred_element_type=F32)


def _rms_bwd(dn, h, g):
    r = lax.rsqrt(jnp.mean(h * h, axis=-1, keepdims=True) + EPS)
    w = dn * g
    dh = r * w - h * ((r * r * r) * jnp.mean(w * h, axis=-1, keepdims=True))
    return dh, r


def _place():
    x, y, c = lax.axis_index("x"), lax.axis_index("y"), lax.axis_index("c")
    chips = [(1 - x, y), (x, 1 - y), (1 - x, 1 - y)]
    return x, y, c, chips


def _remote(src, dst, send_sem, recv_sem, to):
    return pltpu.make_async_remote_copy(src_ref=src, dst_ref=dst, send_sem=send_sem, recv_sem=recv_sem,
                                        device_id=to, device_id_type=MESH)


class _Ride:
    def __init__(self, args, out_shape, n_sem, start, finish, mid=None, mid_frac=0.8, aliases=None):
        self.args, self.out_shape, self.n_sem = list(args), list(out_shape), n_sem
        self.start, self.mid, self.finish, self.mid_frac = start, mid, finish, mid_frac
        self.aliases = dict(aliases or {})


def _call(body, *, name, grid, in_specs, out_specs, out_shape, scratch_shapes=(), sem=None, rides=()):
    single = not isinstance(out_shape, (list, tuple))
    out_specs = [out_specs] if single else list(out_specs)
    out_shape = [out_shape] if single else list(out_shape)
    n_in, n_out, n_scr = len(in_specs), len(out_shape), len(scratch_shapes)
    r_in = [len(r.args) for r in rides]
    r_out = [len(r.out_shape) for r in rides]
    any_spec = pl.BlockSpec(memory_space=pl.ANY)
    aliases, off_i, off_o = {}, n_in, n_out
    for r in rides:
        for i, o in r.aliases.items():
            aliases[off_i + i] = off_o + o
        off_i += len(r.args)
        off_o += len(r.out_shape)
    steps = math.prod(grid)

    def wrapped(*refs):
        p = 0
        ins = refs[p:p + n_in]; p += n_in
        rins = refs[p:p + sum(r_in)]; p += sum(r_in)
        outs = refs[p:p + n_out]; p += n_out
        routs = refs[p:p + sum(r_out)]; p += sum(r_out)
        scr = refs[p:p + n_scr]; p += n_scr
        sems = refs[p:]
        parts, pi, po = [], 0, 0
        for k, r in enumerate(rides):
            parts.append((rins[pi:pi + r_in[k]], routs[po:po + r_out[k]], sems[2 * k], sems[2 * k + 1]))
            pi += r_in[k]
            po += r_out[k]
        lin = 0
        for d in range(len(grid)):
            lin = lin * grid[d] + pl.program_id(d)
        if rides:
            @pl.when(lin == 0)
            def _():
                for r, part in zip(rides, parts):
                    r.start(*part)
        body(*ins, *outs, *scr)
        for r, part in zip(rides, parts):
            if r.mid is not None:
                @pl.when(lin == min(steps - 1, int(r.mid_frac * steps)))
                def _(r=r, part=part):
                    r.mid(*part)
        if rides:
            @pl.when(lin == steps - 1)
            def _():
                for r, part in zip(rides, parts):
                    r.finish(*part)

    scratch = list(scratch_shapes)
    for r in rides:
        scratch += [pltpu.SemaphoreType.DMA((r.n_sem,)), pltpu.SemaphoreType.DMA((r.n_sem,))]
    if rides:
        sem = ("arbitrary",) * len(grid)
    res = pl.pallas_call(
        wrapped, name=name, grid=grid,
        in_specs=list(in_specs) + [any_spec] * sum(r_in),
        out_specs=out_specs + [any_spec] * sum(r_out),
        out_shape=out_shape + [s for r in rides for s in r.out_shape],
        scratch_shapes=scratch, input_output_aliases=aliases,
        compiler_params=_params(sem),
    )

    def run(*args):
        got = res(*args, *[a for r in rides for a in r.args])
        mine = got[0] if single else list(got[:n_out])
        if not rides:
            return mine
        rest, out = list(got[n_out:]), []
        for k in range(len(rides)):
            out.append(rest[:r_out[k]])
            rest = rest[r_out[k]:]
        return mine, out

    return run


def _ride_gather(slot, ici=None, d2d=None, both=None, mid_frac=0.8):
    half = slot.shape[1] // 2

    def rows(part, c):
        k0, k1, n = part
        return pl.ds(c * half + k0 * (half // n), (k1 - k0) * (half // n))

    def ici_copies(outs, ss, rs, part, base):
        x, y, c, chips = _place()
        mine = outs[0].at[2 * x + y, rows(part, c), :]
        return [_remote(mine, mine, ss.at[base + j], rs.at[base + j], (*chip, c)) for j, chip in enumerate(chips)]

    def d2d_copies(outs, ss, rs, part, base):
        x, y, c, chips = _place()
        return [_remote(outs[0].at[2 * chip[0] + chip[1], rows(part, c), :], outs[0].at[2 * chip[0] + chip[1], rows(part, c), :],
                        ss.at[base + j], rs.at[base + j], (x, y, 1 - c)) for j, chip in enumerate(chips)]

    def arrivals(outs, ss, rs, part, base, from_sibling):
        x, y, c, chips = _place()
        for j, chip in enumerate(chips):
            dst = outs[0].at[2 * chip[0] + chip[1], rows(part, 1 - c if from_sibling else c), :]
            _remote(dst, dst, ss.at[base + j], rs.at[base + j], (x, y, c)).wait_recv()

    def start(ins, outs, ss, rs):
        for part, base in ((ici, 0), (both, 6)):
            if part is not None:
                for cp in ici_copies(outs, ss, rs, part, base):
                    cp.start()
        if d2d is not None:
            for cp in d2d_copies(outs, ss, rs, d2d, 3):
                cp.start()

    def mid(ins, outs, ss, rs):
        arrivals(outs, ss, rs, both, 6, False)
        for cp in d2d_copies(outs, ss, rs, both, 9):
            cp.start()

    def finish(ins, outs, ss, rs):
        if ici is not None:
            arrivals(outs, ss, rs, ici, 0, False)
        if d2d is not None:
            arrivals(outs, ss, rs, d2d, 3, True)
        if both is not None:
            arrivals(outs, ss, rs, both, 9, True)
        for part, base, fn in ((ici, 0, ici_copies), (d2d, 3, d2d_copies), (both, 6, ici_copies), (both, 9, d2d_copies)):
            if part is not None:
                for cp in fn(outs, ss, rs, part, base):
                    cp.wait_send()

    return _Ride([slot], [jax.ShapeDtypeStruct(slot.shape, slot.dtype)], 12, start, finish,
                 mid=mid if both is not None else None, mid_frac=mid_frac, aliases={0: 0})


def _ride_sibling_halves(g):
    S, R, C = g.shape
    hr = R // 2

    def copy(ins, outs, ss, rs):
        x, y, c, _ = _place()
        return _remote(ins[0].at[:, pl.ds((1 - c) * hr, hr), :], outs[0], ss.at[0], rs.at[0], (x, y, 1 - c))

    return _Ride([g], [jax.ShapeDtypeStruct((S, hr, C), g.dtype)], 1,
                 lambda *a: copy(*a).start(), lambda *a: copy(*a).wait())


def _ride_scatter(q, land=None, part=(0, 1)):
    k0, k1, n = part if len(part) == 3 else (part[0], part[0] + 1, part[1])
    rows_n = q.shape[1] // n
    rows = pl.ds(k0 * rows_n, (k1 - k0) * rows_n)

    def copies(ins, outs, ss, rs):
        x, y, c, chips = _place()
        return [_remote(ins[0].at[2 * chip[0] + chip[1], rows, :], outs[0].at[j, rows, :], ss.at[j], rs.at[j], (*chip, c))
                for j, chip in enumerate(chips)]

    def start(*a):
        for cp in copies(*a):
            cp.start()

    def finish(*a):
        for cp in copies(*a):
            cp.wait()

    shape = jax.ShapeDtypeStruct((3,) + q.shape[1:], q.dtype)
    if land is None:
        return _Ride([q], [shape], 3, start, finish)
    return _Ride([q, land], [shape], 3, start, finish, aliases={1: 0})


def _ride_to_sibling(a, halves=False, first=False, shards=None, land=None):
    s0, s1 = shards or (0, a.shape[0])

    def copy(ins, outs, ss, rs):
        x, y, c, _ = _place()
        if halves:
            src, dst = ins[0].at[s0:s1, 1 - c], outs[0].at[s0:s1]
        else:
            src, dst = (ins[0].at[0] if first else ins[0]), outs[0]
        return _remote(src, dst, ss.at[0], rs.at[0], (x, y, 1 - c))

    shape = (a.shape[0],) + a.shape[2:] if halves else (a.shape[1:] if first else a.shape)
    return _Ride([a] if land is None else [a, land], [jax.ShapeDtypeStruct(shape, a.dtype)], 1,
                 lambda *a_: copy(*a_).start(), lambda *a_: copy(*a_).wait(), aliases=None if land is None else {1: 0})


def _ride_rows_to_sibling(a, hr, shards, total):
    def copies(ins, outs, ss, rs):
        x, y, c, _ = _place()
        return [_remote(ins[0].at[pl.ds((2 * s + 1 - c) * hr, hr), :], outs[0].at[s], ss.at[s], rs.at[s], (x, y, 1 - c))
                for s in range(shards)]

    def start(*a_):
        for cp in copies(*a_):
            cp.start()

    def finish(*a_):
        for cp in copies(*a_):
            cp.wait()

    return _Ride([a], [jax.ShapeDtypeStruct((total, hr, a.shape[1]), a.dtype)], shards, start, finish)


def _ride_swap(h):
    def copy(ins, outs, ss, rs):
        x, y, c, _ = _place()
        return _remote(ins[0], outs[0], ss.at[0], rs.at[0], (x, y, 1 - c))

    return _Ride([h], [jax.ShapeDtypeStruct(h.shape, h.dtype)], 1,
                 lambda *a: copy(*a).start(), lambda *a: copy(*a).wait())


def _mesh_place(p):
    return (p // 4, (p // 2) % 2, p % 2)


def _ride_small_to_all(packed):
    def copies(ins, outs, ss, rs):
        x, y, c, _ = _place()
        me = 4 * x + 2 * y + c
        return [_remote(ins[0], outs[0].at[me], ss.at[k - 1], rs.at[k - 1], _mesh_place((me + k) % N_DEV))
                for k in range(1, N_DEV)]

    def own(ins, outs, ss, rs):
        x, y, c, _ = _place()
        return pltpu.make_async_copy(ins[0], outs[0].at[4 * x + 2 * y + c], ss.at[N_DEV - 1])

    def start(*a):
        own(*a).start()
        for cp in copies(*a):
            cp.start()

    def finish(ins, outs, ss, rs):
        x, y, c, _ = _place()
        me = 4 * x + 2 * y + c
        for k in range(1, N_DEV):
            _remote(ins[0], outs[0].at[(me + N_DEV - k) % N_DEV], ss.at[k - 1], rs.at[k - 1], (x, y, c)).wait_recv()
        for cp in copies(ins, outs, ss, rs):
            cp.wait_send()
        own(ins, outs, ss, rs).wait()

    return _Ride([packed], [jax.ShapeDtypeStruct((N_DEV,) + packed.shape, packed.dtype)], N_DEV, start, finish)


def _carrier(rides, *, name):
    _, outs = _call(lambda: None, name=name, grid=(1,), in_specs=[], out_specs=[], out_shape=[], rides=rides)()
    return outs


def _sq_relu_bf16(z):
    z = jnp.maximum(z, 0.0)
    return (z * z).astype(BF16)


def _norm_bf16(a_ref, g_ref):
    xf = a_ref[...]
    r = lax.rsqrt(jnp.mean(xf * xf, axis=-1, keepdims=True) + EPS)
    return ((xf * r) * g_ref[...]).astype(BF16)


def _norm_matmul_wide(a, g, b, *, tm, tn, name, rides=()):
    T, K = a.shape
    N = b.shape[0]

    def body(a_ref, g_ref, b_ref, n_ref, o_ref):
        n = _norm_bf16(a_ref, g_ref)
        n_ref[...] = n
        o_ref[...] = _dot_nt(n, b_ref[...])

    return _call(
        body, name=name, grid=(N // tn, T // tm),
        in_specs=[pl.BlockSpec((tm, K), lambda j, i: (i, 0)), pl.BlockSpec((1, K), lambda j, i: (0, 0)),
                  pl.BlockSpec((tn, K), lambda j, i: (j, 0))],
        out_specs=[pl.BlockSpec((None, tm, K), lambda j, i: (j, i, 0)), pl.BlockSpec((tm, tn), lambda j, i: (i, j))],
        out_shape=[jax.ShapeDtypeStruct((N // tn, T, K), BF16), jax.ShapeDtypeStruct((T, N), F32)],
        sem=("arbitrary", "arbitrary"), rides=rides,
    )(a, g, b)


def _norm_matmul_sq(a, g, b, *, tm, tn, name, rides=()):
    T, K = a.shape
    per = b.shape[2] // tn
    N = b.shape[0] * b.shape[2]

    def body(a_ref, g_ref, b_ref, nt_ref, o_ref, z_ref, zt_ref, n_scr):
        @pl.when(pl.program_id(1) == 0)
        def _():
            n = _norm_bf16(a_ref, g_ref)
            n_scr[...] = n
            nt_ref[...] = n.T
        r = jnp.maximum(_dot(n_scr[...], b_ref[...]), 0.0)
        o_ref[...] = r.astype(BF16)
        z = (r * r).astype(BF16)
        z_ref[...] = z
        zt_ref[...] = z.T

    return _call(
        body, name=name, grid=(T // tm, N // tn),
        in_specs=[pl.BlockSpec((tm, K), lambda i, j: (i, 0)), pl.BlockSpec((1, K), lambda i, j: (0, 0)),
                  pl.BlockSpec((None, K, tn), lambda i, j: (j // per, 0, j % per))],
        out_specs=[pl.BlockSpec((K, tm), lambda i, j: (0, i)), pl.BlockSpec((tm, tn), lambda i, j: (i, j)),
                   pl.BlockSpec((tm, tn), lambda i, j: (i, j)), pl.BlockSpec((tn, tm), lambda i, j: (j, i))],
        out_shape=[jax.ShapeDtypeStruct((K, T), BF16), jax.ShapeDtypeStruct((T, N), BF16),
                   jax.ShapeDtypeStruct((T, N), BF16), jax.ShapeDtypeStruct((N, T), BF16)],
        scratch_shapes=[pltpu.VMEM((tm, K), BF16)],
        sem=("parallel", "arbitrary"), rides=rides,
    )(a, g, b)


def _grad_pair(at, at_sib, b, b_sib, *, cols_sharded, tmo, tk, name, rides=()):
    S, _, hr, T = at.shape
    C = b.shape[-1] // N_CHIPS if cols_sharded else b.shape[-1]
    nk = T // tk
    a_sel = (lambda s: 0) if cols_sharded else (lambda s: s)
    b_sel = (lambda s: s) if cols_sharded else (lambda s: 0)
    if b.ndim == 3:
        b_spec = pl.BlockSpec((None, tk, C), lambda s, i, k: (0, k, b_sel(s)))
    else:
        b_spec = pl.BlockSpec((tk, C), lambda s, i, k: (k, b_sel(s)))

    def body(a_ref, as_ref, b_ref, bs_ref, o_ref, ob_ref):
        k = pl.program_id(2)
        p = _dot(a_ref[...], b_ref[...]) + _dot(as_ref[...], bs_ref[...])

        @pl.when(k == 0)
        def _():
            o_ref[...] = p

        @pl.when(k > 0)
        def _():
            o_ref[...] += p

        @pl.when(k == nk - 1)
        def _():
            ob_ref[...] = o_ref[...].astype(BF16)

    out = pl.BlockSpec((None, tmo, C), lambda s, i, k: (s, i, 0))
    return _call(
        body, name=name, grid=(N_CHIPS, hr // tmo, nk),
        in_specs=[pl.BlockSpec((None, None, tmo, tk), lambda s, i, k: (a_sel(s), lax.axis_index("c"), i, k)),
                  pl.BlockSpec((None, tmo, tk), lambda s, i, k: (a_sel(s), i, k)),
                  b_spec, pl.BlockSpec((tk, C), lambda s, i, k: (k, b_sel(s)))],
        out_specs=[out, out],
        out_shape=[jax.ShapeDtypeStruct((N_CHIPS, hr, C), F32), jax.ShapeDtypeStruct((N_CHIPS, hr, C), BF16)],
        sem=("parallel", "parallel", "arbitrary"), rides=rides,
    )(at, at_sib, b, b_sib)


def _matmul_nn(at, b, *, tmo, tn, tk, name, shards=1, rides=()):
    M, T = at.shape[-2:]
    N = b.shape[-1]
    if at.ndim == 3:
        a_spec = pl.BlockSpec((None, tmo, tk), lambda i, j, k: (0, i, k))
    else:
        a_spec = pl.BlockSpec((tmo, tk), lambda i, j, k: (i, k))
    if b.ndim == 3:
        b_spec = pl.BlockSpec((None, tk, tn), lambda i, j, k: (0, k, j))
    else:
        b_spec = pl.BlockSpec((tk, tn), lambda i, j, k: (k, j))
    if shards > 1:
        per = (N // shards) // tn
        out_spec = pl.BlockSpec((None, tmo, tn), lambda i, j, k: (j // per, i, j % per))
        out_shape = jax.ShapeDtypeStruct((shards, M, N // shards), F32)
    else:
        out_spec = pl.BlockSpec((tmo, tn), lambda i, j, k: (i, j))
        out_shape = jax.ShapeDtypeStruct((M, N), F32)

    def body(a_ref, b_ref, o_ref):
        k = pl.program_id(2)
        p = _dot(a_ref[...], b_ref[...])

        @pl.when(k == 0)
        def _():
            o_ref[...] = p

        @pl.when(k > 0)
        def _():
            o_ref[...] += p

    return _call(
        body, name=name, grid=(M // tmo, N // tn, T // tk),
        in_specs=[a_spec, b_spec],
        out_specs=out_spec, out_shape=out_shape,
        sem=("parallel", "parallel", "arbitrary"), rides=rides,
    )(at, b)


def _matmul_parts(parts, b, *, tm, tn, name, rides=()):
    T = parts[0].shape[0]
    N = b.shape[1]
    offs = [sum(p.shape[1] for p in parts[:i]) for i in range(len(parts))]
    assert all(o % p.shape[1] == 0 for o, p in zip(offs, parts))

    def body(*refs):
        n = len(parts)
        acc = _dot(refs[0][...], refs[n][...])
        for i in range(1, n):
            acc = acc + _dot(refs[i][...], refs[n + i][...])
        refs[-1][...] = acc

    a_specs = [pl.BlockSpec((tm, p.shape[1]), lambda i, j: (i, 0)) for p in parts]
    b_specs = [pl.BlockSpec((p.shape[1], tn), lambda i, j, r=o // p.shape[1]: (r, j)) for o, p in zip(offs, parts)]
    return _call(
        body, name=name, grid=(T // tm, N // tn), in_specs=a_specs + b_specs,
        out_specs=pl.BlockSpec((tm, tn), lambda i, j: (i, j)), out_shape=jax.ShapeDtypeStruct((T, N), F32),
        sem=("parallel", "parallel"), rides=rides,
    )(*parts, *([b] * len(parts)))


def _to_bf16(v):
    return v.astype(BF16)


def _matmul_res(a, b, res, *, tm, tn, tk, prologue, name, rides=()):
    T, K = a.shape
    N = b.shape[1]

    def body(a_ref, b_ref, res_ref, o_ref):
        k = pl.program_id(2)
        p = _dot(prologue(a_ref[...]), b_ref[...])

        @pl.when(k == 0)
        def _():
            o_ref[...] = res_ref[...] + p

        @pl.when(k > 0)
        def _():
            o_ref[...] += p

    return _call(
        body, name=name, grid=(T // tm, N // tn, K // tk),
        in_specs=[pl.BlockSpec((tm, tk), lambda i, j, k: (i, k)), pl.BlockSpec((tk, tn), lambda i, j, k: (k, j)),
                  pl.BlockSpec((tm, tn), lambda i, j, k: (i, j))],
        out_specs=pl.BlockSpec((tm, tn), lambda i, j, k: (i, j)),
        out_shape=jax.ShapeDtypeStruct((T, N), F32),
        sem=("parallel", "parallel", "arbitrary"), rides=rides,
    )(a, b, res)


def _matmul_nt(a, b, *, tm, tn, tk, name, extra=None, epilogue=None, out_dtype=F32, rides=()):
    T, K = a.shape
    if b.ndim == 3:
        per = b.shape[2] // tk
        N = b.shape[1]
        b_spec = pl.BlockSpec((None, tn, tk), lambda i, j, k: (k // per, j, k % per))
    else:
        N = b.shape[0]
        b_spec = pl.BlockSpec((tn, tk), lambda i, j, k: (j, k))
    nk = K // tk
    assert out_dtype == F32 or nk == 1
    in_specs = [pl.BlockSpec((tm, tk), lambda i, j, k: (i, k)), b_spec]
    args = [a, b]
    if extra is not None:
        in_specs.append(pl.BlockSpec((tm, tn), lambda i, j, k: (i, j)))
        args.append(extra)

    def body(*refs):
        a_ref, b_ref = refs[0], refs[1]
        o_ref = refs[-1]
        p = _dot_nt(a_ref[...].astype(BF16), b_ref[...])
        if nk == 1:
            if epilogue is not None:
                p = epilogue(p, refs[2][...])
            o_ref[...] = p.astype(out_dtype)
        else:
            k = pl.program_id(2)

            @pl.when(k == 0)
            def _():
                o_ref[...] = p

            @pl.when(k > 0)
            def _():
                o_ref[...] += p

    return _call(
        body, name=name, grid=(T // tm, N // tn, nk),
        in_specs=in_specs,
        out_specs=pl.BlockSpec((tm, tn), lambda i, j, k: (i, j)),
        out_shape=jax.ShapeDtypeStruct((T, N), out_dtype),
        sem=("parallel", "parallel", "arbitrary"), rides=rides,
    )(*args)


def _matmul_tn(a, b, *, tmo, tn, tk, name, a_prologue=_to_bf16, shards=1, rides=()):
    T, M = a.shape
    N = b.shape[1]
    if shards > 1:
        per = (N // shards) // tn
        out_spec = pl.BlockSpec((None, tmo, tn), lambda i, j, k: (j // per, i, j % per))
        out_shape = jax.ShapeDtypeStruct((shards, M, N // shards), F32)
    else:
        out_spec = pl.BlockSpec((tmo, tn), lambda i, j, k: (i, j))
        out_shape = jax.ShapeDtypeStruct((M, N), F32)

    def body(a_ref, b_ref, o_ref):
        k = pl.program_id(2)
        p = _dot_tn(a_prologue(a_ref[...]), b_ref[...].astype(BF16))

        @pl.when(k == 0)
        def _():
            o_ref[...] = p

        @pl.when(k > 0)
        def _():
            o_ref[...] += p

    return _call(
        body, name=name, grid=(M // tmo, N // tn, T // tk),
        in_specs=[pl.BlockSpec((tk, tmo), lambda i, j, k: (k, i)), pl.BlockSpec((tk, tn), lambda i, j, k: (k, j))],
        out_specs=out_spec, out_shape=out_shape,
        sem=("parallel", "parallel", "arbitrary"), rides=rides,
    )(a, b)


def _loss_bwd(h2, tgt, g, *, tm):
    T, D = h2.shape

    def body(h_ref, t_ref, g_ref, dh_ref, dhb_ref, dg_ref, loss_ref):
        @pl.when(pl.program_id(0) == 0)
        def _():
            dg_ref[...] = jnp.zeros_like(dg_ref)
            loss_ref[...] = jnp.zeros_like(loss_ref)
        h = h_ref[...]
        gg = g_ref[...]
        r = lax.rsqrt(jnp.mean(h * h, axis=-1, keepdims=True) + EPS)
        hn = h * r
        err = hn * gg - t_ref[...]
        loss_ref[...] += 0.5 * jnp.sum(jnp.mean(err * err, axis=-1, keepdims=True), axis=0, keepdims=True)
        dy = err * (1.0 / D)
        dg_ref[...] += jnp.sum(dy * hn, axis=0, keepdims=True)
        w = dy * gg
        dh = r * w - h * ((r * r * r) * jnp.mean(w * h, axis=-1, keepdims=True))
        dh_ref[...] = dh
        dhb_ref[...] = dh.astype(BF16)

    tile = pl.BlockSpec((tm, D), lambda i: (i, 0))
    return pl.pallas_call(
        body, name="loss_bwd", grid=(T // tm,),
        in_specs=[tile, tile, pl.BlockSpec((1, D), lambda i: (0, 0))],
        out_specs=[tile, tile, pl.BlockSpec((1, D), lambda i: (0, 0)), pl.BlockSpec((1, 1), lambda i: (0, 0))],
        out_shape=[jax.ShapeDtypeStruct((T, D), F32), jax.ShapeDtypeStruct((T, D), BF16),
                   jax.ShapeDtypeStruct((1, D), F32), jax.ShapeDtypeStruct((1, 1), F32)],
        compiler_params=_params(("arbitrary",)),
    )(h2, tgt, g)


def _rms_bwd_res(dn, h, g, dres, *, tm, name, rides=()):
    T, D = h.shape

    def body(dn_ref, h_ref, g_ref, dres_ref, dh_ref, dhb_ref, dg_ref):
        @pl.when(pl.program_id(0) == 0)
        def _():
            dg_ref[...] = jnp.zeros_like(dg_ref)
        h_ = h_ref[...]
        dn_ = dn_ref[...]
        dh, r = _rms_bwd(dn_, h_, g_ref[...])
        dg_ref[...] += jnp.sum(dn_ * (h_ * r), axis=0, keepdims=True)
        dh = dres_ref[...] + dh
        dh_ref[...] = dh
        dhb_ref[...] = dh.astype(BF16)

    tile = pl.BlockSpec((tm, D), lambda i: (i, 0))
    return _call(
        body, name=name, grid=(T // tm,),
        in_specs=[tile, tile, pl.BlockSpec((1, D), lambda i: (0, 0)), tile],
        out_specs=[tile, tile, pl.BlockSpec((1, D), lambda i: (0, 0))],
        out_shape=[jax.ShapeDtypeStruct((T, D), F32), jax.ShapeDtypeStruct((T, D), BF16),
                   jax.ShapeDtypeStruct((1, D), F32)],
        sem=("arbitrary",), rides=rides,
    )(dn, h, g, dres)


def _rel_distance():
    i = lax.broadcasted_iota(jnp.int32, (CHUNK, 2 * CHUNK), 0)
    j = lax.broadcasted_iota(jnp.int32, (CHUNK, 2 * CHUNK), 1)
    return i + CHUNK - j


def _bias_build(table):
    def body(tab_ref, o_ref):
        rel = _rel_distance()
        ge = [rel >= t for t in BUCKET_THR]
        for h in range(B_HEADS):
            cur = jnp.full((CHUNK, 2 * CHUNK), tab_ref[0, h], F32)
            for b in range(1, N_BUCKETS):
                cur = jnp.where(ge[b - 1], tab_ref[b, h], cur)
            o_ref[h] = cur

    return pl.pallas_call(
        body, name="bias_build",
        in_specs=[pl.BlockSpec(memory_space=pltpu.SMEM)],
        out_specs=pl.BlockSpec(memory_space=pltpu.VMEM),
        out_shape=jax.ShapeDtypeStruct((B_HEADS, CHUNK, 2 * CHUNK), F32),
    )(table)


def _bias_grad(dbias):
    def body(db_ref, o_ref, acc_ref):
        rel = _rel_distance()
        lo = [0] + BUCKET_THR
        hi = BUCKET_THR + [CHUNK]
        for b in range(N_BUCKETS):
            m = (rel >= lo[b]) & (rel < hi[b])
            for h in range(B_HEADS):
                row = b * B_HEADS + h
                acc_ref[row:row + 1, :] = jnp.sum(jnp.where(m, db_ref[h], 0.0), axis=0, keepdims=True)
        o_ref[...] = jnp.sum(acc_ref[...], axis=1, keepdims=True)

    return pl.pallas_call(
        body, name="bias_grad",
        in_specs=[pl.BlockSpec(memory_space=pltpu.VMEM)],
        out_specs=pl.BlockSpec(memory_space=pltpu.VMEM),
        out_shape=jax.ShapeDtypeStruct((N_BUCKETS * B_HEADS, 1), F32),
        scratch_shapes=[pltpu.VMEM((N_BUCKETS * B_HEADS, 2 * CHUNK), F32)],
    )(dbias)


def _causal_mask():
    t = lax.broadcasted_iota(jnp.int32, (CHUNK, CHUNK), 0)
    s = lax.broadcasted_iota(jnp.int32, (CHUNK, CHUNK), 1)
    return s <= t


def _band_mask(n):
    rel = _rel_distance()
    j = lax.broadcasted_iota(jnp.int32, (CHUNK, 2 * CHUNK), 1)
    return (rel >= 0) & (rel < CHUNK) & ((n > 0) | (j >= CHUNK))


def _gate_forward(u, v, lg, lb, wc, bs):
    ug = _gelu(u)
    vg = _gelu(v)
    mu = jnp.mean(vg, axis=-1, keepdims=True)
    xc = vg - mu
    rstd = lax.rsqrt(jnp.mean(xc * xc, axis=-1, keepdims=True) + EPS)
    xhat = xc * rstd
    vl = (xhat * lg + lb).astype(BF16)
    mixed = _dot(wc, vl) + bs
    return ug, xhat, rstd, vl, mixed


def _softmax_scores(qk, bias, mask, sink):
    s = qk * SCALE + bias
    s = jnp.where(mask, s, NEG)
    m = jnp.maximum(jnp.max(s, axis=-1, keepdims=True), sink)
    p = jnp.exp(s - m)
    e_sink = jnp.exp(sink - m)
    inv = 1.0 / (jnp.sum(p, axis=-1, keepdims=True) + e_sink)
    return p * inv, e_sink * inv


PAIRS = Q_PER_KV // 2


def _head(g, pr, e):
    return g * Q_PER_KV + 2 * pr + e


def _stack_pairs(ref, g, col0=0):
    w = 2 * HEAD_DIM
    return jnp.concatenate([ref[:, col0 + (g * PAIRS + pr) * w:col0 + (g * PAIRS + pr + 1) * w] for pr in range(PAIRS)],
                           axis=0)


def _low_lanes():
    return lax.broadcasted_iota(jnp.int32, (2 * CHUNK, 2 * HEAD_DIM), 1) < HEAD_DIM


def _band_operands(kv_prev, kv_cur):
    band = jnp.concatenate([kv_prev, kv_cur], axis=0)
    low = _low_lanes()
    ops = []
    for cat in (band[:, :KV_WIDTH], band[:, KV_WIDTH:]):
        rol = pltpu.roll(cat, HEAD_DIM, 1)
        ops.append([[jnp.where(low if e == 0 else ~low, cat if g == e else rol, 0.0).astype(BF16) for e in range(2)]
                    for g in range(2)])
    return ops


def _mixer_fwd(proj, lg, lb, wsp, bs_col, sinks, bias, ga, gb, rides=()):
    T = proj.shape[0]
    nb = T // CHUNK

    def body(u_ref, v_ref, q_ref, kvc_ref, kvp_ref, lg_ref, lb_ref, w_ref, bs_ref, sink_ref, bias_ref,
             ga_ref, gb_ref, mixed_ref, mixed_t_ref, ab_ref):
        n = pl.program_id(0)
        causal = _causal_mask()
        ssq = jnp.zeros((CHUNK, 1), F32)
        for g in range(A_GROUPS):
            cols = slice(g * CHUNK, (g + 1) * CHUNK)
            wc = jnp.where(causal, w_ref[g], 0.0).astype(BF16)
            ug, _, _, _, mixed = _gate_forward(u_ref[:, cols], v_ref[:, cols], lg_ref[g:g + 1, :], lb_ref[g:g + 1, :],
                                               wc, bs_ref[g])
            a = ug * mixed
            ab_ref[:, cols] = a
            ssq = ssq + jnp.sum(a * a, axis=-1, keepdims=True)
        ra = lax.rsqrt(ssq * (1.0 / A_WIDTH) + EPS)
        mixed_ref[:, :A_WIDTH] = ((ab_ref[:, :A_WIDTH] * ra) * ga_ref[...]).astype(BF16)

        mask = _band_mask(n)
        kops, vops = _band_operands(kvp_ref[...], kvc_ref[...])
        ssq = jnp.zeros((CHUNK, 1), F32)
        for g in range(B_HEADS // Q_PER_KV):
            qst = _stack_pairs(q_ref, g).astype(BF16)
            o_st = jnp.zeros((PAIRS * CHUNK, 2 * HEAD_DIM), F32)
            for e in range(2):
                s_all = _dot_nt(qst, kops[g][e])
                ps = []
                for pr in range(PAIRS):
                    h = _head(g, pr, e)
                    p, _ = _softmax_scores(s_all[pr * CHUNK:(pr + 1) * CHUNK], bias_ref[h], mask, sink_ref[0, h])
                    ps.append(p.astype(BF16))
                o_st = o_st + _dot(jnp.concatenate(ps, axis=0), vops[g][e])
            for pr in range(PAIRS):
                o = o_st[pr * CHUNK:(pr + 1) * CHUNK]
                c0 = A_WIDTH + (g * PAIRS + pr) * 2 * HEAD_DIM
                ab_ref[:, c0:c0 + 2 * HEAD_DIM] = o
                ssq = ssq + jnp.sum(o * o, axis=-1, keepdims=True)
        rb = lax.rsqrt(ssq * (1.0 / B_WIDTH) + EPS)
        mixed_ref[:, A_WIDTH:] = ((ab_ref[:, A_WIDTH:] * rb) * gb_ref[...]).astype(BF16)
        mixed_t_ref[...] = mixed_ref[...].T

    full = lambda *shape: pl.BlockSpec(shape, lambda n: (0,) * len(shape))
    return _call(
        body, name="mixer_fwd", grid=(nb,),
        in_specs=[pl.BlockSpec((CHUNK, A_WIDTH), lambda n: (n, 0)),
                  pl.BlockSpec((CHUNK, A_WIDTH), lambda n: (n, 1)),
                  pl.BlockSpec((CHUNK, B_WIDTH), lambda n: (n, 2)),
                  pl.BlockSpec((CHUNK, 2 * KV_WIDTH), lambda n: (n, 12)),
                  pl.BlockSpec((CHUNK, 2 * KV_WIDTH), lambda n: (jnp.maximum(n - 1, 0), 12)),
                  full(A_GROUPS, CHUNK), full(A_GROUPS, CHUNK), full(A_GROUPS, CHUNK, CHUNK), full(A_GROUPS, CHUNK, 1),
                  pl.BlockSpec(memory_space=pltpu.SMEM), full(B_HEADS, CHUNK, 2 * CHUNK),
                  full(1, A_WIDTH), full(1, B_WIDTH)],
        out_specs=[pl.BlockSpec((CHUNK, D_MODEL), lambda n: (n, 0)), pl.BlockSpec((D_MODEL, CHUNK), lambda n: (0, n)),
                   pl.BlockSpec((CHUNK, D_MODEL), lambda n: (n, 0))],
        out_shape=[jax.ShapeDtypeStruct((T, D_MODEL), BF16), jax.ShapeDtypeStruct((D_MODEL, T), BF16),
                   jax.ShapeDtypeStruct((T, D_MODEL), F32)],
        sem=("parallel",), rides=rides,
    )(proj, proj, proj, proj, proj, lg, lb, wsp, bs_col, sinks, bias, ga, gb)


def _gmlp_bwd(proj, ab, dmixed, ga, lg, lb, wsp, bs_col, rides=()):
    T = proj.shape[0]
    nb = T // CHUNK

    def body(u_ref, v_ref, a_ref, dna_ref, ga_ref, lg_ref, lb_ref, w_ref, bs_ref,
             dp_ref, dpt_ref, dga_ref, dw_ref, dbs_ref, dlg_ref, dlb_ref):
        @pl.when(pl.program_id(0) == 0)
        def _():
            for r in (dga_ref, dw_ref, dbs_ref, dlg_ref, dlb_ref):
                r[...] = jnp.zeros_like(r)
        causal = _causal_mask()
        a_all = a_ref[...]
        dna = dna_ref[...]
        da_all, ra = _rms_bwd(dna, a_all, ga_ref[...])
        dga_ref[...] += jnp.sum(dna * (a_all * ra), axis=0, keepdims=True)
        for g in range(A_GROUPS):
            cols = slice(g * CHUNK, (g + 1) * CHUNK)
            wc = jnp.where(causal, w_ref[g], 0.0).astype(BF16)
            lgg = lg_ref[g:g + 1, :]
            u = u_ref[:, cols]
            v = v_ref[:, cols]
            ug, xhat, rstd, vl, mixed = _gate_forward(u, v, lgg, lb_ref[g:g + 1, :], wc, bs_ref[g])
            da = da_all[:, cols]
            dug = da * mixed
            dmg = da * ug
            dmg_b = dmg.astype(BF16)
            dbs_ref[g] += jnp.sum(dmg, axis=-1, keepdims=True)
            dw_ref[g] += jnp.where(causal, _dot_nt(dmg_b, vl), 0.0)
            dvl = _dot_tn(wc, dmg_b)
            dlg_ref[g:g + 1, :] += jnp.sum(dvl * xhat, axis=0, keepdims=True)
            dlb_ref[g:g + 1, :] += jnp.sum(dvl, axis=0, keepdims=True)
            dxh = dvl * lgg
            dvg = rstd * (dxh - jnp.mean(dxh, axis=-1, keepdims=True)
                          - xhat * jnp.mean(dxh * xhat, axis=-1, keepdims=True))
            _, gu = _gelu_and_grad(u)
            _, gv = _gelu_and_grad(v)
            dp_ref[:, cols] = (dug * gu).astype(BF16)
            dp_ref[:, A_WIDTH + g * CHUNK:A_WIDTH + (g + 1) * CHUNK] = (dvg * gv).astype(BF16)
        dpt_ref[...] = dp_ref[...].T

    full = lambda *shape: pl.BlockSpec(shape, lambda n: (0,) * len(shape))
    return _call(
        body, name="gmlp_bwd", grid=(nb,),
        in_specs=[pl.BlockSpec((CHUNK, A_WIDTH), lambda n: (n, 0)),
                  pl.BlockSpec((CHUNK, A_WIDTH), lambda n: (n, 1)),
                  pl.BlockSpec((CHUNK, A_WIDTH), lambda n: (n, 0)),
                  pl.BlockSpec((CHUNK, A_WIDTH), lambda n: (n, 0)),
                  full(1, A_WIDTH), full(A_GROUPS, CHUNK), full(A_GROUPS, CHUNK), full(A_GROUPS, CHUNK, CHUNK),
                  full(A_GROUPS, CHUNK, 1)],
        out_specs=[pl.BlockSpec((CHUNK, 2 * A_WIDTH), lambda n: (n, 0)), pl.BlockSpec((2 * A_WIDTH, CHUNK), lambda n: (0, n)),
                   full(1, A_WIDTH), full(A_GROUPS, CHUNK, CHUNK), full(A_GROUPS, CHUNK, 1),
                   full(A_GROUPS, CHUNK), full(A_GROUPS, CHUNK)],
        out_shape=[jax.ShapeDtypeStruct((T, 2 * A_WIDTH), BF16), jax.ShapeDtypeStruct((2 * A_WIDTH, T), BF16),
                   jax.ShapeDtypeStruct((1, A_WIDTH), F32), jax.ShapeDtypeStruct((A_GROUPS, CHUNK, CHUNK), F32),
                   jax.ShapeDtypeStruct((A_GROUPS, CHUNK, 1), F32), jax.ShapeDtypeStruct((A_GROUPS, CHUNK), F32),
                   jax.ShapeDtypeStruct((A_GROUPS, CHUNK), F32)],
        sem=("arbitrary",), rides=rides,
    )(proj, proj, ab, dmixed, ga, lg, lb, wsp, bs_col)


def _attn_bwd(proj, ab, dmixed, gb, sinks, bias, rides=()):
    T = proj.shape[0]
    nb = T // CHUNK
    qn = lambda n: jnp.minimum(n, nb - 1)

    def body(q_ref, kvc_ref, kvp_ref, o_ref, dnb_ref, gb_ref, sink_ref, bias_ref,
             dq_ref, dkv_ref, dqt_ref, dkvt_ref, dgb_ref, dsink_ref, dbias_ref, carry_ref, sacc_ref):
        n = pl.program_id(0)

        @pl.when(n == 0)
        def _():
            carry_ref[...] = jnp.zeros_like(carry_ref)
            sacc_ref[...] = jnp.zeros_like(sacc_ref)
            dgb_ref[...] = jnp.zeros_like(dgb_ref)
            dbias_ref[...] = jnp.zeros_like(dbias_ref)

        @pl.when(n < nb)
        def _():
            mask = _band_mask(n)
            o_all = o_ref[...]
            dnb = dnb_ref[...]
            do_all, rb = _rms_bwd(dnb, o_all, gb_ref[...])
            dgb_ref[...] += jnp.sum(dnb * (o_all * rb), axis=0, keepdims=True)
            kops, vops = _band_operands(kvp_ref[...], kvc_ref[...])
            low = _low_lanes()
            halves = []
            for g in range(B_HEADS // Q_PER_KV):
                qst = _stack_pairs(q_ref, g).astype(BF16)
                dost = _stack_pairs(do_all, g).astype(BF16)
                dq_st = jnp.zeros((PAIRS * CHUNK, 2 * HEAD_DIM), F32)
                dk_e, dv_e = [], []
                for e in range(2):
                    s_all = _dot_nt(qst, kops[g][e])
                    dp_all = _dot_nt(dost, vops[g][e])
                    ps, dsrs = [], []
                    for pr in range(PAIRS):
                        h = _head(g, pr, e)
                        rows = slice(pr * CHUNK, (pr + 1) * CHUNK)
                        p, p_sink = _softmax_scores(s_all[rows], bias_ref[h], mask, sink_ref[0, h])
                        dp = dp_all[rows]
                        delta = jnp.sum(p * dp, axis=-1, keepdims=True)
                        ds = p * (dp - delta)
                        sacc_ref[:, h:h + 1] += -(p_sink * delta)
                        dbias_ref[h] += ds
                        ps.append(p.astype(BF16))
                        dsrs.append((ds * SCALE).astype(BF16))
                    dsr_all = jnp.concatenate(dsrs, axis=0)
                    dq_st = dq_st + _dot(dsr_all, kops[g][e])
                    dk_e.append(_dot_tn(dsr_all, qst))
                    dv_e.append(_dot_tn(jnp.concatenate(ps, axis=0), dost))
                for pr in range(PAIRS):
                    c0 = (g * PAIRS + pr) * 2 * HEAD_DIM
                    dq_ref[:, c0:c0 + 2 * HEAD_DIM] = dq_st[pr * CHUNK:(pr + 1) * CHUNK].astype(BF16)
                halves.append((dk_e, dv_e))
            tiles = []
            for t in range(2):
                g0, g1 = halves[0][t], halves[1][t]
                tiles.append(jnp.where(low, g0[0] + pltpu.roll(g0[1], HEAD_DIM, 1), pltpu.roll(g1[0], HEAD_DIM, 1) + g1[1]))
            dband = jnp.concatenate(tiles, axis=1)
            dkv = (carry_ref[...] + dband[:CHUNK]).astype(BF16)
            dkv_ref[...] = dkv
            dkvt_ref[...] = dkv.T
            dqt_ref[...] = dq_ref[...].T
            carry_ref[...] = dband[CHUNK:]

        @pl.when(n == nb)
        def _():
            dkv = carry_ref[...].astype(BF16)
            dkv_ref[...] = dkv
            dkvt_ref[...] = dkv.T
            dsink_ref[...] = jnp.sum(sacc_ref[...], axis=0, keepdims=True)

    full = lambda *shape: pl.BlockSpec(shape, lambda n: (0,) * len(shape))
    return _call(
        body, name="attn_bwd", grid=(nb + 1,),
        in_specs=[pl.BlockSpec((CHUNK, B_WIDTH), lambda n: (qn(n), 2)),
                  pl.BlockSpec((CHUNK, 2 * KV_WIDTH), lambda n: (qn(n), 12)),
                  pl.BlockSpec((CHUNK, 2 * KV_WIDTH), lambda n: (jnp.maximum(qn(n) - 1, 0), 12)),
                  pl.BlockSpec((CHUNK, B_WIDTH), lambda n: (qn(n), 1)),
                  pl.BlockSpec((CHUNK, B_WIDTH), lambda n: (qn(n), 1)),
                  full(1, B_WIDTH), pl.BlockSpec(memory_space=pltpu.SMEM), full(B_HEADS, CHUNK, 2 * CHUNK)],
        out_specs=[pl.BlockSpec((CHUNK, B_WIDTH), lambda n: (qn(n), 0)),
                   pl.BlockSpec((CHUNK, 2 * KV_WIDTH), lambda n: (jnp.maximum(n - 1, 0), 0)),
                   pl.BlockSpec((B_WIDTH, CHUNK), lambda n: (0, qn(n))),
                   pl.BlockSpec((2 * KV_WIDTH, CHUNK), lambda n: (0, jnp.maximum(n - 1, 0))),
                   full(1, B_WIDTH), full(1, B_HEADS), full(B_HEADS, CHUNK, 2 * CHUNK)],
        out_shape=[jax.ShapeDtypeStruct((T, B_WIDTH), BF16), jax.ShapeDtypeStruct((T, 2 * KV_WIDTH), BF16),
                   jax.ShapeDtypeStruct((B_WIDTH, T), BF16), jax.ShapeDtypeStruct((2 * KV_WIDTH, T), BF16),
                   jax.ShapeDtypeStruct((1, B_WIDTH), F32), jax.ShapeDtypeStruct((1, B_HEADS), F32),
                   jax.ShapeDtypeStruct((B_HEADS, CHUNK, 2 * CHUNK), F32)],
        scratch_shapes=[pltpu.VMEM((CHUNK, 2 * KV_WIDTH), F32), pltpu.VMEM((CHUNK, B_HEADS), F32)],
        sem=("arbitrary",), rides=rides,
    )(proj, proj, proj, ab, dmixed, gb, sinks, bias)


def _sq_relu_grad(acc, r):
    return acc * (2.0 * r.astype(F32))


def _local_step(x, tgt, sp, win, wo, wu, wd):
    T = x.shape[0]
    tm = min(512, T)
    tk = min(512, T)
    lg = sp["gate_norm_g"].reshape(A_GROUPS, CHUNK)
    lb = sp["gate_norm_b"].reshape(A_GROUPS, CHUNK)
    wsp = sp["w_spatial"].reshape(A_GROUPS, CHUNK, CHUNK)
    bs_col = sp["b_spatial"].reshape(A_GROUPS, CHUNK, 1)
    sinks = sp["attn_sinks"].reshape(1, B_HEADS)
    ga = sp["out_norm_a_g"].reshape(1, A_WIDTH)
    gb = sp["out_norm_b_g"].reshape(1, B_WIDTH)
    g1 = sp["mix_norm_g"].reshape(1, D_MODEL)
    g2 = sp["ffn_norm_g"].reshape(1, D_MODEL)
    gf = sp["final_norm_g"].reshape(1, D_MODEL)

    bias = _bias_build(sp["rel_bias_table"])
    n1, proj = _norm_matmul(x, g1, win, tm=tm, tn=PROJ_WIDTH // 2, name="in_proj")
    mixed, ab = _mixer_fwd(proj, lg, lb, wsp, bs_col, sinks, bias, ga, gb)
    h1 = _matmul_res(mixed, wo, x, tm=tm, tn=1024, tk=D_MODEL, prologue=_to_bf16, name="out_proj")
    n2, zp = _norm_matmul(h1, g2, wu, tm=tm, tn=1024, name="up_proj")
    h2 = _matmul_res(zp, wd, h1, tm=tm, tn=1024, tk=2048, prologue=_sq_relu_bf16, name="down_proj")

    dh2, dgf, loss = _loss_bwd(h2, tgt, gf, tm=tm)
    dzp = _matmul_nt(dh2, wd, tm=tm, tn=1024, tk=D_MODEL, name="bwd_dz", extra=zp, epilogue=_sq_relu_grad,
                     out_dtype=BF16)
    dwd = _matmul_tn(zp, dh2, tmo=1024, tn=1024, tk=tk, name="grad_w_down", a_prologue=_sq_relu_bf16)
    dwu = _matmul_tn(n2, dzp, tmo=1024, tn=1024, tk=tk, name="grad_w_up", shards=N_CHIPS)
    dn2 = _matmul_nt(dzp, wu, tm=tm, tn=1024, tk=2048, name="bwd_dn2")
    dh1, dg2 = _rms_bwd_res(dn2, h1, g2, dh2, tm=tm, name="ffn_norm_bwd")
    dwo = _matmul_tn(mixed, dh1, tmo=1024, tn=1024, tk=tk, name="grad_w_out")
    dmixed = _matmul_nt(dh1, wo, tm=tm, tn=1024, tk=D_MODEL, name="bwd_dmixed")
    duv, dga, dwsp, dbs, dlg, dlb = _gmlp_bwd(proj, ab, dmixed, ga, lg, lb, wsp, bs_col)
    dq, dkv, dgb, dsinks, dbias = _attn_bwd(proj, ab, dmixed, gb, sinks, bias)
    dtable = _bias_grad(dbias)
    dproj = jnp.concatenate([duv, dq, dkv], axis=1)
    dwin = _matmul_tn(n1, dproj, tmo=1024, tn=PROJ_WIDTH // 2, tk=tk, name="grad_w_in")
    dn1 = _matmul_nt(dproj, win, tm=tm, tn=1024, tk=PROJ_WIDTH, name="bwd_dn1")
    dx, dg1 = _rms_bwd_res(dn1, x, g1, dh1, tm=tm, name="mix_norm_bwd")

    small = {
        "rel_bias_table": dtable.reshape(N_BUCKETS, B_HEADS), "mix_norm_g": dg1, "gate_norm_g": dlg, "gate_norm_b": dlb,
        "w_spatial": dwsp, "b_spatial": dbs, "attn_sinks": dsinks, "out_norm_a_g": dga, "out_norm_b_g": dgb,
        "ffn_norm_g": dg2, "final_norm_g": dgf,
    }
    return loss, dx, (dwin, dwo, dwu, dwd), small


def _place():
    x, y, c = lax.axis_index("x"), lax.axis_index("y"), lax.axis_index("c")
    chips = [(1 - x, y), (x, 1 - y), (1 - x, 1 - y)]
    return x, y, c, chips


def _remote(src, dst, send_sem, recv_sem, to):
    return pltpu.make_async_remote_copy(src_ref=src, dst_ref=dst, send_sem=send_sem, recv_sem=recv_sem,
                                        device_id=to, device_id_type=MESH)


def _core_index():
    return lax.axis_index("c").astype(jnp.int32).reshape(1)


def _chip_index():
    return (2 * lax.axis_index("x") + lax.axis_index("y")).astype(jnp.int32).reshape(1)


def _cast_into_slot(w, *, tm, name):
    _, R, C = w.shape

    def body(me_ref, w_ref, o_ref):
        del me_ref
        o_ref[...] = w_ref[...].astype(BF16)

    return pl.pallas_call(
        body, name=name,
        grid_spec=pltpu.PrefetchScalarGridSpec(
            num_scalar_prefetch=1, grid=(R // tm,),
            in_specs=[pl.BlockSpec((None, tm, C), lambda i, me: (0, i, 0))],
            out_specs=pl.BlockSpec((None, tm, C), lambda i, me: (me[0], i, 0))),
        out_shape=jax.ShapeDtypeStruct((N_CHIPS, R, C), BF16), compiler_params=_params(("parallel",)),
    )(_chip_index(), w)


def _cast_into_slot_carrying(w, *, tm, name, rides):
    _, R, C = w.shape

    def body(w_ref, o_ref):
        o_ref[...] = w_ref[...].astype(BF16)

    return _call(
        body, name=name, grid=(R // tm,),
        in_specs=[pl.BlockSpec((None, tm, C), lambda i: (0, i, 0))],
        out_specs=pl.BlockSpec((None, tm, C), lambda i: (2 * lax.axis_index("x") + lax.axis_index("y"), i, 0)),
        out_shape=jax.ShapeDtypeStruct((N_CHIPS, R, C), BF16), sem=("arbitrary",), rides=rides,
    )(w)


def _gather_weights(slots):
    nw = len(slots)

    def body(*refs):
        fulls = refs[nw:2 * nw]
        send_sems, recv_sems = refs[2 * nw:]
        x, y, c, chips = _place()
        me = 2 * x + y
        sends = []
        for w in range(nw):
            hr = fulls[w].shape[1] // 2
            rows = pl.ds(c * hr, hr)
            for j, chip in enumerate(chips):
                mine = fulls[w].at[me, rows, :]
                cp = _remote(mine, mine, send_sems.at[6 * w + j], recv_sems.at[6 * w + j], (*chip, c))
                cp.start()
                sends.append(cp)
        for w in range(nw):
            hr = fulls[w].shape[1] // 2
            rows = pl.ds(c * hr, hr)
            for j, chip in enumerate(chips):
                landed = fulls[w].at[2 * chip[0] + chip[1], rows, :]
                _remote(landed, landed, send_sems.at[6 * w + j], recv_sems.at[6 * w + j], (x, y, c)).wait_recv()
                cp = _remote(landed, landed, send_sems.at[6 * w + 3 + j], recv_sems.at[6 * w + 3 + j], (x, y, 1 - c))
                cp.start()
                sends.append(cp)
        for w in range(nw):
            hr = fulls[w].shape[1] // 2
            rows = pl.ds((1 - c) * hr, hr)
            for j, chip in enumerate(chips):
                other = fulls[w].at[2 * chip[0] + chip[1], rows, :]
                _remote(other, other, send_sems.at[6 * w + 3 + j], recv_sems.at[6 * w + 3 + j], (x, y, c)).wait_recv()
        for cp in sends:
            cp.wait_send()

    any_spec = pl.BlockSpec(memory_space=pl.ANY)
    return pl.pallas_call(
        body, name="gather_weights",
        in_specs=[any_spec] * nw, out_specs=[any_spec] * nw,
        out_shape=[jax.ShapeDtypeStruct(s.shape, s.dtype) for s in slots],
        scratch_shapes=[pltpu.SemaphoreType.DMA((6 * nw,)), pltpu.SemaphoreType.DMA((6 * nw,))],
        input_output_aliases={w: w for w in range(nw)},
    )(*slots)


def _sibling_halves(grads):
    nw = len(grads)

    def body(*refs):
        gs, outs = refs[:nw], refs[nw:2 * nw]
        send_sems, recv_sems = refs[2 * nw:]
        x, y, c, _ = _place()
        cps = []
        for w in range(nw):
            hr = gs[w].shape[1] // 2
            cp = _remote(gs[w].at[:, pl.ds((1 - c) * hr, hr), :], outs[w], send_sems.at[w], recv_sems.at[w],
                         (x, y, 1 - c))
            cp.start()
            cps.append(cp)
        for cp in cps:
            cp.wait()

    any_spec = pl.BlockSpec(memory_space=pl.ANY)
    return pl.pallas_call(
        body, name="rs_sibling_halves",
        in_specs=[any_spec] * nw, out_specs=[any_spec] * nw,
        out_shape=[jax.ShapeDtypeStruct((g.shape[0], g.shape[1] // 2, g.shape[2]), g.dtype) for g in grads],
        scratch_shapes=[pltpu.SemaphoreType.DMA((nw,)), pltpu.SemaphoreType.DMA((nw,))],
    )(*grads)


def _pair_sum_bf16(g, got, *, tm, name):
    S, R, C = g.shape
    hr = R // 2
    nt = hr // tm

    def body(c_ref, g_ref, got_ref, o_ref):
        del c_ref
        o_ref[...] = (g_ref[...] + got_ref[...]).astype(BF16)

    return pl.pallas_call(
        body, name=name,
        grid_spec=pltpu.PrefetchScalarGridSpec(
            num_scalar_prefetch=1, grid=(S, nt),
            in_specs=[pl.BlockSpec((None, tm, C), lambda s, i, c: (s, c[0] * nt + i, 0)),
                      pl.BlockSpec((None, tm, C), lambda s, i, c: (s, i, 0))],
            out_specs=pl.BlockSpec((None, tm, C), lambda s, i, c: (s, i, 0))),
        out_shape=jax.ShapeDtypeStruct((S, hr, C), BF16),
        compiler_params=_params(("parallel", "parallel")),
    )(_core_index(), g, got)


def _scatter_to_owners(pairs):
    nw = len(pairs)

    def body(*refs):
        qs, outs = refs[:nw], refs[nw:2 * nw]
        send_sems, recv_sems = refs[2 * nw:]
        x, y, c, chips = _place()
        cps = []
        for w in range(nw):
            for j, chip in enumerate(chips):
                cp = _remote(qs[w].at[2 * chip[0] + chip[1]], outs[w].at[j], send_sems.at[3 * w + j],
                             recv_sems.at[3 * w + j], (*chip, c))
                cp.start()
                cps.append(cp)
        for cp in cps:
            cp.wait()

    any_spec = pl.BlockSpec(memory_space=pl.ANY)
    return pl.pallas_call(
        body, name="rs_scatter_to_owners",
        in_specs=[any_spec] * nw, out_specs=[any_spec] * nw,
        out_shape=[jax.ShapeDtypeStruct((3,) + q.shape[1:], q.dtype) for q in pairs],
        scratch_shapes=[pltpu.SemaphoreType.DMA((3 * nw,)), pltpu.SemaphoreType.DMA((3 * nw,))],
    )(*pairs)


def _owner_total(gh, others, *, tm, name):
    _, hr, C = gh.shape

    def body(me_ref, g_ref, o_ref_in, out_ref):
        del me_ref
        acc = g_ref[...]
        for j in range(3):
            acc = acc + o_ref_in[j].astype(F32)
        out_ref[...] = acc

    return pl.pallas_call(
        body, name=name,
        grid_spec=pltpu.PrefetchScalarGridSpec(
            num_scalar_prefetch=1, grid=(hr // tm,),
            in_specs=[pl.BlockSpec((None, tm, C), lambda i, me: (me[0], i, 0)),
                      pl.BlockSpec((3, tm, C), lambda i, me: (0, i, 0))],
            out_specs=pl.BlockSpec((tm, C), lambda i, me: (i, 0))),
        out_shape=jax.ShapeDtypeStruct((hr, C), F32),
        compiler_params=_params(("parallel",)),
    )(_chip_index(), gh, others)


def _owner_sum(g, got, others, *, tm, name):
    S, R, C = g.shape
    hr = R // 2
    nt = hr // tm

    def body(idx_ref, g_ref, got_ref, o_ref_in, out_ref):
        del idx_ref
        acc = g_ref[...] + got_ref[...]
        for j in range(3):
            acc = acc + o_ref_in[j].astype(F32)
        out_ref[...] = acc

    return pl.pallas_call(
        body, name=name,
        grid_spec=pltpu.PrefetchScalarGridSpec(
            num_scalar_prefetch=1, grid=(nt,),
            in_specs=[pl.BlockSpec((None, tm, C), lambda i, p: (p[1], p[0] * nt + i, 0)),
                      pl.BlockSpec((None, tm, C), lambda i, p: (p[1], i, 0)),
                      pl.BlockSpec((3, tm, C), lambda i, p: (0, i, 0))],
            out_specs=pl.BlockSpec((tm, C), lambda i, p: (i, 0))),
        out_shape=jax.ShapeDtypeStruct((hr, C), F32),
        compiler_params=_params(("parallel",)),
    )(jnp.concatenate([_core_index(), _chip_index()]), g, got, others)


def _swap_halves(halves):
    nw = len(halves)

    def body(*refs):
        hs, outs = refs[:nw], refs[nw:2 * nw]
        send_sems, recv_sems = refs[2 * nw:]
        x, y, c, _ = _place()
        cps = []
        for w in range(nw):
            cp = _remote(hs[w], outs[w], send_sems.at[w], recv_sems.at[w], (x, y, 1 - c))
            cp.start()
            cps.append(cp)
        for cp in cps:
            cp.wait()

    any_spec = pl.BlockSpec(memory_space=pl.ANY)
    return pl.pallas_call(
        body, name="rs_swap_halves",
        in_specs=[any_spec] * nw, out_specs=[any_spec] * nw,
        out_shape=[jax.ShapeDtypeStruct(h.shape, h.dtype) for h in halves],
        scratch_shapes=[pltpu.SemaphoreType.DMA((nw,)), pltpu.SemaphoreType.DMA((nw,))],
    )(*halves)


def _all_reduce_small(packed):
    R, C = packed.shape

    def body(in_ref, out_ref, slots, send_sems, recv_sems):
        x, y, c, _ = _place()
        me = 4 * x + 2 * y + c
        cps = []
        for k in range(1, N_DEV):
            p = (me + k) % N_DEV
            cp = _remote(in_ref, slots.at[me], send_sems.at[k - 1], recv_sems.at[k - 1], (p // 4, (p // 2) % 2, p % 2))
            cp.start()
            cps.append(cp)
        slots[me] = in_ref[...]
        for k in range(1, N_DEV):
            src = (me + N_DEV - k) % N_DEV
            _remote(in_ref, slots.at[src], send_sems.at[k - 1], recv_sems.at[k - 1], (x, y, c)).wait_recv()
        for cp in cps:
            cp.wait_send()
        acc = slots[0]
        for d in range(1, N_DEV):
            acc = acc + slots[d]
        out_ref[...] = acc

    vmem = pl.BlockSpec(memory_space=pltpu.VMEM)
    return pl.pallas_call(
        body, name="all_reduce_small", in_specs=[vmem], out_specs=vmem,
        out_shape=jax.ShapeDtypeStruct((R, C), F32),
        scratch_shapes=[pltpu.VMEM((N_DEV, R, C), F32), pltpu.SemaphoreType.DMA((N_DEV - 1,)),
                        pltpu.SemaphoreType.DMA((N_DEV - 1,))],
        compiler_params=_params(),
    )(packed)


def _adamw_math(w, g, m, v):
    m = ADAM_B1 * m + (1.0 - ADAM_B1) * g
    v = ADAM_B2 * v + (1.0 - ADAM_B2) * (g * g)
    m_hat = m / (1.0 - ADAM_B1 ** ADAM_STEP)
    v_hat = v / (1.0 - ADAM_B2 ** ADAM_STEP)
    delta = -ADAM_LR * (m_hat / (jnp.sqrt(v_hat) + ADAM_EPS) + ADAM_WD * w)
    return delta, m, v


def _adamw(w, g, m, v, *, tm, name):
    R, C = w.shape

    def body(w_ref, g_ref, m_ref, v_ref, d_ref, nm_ref, nv_ref):
        d_ref[...], nm_ref[...], nv_ref[...] = _adamw_math(w_ref[...], g_ref[...], m_ref[...], v_ref[...])

    spec = pl.BlockSpec((tm, C), lambda i: (i, 0))
    return pl.pallas_call(
        body, name=name, grid=(R // tm,), in_specs=[spec] * 4, out_specs=[spec] * 3,
        out_shape=[jax.ShapeDtypeStruct((R, C), F32)] * 3, compiler_params=_params(("parallel",)),
    )(w, g, m, v)


def _adamw_halves(w, own, got, m, v, *, tm, name, rides=()):
    _, R, C = w.shape
    nt = (R // 2) // tm

    def body(w_ref, own_ref, got_ref, m_ref, v_ref, g_ref, d_ref, nm_ref, nv_ref):
        g = jnp.where(pl.program_id(0) == lax.axis_index("c"), own_ref[...], got_ref[...])
        g_ref[...] = g
        d_ref[...], nm_ref[...], nv_ref[...] = _adamw_math(w_ref[...], g, m_ref[...], v_ref[...])

    whole = pl.BlockSpec((None, tm, C), lambda h, i: (0, h * nt + i, 0))
    half = pl.BlockSpec((tm, C), lambda h, i: (i, 0))
    return _call(
        body, name=name, grid=(2, nt), in_specs=[whole, half, half, whole, whole], out_specs=[whole] * 4,
        out_shape=[jax.ShapeDtypeStruct((1, R, C), F32)] * 4, sem=("parallel", "parallel"), rides=rides,
    )(w, own, got, m, v)


def _adamw_small(w, slots, m, v, *, name):
    def body(w_ref, slots_ref, m_ref, v_ref, g_ref, d_ref, nm_ref, nv_ref):
        g = slots_ref[0]
        for d in range(1, N_DEV):
            g = g + slots_ref[d]
        g_ref[...] = g
        d_ref[...], nm_ref[...], nv_ref[...] = _adamw_math(w_ref[...], g, m_ref[...], v_ref[...])

    vmem = pl.BlockSpec(memory_space=pltpu.VMEM)
    return pl.pallas_call(
        body, name=name, in_specs=[vmem] * 4, out_specs=[vmem] * 4,
        out_shape=[jax.ShapeDtypeStruct(w.shape, F32)] * 4, compiler_params=_params(),
    )(w, slots, m, v)


SMALL = ["rel_bias_table", "mix_norm_g", "gate_norm_g", "gate_norm_b", "w_spatial", "b_spatial", "attn_sinks",
         "out_norm_a_g", "out_norm_b_g", "ffn_norm_g", "final_norm_g"]
SMALL_A = ["gate_norm_g", "gate_norm_b", "w_spatial", "b_spatial", "out_norm_a_g"]
SMALL_B = ["rel_bias_table", "mix_norm_g", "attn_sinks", "out_norm_b_g", "ffn_norm_g", "final_norm_g"]
LARGE = ["w_in", "w_out", "w_up", "w_down"]
ROW_TILE = {"w_in": 208, "w_out": 256, "w_up": 256, "w_down": 256}
WEIGHTS = ["rel_bias_table", "mix_norm_g", "w_in", "gate_norm_g", "gate_norm_b", "w_spatial", "b_spatial", "attn_sinks",
           "out_norm_a_g", "out_norm_b_g", "w_out", "ffn_norm_g", "w_up", "w_down", "final_norm_g"]
PACK_UNIT = 8 * 128


def _pack(parts):
    rows = []
    for p in parts:
        flat = p.reshape(-1)
        pad = (-flat.shape[0]) % PACK_UNIT
        rows.append(jnp.pad(flat, (0, pad)).reshape(-1, 128))
    return jnp.concatenate(rows, axis=0)


def _unpack(packed, like):
    out, row = [], 0
    for p in like:
        n = math.prod(p.shape)
        nrows = (n + PACK_UNIT - 1) // PACK_UNIT * 8
        out.append(packed[row:row + nrows].reshape(-1)[:n].reshape(p.shape))
        row += nrows
    return out


def kernel(x, rel_bias_table, mix_norm_g, w_in, gate_norm_g, gate_norm_b, w_spatial, b_spatial, attn_sinks, out_norm_a_g, out_norm_b_g, w_out, ffn_norm_g, w_up, w_down, final_norm_g, loss_target, m_rel_bias_table, m_mix_norm_g, m_w_in, m_gate_norm_g, m_gate_norm_b, m_w_spatial, m_b_spatial, m_attn_sinks, m_out_norm_a_g, m_out_norm_b_g, m_w_out, m_ffn_norm_g, m_w_up, m_w_down, m_final_norm_g, v_rel_bias_table, v_mix_norm_g, v_w_in, v_gate_norm_g, v_gate_norm_b, v_w_spatial, v_b_spatial, v_attn_sinks, v_out_norm_a_g, v_out_norm_b_g, v_w_out, v_ffn_norm_g, v_w_up, v_w_down, v_final_norm_g):
    args = dict(locals())
    wts = {n: args[n] for n in WEIGHTS}
    mom = {n: args["m_" + n] for n in WEIGHTS}
    var = {n: args["v_" + n] for n in WEIGHTS}
    sp = {n: wts[n] for n in SMALL}
    x2, tgt = x[0], loss_target[0]
    T = x2.shape[0]
    tm = min(512, T)
    tl = min(1024, T)
    tg = min(2048, T)
    lg = sp["gate_norm_g"].reshape(A_GROUPS, CHUNK)
    lb = sp["gate_norm_b"].reshape(A_GROUPS, CHUNK)
    wsp = sp["w_spatial"].reshape(A_GROUPS, CHUNK, CHUNK)
    bs_col = sp["b_spatial"].reshape(A_GROUPS, CHUNK, 1)
    sinks = sp["attn_sinks"].reshape(1, B_HEADS)
    ga = sp["out_norm_a_g"].reshape(1, A_WIDTH)
    gb = sp["out_norm_b_g"].reshape(1, B_WIDTH)
    g1 = sp["mix_norm_g"].reshape(1, D_MODEL)
    g2 = sp["ffn_norm_g"].reshape(1, D_MODEL)
    gf = sp["final_norm_g"].reshape(1, D_MODEL)

    def owner_total(n, gh, others):
        return _owner_total(gh, others, tm=ROW_TILE[n], name="rs_owner_total_" + n)

    def halves_view(at, shards):
        return at.reshape(shards, 2, at.shape[0] // shards // 2, at.shape[1])

    for d in (wts, mom, var):
        d["w_in"] = jnp.swapaxes(d["w_in"], 1, 2)

    s_in = _cast_into_slot(wts["w_in"], tm=ROW_TILE["w_in"], name="cast_w_in")
    s_up, ((s_in,),) = _cast_into_slot_carrying(wts["w_up"], tm=256, name="cast_w_up",
                                                rides=[_ride_gather(s_in, ici=(0, 1, 2))])
    s_down, ((s_in,),) = _cast_into_slot_carrying(wts["w_down"], tm=256, name="cast_w_down",
                                                  rides=[_ride_gather(s_in, ici=(1, 2, 2), d2d=(0, 1, 2))])
    s_out, ((g_in,),) = _cast_into_slot_carrying(wts["w_out"], tm=256, name="cast_w_out",
                                                 rides=[_ride_gather(s_in, d2d=(1, 2, 2))])
    win_t = g_in.reshape(PROJ_WIDTH, D_MODEL)
    bias = _bias_build(sp["rel_bias_table"])
    (n1, proj), ((s_out,), (s_up,)) = _norm_matmul_wide(
        x2, g1, win_t, tm=tm, tn=PROJ_WIDTH // 2, name="in_proj",
        rides=[_ride_gather(s_out, ici=(0, 1, 1)), _ride_gather(s_up, ici=(0, 2, 8))])
    (mixed, mixed_t, ab), ((g_out,), (s_up,), (n1_sib,)) = _mixer_fwd(
        proj, lg, lb, wsp, bs_col, sinks, bias, ga, gb,
        rides=[_ride_gather(s_out, d2d=(0, 1, 1)), _ride_gather(s_up, d2d=(0, 2, 8), ici=(2, 8, 8)),
               _ride_to_sibling(n1, first=True)])
    wo = g_out.reshape(A_WIDTH + B_WIDTH, D_MODEL)
    mixed_t = halves_view(mixed_t, N_CHIPS)
    h1, ((wu,), (s_down,), (mixed_t_sib,)) = _matmul_res(
        mixed, wo, x2, tm=tl, tn=1024, tk=D_MODEL, prologue=_to_bf16, name="out_proj",
        rides=[_ride_gather(s_up, d2d=(2, 8, 8)), _ride_gather(s_down, ici=(0, 2, 8)), _ride_to_sibling(mixed_t, halves=True)])
    (n2t, zp, z2, z2t), ((g_down,),) = _norm_matmul_sq(
        h1, g2, wu, tm=tl, tn=1024, name="up_proj", rides=[_ride_gather(s_down, d2d=(0, 2, 8), both=(2, 8, 8), mid_frac=0.75)])
    wd = g_down.reshape(D_FF, D_MODEL)
    n2t, z2t = halves_view(n2t, 1), halves_view(z2t, N_CHIPS)
    h2, ((n2t_sib,), (z2t_sib,)) = _matmul_res(
        z2, wd, h1, tm=tl, tn=1024, tk=4096, prologue=_to_bf16, name="down_proj",
        rides=[_ride_to_sibling(n2t, halves=True), _ride_to_sibling(z2t, halves=True)])

    dh2, dh2b, dgf, loss = _loss_bwd(h2, tgt, gf, tm=tm)
    dzp, ((dh2b_sib,),) = _matmul_nt(dh2b, wd, tm=tl, tn=1024, tk=D_MODEL, name="bwd_dz", extra=zp,
                                     epilogue=_sq_relu_grad, out_dtype=BF16, rides=[_ride_to_sibling(dh2b)])
    (gd, gdb), ((dzp_sib,),) = _grad_pair(z2t, z2t_sib, dh2b, dh2b_sib, cols_sharded=False, tmo=1024, tk=tl,
                                          name="grad_w_down", rides=[_ride_to_sibling(dzp)])
    (gu, gub), ((o_d,),) = _grad_pair(n2t, n2t_sib, dzp, dzp_sib, cols_sharded=True, tmo=1024, tk=tl,
                                      name="grad_w_up", rides=[_ride_scatter(gdb, None, (0, 7, 8))])
    dn2, ((o_d,), (o_u,)) = _matmul_nt(dzp, wu, tm=tl, tn=1024, tk=2048, name="bwd_dn2",
                                       rides=[_ride_scatter(gdb, o_d, (7, 8, 8)), _ride_scatter(gub, None, (0, 6, 8))])
    h_d = owner_total("w_down", gd, o_d)
    (dh1, dh1b, dg2), ((o_u,),) = _rms_bwd_res(dn2, h1, g2, dh2, tm=tm, name="ffn_norm_bwd",
                                               rides=[_ride_scatter(gub, o_u, (6, 7, 8))])
    dmixed, ((o_u,), (dh1b_sib,), (w_d,)) = _matmul_nt(
        dh1b, wo, tm=tl, tn=1024, tk=D_MODEL, name="bwd_dmixed",
        rides=[_ride_scatter(gub, o_u, (7, 8, 8)), _ride_to_sibling(dh1b), _ride_swap(h_d)])
    h_u = owner_total("w_up", gu, o_u)
    (go, gob), ((w_u,),) = _grad_pair(mixed_t, mixed_t_sib, dh1b, dh1b_sib, cols_sharded=False, tmo=256, tk=tl,
                                      name="grad_w_out", rides=[_ride_swap(h_u)])
    (duv, duv_t, dga, dwsp, dbs, dlg, dlb), ((o_o,),) = _gmlp_bwd(proj, ab, dmixed, ga, lg, lb, wsp, bs_col,
                                                                  rides=[_ride_scatter(gob)])
    h_o = owner_total("w_out", go, o_o)
    small = {"gate_norm_g": dlg, "gate_norm_b": dlb, "w_spatial": dwsp, "b_spatial": dbs, "out_norm_a_g": dga}
    hr_in = PROJ_WIDTH // N_CHIPS // 2
    (dq, dkv, dq_t, dkv_t, dgb, dsinks, dbias), ((w_o,), (dproj_t_sib,)) = _attn_bwd(
        proj, ab, dmixed, gb, sinks, bias, rides=[_ride_swap(h_o), _ride_rows_to_sibling(duv_t, hr_in, 2, N_CHIPS)])
    dtable = _bias_grad(dbias)
    dproj_t = halves_view(jnp.concatenate([duv_t, dq_t, dkv_t], axis=0), N_CHIPS)
    ((dproj_t_sib,),) = _carrier([_ride_to_sibling(dproj_t, halves=True, shards=(2, N_CHIPS), land=dproj_t_sib)],
                                 name="trade_dproj_t")
    (gi, gib), ((slots_a,),) = _grad_pair(
        dproj_t, dproj_t_sib, n1, n1_sib, cols_sharded=False, tmo=hr_in, tk=tl, name="grad_w_in",
        rides=[_ride_small_to_all(_pack([small[n] for n in SMALL_A]))])
    dn1, ((o_i,),) = _matmul_parts([duv, dq, dkv], win_t, tm=tl, tn=1024, name="bwd_dn1", rides=[_ride_scatter(gib)])
    h_i = owner_total("w_in", gi, o_i)
    dx, _, dg1 = _rms_bwd_res(dn1, x2, g1, dh1, tm=tm, name="mix_norm_bwd")
    small.update({"rel_bias_table": dtable.reshape(N_BUCKETS, B_HEADS), "mix_norm_g": dg1, "attn_sinks": dsinks,
                  "out_norm_b_g": dgb, "ffn_norm_g": dg2, "final_norm_g": dgf})
    (w_i,), (slots_b,) = _carrier([_ride_swap(h_i), _ride_small_to_all(_pack([small[n] for n in SMALL_B] + [loss]))],
                                  name="swap_w_in")

    out_g, out_d, out_m, out_v = {}, {}, {}, {}
    for n, h, s in zip(LARGE, [h_i, h_o, h_u, h_d], [w_i, w_o, w_u, w_d]):
        res = _adamw_halves(wts[n], h, s, mom[n], var[n], tm=ROW_TILE[n], name="adamw_" + n)
        if n == "w_in":
            res = [jnp.swapaxes(r, 1, 2) for r in res]
        out_g[n], out_d[n], out_m[n], out_v[n] = res
    for names, slots, tag in ((SMALL_A, slots_a, "a"), (SMALL_B, slots_b, "b")):
        extra = [jnp.zeros((1, 1), F32)] if tag == "b" else []
        like = [wts[n] for n in names] + extra
        res = _adamw_small(_pack(like), slots, _pack([mom[n] for n in names] + extra),
                           _pack([var[n] for n in names] + extra), name="adamw_small_" + tag)
        for store, packed in zip((out_g, out_d, out_m, out_v), res):
            for n, val in zip(names + ["loss"], _unpack(packed, like)):
                store[n] = val

    total = out_g["loss"][0, 0]
    return (total, dx[None], *[out_g[n] for n in WEIGHTS], *[out_d[n] for n in WEIGHTS],
            *[out_m[n] for n in WEIGHTS], *[out_v[n] for n in WEIGHTS])
```

```python
import functools
import math

import numpy as np
import jax
import jax.numpy as jnp
from jax import lax
from jax.experimental import pallas as pl
from jax.experimental.pallas import tpu as pltpu

F32 = jnp.float32
BF16 = jnp.bfloat16

D_MODEL = 2048
CHUNK = 128
A_GROUPS = 8
A_WIDTH = 1024
HEAD_DIM = 64
B_HEADS = 16
Q_PER_KV = 8
B_WIDTH = 1024
KV_WIDTH = 128
PROJ_WIDTH = 3328
D_FF = 8192
N_BUCKETS = 32
EPS = 1e-5
NEG = -1e30
SCALE = HEAD_DIM ** -0.5
N_CHIPS = 4
N_DEV = 8

ADAM_LR = 0.001
ADAM_B1 = 0.9
ADAM_B2 = 0.999
ADAM_EPS = 1e-08
ADAM_WD = 0.01
ADAM_STEP = 10

VMEM_LIMIT = 60 * 1024 * 1024
MESH = pl.DeviceIdType.MESH


def _bucket_thresholds():
    d = np.arange(CHUNK)
    n_exact = N_BUCKETS // 2
    relf = np.maximum(d, n_exact).astype(np.float64)
    large = n_exact + (np.log(relf / n_exact) / math.log(CHUNK / n_exact) * (N_BUCKETS - n_exact)).astype(np.int32)
    bucket = np.where(d < n_exact, d, np.minimum(large, N_BUCKETS - 1))
    return [int(np.min(d[bucket >= b])) for b in range(1, N_BUCKETS)]


BUCKET_THR = _bucket_thresholds()


def _params(sem=None):
    return pltpu.CompilerParams(dimension_semantics=sem, vmem_limit_bytes=VMEM_LIMIT)


def _gelu(x):
    c = math.sqrt(2.0 / math.pi)
    return 0.5 * x * (1.0 + jnp.tanh(c * (x + 0.044715 * (x * x * x))))


def _gelu_and_grad(x):
    c = math.sqrt(2.0 / math.pi)
    x2 = x * x
    t = jnp.tanh(c * (x + 0.044715 * (x2 * x)))
    g = 0.5 * x * (1.0 + t)
    dg = 0.5 * (1.0 + t) + 0.5 * x * (1.0 - t * t) * (c * (1.0 + 3.0 * 0.044715 * x2))
    return g, dg


def _dot(a, b):
    return jnp.dot(a, b, preferred_element_type=F32)


def _dot_nt(a, b):
    return lax.dot_general(a, b, (((1,), (1,)), ((), ())), preferred_element_type=F32)


def _dot_tn(a, b):
    return lax.dot_general(a, b, (((0,), (0,)), ((), ())), preferred_element_type=F32)


def _rms_bwd(dn, h, g):
    r = lax.rsqrt(jnp.mean(h * h, axis=-1, keepdims=True) + EPS)
    w = dn * g
    dh = r * w - h * ((r * r * r) * jnp.mean(w * h, axis=-1, keepdims=True))
    return dh, r


def _place():
    x, y, c = lax.axis_index("x"), lax.axis_index("y"), lax.axis_index("c")
    chips = [(1 - x, y), (x, 1 - y), (1 - x, 1 - y)]
    return x, y, c, chips


def _remote(src, dst, send_sem, recv_sem, to):
    return pltpu.make_async_remote_copy(src_ref=src, dst_ref=dst, send_sem=send_sem, recv_sem=recv_sem,
                                        device_id=to, device_id_type=MESH)


class _Ride:
    def __init__(self, args, out_shape, n_sem, start, finish, mid=None, mid_frac=0.8, aliases=None):
        self.args, self.out_shape, self.n_sem = list(args), list(out_shape), n_sem
        self.start, self.mid, self.finish, self.mid_frac = start, mid, finish, mid_frac
        self.aliases = dict(aliases or {})


def _call(body, *, name, grid, in_specs, out_specs, out_shape, scratch_shapes=(), sem=None, rides=()):
    single = not isinstance(out_shape, (list, tuple))
    out_specs = [out_specs] if single else list(out_specs)
    out_shape = [out_shape] if single else list(out_shape)
    n_in, n_out, n_scr = len(in_specs), len(out_shape), len(scratch_shapes)
    r_in = [len(r.args) for r in rides]
    r_out = [len(r.out_shape) for r in rides]
    any_spec = pl.BlockSpec(memory_space=pl.ANY)
    aliases, off_i, off_o = {}, n_in, n_out
    for r in rides:
        for i, o in r.aliases.items():
            aliases[off_i + i] = off_o + o
        off_i += len(r.args)
        off_o += len(r.out_shape)
    steps = math.prod(grid)

    def wrapped(*refs):
        p = 0
        ins = refs[p:p + n_in]; p += n_in
        rins = refs[p:p + sum(r_in)]; p += sum(r_in)
        outs = refs[p:p + n_out]; p += n_out
        routs = refs[p:p + sum(r_out)]; p += sum(r_out)
        scr = refs[p:p + n_scr]; p += n_scr
        sems = refs[p:]
        parts, pi, po = [], 0, 0
        for k, r in enumerate(rides):
            parts.append((rins[pi:pi + r_in[k]], routs[po:po + r_out[k]], sems[2 * k], sems[2 * k + 1]))
            pi += r_in[k]
            po += r_out[k]
        lin = 0
        for d in range(len(grid)):
            lin = lin * grid[d] + pl.program_id(d)
        if rides:
            @pl.when(lin == 0)
            def _():
                for r, part in zip(rides, parts):
                    r.start(*part)
        body(*ins, *outs, *scr)
        for r, part in zip(rides, parts):
            if r.mid is not None:
                @pl.when(lin == min(steps - 1, int(r.mid_frac * steps)))
                def _(r=r, part=part):
                    r.mid(*part)
        if rides:
            @pl.when(lin == steps - 1)
            def _():
                for r, part in zip(rides, parts):
                    r.finish(*part)

    scratch = list(scratch_shapes)
    for r in rides:
        scratch += [pltpu.SemaphoreType.DMA((r.n_sem,)), pltpu.SemaphoreType.DMA((r.n_sem,))]
    if rides:
        sem = ("arbitrary",) * len(grid)
    res = pl.pallas_call(
        wrapped, name=name, grid=grid,
        in_specs=list(in_specs) + [any_spec] * sum(r_in),
        out_specs=out_specs + [any_spec] * sum(r_out),
        out_shape=out_shape + [s for r in rides for s in r.out_shape],
        scratch_shapes=scratch, input_output_aliases=aliases,
        compiler_params=_params(sem),
    )

    def run(*args):
        got = res(*args, *[a for r in rides for a in r.args])
        mine = got[0] if single else list(got[:n_out])
        if not rides:
            return mine
        rest, out = list(got[n_out:]), []
        for k in range(len(rides)):
            out.append(rest[:r_out[k]])
            rest = rest[r_out[k]:]
        return mine, out

    return run


def _ride_gather(slot, ici=None, d2d=None, both=None, mid_frac=0.8):
    half = slot.shape[1] // 2

    def rows(part, c):
        k0, k1, n = part
        return pl.ds(c * half + k0 * (half // n), (k1 - k0) * (half // n))

    def ici_copies(outs, ss, rs, part, base):
        x, y, c, chips = _place()
        mine = outs[0].at[2 * x + y, rows(part, c), :]
        return [_remote(mine, mine, ss.at[base + j], rs.at[base + j], (*chip, c)) for j, chip in enumerate(chips)]

    def d2d_copies(outs, ss, rs, part, base):
        x, y, c, chips = _place()
        return [_remote(outs[0].at[2 * chip[0] + chip[1], rows(part, c), :], outs[0].at[2 * chip[0] + chip[1], rows(part, c), :],
                        ss.at[base + j], rs.at[base + j], (x, y, 1 - c)) for j, chip in enumerate(chips)]

    def arrivals(outs, ss, rs, part, base, from_sibling):
        x, y, c, chips = _place()
        for j, chip in enumerate(chips):
            dst = outs[0].at[2 * chip[0] + chip[1], rows(part, 1 - c if from_sibling else c), :]
            _remote(dst, dst, ss.at[base + j], rs.at[base + j], (x, y, c)).wait_recv()

    def start(ins, outs, ss, rs):
        for part, base in ((ici, 0), (both, 6)):
            if part is not None:
                for cp in ici_copies(outs, ss, rs, part, base):
                    cp.start()
        if d2d is not None:
            for cp in d2d_copies(outs, ss, rs, d2d, 3):
                cp.start()

    def mid(ins, outs, ss, rs):
        arrivals(outs, ss, rs, both, 6, False)
        for cp in d2d_copies(outs, ss, rs, both, 9):
            cp.start()

    def finish(ins, outs, ss, rs):
        if ici is not None:
            arrivals(outs, ss, rs, ici, 0, False)
        if d2d is not None:
            arrivals(outs, ss, rs, d2d, 3, True)
        if both is not None:
            arrivals(outs, ss, rs, both, 9, True)
        for part, base, fn in ((ici, 0, ici_copies), (d2d, 3, d2d_copies), (both, 6, ici_copies), (both, 9, d2d_copies)):
            if part is not None:
                for cp in fn(outs, ss, rs, part, base):
                    cp.wait_send()

    return _Ride([slot], [jax.ShapeDtypeStruct(slot.shape, slot.dtype)], 12, start, finish,
                 mid=mid if both is not None else None, mid_frac=mid_frac, aliases={0: 0})


def _ride_sibling_halves(g):
    S, R, C = g.shape
    hr = R // 2

    def copy(ins, outs, ss, rs):
        x, y, c, _ = _place()
        return _remote(ins[0].at[:, pl.ds((1 - c) * hr, hr), :], outs[0], ss.at[0], rs.at[0], (x, y, 1 - c))

    return _Ride([g], [jax.ShapeDtypeStruct((S, hr, C), g.dtype)], 1,
                 lambda *a: copy(*a).start(), lambda *a: copy(*a).wait())


def _ride_scatter(q, land=None, part=(0, 1)):
    k0, k1, n = part if len(part) == 3 else (part[0], part[0] + 1, part[1])
    rows_n = q.shape[1] // n
    rows = pl.ds(k0 * rows_n, (k1 - k0) * rows_n)

    def copies(ins, outs, ss, rs):
        x, y, c, chips = _place()
        return [_remote(ins[0].at[2 * chip[0] + chip[1], rows, :], outs[0].at[j, rows, :], ss.at[j], rs.at[j], (*chip, c))
                for j, chip in enumerate(chips)]

    def start(*a):
        for cp in copies(*a):
            cp.start()

    def finish(*a):
        for cp in copies(*a):
            cp.wait()

    shape = jax.ShapeDtypeStruct((3,) + q.shape[1:], q.dtype)
    if land is None:
        return _Ride([q], [shape], 3, start, finish)
    return _Ride([q, land], [shape], 3, start, finish, aliases={1: 0})


def _ride_to_sibling(a, halves=False, first=False, shards=None, land=None):
    s0, s1 = shards or (0, a.shape[0])

    def copy(ins, outs, ss, rs):
        x, y, c, _ = _place()
        if halves:
            src, dst = ins[0].at[s0:s1, 1 - c], outs[0].at[s0:s1]
        else:
            src, dst = (ins[0].at[0] if first else ins[0]), outs[0]
        return _remote(src, dst, ss.at[0], rs.at[0], (x, y, 1 - c))

    shape = (a.shape[0],) + a.shape[2:] if halves else (a.shape[1:] if first else a.shape)
    return _Ride([a] if land is None else [a, land], [jax.ShapeDtypeStruct(shape, a.dtype)], 1,
                 lambda *a_: copy(*a_).start(), lambda *a_: copy(*a_).wait(), aliases=None if land is None else {1: 0})


def _ride_rows_to_sibling(a, hr, shards, total):
    def copies(ins, outs, ss, rs):
        x, y, c, _ = _place()
        return [_remote(ins[0].at[pl.ds((2 * s + 1 - c) * hr, hr), :], outs[0].at[s], ss.at[s], rs.at[s], (x, y, 1 - c))
                for s in range(shards)]

    def start(*a_):
        for cp in copies(*a_):
            cp.start()

    def finish(*a_):
        for cp in copies(*a_):
            cp.wait()

    return _Ride([a], [jax.ShapeDtypeStruct((total, hr, a.shape[1]), a.dtype)], shards, start, finish)


def _ride_swap(h):
    def copy(ins, outs, ss, rs):
        x, y, c, _ = _place()
        return _remote(ins[0], outs[0], ss.at[0], rs.at[0], (x, y, 1 - c))

    return _Ride([h], [jax.ShapeDtypeStruct(h.shape, h.dtype)], 1,
                 lambda *a: copy(*a).start(), lambda *a: copy(*a).wait())


def _mesh_place(p):
    return (p // 4, (p // 2) % 2, p % 2)


def _ride_small_to_all(packed):
    def copies(ins, outs, ss, rs):
        x, y, c, _ = _place()
        me = 4 * x + 2 * y + c
        return [_remote(ins[0], outs[0].at[me], ss.at[k - 1], rs.at[k - 1], _mesh_place((me + k) % N_DEV))
                for k in range(1, N_DEV)]

    def own(ins, outs, ss, rs):
        x, y, c, _ = _place()
        return pltpu.make_async_copy(ins[0], outs[0].at[4 * x + 2 * y + c], ss.at[N_DEV - 1])

    def start(*a):
        own(*a).start()
        for cp in copies(*a):
            cp.start()

    def finish(ins, outs, ss, rs):
        x, y, c, _ = _place()
        me = 4 * x + 2 * y + c
        for k in range(1, N_DEV):
            _remote(ins[0], outs[0].at[(me + N_DEV - k) % N_DEV], ss.at[k - 1], rs.at[k - 1], (x, y, c)).wait_recv()
        for cp in copies(ins, outs, ss, rs):
            cp.wait_send()
        own(ins, outs, ss, rs).wait()

    return _Ride([packed], [jax.ShapeDtypeStruct((N_DEV,) + packed.shape, packed.dtype)], N_DEV, start, finish)


def _carrier(rides, *, name):
    _, outs = _call(lambda: None, name=name, grid=(1,), in_specs=[], out_specs=[], out_shape=[], rides=rides)()
    return outs


def _sq_relu_bf16(z):
    z = jnp.maximum(z, 0.0)
    return (z * z).astype(BF16)


def _norm_bf16(a_ref, g_ref):
    xf = a_ref[...]
    r = lax.rsqrt(jnp.mean(xf * xf, axis=-1, keepdims=True) + EPS)
    return ((xf * r) * g_ref[...]).astype(BF16)


def _norm_matmul_wide(a, g, b, *, tm, tn, name, rides=()):
    T, K = a.shape
    N = b.shape[0]

    def body(a_ref, g_ref, b_ref, n_ref, o_ref):
        n = _norm_bf16(a_ref, g_ref)
        n_ref[...] = n
        o_ref[...] = _dot_nt(n, b_ref[...])

    return _call(
        body, name=name, grid=(N // tn, T // tm),
        in_specs=[pl.BlockSpec((tm, K), lambda j, i: (i, 0)), pl.BlockSpec((1, K), lambda j, i: (0, 0)),
                  pl.BlockSpec((tn, K), lambda j, i: (j, 0))],
        out_specs=[pl.BlockSpec((None, tm, K), lambda j, i: (j, i, 0)), pl.BlockSpec((tm, tn), lambda j, i: (i, j))],
        out_shape=[jax.ShapeDtypeStruct((N // tn, T, K), BF16), jax.ShapeDtypeStruct((T, N), F32)],
        sem=("arbitrary", "arbitrary"), rides=rides,
    )(a, g, b)


def _norm_matmul_sq(a, g, b, *, tm, tn, name, rides=()):
    T, K = a.shape
    per = b.shape[2] // tn
    N = b.shape[0] * b.shape[2]

    def body(a_ref, g_ref, b_ref, nt_ref, o_ref, z_ref, zt_ref, n_scr):
        @pl.when(pl.program_id(1) == 0)
        def _():
            n = _norm_bf16(a_ref, g_ref)
            n_scr[...] = n
            nt_ref[...] = n.T
        r = jnp.maximum(_dot(n_scr[...], b_ref[...]), 0.0)
        o_ref[...] = r.astype(BF16)
        z = (r * r).astype(BF16)
        z_ref[...] = z
        zt_ref[...] = z.T

    return _call(
        body, name=name, grid=(T // tm, N // tn),
        in_specs=[pl.BlockSpec((tm, K), lambda i, j: (i, 0)), pl.BlockSpec((1, K), lambda i, j: (0, 0)),
                  pl.BlockSpec((None, K, tn), lambda i, j: (j // per, 0, j % per))],
        out_specs=[pl.BlockSpec((K, tm), lambda i, j: (0, i)), pl.BlockSpec((tm, tn), lambda i, j: (i, j)),
                   pl.BlockSpec((tm, tn), lambda i, j: (i, j)), pl.BlockSpec((tn, tm), lambda i, j: (j, i))],
        out_shape=[jax.ShapeDtypeStruct((K, T), BF16), jax.ShapeDtypeStruct((T, N), BF16),
                   jax.ShapeDtypeStruct((T, N), BF16), jax.ShapeDtypeStruct((N, T), BF16)],
        scratch_shapes=[pltpu.VMEM((tm, K), BF16)],
        sem=("parallel", "arbitrary"), rides=rides,
    )(a, g, b)


def _grad_pair(at, at_sib, b, b_sib, *, cols_sharded, tmo, tk, name, rides=()):
    S, _, hr, T = at.shape
    C = b.shape[-1] // N_CHIPS if cols_sharded else b.shape[-1]
    nk = T // tk
    a_sel = (lambda s: 0) if cols_sharded else (lambda s: s)
    b_sel = (lambda s: s) if cols_sharded else (lambda s: 0)
    if b.ndim == 3:
        b_spec = pl.BlockSpec((None, tk, C), lambda s, i, k: (0, k, b_sel(s)))
    else:
        b_spec = pl.BlockSpec((tk, C), lambda s, i, k: (k, b_sel(s)))

    def body(a_ref, as_ref, b_ref, bs_ref, o_ref, ob_ref):
        k = pl.program_id(2)
        p = _dot(a_ref[...], b_ref[...]) + _dot(as_ref[...], bs_ref[...])

        @pl.when(k == 0)
        def _():
            o_ref[...] = p

        @pl.when(k > 0)
        def _():
            o_ref[...] += p

        @pl.when(k == nk - 1)
        def _():
            ob_ref[...] = o_ref[...].astype(BF16)

    out = pl.BlockSpec((None, tmo, C), lambda s, i, k: (s, i, 0))
    return _call(
        body, name=name, grid=(N_CHIPS, hr // tmo, nk),
        in_specs=[pl.BlockSpec((None, None, tmo, tk), lambda s, i, k: (a_sel(s), lax.axis_index("c"), i, k)),
                  pl.BlockSpec((None, tmo, tk), lambda s, i, k: (a_sel(s), i, k)),
                  b_spec, pl.BlockSpec((tk, C), lambda s, i, k: (k, b_sel(s)))],
        out_specs=[out, out],
        out_shape=[jax.ShapeDtypeStruct((N_CHIPS, hr, C), F32), jax.ShapeDtypeStruct((N_CHIPS, hr, C), BF16)],
        sem=("parallel", "parallel", "arbitrary"), rides=rides,
    )(at, at_sib, b, b_sib)


def _grad_pair_merged(at, at_sib, b, b_sib, *, tk, name, rides=()):
    S, _, hr, T = at.shape
    C = b.shape[-1]
    nk = T // tk

    def body(a_ref, as_ref, b_ref, bs_ref, o_ref, ob_ref):
        k = pl.program_id(0)
        p = (_dot(a_ref[...].reshape(S * hr, tk), b_ref[...])
             + _dot(as_ref[...].reshape(S * hr, tk), bs_ref[...])).reshape(S, hr, C)

        @pl.when(k == 0)
        def _():
            o_ref[...] = p

        @pl.when(k > 0)
        def _():
            o_ref[...] += p

        @pl.when(k == nk - 1)
        def _():
            ob_ref[...] = o_ref[...].astype(BF16)

    out = pl.BlockSpec((S, hr, C), lambda k: (0, 0, 0))
    return _call(
        body, name=name, grid=(nk,),
        in_specs=[pl.BlockSpec((S, None, hr, tk), lambda k: (0, lax.axis_index("c"), 0, k)),
                  pl.BlockSpec((S, hr, tk), lambda k: (0, 0, k)),
                  pl.BlockSpec((tk, C), lambda k: (k, 0)), pl.BlockSpec((tk, C), lambda k: (k, 0))],
        out_specs=[out, out],
        out_shape=[jax.ShapeDtypeStruct((S, hr, C), F32), jax.ShapeDtypeStruct((S, hr, C), BF16)],
        sem=("arbitrary",), rides=rides,
    )(at, at_sib, b, b_sib)


def _matmul_nn(at, b, *, tmo, tn, tk, name, shards=1, rides=()):
    M, T = at.shape[-2:]
    N = b.shape[-1]
    if at.ndim == 3:
        a_spec = pl.BlockSpec((None, tmo, tk), lambda i, j, k: (0, i, k))
    else:
        a_spec = pl.BlockSpec((tmo, tk), lambda i, j, k: (i, k))
    if b.ndim == 3:
        b_spec = pl.BlockSpec((None, tk, tn), lambda i, j, k: (0, k, j))
    else:
        b_spec = pl.BlockSpec((tk, tn), lambda i, j, k: (k, j))
    if shards > 1:
        per = (N // shards) // tn
        out_spec = pl.BlockSpec((None, tmo, tn), lambda i, j, k: (j // per, i, j % per))
        out_shape = jax.ShapeDtypeStruct((shards, M, N // shards), F32)
    else:
        out_spec = pl.BlockSpec((tmo, tn), lambda i, j, k: (i, j))
        out_shape = jax.ShapeDtypeStruct((M, N), F32)

    def body(a_ref, b_ref, o_ref):
        k = pl.program_id(2)
        p = _dot(a_ref[...], b_ref[...])

        @pl.when(k == 0)
        def _():
            o_ref[...] = p

        @pl.when(k > 0)
        def _():
            o_ref[...] += p

    return _call(
        body, name=name, grid=(M // tmo, N // tn, T // tk),
        in_specs=[a_spec, b_spec],
        out_specs=out_spec, out_shape=out_shape,
        sem=("parallel", "parallel", "arbitrary"), rides=rides,
    )(at, b)


def _matmul_parts(parts, b, *, tm, tn, name, rides=()):
    T = parts[0].shape[0]
    N = b.shape[1]
    offs = [sum(p.shape[1] for p in parts[:i]) for i in range(len(parts))]
    assert all(o % p.shape[1] == 0 for o, p in zip(offs, parts))

    def body(*refs):
        n = len(parts)
        acc = _dot(refs[0][...], refs[n][...])
        for i in range(1, n):
            acc = acc + _dot(refs[i][...], refs[n + i][...])
        refs[-1][...] = acc

    a_specs = [pl.BlockSpec((tm, p.shape[1]), lambda i, j: (i, 0)) for p in parts]
    b_specs = [pl.BlockSpec((p.shape[1], tn), lambda i, j, r=o // p.shape[1]: (r, j)) for o, p in zip(offs, parts)]
    return _call(
        body, name=name, grid=(T // tm, N // tn), in_specs=a_specs + b_specs,
        out_specs=pl.BlockSpec((tm, tn), lambda i, j: (i, j)), out_shape=jax.ShapeDtypeStruct((T, N), F32),
        sem=("parallel", "parallel"), rides=rides,
    )(*parts, *([b] * len(parts)))


def _to_bf16(v):
    return v.astype(BF16)


def _matmul_res(a, b, res, *, tm, tn, tk, prologue, name, rides=()):
    T, K = a.shape
    N = b.shape[1]

    def body(a_ref, b_ref, res_ref, o_ref):
        k = pl.program_id(2)
        p = _dot(prologue(a_ref[...]), b_ref[...])

        @pl.when(k == 0)
        def _():
            o_ref[...] = res_ref[...] + p

        @pl.when(k > 0)
        def _():
            o_ref[...] += p

    return _call(
        body, name=name, grid=(T // tm, N // tn, K // tk),
        in_specs=[pl.BlockSpec((tm, tk), lambda i, j, k: (i, k)), pl.BlockSpec((tk, tn), lambda i, j, k: (k, j)),
                  pl.BlockSpec((tm, tn), lambda i, j, k: (i, j))],
        out_specs=pl.BlockSpec((tm, tn), lambda i, j, k: (i, j)),
        out_shape=jax.ShapeDtypeStruct((T, N), F32),
        sem=("parallel", "parallel", "arbitrary"), rides=rides,
    )(a, b, res)


def _matmul_nt(a, b, *, tm, tn, tk, name, extra=None, epilogue=None, out_dtype=F32, rides=()):
    T, K = a.shape
    two = b.ndim == 3 and tk == 2 * b.shape[2]
    if two:
        N, ks = b.shape[1], b.shape[2]
        b_specs = [pl.BlockSpec((None, tn, ks), lambda i, j, k: (2 * k, j, 0)),
                   pl.BlockSpec((None, tn, ks), lambda i, j, k: (2 * k + 1, j, 0))]
    elif b.ndim == 3:
        per = b.shape[2] // tk
        N = b.shape[1]
        b_specs = [pl.BlockSpec((None, tn, tk), lambda i, j, k: (k // per, j, k % per))]
    else:
        N = b.shape[0]
        b_specs = [pl.BlockSpec((tn, tk), lambda i, j, k: (j, k))]
    nb = len(b_specs)
    nk = K // tk
    assert out_dtype == F32 or nk == 1
    in_specs = [pl.BlockSpec((tm, tk), lambda i, j, k: (i, k))] + b_specs
    args = [a] + [b] * nb
    if extra is not None:
        in_specs.append(pl.BlockSpec((tm, tn), lambda i, j, k: (i, j)))
        args.append(extra)

    def body(*refs):
        a_ref, b_ref = refs[0], refs[1]
        o_ref = refs[-1]
        if two:
            p = (_dot_nt(a_ref[:, :tk // 2].astype(BF16), refs[1][...])
                 + _dot_nt(a_ref[:, tk // 2:].astype(BF16), refs[2][...]))
        else:
            p = _dot_nt(a_ref[...].astype(BF16), b_ref[...])
        if nk == 1:
            if epilogue is not None:
                p = epilogue(p, refs[1 + nb][...])
            o_ref[...] = p.astype(out_dtype)
        else:
            k = pl.program_id(2)

            @pl.when(k == 0)
            def _():
                o_ref[...] = p

            @pl.when(k > 0)
            def _():
                o_ref[...] += p

    return _call(
        body, name=name, grid=(T // tm, N // tn, nk),
        in_specs=in_specs,
        out_specs=pl.BlockSpec((tm, tn), lambda i, j, k: (i, j)),
        out_shape=jax.ShapeDtypeStruct((T, N), out_dtype),
        sem=("parallel", "parallel", "arbitrary"), rides=rides,
    )(*args)


def _matmul_tn(a, b, *, tmo, tn, tk, name, a_prologue=_to_bf16, shards=1, rides=()):
    T, M = a.shape
    N = b.shape[1]
    if shards > 1:
        per = (N // shards) // tn
        out_spec = pl.BlockSpec((None, tmo, tn), lambda i, j, k: (j // per, i, j % per))
        out_shape = jax.ShapeDtypeStruct((shards, M, N // shards), F32)
    else:
        out_spec = pl.BlockSpec((tmo, tn), lambda i, j, k: (i, j))
        out_shape = jax.ShapeDtypeStruct((M, N), F32)

    def body(a_ref, b_ref, o_ref):
        k = pl.program_id(2)
        p = _dot_tn(a_prologue(a_ref[...]), b_ref[...].astype(BF16))

        @pl.when(k == 0)
        def _():
            o_ref[...] = p

        @pl.when(k > 0)
        def _():
            o_ref[...] += p

    return _call(
        body, name=name, grid=(M // tmo, N // tn, T // tk),
        in_specs=[pl.BlockSpec((tk, tmo), lambda i, j, k: (k, i)), pl.BlockSpec((tk, tn), lambda i, j, k: (k, j))],
        out_specs=out_spec, out_shape=out_shape,
        sem=("parallel", "parallel", "arbitrary"), rides=rides,
    )(a, b)


def _loss_bwd(h2, tgt, g, *, tm):
    T, D = h2.shape

    def body(h_ref, t_ref, g_ref, dh_ref, dhb_ref, dg_ref, loss_ref):
        @pl.when(pl.program_id(0) == 0)
        def _():
            dg_ref[...] = jnp.zeros_like(dg_ref)
            loss_ref[...] = jnp.zeros_like(loss_ref)
        h = h_ref[...]
        gg = g_ref[...]
        r = lax.rsqrt(jnp.mean(h * h, axis=-1, keepdims=True) + EPS)
        hn = h * r
        err = hn * gg - t_ref[...]
        loss_ref[...] += 0.5 * jnp.sum(jnp.mean(err * err, axis=-1, keepdims=True), axis=0, keepdims=True)
        dy = err * (1.0 / D)
        dg_ref[...] += jnp.sum(dy * hn, axis=0, keepdims=True)
        w = dy * gg
        dh = r * w - h * ((r * r * r) * jnp.mean(w * h, axis=-1, keepdims=True))
        dh_ref[...] = dh
        dhb_ref[...] = dh.astype(BF16)

    tile = pl.BlockSpec((tm, D), lambda i: (i, 0))
    return pl.pallas_call(
        body, name="loss_bwd", grid=(T // tm,),
        in_specs=[tile, tile, pl.BlockSpec((1, D), lambda i: (0, 0))],
        out_specs=[tile, tile, pl.BlockSpec((1, D), lambda i: (0, 0)), pl.BlockSpec((1, 1), lambda i: (0, 0))],
        out_shape=[jax.ShapeDtypeStruct((T, D), F32), jax.ShapeDtypeStruct((T, D), BF16),
                   jax.ShapeDtypeStruct((1, D), F32), jax.ShapeDtypeStruct((1, 1), F32)],
        compiler_params=_params(("arbitrary",)),
    )(h2, tgt, g)


def _rms_bwd_res(dn, h, g, dres, *, tm, name, rides=()):
    T, D = h.shape

    def body(dn_ref, h_ref, g_ref, dres_ref, dh_ref, dhb_ref, dg_ref):
        @pl.when(pl.program_id(0) == 0)
        def _():
            dg_ref[...] = jnp.zeros_like(dg_ref)
        h_ = h_ref[...]
        dn_ = dn_ref[...]
        dh, r = _rms_bwd(dn_, h_, g_ref[...])
        dg_ref[...] += jnp.sum(dn_ * (h_ * r), axis=0, keepdims=True)
        dh = dres_ref[...] + dh
        dh_ref[...] = dh
        dhb_ref[...] = dh.astype(BF16)

    tile = pl.BlockSpec((tm, D), lambda i: (i, 0))
    return _call(
        body, name=name, grid=(T // tm,),
        in_specs=[tile, tile, pl.BlockSpec((1, D), lambda i: (0, 0)), tile],
        out_specs=[tile, tile, pl.BlockSpec((1, D), lambda i: (0, 0))],
        out_shape=[jax.ShapeDtypeStruct((T, D), F32), jax.ShapeDtypeStruct((T, D), BF16),
                   jax.ShapeDtypeStruct((1, D), F32)],
        sem=("arbitrary",), rides=rides,
    )(dn, h, g, dres)


def _rel_distance():
    i = lax.broadcasted_iota(jnp.int32, (CHUNK, 2 * CHUNK), 0)
    j = lax.broadcasted_iota(jnp.int32, (CHUNK, 2 * CHUNK), 1)
    return i + CHUNK - j


def _bias_build(table):
    def body(tab_ref, o_ref):
        rel = _rel_distance()
        ge = [rel >= t for t in BUCKET_THR]
        for h in range(B_HEADS):
            cur = jnp.full((CHUNK, 2 * CHUNK), tab_ref[0, h], F32)
            for b in range(1, N_BUCKETS):
                cur = jnp.where(ge[b - 1], tab_ref[b, h], cur)
            o_ref[h] = cur

    return pl.pallas_call(
        body, name="bias_build",
        in_specs=[pl.BlockSpec(memory_space=pltpu.SMEM)],
        out_specs=pl.BlockSpec(memory_space=pltpu.VMEM),
        out_shape=jax.ShapeDtypeStruct((B_HEADS, CHUNK, 2 * CHUNK), F32),
    )(table)


def _bias_grad(dbias):
    def body(db_ref, o_ref, acc_ref):
        rel = _rel_distance()
        lo = [0] + BUCKET_THR
        hi = BUCKET_THR + [CHUNK]
        for b in range(N_BUCKETS):
            m = (rel >= lo[b]) & (rel < hi[b])
            for h in range(B_HEADS):
                row = b * B_HEADS + h
                acc_ref[row:row + 1, :] = jnp.sum(jnp.where(m, db_ref[h], 0.0), axis=0, keepdims=True)
        o_ref[...] = jnp.sum(acc_ref[...], axis=1, keepdims=True)

    return pl.pallas_call(
        body, name="bias_grad",
        in_specs=[pl.BlockSpec(memory_space=pltpu.VMEM)],
        out_specs=pl.BlockSpec(memory_space=pltpu.VMEM),
        out_shape=jax.ShapeDtypeStruct((N_BUCKETS * B_HEADS, 1), F32),
        scratch_shapes=[pltpu.VMEM((N_BUCKETS * B_HEADS, 2 * CHUNK), F32)],
    )(dbias)


def _causal_mask():
    t = lax.broadcasted_iota(jnp.int32, (CHUNK, CHUNK), 0)
    s = lax.broadcasted_iota(jnp.int32, (CHUNK, CHUNK), 1)
    return s <= t


def _band_mask(n):
    rel = _rel_distance()
    j = lax.broadcasted_iota(jnp.int32, (CHUNK, 2 * CHUNK), 1)
    return (rel >= 0) & (rel < CHUNK) & ((n > 0) | (j >= CHUNK))


def _gate_forward(u, v, lg, lb, wc, bs):
    ug = _gelu(u)
    vg = _gelu(v)
    mu = jnp.mean(vg, axis=-1, keepdims=True)
    xc = vg - mu
    rstd = lax.rsqrt(jnp.mean(xc * xc, axis=-1, keepdims=True) + EPS)
    xhat = xc * rstd
    vl = (xhat * lg + lb).astype(BF16)
    mixed = _dot(wc, vl) + bs
    return ug, xhat, rstd, vl, mixed


def _softmax_scores(qk, bias, mask, sink):
    s = qk * SCALE + bias
    s = jnp.where(mask, s, NEG)
    m = jnp.maximum(jnp.max(s, axis=-1, keepdims=True), sink)
    p = jnp.exp(s - m)
    e_sink = jnp.exp(sink - m)
    inv = 1.0 / (jnp.sum(p, axis=-1, keepdims=True) + e_sink)
    return p * inv, e_sink * inv


PAIRS = Q_PER_KV // 2


def _head(g, pr, e):
    return g * Q_PER_KV + 2 * pr + e


def _stack_pairs(ref, g, col0=0):
    w = 2 * HEAD_DIM
    return jnp.concatenate([ref[:, col0 + (g * PAIRS + pr) * w:col0 + (g * PAIRS + pr + 1) * w] for pr in range(PAIRS)],
                           axis=0)


def _low_lanes():
    return lax.broadcasted_iota(jnp.int32, (2 * CHUNK, 2 * HEAD_DIM), 1) < HEAD_DIM


def _band_operands(kv_prev, kv_cur):
    band = jnp.concatenate([kv_prev, kv_cur], axis=0)
    low = _low_lanes()
    ops = []
    for cat in (band[:, :KV_WIDTH], band[:, KV_WIDTH:]):
        rol = pltpu.roll(cat, HEAD_DIM, 1)
        ops.append([[jnp.where(low if e == 0 else ~low, cat if g == e else rol, 0.0).astype(BF16) for e in range(2)]
                    for g in range(2)])
    return ops


def _mixer_fwd(proj, lg, lb, wsp, bs_col, sinks, bias, ga, gb, rides=()):
    T = proj.shape[0]
    nb = T // CHUNK

    def body(u_ref, v_ref, q_ref, kvc_ref, kvp_ref, lg_ref, lb_ref, w_ref, bs_ref, sink_ref, bias_ref,
             ga_ref, gb_ref, mixed_ref, mixed_t_ref, ab_ref):
        n = pl.program_id(0)
        causal = _causal_mask()
        ssq = jnp.zeros((CHUNK, 1), F32)
        for g in range(A_GROUPS):
            cols = slice(g * CHUNK, (g + 1) * CHUNK)
            wc = jnp.where(causal, w_ref[g], 0.0).astype(BF16)
            ug, _, _, _, mixed = _gate_forward(u_ref[:, cols], v_ref[:, cols], lg_ref[g:g + 1, :], lb_ref[g:g + 1, :],
                                               wc, bs_ref[g])
            a = ug * mixed
            ab_ref[:, cols] = a
            ssq = ssq + jnp.sum(a * a, axis=-1, keepdims=True)
        ra = lax.rsqrt(ssq * (1.0 / A_WIDTH) + EPS)
        mixed_ref[:, :A_WIDTH] = ((ab_ref[:, :A_WIDTH] * ra) * ga_ref[...]).astype(BF16)

        mask = _band_mask(n)
        kops, vops = _band_operands(kvp_ref[...], kvc_ref[...])
        ssq = jnp.zeros((CHUNK, 1), F32)
        for g in range(B_HEADS // Q_PER_KV):
            qst = _stack_pairs(q_ref, g).astype(BF16)
            o_st = jnp.zeros((PAIRS * CHUNK, 2 * HEAD_DIM), F32)
            for e in range(2):
                s_all = _dot_nt(qst, kops[g][e])
                ps = []
                for pr in range(PAIRS):
                    h = _head(g, pr, e)
                    p, _ = _softmax_scores(s_all[pr * CHUNK:(pr + 1) * CHUNK], bias_ref[h], mask, sink_ref[0, h])
                    ps.append(p.astype(BF16))
                o_st = o_st + _dot(jnp.concatenate(ps, axis=0), vops[g][e])
            for pr in range(PAIRS):
                o = o_st[pr * CHUNK:(pr + 1) * CHUNK]
                c0 = A_WIDTH + (g * PAIRS + pr) * 2 * HEAD_DIM
                ab_ref[:, c0:c0 + 2 * HEAD_DIM] = o
                ssq = ssq + jnp.sum(o * o, axis=-1, keepdims=True)
        rb = lax.rsqrt(ssq * (1.0 / B_WIDTH) + EPS)
        mixed_ref[:, A_WIDTH:] = ((ab_ref[:, A_WIDTH:] * rb) * gb_ref[...]).astype(BF16)
        mixed_t_ref[...] = mixed_ref[...].T

    full = lambda *shape: pl.BlockSpec(shape, lambda n: (0,) * len(shape))
    return _call(
        body, name="mixer_fwd", grid=(nb,),
        in_specs=[pl.BlockSpec((CHUNK, A_WIDTH), lambda n: (n, 0)),
                  pl.BlockSpec((CHUNK, A_WIDTH), lambda n: (n, 1)),
                  pl.BlockSpec((CHUNK, B_WIDTH), lambda n: (n, 2)),
                  pl.BlockSpec((CHUNK, 2 * KV_WIDTH), lambda n: (n, 12)),
                  pl.BlockSpec((CHUNK, 2 * KV_WIDTH), lambda n: (jnp.maximum(n - 1, 0), 12)),
                  full(A_GROUPS, CHUNK), full(A_GROUPS, CHUNK), full(A_GROUPS, CHUNK, CHUNK), full(A_GROUPS, CHUNK, 1),
                  pl.BlockSpec(memory_space=pltpu.SMEM), full(B_HEADS, CHUNK, 2 * CHUNK),
                  full(1, A_WIDTH), full(1, B_WIDTH)],
        out_specs=[pl.BlockSpec((CHUNK, D_MODEL), lambda n: (n, 0)), pl.BlockSpec((D_MODEL, CHUNK), lambda n: (0, n)),
                   pl.BlockSpec((CHUNK, D_MODEL), lambda n: (n, 0))],
        out_shape=[jax.ShapeDtypeStruct((T, D_MODEL), BF16), jax.ShapeDtypeStruct((D_MODEL, T), BF16),
                   jax.ShapeDtypeStruct((T, D_MODEL), F32)],
        sem=("parallel",), rides=rides,
    )(proj, proj, proj, proj, proj, lg, lb, wsp, bs_col, sinks, bias, ga, gb)


def _gmlp_bwd(proj, ab, dmixed, ga, lg, lb, wsp, bs_col, rides=()):
    T = proj.shape[0]
    nb = T // CHUNK

    def body(u_ref, v_ref, a_ref, dna_ref, ga_ref, lg_ref, lb_ref, w_ref, bs_ref,
             dp_ref, dpt_ref, dga_ref, dw_ref, dbs_ref, dlg_ref, dlb_ref):
        @pl.when(pl.program_id(0) == 0)
        def _():
            for r in (dga_ref, dw_ref, dbs_ref, dlg_ref, dlb_ref):
                r[...] = jnp.zeros_like(r)
        causal = _causal_mask()
        a_all = a_ref[...]
        dna = dna_ref[...]
        da_all, ra = _rms_bwd(dna, a_all, ga_ref[...])
        dga_ref[...] += jnp.sum(dna * (a_all * ra), axis=0, keepdims=True)
        for g in range(A_GROUPS):
            cols = slice(g * CHUNK, (g + 1) * CHUNK)
            wc = jnp.where(causal, w_ref[g], 0.0).astype(BF16)
            lgg = lg_ref[g:g + 1, :]
            u = u_ref[:, cols]
            v = v_ref[:, cols]
            ug, xhat, rstd, vl, mixed = _gate_forward(u, v, lgg, lb_ref[g:g + 1, :], wc, bs_ref[g])
            da = da_all[:, cols]
            dug = da * mixed
            dmg = da * ug
            dmg_b = dmg.astype(BF16)
            dbs_ref[g] += jnp.sum(dmg, axis=-1, keepdims=True)
            dw_ref[g] += jnp.where(causal, _dot_nt(dmg_b, vl), 0.0)
            dvl = _dot_tn(wc, dmg_b)
            dlg_ref[g:g + 1, :] += jnp.sum(dvl * xhat, axis=0, keepdims=True)
            dlb_ref[g:g + 1, :] += jnp.sum(dvl, axis=0, keepdims=True)
            dxh = dvl * lgg
            dvg = rstd * (dxh - jnp.mean(dxh, axis=-1, keepdims=True)
                          - xhat * jnp.mean(dxh * xhat, axis=-1, keepdims=True))
            _, gu = _gelu_and_grad(u)
            _, gv = _gelu_and_grad(v)
            dp_ref[:, cols] = (dug * gu).astype(BF16)
            dp_ref[:, A_WIDTH + g * CHUNK:A_WIDTH + (g + 1) * CHUNK] = (dvg * gv).astype(BF16)
        dpt_ref[...] = dp_ref[...].T

    full = lambda *shape: pl.BlockSpec(shape, lambda n: (0,) * len(shape))
    return _call(
        body, name="gmlp_bwd", grid=(nb,),
        in_specs=[pl.BlockSpec((CHUNK, A_WIDTH), lambda n: (n, 0)),
                  pl.BlockSpec((CHUNK, A_WIDTH), lambda n: (n, 1)),
                  pl.BlockSpec((CHUNK, A_WIDTH), lambda n: (n, 0)),
                  pl.BlockSpec((CHUNK, A_WIDTH), lambda n: (n, 0)),
                  full(1, A_WIDTH), full(A_GROUPS, CHUNK), full(A_GROUPS, CHUNK), full(A_GROUPS, CHUNK, CHUNK),
                  full(A_GROUPS, CHUNK, 1)],
        out_specs=[pl.BlockSpec((CHUNK, 2 * A_WIDTH), lambda n: (n, 0)), pl.BlockSpec((2 * A_WIDTH, CHUNK), lambda n: (0, n)),
                   full(1, A_WIDTH), full(A_GROUPS, CHUNK, CHUNK), full(A_GROUPS, CHUNK, 1),
                   full(A_GROUPS, CHUNK), full(A_GROUPS, CHUNK)],
        out_shape=[jax.ShapeDtypeStruct((T, 2 * A_WIDTH), BF16), jax.ShapeDtypeStruct((2 * A_WIDTH, T), BF16),
                   jax.ShapeDtypeStruct((1, A_WIDTH), F32), jax.ShapeDtypeStruct((A_GROUPS, CHUNK, CHUNK), F32),
                   jax.ShapeDtypeStruct((A_GROUPS, CHUNK, 1), F32), jax.ShapeDtypeStruct((A_GROUPS, CHUNK), F32),
                   jax.ShapeDtypeStruct((A_GROUPS, CHUNK), F32)],
        sem=("arbitrary",), rides=rides,
    )(proj, proj, ab, dmixed, ga, lg, lb, wsp, bs_col)


def _attn_bwd(proj, ab, dmixed, gb, sinks, bias, rides=()):
    T = proj.shape[0]
    nb = T // CHUNK
    qn = lambda n: jnp.minimum(n, nb - 1)

    def body(q_ref, kvc_ref, kvp_ref, o_ref, dnb_ref, gb_ref, sink_ref, bias_ref,
             dq_ref, dkv_ref, dqt_ref, dkvt_ref, dgb_ref, dsink_ref, dbias_ref, carry_ref, sacc_ref):
        n = pl.program_id(0)

        @pl.when(n == 0)
        def _():
            carry_ref[...] = jnp.zeros_like(carry_ref)
            sacc_ref[...] = jnp.zeros_like(sacc_ref)
            dgb_ref[...] = jnp.zeros_like(dgb_ref)
            dbias_ref[...] = jnp.zeros_like(dbias_ref)

        @pl.when(n < nb)
        def _():
            mask = _band_mask(n)
            o_all = o_ref[...]
            dnb = dnb_ref[...]
            do_all, rb = _rms_bwd(dnb, o_all, gb_ref[...])
            dgb_ref[...] += jnp.sum(dnb * (o_all * rb), axis=0, keepdims=True)
            kops, vops = _band_operands(kvp_ref[...], kvc_ref[...])
            low = _low_lanes()
            halves = []
            for g in range(B_HEADS // Q_PER_KV):
                qst = _stack_pairs(q_ref, g).astype(BF16)
                dost = _stack_pairs(do_all, g).astype(BF16)
                dq_st = jnp.zeros((PAIRS * CHUNK, 2 * HEAD_DIM), F32)
                dk_e, dv_e = [], []
                for e in range(2):
                    s_all = _dot_nt(qst, kops[g][e])
                    dp_all = _dot_nt(dost, vops[g][e])
                    ps, dsrs = [], []
                    for pr in range(PAIRS):
                        h = _head(g, pr, e)
                        rows = slice(pr * CHUNK, (pr + 1) * CHUNK)
                        p, p_sink = _softmax_scores(s_all[rows], bias_ref[h], mask, sink_ref[0, h])
                        dp = dp_all[rows]
                        delta = jnp.sum(p * dp, axis=-1, keepdims=True)
                        ds = p * (dp - delta)
                        sacc_ref[:, h:h + 1] += -(p_sink * delta)
                        dbias_ref[h] += ds
                        ps.append(p.astype(BF16))
                        dsrs.append((ds * SCALE).astype(BF16))
                    dsr_all = jnp.concatenate(dsrs, axis=0)
                    dq_st = dq_st + _dot(dsr_all, kops[g][e])
                    dk_e.append(_dot_tn(dsr_all, qst))
                    dv_e.append(_dot_tn(jnp.concatenate(ps, axis=0), dost))
                for pr in range(PAIRS):
                    c0 = (g * PAIRS + pr) * 2 * HEAD_DIM
                    dq_ref[:, c0:c0 + 2 * HEAD_DIM] = dq_st[pr * CHUNK:(pr + 1) * CHUNK].astype(BF16)
                halves.append((dk_e, dv_e))
            tiles = []
            for t in range(2):
                g0, g1 = halves[0][t], halves[1][t]
                tiles.append(jnp.where(low, g0[0] + pltpu.roll(g0[1], HEAD_DIM, 1), pltpu.roll(g1[0], HEAD_DIM, 1) + g1[1]))
            dband = jnp.concatenate(tiles, axis=1)
            dkv = (carry_ref[...] + dband[:CHUNK]).astype(BF16)
            dkv_ref[...] = dkv
            dkvt_ref[...] = dkv.T
            dqt_ref[...] = dq_ref[...].T
            carry_ref[...] = dband[CHUNK:]

        @pl.when(n == nb)
        def _():
            dkv = carry_ref[...].astype(BF16)
            dkv_ref[...] = dkv
            dkvt_ref[...] = dkv.T
            dsink_ref[...] = jnp.sum(sacc_ref[...], axis=0, keepdims=True)

    full = lambda *shape: pl.BlockSpec(shape, lambda n: (0,) * len(shape))
    return _call(
        body, name="attn_bwd", grid=(nb + 1,),
        in_specs=[pl.BlockSpec((CHUNK, B_WIDTH), lambda n: (qn(n), 2)),
                  pl.BlockSpec((CHUNK, 2 * KV_WIDTH), lambda n: (qn(n), 12)),
                  pl.BlockSpec((CHUNK, 2 * KV_WIDTH), lambda n: (jnp.maximum(qn(n) - 1, 0), 12)),
                  pl.BlockSpec((CHUNK, B_WIDTH), lambda n: (qn(n), 1)),
                  pl.BlockSpec((CHUNK, B_WIDTH), lambda n: (qn(n), 1)),
                  full(1, B_WIDTH), pl.BlockSpec(memory_space=pltpu.SMEM), full(B_HEADS, CHUNK, 2 * CHUNK)],
        out_specs=[pl.BlockSpec((CHUNK, B_WIDTH), lambda n: (qn(n), 0)),
                   pl.BlockSpec((CHUNK, 2 * KV_WIDTH), lambda n: (jnp.maximum(n - 1, 0), 0)),
                   pl.BlockSpec((B_WIDTH, CHUNK), lambda n: (0, qn(n))),
                   pl.BlockSpec((2 * KV_WIDTH, CHUNK), lambda n: (0, jnp.maximum(n - 1, 0))),
                   full(1, B_WIDTH), full(1, B_HEADS), full(B_HEADS, CHUNK, 2 * CHUNK)],
        out_shape=[jax.ShapeDtypeStruct((T, B_WIDTH), BF16), jax.ShapeDtypeStruct((T, 2 * KV_WIDTH), BF16),
                   jax.ShapeDtypeStruct((B_WIDTH, T), BF16), jax.ShapeDtypeStruct((2 * KV_WIDTH, T), BF16),
                   jax.ShapeDtypeStruct((1, B_WIDTH), F32), jax.ShapeDtypeStruct((1, B_HEADS), F32),
                   jax.ShapeDtypeStruct((B_HEADS, CHUNK, 2 * CHUNK), F32)],
        scratch_shapes=[pltpu.VMEM((CHUNK, 2 * KV_WIDTH), F32), pltpu.VMEM((CHUNK, B_HEADS), F32)],
        sem=("arbitrary",), rides=rides,
    )(proj, proj, proj, ab, dmixed, gb, sinks, bias)


def _sq_relu_grad(acc, r):
    return acc * (2.0 * r.astype(F32))


def _local_step(x, tgt, sp, win, wo, wu, wd):
    T = x.shape[0]
    tm = min(512, T)
    tk = min(512, T)
    lg = sp["gate_norm_g"].reshape(A_GROUPS, CHUNK)
    lb = sp["gate_norm_b"].reshape(A_GROUPS, CHUNK)
    wsp = sp["w_spatial"].reshape(A_GROUPS, CHUNK, CHUNK)
    bs_col = sp["b_spatial"].reshape(A_GROUPS, CHUNK, 1)
    sinks = sp["attn_sinks"].reshape(1, B_HEADS)
    ga = sp["out_norm_a_g"].reshape(1, A_WIDTH)
    gb = sp["out_norm_b_g"].reshape(1, B_WIDTH)
    g1 = sp["mix_norm_g"].reshape(1, D_MODEL)
    g2 = sp["ffn_norm_g"].reshape(1, D_MODEL)
    gf = sp["final_norm_g"].reshape(1, D_MODEL)

    bias = _bias_build(sp["rel_bias_table"])
    n1, proj = _norm_matmul(x, g1, win, tm=tm, tn=PROJ_WIDTH // 2, name="in_proj")
    mixed, ab = _mixer_fwd(proj, lg, lb, wsp, bs_col, sinks, bias, ga, gb)
    h1 = _matmul_res(mixed, wo, x, tm=tm, tn=1024, tk=D_MODEL, prologue=_to_bf16, name="out_proj")
    n2, zp = _norm_matmul(h1, g2, wu, tm=tm, tn=1024, name="up_proj")
    h2 = _matmul_res(zp, wd, h1, tm=tm, tn=1024, tk=2048, prologue=_sq_relu_bf16, name="down_proj")

    dh2, dgf, loss = _loss_bwd(h2, tgt, gf, tm=tm)
    dzp = _matmul_nt(dh2, wd, tm=tm, tn=1024, tk=D_MODEL, name="bwd_dz", extra=zp, epilogue=_sq_relu_grad,
                     out_dtype=BF16)
    dwd = _matmul_tn(zp, dh2, tmo=1024, tn=1024, tk=tk, name="grad_w_down", a_prologue=_sq_relu_bf16)
    dwu = _matmul_tn(n2, dzp, tmo=1024, tn=1024, tk=tk, name="grad_w_up", shards=N_CHIPS)
    dn2 = _matmul_nt(dzp, wu, tm=tm, tn=1024, tk=2048, name="bwd_dn2")
    dh1, dg2 = _rms_bwd_res(dn2, h1, g2, dh2, tm=tm, name="ffn_norm_bwd")
    dwo = _matmul_tn(mixed, dh1, tmo=1024, tn=1024, tk=tk, name="grad_w_out")
    dmixed = _matmul_nt(dh1, wo, tm=tm, tn=1024, tk=D_MODEL, name="bwd_dmixed")
    duv, dga, dwsp, dbs, dlg, dlb = _gmlp_bwd(proj, ab, dmixed, ga, lg, lb, wsp, bs_col)
    dq, dkv, dgb, dsinks, dbias = _attn_bwd(proj, ab, dmixed, gb, sinks, bias)
    dtable = _bias_grad(dbias)
    dproj = jnp.concatenate([duv, dq, dkv], axis=1)
    dwin = _matmul_tn(n1, dproj, tmo=1024, tn=PROJ_WIDTH // 2, tk=tk, name="grad_w_in")
    dn1 = _matmul_nt(dproj, win, tm=tm, tn=1024, tk=PROJ_WIDTH, name="bwd_dn1")
    dx, dg1 = _rms_bwd_res(dn1, x, g1, dh1, tm=tm, name="mix_norm_bwd")

    small = {
        "rel_bias_table": dtable.reshape(N_BUCKETS, B_HEADS), "mix_norm_g": dg1, "gate_norm_g": dlg, "gate_norm_b": dlb,
        "w_spatial": dwsp, "b_spatial": dbs, "attn_sinks": dsinks, "out_norm_a_g": dga, "out_norm_b_g": dgb,
        "ffn_norm_g": dg2, "final_norm_g": dgf,
    }
    return loss, dx, (dwin, dwo, dwu, dwd), small


def _place():
    x, y, c = lax.axis_index("x"), lax.axis_index("y"), lax.axis_index("c")
    chips = [(1 - x, y), (x, 1 - y), (1 - x, 1 - y)]
    return x, y, c, chips


def _remote(src, dst, send_sem, recv_sem, to):
    return pltpu.make_async_remote_copy(src_ref=src, dst_ref=dst, send_sem=send_sem, recv_sem=recv_sem,
                                        device_id=to, device_id_type=MESH)


def _core_index():
    return lax.axis_index("c").astype(jnp.int32).reshape(1)


def _chip_index():
    return (2 * lax.axis_index("x") + lax.axis_index("y")).astype(jnp.int32).reshape(1)


def _cast_into_slot(w, *, tm, name):
    _, R, C = w.shape

    def body(me_ref, w_ref, o_ref):
        del me_ref
        o_ref[...] = w_ref[...].astype(BF16)

    return pl.pallas_call(
        body, name=name,
        grid_spec=pltpu.PrefetchScalarGridSpec(
            num_scalar_prefetch=1, grid=(R // tm,),
            in_specs=[pl.BlockSpec((None, tm, C), lambda i, me: (0, i, 0))],
            out_specs=pl.BlockSpec((None, tm, C), lambda i, me: (me[0], i, 0))),
        out_shape=jax.ShapeDtypeStruct((N_CHIPS, R, C), BF16), compiler_params=_params(("parallel",)),
    )(_chip_index(), w)


def _cast_into_slot_carrying(w, *, tm, name, rides):
    _, R, C = w.shape

    def body(w_ref, o_ref):
        o_ref[...] = w_ref[...].astype(BF16)

    return _call(
        body, name=name, grid=(R // tm,),
        in_specs=[pl.BlockSpec((None, tm, C), lambda i: (0, i, 0))],
        out_specs=pl.BlockSpec((None, tm, C), lambda i: (2 * lax.axis_index("x") + lax.axis_index("y"), i, 0)),
        out_shape=jax.ShapeDtypeStruct((N_CHIPS, R, C), BF16), sem=("arbitrary",), rides=rides,
    )(w)


def _gather_weights(slots):
    nw = len(slots)

    def body(*refs):
        fulls = refs[nw:2 * nw]
        send_sems, recv_sems = refs[2 * nw:]
        x, y, c, chips = _place()
        me = 2 * x + y
        sends = []
        for w in range(nw):
            hr = fulls[w].shape[1] // 2
            rows = pl.ds(c * hr, hr)
            for j, chip in enumerate(chips):
                mine = fulls[w].at[me, rows, :]
                cp = _remote(mine, mine, send_sems.at[6 * w + j], recv_sems.at[6 * w + j], (*chip, c))
                cp.start()
                sends.append(cp)
        for w in range(nw):
            hr = fulls[w].shape[1] // 2
            rows = pl.ds(c * hr, hr)
            for j, chip in enumerate(chips):
                landed = fulls[w].at[2 * chip[0] + chip[1], rows, :]
                _remote(landed, landed, send_sems.at[6 * w + j], recv_sems.at[6 * w + j], (x, y, c)).wait_recv()
                cp = _remote(landed, landed, send_sems.at[6 * w + 3 + j], recv_sems.at[6 * w + 3 + j], (x, y, 1 - c))
                cp.start()
                sends.append(cp)
        for w in range(nw):
            hr = fulls[w].shape[1] // 2
            rows = pl.ds((1 - c) * hr, hr)
            for j, chip in enumerate(chips):
                other = fulls[w].at[2 * chip[0] + chip[1], rows, :]
                _remote(other, other, send_sems.at[6 * w + 3 + j], recv_sems.at[6 * w + 3 + j], (x, y, c)).wait_recv()
        for cp in sends:
            cp.wait_send()

    any_spec = pl.BlockSpec(memory_space=pl.ANY)
    return pl.pallas_call(
        body, name="gather_weights",
        in_specs=[any_spec] * nw, out_specs=[any_spec] * nw,
        out_shape=[jax.ShapeDtypeStruct(s.shape, s.dtype) for s in slots],
        scratch_shapes=[pltpu.SemaphoreType.DMA((6 * nw,)), pltpu.SemaphoreType.DMA((6 * nw,))],
        input_output_aliases={w: w for w in range(nw)},
    )(*slots)


def _sibling_halves(grads):
    nw = len(grads)

    def body(*refs):
        gs, outs = refs[:nw], refs[nw:2 * nw]
        send_sems, recv_sems = refs[2 * nw:]
        x, y, c, _ = _place()
        cps = []
        for w in range(nw):
            hr = gs[w].shape[1] // 2
            cp = _remote(gs[w].at[:, pl.ds((1 - c) * hr, hr), :], outs[w], send_sems.at[w], recv_sems.at[w],
                         (x, y, 1 - c))
            cp.start()
            cps.append(cp)
        for cp in cps:
            cp.wait()

    any_spec = pl.BlockSpec(memory_space=pl.ANY)
    return pl.pallas_call(
        body, name="rs_sibling_halves",
        in_specs=[any_spec] * nw, out_specs=[any_spec] * nw,
        out_shape=[jax.ShapeDtypeStruct((g.shape[0], g.shape[1] // 2, g.shape[2]), g.dtype) for g in grads],
        scratch_shapes=[pltpu.SemaphoreType.DMA((nw,)), pltpu.SemaphoreType.DMA((nw,))],
    )(*grads)


def _pair_sum_bf16(g, got, *, tm, name):
    S, R, C = g.shape
    hr = R // 2
    nt = hr // tm

    def body(c_ref, g_ref, got_ref, o_ref):
        del c_ref
        o_ref[...] = (g_ref[...] + got_ref[...]).astype(BF16)

    return pl.pallas_call(
        body, name=name,
        grid_spec=pltpu.PrefetchScalarGridSpec(
            num_scalar_prefetch=1, grid=(S, nt),
            in_specs=[pl.BlockSpec((None, tm, C), lambda s, i, c: (s, c[0] * nt + i, 0)),
                      pl.BlockSpec((None, tm, C), lambda s, i, c: (s, i, 0))],
            out_specs=pl.BlockSpec((None, tm, C), lambda s, i, c: (s, i, 0))),
        out_shape=jax.ShapeDtypeStruct((S, hr, C), BF16),
        compiler_params=_params(("parallel", "parallel")),
    )(_core_index(), g, got)


def _scatter_to_owners(pairs):
    nw = len(pairs)

    def body(*refs):
        qs, outs = refs[:nw], refs[nw:2 * nw]
        send_sems, recv_sems = refs[2 * nw:]
        x, y, c, chips = _place()
        cps = []
        for w in range(nw):
            for j, chip in enumerate(chips):
                cp = _remote(qs[w].at[2 * chip[0] + chip[1]], outs[w].at[j], send_sems.at[3 * w + j],
                             recv_sems.at[3 * w + j], (*chip, c))
                cp.start()
                cps.append(cp)
        for cp in cps:
            cp.wait()

    any_spec = pl.BlockSpec(memory_space=pl.ANY)
    return pl.pallas_call(
        body, name="rs_scatter_to_owners",
        in_specs=[any_spec] * nw, out_specs=[any_spec] * nw,
        out_shape=[jax.ShapeDtypeStruct((3,) + q.shape[1:], q.dtype) for q in pairs],
        scratch_shapes=[pltpu.SemaphoreType.DMA((3 * nw,)), pltpu.SemaphoreType.DMA((3 * nw,))],
    )(*pairs)


def _owner_total(gh, others, *, tm, name):
    _, hr, C = gh.shape

    def body(me_ref, g_ref, o_ref_in, out_ref):
        del me_ref
        acc = g_ref[...]
        for j in range(3):
            acc = acc + o_ref_in[j].astype(F32)
        out_ref[...] = acc

    return pl.pallas_call(
        body, name=name,
        grid_spec=pltpu.PrefetchScalarGridSpec(
            num_scalar_prefetch=1, grid=(hr // tm,),
            in_specs=[pl.BlockSpec((None, tm, C), lambda i, me: (me[0], i, 0)),
                      pl.BlockSpec((3, tm, C), lambda i, me: (0, i, 0))],
            out_specs=pl.BlockSpec((tm, C), lambda i, me: (i, 0))),
        out_shape=jax.ShapeDtypeStruct((hr, C), F32),
        compiler_params=_params(("parallel",)),
    )(_chip_index(), gh, others)


def _owner_sum(g, got, others, *, tm, name):
    S, R, C = g.shape
    hr = R // 2
    nt = hr // tm

    def body(idx_ref, g_ref, got_ref, o_ref_in, out_ref):
        del idx_ref
        acc = g_ref[...] + got_ref[...]
        for j in range(3):
            acc = acc + o_ref_in[j].astype(F32)
        out_ref[...] = acc

    return pl.pallas_call(
        body, name=name,
        grid_spec=pltpu.PrefetchScalarGridSpec(
            num_scalar_prefetch=1, grid=(nt,),
            in_specs=[pl.BlockSpec((None, tm, C), lambda i, p: (p[1], p[0] * nt + i, 0)),
                      pl.BlockSpec((None, tm, C), lambda i, p: (p[1], i, 0)),
                      pl.BlockSpec((3, tm, C), lambda i, p: (0, i, 0))],
            out_specs=pl.BlockSpec((tm, C), lambda i, p: (i, 0))),
        out_shape=jax.ShapeDtypeStruct((hr, C), F32),
        compiler_params=_params(("parallel",)),
    )(jnp.concatenate([_core_index(), _chip_index()]), g, got, others)


def _swap_halves(halves):
    nw = len(halves)

    def body(*refs):
        hs, outs = refs[:nw], refs[nw:2 * nw]
        send_sems, recv_sems = refs[2 * nw:]
        x, y, c, _ = _place()
        cps = []
        for w in range(nw):
            cp = _remote(hs[w], outs[w], send_sems.at[w], recv_sems.at[w], (x, y, 1 - c))
            cp.start()
            cps.append(cp)
        for cp in cps:
            cp.wait()

    any_spec = pl.BlockSpec(memory_space=pl.ANY)
    return pl.pallas_call(
        body, name="rs_swap_halves",
        in_specs=[any_spec] * nw, out_specs=[any_spec] * nw,
        out_shape=[jax.ShapeDtypeStruct(h.shape, h.dtype) for h in halves],
        scratch_shapes=[pltpu.SemaphoreType.DMA((nw,)), pltpu.SemaphoreType.DMA((nw,))],
    )(*halves)


def _all_reduce_small(packed):
    R, C = packed.shape

    def body(in_ref, out_ref, slots, send_sems, recv_sems):
        x, y, c, _ = _place()
        me = 4 * x + 2 * y + c
        cps = []
        for k in range(1, N_DEV):
            p = (me + k) % N_DEV
            cp = _remote(in_ref, slots.at[me], send_sems.at[k - 1], recv_sems.at[k - 1], (p // 4, (p // 2) % 2, p % 2))
            cp.start()
            cps.append(cp)
        slots[me] = in_ref[...]
        for k in range(1, N_DEV):
            src = (me + N_DEV - k) % N_DEV
            _remote(in_ref, slots.at[src], send_sems.at[k - 1], recv_sems.at[k - 1], (x, y, c)).wait_recv()
        for cp in cps:
            cp.wait_send()
        acc = slots[0]
        for d in range(1, N_DEV):
            acc = acc + slots[d]
        out_ref[...] = acc

    vmem = pl.BlockSpec(memory_space=pltpu.VMEM)
    return pl.pallas_call(
        body, name="all_reduce_small", in_specs=[vmem], out_specs=vmem,
        out_shape=jax.ShapeDtypeStruct((R, C), F32),
        scratch_shapes=[pltpu.VMEM((N_DEV, R, C), F32), pltpu.SemaphoreType.DMA((N_DEV - 1,)),
                        pltpu.SemaphoreType.DMA((N_DEV - 1,))],
        compiler_params=_params(),
    )(packed)


def _adamw_math(w, g, m, v):
    m = ADAM_B1 * m + (1.0 - ADAM_B1) * g
    v = ADAM_B2 * v + (1.0 - ADAM_B2) * (g * g)
    m_hat = m / (1.0 - ADAM_B1 ** ADAM_STEP)
    v_hat = v / (1.0 - ADAM_B2 ** ADAM_STEP)
    delta = -ADAM_LR * (m_hat / (jnp.sqrt(v_hat) + ADAM_EPS) + ADAM_WD * w)
    return delta, m, v


def _adamw(w, g, m, v, *, tm, name):
    R, C = w.shape

    def body(w_ref, g_ref, m_ref, v_ref, d_ref, nm_ref, nv_ref):
        d_ref[...], nm_ref[...], nv_ref[...] = _adamw_math(w_ref[...], g_ref[...], m_ref[...], v_ref[...])

    spec = pl.BlockSpec((tm, C), lambda i: (i, 0))
    return pl.pallas_call(
        body, name=name, grid=(R // tm,), in_specs=[spec] * 4, out_specs=[spec] * 3,
        out_shape=[jax.ShapeDtypeStruct((R, C), F32)] * 3, compiler_params=_params(("parallel",)),
    )(w, g, m, v)


def _adamw_halves(w, own, got, m, v, *, tm, name, rides=()):
    _, R, C = w.shape
    nt = (R // 2) // tm

    def body(w_ref, own_ref, got_ref, m_ref, v_ref, g_ref, d_ref, nm_ref, nv_ref):
        g = jnp.where(pl.program_id(0) == lax.axis_index("c"), own_ref[...], got_ref[...])
        g_ref[...] = g
        d_ref[...], nm_ref[...], nv_ref[...] = _adamw_math(w_ref[...], g, m_ref[...], v_ref[...])

    whole = pl.BlockSpec((None, tm, C), lambda h, i: (0, h * nt + i, 0))
    half = pl.BlockSpec((tm, C), lambda h, i: (i, 0))
    return _call(
        body, name=name, grid=(2, nt), in_specs=[whole, half, half, whole, whole], out_specs=[whole] * 4,
        out_shape=[jax.ShapeDtypeStruct((1, R, C), F32)] * 4, sem=("parallel", "parallel"), rides=rides,
    )(w, own, got, m, v)


def _adamw_small(w, slots, m, v, *, name):
    def body(w_ref, slots_ref, m_ref, v_ref, g_ref, d_ref, nm_ref, nv_ref):
        g = slots_ref[0]
        for d in range(1, N_DEV):
            g = g + slots_ref[d]
        g_ref[...] = g
        d_ref[...], nm_ref[...], nv_ref[...] = _adamw_math(w_ref[...], g, m_ref[...], v_ref[...])

    vmem = pl.BlockSpec(memory_space=pltpu.VMEM)
    return pl.pallas_call(
        body, name=name, in_specs=[vmem] * 4, out_specs=[vmem] * 4,
        out_shape=[jax.ShapeDtypeStruct(w.shape, F32)] * 4, compiler_params=_params(),
    )(w, slots, m, v)


SMALL = ["rel_bias_table", "mix_norm_g", "gate_norm_g", "gate_norm_b", "w_spatial", "b_spatial", "attn_sinks",
         "out_norm_a_g", "out_norm_b_g", "ffn_norm_g", "final_norm_g"]
SMALL_A = ["gate_norm_g", "gate_norm_b", "w_spatial", "b_spatial", "out_norm_a_g"]
SMALL_B = ["rel_bias_table", "mix_norm_g", "attn_sinks", "out_norm_b_g", "ffn_norm_g", "final_norm_g"]
LARGE = ["w_in", "w_out", "w_up", "w_down"]
ROW_TILE = {"w_in": 208, "w_out": 256, "w_up": 256, "w_down": 256}
WEIGHTS = ["rel_bias_table", "mix_norm_g", "w_in", "gate_norm_g", "gate_norm_b", "w_spatial", "b_spatial", "attn_sinks",
           "out_norm_a_g", "out_norm_b_g", "w_out", "ffn_norm_g", "w_up", "w_down", "final_norm_g"]
PACK_UNIT = 8 * 128


def _pack(parts):
    rows = []
    for p in parts:
        flat = p.reshape(-1)
        pad = (-flat.shape[0]) % PACK_UNIT
        rows.append(jnp.pad(flat, (0, pad)).reshape(-1, 128))
    return jnp.concatenate(rows, axis=0)


def _unpack(packed, like):
    out, row = [], 0
    for p in like:
        n = math.prod(p.shape)
        nrows = (n + PACK_UNIT - 1) // PACK_UNIT * 8
        out.append(packed[row:row + nrows].reshape(-1)[:n].reshape(p.shape))
        row += nrows
    return out


def kernel(x, rel_bias_table, mix_norm_g, w_in, gate_norm_g, gate_norm_b, w_spatial, b_spatial, attn_sinks, out_norm_a_g, out_norm_b_g, w_out, ffn_norm_g, w_up, w_down, final_norm_g, loss_target, m_rel_bias_table, m_mix_norm_g, m_w_in, m_gate_norm_g, m_gate_norm_b, m_w_spatial, m_b_spatial, m_attn_sinks, m_out_norm_a_g, m_out_norm_b_g, m_w_out, m_ffn_norm_g, m_w_up, m_w_down, m_final_norm_g, v_rel_bias_table, v_mix_norm_g, v_w_in, v_gate_norm_g, v_gate_norm_b, v_w_spatial, v_b_spatial, v_attn_sinks, v_out_norm_a_g, v_out_norm_b_g, v_w_out, v_ffn_norm_g, v_w_up, v_w_down, v_final_norm_g):
    args = dict(locals())
    wts = {n: args[n] for n in WEIGHTS}
    mom = {n: args["m_" + n] for n in WEIGHTS}
    var = {n: args["v_" + n] for n in WEIGHTS}
    sp = {n: wts[n] for n in SMALL}
    x2, tgt = x[0], loss_target[0]
    T = x2.shape[0]
    tm = min(512, T)
    tl = min(1024, T)
    tg = min(2048, T)
    lg = sp["gate_norm_g"].reshape(A_GROUPS, CHUNK)
    lb = sp["gate_norm_b"].reshape(A_GROUPS, CHUNK)
    wsp = sp["w_spatial"].reshape(A_GROUPS, CHUNK, CHUNK)
    bs_col = sp["b_spatial"].reshape(A_GROUPS, CHUNK, 1)
    sinks = sp["attn_sinks"].reshape(1, B_HEADS)
    ga = sp["out_norm_a_g"].reshape(1, A_WIDTH)
    gb = sp["out_norm_b_g"].reshape(1, B_WIDTH)
    g1 = sp["mix_norm_g"].reshape(1, D_MODEL)
    g2 = sp["ffn_norm_g"].reshape(1, D_MODEL)
    gf = sp["final_norm_g"].reshape(1, D_MODEL)

    def owner_total(n, gh, others):
        return _owner_total(gh, others, tm=ROW_TILE[n], name="rs_owner_total_" + n)

    def halves_view(at, shards):
        return at.reshape(shards, 2, at.shape[0] // shards // 2, at.shape[1])

    for d in (wts, mom, var):
        d["w_in"] = jnp.swapaxes(d["w_in"], 1, 2)

    s_in = _cast_into_slot(wts["w_in"], tm=ROW_TILE["w_in"], name="cast_w_in")
    s_up, ((s_in,),) = _cast_into_slot_carrying(wts["w_up"], tm=256, name="cast_w_up",
                                                rides=[_ride_gather(s_in, ici=(0, 1, 1))])
    s_down, ((g_in,),) = _cast_into_slot_carrying(wts["w_down"], tm=256, name="cast_w_down",
                                                  rides=[_ride_gather(s_in, d2d=(0, 1, 1))])
    s_out = _cast_into_slot(wts["w_out"], tm=256, name="cast_w_out")
    win_t = g_in.reshape(PROJ_WIDTH, D_MODEL)
    bias = _bias_build(sp["rel_bias_table"])
    (n1, proj), ((s_out,), (s_up,)) = _norm_matmul_wide(
        x2, g1, win_t, tm=tm, tn=PROJ_WIDTH // 2, name="in_proj",
        rides=[_ride_gather(s_out, ici=(0, 1, 1)), _ride_gather(s_up, ici=(0, 2, 8))])
    (mixed, mixed_t, ab), ((g_out,), (s_up,), (n1_sib,)) = _mixer_fwd(
        proj, lg, lb, wsp, bs_col, sinks, bias, ga, gb,
        rides=[_ride_gather(s_out, d2d=(0, 1, 1)), _ride_gather(s_up, d2d=(0, 2, 8), ici=(2, 8, 8)),
               _ride_to_sibling(n1, first=True)])
    wo = g_out.reshape(A_WIDTH + B_WIDTH, D_MODEL)
    mixed_t = halves_view(mixed_t, N_CHIPS)
    h1, ((wu,), (s_down,), (mixed_t_sib,)) = _matmul_res(
        mixed, wo, x2, tm=tl, tn=1024, tk=D_MODEL, prologue=_to_bf16, name="out_proj",
        rides=[_ride_gather(s_up, d2d=(2, 8, 8)), _ride_gather(s_down, ici=(0, 2, 8)), _ride_to_sibling(mixed_t, halves=True)])
    (n2t, zp, z2, z2t), ((g_down,),) = _norm_matmul_sq(
        h1, g2, wu, tm=tl, tn=1024, name="up_proj", rides=[_ride_gather(s_down, d2d=(0, 2, 8), both=(2, 8, 8), mid_frac=0.75)])
    wd = g_down.reshape(D_FF, D_MODEL)
    n2t, z2t = halves_view(n2t, 1), halves_view(z2t, N_CHIPS)
    h2, ((n2t_sib,), (z2t_sib,)) = _matmul_res(
        z2, wd, h1, tm=tl, tn=1024, tk=4096, prologue=_to_bf16, name="down_proj",
        rides=[_ride_to_sibling(n2t, halves=True), _ride_to_sibling(z2t, halves=True)])

    dh2, dh2b, dgf, loss = _loss_bwd(h2, tgt, gf, tm=tm)
    dzp, ((dh2b_sib,),) = _matmul_nt(dh2b, wd, tm=tl, tn=1024, tk=D_MODEL, name="bwd_dz", extra=zp,
                                     epilogue=_sq_relu_grad, out_dtype=BF16, rides=[_ride_to_sibling(dh2b)])
    (gd, gdb), ((dzp_sib,),) = _grad_pair(z2t, z2t_sib, dh2b, dh2b_sib, cols_sharded=False, tmo=1024, tk=tl,
                                          name="grad_w_down", rides=[_ride_to_sibling(dzp)])
    (gu, gub), ((o_d,),) = _grad_pair(n2t, n2t_sib, dzp, dzp_sib, cols_sharded=True, tmo=1024, tk=tl,
                                      name="grad_w_up", rides=[_ride_scatter(gdb, None, (0, 7, 8))])
    dn2, ((o_d,), (o_u,)) = _matmul_nt(dzp, wu, tm=tl, tn=1024, tk=4096, name="bwd_dn2",
                                       rides=[_ride_scatter(gdb, o_d, (7, 8, 8)), _ride_scatter(gub, None, (0, 6, 8))])
    h_d = owner_total("w_down", gd, o_d)
    (dh1, dh1b, dg2), ((o_u,),) = _rms_bwd_res(dn2, h1, g2, dh2, tm=tm, name="ffn_norm_bwd",
                                               rides=[_ride_scatter(gub, o_u, (6, 7, 8))])
    dmixed, ((o_u,), (dh1b_sib,), (w_d,)) = _matmul_nt(
        dh1b, wo, tm=tl, tn=1024, tk=D_MODEL, name="bwd_dmixed",
        rides=[_ride_scatter(gub, o_u, (7, 8, 8)), _ride_to_sibling(dh1b), _ride_swap(h_d)])
    h_u = owner_total("w_up", gu, o_u)
    (go, gob), ((w_u,),) = _grad_pair_merged(mixed_t, mixed_t_sib, dh1b, dh1b_sib, tk=tl, name="grad_w_out",
                                             rides=[_ride_swap(h_u)])
    (duv, duv_t, dga, dwsp, dbs, dlg, dlb), ((o_o,),) = _gmlp_bwd(proj, ab, dmixed, ga, lg, lb, wsp, bs_col,
                                                                  rides=[_ride_scatter(gob)])
    h_o = owner_total("w_out", go, o_o)
    small = {"gate_norm_g": dlg, "gate_norm_b": dlb, "w_spatial": dwsp, "b_spatial": dbs, "out_norm_a_g": dga}
    hr_in = PROJ_WIDTH // N_CHIPS // 2
    (dq, dkv, dq_t, dkv_t, dgb, dsinks, dbias), ((w_o,), (dproj_t_sib,)) = _attn_bwd(
        proj, ab, dmixed, gb, sinks, bias, rides=[_ride_swap(h_o), _ride_rows_to_sibling(duv_t, hr_in, 2, N_CHIPS)])
    dtable = _bias_grad(dbias)
    dproj_t = halves_view(jnp.concatenate([duv_t, dq_t, dkv_t], axis=0), N_CHIPS)
    ((dproj_t_sib,),) = _carrier([_ride_to_sibling(dproj_t, halves=True, shards=(2, N_CHIPS), land=dproj_t_sib)],
                                 name="trade_dproj_t")
    (gi, gib), ((slots_a,),) = _grad_pair(
        dproj_t, dproj_t_sib, n1, n1_sib, cols_sharded=False, tmo=hr_in, tk=tl, name="grad_w_in",
        rides=[_ride_small_to_all(_pack([small[n] for n in SMALL_A]))])
    dn1, ((o_i,),) = _matmul_parts([duv, dq, dkv], win_t, tm=tl, tn=1024, name="bwd_dn1", rides=[_ride_scatter(gib)])
    h_i = owner_total("w_in", gi, o_i)
    dx, _, dg1 = _rms_bwd_res(dn1, x2, g1, dh1, tm=tm, name="mix_norm_bwd")
    small.update({"rel_bias_table": dtable.reshape(N_BUCKETS, B_HEADS), "mix_norm_g": dg1, "attn_sinks": dsinks,
                  "out_norm_b_g": dgb, "ffn_norm_g": dg2, "final_norm_g": dgf})
    (w_i,), (slots_b,) = _carrier([_ride_swap(h_i), _ride_small_to_all(_pack([small[n] for n in SMALL_B] + [loss]))],
                                  name="swap_w_in")

    out_g, out_d, out_m, out_v = {}, {}, {}, {}
    for n, h, s in zip(LARGE, [h_i, h_o, h_u, h_d], [w_i, w_o, w_u, w_d]):
        res = _adamw_halves(wts[n], h, s, mom[n], var[n], tm=ROW_TILE[n], name="adamw_" + n)
        if n == "w_in":
            res = [jnp.swapaxes(r, 1, 2) for r in res]
        out_g[n], out_d[n], out_m[n], out_v[n] = res
    for names, slots, tag in ((SMALL_A, slots_a, "a"), (SMALL_B, slots_b, "b")):
        extra = [jnp.zeros((1, 1), F32)] if tag == "b" else []
        like = [wts[n] for n in names] + extra
        res = _adamw_small(_pack(like), slots, _pack([mom[n] for n in names] + extra),
                           _pack([var[n] for n in names] + extra), name="adamw_small_" + tag)
        for store, packed in zip((out_g, out_d, out_m, out_v), res):
            for n, val in zip(names + ["loss"], _unpack(packed, like)):
                store[n] = val

    total = out_g["loss"][0, 0]
    return (total, dx[None], *[out_g[n] for n in WEIGHTS], *[out_d[n] for n in WEIGHTS],
            *[out_m[n] for n in WEIGHTS], *[out_v[n] for n in WEIGHTS])
```

```python
import functools
import math

import numpy as np
import jax
import jax.numpy as jnp
from jax import lax
from jax.experimental import pallas as pl
from jax.experimental.pallas import tpu as pltpu

F32 = jnp.float32
BF16 = jnp.bfloat16

D_MODEL = 2048
CHUNK = 128
A_GROUPS = 8
A_WIDTH = 1024
HEAD_DIM = 64
B_HEADS = 16
Q_PER_KV = 8
B_WIDTH = 1024
KV_WIDTH = 128
PROJ_WIDTH = 3328
D_FF = 8192
N_BUCKETS = 32
EPS = 1e-5
NEG = -1e30
SCALE = HEAD_DIM ** -0.5
N_CHIPS = 4
N_DEV = 8

ADAM_LR = 0.001
ADAM_B1 = 0.9
ADAM_B2 = 0.999
ADAM_EPS = 1e-08
ADAM_WD = 0.01
ADAM_STEP = 10

VMEM_LIMIT = 60 * 1024 * 1024
MESH = pl.DeviceIdType.MESH


def _bucket_thresholds():
    d = np.arange(CHUNK)
    n_exact = N_BUCKETS // 2
    relf = np.maximum(d, n_exact).astype(np.float64)
    large = n_exact + (np.log(relf / n_exact) / math.log(CHUNK / n_exact) * (N_BUCKETS - n_exact)).astype(np.int32)
    bucket = np.where(d < n_exact, d, np.minimum(large, N_BUCKETS - 1))
    return [int(np.min(d[bucket >= b])) for b in range(1, N_BUCKETS)]


BUCKET_THR = _bucket_thresholds()


def _params(sem=None):
    return pltpu.CompilerParams(dimension_semantics=sem, vmem_limit_bytes=VMEM_LIMIT)


def _gelu(x):
    c = math.sqrt(2.0 / math.pi)
    return 0.5 * x * (1.0 + jnp.tanh(c * (x + 0.044715 * (x * x * x))))


def _gelu_and_grad(x):
    c = math.sqrt(2.0 / math.pi)
    x2 = x * x
    t = jnp.tanh(c * (x + 0.044715 * (x2 * x)))
    g = 0.5 * x * (1.0 + t)
    dg = 0.5 * (1.0 + t) + 0.5 * x * (1.0 - t * t) * (c * (1.0 + 3.0 * 0.044715 * x2))
    return g, dg


def _dot(a, b):
    return jnp.dot(a, b, preferred_element_type=F32)


def _dot_nt(a, b):
    return lax.dot_general(a, b, (((1,), (1,)), ((), ())), preferred_element_type=F32)


def _dot_tn(a, b):
    return lax.dot_general(a, b, (((0,), (0,)), ((), ())), preferred_element_type=F32)


def _rms_bwd(dn, h, g):
    r = lax.rsqrt(jnp.mean(h * h, axis=-1, keepdims=True) + EPS)
    w = dn * g
    dh = r * w - h * ((r * r * r) * jnp.mean(w * h, axis=-1, keepdims=True))
    return dh, r


def _place():
    x, y, c = lax.axis_index("x"), lax.axis_index("y"), lax.axis_index("c")
    chips = [(1 - x, y), (x, 1 - y), (1 - x, 1 - y)]
    return x, y, c, chips


def _remote(src, dst, send_sem, recv_sem, to):
    return pltpu.make_async_remote_copy(src_ref=src, dst_ref=dst, send_sem=send_sem, recv_sem=recv_sem,
                                        device_id=to, device_id_type=MESH)


class _Ride:
    def __init__(self, args, out_shape, n_sem, start, finish, mid=None, mid_frac=0.8, aliases=None):
        self.args, self.out_shape, self.n_sem = list(args), list(out_shape), n_sem
        self.start, self.mid, self.finish, self.mid_frac = start, mid, finish, mid_frac
        self.aliases = dict(aliases or {})


def _call(body, *, name, grid, in_specs, out_specs, out_shape, scratch_shapes=(), sem=None, rides=()):
    single = not isinstance(out_shape, (list, tuple))
    out_specs = [out_specs] if single else list(out_specs)
    out_shape = [out_shape] if single else list(out_shape)
    n_in, n_out, n_scr = len(in_specs), len(out_shape), len(scratch_shapes)
    r_in = [len(r.args) for r in rides]
    r_out = [len(r.out_shape) for r in rides]
    any_spec = pl.BlockSpec(memory_space=pl.ANY)
    aliases, off_i, off_o = {}, n_in, n_out
    for r in rides:
        for i, o in r.aliases.items():
            aliases[off_i + i] = off_o + o
        off_i += len(r.args)
        off_o += len(r.out_shape)
    steps = math.prod(grid)

    def wrapped(*refs):
        p = 0
        ins = refs[p:p + n_in]; p += n_in
        rins = refs[p:p + sum(r_in)]; p += sum(r_in)
        outs = refs[p:p + n_out]; p += n_out
        routs = refs[p:p + sum(r_out)]; p += sum(r_out)
        scr = refs[p:p + n_scr]; p += n_scr
        sems = refs[p:]
        parts, pi, po = [], 0, 0
        for k, r in enumerate(rides):
            parts.append((rins[pi:pi + r_in[k]], routs[po:po + r_out[k]], sems[2 * k], sems[2 * k + 1]))
            pi += r_in[k]
            po += r_out[k]
        lin = 0
        for d in range(len(grid)):
            lin = lin * grid[d] + pl.program_id(d)
        if rides:
            @pl.when(lin == 0)
            def _():
                for r, part in zip(rides, parts):
                    r.start(*part)
        body(*ins, *outs, *scr)
        for r, part in zip(rides, parts):
            if r.mid is not None:
                @pl.when(lin == min(steps - 1, int(r.mid_frac * steps)))
                def _(r=r, part=part):
                    r.mid(*part)
        if rides:
            @pl.when(lin == steps - 1)
            def _():
                for r, part in zip(rides, parts):
                    r.finish(*part)

    scratch = list(scratch_shapes)
    for r in rides:
        scratch += [pltpu.SemaphoreType.DMA((r.n_sem,)), pltpu.SemaphoreType.DMA((r.n_sem,))]
    if rides:
        sem = ("arbitrary",) * len(grid)
    res = pl.pallas_call(
        wrapped, name=name, grid=grid,
        in_specs=list(in_specs) + [any_spec] * sum(r_in),
        out_specs=out_specs + [any_spec] * sum(r_out),
        out_shape=out_shape + [s for r in rides for s in r.out_shape],
        scratch_shapes=scratch, input_output_aliases=aliases,
        compiler_params=_params(sem),
    )

    def run(*args):
        got = res(*args, *[a for r in rides for a in r.args])
        mine = got[0] if single else list(got[:n_out])
        if not rides:
            return mine
        rest, out = list(got[n_out:]), []
        for k in range(len(rides)):
            out.append(rest[:r_out[k]])
            rest = rest[r_out[k]:]
        return mine, out

    return run


def _ride_gather(slot, s1=None, s2=None, s3=None, tail=None, mid_frac=0.6):
    half = slot.shape[1] // 2

    def rows(part, c, which=None):
        k0, k1, n = part
        count, first = (k1 - k0) * (half // n), c * half + k0 * (half // n)
        return pl.ds(first, count) if which is None else pl.ds(first + which * (count // 2), count // 2)

    def ids():
        x, y, c, _ = _place()
        return x, y, c, 2 * x + y, 2 * (1 - x) + y, 2 * x + (1 - y), 2 * (1 - x) + (1 - y)

    def copy(full, chip, r, ss, rs, k, to):
        piece = full.at[chip, r, :]
        return _remote(piece, piece, ss.at[k], rs.at[k], to)

    def to_neighbours(full, ss, rs, part, base):
        x, y, c, me, _, _, _ = ids()
        return [copy(full, me, rows(part, c), ss, rs, base, (1 - x, y, c)),
                copy(full, me, rows(part, c), ss, rs, base + 1, (x, 1 - y, c))]

    def from_neighbours(full, ss, rs, part, base):
        x, y, c, _, cx, cy, _ = ids()
        return [copy(full, cx, rows(part, c), ss, rs, base, (x, y, c)), copy(full, cy, rows(part, c), ss, rs, base + 1, (x, y, c))]

    def onward(full, ss, rs, part, base):
        x, y, c, _, cx, cy, _ = ids()
        return [copy(full, cx, rows(part, c, 0), ss, rs, base, (x, 1 - y, c)),
                copy(full, cy, rows(part, c, 1), ss, rs, base + 1, (1 - x, y, c))]

    def from_onward(full, ss, rs, part, base):
        x, y, c, _, _, _, cd = ids()
        return [copy(full, cd, rows(part, c, 0), ss, rs, base, (x, y, c)), copy(full, cd, rows(part, c, 1), ss, rs, base + 1, (x, y, c))]

    def to_sibling(full, ss, rs, part, base, diagonal):
        x, y, c, _, cx, cy, cd = ids()
        return [copy(full, chip, rows(part, c), ss, rs, base + j, (x, y, 1 - c))
                for j, chip in enumerate([cd] if diagonal else [cx, cy])]

    def from_sibling(full, ss, rs, part, base, diagonal):
        x, y, c, _, cx, cy, cd = ids()
        return [copy(full, chip, rows(part, 1 - c), ss, rs, base + j, (x, y, c))
                for j, chip in enumerate([cd] if diagonal else [cx, cy])]

    def start(ins, outs, ss, rs):
        full, cps = outs[0], []
        if s1 is not None:
            cps += to_neighbours(full, ss, rs, s1, 0)
        for part, b_ici, b_sib in ((s2, 2, 4), (tail, 7, 9)):
            if part is not None:
                cps += onward(full, ss, rs, part, b_ici) + to_sibling(full, ss, rs, part, b_sib, False)
        if s3 is not None:
            cps += to_sibling(full, ss, rs, s3, 6, True)
        for cp in cps:
            cp.start()

    def mid(ins, outs, ss, rs):
        for cp in from_onward(outs[0], ss, rs, tail, 7):
            cp.wait_recv()
        for cp in to_sibling(outs[0], ss, rs, tail, 11, True):
            cp.start()

    def finish(ins, outs, ss, rs):
        full, got, sent = outs[0], [], []
        if s1 is not None:
            got += from_neighbours(full, ss, rs, s1, 0)
            sent += to_neighbours(full, ss, rs, s1, 0)
        if s2 is not None:
            got += from_onward(full, ss, rs, s2, 2) + from_sibling(full, ss, rs, s2, 4, False)
            sent += onward(full, ss, rs, s2, 2) + to_sibling(full, ss, rs, s2, 4, False)
        if s3 is not None:
            got += from_sibling(full, ss, rs, s3, 6, True)
            sent += to_sibling(full, ss, rs, s3, 6, True)
        if tail is not None:
            got += from_sibling(full, ss, rs, tail, 9, False) + from_sibling(full, ss, rs, tail, 11, True)
            sent += onward(full, ss, rs, tail, 7) + to_sibling(full, ss, rs, tail, 9, False) + to_sibling(full, ss, rs, tail, 11, True)
        for cp in got:
            cp.wait_recv()
        for cp in sent:
            cp.wait_send()

    return _Ride([slot], [jax.ShapeDtypeStruct(slot.shape, slot.dtype)], 12, start, finish,
                 mid=mid if tail is not None else None, mid_frac=mid_frac, aliases={0: 0})


def _ride_sibling_halves(g):
    S, R, C = g.shape
    hr = R // 2

    def copy(ins, outs, ss, rs):
        x, y, c, _ = _place()
        return _remote(ins[0].at[:, pl.ds((1 - c) * hr, hr), :], outs[0], ss.at[0], rs.at[0], (x, y, 1 - c))

    return _Ride([g], [jax.ShapeDtypeStruct((S, hr, C), g.dtype)], 1,
                 lambda *a: copy(*a).start(), lambda *a: copy(*a).wait())


def _ride_scatter(q, land=None, part=(0, 1)):
    k0, k1, n = part if len(part) == 3 else (part[0], part[0] + 1, part[1])
    rows_n = q.shape[1] // n
    rows = pl.ds(k0 * rows_n, (k1 - k0) * rows_n)

    def copies(ins, outs, ss, rs):
        x, y, c, chips = _place()
        return [_remote(ins[0].at[2 * chip[0] + chip[1], rows, :], outs[0].at[j, rows, :], ss.at[j], rs.at[j], (*chip, c))
                for j, chip in enumerate(chips)]

    def start(*a):
        for cp in copies(*a):
            cp.start()

    def finish(*a):
        for cp in copies(*a):
            cp.wait()

    shape = jax.ShapeDtypeStruct((3,) + q.shape[1:], q.dtype)
    if land is None:
        return _Ride([q], [shape], 3, start, finish)
    return _Ride([q, land], [shape], 3, start, finish, aliases={1: 0})


def _ride_to_sibling(a, halves=False, first=False, shards=None, land=None):
    s0, s1 = shards or (0, a.shape[0])

    def copy(ins, outs, ss, rs):
        x, y, c, _ = _place()
        if halves:
            src, dst = ins[0].at[s0:s1, 1 - c], outs[0].at[s0:s1]
        else:
            src, dst = (ins[0].at[0] if first else ins[0]), outs[0]
        return _remote(src, dst, ss.at[0], rs.at[0], (x, y, 1 - c))

    shape = (a.shape[0],) + a.shape[2:] if halves else (a.shape[1:] if first else a.shape)
    return _Ride([a] if land is None else [a, land], [jax.ShapeDtypeStruct(shape, a.dtype)], 1,
                 lambda *a_: copy(*a_).start(), lambda *a_: copy(*a_).wait(), aliases=None if land is None else {1: 0})


def _ride_rows_to_sibling(a, hr, shards, total):
    def copies(ins, outs, ss, rs):
        x, y, c, _ = _place()
        return [_remote(ins[0].at[pl.ds((2 * s + 1 - c) * hr, hr), :], outs[0].at[s], ss.at[s], rs.at[s], (x, y, 1 - c))
                for s in range(shards)]

    def start(*a_):
        for cp in copies(*a_):
            cp.start()

    def finish(*a_):
        for cp in copies(*a_):
            cp.wait()

    return _Ride([a], [jax.ShapeDtypeStruct((total, hr, a.shape[1]), a.dtype)], shards, start, finish)


def _ride_swap(h):
    def copy(ins, outs, ss, rs):
        x, y, c, _ = _place()
        return _remote(ins[0], outs[0], ss.at[0], rs.at[0], (x, y, 1 - c))

    return _Ride([h], [jax.ShapeDtypeStruct(h.shape, h.dtype)], 1,
                 lambda *a: copy(*a).start(), lambda *a: copy(*a).wait())


def _mesh_place(p):
    return (p // 4, (p // 2) % 2, p % 2)


def _ride_small_to_all(packed):
    def copies(ins, outs, ss, rs):
        x, y, c, _ = _place()
        me = 4 * x + 2 * y + c
        return [_remote(ins[0], outs[0].at[me], ss.at[k - 1], rs.at[k - 1], _mesh_place((me + k) % N_DEV))
                for k in range(1, N_DEV)]

    def own(ins, outs, ss, rs):
        x, y, c, _ = _place()
        return pltpu.make_async_copy(ins[0], outs[0].at[4 * x + 2 * y + c], ss.at[N_DEV - 1])

    def start(*a):
        own(*a).start()
        for cp in copies(*a):
            cp.start()

    def finish(ins, outs, ss, rs):
        x, y, c, _ = _place()
        me = 4 * x + 2 * y + c
        for k in range(1, N_DEV):
            _remote(ins[0], outs[0].at[(me + N_DEV - k) % N_DEV], ss.at[k - 1], rs.at[k - 1], (x, y, c)).wait_recv()
        for cp in copies(ins, outs, ss, rs):
            cp.wait_send()
        own(ins, outs, ss, rs).wait()

    return _Ride([packed], [jax.ShapeDtypeStruct((N_DEV,) + packed.shape, packed.dtype)], N_DEV, start, finish)


def _carrier(rides, *, name):
    _, outs = _call(lambda: None, name=name, grid=(1,), in_specs=[], out_specs=[], out_shape=[], rides=rides)()
    return outs


def _sq_relu_bf16(z):
    z = jnp.maximum(z, 0.0)
    return (z * z).astype(BF16)


def _norm_bf16(a_ref, g_ref):
    xf = a_ref[...]
    r = lax.rsqrt(jnp.mean(xf * xf, axis=-1, keepdims=True) + EPS)
    return ((xf * r) * g_ref[...]).astype(BF16)


def _norm_matmul_wide(a, g, b, *, tm, tn, name, rides=()):
    T, K = a.shape
    N = b.shape[0]

    def body(a_ref, g_ref, b_ref, n_ref, o_ref):
        n = _norm_bf16(a_ref, g_ref)
        n_ref[...] = n
        o_ref[...] = _dot_nt(n, b_ref[...])

    return _call(
        body, name=name, grid=(N // tn, T // tm),
        in_specs=[pl.BlockSpec((tm, K), lambda j, i: (i, 0)), pl.BlockSpec((1, K), lambda j, i: (0, 0)),
                  pl.BlockSpec((tn, K), lambda j, i: (j, 0))],
        out_specs=[pl.BlockSpec((None, tm, K), lambda j, i: (j, i, 0)), pl.BlockSpec((tm, tn), lambda j, i: (i, j))],
        out_shape=[jax.ShapeDtypeStruct((N // tn, T, K), BF16), jax.ShapeDtypeStruct((T, N), F32)],
        sem=("arbitrary", "arbitrary"), rides=rides,
    )(a, g, b)


def _norm_matmul_sq(a, g, b, *, tm, tn, name, rides=()):
    T, K = a.shape
    per = b.shape[2] // tn
    N = b.shape[0] * b.shape[2]

    def body(a_ref, g_ref, b_ref, nt_ref, o_ref, z_ref, zt_ref, n_scr):
        @pl.when(pl.program_id(1) == 0)
        def _():
            n = _norm_bf16(a_ref, g_ref)
            n_scr[...] = n
            nt_ref[...] = n.T
        r = jnp.maximum(_dot(n_scr[...], b_ref[...]), 0.0)
        o_ref[...] = r.astype(BF16)
        z = (r * r).astype(BF16)
        z_ref[...] = z
        zt_ref[...] = z.T

    return _call(
        body, name=name, grid=(T // tm, N // tn),
        in_specs=[pl.BlockSpec((tm, K), lambda i, j: (i, 0)), pl.BlockSpec((1, K), lambda i, j: (0, 0)),
                  pl.BlockSpec((None, K, tn), lambda i, j: (j // per, 0, j % per))],
        out_specs=[pl.BlockSpec((K, tm), lambda i, j: (0, i)), pl.BlockSpec((tm, tn), lambda i, j: (i, j)),
                   pl.BlockSpec((tm, tn), lambda i, j: (i, j)), pl.BlockSpec((tn, tm), lambda i, j: (j, i))],
        out_shape=[jax.ShapeDtypeStruct((K, T), BF16), jax.ShapeDtypeStruct((T, N), BF16),
                   jax.ShapeDtypeStruct((T, N), BF16), jax.ShapeDtypeStruct((N, T), BF16)],
        scratch_shapes=[pltpu.VMEM((tm, K), BF16)],
        sem=("parallel", "arbitrary"), rides=rides,
    )(a, g, b)


def _grad_pair(at, at_sib, b, b_sib, *, cols_sharded, tmo, tk, name, rides=()):
    S, _, hr, T = at.shape
    C = b.shape[-1] // N_CHIPS if cols_sharded else b.shape[-1]
    nk = T // tk
    a_sel = (lambda s: 0) if cols_sharded else (lambda s: s)
    b_sel = (lambda s: s) if cols_sharded else (lambda s: 0)
    if b.ndim == 3:
        b_spec = pl.BlockSpec((None, tk, C), lambda s, i, k: (0, k, b_sel(s)))
    else:
        b_spec = pl.BlockSpec((tk, C), lambda s, i, k: (k, b_sel(s)))

    def body(a_ref, as_ref, b_ref, bs_ref, o_ref, ob_ref):
        k = pl.program_id(2)
        p = _dot(a_ref[...], b_ref[...]) + _dot(as_ref[...], bs_ref[...])

        @pl.when(k == 0)
        def _():
            o_ref[...] = p

        @pl.when(k > 0)
        def _():
            o_ref[...] += p

        @pl.when(k == nk - 1)
        def _():
            ob_ref[...] = o_ref[...].astype(BF16)

    out = pl.BlockSpec((None, tmo, C), lambda s, i, k: (s, i, 0))
    return _call(
        body, name=name, grid=(N_CHIPS, hr // tmo, nk),
        in_specs=[pl.BlockSpec((None, None, tmo, tk), lambda s, i, k: (a_sel(s), lax.axis_index("c"), i, k)),
                  pl.BlockSpec((None, tmo, tk), lambda s, i, k: (a_sel(s), i, k)),
                  b_spec, pl.BlockSpec((tk, C), lambda s, i, k: (k, b_sel(s)))],
        out_specs=[out, out],
        out_shape=[jax.ShapeDtypeStruct((N_CHIPS, hr, C), F32), jax.ShapeDtypeStruct((N_CHIPS, hr, C), BF16)],
        sem=("parallel", "parallel", "arbitrary"), rides=rides,
    )(at, at_sib, b, b_sib)


def _grad_pair_merged(at, at_sib, b, b_sib, *, tk, name, rides=()):
    S, _, hr, T = at.shape
    C = b.shape[-1]
    nk = T // tk

    def body(a_ref, as_ref, b_ref, bs_ref, o_ref, ob_ref):
        k = pl.program_id(0)
        p = (_dot(a_ref[...].reshape(S * hr, tk), b_ref[...])
             + _dot(as_ref[...].reshape(S * hr, tk), bs_ref[...])).reshape(S, hr, C)

        @pl.when(k == 0)
        def _():
            o_ref[...] = p

        @pl.when(k > 0)
        def _():
            o_ref[...] += p

        @pl.when(k == nk - 1)
        def _():
            ob_ref[...] = o_ref[...].astype(BF16)

    out = pl.BlockSpec((S, hr, C), lambda k: (0, 0, 0))
    return _call(
        body, name=name, grid=(nk,),
        in_specs=[pl.BlockSpec((S, None, hr, tk), lambda k: (0, lax.axis_index("c"), 0, k)),
                  pl.BlockSpec((S, hr, tk), lambda k: (0, 0, k)),
                  pl.BlockSpec((tk, C), lambda k: (k, 0)), pl.BlockSpec((tk, C), lambda k: (k, 0))],
        out_specs=[out, out],
        out_shape=[jax.ShapeDtypeStruct((S, hr, C), F32), jax.ShapeDtypeStruct((S, hr, C), BF16)],
        sem=("arbitrary",), rides=rides,
    )(at, at_sib, b, b_sib)


def _matmul_nn(at, b, *, tmo, tn, tk, name, shards=1, rides=()):
    M, T = at.shape[-2:]
    N = b.shape[-1]
    if at.ndim == 3:
        a_spec = pl.BlockSpec((None, tmo, tk), lambda i, j, k: (0, i, k))
    else:
        a_spec = pl.BlockSpec((tmo, tk), lambda i, j, k: (i, k))
    if b.ndim == 3:
        b_spec = pl.BlockSpec((None, tk, tn), lambda i, j, k: (0, k, j))
    else:
        b_spec = pl.BlockSpec((tk, tn), lambda i, j, k: (k, j))
    if shards > 1:
        per = (N // shards) // tn
        out_spec = pl.BlockSpec((None, tmo, tn), lambda i, j, k: (j // per, i, j % per))
        out_shape = jax.ShapeDtypeStruct((shards, M, N // shards), F32)
    else:
        out_spec = pl.BlockSpec((tmo, tn), lambda i, j, k: (i, j))
        out_shape = jax.ShapeDtypeStruct((M, N), F32)

    def body(a_ref, b_ref, o_ref):
        k = pl.program_id(2)
        p = _dot(a_ref[...], b_ref[...])

        @pl.when(k == 0)
        def _():
            o_ref[...] = p

        @pl.when(k > 0)
        def _():
            o_ref[...] += p

    return _call(
        body, name=name, grid=(M // tmo, N // tn, T // tk),
        in_specs=[a_spec, b_spec],
        out_specs=out_spec, out_shape=out_shape,
        sem=("parallel", "parallel", "arbitrary"), rides=rides,
    )(at, b)


def _matmul_parts(parts, b, *, tm, tn, name, rides=()):
    T = parts[0].shape[0]
    N = b.shape[1]
    offs = [sum(p.shape[1] for p in parts[:i]) for i in range(len(parts))]
    assert all(o % p.shape[1] == 0 for o, p in zip(offs, parts))

    def body(*refs):
        n = len(parts)
        acc = _dot(refs[0][...], refs[n][...])
        for i in range(1, n):
            acc = acc + _dot(refs[i][...], refs[n + i][...])
        refs[-1][...] = acc

    a_specs = [pl.BlockSpec((tm, p.shape[1]), lambda i, j: (i, 0)) for p in parts]
    b_specs = [pl.BlockSpec((p.shape[1], tn), lambda i, j, r=o // p.shape[1]: (r, j)) for o, p in zip(offs, parts)]
    return _call(
        body, name=name, grid=(T // tm, N // tn), in_specs=a_specs + b_specs,
        out_specs=pl.BlockSpec((tm, tn), lambda i, j: (i, j)), out_shape=jax.ShapeDtypeStruct((T, N), F32),
        sem=("parallel", "parallel"), rides=rides,
    )(*parts, *([b] * len(parts)))


def _to_bf16(v):
    return v.astype(BF16)


def _matmul_res(a, b, res, *, tm, tn, tk, prologue, name, rides=()):
    T, K = a.shape
    N = b.shape[1]

    def body(a_ref, b_ref, res_ref, o_ref):
        k = pl.program_id(2)
        p = _dot(prologue(a_ref[...]), b_ref[...])

        @pl.when(k == 0)
        def _():
            o_ref[...] = res_ref[...] + p

        @pl.when(k > 0)
        def _():
            o_ref[...] += p

    return _call(
        body, name=name, grid=(T // tm, N // tn, K // tk),
        in_specs=[pl.BlockSpec((tm, tk), lambda i, j, k: (i, k)), pl.BlockSpec((tk, tn), lambda i, j, k: (k, j)),
                  pl.BlockSpec((tm, tn), lambda i, j, k: (i, j))],
        out_specs=pl.BlockSpec((tm, tn), lambda i, j, k: (i, j)),
        out_shape=jax.ShapeDtypeStruct((T, N), F32),
        sem=("parallel", "parallel", "arbitrary"), rides=rides,
    )(a, b, res)


def _matmul_nt(a, b, *, tm, tn, tk, name, extra=None, epilogue=None, out_dtype=F32, rides=()):
    T, K = a.shape
    two = b.ndim == 3 and tk == 2 * b.shape[2]
    if two:
        N, ks = b.shape[1], b.shape[2]
        b_specs = [pl.BlockSpec((None, tn, ks), lambda i, j, k: (2 * k, j, 0)),
                   pl.BlockSpec((None, tn, ks), lambda i, j, k: (2 * k + 1, j, 0))]
    elif b.ndim == 3:
        per = b.shape[2] // tk
        N = b.shape[1]
        b_specs = [pl.BlockSpec((None, tn, tk), lambda i, j, k: (k // per, j, k % per))]
    else:
        N = b.shape[0]
        b_specs = [pl.BlockSpec((tn, tk), lambda i, j, k: (j, k))]
    nb = len(b_specs)
    nk = K // tk
    assert out_dtype == F32 or nk == 1
    in_specs = [pl.BlockSpec((tm, tk), lambda i, j, k: (i, k))] + b_specs
    args = [a] + [b] * nb
    if extra is not None:
        in_specs.append(pl.BlockSpec((tm, tn), lambda i, j, k: (i, j)))
        args.append(extra)

    def body(*refs):
        a_ref, b_ref = refs[0], refs[1]
        o_ref = refs[-1]
        if two:
            p = (_dot_nt(a_ref[:, :tk // 2].astype(BF16), refs[1][...])
                 + _dot_nt(a_ref[:, tk // 2:].astype(BF16), refs[2][...]))
        else:
            p = _dot_nt(a_ref[...].astype(BF16), b_ref[...])
        if nk == 1:
            if epilogue is not None:
                p = epilogue(p, refs[1 + nb][...])
            o_ref[...] = p.astype(out_dtype)
        else:
            k = pl.program_id(2)

            @pl.when(k == 0)
            def _():
                o_ref[...] = p

            @pl.when(k > 0)
            def _():
                o_ref[...] += p

    return _call(
        body, name=name, grid=(T // tm, N // tn, nk),
        in_specs=in_specs,
        out_specs=pl.BlockSpec((tm, tn), lambda i, j, k: (i, j)),
        out_shape=jax.ShapeDtypeStruct((T, N), out_dtype),
        sem=("parallel", "parallel", "arbitrary"), rides=rides,
    )(*args)


def _matmul_tn(a, b, *, tmo, tn, tk, name, a_prologue=_to_bf16, shards=1, rides=()):
    T, M = a.shape
    N = b.shape[1]
    if shards > 1:
        per = (N // shards) // tn
        out_spec = pl.BlockSpec((None, tmo, tn), lambda i, j, k: (j // per, i, j % per))
        out_shape = jax.ShapeDtypeStruct((shards, M, N // shards), F32)
    else:
        out_spec = pl.BlockSpec((tmo, tn), lambda i, j, k: (i, j))
        out_shape = jax.ShapeDtypeStruct((M, N), F32)

    def body(a_ref, b_ref, o_ref):
        k = pl.program_id(2)
        p = _dot_tn(a_prologue(a_ref[...]), b_ref[...].astype(BF16))

        @pl.when(k == 0)
        def _():
            o_ref[...] = p

        @pl.when(k > 0)
        def _():
            o_ref[...] += p

    return _call(
        body, name=name, grid=(M // tmo, N // tn, T // tk),
        in_specs=[pl.BlockSpec((tk, tmo), lambda i, j, k: (k, i)), pl.BlockSpec((tk, tn), lambda i, j, k: (k, j))],
        out_specs=out_spec, out_shape=out_shape,
        sem=("parallel", "parallel", "arbitrary"), rides=rides,
    )(a, b)


def _loss_bwd(h2, tgt, g, *, tm):
    T, D = h2.shape

    def body(h_ref, t_ref, g_ref, dh_ref, dhb_ref, dg_ref, loss_ref):
        @pl.when(pl.program_id(0) == 0)
        def _():
            dg_ref[...] = jnp.zeros_like(dg_ref)
            loss_ref[...] = jnp.zeros_like(loss_ref)
        h = h_ref[...]
        gg = g_ref[...]
        r = lax.rsqrt(jnp.mean(h * h, axis=-1, keepdims=True) + EPS)
        hn = h * r
        err = hn * gg - t_ref[...]
        loss_ref[...] += 0.5 * jnp.sum(jnp.mean(err * err, axis=-1, keepdims=True), axis=0, keepdims=True)
        dy = err * (1.0 / D)
        dg_ref[...] += jnp.sum(dy * hn, axis=0, keepdims=True)
        w = dy * gg
        dh = r * w - h * ((r * r * r) * jnp.mean(w * h, axis=-1, keepdims=True))
        dh_ref[...] = dh
        dhb_ref[...] = dh.astype(BF16)

    tile = pl.BlockSpec((tm, D), lambda i: (i, 0))
    return pl.pallas_call(
        body, name="loss_bwd", grid=(T // tm,),
        in_specs=[tile, tile, pl.BlockSpec((1, D), lambda i: (0, 0))],
        out_specs=[tile, tile, pl.BlockSpec((1, D), lambda i: (0, 0)), pl.BlockSpec((1, 1), lambda i: (0, 0))],
        out_shape=[jax.ShapeDtypeStruct((T, D), F32), jax.ShapeDtypeStruct((T, D), BF16),
                   jax.ShapeDtypeStruct((1, D), F32), jax.ShapeDtypeStruct((1, 1), F32)],
        compiler_params=_params(("arbitrary",)),
    )(h2, tgt, g)


def _rms_bwd_res(dn, h, g, dres, *, tm, name, rides=()):
    T, D = h.shape

    def body(dn_ref, h_ref, g_ref, dres_ref, dh_ref, dhb_ref, dg_ref):
        @pl.when(pl.program_id(0) == 0)
        def _():
            dg_ref[...] = jnp.zeros_like(dg_ref)
        h_ = h_ref[...]
        dn_ = dn_ref[...]
        dh, r = _rms_bwd(dn_, h_, g_ref[...])
        dg_ref[...] += jnp.sum(dn_ * (h_ * r), axis=0, keepdims=True)
        dh = dres_ref[...] + dh
        dh_ref[...] = dh
        dhb_ref[...] = dh.astype(BF16)

    tile = pl.BlockSpec((tm, D), lambda i: (i, 0))
    return _call(
        body, name=name, grid=(T // tm,),
        in_specs=[tile, tile, pl.BlockSpec((1, D), lambda i: (0, 0)), tile],
        out_specs=[tile, tile, pl.BlockSpec((1, D), lambda i: (0, 0))],
        out_shape=[jax.ShapeDtypeStruct((T, D), F32), jax.ShapeDtypeStruct((T, D), BF16),
                   jax.ShapeDtypeStruct((1, D), F32)],
        sem=("arbitrary",), rides=rides,
    )(dn, h, g, dres)


def _rel_distance():
    i = lax.broadcasted_iota(jnp.int32, (CHUNK, 2 * CHUNK), 0)
    j = lax.broadcasted_iota(jnp.int32, (CHUNK, 2 * CHUNK), 1)
    return i + CHUNK - j


def _bias_build(table):
    def body(tab_ref, o_ref):
        rel = _rel_distance()
        ge = [rel >= t for t in BUCKET_THR]
        for h in range(B_HEADS):
            cur = jnp.full((CHUNK, 2 * CHUNK), tab_ref[0, h], F32)
            for b in range(1, N_BUCKETS):
                cur = jnp.where(ge[b - 1], tab_ref[b, h], cur)
            o_ref[h] = cur

    return pl.pallas_call(
        body, name="bias_build",
        in_specs=[pl.BlockSpec(memory_space=pltpu.SMEM)],
        out_specs=pl.BlockSpec(memory_space=pltpu.VMEM),
        out_shape=jax.ShapeDtypeStruct((B_HEADS, CHUNK, 2 * CHUNK), F32),
    )(table)


def _bias_grad(dbias):
    def body(db_ref, o_ref, acc_ref):
        rel = _rel_distance()
        lo = [0] + BUCKET_THR
        hi = BUCKET_THR + [CHUNK]
        for b in range(N_BUCKETS):
            m = (rel >= lo[b]) & (rel < hi[b])
            for h in range(B_HEADS):
                row = b * B_HEADS + h
                acc_ref[row:row + 1, :] = jnp.sum(jnp.where(m, db_ref[h], 0.0), axis=0, keepdims=True)
        o_ref[...] = jnp.sum(acc_ref[...], axis=1, keepdims=True)

    return pl.pallas_call(
        body, name="bias_grad",
        in_specs=[pl.BlockSpec(memory_space=pltpu.VMEM)],
        out_specs=pl.BlockSpec(memory_space=pltpu.VMEM),
        out_shape=jax.ShapeDtypeStruct((N_BUCKETS * B_HEADS, 1), F32),
        scratch_shapes=[pltpu.VMEM((N_BUCKETS * B_HEADS, 2 * CHUNK), F32)],
    )(dbias)


def _causal_mask():
    t = lax.broadcasted_iota(jnp.int32, (CHUNK, CHUNK), 0)
    s = lax.broadcasted_iota(jnp.int32, (CHUNK, CHUNK), 1)
    return s <= t


def _band_mask(n):
    rel = _rel_distance()
    j = lax.broadcasted_iota(jnp.int32, (CHUNK, 2 * CHUNK), 1)
    return (rel >= 0) & (rel < CHUNK) & ((n > 0) | (j >= CHUNK))


def _gate_forward(u, v, lg, lb, wc, bs):
    ug = _gelu(u)
    vg = _gelu(v)
    mu = jnp.mean(vg, axis=-1, keepdims=True)
    xc = vg - mu
    rstd = lax.rsqrt(jnp.mean(xc * xc, axis=-1, keepdims=True) + EPS)
    xhat = xc * rstd
    vl = (xhat * lg + lb).astype(BF16)
    mixed = _dot(wc, vl) + bs
    return ug, xhat, rstd, vl, mixed


def _softmax_scores(qk, bias, mask, sink):
    s = qk * SCALE + bias
    s = jnp.where(mask, s, NEG)
    m = jnp.maximum(jnp.max(s, axis=-1, keepdims=True), sink)
    p = jnp.exp(s - m)
    e_sink = jnp.exp(sink - m)
    inv = 1.0 / (jnp.sum(p, axis=-1, keepdims=True) + e_sink)
    return p * inv, e_sink * inv


PAIRS = Q_PER_KV // 2


def _head(g, pr, e):
    return g * Q_PER_KV + 2 * pr + e


def _stack_pairs(ref, g, col0=0):
    w = 2 * HEAD_DIM
    return jnp.concatenate([ref[:, col0 + (g * PAIRS + pr) * w:col0 + (g * PAIRS + pr + 1) * w] for pr in range(PAIRS)],
                           axis=0)


def _low_lanes():
    return lax.broadcasted_iota(jnp.int32, (2 * CHUNK, 2 * HEAD_DIM), 1) < HEAD_DIM


def _band_operands(kv_prev, kv_cur):
    band = jnp.concatenate([kv_prev, kv_cur], axis=0)
    low = _low_lanes()
    ops = []
    for cat in (band[:, :KV_WIDTH], band[:, KV_WIDTH:]):
        rol = pltpu.roll(cat, HEAD_DIM, 1)
        ops.append([[jnp.where(low if e == 0 else ~low, cat if g == e else rol, 0.0).astype(BF16) for e in range(2)]
                    for g in range(2)])
    return ops


def _mixer_fwd(proj, lg, lb, wsp, bs_col, sinks, bias, ga, gb, rides=()):
    T = proj.shape[0]
    nb = T // CHUNK

    def body(u_ref, v_ref, q_ref, kvc_ref, kvp_ref, lg_ref, lb_ref, w_ref, bs_ref, sink_ref, bias_ref,
             ga_ref, gb_ref, mixed_ref, mixed_t_ref, ab_ref):
        n = pl.program_id(0)
        causal = _causal_mask()
        ssq = jnp.zeros((CHUNK, 1), F32)
        for g in range(A_GROUPS):
            cols = slice(g * CHUNK, (g + 1) * CHUNK)
            wc = jnp.where(causal, w_ref[g], 0.0).astype(BF16)
            ug, _, _, _, mixed = _gate_forward(u_ref[:, cols], v_ref[:, cols], lg_ref[g:g + 1, :], lb_ref[g:g + 1, :],
                                               wc, bs_ref[g])
            a = ug * mixed
            ab_ref[:, cols] = a
            ssq = ssq + jnp.sum(a * a, axis=-1, keepdims=True)
        ra = lax.rsqrt(ssq * (1.0 / A_WIDTH) + EPS)
        mixed_ref[:, :A_WIDTH] = ((ab_ref[:, :A_WIDTH] * ra) * ga_ref[...]).astype(BF16)

        mask = _band_mask(n)
        kops, vops = _band_operands(kvp_ref[...], kvc_ref[...])
        ssq = jnp.zeros((CHUNK, 1), F32)
        for g in range(B_HEADS // Q_PER_KV):
            qst = _stack_pairs(q_ref, g).astype(BF16)
            o_st = jnp.zeros((PAIRS * CHUNK, 2 * HEAD_DIM), F32)
            for e in range(2):
                s_all = _dot_nt(qst, kops[g][e])
                ps = []
                for pr in range(PAIRS):
                    h = _head(g, pr, e)
                    p, _ = _softmax_scores(s_all[pr * CHUNK:(pr + 1) * CHUNK], bias_ref[h], mask, sink_ref[0, h])
                    ps.append(p.astype(BF16))
                o_st = o_st + _dot(jnp.concatenate(ps, axis=0), vops[g][e])
            for pr in range(PAIRS):
                o = o_st[pr * CHUNK:(pr + 1) * CHUNK]
                c0 = A_WIDTH + (g * PAIRS + pr) * 2 * HEAD_DIM
                ab_ref[:, c0:c0 + 2 * HEAD_DIM] = o
                ssq = ssq + jnp.sum(o * o, axis=-1, keepdims=True)
        rb = lax.rsqrt(ssq * (1.0 / B_WIDTH) + EPS)
        mixed_ref[:, A_WIDTH:] = ((ab_ref[:, A_WIDTH:] * rb) * gb_ref[...]).astype(BF16)
        mixed_t_ref[...] = mixed_ref[...].T

    full = lambda *shape: pl.BlockSpec(shape, lambda n: (0,) * len(shape))
    return _call(
        body, name="mixer_fwd", grid=(nb,),
        in_specs=[pl.BlockSpec((CHUNK, A_WIDTH), lambda n: (n, 0)),
                  pl.BlockSpec((CHUNK, A_WIDTH), lambda n: (n, 1)),
                  pl.BlockSpec((CHUNK, B_WIDTH), lambda n: (n, 2)),
                  pl.BlockSpec((CHUNK, 2 * KV_WIDTH), lambda n: (n, 12)),
                  pl.BlockSpec((CHUNK, 2 * KV_WIDTH), lambda n: (jnp.maximum(n - 1, 0), 12)),
                  full(A_GROUPS, CHUNK), full(A_GROUPS, CHUNK), full(A_GROUPS, CHUNK, CHUNK), full(A_GROUPS, CHUNK, 1),
                  pl.BlockSpec(memory_space=pltpu.SMEM), full(B_HEADS, CHUNK, 2 * CHUNK),
                  full(1, A_WIDTH), full(1, B_WIDTH)],
        out_specs=[pl.BlockSpec((CHUNK, D_MODEL), lambda n: (n, 0)), pl.BlockSpec((D_MODEL, CHUNK), lambda n: (0, n)),
                   pl.BlockSpec((CHUNK, D_MODEL), lambda n: (n, 0))],
        out_shape=[jax.ShapeDtypeStruct((T, D_MODEL), BF16), jax.ShapeDtypeStruct((D_MODEL, T), BF16),
                   jax.ShapeDtypeStruct((T, D_MODEL), F32)],
        sem=("parallel",), rides=rides,
    )(proj, proj, proj, proj, proj, lg, lb, wsp, bs_col, sinks, bias, ga, gb)


def _gmlp_bwd(proj, ab, dmixed, ga, lg, lb, wsp, bs_col, rides=()):
    T = proj.shape[0]
    nb = T // CHUNK

    def body(u_ref, v_ref, a_ref, dna_ref, ga_ref, lg_ref, lb_ref, w_ref, bs_ref,
             dp_ref, dpt_ref, dga_ref, dw_ref, dbs_ref, dlg_ref, dlb_ref):
        @pl.when(pl.program_id(0) == 0)
        def _():
            for r in (dga_ref, dw_ref, dbs_ref, dlg_ref, dlb_ref):
                r[...] = jnp.zeros_like(r)
        causal = _causal_mask()
        a_all = a_ref[...]
        dna = dna_ref[...]
        da_all, ra = _rms_bwd(dna, a_all, ga_ref[...])
        dga_ref[...] += jnp.sum(dna * (a_all * ra), axis=0, keepdims=True)
        for g in range(A_GROUPS):
            cols = slice(g * CHUNK, (g + 1) * CHUNK)
            wc = jnp.where(causal, w_ref[g], 0.0).astype(BF16)
            lgg = lg_ref[g:g + 1, :]
            u = u_ref[:, cols]
            v = v_ref[:, cols]
            ug, xhat, rstd, vl, mixed = _gate_forward(u, v, lgg, lb_ref[g:g + 1, :], wc, bs_ref[g])
            da = da_all[:, cols]
            dug = da * mixed
            dmg = da * ug
            dmg_b = dmg.astype(BF16)
            dbs_ref[g] += jnp.sum(dmg, axis=-1, keepdims=True)
            dw_ref[g] += jnp.where(causal, _dot_nt(dmg_b, vl), 0.0)
            dvl = _dot_tn(wc, dmg_b)
            dlg_ref[g:g + 1, :] += jnp.sum(dvl * xhat, axis=0, keepdims=True)
            dlb_ref[g:g + 1, :] += jnp.sum(dvl, axis=0, keepdims=True)
            dxh = dvl * lgg
            dvg = rstd * (dxh - jnp.mean(dxh, axis=-1, keepdims=True)
                          - xhat * jnp.mean(dxh * xhat, axis=-1, keepdims=True))
            _, gu = _gelu_and_grad(u)
            _, gv = _gelu_and_grad(v)
            dp_ref[:, cols] = (dug * gu).astype(BF16)
            dp_ref[:, A_WIDTH + g * CHUNK:A_WIDTH + (g + 1) * CHUNK] = (dvg * gv).astype(BF16)
        dpt_ref[...] = dp_ref[...].T

    full = lambda *shape: pl.BlockSpec(shape, lambda n: (0,) * len(shape))
    return _call(
        body, name="gmlp_bwd", grid=(nb,),
        in_specs=[pl.BlockSpec((CHUNK, A_WIDTH), lambda n: (n, 0)),
                  pl.BlockSpec((CHUNK, A_WIDTH), lambda n: (n, 1)),
                  pl.BlockSpec((CHUNK, A_WIDTH), lambda n: (n, 0)),
                  pl.BlockSpec((CHUNK, A_WIDTH), lambda n: (n, 0)),
                  full(1, A_WIDTH), full(A_GROUPS, CHUNK), full(A_GROUPS, CHUNK), full(A_GROUPS, CHUNK, CHUNK),
                  full(A_GROUPS, CHUNK, 1)],
        out_specs=[pl.BlockSpec((CHUNK, 2 * A_WIDTH), lambda n: (n, 0)), pl.BlockSpec((2 * A_WIDTH, CHUNK), lambda n: (0, n)),
                   full(1, A_WIDTH), full(A_GROUPS, CHUNK, CHUNK), full(A_GROUPS, CHUNK, 1),
                   full(A_GROUPS, CHUNK), full(A_GROUPS, CHUNK)],
        out_shape=[jax.ShapeDtypeStruct((T, 2 * A_WIDTH), BF16), jax.ShapeDtypeStruct((2 * A_WIDTH, T), BF16),
                   jax.ShapeDtypeStruct((1, A_WIDTH), F32), jax.ShapeDtypeStruct((A_GROUPS, CHUNK, CHUNK), F32),
                   jax.ShapeDtypeStruct((A_GROUPS, CHUNK, 1), F32), jax.ShapeDtypeStruct((A_GROUPS, CHUNK), F32),
                   jax.ShapeDtypeStruct((A_GROUPS, CHUNK), F32)],
        sem=("arbitrary",), rides=rides,
    )(proj, proj, ab, dmixed, ga, lg, lb, wsp, bs_col)


def _attn_bwd(proj, ab, dmixed, gb, sinks, bias, rides=()):
    T = proj.shape[0]
    nb = T // CHUNK
    qn = lambda n: jnp.minimum(n, nb - 1)

    def body(q_ref, kvc_ref, kvp_ref, o_ref, dnb_ref, gb_ref, sink_ref, bias_ref,
             dq_ref, dkv_ref, dqt_ref, dkvt_ref, dgb_ref, dsink_ref, dbias_ref, carry_ref, sacc_ref):
        n = pl.program_id(0)

        @pl.when(n == 0)
        def _():
            carry_ref[...] = jnp.zeros_like(carry_ref)
            sacc_ref[...] = jnp.zeros_like(sacc_ref)
            dgb_ref[...] = jnp.zeros_like(dgb_ref)
            dbias_ref[...] = jnp.zeros_like(dbias_ref)

        @pl.when(n < nb)
        def _():
            mask = _band_mask(n)
            o_all = o_ref[...]
            dnb = dnb_ref[...]
            do_all, rb = _rms_bwd(dnb, o_all, gb_ref[...])
            dgb_ref[...] += jnp.sum(dnb * (o_all * rb), axis=0, keepdims=True)
            kops, vops = _band_operands(kvp_ref[...], kvc_ref[...])
            low = _low_lanes()
            halves = []
            for g in range(B_HEADS // Q_PER_KV):
                qst = _stack_pairs(q_ref, g).astype(BF16)
                dost = _stack_pairs(do_all, g).astype(BF16)
                dq_st = jnp.zeros((PAIRS * CHUNK, 2 * HEAD_DIM), F32)
                dk_e, dv_e = [], []
                for e in range(2):
                    s_all = _dot_nt(qst, kops[g][e])
                    dp_all = _dot_nt(dost, vops[g][e])
                    ps, dsrs = [], []
                    for pr in range(PAIRS):
                        h = _head(g, pr, e)
                        rows = slice(pr * CHUNK, (pr + 1) * CHUNK)
                        p, p_sink = _softmax_scores(s_all[rows], bias_ref[h], mask, sink_ref[0, h])
                        dp = dp_all[rows]
                        delta = jnp.sum(p * dp, axis=-1, keepdims=True)
                        ds = p * (dp - delta)
                        sacc_ref[:, h:h + 1] += -(p_sink * delta)
                        dbias_ref[h] += ds
                        ps.append(p.astype(BF16))
                        dsrs.append((ds * SCALE).astype(BF16))
                    dsr_all = jnp.concatenate(dsrs, axis=0)
                    dq_st = dq_st + _dot(dsr_all, kops[g][e])
                    dk_e.append(_dot_tn(dsr_all, qst))
                    dv_e.append(_dot_tn(jnp.concatenate(ps, axis=0), dost))
                for pr in range(PAIRS):
                    c0 = (g * PAIRS + pr) * 2 * HEAD_DIM
                    dq_ref[:, c0:c0 + 2 * HEAD_DIM] = dq_st[pr * CHUNK:(pr + 1) * CHUNK].astype(BF16)
                halves.append((dk_e, dv_e))
            tiles = []
            for t in range(2):
                g0, g1 = halves[0][t], halves[1][t]
                tiles.append(jnp.where(low, g0[0] + pltpu.roll(g0[1], HEAD_DIM, 1), pltpu.roll(g1[0], HEAD_DIM, 1) + g1[1]))
            dband = jnp.concatenate(tiles, axis=1)
            dkv = (carry_ref[...] + dband[:CHUNK]).astype(BF16)
            dkv_ref[...] = dkv
            dkvt_ref[...] = dkv.T
            dqt_ref[...] = dq_ref[...].T
            carry_ref[...] = dband[CHUNK:]

        @pl.when(n == nb)
        def _():
            dkv = carry_ref[...].astype(BF16)
            dkv_ref[...] = dkv
            dkvt_ref[...] = dkv.T
            dsink_ref[...] = jnp.sum(sacc_ref[...], axis=0, keepdims=True)

    full = lambda *shape: pl.BlockSpec(shape, lambda n: (0,) * len(shape))
    return _call(
        body, name="attn_bwd", grid=(nb + 1,),
        in_specs=[pl.BlockSpec((CHUNK, B_WIDTH), lambda n: (qn(n), 2)),
                  pl.BlockSpec((CHUNK, 2 * KV_WIDTH), lambda n: (qn(n), 12)),
                  pl.BlockSpec((CHUNK, 2 * KV_WIDTH), lambda n: (jnp.maximum(qn(n) - 1, 0), 12)),
                  pl.BlockSpec((CHUNK, B_WIDTH), lambda n: (qn(n), 1)),
                  pl.BlockSpec((CHUNK, B_WIDTH), lambda n: (qn(n), 1)),
                  full(1, B_WIDTH), pl.BlockSpec(memory_space=pltpu.SMEM), full(B_HEADS, CHUNK, 2 * CHUNK)],
        out_specs=[pl.BlockSpec((CHUNK, B_WIDTH), lambda n: (qn(n), 0)),
                   pl.BlockSpec((CHUNK, 2 * KV_WIDTH), lambda n: (jnp.maximum(n - 1, 0), 0)),
                   pl.BlockSpec((B_WIDTH, CHUNK), lambda n: (0, qn(n))),
                   pl.BlockSpec((2 * KV_WIDTH, CHUNK), lambda n: (0, jnp.maximum(n - 1, 0))),
                   full(1, B_WIDTH), full(1, B_HEADS), full(B_HEADS, CHUNK, 2 * CHUNK)],
        out_shape=[jax.ShapeDtypeStruct((T, B_WIDTH), BF16), jax.ShapeDtypeStruct((T, 2 * KV_WIDTH), BF16),
                   jax.ShapeDtypeStruct((B_WIDTH, T), BF16), jax.ShapeDtypeStruct((2 * KV_WIDTH, T), BF16),
                   jax.ShapeDtypeStruct((1, B_WIDTH), F32), jax.ShapeDtypeStruct((1, B_HEADS), F32),
                   jax.ShapeDtypeStruct((B_HEADS, CHUNK, 2 * CHUNK), F32)],
        scratch_shapes=[pltpu.VMEM((CHUNK, 2 * KV_WIDTH), F32), pltpu.VMEM((CHUNK, B_HEADS), F32)],
        sem=("arbitrary",), rides=rides,
    )(proj, proj, proj, ab, dmixed, gb, sinks, bias)


def _sq_relu_grad(acc, r):
    return acc * (2.0 * r.astype(F32))


def _local_step(x, tgt, sp, win, wo, wu, wd):
    T = x.shape[0]
    tm = min(512, T)
    tk = min(512, T)
    lg = sp["gate_norm_g"].reshape(A_GROUPS, CHUNK)
    lb = sp["gate_norm_b"].reshape(A_GROUPS, CHUNK)
    wsp = sp["w_spatial"].reshape(A_GROUPS, CHUNK, CHUNK)
    bs_col = sp["b_spatial"].reshape(A_GROUPS, CHUNK, 1)
    sinks = sp["attn_sinks"].reshape(1, B_HEADS)
    ga = sp["out_norm_a_g"].reshape(1, A_WIDTH)
    gb = sp["out_norm_b_g"].reshape(1, B_WIDTH)
    g1 = sp["mix_norm_g"].reshape(1, D_MODEL)
    g2 = sp["ffn_norm_g"].reshape(1, D_MODEL)
    gf = sp["final_norm_g"].reshape(1, D_MODEL)

    bias = _bias_build(sp["rel_bias_table"])
    n1, proj = _norm_matmul(x, g1, win, tm=tm, tn=PROJ_WIDTH // 2, name="in_proj")
    mixed, ab = _mixer_fwd(proj, lg, lb, wsp, bs_col, sinks, bias, ga, gb)
    h1 = _matmul_res(mixed, wo, x, tm=tm, tn=1024, tk=D_MODEL, prologue=_to_bf16, name="out_proj")
    n2, zp = _norm_matmul(h1, g2, wu, tm=tm, tn=1024, name="up_proj")
    h2 = _matmul_res(zp, wd, h1, tm=tm, tn=1024, tk=2048, prologue=_sq_relu_bf16, name="down_proj")

    dh2, dgf, loss = _loss_bwd(h2, tgt, gf, tm=tm)
    dzp = _matmul_nt(dh2, wd, tm=tm, tn=1024, tk=D_MODEL, name="bwd_dz", extra=zp, epilogue=_sq_relu_grad,
                     out_dtype=BF16)
    dwd = _matmul_tn(zp, dh2, tmo=1024, tn=1024, tk=tk, name="grad_w_down", a_prologue=_sq_relu_bf16)
    dwu = _matmul_tn(n2, dzp, tmo=1024, tn=1024, tk=tk, name="grad_w_up", shards=N_CHIPS)
    dn2 = _matmul_nt(dzp, wu, tm=tm, tn=1024, tk=2048, name="bwd_dn2")
    dh1, dg2 = _rms_bwd_res(dn2, h1, g2, dh2, tm=tm, name="ffn_norm_bwd")
    dwo = _matmul_tn(mixed, dh1, tmo=1024, tn=1024, tk=tk, name="grad_w_out")
    dmixed = _matmul_nt(dh1, wo, tm=tm, tn=1024, tk=D_MODEL, name="bwd_dmixed")
    duv, dga, dwsp, dbs, dlg, dlb = _gmlp_bwd(proj, ab, dmixed, ga, lg, lb, wsp, bs_col)
    dq, dkv, dgb, dsinks, dbias = _attn_bwd(proj, ab, dmixed, gb, sinks, bias)
    dtable = _bias_grad(dbias)
    dproj = jnp.concatenate([duv, dq, dkv], axis=1)
    dwin = _matmul_tn(n1, dproj, tmo=1024, tn=PROJ_WIDTH // 2, tk=tk, name="grad_w_in")
    dn1 = _matmul_nt(dproj, win, tm=tm, tn=1024, tk=PROJ_WIDTH, name="bwd_dn1")
    dx, dg1 = _rms_bwd_res(dn1, x, g1, dh1, tm=tm, name="mix_norm_bwd")

    small = {
        "rel_bias_table": dtable.reshape(N_BUCKETS, B_HEADS), "mix_norm_g": dg1, "gate_norm_g": dlg, "gate_norm_b": dlb,
        "w_spatial": dwsp, "b_spatial": dbs, "attn_sinks": dsinks, "out_norm_a_g": dga, "out_norm_b_g": dgb,
        "ffn_norm_g": dg2, "final_norm_g": dgf,
    }
    return loss, dx, (dwin, dwo, dwu, dwd), small


def _place():
    x, y, c = lax.axis_index("x"), lax.axis_index("y"), lax.axis_index("c")
    chips = [(1 - x, y), (x, 1 - y), (1 - x, 1 - y)]
    return x, y, c, chips


def _remote(src, dst, send_sem, recv_sem, to):
    return pltpu.make_async_remote_copy(src_ref=src, dst_ref=dst, send_sem=send_sem, recv_sem=recv_sem,
                                        device_id=to, device_id_type=MESH)


def _core_index():
    return lax.axis_index("c").astype(jnp.int32).reshape(1)


def _chip_index():
    return (2 * lax.axis_index("x") + lax.axis_index("y")).astype(jnp.int32).reshape(1)


def _cast_into_slot(w, *, tm, name):
    _, R, C = w.shape

    def body(me_ref, w_ref, o_ref):
        del me_ref
        o_ref[...] = w_ref[...].astype(BF16)

    return pl.pallas_call(
        body, name=name,
        grid_spec=pltpu.PrefetchScalarGridSpec(
            num_scalar_prefetch=1, grid=(R // tm,),
            in_specs=[pl.BlockSpec((None, tm, C), lambda i, me: (0, i, 0))],
            out_specs=pl.BlockSpec((None, tm, C), lambda i, me: (me[0], i, 0))),
        out_shape=jax.ShapeDtypeStruct((N_CHIPS, R, C), BF16), compiler_params=_params(("parallel",)),
    )(_chip_index(), w)


def _cast_into_slot_carrying(w, *, tm, name, rides):
    _, R, C = w.shape

    def body(w_ref, o_ref):
        o_ref[...] = w_ref[...].astype(BF16)

    return _call(
        body, name=name, grid=(R // tm,),
        in_specs=[pl.BlockSpec((None, tm, C), lambda i: (0, i, 0))],
        out_specs=pl.BlockSpec((None, tm, C), lambda i: (2 * lax.axis_index("x") + lax.axis_index("y"), i, 0)),
        out_shape=jax.ShapeDtypeStruct((N_CHIPS, R, C), BF16), sem=("arbitrary",), rides=rides,
    )(w)


def _gather_weights(slots):
    nw = len(slots)

    def body(*refs):
        fulls = refs[nw:2 * nw]
        send_sems, recv_sems = refs[2 * nw:]
        x, y, c, chips = _place()
        me = 2 * x + y
        sends = []
        for w in range(nw):
            hr = fulls[w].shape[1] // 2
            rows = pl.ds(c * hr, hr)
            for j, chip in enumerate(chips):
                mine = fulls[w].at[me, rows, :]
                cp = _remote(mine, mine, send_sems.at[6 * w + j], recv_sems.at[6 * w + j], (*chip, c))
                cp.start()
                sends.append(cp)
        for w in range(nw):
            hr = fulls[w].shape[1] // 2
            rows = pl.ds(c * hr, hr)
            for j, chip in enumerate(chips):
                landed = fulls[w].at[2 * chip[0] + chip[1], rows, :]
                _remote(landed, landed, send_sems.at[6 * w + j], recv_sems.at[6 * w + j], (x, y, c)).wait_recv()
                cp = _remote(landed, landed, send_sems.at[6 * w + 3 + j], recv_sems.at[6 * w + 3 + j], (x, y, 1 - c))
                cp.start()
                sends.append(cp)
        for w in range(nw):
            hr = fulls[w].shape[1] // 2
            rows = pl.ds((1 - c) * hr, hr)
            for j, chip in enumerate(chips):
                other = fulls[w].at[2 * chip[0] + chip[1], rows, :]
                _remote(other, other, send_sems.at[6 * w + 3 + j], recv_sems.at[6 * w + 3 + j], (x, y, c)).wait_recv()
        for cp in sends:
            cp.wait_send()

    any_spec = pl.BlockSpec(memory_space=pl.ANY)
    return pl.pallas_call(
        body, name="gather_weights",
        in_specs=[any_spec] * nw, out_specs=[any_spec] * nw,
        out_shape=[jax.ShapeDtypeStruct(s.shape, s.dtype) for s in slots],
        scratch_shapes=[pltpu.SemaphoreType.DMA((6 * nw,)), pltpu.SemaphoreType.DMA((6 * nw,))],
        input_output_aliases={w: w for w in range(nw)},
    )(*slots)


def _sibling_halves(grads):
    nw = len(grads)

    def body(*refs):
        gs, outs = refs[:nw], refs[nw:2 * nw]
        send_sems, recv_sems = refs[2 * nw:]
        x, y, c, _ = _place()
        cps = []
        for w in range(nw):
            hr = gs[w].shape[1] // 2
            cp = _remote(gs[w].at[:, pl.ds((1 - c) * hr, hr), :], outs[w], send_sems.at[w], recv_sems.at[w],
                         (x, y, 1 - c))
            cp.start()
            cps.append(cp)
        for cp in cps:
            cp.wait()

    any_spec = pl.BlockSpec(memory_space=pl.ANY)
    return pl.pallas_call(
        body, name="rs_sibling_halves",
        in_specs=[any_spec] * nw, out_specs=[any_spec] * nw,
        out_shape=[jax.ShapeDtypeStruct((g.shape[0], g.shape[1] // 2, g.shape[2]), g.dtype) for g in grads],
        scratch_shapes=[pltpu.SemaphoreType.DMA((nw,)), pltpu.SemaphoreType.DMA((nw,))],
    )(*grads)


def _pair_sum_bf16(g, got, *, tm, name):
    S, R, C = g.shape
    hr = R // 2
    nt = hr // tm

    def body(c_ref, g_ref, got_ref, o_ref):
        del c_ref
        o_ref[...] = (g_ref[...] + got_ref[...]).astype(BF16)

    return pl.pallas_call(
        body, name=name,
        grid_spec=pltpu.PrefetchScalarGridSpec(
            num_scalar_prefetch=1, grid=(S, nt),
            in_specs=[pl.BlockSpec((None, tm, C), lambda s, i, c: (s, c[0] * nt + i, 0)),
                      pl.BlockSpec((None, tm, C), lambda s, i, c: (s, i, 0))],
            out_specs=pl.BlockSpec((None, tm, C), lambda s, i, c: (s, i, 0))),
        out_shape=jax.ShapeDtypeStruct((S, hr, C), BF16),
        compiler_params=_params(("parallel", "parallel")),
    )(_core_index(), g, got)


def _scatter_to_owners(pairs):
    nw = len(pairs)

    def body(*refs):
        qs, outs = refs[:nw], refs[nw:2 * nw]
        send_sems, recv_sems = refs[2 * nw:]
        x, y, c, chips = _place()
        cps = []
        for w in range(nw):
            for j, chip in enumerate(chips):
                cp = _remote(qs[w].at[2 * chip[0] + chip[1]], outs[w].at[j], send_sems.at[3 * w + j],
                             recv_sems.at[3 * w + j], (*chip, c))
                cp.start()
                cps.append(cp)
        for cp in cps:
            cp.wait()

    any_spec = pl.BlockSpec(memory_space=pl.ANY)
    return pl.pallas_call(
        body, name="rs_scatter_to_owners",
        in_specs=[any_spec] * nw, out_specs=[any_spec] * nw,
        out_shape=[jax.ShapeDtypeStruct((3,) + q.shape[1:], q.dtype) for q in pairs],
        scratch_shapes=[pltpu.SemaphoreType.DMA((3 * nw,)), pltpu.SemaphoreType.DMA((3 * nw,))],
    )(*pairs)


def _owner_total(gh, others, *, tm, name):
    _, hr, C = gh.shape

    def body(me_ref, g_ref, o_ref_in, out_ref):
        del me_ref
        acc = g_ref[...]
        for j in range(3):
            acc = acc + o_ref_in[j].astype(F32)
        out_ref[...] = acc

    return pl.pallas_call(
        body, name=name,
        grid_spec=pltpu.PrefetchScalarGridSpec(
            num_scalar_prefetch=1, grid=(hr // tm,),
            in_specs=[pl.BlockSpec((None, tm, C), lambda i, me: (me[0], i, 0)),
                      pl.BlockSpec((3, tm, C), lambda i, me: (0, i, 0))],
            out_specs=pl.BlockSpec((tm, C), lambda i, me: (i, 0))),
        out_shape=jax.ShapeDtypeStruct((hr, C), F32),
        compiler_params=_params(("parallel",)),
    )(_chip_index(), gh, others)


def _owner_sum(g, got, others, *, tm, name):
    S, R, C = g.shape
    hr = R // 2
    nt = hr // tm

    def body(idx_ref, g_ref, got_ref, o_ref_in, out_ref):
        del idx_ref
        acc = g_ref[...] + got_ref[...]
        for j in range(3):
            acc = acc + o_ref_in[j].astype(F32)
        out_ref[...] = acc

    return pl.pallas_call(
        body, name=name,
        grid_spec=pltpu.PrefetchScalarGridSpec(
            num_scalar_prefetch=1, grid=(nt,),
            in_specs=[pl.BlockSpec((None, tm, C), lambda i, p: (p[1], p[0] * nt + i, 0)),
                      pl.BlockSpec((None, tm, C), lambda i, p: (p[1], i, 0)),
                      pl.BlockSpec((3, tm, C), lambda i, p: (0, i, 0))],
            out_specs=pl.BlockSpec((tm, C), lambda i, p: (i, 0))),
        out_shape=jax.ShapeDtypeStruct((hr, C), F32),
        compiler_params=_params(("parallel",)),
    )(jnp.concatenate([_core_index(), _chip_index()]), g, got, others)


def _swap_halves(halves):
    nw = len(halves)

    def body(*refs):
        hs, outs = refs[:nw], refs[nw:2 * nw]
        send_sems, recv_sems = refs[2 * nw:]
        x, y, c, _ = _place()
        cps = []
        for w in range(nw):
            cp = _remote(hs[w], outs[w], send_sems.at[w], recv_sems.at[w], (x, y, 1 - c))
            cp.start()
            cps.append(cp)
        for cp in cps:
            cp.wait()

    any_spec = pl.BlockSpec(memory_space=pl.ANY)
    return pl.pallas_call(
        body, name="rs_swap_halves",
        in_specs=[any_spec] * nw, out_specs=[any_spec] * nw,
        out_shape=[jax.ShapeDtypeStruct(h.shape, h.dtype) for h in halves],
        scratch_shapes=[pltpu.SemaphoreType.DMA((nw,)), pltpu.SemaphoreType.DMA((nw,))],
    )(*halves)


def _all_reduce_small(packed):
    R, C = packed.shape

    def body(in_ref, out_ref, slots, send_sems, recv_sems):
        x, y, c, _ = _place()
        me = 4 * x + 2 * y + c
        cps = []
        for k in range(1, N_DEV):
            p = (me + k) % N_DEV
            cp = _remote(in_ref, slots.at[me], send_sems.at[k - 1], recv_sems.at[k - 1], (p // 4, (p // 2) % 2, p % 2))
            cp.start()
            cps.append(cp)
        slots[me] = in_ref[...]
        for k in range(1, N_DEV):
            src = (me + N_DEV - k) % N_DEV
            _remote(in_ref, slots.at[src], send_sems.at[k - 1], recv_sems.at[k - 1], (x, y, c)).wait_recv()
        for cp in cps:
            cp.wait_send()
        acc = slots[0]
        for d in range(1, N_DEV):
            acc = acc + slots[d]
        out_ref[...] = acc

    vmem = pl.BlockSpec(memory_space=pltpu.VMEM)
    return pl.pallas_call(
        body, name="all_reduce_small", in_specs=[vmem], out_specs=vmem,
        out_shape=jax.ShapeDtypeStruct((R, C), F32),
        scratch_shapes=[pltpu.VMEM((N_DEV, R, C), F32), pltpu.SemaphoreType.DMA((N_DEV - 1,)),
                        pltpu.SemaphoreType.DMA((N_DEV - 1,))],
        compiler_params=_params(),
    )(packed)


def _adamw_math(w, g, m, v):
    m = ADAM_B1 * m + (1.0 - ADAM_B1) * g
    v = ADAM_B2 * v + (1.0 - ADAM_B2) * (g * g)
    m_hat = m / (1.0 - ADAM_B1 ** ADAM_STEP)
    v_hat = v / (1.0 - ADAM_B2 ** ADAM_STEP)
    delta = -ADAM_LR * (m_hat / (jnp.sqrt(v_hat) + ADAM_EPS) + ADAM_WD * w)
    return delta, m, v


def _adamw(w, g, m, v, *, tm, name):
    R, C = w.shape

    def body(w_ref, g_ref, m_ref, v_ref, d_ref, nm_ref, nv_ref):
        d_ref[...], nm_ref[...], nv_ref[...] = _adamw_math(w_ref[...], g_ref[...], m_ref[...], v_ref[...])

    spec = pl.BlockSpec((tm, C), lambda i: (i, 0))
    return pl.pallas_call(
        body, name=name, grid=(R // tm,), in_specs=[spec] * 4, out_specs=[spec] * 3,
        out_shape=[jax.ShapeDtypeStruct((R, C), F32)] * 3, compiler_params=_params(("parallel",)),
    )(w, g, m, v)


def _adamw_halves(w, own, got, m, v, *, tm, name, rides=()):
    _, R, C = w.shape
    nt = (R // 2) // tm

    def body(w_ref, own_ref, got_ref, m_ref, v_ref, g_ref, d_ref, nm_ref, nv_ref):
        g = jnp.where(pl.program_id(0) == lax.axis_index("c"), own_ref[...], got_ref[...])
        g_ref[...] = g
        d_ref[...], nm_ref[...], nv_ref[...] = _adamw_math(w_ref[...], g, m_ref[...], v_ref[...])

    whole = pl.BlockSpec((None, tm, C), lambda h, i: (0, h * nt + i, 0))
    half = pl.BlockSpec((tm, C), lambda h, i: (i, 0))
    return _call(
        body, name=name, grid=(2, nt), in_specs=[whole, half, half, whole, whole], out_specs=[whole] * 4,
        out_shape=[jax.ShapeDtypeStruct((1, R, C), F32)] * 4, sem=("parallel", "parallel"), rides=rides,
    )(w, own, got, m, v)


def _adamw_small(w, slots, m, v, *, name):
    def body(w_ref, slots_ref, m_ref, v_ref, g_ref, d_ref, nm_ref, nv_ref):
        g = slots_ref[0]
        for d in range(1, N_DEV):
            g = g + slots_ref[d]
        g_ref[...] = g
        d_ref[...], nm_ref[...], nv_ref[...] = _adamw_math(w_ref[...], g, m_ref[...], v_ref[...])

    vmem = pl.BlockSpec(memory_space=pltpu.VMEM)
    return pl.pallas_call(
        body, name=name, in_specs=[vmem] * 4, out_specs=[vmem] * 4,
        out_shape=[jax.ShapeDtypeStruct(w.shape, F32)] * 4, compiler_params=_params(),
    )(w, slots, m, v)


SMALL = ["rel_bias_table", "mix_norm_g", "gate_norm_g", "gate_norm_b", "w_spatial", "b_spatial", "attn_sinks",
         "out_norm_a_g", "out_norm_b_g", "ffn_norm_g", "final_norm_g"]
SMALL_A = ["gate_norm_g", "gate_norm_b", "w_spatial", "b_spatial", "out_norm_a_g"]
SMALL_B = ["rel_bias_table", "mix_norm_g", "attn_sinks", "out_norm_b_g", "ffn_norm_g", "final_norm_g"]
LARGE = ["w_in", "w_out", "w_up", "w_down"]
ROW_TILE = {"w_in": 208, "w_out": 256, "w_up": 256, "w_down": 256}
WEIGHTS = ["rel_bias_table", "mix_norm_g", "w_in", "gate_norm_g", "gate_norm_b", "w_spatial", "b_spatial", "attn_sinks",
           "out_norm_a_g", "out_norm_b_g", "w_out", "ffn_norm_g", "w_up", "w_down", "final_norm_g"]
PACK_UNIT = 8 * 128


def _pack(parts):
    rows = []
    for p in parts:
        flat = p.reshape(-1)
        pad = (-flat.shape[0]) % PACK_UNIT
        rows.append(jnp.pad(flat, (0, pad)).reshape(-1, 128))
    return jnp.concatenate(rows, axis=0)


def _unpack(packed, like):
    out, row = [], 0
    for p in like:
        n = math.prod(p.shape)
        nrows = (n + PACK_UNIT - 1) // PACK_UNIT * 8
        out.append(packed[row:row + nrows].reshape(-1)[:n].reshape(p.shape))
        row += nrows
    return out


def kernel(x, rel_bias_table, mix_norm_g, w_in, gate_norm_g, gate_norm_b, w_spatial, b_spatial, attn_sinks, out_norm_a_g, out_norm_b_g, w_out, ffn_norm_g, w_up, w_down, final_norm_g, loss_target, m_rel_bias_table, m_mix_norm_g, m_w_in, m_gate_norm_g, m_gate_norm_b, m_w_spatial, m_b_spatial, m_attn_sinks, m_out_norm_a_g, m_out_norm_b_g, m_w_out, m_ffn_norm_g, m_w_up, m_w_down, m_final_norm_g, v_rel_bias_table, v_mix_norm_g, v_w_in, v_gate_norm_g, v_gate_norm_b, v_w_spatial, v_b_spatial, v_attn_sinks, v_out_norm_a_g, v_out_norm_b_g, v_w_out, v_ffn_norm_g, v_w_up, v_w_down, v_final_norm_g):
    args = dict(locals())
    wts = {n: args[n] for n in WEIGHTS}
    mom = {n: args["m_" + n] for n in WEIGHTS}
    var = {n: args["v_" + n] for n in WEIGHTS}
    sp = {n: wts[n] for n in SMALL}
    x2, tgt = x[0], loss_target[0]
    T = x2.shape[0]
    tm = min(512, T)
    tl = min(1024, T)
    tg = min(2048, T)
    lg = sp["gate_norm_g"].reshape(A_GROUPS, CHUNK)
    lb = sp["gate_norm_b"].reshape(A_GROUPS, CHUNK)
    wsp = sp["w_spatial"].reshape(A_GROUPS, CHUNK, CHUNK)
    bs_col = sp["b_spatial"].reshape(A_GROUPS, CHUNK, 1)
    sinks = sp["attn_sinks"].reshape(1, B_HEADS)
    ga = sp["out_norm_a_g"].reshape(1, A_WIDTH)
    gb = sp["out_norm_b_g"].reshape(1, B_WIDTH)
    g1 = sp["mix_norm_g"].reshape(1, D_MODEL)
    g2 = sp["ffn_norm_g"].reshape(1, D_MODEL)
    gf = sp["final_norm_g"].reshape(1, D_MODEL)

    def owner_total(n, gh, others):
        return _owner_total(gh, others, tm=ROW_TILE[n], name="rs_owner_total_" + n)

    def halves_view(at, shards):
        return at.reshape(shards, 2, at.shape[0] // shards // 2, at.shape[1])

    for d in (wts, mom, var):
        d["w_in"] = jnp.swapaxes(d["w_in"], 1, 2)

    s_in = _cast_into_slot(wts["w_in"], tm=ROW_TILE["w_in"], name="cast_w_in")
    s_out = _cast_into_slot(wts["w_out"], tm=256, name="cast_w_out")
    s_up, ((s_in,),) = _cast_into_slot_carrying(wts["w_up"], tm=256, name="cast_w_up",
                                                rides=[_ride_gather(s_in, s1=(0, 1, 1))])
    s_down, ((s_in,), (s_out,)) = _cast_into_slot_carrying(
        wts["w_down"], tm=256, name="cast_w_down", rides=[_ride_gather(s_in, s2=(0, 1, 1)), _ride_gather(s_out, s1=(0, 1, 1))])
    (g_in,), (s_out,), (s_up,) = _carrier(
        [_ride_gather(s_in, s3=(0, 1, 1)), _ride_gather(s_out, s2=(0, 1, 1)), _ride_gather(s_up, s1=(0, 1, 8))],
        name="gather_stage")
    win_t = g_in.reshape(PROJ_WIDTH, D_MODEL)
    bias = _bias_build(sp["rel_bias_table"])
    (n1, proj), ((g_out,), (s_up,)) = _norm_matmul_wide(
        x2, g1, win_t, tm=tm, tn=PROJ_WIDTH // 2, name="in_proj",
        rides=[_ride_gather(s_out, s3=(0, 1, 1)), _ride_gather(s_up, s2=(0, 1, 8), s1=(1, 8, 8))])
    wo = g_out.reshape(A_WIDTH + B_WIDTH, D_MODEL)
    (mixed, mixed_t, ab), ((s_up,), (s_down,), (n1_sib,)) = _mixer_fwd(
        proj, lg, lb, wsp, bs_col, sinks, bias, ga, gb,
        rides=[_ride_gather(s_up, s3=(0, 1, 8), s2=(1, 8, 8)), _ride_gather(s_down, s1=(0, 6, 8)),
               _ride_to_sibling(n1, first=True)])
    mixed_t = halves_view(mixed_t, N_CHIPS)
    h1, ((wu,), (s_down,), (mixed_t_sib,)) = _matmul_res(
        mixed, wo, x2, tm=tl, tn=1024, tk=D_MODEL, prologue=_to_bf16, name="out_proj",
        rides=[_ride_gather(s_up, s3=(1, 8, 8)), _ride_gather(s_down, s2=(0, 6, 8), s1=(6, 8, 8)),
               _ride_to_sibling(mixed_t, halves=True)])
    (n2t, zp, z2, z2t), ((g_down,),) = _norm_matmul_sq(
        h1, g2, wu, tm=tl, tn=1024, name="up_proj", rides=[_ride_gather(s_down, s3=(0, 6, 8), tail=(6, 8, 8))])
    wd = g_down.reshape(D_FF, D_MODEL)
    n2t, z2t = halves_view(n2t, 1), halves_view(z2t, N_CHIPS)
    h2, ((n2t_sib,), (z2t_sib,)) = _matmul_res(
        z2, wd, h1, tm=tl, tn=1024, tk=4096, prologue=_to_bf16, name="down_proj",
        rides=[_ride_to_sibling(n2t, halves=True), _ride_to_sibling(z2t, halves=True)])

    dh2, dh2b, dgf, loss = _loss_bwd(h2, tgt, gf, tm=tm)
    dzp, ((dh2b_sib,),) = _matmul_nt(dh2b, wd, tm=tl, tn=1024, tk=D_MODEL, name="bwd_dz", extra=zp,
                                     epilogue=_sq_relu_grad, out_dtype=BF16, rides=[_ride_to_sibling(dh2b)])
    (gd, gdb), ((dzp_sib,),) = _grad_pair(z2t, z2t_sib, dh2b, dh2b_sib, cols_sharded=False, tmo=1024, tk=tl,
                                          name="grad_w_down", rides=[_ride_to_sibling(dzp)])
    (gu, gub), ((o_d,),) = _grad_pair(n2t, n2t_sib, dzp, dzp_sib, cols_sharded=True, tmo=1024, tk=tl,
                                      name="grad_w_up", rides=[_ride_scatter(gdb, None, (0, 7, 8))])
    dn2, ((o_d,), (o_u,)) = _matmul_nt(dzp, wu, tm=tl, tn=1024, tk=4096, name="bwd_dn2",
                                       rides=[_ride_scatter(gdb, o_d, (7, 8, 8)), _ride_scatter(gub, None, (0, 6, 8))])
    h_d = owner_total("w_down", gd, o_d)
    (dh1, dh1b, dg2), ((o_u,),) = _rms_bwd_res(dn2, h1, g2, dh2, tm=tm, name="ffn_norm_bwd",
                                               rides=[_ride_scatter(gub, o_u, (6, 7, 8))])
    dmixed, ((o_u,), (dh1b_sib,), (w_d,)) = _matmul_nt(
        dh1b, wo, tm=tl, tn=1024, tk=D_MODEL, name="bwd_dmixed",
        rides=[_ride_scatter(gub, o_u, (7, 8, 8)), _ride_to_sibling(dh1b), _ride_swap(h_d)])
    h_u = owner_total("w_up", gu, o_u)
    (go, gob), ((w_u,),) = _grad_pair_merged(mixed_t, mixed_t_sib, dh1b, dh1b_sib, tk=tl, name="grad_w_out",
                                             rides=[_ride_swap(h_u)])
    (duv, duv_t, dga, dwsp, dbs, dlg, dlb), ((o_o,),) = _gmlp_bwd(proj, ab, dmixed, ga, lg, lb, wsp, bs_col,
                                                                  rides=[_ride_scatter(gob)])
    h_o = owner_total("w_out", go, o_o)
    small = {"gate_norm_g": dlg, "gate_norm_b": dlb, "w_spatial": dwsp, "b_spatial": dbs, "out_norm_a_g": dga}
    hr_in = PROJ_WIDTH // N_CHIPS // 2
    (dq, dkv, dq_t, dkv_t, dgb, dsinks, dbias), ((w_o,), (dproj_t_sib,)) = _attn_bwd(
        proj, ab, dmixed, gb, sinks, bias, rides=[_ride_swap(h_o), _ride_rows_to_sibling(duv_t, hr_in, 2, N_CHIPS)])
    dtable = _bias_grad(dbias)
    dproj_t = halves_view(jnp.concatenate([duv_t, dq_t, dkv_t], axis=0), N_CHIPS)
    ((dproj_t_sib,),) = _carrier([_ride_to_sibling(dproj_t, halves=True, shards=(2, N_CHIPS), land=dproj_t_sib)],
                                 name="trade_dproj_t")
    (gi, gib), ((slots_a,),) = _grad_pair(
        dproj_t, dproj_t_sib, n1, n1_sib, cols_sharded=False, tmo=hr_in, tk=tl, name="grad_w_in",
        rides=[_ride_small_to_all(_pack([small[n] for n in SMALL_A]))])
    dn1, ((o_i,),) = _matmul_parts([duv, dq, dkv], win_t, tm=tl, tn=1024, name="bwd_dn1", rides=[_ride_scatter(gib)])
    h_i = owner_total("w_in", gi, o_i)
    dx, _, dg1 = _rms_bwd_res(dn1, x2, g1, dh1, tm=tm, name="mix_norm_bwd")
    small.update({"rel_bias_table": dtable.reshape(N_BUCKETS, B_HEADS), "mix_norm_g": dg1, "attn_sinks": dsinks,
                  "out_norm_b_g": dgb, "ffn_norm_g": dg2, "final_norm_g": dgf})
    (w_i,), (slots_b,) = _carrier([_ride_swap(h_i), _ride_small_to_all(_pack([small[n] for n in SMALL_B] + [loss]))],
                                  name="swap_w_in")

    out_g, out_d, out_m, out_v = {}, {}, {}, {}
    for n, h, s in zip(LARGE, [h_i, h_o, h_u, h_d], [w_i, w_o, w_u, w_d]):
        res = _adamw_halves(wts[n], h, s, mom[n], var[n], tm=ROW_TILE[n], name="adamw_" + n)
        if n == "w_in":
            res = [jnp.swapaxes(r, 1, 2) for r in res]
        out_g[n], out_d[n], out_m[n], out_v[n] = res
    for names, slots, tag in ((SMALL_A, slots_a, "a"), (SMALL_B, slots_b, "b")):
        extra = [jnp.zeros((1, 1), F32)] if tag == "b" else []
        like = [wts[n] for n in names] + extra
        res = _adamw_small(_pack(like), slots, _pack([mom[n] for n in names] + extra),
                           _pack([var[n] for n in names] + extra), name="adamw_small_" + tag)
        for store, packed in zip((out_g, out_d, out_m, out_v), res):
            for n, val in zip(names + ["loss"], _unpack(packed, like)):
                store[n] = val

    total = out_g["loss"][0, 0]
    return (total, dx[None], *[out_g[n] for n in WEIGHTS], *[out_d[n] for n in WEIGHTS],
            *[out_m[n] for n in WEIGHTS], *[out_v[n] for n in WEIGHTS])
```

```python
import functools
import math

import numpy as np
import jax
import jax.numpy as jnp
from jax import lax
from jax.experimental import pallas as pl
from jax.experimental.pallas import tpu as pltpu

F32 = jnp.float32
BF16 = jnp.bfloat16

D_MODEL = 2048
CHUNK = 128
A_GROUPS = 8
A_WIDTH = 1024
HEAD_DIM = 64
B_HEADS = 16
Q_PER_KV = 8
B_WIDTH = 1024
KV_WIDTH = 128
PROJ_WIDTH = 3328
D_FF = 8192
N_BUCKETS = 32
EPS = 1e-5
NEG = -1e30
SCALE = HEAD_DIM ** -0.5
N_CHIPS = 4
N_DEV = 8

ADAM_LR = 0.001
ADAM_B1 = 0.9
ADAM_B2 = 0.999
ADAM_EPS = 1e-08
ADAM_WD = 0.01
ADAM_STEP = 10

VMEM_LIMIT = 60 * 1024 * 1024
MESH = pl.DeviceIdType.MESH


def _bucket_thresholds():
    d = np.arange(CHUNK)
    n_exact = N_BUCKETS // 2
    relf = np.maximum(d, n_exact).astype(np.float64)
    large = n_exact + (np.log(relf / n_exact) / math.log(CHUNK / n_exact) * (N_BUCKETS - n_exact)).astype(np.int32)
    bucket = np.where(d < n_exact, d, np.minimum(large, N_BUCKETS - 1))
    return [int(np.min(d[bucket >= b])) for b in range(1, N_BUCKETS)]


BUCKET_THR = _bucket_thresholds()


def _params(sem=None):
    return pltpu.CompilerParams(dimension_semantics=sem, vmem_limit_bytes=VMEM_LIMIT)


def _gelu(x):
    c = math.sqrt(2.0 / math.pi)
    return 0.5 * x * (1.0 + jnp.tanh(c * (x + 0.044715 * (x * x * x))))


def _gelu_and_grad(x):
    c = math.sqrt(2.0 / math.pi)
    x2 = x * x
    t = jnp.tanh(c * (x + 0.044715 * (x2 * x)))
    g = 0.5 * x * (1.0 + t)
    dg = 0.5 * (1.0 + t) + 0.5 * x * (1.0 - t * t) * (c * (1.0 + 3.0 * 0.044715 * x2))
    return g, dg


def _dot(a, b):
    return jnp.dot(a, b, preferred_element_type=F32)


def _dot_nt(a, b):
    return lax.dot_general(a, b, (((1,), (1,)), ((), ())), preferred_element_type=F32)


def _dot_tn(a, b):
    return lax.dot_general(a, b, (((0,), (0,)), ((), ())), preferred_element_type=F32)


def _rms_bwd(dn, h, g):
    r = lax.rsqrt(jnp.mean(h * h, axis=-1, keepdims=True) + EPS)
    w = dn * g
    dh = r * w - h * ((r * r * r) * jnp.mean(w * h, axis=-1, keepdims=True))
    return dh, r


def _place():
    x, y, c = lax.axis_index("x"), lax.axis_index("y"), lax.axis_index("c")
    chips = [(1 - x, y), (x, 1 - y), (1 - x, 1 - y)]
    return x, y, c, chips


def _remote(src, dst, send_sem, recv_sem, to):
    return pltpu.make_async_remote_copy(src_ref=src, dst_ref=dst, send_sem=send_sem, recv_sem=recv_sem,
                                        device_id=to, device_id_type=MESH)


class _Ride:
    def __init__(self, args, out_shape, n_sem, start, finish, mid=None, mid_frac=0.8, aliases=None):
        self.args, self.out_shape, self.n_sem = list(args), list(out_shape), n_sem
        self.start, self.mid, self.finish, self.mid_frac = start, mid, finish, mid_frac
        self.aliases = dict(aliases or {})


def _call(body, *, name, grid, in_specs, out_specs, out_shape, scratch_shapes=(), sem=None, rides=()):
    single = not isinstance(out_shape, (list, tuple))
    out_specs = [out_specs] if single else list(out_specs)
    out_shape = [out_shape] if single else list(out_shape)
    n_in, n_out, n_scr = len(in_specs), len(out_shape), len(scratch_shapes)
    r_in = [len(r.args) for r in rides]
    r_out = [len(r.out_shape) for r in rides]
    any_spec = pl.BlockSpec(memory_space=pl.ANY)
    aliases, off_i, off_o = {}, n_in, n_out
    for r in rides:
        for i, o in r.aliases.items():
            aliases[off_i + i] = off_o + o
        off_i += len(r.args)
        off_o += len(r.out_shape)
    steps = math.prod(grid)

    def wrapped(*refs):
        p = 0
        ins = refs[p:p + n_in]; p += n_in
        rins = refs[p:p + sum(r_in)]; p += sum(r_in)
        outs = refs[p:p + n_out]; p += n_out
        routs = refs[p:p + sum(r_out)]; p += sum(r_out)
        scr = refs[p:p + n_scr]; p += n_scr
        sems = refs[p:]
        parts, pi, po = [], 0, 0
        for k, r in enumerate(rides):
            parts.append((rins[pi:pi + r_in[k]], routs[po:po + r_out[k]], sems[2 * k], sems[2 * k + 1]))
            pi += r_in[k]
            po += r_out[k]
        lin = 0
        for d in range(len(grid)):
            lin = lin * grid[d] + pl.program_id(d)
        if rides:
            @pl.when(lin == 0)
            def _():
                for r, part in zip(rides, parts):
                    r.start(*part)
        body(*ins, *outs, *scr)
        for r, part in zip(rides, parts):
            if r.mid is not None:
                @pl.when(lin == min(steps - 1, int(r.mid_frac * steps)))
                def _(r=r, part=part):
                    r.mid(*part)
        if rides:
            @pl.when(lin == steps - 1)
            def _():
                for r, part in zip(rides, parts):
                    r.finish(*part)

    scratch = list(scratch_shapes)
    for r in rides:
        scratch += [pltpu.SemaphoreType.DMA((r.n_sem,)), pltpu.SemaphoreType.DMA((r.n_sem,))]
    if rides:
        sem = ("arbitrary",) * len(grid)
    res = pl.pallas_call(
        wrapped, name=name, grid=grid,
        in_specs=list(in_specs) + [any_spec] * sum(r_in),
        out_specs=out_specs + [any_spec] * sum(r_out),
        out_shape=out_shape + [s for r in rides for s in r.out_shape],
        scratch_shapes=scratch, input_output_aliases=aliases,
        compiler_params=_params(sem),
    )

    def run(*args):
        got = res(*args, *[a for r in rides for a in r.args])
        mine = got[0] if single else list(got[:n_out])
        if not rides:
            return mine
        rest, out = list(got[n_out:]), []
        for k in range(len(rides)):
            out.append(rest[:r_out[k]])
            rest = rest[r_out[k]:]
        return mine, out

    return run


def _ride_gather(slot, s1=None, s2=None, s3=None, tail=None, direct=None, mid_frac=0.6):
    half = slot.shape[1] // 2

    def rows(part, c, which=None):
        k0, k1, n = part
        count, first = (k1 - k0) * (half // n), c * half + k0 * (half // n)
        return pl.ds(first, count) if which is None else pl.ds(first + which * (count // 2), count // 2)

    def ids():
        x, y, c, _ = _place()
        return x, y, c, 2 * x + y, 2 * (1 - x) + y, 2 * x + (1 - y), 2 * (1 - x) + (1 - y)

    def copy(full, chip, r, ss, rs, k, to):
        piece = full.at[chip, r, :]
        return _remote(piece, piece, ss.at[k], rs.at[k], to)

    def to_neighbours(full, ss, rs, part, base):
        x, y, c, me, _, _, _ = ids()
        return [copy(full, me, rows(part, c), ss, rs, base, (1 - x, y, c)),
                copy(full, me, rows(part, c), ss, rs, base + 1, (x, 1 - y, c))]

    def from_neighbours(full, ss, rs, part, base):
        x, y, c, _, cx, cy, _ = ids()
        return [copy(full, cx, rows(part, c), ss, rs, base, (x, y, c)), copy(full, cy, rows(part, c), ss, rs, base + 1, (x, y, c))]

    def onward(full, ss, rs, part, base):
        x, y, c, _, cx, cy, _ = ids()
        return [copy(full, cx, rows(part, c, 0), ss, rs, base, (x, 1 - y, c)),
                copy(full, cy, rows(part, c, 1), ss, rs, base + 1, (1 - x, y, c))]

    def from_onward(full, ss, rs, part, base):
        x, y, c, _, _, _, cd = ids()
        return [copy(full, cd, rows(part, c, 0), ss, rs, base, (x, y, c)), copy(full, cd, rows(part, c, 1), ss, rs, base + 1, (x, y, c))]

    def to_sibling(full, ss, rs, part, base, diagonal):
        x, y, c, _, cx, cy, cd = ids()
        return [copy(full, chip, rows(part, c), ss, rs, base + j, (x, y, 1 - c))
                for j, chip in enumerate([cd] if diagonal else [cx, cy])]

    def from_sibling(full, ss, rs, part, base, diagonal):
        x, y, c, _, cx, cy, cd = ids()
        return [copy(full, chip, rows(part, 1 - c), ss, rs, base + j, (x, y, c))
                for j, chip in enumerate([cd] if diagonal else [cx, cy])]

    def to_chips(full, ss, rs, part, base):
        x, y, c, me, _, _, _ = ids()
        chips = [(1 - x, y), (x, 1 - y), (1 - x, 1 - y)]
        return [copy(full, me, rows(part, c), ss, rs, base + j, (*chip, c)) for j, chip in enumerate(chips)]

    def all_three(full, ss, rs, part, base, half_of, to):
        x, y, c, _, cx, cy, cd = ids()
        return [copy(full, chip, rows(part, half_of(c)), ss, rs, base + j, to(x, y, c)) for j, chip in enumerate([cx, cy, cd])]

    def start(ins, outs, ss, rs):
        full, cps = outs[0], []
        if s1 is not None:
            cps += to_neighbours(full, ss, rs, s1, 0)
        for part, b_ici, b_sib in ((s2, 2, 4), (tail, 7, 9)):
            if part is not None:
                cps += onward(full, ss, rs, part, b_ici) + to_sibling(full, ss, rs, part, b_sib, False)
        if s3 is not None:
            cps += to_sibling(full, ss, rs, s3, 6, True)
        if direct is not None:
            cps += to_chips(full, ss, rs, direct, 12)
        for cp in cps:
            cp.start()

    def mid(ins, outs, ss, rs):
        if tail is not None:
            for cp in from_onward(outs[0], ss, rs, tail, 7):
                cp.wait_recv()
            for cp in to_sibling(outs[0], ss, rs, tail, 11, True):
                cp.start()
        if direct is not None:
            for cp in all_three(outs[0], ss, rs, direct, 12, lambda c: c, lambda x, y, c: (x, y, c)):
                cp.wait_recv()
            for cp in all_three(outs[0], ss, rs, direct, 15, lambda c: c, lambda x, y, c: (x, y, 1 - c)):
                cp.start()

    def finish(ins, outs, ss, rs):
        full, got, sent = outs[0], [], []
        if s1 is not None:
            got += from_neighbours(full, ss, rs, s1, 0)
            sent += to_neighbours(full, ss, rs, s1, 0)
        if s2 is not None:
            got += from_onward(full, ss, rs, s2, 2) + from_sibling(full, ss, rs, s2, 4, False)
            sent += onward(full, ss, rs, s2, 2) + to_sibling(full, ss, rs, s2, 4, False)
        if s3 is not None:
            got += from_sibling(full, ss, rs, s3, 6, True)
            sent += to_sibling(full, ss, rs, s3, 6, True)
        if tail is not None:
            got += from_sibling(full, ss, rs, tail, 9, False) + from_sibling(full, ss, rs, tail, 11, True)
            sent += onward(full, ss, rs, tail, 7) + to_sibling(full, ss, rs, tail, 9, False) + to_sibling(full, ss, rs, tail, 11, True)
        if direct is not None:
            got += all_three(full, ss, rs, direct, 15, lambda c: 1 - c, lambda x, y, c: (x, y, c))
            sent += to_chips(full, ss, rs, direct, 12)
            sent += all_three(full, ss, rs, direct, 15, lambda c: c, lambda x, y, c: (x, y, 1 - c))
        for cp in got:
            cp.wait_recv()
        for cp in sent:
            cp.wait_send()

    return _Ride([slot], [jax.ShapeDtypeStruct(slot.shape, slot.dtype)], 18, start, finish,
                 mid=mid if (tail is not None or direct is not None) else None, mid_frac=mid_frac, aliases={0: 0})


def _ride_sibling_halves(g):
    S, R, C = g.shape
    hr = R // 2

    def copy(ins, outs, ss, rs):
        x, y, c, _ = _place()
        return _remote(ins[0].at[:, pl.ds((1 - c) * hr, hr), :], outs[0], ss.at[0], rs.at[0], (x, y, 1 - c))

    return _Ride([g], [jax.ShapeDtypeStruct((S, hr, C), g.dtype)], 1,
                 lambda *a: copy(*a).start(), lambda *a: copy(*a).wait())


def _ride_scatter(q, land=None, part=(0, 1)):
    k0, k1, n = part if len(part) == 3 else (part[0], part[0] + 1, part[1])
    rows_n = q.shape[1] // n
    rows = pl.ds(k0 * rows_n, (k1 - k0) * rows_n)

    def copies(ins, outs, ss, rs):
        x, y, c, chips = _place()
        return [_remote(ins[0].at[2 * chip[0] + chip[1], rows, :], outs[0].at[j, rows, :], ss.at[j], rs.at[j], (*chip, c))
                for j, chip in enumerate(chips)]

    def start(*a):
        for cp in copies(*a):
            cp.start()

    def finish(*a):
        for cp in copies(*a):
            cp.wait()

    shape = jax.ShapeDtypeStruct((3,) + q.shape[1:], q.dtype)
    if land is None:
        return _Ride([q], [shape], 3, start, finish)
    return _Ride([q, land], [shape], 3, start, finish, aliases={1: 0})


def _ride_to_sibling(a, halves=False, first=False, shards=None, land=None):
    s0, s1 = shards or (0, a.shape[0])

    def copy(ins, outs, ss, rs):
        x, y, c, _ = _place()
        if halves:
            src, dst = ins[0].at[s0:s1, 1 - c], outs[0].at[s0:s1]
        else:
            src, dst = (ins[0].at[0] if first else ins[0]), outs[0]
        return _remote(src, dst, ss.at[0], rs.at[0], (x, y, 1 - c))

    shape = (a.shape[0],) + a.shape[2:] if halves else (a.shape[1:] if first else a.shape)
    return _Ride([a] if land is None else [a, land], [jax.ShapeDtypeStruct(shape, a.dtype)], 1,
                 lambda *a_: copy(*a_).start(), lambda *a_: copy(*a_).wait(), aliases=None if land is None else {1: 0})


def _ride_rows_to_sibling(a, hr, shards, total):
    def copies(ins, outs, ss, rs):
        x, y, c, _ = _place()
        return [_remote(ins[0].at[pl.ds((2 * s + 1 - c) * hr, hr), :], outs[0].at[s], ss.at[s], rs.at[s], (x, y, 1 - c))
                for s in range(shards)]

    def start(*a_):
        for cp in copies(*a_):
            cp.start()

    def finish(*a_):
        for cp in copies(*a_):
            cp.wait()

    return _Ride([a], [jax.ShapeDtypeStruct((total, hr, a.shape[1]), a.dtype)], shards, start, finish)


def _ride_swap(h):
    def copy(ins, outs, ss, rs):
        x, y, c, _ = _place()
        return _remote(ins[0], outs[0], ss.at[0], rs.at[0], (x, y, 1 - c))

    return _Ride([h], [jax.ShapeDtypeStruct(h.shape, h.dtype)], 1,
                 lambda *a: copy(*a).start(), lambda *a: copy(*a).wait())


def _mesh_place(p):
    return (p // 4, (p // 2) % 2, p % 2)


def _ride_small_to_all(packed):
    def copies(ins, outs, ss, rs):
        x, y, c, _ = _place()
        me = 4 * x + 2 * y + c
        return [_remote(ins[0], outs[0].at[me], ss.at[k - 1], rs.at[k - 1], _mesh_place((me + k) % N_DEV))
                for k in range(1, N_DEV)]

    def own(ins, outs, ss, rs):
        x, y, c, _ = _place()
        return pltpu.make_async_copy(ins[0], outs[0].at[4 * x + 2 * y + c], ss.at[N_DEV - 1])

    def start(*a):
        own(*a).start()
        for cp in copies(*a):
            cp.start()

    def finish(ins, outs, ss, rs):
        x, y, c, _ = _place()
        me = 4 * x + 2 * y + c
        for k in range(1, N_DEV):
            _remote(ins[0], outs[0].at[(me + N_DEV - k) % N_DEV], ss.at[k - 1], rs.at[k - 1], (x, y, c)).wait_recv()
        for cp in copies(ins, outs, ss, rs):
            cp.wait_send()
        own(ins, outs, ss, rs).wait()

    return _Ride([packed], [jax.ShapeDtypeStruct((N_DEV,) + packed.shape, packed.dtype)], N_DEV, start, finish)


def _carrier(rides, *, name):
    _, outs = _call(lambda: None, name=name, grid=(1,), in_specs=[], out_specs=[], out_shape=[], rides=rides)()
    return outs


def _sq_relu_bf16(z):
    z = jnp.maximum(z, 0.0)
    return (z * z).astype(BF16)


def _norm_bf16(a_ref, g_ref):
    xf = a_ref[...]
    r = lax.rsqrt(jnp.mean(xf * xf, axis=-1, keepdims=True) + EPS)
    return ((xf * r) * g_ref[...]).astype(BF16)


def _norm_matmul_wide(a, g, b, *, tm, tn, name, rides=()):
    T, K = a.shape
    N = b.shape[0]

    def body(a_ref, g_ref, b_ref, n_ref, o_ref):
        n = _norm_bf16(a_ref, g_ref)
        n_ref[...] = n
        o_ref[...] = _dot_nt(n, b_ref[...])

    return _call(
        body, name=name, grid=(N // tn, T // tm),
        in_specs=[pl.BlockSpec((tm, K), lambda j, i: (i, 0)), pl.BlockSpec((1, K), lambda j, i: (0, 0)),
                  pl.BlockSpec((tn, K), lambda j, i: (j, 0))],
        out_specs=[pl.BlockSpec((None, tm, K), lambda j, i: (j, i, 0)), pl.BlockSpec((tm, tn), lambda j, i: (i, j))],
        out_shape=[jax.ShapeDtypeStruct((N // tn, T, K), BF16), jax.ShapeDtypeStruct((T, N), F32)],
        sem=("arbitrary", "arbitrary"), rides=rides,
    )(a, g, b)


def _norm_matmul_sq(a, g, b, *, tm, tn, name, rides=()):
    T, K = a.shape
    per = b.shape[2] // tn
    N = b.shape[0] * b.shape[2]

    def body(a_ref, g_ref, b_ref, nt_ref, o_ref, z_ref, zt_ref, n_scr):
        @pl.when(pl.program_id(1) == 0)
        def _():
            n = _norm_bf16(a_ref, g_ref)
            n_scr[...] = n
            nt_ref[...] = n.T
        r = jnp.maximum(_dot(n_scr[...], b_ref[...]), 0.0)
        o_ref[...] = r.astype(BF16)
        z = (r * r).astype(BF16)
        z_ref[...] = z
        zt_ref[...] = z.T

    return _call(
        body, name=name, grid=(T // tm, N // tn),
        in_specs=[pl.BlockSpec((tm, K), lambda i, j: (i, 0)), pl.BlockSpec((1, K), lambda i, j: (0, 0)),
                  pl.BlockSpec((None, K, tn), lambda i, j: (j // per, 0, j % per))],
        out_specs=[pl.BlockSpec((K, tm), lambda i, j: (0, i)), pl.BlockSpec((tm, tn), lambda i, j: (i, j)),
                   pl.BlockSpec((tm, tn), lambda i, j: (i, j)), pl.BlockSpec((tn, tm), lambda i, j: (j, i))],
        out_shape=[jax.ShapeDtypeStruct((K, T), BF16), jax.ShapeDtypeStruct((T, N), BF16),
                   jax.ShapeDtypeStruct((T, N), BF16), jax.ShapeDtypeStruct((N, T), BF16)],
        scratch_shapes=[pltpu.VMEM((tm, K), BF16)],
        sem=("parallel", "arbitrary"), rides=rides,
    )(a, g, b)


def _grad_pair(at, at_sib, b, b_sib, *, cols_sharded, tmo, tk, name, rides=()):
    S, _, hr, T = at.shape
    C = b.shape[-1] // N_CHIPS if cols_sharded else b.shape[-1]
    nk = T // tk
    a_sel = (lambda s: 0) if cols_sharded else (lambda s: s)
    b_sel = (lambda s: s) if cols_sharded else (lambda s: 0)
    if b.ndim == 3:
        b_spec = pl.BlockSpec((None, tk, C), lambda s, i, k: (0, k, b_sel(s)))
    else:
        b_spec = pl.BlockSpec((tk, C), lambda s, i, k: (k, b_sel(s)))

    def body(a_ref, as_ref, b_ref, bs_ref, o_ref, ob_ref):
        k = pl.program_id(2)
        p = _dot(a_ref[...], b_ref[...]) + _dot(as_ref[...], bs_ref[...])

        @pl.when(k == 0)
        def _():
            o_ref[...] = p

        @pl.when(k > 0)
        def _():
            o_ref[...] += p

        @pl.when(k == nk - 1)
        def _():
            ob_ref[...] = o_ref[...].astype(BF16)

    out = pl.BlockSpec((None, tmo, C), lambda s, i, k: (s, i, 0))
    return _call(
        body, name=name, grid=(N_CHIPS, hr // tmo, nk),
        in_specs=[pl.BlockSpec((None, None, tmo, tk), lambda s, i, k: (a_sel(s), lax.axis_index("c"), i, k)),
                  pl.BlockSpec((None, tmo, tk), lambda s, i, k: (a_sel(s), i, k)),
                  b_spec, pl.BlockSpec((tk, C), lambda s, i, k: (k, b_sel(s)))],
        out_specs=[out, out],
        out_shape=[jax.ShapeDtypeStruct((N_CHIPS, hr, C), F32), jax.ShapeDtypeStruct((N_CHIPS, hr, C), BF16)],
        sem=("parallel", "parallel", "arbitrary"), rides=rides,
    )(at, at_sib, b, b_sib)


def _grad_pair_merged(at, at_sib, b, b_sib, *, tk, name, rides=()):
    S, _, hr, T = at.shape
    C = b.shape[-1]
    nk = T // tk

    def body(a_ref, as_ref, b_ref, bs_ref, o_ref, ob_ref):
        k = pl.program_id(0)
        p = (_dot(a_ref[...].reshape(S * hr, tk), b_ref[...])
             + _dot(as_ref[...].reshape(S * hr, tk), bs_ref[...])).reshape(S, hr, C)

        @pl.when(k == 0)
        def _():
            o_ref[...] = p

        @pl.when(k > 0)
        def _():
            o_ref[...] += p

        @pl.when(k == nk - 1)
        def _():
            ob_ref[...] = o_ref[...].astype(BF16)

    out = pl.BlockSpec((S, hr, C), lambda k: (0, 0, 0))
    return _call(
        body, name=name, grid=(nk,),
        in_specs=[pl.BlockSpec((S, None, hr, tk), lambda k: (0, lax.axis_index("c"), 0, k)),
                  pl.BlockSpec((S, hr, tk), lambda k: (0, 0, k)),
                  pl.BlockSpec((tk, C), lambda k: (k, 0)), pl.BlockSpec((tk, C), lambda k: (k, 0))],
        out_specs=[out, out],
        out_shape=[jax.ShapeDtypeStruct((S, hr, C), F32), jax.ShapeDtypeStruct((S, hr, C), BF16)],
        sem=("arbitrary",), rides=rides,
    )(at, at_sib, b, b_sib)


def _matmul_nn(at, b, *, tmo, tn, tk, name, shards=1, rides=()):
    M, T = at.shape[-2:]
    N = b.shape[-1]
    if at.ndim == 3:
        a_spec = pl.BlockSpec((None, tmo, tk), lambda i, j, k: (0, i, k))
    else:
        a_spec = pl.BlockSpec((tmo, tk), lambda i, j, k: (i, k))
    if b.ndim == 3:
        b_spec = pl.BlockSpec((None, tk, tn), lambda i, j, k: (0, k, j))
    else:
        b_spec = pl.BlockSpec((tk, tn), lambda i, j, k: (k, j))
    if shards > 1:
        per = (N // shards) // tn
        out_spec = pl.BlockSpec((None, tmo, tn), lambda i, j, k: (j // per, i, j % per))
        out_shape = jax.ShapeDtypeStruct((shards, M, N // shards), F32)
    else:
        out_spec = pl.BlockSpec((tmo, tn), lambda i, j, k: (i, j))
        out_shape = jax.ShapeDtypeStruct((M, N), F32)

    def body(a_ref, b_ref, o_ref):
        k = pl.program_id(2)
        p = _dot(a_ref[...], b_ref[...])

        @pl.when(k == 0)
        def _():
            o_ref[...] = p

        @pl.when(k > 0)
        def _():
            o_ref[...] += p

    return _call(
        body, name=name, grid=(M // tmo, N // tn, T // tk),
        in_specs=[a_spec, b_spec],
        out_specs=out_spec, out_shape=out_shape,
        sem=("parallel", "parallel", "arbitrary"), rides=rides,
    )(at, b)


def _matmul_parts(parts, b, *, tm, tn, name, rides=()):
    T = parts[0].shape[0]
    N = b.shape[1]
    offs = [sum(p.shape[1] for p in parts[:i]) for i in range(len(parts))]
    assert all(o % p.shape[1] == 0 for o, p in zip(offs, parts))

    def body(*refs):
        n = len(parts)
        acc = _dot(refs[0][...], refs[n][...])
        for i in range(1, n):
            acc = acc + _dot(refs[i][...], refs[n + i][...])
        refs[-1][...] = acc

    a_specs = [pl.BlockSpec((tm, p.shape[1]), lambda i, j: (i, 0)) for p in parts]
    b_specs = [pl.BlockSpec((p.shape[1], tn), lambda i, j, r=o // p.shape[1]: (r, j)) for o, p in zip(offs, parts)]
    return _call(
        body, name=name, grid=(T // tm, N // tn), in_specs=a_specs + b_specs,
        out_specs=pl.BlockSpec((tm, tn), lambda i, j: (i, j)), out_shape=jax.ShapeDtypeStruct((T, N), F32),
        sem=("parallel", "parallel"), rides=rides,
    )(*parts, *([b] * len(parts)))


def _to_bf16(v):
    return v.astype(BF16)


def _matmul_res(a, b, res, *, tm, tn, tk, prologue, name, rides=()):
    T, K = a.shape
    N = b.shape[1]

    def body(a_ref, b_ref, res_ref, o_ref):
        k = pl.program_id(2)
        p = _dot(prologue(a_ref[...]), b_ref[...])

        @pl.when(k == 0)
        def _():
            o_ref[...] = res_ref[...] + p

        @pl.when(k > 0)
        def _():
            o_ref[...] += p

    return _call(
        body, name=name, grid=(T // tm, N // tn, K // tk),
        in_specs=[pl.BlockSpec((tm, tk), lambda i, j, k: (i, k)), pl.BlockSpec((tk, tn), lambda i, j, k: (k, j)),
                  pl.BlockSpec((tm, tn), lambda i, j, k: (i, j))],
        out_specs=pl.BlockSpec((tm, tn), lambda i, j, k: (i, j)),
        out_shape=jax.ShapeDtypeStruct((T, N), F32),
        sem=("parallel", "parallel", "arbitrary"), rides=rides,
    )(a, b, res)


def _matmul_nt(a, b, *, tm, tn, tk, name, extra=None, epilogue=None, out_dtype=F32, rides=()):
    T, K = a.shape
    two = b.ndim == 3 and tk == 2 * b.shape[2]
    if two:
        N, ks = b.shape[1], b.shape[2]
        b_specs = [pl.BlockSpec((None, tn, ks), lambda i, j, k: (2 * k, j, 0)),
                   pl.BlockSpec((None, tn, ks), lambda i, j, k: (2 * k + 1, j, 0))]
    elif b.ndim == 3:
        per = b.shape[2] // tk
        N = b.shape[1]
        b_specs = [pl.BlockSpec((None, tn, tk), lambda i, j, k: (k // per, j, k % per))]
    else:
        N = b.shape[0]
        b_specs = [pl.BlockSpec((tn, tk), lambda i, j, k: (j, k))]
    nb = len(b_specs)
    nk = K // tk
    assert out_dtype == F32 or nk == 1
    in_specs = [pl.BlockSpec((tm, tk), lambda i, j, k: (i, k))] + b_specs
    args = [a] + [b] * nb
    if extra is not None:
        in_specs.append(pl.BlockSpec((tm, tn), lambda i, j, k: (i, j)))
        args.append(extra)

    def body(*refs):
        a_ref, b_ref = refs[0], refs[1]
        o_ref = refs[-1]
        if two:
            p = (_dot_nt(a_ref[:, :tk // 2].astype(BF16), refs[1][...])
                 + _dot_nt(a_ref[:, tk // 2:].astype(BF16), refs[2][...]))
        else:
            p = _dot_nt(a_ref[...].astype(BF16), b_ref[...])
        if nk == 1:
            if epilogue is not None:
                p = epilogue(p, refs[1 + nb][...])
            o_ref[...] = p.astype(out_dtype)
        else:
            k = pl.program_id(2)

            @pl.when(k == 0)
            def _():
                o_ref[...] = p

            @pl.when(k > 0)
            def _():
                o_ref[...] += p

    return _call(
        body, name=name, grid=(T // tm, N // tn, nk),
        in_specs=in_specs,
        out_specs=pl.BlockSpec((tm, tn), lambda i, j, k: (i, j)),
        out_shape=jax.ShapeDtypeStruct((T, N), out_dtype),
        sem=("parallel", "parallel", "arbitrary"), rides=rides,
    )(*args)


def _matmul_tn(a, b, *, tmo, tn, tk, name, a_prologue=_to_bf16, shards=1, rides=()):
    T, M = a.shape
    N = b.shape[1]
    if shards > 1:
        per = (N // shards) // tn
        out_spec = pl.BlockSpec((None, tmo, tn), lambda i, j, k: (j // per, i, j % per))
        out_shape = jax.ShapeDtypeStruct((shards, M, N // shards), F32)
    else:
        out_spec = pl.BlockSpec((tmo, tn), lambda i, j, k: (i, j))
        out_shape = jax.ShapeDtypeStruct((M, N), F32)

    def body(a_ref, b_ref, o_ref):
        k = pl.program_id(2)
        p = _dot_tn(a_prologue(a_ref[...]), b_ref[...].astype(BF16))

        @pl.when(k == 0)
        def _():
            o_ref[...] = p

        @pl.when(k > 0)
        def _():
            o_ref[...] += p

    return _call(
        body, name=name, grid=(M // tmo, N // tn, T // tk),
        in_specs=[pl.BlockSpec((tk, tmo), lambda i, j, k: (k, i)), pl.BlockSpec((tk, tn), lambda i, j, k: (k, j))],
        out_specs=out_spec, out_shape=out_shape,
        sem=("parallel", "parallel", "arbitrary"), rides=rides,
    )(a, b)


def _loss_bwd(h2, tgt, g, *, tm):
    T, D = h2.shape

    def body(h_ref, t_ref, g_ref, dh_ref, dhb_ref, dg_ref, loss_ref):
        @pl.when(pl.program_id(0) == 0)
        def _():
            dg_ref[...] = jnp.zeros_like(dg_ref)
            loss_ref[...] = jnp.zeros_like(loss_ref)
        h = h_ref[...]
        gg = g_ref[...]
        r = lax.rsqrt(jnp.mean(h * h, axis=-1, keepdims=True) + EPS)
        hn = h * r
        err = hn * gg - t_ref[...]
        loss_ref[...] += 0.5 * jnp.sum(jnp.mean(err * err, axis=-1, keepdims=True), axis=0, keepdims=True)
        dy = err * (1.0 / D)
        dg_ref[...] += jnp.sum(dy * hn, axis=0, keepdims=True)
        w = dy * gg
        dh = r * w - h * ((r * r * r) * jnp.mean(w * h, axis=-1, keepdims=True))
        dh_ref[...] = dh
        dhb_ref[...] = dh.astype(BF16)

    tile = pl.BlockSpec((tm, D), lambda i: (i, 0))
    return pl.pallas_call(
        body, name="loss_bwd", grid=(T // tm,),
        in_specs=[tile, tile, pl.BlockSpec((1, D), lambda i: (0, 0))],
        out_specs=[tile, tile, pl.BlockSpec((1, D), lambda i: (0, 0)), pl.BlockSpec((1, 1), lambda i: (0, 0))],
        out_shape=[jax.ShapeDtypeStruct((T, D), F32), jax.ShapeDtypeStruct((T, D), BF16),
                   jax.ShapeDtypeStruct((1, D), F32), jax.ShapeDtypeStruct((1, 1), F32)],
        compiler_params=_params(("arbitrary",)),
    )(h2, tgt, g)


def _rms_bwd_res(dn, h, g, dres, *, tm, name, rides=()):
    T, D = h.shape

    def body(dn_ref, h_ref, g_ref, dres_ref, dh_ref, dhb_ref, dg_ref):
        @pl.when(pl.program_id(0) == 0)
        def _():
            dg_ref[...] = jnp.zeros_like(dg_ref)
        h_ = h_ref[...]
        dn_ = dn_ref[...]
        dh, r = _rms_bwd(dn_, h_, g_ref[...])
        dg_ref[...] += jnp.sum(dn_ * (h_ * r), axis=0, keepdims=True)
        dh = dres_ref[...] + dh
        dh_ref[...] = dh
        dhb_ref[...] = dh.astype(BF16)

    tile = pl.BlockSpec((tm, D), lambda i: (i, 0))
    return _call(
        body, name=name, grid=(T // tm,),
        in_specs=[tile, tile, pl.BlockSpec((1, D), lambda i: (0, 0)), tile],
        out_specs=[tile, tile, pl.BlockSpec((1, D), lambda i: (0, 0))],
        out_shape=[jax.ShapeDtypeStruct((T, D), F32), jax.ShapeDtypeStruct((T, D), BF16),
                   jax.ShapeDtypeStruct((1, D), F32)],
        sem=("arbitrary",), rides=rides,
    )(dn, h, g, dres)


def _rel_distance():
    i = lax.broadcasted_iota(jnp.int32, (CHUNK, 2 * CHUNK), 0)
    j = lax.broadcasted_iota(jnp.int32, (CHUNK, 2 * CHUNK), 1)
    return i + CHUNK - j


def _bias_build(table):
    def body(tab_ref, o_ref):
        rel = _rel_distance()
        ge = [rel >= t for t in BUCKET_THR]
        for h in range(B_HEADS):
            cur = jnp.full((CHUNK, 2 * CHUNK), tab_ref[0, h], F32)
            for b in range(1, N_BUCKETS):
                cur = jnp.where(ge[b - 1], tab_ref[b, h], cur)
            o_ref[h] = cur

    return pl.pallas_call(
        body, name="bias_build",
        in_specs=[pl.BlockSpec(memory_space=pltpu.SMEM)],
        out_specs=pl.BlockSpec(memory_space=pltpu.VMEM),
        out_shape=jax.ShapeDtypeStruct((B_HEADS, CHUNK, 2 * CHUNK), F32),
    )(table)


def _bias_grad(dbias):
    def body(db_ref, o_ref, acc_ref):
        rel = _rel_distance()
        lo = [0] + BUCKET_THR
        hi = BUCKET_THR + [CHUNK]
        for b in range(N_BUCKETS):
            m = (rel >= lo[b]) & (rel < hi[b])
            for h in range(B_HEADS):
                row = b * B_HEADS + h
                acc_ref[row:row + 1, :] = jnp.sum(jnp.where(m, db_ref[h], 0.0), axis=0, keepdims=True)
        o_ref[...] = jnp.sum(acc_ref[...], axis=1, keepdims=True)

    return pl.pallas_call(
        body, name="bias_grad",
        in_specs=[pl.BlockSpec(memory_space=pltpu.VMEM)],
        out_specs=pl.BlockSpec(memory_space=pltpu.VMEM),
        out_shape=jax.ShapeDtypeStruct((N_BUCKETS * B_HEADS, 1), F32),
        scratch_shapes=[pltpu.VMEM((N_BUCKETS * B_HEADS, 2 * CHUNK), F32)],
    )(dbias)


def _causal_mask():
    t = lax.broadcasted_iota(jnp.int32, (CHUNK, CHUNK), 0)
    s = lax.broadcasted_iota(jnp.int32, (CHUNK, CHUNK), 1)
    return s <= t


def _band_mask(n):
    rel = _rel_distance()
    j = lax.broadcasted_iota(jnp.int32, (CHUNK, 2 * CHUNK), 1)
    return (rel >= 0) & (rel < CHUNK) & ((n > 0) | (j >= CHUNK))


def _gate_forward(u, v, lg, lb, wc, bs):
    ug = _gelu(u)
    vg = _gelu(v)
    mu = jnp.mean(vg, axis=-1, keepdims=True)
    xc = vg - mu
    rstd = lax.rsqrt(jnp.mean(xc * xc, axis=-1, keepdims=True) + EPS)
    xhat = xc * rstd
    vl = (xhat * lg + lb).astype(BF16)
    mixed = _dot(wc, vl) + bs
    return ug, xhat, rstd, vl, mixed


def _softmax_scores(qk, bias, mask, sink):
    s = qk * SCALE + bias
    s = jnp.where(mask, s, NEG)
    m = jnp.maximum(jnp.max(s, axis=-1, keepdims=True), sink)
    p = jnp.exp(s - m)
    e_sink = jnp.exp(sink - m)
    inv = 1.0 / (jnp.sum(p, axis=-1, keepdims=True) + e_sink)
    return p * inv, e_sink * inv


PAIRS = Q_PER_KV // 2


def _head(g, pr, e):
    return g * Q_PER_KV + 2 * pr + e


def _stack_pairs(ref, g, col0=0):
    w = 2 * HEAD_DIM
    return jnp.concatenate([ref[:, col0 + (g * PAIRS + pr) * w:col0 + (g * PAIRS + pr + 1) * w] for pr in range(PAIRS)],
                           axis=0)


def _low_lanes():
    return lax.broadcasted_iota(jnp.int32, (2 * CHUNK, 2 * HEAD_DIM), 1) < HEAD_DIM


def _band_operands(kv_prev, kv_cur):
    band = jnp.concatenate([kv_prev, kv_cur], axis=0)
    low = _low_lanes()
    ops = []
    for cat in (band[:, :KV_WIDTH], band[:, KV_WIDTH:]):
        rol = pltpu.roll(cat, HEAD_DIM, 1)
        ops.append([[jnp.where(low if e == 0 else ~low, cat if g == e else rol, 0.0).astype(BF16) for e in range(2)]
                    for g in range(2)])
    return ops


def _mixer_fwd(proj, lg, lb, wsp, bs_col, sinks, bias, ga, gb, rides=()):
    T = proj.shape[0]
    nb = T // CHUNK

    def body(u_ref, v_ref, q_ref, kvc_ref, kvp_ref, lg_ref, lb_ref, w_ref, bs_ref, sink_ref, bias_ref,
             ga_ref, gb_ref, mixed_ref, mixed_t_ref, ab_ref):
        n = pl.program_id(0)
        causal = _causal_mask()
        ssq = jnp.zeros((CHUNK, 1), F32)
        for g in range(A_GROUPS):
            cols = slice(g * CHUNK, (g + 1) * CHUNK)
            wc = jnp.where(causal, w_ref[g], 0.0).astype(BF16)
            ug, _, _, _, mixed = _gate_forward(u_ref[:, cols], v_ref[:, cols], lg_ref[g:g + 1, :], lb_ref[g:g + 1, :],
                                               wc, bs_ref[g])
            a = ug * mixed
            ab_ref[:, cols] = a
            ssq = ssq + jnp.sum(a * a, axis=-1, keepdims=True)
        ra = lax.rsqrt(ssq * (1.0 / A_WIDTH) + EPS)
        mixed_ref[:, :A_WIDTH] = ((ab_ref[:, :A_WIDTH] * ra) * ga_ref[...]).astype(BF16)

        mask = _band_mask(n)
        kops, vops = _band_operands(kvp_ref[...], kvc_ref[...])
        ssq = jnp.zeros((CHUNK, 1), F32)
        for g in range(B_HEADS // Q_PER_KV):
            qst = _stack_pairs(q_ref, g).astype(BF16)
            o_st = jnp.zeros((PAIRS * CHUNK, 2 * HEAD_DIM), F32)
            for e in range(2):
                s_all = _dot_nt(qst, kops[g][e])
                ps = []
                for pr in range(PAIRS):
                    h = _head(g, pr, e)
                    p, _ = _softmax_scores(s_all[pr * CHUNK:(pr + 1) * CHUNK], bias_ref[h], mask, sink_ref[0, h])
                    ps.append(p.astype(BF16))
                o_st = o_st + _dot(jnp.concatenate(ps, axis=0), vops[g][e])
            for pr in range(PAIRS):
                o = o_st[pr * CHUNK:(pr + 1) * CHUNK]
                c0 = A_WIDTH + (g * PAIRS + pr) * 2 * HEAD_DIM
                ab_ref[:, c0:c0 + 2 * HEAD_DIM] = o
                ssq = ssq + jnp.sum(o * o, axis=-1, keepdims=True)
        rb = lax.rsqrt(ssq * (1.0 / B_WIDTH) + EPS)
        mixed_ref[:, A_WIDTH:] = ((ab_ref[:, A_WIDTH:] * rb) * gb_ref[...]).astype(BF16)
        mixed_t_ref[...] = mixed_ref[...].T

    full = lambda *shape: pl.BlockSpec(shape, lambda n: (0,) * len(shape))
    return _call(
        body, name="mixer_fwd", grid=(nb,),
        in_specs=[pl.BlockSpec((CHUNK, A_WIDTH), lambda n: (n, 0)),
                  pl.BlockSpec((CHUNK, A_WIDTH), lambda n: (n, 1)),
                  pl.BlockSpec((CHUNK, B_WIDTH), lambda n: (n, 2)),
                  pl.BlockSpec((CHUNK, 2 * KV_WIDTH), lambda n: (n, 12)),
                  pl.BlockSpec((CHUNK, 2 * KV_WIDTH), lambda n: (jnp.maximum(n - 1, 0), 12)),
                  full(A_GROUPS, CHUNK), full(A_GROUPS, CHUNK), full(A_GROUPS, CHUNK, CHUNK), full(A_GROUPS, CHUNK, 1),
                  pl.BlockSpec(memory_space=pltpu.SMEM), full(B_HEADS, CHUNK, 2 * CHUNK),
                  full(1, A_WIDTH), full(1, B_WIDTH)],
        out_specs=[pl.BlockSpec((CHUNK, D_MODEL), lambda n: (n, 0)), pl.BlockSpec((D_MODEL, CHUNK), lambda n: (0, n)),
                   pl.BlockSpec((CHUNK, D_MODEL), lambda n: (n, 0))],
        out_shape=[jax.ShapeDtypeStruct((T, D_MODEL), BF16), jax.ShapeDtypeStruct((D_MODEL, T), BF16),
                   jax.ShapeDtypeStruct((T, D_MODEL), F32)],
        sem=("parallel",), rides=rides,
    )(proj, proj, proj, proj, proj, lg, lb, wsp, bs_col, sinks, bias, ga, gb)


def _gmlp_bwd(proj, ab, dmixed, ga, lg, lb, wsp, bs_col, rides=()):
    T = proj.shape[0]
    nb = T // CHUNK

    def body(u_ref, v_ref, a_ref, dna_ref, ga_ref, lg_ref, lb_ref, w_ref, bs_ref,
             dp_ref, dpt_ref, dga_ref, dw_ref, dbs_ref, dlg_ref, dlb_ref):
        @pl.when(pl.program_id(0) == 0)
        def _():
            for r in (dga_ref, dw_ref, dbs_ref, dlg_ref, dlb_ref):
                r[...] = jnp.zeros_like(r)
        causal = _causal_mask()
        a_all = a_ref[...]
        dna = dna_ref[...]
        da_all, ra = _rms_bwd(dna, a_all, ga_ref[...])
        dga_ref[...] += jnp.sum(dna * (a_all * ra), axis=0, keepdims=True)
        for g in range(A_GROUPS):
            cols = slice(g * CHUNK, (g + 1) * CHUNK)
            wc = jnp.where(causal, w_ref[g], 0.0).astype(BF16)
            lgg = lg_ref[g:g + 1, :]
            u = u_ref[:, cols]
            v = v_ref[:, cols]
            ug, xhat, rstd, vl, mixed = _gate_forward(u, v, lgg, lb_ref[g:g + 1, :], wc, bs_ref[g])
            da = da_all[:, cols]
            dug = da * mixed
            dmg = da * ug
            dmg_b = dmg.astype(BF16)
            dbs_ref[g] += jnp.sum(dmg, axis=-1, keepdims=True)
            dw_ref[g] += jnp.where(causal, _dot_nt(dmg_b, vl), 0.0)
            dvl = _dot_tn(wc, dmg_b)
            dlg_ref[g:g + 1, :] += jnp.sum(dvl * xhat, axis=0, keepdims=True)
            dlb_ref[g:g + 1, :] += jnp.sum(dvl, axis=0, keepdims=True)
            dxh = dvl * lgg
            dvg = rstd * (dxh - jnp.mean(dxh, axis=-1, keepdims=True)
                          - xhat * jnp.mean(dxh * xhat, axis=-1, keepdims=True))
            _, gu = _gelu_and_grad(u)
            _, gv = _gelu_and_grad(v)
            dp_ref[:, cols] = (dug * gu).astype(BF16)
            dp_ref[:, A_WIDTH + g * CHUNK:A_WIDTH + (g + 1) * CHUNK] = (dvg * gv).astype(BF16)
        dpt_ref[...] = dp_ref[...].T

    full = lambda *shape: pl.BlockSpec(shape, lambda n: (0,) * len(shape))
    return _call(
        body, name="gmlp_bwd", grid=(nb,),
        in_specs=[pl.BlockSpec((CHUNK, A_WIDTH), lambda n: (n, 0)),
                  pl.BlockSpec((CHUNK, A_WIDTH), lambda n: (n, 1)),
                  pl.BlockSpec((CHUNK, A_WIDTH), lambda n: (n, 0)),
                  pl.BlockSpec((CHUNK, A_WIDTH), lambda n: (n, 0)),
                  full(1, A_WIDTH), full(A_GROUPS, CHUNK), full(A_GROUPS, CHUNK), full(A_GROUPS, CHUNK, CHUNK),
                  full(A_GROUPS, CHUNK, 1)],
        out_specs=[pl.BlockSpec((CHUNK, 2 * A_WIDTH), lambda n: (n, 0)), pl.BlockSpec((2 * A_WIDTH, CHUNK), lambda n: (0, n)),
                   full(1, A_WIDTH), full(A_GROUPS, CHUNK, CHUNK), full(A_GROUPS, CHUNK, 1),
                   full(A_GROUPS, CHUNK), full(A_GROUPS, CHUNK)],
        out_shape=[jax.ShapeDtypeStruct((T, 2 * A_WIDTH), BF16), jax.ShapeDtypeStruct((2 * A_WIDTH, T), BF16),
                   jax.ShapeDtypeStruct((1, A_WIDTH), F32), jax.ShapeDtypeStruct((A_GROUPS, CHUNK, CHUNK), F32),
                   jax.ShapeDtypeStruct((A_GROUPS, CHUNK, 1), F32), jax.ShapeDtypeStruct((A_GROUPS, CHUNK), F32),
                   jax.ShapeDtypeStruct((A_GROUPS, CHUNK), F32)],
        sem=("arbitrary",), rides=rides,
    )(proj, proj, ab, dmixed, ga, lg, lb, wsp, bs_col)


def _attn_bwd(proj, ab, dmixed, gb, sinks, bias, rides=()):
    T = proj.shape[0]
    nb = T // CHUNK
    qn = lambda n: jnp.minimum(n, nb - 1)

    def body(q_ref, kvc_ref, kvp_ref, o_ref, dnb_ref, gb_ref, sink_ref, bias_ref,
             dq_ref, dkv_ref, dqt_ref, dkvt_ref, dgb_ref, dsink_ref, dbias_ref, carry_ref, sacc_ref):
        n = pl.program_id(0)

        @pl.when(n == 0)
        def _():
            carry_ref[...] = jnp.zeros_like(carry_ref)
            sacc_ref[...] = jnp.zeros_like(sacc_ref)
            dgb_ref[...] = jnp.zeros_like(dgb_ref)
            dbias_ref[...] = jnp.zeros_like(dbias_ref)

        @pl.when(n < nb)
        def _():
            mask = _band_mask(n)
            o_all = o_ref[...]
            dnb = dnb_ref[...]
            do_all, rb = _rms_bwd(dnb, o_all, gb_ref[...])
            dgb_ref[...] += jnp.sum(dnb * (o_all * rb), axis=0, keepdims=True)
            kops, vops = _band_operands(kvp_ref[...], kvc_ref[...])
            low = _low_lanes()
            halves = []
            for g in range(B_HEADS // Q_PER_KV):
                qst = _stack_pairs(q_ref, g).astype(BF16)
                dost = _stack_pairs(do_all, g).astype(BF16)
                dq_st = jnp.zeros((PAIRS * CHUNK, 2 * HEAD_DIM), F32)
                dk_e, dv_e = [], []
                for e in range(2):
                    s_all = _dot_nt(qst, kops[g][e])
                    dp_all = _dot_nt(dost, vops[g][e])
                    ps, dsrs = [], []
                    for pr in range(PAIRS):
                        h = _head(g, pr, e)
                        rows = slice(pr * CHUNK, (pr + 1) * CHUNK)
                        p, p_sink = _softmax_scores(s_all[rows], bias_ref[h], mask, sink_ref[0, h])
                        dp = dp_all[rows]
                        delta = jnp.sum(p * dp, axis=-1, keepdims=True)
                        ds = p * (dp - delta)
                        sacc_ref[:, h:h + 1] += -(p_sink * delta)
                        dbias_ref[h] += ds
                        ps.append(p.astype(BF16))
                        dsrs.append((ds * SCALE).astype(BF16))
                    dsr_all = jnp.concatenate(dsrs, axis=0)
                    dq_st = dq_st + _dot(dsr_all, kops[g][e])
                    dk_e.append(_dot_tn(dsr_all, qst))
                    dv_e.append(_dot_tn(jnp.concatenate(ps, axis=0), dost))
                for pr in range(PAIRS):
                    c0 = (g * PAIRS + pr) * 2 * HEAD_DIM
                    dq_ref[:, c0:c0 + 2 * HEAD_DIM] = dq_st[pr * CHUNK:(pr + 1) * CHUNK].astype(BF16)
                halves.append((dk_e, dv_e))
            tiles = []
            for t in range(2):
                g0, g1 = halves[0][t], halves[1][t]
                tiles.append(jnp.where(low, g0[0] + pltpu.roll(g0[1], HEAD_DIM, 1), pltpu.roll(g1[0], HEAD_DIM, 1) + g1[1]))
            dband = jnp.concatenate(tiles, axis=1)
            dkv = (carry_ref[...] + dband[:CHUNK]).astype(BF16)
            dkv_ref[...] = dkv
            dkvt_ref[...] = dkv.T
            dqt_ref[...] = dq_ref[...].T
            carry_ref[...] = dband[CHUNK:]

        @pl.when(n == nb)
        def _():
            dkv = carry_ref[...].astype(BF16)
            dkv_ref[...] = dkv
            dkvt_ref[...] = dkv.T
            dsink_ref[...] = jnp.sum(sacc_ref[...], axis=0, keepdims=True)

    full = lambda *shape: pl.BlockSpec(shape, lambda n: (0,) * len(shape))
    return _call(
        body, name="attn_bwd", grid=(nb + 1,),
        in_specs=[pl.BlockSpec((CHUNK, B_WIDTH), lambda n: (qn(n), 2)),
                  pl.BlockSpec((CHUNK, 2 * KV_WIDTH), lambda n: (qn(n), 12)),
                  pl.BlockSpec((CHUNK, 2 * KV_WIDTH), lambda n: (jnp.maximum(qn(n) - 1, 0), 12)),
                  pl.BlockSpec((CHUNK, B_WIDTH), lambda n: (qn(n), 1)),
                  pl.BlockSpec((CHUNK, B_WIDTH), lambda n: (qn(n), 1)),
                  full(1, B_WIDTH), pl.BlockSpec(memory_space=pltpu.SMEM), full(B_HEADS, CHUNK, 2 * CHUNK)],
        out_specs=[pl.BlockSpec((CHUNK, B_WIDTH), lambda n: (qn(n), 0)),
                   pl.BlockSpec((CHUNK, 2 * KV_WIDTH), lambda n: (jnp.maximum(n - 1, 0), 0)),
                   pl.BlockSpec((B_WIDTH, CHUNK), lambda n: (0, qn(n))),
                   pl.BlockSpec((2 * KV_WIDTH, CHUNK), lambda n: (0, jnp.maximum(n - 1, 0))),
                   full(1, B_WIDTH), full(1, B_HEADS), full(B_HEADS, CHUNK, 2 * CHUNK)],
        out_shape=[jax.ShapeDtypeStruct((T, B_WIDTH), BF16), jax.ShapeDtypeStruct((T, 2 * KV_WIDTH), BF16),
                   jax.ShapeDtypeStruct((B_WIDTH, T), BF16), jax.ShapeDtypeStruct((2 * KV_WIDTH, T), BF16),
                   jax.ShapeDtypeStruct((1, B_WIDTH), F32), jax.ShapeDtypeStruct((1, B_HEADS), F32),
                   jax.ShapeDtypeStruct((B_HEADS, CHUNK, 2 * CHUNK), F32)],
        scratch_shapes=[pltpu.VMEM((CHUNK, 2 * KV_WIDTH), F32), pltpu.VMEM((CHUNK, B_HEADS), F32)],
        sem=("arbitrary",), rides=rides,
    )(proj, proj, proj, ab, dmixed, gb, sinks, bias)


def _sq_relu_grad(acc, r):
    return acc * (2.0 * r.astype(F32))


def _local_step(x, tgt, sp, win, wo, wu, wd):
    T = x.shape[0]
    tm = min(512, T)
    tk = min(512, T)
    lg = sp["gate_norm_g"].reshape(A_GROUPS, CHUNK)
    lb = sp["gate_norm_b"].reshape(A_GROUPS, CHUNK)
    wsp = sp["w_spatial"].reshape(A_GROUPS, CHUNK, CHUNK)
    bs_col = sp["b_spatial"].reshape(A_GROUPS, CHUNK, 1)
    sinks = sp["attn_sinks"].reshape(1, B_HEADS)
    ga = sp["out_norm_a_g"].reshape(1, A_WIDTH)
    gb = sp["out_norm_b_g"].reshape(1, B_WIDTH)
    g1 = sp["mix_norm_g"].reshape(1, D_MODEL)
    g2 = sp["ffn_norm_g"].reshape(1, D_MODEL)
    gf = sp["final_norm_g"].reshape(1, D_MODEL)

    bias = _bias_build(sp["rel_bias_table"])
    n1, proj = _norm_matmul(x, g1, win, tm=tm, tn=PROJ_WIDTH // 2, name="in_proj")
    mixed, ab = _mixer_fwd(proj, lg, lb, wsp, bs_col, sinks, bias, ga, gb)
    h1 = _matmul_res(mixed, wo, x, tm=tm, tn=1024, tk=D_MODEL, prologue=_to_bf16, name="out_proj")
    n2, zp = _norm_matmul(h1, g2, wu, tm=tm, tn=1024, name="up_proj")
    h2 = _matmul_res(zp, wd, h1, tm=tm, tn=1024, tk=2048, prologue=_sq_relu_bf16, name="down_proj")

    dh2, dgf, loss = _loss_bwd(h2, tgt, gf, tm=tm)
    dzp = _matmul_nt(dh2, wd, tm=tm, tn=1024, tk=D_MODEL, name="bwd_dz", extra=zp, epilogue=_sq_relu_grad,
                     out_dtype=BF16)
    dwd = _matmul_tn(zp, dh2, tmo=1024, tn=1024, tk=tk, name="grad_w_down", a_prologue=_sq_relu_bf16)
    dwu = _matmul_tn(n2, dzp, tmo=1024, tn=1024, tk=tk, name="grad_w_up", shards=N_CHIPS)
    dn2 = _matmul_nt(dzp, wu, tm=tm, tn=1024, tk=2048, name="bwd_dn2")
    dh1, dg2 = _rms_bwd_res(dn2, h1, g2, dh2, tm=tm, name="ffn_norm_bwd")
    dwo = _matmul_tn(mixed, dh1, tmo=1024, tn=1024, tk=tk, name="grad_w_out")
    dmixed = _matmul_nt(dh1, wo, tm=tm, tn=1024, tk=D_MODEL, name="bwd_dmixed")
    duv, dga, dwsp, dbs, dlg, dlb = _gmlp_bwd(proj, ab, dmixed, ga, lg, lb, wsp, bs_col)
    dq, dkv, dgb, dsinks, dbias = _attn_bwd(proj, ab, dmixed, gb, sinks, bias)
    dtable = _bias_grad(dbias)
    dproj = jnp.concatenate([duv, dq, dkv], axis=1)
    dwin = _matmul_tn(n1, dproj, tmo=1024, tn=PROJ_WIDTH // 2, tk=tk, name="grad_w_in")
    dn1 = _matmul_nt(dproj, win, tm=tm, tn=1024, tk=PROJ_WIDTH, name="bwd_dn1")
    dx, dg1 = _rms_bwd_res(dn1, x, g1, dh1, tm=tm, name="mix_norm_bwd")

    small = {
        "rel_bias_table": dtable.reshape(N_BUCKETS, B_HEADS), "mix_norm_g": dg1, "gate_norm_g": dlg, "gate_norm_b": dlb,
        "w_spatial": dwsp, "b_spatial": dbs, "attn_sinks": dsinks, "out_norm_a_g": dga, "out_norm_b_g": dgb,
        "ffn_norm_g": dg2, "final_norm_g": dgf,
    }
    return loss, dx, (dwin, dwo, dwu, dwd), small


def _place():
    x, y, c = lax.axis_index("x"), lax.axis_index("y"), lax.axis_index("c")
    chips = [(1 - x, y), (x, 1 - y), (1 - x, 1 - y)]
    return x, y, c, chips


def _remote(src, dst, send_sem, recv_sem, to):
    return pltpu.make_async_remote_copy(src_ref=src, dst_ref=dst, send_sem=send_sem, recv_sem=recv_sem,
                                        device_id=to, device_id_type=MESH)


def _core_index():
    return lax.axis_index("c").astype(jnp.int32).reshape(1)


def _chip_index():
    return (2 * lax.axis_index("x") + lax.axis_index("y")).astype(jnp.int32).reshape(1)


def _cast_into_slot(w, *, tm, name):
    _, R, C = w.shape

    def body(me_ref, w_ref, o_ref):
        del me_ref
        o_ref[...] = w_ref[...].astype(BF16)

    return pl.pallas_call(
        body, name=name,
        grid_spec=pltpu.PrefetchScalarGridSpec(
            num_scalar_prefetch=1, grid=(R // tm,),
            in_specs=[pl.BlockSpec((None, tm, C), lambda i, me: (0, i, 0))],
            out_specs=pl.BlockSpec((None, tm, C), lambda i, me: (me[0], i, 0))),
        out_shape=jax.ShapeDtypeStruct((N_CHIPS, R, C), BF16), compiler_params=_params(("parallel",)),
    )(_chip_index(), w)


def _cast_into_slot_carrying(w, *, tm, name, rides):
    _, R, C = w.shape

    def body(w_ref, o_ref):
        o_ref[...] = w_ref[...].astype(BF16)

    return _call(
        body, name=name, grid=(R // tm,),
        in_specs=[pl.BlockSpec((None, tm, C), lambda i: (0, i, 0))],
        out_specs=pl.BlockSpec((None, tm, C), lambda i: (2 * lax.axis_index("x") + lax.axis_index("y"), i, 0)),
        out_shape=jax.ShapeDtypeStruct((N_CHIPS, R, C), BF16), sem=("arbitrary",), rides=rides,
    )(w)


def _gather_weights(slots):
    nw = len(slots)

    def body(*refs):
        fulls = refs[nw:2 * nw]
        send_sems, recv_sems = refs[2 * nw:]
        x, y, c, chips = _place()
        me = 2 * x + y
        sends = []
        for w in range(nw):
            hr = fulls[w].shape[1] // 2
            rows = pl.ds(c * hr, hr)
            for j, chip in enumerate(chips):
                mine = fulls[w].at[me, rows, :]
                cp = _remote(mine, mine, send_sems.at[6 * w + j], recv_sems.at[6 * w + j], (*chip, c))
                cp.start()
                sends.append(cp)
        for w in range(nw):
            hr = fulls[w].shape[1] // 2
            rows = pl.ds(c * hr, hr)
            for j, chip in enumerate(chips):
                landed = fulls[w].at[2 * chip[0] + chip[1], rows, :]
                _remote(landed, landed, send_sems.at[6 * w + j], recv_sems.at[6 * w + j], (x, y, c)).wait_recv()
                cp = _remote(landed, landed, send_sems.at[6 * w + 3 + j], recv_sems.at[6 * w + 3 + j], (x, y, 1 - c))
                cp.start()
                sends.append(cp)
        for w in range(nw):
            hr = fulls[w].shape[1] // 2
            rows = pl.ds((1 - c) * hr, hr)
            for j, chip in enumerate(chips):
                other = fulls[w].at[2 * chip[0] + chip[1], rows, :]
                _remote(other, other, send_sems.at[6 * w + 3 + j], recv_sems.at[6 * w + 3 + j], (x, y, c)).wait_recv()
        for cp in sends:
            cp.wait_send()

    any_spec = pl.BlockSpec(memory_space=pl.ANY)
    return pl.pallas_call(
        body, name="gather_weights",
        in_specs=[any_spec] * nw, out_specs=[any_spec] * nw,
        out_shape=[jax.ShapeDtypeStruct(s.shape, s.dtype) for s in slots],
        scratch_shapes=[pltpu.SemaphoreType.DMA((6 * nw,)), pltpu.SemaphoreType.DMA((6 * nw,))],
        input_output_aliases={w: w for w in range(nw)},
    )(*slots)


def _sibling_halves(grads):
    nw = len(grads)

    def body(*refs):
        gs, outs = refs[:nw], refs[nw:2 * nw]
        send_sems, recv_sems = refs[2 * nw:]
        x, y, c, _ = _place()
        cps = []
        for w in range(nw):
            hr = gs[w].shape[1] // 2
            cp = _remote(gs[w].at[:, pl.ds((1 - c) * hr, hr), :], outs[w], send_sems.at[w], recv_sems.at[w],
                         (x, y, 1 - c))
            cp.start()
            cps.append(cp)
        for cp in cps:
            cp.wait()

    any_spec = pl.BlockSpec(memory_space=pl.ANY)
    return pl.pallas_call(
        body, name="rs_sibling_halves",
        in_specs=[any_spec] * nw, out_specs=[any_spec] * nw,
        out_shape=[jax.ShapeDtypeStruct((g.shape[0], g.shape[1] // 2, g.shape[2]), g.dtype) for g in grads],
        scratch_shapes=[pltpu.SemaphoreType.DMA((nw,)), pltpu.SemaphoreType.DMA((nw,))],
    )(*grads)


def _pair_sum_bf16(g, got, *, tm, name):
    S, R, C = g.shape
    hr = R // 2
    nt = hr // tm

    def body(c_ref, g_ref, got_ref, o_ref):
        del c_ref
        o_ref[...] = (g_ref[...] + got_ref[...]).astype(BF16)

    return pl.pallas_call(
        body, name=name,
        grid_spec=pltpu.PrefetchScalarGridSpec(
            num_scalar_prefetch=1, grid=(S, nt),
            in_specs=[pl.BlockSpec((None, tm, C), lambda s, i, c: (s, c[0] * nt + i, 0)),
                      pl.BlockSpec((None, tm, C), lambda s, i, c: (s, i, 0))],
            out_specs=pl.BlockSpec((None, tm, C), lambda s, i, c: (s, i, 0))),
        out_shape=jax.ShapeDtypeStruct((S, hr, C), BF16),
        compiler_params=_params(("parallel", "parallel")),
    )(_core_index(), g, got)


def _scatter_to_owners(pairs):
    nw = len(pairs)

    def body(*refs):
        qs, outs = refs[:nw], refs[nw:2 * nw]
        send_sems, recv_sems = refs[2 * nw:]
        x, y, c, chips = _place()
        cps = []
        for w in range(nw):
            for j, chip in enumerate(chips):
                cp = _remote(qs[w].at[2 * chip[0] + chip[1]], outs[w].at[j], send_sems.at[3 * w + j],
                             recv_sems.at[3 * w + j], (*chip, c))
                cp.start()
                cps.append(cp)
        for cp in cps:
            cp.wait()

    any_spec = pl.BlockSpec(memory_space=pl.ANY)
    return pl.pallas_call(
        body, name="rs_scatter_to_owners",
        in_specs=[any_spec] * nw, out_specs=[any_spec] * nw,
        out_shape=[jax.ShapeDtypeStruct((3,) + q.shape[1:], q.dtype) for q in pairs],
        scratch_shapes=[pltpu.SemaphoreType.DMA((3 * nw,)), pltpu.SemaphoreType.DMA((3 * nw,))],
    )(*pairs)


def _owner_total(gh, others, *, tm, name):
    _, hr, C = gh.shape

    def body(me_ref, g_ref, o_ref_in, out_ref):
        del me_ref
        acc = g_ref[...]
        for j in range(3):
            acc = acc + o_ref_in[j].astype(F32)
        out_ref[...] = acc

    return pl.pallas_call(
        body, name=name,
        grid_spec=pltpu.PrefetchScalarGridSpec(
            num_scalar_prefetch=1, grid=(hr // tm,),
            in_specs=[pl.BlockSpec((None, tm, C), lambda i, me: (me[0], i, 0)),
                      pl.BlockSpec((3, tm, C), lambda i, me: (0, i, 0))],
            out_specs=pl.BlockSpec((tm, C), lambda i, me: (i, 0))),
        out_shape=jax.ShapeDtypeStruct((hr, C), F32),
        compiler_params=_params(("parallel",)),
    )(_chip_index(), gh, others)


def _owner_sum(g, got, others, *, tm, name):
    S, R, C = g.shape
    hr = R // 2
    nt = hr // tm

    def body(idx_ref, g_ref, got_ref, o_ref_in, out_ref):
        del idx_ref
        acc = g_ref[...] + got_ref[...]
        for j in range(3):
            acc = acc + o_ref_in[j].astype(F32)
        out_ref[...] = acc

    return pl.pallas_call(
        body, name=name,
        grid_spec=pltpu.PrefetchScalarGridSpec(
            num_scalar_prefetch=1, grid=(nt,),
            in_specs=[pl.BlockSpec((None, tm, C), lambda i, p: (p[1], p[0] * nt + i, 0)),
                      pl.BlockSpec((None, tm, C), lambda i, p: (p[1], i, 0)),
                      pl.BlockSpec((3, tm, C), lambda i, p: (0, i, 0))],
            out_specs=pl.BlockSpec((tm, C), lambda i, p: (i, 0))),
        out_shape=jax.ShapeDtypeStruct((hr, C), F32),
        compiler_params=_params(("parallel",)),
    )(jnp.concatenate([_core_index(), _chip_index()]), g, got, others)


def _swap_halves(halves):
    nw = len(halves)

    def body(*refs):
        hs, outs = refs[:nw], refs[nw:2 * nw]
        send_sems, recv_sems = refs[2 * nw:]
        x, y, c, _ = _place()
        cps = []
        for w in range(nw):
            cp = _remote(hs[w], outs[w], send_sems.at[w], recv_sems.at[w], (x, y, 1 - c))
            cp.start()
            cps.append(cp)
        for cp in cps:
            cp.wait()

    any_spec = pl.BlockSpec(memory_space=pl.ANY)
    return pl.pallas_call(
        body, name="rs_swap_halves",
        in_specs=[any_spec] * nw, out_specs=[any_spec] * nw,
        out_shape=[jax.ShapeDtypeStruct(h.shape, h.dtype) for h in halves],
        scratch_shapes=[pltpu.SemaphoreType.DMA((nw,)), pltpu.SemaphoreType.DMA((nw,))],
    )(*halves)


def _all_reduce_small(packed):
    R, C = packed.shape

    def body(in_ref, out_ref, slots, send_sems, recv_sems):
        x, y, c, _ = _place()
        me = 4 * x + 2 * y + c
        cps = []
        for k in range(1, N_DEV):
            p = (me + k) % N_DEV
            cp = _remote(in_ref, slots.at[me], send_sems.at[k - 1], recv_sems.at[k - 1], (p // 4, (p // 2) % 2, p % 2))
            cp.start()
            cps.append(cp)
        slots[me] = in_ref[...]
        for k in range(1, N_DEV):
            src = (me + N_DEV - k) % N_DEV
            _remote(in_ref, slots.at[src], send_sems.at[k - 1], recv_sems.at[k - 1], (x, y, c)).wait_recv()
        for cp in cps:
            cp.wait_send()
        acc = slots[0]
        for d in range(1, N_DEV):
            acc = acc + slots[d]
        out_ref[...] = acc

    vmem = pl.BlockSpec(memory_space=pltpu.VMEM)
    return pl.pallas_call(
        body, name="all_reduce_small", in_specs=[vmem], out_specs=vmem,
        out_shape=jax.ShapeDtypeStruct((R, C), F32),
        scratch_shapes=[pltpu.VMEM((N_DEV, R, C), F32), pltpu.SemaphoreType.DMA((N_DEV - 1,)),
                        pltpu.SemaphoreType.DMA((N_DEV - 1,))],
        compiler_params=_params(),
    )(packed)


def _adamw_math(w, g, m, v):
    m = ADAM_B1 * m + (1.0 - ADAM_B1) * g
    v = ADAM_B2 * v + (1.0 - ADAM_B2) * (g * g)
    m_hat = m / (1.0 - ADAM_B1 ** ADAM_STEP)
    v_hat = v / (1.0 - ADAM_B2 ** ADAM_STEP)
    delta = -ADAM_LR * (m_hat / (jnp.sqrt(v_hat) + ADAM_EPS) + ADAM_WD * w)
    return delta, m, v


def _adamw(w, g, m, v, *, tm, name):
    R, C = w.shape

    def body(w_ref, g_ref, m_ref, v_ref, d_ref, nm_ref, nv_ref):
        d_ref[...], nm_ref[...], nv_ref[...] = _adamw_math(w_ref[...], g_ref[...], m_ref[...], v_ref[...])

    spec = pl.BlockSpec((tm, C), lambda i: (i, 0))
    return pl.pallas_call(
        body, name=name, grid=(R // tm,), in_specs=[spec] * 4, out_specs=[spec] * 3,
        out_shape=[jax.ShapeDtypeStruct((R, C), F32)] * 3, compiler_params=_params(("parallel",)),
    )(w, g, m, v)


def _adamw_halves(w, own, got, m, v, *, tm, name, rides=()):
    _, R, C = w.shape
    nt = (R // 2) // tm

    def body(w_ref, own_ref, got_ref, m_ref, v_ref, g_ref, d_ref, nm_ref, nv_ref):
        g = jnp.where(pl.program_id(0) == lax.axis_index("c"), own_ref[...], got_ref[...])
        g_ref[...] = g
        d_ref[...], nm_ref[...], nv_ref[...] = _adamw_math(w_ref[...], g, m_ref[...], v_ref[...])

    whole = pl.BlockSpec((None, tm, C), lambda h, i: (0, h * nt + i, 0))
    half = pl.BlockSpec((tm, C), lambda h, i: (i, 0))
    return _call(
        body, name=name, grid=(2, nt), in_specs=[whole, half, half, whole, whole], out_specs=[whole] * 4,
        out_shape=[jax.ShapeDtypeStruct((1, R, C), F32)] * 4, sem=("parallel", "parallel"), rides=rides,
    )(w, own, got, m, v)


def _adamw_small(w, slots, m, v, *, name):
    def body(w_ref, slots_ref, m_ref, v_ref, g_ref, d_ref, nm_ref, nv_ref):
        g = slots_ref[0]
        for d in range(1, N_DEV):
            g = g + slots_ref[d]
        g_ref[...] = g
        d_ref[...], nm_ref[...], nv_ref[...] = _adamw_math(w_ref[...], g, m_ref[...], v_ref[...])

    vmem = pl.BlockSpec(memory_space=pltpu.VMEM)
    return pl.pallas_call(
        body, name=name, in_specs=[vmem] * 4, out_specs=[vmem] * 4,
        out_shape=[jax.ShapeDtypeStruct(w.shape, F32)] * 4, compiler_params=_params(),
    )(w, slots, m, v)


SMALL = ["rel_bias_table", "mix_norm_g", "gate_norm_g", "gate_norm_b", "w_spatial", "b_spatial", "attn_sinks",
         "out_norm_a_g", "out_norm_b_g", "ffn_norm_g", "final_norm_g"]
SMALL_A = ["gate_norm_g", "gate_norm_b", "w_spatial", "b_spatial", "out_norm_a_g"]
SMALL_B = ["rel_bias_table", "mix_norm_g", "attn_sinks", "out_norm_b_g", "ffn_norm_g", "final_norm_g"]
LARGE = ["w_in", "w_out", "w_up", "w_down"]
ROW_TILE = {"w_in": 208, "w_out": 256, "w_up": 256, "w_down": 256}
WEIGHTS = ["rel_bias_table", "mix_norm_g", "w_in", "gate_norm_g", "gate_norm_b", "w_spatial", "b_spatial", "attn_sinks",
           "out_norm_a_g", "out_norm_b_g", "w_out", "ffn_norm_g", "w_up", "w_down", "final_norm_g"]
PACK_UNIT = 8 * 128


def _pack(parts):
    rows = []
    for p in parts:
        flat = p.reshape(-1)
        pad = (-flat.shape[0]) % PACK_UNIT
        rows.append(jnp.pad(flat, (0, pad)).reshape(-1, 128))
    return jnp.concatenate(rows, axis=0)


def _unpack(packed, like):
    out, row = [], 0
    for p in like:
        n = math.prod(p.shape)
        nrows = (n + PACK_UNIT - 1) // PACK_UNIT * 8
        out.append(packed[row:row + nrows].reshape(-1)[:n].reshape(p.shape))
        row += nrows
    return out


def kernel(x, rel_bias_table, mix_norm_g, w_in, gate_norm_g, gate_norm_b, w_spatial, b_spatial, attn_sinks, out_norm_a_g, out_norm_b_g, w_out, ffn_norm_g, w_up, w_down, final_norm_g, loss_target, m_rel_bias_table, m_mix_norm_g, m_w_in, m_gate_norm_g, m_gate_norm_b, m_w_spatial, m_b_spatial, m_attn_sinks, m_out_norm_a_g, m_out_norm_b_g, m_w_out, m_ffn_norm_g, m_w_up, m_w_down, m_final_norm_g, v_rel_bias_table, v_mix_norm_g, v_w_in, v_gate_norm_g, v_gate_norm_b, v_w_spatial, v_b_spatial, v_attn_sinks, v_out_norm_a_g, v_out_norm_b_g, v_w_out, v_ffn_norm_g, v_w_up, v_w_down, v_final_norm_g):
    args = dict(locals())
    wts = {n: args[n] for n in WEIGHTS}
    mom = {n: args["m_" + n] for n in WEIGHTS}
    var = {n: args["v_" + n] for n in WEIGHTS}
    sp = {n: wts[n] for n in SMALL}
    x2, tgt = x[0], loss_target[0]
    T = x2.shape[0]
    tm = min(512, T)
    tl = min(1024, T)
    tg = min(2048, T)
    lg = sp["gate_norm_g"].reshape(A_GROUPS, CHUNK)
    lb = sp["gate_norm_b"].reshape(A_GROUPS, CHUNK)
    wsp = sp["w_spatial"].reshape(A_GROUPS, CHUNK, CHUNK)
    bs_col = sp["b_spatial"].reshape(A_GROUPS, CHUNK, 1)
    sinks = sp["attn_sinks"].reshape(1, B_HEADS)
    ga = sp["out_norm_a_g"].reshape(1, A_WIDTH)
    gb = sp["out_norm_b_g"].reshape(1, B_WIDTH)
    g1 = sp["mix_norm_g"].reshape(1, D_MODEL)
    g2 = sp["ffn_norm_g"].reshape(1, D_MODEL)
    gf = sp["final_norm_g"].reshape(1, D_MODEL)

    def owner_total(n, gh, others):
        return _owner_total(gh, others, tm=ROW_TILE[n], name="rs_owner_total_" + n)

    def halves_view(at, shards):
        return at.reshape(shards, 2, at.shape[0] // shards // 2, at.shape[1])

    for d in (wts, mom, var):
        d["w_in"] = jnp.swapaxes(d["w_in"], 1, 2)

    s_in = _cast_into_slot(wts["w_in"], tm=ROW_TILE["w_in"], name="cast_w_in")
    s_out = _cast_into_slot(wts["w_out"], tm=256, name="cast_w_out")
    s_up, ((g_in,),) = _cast_into_slot_carrying(wts["w_up"], tm=256, name="cast_w_up",
                                                rides=[_ride_gather(s_in, direct=(0, 1, 1))])
    s_down = _cast_into_slot(wts["w_down"], tm=256, name="cast_w_down")
    win_t = g_in.reshape(PROJ_WIDTH, D_MODEL)
    bias = _bias_build(sp["rel_bias_table"])
    (n1, proj), ((g_out,), (s_up,)) = _norm_matmul_wide(
        x2, g1, win_t, tm=tm, tn=PROJ_WIDTH // 2, name="in_proj",
        rides=[_ride_gather(s_out, direct=(0, 1, 1)), _ride_gather(s_up, s1=(0, 4, 8))])
    wo = g_out.reshape(A_WIDTH + B_WIDTH, D_MODEL)
    (mixed, mixed_t, ab), ((s_up,), (s_down,), (n1_sib,)) = _mixer_fwd(
        proj, lg, lb, wsp, bs_col, sinks, bias, ga, gb,
        rides=[_ride_gather(s_up, s2=(0, 4, 8), s1=(4, 8, 8)), _ride_gather(s_down, s1=(0, 4, 8)),
               _ride_to_sibling(n1, first=True)])
    mixed_t = halves_view(mixed_t, N_CHIPS)
    h1, ((wu,), (s_down,), (mixed_t_sib,)) = _matmul_res(
        mixed, wo, x2, tm=tl, tn=1024, tk=D_MODEL, prologue=_to_bf16, name="out_proj",
        rides=[_ride_gather(s_up, s3=(0, 4, 8), tail=(4, 8, 8), mid_frac=0.75), _ride_gather(s_down, s2=(0, 4, 8)),
               _ride_to_sibling(mixed_t, halves=True)])
    (n2t, zp, z2, z2t), ((g_down,),) = _norm_matmul_sq(
        h1, g2, wu, tm=tl, tn=1024, name="up_proj", rides=[_ride_gather(s_down, s3=(0, 4, 8), direct=(4, 8, 8))])
    wd = g_down.reshape(D_FF, D_MODEL)
    n2t, z2t = halves_view(n2t, 1), halves_view(z2t, N_CHIPS)
    h2, ((n2t_sib,), (z2t_sib,)) = _matmul_res(
        z2, wd, h1, tm=tl, tn=1024, tk=4096, prologue=_to_bf16, name="down_proj",
        rides=[_ride_to_sibling(n2t, halves=True), _ride_to_sibling(z2t, halves=True)])

    dh2, dh2b, dgf, loss = _loss_bwd(h2, tgt, gf, tm=tm)
    dzp, ((dh2b_sib,),) = _matmul_nt(dh2b, wd, tm=tl, tn=1024, tk=D_MODEL, name="bwd_dz", extra=zp,
                                     epilogue=_sq_relu_grad, out_dtype=BF16, rides=[_ride_to_sibling(dh2b)])
    (gd, gdb), ((dzp_sib,),) = _grad_pair(z2t, z2t_sib, dh2b, dh2b_sib, cols_sharded=False, tmo=1024, tk=tl,
                                          name="grad_w_down", rides=[_ride_to_sibling(dzp)])
    (gu, gub), ((o_d,),) = _grad_pair(n2t, n2t_sib, dzp, dzp_sib, cols_sharded=True, tmo=1024, tk=tl,
                                      name="grad_w_up", rides=[_ride_scatter(gdb, None, (0, 7, 8))])
    dn2, ((o_d,), (o_u,)) = _matmul_nt(dzp, wu, tm=tl, tn=1024, tk=4096, name="bwd_dn2",
                                       rides=[_ride_scatter(gdb, o_d, (7, 8, 8)), _ride_scatter(gub, None, (0, 6, 8))])
    h_d = owner_total("w_down", gd, o_d)
    (dh1, dh1b, dg2), ((o_u,),) = _rms_bwd_res(dn2, h1, g2, dh2, tm=tm, name="ffn_norm_bwd",
                                               rides=[_ride_scatter(gub, o_u, (6, 7, 8))])
    dmixed, ((o_u,), (dh1b_sib,), (w_d,)) = _matmul_nt(
        dh1b, wo, tm=tl, tn=1024, tk=D_MODEL, name="bwd_dmixed",
        rides=[_ride_scatter(gub, o_u, (7, 8, 8)), _ride_to_sibling(dh1b), _ride_swap(h_d)])
    h_u = owner_total("w_up", gu, o_u)
    (go, gob), ((w_u,),) = _grad_pair_merged(mixed_t, mixed_t_sib, dh1b, dh1b_sib, tk=tl, name="grad_w_out",
                                             rides=[_ride_swap(h_u)])
    (duv, duv_t, dga, dwsp, dbs, dlg, dlb), ((o_o,),) = _gmlp_bwd(proj, ab, dmixed, ga, lg, lb, wsp, bs_col,
                                                                  rides=[_ride_scatter(gob)])
    h_o = owner_total("w_out", go, o_o)
    small = {"gate_norm_g": dlg, "gate_norm_b": dlb, "w_spatial": dwsp, "b_spatial": dbs, "out_norm_a_g": dga}
    hr_in = PROJ_WIDTH // N_CHIPS // 2
    (dq, dkv, dq_t, dkv_t, dgb, dsinks, dbias), ((w_o,), (dproj_t_sib,)) = _attn_bwd(
        proj, ab, dmixed, gb, sinks, bias, rides=[_ride_swap(h_o), _ride_rows_to_sibling(duv_t, hr_in, 2, N_CHIPS)])
    dtable = _bias_grad(dbias)
    dproj_t = halves_view(jnp.concatenate([duv_t, dq_t, dkv_t], axis=0), N_CHIPS)
    ((dproj_t_sib,),) = _carrier([_ride_to_sibling(dproj_t, halves=True, shards=(2, N_CHIPS), land=dproj_t_sib)],
                                 name="trade_dproj_t")
    (gi, gib), ((slots_a,),) = _grad_pair(
        dproj_t, dproj_t_sib, n1, n1_sib, cols_sharded=False, tmo=hr_in, tk=tl, name="grad_w_in",
        rides=[_ride_small_to_all(_pack([small[n] for n in SMALL_A]))])
    dn1, ((o_i,),) = _matmul_parts([duv, dq, dkv], win_t, tm=tl, tn=1024, name="bwd_dn1", rides=[_ride_scatter(gib)])
    h_i = owner_total("w_in", gi, o_i)
    dx, _, dg1 = _rms_bwd_res(dn1, x2, g1, dh1, tm=tm, name="mix_norm_bwd")
    small.update({"rel_bias_table": dtable.reshape(N_BUCKETS, B_HEADS), "mix_norm_g": dg1, "attn_sinks": dsinks,
                  "out_norm_b_g": dgb, "ffn_norm_g": dg2, "final_norm_g": dgf})
    (w_i,), (slots_b,) = _carrier([_ride_swap(h_i), _ride_small_to_all(_pack([small[n] for n in SMALL_B] + [loss]))],
                                  name="swap_w_in")

    out_g, out_d, out_m, out_v = {}, {}, {}, {}
    for n, h, s in zip(LARGE, [h_i, h_o, h_u, h_d], [w_i, w_o, w_u, w_d]):
        res = _adamw_halves(wts[n], h, s, mom[n], var[n], tm=ROW_TILE[n], name="adamw_" + n)
        if n == "w_in":
            res = [jnp.swapaxes(r, 1, 2) for r in res]
        out_g[n], out_d[n], out_m[n], out_v[n] = res
    for names, slots, tag in ((SMALL_A, slots_a, "a"), (SMALL_B, slots_b, "b")):
        extra = [jnp.zeros((1, 1), F32)] if tag == "b" else []
        like = [wts[n] for n in names] + extra
        res = _adamw_small(_pack(like), slots, _pack([mom[n] for n in names] + extra),
                           _pack([var[n] for n in names] + extra), name="adamw_small_" + tag)
        for store, packed in zip((out_g, out_d, out_m, out_v), res):
            for n, val in zip(names + ["loss"], _unpack(packed, like)):
                store[n] = val

    total = out_g["loss"][0, 0]
    return (total, dx[None], *[out_g[n] for n in WEIGHTS], *[out_d[n] for n in WEIGHTS],
            *[out_m[n] for n in WEIGHTS], *[out_v[n] for n in WEIGHTS])
```

```python
import math

import numpy as np
import jax
import jax.numpy as jnp
from jax import lax
from jax.experimental import pallas as pl
from jax.experimental.pallas import tpu as pltpu

F32 = jnp.float32
BF16 = jnp.bfloat16

D_MODEL = 2048
CHUNK = 128
A_GROUPS = 8
A_WIDTH = 1024
HEAD_DIM = 64
B_HEADS = 16
Q_PER_KV = 8
B_WIDTH = 1024
KV_WIDTH = 128
PROJ_WIDTH = 3328
D_FF = 8192
N_BUCKETS = 32
EPS = 1e-5
NEG = -1e30
SCALE = HEAD_DIM ** -0.5
N_CHIPS = 4
N_DEV = 8

ADAM_LR = 0.001
ADAM_B1 = 0.9
ADAM_B2 = 0.999
ADAM_EPS = 1e-08
ADAM_WD = 0.01
ADAM_STEP = 10

VMEM_LIMIT = 60 * 1024 * 1024
MESH = pl.DeviceIdType.MESH


def _bucket_thresholds():
    d = np.arange(CHUNK)
    n_exact = N_BUCKETS // 2
    relf = np.maximum(d, n_exact).astype(np.float64)
    large = n_exact + (np.log(relf / n_exact) / math.log(CHUNK / n_exact) * (N_BUCKETS - n_exact)).astype(np.int32)
    bucket = np.where(d < n_exact, d, np.minimum(large, N_BUCKETS - 1))
    return [int(np.min(d[bucket >= b])) for b in range(1, N_BUCKETS)]


BUCKET_THR = _bucket_thresholds()


def _params(sem=None):
    return pltpu.CompilerParams(dimension_semantics=sem, vmem_limit_bytes=VMEM_LIMIT)


def _gelu(x):
    c = math.sqrt(2.0 / math.pi)
    return 0.5 * x * (1.0 + jnp.tanh(c * (x + 0.044715 * (x * x * x))))


def _gelu_and_grad(x):
    c = math.sqrt(2.0 / math.pi)
    x2 = x * x
    t = jnp.tanh(c * (x + 0.044715 * (x2 * x)))
    g = 0.5 * x * (1.0 + t)
    dg = 0.5 * (1.0 + t) + 0.5 * x * (1.0 - t * t) * (c * (1.0 + 3.0 * 0.044715 * x2))
    return g, dg


def _dot(a, b):
    return jnp.dot(a, b, preferred_element_type=F32)


def _dot_nt(a, b):
    return lax.dot_general(a, b, (((1,), (1,)), ((), ())), preferred_element_type=F32)


def _dot_tn(a, b):
    return lax.dot_general(a, b, (((0,), (0,)), ((), ())), preferred_element_type=F32)


def _rms_bwd(dn, h, g):
    r = lax.rsqrt(jnp.mean(h * h, axis=-1, keepdims=True) + EPS)
    w = dn * g
    dh = r * w - h * ((r * r * r) * jnp.mean(w * h, axis=-1, keepdims=True))
    return dh, r


def _place():
    x, y, c = lax.axis_index("x"), lax.axis_index("y"), lax.axis_index("c")
    chips = [(1 - x, y), (x, 1 - y), (1 - x, 1 - y)]
    return x, y, c, chips


def _remote(src, dst, send_sem, recv_sem, to):
    return pltpu.make_async_remote_copy(src_ref=src, dst_ref=dst, send_sem=send_sem, recv_sem=recv_sem,
                                        device_id=to, device_id_type=MESH)


class _Ride:
    def __init__(self, args, out_shape, n_sem, start, finish, mid=None, mid_frac=0.8, aliases=None):
        self.args, self.out_shape, self.n_sem = list(args), list(out_shape), n_sem
        self.start, self.mid, self.finish, self.mid_frac = start, mid, finish, mid_frac
        self.aliases = dict(aliases or {})


def _call(body, *, name, grid, in_specs, out_specs, out_shape, scratch_shapes=(), sem=None, rides=()):
    single = not isinstance(out_shape, (list, tuple))
    out_specs = [out_specs] if single else list(out_specs)
    out_shape = [out_shape] if single else list(out_shape)
    n_in, n_out, n_scr = len(in_specs), len(out_shape), len(scratch_shapes)
    r_in = [len(r.args) for r in rides]
    r_out = [len(r.out_shape) for r in rides]
    any_spec = pl.BlockSpec(memory_space=pl.ANY)
    aliases, off_i, off_o = {}, n_in, n_out
    for r in rides:
        for i, o in r.aliases.items():
            aliases[off_i + i] = off_o + o
        off_i += len(r.args)
        off_o += len(r.out_shape)
    steps = math.prod(grid)

    def wrapped(*refs):
        p = 0
        ins = refs[p:p + n_in]; p += n_in
        rins = refs[p:p + sum(r_in)]; p += sum(r_in)
        outs = refs[p:p + n_out]; p += n_out
        routs = refs[p:p + sum(r_out)]; p += sum(r_out)
        scr = refs[p:p + n_scr]; p += n_scr
        sems = refs[p:]
        parts, pi, po = [], 0, 0
        for k, r in enumerate(rides):
            parts.append((rins[pi:pi + r_in[k]], routs[po:po + r_out[k]], sems[2 * k], sems[2 * k + 1]))
            pi += r_in[k]
            po += r_out[k]
        lin = 0
        for d in range(len(grid)):
            lin = lin * grid[d] + pl.program_id(d)
        if rides:
            @pl.when(lin == 0)
            def _():
                for r, part in zip(rides, parts):
                    r.start(*part)
        body(*ins, *outs, *scr)
        for r, part in zip(rides, parts):
            if r.mid is not None:
                @pl.when(lin == min(steps - 1, int(r.mid_frac * steps)))
                def _(r=r, part=part):
                    r.mid(*part)
        if rides:
            @pl.when(lin == steps - 1)
            def _():
                for r, part in zip(rides, parts):
                    r.finish(*part)

    scratch = list(scratch_shapes)
    for r in rides:
        scratch += [pltpu.SemaphoreType.DMA((r.n_sem,)), pltpu.SemaphoreType.DMA((r.n_sem,))]
    if rides:
        sem = ("arbitrary",) * len(grid)
    res = pl.pallas_call(
        wrapped, name=name, grid=grid,
        in_specs=list(in_specs) + [any_spec] * sum(r_in),
        out_specs=out_specs + [any_spec] * sum(r_out),
        out_shape=out_shape + [s for r in rides for s in r.out_shape],
        scratch_shapes=scratch, input_output_aliases=aliases,
        compiler_params=_params(sem),
    )

    def run(*args):
        got = res(*args, *[a for r in rides for a in r.args])
        mine = got[0] if single else list(got[:n_out])
        if not rides:
            return mine
        rest, out = list(got[n_out:]), []
        for k in range(len(rides)):
            out.append(rest[:r_out[k]])
            rest = rest[r_out[k]:]
        return mine, out

    return run


def _ride_gather(slot, s1=None, s2=None, s3=None, tail=None, direct=None, mid_frac=0.6):
    half = slot.shape[1] // 2

    def rows(part, c, which=None):
        k0, k1, n = part
        count, first = (k1 - k0) * (half // n), c * half + k0 * (half // n)
        return pl.ds(first, count) if which is None else pl.ds(first + which * (count // 2), count // 2)

    def ids():
        x, y, c, _ = _place()
        return x, y, c, 2 * x + y, 2 * (1 - x) + y, 2 * x + (1 - y), 2 * (1 - x) + (1 - y)

    def copy(full, chip, r, ss, rs, k, to):
        piece = full.at[chip, r, :]
        return _remote(piece, piece, ss.at[k], rs.at[k], to)

    def to_neighbours(full, ss, rs, part, base):
        x, y, c, me, _, _, _ = ids()
        return [copy(full, me, rows(part, c), ss, rs, base, (1 - x, y, c)),
                copy(full, me, rows(part, c), ss, rs, base + 1, (x, 1 - y, c))]

    def from_neighbours(full, ss, rs, part, base):
        x, y, c, _, cx, cy, _ = ids()
        return [copy(full, cx, rows(part, c), ss, rs, base, (x, y, c)), copy(full, cy, rows(part, c), ss, rs, base + 1, (x, y, c))]

    def onward(full, ss, rs, part, base):
        x, y, c, _, cx, cy, _ = ids()
        return [copy(full, cx, rows(part, c, 0), ss, rs, base, (x, 1 - y, c)),
                copy(full, cy, rows(part, c, 1), ss, rs, base + 1, (1 - x, y, c))]

    def from_onward(full, ss, rs, part, base):
        x, y, c, _, _, _, cd = ids()
        return [copy(full, cd, rows(part, c, 0), ss, rs, base, (x, y, c)), copy(full, cd, rows(part, c, 1), ss, rs, base + 1, (x, y, c))]

    def to_sibling(full, ss, rs, part, base, diagonal):
        x, y, c, _, cx, cy, cd = ids()
        return [copy(full, chip, rows(part, c), ss, rs, base + j, (x, y, 1 - c))
                for j, chip in enumerate([cd] if diagonal else [cx, cy])]

    def from_sibling(full, ss, rs, part, base, diagonal):
        x, y, c, _, cx, cy, cd = ids()
        return [copy(full, chip, rows(part, 1 - c), ss, rs, base + j, (x, y, c))
                for j, chip in enumerate([cd] if diagonal else [cx, cy])]

    def to_chips(full, ss, rs, part, base):
        x, y, c, me, _, _, _ = ids()
        chips = [(1 - x, y), (x, 1 - y), (1 - x, 1 - y)]
        return [copy(full, me, rows(part, c), ss, rs, base + j, (*chip, c)) for j, chip in enumerate(chips)]

    def all_three(full, ss, rs, part, base, half_of, to):
        x, y, c, _, cx, cy, cd = ids()
        return [copy(full, chip, rows(part, half_of(c)), ss, rs, base + j, to(x, y, c)) for j, chip in enumerate([cx, cy, cd])]

    def start(ins, outs, ss, rs):
        full, cps = outs[0], []
        if s1 is not None:
            cps += to_neighbours(full, ss, rs, s1, 0)
        for part, b_ici, b_sib in ((s2, 2, 4), (tail, 7, 9)):
            if part is not None:
                cps += onward(full, ss, rs, part, b_ici) + to_sibling(full, ss, rs, part, b_sib, False)
        if s3 is not None:
            cps += to_sibling(full, ss, rs, s3, 6, True)
        if direct is not None:
            cps += to_chips(full, ss, rs, direct, 12)
        for cp in cps:
            cp.start()

    def mid(ins, outs, ss, rs):
        if tail is not None:
            for cp in from_onward(outs[0], ss, rs, tail, 7):
                cp.wait_recv()
            for cp in to_sibling(outs[0], ss, rs, tail, 11, True):
                cp.start()
        if direct is not None:
            for cp in all_three(outs[0], ss, rs, direct, 12, lambda c: c, lambda x, y, c: (x, y, c)):
                cp.wait_recv()
            for cp in all_three(outs[0], ss, rs, direct, 15, lambda c: c, lambda x, y, c: (x, y, 1 - c)):
                cp.start()

    def finish(ins, outs, ss, rs):
        full, got, sent = outs[0], [], []
        if s1 is not None:
            got += from_neighbours(full, ss, rs, s1, 0)
            sent += to_neighbours(full, ss, rs, s1, 0)
        if s2 is not None:
            got += from_onward(full, ss, rs, s2, 2) + from_sibling(full, ss, rs, s2, 4, False)
            sent += onward(full, ss, rs, s2, 2) + to_sibling(full, ss, rs, s2, 4, False)
        if s3 is not None:
            got += from_sibling(full, ss, rs, s3, 6, True)
            sent += to_sibling(full, ss, rs, s3, 6, True)
        if tail is not None:
            got += from_sibling(full, ss, rs, tail, 9, False) + from_sibling(full, ss, rs, tail, 11, True)
            sent += onward(full, ss, rs, tail, 7) + to_sibling(full, ss, rs, tail, 9, False) + to_sibling(full, ss, rs, tail, 11, True)
        if direct is not None:
            got += all_three(full, ss, rs, direct, 15, lambda c: 1 - c, lambda x, y, c: (x, y, c))
            sent += to_chips(full, ss, rs, direct, 12)
            sent += all_three(full, ss, rs, direct, 15, lambda c: c, lambda x, y, c: (x, y, 1 - c))
        for cp in got:
            cp.wait_recv()
        for cp in sent:
            cp.wait_send()

    return _Ride([slot], [jax.ShapeDtypeStruct(slot.shape, slot.dtype)], 18, start, finish,
                 mid=mid if (tail is not None or direct is not None) else None, mid_frac=mid_frac, aliases={0: 0})


def _ride_scatter(q, land=None, part=(0, 1)):
    k0, k1, n = part if len(part) == 3 else (part[0], part[0] + 1, part[1])
    rows_n = q.shape[1] // n
    rows = pl.ds(k0 * rows_n, (k1 - k0) * rows_n)

    def copies(ins, outs, ss, rs):
        x, y, c, chips = _place()
        return [_remote(ins[0].at[2 * chip[0] + chip[1], rows, :], outs[0].at[j, rows, :], ss.at[j], rs.at[j], (*chip, c))
                for j, chip in enumerate(chips)]

    def start(*a):
        for cp in copies(*a):
            cp.start()

    def finish(*a):
        for cp in copies(*a):
            cp.wait()

    shape = jax.ShapeDtypeStruct((3,) + q.shape[1:], q.dtype)
    if land is None:
        return _Ride([q], [shape], 3, start, finish)
    return _Ride([q, land], [shape], 3, start, finish, aliases={1: 0})


def _ride_to_sibling(a, halves=False, first=False, shards=None, land=None):
    s0, s1 = shards or (0, a.shape[0])

    def copy(ins, outs, ss, rs):
        x, y, c, _ = _place()
        if halves:
            src, dst = ins[0].at[s0:s1, 1 - c], outs[0].at[s0:s1]
        else:
            src, dst = (ins[0].at[0] if first else ins[0]), outs[0]
        return _remote(src, dst, ss.at[0], rs.at[0], (x, y, 1 - c))

    shape = (a.shape[0],) + a.shape[2:] if halves else (a.shape[1:] if first else a.shape)
    return _Ride([a] if land is None else [a, land], [jax.ShapeDtypeStruct(shape, a.dtype)], 1,
                 lambda *a_: copy(*a_).start(), lambda *a_: copy(*a_).wait(), aliases=None if land is None else {1: 0})


def _ride_rows_to_sibling(a, hr, shards, total):
    def copies(ins, outs, ss, rs):
        x, y, c, _ = _place()
        return [_remote(ins[0].at[pl.ds((2 * s + 1 - c) * hr, hr), :], outs[0].at[s], ss.at[s], rs.at[s], (x, y, 1 - c))
                for s in range(shards)]

    def start(*a_):
        for cp in copies(*a_):
            cp.start()

    def finish(*a_):
        for cp in copies(*a_):
            cp.wait()

    return _Ride([a], [jax.ShapeDtypeStruct((total, hr, a.shape[1]), a.dtype)], shards, start, finish)


def _ride_swap(h):
    def copy(ins, outs, ss, rs):
        x, y, c, _ = _place()
        return _remote(ins[0], outs[0], ss.at[0], rs.at[0], (x, y, 1 - c))

    return _Ride([h], [jax.ShapeDtypeStruct(h.shape, h.dtype)], 1,
                 lambda *a: copy(*a).start(), lambda *a: copy(*a).wait())


def _mesh_place(p):
    return (p // 4, (p // 2) % 2, p % 2)


def _ride_small_to_all(packed):
    def copies(ins, outs, ss, rs):
        x, y, c, _ = _place()
        me = 4 * x + 2 * y + c
        return [_remote(ins[0], outs[0].at[me], ss.at[k - 1], rs.at[k - 1], _mesh_place((me + k) % N_DEV))
                for k in range(1, N_DEV)]

    def own(ins, outs, ss, rs):
        x, y, c, _ = _place()
        return pltpu.make_async_copy(ins[0], outs[0].at[4 * x + 2 * y + c], ss.at[N_DEV - 1])

    def start(*a):
        own(*a).start()
        for cp in copies(*a):
            cp.start()

    def finish(ins, outs, ss, rs):
        x, y, c, _ = _place()
        me = 4 * x + 2 * y + c
        for k in range(1, N_DEV):
            _remote(ins[0], outs[0].at[(me + N_DEV - k) % N_DEV], ss.at[k - 1], rs.at[k - 1], (x, y, c)).wait_recv()
        for cp in copies(ins, outs, ss, rs):
            cp.wait_send()
        own(ins, outs, ss, rs).wait()

    return _Ride([packed], [jax.ShapeDtypeStruct((N_DEV,) + packed.shape, packed.dtype)], N_DEV, start, finish)


def _carrier(rides, *, name):
    _, outs = _call(lambda: None, name=name, grid=(1,), in_specs=[], out_specs=[], out_shape=[], rides=rides)()
    return outs


def _norm_bf16(a_ref, g_ref):
    xf = a_ref[...]
    r = lax.rsqrt(jnp.mean(xf * xf, axis=-1, keepdims=True) + EPS)
    return ((xf * r) * g_ref[...]).astype(BF16)


def _norm_matmul_wide(a, g, b, *, tm, tn, name, rides=()):
    T, K = a.shape
    N = b.shape[0]

    def body(a_ref, g_ref, b_ref, n_ref, o_ref):
        n = _norm_bf16(a_ref, g_ref)
        n_ref[...] = n
        o_ref[...] = _dot_nt(n, b_ref[...])

    return _call(
        body, name=name, grid=(N // tn, T // tm),
        in_specs=[pl.BlockSpec((tm, K), lambda j, i: (i, 0)), pl.BlockSpec((1, K), lambda j, i: (0, 0)),
                  pl.BlockSpec((tn, K), lambda j, i: (j, 0))],
        out_specs=[pl.BlockSpec((None, tm, K), lambda j, i: (j, i, 0)), pl.BlockSpec((tm, tn), lambda j, i: (i, j))],
        out_shape=[jax.ShapeDtypeStruct((N // tn, T, K), BF16), jax.ShapeDtypeStruct((T, N), F32)],
        sem=("arbitrary", "arbitrary"), rides=rides,
    )(a, g, b)


def _norm_matmul_sq(a, g, b, *, tm, tn, name, rides=()):
    T, K = a.shape
    per = b.shape[2] // tn
    N = b.shape[0] * b.shape[2]

    def body(a_ref, g_ref, b_ref, nt_ref, o_ref, z_ref, zt_ref, n_scr):
        @pl.when(pl.program_id(1) == 0)
        def _():
            n = _norm_bf16(a_ref, g_ref)
            n_scr[...] = n
            nt_ref[...] = n.T
        r = jnp.maximum(_dot(n_scr[...], b_ref[...]), 0.0)
        o_ref[...] = r.astype(BF16)
        z = (r * r).astype(BF16)
        z_ref[...] = z
        zt_ref[...] = z.T

    return _call(
        body, name=name, grid=(T // tm, N // tn),
        in_specs=[pl.BlockSpec((tm, K), lambda i, j: (i, 0)), pl.BlockSpec((1, K), lambda i, j: (0, 0)),
                  pl.BlockSpec((None, K, tn), lambda i, j: (j // per, 0, j % per))],
        out_specs=[pl.BlockSpec((K, tm), lambda i, j: (0, i)), pl.BlockSpec((tm, tn), lambda i, j: (i, j)),
                   pl.BlockSpec((tm, tn), lambda i, j: (i, j)), pl.BlockSpec((tn, tm), lambda i, j: (j, i))],
        out_shape=[jax.ShapeDtypeStruct((K, T), BF16), jax.ShapeDtypeStruct((T, N), BF16),
                   jax.ShapeDtypeStruct((T, N), BF16), jax.ShapeDtypeStruct((N, T), BF16)],
        scratch_shapes=[pltpu.VMEM((tm, K), BF16)],
        sem=("parallel", "arbitrary"), rides=rides,
    )(a, g, b)


def _grad_pair(at, at_sib, b, b_sib, *, cols_sharded, tmo, tk, name, rides=()):
    S, _, hr, T = at.shape
    C = b.shape[-1] // N_CHIPS if cols_sharded else b.shape[-1]
    nk = T // tk
    a_sel = (lambda s: 0) if cols_sharded else (lambda s: s)
    b_sel = (lambda s: s) if cols_sharded else (lambda s: 0)
    if b.ndim == 3:
        b_spec = pl.BlockSpec((None, tk, C), lambda s, i, k: (0, k, b_sel(s)))
    else:
        b_spec = pl.BlockSpec((tk, C), lambda s, i, k: (k, b_sel(s)))

    def body(a_ref, as_ref, b_ref, bs_ref, o_ref, ob_ref):
        k = pl.program_id(2)
        p = _dot(a_ref[...], b_ref[...]) + _dot(as_ref[...], bs_ref[...])

        @pl.when(k == 0)
        def _():
            o_ref[...] = p

        @pl.when(k > 0)
        def _():
            o_ref[...] += p

        @pl.when(k == nk - 1)
        def _():
            ob_ref[...] = o_ref[...].astype(BF16)

    out = pl.BlockSpec((None, tmo, C), lambda s, i, k: (s, i, 0))
    return _call(
        body, name=name, grid=(N_CHIPS, hr // tmo, nk),
        in_specs=[pl.BlockSpec((None, None, tmo, tk), lambda s, i, k: (a_sel(s), lax.axis_index("c"), i, k)),
                  pl.BlockSpec((None, tmo, tk), lambda s, i, k: (a_sel(s), i, k)),
                  b_spec, pl.BlockSpec((tk, C), lambda s, i, k: (k, b_sel(s)))],
        out_specs=[out, out],
        out_shape=[jax.ShapeDtypeStruct((N_CHIPS, hr, C), F32), jax.ShapeDtypeStruct((N_CHIPS, hr, C), BF16)],
        sem=("parallel", "parallel", "arbitrary"), rides=rides,
    )(at, at_sib, b, b_sib)


def _grad_pair_merged(at, at_sib, b, b_sib, *, tk, name, rides=()):
    S, _, hr, T = at.shape
    C = b.shape[-1]
    nk = T // tk

    def body(a_ref, as_ref, b_ref, bs_ref, o_ref, ob_ref):
        k = pl.program_id(0)
        p = (_dot(a_ref[...].reshape(S * hr, tk), b_ref[...])
             + _dot(as_ref[...].reshape(S * hr, tk), bs_ref[...])).reshape(S, hr, C)

        @pl.when(k == 0)
        def _():
            o_ref[...] = p

        @pl.when(k > 0)
        def _():
            o_ref[...] += p

        @pl.when(k == nk - 1)
        def _():
            ob_ref[...] = o_ref[...].astype(BF16)

    out = pl.BlockSpec((S, hr, C), lambda k: (0, 0, 0))
    return _call(
        body, name=name, grid=(nk,),
        in_specs=[pl.BlockSpec((S, None, hr, tk), lambda k: (0, lax.axis_index("c"), 0, k)),
                  pl.BlockSpec((S, hr, tk), lambda k: (0, 0, k)),
                  pl.BlockSpec((tk, C), lambda k: (k, 0)), pl.BlockSpec((tk, C), lambda k: (k, 0))],
        out_specs=[out, out],
        out_shape=[jax.ShapeDtypeStruct((S, hr, C), F32), jax.ShapeDtypeStruct((S, hr, C), BF16)],
        sem=("arbitrary",), rides=rides,
    )(at, at_sib, b, b_sib)


def _matmul_parts(parts, b, *, tm, tn, name, rides=()):
    T = parts[0].shape[0]
    N = b.shape[1]
    offs = [sum(p.shape[1] for p in parts[:i]) for i in range(len(parts))]
    assert all(o % p.shape[1] == 0 for o, p in zip(offs, parts))

    def body(*refs):
        n = len(parts)
        acc = _dot(refs[0][...], refs[n][...])
        for i in range(1, n):
            acc = acc + _dot(refs[i][...], refs[n + i][...])
        refs[-1][...] = acc

    a_specs = [pl.BlockSpec((tm, p.shape[1]), lambda i, j: (i, 0)) for p in parts]
    b_specs = [pl.BlockSpec((p.shape[1], tn), lambda i, j, r=o // p.shape[1]: (r, j)) for o, p in zip(offs, parts)]
    return _call(
        body, name=name, grid=(T // tm, N // tn), in_specs=a_specs + b_specs,
        out_specs=pl.BlockSpec((tm, tn), lambda i, j: (i, j)), out_shape=jax.ShapeDtypeStruct((T, N), F32),
        sem=("parallel", "parallel"), rides=rides,
    )(*parts, *([b] * len(parts)))


def _to_bf16(v):
    return v.astype(BF16)


def _matmul_res(a, b, res, *, tm, tn, tk, prologue, name, rides=()):
    T, K = a.shape
    N = b.shape[1]

    def body(a_ref, b_ref, res_ref, o_ref):
        k = pl.program_id(2)
        p = _dot(prologue(a_ref[...]), b_ref[...])

        @pl.when(k == 0)
        def _():
            o_ref[...] = res_ref[...] + p

        @pl.when(k > 0)
        def _():
            o_ref[...] += p

    return _call(
        body, name=name, grid=(T // tm, N // tn, K // tk),
        in_specs=[pl.BlockSpec((tm, tk), lambda i, j, k: (i, k)), pl.BlockSpec((tk, tn), lambda i, j, k: (k, j)),
                  pl.BlockSpec((tm, tn), lambda i, j, k: (i, j))],
        out_specs=pl.BlockSpec((tm, tn), lambda i, j, k: (i, j)),
        out_shape=jax.ShapeDtypeStruct((T, N), F32),
        sem=("parallel", "parallel", "arbitrary"), rides=rides,
    )(a, b, res)


def _matmul_nt(a, b, *, tm, tn, tk, name, extra=None, epilogue=None, out_dtype=F32, rides=()):
    T, K = a.shape
    two = b.ndim == 3 and tk == 2 * b.shape[2]
    if two:
        N, ks = b.shape[1], b.shape[2]
        b_specs = [pl.BlockSpec((None, tn, ks), lambda i, j, k: (2 * k, j, 0)),
                   pl.BlockSpec((None, tn, ks), lambda i, j, k: (2 * k + 1, j, 0))]
    elif b.ndim == 3:
        per = b.shape[2] // tk
        N = b.shape[1]
        b_specs = [pl.BlockSpec((None, tn, tk), lambda i, j, k: (k // per, j, k % per))]
    else:
        N = b.shape[0]
        b_specs = [pl.BlockSpec((tn, tk), lambda i, j, k: (j, k))]
    nb = len(b_specs)
    nk = K // tk
    assert out_dtype == F32 or nk == 1
    in_specs = [pl.BlockSpec((tm, tk), lambda i, j, k: (i, k))] + b_specs
    args = [a] + [b] * nb
    if extra is not None:
        in_specs.append(pl.BlockSpec((tm, tn), lambda i, j, k: (i, j)))
        args.append(extra)

    def body(*refs):
        a_ref, b_ref = refs[0], refs[1]
        o_ref = refs[-1]
        if two:
            p = (_dot_nt(a_ref[:, :tk // 2].astype(BF16), refs[1][...])
                 + _dot_nt(a_ref[:, tk // 2:].astype(BF16), refs[2][...]))
        else:
            p = _dot_nt(a_ref[...].astype(BF16), b_ref[...])
        if nk == 1:
            if epilogue is not None:
                p = epilogue(p, refs[1 + nb][...])
            o_ref[...] = p.astype(out_dtype)
        else:
            k = pl.program_id(2)

            @pl.when(k == 0)
            def _():
                o_ref[...] = p

            @pl.when(k > 0)
            def _():
                o_ref[...] += p

    return _call(
        body, name=name, grid=(T // tm, N // tn, nk),
        in_specs=in_specs,
        out_specs=pl.BlockSpec((tm, tn), lambda i, j, k: (i, j)),
        out_shape=jax.ShapeDtypeStruct((T, N), out_dtype),
        sem=("parallel", "parallel", "arbitrary"), rides=rides,
    )(*args)


def _loss_bwd(h2, tgt, g, *, tm):
    T, D = h2.shape

    def body(h_ref, t_ref, g_ref, dh_ref, dhb_ref, dg_ref, loss_ref):
        @pl.when(pl.program_id(0) == 0)
        def _():
            dg_ref[...] = jnp.zeros_like(dg_ref)
            loss_ref[...] = jnp.zeros_like(loss_ref)
        h = h_ref[...]
        gg = g_ref[...]
        r = lax.rsqrt(jnp.mean(h * h, axis=-1, keepdims=True) + EPS)
        hn = h * r
        err = hn * gg - t_ref[...]
        loss_ref[...] += 0.5 * jnp.sum(jnp.mean(err * err, axis=-1, keepdims=True), axis=0, keepdims=True)
        dy = err * (1.0 / D)
        dg_ref[...] += jnp.sum(dy * hn, axis=0, keepdims=True)
        w = dy * gg
        dh = r * w - h * ((r * r * r) * jnp.mean(w * h, axis=-1, keepdims=True))
        dh_ref[...] = dh
        dhb_ref[...] = dh.astype(BF16)

    tile = pl.BlockSpec((tm, D), lambda i: (i, 0))
    return pl.pallas_call(
        body, name="loss_bwd", grid=(T // tm,),
        in_specs=[tile, tile, pl.BlockSpec((1, D), lambda i: (0, 0))],
        out_specs=[tile, tile, pl.BlockSpec((1, D), lambda i: (0, 0)), pl.BlockSpec((1, 1), lambda i: (0, 0))],
        out_shape=[jax.ShapeDtypeStruct((T, D), F32), jax.ShapeDtypeStruct((T, D), BF16),
                   jax.ShapeDtypeStruct((1, D), F32), jax.ShapeDtypeStruct((1, 1), F32)],
        compiler_params=_params(("arbitrary",)),
    )(h2, tgt, g)


def _rms_bwd_res(dn, h, g, dres, *, tm, name, bf16_copy=True, rides=()):
    T, D = h.shape

    def body(dn_ref, h_ref, g_ref, dres_ref, dh_ref, *rest):
        dg_ref = rest[-1]

        @pl.when(pl.program_id(0) == 0)
        def _():
            dg_ref[...] = jnp.zeros_like(dg_ref)
        h_ = h_ref[...]
        dn_ = dn_ref[...]
        dh, r = _rms_bwd(dn_, h_, g_ref[...])
        dg_ref[...] += jnp.sum(dn_ * (h_ * r), axis=0, keepdims=True)
        dh = dres_ref[...] + dh
        dh_ref[...] = dh
        if bf16_copy:
            rest[0][...] = dh.astype(BF16)

    tile = pl.BlockSpec((tm, D), lambda i: (i, 0))
    row = pl.BlockSpec((1, D), lambda i: (0, 0))
    copy_spec = [tile] if bf16_copy else []
    copy_shape = [jax.ShapeDtypeStruct((T, D), BF16)] if bf16_copy else []
    return _call(
        body, name=name, grid=(T // tm,),
        in_specs=[tile, tile, row, tile], out_specs=[tile] + copy_spec + [row],
        out_shape=[jax.ShapeDtypeStruct((T, D), F32)] + copy_shape + [jax.ShapeDtypeStruct((1, D), F32)],
        sem=("arbitrary",), rides=rides,
    )(dn, h, g, dres)


def _rel_distance():
    i = lax.broadcasted_iota(jnp.int32, (CHUNK, 2 * CHUNK), 0)
    j = lax.broadcasted_iota(jnp.int32, (CHUNK, 2 * CHUNK), 1)
    return i + CHUNK - j


def _bias_build(table):
    def body(tab_ref, o_ref):
        rel = _rel_distance()
        ge = [rel >= t for t in BUCKET_THR]
        for h in range(B_HEADS):
            cur = jnp.full((CHUNK, 2 * CHUNK), tab_ref[0, h], F32)
            for b in range(1, N_BUCKETS):
                cur = jnp.where(ge[b - 1], tab_ref[b, h], cur)
            o_ref[h] = cur

    return pl.pallas_call(
        body, name="bias_build",
        in_specs=[pl.BlockSpec(memory_space=pltpu.SMEM)],
        out_specs=pl.BlockSpec(memory_space=pltpu.VMEM),
        out_shape=jax.ShapeDtypeStruct((B_HEADS, CHUNK, 2 * CHUNK), F32),
    )(table)


def _bias_grad(dbias):
    def body(db_ref, o_ref, acc_ref):
        rel = _rel_distance()
        lo = [0] + BUCKET_THR
        hi = BUCKET_THR + [CHUNK]
        for b in range(N_BUCKETS):
            m = (rel >= lo[b]) & (rel < hi[b])
            for h in range(B_HEADS):
                row = b * B_HEADS + h
                acc_ref[row:row + 1, :] = jnp.sum(jnp.where(m, db_ref[h], 0.0), axis=0, keepdims=True)
        o_ref[...] = jnp.sum(acc_ref[...], axis=1, keepdims=True)

    return pl.pallas_call(
        body, name="bias_grad",
        in_specs=[pl.BlockSpec(memory_space=pltpu.VMEM)],
        out_specs=pl.BlockSpec(memory_space=pltpu.VMEM),
        out_shape=jax.ShapeDtypeStruct((N_BUCKETS * B_HEADS, 1), F32),
        scratch_shapes=[pltpu.VMEM((N_BUCKETS * B_HEADS, 2 * CHUNK), F32)],
    )(dbias)


def _causal_mask():
    t = lax.broadcasted_iota(jnp.int32, (CHUNK, CHUNK), 0)
    s = lax.broadcasted_iota(jnp.int32, (CHUNK, CHUNK), 1)
    return s <= t


def _band_mask(n):
    rel = _rel_distance()
    j = lax.broadcasted_iota(jnp.int32, (CHUNK, 2 * CHUNK), 1)
    return (rel >= 0) & (rel < CHUNK) & ((n > 0) | (j >= CHUNK))


def _gate_forward(u, v, lg, lb, wc, bs):
    ug = _gelu(u)
    vg = _gelu(v)
    mu = jnp.mean(vg, axis=-1, keepdims=True)
    xc = vg - mu
    rstd = lax.rsqrt(jnp.mean(xc * xc, axis=-1, keepdims=True) + EPS)
    xhat = xc * rstd
    vl = (xhat * lg + lb).astype(BF16)
    mixed = _dot(wc, vl) + bs
    return ug, xhat, rstd, vl, mixed


def _softmax_scores(qk, bias, mask, sink):
    s = qk * SCALE + bias
    s = jnp.where(mask, s, NEG)
    m = jnp.maximum(jnp.max(s, axis=-1, keepdims=True), sink)
    p = jnp.exp(s - m)
    e_sink = jnp.exp(sink - m)
    inv = 1.0 / (jnp.sum(p, axis=-1, keepdims=True) + e_sink)
    return p * inv, e_sink * inv


PAIRS = Q_PER_KV // 2


def _head(g, pr, e):
    return g * Q_PER_KV + 2 * pr + e


def _stack_pairs(ref, g, col0=0):
    w = 2 * HEAD_DIM
    return jnp.concatenate([ref[:, col0 + (g * PAIRS + pr) * w:col0 + (g * PAIRS + pr + 1) * w] for pr in range(PAIRS)],
                           axis=0)


def _low_lanes():
    return lax.broadcasted_iota(jnp.int32, (2 * CHUNK, 2 * HEAD_DIM), 1) < HEAD_DIM


def _band_operands(kv_prev, kv_cur):
    band = jnp.concatenate([kv_prev, kv_cur], axis=0)
    low = _low_lanes()
    ops = []
    for cat in (band[:, :KV_WIDTH], band[:, KV_WIDTH:]):
        rol = pltpu.roll(cat, HEAD_DIM, 1)
        ops.append([[jnp.where(low if e == 0 else ~low, cat if g == e else rol, 0.0).astype(BF16) for e in range(2)]
                    for g in range(2)])
    return ops


def _mixer_fwd(proj, lg, lb, wsp, bs_col, sinks, bias, ga, gb, rides=()):
    T = proj.shape[0]
    nb = T // CHUNK

    def body(u_ref, v_ref, q_ref, kvc_ref, kvp_ref, lg_ref, lb_ref, w_ref, bs_ref, sink_ref, bias_ref,
             ga_ref, gb_ref, mixed_ref, mixed_t_ref, ab_ref):
        n = pl.program_id(0)
        causal = _causal_mask()
        ssq = jnp.zeros((CHUNK, 1), F32)
        for g in range(A_GROUPS):
            cols = slice(g * CHUNK, (g + 1) * CHUNK)
            wc = jnp.where(causal, w_ref[g], 0.0).astype(BF16)
            ug, _, _, _, mixed = _gate_forward(u_ref[:, cols], v_ref[:, cols], lg_ref[g:g + 1, :], lb_ref[g:g + 1, :],
                                               wc, bs_ref[g])
            a = ug * mixed
            ab_ref[:, cols] = a
            ssq = ssq + jnp.sum(a * a, axis=-1, keepdims=True)
        ra = lax.rsqrt(ssq * (1.0 / A_WIDTH) + EPS)
        mixed_ref[:, :A_WIDTH] = ((ab_ref[:, :A_WIDTH] * ra) * ga_ref[...]).astype(BF16)

        mask = _band_mask(n)
        kops, vops = _band_operands(kvp_ref[...], kvc_ref[...])
        ssq = jnp.zeros((CHUNK, 1), F32)
        for g in range(B_HEADS // Q_PER_KV):
            qst = _stack_pairs(q_ref, g).astype(BF16)
            o_st = jnp.zeros((PAIRS * CHUNK, 2 * HEAD_DIM), F32)
            for e in range(2):
                s_all = _dot_nt(qst, kops[g][e])
                ps = []
                for pr in range(PAIRS):
                    h = _head(g, pr, e)
                    p, _ = _softmax_scores(s_all[pr * CHUNK:(pr + 1) * CHUNK], bias_ref[h], mask, sink_ref[0, h])
                    ps.append(p.astype(BF16))
                o_st = o_st + _dot(jnp.concatenate(ps, axis=0), vops[g][e])
            for pr in range(PAIRS):
                o = o_st[pr * CHUNK:(pr + 1) * CHUNK]
                c0 = A_WIDTH + (g * PAIRS + pr) * 2 * HEAD_DIM
                ab_ref[:, c0:c0 + 2 * HEAD_DIM] = o
                ssq = ssq + jnp.sum(o * o, axis=-1, keepdims=True)
        rb = lax.rsqrt(ssq * (1.0 / B_WIDTH) + EPS)
        mixed_ref[:, A_WIDTH:] = ((ab_ref[:, A_WIDTH:] * rb) * gb_ref[...]).astype(BF16)
        mixed_t_ref[...] = mixed_ref[...].T

    full = lambda *shape: pl.BlockSpec(shape, lambda n: (0,) * len(shape))
    return _call(
        body, name="mixer_fwd", grid=(nb,),
        in_specs=[pl.BlockSpec((CHUNK, A_WIDTH), lambda n: (n, 0)),
                  pl.BlockSpec((CHUNK, A_WIDTH), lambda n: (n, 1)),
                  pl.BlockSpec((CHUNK, B_WIDTH), lambda n: (n, 2)),
                  pl.BlockSpec((CHUNK, 2 * KV_WIDTH), lambda n: (n, 12)),
                  pl.BlockSpec((CHUNK, 2 * KV_WIDTH), lambda n: (jnp.maximum(n - 1, 0), 12)),
                  full(A_GROUPS, CHUNK), full(A_GROUPS, CHUNK), full(A_GROUPS, CHUNK, CHUNK), full(A_GROUPS, CHUNK, 1),
                  pl.BlockSpec(memory_space=pltpu.SMEM), full(B_HEADS, CHUNK, 2 * CHUNK),
                  full(1, A_WIDTH), full(1, B_WIDTH)],
        out_specs=[pl.BlockSpec((CHUNK, D_MODEL), lambda n: (n, 0)), pl.BlockSpec((D_MODEL, CHUNK), lambda n: (0, n)),
                   pl.BlockSpec((CHUNK, D_MODEL), lambda n: (n, 0))],
        out_shape=[jax.ShapeDtypeStruct((T, D_MODEL), BF16), jax.ShapeDtypeStruct((D_MODEL, T), BF16),
                   jax.ShapeDtypeStruct((T, D_MODEL), F32)],
        sem=("parallel",), rides=rides,
    )(proj, proj, proj, proj, proj, lg, lb, wsp, bs_col, sinks, bias, ga, gb)


def _gmlp_bwd(proj, ab, dmixed, ga, lg, lb, wsp, bs_col, rides=()):
    T = proj.shape[0]
    nb = T // CHUNK

    def body(u_ref, v_ref, a_ref, dna_ref, ga_ref, lg_ref, lb_ref, w_ref, bs_ref,
             dp_ref, dpt_ref, dga_ref, dw_ref, dbs_ref, dlg_ref, dlb_ref):
        @pl.when(pl.program_id(0) == 0)
        def _():
            for r in (dga_ref, dw_ref, dbs_ref, dlg_ref, dlb_ref):
                r[...] = jnp.zeros_like(r)
        causal = _causal_mask()
        a_all = a_ref[...]
        dna = dna_ref[...]
        da_all, ra = _rms_bwd(dna, a_all, ga_ref[...])
        dga_ref[...] += jnp.sum(dna * (a_all * ra), axis=0, keepdims=True)
        for g in range(A_GROUPS):
            cols = slice(g * CHUNK, (g + 1) * CHUNK)
            wc = jnp.where(causal, w_ref[g], 0.0).astype(BF16)
            lgg = lg_ref[g:g + 1, :]
            u = u_ref[:, cols]
            v = v_ref[:, cols]
            ug, xhat, rstd, vl, mixed = _gate_forward(u, v, lgg, lb_ref[g:g + 1, :], wc, bs_ref[g])
            da = da_all[:, cols]
            dug = da * mixed
            dmg = da * ug
            dmg_b = dmg.astype(BF16)
            dbs_ref[g] += jnp.sum(dmg, axis=-1, keepdims=True)
            dw_ref[g] += jnp.where(causal, _dot_nt(dmg_b, vl), 0.0)
            dvl = _dot_tn(wc, dmg_b)
            dlg_ref[g:g + 1, :] += jnp.sum(dvl * xhat, axis=0, keepdims=True)
            dlb_ref[g:g + 1, :] += jnp.sum(dvl, axis=0, keepdims=True)
            dxh = dvl * lgg
            dvg = rstd * (dxh - jnp.mean(dxh, axis=-1, keepdims=True)
                          - xhat * jnp.mean(dxh * xhat, axis=-1, keepdims=True))
            _, gu = _gelu_and_grad(u)
            _, gv = _gelu_and_grad(v)
            dp_ref[:, cols] = (dug * gu).astype(BF16)
            dp_ref[:, A_WIDTH + g * CHUNK:A_WIDTH + (g + 1) * CHUNK] = (dvg * gv).astype(BF16)
        dpt_ref[...] = dp_ref[...].T

    full = lambda *shape: pl.BlockSpec(shape, lambda n: (0,) * len(shape))
    return _call(
        body, name="gmlp_bwd", grid=(nb,),
        in_specs=[pl.BlockSpec((CHUNK, A_WIDTH), lambda n: (n, 0)),
                  pl.BlockSpec((CHUNK, A_WIDTH), lambda n: (n, 1)),
                  pl.BlockSpec((CHUNK, A_WIDTH), lambda n: (n, 0)),
                  pl.BlockSpec((CHUNK, A_WIDTH), lambda n: (n, 0)),
                  full(1, A_WIDTH), full(A_GROUPS, CHUNK), full(A_GROUPS, CHUNK), full(A_GROUPS, CHUNK, CHUNK),
                  full(A_GROUPS, CHUNK, 1)],
        out_specs=[pl.BlockSpec((CHUNK, 2 * A_WIDTH), lambda n: (n, 0)), pl.BlockSpec((2 * A_WIDTH, CHUNK), lambda n: (0, n)),
                   full(1, A_WIDTH), full(A_GROUPS, CHUNK, CHUNK), full(A_GROUPS, CHUNK, 1),
                   full(A_GROUPS, CHUNK), full(A_GROUPS, CHUNK)],
        out_shape=[jax.ShapeDtypeStruct((T, 2 * A_WIDTH), BF16), jax.ShapeDtypeStruct((2 * A_WIDTH, T), BF16),
                   jax.ShapeDtypeStruct((1, A_WIDTH), F32), jax.ShapeDtypeStruct((A_GROUPS, CHUNK, CHUNK), F32),
                   jax.ShapeDtypeStruct((A_GROUPS, CHUNK, 1), F32), jax.ShapeDtypeStruct((A_GROUPS, CHUNK), F32),
                   jax.ShapeDtypeStruct((A_GROUPS, CHUNK), F32)],
        sem=("arbitrary",), rides=rides,
    )(proj, proj, ab, dmixed, ga, lg, lb, wsp, bs_col)


def _attn_bwd(proj, ab, dmixed, gb, sinks, bias, rides=()):
    T = proj.shape[0]
    nb = T // CHUNK
    qn = lambda n: jnp.minimum(n, nb - 1)

    def body(q_ref, kvc_ref, kvp_ref, o_ref, dnb_ref, gb_ref, sink_ref, bias_ref,
             dq_ref, dkv_ref, dqt_ref, dkvt_ref, dgb_ref, dsink_ref, dbias_ref, carry_ref, sacc_ref):
        n = pl.program_id(0)

        @pl.when(n == 0)
        def _():
            carry_ref[...] = jnp.zeros_like(carry_ref)
            sacc_ref[...] = jnp.zeros_like(sacc_ref)
            dgb_ref[...] = jnp.zeros_like(dgb_ref)
            dbias_ref[...] = jnp.zeros_like(dbias_ref)

        @pl.when(n < nb)
        def _():
            mask = _band_mask(n)
            o_all = o_ref[...]
            dnb = dnb_ref[...]
            do_all, rb = _rms_bwd(dnb, o_all, gb_ref[...])
            dgb_ref[...] += jnp.sum(dnb * (o_all * rb), axis=0, keepdims=True)
            kops, vops = _band_operands(kvp_ref[...], kvc_ref[...])
            low = _low_lanes()
            halves = []
            for g in range(B_HEADS // Q_PER_KV):
                qst = _stack_pairs(q_ref, g).astype(BF16)
                dost = _stack_pairs(do_all, g).astype(BF16)
                dq_st = jnp.zeros((PAIRS * CHUNK, 2 * HEAD_DIM), F32)
                dk_e, dv_e = [], []
                for e in range(2):
                    s_all = _dot_nt(qst, kops[g][e])
                    dp_all = _dot_nt(dost, vops[g][e])
                    ps, dsrs = [], []
                    for pr in range(PAIRS):
                        h = _head(g, pr, e)
                        rows = slice(pr * CHUNK, (pr + 1) * CHUNK)
                        p, p_sink = _softmax_scores(s_all[rows], bias_ref[h], mask, sink_ref[0, h])
                        dp = dp_all[rows]
                        delta = jnp.sum(p * dp, axis=-1, keepdims=True)
                        ds = p * (dp - delta)
                        sacc_ref[:, h:h + 1] += -(p_sink * delta)
                        dbias_ref[h] += ds
                        ps.append(p.astype(BF16))
                        dsrs.append((ds * SCALE).astype(BF16))
                    dsr_all = jnp.concatenate(dsrs, axis=0)
                    dq_st = dq_st + _dot(dsr_all, kops[g][e])
                    dk_e.append(_dot_tn(dsr_all, qst))
                    dv_e.append(_dot_tn(jnp.concatenate(ps, axis=0), dost))
                for pr in range(PAIRS):
                    c0 = (g * PAIRS + pr) * 2 * HEAD_DIM
                    dq_ref[:, c0:c0 + 2 * HEAD_DIM] = dq_st[pr * CHUNK:(pr + 1) * CHUNK].astype(BF16)
                halves.append((dk_e, dv_e))
            tiles = []
            for t in range(2):
                g0, g1 = halves[0][t], halves[1][t]
                tiles.append(jnp.where(low, g0[0] + pltpu.roll(g0[1], HEAD_DIM, 1), pltpu.roll(g1[0], HEAD_DIM, 1) + g1[1]))
            dband = jnp.concatenate(tiles, axis=1)
            dkv = (carry_ref[...] + dband[:CHUNK]).astype(BF16)
            dkv_ref[...] = dkv
            dkvt_ref[...] = dkv.T
            dqt_ref[...] = dq_ref[...].T
            carry_ref[...] = dband[CHUNK:]

        @pl.when(n == nb)
        def _():
            dkv = carry_ref[...].astype(BF16)
            dkv_ref[...] = dkv
            dkvt_ref[...] = dkv.T
            dsink_ref[...] = jnp.sum(sacc_ref[...], axis=0, keepdims=True)

    full = lambda *shape: pl.BlockSpec(shape, lambda n: (0,) * len(shape))
    return _call(
        body, name="attn_bwd", grid=(nb + 1,),
        in_specs=[pl.BlockSpec((CHUNK, B_WIDTH), lambda n: (qn(n), 2)),
                  pl.BlockSpec((CHUNK, 2 * KV_WIDTH), lambda n: (qn(n), 12)),
                  pl.BlockSpec((CHUNK, 2 * KV_WIDTH), lambda n: (jnp.maximum(qn(n) - 1, 0), 12)),
                  pl.BlockSpec((CHUNK, B_WIDTH), lambda n: (qn(n), 1)),
                  pl.BlockSpec((CHUNK, B_WIDTH), lambda n: (qn(n), 1)),
                  full(1, B_WIDTH), pl.BlockSpec(memory_space=pltpu.SMEM), full(B_HEADS, CHUNK, 2 * CHUNK)],
        out_specs=[pl.BlockSpec((CHUNK, B_WIDTH), lambda n: (qn(n), 0)),
                   pl.BlockSpec((CHUNK, 2 * KV_WIDTH), lambda n: (jnp.maximum(n - 1, 0), 0)),
                   pl.BlockSpec((B_WIDTH, CHUNK), lambda n: (0, qn(n))),
                   pl.BlockSpec((2 * KV_WIDTH, CHUNK), lambda n: (0, jnp.maximum(n - 1, 0))),
                   full(1, B_WIDTH), full(1, B_HEADS), full(B_HEADS, CHUNK, 2 * CHUNK)],
        out_shape=[jax.ShapeDtypeStruct((T, B_WIDTH), BF16), jax.ShapeDtypeStruct((T, 2 * KV_WIDTH), BF16),
                   jax.ShapeDtypeStruct((B_WIDTH, T), BF16), jax.ShapeDtypeStruct((2 * KV_WIDTH, T), BF16),
                   jax.ShapeDtypeStruct((1, B_WIDTH), F32), jax.ShapeDtypeStruct((1, B_HEADS), F32),
                   jax.ShapeDtypeStruct((B_HEADS, CHUNK, 2 * CHUNK), F32)],
        scratch_shapes=[pltpu.VMEM((CHUNK, 2 * KV_WIDTH), F32), pltpu.VMEM((CHUNK, B_HEADS), F32)],
        sem=("arbitrary",), rides=rides,
    )(proj, proj, proj, ab, dmixed, gb, sinks, bias)


def _sq_relu_grad(acc, r):
    return acc * (2.0 * r.astype(F32))


def _chip_index():
    return (2 * lax.axis_index("x") + lax.axis_index("y")).astype(jnp.int32).reshape(1)


def _cast_into_slot(w, *, tm, name):
    _, R, C = w.shape

    def body(me_ref, w_ref, o_ref):
        del me_ref
        o_ref[...] = w_ref[...].astype(BF16)

    return pl.pallas_call(
        body, name=name,
        grid_spec=pltpu.PrefetchScalarGridSpec(
            num_scalar_prefetch=1, grid=(R // tm,),
            in_specs=[pl.BlockSpec((None, tm, C), lambda i, me: (0, i, 0))],
            out_specs=pl.BlockSpec((None, tm, C), lambda i, me: (me[0], i, 0))),
        out_shape=jax.ShapeDtypeStruct((N_CHIPS, R, C), BF16), compiler_params=_params(("parallel",)),
    )(_chip_index(), w)


def _cast_into_slot_carrying(w, *, tm, name, rides):
    _, R, C = w.shape

    def body(w_ref, o_ref):
        o_ref[...] = w_ref[...].astype(BF16)

    return _call(
        body, name=name, grid=(R // tm,),
        in_specs=[pl.BlockSpec((None, tm, C), lambda i: (0, i, 0))],
        out_specs=pl.BlockSpec((None, tm, C), lambda i: (2 * lax.axis_index("x") + lax.axis_index("y"), i, 0)),
        out_shape=jax.ShapeDtypeStruct((N_CHIPS, R, C), BF16), sem=("arbitrary",), rides=rides,
    )(w)


def _owner_total(gh, others, *, tm, name):
    _, hr, C = gh.shape

    def body(me_ref, g_ref, o_ref_in, out_ref):
        del me_ref
        acc = g_ref[...]
        for j in range(3):
            acc = acc + o_ref_in[j].astype(F32)
        out_ref[...] = acc

    return pl.pallas_call(
        body, name=name,
        grid_spec=pltpu.PrefetchScalarGridSpec(
            num_scalar_prefetch=1, grid=(hr // tm,),
            in_specs=[pl.BlockSpec((None, tm, C), lambda i, me: (me[0], i, 0)),
                      pl.BlockSpec((3, tm, C), lambda i, me: (0, i, 0))],
            out_specs=pl.BlockSpec((tm, C), lambda i, me: (i, 0))),
        out_shape=jax.ShapeDtypeStruct((hr, C), F32),
        compiler_params=_params(("parallel",)),
    )(_chip_index(), gh, others)


def _adamw_math(w, g, m, v):
    m = ADAM_B1 * m + (1.0 - ADAM_B1) * g
    v = ADAM_B2 * v + (1.0 - ADAM_B2) * (g * g)
    m_hat = m / (1.0 - ADAM_B1 ** ADAM_STEP)
    v_hat = v / (1.0 - ADAM_B2 ** ADAM_STEP)
    delta = -ADAM_LR * (m_hat / (jnp.sqrt(v_hat) + ADAM_EPS) + ADAM_WD * w)
    return delta, m, v


def _adamw_halves(w, own, got, m, v, *, tm, name, rides=()):
    _, R, C = w.shape
    nt = (R // 2) // tm

    def body(w_ref, own_ref, got_ref, m_ref, v_ref, g_ref, d_ref, nm_ref, nv_ref):
        g = jnp.where(pl.program_id(0) == lax.axis_index("c"), own_ref[...], got_ref[...])
        g_ref[...] = g
        d_ref[...], nm_ref[...], nv_ref[...] = _adamw_math(w_ref[...], g, m_ref[...], v_ref[...])

    whole = pl.BlockSpec((None, tm, C), lambda h, i: (0, h * nt + i, 0))
    half = pl.BlockSpec((tm, C), lambda h, i: (i, 0))
    return _call(
        body, name=name, grid=(2, nt), in_specs=[whole, half, half, whole, whole], out_specs=[whole] * 4,
        out_shape=[jax.ShapeDtypeStruct((1, R, C), F32)] * 4, sem=("parallel", "parallel"), rides=rides,
    )(w, own, got, m, v)


def _adamw_small(w, slots, m, v, *, name):
    def body(w_ref, slots_ref, m_ref, v_ref, g_ref, d_ref, nm_ref, nv_ref):
        g = slots_ref[0]
        for d in range(1, N_DEV):
            g = g + slots_ref[d]
        g_ref[...] = g
        d_ref[...], nm_ref[...], nv_ref[...] = _adamw_math(w_ref[...], g, m_ref[...], v_ref[...])

    vmem = pl.BlockSpec(memory_space=pltpu.VMEM)
    return pl.pallas_call(
        body, name=name, in_specs=[vmem] * 4, out_specs=[vmem] * 4,
        out_shape=[jax.ShapeDtypeStruct(w.shape, F32)] * 4, compiler_params=_params(),
    )(w, slots, m, v)


SMALL = ["rel_bias_table", "mix_norm_g", "gate_norm_g", "gate_norm_b", "w_spatial", "b_spatial", "attn_sinks",
         "out_norm_a_g", "out_norm_b_g", "ffn_norm_g", "final_norm_g"]
SMALL_A = ["gate_norm_g", "gate_norm_b", "w_spatial", "b_spatial", "out_norm_a_g"]
SMALL_B = ["rel_bias_table", "mix_norm_g", "attn_sinks", "out_norm_b_g", "ffn_norm_g", "final_norm_g"]
LARGE = ["w_in", "w_out", "w_up", "w_down"]
ROW_TILE = {"w_in": 208, "w_out": 256, "w_up": 256, "w_down": 256}
WEIGHTS = ["rel_bias_table", "mix_norm_g", "w_in", "gate_norm_g", "gate_norm_b", "w_spatial", "b_spatial", "attn_sinks",
           "out_norm_a_g", "out_norm_b_g", "w_out", "ffn_norm_g", "w_up", "w_down", "final_norm_g"]
PACK_UNIT = 8 * 128


def _pack(parts):
    rows = []
    for p in parts:
        flat = p.reshape(-1)
        pad = (-flat.shape[0]) % PACK_UNIT
        rows.append(jnp.pad(flat, (0, pad)).reshape(-1, 128))
    return jnp.concatenate(rows, axis=0)


def _unpack(packed, like):
    out, row = [], 0
    for p in like:
        n = math.prod(p.shape)
        nrows = (n + PACK_UNIT - 1) // PACK_UNIT * 8
        out.append(packed[row:row + nrows].reshape(-1)[:n].reshape(p.shape))
        row += nrows
    return out


def kernel(x, rel_bias_table, mix_norm_g, w_in, gate_norm_g, gate_norm_b, w_spatial, b_spatial, attn_sinks, out_norm_a_g, out_norm_b_g, w_out, ffn_norm_g, w_up, w_down, final_norm_g, loss_target, m_rel_bias_table, m_mix_norm_g, m_w_in, m_gate_norm_g, m_gate_norm_b, m_w_spatial, m_b_spatial, m_attn_sinks, m_out_norm_a_g, m_out_norm_b_g, m_w_out, m_ffn_norm_g, m_w_up, m_w_down, m_final_norm_g, v_rel_bias_table, v_mix_norm_g, v_w_in, v_gate_norm_g, v_gate_norm_b, v_w_spatial, v_b_spatial, v_attn_sinks, v_out_norm_a_g, v_out_norm_b_g, v_w_out, v_ffn_norm_g, v_w_up, v_w_down, v_final_norm_g):
    args = dict(locals())
    wts = {n: args[n] for n in WEIGHTS}
    mom = {n: args["m_" + n] for n in WEIGHTS}
    var = {n: args["v_" + n] for n in WEIGHTS}
    sp = {n: wts[n] for n in SMALL}
    x2, tgt = x[0], loss_target[0]
    T = x2.shape[0]
    tm = min(512, T)
    tl = min(1024, T)
    lg = sp["gate_norm_g"].reshape(A_GROUPS, CHUNK)
    lb = sp["gate_norm_b"].reshape(A_GROUPS, CHUNK)
    wsp = sp["w_spatial"].reshape(A_GROUPS, CHUNK, CHUNK)
    bs_col = sp["b_spatial"].reshape(A_GROUPS, CHUNK, 1)
    sinks = sp["attn_sinks"].reshape(1, B_HEADS)
    ga = sp["out_norm_a_g"].reshape(1, A_WIDTH)
    gb = sp["out_norm_b_g"].reshape(1, B_WIDTH)
    g1 = sp["mix_norm_g"].reshape(1, D_MODEL)
    g2 = sp["ffn_norm_g"].reshape(1, D_MODEL)
    gf = sp["final_norm_g"].reshape(1, D_MODEL)

    def owner_total(n, gh, others):
        return _owner_total(gh, others, tm=ROW_TILE[n], name="rs_owner_total_" + n)

    def halves_view(at, shards):
        return at.reshape(shards, 2, at.shape[0] // shards // 2, at.shape[1])

    for d in (wts, mom, var):
        d["w_in"] = jnp.swapaxes(d["w_in"], 1, 2)

    s_in = _cast_into_slot(wts["w_in"], tm=ROW_TILE["w_in"], name="cast_w_in")
    s_out = _cast_into_slot(wts["w_out"], tm=256, name="cast_w_out")
    s_up, ((g_in,),) = _cast_into_slot_carrying(wts["w_up"], tm=256, name="cast_w_up",
                                                rides=[_ride_gather(s_in, direct=(0, 1, 1))])
    s_down = _cast_into_slot(wts["w_down"], tm=256, name="cast_w_down")
    win_t = g_in.reshape(PROJ_WIDTH, D_MODEL)
    bias = _bias_build(sp["rel_bias_table"])
    (n1, proj), ((g_out,), (s_up,)) = _norm_matmul_wide(
        x2, g1, win_t, tm=tm, tn=PROJ_WIDTH // 2, name="in_proj",
        rides=[_ride_gather(s_out, direct=(0, 1, 1)), _ride_gather(s_up, s1=(0, 3, 8))])
    wo = g_out.reshape(A_WIDTH + B_WIDTH, D_MODEL)
    (mixed, mixed_t, ab), ((s_up,), (s_down,), (n1_sib,)) = _mixer_fwd(
        proj, lg, lb, wsp, bs_col, sinks, bias, ga, gb,
        rides=[_ride_gather(s_up, s2=(0, 3, 8), s1=(3, 8, 8)), _ride_gather(s_down, s1=(0, 3, 8)),
               _ride_to_sibling(n1, first=True)])
    mixed_t = halves_view(mixed_t, N_CHIPS)
    h1, ((wu,), (s_down,), (mixed_t_sib,)) = _matmul_res(
        mixed, wo, x2, tm=tl, tn=1024, tk=D_MODEL, prologue=_to_bf16, name="out_proj",
        rides=[_ride_gather(s_up, s3=(0, 3, 8), tail=(3, 8, 8), mid_frac=0.75), _ride_gather(s_down, s2=(0, 3, 8)),
               _ride_to_sibling(mixed_t, halves=True)])
    (n2t, zp, z2, z2t), ((g_down,),) = _norm_matmul_sq(
        h1, g2, wu, tm=tl, tn=1024, name="up_proj", rides=[_ride_gather(s_down, s3=(0, 3, 8), direct=(3, 8, 8))])
    wd = g_down.reshape(D_FF, D_MODEL)
    n2t, z2t = halves_view(n2t, 1), halves_view(z2t, N_CHIPS)
    h2, ((n2t_sib,), (z2t_sib,)) = _matmul_res(
        z2, wd, h1, tm=tl, tn=1024, tk=4096, prologue=_to_bf16, name="down_proj",
        rides=[_ride_to_sibling(n2t, halves=True), _ride_to_sibling(z2t, halves=True)])

    dh2, dh2b, dgf, loss = _loss_bwd(h2, tgt, gf, tm=tm)
    dzp, ((dh2b_sib,),) = _matmul_nt(dh2b, wd, tm=tl, tn=1024, tk=D_MODEL, name="bwd_dz", extra=zp,
                                     epilogue=_sq_relu_grad, out_dtype=BF16, rides=[_ride_to_sibling(dh2b)])
    (gd, gdb), ((dzp_sib,),) = _grad_pair(z2t, z2t_sib, dh2b, dh2b_sib, cols_sharded=False, tmo=1024, tk=tl,
                                          name="grad_w_down", rides=[_ride_to_sibling(dzp)])
    (gu, gub), ((o_d,),) = _grad_pair(n2t, n2t_sib, dzp, dzp_sib, cols_sharded=True, tmo=1024, tk=tl,
                                      name="grad_w_up", rides=[_ride_scatter(gdb, None, (0, 7, 8))])
    dn2, ((o_d,), (o_u,)) = _matmul_nt(dzp, wu, tm=tl, tn=1024, tk=4096, name="bwd_dn2",
                                       rides=[_ride_scatter(gdb, o_d, (7, 8, 8)), _ride_scatter(gub, None, (0, 6, 8))])
    h_d = owner_total("w_down", gd, o_d)
    (dh1, dh1b, dg2), ((o_u,),) = _rms_bwd_res(dn2, h1, g2, dh2, tm=tm, name="ffn_norm_bwd",
                                               rides=[_ride_scatter(gub, o_u, (6, 7, 8))])
    dmixed, ((o_u,), (dh1b_sib,), (w_d,)) = _matmul_nt(
        dh1b, wo, tm=tl, tn=1024, tk=D_MODEL, name="bwd_dmixed",
        rides=[_ride_scatter(gub, o_u, (7, 8, 8)), _ride_to_sibling(dh1b), _ride_swap(h_d)])
    h_u = owner_total("w_up", gu, o_u)
    (go, gob), ((w_u,),) = _grad_pair_merged(mixed_t, mixed_t_sib, dh1b, dh1b_sib, tk=tl, name="grad_w_out",
                                             rides=[_ride_swap(h_u)])
    (duv, duv_t, dga, dwsp, dbs, dlg, dlb), ((o_o,),) = _gmlp_bwd(proj, ab, dmixed, ga, lg, lb, wsp, bs_col,
                                                                  rides=[_ride_scatter(gob)])
    h_o = owner_total("w_out", go, o_o)
    small = {"gate_norm_g": dlg, "gate_norm_b": dlb, "w_spatial": dwsp, "b_spatial": dbs, "out_norm_a_g": dga}
    hr_in = PROJ_WIDTH // N_CHIPS // 2
    (dq, dkv, dq_t, dkv_t, dgb, dsinks, dbias), ((w_o,), (dproj_t_sib,)) = _attn_bwd(
        proj, ab, dmixed, gb, sinks, bias, rides=[_ride_swap(h_o), _ride_rows_to_sibling(duv_t, hr_in, 2, N_CHIPS)])
    dtable = _bias_grad(dbias)
    dproj_t = halves_view(jnp.concatenate([duv_t, dq_t, dkv_t], axis=0), N_CHIPS)
    ((dproj_t_sib,),) = _carrier([_ride_to_sibling(dproj_t, halves=True, shards=(2, N_CHIPS), land=dproj_t_sib)],
                                 name="trade_dproj_t")
    (gi, gib), ((slots_a,),) = _grad_pair(
        dproj_t, dproj_t_sib, n1, n1_sib, cols_sharded=False, tmo=hr_in, tk=tl, name="grad_w_in",
        rides=[_ride_small_to_all(_pack([small[n] for n in SMALL_A]))])
    dn1, ((o_i,),) = _matmul_parts([duv, dq, dkv], win_t, tm=tl, tn=1024, name="bwd_dn1", rides=[_ride_scatter(gib)])
    h_i = owner_total("w_in", gi, o_i)
    dx, dg1 = _rms_bwd_res(dn1, x2, g1, dh1, tm=tm, name="mix_norm_bwd", bf16_copy=False)
    small.update({"rel_bias_table": dtable.reshape(N_BUCKETS, B_HEADS), "mix_norm_g": dg1, "attn_sinks": dsinks,
                  "out_norm_b_g": dgb, "ffn_norm_g": dg2, "final_norm_g": dgf})
    (w_i,), (slots_b,) = _carrier([_ride_swap(h_i), _ride_small_to_all(_pack([small[n] for n in SMALL_B] + [loss]))],
                                  name="swap_w_in")

    out_g, out_d, out_m, out_v = {}, {}, {}, {}
    for n, h, s in zip(LARGE, [h_i, h_o, h_u, h_d], [w_i, w_o, w_u, w_d]):
        res = _adamw_halves(wts[n], h, s, mom[n], var[n], tm=ROW_TILE[n], name="adamw_" + n)
        if n == "w_in":
            res = [jnp.swapaxes(r, 1, 2) for r in res]
        out_g[n], out_d[n], out_m[n], out_v[n] = res
    for names, slots, tag in ((SMALL_A, slots_a, "a"), (SMALL_B, slots_b, "b")):
        extra = [jnp.zeros((1, 1), F32)] if tag == "b" else []
        like = [wts[n] for n in names] + extra
        res = _adamw_small(_pack(like), slots, _pack([mom[n] for n in names] + extra),
                           _pack([var[n] for n in names] + extra), name="adamw_small_" + tag)
        for store, packed in zip((out_g, out_d, out_m, out_v), res):
            for n, val in zip(names + ["loss"], _unpack(packed, like)):
                store[n] = val

    total = out_g["loss"][0, 0]
    return (total, dx[None], *[out_g[n] for n in WEIGHTS], *[out_d[n] for n in WEIGHTS],
            *[out_m[n] for n in WEIGHTS], *[out_v[n] for n in WEIGHTS])
```

```python
import math

import numpy as np
import jax
import jax.numpy as jnp
from jax import lax
from jax.experimental import pallas as pl
from jax.experimental.pallas import tpu as pltpu

F32 = jnp.float32
BF16 = jnp.bfloat16

D_MODEL = 2048
CHUNK = 128
A_GROUPS = 8
A_WIDTH = 1024
HEAD_DIM = 64
B_HEADS = 16
Q_PER_KV = 8
B_WIDTH = 1024
KV_WIDTH = 128
PROJ_WIDTH = 3328
D_FF = 8192
N_BUCKETS = 32
EPS = 1e-5
NEG = -1e30
SCALE = HEAD_DIM ** -0.5
N_CHIPS = 4
N_DEV = 8

ADAM_LR = 0.001
ADAM_B1 = 0.9
ADAM_B2 = 0.999
ADAM_EPS = 1e-08
ADAM_WD = 0.01
ADAM_STEP = 10

VMEM_LIMIT = 60 * 1024 * 1024
MESH = pl.DeviceIdType.MESH


def _bucket_thresholds():
    d = np.arange(CHUNK)
    n_exact = N_BUCKETS // 2
    relf = np.maximum(d, n_exact).astype(np.float64)
    large = n_exact + (np.log(relf / n_exact) / math.log(CHUNK / n_exact) * (N_BUCKETS - n_exact)).astype(np.int32)
    bucket = np.where(d < n_exact, d, np.minimum(large, N_BUCKETS - 1))
    return [int(np.min(d[bucket >= b])) for b in range(1, N_BUCKETS)]


BUCKET_THR = _bucket_thresholds()


def _params(sem=None):
    return pltpu.CompilerParams(dimension_semantics=sem, vmem_limit_bytes=VMEM_LIMIT)


def _gelu(x):
    c = math.sqrt(2.0 / math.pi)
    return 0.5 * x * (1.0 + jnp.tanh(c * (x + 0.044715 * (x * x * x))))


def _gelu_and_grad(x):
    c = math.sqrt(2.0 / math.pi)
    x2 = x * x
    t = jnp.tanh(c * (x + 0.044715 * (x2 * x)))
    g = 0.5 * x * (1.0 + t)
    dg = 0.5 * (1.0 + t) + 0.5 * x * (1.0 - t * t) * (c * (1.0 + 3.0 * 0.044715 * x2))
    return g, dg


def _dot(a, b):
    return jnp.dot(a, b, preferred_element_type=F32)


def _dot_nt(a, b):
    return lax.dot_general(a, b, (((1,), (1,)), ((), ())), preferred_element_type=F32)


def _dot_tn(a, b):
    return lax.dot_general(a, b, (((0,), (0,)), ((), ())), preferred_element_type=F32)


def _rms_bwd(dn, h, g):
    r = lax.rsqrt(jnp.mean(h * h, axis=-1, keepdims=True) + EPS)
    w = dn * g
    dh = r * w - h * ((r * r * r) * jnp.mean(w * h, axis=-1, keepdims=True))
    return dh, r


def _place():
    x, y, c = lax.axis_index("x"), lax.axis_index("y"), lax.axis_index("c")
    chips = [(1 - x, y), (x, 1 - y), (1 - x, 1 - y)]
    return x, y, c, chips


def _remote(src, dst, send_sem, recv_sem, to):
    return pltpu.make_async_remote_copy(src_ref=src, dst_ref=dst, send_sem=send_sem, recv_sem=recv_sem,
                                        device_id=to, device_id_type=MESH)


class _Ride:
    def __init__(self, args, out_shape, n_sem, start, finish, mid=None, mid_frac=0.8, aliases=None):
        self.args, self.out_shape, self.n_sem = list(args), list(out_shape), n_sem
        self.start, self.mid, self.finish, self.mid_frac = start, mid, finish, mid_frac
        self.aliases = dict(aliases or {})


def _call(body, *, name, grid, in_specs, out_specs, out_shape, scratch_shapes=(), sem=None, rides=()):
    single = not isinstance(out_shape, (list, tuple))
    out_specs = [out_specs] if single else list(out_specs)
    out_shape = [out_shape] if single else list(out_shape)
    n_in, n_out, n_scr = len(in_specs), len(out_shape), len(scratch_shapes)
    r_in = [len(r.args) for r in rides]
    r_out = [len(r.out_shape) for r in rides]
    any_spec = pl.BlockSpec(memory_space=pl.ANY)
    aliases, off_i, off_o = {}, n_in, n_out
    for r in rides:
        for i, o in r.aliases.items():
            aliases[off_i + i] = off_o + o
        off_i += len(r.args)
        off_o += len(r.out_shape)
    steps = math.prod(grid)

    def wrapped(*refs):
        p = 0
        ins = refs[p:p + n_in]; p += n_in
        rins = refs[p:p + sum(r_in)]; p += sum(r_in)
        outs = refs[p:p + n_out]; p += n_out
        routs = refs[p:p + sum(r_out)]; p += sum(r_out)
        scr = refs[p:p + n_scr]; p += n_scr
        sems = refs[p:]
        parts, pi, po = [], 0, 0
        for k, r in enumerate(rides):
            parts.append((rins[pi:pi + r_in[k]], routs[po:po + r_out[k]], sems[2 * k], sems[2 * k + 1]))
            pi += r_in[k]
            po += r_out[k]
        lin = 0
        for d in range(len(grid)):
            lin = lin * grid[d] + pl.program_id(d)
        if rides:
            @pl.when(lin == 0)
            def _():
                for r, part in zip(rides, parts):
                    r.start(*part)
        body(*ins, *outs, *scr)
        for r, part in zip(rides, parts):
            if r.mid is not None:
                @pl.when(lin == min(steps - 1, int(r.mid_frac * steps)))
                def _(r=r, part=part):
                    r.mid(*part)
        if rides:
            @pl.when(lin == steps - 1)
            def _():
                for r, part in zip(rides, parts):
                    r.finish(*part)

    scratch = list(scratch_shapes)
    for r in rides:
        scratch += [pltpu.SemaphoreType.DMA((r.n_sem,)), pltpu.SemaphoreType.DMA((r.n_sem,))]
    if rides:
        sem = ("arbitrary",) * len(grid)
    res = pl.pallas_call(
        wrapped, name=name, grid=grid,
        in_specs=list(in_specs) + [any_spec] * sum(r_in),
        out_specs=out_specs + [any_spec] * sum(r_out),
        out_shape=out_shape + [s for r in rides for s in r.out_shape],
        scratch_shapes=scratch, input_output_aliases=aliases,
        compiler_params=_params(sem),
    )

    def run(*args):
        got = res(*args, *[a for r in rides for a in r.args])
        mine = got[0] if single else list(got[:n_out])
        if not rides:
            return mine
        rest, out = list(got[n_out:]), []
        for k in range(len(rides)):
            out.append(rest[:r_out[k]])
            rest = rest[r_out[k]:]
        return mine, out

    return run


def _ride_gather(slot, s1=None, s2=None, s3=None, tail=None, direct=None, mid_frac=0.6):
    half = slot.shape[1] // 2

    def rows(part, c, which=None):
        k0, k1, n = part
        count, first = (k1 - k0) * (half // n), c * half + k0 * (half // n)
        return pl.ds(first, count) if which is None else pl.ds(first + which * (count // 2), count // 2)

    def ids():
        x, y, c, _ = _place()
        return x, y, c, 2 * x + y, 2 * (1 - x) + y, 2 * x + (1 - y), 2 * (1 - x) + (1 - y)

    def copy(full, chip, r, ss, rs, k, to):
        piece = full.at[chip, r, :]
        return _remote(piece, piece, ss.at[k], rs.at[k], to)

    def to_neighbours(full, ss, rs, part, base):
        x, y, c, me, _, _, _ = ids()
        return [copy(full, me, rows(part, c), ss, rs, base, (1 - x, y, c)),
                copy(full, me, rows(part, c), ss, rs, base + 1, (x, 1 - y, c))]

    def from_neighbours(full, ss, rs, part, base):
        x, y, c, _, cx, cy, _ = ids()
        return [copy(full, cx, rows(part, c), ss, rs, base, (x, y, c)), copy(full, cy, rows(part, c), ss, rs, base + 1, (x, y, c))]

    def onward(full, ss, rs, part, base):
        x, y, c, _, cx, cy, _ = ids()
        return [copy(full, cx, rows(part, c, 0), ss, rs, base, (x, 1 - y, c)),
                copy(full, cy, rows(part, c, 1), ss, rs, base + 1, (1 - x, y, c))]

    def from_onward(full, ss, rs, part, base):
        x, y, c, _, _, _, cd = ids()
        return [copy(full, cd, rows(part, c, 0), ss, rs, base, (x, y, c)), copy(full, cd, rows(part, c, 1), ss, rs, base + 1, (x, y, c))]

    def to_sibling(full, ss, rs, part, base, diagonal):
        x, y, c, _, cx, cy, cd = ids()
        return [copy(full, chip, rows(part, c), ss, rs, base + j, (x, y, 1 - c))
                for j, chip in enumerate([cd] if diagonal else [cx, cy])]

    def from_sibling(full, ss, rs, part, base, diagonal):
        x, y, c, _, cx, cy, cd = ids()
        return [copy(full, chip, rows(part, 1 - c), ss, rs, base + j, (x, y, c))
                for j, chip in enumerate([cd] if diagonal else [cx, cy])]

    def to_chips(full, ss, rs, part, base):
        x, y, c, me, _, _, _ = ids()
        chips = [(1 - x, y), (x, 1 - y), (1 - x, 1 - y)]
        return [copy(full, me, rows(part, c), ss, rs, base + j, (*chip, c)) for j, chip in enumerate(chips)]

    def all_three(full, ss, rs, part, base, half_of, to):
        x, y, c, _, cx, cy, cd = ids()
        return [copy(full, chip, rows(part, half_of(c)), ss, rs, base + j, to(x, y, c)) for j, chip in enumerate([cx, cy, cd])]

    def start(ins, outs, ss, rs):
        full, cps = outs[0], []
        if s1 is not None:
            cps += to_neighbours(full, ss, rs, s1, 0)
        for part, b_ici, b_sib in ((s2, 2, 4), (tail, 7, 9)):
            if part is not None:
                cps += onward(full, ss, rs, part, b_ici) + to_sibling(full, ss, rs, part, b_sib, False)
        if s3 is not None:
            cps += to_sibling(full, ss, rs, s3, 6, True)
        if direct is not None:
            cps += to_chips(full, ss, rs, direct, 12)
        for cp in cps:
            cp.start()

    def mid(ins, outs, ss, rs):
        if tail is not None:
            for cp in from_onward(outs[0], ss, rs, tail, 7):
                cp.wait_recv()
            for cp in to_sibling(outs[0], ss, rs, tail, 11, True):
                cp.start()
        if direct is not None:
            for cp in all_three(outs[0], ss, rs, direct, 12, lambda c: c, lambda x, y, c: (x, y, c)):
                cp.wait_recv()
            for cp in all_three(outs[0], ss, rs, direct, 15, lambda c: c, lambda x, y, c: (x, y, 1 - c)):
                cp.start()

    def finish(ins, outs, ss, rs):
        full, got, sent = outs[0], [], []
        if s1 is not None:
            got += from_neighbours(full, ss, rs, s1, 0)
            sent += to_neighbours(full, ss, rs, s1, 0)
        if s2 is not None:
            got += from_onward(full, ss, rs, s2, 2) + from_sibling(full, ss, rs, s2, 4, False)
            sent += onward(full, ss, rs, s2, 2) + to_sibling(full, ss, rs, s2, 4, False)
        if s3 is not None:
            got += from_sibling(full, ss, rs, s3, 6, True)
            sent += to_sibling(full, ss, rs, s3, 6, True)
        if tail is not None:
            got += from_sibling(full, ss, rs, tail, 9, False) + from_sibling(full, ss, rs, tail, 11, True)
            sent += onward(full, ss, rs, tail, 7) + to_sibling(full, ss, rs, tail, 9, False) + to_sibling(full, ss, rs, tail, 11, True)
        if direct is not None:
            got += all_three(full, ss, rs, direct, 15, lambda c: 1 - c, lambda x, y, c: (x, y, c))
            sent += to_chips(full, ss, rs, direct, 12)
            sent += all_three(full, ss, rs, direct, 15, lambda c: c, lambda x, y, c: (x, y, 1 - c))
        for cp in got:
            cp.wait_recv()
        for cp in sent:
            cp.wait_send()

    return _Ride([slot], [jax.ShapeDtypeStruct(slot.shape, slot.dtype)], 18, start, finish,
                 mid=mid if (tail is not None or direct is not None) else None, mid_frac=mid_frac, aliases={0: 0})


def _ride_scatter(q, land=None, part=(0, 1)):
    k0, k1, n = part if len(part) == 3 else (part[0], part[0] + 1, part[1])
    rows_n = q.shape[1] // n
    rows = pl.ds(k0 * rows_n, (k1 - k0) * rows_n)

    def copies(ins, outs, ss, rs):
        x, y, c, chips = _place()
        return [_remote(ins[0].at[2 * chip[0] + chip[1], rows, :], outs[0].at[j, rows, :], ss.at[j], rs.at[j], (*chip, c))
                for j, chip in enumerate(chips)]

    def start(*a):
        for cp in copies(*a):
            cp.start()

    def finish(*a):
        for cp in copies(*a):
            cp.wait()

    shape = jax.ShapeDtypeStruct((3,) + q.shape[1:], q.dtype)
    if land is None:
        return _Ride([q], [shape], 3, start, finish)
    return _Ride([q, land], [shape], 3, start, finish, aliases={1: 0})


def _ride_to_sibling(a, halves=False, first=False, shards=None, land=None):
    s0, s1 = shards or (0, a.shape[0])

    def copy(ins, outs, ss, rs):
        x, y, c, _ = _place()
        if halves:
            src, dst = ins[0].at[s0:s1, 1 - c], outs[0].at[s0:s1]
        else:
            src, dst = (ins[0].at[0] if first else ins[0]), outs[0]
        return _remote(src, dst, ss.at[0], rs.at[0], (x, y, 1 - c))

    shape = (a.shape[0],) + a.shape[2:] if halves else (a.shape[1:] if first else a.shape)
    return _Ride([a] if land is None else [a, land], [jax.ShapeDtypeStruct(shape, a.dtype)], 1,
                 lambda *a_: copy(*a_).start(), lambda *a_: copy(*a_).wait(), aliases=None if land is None else {1: 0})


def _ride_rows_to_sibling(a, hr, shards, total):
    def copies(ins, outs, ss, rs):
        x, y, c, _ = _place()
        return [_remote(ins[0].at[pl.ds((2 * s + 1 - c) * hr, hr), :], outs[0].at[s], ss.at[s], rs.at[s], (x, y, 1 - c))
                for s in range(shards)]

    def start(*a_):
        for cp in copies(*a_):
            cp.start()

    def finish(*a_):
        for cp in copies(*a_):
            cp.wait()

    return _Ride([a], [jax.ShapeDtypeStruct((total, hr, a.shape[1]), a.dtype)], shards, start, finish)


def _ride_swap(h):
    def copy(ins, outs, ss, rs):
        x, y, c, _ = _place()
        return _remote(ins[0], outs[0], ss.at[0], rs.at[0], (x, y, 1 - c))

    return _Ride([h], [jax.ShapeDtypeStruct(h.shape, h.dtype)], 1,
                 lambda *a: copy(*a).start(), lambda *a: copy(*a).wait())


def _mesh_place(p):
    return (p // 4, (p // 2) % 2, p % 2)


def _ride_small_to_all(packed):
    def copies(ins, outs, ss, rs):
        x, y, c, _ = _place()
        me = 4 * x + 2 * y + c
        return [_remote(ins[0], outs[0].at[me], ss.at[k - 1], rs.at[k - 1], _mesh_place((me + k) % N_DEV))
                for k in range(1, N_DEV)]

    def own(ins, outs, ss, rs):
        x, y, c, _ = _place()
        return pltpu.make_async_copy(ins[0], outs[0].at[4 * x + 2 * y + c], ss.at[N_DEV - 1])

    def start(*a):
        own(*a).start()
        for cp in copies(*a):
            cp.start()

    def finish(ins, outs, ss, rs):
        x, y, c, _ = _place()
        me = 4 * x + 2 * y + c
        for k in range(1, N_DEV):
            _remote(ins[0], outs[0].at[(me + N_DEV - k) % N_DEV], ss.at[k - 1], rs.at[k - 1], (x, y, c)).wait_recv()
        for cp in copies(ins, outs, ss, rs):
            cp.wait_send()
        own(ins, outs, ss, rs).wait()

    return _Ride([packed], [jax.ShapeDtypeStruct((N_DEV,) + packed.shape, packed.dtype)], N_DEV, start, finish)


def _carrier(rides, *, name):
    _, outs = _call(lambda: None, name=name, grid=(1,), in_specs=[], out_specs=[], out_shape=[], rides=rides)()
    return outs


def _norm_bf16(a_ref, g_ref):
    xf = a_ref[...]
    r = lax.rsqrt(jnp.mean(xf * xf, axis=-1, keepdims=True) + EPS)
    return ((xf * r) * g_ref[...]).astype(BF16)


def _norm_matmul_wide(a, g, b, *, tm, tn, name, rides=()):
    T, K = a.shape
    N = b.shape[0]

    def body(a_ref, g_ref, b_ref, n_ref, o_ref):
        n = _norm_bf16(a_ref, g_ref)
        n_ref[...] = n
        o_ref[...] = _dot_nt(n, b_ref[...])

    return _call(
        body, name=name, grid=(N // tn, T // tm),
        in_specs=[pl.BlockSpec((tm, K), lambda j, i: (i, 0)), pl.BlockSpec((1, K), lambda j, i: (0, 0)),
                  pl.BlockSpec((tn, K), lambda j, i: (j, 0))],
        out_specs=[pl.BlockSpec((None, tm, K), lambda j, i: (j, i, 0)), pl.BlockSpec((tm, tn), lambda j, i: (i, j))],
        out_shape=[jax.ShapeDtypeStruct((N // tn, T, K), BF16), jax.ShapeDtypeStruct((T, N), F32)],
        sem=("arbitrary", "arbitrary"), rides=rides,
    )(a, g, b)


def _norm_matmul_sq(a, g, b, *, tm, tn, name, rides=()):
    T, K = a.shape
    per = b.shape[2] // tn
    N = b.shape[0] * b.shape[2]

    def body(a_ref, g_ref, b_ref, nt_ref, o_ref, z_ref, zt_ref, n_scr):
        @pl.when(pl.program_id(1) == 0)
        def _():
            n = _norm_bf16(a_ref, g_ref)
            n_scr[...] = n
            nt_ref[...] = n.T
        r = jnp.maximum(_dot(n_scr[...], b_ref[...]), 0.0)
        o_ref[...] = r.astype(BF16)
        z = (r * r).astype(BF16)
        z_ref[...] = z
        zt_ref[...] = z.T

    return _call(
        body, name=name, grid=(T // tm, N // tn),
        in_specs=[pl.BlockSpec((tm, K), lambda i, j: (i, 0)), pl.BlockSpec((1, K), lambda i, j: (0, 0)),
                  pl.BlockSpec((None, K, tn), lambda i, j: (j // per, 0, j % per))],
        out_specs=[pl.BlockSpec((K, tm), lambda i, j: (0, i)), pl.BlockSpec((tm, tn), lambda i, j: (i, j)),
                   pl.BlockSpec((tm, tn), lambda i, j: (i, j)), pl.BlockSpec((tn, tm), lambda i, j: (j, i))],
        out_shape=[jax.ShapeDtypeStruct((K, T), BF16), jax.ShapeDtypeStruct((T, N), BF16),
                   jax.ShapeDtypeStruct((T, N), BF16), jax.ShapeDtypeStruct((N, T), BF16)],
        scratch_shapes=[pltpu.VMEM((tm, K), BF16)],
        sem=("parallel", "arbitrary"), rides=rides,
    )(a, g, b)


def _grad_pair(at, at_sib, b, b_sib, *, cols_sharded, tmo, tk, name, tn=None, rides=()):
    S, _, hr, T = at.shape
    C = b.shape[-1] // N_CHIPS if cols_sharded else b.shape[-1]
    tn = tn or C
    nj, nk = C // tn, T // tk
    a_sel = (lambda s: 0) if cols_sharded else (lambda s: s)
    b_col = (lambda s, j: s * nj + j) if cols_sharded else (lambda s, j: j)
    if b.ndim == 3:
        b_spec = pl.BlockSpec((None, tk, tn), lambda s, i, j, k: (0, k, b_col(s, j)))
    else:
        b_spec = pl.BlockSpec((tk, tn), lambda s, i, j, k: (k, b_col(s, j)))

    def body(a_ref, as_ref, b_ref, bs_ref, o_ref, ob_ref):
        k = pl.program_id(3)
        p = _dot(a_ref[...], b_ref[...]) + _dot(as_ref[...], bs_ref[...])

        @pl.when(k == 0)
        def _():
            o_ref[...] = p

        @pl.when(k > 0)
        def _():
            o_ref[...] += p

        @pl.when(k == nk - 1)
        def _():
            ob_ref[...] = o_ref[...].astype(BF16)

    out = pl.BlockSpec((None, tmo, tn), lambda s, i, j, k: (s, i, j))
    return _call(
        body, name=name, grid=(N_CHIPS, hr // tmo, nj, nk),
        in_specs=[pl.BlockSpec((None, None, tmo, tk), lambda s, i, j, k: (a_sel(s), lax.axis_index("c"), i, k)),
                  pl.BlockSpec((None, tmo, tk), lambda s, i, j, k: (a_sel(s), i, k)),
                  b_spec, pl.BlockSpec((tk, tn), lambda s, i, j, k: (k, b_col(s, j)))],
        out_specs=[out, out],
        out_shape=[jax.ShapeDtypeStruct((N_CHIPS, hr, C), F32), jax.ShapeDtypeStruct((N_CHIPS, hr, C), BF16)],
        sem=("parallel", "parallel", "parallel", "arbitrary"), rides=rides,
    )(at, at_sib, b, b_sib)


def _grad_pair_merged(at, at_sib, b, b_sib, *, tk, name, rides=()):
    S, _, hr, T = at.shape
    C = b.shape[-1]
    nk = T // tk

    def body(a_ref, as_ref, b_ref, bs_ref, o_ref, ob_ref):
        k = pl.program_id(0)
        p = (_dot(a_ref[...].reshape(S * hr, tk), b_ref[...])
             + _dot(as_ref[...].reshape(S * hr, tk), bs_ref[...])).reshape(S, hr, C)

        @pl.when(k == 0)
        def _():
            o_ref[...] = p

        @pl.when(k > 0)
        def _():
            o_ref[...] += p

        @pl.when(k == nk - 1)
        def _():
            ob_ref[...] = o_ref[...].astype(BF16)

    out = pl.BlockSpec((S, hr, C), lambda k: (0, 0, 0))
    return _call(
        body, name=name, grid=(nk,),
        in_specs=[pl.BlockSpec((S, None, hr, tk), lambda k: (0, lax.axis_index("c"), 0, k)),
                  pl.BlockSpec((S, hr, tk), lambda k: (0, 0, k)),
                  pl.BlockSpec((tk, C), lambda k: (k, 0)), pl.BlockSpec((tk, C), lambda k: (k, 0))],
        out_specs=[out, out],
        out_shape=[jax.ShapeDtypeStruct((S, hr, C), F32), jax.ShapeDtypeStruct((S, hr, C), BF16)],
        sem=("arbitrary",), rides=rides,
    )(at, at_sib, b, b_sib)


def _matmul_parts(parts, b, *, tm, tn, name, rides=()):
    T = parts[0].shape[0]
    N = b.shape[1]
    offs = [sum(p.shape[1] for p in parts[:i]) for i in range(len(parts))]
    assert all(o % p.shape[1] == 0 for o, p in zip(offs, parts))

    def body(*refs):
        n = len(parts)
        acc = _dot(refs[0][...], refs[n][...])
        for i in range(1, n):
            acc = acc + _dot(refs[i][...], refs[n + i][...])
        refs[-1][...] = acc

    a_specs = [pl.BlockSpec((tm, p.shape[1]), lambda i, j: (i, 0)) for p in parts]
    b_specs = [pl.BlockSpec((p.shape[1], tn), lambda i, j, r=o // p.shape[1]: (r, j)) for o, p in zip(offs, parts)]
    return _call(
        body, name=name, grid=(T // tm, N // tn), in_specs=a_specs + b_specs,
        out_specs=pl.BlockSpec((tm, tn), lambda i, j: (i, j)), out_shape=jax.ShapeDtypeStruct((T, N), F32),
        sem=("parallel", "parallel"), rides=rides,
    )(*parts, *([b] * len(parts)))


def _to_bf16(v):
    return v.astype(BF16)


def _matmul_res(a, b, res, *, tm, tn, tk, prologue, name, rides=()):
    T, K = a.shape
    N = b.shape[1]

    def body(a_ref, b_ref, res_ref, o_ref):
        k = pl.program_id(2)
        p = _dot(prologue(a_ref[...]), b_ref[...])

        @pl.when(k == 0)
        def _():
            o_ref[...] = res_ref[...] + p

        @pl.when(k > 0)
        def _():
            o_ref[...] += p

    return _call(
        body, name=name, grid=(T // tm, N // tn, K // tk),
        in_specs=[pl.BlockSpec((tm, tk), lambda i, j, k: (i, k)), pl.BlockSpec((tk, tn), lambda i, j, k: (k, j)),
                  pl.BlockSpec((tm, tn), lambda i, j, k: (i, j))],
        out_specs=pl.BlockSpec((tm, tn), lambda i, j, k: (i, j)),
        out_shape=jax.ShapeDtypeStruct((T, N), F32),
        sem=("parallel", "parallel", "arbitrary"), rides=rides,
    )(a, b, res)


def _matmul_nt(a, b, *, tm, tn, tk, name, extra=None, epilogue=None, out_dtype=F32, rides=()):
    T, K = a.shape
    two = b.ndim == 3 and tk == 2 * b.shape[2]
    if two:
        N, ks = b.shape[1], b.shape[2]
        b_specs = [pl.BlockSpec((None, tn, ks), lambda i, j, k: (2 * k, j, 0)),
                   pl.BlockSpec((None, tn, ks), lambda i, j, k: (2 * k + 1, j, 0))]
    elif b.ndim == 3:
        per = b.shape[2] // tk
        N = b.shape[1]
        b_specs = [pl.BlockSpec((None, tn, tk), lambda i, j, k: (k // per, j, k % per))]
    else:
        N = b.shape[0]
        b_specs = [pl.BlockSpec((tn, tk), lambda i, j, k: (j, k))]
    nb = len(b_specs)
    nk = K // tk
    assert out_dtype == F32 or nk == 1
    in_specs = [pl.BlockSpec((tm, tk), lambda i, j, k: (i, k))] + b_specs
    args = [a] + [b] * nb
    if extra is not None:
        in_specs.append(pl.BlockSpec((tm, tn), lambda i, j, k: (i, j)))
        args.append(extra)

    def body(*refs):
        a_ref, b_ref = refs[0], refs[1]
        o_ref = refs[-1]
        if two:
            p = (_dot_nt(a_ref[:, :tk // 2].astype(BF16), refs[1][...])
                 + _dot_nt(a_ref[:, tk // 2:].astype(BF16), refs[2][...]))
        else:
            p = _dot_nt(a_ref[...].astype(BF16), b_ref[...])
        if nk == 1:
            if epilogue is not None:
                p = epilogue(p, refs[1 + nb][...])
            o_ref[...] = p.astype(out_dtype)
        else:
            k = pl.program_id(2)

            @pl.when(k == 0)
            def _():
                o_ref[...] = p

            @pl.when(k > 0)
            def _():
                o_ref[...] += p

    return _call(
        body, name=name, grid=(T // tm, N // tn, nk),
        in_specs=in_specs,
        out_specs=pl.BlockSpec((tm, tn), lambda i, j, k: (i, j)),
        out_shape=jax.ShapeDtypeStruct((T, N), out_dtype),
        sem=("parallel", "parallel", "arbitrary"), rides=rides,
    )(*args)


def _loss_bwd(h2, tgt, g, *, tm):
    T, D = h2.shape

    def body(h_ref, t_ref, g_ref, dh_ref, dhb_ref, dg_ref, loss_ref):
        @pl.when(pl.program_id(0) == 0)
        def _():
            dg_ref[...] = jnp.zeros_like(dg_ref)
            loss_ref[...] = jnp.zeros_like(loss_ref)
        h = h_ref[...]
        gg = g_ref[...]
        r = lax.rsqrt(jnp.mean(h * h, axis=-1, keepdims=True) + EPS)
        hn = h * r
        err = hn * gg - t_ref[...]
        loss_ref[...] += 0.5 * jnp.sum(jnp.mean(err * err, axis=-1, keepdims=True), axis=0, keepdims=True)
        dy = err * (1.0 / D)
        dg_ref[...] += jnp.sum(dy * hn, axis=0, keepdims=True)
        w = dy * gg
        dh = r * w - h * ((r * r * r) * jnp.mean(w * h, axis=-1, keepdims=True))
        dh_ref[...] = dh
        dhb_ref[...] = dh.astype(BF16)

    tile = pl.BlockSpec((tm, D), lambda i: (i, 0))
    return pl.pallas_call(
        body, name="loss_bwd", grid=(T // tm,),
        in_specs=[tile, tile, pl.BlockSpec((1, D), lambda i: (0, 0))],
        out_specs=[tile, tile, pl.BlockSpec((1, D), lambda i: (0, 0)), pl.BlockSpec((1, 1), lambda i: (0, 0))],
        out_shape=[jax.ShapeDtypeStruct((T, D), F32), jax.ShapeDtypeStruct((T, D), BF16),
                   jax.ShapeDtypeStruct((1, D), F32), jax.ShapeDtypeStruct((1, 1), F32)],
        compiler_params=_params(("arbitrary",)),
    )(h2, tgt, g)


def _rms_bwd_res(dn, h, g, dres, *, tm, name, bf16_copy=True, rides=()):
    T, D = h.shape

    def body(dn_ref, h_ref, g_ref, dres_ref, dh_ref, *rest):
        dg_ref = rest[-1]

        @pl.when(pl.program_id(0) == 0)
        def _():
            dg_ref[...] = jnp.zeros_like(dg_ref)
        h_ = h_ref[...]
        dn_ = dn_ref[...]
        dh, r = _rms_bwd(dn_, h_, g_ref[...])
        dg_ref[...] += jnp.sum(dn_ * (h_ * r), axis=0, keepdims=True)
        dh = dres_ref[...] + dh
        dh_ref[...] = dh
        if bf16_copy:
            rest[0][...] = dh.astype(BF16)

    tile = pl.BlockSpec((tm, D), lambda i: (i, 0))
    row = pl.BlockSpec((1, D), lambda i: (0, 0))
    copy_spec = [tile] if bf16_copy else []
    copy_shape = [jax.ShapeDtypeStruct((T, D), BF16)] if bf16_copy else []
    return _call(
        body, name=name, grid=(T // tm,),
        in_specs=[tile, tile, row, tile], out_specs=[tile] + copy_spec + [row],
        out_shape=[jax.ShapeDtypeStruct((T, D), F32)] + copy_shape + [jax.ShapeDtypeStruct((1, D), F32)],
        sem=("arbitrary",), rides=rides,
    )(dn, h, g, dres)


def _rel_distance():
    i = lax.broadcasted_iota(jnp.int32, (CHUNK, 2 * CHUNK), 0)
    j = lax.broadcasted_iota(jnp.int32, (CHUNK, 2 * CHUNK), 1)
    return i + CHUNK - j


def _bias_build(table):
    def body(tab_ref, o_ref):
        rel = _rel_distance()
        ge = [rel >= t for t in BUCKET_THR]
        for h in range(B_HEADS):
            cur = jnp.full((CHUNK, 2 * CHUNK), tab_ref[0, h], F32)
            for b in range(1, N_BUCKETS):
                cur = jnp.where(ge[b - 1], tab_ref[b, h], cur)
            o_ref[h] = cur

    return pl.pallas_call(
        body, name="bias_build",
        in_specs=[pl.BlockSpec(memory_space=pltpu.SMEM)],
        out_specs=pl.BlockSpec(memory_space=pltpu.VMEM),
        out_shape=jax.ShapeDtypeStruct((B_HEADS, CHUNK, 2 * CHUNK), F32),
    )(table)


def _bias_grad(dbias):
    def body(db_ref, o_ref, acc_ref):
        rel = _rel_distance()
        lo = [0] + BUCKET_THR
        hi = BUCKET_THR + [CHUNK]
        for b in range(N_BUCKETS):
            m = (rel >= lo[b]) & (rel < hi[b])
            for h in range(B_HEADS):
                row = b * B_HEADS + h
                acc_ref[row:row + 1, :] = jnp.sum(jnp.where(m, db_ref[h], 0.0), axis=0, keepdims=True)
        o_ref[...] = jnp.sum(acc_ref[...], axis=1, keepdims=True)

    return pl.pallas_call(
        body, name="bias_grad",
        in_specs=[pl.BlockSpec(memory_space=pltpu.VMEM)],
        out_specs=pl.BlockSpec(memory_space=pltpu.VMEM),
        out_shape=jax.ShapeDtypeStruct((N_BUCKETS * B_HEADS, 1), F32),
        scratch_shapes=[pltpu.VMEM((N_BUCKETS * B_HEADS, 2 * CHUNK), F32)],
    )(dbias)


def _causal_mask():
    t = lax.broadcasted_iota(jnp.int32, (CHUNK, CHUNK), 0)
    s = lax.broadcasted_iota(jnp.int32, (CHUNK, CHUNK), 1)
    return s <= t


def _band_mask(n):
    rel = _rel_distance()
    j = lax.broadcasted_iota(jnp.int32, (CHUNK, 2 * CHUNK), 1)
    return (rel >= 0) & (rel < CHUNK) & ((n > 0) | (j >= CHUNK))


def _gate_forward(u, v, lg, lb, wc, bs):
    ug = _gelu(u)
    vg = _gelu(v)
    mu = jnp.mean(vg, axis=-1, keepdims=True)
    xc = vg - mu
    rstd = lax.rsqrt(jnp.mean(xc * xc, axis=-1, keepdims=True) + EPS)
    xhat = xc * rstd
    vl = (xhat * lg + lb).astype(BF16)
    mixed = _dot(wc, vl) + bs
    return ug, xhat, rstd, vl, mixed


def _softmax_scores(qk, bias, mask, sink):
    s = qk * SCALE + bias
    s = jnp.where(mask, s, NEG)
    m = jnp.maximum(jnp.max(s, axis=-1, keepdims=True), sink)
    p = jnp.exp(s - m)
    e_sink = jnp.exp(sink - m)
    inv = 1.0 / (jnp.sum(p, axis=-1, keepdims=True) + e_sink)
    return p * inv, e_sink * inv


PAIRS = Q_PER_KV // 2


def _head(g, pr, e):
    return g * Q_PER_KV + 2 * pr + e


def _stack_pairs(ref, g, col0=0):
    w = 2 * HEAD_DIM
    return jnp.concatenate([ref[:, col0 + (g * PAIRS + pr) * w:col0 + (g * PAIRS + pr + 1) * w] for pr in range(PAIRS)],
                           axis=0)


def _low_lanes():
    return lax.broadcasted_iota(jnp.int32, (2 * CHUNK, 2 * HEAD_DIM), 1) < HEAD_DIM


def _band_operands(kv_prev, kv_cur):
    band = jnp.concatenate([kv_prev, kv_cur], axis=0)
    low = _low_lanes()
    ops = []
    for cat in (band[:, :KV_WIDTH], band[:, KV_WIDTH:]):
        rol = pltpu.roll(cat, HEAD_DIM, 1)
        ops.append([[jnp.where(low if e == 0 else ~low, cat if g == e else rol, 0.0).astype(BF16) for e in range(2)]
                    for g in range(2)])
    return ops


def _mixer_fwd(proj, lg, lb, wsp, bs_col, sinks, bias, ga, gb, rides=()):
    T = proj.shape[0]
    nb = T // CHUNK

    def body(u_ref, v_ref, q_ref, kvc_ref, kvp_ref, lg_ref, lb_ref, w_ref, bs_ref, sink_ref, bias_ref,
             ga_ref, gb_ref, mixed_ref, mixed_t_ref, ab_ref):
        n = pl.program_id(0)
        causal = _causal_mask()
        ssq = jnp.zeros((CHUNK, 1), F32)
        for g in range(A_GROUPS):
            cols = slice(g * CHUNK, (g + 1) * CHUNK)
            wc = jnp.where(causal, w_ref[g], 0.0).astype(BF16)
            ug, _, _, _, mixed = _gate_forward(u_ref[:, cols], v_ref[:, cols], lg_ref[g:g + 1, :], lb_ref[g:g + 1, :],
                                               wc, bs_ref[g])
            a = ug * mixed
            ab_ref[:, cols] = a
            ssq = ssq + jnp.sum(a * a, axis=-1, keepdims=True)
        ra = lax.rsqrt(ssq * (1.0 / A_WIDTH) + EPS)
        mixed_ref[:, :A_WIDTH] = ((ab_ref[:, :A_WIDTH] * ra) * ga_ref[...]).astype(BF16)

        mask = _band_mask(n)
        kops, vops = _band_operands(kvp_ref[...], kvc_ref[...])
        ssq = jnp.zeros((CHUNK, 1), F32)
        for g in range(B_HEADS // Q_PER_KV):
            qst = _stack_pairs(q_ref, g).astype(BF16)
            o_st = jnp.zeros((PAIRS * CHUNK, 2 * HEAD_DIM), F32)
            for e in range(2):
                s_all = _dot_nt(qst, kops[g][e])
                ps = []
                for pr in range(PAIRS):
                    h = _head(g, pr, e)
                    p, _ = _softmax_scores(s_all[pr * CHUNK:(pr + 1) * CHUNK], bias_ref[h], mask, sink_ref[0, h])
                    ps.append(p.astype(BF16))
                o_st = o_st + _dot(jnp.concatenate(ps, axis=0), vops[g][e])
            for pr in range(PAIRS):
                o = o_st[pr * CHUNK:(pr + 1) * CHUNK]
                c0 = A_WIDTH + (g * PAIRS + pr) * 2 * HEAD_DIM
                ab_ref[:, c0:c0 + 2 * HEAD_DIM] = o
                ssq = ssq + jnp.sum(o * o, axis=-1, keepdims=True)
        rb = lax.rsqrt(ssq * (1.0 / B_WIDTH) + EPS)
        mixed_ref[:, A_WIDTH:] = ((ab_ref[:, A_WIDTH:] * rb) * gb_ref[...]).astype(BF16)
        mixed_t_ref[...] = mixed_ref[...].T

    full = lambda *shape: pl.BlockSpec(shape, lambda n: (0,) * len(shape))
    return _call(
        body, name="mixer_fwd", grid=(nb,),
        in_specs=[pl.BlockSpec((CHUNK, A_WIDTH), lambda n: (n, 0)),
                  pl.BlockSpec((CHUNK, A_WIDTH), lambda n: (n, 1)),
                  pl.BlockSpec((CHUNK, B_WIDTH), lambda n: (n, 2)),
                  pl.BlockSpec((CHUNK, 2 * KV_WIDTH), lambda n: (n, 12)),
                  pl.BlockSpec((CHUNK, 2 * KV_WIDTH), lambda n: (jnp.maximum(n - 1, 0), 12)),
                  full(A_GROUPS, CHUNK), full(A_GROUPS, CHUNK), full(A_GROUPS, CHUNK, CHUNK), full(A_GROUPS, CHUNK, 1),
                  pl.BlockSpec(memory_space=pltpu.SMEM), full(B_HEADS, CHUNK, 2 * CHUNK),
                  full(1, A_WIDTH), full(1, B_WIDTH)],
        out_specs=[pl.BlockSpec((CHUNK, D_MODEL), lambda n: (n, 0)), pl.BlockSpec((D_MODEL, CHUNK), lambda n: (0, n)),
                   pl.BlockSpec((CHUNK, D_MODEL), lambda n: (n, 0))],
        out_shape=[jax.ShapeDtypeStruct((T, D_MODEL), BF16), jax.ShapeDtypeStruct((D_MODEL, T), BF16),
                   jax.ShapeDtypeStruct((T, D_MODEL), F32)],
        sem=("parallel",), rides=rides,
    )(proj, proj, proj, proj, proj, lg, lb, wsp, bs_col, sinks, bias, ga, gb)


def _gmlp_bwd(proj, ab, dmixed, ga, lg, lb, wsp, bs_col, rides=()):
    T = proj.shape[0]
    nb = T // CHUNK

    def body(u_ref, v_ref, a_ref, dna_ref, ga_ref, lg_ref, lb_ref, w_ref, bs_ref,
             dp_ref, dpt_ref, dga_ref, dw_ref, dbs_ref, dlg_ref, dlb_ref):
        @pl.when(pl.program_id(0) == 0)
        def _():
            for r in (dga_ref, dw_ref, dbs_ref, dlg_ref, dlb_ref):
                r[...] = jnp.zeros_like(r)
        causal = _causal_mask()
        a_all = a_ref[...]
        dna = dna_ref[...]
        da_all, ra = _rms_bwd(dna, a_all, ga_ref[...])
        dga_ref[...] += jnp.sum(dna * (a_all * ra), axis=0, keepdims=True)
        for g in range(A_GROUPS):
            cols = slice(g * CHUNK, (g + 1) * CHUNK)
            wc = jnp.where(causal, w_ref[g], 0.0).astype(BF16)
            lgg = lg_ref[g:g + 1, :]
            u = u_ref[:, cols]
            v = v_ref[:, cols]
            ug, xhat, rstd, vl, mixed = _gate_forward(u, v, lgg, lb_ref[g:g + 1, :], wc, bs_ref[g])
            da = da_all[:, cols]
            dug = da * mixed
            dmg = da * ug
            dmg_b = dmg.astype(BF16)
            dbs_ref[g] += jnp.sum(dmg, axis=-1, keepdims=True)
            dw_ref[g] += jnp.where(causal, _dot_nt(dmg_b, vl), 0.0)
            dvl = _dot_tn(wc, dmg_b)
            dlg_ref[g:g + 1, :] += jnp.sum(dvl * xhat, axis=0, keepdims=True)
            dlb_ref[g:g + 1, :] += jnp.sum(dvl, axis=0, keepdims=True)
            dxh = dvl * lgg
            dvg = rstd * (dxh - jnp.mean(dxh, axis=-1, keepdims=True)
                          - xhat * jnp.mean(dxh * xhat, axis=-1, keepdims=True))
            _, gu = _gelu_and_grad(u)
            _, gv = _gelu_and_grad(v)
            dp_ref[:, cols] = (dug * gu).astype(BF16)
            dp_ref[:, A_WIDTH + g * CHUNK:A_WIDTH + (g + 1) * CHUNK] = (dvg * gv).astype(BF16)
        dpt_ref[...] = dp_ref[...].T

    full = lambda *shape: pl.BlockSpec(shape, lambda n: (0,) * len(shape))
    return _call(
        body, name="gmlp_bwd", grid=(nb,),
        in_specs=[pl.BlockSpec((CHUNK, A_WIDTH), lambda n: (n, 0)),
                  pl.BlockSpec((CHUNK, A_WIDTH), lambda n: (n, 1)),
                  pl.BlockSpec((CHUNK, A_WIDTH), lambda n: (n, 0)),
                  pl.BlockSpec((CHUNK, A_WIDTH), lambda n: (n, 0)),
                  full(1, A_WIDTH), full(A_GROUPS, CHUNK), full(A_GROUPS, CHUNK), full(A_GROUPS, CHUNK, CHUNK),
                  full(A_GROUPS, CHUNK, 1)],
        out_specs=[pl.BlockSpec((CHUNK, 2 * A_WIDTH), lambda n: (n, 0)), pl.BlockSpec((2 * A_WIDTH, CHUNK), lambda n: (0, n)),
                   full(1, A_WIDTH), full(A_GROUPS, CHUNK, CHUNK), full(A_GROUPS, CHUNK, 1),
                   full(A_GROUPS, CHUNK), full(A_GROUPS, CHUNK)],
        out_shape=[jax.ShapeDtypeStruct((T, 2 * A_WIDTH), BF16), jax.ShapeDtypeStruct((2 * A_WIDTH, T), BF16),
                   jax.ShapeDtypeStruct((1, A_WIDTH), F32), jax.ShapeDtypeStruct((A_GROUPS, CHUNK, CHUNK), F32),
                   jax.ShapeDtypeStruct((A_GROUPS, CHUNK, 1), F32), jax.ShapeDtypeStruct((A_GROUPS, CHUNK), F32),
                   jax.ShapeDtypeStruct((A_GROUPS, CHUNK), F32)],
        sem=("arbitrary",), rides=rides,
    )(proj, proj, ab, dmixed, ga, lg, lb, wsp, bs_col)


def _attn_bwd(proj, ab, dmixed, gb, sinks, bias, rides=()):
    T = proj.shape[0]
    nb = T // CHUNK
    qn = lambda n: jnp.minimum(n, nb - 1)

    def body(q_ref, kvc_ref, kvp_ref, o_ref, dnb_ref, gb_ref, sink_ref, bias_ref,
             dq_ref, dkv_ref, dqt_ref, dkvt_ref, dgb_ref, dsink_ref, dbias_ref, carry_ref, sacc_ref):
        n = pl.program_id(0)

        @pl.when(n == 0)
        def _():
            carry_ref[...] = jnp.zeros_like(carry_ref)
            sacc_ref[...] = jnp.zeros_like(sacc_ref)
            dgb_ref[...] = jnp.zeros_like(dgb_ref)
            dbias_ref[...] = jnp.zeros_like(dbias_ref)

        @pl.when(n < nb)
        def _():
            mask = _band_mask(n)
            o_all = o_ref[...]
            dnb = dnb_ref[...]
            do_all, rb = _rms_bwd(dnb, o_all, gb_ref[...])
            dgb_ref[...] += jnp.sum(dnb * (o_all * rb), axis=0, keepdims=True)
            kops, vops = _band_operands(kvp_ref[...], kvc_ref[...])
            low = _low_lanes()
            halves = []
            for g in range(B_HEADS // Q_PER_KV):
                qst = _stack_pairs(q_ref, g).astype(BF16)
                dost = _stack_pairs(do_all, g).astype(BF16)
                dq_st = jnp.zeros((PAIRS * CHUNK, 2 * HEAD_DIM), F32)
                dk_e, dv_e = [], []
                for e in range(2):
                    s_all = _dot_nt(qst, kops[g][e])
                    dp_all = _dot_nt(dost, vops[g][e])
                    ps, dsrs = [], []
                    for pr in range(PAIRS):
                        h = _head(g, pr, e)
                        rows = slice(pr * CHUNK, (pr + 1) * CHUNK)
                        p, p_sink = _softmax_scores(s_all[rows], bias_ref[h], mask, sink_ref[0, h])
                        dp = dp_all[rows]
                        delta = jnp.sum(p * dp, axis=-1, keepdims=True)
                        ds = p * (dp - delta)
                        sacc_ref[:, h:h + 1] += -(p_sink * delta)
                        dbias_ref[h] += ds
                        ps.append(p.astype(BF16))
                        dsrs.append((ds * SCALE).astype(BF16))
                    dsr_all = jnp.concatenate(dsrs, axis=0)
                    dq_st = dq_st + _dot(dsr_all, kops[g][e])
                    dk_e.append(_dot_tn(dsr_all, qst))
                    dv_e.append(_dot_tn(jnp.concatenate(ps, axis=0), dost))
                for pr in range(PAIRS):
                    c0 = (g * PAIRS + pr) * 2 * HEAD_DIM
                    dq_ref[:, c0:c0 + 2 * HEAD_DIM] = dq_st[pr * CHUNK:(pr + 1) * CHUNK].astype(BF16)
                halves.append((dk_e, dv_e))
            tiles = []
            for t in range(2):
                g0, g1 = halves[0][t], halves[1][t]
                tiles.append(jnp.where(low, g0[0] + pltpu.roll(g0[1], HEAD_DIM, 1), pltpu.roll(g1[0], HEAD_DIM, 1) + g1[1]))
            dband = jnp.concatenate(tiles, axis=1)
            dkv = (carry_ref[...] + dband[:CHUNK]).astype(BF16)
            dkv_ref[...] = dkv
            dkvt_ref[...] = dkv.T
            dqt_ref[...] = dq_ref[...].T
            carry_ref[...] = dband[CHUNK:]

        @pl.when(n == nb)
        def _():
            dkv = carry_ref[...].astype(BF16)
            dkv_ref[...] = dkv
            dkvt_ref[...] = dkv.T
            dsink_ref[...] = jnp.sum(sacc_ref[...], axis=0, keepdims=True)

    full = lambda *shape: pl.BlockSpec(shape, lambda n: (0,) * len(shape))
    return _call(
        body, name="attn_bwd", grid=(nb + 1,),
        in_specs=[pl.BlockSpec((CHUNK, B_WIDTH), lambda n: (qn(n), 2)),
                  pl.BlockSpec((CHUNK, 2 * KV_WIDTH), lambda n: (qn(n), 12)),
                  pl.BlockSpec((CHUNK, 2 * KV_WIDTH), lambda n: (jnp.maximum(qn(n) - 1, 0), 12)),
                  pl.BlockSpec((CHUNK, B_WIDTH), lambda n: (qn(n), 1)),
                  pl.BlockSpec((CHUNK, B_WIDTH), lambda n: (qn(n), 1)),
                  full(1, B_WIDTH), pl.BlockSpec(memory_space=pltpu.SMEM), full(B_HEADS, CHUNK, 2 * CHUNK)],
        out_specs=[pl.BlockSpec((CHUNK, B_WIDTH), lambda n: (qn(n), 0)),
                   pl.BlockSpec((CHUNK, 2 * KV_WIDTH), lambda n: (jnp.maximum(n - 1, 0), 0)),
                   pl.BlockSpec((B_WIDTH, CHUNK), lambda n: (0, qn(n))),
                   pl.BlockSpec((2 * KV_WIDTH, CHUNK), lambda n: (0, jnp.maximum(n - 1, 0))),
                   full(1, B_WIDTH), full(1, B_HEADS), full(B_HEADS, CHUNK, 2 * CHUNK)],
        out_shape=[jax.ShapeDtypeStruct((T, B_WIDTH), BF16), jax.ShapeDtypeStruct((T, 2 * KV_WIDTH), BF16),
                   jax.ShapeDtypeStruct((B_WIDTH, T), BF16), jax.ShapeDtypeStruct((2 * KV_WIDTH, T), BF16),
                   jax.ShapeDtypeStruct((1, B_WIDTH), F32), jax.ShapeDtypeStruct((1, B_HEADS), F32),
                   jax.ShapeDtypeStruct((B_HEADS, CHUNK, 2 * CHUNK), F32)],
        scratch_shapes=[pltpu.VMEM((CHUNK, 2 * KV_WIDTH), F32), pltpu.VMEM((CHUNK, B_HEADS), F32)],
        sem=("arbitrary",), rides=rides,
    )(proj, proj, proj, ab, dmixed, gb, sinks, bias)


def _sq_relu_grad(acc, r):
    return acc * (2.0 * r.astype(F32))


def _chip_index():
    return (2 * lax.axis_index("x") + lax.axis_index("y")).astype(jnp.int32).reshape(1)


def _cast_into_slot(w, *, tm, name):
    _, R, C = w.shape

    def body(me_ref, w_ref, o_ref):
        del me_ref
        o_ref[...] = w_ref[...].astype(BF16)

    return pl.pallas_call(
        body, name=name,
        grid_spec=pltpu.PrefetchScalarGridSpec(
            num_scalar_prefetch=1, grid=(R // tm,),
            in_specs=[pl.BlockSpec((None, tm, C), lambda i, me: (0, i, 0))],
            out_specs=pl.BlockSpec((None, tm, C), lambda i, me: (me[0], i, 0))),
        out_shape=jax.ShapeDtypeStruct((N_CHIPS, R, C), BF16), compiler_params=_params(("parallel",)),
    )(_chip_index(), w)


def _cast_into_slot_carrying(w, *, tm, name, rides):
    _, R, C = w.shape

    def body(w_ref, o_ref):
        o_ref[...] = w_ref[...].astype(BF16)

    return _call(
        body, name=name, grid=(R // tm,),
        in_specs=[pl.BlockSpec((None, tm, C), lambda i: (0, i, 0))],
        out_specs=pl.BlockSpec((None, tm, C), lambda i: (2 * lax.axis_index("x") + lax.axis_index("y"), i, 0)),
        out_shape=jax.ShapeDtypeStruct((N_CHIPS, R, C), BF16), sem=("arbitrary",), rides=rides,
    )(w)


def _owner_total(gh, others, *, tm, name):
    _, hr, C = gh.shape

    def body(me_ref, g_ref, o_ref_in, out_ref):
        del me_ref
        acc = g_ref[...]
        for j in range(3):
            acc = acc + o_ref_in[j].astype(F32)
        out_ref[...] = acc

    return pl.pallas_call(
        body, name=name,
        grid_spec=pltpu.PrefetchScalarGridSpec(
            num_scalar_prefetch=1, grid=(hr // tm,),
            in_specs=[pl.BlockSpec((None, tm, C), lambda i, me: (me[0], i, 0)),
                      pl.BlockSpec((3, tm, C), lambda i, me: (0, i, 0))],
            out_specs=pl.BlockSpec((tm, C), lambda i, me: (i, 0))),
        out_shape=jax.ShapeDtypeStruct((hr, C), F32),
        compiler_params=_params(("parallel",)),
    )(_chip_index(), gh, others)


def _adamw_math(w, g, m, v):
    m = ADAM_B1 * m + (1.0 - ADAM_B1) * g
    v = ADAM_B2 * v + (1.0 - ADAM_B2) * (g * g)
    m_hat = m / (1.0 - ADAM_B1 ** ADAM_STEP)
    v_hat = v / (1.0 - ADAM_B2 ** ADAM_STEP)
    delta = -ADAM_LR * (m_hat / (jnp.sqrt(v_hat) + ADAM_EPS) + ADAM_WD * w)
    return delta, m, v


def _adamw_halves(w, own, got, m, v, *, tm, name, rides=()):
    _, R, C = w.shape
    nt = (R // 2) // tm

    def body(w_ref, own_ref, got_ref, m_ref, v_ref, g_ref, d_ref, nm_ref, nv_ref):
        g = jnp.where(pl.program_id(0) == lax.axis_index("c"), own_ref[...], got_ref[...])
        g_ref[...] = g
        d_ref[...], nm_ref[...], nv_ref[...] = _adamw_math(w_ref[...], g, m_ref[...], v_ref[...])

    whole = pl.BlockSpec((None, tm, C), lambda h, i: (0, h * nt + i, 0))
    half = pl.BlockSpec((tm, C), lambda h, i: (i, 0))
    return _call(
        body, name=name, grid=(2, nt), in_specs=[whole, half, half, whole, whole], out_specs=[whole] * 4,
        out_shape=[jax.ShapeDtypeStruct((1, R, C), F32)] * 4, sem=("parallel", "parallel"), rides=rides,
    )(w, own, got, m, v)


def _adamw_small(w, slots, m, v, *, name):
    def body(w_ref, slots_ref, m_ref, v_ref, g_ref, d_ref, nm_ref, nv_ref):
        g = slots_ref[0]
        for d in range(1, N_DEV):
            g = g + slots_ref[d]
        g_ref[...] = g
        d_ref[...], nm_ref[...], nv_ref[...] = _adamw_math(w_ref[...], g, m_ref[...], v_ref[...])

    vmem = pl.BlockSpec(memory_space=pltpu.VMEM)
    return pl.pallas_call(
        body, name=name, in_specs=[vmem] * 4, out_specs=[vmem] * 4,
        out_shape=[jax.ShapeDtypeStruct(w.shape, F32)] * 4, compiler_params=_params(),
    )(w, slots, m, v)


SMALL = ["rel_bias_table", "mix_norm_g", "gate_norm_g", "gate_norm_b", "w_spatial", "b_spatial", "attn_sinks",
         "out_norm_a_g", "out_norm_b_g", "ffn_norm_g", "final_norm_g"]
SMALL_A = ["gate_norm_g", "gate_norm_b", "w_spatial", "b_spatial", "out_norm_a_g"]
SMALL_B = ["rel_bias_table", "mix_norm_g", "attn_sinks", "out_norm_b_g", "ffn_norm_g", "final_norm_g"]
LARGE = ["w_in", "w_out", "w_up", "w_down"]
ROW_TILE = {"w_in": 208, "w_out": 256, "w_up": 256, "w_down": 256}
WEIGHTS = ["rel_bias_table", "mix_norm_g", "w_in", "gate_norm_g", "gate_norm_b", "w_spatial", "b_spatial", "attn_sinks",
           "out_norm_a_g", "out_norm_b_g", "w_out", "ffn_norm_g", "w_up", "w_down", "final_norm_g"]
PACK_UNIT = 8 * 128


def _pack(parts):
    rows = []
    for p in parts:
        flat = p.reshape(-1)
        pad = (-flat.shape[0]) % PACK_UNIT
        rows.append(jnp.pad(flat, (0, pad)).reshape(-1, 128))
    return jnp.concatenate(rows, axis=0)


def _unpack(packed, like):
    out, row = [], 0
    for p in like:
        n = math.prod(p.shape)
        nrows = (n + PACK_UNIT - 1) // PACK_UNIT * 8
        out.append(packed[row:row + nrows].reshape(-1)[:n].reshape(p.shape))
        row += nrows
    return out


def kernel(x, rel_bias_table, mix_norm_g, w_in, gate_norm_g, gate_norm_b, w_spatial, b_spatial, attn_sinks, out_norm_a_g, out_norm_b_g, w_out, ffn_norm_g, w_up, w_down, final_norm_g, loss_target, m_rel_bias_table, m_mix_norm_g, m_w_in, m_gate_norm_g, m_gate_norm_b, m_w_spatial, m_b_spatial, m_attn_sinks, m_out_norm_a_g, m_out_norm_b_g, m_w_out, m_ffn_norm_g, m_w_up, m_w_down, m_final_norm_g, v_rel_bias_table, v_mix_norm_g, v_w_in, v_gate_norm_g, v_gate_norm_b, v_w_spatial, v_b_spatial, v_attn_sinks, v_out_norm_a_g, v_out_norm_b_g, v_w_out, v_ffn_norm_g, v_w_up, v_w_down, v_final_norm_g):
    args = dict(locals())
    wts = {n: args[n] for n in WEIGHTS}
    mom = {n: args["m_" + n] for n in WEIGHTS}
    var = {n: args["v_" + n] for n in WEIGHTS}
    sp = {n: wts[n] for n in SMALL}
    x2, tgt = x[0], loss_target[0]
    T = x2.shape[0]
    tm = min(512, T)
    tl = min(1024, T)
    lg = sp["gate_norm_g"].reshape(A_GROUPS, CHUNK)
    lb = sp["gate_norm_b"].reshape(A_GROUPS, CHUNK)
    wsp = sp["w_spatial"].reshape(A_GROUPS, CHUNK, CHUNK)
    bs_col = sp["b_spatial"].reshape(A_GROUPS, CHUNK, 1)
    sinks = sp["attn_sinks"].reshape(1, B_HEADS)
    ga = sp["out_norm_a_g"].reshape(1, A_WIDTH)
    gb = sp["out_norm_b_g"].reshape(1, B_WIDTH)
    g1 = sp["mix_norm_g"].reshape(1, D_MODEL)
    g2 = sp["ffn_norm_g"].reshape(1, D_MODEL)
    gf = sp["final_norm_g"].reshape(1, D_MODEL)

    def owner_total(n, gh, others):
        return _owner_total(gh, others, tm=ROW_TILE[n], name="rs_owner_total_" + n)

    def halves_view(at, shards):
        return at.reshape(shards, 2, at.shape[0] // shards // 2, at.shape[1])

    for d in (wts, mom, var):
        d["w_in"] = jnp.swapaxes(d["w_in"], 1, 2)

    s_in = _cast_into_slot(wts["w_in"], tm=ROW_TILE["w_in"], name="cast_w_in")
    s_out = _cast_into_slot(wts["w_out"], tm=256, name="cast_w_out")
    s_up, ((g_in,),) = _cast_into_slot_carrying(wts["w_up"], tm=256, name="cast_w_up",
                                                rides=[_ride_gather(s_in, direct=(0, 1, 1))])
    s_down = _cast_into_slot(wts["w_down"], tm=256, name="cast_w_down")
    win_t = g_in.reshape(PROJ_WIDTH, D_MODEL)
    bias = _bias_build(sp["rel_bias_table"])
    (n1, proj), ((g_out,), (s_up,)) = _norm_matmul_wide(
        x2, g1, win_t, tm=tm, tn=PROJ_WIDTH // 2, name="in_proj",
        rides=[_ride_gather(s_out, direct=(0, 1, 1)), _ride_gather(s_up, s1=(0, 3, 8))])
    wo = g_out.reshape(A_WIDTH + B_WIDTH, D_MODEL)
    (mixed, mixed_t, ab), ((s_up,), (s_down,), (n1_sib,)) = _mixer_fwd(
        proj, lg, lb, wsp, bs_col, sinks, bias, ga, gb,
        rides=[_ride_gather(s_up, s2=(0, 3, 8), s1=(3, 8, 8)), _ride_gather(s_down, s1=(0, 3, 8)),
               _ride_to_sibling(n1, first=True)])
    mixed_t = halves_view(mixed_t, N_CHIPS)
    h1, ((wu,), (mixed_t_sib,)) = _matmul_res(
        mixed, wo, x2, tm=tl, tn=1024, tk=D_MODEL, prologue=_to_bf16, name="out_proj",
        rides=[_ride_gather(s_up, s3=(0, 3, 8), tail=(3, 8, 8), mid_frac=0.75), _ride_to_sibling(mixed_t, halves=True)])
    (n2t, zp, z2, z2t), ((g_down,),) = _norm_matmul_sq(
        h1, g2, wu, tm=tl, tn=1024, name="up_proj", rides=[_ride_gather(s_down, tail=(0, 3, 8), direct=(3, 8, 8))])
    wd = g_down.reshape(D_FF, D_MODEL)
    n2t, z2t = halves_view(n2t, 1), halves_view(z2t, N_CHIPS)
    h2, ((n2t_sib,), (z2t_sib,)) = _matmul_res(
        z2, wd, h1, tm=tl, tn=1024, tk=4096, prologue=_to_bf16, name="down_proj",
        rides=[_ride_to_sibling(n2t, halves=True), _ride_to_sibling(z2t, halves=True)])

    dh2, dh2b, dgf, loss = _loss_bwd(h2, tgt, gf, tm=tm)
    dzp, ((dh2b_sib,),) = _matmul_nt(dh2b, wd, tm=tl, tn=1024, tk=D_MODEL, name="bwd_dz", extra=zp,
                                     epilogue=_sq_relu_grad, out_dtype=BF16, rides=[_ride_to_sibling(dh2b)])
    (gd, gdb), ((dzp_sib,),) = _grad_pair(z2t, z2t_sib, dh2b, dh2b_sib, cols_sharded=False, tmo=1024, tk=min(2048, T),
                                          tn=1024, name="grad_w_down", rides=[_ride_to_sibling(dzp)])
    (gu, gub), ((o_d,),) = _grad_pair(n2t, n2t_sib, dzp, dzp_sib, cols_sharded=True, tmo=1024, tk=min(2048, T),
                                      tn=1024, name="grad_w_up", rides=[_ride_scatter(gdb, None, (0, 7, 8))])
    dn2, ((o_d,), (o_u,)) = _matmul_nt(dzp, wu, tm=tl, tn=1024, tk=4096, name="bwd_dn2",
                                       rides=[_ride_scatter(gdb, o_d, (7, 8, 8)), _ride_scatter(gub, None, (0, 6, 8))])
    h_d = owner_total("w_down", gd, o_d)
    (dh1, dh1b, dg2), ((o_u,),) = _rms_bwd_res(dn2, h1, g2, dh2, tm=tm, name="ffn_norm_bwd",
                                               rides=[_ride_scatter(gub, o_u, (6, 7, 8))])
    dmixed, ((o_u,), (dh1b_sib,), (w_d,)) = _matmul_nt(
        dh1b, wo, tm=tl, tn=1024, tk=D_MODEL, name="bwd_dmixed",
        rides=[_ride_scatter(gub, o_u, (7, 8, 8)), _ride_to_sibling(dh1b), _ride_swap(h_d)])
    h_u = owner_total("w_up", gu, o_u)
    (go, gob), ((w_u,),) = _grad_pair_merged(mixed_t, mixed_t_sib, dh1b, dh1b_sib, tk=tl, name="grad_w_out",
                                             rides=[_ride_swap(h_u)])
    (duv, duv_t, dga, dwsp, dbs, dlg, dlb), ((o_o,),) = _gmlp_bwd(proj, ab, dmixed, ga, lg, lb, wsp, bs_col,
                                                                  rides=[_ride_scatter(gob)])
    h_o = owner_total("w_out", go, o_o)
    small = {"gate_norm_g": dlg, "gate_norm_b": dlb, "w_spatial": dwsp, "b_spatial": dbs, "out_norm_a_g": dga}
    hr_in = PROJ_WIDTH // N_CHIPS // 2
    (dq, dkv, dq_t, dkv_t, dgb, dsinks, dbias), ((w_o,), (dproj_t_sib,)) = _attn_bwd(
        proj, ab, dmixed, gb, sinks, bias, rides=[_ride_swap(h_o), _ride_rows_to_sibling(duv_t, hr_in, 2, N_CHIPS)])
    dtable = _bias_grad(dbias)
    dproj_t = halves_view(jnp.concatenate([duv_t, dq_t, dkv_t], axis=0), N_CHIPS)
    ((dproj_t_sib,),) = _carrier([_ride_to_sibling(dproj_t, halves=True, shards=(2, N_CHIPS), land=dproj_t_sib)],
                                 name="trade_dproj_t")
    (gi, gib), ((slots_a,),) = _grad_pair(
        dproj_t, dproj_t_sib, n1, n1_sib, cols_sharded=False, tmo=hr_in, tk=tl, name="grad_w_in",
        rides=[_ride_small_to_all(_pack([small[n] for n in SMALL_A]))])
    dn1, ((o_i,),) = _matmul_parts([duv, dq, dkv], win_t, tm=tl, tn=1024, name="bwd_dn1", rides=[_ride_scatter(gib)])
    h_i = owner_total("w_in", gi, o_i)
    dx, dg1 = _rms_bwd_res(dn1, x2, g1, dh1, tm=tm, name="mix_norm_bwd", bf16_copy=False)
    small.update({"rel_bias_table": dtable.reshape(N_BUCKETS, B_HEADS), "mix_norm_g": dg1, "attn_sinks": dsinks,
                  "out_norm_b_g": dgb, "ffn_norm_g": dg2, "final_norm_g": dgf})
    (w_i,), (slots_b,) = _carrier([_ride_swap(h_i), _ride_small_to_all(_pack([small[n] for n in SMALL_B] + [loss]))],
                                  name="swap_w_in")

    out_g, out_d, out_m, out_v = {}, {}, {}, {}
    for n, h, s in zip(LARGE, [h_i, h_o, h_u, h_d], [w_i, w_o, w_u, w_d]):
        res = _adamw_halves(wts[n], h, s, mom[n], var[n], tm=ROW_TILE[n], name="adamw_" + n)
        if n == "w_in":
            res = [jnp.swapaxes(r, 1, 2) for r in res]
        out_g[n], out_d[n], out_m[n], out_v[n] = res
    for names, slots, tag in ((SMALL_A, slots_a, "a"), (SMALL_B, slots_b, "b")):
        extra = [jnp.zeros((1, 1), F32)] if tag == "b" else []
        like = [wts[n] for n in names] + extra
        res = _adamw_small(_pack(like), slots, _pack([mom[n] for n in names] + extra),
                           _pack([var[n] for n in names] + extra), name="adamw_small_" + tag)
        for store, packed in zip((out_g, out_d, out_m, out_v), res):
            for n, val in zip(names + ["loss"], _unpack(packed, like)):
                store[n] = val

    total = out_g["loss"][0, 0]
    return (total, dx[None], *[out_g[n] for n in WEIGHTS], *[out_d[n] for n in WEIGHTS],
            *[out_m[n] for n in WEIGHTS], *[out_v[n] for n in WEIGHTS])
```

```python
import math

import numpy as np
import jax
import jax.numpy as jnp
from jax import lax
from jax.experimental import pallas as pl
from jax.experimental.pallas import tpu as pltpu

F32 = jnp.float32
BF16 = jnp.bfloat16

D_MODEL = 2048
CHUNK = 128
A_GROUPS = 8
A_WIDTH = 1024
HEAD_DIM = 64
B_HEADS = 16
Q_PER_KV = 8
B_WIDTH = 1024
KV_WIDTH = 128
PROJ_WIDTH = 3328
D_FF = 8192
N_BUCKETS = 32
EPS = 1e-5
NEG = -1e30
SCALE = HEAD_DIM ** -0.5
N_CHIPS = 4
N_DEV = 8

ADAM_LR = 0.001
ADAM_B1 = 0.9
ADAM_B2 = 0.999
ADAM_EPS = 1e-08
ADAM_WD = 0.01
ADAM_STEP = 10

VMEM_LIMIT = 60 * 1024 * 1024
MESH = pl.DeviceIdType.MESH


def _bucket_thresholds():
    d = np.arange(CHUNK)
    n_exact = N_BUCKETS // 2
    relf = np.maximum(d, n_exact).astype(np.float64)
    large = n_exact + (np.log(relf / n_exact) / math.log(CHUNK / n_exact) * (N_BUCKETS - n_exact)).astype(np.int32)
    bucket = np.where(d < n_exact, d, np.minimum(large, N_BUCKETS - 1))
    return [int(np.min(d[bucket >= b])) for b in range(1, N_BUCKETS)]


BUCKET_THR = _bucket_thresholds()


def _params(sem=None):
    return pltpu.CompilerParams(dimension_semantics=sem, vmem_limit_bytes=VMEM_LIMIT)


def _gelu(x):
    c = math.sqrt(2.0 / math.pi)
    return 0.5 * x * (1.0 + jnp.tanh(c * (x + 0.044715 * (x * x * x))))


def _gelu_and_grad(x):
    c = math.sqrt(2.0 / math.pi)
    x2 = x * x
    t = jnp.tanh(c * (x + 0.044715 * (x2 * x)))
    g = 0.5 * x * (1.0 + t)
    dg = 0.5 * (1.0 + t) + 0.5 * x * (1.0 - t * t) * (c * (1.0 + 3.0 * 0.044715 * x2))
    return g, dg


def _dot(a, b):
    return jnp.dot(a, b, preferred_element_type=F32)


def _dot_nt(a, b):
    return lax.dot_general(a, b, (((1,), (1,)), ((), ())), preferred_element_type=F32)


def _dot_tn(a, b):
    return lax.dot_general(a, b, (((0,), (0,)), ((), ())), preferred_element_type=F32)


def _rms_bwd(dn, h, g):
    r = lax.rsqrt(jnp.mean(h * h, axis=-1, keepdims=True) + EPS)
    w = dn * g
    dh = r * w - h * ((r * r * r) * jnp.mean(w * h, axis=-1, keepdims=True))
    return dh, r


def _place():
    x, y, c = lax.axis_index("x"), lax.axis_index("y"), lax.axis_index("c")
    chips = [(1 - x, y), (x, 1 - y), (1 - x, 1 - y)]
    return x, y, c, chips


def _remote(src, dst, send_sem, recv_sem, to):
    return pltpu.make_async_remote_copy(src_ref=src, dst_ref=dst, send_sem=send_sem, recv_sem=recv_sem,
                                        device_id=to, device_id_type=MESH)


class _Ride:
    def __init__(self, args, out_shape, n_sem, start, finish, mid=None, mid_frac=0.8, aliases=None):
        self.args, self.out_shape, self.n_sem = list(args), list(out_shape), n_sem
        self.start, self.mid, self.finish, self.mid_frac = start, mid, finish, mid_frac
        self.aliases = dict(aliases or {})


def _call(body, *, name, grid, in_specs, out_specs, out_shape, scratch_shapes=(), sem=None, rides=()):
    single = not isinstance(out_shape, (list, tuple))
    out_specs = [out_specs] if single else list(out_specs)
    out_shape = [out_shape] if single else list(out_shape)
    n_in, n_out, n_scr = len(in_specs), len(out_shape), len(scratch_shapes)
    r_in = [len(r.args) for r in rides]
    r_out = [len(r.out_shape) for r in rides]
    any_spec = pl.BlockSpec(memory_space=pl.ANY)
    aliases, off_i, off_o = {}, n_in, n_out
    for r in rides:
        for i, o in r.aliases.items():
            aliases[off_i + i] = off_o + o
        off_i += len(r.args)
        off_o += len(r.out_shape)
    steps = math.prod(grid)

    def wrapped(*refs):
        p = 0
        ins = refs[p:p + n_in]; p += n_in
        rins = refs[p:p + sum(r_in)]; p += sum(r_in)
        outs = refs[p:p + n_out]; p += n_out
        routs = refs[p:p + sum(r_out)]; p += sum(r_out)
        scr = refs[p:p + n_scr]; p += n_scr
        sems = refs[p:]
        parts, pi, po = [], 0, 0
        for k, r in enumerate(rides):
            parts.append((rins[pi:pi + r_in[k]], routs[po:po + r_out[k]], sems[2 * k], sems[2 * k + 1]))
            pi += r_in[k]
            po += r_out[k]
        lin = 0
        for d in range(len(grid)):
            lin = lin * grid[d] + pl.program_id(d)
        if rides:
            @pl.when(lin == 0)
            def _():
                for r, part in zip(rides, parts):
                    r.start(*part)
        body(*ins, *outs, *scr)
        for r, part in zip(rides, parts):
            if r.mid is not None:
                @pl.when(lin == min(steps - 1, int(r.mid_frac * steps)))
                def _(r=r, part=part):
                    r.mid(*part)
        if rides:
            @pl.when(lin == steps - 1)
            def _():
                for r, part in zip(rides, parts):
                    r.finish(*part)

    scratch = list(scratch_shapes)
    for r in rides:
        scratch += [pltpu.SemaphoreType.DMA((r.n_sem,)), pltpu.SemaphoreType.DMA((r.n_sem,))]
    if rides:
        sem = ("arbitrary",) * len(grid)
    res = pl.pallas_call(
        wrapped, name=name, grid=grid,
        in_specs=list(in_specs) + [any_spec] * sum(r_in),
        out_specs=out_specs + [any_spec] * sum(r_out),
        out_shape=out_shape + [s for r in rides for s in r.out_shape],
        scratch_shapes=scratch, input_output_aliases=aliases,
        compiler_params=_params(sem),
    )

    def run(*args):
        got = res(*args, *[a for r in rides for a in r.args])
        mine = got[0] if single else list(got[:n_out])
        if not rides:
            return mine
        rest, out = list(got[n_out:]), []
        for k in range(len(rides)):
            out.append(rest[:r_out[k]])
            rest = rest[r_out[k]:]
        return mine, out

    return run


def _ride_gather(slot, s1=None, s2=None, s3=None, tail=None, direct=None, mid_frac=0.6):
    half = slot.shape[1] // 2

    def rows(part, c, which=None):
        k0, k1, n = part
        count, first = (k1 - k0) * (half // n), c * half + k0 * (half // n)
        return pl.ds(first, count) if which is None else pl.ds(first + which * (count // 2), count // 2)

    def ids():
        x, y, c, _ = _place()
        return x, y, c, 2 * x + y, 2 * (1 - x) + y, 2 * x + (1 - y), 2 * (1 - x) + (1 - y)

    def copy(full, chip, r, ss, rs, k, to):
        piece = full.at[chip, r, :]
        return _remote(piece, piece, ss.at[k], rs.at[k], to)

    def to_neighbours(full, ss, rs, part, base):
        x, y, c, me, _, _, _ = ids()
        return [copy(full, me, rows(part, c), ss, rs, base, (1 - x, y, c)),
                copy(full, me, rows(part, c), ss, rs, base + 1, (x, 1 - y, c))]

    def from_neighbours(full, ss, rs, part, base):
        x, y, c, _, cx, cy, _ = ids()
        return [copy(full, cx, rows(part, c), ss, rs, base, (x, y, c)), copy(full, cy, rows(part, c), ss, rs, base + 1, (x, y, c))]

    def onward(full, ss, rs, part, base):
        x, y, c, _, cx, cy, _ = ids()
        return [copy(full, cx, rows(part, c, 0), ss, rs, base, (x, 1 - y, c)),
                copy(full, cy, rows(part, c, 1), ss, rs, base + 1, (1 - x, y, c))]

    def from_onward(full, ss, rs, part, base):
        x, y, c, _, _, _, cd = ids()
        return [copy(full, cd, rows(part, c, 0), ss, rs, base, (x, y, c)), copy(full, cd, rows(part, c, 1), ss, rs, base + 1, (x, y, c))]

    def to_sibling(full, ss, rs, part, base, diagonal):
        x, y, c, _, cx, cy, cd = ids()
        return [copy(full, chip, rows(part, c), ss, rs, base + j, (x, y, 1 - c))
                for j, chip in enumerate([cd] if diagonal else [cx, cy])]

    def from_sibling(full, ss, rs, part, base, diagonal):
        x, y, c, _, cx, cy, cd = ids()
        return [copy(full, chip, rows(part, 1 - c), ss, rs, base + j, (x, y, c))
                for j, chip in enumerate([cd] if diagonal else [cx, cy])]

    def to_chips(full, ss, rs, part, base):
        x, y, c, me, _, _, _ = ids()
        chips = [(1 - x, y), (x, 1 - y), (1 - x, 1 - y)]
        return [copy(full, me, rows(part, c), ss, rs, base + j, (*chip, c)) for j, chip in enumerate(chips)]

    def all_three(full, ss, rs, part, base, half_of, to):
        x, y, c, _, cx, cy, cd = ids()
        return [copy(full, chip, rows(part, half_of(c)), ss, rs, base + j, to(x, y, c)) for j, chip in enumerate([cx, cy, cd])]

    def start(ins, outs, ss, rs):
        full, cps = outs[0], []
        if s1 is not None:
            cps += to_neighbours(full, ss, rs, s1, 0)
        for part, b_ici, b_sib in ((s2, 2, 4), (tail, 7, 9)):
            if part is not None:
                cps += onward(full, ss, rs, part, b_ici) + to_sibling(full, ss, rs, part, b_sib, False)
        if s3 is not None:
            cps += to_sibling(full, ss, rs, s3, 6, True)
        if direct is not None:
            cps += to_chips(full, ss, rs, direct, 12)
        for cp in cps:
            cp.start()

    def mid(ins, outs, ss, rs):
        if tail is not None:
            for cp in from_onward(outs[0], ss, rs, tail, 7):
                cp.wait_recv()
            for cp in to_sibling(outs[0], ss, rs, tail, 11, True):
                cp.start()
        if direct is not None:
            for cp in all_three(outs[0], ss, rs, direct, 12, lambda c: c, lambda x, y, c: (x, y, c)):
                cp.wait_recv()
            for cp in all_three(outs[0], ss, rs, direct, 15, lambda c: c, lambda x, y, c: (x, y, 1 - c)):
                cp.start()

    def finish(ins, outs, ss, rs):
        full, got, sent = outs[0], [], []
        if s1 is not None:
            got += from_neighbours(full, ss, rs, s1, 0)
            sent += to_neighbours(full, ss, rs, s1, 0)
        if s2 is not None:
            got += from_onward(full, ss, rs, s2, 2) + from_sibling(full, ss, rs, s2, 4, False)
            sent += onward(full, ss, rs, s2, 2) + to_sibling(full, ss, rs, s2, 4, False)
        if s3 is not None:
            got += from_sibling(full, ss, rs, s3, 6, True)
            sent += to_sibling(full, ss, rs, s3, 6, True)
        if tail is not None:
            got += from_sibling(full, ss, rs, tail, 9, False) + from_sibling(full, ss, rs, tail, 11, True)
            sent += onward(full, ss, rs, tail, 7) + to_sibling(full, ss, rs, tail, 9, False) + to_sibling(full, ss, rs, tail, 11, True)
        if direct is not None:
            got += all_three(full, ss, rs, direct, 15, lambda c: 1 - c, lambda x, y, c: (x, y, c))
            sent += to_chips(full, ss, rs, direct, 12)
            sent += all_three(full, ss, rs, direct, 15, lambda c: c, lambda x, y, c: (x, y, 1 - c))
        for cp in got:
            cp.wait_recv()
        for cp in sent:
            cp.wait_send()

    return _Ride([slot], [jax.ShapeDtypeStruct(slot.shape, slot.dtype)], 18, start, finish,
                 mid=mid if (tail is not None or direct is not None) else None, mid_frac=mid_frac, aliases={0: 0})


def _ride_scatter(q, land=None, part=(0, 1)):
    k0, k1, n = part if len(part) == 3 else (part[0], part[0] + 1, part[1])
    rows_n = q.shape[1] // n
    rows = pl.ds(k0 * rows_n, (k1 - k0) * rows_n)

    def copies(ins, outs, ss, rs):
        x, y, c, chips = _place()
        return [_remote(ins[0].at[2 * chip[0] + chip[1], rows, :], outs[0].at[j, rows, :], ss.at[j], rs.at[j], (*chip, c))
                for j, chip in enumerate(chips)]

    def start(*a):
        for cp in copies(*a):
            cp.start()

    def finish(*a):
        for cp in copies(*a):
            cp.wait()

    shape = jax.ShapeDtypeStruct((3,) + q.shape[1:], q.dtype)
    if land is None:
        return _Ride([q], [shape], 3, start, finish)
    return _Ride([q, land], [shape], 3, start, finish, aliases={1: 0})


def _ride_to_sibling(a, halves=False, first=False, shards=None, land=None):
    s0, s1 = shards or (0, a.shape[0])

    def copy(ins, outs, ss, rs):
        x, y, c, _ = _place()
        if halves:
            src, dst = ins[0].at[s0:s1, 1 - c], outs[0].at[s0:s1]
        else:
            src, dst = (ins[0].at[0] if first else ins[0]), outs[0]
        return _remote(src, dst, ss.at[0], rs.at[0], (x, y, 1 - c))

    shape = (a.shape[0],) + a.shape[2:] if halves else (a.shape[1:] if first else a.shape)
    return _Ride([a] if land is None else [a, land], [jax.ShapeDtypeStruct(shape, a.dtype)], 1,
                 lambda *a_: copy(*a_).start(), lambda *a_: copy(*a_).wait(), aliases=None if land is None else {1: 0})


def _ride_rows_to_sibling(a, hr, shards, total):
    def copies(ins, outs, ss, rs):
        x, y, c, _ = _place()
        return [_remote(ins[0].at[pl.ds((2 * s + 1 - c) * hr, hr), :], outs[0].at[s], ss.at[s], rs.at[s], (x, y, 1 - c))
                for s in range(shards)]

    def start(*a_):
        for cp in copies(*a_):
            cp.start()

    def finish(*a_):
        for cp in copies(*a_):
            cp.wait()

    return _Ride([a], [jax.ShapeDtypeStruct((total, hr, a.shape[1]), a.dtype)], shards, start, finish)


def _ride_swap(h):
    def copy(ins, outs, ss, rs):
        x, y, c, _ = _place()
        return _remote(ins[0], outs[0], ss.at[0], rs.at[0], (x, y, 1 - c))

    return _Ride([h], [jax.ShapeDtypeStruct(h.shape, h.dtype)], 1,
                 lambda *a: copy(*a).start(), lambda *a: copy(*a).wait())


def _mesh_place(p):
    return (p // 4, (p // 2) % 2, p % 2)


def _ride_small_to_all(packed):
    def copies(ins, outs, ss, rs):
        x, y, c, _ = _place()
        me = 4 * x + 2 * y + c
        return [_remote(ins[0], outs[0].at[me], ss.at[k - 1], rs.at[k - 1], _mesh_place((me + k) % N_DEV))
                for k in range(1, N_DEV)]

    def own(ins, outs, ss, rs):
        x, y, c, _ = _place()
        return pltpu.make_async_copy(ins[0], outs[0].at[4 * x + 2 * y + c], ss.at[N_DEV - 1])

    def start(*a):
        own(*a).start()
        for cp in copies(*a):
            cp.start()

    def finish(ins, outs, ss, rs):
        x, y, c, _ = _place()
        me = 4 * x + 2 * y + c
        for k in range(1, N_DEV):
            _remote(ins[0], outs[0].at[(me + N_DEV - k) % N_DEV], ss.at[k - 1], rs.at[k - 1], (x, y, c)).wait_recv()
        for cp in copies(ins, outs, ss, rs):
            cp.wait_send()
        own(ins, outs, ss, rs).wait()

    return _Ride([packed], [jax.ShapeDtypeStruct((N_DEV,) + packed.shape, packed.dtype)], N_DEV, start, finish)


def _carrier(rides, *, name):
    _, outs = _call(lambda: None, name=name, grid=(1,), in_specs=[], out_specs=[], out_shape=[], rides=rides)()
    return outs


def _norm_bf16(a_ref, g_ref):
    xf = a_ref[...]
    r = lax.rsqrt(jnp.mean(xf * xf, axis=-1, keepdims=True) + EPS)
    return ((xf * r) * g_ref[...]).astype(BF16)


def _norm_matmul_wide(a, g, b, *, tm, tn, name, rides=()):
    T, K = a.shape
    N = b.shape[0]

    def body(a_ref, g_ref, b_ref, n_ref, o_ref):
        n = _norm_bf16(a_ref, g_ref)
        n_ref[...] = n
        o_ref[...] = _dot_nt(n, b_ref[...])

    return _call(
        body, name=name, grid=(N // tn, T // tm),
        in_specs=[pl.BlockSpec((tm, K), lambda j, i: (i, 0)), pl.BlockSpec((1, K), lambda j, i: (0, 0)),
                  pl.BlockSpec((tn, K), lambda j, i: (j, 0))],
        out_specs=[pl.BlockSpec((None, tm, K), lambda j, i: (j, i, 0)), pl.BlockSpec((tm, tn), lambda j, i: (i, j))],
        out_shape=[jax.ShapeDtypeStruct((N // tn, T, K), BF16), jax.ShapeDtypeStruct((T, N), F32)],
        sem=("arbitrary", "arbitrary"), rides=rides,
    )(a, g, b)


def _norm_matmul_sq(a, g, b, *, tm, tn, name, rides=()):
    T, K = a.shape
    per = b.shape[2] // tn
    N = b.shape[0] * b.shape[2]

    def body(a_ref, g_ref, b_ref, nt_ref, o_ref, z_ref, zt_ref, n_scr):
        @pl.when(pl.program_id(1) == 0)
        def _():
            n = _norm_bf16(a_ref, g_ref)
            n_scr[...] = n
            nt_ref[...] = n.T
        r = jnp.maximum(_dot(n_scr[...], b_ref[...]), 0.0)
        o_ref[...] = r.astype(BF16)
        z = (r * r).astype(BF16)
        z_ref[...] = z
        zt_ref[...] = z.T

    return _call(
        body, name=name, grid=(T // tm, N // tn),
        in_specs=[pl.BlockSpec((tm, K), lambda i, j: (i, 0)), pl.BlockSpec((1, K), lambda i, j: (0, 0)),
                  pl.BlockSpec((None, K, tn), lambda i, j: (j // per, 0, j % per))],
        out_specs=[pl.BlockSpec((K, tm), lambda i, j: (0, i)), pl.BlockSpec((tm, tn), lambda i, j: (i, j)),
                   pl.BlockSpec((tm, tn), lambda i, j: (i, j)), pl.BlockSpec((tn, tm), lambda i, j: (j, i))],
        out_shape=[jax.ShapeDtypeStruct((K, T), BF16), jax.ShapeDtypeStruct((T, N), BF16),
                   jax.ShapeDtypeStruct((T, N), BF16), jax.ShapeDtypeStruct((N, T), BF16)],
        scratch_shapes=[pltpu.VMEM((tm, K), BF16)],
        sem=("parallel", "arbitrary"), rides=rides,
    )(a, g, b)


def _grad_pair(at, at_sib, b, b_sib, *, cols_sharded, tmo, tk, name, rides=()):
    S, _, hr, T = at.shape
    C = b.shape[-1] // N_CHIPS if cols_sharded else b.shape[-1]
    nk = T // tk
    a_sel = (lambda s: 0) if cols_sharded else (lambda s: s)
    b_sel = (lambda s: s) if cols_sharded else (lambda s: 0)
    if b.ndim == 3:
        b_spec = pl.BlockSpec((None, tk, C), lambda s, i, k: (0, k, b_sel(s)))
    else:
        b_spec = pl.BlockSpec((tk, C), lambda s, i, k: (k, b_sel(s)))

    def body(a_ref, as_ref, b_ref, bs_ref, o_ref, ob_ref):
        k = pl.program_id(2)
        p = _dot(a_ref[...], b_ref[...]) + _dot(as_ref[...], bs_ref[...])

        @pl.when(k == 0)
        def _():
            o_ref[...] = p

        @pl.when(k > 0)
        def _():
            o_ref[...] += p

        @pl.when(k == nk - 1)
        def _():
            ob_ref[...] = o_ref[...].astype(BF16)

    out = pl.BlockSpec((None, tmo, C), lambda s, i, k: (s, i, 0))
    return _call(
        body, name=name, grid=(N_CHIPS, hr // tmo, nk),
        in_specs=[pl.BlockSpec((None, None, tmo, tk), lambda s, i, k: (a_sel(s), lax.axis_index("c"), i, k)),
                  pl.BlockSpec((None, tmo, tk), lambda s, i, k: (a_sel(s), i, k)),
                  b_spec, pl.BlockSpec((tk, C), lambda s, i, k: (k, b_sel(s)))],
        out_specs=[out, out],
        out_shape=[jax.ShapeDtypeStruct((N_CHIPS, hr, C), F32), jax.ShapeDtypeStruct((N_CHIPS, hr, C), BF16)],
        sem=("parallel", "parallel", "arbitrary"), rides=rides,
    )(at, at_sib, b, b_sib)


def _grad_pair_merged(at, at_sib, b, b_sib, *, tk, name, rides=()):
    S, _, hr, T = at.shape
    C = b.shape[-1]
    nk = T // tk

    def body(a_ref, as_ref, b_ref, bs_ref, o_ref, ob_ref):
        k = pl.program_id(0)
        p = (_dot(a_ref[...].reshape(S * hr, tk), b_ref[...])
             + _dot(as_ref[...].reshape(S * hr, tk), bs_ref[...])).reshape(S, hr, C)

        @pl.when(k == 0)
        def _():
            o_ref[...] = p

        @pl.when(k > 0)
        def _():
            o_ref[...] += p

        @pl.when(k == nk - 1)
        def _():
            ob_ref[...] = o_ref[...].astype(BF16)

    out = pl.BlockSpec((S, hr, C), lambda k: (0, 0, 0))
    return _call(
        body, name=name, grid=(nk,),
        in_specs=[pl.BlockSpec((S, None, hr, tk), lambda k: (0, lax.axis_index("c"), 0, k)),
                  pl.BlockSpec((S, hr, tk), lambda k: (0, 0, k)),
                  pl.BlockSpec((tk, C), lambda k: (k, 0)), pl.BlockSpec((tk, C), lambda k: (k, 0))],
        out_specs=[out, out],
        out_shape=[jax.ShapeDtypeStruct((S, hr, C), F32), jax.ShapeDtypeStruct((S, hr, C), BF16)],
        sem=("arbitrary",), rides=rides,
    )(at, at_sib, b, b_sib)


def _matmul_parts(parts, b, *, tm, tn, name, rides=()):
    T = parts[0].shape[0]
    N = b.shape[1]
    offs = [sum(p.shape[1] for p in parts[:i]) for i in range(len(parts))]
    assert all(o % p.shape[1] == 0 for o, p in zip(offs, parts))

    def body(*refs):
        n = len(parts)
        acc = _dot(refs[0][...], refs[n][...])
        for i in range(1, n):
            acc = acc + _dot(refs[i][...], refs[n + i][...])
        refs[-1][...] = acc

    a_specs = [pl.BlockSpec((tm, p.shape[1]), lambda i, j: (i, 0)) for p in parts]
    b_specs = [pl.BlockSpec((p.shape[1], tn), lambda i, j, r=o // p.shape[1]: (r, j)) for o, p in zip(offs, parts)]
    return _call(
        body, name=name, grid=(T // tm, N // tn), in_specs=a_specs + b_specs,
        out_specs=pl.BlockSpec((tm, tn), lambda i, j: (i, j)), out_shape=jax.ShapeDtypeStruct((T, N), F32),
        sem=("parallel", "parallel"), rides=rides,
    )(*parts, *([b] * len(parts)))


def _to_bf16(v):
    return v.astype(BF16)


def _matmul_res(a, b, res, *, tm, tn, tk, prologue, name, rides=()):
    T, K = a.shape
    N = b.shape[1]

    def body(a_ref, b_ref, res_ref, o_ref):
        k = pl.program_id(2)
        p = _dot(prologue(a_ref[...]), b_ref[...])

        @pl.when(k == 0)
        def _():
            o_ref[...] = res_ref[...] + p

        @pl.when(k > 0)
        def _():
            o_ref[...] += p

    return _call(
        body, name=name, grid=(T // tm, N // tn, K // tk),
        in_specs=[pl.BlockSpec((tm, tk), lambda i, j, k: (i, k)), pl.BlockSpec((tk, tn), lambda i, j, k: (k, j)),
                  pl.BlockSpec((tm, tn), lambda i, j, k: (i, j))],
        out_specs=pl.BlockSpec((tm, tn), lambda i, j, k: (i, j)),
        out_shape=jax.ShapeDtypeStruct((T, N), F32),
        sem=("parallel", "parallel", "arbitrary"), rides=rides,
    )(a, b, res)


def _matmul_nt(a, b, *, tm, tn, tk, name, extra=None, epilogue=None, out_dtype=F32, rides=()):
    T, K = a.shape
    two = b.ndim == 3 and tk == 2 * b.shape[2]
    if two:
        N, ks = b.shape[1], b.shape[2]
        b_specs = [pl.BlockSpec((None, tn, ks), lambda i, j, k: (2 * k, j, 0)),
                   pl.BlockSpec((None, tn, ks), lambda i, j, k: (2 * k + 1, j, 0))]
    elif b.ndim == 3:
        per = b.shape[2] // tk
        N = b.shape[1]
        b_specs = [pl.BlockSpec((None, tn, tk), lambda i, j, k: (k // per, j, k % per))]
    else:
        N = b.shape[0]
        b_specs = [pl.BlockSpec((tn, tk), lambda i, j, k: (j, k))]
    nb = len(b_specs)
    nk = K // tk
    assert out_dtype == F32 or nk == 1
    in_specs = [pl.BlockSpec((tm, tk), lambda i, j, k: (i, k))] + b_specs
    args = [a] + [b] * nb
    if extra is not None:
        in_specs.append(pl.BlockSpec((tm, tn), lambda i, j, k: (i, j)))
        args.append(extra)

    def body(*refs):
        a_ref, b_ref = refs[0], refs[1]
        o_ref = refs[-1]
        if two:
            p = (_dot_nt(a_ref[:, :tk // 2].astype(BF16), refs[1][...])
                 + _dot_nt(a_ref[:, tk // 2:].astype(BF16), refs[2][...]))
        else:
            p = _dot_nt(a_ref[...].astype(BF16), b_ref[...])
        if nk == 1:
            if epilogue is not None:
                p = epilogue(p, refs[1 + nb][...])
            o_ref[...] = p.astype(out_dtype)
        else:
            k = pl.program_id(2)

            @pl.when(k == 0)
            def _():
                o_ref[...] = p

            @pl.when(k > 0)
            def _():
                o_ref[...] += p

    return _call(
        body, name=name, grid=(T // tm, N // tn, nk),
        in_specs=in_specs,
        out_specs=pl.BlockSpec((tm, tn), lambda i, j, k: (i, j)),
        out_shape=jax.ShapeDtypeStruct((T, N), out_dtype),
        sem=("parallel", "parallel", "arbitrary"), rides=rides,
    )(*args)


def _loss_bwd(h2, tgt, g, *, tm):
    T, D = h2.shape

    def body(h_ref, t_ref, g_ref, dh_ref, dhb_ref, dg_ref, loss_ref):
        @pl.when(pl.program_id(0) == 0)
        def _():
            dg_ref[...] = jnp.zeros_like(dg_ref)
            loss_ref[...] = jnp.zeros_like(loss_ref)
        h = h_ref[...]
        gg = g_ref[...]
        r = lax.rsqrt(jnp.mean(h * h, axis=-1, keepdims=True) + EPS)
        hn = h * r
        err = hn * gg - t_ref[...]
        loss_ref[...] += 0.5 * jnp.sum(jnp.mean(err * err, axis=-1, keepdims=True), axis=0, keepdims=True)
        dy = err * (1.0 / D)
        dg_ref[...] += jnp.sum(dy * hn, axis=0, keepdims=True)
        w = dy * gg
        dh = r * w - h * ((r * r * r) * jnp.mean(w * h, axis=-1, keepdims=True))
        dh_ref[...] = dh
        dhb_ref[...] = dh.astype(BF16)

    tile = pl.BlockSpec((tm, D), lambda i: (i, 0))
    return pl.pallas_call(
        body, name="loss_bwd", grid=(T // tm,),
        in_specs=[tile, tile, pl.BlockSpec((1, D), lambda i: (0, 0))],
        out_specs=[tile, tile, pl.BlockSpec((1, D), lambda i: (0, 0)), pl.BlockSpec((1, 1), lambda i: (0, 0))],
        out_shape=[jax.ShapeDtypeStruct((T, D), F32), jax.ShapeDtypeStruct((T, D), BF16),
                   jax.ShapeDtypeStruct((1, D), F32), jax.ShapeDtypeStruct((1, 1), F32)],
        compiler_params=_params(("arbitrary",)),
    )(h2, tgt, g)


def _rms_bwd_res(dn, h, g, dres, *, tm, name, bf16_copy=True, rides=()):
    T, D = h.shape

    def body(dn_ref, h_ref, g_ref, dres_ref, dh_ref, *rest):
        dg_ref = rest[-1]

        @pl.when(pl.program_id(0) == 0)
        def _():
            dg_ref[...] = jnp.zeros_like(dg_ref)
        h_ = h_ref[...]
        dn_ = dn_ref[...]
        dh, r = _rms_bwd(dn_, h_, g_ref[...])
        dg_ref[...] += jnp.sum(dn_ * (h_ * r), axis=0, keepdims=True)
        dh = dres_ref[...] + dh
        dh_ref[...] = dh
        if bf16_copy:
            rest[0][...] = dh.astype(BF16)

    tile = pl.BlockSpec((tm, D), lambda i: (i, 0))
    row = pl.BlockSpec((1, D), lambda i: (0, 0))
    copy_spec = [tile] if bf16_copy else []
    copy_shape = [jax.ShapeDtypeStruct((T, D), BF16)] if bf16_copy else []
    return _call(
        body, name=name, grid=(T // tm,),
        in_specs=[tile, tile, row, tile], out_specs=[tile] + copy_spec + [row],
        out_shape=[jax.ShapeDtypeStruct((T, D), F32)] + copy_shape + [jax.ShapeDtypeStruct((1, D), F32)],
        sem=("arbitrary",), rides=rides,
    )(dn, h, g, dres)


def _rel_distance():
    i = lax.broadcasted_iota(jnp.int32, (CHUNK, 2 * CHUNK), 0)
    j = lax.broadcasted_iota(jnp.int32, (CHUNK, 2 * CHUNK), 1)
    return i + CHUNK - j


def _bias_build(table):
    def body(tab_ref, o_ref):
        rel = _rel_distance()
        j = lax.broadcasted_iota(jnp.int32, (CHUNK, 2 * CHUNK), 1)
        band = (rel >= 0) & (rel < CHUNK)
        ge = [rel >= t for t in BUCKET_THR]
        for h in range(B_HEADS):
            cur = jnp.full((CHUNK, 2 * CHUNK), tab_ref[0, h], F32)
            for b in range(1, N_BUCKETS):
                cur = jnp.where(ge[b - 1], tab_ref[b, h], cur)
            o_ref[0, h] = jnp.where(band & (j >= CHUNK), cur, NEG)
            o_ref[1, h] = jnp.where(band, cur, NEG)

    return pl.pallas_call(
        body, name="bias_build",
        in_specs=[pl.BlockSpec(memory_space=pltpu.SMEM)],
        out_specs=pl.BlockSpec(memory_space=pltpu.VMEM),
        out_shape=jax.ShapeDtypeStruct((2, B_HEADS, CHUNK, 2 * CHUNK), F32),
    )(table)


def _bias_grad(dbias):
    def body(db_ref, o_ref, acc_ref):
        rel = _rel_distance()
        lo = [0] + BUCKET_THR
        hi = BUCKET_THR + [CHUNK]
        for b in range(N_BUCKETS):
            m = (rel >= lo[b]) & (rel < hi[b])
            for h in range(B_HEADS):
                row = b * B_HEADS + h
                acc_ref[row:row + 1, :] = jnp.sum(jnp.where(m, db_ref[h], 0.0), axis=0, keepdims=True)
        o_ref[...] = jnp.sum(acc_ref[...], axis=1, keepdims=True)

    return pl.pallas_call(
        body, name="bias_grad",
        in_specs=[pl.BlockSpec(memory_space=pltpu.VMEM)],
        out_specs=pl.BlockSpec(memory_space=pltpu.VMEM),
        out_shape=jax.ShapeDtypeStruct((N_BUCKETS * B_HEADS, 1), F32),
        scratch_shapes=[pltpu.VMEM((N_BUCKETS * B_HEADS, 2 * CHUNK), F32)],
    )(dbias)


def _causal_mask():
    t = lax.broadcasted_iota(jnp.int32, (CHUNK, CHUNK), 0)
    s = lax.broadcasted_iota(jnp.int32, (CHUNK, CHUNK), 1)
    return s <= t


def _gate_forward(u, v, lg, lb, wc, bs):
    ug = _gelu(u)
    vg = _gelu(v)
    mu = jnp.mean(vg, axis=-1, keepdims=True)
    xc = vg - mu
    rstd = lax.rsqrt(jnp.mean(xc * xc, axis=-1, keepdims=True) + EPS)
    xhat = xc * rstd
    vl = (xhat * lg + lb).astype(BF16)
    mixed = _dot(wc, vl) + bs
    return ug, xhat, rstd, vl, mixed


def _softmax_scores(qk, bias, sink):
    s = qk * SCALE + bias
    m = jnp.maximum(jnp.max(s, axis=-1, keepdims=True), sink)
    p = jnp.exp(s - m)
    e_sink = jnp.exp(sink - m)
    inv = 1.0 / (jnp.sum(p, axis=-1, keepdims=True) + e_sink)
    return p * inv, e_sink * inv


PAIRS = Q_PER_KV // 2


def _head(g, pr, e):
    return g * Q_PER_KV + 2 * pr + e


def _stack_pairs(ref, g, col0=0):
    w = 2 * HEAD_DIM
    return jnp.concatenate([ref[:, col0 + (g * PAIRS + pr) * w:col0 + (g * PAIRS + pr + 1) * w] for pr in range(PAIRS)],
                           axis=0)


def _low_lanes():
    return lax.broadcasted_iota(jnp.int32, (2 * CHUNK, 2 * HEAD_DIM), 1) < HEAD_DIM


def _band_operands(kv_prev, kv_cur):
    band = jnp.concatenate([kv_prev, kv_cur], axis=0)
    low = _low_lanes()
    ops = []
    for cat in (band[:, :KV_WIDTH], band[:, KV_WIDTH:]):
        rol = pltpu.roll(cat, HEAD_DIM, 1)
        ops.append([[jnp.where(low if e == 0 else ~low, cat if g == e else rol, 0.0).astype(BF16) for e in range(2)]
                    for g in range(2)])
    return ops


def _mixer_fwd(proj, lg, lb, wsp, bs_col, sinks, bias, ga, gb, rides=()):
    T = proj.shape[0]
    nb = T // CHUNK

    def body(u_ref, v_ref, q_ref, kvc_ref, kvp_ref, lg_ref, lb_ref, w_ref, bs_ref, sink_ref, bias_ref,
             ga_ref, gb_ref, mixed_ref, mixed_t_ref, ab_ref):
        causal = _causal_mask()
        ssq = jnp.zeros((CHUNK, 1), F32)
        for g in range(A_GROUPS):
            cols = slice(g * CHUNK, (g + 1) * CHUNK)
            wc = jnp.where(causal, w_ref[g], 0.0).astype(BF16)
            ug, _, _, _, mixed = _gate_forward(u_ref[:, cols], v_ref[:, cols], lg_ref[g:g + 1, :], lb_ref[g:g + 1, :],
                                               wc, bs_ref[g])
            a = ug * mixed
            ab_ref[:, cols] = a
            ssq = ssq + jnp.sum(a * a, axis=-1, keepdims=True)
        ra = lax.rsqrt(ssq * (1.0 / A_WIDTH) + EPS)
        mixed_ref[:, :A_WIDTH] = ((ab_ref[:, :A_WIDTH] * ra) * ga_ref[...]).astype(BF16)

        kops, vops = _band_operands(kvp_ref[...], kvc_ref[...])
        ssq = jnp.zeros((CHUNK, 1), F32)
        for g in range(B_HEADS // Q_PER_KV):
            qst = _stack_pairs(q_ref, g).astype(BF16)
            o_st = jnp.zeros((PAIRS * CHUNK, 2 * HEAD_DIM), F32)
            for e in range(2):
                s_all = _dot_nt(qst, kops[g][e])
                ps = []
                for pr in range(PAIRS):
                    h = _head(g, pr, e)
                    p, _ = _softmax_scores(s_all[pr * CHUNK:(pr + 1) * CHUNK], bias_ref[h], sink_ref[0, h])
                    ps.append(p.astype(BF16))
                o_st = o_st + _dot(jnp.concatenate(ps, axis=0), vops[g][e])
            for pr in range(PAIRS):
                o = o_st[pr * CHUNK:(pr + 1) * CHUNK]
                c0 = A_WIDTH + (g * PAIRS + pr) * 2 * HEAD_DIM
                ab_ref[:, c0:c0 + 2 * HEAD_DIM] = o
                ssq = ssq + jnp.sum(o * o, axis=-1, keepdims=True)
        rb = lax.rsqrt(ssq * (1.0 / B_WIDTH) + EPS)
        mixed_ref[:, A_WIDTH:] = ((ab_ref[:, A_WIDTH:] * rb) * gb_ref[...]).astype(BF16)
        mixed_t_ref[...] = mixed_ref[...].T

    full = lambda *shape: pl.BlockSpec(shape, lambda n: (0,) * len(shape))
    return _call(
        body, name="mixer_fwd", grid=(nb,),
        in_specs=[pl.BlockSpec((CHUNK, A_WIDTH), lambda n: (n, 0)),
                  pl.BlockSpec((CHUNK, A_WIDTH), lambda n: (n, 1)),
                  pl.BlockSpec((CHUNK, B_WIDTH), lambda n: (n, 2)),
                  pl.BlockSpec((CHUNK, 2 * KV_WIDTH), lambda n: (n, 12)),
                  pl.BlockSpec((CHUNK, 2 * KV_WIDTH), lambda n: (jnp.maximum(n - 1, 0), 12)),
                  full(A_GROUPS, CHUNK), full(A_GROUPS, CHUNK), full(A_GROUPS, CHUNK, CHUNK), full(A_GROUPS, CHUNK, 1),
                  pl.BlockSpec(memory_space=pltpu.SMEM),
                  pl.BlockSpec((None, B_HEADS, CHUNK, 2 * CHUNK), lambda n: (jnp.minimum(n, 1), 0, 0, 0)),
                  full(1, A_WIDTH), full(1, B_WIDTH)],
        out_specs=[pl.BlockSpec((CHUNK, D_MODEL), lambda n: (n, 0)), pl.BlockSpec((D_MODEL, CHUNK), lambda n: (0, n)),
                   pl.BlockSpec((CHUNK, D_MODEL), lambda n: (n, 0))],
        out_shape=[jax.ShapeDtypeStruct((T, D_MODEL), BF16), jax.ShapeDtypeStruct((D_MODEL, T), BF16),
                   jax.ShapeDtypeStruct((T, D_MODEL), F32)],
        sem=("parallel",), rides=rides,
    )(proj, proj, proj, proj, proj, lg, lb, wsp, bs_col, sinks, bias, ga, gb)


def _gmlp_bwd(proj, ab, dmixed, ga, lg, lb, wsp, bs_col, rides=()):
    T = proj.shape[0]
    nb = T // CHUNK

    def body(u_ref, v_ref, a_ref, dna_ref, ga_ref, lg_ref, lb_ref, w_ref, bs_ref,
             dp_ref, dpt_ref, dga_ref, dw_ref, dbs_ref, dlg_ref, dlb_ref):
        @pl.when(pl.program_id(0) == 0)
        def _():
            for r in (dga_ref, dw_ref, dbs_ref, dlg_ref, dlb_ref):
                r[...] = jnp.zeros_like(r)
        causal = _causal_mask()
        a_all = a_ref[...]
        dna = dna_ref[...]
        da_all, ra = _rms_bwd(dna, a_all, ga_ref[...])
        dga_ref[...] += jnp.sum(dna * (a_all * ra), axis=0, keepdims=True)
        for g in range(A_GROUPS):
            cols = slice(g * CHUNK, (g + 1) * CHUNK)
            wc = jnp.where(causal, w_ref[g], 0.0).astype(BF16)
            lgg = lg_ref[g:g + 1, :]
            u = u_ref[:, cols]
            v = v_ref[:, cols]
            ug, xhat, rstd, vl, mixed = _gate_forward(u, v, lgg, lb_ref[g:g + 1, :], wc, bs_ref[g])
            da = da_all[:, cols]
            dug = da * mixed
            dmg = da * ug
            dmg_b = dmg.astype(BF16)
            dbs_ref[g] += jnp.sum(dmg, axis=-1, keepdims=True)
            dw_ref[g] += jnp.where(causal, _dot_nt(dmg_b, vl), 0.0)
            dvl = _dot_tn(wc, dmg_b)
            dlg_ref[g:g + 1, :] += jnp.sum(dvl * xhat, axis=0, keepdims=True)
            dlb_ref[g:g + 1, :] += jnp.sum(dvl, axis=0, keepdims=True)
            dxh = dvl * lgg
            dvg = rstd * (dxh - jnp.mean(dxh, axis=-1, keepdims=True)
                          - xhat * jnp.mean(dxh * xhat, axis=-1, keepdims=True))
            _, gu = _gelu_and_grad(u)
            _, gv = _gelu_and_grad(v)
            dp_ref[:, cols] = (dug * gu).astype(BF16)
            dp_ref[:, A_WIDTH + g * CHUNK:A_WIDTH + (g + 1) * CHUNK] = (dvg * gv).astype(BF16)
        dpt_ref[...] = dp_ref[...].T

    full = lambda *shape: pl.BlockSpec(shape, lambda n: (0,) * len(shape))
    return _call(
        body, name="gmlp_bwd", grid=(nb,),
        in_specs=[pl.BlockSpec((CHUNK, A_WIDTH), lambda n: (n, 0)),
                  pl.BlockSpec((CHUNK, A_WIDTH), lambda n: (n, 1)),
                  pl.BlockSpec((CHUNK, A_WIDTH), lambda n: (n, 0)),
                  pl.BlockSpec((CHUNK, A_WIDTH), lambda n: (n, 0)),
                  full(1, A_WIDTH), full(A_GROUPS, CHUNK), full(A_GROUPS, CHUNK), full(A_GROUPS, CHUNK, CHUNK),
                  full(A_GROUPS, CHUNK, 1)],
        out_specs=[pl.BlockSpec((CHUNK, 2 * A_WIDTH), lambda n: (n, 0)), pl.BlockSpec((2 * A_WIDTH, CHUNK), lambda n: (0, n)),
                   full(1, A_WIDTH), full(A_GROUPS, CHUNK, CHUNK), full(A_GROUPS, CHUNK, 1),
                   full(A_GROUPS, CHUNK), full(A_GROUPS, CHUNK)],
        out_shape=[jax.ShapeDtypeStruct((T, 2 * A_WIDTH), BF16), jax.ShapeDtypeStruct((2 * A_WIDTH, T), BF16),
                   jax.ShapeDtypeStruct((1, A_WIDTH), F32), jax.ShapeDtypeStruct((A_GROUPS, CHUNK, CHUNK), F32),
                   jax.ShapeDtypeStruct((A_GROUPS, CHUNK, 1), F32), jax.ShapeDtypeStruct((A_GROUPS, CHUNK), F32),
                   jax.ShapeDtypeStruct((A_GROUPS, CHUNK), F32)],
        sem=("arbitrary",), rides=rides,
    )(proj, proj, ab, dmixed, ga, lg, lb, wsp, bs_col)


def _attn_bwd(proj, ab, dmixed, gb, sinks, bias, rides=()):
    T = proj.shape[0]
    nb = T // CHUNK
    qn = lambda n: jnp.minimum(n, nb - 1)

    def body(q_ref, kvc_ref, kvp_ref, o_ref, dnb_ref, gb_ref, sink_ref, bias_ref,
             dq_ref, dkv_ref, dqt_ref, dkvt_ref, dgb_ref, dsink_ref, dbias_ref, carry_ref, sacc_ref):
        n = pl.program_id(0)

        @pl.when(n == 0)
        def _():
            carry_ref[...] = jnp.zeros_like(carry_ref)
            sacc_ref[...] = jnp.zeros_like(sacc_ref)
            dgb_ref[...] = jnp.zeros_like(dgb_ref)
            dbias_ref[...] = jnp.zeros_like(dbias_ref)

        @pl.when(n < nb)
        def _():
            o_all = o_ref[...]
            dnb = dnb_ref[...]
            do_all, rb = _rms_bwd(dnb, o_all, gb_ref[...])
            dgb_ref[...] += jnp.sum(dnb * (o_all * rb), axis=0, keepdims=True)
            kops, vops = _band_operands(kvp_ref[...], kvc_ref[...])
            low = _low_lanes()
            halves = []
            for g in range(B_HEADS // Q_PER_KV):
                qst = _stack_pairs(q_ref, g).astype(BF16)
                dost = _stack_pairs(do_all, g).astype(BF16)
                dq_st = jnp.zeros((PAIRS * CHUNK, 2 * HEAD_DIM), F32)
                dk_e, dv_e = [], []
                for e in range(2):
                    s_all = _dot_nt(qst, kops[g][e])
                    dp_all = _dot_nt(dost, vops[g][e])
                    ps, dsrs = [], []
                    for pr in range(PAIRS):
                        h = _head(g, pr, e)
                        rows = slice(pr * CHUNK, (pr + 1) * CHUNK)
                        p, p_sink = _softmax_scores(s_all[rows], bias_ref[h], sink_ref[0, h])
                        dp = dp_all[rows]
                        delta = jnp.sum(p * dp, axis=-1, keepdims=True)
                        ds = p * (dp - delta)
                        sacc_ref[:, h:h + 1] += -(p_sink * delta)
                        dbias_ref[h] += ds
                        ps.append(p.astype(BF16))
                        dsrs.append((ds * SCALE).astype(BF16))
                    dsr_all = jnp.concatenate(dsrs, axis=0)
                    dq_st = dq_st + _dot(dsr_all, kops[g][e])
                    dk_e.append(_dot_tn(dsr_all, qst))
                    dv_e.append(_dot_tn(jnp.concatenate(ps, axis=0), dost))
                for pr in range(PAIRS):
                    c0 = (g * PAIRS + pr) * 2 * HEAD_DIM
                    dq_ref[:, c0:c0 + 2 * HEAD_DIM] = dq_st[pr * CHUNK:(pr + 1) * CHUNK].astype(BF16)
                halves.append((dk_e, dv_e))
            tiles = []
            for t in range(2):
                g0, g1 = halves[0][t], halves[1][t]
                tiles.append(jnp.where(low, g0[0] + pltpu.roll(g0[1], HEAD_DIM, 1), pltpu.roll(g1[0], HEAD_DIM, 1) + g1[1]))
            dband = jnp.concatenate(tiles, axis=1)
            dkv = (carry_ref[...] + dband[:CHUNK]).astype(BF16)
            dkv_ref[...] = dkv
            dkvt_ref[...] = dkv.T
            dqt_ref[...] = dq_ref[...].T
            carry_ref[...] = dband[CHUNK:]

        @pl.when(n == nb)
        def _():
            dkv = carry_ref[...].astype(BF16)
            dkv_ref[...] = dkv
            dkvt_ref[...] = dkv.T
            dsink_ref[...] = jnp.sum(sacc_ref[...], axis=0, keepdims=True)

    full = lambda *shape: pl.BlockSpec(shape, lambda n: (0,) * len(shape))
    return _call(
        body, name="attn_bwd", grid=(nb + 1,),
        in_specs=[pl.BlockSpec((CHUNK, B_WIDTH), lambda n: (qn(n), 2)),
                  pl.BlockSpec((CHUNK, 2 * KV_WIDTH), lambda n: (qn(n), 12)),
                  pl.BlockSpec((CHUNK, 2 * KV_WIDTH), lambda n: (jnp.maximum(qn(n) - 1, 0), 12)),
                  pl.BlockSpec((CHUNK, B_WIDTH), lambda n: (qn(n), 1)),
                  pl.BlockSpec((CHUNK, B_WIDTH), lambda n: (qn(n), 1)),
                  full(1, B_WIDTH), pl.BlockSpec(memory_space=pltpu.SMEM),
                  pl.BlockSpec((None, B_HEADS, CHUNK, 2 * CHUNK), lambda n: (jnp.minimum(n, 1), 0, 0, 0))],
        out_specs=[pl.BlockSpec((CHUNK, B_WIDTH), lambda n: (qn(n), 0)),
                   pl.BlockSpec((CHUNK, 2 * KV_WIDTH), lambda n: (jnp.maximum(n - 1, 0), 0)),
                   pl.BlockSpec((B_WIDTH, CHUNK), lambda n: (0, qn(n))),
                   pl.BlockSpec((2 * KV_WIDTH, CHUNK), lambda n: (0, jnp.maximum(n - 1, 0))),
                   full(1, B_WIDTH), full(1, B_HEADS), full(B_HEADS, CHUNK, 2 * CHUNK)],
        out_shape=[jax.ShapeDtypeStruct((T, B_WIDTH), BF16), jax.ShapeDtypeStruct((T, 2 * KV_WIDTH), BF16),
                   jax.ShapeDtypeStruct((B_WIDTH, T), BF16), jax.ShapeDtypeStruct((2 * KV_WIDTH, T), BF16),
                   jax.ShapeDtypeStruct((1, B_WIDTH), F32), jax.ShapeDtypeStruct((1, B_HEADS), F32),
                   jax.ShapeDtypeStruct((B_HEADS, CHUNK, 2 * CHUNK), F32)],
        scratch_shapes=[pltpu.VMEM((CHUNK, 2 * KV_WIDTH), F32), pltpu.VMEM((CHUNK, B_HEADS), F32)],
        sem=("arbitrary",), rides=rides,
    )(proj, proj, proj, ab, dmixed, gb, sinks, bias)


def _sq_relu_grad(acc, r):
    return acc * (2.0 * r.astype(F32))


def _chip_index():
    return (2 * lax.axis_index("x") + lax.axis_index("y")).astype(jnp.int32).reshape(1)


def _cast_into_slot(w, *, tm, name):
    _, R, C = w.shape

    def body(me_ref, w_ref, o_ref):
        del me_ref
        o_ref[...] = w_ref[...].astype(BF16)

    return pl.pallas_call(
        body, name=name,
        grid_spec=pltpu.PrefetchScalarGridSpec(
            num_scalar_prefetch=1, grid=(R // tm,),
            in_specs=[pl.BlockSpec((None, tm, C), lambda i, me: (0, i, 0))],
            out_specs=pl.BlockSpec((None, tm, C), lambda i, me: (me[0], i, 0))),
        out_shape=jax.ShapeDtypeStruct((N_CHIPS, R, C), BF16), compiler_params=_params(("parallel",)),
    )(_chip_index(), w)


def _cast_into_slots_carrying(ws, *, steps, name, rides):
    n = len(ws)

    def body(*refs):
        for w_ref, o_ref in zip(refs[:n], refs[n:]):
            o_ref[...] = w_ref[...].astype(BF16)

    me = lambda: 2 * lax.axis_index("x") + lax.axis_index("y")
    return _call(
        body, name=name, grid=(steps,),
        in_specs=[pl.BlockSpec((None, w.shape[1] // steps, w.shape[2]), lambda i: (0, i, 0)) for w in ws],
        out_specs=[pl.BlockSpec((None, w.shape[1] // steps, w.shape[2]), lambda i: (me(), i, 0)) for w in ws],
        out_shape=[jax.ShapeDtypeStruct((N_CHIPS,) + w.shape[1:], BF16) for w in ws], sem=("arbitrary",), rides=rides,
    )(*ws)


def _owner_total(gh, others, *, tm, name):
    _, hr, C = gh.shape

    def body(me_ref, g_ref, o_ref_in, out_ref):
        del me_ref
        acc = g_ref[...]
        for j in range(3):
            acc = acc + o_ref_in[j].astype(F32)
        out_ref[...] = acc

    return pl.pallas_call(
        body, name=name,
        grid_spec=pltpu.PrefetchScalarGridSpec(
            num_scalar_prefetch=1, grid=(hr // tm,),
            in_specs=[pl.BlockSpec((None, tm, C), lambda i, me: (me[0], i, 0)),
                      pl.BlockSpec((3, tm, C), lambda i, me: (0, i, 0))],
            out_specs=pl.BlockSpec((tm, C), lambda i, me: (i, 0))),
        out_shape=jax.ShapeDtypeStruct((hr, C), F32),
        compiler_params=_params(("parallel",)),
    )(_chip_index(), gh, others)


def _adamw_math(w, g, m, v):
    m = ADAM_B1 * m + (1.0 - ADAM_B1) * g
    v = ADAM_B2 * v + (1.0 - ADAM_B2) * (g * g)
    m_hat = m / (1.0 - ADAM_B1 ** ADAM_STEP)
    v_hat = v / (1.0 - ADAM_B2 ** ADAM_STEP)
    delta = -ADAM_LR * (m_hat / (jnp.sqrt(v_hat) + ADAM_EPS) + ADAM_WD * w)
    return delta, m, v


def _adamw_halves(w, own, got, m, v, *, tm, name, rides=()):
    _, R, C = w.shape
    nt = (R // 2) // tm

    def body(w_ref, own_ref, got_ref, m_ref, v_ref, g_ref, d_ref, nm_ref, nv_ref):
        g = jnp.where(pl.program_id(0) == lax.axis_index("c"), own_ref[...], got_ref[...])
        g_ref[...] = g
        d_ref[...], nm_ref[...], nv_ref[...] = _adamw_math(w_ref[...], g, m_ref[...], v_ref[...])

    whole = pl.BlockSpec((None, tm, C), lambda h, i: (0, h * nt + i, 0))
    half = pl.BlockSpec((tm, C), lambda h, i: (i, 0))
    return _call(
        body, name=name, grid=(2, nt), in_specs=[whole, half, half, whole, whole], out_specs=[whole] * 4,
        out_shape=[jax.ShapeDtypeStruct((1, R, C), F32)] * 4, sem=("parallel", "parallel"), rides=rides,
    )(w, own, got, m, v)


def _adamw_small(w, slots, m, v, *, name):
    def body(w_ref, slots_ref, m_ref, v_ref, g_ref, d_ref, nm_ref, nv_ref):
        g = slots_ref[0]
        for d in range(1, N_DEV):
            g = g + slots_ref[d]
        g_ref[...] = g
        d_ref[...], nm_ref[...], nv_ref[...] = _adamw_math(w_ref[...], g, m_ref[...], v_ref[...])

    vmem = pl.BlockSpec(memory_space=pltpu.VMEM)
    return pl.pallas_call(
        body, name=name, in_specs=[vmem] * 4, out_specs=[vmem] * 4,
        out_shape=[jax.ShapeDtypeStruct(w.shape, F32)] * 4, compiler_params=_params(),
    )(w, slots, m, v)


SMALL = ["rel_bias_table", "mix_norm_g", "gate_norm_g", "gate_norm_b", "w_spatial", "b_spatial", "attn_sinks",
         "out_norm_a_g", "out_norm_b_g", "ffn_norm_g", "final_norm_g"]
SMALL_A = ["gate_norm_g", "gate_norm_b", "w_spatial", "b_spatial", "out_norm_a_g"]
SMALL_B = ["rel_bias_table", "mix_norm_g", "attn_sinks", "out_norm_b_g", "ffn_norm_g", "final_norm_g"]
LARGE = ["w_in", "w_out", "w_up", "w_down"]
ROW_TILE = {"w_in": 208, "w_out": 256, "w_up": 256, "w_down": 256}
WEIGHTS = ["rel_bias_table", "mix_norm_g", "w_in", "gate_norm_g", "gate_norm_b", "w_spatial", "b_spatial", "attn_sinks",
           "out_norm_a_g", "out_norm_b_g", "w_out", "ffn_norm_g", "w_up", "w_down", "final_norm_g"]
PACK_UNIT = 8 * 128


def _pack(parts):
    rows = []
    for p in parts:
        flat = p.reshape(-1)
        pad = (-flat.shape[0]) % PACK_UNIT
        rows.append(jnp.pad(flat, (0, pad)).reshape(-1, 128))
    return jnp.concatenate(rows, axis=0)


def _unpack(packed, like):
    out, row = [], 0
    for p in like:
        n = math.prod(p.shape)
        nrows = (n + PACK_UNIT - 1) // PACK_UNIT * 8
        out.append(packed[row:row + nrows].reshape(-1)[:n].reshape(p.shape))
        row += nrows
    return out


def kernel(x, rel_bias_table, mix_norm_g, w_in, gate_norm_g, gate_norm_b, w_spatial, b_spatial, attn_sinks, out_norm_a_g, out_norm_b_g, w_out, ffn_norm_g, w_up, w_down, final_norm_g, loss_target, m_rel_bias_table, m_mix_norm_g, m_w_in, m_gate_norm_g, m_gate_norm_b, m_w_spatial, m_b_spatial, m_attn_sinks, m_out_norm_a_g, m_out_norm_b_g, m_w_out, m_ffn_norm_g, m_w_up, m_w_down, m_final_norm_g, v_rel_bias_table, v_mix_norm_g, v_w_in, v_gate_norm_g, v_gate_norm_b, v_w_spatial, v_b_spatial, v_attn_sinks, v_out_norm_a_g, v_out_norm_b_g, v_w_out, v_ffn_norm_g, v_w_up, v_w_down, v_final_norm_g):
    args = dict(locals())
    wts = {n: args[n] for n in WEIGHTS}
    mom = {n: args["m_" + n] for n in WEIGHTS}
    var = {n: args["v_" + n] for n in WEIGHTS}
    sp = {n: wts[n] for n in SMALL}
    x2, tgt = x[0], loss_target[0]
    T = x2.shape[0]
    tm = min(512, T)
    tl = min(1024, T)
    lg = sp["gate_norm_g"].reshape(A_GROUPS, CHUNK)
    lb = sp["gate_norm_b"].reshape(A_GROUPS, CHUNK)
    wsp = sp["w_spatial"].reshape(A_GROUPS, CHUNK, CHUNK)
    bs_col = sp["b_spatial"].reshape(A_GROUPS, CHUNK, 1)
    sinks = sp["attn_sinks"].reshape(1, B_HEADS)
    ga = sp["out_norm_a_g"].reshape(1, A_WIDTH)
    gb = sp["out_norm_b_g"].reshape(1, B_WIDTH)
    g1 = sp["mix_norm_g"].reshape(1, D_MODEL)
    g2 = sp["ffn_norm_g"].reshape(1, D_MODEL)
    gf = sp["final_norm_g"].reshape(1, D_MODEL)

    def owner_total(n, gh, others):
        return _owner_total(gh, others, tm=ROW_TILE[n], name="rs_owner_total_" + n)

    def halves_view(at, shards):
        return at.reshape(shards, 2, at.shape[0] // shards // 2, at.shape[1])

    for d in (wts, mom, var):
        d["w_in"] = jnp.swapaxes(d["w_in"], 1, 2)

    s_in = _cast_into_slot(wts["w_in"], tm=ROW_TILE["w_in"], name="cast_w_in")
    (s_out, s_up, s_down), ((g_in,),) = _cast_into_slots_carrying(
        [wts["w_out"], wts["w_up"], wts["w_down"]], steps=8, name="cast_w_rest", rides=[_ride_gather(s_in, direct=(0, 1, 1))])
    win_t = g_in.reshape(PROJ_WIDTH, D_MODEL)
    bias = _bias_build(sp["rel_bias_table"])
    (n1, proj), ((g_out,), (s_up,)) = _norm_matmul_wide(
        x2, g1, win_t, tm=tm, tn=PROJ_WIDTH // 2, name="in_proj",
        rides=[_ride_gather(s_out, direct=(0, 1, 1)), _ride_gather(s_up, s1=(0, 3, 8))])
    wo = g_out.reshape(A_WIDTH + B_WIDTH, D_MODEL)
    (mixed, mixed_t, ab), ((s_up,), (s_down,), (n1_sib,)) = _mixer_fwd(
        proj, lg, lb, wsp, bs_col, sinks, bias, ga, gb,
        rides=[_ride_gather(s_up, s2=(0, 3, 8), s1=(3, 8, 8)), _ride_gather(s_down, s1=(0, 3, 8)),
               _ride_to_sibling(n1, first=True)])
    mixed_t = halves_view(mixed_t, N_CHIPS)
    h1, ((wu,), (s_down,), (mixed_t_sib,)) = _matmul_res(
        mixed, wo, x2, tm=tl, tn=1024, tk=D_MODEL, prologue=_to_bf16, name="out_proj",
        rides=[_ride_gather(s_up, s3=(0, 3, 8), tail=(3, 8, 8), mid_frac=0.75), _ride_gather(s_down, s2=(0, 3, 8)),
               _ride_to_sibling(mixed_t, halves=True)])
    (n2t, zp, z2, z2t), ((g_down,),) = _norm_matmul_sq(
        h1, g2, wu, tm=tl, tn=1024, name="up_proj", rides=[_ride_gather(s_down, s3=(0, 3, 8), direct=(3, 8, 8))])
    wd = g_down.reshape(D_FF, D_MODEL)
    n2t, z2t = halves_view(n2t, 1), halves_view(z2t, N_CHIPS)
    h2, ((n2t_sib,), (z2t_sib,)) = _matmul_res(
        z2, wd, h1, tm=tl, tn=1024, tk=4096, prologue=_to_bf16, name="down_proj",
        rides=[_ride_to_sibling(n2t, halves=True), _ride_to_sibling(z2t, halves=True)])

    dh2, dh2b, dgf, loss = _loss_bwd(h2, tgt, gf, tm=tm)
    dzp, ((dh2b_sib,),) = _matmul_nt(dh2b, wd, tm=tl, tn=1024, tk=D_MODEL, name="bwd_dz", extra=zp,
                                     epilogue=_sq_relu_grad, out_dtype=BF16, rides=[_ride_to_sibling(dh2b)])
    (gd, gdb), ((dzp_sib,),) = _grad_pair(z2t, z2t_sib, dh2b, dh2b_sib, cols_sharded=False, tmo=1024, tk=tl,
                                          name="grad_w_down", rides=[_ride_to_sibling(dzp)])
    (gu, gub), ((o_d,),) = _grad_pair(n2t, n2t_sib, dzp, dzp_sib, cols_sharded=True, tmo=1024, tk=tl,
                                      name="grad_w_up", rides=[_ride_scatter(gdb, None, (0, 7, 8))])
    dn2, ((o_d,), (o_u,)) = _matmul_nt(dzp, wu, tm=tl, tn=1024, tk=4096, name="bwd_dn2",
                                       rides=[_ride_scatter(gdb, o_d, (7, 8, 8)), _ride_scatter(gub, None, (0, 6, 8))])
    h_d = owner_total("w_down", gd, o_d)
    (dh1, dh1b, dg2), ((o_u,),) = _rms_bwd_res(dn2, h1, g2, dh2, tm=tm, name="ffn_norm_bwd",
                                               rides=[_ride_scatter(gub, o_u, (6, 7, 8))])
    dmixed, ((o_u,), (dh1b_sib,), (w_d,)) = _matmul_nt(
        dh1b, wo, tm=tl, tn=1024, tk=D_MODEL, name="bwd_dmixed",
        rides=[_ride_scatter(gub, o_u, (7, 8, 8)), _ride_to_sibling(dh1b), _ride_swap(h_d)])
    h_u = owner_total("w_up", gu, o_u)
    (go, gob), ((w_u,),) = _grad_pair_merged(mixed_t, mixed_t_sib, dh1b, dh1b_sib, tk=tl, name="grad_w_out",
                                             rides=[_ride_swap(h_u)])
    (duv, duv_t, dga, dwsp, dbs, dlg, dlb), ((o_o,),) = _gmlp_bwd(proj, ab, dmixed, ga, lg, lb, wsp, bs_col,
                                                                  rides=[_ride_scatter(gob)])
    h_o = owner_total("w_out", go, o_o)
    small = {"gate_norm_g": dlg, "gate_norm_b": dlb, "w_spatial": dwsp, "b_spatial": dbs, "out_norm_a_g": dga}
    hr_in = PROJ_WIDTH // N_CHIPS // 2
    (dq, dkv, dq_t, dkv_t, dgb, dsinks, dbias), ((w_o,), (dproj_t_sib,)) = _attn_bwd(
        proj, ab, dmixed, gb, sinks, bias, rides=[_ride_swap(h_o), _ride_rows_to_sibling(duv_t, hr_in, 2, N_CHIPS)])
    dtable = _bias_grad(dbias)
    dproj_t = halves_view(jnp.concatenate([duv_t, dq_t, dkv_t], axis=0), N_CHIPS)
    ((dproj_t_sib,),) = _carrier([_ride_to_sibling(dproj_t, halves=True, shards=(2, N_CHIPS), land=dproj_t_sib)],
                                 name="trade_dproj_t")
    (gi, gib), ((slots_a,),) = _grad_pair(
        dproj_t, dproj_t_sib, n1, n1_sib, cols_sharded=False, tmo=hr_in, tk=tl, name="grad_w_in",
        rides=[_ride_small_to_all(_pack([small[n] for n in SMALL_A]))])
    dn1, ((o_i,),) = _matmul_parts([duv, dq, dkv], win_t, tm=tl, tn=1024, name="bwd_dn1", rides=[_ride_scatter(gib)])
    h_i = owner_total("w_in", gi, o_i)
    dx, dg1 = _rms_bwd_res(dn1, x2, g1, dh1, tm=tm, name="mix_norm_bwd", bf16_copy=False)
    small.update({"rel_bias_table": dtable.reshape(N_BUCKETS, B_HEADS), "mix_norm_g": dg1, "attn_sinks": dsinks,
                  "out_norm_b_g": dgb, "ffn_norm_g": dg2, "final_norm_g": dgf})
    (w_i,), (slots_b,) = _carrier([_ride_swap(h_i), _ride_small_to_all(_pack([small[n] for n in SMALL_B] + [loss]))],
                                  name="swap_w_in")

    out_g, out_d, out_m, out_v = {}, {}, {}, {}
    for n, h, s in zip(LARGE, [h_i, h_o, h_u, h_d], [w_i, w_o, w_u, w_d]):
        res = _adamw_halves(wts[n], h, s, mom[n], var[n], tm=ROW_TILE[n], name="adamw_" + n)
        if n == "w_in":
            res = [jnp.swapaxes(r, 1, 2) for r in res]
        out_g[n], out_d[n], out_m[n], out_v[n] = res
    for names, slots, tag in ((SMALL_A, slots_a, "a"), (SMALL_B, slots_b, "b")):
        extra = [jnp.zeros((1, 1), F32)] if tag == "b" else []
        like = [wts[n] for n in names] + extra
        res = _adamw_small(_pack(like), slots, _pack([mom[n] for n in names] + extra),
                           _pack([var[n] for n in names] + extra), name="adamw_small_" + tag)
        for store, packed in zip((out_g, out_d, out_m, out_v), res):
            for n, val in zip(names + ["loss"], _unpack(packed, like)):
                store[n] = val

    total = out_g["loss"][0, 0]
    return (total, dx[None], *[out_g[n] for n in WEIGHTS], *[out_d[n] for n in WEIGHTS],
            *[out_m[n] for n in WEIGHTS], *[out_v[n] for n in WEIGHTS])
```

```python
import math

import numpy as np
import jax
import jax.numpy as jnp
from jax import lax
from jax.experimental import pallas as pl
from jax.experimental.pallas import tpu as pltpu

F32 = jnp.float32
BF16 = jnp.bfloat16

D_MODEL = 2048
CHUNK = 128
A_GROUPS = 8
A_WIDTH = 1024
HEAD_DIM = 64
B_HEADS = 16
Q_PER_KV = 8
B_WIDTH = 1024
KV_WIDTH = 128
PROJ_WIDTH = 3328
D_FF = 8192
N_BUCKETS = 32
EPS = 1e-5
NEG = -1e30
SCALE = HEAD_DIM ** -0.5
N_CHIPS = 4
N_DEV = 8

ADAM_LR = 0.001
ADAM_B1 = 0.9
ADAM_B2 = 0.999
ADAM_EPS = 1e-08
ADAM_WD = 0.01
ADAM_STEP = 10

VMEM_LIMIT = 60 * 1024 * 1024
MESH = pl.DeviceIdType.MESH


def _bucket_thresholds():
    d = np.arange(CHUNK)
    n_exact = N_BUCKETS // 2
    relf = np.maximum(d, n_exact).astype(np.float64)
    large = n_exact + (np.log(relf / n_exact) / math.log(CHUNK / n_exact) * (N_BUCKETS - n_exact)).astype(np.int32)
    bucket = np.where(d < n_exact, d, np.minimum(large, N_BUCKETS - 1))
    return [int(np.min(d[bucket >= b])) for b in range(1, N_BUCKETS)]


BUCKET_THR = _bucket_thresholds()


def _params(sem=None):
    return pltpu.CompilerParams(dimension_semantics=sem, vmem_limit_bytes=VMEM_LIMIT)


def _gelu(x):
    c = math.sqrt(2.0 / math.pi)
    return 0.5 * x * (1.0 + jnp.tanh(c * (x + 0.044715 * (x * x * x))))


def _gelu_and_grad(x):
    c = math.sqrt(2.0 / math.pi)
    x2 = x * x
    t = jnp.tanh(c * (x + 0.044715 * (x2 * x)))
    g = 0.5 * x * (1.0 + t)
    dg = 0.5 * (1.0 + t) + 0.5 * x * (1.0 - t * t) * (c * (1.0 + 3.0 * 0.044715 * x2))
    return g, dg


def _dot(a, b):
    return jnp.dot(a, b, preferred_element_type=F32)


def _dot_nt(a, b):
    return lax.dot_general(a, b, (((1,), (1,)), ((), ())), preferred_element_type=F32)


def _dot_tn(a, b):
    return lax.dot_general(a, b, (((0,), (0,)), ((), ())), preferred_element_type=F32)


def _rms_bwd(dn, h, g):
    r = lax.rsqrt(jnp.mean(h * h, axis=-1, keepdims=True) + EPS)
    w = dn * g
    dh = r * w - h * ((r * r * r) * jnp.mean(w * h, axis=-1, keepdims=True))
    return dh, r


def _place():
    x, y, c = lax.axis_index("x"), lax.axis_index("y"), lax.axis_index("c")
    chips = [(1 - x, y), (x, 1 - y), (1 - x, 1 - y)]
    return x, y, c, chips


def _remote(src, dst, send_sem, recv_sem, to):
    return pltpu.make_async_remote_copy(src_ref=src, dst_ref=dst, send_sem=send_sem, recv_sem=recv_sem,
                                        device_id=to, device_id_type=MESH)


class _Ride:
    def __init__(self, args, out_shape, n_sem, start, finish, mids=(), aliases=None):
        self.args, self.out_shape, self.n_sem = list(args), list(out_shape), n_sem
        self.start, self.mids, self.finish = start, list(mids), finish
        self.aliases = dict(aliases or {})


def _call(body, *, name, grid, in_specs, out_specs, out_shape, scratch_shapes=(), sem=None, rides=()):
    single = not isinstance(out_shape, (list, tuple))
    out_specs = [out_specs] if single else list(out_specs)
    out_shape = [out_shape] if single else list(out_shape)
    n_in, n_out, n_scr = len(in_specs), len(out_shape), len(scratch_shapes)
    r_in = [len(r.args) for r in rides]
    r_out = [len(r.out_shape) for r in rides]
    any_spec = pl.BlockSpec(memory_space=pl.ANY)
    aliases, off_i, off_o = {}, n_in, n_out
    for r in rides:
        for i, o in r.aliases.items():
            aliases[off_i + i] = off_o + o
        off_i += len(r.args)
        off_o += len(r.out_shape)
    steps = math.prod(grid)

    def wrapped(*refs):
        p = 0
        ins = refs[p:p + n_in]; p += n_in
        rins = refs[p:p + sum(r_in)]; p += sum(r_in)
        outs = refs[p:p + n_out]; p += n_out
        routs = refs[p:p + sum(r_out)]; p += sum(r_out)
        scr = refs[p:p + n_scr]; p += n_scr
        sems = refs[p:]
        parts, pi, po = [], 0, 0
        for k, r in enumerate(rides):
            parts.append((rins[pi:pi + r_in[k]], routs[po:po + r_out[k]], sems[2 * k], sems[2 * k + 1]))
            pi += r_in[k]
            po += r_out[k]
        lin = 0
        for d in range(len(grid)):
            lin = lin * grid[d] + pl.program_id(d)
        if rides:
            @pl.when(lin == 0)
            def _():
                for r, part in zip(rides, parts):
                    r.start(*part)
        body(*ins, *outs, *scr)
        for r, part in zip(rides, parts):
            for frac, fn in r.mids:
                @pl.when(lin == min(steps - 1, int(frac * steps)))
                def _(fn=fn, part=part):
                    fn(*part)
        if rides:
            @pl.when(lin == steps - 1)
            def _():
                for r, part in zip(rides, parts):
                    r.finish(*part)

    scratch = list(scratch_shapes)
    for r in rides:
        scratch += [pltpu.SemaphoreType.DMA((r.n_sem,)), pltpu.SemaphoreType.DMA((r.n_sem,))]
    if rides:
        sem = ("arbitrary",) * len(grid)
    res = pl.pallas_call(
        wrapped, name=name, grid=grid,
        in_specs=list(in_specs) + [any_spec] * sum(r_in),
        out_specs=out_specs + [any_spec] * sum(r_out),
        out_shape=out_shape + [s for r in rides for s in r.out_shape],
        scratch_shapes=scratch, input_output_aliases=aliases,
        compiler_params=_params(sem),
    )

    def run(*args):
        got = res(*args, *[a for r in rides for a in r.args])
        mine = got[0] if single else list(got[:n_out])
        if not rides:
            return mine
        rest, out = list(got[n_out:]), []
        for k in range(len(rides)):
            out.append(rest[:r_out[k]])
            rest = rest[r_out[k]:]
        return mine, out

    return run


def _ride_gather(slot, s1=None, s2=None, s3=None, tail=None, chain=None, mid_frac=0.6, chain_fracs=(0.35, 0.7)):
    half = slot.shape[1] // 2

    def rows(part, c, which=None):
        k0, k1, n = part
        count, first = (k1 - k0) * (half // n), c * half + k0 * (half // n)
        return pl.ds(first, count) if which is None else pl.ds(first + which * (count // 2), count // 2)

    def ids():
        x, y, c, _ = _place()
        return x, y, c, 2 * x + y, 2 * (1 - x) + y, 2 * x + (1 - y), 2 * (1 - x) + (1 - y)

    def copy(full, chip, r, ss, rs, k, to):
        piece = full.at[chip, r, :]
        return _remote(piece, piece, ss.at[k], rs.at[k], to)

    def to_neighbours(full, ss, rs, part, base):
        x, y, c, me, _, _, _ = ids()
        return [copy(full, me, rows(part, c), ss, rs, base, (1 - x, y, c)),
                copy(full, me, rows(part, c), ss, rs, base + 1, (x, 1 - y, c))]

    def from_neighbours(full, ss, rs, part, base):
        x, y, c, _, cx, cy, _ = ids()
        return [copy(full, cx, rows(part, c), ss, rs, base, (x, y, c)), copy(full, cy, rows(part, c), ss, rs, base + 1, (x, y, c))]

    def onward(full, ss, rs, part, base):
        x, y, c, _, cx, cy, _ = ids()
        return [copy(full, cx, rows(part, c, 0), ss, rs, base, (x, 1 - y, c)),
                copy(full, cy, rows(part, c, 1), ss, rs, base + 1, (1 - x, y, c))]

    def from_onward(full, ss, rs, part, base):
        x, y, c, _, _, _, cd = ids()
        return [copy(full, cd, rows(part, c, 0), ss, rs, base, (x, y, c)), copy(full, cd, rows(part, c, 1), ss, rs, base + 1, (x, y, c))]

    def to_sibling(full, ss, rs, part, base, diagonal):
        x, y, c, _, cx, cy, cd = ids()
        return [copy(full, chip, rows(part, c), ss, rs, base + j, (x, y, 1 - c))
                for j, chip in enumerate([cd] if diagonal else [cx, cy])]

    def from_sibling(full, ss, rs, part, base, diagonal):
        x, y, c, _, cx, cy, cd = ids()
        return [copy(full, chip, rows(part, 1 - c), ss, rs, base + j, (x, y, c))
                for j, chip in enumerate([cd] if diagonal else [cx, cy])]

    def start(ins, outs, ss, rs):
        full, cps = outs[0], []
        for part, base in ((s1, 0), (chain, 12)):
            if part is not None:
                cps += to_neighbours(full, ss, rs, part, base)
        for part, b_ici, b_sib in ((s2, 2, 4), (tail, 7, 9)):
            if part is not None:
                cps += onward(full, ss, rs, part, b_ici) + to_sibling(full, ss, rs, part, b_sib, False)
        if s3 is not None:
            cps += to_sibling(full, ss, rs, s3, 6, True)
        for cp in cps:
            cp.start()

    def second(part, b_in, b_ici, b_sib):
        def fn(ins, outs, ss, rs):
            for cp in from_neighbours(outs[0], ss, rs, part, b_in):
                cp.wait_recv()
            for cp in onward(outs[0], ss, rs, part, b_ici) + to_sibling(outs[0], ss, rs, part, b_sib, False):
                cp.start()
        return fn

    def third(part, b_ici, b_sib):
        def fn(ins, outs, ss, rs):
            for cp in from_onward(outs[0], ss, rs, part, b_ici):
                cp.wait_recv()
            for cp in to_sibling(outs[0], ss, rs, part, b_sib, True):
                cp.start()
        return fn

    mids = []
    if tail is not None:
        mids.append((mid_frac, third(tail, 7, 11)))
    if chain is not None:
        mids += [(chain_fracs[0], second(chain, 12, 14, 16)), (chain_fracs[1], third(chain, 14, 18))]

    def finish(ins, outs, ss, rs):
        full, got, sent = outs[0], [], []
        if s1 is not None:
            got += from_neighbours(full, ss, rs, s1, 0)
            sent += to_neighbours(full, ss, rs, s1, 0)
        if s2 is not None:
            got += from_onward(full, ss, rs, s2, 2) + from_sibling(full, ss, rs, s2, 4, False)
            sent += onward(full, ss, rs, s2, 2) + to_sibling(full, ss, rs, s2, 4, False)
        if s3 is not None:
            got += from_sibling(full, ss, rs, s3, 6, True)
            sent += to_sibling(full, ss, rs, s3, 6, True)
        if tail is not None:
            got += from_sibling(full, ss, rs, tail, 9, False) + from_sibling(full, ss, rs, tail, 11, True)
            sent += onward(full, ss, rs, tail, 7) + to_sibling(full, ss, rs, tail, 9, False) + to_sibling(full, ss, rs, tail, 11, True)
        if chain is not None:
            got += from_sibling(full, ss, rs, chain, 16, False) + from_sibling(full, ss, rs, chain, 18, True)
            sent += (to_neighbours(full, ss, rs, chain, 12) + onward(full, ss, rs, chain, 14)
                     + to_sibling(full, ss, rs, chain, 16, False) + to_sibling(full, ss, rs, chain, 18, True))
        for cp in got:
            cp.wait_recv()
        for cp in sent:
            cp.wait_send()

    return _Ride([slot], [jax.ShapeDtypeStruct(slot.shape, slot.dtype)], 19, start, finish, mids=mids, aliases={0: 0})


def _ride_scatter(q, land=None, part=(0, 1)):
    k0, k1, n = part if len(part) == 3 else (part[0], part[0] + 1, part[1])
    rows_n = q.shape[1] // n
    rows = pl.ds(k0 * rows_n, (k1 - k0) * rows_n)

    def copies(ins, outs, ss, rs):
        x, y, c, chips = _place()
        return [_remote(ins[0].at[2 * chip[0] + chip[1], rows, :], outs[0].at[j, rows, :], ss.at[j], rs.at[j], (*chip, c))
                for j, chip in enumerate(chips)]

    def start(*a):
        for cp in copies(*a):
            cp.start()

    def finish(*a):
        for cp in copies(*a):
            cp.wait()

    shape = jax.ShapeDtypeStruct((3,) + q.shape[1:], q.dtype)
    if land is None:
        return _Ride([q], [shape], 3, start, finish)
    return _Ride([q, land], [shape], 3, start, finish, aliases={1: 0})


def _ride_to_sibling(a, halves=False, first=False, shards=None, land=None):
    s0, s1 = shards or (0, a.shape[0])

    def copy(ins, outs, ss, rs):
        x, y, c, _ = _place()
        if halves:
            src, dst = ins[0].at[s0:s1, 1 - c], outs[0].at[s0:s1]
        else:
            src, dst = (ins[0].at[0] if first else ins[0]), outs[0]
        return _remote(src, dst, ss.at[0], rs.at[0], (x, y, 1 - c))

    shape = (a.shape[0],) + a.shape[2:] if halves else (a.shape[1:] if first else a.shape)
    return _Ride([a] if land is None else [a, land], [jax.ShapeDtypeStruct(shape, a.dtype)], 1,
                 lambda *a_: copy(*a_).start(), lambda *a_: copy(*a_).wait(), aliases=None if land is None else {1: 0})


def _ride_rows_to_sibling(a, hr, shards, total):
    def copies(ins, outs, ss, rs):
        x, y, c, _ = _place()
        return [_remote(ins[0].at[pl.ds((2 * s + 1 - c) * hr, hr), :], outs[0].at[s], ss.at[s], rs.at[s], (x, y, 1 - c))
                for s in range(shards)]

    def start(*a_):
        for cp in copies(*a_):
            cp.start()

    def finish(*a_):
        for cp in copies(*a_):
            cp.wait()

    return _Ride([a], [jax.ShapeDtypeStruct((total, hr, a.shape[1]), a.dtype)], shards, start, finish)


def _ride_swap(h):
    def copy(ins, outs, ss, rs):
        x, y, c, _ = _place()
        return _remote(ins[0], outs[0], ss.at[0], rs.at[0], (x, y, 1 - c))

    return _Ride([h], [jax.ShapeDtypeStruct(h.shape, h.dtype)], 1,
                 lambda *a: copy(*a).start(), lambda *a: copy(*a).wait())


def _mesh_place(p):
    return (p // 4, (p // 2) % 2, p % 2)


def _ride_small_to_all(packed):
    def copies(ins, outs, ss, rs):
        x, y, c, _ = _place()
        me = 4 * x + 2 * y + c
        return [_remote(ins[0], outs[0].at[me], ss.at[k - 1], rs.at[k - 1], _mesh_place((me + k) % N_DEV))
                for k in range(1, N_DEV)]

    def own(ins, outs, ss, rs):
        x, y, c, _ = _place()
        return pltpu.make_async_copy(ins[0], outs[0].at[4 * x + 2 * y + c], ss.at[N_DEV - 1])

    def start(*a):
        own(*a).start()
        for cp in copies(*a):
            cp.start()

    def finish(ins, outs, ss, rs):
        x, y, c, _ = _place()
        me = 4 * x + 2 * y + c
        for k in range(1, N_DEV):
            _remote(ins[0], outs[0].at[(me + N_DEV - k) % N_DEV], ss.at[k - 1], rs.at[k - 1], (x, y, c)).wait_recv()
        for cp in copies(ins, outs, ss, rs):
            cp.wait_send()
        own(ins, outs, ss, rs).wait()

    return _Ride([packed], [jax.ShapeDtypeStruct((N_DEV,) + packed.shape, packed.dtype)], N_DEV, start, finish)


def _carrier(rides, *, name):
    _, outs = _call(lambda: None, name=name, grid=(1,), in_specs=[], out_specs=[], out_shape=[], rides=rides)()
    return outs


def _norm_bf16(a_ref, g_ref):
    xf = a_ref[...]
    r = lax.rsqrt(jnp.mean(xf * xf, axis=-1, keepdims=True) + EPS)
    return ((xf * r) * g_ref[...]).astype(BF16)


def _norm_matmul_wide(a, g, b, *, tm, tn, name, rides=()):
    T, K = a.shape
    N = b.shape[0]

    def body(a_ref, g_ref, b_ref, n_ref, o_ref):
        n = _norm_bf16(a_ref, g_ref)
        n_ref[...] = n
        o_ref[...] = _dot_nt(n, b_ref[...])

    return _call(
        body, name=name, grid=(N // tn, T // tm),
        in_specs=[pl.BlockSpec((tm, K), lambda j, i: (i, 0)), pl.BlockSpec((1, K), lambda j, i: (0, 0)),
                  pl.BlockSpec((tn, K), lambda j, i: (j, 0))],
        out_specs=[pl.BlockSpec((None, tm, K), lambda j, i: (j, i, 0)), pl.BlockSpec((tm, tn), lambda j, i: (i, j))],
        out_shape=[jax.ShapeDtypeStruct((N // tn, T, K), BF16), jax.ShapeDtypeStruct((T, N), F32)],
        sem=("arbitrary", "arbitrary"), rides=rides,
    )(a, g, b)


def _norm_matmul_sq(a, g, b, *, tm, tn, name, rides=()):
    T, K = a.shape
    per = b.shape[2] // tn
    N = b.shape[0] * b.shape[2]

    def body(a_ref, g_ref, b_ref, nt_ref, o_ref, z_ref, zt_ref, n_scr):
        @pl.when(pl.program_id(1) == 0)
        def _():
            n = _norm_bf16(a_ref, g_ref)
            n_scr[...] = n
            nt_ref[...] = n.T
        r = jnp.maximum(_dot(n_scr[...], b_ref[...]), 0.0)
        o_ref[...] = r.astype(BF16)
        z = (r * r).astype(BF16)
        z_ref[...] = z
        zt_ref[...] = z.T

    return _call(
        body, name=name, grid=(T // tm, N // tn),
        in_specs=[pl.BlockSpec((tm, K), lambda i, j: (i, 0)), pl.BlockSpec((1, K), lambda i, j: (0, 0)),
                  pl.BlockSpec((None, K, tn), lambda i, j: (j // per, 0, j % per))],
        out_specs=[pl.BlockSpec((K, tm), lambda i, j: (0, i)), pl.BlockSpec((tm, tn), lambda i, j: (i, j)),
                   pl.BlockSpec((tm, tn), lambda i, j: (i, j)), pl.BlockSpec((tn, tm), lambda i, j: (j, i))],
        out_shape=[jax.ShapeDtypeStruct((K, T), BF16), jax.ShapeDtypeStruct((T, N), BF16),
                   jax.ShapeDtypeStruct((T, N), BF16), jax.ShapeDtypeStruct((N, T), BF16)],
        scratch_shapes=[pltpu.VMEM((tm, K), BF16)],
        sem=("parallel", "arbitrary"), rides=rides,
    )(a, g, b)


def _grad_pair(at, at_sib, b, b_sib, *, cols_sharded, tmo, tk, name, rides=()):
    S, _, hr, T = at.shape
    C = b.shape[-1] // N_CHIPS if cols_sharded else b.shape[-1]
    nk = T // tk
    a_sel = (lambda s: 0) if cols_sharded else (lambda s: s)
    b_sel = (lambda s: s) if cols_sharded else (lambda s: 0)
    if b.ndim == 3:
        b_spec = pl.BlockSpec((None, tk, C), lambda s, i, k: (0, k, b_sel(s)))
    else:
        b_spec = pl.BlockSpec((tk, C), lambda s, i, k: (k, b_sel(s)))

    def body(a_ref, as_ref, b_ref, bs_ref, o_ref, ob_ref):
        k = pl.program_id(2)
        p = _dot(a_ref[...], b_ref[...]) + _dot(as_ref[...], bs_ref[...])

        @pl.when(k == 0)
        def _():
            o_ref[...] = p

        @pl.when(k > 0)
        def _():
            o_ref[...] += p

        @pl.when(k == nk - 1)
        def _():
            ob_ref[...] = o_ref[...].astype(BF16)

    out = pl.BlockSpec((None, tmo, C), lambda s, i, k: (s, i, 0))
    return _call(
        body, name=name, grid=(N_CHIPS, hr // tmo, nk),
        in_specs=[pl.BlockSpec((None, None, tmo, tk), lambda s, i, k: (a_sel(s), lax.axis_index("c"), i, k)),
                  pl.BlockSpec((None, tmo, tk), lambda s, i, k: (a_sel(s), i, k)),
                  b_spec, pl.BlockSpec((tk, C), lambda s, i, k: (k, b_sel(s)))],
        out_specs=[out, out],
        out_shape=[jax.ShapeDtypeStruct((N_CHIPS, hr, C), F32), jax.ShapeDtypeStruct((N_CHIPS, hr, C), BF16)],
        sem=("parallel", "parallel", "arbitrary"), rides=rides,
    )(at, at_sib, b, b_sib)


def _grad_pair_merged(at, at_sib, b, b_sib, *, tk, name, rides=()):
    S, _, hr, T = at.shape
    C = b.shape[-1]
    nk = T // tk

    def body(a_ref, as_ref, b_ref, bs_ref, o_ref, ob_ref):
        k = pl.program_id(0)
        p = (_dot(a_ref[...].reshape(S * hr, tk), b_ref[...])
             + _dot(as_ref[...].reshape(S * hr, tk), bs_ref[...])).reshape(S, hr, C)

        @pl.when(k == 0)
        def _():
            o_ref[...] = p

        @pl.when(k > 0)
        def _():
            o_ref[...] += p

        @pl.when(k == nk - 1)
        def _():
            ob_ref[...] = o_ref[...].astype(BF16)

    out = pl.BlockSpec((S, hr, C), lambda k: (0, 0, 0))
    return _call(
        body, name=name, grid=(nk,),
        in_specs=[pl.BlockSpec((S, None, hr, tk), lambda k: (0, lax.axis_index("c"), 0, k)),
                  pl.BlockSpec((S, hr, tk), lambda k: (0, 0, k)),
                  pl.BlockSpec((tk, C), lambda k: (k, 0)), pl.BlockSpec((tk, C), lambda k: (k, 0))],
        out_specs=[out, out],
        out_shape=[jax.ShapeDtypeStruct((S, hr, C), F32), jax.ShapeDtypeStruct((S, hr, C), BF16)],
        sem=("arbitrary",), rides=rides,
    )(at, at_sib, b, b_sib)


def _matmul_parts(parts, b, *, tm, tn, name, rides=()):
    T = parts[0].shape[0]
    N = b.shape[1]
    offs = [sum(p.shape[1] for p in parts[:i]) for i in range(len(parts))]
    assert all(o % p.shape[1] == 0 for o, p in zip(offs, parts))

    def body(*refs):
        n = len(parts)
        acc = _dot(refs[0][...], refs[n][...])
        for i in range(1, n):
            acc = acc + _dot(refs[i][...], refs[n + i][...])
        refs[-1][...] = acc

    a_specs = [pl.BlockSpec((tm, p.shape[1]), lambda i, j: (i, 0)) for p in parts]
    b_specs = [pl.BlockSpec((p.shape[1], tn), lambda i, j, r=o // p.shape[1]: (r, j)) for o, p in zip(offs, parts)]
    return _call(
        body, name=name, grid=(T // tm, N // tn), in_specs=a_specs + b_specs,
        out_specs=pl.BlockSpec((tm, tn), lambda i, j: (i, j)), out_shape=jax.ShapeDtypeStruct((T, N), F32),
        sem=("parallel", "parallel"), rides=rides,
    )(*parts, *([b] * len(parts)))


def _to_bf16(v):
    return v.astype(BF16)


def _matmul_res(a, b, res, *, tm, tn, tk, prologue, name, rides=()):
    T, K = a.shape
    N = b.shape[1]

    def body(a_ref, b_ref, res_ref, o_ref):
        k = pl.program_id(2)
        p = _dot(prologue(a_ref[...]), b_ref[...])

        @pl.when(k == 0)
        def _():
            o_ref[...] = res_ref[...] + p

        @pl.when(k > 0)
        def _():
            o_ref[...] += p

    return _call(
        body, name=name, grid=(T // tm, N // tn, K // tk),
        in_specs=[pl.BlockSpec((tm, tk), lambda i, j, k: (i, k)), pl.BlockSpec((tk, tn), lambda i, j, k: (k, j)),
                  pl.BlockSpec((tm, tn), lambda i, j, k: (i, j))],
        out_specs=pl.BlockSpec((tm, tn), lambda i, j, k: (i, j)),
        out_shape=jax.ShapeDtypeStruct((T, N), F32),
        sem=("parallel", "parallel", "arbitrary"), rides=rides,
    )(a, b, res)


def _matmul_nt(a, b, *, tm, tn, tk, name, extra=None, epilogue=None, out_dtype=F32, rides=()):
    T, K = a.shape
    two = b.ndim == 3 and tk == 2 * b.shape[2]
    if two:
        N, ks = b.shape[1], b.shape[2]
        b_specs = [pl.BlockSpec((None, tn, ks), lambda i, j, k: (2 * k, j, 0)),
                   pl.BlockSpec((None, tn, ks), lambda i, j, k: (2 * k + 1, j, 0))]
    elif b.ndim == 3:
        per = b.shape[2] // tk
        N = b.shape[1]
        b_specs = [pl.BlockSpec((None, tn, tk), lambda i, j, k: (k // per, j, k % per))]
    else:
        N = b.shape[0]
        b_specs = [pl.BlockSpec((tn, tk), lambda i, j, k: (j, k))]
    nb = len(b_specs)
    nk = K // tk
    assert out_dtype == F32 or nk == 1
    in_specs = [pl.BlockSpec((tm, tk), lambda i, j, k: (i, k))] + b_specs
    args = [a] + [b] * nb
    if extra is not None:
        in_specs.append(pl.BlockSpec((tm, tn), lambda i, j, k: (i, j)))
        args.append(extra)

    def body(*refs):
        a_ref, b_ref = refs[0], refs[1]
        o_ref = refs[-1]
        if two:
            p = (_dot_nt(a_ref[:, :tk // 2].astype(BF16), refs[1][...])
                 + _dot_nt(a_ref[:, tk // 2:].astype(BF16), refs[2][...]))
        else:
            p = _dot_nt(a_ref[...].astype(BF16), b_ref[...])
        if nk == 1:
            if epilogue is not None:
                p = epilogue(p, refs[1 + nb][...])
            o_ref[...] = p.astype(out_dtype)
        else:
            k = pl.program_id(2)

            @pl.when(k == 0)
            def _():
                o_ref[...] = p

            @pl.when(k > 0)
            def _():
                o_ref[...] += p

    return _call(
        body, name=name, grid=(T // tm, N // tn, nk),
        in_specs=in_specs,
        out_specs=pl.BlockSpec((tm, tn), lambda i, j, k: (i, j)),
        out_shape=jax.ShapeDtypeStruct((T, N), out_dtype),
        sem=("parallel", "parallel", "arbitrary"), rides=rides,
    )(*args)


def _loss_bwd(h2, tgt, g, *, tm):
    T, D = h2.shape

    def body(h_ref, t_ref, g_ref, dh_ref, dhb_ref, dg_ref, loss_ref):
        @pl.when(pl.program_id(0) == 0)
        def _():
            dg_ref[...] = jnp.zeros_like(dg_ref)
            loss_ref[...] = jnp.zeros_like(loss_ref)
        h = h_ref[...]
        gg = g_ref[...]
        r = lax.rsqrt(jnp.mean(h * h, axis=-1, keepdims=True) + EPS)
        hn = h * r
        err = hn * gg - t_ref[...]
        loss_ref[...] += 0.5 * jnp.sum(jnp.mean(err * err, axis=-1, keepdims=True), axis=0, keepdims=True)
        dy = err * (1.0 / D)
        dg_ref[...] += jnp.sum(dy * hn, axis=0, keepdims=True)
        w = dy * gg
        dh = r * w - h * ((r * r * r) * jnp.mean(w * h, axis=-1, keepdims=True))
        dh_ref[...] = dh
        dhb_ref[...] = dh.astype(BF16)

    tile = pl.BlockSpec((tm, D), lambda i: (i, 0))
    return pl.pallas_call(
        body, name="loss_bwd", grid=(T // tm,),
        in_specs=[tile, tile, pl.BlockSpec((1, D), lambda i: (0, 0))],
        out_specs=[tile, tile, pl.BlockSpec((1, D), lambda i: (0, 0)), pl.BlockSpec((1, 1), lambda i: (0, 0))],
        out_shape=[jax.ShapeDtypeStruct((T, D), F32), jax.ShapeDtypeStruct((T, D), BF16),
                   jax.ShapeDtypeStruct((1, D), F32), jax.ShapeDtypeStruct((1, 1), F32)],
        compiler_params=_params(("arbitrary",)),
    )(h2, tgt, g)


def _rms_bwd_res(dn, h, g, dres, *, tm, name, bf16_copy=True, rides=()):
    T, D = h.shape

    def body(dn_ref, h_ref, g_ref, dres_ref, dh_ref, *rest):
        dg_ref = rest[-1]

        @pl.when(pl.program_id(0) == 0)
        def _():
            dg_ref[...] = jnp.zeros_like(dg_ref)
        h_ = h_ref[...]
        dn_ = dn_ref[...]
        dh, r = _rms_bwd(dn_, h_, g_ref[...])
        dg_ref[...] += jnp.sum(dn_ * (h_ * r), axis=0, keepdims=True)
        dh = dres_ref[...] + dh
        dh_ref[...] = dh
        if bf16_copy:
            rest[0][...] = dh.astype(BF16)

    tile = pl.BlockSpec((tm, D), lambda i: (i, 0))
    row = pl.BlockSpec((1, D), lambda i: (0, 0))
    copy_spec = [tile] if bf16_copy else []
    copy_shape = [jax.ShapeDtypeStruct((T, D), BF16)] if bf16_copy else []
    return _call(
        body, name=name, grid=(T // tm,),
        in_specs=[tile, tile, row, tile], out_specs=[tile] + copy_spec + [row],
        out_shape=[jax.ShapeDtypeStruct((T, D), F32)] + copy_shape + [jax.ShapeDtypeStruct((1, D), F32)],
        sem=("arbitrary",), rides=rides,
    )(dn, h, g, dres)


def _rel_distance():
    i = lax.broadcasted_iota(jnp.int32, (CHUNK, 2 * CHUNK), 0)
    j = lax.broadcasted_iota(jnp.int32, (CHUNK, 2 * CHUNK), 1)
    return i + CHUNK - j


def _bias_build(table):
    def body(tab_ref, o_ref):
        rel = _rel_distance()
        j = lax.broadcasted_iota(jnp.int32, (CHUNK, 2 * CHUNK), 1)
        band = (rel >= 0) & (rel < CHUNK)
        ge = [rel >= t for t in BUCKET_THR]
        for h in range(B_HEADS):
            cur = jnp.full((CHUNK, 2 * CHUNK), tab_ref[0, h], F32)
            for b in range(1, N_BUCKETS):
                cur = jnp.where(ge[b - 1], tab_ref[b, h], cur)
            o_ref[0, h] = jnp.where(band & (j >= CHUNK), cur, NEG)
            o_ref[1, h] = jnp.where(band, cur, NEG)

    return pl.pallas_call(
        body, name="bias_build",
        in_specs=[pl.BlockSpec(memory_space=pltpu.SMEM)],
        out_specs=pl.BlockSpec(memory_space=pltpu.VMEM),
        out_shape=jax.ShapeDtypeStruct((2, B_HEADS, CHUNK, 2 * CHUNK), F32),
    )(table)


def _bias_grad(dbias):
    def body(db_ref, o_ref, acc_ref):
        rel = _rel_distance()
        lo = [0] + BUCKET_THR
        hi = BUCKET_THR + [CHUNK]
        for b in range(N_BUCKETS):
            m = (rel >= lo[b]) & (rel < hi[b])
            for h in range(B_HEADS):
                row = b * B_HEADS + h
                acc_ref[row:row + 1, :] = jnp.sum(jnp.where(m, db_ref[h], 0.0), axis=0, keepdims=True)
        o_ref[...] = jnp.sum(acc_ref[...], axis=1, keepdims=True)

    return pl.pallas_call(
        body, name="bias_grad",
        in_specs=[pl.BlockSpec(memory_space=pltpu.VMEM)],
        out_specs=pl.BlockSpec(memory_space=pltpu.VMEM),
        out_shape=jax.ShapeDtypeStruct((N_BUCKETS * B_HEADS, 1), F32),
        scratch_shapes=[pltpu.VMEM((N_BUCKETS * B_HEADS, 2 * CHUNK), F32)],
    )(dbias)


def _causal_mask():
    t = lax.broadcasted_iota(jnp.int32, (CHUNK, CHUNK), 0)
    s = lax.broadcasted_iota(jnp.int32, (CHUNK, CHUNK), 1)
    return s <= t


def _gate_forward(u, v, lg, lb, wc, bs):
    ug = _gelu(u)
    vg = _gelu(v)
    mu = jnp.mean(vg, axis=-1, keepdims=True)
    xc = vg - mu
    rstd = lax.rsqrt(jnp.mean(xc * xc, axis=-1, keepdims=True) + EPS)
    xhat = xc * rstd
    vl = (xhat * lg + lb).astype(BF16)
    mixed = _dot(wc, vl) + bs
    return ug, xhat, rstd, vl, mixed


def _softmax_scores(qk, bias, sink):
    s = qk * SCALE + bias
    m = jnp.maximum(jnp.max(s, axis=-1, keepdims=True), sink)
    p = jnp.exp(s - m)
    e_sink = jnp.exp(sink - m)
    inv = 1.0 / (jnp.sum(p, axis=-1, keepdims=True) + e_sink)
    return p * inv, e_sink * inv


PAIRS = Q_PER_KV // 2


def _head(g, pr, e):
    return g * Q_PER_KV + 2 * pr + e


def _stack_pairs(ref, g, col0=0):
    w = 2 * HEAD_DIM
    return jnp.concatenate([ref[:, col0 + (g * PAIRS + pr) * w:col0 + (g * PAIRS + pr + 1) * w] for pr in range(PAIRS)],
                           axis=0)


def _low_lanes():
    return lax.broadcasted_iota(jnp.int32, (2 * CHUNK, 2 * HEAD_DIM), 1) < HEAD_DIM


def _band_operands(kv_prev, kv_cur):
    band = jnp.concatenate([kv_prev, kv_cur], axis=0)
    low = _low_lanes()
    ops = []
    for cat in (band[:, :KV_WIDTH], band[:, KV_WIDTH:]):
        rol = pltpu.roll(cat, HEAD_DIM, 1)
        ops.append([[jnp.where(low if e == 0 else ~low, cat if g == e else rol, 0.0).astype(BF16) for e in range(2)]
                    for g in range(2)])
    return ops


def _mixer_fwd(proj, lg, lb, wsp, bs_col, sinks, bias, ga, gb, rides=()):
    T = proj.shape[0]
    nb = T // CHUNK

    def body(u_ref, v_ref, q_ref, kvc_ref, kvp_ref, lg_ref, lb_ref, w_ref, bs_ref, sink_ref, bias_ref,
             ga_ref, gb_ref, mixed_ref, mixed_t_ref, ab_ref):
        causal = _causal_mask()
        ssq = jnp.zeros((CHUNK, 1), F32)
        for g in range(A_GROUPS):
            cols = slice(g * CHUNK, (g + 1) * CHUNK)
            wc = jnp.where(causal, w_ref[g], 0.0).astype(BF16)
            ug, _, _, _, mixed = _gate_forward(u_ref[:, cols], v_ref[:, cols], lg_ref[g:g + 1, :], lb_ref[g:g + 1, :],
                                               wc, bs_ref[g])
            a = ug * mixed
            ab_ref[:, cols] = a
            ssq = ssq + jnp.sum(a * a, axis=-1, keepdims=True)
        ra = lax.rsqrt(ssq * (1.0 / A_WIDTH) + EPS)
        mixed_ref[:, :A_WIDTH] = ((ab_ref[:, :A_WIDTH] * ra) * ga_ref[...]).astype(BF16)

        kops, vops = _band_operands(kvp_ref[...], kvc_ref[...])
        ssq = jnp.zeros((CHUNK, 1), F32)
        for g in range(B_HEADS // Q_PER_KV):
            qst = _stack_pairs(q_ref, g).astype(BF16)
            o_st = jnp.zeros((PAIRS * CHUNK, 2 * HEAD_DIM), F32)
            for e in range(2):
                s_all = _dot_nt(qst, kops[g][e])
                ps = []
                for pr in range(PAIRS):
                    h = _head(g, pr, e)
                    p, _ = _softmax_scores(s_all[pr * CHUNK:(pr + 1) * CHUNK], bias_ref[h], sink_ref[0, h])
                    ps.append(p.astype(BF16))
                o_st = o_st + _dot(jnp.concatenate(ps, axis=0), vops[g][e])
            for pr in range(PAIRS):
                o = o_st[pr * CHUNK:(pr + 1) * CHUNK]
                c0 = A_WIDTH + (g * PAIRS + pr) * 2 * HEAD_DIM
                ab_ref[:, c0:c0 + 2 * HEAD_DIM] = o
                ssq = ssq + jnp.sum(o * o, axis=-1, keepdims=True)
        rb = lax.rsqrt(ssq * (1.0 / B_WIDTH) + EPS)
        mixed_ref[:, A_WIDTH:] = ((ab_ref[:, A_WIDTH:] * rb) * gb_ref[...]).astype(BF16)
        mixed_t_ref[...] = mixed_ref[...].T

    full = lambda *shape: pl.BlockSpec(shape, lambda n: (0,) * len(shape))
    return _call(
        body, name="mixer_fwd", grid=(nb,),
        in_specs=[pl.BlockSpec((CHUNK, A_WIDTH), lambda n: (n, 0)),
                  pl.BlockSpec((CHUNK, A_WIDTH), lambda n: (n, 1)),
                  pl.BlockSpec((CHUNK, B_WIDTH), lambda n: (n, 2)),
                  pl.BlockSpec((CHUNK, 2 * KV_WIDTH), lambda n: (n, 12)),
                  pl.BlockSpec((CHUNK, 2 * KV_WIDTH), lambda n: (jnp.maximum(n - 1, 0), 12)),
                  full(A_GROUPS, CHUNK), full(A_GROUPS, CHUNK), full(A_GROUPS, CHUNK, CHUNK), full(A_GROUPS, CHUNK, 1),
                  pl.BlockSpec(memory_space=pltpu.SMEM),
                  pl.BlockSpec((None, B_HEADS, CHUNK, 2 * CHUNK), lambda n: (jnp.minimum(n, 1), 0, 0, 0)),
                  full(1, A_WIDTH), full(1, B_WIDTH)],
        out_specs=[pl.BlockSpec((CHUNK, D_MODEL), lambda n: (n, 0)), pl.BlockSpec((D_MODEL, CHUNK), lambda n: (0, n)),
                   pl.BlockSpec((CHUNK, D_MODEL), lambda n: (n, 0))],
        out_shape=[jax.ShapeDtypeStruct((T, D_MODEL), BF16), jax.ShapeDtypeStruct((D_MODEL, T), BF16),
                   jax.ShapeDtypeStruct((T, D_MODEL), F32)],
        sem=("parallel",), rides=rides,
    )(proj, proj, proj, proj, proj, lg, lb, wsp, bs_col, sinks, bias, ga, gb)


def _gmlp_bwd(proj, ab, dmixed, ga, lg, lb, wsp, bs_col, rides=()):
    T = proj.shape[0]
    nb = T // CHUNK

    def body(u_ref, v_ref, a_ref, dna_ref, ga_ref, lg_ref, lb_ref, w_ref, bs_ref,
             dp_ref, dpt_ref, dga_ref, dw_ref, dbs_ref, dlg_ref, dlb_ref):
        @pl.when(pl.program_id(0) == 0)
        def _():
            for r in (dga_ref, dw_ref, dbs_ref, dlg_ref, dlb_ref):
                r[...] = jnp.zeros_like(r)
        causal = _causal_mask()
        a_all = a_ref[...]
        dna = dna_ref[...]
        da_all, ra = _rms_bwd(dna, a_all, ga_ref[...])
        dga_ref[...] += jnp.sum(dna * (a_all * ra), axis=0, keepdims=True)
        for g in range(A_GROUPS):
            cols = slice(g * CHUNK, (g + 1) * CHUNK)
            wc = jnp.where(causal, w_ref[g], 0.0).astype(BF16)
            lgg = lg_ref[g:g + 1, :]
            u = u_ref[:, cols]
            v = v_ref[:, cols]
            ug, xhat, rstd, vl, mixed = _gate_forward(u, v, lgg, lb_ref[g:g + 1, :], wc, bs_ref[g])
            da = da_all[:, cols]
            dug = da * mixed
            dmg = da * ug
            dmg_b = dmg.astype(BF16)
            dbs_ref[g] += jnp.sum(dmg, axis=-1, keepdims=True)
            dw_ref[g] += jnp.where(causal, _dot_nt(dmg_b, vl), 0.0)
            dvl = _dot_tn(wc, dmg_b)
            dlg_ref[g:g + 1, :] += jnp.sum(dvl * xhat, axis=0, keepdims=True)
            dlb_ref[g:g + 1, :] += jnp.sum(dvl, axis=0, keepdims=True)
            dxh = dvl * lgg
            dvg = rstd * (dxh - jnp.mean(dxh, axis=-1, keepdims=True)
                          - xhat * jnp.mean(dxh * xhat, axis=-1, keepdims=True))
            _, gu = _gelu_and_grad(u)
            _, gv = _gelu_and_grad(v)
            dp_ref[:, cols] = (dug * gu).astype(BF16)
            dp_ref[:, A_WIDTH + g * CHUNK:A_WIDTH + (g + 1) * CHUNK] = (dvg * gv).astype(BF16)
        dpt_ref[...] = dp_ref[...].T

    full = lambda *shape: pl.BlockSpec(shape, lambda n: (0,) * len(shape))
    return _call(
        body, name="gmlp_bwd", grid=(nb,),
        in_specs=[pl.BlockSpec((CHUNK, A_WIDTH), lambda n: (n, 0)),
                  pl.BlockSpec((CHUNK, A_WIDTH), lambda n: (n, 1)),
                  pl.BlockSpec((CHUNK, A_WIDTH), lambda n: (n, 0)),
                  pl.BlockSpec((CHUNK, A_WIDTH), lambda n: (n, 0)),
                  full(1, A_WIDTH), full(A_GROUPS, CHUNK), full(A_GROUPS, CHUNK), full(A_GROUPS, CHUNK, CHUNK),
                  full(A_GROUPS, CHUNK, 1)],
        out_specs=[pl.BlockSpec((CHUNK, 2 * A_WIDTH), lambda n: (n, 0)), pl.BlockSpec((2 * A_WIDTH, CHUNK), lambda n: (0, n)),
                   full(1, A_WIDTH), full(A_GROUPS, CHUNK, CHUNK), full(A_GROUPS, CHUNK, 1),
                   full(A_GROUPS, CHUNK), full(A_GROUPS, CHUNK)],
        out_shape=[jax.ShapeDtypeStruct((T, 2 * A_WIDTH), BF16), jax.ShapeDtypeStruct((2 * A_WIDTH, T), BF16),
                   jax.ShapeDtypeStruct((1, A_WIDTH), F32), jax.ShapeDtypeStruct((A_GROUPS, CHUNK, CHUNK), F32),
                   jax.ShapeDtypeStruct((A_GROUPS, CHUNK, 1), F32), jax.ShapeDtypeStruct((A_GROUPS, CHUNK), F32),
                   jax.ShapeDtypeStruct((A_GROUPS, CHUNK), F32)],
        sem=("arbitrary",), rides=rides,
    )(proj, proj, ab, dmixed, ga, lg, lb, wsp, bs_col)


def _attn_bwd(proj, ab, dmixed, gb, sinks, bias, rides=()):
    T = proj.shape[0]
    nb = T // CHUNK
    qn = lambda n: jnp.minimum(n, nb - 1)

    def body(q_ref, kvc_ref, kvp_ref, o_ref, dnb_ref, gb_ref, sink_ref, bias_ref,
             dq_ref, dkv_ref, dqt_ref, dkvt_ref, dgb_ref, dsink_ref, dbias_ref, carry_ref, sacc_ref):
        n = pl.program_id(0)

        @pl.when(n == 0)
        def _():
            carry_ref[...] = jnp.zeros_like(carry_ref)
            sacc_ref[...] = jnp.zeros_like(sacc_ref)
            dgb_ref[...] = jnp.zeros_like(dgb_ref)
            dbias_ref[...] = jnp.zeros_like(dbias_ref)

        @pl.when(n < nb)
        def _():
            o_all = o_ref[...]
            dnb = dnb_ref[...]
            do_all, rb = _rms_bwd(dnb, o_all, gb_ref[...])
            dgb_ref[...] += jnp.sum(dnb * (o_all * rb), axis=0, keepdims=True)
            kops, vops = _band_operands(kvp_ref[...], kvc_ref[...])
            low = _low_lanes()
            halves = []
            for g in range(B_HEADS // Q_PER_KV):
                qst = _stack_pairs(q_ref, g).astype(BF16)
                dost = _stack_pairs(do_all, g).astype(BF16)
                dq_st = jnp.zeros((PAIRS * CHUNK, 2 * HEAD_DIM), F32)
                dk_e, dv_e = [], []
                for e in range(2):
                    s_all = _dot_nt(qst, kops[g][e])
                    dp_all = _dot_nt(dost, vops[g][e])
                    ps, dsrs = [], []
                    for pr in range(PAIRS):
                        h = _head(g, pr, e)
                        rows = slice(pr * CHUNK, (pr + 1) * CHUNK)
                        p, p_sink = _softmax_scores(s_all[rows], bias_ref[h], sink_ref[0, h])
                        dp = dp_all[rows]
                        delta = jnp.sum(p * dp, axis=-1, keepdims=True)
                        ds = p * (dp - delta)
                        sacc_ref[:, h:h + 1] += -(p_sink * delta)
                        dbias_ref[h] += ds
                        ps.append(p.astype(BF16))
                        dsrs.append((ds * SCALE).astype(BF16))
                    dsr_all = jnp.concatenate(dsrs, axis=0)
                    dq_st = dq_st + _dot(dsr_all, kops[g][e])
                    dk_e.append(_dot_tn(dsr_all, qst))
                    dv_e.append(_dot_tn(jnp.concatenate(ps, axis=0), dost))
                for pr in range(PAIRS):
                    c0 = (g * PAIRS + pr) * 2 * HEAD_DIM
                    dq_ref[:, c0:c0 + 2 * HEAD_DIM] = dq_st[pr * CHUNK:(pr + 1) * CHUNK].astype(BF16)
                halves.append((dk_e, dv_e))
            tiles = []
            for t in range(2):
                g0, g1 = halves[0][t], halves[1][t]
                tiles.append(jnp.where(low, g0[0] + pltpu.roll(g0[1], HEAD_DIM, 1), pltpu.roll(g1[0], HEAD_DIM, 1) + g1[1]))
            dband = jnp.concatenate(tiles, axis=1)
            dkv = (carry_ref[...] + dband[:CHUNK]).astype(BF16)
            dkv_ref[...] = dkv
            dkvt_ref[...] = dkv.T
            dqt_ref[...] = dq_ref[...].T
            carry_ref[...] = dband[CHUNK:]

        @pl.when(n == nb)
        def _():
            dkv = carry_ref[...].astype(BF16)
            dkv_ref[...] = dkv
            dkvt_ref[...] = dkv.T
            dsink_ref[...] = jnp.sum(sacc_ref[...], axis=0, keepdims=True)

    full = lambda *shape: pl.BlockSpec(shape, lambda n: (0,) * len(shape))
    return _call(
        body, name="attn_bwd", grid=(nb + 1,),
        in_specs=[pl.BlockSpec((CHUNK, B_WIDTH), lambda n: (qn(n), 2)),
                  pl.BlockSpec((CHUNK, 2 * KV_WIDTH), lambda n: (qn(n), 12)),
                  pl.BlockSpec((CHUNK, 2 * KV_WIDTH), lambda n: (jnp.maximum(qn(n) - 1, 0), 12)),
                  pl.BlockSpec((CHUNK, B_WIDTH), lambda n: (qn(n), 1)),
                  pl.BlockSpec((CHUNK, B_WIDTH), lambda n: (qn(n), 1)),
                  full(1, B_WIDTH), pl.BlockSpec(memory_space=pltpu.SMEM),
                  pl.BlockSpec((None, B_HEADS, CHUNK, 2 * CHUNK), lambda n: (jnp.minimum(n, 1), 0, 0, 0))],
        out_specs=[pl.BlockSpec((CHUNK, B_WIDTH), lambda n: (qn(n), 0)),
                   pl.BlockSpec((CHUNK, 2 * KV_WIDTH), lambda n: (jnp.maximum(n - 1, 0), 0)),
                   pl.BlockSpec((B_WIDTH, CHUNK), lambda n: (0, qn(n))),
                   pl.BlockSpec((2 * KV_WIDTH, CHUNK), lambda n: (0, jnp.maximum(n - 1, 0))),
                   full(1, B_WIDTH), full(1, B_HEADS), full(B_HEADS, CHUNK, 2 * CHUNK)],
        out_shape=[jax.ShapeDtypeStruct((T, B_WIDTH), BF16), jax.ShapeDtypeStruct((T, 2 * KV_WIDTH), BF16),
                   jax.ShapeDtypeStruct((B_WIDTH, T), BF16), jax.ShapeDtypeStruct((2 * KV_WIDTH, T), BF16),
                   jax.ShapeDtypeStruct((1, B_WIDTH), F32), jax.ShapeDtypeStruct((1, B_HEADS), F32),
                   jax.ShapeDtypeStruct((B_HEADS, CHUNK, 2 * CHUNK), F32)],
        scratch_shapes=[pltpu.VMEM((CHUNK, 2 * KV_WIDTH), F32), pltpu.VMEM((CHUNK, B_HEADS), F32)],
        sem=("arbitrary",), rides=rides,
    )(proj, proj, proj, ab, dmixed, gb, sinks, bias)


def _sq_relu_grad(acc, r):
    return acc * (2.0 * r.astype(F32))


def _chip_index():
    return (2 * lax.axis_index("x") + lax.axis_index("y")).astype(jnp.int32).reshape(1)


def _cast_into_slot(w, *, tm, name):
    _, R, C = w.shape

    def body(me_ref, w_ref, o_ref):
        del me_ref
        o_ref[...] = w_ref[...].astype(BF16)

    return pl.pallas_call(
        body, name=name,
        grid_spec=pltpu.PrefetchScalarGridSpec(
            num_scalar_prefetch=1, grid=(R // tm,),
            in_specs=[pl.BlockSpec((None, tm, C), lambda i, me: (0, i, 0))],
            out_specs=pl.BlockSpec((None, tm, C), lambda i, me: (me[0], i, 0))),
        out_shape=jax.ShapeDtypeStruct((N_CHIPS, R, C), BF16), compiler_params=_params(("parallel",)),
    )(_chip_index(), w)


def _cast_into_slots_carrying(ws, *, steps, name, rides):
    n = len(ws)

    def body(*refs):
        for w_ref, o_ref in zip(refs[:n], refs[n:]):
            o_ref[...] = w_ref[...].astype(BF16)

    me = lambda: 2 * lax.axis_index("x") + lax.axis_index("y")
    return _call(
        body, name=name, grid=(steps,),
        in_specs=[pl.BlockSpec((None, w.shape[1] // steps, w.shape[2]), lambda i: (0, i, 0)) for w in ws],
        out_specs=[pl.BlockSpec((None, w.shape[1] // steps, w.shape[2]), lambda i: (me(), i, 0)) for w in ws],
        out_shape=[jax.ShapeDtypeStruct((N_CHIPS,) + w.shape[1:], BF16) for w in ws], sem=("arbitrary",), rides=rides,
    )(*ws)


def _owner_total(gh, others, *, tm, name):
    _, hr, C = gh.shape

    def body(me_ref, g_ref, o_ref_in, out_ref):
        del me_ref
        acc = g_ref[...]
        for j in range(3):
            acc = acc + o_ref_in[j].astype(F32)
        out_ref[...] = acc

    return pl.pallas_call(
        body, name=name,
        grid_spec=pltpu.PrefetchScalarGridSpec(
            num_scalar_prefetch=1, grid=(hr // tm,),
            in_specs=[pl.BlockSpec((None, tm, C), lambda i, me: (me[0], i, 0)),
                      pl.BlockSpec((3, tm, C), lambda i, me: (0, i, 0))],
            out_specs=pl.BlockSpec((tm, C), lambda i, me: (i, 0))),
        out_shape=jax.ShapeDtypeStruct((hr, C), F32),
        compiler_params=_params(("parallel",)),
    )(_chip_index(), gh, others)


def _adamw_math(w, g, m, v):
    m = ADAM_B1 * m + (1.0 - ADAM_B1) * g
    v = ADAM_B2 * v + (1.0 - ADAM_B2) * (g * g)
    m_hat = m / (1.0 - ADAM_B1 ** ADAM_STEP)
    v_hat = v / (1.0 - ADAM_B2 ** ADAM_STEP)
    delta = -ADAM_LR * (m_hat / (jnp.sqrt(v_hat) + ADAM_EPS) + ADAM_WD * w)
    return delta, m, v


def _adamw_halves(w, own, got, m, v, *, tm, name, rides=()):
    _, R, C = w.shape
    nt = (R // 2) // tm

    def body(w_ref, own_ref, got_ref, m_ref, v_ref, g_ref, d_ref, nm_ref, nv_ref):
        g = jnp.where(pl.program_id(0) == lax.axis_index("c"), own_ref[...], got_ref[...])
        g_ref[...] = g
        d_ref[...], nm_ref[...], nv_ref[...] = _adamw_math(w_ref[...], g, m_ref[...], v_ref[...])

    whole = pl.BlockSpec((None, tm, C), lambda h, i: (0, h * nt + i, 0))
    half = pl.BlockSpec((tm, C), lambda h, i: (i, 0))
    return _call(
        body, name=name, grid=(2, nt), in_specs=[whole, half, half, whole, whole], out_specs=[whole] * 4,
        out_shape=[jax.ShapeDtypeStruct((1, R, C), F32)] * 4, sem=("parallel", "parallel"), rides=rides,
    )(w, own, got, m, v)


def _adamw_small(w, slots, m, v, *, name):
    def body(w_ref, slots_ref, m_ref, v_ref, g_ref, d_ref, nm_ref, nv_ref):
        g = slots_ref[0]
        for d in range(1, N_DEV):
            g = g + slots_ref[d]
        g_ref[...] = g
        d_ref[...], nm_ref[...], nv_ref[...] = _adamw_math(w_ref[...], g, m_ref[...], v_ref[...])

    vmem = pl.BlockSpec(memory_space=pltpu.VMEM)
    return pl.pallas_call(
        body, name=name, in_specs=[vmem] * 4, out_specs=[vmem] * 4,
        out_shape=[jax.ShapeDtypeStruct(w.shape, F32)] * 4, compiler_params=_params(),
    )(w, slots, m, v)


SMALL = ["rel_bias_table", "mix_norm_g", "gate_norm_g", "gate_norm_b", "w_spatial", "b_spatial", "attn_sinks",
         "out_norm_a_g", "out_norm_b_g", "ffn_norm_g", "final_norm_g"]
SMALL_A = ["gate_norm_g", "gate_norm_b", "w_spatial", "b_spatial", "out_norm_a_g"]
SMALL_B = ["rel_bias_table", "mix_norm_g", "attn_sinks", "out_norm_b_g", "ffn_norm_g", "final_norm_g"]
LARGE = ["w_in", "w_out", "w_up", "w_down"]
ROW_TILE = {"w_in": 208, "w_out": 256, "w_up": 256, "w_down": 256}
WEIGHTS = ["rel_bias_table", "mix_norm_g", "w_in", "gate_norm_g", "gate_norm_b", "w_spatial", "b_spatial", "attn_sinks",
           "out_norm_a_g", "out_norm_b_g", "w_out", "ffn_norm_g", "w_up", "w_down", "final_norm_g"]
PACK_UNIT = 8 * 128


def _pack(parts):
    rows = []
    for p in parts:
        flat = p.reshape(-1)
        pad = (-flat.shape[0]) % PACK_UNIT
        rows.append(jnp.pad(flat, (0, pad)).reshape(-1, 128))
    return jnp.concatenate(rows, axis=0)


def _unpack(packed, like):
    out, row = [], 0
    for p in like:
        n = math.prod(p.shape)
        nrows = (n + PACK_UNIT - 1) // PACK_UNIT * 8
        out.append(packed[row:row + nrows].reshape(-1)[:n].reshape(p.shape))
        row += nrows
    return out


def kernel(x, rel_bias_table, mix_norm_g, w_in, gate_norm_g, gate_norm_b, w_spatial, b_spatial, attn_sinks, out_norm_a_g, out_norm_b_g, w_out, ffn_norm_g, w_up, w_down, final_norm_g, loss_target, m_rel_bias_table, m_mix_norm_g, m_w_in, m_gate_norm_g, m_gate_norm_b, m_w_spatial, m_b_spatial, m_attn_sinks, m_out_norm_a_g, m_out_norm_b_g, m_w_out, m_ffn_norm_g, m_w_up, m_w_down, m_final_norm_g, v_rel_bias_table, v_mix_norm_g, v_w_in, v_gate_norm_g, v_gate_norm_b, v_w_spatial, v_b_spatial, v_attn_sinks, v_out_norm_a_g, v_out_norm_b_g, v_w_out, v_ffn_norm_g, v_w_up, v_w_down, v_final_norm_g):
    args = dict(locals())
    wts = {n: args[n] for n in WEIGHTS}
    mom = {n: args["m_" + n] for n in WEIGHTS}
    var = {n: args["v_" + n] for n in WEIGHTS}
    sp = {n: wts[n] for n in SMALL}
    x2, tgt = x[0], loss_target[0]
    T = x2.shape[0]
    tm = min(512, T)
    tl = min(1024, T)
    lg = sp["gate_norm_g"].reshape(A_GROUPS, CHUNK)
    lb = sp["gate_norm_b"].reshape(A_GROUPS, CHUNK)
    wsp = sp["w_spatial"].reshape(A_GROUPS, CHUNK, CHUNK)
    bs_col = sp["b_spatial"].reshape(A_GROUPS, CHUNK, 1)
    sinks = sp["attn_sinks"].reshape(1, B_HEADS)
    ga = sp["out_norm_a_g"].reshape(1, A_WIDTH)
    gb = sp["out_norm_b_g"].reshape(1, B_WIDTH)
    g1 = sp["mix_norm_g"].reshape(1, D_MODEL)
    g2 = sp["ffn_norm_g"].reshape(1, D_MODEL)
    gf = sp["final_norm_g"].reshape(1, D_MODEL)

    def owner_total(n, gh, others):
        return _owner_total(gh, others, tm=ROW_TILE[n], name="rs_owner_total_" + n)

    def halves_view(at, shards):
        return at.reshape(shards, 2, at.shape[0] // shards // 2, at.shape[1])

    for d in (wts, mom, var):
        d["w_in"] = jnp.swapaxes(d["w_in"], 1, 2)

    s_in = _cast_into_slot(wts["w_in"], tm=ROW_TILE["w_in"], name="cast_w_in")
    (s_out, s_up, s_down), ((g_in,),) = _cast_into_slots_carrying(
        [wts["w_out"], wts["w_up"], wts["w_down"]], steps=8, name="cast_w_rest",
        rides=[_ride_gather(s_in, chain=(0, 1, 1), chain_fracs=(0.3, 0.6))])
    win_t = g_in.reshape(PROJ_WIDTH, D_MODEL)
    bias = _bias_build(sp["rel_bias_table"])
    (n1, proj), ((g_out,), (s_up,)) = _norm_matmul_wide(
        x2, g1, win_t, tm=tm, tn=PROJ_WIDTH // 2, name="in_proj",
        rides=[_ride_gather(s_out, chain=(0, 1, 1), chain_fracs=(0.3, 0.6)), _ride_gather(s_up, s1=(0, 3, 8))])
    wo = g_out.reshape(A_WIDTH + B_WIDTH, D_MODEL)
    (mixed, mixed_t, ab), ((s_up,), (s_down,), (n1_sib,)) = _mixer_fwd(
        proj, lg, lb, wsp, bs_col, sinks, bias, ga, gb,
        rides=[_ride_gather(s_up, s2=(0, 3, 8), s1=(3, 8, 8)), _ride_gather(s_down, s1=(0, 3, 8)),
               _ride_to_sibling(n1, first=True)])
    mixed_t = halves_view(mixed_t, N_CHIPS)
    h1, ((wu,), (s_down,), (mixed_t_sib,)) = _matmul_res(
        mixed, wo, x2, tm=tl, tn=1024, tk=D_MODEL, prologue=_to_bf16, name="out_proj",
        rides=[_ride_gather(s_up, s3=(0, 3, 8), tail=(3, 8, 8), mid_frac=0.75), _ride_gather(s_down, s2=(0, 3, 8)),
               _ride_to_sibling(mixed_t, halves=True)])
    (n2t, zp, z2, z2t), ((g_down,),) = _norm_matmul_sq(
        h1, g2, wu, tm=tl, tn=1024, name="up_proj", rides=[_ride_gather(s_down, s3=(0, 3, 8), chain=(3, 8, 8), chain_fracs=(0.45, 0.75))])
    wd = g_down.reshape(D_FF, D_MODEL)
    n2t, z2t = halves_view(n2t, 1), halves_view(z2t, N_CHIPS)
    h2, ((n2t_sib,), (z2t_sib,)) = _matmul_res(
        z2, wd, h1, tm=tl, tn=1024, tk=4096, prologue=_to_bf16, name="down_proj",
        rides=[_ride_to_sibling(n2t, halves=True), _ride_to_sibling(z2t, halves=True)])

    dh2, dh2b, dgf, loss = _loss_bwd(h2, tgt, gf, tm=tm)
    dzp, ((dh2b_sib,),) = _matmul_nt(dh2b, wd, tm=tl, tn=1024, tk=D_MODEL, name="bwd_dz", extra=zp,
                                     epilogue=_sq_relu_grad, out_dtype=BF16, rides=[_ride_to_sibling(dh2b)])
    (gd, gdb), ((dzp_sib,),) = _grad_pair(z2t, z2t_sib, dh2b, dh2b_sib, cols_sharded=False, tmo=1024, tk=tl,
                                          name="grad_w_down", rides=[_ride_to_sibling(dzp)])
    (gu, gub), ((o_d,),) = _grad_pair(n2t, n2t_sib, dzp, dzp_sib, cols_sharded=True, tmo=1024, tk=tl,
                                      name="grad_w_up", rides=[_ride_scatter(gdb, None, (0, 7, 8))])
    dn2, ((o_d,), (o_u,)) = _matmul_nt(dzp, wu, tm=tl, tn=1024, tk=4096, name="bwd_dn2",
                                       rides=[_ride_scatter(gdb, o_d, (7, 8, 8)), _ride_scatter(gub, None, (0, 6, 8))])
    h_d = owner_total("w_down", gd, o_d)
    (dh1, dh1b, dg2), ((o_u,),) = _rms_bwd_res(dn2, h1, g2, dh2, tm=tm, name="ffn_norm_bwd",
                                               rides=[_ride_scatter(gub, o_u, (6, 7, 8))])
    dmixed, ((o_u,), (dh1b_sib,), (w_d,)) = _matmul_nt(
        dh1b, wo, tm=tl, tn=1024, tk=D_MODEL, name="bwd_dmixed",
        rides=[_ride_scatter(gub, o_u, (7, 8, 8)), _ride_to_sibling(dh1b), _ride_swap(h_d)])
    h_u = owner_total("w_up", gu, o_u)
    (go, gob), ((w_u,),) = _grad_pair_merged(mixed_t, mixed_t_sib, dh1b, dh1b_sib, tk=tl, name="grad_w_out",
                                             rides=[_ride_swap(h_u)])
    (duv, duv_t, dga, dwsp, dbs, dlg, dlb), ((o_o,),) = _gmlp_bwd(proj, ab, dmixed, ga, lg, lb, wsp, bs_col,
                                                                  rides=[_ride_scatter(gob)])
    h_o = owner_total("w_out", go, o_o)
    small = {"gate_norm_g": dlg, "gate_norm_b": dlb, "w_spatial": dwsp, "b_spatial": dbs, "out_norm_a_g": dga}
    hr_in = PROJ_WIDTH // N_CHIPS // 2
    (dq, dkv, dq_t, dkv_t, dgb, dsinks, dbias), ((w_o,), (dproj_t_sib,)) = _attn_bwd(
        proj, ab, dmixed, gb, sinks, bias, rides=[_ride_swap(h_o), _ride_rows_to_sibling(duv_t, hr_in, 2, N_CHIPS)])
    dtable = _bias_grad(dbias)
    dproj_t = halves_view(jnp.concatenate([duv_t, dq_t, dkv_t], axis=0), N_CHIPS)
    ((dproj_t_sib,),) = _carrier([_ride_to_sibling(dproj_t, halves=True, shards=(2, N_CHIPS), land=dproj_t_sib)],
                                 name="trade_dproj_t")
    (gi, gib), ((slots_a,),) = _grad_pair(
        dproj_t, dproj_t_sib, n1, n1_sib, cols_sharded=False, tmo=hr_in, tk=tl, name="grad_w_in",
        rides=[_ride_small_to_all(_pack([small[n] for n in SMALL_A]))])
    dn1, ((o_i,),) = _matmul_parts([duv, dq, dkv], win_t, tm=tl, tn=1024, name="bwd_dn1", rides=[_ride_scatter(gib)])
    h_i = owner_total("w_in", gi, o_i)
    dx, dg1 = _rms_bwd_res(dn1, x2, g1, dh1, tm=tm, name="mix_norm_bwd", bf16_copy=False)
    small.update({"rel_bias_table": dtable.reshape(N_BUCKETS, B_HEADS), "mix_norm_g": dg1, "attn_sinks": dsinks,
                  "out_norm_b_g": dgb, "ffn_norm_g": dg2, "final_norm_g": dgf})
    (w_i,), (slots_b,) = _carrier([_ride_swap(h_i), _ride_small_to_all(_pack([small[n] for n in SMALL_B] + [loss]))],
                                  name="swap_w_in")

    out_g, out_d, out_m, out_v = {}, {}, {}, {}
    for n, h, s in zip(LARGE, [h_i, h_o, h_u, h_d], [w_i, w_o, w_u, w_d]):
        res = _adamw_halves(wts[n], h, s, mom[n], var[n], tm=ROW_TILE[n], name="adamw_" + n)
        if n == "w_in":
            res = [jnp.swapaxes(r, 1, 2) for r in res]
        out_g[n], out_d[n], out_m[n], out_v[n] = res
    for names, slots, tag in ((SMALL_A, slots_a, "a"), (SMALL_B, slots_b, "b")):
        extra = [jnp.zeros((1, 1), F32)] if tag == "b" else []
        like = [wts[n] for n in names] + extra
        res = _adamw_small(_pack(like), slots, _pack([mom[n] for n in names] + extra),
                           _pack([var[n] for n in names] + extra), name="adamw_small_" + tag)
        for store, packed in zip((out_g, out_d, out_m, out_v), res):
            for n, val in zip(names + ["loss"], _unpack(packed, like)):
                store[n] = val

    total = out_g["loss"][0, 0]
    return (total, dx[None], *[out_g[n] for n in WEIGHTS], *[out_d[n] for n in WEIGHTS],
            *[out_m[n] for n in WEIGHTS], *[out_v[n] for n in WEIGHTS])
```

```python
import math

import numpy as np
import jax
import jax.numpy as jnp
from jax import lax
from jax.experimental import pallas as pl
from jax.experimental.pallas import tpu as pltpu

F32 = jnp.float32
BF16 = jnp.bfloat16

D_MODEL = 2048
CHUNK = 128
A_GROUPS = 8
A_WIDTH = 1024
HEAD_DIM = 64
B_HEADS = 16
Q_PER_KV = 8
B_WIDTH = 1024
KV_WIDTH = 128
PROJ_WIDTH = 3328
D_FF = 8192
N_BUCKETS = 32
EPS = 1e-5
NEG = -1e30
SCALE = HEAD_DIM ** -0.5
N_CHIPS = 4
N_DEV = 8

ADAM_LR = 0.001
ADAM_B1 = 0.9
ADAM_B2 = 0.999
ADAM_EPS = 1e-08
ADAM_WD = 0.01
ADAM_STEP = 10

VMEM_LIMIT = 60 * 1024 * 1024
MESH = pl.DeviceIdType.MESH


def _bucket_thresholds():
    d = np.arange(CHUNK)
    n_exact = N_BUCKETS // 2
    relf = np.maximum(d, n_exact).astype(np.float64)
    large = n_exact + (np.log(relf / n_exact) / math.log(CHUNK / n_exact) * (N_BUCKETS - n_exact)).astype(np.int32)
    bucket = np.where(d < n_exact, d, np.minimum(large, N_BUCKETS - 1))
    return [int(np.min(d[bucket >= b])) for b in range(1, N_BUCKETS)]


BUCKET_THR = _bucket_thresholds()


def _params(sem=None):
    return pltpu.CompilerParams(dimension_semantics=sem, vmem_limit_bytes=VMEM_LIMIT)


def _gelu(x):
    c = math.sqrt(2.0 / math.pi)
    return 0.5 * x * (1.0 + jnp.tanh(c * (x + 0.044715 * (x * x * x))))


def _gelu_and_grad(x):
    c = math.sqrt(2.0 / math.pi)
    x2 = x * x
    t = jnp.tanh(c * (x + 0.044715 * (x2 * x)))
    g = 0.5 * x * (1.0 + t)
    dg = 0.5 * (1.0 + t) + 0.5 * x * (1.0 - t * t) * (c * (1.0 + 3.0 * 0.044715 * x2))
    return g, dg


def _dot(a, b):
    return jnp.dot(a, b, preferred_element_type=F32)


def _dot_nt(a, b):
    return lax.dot_general(a, b, (((1,), (1,)), ((), ())), preferred_element_type=F32)


def _dot_tn(a, b):
    return lax.dot_general(a, b, (((0,), (0,)), ((), ())), preferred_element_type=F32)


def _rms_bwd(dn, h, g):
    r = lax.rsqrt(jnp.mean(h * h, axis=-1, keepdims=True) + EPS)
    w = dn * g
    dh = r * w - h * ((r * r * r) * jnp.mean(w * h, axis=-1, keepdims=True))
    return dh, r


def _place():
    x, y, c = lax.axis_index("x"), lax.axis_index("y"), lax.axis_index("c")
    chips = [(1 - x, y), (x, 1 - y), (1 - x, 1 - y)]
    return x, y, c, chips


def _remote(src, dst, send_sem, recv_sem, to):
    return pltpu.make_async_remote_copy(src_ref=src, dst_ref=dst, send_sem=send_sem, recv_sem=recv_sem,
                                        device_id=to, device_id_type=MESH)


class _Ride:
    def __init__(self, args, out_shape, n_sem, start, finish, mids=(), aliases=None):
        self.args, self.out_shape, self.n_sem = list(args), list(out_shape), n_sem
        self.start, self.mids, self.finish = start, list(mids), finish
        self.aliases = dict(aliases or {})


def _call(body, *, name, grid, in_specs, out_specs, out_shape, scratch_shapes=(), sem=None, rides=()):
    single = not isinstance(out_shape, (list, tuple))
    out_specs = [out_specs] if single else list(out_specs)
    out_shape = [out_shape] if single else list(out_shape)
    n_in, n_out, n_scr = len(in_specs), len(out_shape), len(scratch_shapes)
    r_in = [len(r.args) for r in rides]
    r_out = [len(r.out_shape) for r in rides]
    any_spec = pl.BlockSpec(memory_space=pl.ANY)
    aliases, off_i, off_o = {}, n_in, n_out
    for r in rides:
        for i, o in r.aliases.items():
            aliases[off_i + i] = off_o + o
        off_i += len(r.args)
        off_o += len(r.out_shape)
    steps = math.prod(grid)

    def wrapped(*refs):
        p = 0
        ins = refs[p:p + n_in]; p += n_in
        rins = refs[p:p + sum(r_in)]; p += sum(r_in)
        outs = refs[p:p + n_out]; p += n_out
        routs = refs[p:p + sum(r_out)]; p += sum(r_out)
        scr = refs[p:p + n_scr]; p += n_scr
        sems = refs[p:]
        parts, pi, po = [], 0, 0
        for k, r in enumerate(rides):
            parts.append((rins[pi:pi + r_in[k]], routs[po:po + r_out[k]], sems[2 * k], sems[2 * k + 1]))
            pi += r_in[k]
            po += r_out[k]
        lin = 0
        for d in range(len(grid)):
            lin = lin * grid[d] + pl.program_id(d)
        if rides:
            @pl.when(lin == 0)
            def _():
                for r, part in zip(rides, parts):
                    r.start(*part)
        body(*ins, *outs, *scr)
        for r, part in zip(rides, parts):
            for frac, fn in r.mids:
                @pl.when(lin == min(steps - 1, int(frac * steps)))
                def _(fn=fn, part=part):
                    fn(*part)
        if rides:
            @pl.when(lin == steps - 1)
            def _():
                for r, part in zip(rides, parts):
                    r.finish(*part)

    scratch = list(scratch_shapes)
    for r in rides:
        scratch += [pltpu.SemaphoreType.DMA((r.n_sem,)), pltpu.SemaphoreType.DMA((r.n_sem,))]
    if rides:
        sem = ("arbitrary",) * len(grid)
    res = pl.pallas_call(
        wrapped, name=name, grid=grid,
        in_specs=list(in_specs) + [any_spec] * sum(r_in),
        out_specs=out_specs + [any_spec] * sum(r_out),
        out_shape=out_shape + [s for r in rides for s in r.out_shape],
        scratch_shapes=scratch, input_output_aliases=aliases,
        compiler_params=_params(sem),
    )

    def run(*args):
        got = res(*args, *[a for r in rides for a in r.args])
        mine = got[0] if single else list(got[:n_out])
        if not rides:
            return mine
        rest, out = list(got[n_out:]), []
        for k in range(len(rides)):
            out.append(rest[:r_out[k]])
            rest = rest[r_out[k]:]
        return mine, out

    return run


def _ride_gather(slot, s1=None, s2=None, s3=None, tail=None, chain=None, mid_frac=0.6, chain_fracs=(0.35, 0.7)):
    half = slot.shape[1] // 2

    def rows(part, c, which=None):
        k0, k1, n = part
        count, first = (k1 - k0) * (half // n), c * half + k0 * (half // n)
        return pl.ds(first, count) if which is None else pl.ds(first + which * (count // 2), count // 2)

    def ids():
        x, y, c, _ = _place()
        return x, y, c, 2 * x + y, 2 * (1 - x) + y, 2 * x + (1 - y), 2 * (1 - x) + (1 - y)

    def copy(full, chip, r, ss, rs, k, to):
        piece = full.at[chip, r, :]
        return _remote(piece, piece, ss.at[k], rs.at[k], to)

    def to_neighbours(full, ss, rs, part, base):
        x, y, c, me, _, _, _ = ids()
        return [copy(full, me, rows(part, c), ss, rs, base, (1 - x, y, c)),
                copy(full, me, rows(part, c), ss, rs, base + 1, (x, 1 - y, c))]

    def from_neighbours(full, ss, rs, part, base):
        x, y, c, _, cx, cy, _ = ids()
        return [copy(full, cx, rows(part, c), ss, rs, base, (x, y, c)), copy(full, cy, rows(part, c), ss, rs, base + 1, (x, y, c))]

    def onward(full, ss, rs, part, base):
        x, y, c, _, cx, cy, _ = ids()
        return [copy(full, cx, rows(part, c, 0), ss, rs, base, (x, 1 - y, c)),
                copy(full, cy, rows(part, c, 1), ss, rs, base + 1, (1 - x, y, c))]

    def from_onward(full, ss, rs, part, base):
        x, y, c, _, _, _, cd = ids()
        return [copy(full, cd, rows(part, c, 0), ss, rs, base, (x, y, c)), copy(full, cd, rows(part, c, 1), ss, rs, base + 1, (x, y, c))]

    def to_sibling(full, ss, rs, part, base, diagonal):
        x, y, c, _, cx, cy, cd = ids()
        return [copy(full, chip, rows(part, c), ss, rs, base + j, (x, y, 1 - c))
                for j, chip in enumerate([cd] if diagonal else [cx, cy])]

    def from_sibling(full, ss, rs, part, base, diagonal):
        x, y, c, _, cx, cy, cd = ids()
        return [copy(full, chip, rows(part, 1 - c), ss, rs, base + j, (x, y, c))
                for j, chip in enumerate([cd] if diagonal else [cx, cy])]

    def start(ins, outs, ss, rs):
        full, cps = outs[0], []
        for part, base in ((s1, 0), (chain, 12)):
            if part is not None:
                cps += to_neighbours(full, ss, rs, part, base)
        for part, b_ici, b_sib in ((s2, 2, 4), (tail, 7, 9)):
            if part is not None:
                cps += onward(full, ss, rs, part, b_ici) + to_sibling(full, ss, rs, part, b_sib, False)
        if s3 is not None:
            cps += to_sibling(full, ss, rs, s3, 6, True)
        for cp in cps:
            cp.start()

    def second(part, b_in, b_ici, b_sib):
        def fn(ins, outs, ss, rs):
            for cp in from_neighbours(outs[0], ss, rs, part, b_in):
                cp.wait_recv()
            for cp in onward(outs[0], ss, rs, part, b_ici) + to_sibling(outs[0], ss, rs, part, b_sib, False):
                cp.start()
        return fn

    def third(part, b_ici, b_sib):
        def fn(ins, outs, ss, rs):
            for cp in from_onward(outs[0], ss, rs, part, b_ici):
                cp.wait_recv()
            for cp in to_sibling(outs[0], ss, rs, part, b_sib, True):
                cp.start()
        return fn

    mids = []
    if tail is not None:
        mids.append((mid_frac, third(tail, 7, 11)))
    if chain is not None:
        mids += [(chain_fracs[0], second(chain, 12, 14, 16)), (chain_fracs[1], third(chain, 14, 18))]

    def finish(ins, outs, ss, rs):
        full, got, sent = outs[0], [], []
        if s1 is not None:
            got += from_neighbours(full, ss, rs, s1, 0)
            sent += to_neighbours(full, ss, rs, s1, 0)
        if s2 is not None:
            got += from_onward(full, ss, rs, s2, 2) + from_sibling(full, ss, rs, s2, 4, False)
            sent += onward(full, ss, rs, s2, 2) + to_sibling(full, ss, rs, s2, 4, False)
        if s3 is not None:
            got += from_sibling(full, ss, rs, s3, 6, True)
            sent += to_sibling(full, ss, rs, s3, 6, True)
        if tail is not None:
            got += from_sibling(full, ss, rs, tail, 9, False) + from_sibling(full, ss, rs, tail, 11, True)
            sent += onward(full, ss, rs, tail, 7) + to_sibling(full, ss, rs, tail, 9, False) + to_sibling(full, ss, rs, tail, 11, True)
        if chain is not None:
            got += from_sibling(full, ss, rs, chain, 16, False) + from_sibling(full, ss, rs, chain, 18, True)
            sent += (to_neighbours(full, ss, rs, chain, 12) + onward(full, ss, rs, chain, 14)
                     + to_sibling(full, ss, rs, chain, 16, False) + to_sibling(full, ss, rs, chain, 18, True))
        for cp in got:
            cp.wait_recv()
        for cp in sent:
            cp.wait_send()

    return _Ride([slot], [jax.ShapeDtypeStruct(slot.shape, slot.dtype)], 19, start, finish, mids=mids, aliases={0: 0})


def _ride_scatter(q, land=None, part=(0, 1)):
    k0, k1, n = part if len(part) == 3 else (part[0], part[0] + 1, part[1])
    rows_n = q.shape[1] // n
    rows = pl.ds(k0 * rows_n, (k1 - k0) * rows_n)

    def copies(ins, outs, ss, rs):
        x, y, c, chips = _place()
        return [_remote(ins[0].at[2 * chip[0] + chip[1], rows, :], outs[0].at[j, rows, :], ss.at[j], rs.at[j], (*chip, c))
                for j, chip in enumerate(chips)]

    def start(*a):
        for cp in copies(*a):
            cp.start()

    def finish(*a):
        for cp in copies(*a):
            cp.wait()

    shape = jax.ShapeDtypeStruct((3,) + q.shape[1:], q.dtype)
    if land is None:
        return _Ride([q], [shape], 3, start, finish)
    return _Ride([q, land], [shape], 3, start, finish, aliases={1: 0})


def _ride_to_sibling(a, halves=False, first=False, shards=None, land=None):
    s0, s1 = shards or (0, a.shape[0])

    def copy(ins, outs, ss, rs):
        x, y, c, _ = _place()
        if halves:
            src, dst = ins[0].at[s0:s1, 1 - c], outs[0].at[s0:s1]
        else:
            src, dst = (ins[0].at[0] if first else ins[0]), outs[0]
        return _remote(src, dst, ss.at[0], rs.at[0], (x, y, 1 - c))

    shape = (a.shape[0],) + a.shape[2:] if halves else (a.shape[1:] if first else a.shape)
    return _Ride([a] if land is None else [a, land], [jax.ShapeDtypeStruct(shape, a.dtype)], 1,
                 lambda *a_: copy(*a_).start(), lambda *a_: copy(*a_).wait(), aliases=None if land is None else {1: 0})


def _ride_rows_to_sibling(a, hr, shards, total):
    def copies(ins, outs, ss, rs):
        x, y, c, _ = _place()
        return [_remote(ins[0].at[pl.ds((2 * s + 1 - c) * hr, hr), :], outs[0].at[s], ss.at[s], rs.at[s], (x, y, 1 - c))
                for s in range(shards)]

    def start(*a_):
        for cp in copies(*a_):
            cp.start()

    def finish(*a_):
        for cp in copies(*a_):
            cp.wait()

    return _Ride([a], [jax.ShapeDtypeStruct((total, hr, a.shape[1]), a.dtype)], shards, start, finish)


def _ride_swap(h):
    def copy(ins, outs, ss, rs):
        x, y, c, _ = _place()
        return _remote(ins[0], outs[0], ss.at[0], rs.at[0], (x, y, 1 - c))

    return _Ride([h], [jax.ShapeDtypeStruct(h.shape, h.dtype)], 1,
                 lambda *a: copy(*a).start(), lambda *a: copy(*a).wait())


def _mesh_place(p):
    return (p // 4, (p // 2) % 2, p % 2)


def _ride_small_to_all(packed):
    def copies(ins, outs, ss, rs):
        x, y, c, _ = _place()
        me = 4 * x + 2 * y + c
        return [_remote(ins[0], outs[0].at[me], ss.at[k - 1], rs.at[k - 1], _mesh_place((me + k) % N_DEV))
                for k in range(1, N_DEV)]

    def own(ins, outs, ss, rs):
        x, y, c, _ = _place()
        return pltpu.make_async_copy(ins[0], outs[0].at[4 * x + 2 * y + c], ss.at[N_DEV - 1])

    def start(*a):
        own(*a).start()
        for cp in copies(*a):
            cp.start()

    def finish(ins, outs, ss, rs):
        x, y, c, _ = _place()
        me = 4 * x + 2 * y + c
        for k in range(1, N_DEV):
            _remote(ins[0], outs[0].at[(me + N_DEV - k) % N_DEV], ss.at[k - 1], rs.at[k - 1], (x, y, c)).wait_recv()
        for cp in copies(ins, outs, ss, rs):
            cp.wait_send()
        own(ins, outs, ss, rs).wait()

    return _Ride([packed], [jax.ShapeDtypeStruct((N_DEV,) + packed.shape, packed.dtype)], N_DEV, start, finish)


def _carrier(rides, *, name):
    _, outs = _call(lambda: None, name=name, grid=(1,), in_specs=[], out_specs=[], out_shape=[], rides=rides)()
    return outs


def _norm_bf16(a_ref, g_ref):
    xf = a_ref[...]
    r = lax.rsqrt(jnp.mean(xf * xf, axis=-1, keepdims=True) + EPS)
    return ((xf * r) * g_ref[...]).astype(BF16)


def _norm_matmul_wide(a, g, b, *, tm, tn, name, rides=()):
    T, K = a.shape
    N = b.shape[0]

    def body(a_ref, g_ref, b_ref, n_ref, o_ref):
        n = _norm_bf16(a_ref, g_ref)
        n_ref[...] = n
        o_ref[...] = _dot_nt(n, b_ref[...])

    return _call(
        body, name=name, grid=(N // tn, T // tm),
        in_specs=[pl.BlockSpec((tm, K), lambda j, i: (i, 0)), pl.BlockSpec((1, K), lambda j, i: (0, 0)),
                  pl.BlockSpec((tn, K), lambda j, i: (j, 0))],
        out_specs=[pl.BlockSpec((None, tm, K), lambda j, i: (j, i, 0)), pl.BlockSpec((tm, tn), lambda j, i: (i, j))],
        out_shape=[jax.ShapeDtypeStruct((N // tn, T, K), BF16), jax.ShapeDtypeStruct((T, N), F32)],
        sem=("arbitrary", "arbitrary"), rides=rides,
    )(a, g, b)


def _norm_matmul_sq(a, g, b, *, tm, tn, name, rides=()):
    T, K = a.shape
    per = b.shape[2] // tn
    N = b.shape[0] * b.shape[2]

    def body(a_ref, g_ref, b_ref, nt_ref, o_ref, z_ref, zt_ref, n_scr):
        @pl.when(pl.program_id(1) == 0)
        def _():
            n = _norm_bf16(a_ref, g_ref)
            n_scr[...] = n
            nt_ref[...] = n.T
        r = jnp.maximum(_dot(n_scr[...], b_ref[...]), 0.0)
        o_ref[...] = r.astype(BF16)
        z = (r * r).astype(BF16)
        z_ref[...] = z
        zt_ref[...] = z.T

    return _call(
        body, name=name, grid=(T // tm, N // tn),
        in_specs=[pl.BlockSpec((tm, K), lambda i, j: (i, 0)), pl.BlockSpec((1, K), lambda i, j: (0, 0)),
                  pl.BlockSpec((None, K, tn), lambda i, j: (j // per, 0, j % per))],
        out_specs=[pl.BlockSpec((K, tm), lambda i, j: (0, i)), pl.BlockSpec((tm, tn), lambda i, j: (i, j)),
                   pl.BlockSpec((tm, tn), lambda i, j: (i, j)), pl.BlockSpec((tn, tm), lambda i, j: (j, i))],
        out_shape=[jax.ShapeDtypeStruct((K, T), BF16), jax.ShapeDtypeStruct((T, N), BF16),
                   jax.ShapeDtypeStruct((T, N), BF16), jax.ShapeDtypeStruct((N, T), BF16)],
        scratch_shapes=[pltpu.VMEM((tm, K), BF16)],
        sem=("parallel", "arbitrary"), rides=rides,
    )(a, g, b)


def _grad_pair(at, at_sib, b, b_sib, *, cols_sharded, tmo, tk, name, rides=()):
    S, _, hr, T = at.shape
    C = b.shape[-1] // N_CHIPS if cols_sharded else b.shape[-1]
    nk = T // tk
    a_sel = (lambda s: 0) if cols_sharded else (lambda s: s)
    b_sel = (lambda s: s) if cols_sharded else (lambda s: 0)
    if b.ndim == 3:
        b_spec = pl.BlockSpec((None, tk, C), lambda s, i, k: (0, k, b_sel(s)))
    else:
        b_spec = pl.BlockSpec((tk, C), lambda s, i, k: (k, b_sel(s)))

    def body(a_ref, as_ref, b_ref, bs_ref, o_ref, ob_ref):
        k = pl.program_id(2)
        p = _dot(a_ref[...], b_ref[...]) + _dot(as_ref[...], bs_ref[...])

        @pl.when(k == 0)
        def _():
            o_ref[...] = p

        @pl.when(k > 0)
        def _():
            o_ref[...] += p

        @pl.when(k == nk - 1)
        def _():
            ob_ref[...] = o_ref[...].astype(BF16)

    out = pl.BlockSpec((None, tmo, C), lambda s, i, k: (s, i, 0))
    return _call(
        body, name=name, grid=(N_CHIPS, hr // tmo, nk),
        in_specs=[pl.BlockSpec((None, None, tmo, tk), lambda s, i, k: (a_sel(s), lax.axis_index("c"), i, k)),
                  pl.BlockSpec((None, tmo, tk), lambda s, i, k: (a_sel(s), i, k)),
                  b_spec, pl.BlockSpec((tk, C), lambda s, i, k: (k, b_sel(s)))],
        out_specs=[out, out],
        out_shape=[jax.ShapeDtypeStruct((N_CHIPS, hr, C), F32), jax.ShapeDtypeStruct((N_CHIPS, hr, C), BF16)],
        sem=("parallel", "parallel", "arbitrary"), rides=rides,
    )(at, at_sib, b, b_sib)


def _grad_pair_merged(at, at_sib, b, b_sib, *, tk, name, rides=()):
    S, _, hr, T = at.shape
    C = b.shape[-1]
    nk = T // tk

    def body(a_ref, as_ref, b_ref, bs_ref, o_ref, ob_ref):
        k = pl.program_id(0)
        p = (_dot(a_ref[...].reshape(S * hr, tk), b_ref[...])
             + _dot(as_ref[...].reshape(S * hr, tk), bs_ref[...])).reshape(S, hr, C)

        @pl.when(k == 0)
        def _():
            o_ref[...] = p

        @pl.when(k > 0)
        def _():
            o_ref[...] += p

        @pl.when(k == nk - 1)
        def _():
            ob_ref[...] = o_ref[...].astype(BF16)

    out = pl.BlockSpec((S, hr, C), lambda k: (0, 0, 0))
    return _call(
        body, name=name, grid=(nk,),
        in_specs=[pl.BlockSpec((S, None, hr, tk), lambda k: (0, lax.axis_index("c"), 0, k)),
                  pl.BlockSpec((S, hr, tk), lambda k: (0, 0, k)),
                  pl.BlockSpec((tk, C), lambda k: (k, 0)), pl.BlockSpec((tk, C), lambda k: (k, 0))],
        out_specs=[out, out],
        out_shape=[jax.ShapeDtypeStruct((S, hr, C), F32), jax.ShapeDtypeStruct((S, hr, C), BF16)],
        sem=("arbitrary",), rides=rides,
    )(at, at_sib, b, b_sib)


def _matmul_parts(parts, b, *, tm, tn, name, rides=()):
    T = parts[0].shape[0]
    N = b.shape[1]
    offs = [sum(p.shape[1] for p in parts[:i]) for i in range(len(parts))]
    assert all(o % p.shape[1] == 0 for o, p in zip(offs, parts))

    def body(*refs):
        n = len(parts)
        acc = _dot(refs[0][...], refs[n][...])
        for i in range(1, n):
            acc = acc + _dot(refs[i][...], refs[n + i][...])
        refs[-1][...] = acc

    a_specs = [pl.BlockSpec((tm, p.shape[1]), lambda i, j: (i, 0)) for p in parts]
    b_specs = [pl.BlockSpec((p.shape[1], tn), lambda i, j, r=o // p.shape[1]: (r, j)) for o, p in zip(offs, parts)]
    return _call(
        body, name=name, grid=(T // tm, N // tn), in_specs=a_specs + b_specs,
        out_specs=pl.BlockSpec((tm, tn), lambda i, j: (i, j)), out_shape=jax.ShapeDtypeStruct((T, N), F32),
        sem=("parallel", "parallel"), rides=rides,
    )(*parts, *([b] * len(parts)))


def _to_bf16(v):
    return v.astype(BF16)


def _matmul_res(a, b, res, *, tm, tn, tk, prologue, name, rides=()):
    T, K = a.shape
    N = b.shape[1]

    def body(a_ref, b_ref, res_ref, o_ref):
        k = pl.program_id(2)
        p = _dot(prologue(a_ref[...]), b_ref[...])

        @pl.when(k == 0)
        def _():
            o_ref[...] = res_ref[...] + p

        @pl.when(k > 0)
        def _():
            o_ref[...] += p

    return _call(
        body, name=name, grid=(T // tm, N // tn, K // tk),
        in_specs=[pl.BlockSpec((tm, tk), lambda i, j, k: (i, k)), pl.BlockSpec((tk, tn), lambda i, j, k: (k, j)),
                  pl.BlockSpec((tm, tn), lambda i, j, k: (i, j))],
        out_specs=pl.BlockSpec((tm, tn), lambda i, j, k: (i, j)),
        out_shape=jax.ShapeDtypeStruct((T, N), F32),
        sem=("parallel", "parallel", "arbitrary"), rides=rides,
    )(a, b, res)


def _matmul_nt(a, b, *, tm, tn, tk, name, extra=None, epilogue=None, out_dtype=F32, rides=()):
    T, K = a.shape
    two = b.ndim == 3 and tk == 2 * b.shape[2]
    if two:
        N, ks = b.shape[1], b.shape[2]
        b_specs = [pl.BlockSpec((None, tn, ks), lambda i, j, k: (2 * k, j, 0)),
                   pl.BlockSpec((None, tn, ks), lambda i, j, k: (2 * k + 1, j, 0))]
    elif b.ndim == 3:
        per = b.shape[2] // tk
        N = b.shape[1]
        b_specs = [pl.BlockSpec((None, tn, tk), lambda i, j, k: (k // per, j, k % per))]
    else:
        N = b.shape[0]
        b_specs = [pl.BlockSpec((tn, tk), lambda i, j, k: (j, k))]
    nb = len(b_specs)
    nk = K // tk
    assert out_dtype == F32 or nk == 1
    in_specs = [pl.BlockSpec((tm, tk), lambda i, j, k: (i, k))] + b_specs
    args = [a] + [b] * nb
    if extra is not None:
        in_specs.append(pl.BlockSpec((tm, tn), lambda i, j, k: (i, j)))
        args.append(extra)

    def body(*refs):
        a_ref, b_ref = refs[0], refs[1]
        o_ref = refs[-1]
        if two:
            p = (_dot_nt(a_ref[:, :tk // 2].astype(BF16), refs[1][...])
                 + _dot_nt(a_ref[:, tk // 2:].astype(BF16), refs[2][...]))
        else:
            p = _dot_nt(a_ref[...].astype(BF16), b_ref[...])
        if nk == 1:
            if epilogue is not None:
                p = epilogue(p, refs[1 + nb][...])
            o_ref[...] = p.astype(out_dtype)
        else:
            k = pl.program_id(2)

            @pl.when(k == 0)
            def _():
                o_ref[...] = p

            @pl.when(k > 0)
            def _():
                o_ref[...] += p

    return _call(
        body, name=name, grid=(T // tm, N // tn, nk),
        in_specs=in_specs,
        out_specs=pl.BlockSpec((tm, tn), lambda i, j, k: (i, j)),
        out_shape=jax.ShapeDtypeStruct((T, N), out_dtype),
        sem=("parallel", "parallel", "arbitrary"), rides=rides,
    )(*args)


def _loss_bwd(h2, tgt, g, *, tm):
    T, D = h2.shape

    def body(h_ref, t_ref, g_ref, dh_ref, dhb_ref, dg_ref, loss_ref):
        @pl.when(pl.program_id(0) == 0)
        def _():
            dg_ref[...] = jnp.zeros_like(dg_ref)
            loss_ref[...] = jnp.zeros_like(loss_ref)
        h = h_ref[...]
        gg = g_ref[...]
        r = lax.rsqrt(jnp.mean(h * h, axis=-1, keepdims=True) + EPS)
        hn = h * r
        err = hn * gg - t_ref[...]
        loss_ref[...] += 0.5 * jnp.sum(jnp.mean(err * err, axis=-1, keepdims=True), axis=0, keepdims=True)
        dy = err * (1.0 / D)
        dg_ref[...] += jnp.sum(dy * hn, axis=0, keepdims=True)
        w = dy * gg
        dh = r * w - h * ((r * r * r) * jnp.mean(w * h, axis=-1, keepdims=True))
        dh_ref[...] = dh
        dhb_ref[...] = dh.astype(BF16)

    tile = pl.BlockSpec((tm, D), lambda i: (i, 0))
    return pl.pallas_call(
        body, name="loss_bwd", grid=(T // tm,),
        in_specs=[tile, tile, pl.BlockSpec((1, D), lambda i: (0, 0))],
        out_specs=[tile, tile, pl.BlockSpec((1, D), lambda i: (0, 0)), pl.BlockSpec((1, 1), lambda i: (0, 0))],
        out_shape=[jax.ShapeDtypeStruct((T, D), F32), jax.ShapeDtypeStruct((T, D), BF16),
                   jax.ShapeDtypeStruct((1, D), F32), jax.ShapeDtypeStruct((1, 1), F32)],
        compiler_params=_params(("arbitrary",)),
    )(h2, tgt, g)


def _rms_bwd_res(dn, h, g, dres, *, tm, name, bf16_copy=True, rides=()):
    T, D = h.shape

    def body(dn_ref, h_ref, g_ref, dres_ref, dh_ref, *rest):
        dg_ref = rest[-1]

        @pl.when(pl.program_id(0) == 0)
        def _():
            dg_ref[...] = jnp.zeros_like(dg_ref)
        h_ = h_ref[...]
        dn_ = dn_ref[...]
        dh, r = _rms_bwd(dn_, h_, g_ref[...])
        dg_ref[...] += jnp.sum(dn_ * (h_ * r), axis=0, keepdims=True)
        dh = dres_ref[...] + dh
        dh_ref[...] = dh
        if bf16_copy:
            rest[0][...] = dh.astype(BF16)

    tile = pl.BlockSpec((tm, D), lambda i: (i, 0))
    row = pl.BlockSpec((1, D), lambda i: (0, 0))
    copy_spec = [tile] if bf16_copy else []
    copy_shape = [jax.ShapeDtypeStruct((T, D), BF16)] if bf16_copy else []
    return _call(
        body, name=name, grid=(T // tm,),
        in_specs=[tile, tile, row, tile], out_specs=[tile] + copy_spec + [row],
        out_shape=[jax.ShapeDtypeStruct((T, D), F32)] + copy_shape + [jax.ShapeDtypeStruct((1, D), F32)],
        sem=("arbitrary",), rides=rides,
    )(dn, h, g, dres)


def _rel_distance():
    i = lax.broadcasted_iota(jnp.int32, (CHUNK, 2 * CHUNK), 0)
    j = lax.broadcasted_iota(jnp.int32, (CHUNK, 2 * CHUNK), 1)
    return i + CHUNK - j


def _bias_build(table):
    def body(tab_ref, o_ref):
        rel = _rel_distance()
        j = lax.broadcasted_iota(jnp.int32, (CHUNK, 2 * CHUNK), 1)
        band = (rel >= 0) & (rel < CHUNK)
        ge = [rel >= t for t in BUCKET_THR]
        for h in range(B_HEADS):
            cur = jnp.full((CHUNK, 2 * CHUNK), tab_ref[0, h], F32)
            for b in range(1, N_BUCKETS):
                cur = jnp.where(ge[b - 1], tab_ref[b, h], cur)
            o_ref[0, h] = jnp.where(band & (j >= CHUNK), cur, NEG)
            o_ref[1, h] = jnp.where(band, cur, NEG)

    return pl.pallas_call(
        body, name="bias_build",
        in_specs=[pl.BlockSpec(memory_space=pltpu.SMEM)],
        out_specs=pl.BlockSpec(memory_space=pltpu.VMEM),
        out_shape=jax.ShapeDtypeStruct((2, B_HEADS, CHUNK, 2 * CHUNK), F32),
    )(table)


def _bias_grad(dbias):
    def body(db_ref, o_ref, acc_ref):
        rel = _rel_distance()
        lo = [0] + BUCKET_THR
        hi = BUCKET_THR + [CHUNK]
        for b in range(N_BUCKETS):
            m = (rel >= lo[b]) & (rel < hi[b])
            for h in range(B_HEADS):
                row = b * B_HEADS + h
                acc_ref[row:row + 1, :] = jnp.sum(jnp.where(m, db_ref[h], 0.0), axis=0, keepdims=True)
        o_ref[...] = jnp.sum(acc_ref[...], axis=1, keepdims=True)

    return pl.pallas_call(
        body, name="bias_grad",
        in_specs=[pl.BlockSpec(memory_space=pltpu.VMEM)],
        out_specs=pl.BlockSpec(memory_space=pltpu.VMEM),
        out_shape=jax.ShapeDtypeStruct((N_BUCKETS * B_HEADS, 1), F32),
        scratch_shapes=[pltpu.VMEM((N_BUCKETS * B_HEADS, 2 * CHUNK), F32)],
    )(dbias)


def _causal_mask():
    t = lax.broadcasted_iota(jnp.int32, (CHUNK, CHUNK), 0)
    s = lax.broadcasted_iota(jnp.int32, (CHUNK, CHUNK), 1)
    return s <= t


def _gate_forward(u, v, lg, lb, wc, bs):
    ug = _gelu(u)
    vg = _gelu(v)
    mu = jnp.mean(vg, axis=-1, keepdims=True)
    xc = vg - mu
    rstd = lax.rsqrt(jnp.mean(xc * xc, axis=-1, keepdims=True) + EPS)
    xhat = xc * rstd
    vl = (xhat * lg + lb).astype(BF16)
    mixed = _dot(wc, vl) + bs
    return ug, xhat, rstd, vl, mixed


def _softmax_scores(qk, bias, sink):
    s = qk * SCALE + bias
    m = jnp.maximum(jnp.max(s, axis=-1, keepdims=True), sink)
    p = jnp.exp(s - m)
    e_sink = jnp.exp(sink - m)
    inv = 1.0 / (jnp.sum(p, axis=-1, keepdims=True) + e_sink)
    return p * inv, e_sink * inv


PAIRS = Q_PER_KV // 2


def _head(g, pr, e):
    return g * Q_PER_KV + 2 * pr + e


def _stack_pairs(ref, g, col0=0):
    w = 2 * HEAD_DIM
    return jnp.concatenate([ref[:, col0 + (g * PAIRS + pr) * w:col0 + (g * PAIRS + pr + 1) * w] for pr in range(PAIRS)],
                           axis=0)


def _low_lanes():
    return lax.broadcasted_iota(jnp.int32, (2 * CHUNK, 2 * HEAD_DIM), 1) < HEAD_DIM


def _band_operands(kv_prev, kv_cur):
    band = jnp.concatenate([kv_prev, kv_cur], axis=0)
    low = _low_lanes()
    ops = []
    for cat in (band[:, :KV_WIDTH], band[:, KV_WIDTH:]):
        rol = pltpu.roll(cat, HEAD_DIM, 1)
        ops.append([[jnp.where(low if e == 0 else ~low, cat if g == e else rol, 0.0).astype(BF16) for e in range(2)]
                    for g in range(2)])
    return ops


def _mixer_fwd(proj, lg, lb, wsp, bs_col, sinks, bias, ga, gb, rides=()):
    T = proj.shape[0]
    nb = T // CHUNK

    def body(u_ref, v_ref, q_ref, kvc_ref, kvp_ref, lg_ref, lb_ref, w_ref, bs_ref, sink_ref, bias_ref,
             ga_ref, gb_ref, mixed_ref, mixed_t_ref, ab_ref):
        causal = _causal_mask()
        ssq = jnp.zeros((CHUNK, 1), F32)
        for g in range(A_GROUPS):
            cols = slice(g * CHUNK, (g + 1) * CHUNK)
            wc = jnp.where(causal, w_ref[g], 0.0).astype(BF16)
            ug, _, _, _, mixed = _gate_forward(u_ref[:, cols], v_ref[:, cols], lg_ref[g:g + 1, :], lb_ref[g:g + 1, :],
                                               wc, bs_ref[g])
            a = ug * mixed
            ab_ref[:, cols] = a
            ssq = ssq + jnp.sum(a * a, axis=-1, keepdims=True)
        ra = lax.rsqrt(ssq * (1.0 / A_WIDTH) + EPS)
        mixed_ref[:, :A_WIDTH] = ((ab_ref[:, :A_WIDTH] * ra) * ga_ref[...]).astype(BF16)

        kops, vops = _band_operands(kvp_ref[...], kvc_ref[...])
        ssq = jnp.zeros((CHUNK, 1), F32)
        for g in range(B_HEADS // Q_PER_KV):
            qst = _stack_pairs(q_ref, g).astype(BF16)
            o_st = jnp.zeros((PAIRS * CHUNK, 2 * HEAD_DIM), F32)
            for e in range(2):
                s_all = _dot_nt(qst, kops[g][e])
                ps = []
                for pr in range(PAIRS):
                    h = _head(g, pr, e)
                    p, _ = _softmax_scores(s_all[pr * CHUNK:(pr + 1) * CHUNK], bias_ref[h], sink_ref[0, h])
                    ps.append(p.astype(BF16))
                o_st = o_st + _dot(jnp.concatenate(ps, axis=0), vops[g][e])
            for pr in range(PAIRS):
                o = o_st[pr * CHUNK:(pr + 1) * CHUNK]
                c0 = A_WIDTH + (g * PAIRS + pr) * 2 * HEAD_DIM
                ab_ref[:, c0:c0 + 2 * HEAD_DIM] = o
                ssq = ssq + jnp.sum(o * o, axis=-1, keepdims=True)
        rb = lax.rsqrt(ssq * (1.0 / B_WIDTH) + EPS)
        mixed_ref[:, A_WIDTH:] = ((ab_ref[:, A_WIDTH:] * rb) * gb_ref[...]).astype(BF16)
        mixed_t_ref[...] = mixed_ref[...].T

    full = lambda *shape: pl.BlockSpec(shape, lambda n: (0,) * len(shape))
    return _call(
        body, name="mixer_fwd", grid=(nb,),
        in_specs=[pl.BlockSpec((CHUNK, A_WIDTH), lambda n: (n, 0)),
                  pl.BlockSpec((CHUNK, A_WIDTH), lambda n: (n, 1)),
                  pl.BlockSpec((CHUNK, B_WIDTH), lambda n: (n, 2)),
                  pl.BlockSpec((CHUNK, 2 * KV_WIDTH), lambda n: (n, 12)),
                  pl.BlockSpec((CHUNK, 2 * KV_WIDTH), lambda n: (jnp.maximum(n - 1, 0), 12)),
                  full(A_GROUPS, CHUNK), full(A_GROUPS, CHUNK), full(A_GROUPS, CHUNK, CHUNK), full(A_GROUPS, CHUNK, 1),
                  pl.BlockSpec(memory_space=pltpu.SMEM),
                  pl.BlockSpec((None, B_HEADS, CHUNK, 2 * CHUNK), lambda n: (jnp.minimum(n, 1), 0, 0, 0)),
                  full(1, A_WIDTH), full(1, B_WIDTH)],
        out_specs=[pl.BlockSpec((CHUNK, D_MODEL), lambda n: (n, 0)), pl.BlockSpec((D_MODEL, CHUNK), lambda n: (0, n)),
                   pl.BlockSpec((CHUNK, D_MODEL), lambda n: (n, 0))],
        out_shape=[jax.ShapeDtypeStruct((T, D_MODEL), BF16), jax.ShapeDtypeStruct((D_MODEL, T), BF16),
                   jax.ShapeDtypeStruct((T, D_MODEL), F32)],
        sem=("parallel",), rides=rides,
    )(proj, proj, proj, proj, proj, lg, lb, wsp, bs_col, sinks, bias, ga, gb)


def _gmlp_bwd(proj, ab, dmixed, ga, lg, lb, wsp, bs_col, rides=()):
    T = proj.shape[0]
    nb = T // CHUNK

    def body(u_ref, v_ref, a_ref, dna_ref, ga_ref, lg_ref, lb_ref, w_ref, bs_ref,
             dp_ref, dpt_ref, dga_ref, dw_ref, dbs_ref, dlg_ref, dlb_ref):
        @pl.when(pl.program_id(0) == 0)
        def _():
            for r in (dga_ref, dw_ref, dbs_ref, dlg_ref, dlb_ref):
                r[...] = jnp.zeros_like(r)
        causal = _causal_mask()
        a_all = a_ref[...]
        dna = dna_ref[...]
        da_all, ra = _rms_bwd(dna, a_all, ga_ref[...])
        dga_ref[...] += jnp.sum(dna * (a_all * ra), axis=0, keepdims=True)
        for g in range(A_GROUPS):
            cols = slice(g * CHUNK, (g + 1) * CHUNK)
            wc = jnp.where(causal, w_ref[g], 0.0).astype(BF16)
            lgg = lg_ref[g:g + 1, :]
            u = u_ref[:, cols]
            v = v_ref[:, cols]
            ug, xhat, rstd, vl, mixed = _gate_forward(u, v, lgg, lb_ref[g:g + 1, :], wc, bs_ref[g])
            da = da_all[:, cols]
            dug = da * mixed
            dmg = da * ug
            dmg_b = dmg.astype(BF16)
            dbs_ref[g] += jnp.sum(dmg, axis=-1, keepdims=True)
            dw_ref[g] += jnp.where(causal, _dot_nt(dmg_b, vl), 0.0)
            dvl = _dot_tn(wc, dmg_b)
            dlg_ref[g:g + 1, :] += jnp.sum(dvl * xhat, axis=0, keepdims=True)
            dlb_ref[g:g + 1, :] += jnp.sum(dvl, axis=0, keepdims=True)
            dxh = dvl * lgg
            dvg = rstd * (dxh - jnp.mean(dxh, axis=-1, keepdims=True)
                          - xhat * jnp.mean(dxh * xhat, axis=-1, keepdims=True))
            _, gu = _gelu_and_grad(u)
            _, gv = _gelu_and_grad(v)
            dp_ref[:, cols] = (dug * gu).astype(BF16)
            dp_ref[:, A_WIDTH + g * CHUNK:A_WIDTH + (g + 1) * CHUNK] = (dvg * gv).astype(BF16)
        dpt_ref[...] = dp_ref[...].T

    full = lambda *shape: pl.BlockSpec(shape, lambda n: (0,) * len(shape))
    return _call(
        body, name="gmlp_bwd", grid=(nb,),
        in_specs=[pl.BlockSpec((CHUNK, A_WIDTH), lambda n: (n, 0)),
                  pl.BlockSpec((CHUNK, A_WIDTH), lambda n: (n, 1)),
                  pl.BlockSpec((CHUNK, A_WIDTH), lambda n: (n, 0)),
                  pl.BlockSpec((CHUNK, A_WIDTH), lambda n: (n, 0)),
                  full(1, A_WIDTH), full(A_GROUPS, CHUNK), full(A_GROUPS, CHUNK), full(A_GROUPS, CHUNK, CHUNK),
                  full(A_GROUPS, CHUNK, 1)],
        out_specs=[pl.BlockSpec((CHUNK, 2 * A_WIDTH), lambda n: (n, 0)), pl.BlockSpec((2 * A_WIDTH, CHUNK), lambda n: (0, n)),
                   full(1, A_WIDTH), full(A_GROUPS, CHUNK, CHUNK), full(A_GROUPS, CHUNK, 1),
                   full(A_GROUPS, CHUNK), full(A_GROUPS, CHUNK)],
        out_shape=[jax.ShapeDtypeStruct((T, 2 * A_WIDTH), BF16), jax.ShapeDtypeStruct((2 * A_WIDTH, T), BF16),
                   jax.ShapeDtypeStruct((1, A_WIDTH), F32), jax.ShapeDtypeStruct((A_GROUPS, CHUNK, CHUNK), F32),
                   jax.ShapeDtypeStruct((A_GROUPS, CHUNK, 1), F32), jax.ShapeDtypeStruct((A_GROUPS, CHUNK), F32),
                   jax.ShapeDtypeStruct((A_GROUPS, CHUNK), F32)],
        sem=("arbitrary",), rides=rides,
    )(proj, proj, ab, dmixed, ga, lg, lb, wsp, bs_col)


def _attn_bwd(proj, ab, dmixed, gb, sinks, bias, rides=()):
    T = proj.shape[0]
    nb = T // CHUNK
    qn = lambda n: jnp.minimum(n, nb - 1)

    def body(q_ref, kvc_ref, kvp_ref, o_ref, dnb_ref, gb_ref, sink_ref, bias_ref,
             dq_ref, dkv_ref, dqt_ref, dkvt_ref, dgb_ref, dsink_ref, dbias_ref, carry_ref, sacc_ref):
        n = pl.program_id(0)

        @pl.when(n == 0)
        def _():
            carry_ref[...] = jnp.zeros_like(carry_ref)
            sacc_ref[...] = jnp.zeros_like(sacc_ref)
            dgb_ref[...] = jnp.zeros_like(dgb_ref)
            dbias_ref[...] = jnp.zeros_like(dbias_ref)

        @pl.when(n < nb)
        def _():
            o_all = o_ref[...]
            dnb = dnb_ref[...]
            do_all, rb = _rms_bwd(dnb, o_all, gb_ref[...])
            dgb_ref[...] += jnp.sum(dnb * (o_all * rb), axis=0, keepdims=True)
            kops, vops = _band_operands(kvp_ref[...], kvc_ref[...])
            low = _low_lanes()
            halves = []
            for g in range(B_HEADS // Q_PER_KV):
                qst = _stack_pairs(q_ref, g).astype(BF16)
                dost = _stack_pairs(do_all, g).astype(BF16)
                dq_st = jnp.zeros((PAIRS * CHUNK, 2 * HEAD_DIM), F32)
                dk_e, dv_e = [], []
                for e in range(2):
                    s_all = _dot_nt(qst, kops[g][e])
                    dp_all = _dot_nt(dost, vops[g][e])
                    ps, dsrs = [], []
                    for pr in range(PAIRS):
                        h = _head(g, pr, e)
                        rows = slice(pr * CHUNK, (pr + 1) * CHUNK)
                        p, p_sink = _softmax_scores(s_all[rows], bias_ref[h], sink_ref[0, h])
                        dp = dp_all[rows]
                        delta = jnp.sum(p * dp, axis=-1, keepdims=True)
                        ds = p * (dp - delta)
                        sacc_ref[:, h:h + 1] += -(p_sink * delta)
                        dbias_ref[h] += ds
                        ps.append(p.astype(BF16))
                        dsrs.append((ds * SCALE).astype(BF16))
                    dsr_all = jnp.concatenate(dsrs, axis=0)
                    dq_st = dq_st + _dot(dsr_all, kops[g][e])
                    dk_e.append(_dot_tn(dsr_all, qst))
                    dv_e.append(_dot_tn(jnp.concatenate(ps, axis=0), dost))
                for pr in range(PAIRS):
                    c0 = (g * PAIRS + pr) * 2 * HEAD_DIM
                    dq_ref[:, c0:c0 + 2 * HEAD_DIM] = dq_st[pr * CHUNK:(pr + 1) * CHUNK].astype(BF16)
                halves.append((dk_e, dv_e))
            tiles = []
            for t in range(2):
                g0, g1 = halves[0][t], halves[1][t]
                tiles.append(jnp.where(low, g0[0] + pltpu.roll(g0[1], HEAD_DIM, 1), pltpu.roll(g1[0], HEAD_DIM, 1) + g1[1]))
            dband = jnp.concatenate(tiles, axis=1)
            dkv = (carry_ref[...] + dband[:CHUNK]).astype(BF16)
            dkv_ref[...] = dkv
            dkvt_ref[...] = dkv.T
            dqt_ref[...] = dq_ref[...].T
            carry_ref[...] = dband[CHUNK:]

        @pl.when(n == nb)
        def _():
            dkv = carry_ref[...].astype(BF16)
            dkv_ref[...] = dkv
            dkvt_ref[...] = dkv.T
            dsink_ref[...] = jnp.sum(sacc_ref[...], axis=0, keepdims=True)

    full = lambda *shape: pl.BlockSpec(shape, lambda n: (0,) * len(shape))
    return _call(
        body, name="attn_bwd", grid=(nb + 1,),
        in_specs=[pl.BlockSpec((CHUNK, B_WIDTH), lambda n: (qn(n), 2)),
                  pl.BlockSpec((CHUNK, 2 * KV_WIDTH), lambda n: (qn(n), 12)),
                  pl.BlockSpec((CHUNK, 2 * KV_WIDTH), lambda n: (jnp.maximum(qn(n) - 1, 0), 12)),
                  pl.BlockSpec((CHUNK, B_WIDTH), lambda n: (qn(n), 1)),
                  pl.BlockSpec((CHUNK, B_WIDTH), lambda n: (qn(n), 1)),
                  full(1, B_WIDTH), pl.BlockSpec(memory_space=pltpu.SMEM),
                  pl.BlockSpec((None, B_HEADS, CHUNK, 2 * CHUNK), lambda n: (jnp.minimum(n, 1), 0, 0, 0))],
        out_specs=[pl.BlockSpec((CHUNK, B_WIDTH), lambda n: (qn(n), 0)),
                   pl.BlockSpec((CHUNK, 2 * KV_WIDTH), lambda n: (jnp.maximum(n - 1, 0), 0)),
                   pl.BlockSpec((B_WIDTH, CHUNK), lambda n: (0, qn(n))),
                   pl.BlockSpec((2 * KV_WIDTH, CHUNK), lambda n: (0, jnp.maximum(n - 1, 0))),
                   full(1, B_WIDTH), full(1, B_HEADS), full(B_HEADS, CHUNK, 2 * CHUNK)],
        out_shape=[jax.ShapeDtypeStruct((T, B_WIDTH), BF16), jax.ShapeDtypeStruct((T, 2 * KV_WIDTH), BF16),
                   jax.ShapeDtypeStruct((B_WIDTH, T), BF16), jax.ShapeDtypeStruct((2 * KV_WIDTH, T), BF16),
                   jax.ShapeDtypeStruct((1, B_WIDTH), F32), jax.ShapeDtypeStruct((1, B_HEADS), F32),
                   jax.ShapeDtypeStruct((B_HEADS, CHUNK, 2 * CHUNK), F32)],
        scratch_shapes=[pltpu.VMEM((CHUNK, 2 * KV_WIDTH), F32), pltpu.VMEM((CHUNK, B_HEADS), F32)],
        sem=("arbitrary",), rides=rides,
    )(proj, proj, proj, ab, dmixed, gb, sinks, bias)


def _sq_relu_grad(acc, r):
    return acc * (2.0 * r.astype(F32))


def _chip_index():
    return (2 * lax.axis_index("x") + lax.axis_index("y")).astype(jnp.int32).reshape(1)


def _cast_into_slot(w, *, tm, name):
    _, R, C = w.shape

    def body(me_ref, w_ref, o_ref):
        del me_ref
        o_ref[...] = w_ref[...].astype(BF16)

    return pl.pallas_call(
        body, name=name,
        grid_spec=pltpu.PrefetchScalarGridSpec(
            num_scalar_prefetch=1, grid=(R // tm,),
            in_specs=[pl.BlockSpec((None, tm, C), lambda i, me: (0, i, 0))],
            out_specs=pl.BlockSpec((None, tm, C), lambda i, me: (me[0], i, 0))),
        out_shape=jax.ShapeDtypeStruct((N_CHIPS, R, C), BF16), compiler_params=_params(("parallel",)),
    )(_chip_index(), w)


def _cast_into_slots_carrying(ws, *, steps, name, rides):
    n = len(ws)

    def body(*refs):
        for w_ref, o_ref in zip(refs[:n], refs[n:]):
            o_ref[...] = w_ref[...].astype(BF16)

    me = lambda: 2 * lax.axis_index("x") + lax.axis_index("y")
    return _call(
        body, name=name, grid=(steps,),
        in_specs=[pl.BlockSpec((None, w.shape[1] // steps, w.shape[2]), lambda i: (0, i, 0)) for w in ws],
        out_specs=[pl.BlockSpec((None, w.shape[1] // steps, w.shape[2]), lambda i: (me(), i, 0)) for w in ws],
        out_shape=[jax.ShapeDtypeStruct((N_CHIPS,) + w.shape[1:], BF16) for w in ws], sem=("arbitrary",), rides=rides,
    )(*ws)


def _owner_total(gh, others, *, tm, name):
    _, hr, C = gh.shape

    def body(me_ref, g_ref, o_ref_in, out_ref):
        del me_ref
        acc = g_ref[...]
        for j in range(3):
            acc = acc + o_ref_in[j].astype(F32)
        out_ref[...] = acc

    return pl.pallas_call(
        body, name=name,
        grid_spec=pltpu.PrefetchScalarGridSpec(
            num_scalar_prefetch=1, grid=(hr // tm,),
            in_specs=[pl.BlockSpec((None, tm, C), lambda i, me: (me[0], i, 0)),
                      pl.BlockSpec((3, tm, C), lambda i, me: (0, i, 0))],
            out_specs=pl.BlockSpec((tm, C), lambda i, me: (i, 0))),
        out_shape=jax.ShapeDtypeStruct((hr, C), F32),
        compiler_params=_params(("parallel",)),
    )(_chip_index(), gh, others)


def _adamw_math(w, g, m, v):
    m = ADAM_B1 * m + (1.0 - ADAM_B1) * g
    v = ADAM_B2 * v + (1.0 - ADAM_B2) * (g * g)
    m_hat = m / (1.0 - ADAM_B1 ** ADAM_STEP)
    v_hat = v / (1.0 - ADAM_B2 ** ADAM_STEP)
    delta = -ADAM_LR * (m_hat / (jnp.sqrt(v_hat) + ADAM_EPS) + ADAM_WD * w)
    return delta, m, v


def _adamw_halves(w, own, got, m, v, *, tm, name, rides=()):
    _, R, C = w.shape
    nt = (R // 2) // tm

    def body(w_ref, own_ref, got_ref, m_ref, v_ref, g_ref, d_ref, nm_ref, nv_ref):
        g = jnp.where(pl.program_id(0) == lax.axis_index("c"), own_ref[...], got_ref[...])
        g_ref[...] = g
        d_ref[...], nm_ref[...], nv_ref[...] = _adamw_math(w_ref[...], g, m_ref[...], v_ref[...])

    whole = pl.BlockSpec((None, tm, C), lambda h, i: (0, h * nt + i, 0))
    half = pl.BlockSpec((tm, C), lambda h, i: (i, 0))
    return _call(
        body, name=name, grid=(2, nt), in_specs=[whole, half, half, whole, whole], out_specs=[whole] * 4,
        out_shape=[jax.ShapeDtypeStruct((1, R, C), F32)] * 4, sem=("parallel", "parallel"), rides=rides,
    )(w, own, got, m, v)


def _adamw_small(w, slots, m, v, *, name):
    def body(w_ref, slots_ref, m_ref, v_ref, g_ref, d_ref, nm_ref, nv_ref):
        g = slots_ref[0]
        for d in range(1, N_DEV):
            g = g + slots_ref[d]
        g_ref[...] = g
        d_ref[...], nm_ref[...], nv_ref[...] = _adamw_math(w_ref[...], g, m_ref[...], v_ref[...])

    vmem = pl.BlockSpec(memory_space=pltpu.VMEM)
    return pl.pallas_call(
        body, name=name, in_specs=[vmem] * 4, out_specs=[vmem] * 4,
        out_shape=[jax.ShapeDtypeStruct(w.shape, F32)] * 4, compiler_params=_params(),
    )(w, slots, m, v)


SMALL = ["rel_bias_table", "mix_norm_g", "gate_norm_g", "gate_norm_b", "w_spatial", "b_spatial", "attn_sinks",
         "out_norm_a_g", "out_norm_b_g", "ffn_norm_g", "final_norm_g"]
SMALL_A = ["gate_norm_g", "gate_norm_b", "w_spatial", "b_spatial", "out_norm_a_g"]
SMALL_B = ["rel_bias_table", "mix_norm_g", "attn_sinks", "out_norm_b_g", "ffn_norm_g", "final_norm_g"]
LARGE = ["w_in", "w_out", "w_up", "w_down"]
ROW_TILE = {"w_in": 208, "w_out": 256, "w_up": 256, "w_down": 256}
WEIGHTS = ["rel_bias_table", "mix_norm_g", "w_in", "gate_norm_g", "gate_norm_b", "w_spatial", "b_spatial", "attn_sinks",
           "out_norm_a_g", "out_norm_b_g", "w_out", "ffn_norm_g", "w_up", "w_down", "final_norm_g"]
PACK_UNIT = 8 * 128


def _pack(parts):
    rows = []
    for p in parts:
        flat = p.reshape(-1)
        pad = (-flat.shape[0]) % PACK_UNIT
        rows.append(jnp.pad(flat, (0, pad)).reshape(-1, 128))
    return jnp.concatenate(rows, axis=0)


def _unpack(packed, like):
    out, row = [], 0
    for p in like:
        n = math.prod(p.shape)
        nrows = (n + PACK_UNIT - 1) // PACK_UNIT * 8
        out.append(packed[row:row + nrows].reshape(-1)[:n].reshape(p.shape))
        row += nrows
    return out


def kernel(x, rel_bias_table, mix_norm_g, w_in, gate_norm_g, gate_norm_b, w_spatial, b_spatial, attn_sinks, out_norm_a_g, out_norm_b_g, w_out, ffn_norm_g, w_up, w_down, final_norm_g, loss_target, m_rel_bias_table, m_mix_norm_g, m_w_in, m_gate_norm_g, m_gate_norm_b, m_w_spatial, m_b_spatial, m_attn_sinks, m_out_norm_a_g, m_out_norm_b_g, m_w_out, m_ffn_norm_g, m_w_up, m_w_down, m_final_norm_g, v_rel_bias_table, v_mix_norm_g, v_w_in, v_gate_norm_g, v_gate_norm_b, v_w_spatial, v_b_spatial, v_attn_sinks, v_out_norm_a_g, v_out_norm_b_g, v_w_out, v_ffn_norm_g, v_w_up, v_w_down, v_final_norm_g):
    args = dict(locals())
    wts = {n: args[n] for n in WEIGHTS}
    mom = {n: args["m_" + n] for n in WEIGHTS}
    var = {n: args["v_" + n] for n in WEIGHTS}
    sp = {n: wts[n] for n in SMALL}
    x2, tgt = x[0], loss_target[0]
    T = x2.shape[0]
    tm = min(512, T)
    tl = min(1024, T)
    lg = sp["gate_norm_g"].reshape(A_GROUPS, CHUNK)
    lb = sp["gate_norm_b"].reshape(A_GROUPS, CHUNK)
    wsp = sp["w_spatial"].reshape(A_GROUPS, CHUNK, CHUNK)
    bs_col = sp["b_spatial"].reshape(A_GROUPS, CHUNK, 1)
    sinks = sp["attn_sinks"].reshape(1, B_HEADS)
    ga = sp["out_norm_a_g"].reshape(1, A_WIDTH)
    gb = sp["out_norm_b_g"].reshape(1, B_WIDTH)
    g1 = sp["mix_norm_g"].reshape(1, D_MODEL)
    g2 = sp["ffn_norm_g"].reshape(1, D_MODEL)
    gf = sp["final_norm_g"].reshape(1, D_MODEL)

    def owner_total(n, gh, others):
        return _owner_total(gh, others, tm=ROW_TILE[n], name="rs_owner_total_" + n)

    def halves_view(at, shards):
        return at.reshape(shards, 2, at.shape[0] // shards // 2, at.shape[1])

    for d in (wts, mom, var):
        d["w_in"] = jnp.swapaxes(d["w_in"], 1, 2)

    s_in = _cast_into_slot(wts["w_in"], tm=ROW_TILE["w_in"], name="cast_w_in")
    (s_out, s_up, s_down), ((g_in,),) = _cast_into_slots_carrying(
        [wts["w_out"], wts["w_up"], wts["w_down"]], steps=8, name="cast_w_rest",
        rides=[_ride_gather(s_in, chain=(0, 1, 1), chain_fracs=(0.3, 0.6))])
    win_t = g_in.reshape(PROJ_WIDTH, D_MODEL)
    bias = _bias_build(sp["rel_bias_table"])
    (n1, proj), ((g_out,), (s_up,)) = _norm_matmul_wide(
        x2, g1, win_t, tm=tm, tn=PROJ_WIDTH // 2, name="in_proj",
        rides=[_ride_gather(s_out, chain=(0, 1, 1), chain_fracs=(0.65, 0.85)), _ride_gather(s_up, s1=(0, 3, 8))])
    wo = g_out.reshape(A_WIDTH + B_WIDTH, D_MODEL)
    (mixed, mixed_t, ab), ((s_up,), (s_down,), (n1_sib,)) = _mixer_fwd(
        proj, lg, lb, wsp, bs_col, sinks, bias, ga, gb,
        rides=[_ride_gather(s_up, s2=(0, 3, 8), s1=(3, 8, 8)), _ride_gather(s_down, s1=(0, 3, 8)),
               _ride_to_sibling(n1, first=True)])
    mixed_t = halves_view(mixed_t, N_CHIPS)
    h1, ((wu,), (s_down,), (mixed_t_sib,)) = _matmul_res(
        mixed, wo, x2, tm=tl, tn=1024, tk=D_MODEL, prologue=_to_bf16, name="out_proj",
        rides=[_ride_gather(s_up, s3=(0, 3, 8), tail=(3, 8, 8), mid_frac=0.75), _ride_gather(s_down, s2=(0, 3, 8)),
               _ride_to_sibling(mixed_t, halves=True)])
    (n2t, zp, z2, z2t), ((g_down,),) = _norm_matmul_sq(
        h1, g2, wu, tm=tl, tn=1024, name="up_proj", rides=[_ride_gather(s_down, s3=(0, 3, 8), chain=(3, 8, 8), chain_fracs=(0.45, 0.75))])
    wd = g_down.reshape(D_FF, D_MODEL)
    n2t, z2t = halves_view(n2t, 1), halves_view(z2t, N_CHIPS)
    h2, ((n2t_sib,), (z2t_sib,)) = _matmul_res(
        z2, wd, h1, tm=tl, tn=1024, tk=4096, prologue=_to_bf16, name="down_proj",
        rides=[_ride_to_sibling(n2t, halves=True), _ride_to_sibling(z2t, halves=True)])

    dh2, dh2b, dgf, loss = _loss_bwd(h2, tgt, gf, tm=tm)
    dzp, ((dh2b_sib,),) = _matmul_nt(dh2b, wd, tm=tl, tn=1024, tk=D_MODEL, name="bwd_dz", extra=zp,
                                     epilogue=_sq_relu_grad, out_dtype=BF16, rides=[_ride_to_sibling(dh2b)])
    (gd, gdb), ((dzp_sib,),) = _grad_pair(z2t, z2t_sib, dh2b, dh2b_sib, cols_sharded=False, tmo=1024, tk=tl,
                                          name="grad_w_down", rides=[_ride_to_sibling(dzp)])
    (gu, gub), ((o_d,),) = _grad_pair(n2t, n2t_sib, dzp, dzp_sib, cols_sharded=True, tmo=1024, tk=tl,
                                      name="grad_w_up", rides=[_ride_scatter(gdb, None, (0, 7, 8))])
    dn2, ((o_d,), (o_u,)) = _matmul_nt(dzp, wu, tm=tl, tn=1024, tk=4096, name="bwd_dn2",
                                       rides=[_ride_scatter(gdb, o_d, (7, 8, 8)), _ride_scatter(gub, None, (0, 6, 8))])
    h_d = owner_total("w_down", gd, o_d)
    (dh1, dh1b, dg2), ((o_u,),) = _rms_bwd_res(dn2, h1, g2, dh2, tm=tm, name="ffn_norm_bwd",
                                               rides=[_ride_scatter(gub, o_u, (6, 7, 8))])
    dmixed, ((o_u,), (dh1b_sib,), (w_d,)) = _matmul_nt(
        dh1b, wo, tm=tl, tn=1024, tk=D_MODEL, name="bwd_dmixed",
        rides=[_ride_scatter(gub, o_u, (7, 8, 8)), _ride_to_sibling(dh1b), _ride_swap(h_d)])
    h_u = owner_total("w_up", gu, o_u)
    (go, gob), ((w_u,),) = _grad_pair_merged(mixed_t, mixed_t_sib, dh1b, dh1b_sib, tk=tl, name="grad_w_out",
                                             rides=[_ride_swap(h_u)])
    (duv, duv_t, dga, dwsp, dbs, dlg, dlb), ((o_o,),) = _gmlp_bwd(proj, ab, dmixed, ga, lg, lb, wsp, bs_col,
                                                                  rides=[_ride_scatter(gob)])
    h_o = owner_total("w_out", go, o_o)
    small = {"gate_norm_g": dlg, "gate_norm_b": dlb, "w_spatial": dwsp, "b_spatial": dbs, "out_norm_a_g": dga}
    hr_in = PROJ_WIDTH // N_CHIPS // 2
    (dq, dkv, dq_t, dkv_t, dgb, dsinks, dbias), ((w_o,), (dproj_t_sib,)) = _attn_bwd(
        proj, ab, dmixed, gb, sinks, bias, rides=[_ride_swap(h_o), _ride_rows_to_sibling(duv_t, hr_in, 2, N_CHIPS)])
    dtable = _bias_grad(dbias)
    dproj_t = halves_view(jnp.concatenate([duv_t, dq_t, dkv_t], axis=0), N_CHIPS)
    ((dproj_t_sib,),) = _carrier([_ride_to_sibling(dproj_t, halves=True, shards=(2, N_CHIPS), land=dproj_t_sib)],
                                 name="trade_dproj_t")
    (gi, gib), ((slots_a,),) = _grad_pair(
        dproj_t, dproj_t_sib, n1, n1_sib, cols_sharded=False, tmo=hr_in, tk=tl, name="grad_w_in",
        rides=[_ride_small_to_all(_pack([small[n] for n in SMALL_A]))])
    dn1, ((o_i,),) = _matmul_parts([duv, dq, dkv], win_t, tm=tl, tn=1024, name="bwd_dn1", rides=[_ride_scatter(gib)])
    h_i = owner_total("w_in", gi, o_i)
    dx, dg1 = _rms_bwd_res(dn1, x2, g1, dh1, tm=tm, name="mix_norm_bwd", bf16_copy=False)
    small.update({"rel_bias_table": dtable.reshape(N_BUCKETS, B_HEADS), "mix_norm_g": dg1, "attn_sinks": dsinks,
                  "out_norm_b_g": dgb, "ffn_norm_g": dg2, "final_norm_g": dgf})
    (w_i,), (slots_b,) = _carrier([_ride_swap(h_i), _ride_small_to_all(_pack([small[n] for n in SMALL_B] + [loss]))],
                                  name="swap_w_in")

    out_g, out_d, out_m, out_v = {}, {}, {}, {}
    for n, h, s in zip(LARGE, [h_i, h_o, h_u, h_d], [w_i, w_o, w_u, w_d]):
        res = _adamw_halves(wts[n], h, s, mom[n], var[n], tm=ROW_TILE[n], name="adamw_" + n)
        if n == "w_in":
            res = [jnp.swapaxes(r, 1, 2) for r in res]
        out_g[n], out_d[n], out_m[n], out_v[n] = res
    for names, slots, tag in ((SMALL_A, slots_a, "a"), (SMALL_B, slots_b, "b")):
        extra = [jnp.zeros((1, 1), F32)] if tag == "b" else []
        like = [wts[n] for n in names] + extra
        res = _adamw_small(_pack(like), slots, _pack([mom[n] for n in names] + extra),
                           _pack([var[n] for n in names] + extra), name="adamw_small_" + tag)
        for store, packed in zip((out_g, out_d, out_m, out_v), res):
            for n, val in zip(names + ["loss"], _unpack(packed, like)):
                store[n] = val

    total = out_g["loss"][0, 0]
    return (total, dx[None], *[out_g[n] for n in WEIGHTS], *[out_d[n] for n in WEIGHTS],
            *[out_m[n] for n in WEIGHTS], *[out_v[n] for n in WEIGHTS])
```

```python
import math

import numpy as np
import jax
import jax.numpy as jnp
from jax import lax
from jax.experimental import pallas as pl
from jax.experimental.pallas import tpu as pltpu

F32 = jnp.float32
BF16 = jnp.bfloat16

D_MODEL = 2048
CHUNK = 128
A_GROUPS = 8
A_WIDTH = 1024
HEAD_DIM = 64
B_HEADS = 16
Q_PER_KV = 8
B_WIDTH = 1024
KV_WIDTH = 128
PROJ_WIDTH = 3328
D_FF = 8192
N_BUCKETS = 32
EPS = 1e-5
NEG = -1e30
SCALE = HEAD_DIM ** -0.5
N_CHIPS = 4
N_DEV = 8

ADAM_LR = 0.001
ADAM_B1 = 0.9
ADAM_B2 = 0.999
ADAM_EPS = 1e-08
ADAM_WD = 0.01
ADAM_STEP = 10

VMEM_LIMIT = 60 * 1024 * 1024
MESH = pl.DeviceIdType.MESH


def _bucket_thresholds():
    d = np.arange(CHUNK)
    n_exact = N_BUCKETS // 2
    relf = np.maximum(d, n_exact).astype(np.float64)
    large = n_exact + (np.log(relf / n_exact) / math.log(CHUNK / n_exact) * (N_BUCKETS - n_exact)).astype(np.int32)
    bucket = np.where(d < n_exact, d, np.minimum(large, N_BUCKETS - 1))
    return [int(np.min(d[bucket >= b])) for b in range(1, N_BUCKETS)]


BUCKET_THR = _bucket_thresholds()


def _params(sem=None):
    return pltpu.CompilerParams(dimension_semantics=sem, vmem_limit_bytes=VMEM_LIMIT)


def _gelu(x):
    c = math.sqrt(2.0 / math.pi)
    return 0.5 * x * (1.0 + jnp.tanh(c * (x + 0.044715 * (x * x * x))))


def _gelu_and_grad(x):
    c = math.sqrt(2.0 / math.pi)
    x2 = x * x
    t = jnp.tanh(c * (x + 0.044715 * (x2 * x)))
    g = 0.5 * x * (1.0 + t)
    dg = 0.5 * (1.0 + t) + 0.5 * x * (1.0 - t * t) * (c * (1.0 + 3.0 * 0.044715 * x2))
    return g, dg


def _dot(a, b):
    return jnp.dot(a, b, preferred_element_type=F32)


def _dot_nt(a, b):
    return lax.dot_general(a, b, (((1,), (1,)), ((), ())), preferred_element_type=F32)


def _dot_tn(a, b):
    return lax.dot_general(a, b, (((0,), (0,)), ((), ())), preferred_element_type=F32)


def _rms_bwd(dn, h, g):
    r = lax.rsqrt(jnp.mean(h * h, axis=-1, keepdims=True) + EPS)
    w = dn * g
    dh = r * w - h * ((r * r * r) * jnp.mean(w * h, axis=-1, keepdims=True))
    return dh, r


def _place():
    x, y, c = lax.axis_index("x"), lax.axis_index("y"), lax.axis_index("c")
    chips = [(1 - x, y), (x, 1 - y), (1 - x, 1 - y)]
    return x, y, c, chips


def _remote(src, dst, send_sem, recv_sem, to):
    return pltpu.make_async_remote_copy(src_ref=src, dst_ref=dst, send_sem=send_sem, recv_sem=recv_sem,
                                        device_id=to, device_id_type=MESH)


class _Ride:
    def __init__(self, args, out_shape, n_sem, start, finish, mids=(), aliases=None):
        self.args, self.out_shape, self.n_sem = list(args), list(out_shape), n_sem
        self.start, self.mids, self.finish = start, list(mids), finish
        self.aliases = dict(aliases or {})


def _call(body, *, name, grid, in_specs, out_specs, out_shape, scratch_shapes=(), sem=None, rides=()):
    single = not isinstance(out_shape, (list, tuple))
    out_specs = [out_specs] if single else list(out_specs)
    out_shape = [out_shape] if single else list(out_shape)
    n_in, n_out, n_scr = len(in_specs), len(out_shape), len(scratch_shapes)
    r_in = [len(r.args) for r in rides]
    r_out = [len(r.out_shape) for r in rides]
    any_spec = pl.BlockSpec(memory_space=pl.ANY)
    aliases, off_i, off_o = {}, n_in, n_out
    for r in rides:
        for i, o in r.aliases.items():
            aliases[off_i + i] = off_o + o
        off_i += len(r.args)
        off_o += len(r.out_shape)
    steps = math.prod(grid)

    def wrapped(*refs):
        p = 0
        ins = refs[p:p + n_in]; p += n_in
        rins = refs[p:p + sum(r_in)]; p += sum(r_in)
        outs = refs[p:p + n_out]; p += n_out
        routs = refs[p:p + sum(r_out)]; p += sum(r_out)
        scr = refs[p:p + n_scr]; p += n_scr
        sems = refs[p:]
        parts, pi, po = [], 0, 0
        for k, r in enumerate(rides):
            parts.append((rins[pi:pi + r_in[k]], routs[po:po + r_out[k]], sems[2 * k], sems[2 * k + 1]))
            pi += r_in[k]
            po += r_out[k]
        lin = 0
        for d in range(len(grid)):
            lin = lin * grid[d] + pl.program_id(d)
        if rides:
            @pl.when(lin == 0)
            def _():
                for r, part in zip(rides, parts):
                    r.start(*part)
        body(*ins, *outs, *scr)
        for r, part in zip(rides, parts):
            for frac, fn in r.mids:
                @pl.when(lin == min(steps - 1, int(frac * steps)))
                def _(fn=fn, part=part):
                    fn(*part)
        if rides:
            @pl.when(lin == steps - 1)
            def _():
                for r, part in zip(rides, parts):
                    r.finish(*part)

    scratch = list(scratch_shapes)
    for r in rides:
        scratch += [pltpu.SemaphoreType.DMA((r.n_sem,)), pltpu.SemaphoreType.DMA((r.n_sem,))]
    if rides:
        sem = ("arbitrary",) * len(grid)
    res = pl.pallas_call(
        wrapped, name=name, grid=grid,
        in_specs=list(in_specs) + [any_spec] * sum(r_in),
        out_specs=out_specs + [any_spec] * sum(r_out),
        out_shape=out_shape + [s for r in rides for s in r.out_shape],
        scratch_shapes=scratch, input_output_aliases=aliases,
        compiler_params=_params(sem),
    )

    def run(*args):
        got = res(*args, *[a for r in rides for a in r.args])
        mine = got[0] if single else list(got[:n_out])
        if not rides:
            return mine
        rest, out = list(got[n_out:]), []
        for k in range(len(rides)):
            out.append(rest[:r_out[k]])
            rest = rest[r_out[k]:]
        return mine, out

    return run


def _ride_gather(slot, s1=None, s2=None, s3=None, tail=None, chain=None, mid_frac=0.6, chain_fracs=(0.35, 0.7)):
    half = slot.shape[1] // 2

    def rows(part, c, which=None):
        k0, k1, n = part
        count, first = (k1 - k0) * (half // n), c * half + k0 * (half // n)
        return pl.ds(first, count) if which is None else pl.ds(first + which * (count // 2), count // 2)

    def ids():
        x, y, c, _ = _place()
        return x, y, c, 2 * x + y, 2 * (1 - x) + y, 2 * x + (1 - y), 2 * (1 - x) + (1 - y)

    def copy(full, chip, r, ss, rs, k, to):
        piece = full.at[chip, r, :]
        return _remote(piece, piece, ss.at[k], rs.at[k], to)

    def to_neighbours(full, ss, rs, part, base):
        x, y, c, me, _, _, _ = ids()
        return [copy(full, me, rows(part, c), ss, rs, base, (1 - x, y, c)),
                copy(full, me, rows(part, c), ss, rs, base + 1, (x, 1 - y, c))]

    def from_neighbours(full, ss, rs, part, base):
        x, y, c, _, cx, cy, _ = ids()
        return [copy(full, cx, rows(part, c), ss, rs, base, (x, y, c)), copy(full, cy, rows(part, c), ss, rs, base + 1, (x, y, c))]

    def onward(full, ss, rs, part, base):
        x, y, c, _, cx, cy, _ = ids()
        return [copy(full, cx, rows(part, c, 0), ss, rs, base, (x, 1 - y, c)),
                copy(full, cy, rows(part, c, 1), ss, rs, base + 1, (1 - x, y, c))]

    def from_onward(full, ss, rs, part, base):
        x, y, c, _, _, _, cd = ids()
        return [copy(full, cd, rows(part, c, 0), ss, rs, base, (x, y, c)), copy(full, cd, rows(part, c, 1), ss, rs, base + 1, (x, y, c))]

    def to_sibling(full, ss, rs, part, base, diagonal):
        x, y, c, _, cx, cy, cd = ids()
        return [copy(full, chip, rows(part, c), ss, rs, base + j, (x, y, 1 - c))
                for j, chip in enumerate([cd] if diagonal else [cx, cy])]

    def from_sibling(full, ss, rs, part, base, diagonal):
        x, y, c, _, cx, cy, cd = ids()
        return [copy(full, chip, rows(part, 1 - c), ss, rs, base + j, (x, y, c))
                for j, chip in enumerate([cd] if diagonal else [cx, cy])]

    def start(ins, outs, ss, rs):
        full, cps = outs[0], []
        for part, base in ((s1, 0), (chain, 12)):
            if part is not None:
                cps += to_neighbours(full, ss, rs, part, base)
        for part, b_ici, b_sib in ((s2, 2, 4), (tail, 7, 9)):
            if part is not None:
                cps += onward(full, ss, rs, part, b_ici) + to_sibling(full, ss, rs, part, b_sib, False)
        if s3 is not None:
            cps += to_sibling(full, ss, rs, s3, 6, True)
        for cp in cps:
            cp.start()

    def second(part, b_in, b_ici, b_sib):
        def fn(ins, outs, ss, rs):
            for cp in from_neighbours(outs[0], ss, rs, part, b_in):
                cp.wait_recv()
            for cp in onward(outs[0], ss, rs, part, b_ici) + to_sibling(outs[0], ss, rs, part, b_sib, False):
                cp.start()
        return fn

    def third(part, b_ici, b_sib):
        def fn(ins, outs, ss, rs):
            for cp in from_onward(outs[0], ss, rs, part, b_ici):
                cp.wait_recv()
            for cp in to_sibling(outs[0], ss, rs, part, b_sib, True):
                cp.start()
        return fn

    mids = []
    if tail is not None:
        mids.append((mid_frac, third(tail, 7, 11)))
    if chain is not None:
        mids += [(chain_fracs[0], second(chain, 12, 14, 16)), (chain_fracs[1], third(chain, 14, 18))]

    def finish(ins, outs, ss, rs):
        full, got, sent = outs[0], [], []
        if s1 is not None:
            got += from_neighbours(full, ss, rs, s1, 0)
            sent += to_neighbours(full, ss, rs, s1, 0)
        if s2 is not None:
            got += from_onward(full, ss, rs, s2, 2) + from_sibling(full, ss, rs, s2, 4, False)
            sent += onward(full, ss, rs, s2, 2) + to_sibling(full, ss, rs, s2, 4, False)
        if s3 is not None:
            got += from_sibling(full, ss, rs, s3, 6, True)
            sent += to_sibling(full, ss, rs, s3, 6, True)
        if tail is not None:
            got += from_sibling(full, ss, rs, tail, 9, False) + from_sibling(full, ss, rs, tail, 11, True)
            sent += onward(full, ss, rs, tail, 7) + to_sibling(full, ss, rs, tail, 9, False) + to_sibling(full, ss, rs, tail, 11, True)
        if chain is not None:
            got += from_sibling(full, ss, rs, chain, 16, False) + from_sibling(full, ss, rs, chain, 18, True)
            sent += (to_neighbours(full, ss, rs, chain, 12) + onward(full, ss, rs, chain, 14)
                     + to_sibling(full, ss, rs, chain, 16, False) + to_sibling(full, ss, rs, chain, 18, True))
        for cp in got:
            cp.wait_recv()
        for cp in sent:
            cp.wait_send()

    return _Ride([slot], [jax.ShapeDtypeStruct(slot.shape, slot.dtype)], 19, start, finish, mids=mids, aliases={0: 0})


def _ride_scatter(q, land=None, part=(0, 1)):
    k0, k1, n = part if len(part) == 3 else (part[0], part[0] + 1, part[1])
    rows_n = q.shape[1] // n
    rows = pl.ds(k0 * rows_n, (k1 - k0) * rows_n)

    def copies(ins, outs, ss, rs):
        x, y, c, chips = _place()
        return [_remote(ins[0].at[2 * chip[0] + chip[1], rows, :], outs[0].at[j, rows, :], ss.at[j], rs.at[j], (*chip, c))
                for j, chip in enumerate(chips)]

    def start(*a):
        for cp in copies(*a):
            cp.start()

    def finish(*a):
        for cp in copies(*a):
            cp.wait()

    shape = jax.ShapeDtypeStruct((3,) + q.shape[1:], q.dtype)
    if land is None:
        return _Ride([q], [shape], 3, start, finish)
    return _Ride([q, land], [shape], 3, start, finish, aliases={1: 0})


def _ride_to_sibling(a, halves=False, first=False, shards=None, land=None):
    s0, s1 = shards or (0, a.shape[0])

    def copy(ins, outs, ss, rs):
        x, y, c, _ = _place()
        if halves:
            src, dst = ins[0].at[s0:s1, 1 - c], outs[0].at[s0:s1]
        else:
            src, dst = (ins[0].at[0] if first else ins[0]), outs[0]
        return _remote(src, dst, ss.at[0], rs.at[0], (x, y, 1 - c))

    shape = (a.shape[0],) + a.shape[2:] if halves else (a.shape[1:] if first else a.shape)
    return _Ride([a] if land is None else [a, land], [jax.ShapeDtypeStruct(shape, a.dtype)], 1,
                 lambda *a_: copy(*a_).start(), lambda *a_: copy(*a_).wait(), aliases=None if land is None else {1: 0})


def _ride_rows_to_sibling(a, hr, shards, total):
    def copies(ins, outs, ss, rs):
        x, y, c, _ = _place()
        return [_remote(ins[0].at[pl.ds((2 * s + 1 - c) * hr, hr), :], outs[0].at[s], ss.at[s], rs.at[s], (x, y, 1 - c))
                for s in range(shards)]

    def start(*a_):
        for cp in copies(*a_):
            cp.start()

    def finish(*a_):
        for cp in copies(*a_):
            cp.wait()

    return _Ride([a], [jax.ShapeDtypeStruct((total, hr, a.shape[1]), a.dtype)], shards, start, finish)


def _ride_swap(h):
    def copy(ins, outs, ss, rs):
        x, y, c, _ = _place()
        return _remote(ins[0], outs[0], ss.at[0], rs.at[0], (x, y, 1 - c))

    return _Ride([h], [jax.ShapeDtypeStruct(h.shape, h.dtype)], 1,
                 lambda *a: copy(*a).start(), lambda *a: copy(*a).wait())


def _mesh_place(p):
    return (p // 4, (p // 2) % 2, p % 2)


def _ride_small_to_all(packed):
    def copies(ins, outs, ss, rs):
        x, y, c, _ = _place()
        me = 4 * x + 2 * y + c
        return [_remote(ins[0], outs[0].at[me], ss.at[k - 1], rs.at[k - 1], _mesh_place((me + k) % N_DEV))
                for k in range(1, N_DEV)]

    def own(ins, outs, ss, rs):
        x, y, c, _ = _place()
        return pltpu.make_async_copy(ins[0], outs[0].at[4 * x + 2 * y + c], ss.at[N_DEV - 1])

    def start(*a):
        own(*a).start()
        for cp in copies(*a):
            cp.start()

    def finish(ins, outs, ss, rs):
        x, y, c, _ = _place()
        me = 4 * x + 2 * y + c
        for k in range(1, N_DEV):
            _remote(ins[0], outs[0].at[(me + N_DEV - k) % N_DEV], ss.at[k - 1], rs.at[k - 1], (x, y, c)).wait_recv()
        for cp in copies(ins, outs, ss, rs):
            cp.wait_send()
        own(ins, outs, ss, rs).wait()

    return _Ride([packed], [jax.ShapeDtypeStruct((N_DEV,) + packed.shape, packed.dtype)], N_DEV, start, finish)


def _carrier(rides, *, name):
    _, outs = _call(lambda: None, name=name, grid=(1,), in_specs=[], out_specs=[], out_shape=[], rides=rides)()
    return outs


def _norm_bf16(a_ref, g_ref):
    xf = a_ref[...]
    r = lax.rsqrt(jnp.mean(xf * xf, axis=-1, keepdims=True) + EPS)
    return ((xf * r) * g_ref[...]).astype(BF16)


def _norm_matmul_wide(a, g, b, *, tm, tn, name, rides=()):
    T, K = a.shape
    N = b.shape[0]

    def body(a_ref, g_ref, b_ref, n_ref, o_ref):
        n = _norm_bf16(a_ref, g_ref)
        n_ref[...] = n
        o_ref[...] = _dot_nt(n, b_ref[...])

    return _call(
        body, name=name, grid=(N // tn, T // tm),
        in_specs=[pl.BlockSpec((tm, K), lambda j, i: (i, 0)), pl.BlockSpec((1, K), lambda j, i: (0, 0)),
                  pl.BlockSpec((tn, K), lambda j, i: (j, 0))],
        out_specs=[pl.BlockSpec((None, tm, K), lambda j, i: (j, i, 0)), pl.BlockSpec((tm, tn), lambda j, i: (i, j))],
        out_shape=[jax.ShapeDtypeStruct((N // tn, T, K), BF16), jax.ShapeDtypeStruct((T, N), F32)],
        sem=("arbitrary", "arbitrary"), rides=rides,
    )(a, g, b)


def _norm_matmul_sq(a, g, b, *, tm, tn, name, rides=()):
    T, K = a.shape
    per = b.shape[2] // tn
    N = b.shape[0] * b.shape[2]

    def body(a_ref, g_ref, b_ref, nt_ref, o_ref, z_ref, zt_ref, n_scr):
        @pl.when(pl.program_id(1) == 0)
        def _():
            n = _norm_bf16(a_ref, g_ref)
            n_scr[...] = n
            nt_ref[...] = n.T
        r = jnp.maximum(_dot(n_scr[...], b_ref[...]), 0.0)
        o_ref[...] = r.astype(BF16)
        z = (r * r).astype(BF16)
        z_ref[...] = z
        zt_ref[...] = z.T

    return _call(
        body, name=name, grid=(T // tm, N // tn),
        in_specs=[pl.BlockSpec((tm, K), lambda i, j: (i, 0)), pl.BlockSpec((1, K), lambda i, j: (0, 0)),
                  pl.BlockSpec((None, K, tn), lambda i, j: (j // per, 0, j % per))],
        out_specs=[pl.BlockSpec((K, tm), lambda i, j: (0, i)), pl.BlockSpec((tm, tn), lambda i, j: (i, j)),
                   pl.BlockSpec((tm, tn), lambda i, j: (i, j)), pl.BlockSpec((tn, tm), lambda i, j: (j, i))],
        out_shape=[jax.ShapeDtypeStruct((K, T), BF16), jax.ShapeDtypeStruct((T, N), BF16),
                   jax.ShapeDtypeStruct((T, N), BF16), jax.ShapeDtypeStruct((N, T), BF16)],
        scratch_shapes=[pltpu.VMEM((tm, K), BF16)],
        sem=("parallel", "arbitrary"), rides=rides,
    )(a, g, b)


def _grad_pair(at, at_sib, b, b_sib, *, cols_sharded, tmo, tk, name, rides=()):
    S, _, hr, T = at.shape
    C = b.shape[-1] // N_CHIPS if cols_sharded else b.shape[-1]
    nk = T // tk
    a_sel = (lambda s: 0) if cols_sharded else (lambda s: s)
    b_sel = (lambda s: s) if cols_sharded else (lambda s: 0)
    if b.ndim == 3:
        b_spec = pl.BlockSpec((None, tk, C), lambda s, i, k: (0, k, b_sel(s)))
    else:
        b_spec = pl.BlockSpec((tk, C), lambda s, i, k: (k, b_sel(s)))

    def body(a_ref, as_ref, b_ref, bs_ref, o_ref, ob_ref):
        k = pl.program_id(2)
        p = _dot(a_ref[...], b_ref[...]) + _dot(as_ref[...], bs_ref[...])

        @pl.when(k == 0)
        def _():
            o_ref[...] = p

        @pl.when(k > 0)
        def _():
            o_ref[...] += p

        @pl.when(k == nk - 1)
        def _():
            ob_ref[...] = o_ref[...].astype(BF16)

    out = pl.BlockSpec((None, tmo, C), lambda s, i, k: (s, i, 0))
    return _call(
        body, name=name, grid=(N_CHIPS, hr // tmo, nk),
        in_specs=[pl.BlockSpec((None, None, tmo, tk), lambda s, i, k: (a_sel(s), lax.axis_index("c"), i, k)),
                  pl.BlockSpec((None, tmo, tk), lambda s, i, k: (a_sel(s), i, k)),
                  b_spec, pl.BlockSpec((tk, C), lambda s, i, k: (k, b_sel(s)))],
        out_specs=[out, out],
        out_shape=[jax.ShapeDtypeStruct((N_CHIPS, hr, C), F32), jax.ShapeDtypeStruct((N_CHIPS, hr, C), BF16)],
        sem=("parallel", "parallel", "arbitrary"), rides=rides,
    )(at, at_sib, b, b_sib)


def _grad_pair_merged(at, at_sib, b, b_sib, *, tk, name, rides=()):
    S, _, hr, T = at.shape
    C = b.shape[-1]
    nk = T // tk

    def body(a_ref, as_ref, b_ref, bs_ref, o_ref, ob_ref):
        k = pl.program_id(0)
        p = (_dot(a_ref[...].reshape(S * hr, tk), b_ref[...])
             + _dot(as_ref[...].reshape(S * hr, tk), bs_ref[...])).reshape(S, hr, C)

        @pl.when(k == 0)
        def _():
            o_ref[...] = p

        @pl.when(k > 0)
        def _():
            o_ref[...] += p

        @pl.when(k == nk - 1)
        def _():
            ob_ref[...] = o_ref[...].astype(BF16)

    out = pl.BlockSpec((S, hr, C), lambda k: (0, 0, 0))
    return _call(
        body, name=name, grid=(nk,),
        in_specs=[pl.BlockSpec((S, None, hr, tk), lambda k: (0, lax.axis_index("c"), 0, k)),
                  pl.BlockSpec((S, hr, tk), lambda k: (0, 0, k)),
                  pl.BlockSpec((tk, C), lambda k: (k, 0)), pl.BlockSpec((tk, C), lambda k: (k, 0))],
        out_specs=[out, out],
        out_shape=[jax.ShapeDtypeStruct((S, hr, C), F32), jax.ShapeDtypeStruct((S, hr, C), BF16)],
        sem=("arbitrary",), rides=rides,
    )(at, at_sib, b, b_sib)


def _matmul_parts(parts, b, *, tm, tn, name, rides=()):
    T = parts[0].shape[0]
    N = b.shape[1]
    offs = [sum(p.shape[1] for p in parts[:i]) for i in range(len(parts))]
    assert all(o % p.shape[1] == 0 for o, p in zip(offs, parts))

    def body(*refs):
        n = len(parts)
        acc = _dot(refs[0][...], refs[n][...])
        for i in range(1, n):
            acc = acc + _dot(refs[i][...], refs[n + i][...])
        refs[-1][...] = acc

    a_specs = [pl.BlockSpec((tm, p.shape[1]), lambda i, j: (i, 0)) for p in parts]
    b_specs = [pl.BlockSpec((p.shape[1], tn), lambda i, j, r=o // p.shape[1]: (r, j)) for o, p in zip(offs, parts)]
    return _call(
        body, name=name, grid=(T // tm, N // tn), in_specs=a_specs + b_specs,
        out_specs=pl.BlockSpec((tm, tn), lambda i, j: (i, j)), out_shape=jax.ShapeDtypeStruct((T, N), F32),
        sem=("parallel", "parallel"), rides=rides,
    )(*parts, *([b] * len(parts)))


def _to_bf16(v):
    return v.astype(BF16)


def _matmul_res(a, b, res, *, tm, tn, tk, prologue, name, rides=()):
    T, K = a.shape
    N = b.shape[1]

    def body(a_ref, b_ref, res_ref, o_ref):
        k = pl.program_id(2)
        p = _dot(prologue(a_ref[...]), b_ref[...])

        @pl.when(k == 0)
        def _():
            o_ref[...] = res_ref[...] + p

        @pl.when(k > 0)
        def _():
            o_ref[...] += p

    return _call(
        body, name=name, grid=(T // tm, N // tn, K // tk),
        in_specs=[pl.BlockSpec((tm, tk), lambda i, j, k: (i, k)), pl.BlockSpec((tk, tn), lambda i, j, k: (k, j)),
                  pl.BlockSpec((tm, tn), lambda i, j, k: (i, j))],
        out_specs=pl.BlockSpec((tm, tn), lambda i, j, k: (i, j)),
        out_shape=jax.ShapeDtypeStruct((T, N), F32),
        sem=("parallel", "parallel", "arbitrary"), rides=rides,
    )(a, b, res)


def _matmul_nt(a, b, *, tm, tn, tk, name, extra=None, epilogue=None, out_dtype=F32, rides=()):
    T, K = a.shape
    two = b.ndim == 3 and tk == 2 * b.shape[2]
    if two:
        N, ks = b.shape[1], b.shape[2]
        b_specs = [pl.BlockSpec((None, tn, ks), lambda i, j, k: (2 * k, j, 0)),
                   pl.BlockSpec((None, tn, ks), lambda i, j, k: (2 * k + 1, j, 0))]
    elif b.ndim == 3:
        per = b.shape[2] // tk
        N = b.shape[1]
        b_specs = [pl.BlockSpec((None, tn, tk), lambda i, j, k: (k // per, j, k % per))]
    else:
        N = b.shape[0]
        b_specs = [pl.BlockSpec((tn, tk), lambda i, j, k: (j, k))]
    nb = len(b_specs)
    nk = K // tk
    assert out_dtype == F32 or nk == 1
    in_specs = [pl.BlockSpec((tm, tk), lambda i, j, k: (i, k))] + b_specs
    args = [a] + [b] * nb
    if extra is not None:
        in_specs.append(pl.BlockSpec((tm, tn), lambda i, j, k: (i, j)))
        args.append(extra)

    def body(*refs):
        a_ref, b_ref = refs[0], refs[1]
        o_ref = refs[-1]
        if two:
            p = (_dot_nt(a_ref[:, :tk // 2].astype(BF16), refs[1][...])
                 + _dot_nt(a_ref[:, tk // 2:].astype(BF16), refs[2][...]))
        else:
            p = _dot_nt(a_ref[...].astype(BF16), b_ref[...])
        if nk == 1:
            if epilogue is not None:
                p = epilogue(p, refs[1 + nb][...])
            o_ref[...] = p.astype(out_dtype)
        else:
            k = pl.program_id(2)

            @pl.when(k == 0)
            def _():
                o_ref[...] = p

            @pl.when(k > 0)
            def _():
                o_ref[...] += p

    return _call(
        body, name=name, grid=(T // tm, N // tn, nk),
        in_specs=in_specs,
        out_specs=pl.BlockSpec((tm, tn), lambda i, j, k: (i, j)),
        out_shape=jax.ShapeDtypeStruct((T, N), out_dtype),
        sem=("parallel", "parallel", "arbitrary"), rides=rides,
    )(*args)


def _loss_bwd(h2, tgt, g, *, tm):
    T, D = h2.shape

    def body(h_ref, t_ref, g_ref, dh_ref, dhb_ref, dg_ref, loss_ref):
        @pl.when(pl.program_id(0) == 0)
        def _():
            dg_ref[...] = jnp.zeros_like(dg_ref)
            loss_ref[...] = jnp.zeros_like(loss_ref)
        h = h_ref[...]
        gg = g_ref[...]
        r = lax.rsqrt(jnp.mean(h * h, axis=-1, keepdims=True) + EPS)
        hn = h * r
        err = hn * gg - t_ref[...]
        loss_ref[...] += 0.5 * jnp.sum(jnp.mean(err * err, axis=-1, keepdims=True), axis=0, keepdims=True)
        dy = err * (1.0 / D)
        dg_ref[...] += jnp.sum(dy * hn, axis=0, keepdims=True)
        w = dy * gg
        dh = r * w - h * ((r * r * r) * jnp.mean(w * h, axis=-1, keepdims=True))
        dh_ref[...] = dh
        dhb_ref[...] = dh.astype(BF16)

    tile = pl.BlockSpec((tm, D), lambda i: (i, 0))
    return pl.pallas_call(
        body, name="loss_bwd", grid=(T // tm,),
        in_specs=[tile, tile, pl.BlockSpec((1, D), lambda i: (0, 0))],
        out_specs=[tile, tile, pl.BlockSpec((1, D), lambda i: (0, 0)), pl.BlockSpec((1, 1), lambda i: (0, 0))],
        out_shape=[jax.ShapeDtypeStruct((T, D), F32), jax.ShapeDtypeStruct((T, D), BF16),
                   jax.ShapeDtypeStruct((1, D), F32), jax.ShapeDtypeStruct((1, 1), F32)],
        compiler_params=_params(("arbitrary",)),
    )(h2, tgt, g)


def _rms_bwd_res(dn, h, g, dres, *, tm, name, bf16_copy=True, rides=()):
    T, D = h.shape

    def body(dn_ref, h_ref, g_ref, dres_ref, dh_ref, *rest):
        dg_ref = rest[-1]

        @pl.when(pl.program_id(0) == 0)
        def _():
            dg_ref[...] = jnp.zeros_like(dg_ref)
        h_ = h_ref[...]
        dn_ = dn_ref[...]
        dh, r = _rms_bwd(dn_, h_, g_ref[...])
        dg_ref[...] += jnp.sum(dn_ * (h_ * r), axis=0, keepdims=True)
        dh = dres_ref[...] + dh
        dh_ref[...] = dh
        if bf16_copy:
            rest[0][...] = dh.astype(BF16)

    tile = pl.BlockSpec((tm, D), lambda i: (i, 0))
    row = pl.BlockSpec((1, D), lambda i: (0, 0))
    copy_spec = [tile] if bf16_copy else []
    copy_shape = [jax.ShapeDtypeStruct((T, D), BF16)] if bf16_copy else []
    return _call(
        body, name=name, grid=(T // tm,),
        in_specs=[tile, tile, row, tile], out_specs=[tile] + copy_spec + [row],
        out_shape=[jax.ShapeDtypeStruct((T, D), F32)] + copy_shape + [jax.ShapeDtypeStruct((1, D), F32)],
        sem=("arbitrary",), rides=rides,
    )(dn, h, g, dres)


def _rel_distance():
    i = lax.broadcasted_iota(jnp.int32, (CHUNK, 2 * CHUNK), 0)
    j = lax.broadcasted_iota(jnp.int32, (CHUNK, 2 * CHUNK), 1)
    return i + CHUNK - j


def _bias_build(table):
    def body(tab_ref, o_ref):
        rel = _rel_distance()
        j = lax.broadcasted_iota(jnp.int32, (CHUNK, 2 * CHUNK), 1)
        band = (rel >= 0) & (rel < CHUNK)
        ge = [rel >= t for t in BUCKET_THR]
        for h in range(B_HEADS):
            cur = jnp.full((CHUNK, 2 * CHUNK), tab_ref[0, h], F32)
            for b in range(1, N_BUCKETS):
                cur = jnp.where(ge[b - 1], tab_ref[b, h], cur)
            o_ref[0, h] = jnp.where(band & (j >= CHUNK), cur, NEG)
            o_ref[1, h] = jnp.where(band, cur, NEG)

    return pl.pallas_call(
        body, name="bias_build",
        in_specs=[pl.BlockSpec(memory_space=pltpu.SMEM)],
        out_specs=pl.BlockSpec(memory_space=pltpu.VMEM),
        out_shape=jax.ShapeDtypeStruct((2, B_HEADS, CHUNK, 2 * CHUNK), F32),
    )(table)


def _bias_grad(dbias):
    def body(db_ref, o_ref, acc_ref):
        rel = _rel_distance()
        lo = [0] + BUCKET_THR
        hi = BUCKET_THR + [CHUNK]
        for b in range(N_BUCKETS):
            m = (rel >= lo[b]) & (rel < hi[b])
            for h in range(B_HEADS):
                row = b * B_HEADS + h
                acc_ref[row:row + 1, :] = jnp.sum(jnp.where(m, db_ref[h], 0.0), axis=0, keepdims=True)
        o_ref[...] = jnp.sum(acc_ref[...], axis=1, keepdims=True)

    return pl.pallas_call(
        body, name="bias_grad",
        in_specs=[pl.BlockSpec(memory_space=pltpu.VMEM)],
        out_specs=pl.BlockSpec(memory_space=pltpu.VMEM),
        out_shape=jax.ShapeDtypeStruct((N_BUCKETS * B_HEADS, 1), F32),
        scratch_shapes=[pltpu.VMEM((N_BUCKETS * B_HEADS, 2 * CHUNK), F32)],
    )(dbias)


def _causal_mask():
    t = lax.broadcasted_iota(jnp.int32, (CHUNK, CHUNK), 0)
    s = lax.broadcasted_iota(jnp.int32, (CHUNK, CHUNK), 1)
    return s <= t


def _gate_forward(u, v, lg, lb, wc, bs):
    ug = _gelu(u)
    vg = _gelu(v)
    mu = jnp.mean(vg, axis=-1, keepdims=True)
    xc = vg - mu
    rstd = lax.rsqrt(jnp.mean(xc * xc, axis=-1, keepdims=True) + EPS)
    xhat = xc * rstd
    vl = (xhat * lg + lb).astype(BF16)
    mixed = _dot(wc, vl) + bs
    return ug, xhat, rstd, vl, mixed


def _softmax_scores(qk, bias, sink):
    s = qk + bias
    m = jnp.maximum(jnp.max(s, axis=-1, keepdims=True), sink)
    p = jnp.exp(s - m)
    e_sink = jnp.exp(sink - m)
    inv = 1.0 / (jnp.sum(p, axis=-1, keepdims=True) + e_sink)
    return p * inv, e_sink * inv


PAIRS = Q_PER_KV // 2


def _head(g, pr, e):
    return g * Q_PER_KV + 2 * pr + e


def _stack_pairs(ref, g, col0=0):
    w = 2 * HEAD_DIM
    return jnp.concatenate([ref[:, col0 + (g * PAIRS + pr) * w:col0 + (g * PAIRS + pr + 1) * w] for pr in range(PAIRS)],
                           axis=0)


def _low_lanes():
    return lax.broadcasted_iota(jnp.int32, (2 * CHUNK, 2 * HEAD_DIM), 1) < HEAD_DIM


def _band_operands(kv_prev, kv_cur):
    band = jnp.concatenate([kv_prev, kv_cur], axis=0)
    low = _low_lanes()
    ops = []
    for cat in (band[:, :KV_WIDTH], band[:, KV_WIDTH:]):
        rol = pltpu.roll(cat, HEAD_DIM, 1)
        ops.append([[jnp.where(low if e == 0 else ~low, cat if g == e else rol, 0.0).astype(BF16) for e in range(2)]
                    for g in range(2)])
    return ops


def _mixer_fwd(proj, lg, lb, wsp, bs_col, sinks, bias, ga, gb, rides=()):
    T = proj.shape[0]
    nb = T // CHUNK

    def body(u_ref, v_ref, q_ref, kvc_ref, kvp_ref, lg_ref, lb_ref, w_ref, bs_ref, sink_ref, bias_ref,
             ga_ref, gb_ref, mixed_ref, mixed_t_ref, ab_ref):
        causal = _causal_mask()
        ssq = jnp.zeros((CHUNK, 1), F32)
        for g in range(A_GROUPS):
            cols = slice(g * CHUNK, (g + 1) * CHUNK)
            wc = jnp.where(causal, w_ref[g], 0.0).astype(BF16)
            ug, _, _, _, mixed = _gate_forward(u_ref[:, cols], v_ref[:, cols], lg_ref[g:g + 1, :], lb_ref[g:g + 1, :],
                                               wc, bs_ref[g])
            a = ug * mixed
            ab_ref[:, cols] = a
            ssq = ssq + jnp.sum(a * a, axis=-1, keepdims=True)
        ra = lax.rsqrt(ssq * (1.0 / A_WIDTH) + EPS)
        mixed_ref[:, :A_WIDTH] = ((ab_ref[:, :A_WIDTH] * ra) * ga_ref[...]).astype(BF16)

        kops, vops = _band_operands(kvp_ref[...], kvc_ref[...])
        ssq = jnp.zeros((CHUNK, 1), F32)
        for g in range(B_HEADS // Q_PER_KV):
            qst = (_stack_pairs(q_ref, g) * SCALE).astype(BF16)
            o_st = jnp.zeros((PAIRS * CHUNK, 2 * HEAD_DIM), F32)
            for e in range(2):
                s_all = _dot_nt(qst, kops[g][e])
                ps = []
                for pr in range(PAIRS):
                    h = _head(g, pr, e)
                    p, _ = _softmax_scores(s_all[pr * CHUNK:(pr + 1) * CHUNK], bias_ref[h], sink_ref[0, h])
                    ps.append(p.astype(BF16))
                o_st = o_st + _dot(jnp.concatenate(ps, axis=0), vops[g][e])
            for pr in range(PAIRS):
                o = o_st[pr * CHUNK:(pr + 1) * CHUNK]
                c0 = A_WIDTH + (g * PAIRS + pr) * 2 * HEAD_DIM
                ab_ref[:, c0:c0 + 2 * HEAD_DIM] = o
                ssq = ssq + jnp.sum(o * o, axis=-1, keepdims=True)
        rb = lax.rsqrt(ssq * (1.0 / B_WIDTH) + EPS)
        mixed_ref[:, A_WIDTH:] = ((ab_ref[:, A_WIDTH:] * rb) * gb_ref[...]).astype(BF16)
        mixed_t_ref[...] = mixed_ref[...].T

    full = lambda *shape: pl.BlockSpec(shape, lambda n: (0,) * len(shape))
    return _call(
        body, name="mixer_fwd", grid=(nb,),
        in_specs=[pl.BlockSpec((CHUNK, A_WIDTH), lambda n: (n, 0)),
                  pl.BlockSpec((CHUNK, A_WIDTH), lambda n: (n, 1)),
                  pl.BlockSpec((CHUNK, B_WIDTH), lambda n: (n, 2)),
                  pl.BlockSpec((CHUNK, 2 * KV_WIDTH), lambda n: (n, 12)),
                  pl.BlockSpec((CHUNK, 2 * KV_WIDTH), lambda n: (jnp.maximum(n - 1, 0), 12)),
                  full(A_GROUPS, CHUNK), full(A_GROUPS, CHUNK), full(A_GROUPS, CHUNK, CHUNK), full(A_GROUPS, CHUNK, 1),
                  pl.BlockSpec(memory_space=pltpu.SMEM),
                  pl.BlockSpec((None, B_HEADS, CHUNK, 2 * CHUNK), lambda n: (jnp.minimum(n, 1), 0, 0, 0)),
                  full(1, A_WIDTH), full(1, B_WIDTH)],
        out_specs=[pl.BlockSpec((CHUNK, D_MODEL), lambda n: (n, 0)), pl.BlockSpec((D_MODEL, CHUNK), lambda n: (0, n)),
                   pl.BlockSpec((CHUNK, D_MODEL), lambda n: (n, 0))],
        out_shape=[jax.ShapeDtypeStruct((T, D_MODEL), BF16), jax.ShapeDtypeStruct((D_MODEL, T), BF16),
                   jax.ShapeDtypeStruct((T, D_MODEL), F32)],
        sem=("parallel",), rides=rides,
    )(proj, proj, proj, proj, proj, lg, lb, wsp, bs_col, sinks, bias, ga, gb)


def _gmlp_bwd(proj, ab, dmixed, ga, lg, lb, wsp, bs_col, rides=()):
    T = proj.shape[0]
    nb = T // CHUNK

    def body(u_ref, v_ref, a_ref, dna_ref, ga_ref, lg_ref, lb_ref, w_ref, bs_ref,
             dp_ref, dpt_ref, dga_ref, dw_ref, dbs_ref, dlg_ref, dlb_ref):
        @pl.when(pl.program_id(0) == 0)
        def _():
            for r in (dga_ref, dw_ref, dbs_ref, dlg_ref, dlb_ref):
                r[...] = jnp.zeros_like(r)
        causal = _causal_mask()
        a_all = a_ref[...]
        dna = dna_ref[...]
        da_all, ra = _rms_bwd(dna, a_all, ga_ref[...])
        dga_ref[...] += jnp.sum(dna * (a_all * ra), axis=0, keepdims=True)
        for g in range(A_GROUPS):
            cols = slice(g * CHUNK, (g + 1) * CHUNK)
            wc = jnp.where(causal, w_ref[g], 0.0).astype(BF16)
            lgg = lg_ref[g:g + 1, :]
            u = u_ref[:, cols]
            v = v_ref[:, cols]
            ug, xhat, rstd, vl, mixed = _gate_forward(u, v, lgg, lb_ref[g:g + 1, :], wc, bs_ref[g])
            da = da_all[:, cols]
            dug = da * mixed
            dmg = da * ug
            dmg_b = dmg.astype(BF16)
            dbs_ref[g] += jnp.sum(dmg, axis=-1, keepdims=True)
            dw_ref[g] += jnp.where(causal, _dot_nt(dmg_b, vl), 0.0)
            dvl = _dot_tn(wc, dmg_b)
            dlg_ref[g:g + 1, :] += jnp.sum(dvl * xhat, axis=0, keepdims=True)
            dlb_ref[g:g + 1, :] += jnp.sum(dvl, axis=0, keepdims=True)
            dxh = dvl * lgg
            dvg = rstd * (dxh - jnp.mean(dxh, axis=-1, keepdims=True)
                          - xhat * jnp.mean(dxh * xhat, axis=-1, keepdims=True))
            _, gu = _gelu_and_grad(u)
            _, gv = _gelu_and_grad(v)
            dp_ref[:, cols] = (dug * gu).astype(BF16)
            dp_ref[:, A_WIDTH + g * CHUNK:A_WIDTH + (g + 1) * CHUNK] = (dvg * gv).astype(BF16)
        dpt_ref[...] = dp_ref[...].T

    full = lambda *shape: pl.BlockSpec(shape, lambda n: (0,) * len(shape))
    return _call(
        body, name="gmlp_bwd", grid=(nb,),
        in_specs=[pl.BlockSpec((CHUNK, A_WIDTH), lambda n: (n, 0)),
                  pl.BlockSpec((CHUNK, A_WIDTH), lambda n: (n, 1)),
                  pl.BlockSpec((CHUNK, A_WIDTH), lambda n: (n, 0)),
                  pl.BlockSpec((CHUNK, A_WIDTH), lambda n: (n, 0)),
                  full(1, A_WIDTH), full(A_GROUPS, CHUNK), full(A_GROUPS, CHUNK), full(A_GROUPS, CHUNK, CHUNK),
                  full(A_GROUPS, CHUNK, 1)],
        out_specs=[pl.BlockSpec((CHUNK, 2 * A_WIDTH), lambda n: (n, 0)), pl.BlockSpec((2 * A_WIDTH, CHUNK), lambda n: (0, n)),
                   full(1, A_WIDTH), full(A_GROUPS, CHUNK, CHUNK), full(A_GROUPS, CHUNK, 1),
                   full(A_GROUPS, CHUNK), full(A_GROUPS, CHUNK)],
        out_shape=[jax.ShapeDtypeStruct((T, 2 * A_WIDTH), BF16), jax.ShapeDtypeStruct((2 * A_WIDTH, T), BF16),
                   jax.ShapeDtypeStruct((1, A_WIDTH), F32), jax.ShapeDtypeStruct((A_GROUPS, CHUNK, CHUNK), F32),
                   jax.ShapeDtypeStruct((A_GROUPS, CHUNK, 1), F32), jax.ShapeDtypeStruct((A_GROUPS, CHUNK), F32),
                   jax.ShapeDtypeStruct((A_GROUPS, CHUNK), F32)],
        sem=("arbitrary",), rides=rides,
    )(proj, proj, ab, dmixed, ga, lg, lb, wsp, bs_col)


def _attn_bwd(proj, ab, dmixed, gb, sinks, bias, rides=()):
    T = proj.shape[0]
    nb = T // CHUNK
    qn = lambda n: jnp.minimum(n, nb - 1)

    def body(q_ref, kvc_ref, kvp_ref, o_ref, dnb_ref, gb_ref, sink_ref, bias_ref,
             dq_ref, dkv_ref, dqt_ref, dkvt_ref, dgb_ref, dsink_ref, dbias_ref, carry_ref, sacc_ref):
        n = pl.program_id(0)

        @pl.when(n == 0)
        def _():
            carry_ref[...] = jnp.zeros_like(carry_ref)
            sacc_ref[...] = jnp.zeros_like(sacc_ref)
            dgb_ref[...] = jnp.zeros_like(dgb_ref)
            dbias_ref[...] = jnp.zeros_like(dbias_ref)

        @pl.when(n < nb)
        def _():
            o_all = o_ref[...]
            dnb = dnb_ref[...]
            do_all, rb = _rms_bwd(dnb, o_all, gb_ref[...])
            dgb_ref[...] += jnp.sum(dnb * (o_all * rb), axis=0, keepdims=True)
            kops, vops = _band_operands(kvp_ref[...], kvc_ref[...])
            low = _low_lanes()
            halves = []
            for g in range(B_HEADS // Q_PER_KV):
                qst = (_stack_pairs(q_ref, g) * SCALE).astype(BF16)
                dost = _stack_pairs(do_all, g).astype(BF16)
                dq_st = jnp.zeros((PAIRS * CHUNK, 2 * HEAD_DIM), F32)
                dk_e, dv_e = [], []
                for e in range(2):
                    s_all = _dot_nt(qst, kops[g][e])
                    dp_all = _dot_nt(dost, vops[g][e])
                    ps, dsrs = [], []
                    for pr in range(PAIRS):
                        h = _head(g, pr, e)
                        rows = slice(pr * CHUNK, (pr + 1) * CHUNK)
                        p, p_sink = _softmax_scores(s_all[rows], bias_ref[h], sink_ref[0, h])
                        dp = dp_all[rows]
                        delta = jnp.sum(p * dp, axis=-1, keepdims=True)
                        ds = p * (dp - delta)
                        sacc_ref[:, h:h + 1] += -(p_sink * delta)
                        dbias_ref[h] += ds
                        ps.append(p.astype(BF16))
                        dsrs.append(ds.astype(BF16))
                    dsr_all = jnp.concatenate(dsrs, axis=0)
                    dq_st = dq_st + _dot(dsr_all, kops[g][e])
                    dk_e.append(_dot_tn(dsr_all, qst))
                    dv_e.append(_dot_tn(jnp.concatenate(ps, axis=0), dost))
                for pr in range(PAIRS):
                    c0 = (g * PAIRS + pr) * 2 * HEAD_DIM
                    dq_ref[:, c0:c0 + 2 * HEAD_DIM] = (dq_st[pr * CHUNK:(pr + 1) * CHUNK] * SCALE).astype(BF16)
                halves.append((dk_e, dv_e))
            tiles = []
            for t in range(2):
                g0, g1 = halves[0][t], halves[1][t]
                tiles.append(jnp.where(low, g0[0] + pltpu.roll(g0[1], HEAD_DIM, 1), pltpu.roll(g1[0], HEAD_DIM, 1) + g1[1]))
            dband = jnp.concatenate(tiles, axis=1)
            dkv = (carry_ref[...] + dband[:CHUNK]).astype(BF16)
            dkv_ref[...] = dkv
            dkvt_ref[...] = dkv.T
            dqt_ref[...] = dq_ref[...].T
            carry_ref[...] = dband[CHUNK:]

        @pl.when(n == nb)
        def _():
            dkv = carry_ref[...].astype(BF16)
            dkv_ref[...] = dkv
            dkvt_ref[...] = dkv.T
            dsink_ref[...] = jnp.sum(sacc_ref[...], axis=0, keepdims=True)

    full = lambda *shape: pl.BlockSpec(shape, lambda n: (0,) * len(shape))
    return _call(
        body, name="attn_bwd", grid=(nb + 1,),
        in_specs=[pl.BlockSpec((CHUNK, B_WIDTH), lambda n: (qn(n), 2)),
                  pl.BlockSpec((CHUNK, 2 * KV_WIDTH), lambda n: (qn(n), 12)),
                  pl.BlockSpec((CHUNK, 2 * KV_WIDTH), lambda n: (jnp.maximum(qn(n) - 1, 0), 12)),
                  pl.BlockSpec((CHUNK, B_WIDTH), lambda n: (qn(n), 1)),
                  pl.BlockSpec((CHUNK, B_WIDTH), lambda n: (qn(n), 1)),
                  full(1, B_WIDTH), pl.BlockSpec(memory_space=pltpu.SMEM),
                  pl.BlockSpec((None, B_HEADS, CHUNK, 2 * CHUNK), lambda n: (jnp.minimum(n, 1), 0, 0, 0))],
        out_specs=[pl.BlockSpec((CHUNK, B_WIDTH), lambda n: (qn(n), 0)),
                   pl.BlockSpec((CHUNK, 2 * KV_WIDTH), lambda n: (jnp.maximum(n - 1, 0), 0)),
                   pl.BlockSpec((B_WIDTH, CHUNK), lambda n: (0, qn(n))),
                   pl.BlockSpec((2 * KV_WIDTH, CHUNK), lambda n: (0, jnp.maximum(n - 1, 0))),
                   full(1, B_WIDTH), full(1, B_HEADS), full(B_HEADS, CHUNK, 2 * CHUNK)],
        out_shape=[jax.ShapeDtypeStruct((T, B_WIDTH), BF16), jax.ShapeDtypeStruct((T, 2 * KV_WIDTH), BF16),
                   jax.ShapeDtypeStruct((B_WIDTH, T), BF16), jax.ShapeDtypeStruct((2 * KV_WIDTH, T), BF16),
                   jax.ShapeDtypeStruct((1, B_WIDTH), F32), jax.ShapeDtypeStruct((1, B_HEADS), F32),
                   jax.ShapeDtypeStruct((B_HEADS, CHUNK, 2 * CHUNK), F32)],
        scratch_shapes=[pltpu.VMEM((CHUNK, 2 * KV_WIDTH), F32), pltpu.VMEM((CHUNK, B_HEADS), F32)],
        sem=("arbitrary",), rides=rides,
    )(proj, proj, proj, ab, dmixed, gb, sinks, bias)


def _sq_relu_grad(acc, r):
    return acc * (2.0 * r.astype(F32))


def _chip_index():
    return (2 * lax.axis_index("x") + lax.axis_index("y")).astype(jnp.int32).reshape(1)


def _cast_into_slot(w, *, tm, name):
    _, R, C = w.shape

    def body(me_ref, w_ref, o_ref):
        del me_ref
        o_ref[...] = w_ref[...].astype(BF16)

    return pl.pallas_call(
        body, name=name,
        grid_spec=pltpu.PrefetchScalarGridSpec(
            num_scalar_prefetch=1, grid=(R // tm,),
            in_specs=[pl.BlockSpec((None, tm, C), lambda i, me: (0, i, 0))],
            out_specs=pl.BlockSpec((None, tm, C), lambda i, me: (me[0], i, 0))),
        out_shape=jax.ShapeDtypeStruct((N_CHIPS, R, C), BF16), compiler_params=_params(("parallel",)),
    )(_chip_index(), w)


def _cast_into_slots_carrying(ws, *, steps, name, rides):
    n = len(ws)

    def body(*refs):
        for w_ref, o_ref in zip(refs[:n], refs[n:]):
            o_ref[...] = w_ref[...].astype(BF16)

    me = lambda: 2 * lax.axis_index("x") + lax.axis_index("y")
    return _call(
        body, name=name, grid=(steps,),
        in_specs=[pl.BlockSpec((None, w.shape[1] // steps, w.shape[2]), lambda i: (0, i, 0)) for w in ws],
        out_specs=[pl.BlockSpec((None, w.shape[1] // steps, w.shape[2]), lambda i: (me(), i, 0)) for w in ws],
        out_shape=[jax.ShapeDtypeStruct((N_CHIPS,) + w.shape[1:], BF16) for w in ws], sem=("arbitrary",), rides=rides,
    )(*ws)


def _owner_total(gh, others, *, tm, name):
    _, hr, C = gh.shape

    def body(me_ref, g_ref, o_ref_in, out_ref):
        del me_ref
        acc = g_ref[...]
        for j in range(3):
            acc = acc + o_ref_in[j].astype(F32)
        out_ref[...] = acc

    return pl.pallas_call(
        body, name=name,
        grid_spec=pltpu.PrefetchScalarGridSpec(
            num_scalar_prefetch=1, grid=(hr // tm,),
            in_specs=[pl.BlockSpec((None, tm, C), lambda i, me: (me[0], i, 0)),
                      pl.BlockSpec((3, tm, C), lambda i, me: (0, i, 0))],
            out_specs=pl.BlockSpec((tm, C), lambda i, me: (i, 0))),
        out_shape=jax.ShapeDtypeStruct((hr, C), F32),
        compiler_params=_params(("parallel",)),
    )(_chip_index(), gh, others)


def _adamw_math(w, g, m, v):
    m = ADAM_B1 * m + (1.0 - ADAM_B1) * g
    v = ADAM_B2 * v + (1.0 - ADAM_B2) * (g * g)
    m_hat = m / (1.0 - ADAM_B1 ** ADAM_STEP)
    v_hat = v / (1.0 - ADAM_B2 ** ADAM_STEP)
    delta = -ADAM_LR * (m_hat / (jnp.sqrt(v_hat) + ADAM_EPS) + ADAM_WD * w)
    return delta, m, v


def _adamw_halves(w, own, got, m, v, *, tm, name, rides=()):
    _, R, C = w.shape
    nt = (R // 2) // tm

    def body(w_ref, own_ref, got_ref, m_ref, v_ref, g_ref, d_ref, nm_ref, nv_ref):
        g = jnp.where(pl.program_id(0) == lax.axis_index("c"), own_ref[...], got_ref[...])
        g_ref[...] = g
        d_ref[...], nm_ref[...], nv_ref[...] = _adamw_math(w_ref[...], g, m_ref[...], v_ref[...])

    whole = pl.BlockSpec((None, tm, C), lambda h, i: (0, h * nt + i, 0))
    half = pl.BlockSpec((tm, C), lambda h, i: (i, 0))
    return _call(
        body, name=name, grid=(2, nt), in_specs=[whole, half, half, whole, whole], out_specs=[whole] * 4,
        out_shape=[jax.ShapeDtypeStruct((1, R, C), F32)] * 4, sem=("parallel", "parallel"), rides=rides,
    )(w, own, got, m, v)


def _adamw_small(w, slots, m, v, *, name):
    def body(w_ref, slots_ref, m_ref, v_ref, g_ref, d_ref, nm_ref, nv_ref):
        g = slots_ref[0]
        for d in range(1, N_DEV):
            g = g + slots_ref[d]
        g_ref[...] = g
        d_ref[...], nm_ref[...], nv_ref[...] = _adamw_math(w_ref[...], g, m_ref[...], v_ref[...])

    vmem = pl.BlockSpec(memory_space=pltpu.VMEM)
    return pl.pallas_call(
        body, name=name, in_specs=[vmem] * 4, out_specs=[vmem] * 4,
        out_shape=[jax.ShapeDtypeStruct(w.shape, F32)] * 4, compiler_params=_params(),
    )(w, slots, m, v)


SMALL = ["rel_bias_table", "mix_norm_g", "gate_norm_g", "gate_norm_b", "w_spatial", "b_spatial", "attn_sinks",
         "out_norm_a_g", "out_norm_b_g", "ffn_norm_g", "final_norm_g"]
SMALL_A = ["gate_norm_g", "gate_norm_b", "w_spatial", "b_spatial", "out_norm_a_g"]
SMALL_B = ["rel_bias_table", "mix_norm_g", "attn_sinks", "out_norm_b_g", "ffn_norm_g", "final_norm_g"]
LARGE = ["w_in", "w_out", "w_up", "w_down"]
ROW_TILE = {"w_in": 208, "w_out": 256, "w_up": 256, "w_down": 256}
WEIGHTS = ["rel_bias_table", "mix_norm_g", "w_in", "gate_norm_g", "gate_norm_b", "w_spatial", "b_spatial", "attn_sinks",
           "out_norm_a_g", "out_norm_b_g", "w_out", "ffn_norm_g", "w_up", "w_down", "final_norm_g"]
PACK_UNIT = 8 * 128


def _pack(parts):
    rows = []
    for p in parts:
        flat = p.reshape(-1)
        pad = (-flat.shape[0]) % PACK_UNIT
        rows.append(jnp.pad(flat, (0, pad)).reshape(-1, 128))
    return jnp.concatenate(rows, axis=0)


def _unpack(packed, like):
    out, row = [], 0
    for p in like:
        n = math.prod(p.shape)
        nrows = (n + PACK_UNIT - 1) // PACK_UNIT * 8
        out.append(packed[row:row + nrows].reshape(-1)[:n].reshape(p.shape))
        row += nrows
    return out


def kernel(x, rel_bias_table, mix_norm_g, w_in, gate_norm_g, gate_norm_b, w_spatial, b_spatial, attn_sinks, out_norm_a_g, out_norm_b_g, w_out, ffn_norm_g, w_up, w_down, final_norm_g, loss_target, m_rel_bias_table, m_mix_norm_g, m_w_in, m_gate_norm_g, m_gate_norm_b, m_w_spatial, m_b_spatial, m_attn_sinks, m_out_norm_a_g, m_out_norm_b_g, m_w_out, m_ffn_norm_g, m_w_up, m_w_down, m_final_norm_g, v_rel_bias_table, v_mix_norm_g, v_w_in, v_gate_norm_g, v_gate_norm_b, v_w_spatial, v_b_spatial, v_attn_sinks, v_out_norm_a_g, v_out_norm_b_g, v_w_out, v_ffn_norm_g, v_w_up, v_w_down, v_final_norm_g):
    args = dict(locals())
    wts = {n: args[n] for n in WEIGHTS}
    mom = {n: args["m_" + n] for n in WEIGHTS}
    var = {n: args["v_" + n] for n in WEIGHTS}
    sp = {n: wts[n] for n in SMALL}
    x2, tgt = x[0], loss_target[0]
    T = x2.shape[0]
    tm = min(512, T)
    tl = min(1024, T)
    lg = sp["gate_norm_g"].reshape(A_GROUPS, CHUNK)
    lb = sp["gate_norm_b"].reshape(A_GROUPS, CHUNK)
    wsp = sp["w_spatial"].reshape(A_GROUPS, CHUNK, CHUNK)
    bs_col = sp["b_spatial"].reshape(A_GROUPS, CHUNK, 1)
    sinks = sp["attn_sinks"].reshape(1, B_HEADS)
    ga = sp["out_norm_a_g"].reshape(1, A_WIDTH)
    gb = sp["out_norm_b_g"].reshape(1, B_WIDTH)
    g1 = sp["mix_norm_g"].reshape(1, D_MODEL)
    g2 = sp["ffn_norm_g"].reshape(1, D_MODEL)
    gf = sp["final_norm_g"].reshape(1, D_MODEL)

    def owner_total(n, gh, others):
        return _owner_total(gh, others, tm=ROW_TILE[n], name="rs_owner_total_" + n)

    def halves_view(at, shards):
        return at.reshape(shards, 2, at.shape[0] // shards // 2, at.shape[1])

    for d in (wts, mom, var):
        d["w_in"] = jnp.swapaxes(d["w_in"], 1, 2)

    s_in = _cast_into_slot(wts["w_in"], tm=ROW_TILE["w_in"], name="cast_w_in")
    (s_out, s_up, s_down), ((g_in,),) = _cast_into_slots_carrying(
        [wts["w_out"], wts["w_up"], wts["w_down"]], steps=8, name="cast_w_rest",
        rides=[_ride_gather(s_in, chain=(0, 1, 1), chain_fracs=(0.3, 0.6))])
    win_t = g_in.reshape(PROJ_WIDTH, D_MODEL)
    bias = _bias_build(sp["rel_bias_table"])
    (n1, proj), ((g_out,), (s_up,)) = _norm_matmul_wide(
        x2, g1, win_t, tm=tm, tn=PROJ_WIDTH // 2, name="in_proj",
        rides=[_ride_gather(s_out, chain=(0, 1, 1), chain_fracs=(0.65, 0.85)), _ride_gather(s_up, s1=(0, 3, 8))])
    wo = g_out.reshape(A_WIDTH + B_WIDTH, D_MODEL)
    (mixed, mixed_t, ab), ((s_up,), (s_down,), (n1_sib,)) = _mixer_fwd(
        proj, lg, lb, wsp, bs_col, sinks, bias, ga, gb,
        rides=[_ride_gather(s_up, s2=(0, 3, 8), s1=(3, 8, 8)), _ride_gather(s_down, s1=(0, 3, 8)),
               _ride_to_sibling(n1, first=True)])
    mixed_t = halves_view(mixed_t, N_CHIPS)
    h1, ((wu,), (s_down,), (mixed_t_sib,)) = _matmul_res(
        mixed, wo, x2, tm=tl, tn=1024, tk=D_MODEL, prologue=_to_bf16, name="out_proj",
        rides=[_ride_gather(s_up, s3=(0, 3, 8), tail=(3, 8, 8), mid_frac=0.75), _ride_gather(s_down, s2=(0, 3, 8)),
               _ride_to_sibling(mixed_t, halves=True)])
    (n2t, zp, z2, z2t), ((g_down,),) = _norm_matmul_sq(
        h1, g2, wu, tm=tl, tn=1024, name="up_proj", rides=[_ride_gather(s_down, s3=(0, 3, 8), chain=(3, 8, 8), chain_fracs=(0.45, 0.75))])
    wd = g_down.reshape(D_FF, D_MODEL)
    n2t, z2t = halves_view(n2t, 1), halves_view(z2t, N_CHIPS)
    h2, ((n2t_sib,), (z2t_sib,)) = _matmul_res(
        z2, wd, h1, tm=tl, tn=1024, tk=4096, prologue=_to_bf16, name="down_proj",
        rides=[_ride_to_sibling(n2t, halves=True), _ride_to_sibling(z2t, halves=True)])

    dh2, dh2b, dgf, loss = _loss_bwd(h2, tgt, gf, tm=tm)
    dzp, ((dh2b_sib,),) = _matmul_nt(dh2b, wd, tm=tl, tn=1024, tk=D_MODEL, name="bwd_dz", extra=zp,
                                     epilogue=_sq_relu_grad, out_dtype=BF16, rides=[_ride_to_sibling(dh2b)])
    (gd, gdb), ((dzp_sib,),) = _grad_pair(z2t, z2t_sib, dh2b, dh2b_sib, cols_sharded=False, tmo=1024, tk=tl,
                                          name="grad_w_down", rides=[_ride_to_sibling(dzp)])
    (gu, gub), ((o_d,),) = _grad_pair(n2t, n2t_sib, dzp, dzp_sib, cols_sharded=True, tmo=1024, tk=tl,
                                      name="grad_w_up", rides=[_ride_scatter(gdb, None, (0, 7, 8))])
    dn2, ((o_d,), (o_u,)) = _matmul_nt(dzp, wu, tm=tl, tn=1024, tk=4096, name="bwd_dn2",
                                       rides=[_ride_scatter(gdb, o_d, (7, 8, 8)), _ride_scatter(gub, None, (0, 6, 8))])
    h_d = owner_total("w_down", gd, o_d)
    (dh1, dh1b, dg2), ((o_u,),) = _rms_bwd_res(dn2, h1, g2, dh2, tm=tm, name="ffn_norm_bwd",
                                               rides=[_ride_scatter(gub, o_u, (6, 7, 8))])
    dmixed, ((o_u,), (dh1b_sib,), (w_d,)) = _matmul_nt(
        dh1b, wo, tm=tl, tn=1024, tk=D_MODEL, name="bwd_dmixed",
        rides=[_ride_scatter(gub, o_u, (7, 8, 8)), _ride_to_sibling(dh1b), _ride_swap(h_d)])
    h_u = owner_total("w_up", gu, o_u)
    (go, gob), ((w_u,),) = _grad_pair_merged(mixed_t, mixed_t_sib, dh1b, dh1b_sib, tk=tl, name="grad_w_out",
                                             rides=[_ride_swap(h_u)])
    (duv, duv_t, dga, dwsp, dbs, dlg, dlb), ((o_o,),) = _gmlp_bwd(proj, ab, dmixed, ga, lg, lb, wsp, bs_col,
                                                                  rides=[_ride_scatter(gob)])
    h_o = owner_total("w_out", go, o_o)
    small = {"gate_norm_g": dlg, "gate_norm_b": dlb, "w_spatial": dwsp, "b_spatial": dbs, "out_norm_a_g": dga}
    hr_in = PROJ_WIDTH // N_CHIPS // 2
    (dq, dkv, dq_t, dkv_t, dgb, dsinks, dbias), ((w_o,), (dproj_t_sib,)) = _attn_bwd(
        proj, ab, dmixed, gb, sinks, bias, rides=[_ride_swap(h_o), _ride_rows_to_sibling(duv_t, hr_in, 2, N_CHIPS)])
    dtable = _bias_grad(dbias)
    dproj_t = halves_view(jnp.concatenate([duv_t, dq_t, dkv_t], axis=0), N_CHIPS)
    ((dproj_t_sib,),) = _carrier([_ride_to_sibling(dproj_t, halves=True, shards=(2, N_CHIPS), land=dproj_t_sib)],
                                 name="trade_dproj_t")
    (gi, gib), ((slots_a,),) = _grad_pair(
        dproj_t, dproj_t_sib, n1, n1_sib, cols_sharded=False, tmo=hr_in, tk=tl, name="grad_w_in",
        rides=[_ride_small_to_all(_pack([small[n] for n in SMALL_A]))])
    dn1, ((o_i,),) = _matmul_parts([duv, dq, dkv], win_t, tm=tl, tn=1024, name="bwd_dn1", rides=[_ride_scatter(gib)])
    h_i = owner_total("w_in", gi, o_i)
    dx, dg1 = _rms_bwd_res(dn1, x2, g1, dh1, tm=tm, name="mix_norm_bwd", bf16_copy=False)
    small.update({"rel_bias_table": dtable.reshape(N_BUCKETS, B_HEADS), "mix_norm_g": dg1, "attn_sinks": dsinks,
                  "out_norm_b_g": dgb, "ffn_norm_g": dg2, "final_norm_g": dgf})
    (w_i,), (slots_b,) = _carrier([_ride_swap(h_i), _ride_small_to_all(_pack([small[n] for n in SMALL_B] + [loss]))],
                                  name="swap_w_in")

    out_g, out_d, out_m, out_v = {}, {}, {}, {}
    for n, h, s in zip(LARGE, [h_i, h_o, h_u, h_d], [w_i, w_o, w_u, w_d]):
        res = _adamw_halves(wts[n], h, s, mom[n], var[n], tm=ROW_TILE[n], name="adamw_" + n)
        if n == "w_in":
            res = [jnp.swapaxes(r, 1, 2) for r in res]
        out_g[n], out_d[n], out_m[n], out_v[n] = res
    for names, slots, tag in ((SMALL_A, slots_a, "a"), (SMALL_B, slots_b, "b")):
        extra = [jnp.zeros((1, 1), F32)] if tag == "b" else []
        like = [wts[n] for n in names] + extra
        res = _adamw_small(_pack(like), slots, _pack([mom[n] for n in names] + extra),
                           _pack([var[n] for n in names] + extra), name="adamw_small_" + tag)
        for store, packed in zip((out_g, out_d, out_m, out_v), res):
            for n, val in zip(names + ["loss"], _unpack(packed, like)):
                store[n] = val

    total = out_g["loss"][0, 0]
    return (total, dx[None], *[out_g[n] for n in WEIGHTS], *[out_d[n] for n in WEIGHTS],
            *[out_m[n] for n in WEIGHTS], *[out_v[n] for n in WEIGHTS])
```

```python
import math

import numpy as np
import jax
import jax.numpy as jnp
from jax import lax
from jax.experimental import pallas as pl
from jax.experimental.pallas import tpu as pltpu

F32 = jnp.float32
BF16 = jnp.bfloat16

D_MODEL = 2048
CHUNK = 128
A_GROUPS = 8
A_WIDTH = 1024
HEAD_DIM = 64
B_HEADS = 16
Q_PER_KV = 8
B_WIDTH = 1024
KV_WIDTH = 128
PROJ_WIDTH = 3328
D_FF = 8192
N_BUCKETS = 32
EPS = 1e-5
NEG = -1e30
SCALE = HEAD_DIM ** -0.5
N_CHIPS = 4
N_DEV = 8

ADAM_LR = 0.001
ADAM_B1 = 0.9
ADAM_B2 = 0.999
ADAM_EPS = 1e-08
ADAM_WD = 0.01
ADAM_STEP = 10

VMEM_LIMIT = 60 * 1024 * 1024
MESH = pl.DeviceIdType.MESH


def _bucket_thresholds():
    d = np.arange(CHUNK)
    n_exact = N_BUCKETS // 2
    relf = np.maximum(d, n_exact).astype(np.float64)
    large = n_exact + (np.log(relf / n_exact) / math.log(CHUNK / n_exact) * (N_BUCKETS - n_exact)).astype(np.int32)
    bucket = np.where(d < n_exact, d, np.minimum(large, N_BUCKETS - 1))
    return [int(np.min(d[bucket >= b])) for b in range(1, N_BUCKETS)]


BUCKET_THR = _bucket_thresholds()


def _params(sem=None):
    return pltpu.CompilerParams(dimension_semantics=sem, vmem_limit_bytes=VMEM_LIMIT)


def _gelu(x):
    c = math.sqrt(2.0 / math.pi)
    return 0.5 * x * (1.0 + jnp.tanh(c * (x + 0.044715 * (x * x * x))))


def _gelu_and_grad(x):
    c = math.sqrt(2.0 / math.pi)
    x2 = x * x
    t = jnp.tanh(c * (x + 0.044715 * (x2 * x)))
    g = 0.5 * x * (1.0 + t)
    dg = 0.5 * (1.0 + t) + 0.5 * x * (1.0 - t * t) * (c * (1.0 + 3.0 * 0.044715 * x2))
    return g, dg


def _dot(a, b):
    return jnp.dot(a, b, preferred_element_type=F32)


def _dot_nt(a, b):
    return lax.dot_general(a, b, (((1,), (1,)), ((), ())), preferred_element_type=F32)


def _dot_tn(a, b):
    return lax.dot_general(a, b, (((0,), (0,)), ((), ())), preferred_element_type=F32)


def _rms_bwd(dn, h, g):
    r = lax.rsqrt(jnp.mean(h * h, axis=-1, keepdims=True) + EPS)
    w = dn * g
    dh = r * w - h * ((r * r * r) * jnp.mean(w * h, axis=-1, keepdims=True))
    return dh, r


def _place():
    x, y, c = lax.axis_index("x"), lax.axis_index("y"), lax.axis_index("c")
    chips = [(1 - x, y), (x, 1 - y), (1 - x, 1 - y)]
    return x, y, c, chips


def _remote(src, dst, send_sem, recv_sem, to):
    return pltpu.make_async_remote_copy(src_ref=src, dst_ref=dst, send_sem=send_sem, recv_sem=recv_sem,
                                        device_id=to, device_id_type=MESH)


class _Ride:
    def __init__(self, args, out_shape, n_sem, start, finish, mids=(), aliases=None):
        self.args, self.out_shape, self.n_sem = list(args), list(out_shape), n_sem
        self.start, self.mids, self.finish = start, list(mids), finish
        self.aliases = dict(aliases or {})


def _call(body, *, name, grid, in_specs, out_specs, out_shape, scratch_shapes=(), sem=None, rides=(), aliases=None):
    single = not isinstance(out_shape, (list, tuple))
    out_specs = [out_specs] if single else list(out_specs)
    out_shape = [out_shape] if single else list(out_shape)
    n_in, n_out, n_scr = len(in_specs), len(out_shape), len(scratch_shapes)
    r_in = [len(r.args) for r in rides]
    r_out = [len(r.out_shape) for r in rides]
    any_spec = pl.BlockSpec(memory_space=pl.ANY)
    aliases, off_i, off_o = dict(aliases or {}), n_in, n_out
    for r in rides:
        for i, o in r.aliases.items():
            aliases[off_i + i] = off_o + o
        off_i += len(r.args)
        off_o += len(r.out_shape)
    steps = math.prod(grid)

    def wrapped(*refs):
        p = 0
        ins = refs[p:p + n_in]; p += n_in
        rins = refs[p:p + sum(r_in)]; p += sum(r_in)
        outs = refs[p:p + n_out]; p += n_out
        routs = refs[p:p + sum(r_out)]; p += sum(r_out)
        scr = refs[p:p + n_scr]; p += n_scr
        sems = refs[p:]
        parts, pi, po = [], 0, 0
        for k, r in enumerate(rides):
            parts.append((rins[pi:pi + r_in[k]], routs[po:po + r_out[k]], sems[2 * k], sems[2 * k + 1]))
            pi += r_in[k]
            po += r_out[k]
        lin = 0
        for d in range(len(grid)):
            lin = lin * grid[d] + pl.program_id(d)
        if rides:
            @pl.when(lin == 0)
            def _():
                for r, part in zip(rides, parts):
                    r.start(*part)
        body(*ins, *outs, *scr)
        for r, part in zip(rides, parts):
            for frac, fn in r.mids:
                @pl.when(lin == min(steps - 1, int(frac * steps)))
                def _(fn=fn, part=part):
                    fn(*part)
        if rides:
            @pl.when(lin == steps - 1)
            def _():
                for r, part in zip(rides, parts):
                    r.finish(*part)

    scratch = list(scratch_shapes)
    for r in rides:
        scratch += [pltpu.SemaphoreType.DMA((r.n_sem,)), pltpu.SemaphoreType.DMA((r.n_sem,))]
    if rides:
        sem = ("arbitrary",) * len(grid)
    res = pl.pallas_call(
        wrapped, name=name, grid=grid,
        in_specs=list(in_specs) + [any_spec] * sum(r_in),
        out_specs=out_specs + [any_spec] * sum(r_out),
        out_shape=out_shape + [s for r in rides for s in r.out_shape],
        scratch_shapes=scratch, input_output_aliases=aliases,
        compiler_params=_params(sem),
    )

    def run(*args):
        got = res(*args, *[a for r in rides for a in r.args])
        mine = got[0] if single else list(got[:n_out])
        if not rides:
            return mine
        rest, out = list(got[n_out:]), []
        for k in range(len(rides)):
            out.append(rest[:r_out[k]])
            rest = rest[r_out[k]:]
        return mine, out

    return run


def _ride_gather(slot, s1=None, s2=None, s3=None, tail=None, chain=None, mid_frac=0.6, chain_fracs=(0.35, 0.7)):
    half = slot.shape[1] // 2

    def rows(part, c, which=None):
        k0, k1, n = part
        count, first = (k1 - k0) * (half // n), c * half + k0 * (half // n)
        return pl.ds(first, count) if which is None else pl.ds(first + which * (count // 2), count // 2)

    def ids():
        x, y, c, _ = _place()
        return x, y, c, 2 * x + y, 2 * (1 - x) + y, 2 * x + (1 - y), 2 * (1 - x) + (1 - y)

    def copy(full, chip, r, ss, rs, k, to):
        piece = full.at[chip, r, :]
        return _remote(piece, piece, ss.at[k], rs.at[k], to)

    def to_neighbours(full, ss, rs, part, base):
        x, y, c, me, _, _, _ = ids()
        return [copy(full, me, rows(part, c), ss, rs, base, (1 - x, y, c)),
                copy(full, me, rows(part, c), ss, rs, base + 1, (x, 1 - y, c))]

    def from_neighbours(full, ss, rs, part, base):
        x, y, c, _, cx, cy, _ = ids()
        return [copy(full, cx, rows(part, c), ss, rs, base, (x, y, c)), copy(full, cy, rows(part, c), ss, rs, base + 1, (x, y, c))]

    def onward(full, ss, rs, part, base):
        x, y, c, _, cx, cy, _ = ids()
        return [copy(full, cx, rows(part, c, 0), ss, rs, base, (x, 1 - y, c)),
                copy(full, cy, rows(part, c, 1), ss, rs, base + 1, (1 - x, y, c))]

    def from_onward(full, ss, rs, part, base):
        x, y, c, _, _, _, cd = ids()
        return [copy(full, cd, rows(part, c, 0), ss, rs, base, (x, y, c)), copy(full, cd, rows(part, c, 1), ss, rs, base + 1, (x, y, c))]

    def to_sibling(full, ss, rs, part, base, diagonal):
        x, y, c, _, cx, cy, cd = ids()
        return [copy(full, chip, rows(part, c), ss, rs, base + j, (x, y, 1 - c))
                for j, chip in enumerate([cd] if diagonal else [cx, cy])]

    def from_sibling(full, ss, rs, part, base, diagonal):
        x, y, c, _, cx, cy, cd = ids()
        return [copy(full, chip, rows(part, 1 - c), ss, rs, base + j, (x, y, c))
                for j, chip in enumerate([cd] if diagonal else [cx, cy])]

    def start(ins, outs, ss, rs):
        full, cps = outs[0], []
        for part, base in ((s1, 0), (chain, 12)):
            if part is not None:
                cps += to_neighbours(full, ss, rs, part, base)
        for part, b_ici, b_sib in ((s2, 2, 4), (tail, 7, 9)):
            if part is not None:
                cps += onward(full, ss, rs, part, b_ici) + to_sibling(full, ss, rs, part, b_sib, False)
        if s3 is not None:
            cps += to_sibling(full, ss, rs, s3, 6, True)
        for cp in cps:
            cp.start()

    def second(part, b_in, b_ici, b_sib):
        def fn(ins, outs, ss, rs):
            for cp in from_neighbours(outs[0], ss, rs, part, b_in):
                cp.wait_recv()
            for cp in onward(outs[0], ss, rs, part, b_ici) + to_sibling(outs[0], ss, rs, part, b_sib, False):
                cp.start()
        return fn

    def third(part, b_ici, b_sib):
        def fn(ins, outs, ss, rs):
            for cp in from_onward(outs[0], ss, rs, part, b_ici):
                cp.wait_recv()
            for cp in to_sibling(outs[0], ss, rs, part, b_sib, True):
                cp.start()
        return fn

    mids = []
    if tail is not None:
        mids.append((mid_frac, third(tail, 7, 11)))
    if chain is not None:
        mids += [(chain_fracs[0], second(chain, 12, 14, 16)), (chain_fracs[1], third(chain, 14, 18))]

    def finish(ins, outs, ss, rs):
        full, got, sent = outs[0], [], []
        if s1 is not None:
            got += from_neighbours(full, ss, rs, s1, 0)
            sent += to_neighbours(full, ss, rs, s1, 0)
        if s2 is not None:
            got += from_onward(full, ss, rs, s2, 2) + from_sibling(full, ss, rs, s2, 4, False)
            sent += onward(full, ss, rs, s2, 2) + to_sibling(full, ss, rs, s2, 4, False)
        if s3 is not None:
            got += from_sibling(full, ss, rs, s3, 6, True)
            sent += to_sibling(full, ss, rs, s3, 6, True)
        if tail is not None:
            got += from_sibling(full, ss, rs, tail, 9, False) + from_sibling(full, ss, rs, tail, 11, True)
            sent += onward(full, ss, rs, tail, 7) + to_sibling(full, ss, rs, tail, 9, False) + to_sibling(full, ss, rs, tail, 11, True)
        if chain is not None:
            got += from_sibling(full, ss, rs, chain, 16, False) + from_sibling(full, ss, rs, chain, 18, True)
            sent += (to_neighbours(full, ss, rs, chain, 12) + onward(full, ss, rs, chain, 14)
                     + to_sibling(full, ss, rs, chain, 16, False) + to_sibling(full, ss, rs, chain, 18, True))
        for cp in got:
            cp.wait_recv()
        for cp in sent:
            cp.wait_send()

    return _Ride([slot], [jax.ShapeDtypeStruct(slot.shape, slot.dtype)], 19, start, finish, mids=mids, aliases={0: 0})


def _ride_scatter(q, land=None, part=(0, 1), to=(0, 1, 2)):
    k0, k1, n = part if len(part) == 3 else (part[0], part[0] + 1, part[1])
    rows_n = q.shape[1] // n
    rows = pl.ds(k0 * rows_n, (k1 - k0) * rows_n)

    def copies(ins, outs, ss, rs):
        x, y, c, chips = _place()
        return [_remote(ins[0].at[2 * chip[0] + chip[1], rows, :], outs[0].at[j, rows, :], ss.at[j], rs.at[j], (*chip, c))
                for j, chip in enumerate(chips) if j in to]

    def start(*a):
        for cp in copies(*a):
            cp.start()

    def finish(*a):
        for cp in copies(*a):
            cp.wait()

    shape = jax.ShapeDtypeStruct((3,) + q.shape[1:], q.dtype)
    if land is None:
        return _Ride([q], [shape], 3, start, finish)
    return _Ride([q, land], [shape], 3, start, finish, aliases={1: 0})


def _ride_to_sibling(a, halves=False, first=False, shards=None, land=None):
    s0, s1 = shards or (0, a.shape[0])

    def copy(ins, outs, ss, rs):
        x, y, c, _ = _place()
        if halves:
            src, dst = ins[0].at[s0:s1, 1 - c], outs[0].at[s0:s1]
        else:
            src, dst = (ins[0].at[0] if first else ins[0]), outs[0]
        return _remote(src, dst, ss.at[0], rs.at[0], (x, y, 1 - c))

    shape = (a.shape[0],) + a.shape[2:] if halves else (a.shape[1:] if first else a.shape)
    return _Ride([a] if land is None else [a, land], [jax.ShapeDtypeStruct(shape, a.dtype)], 1,
                 lambda *a_: copy(*a_).start(), lambda *a_: copy(*a_).wait(), aliases=None if land is None else {1: 0})


def _ride_rows_to_sibling(a, hr, shards, total):
    def copies(ins, outs, ss, rs):
        x, y, c, _ = _place()
        return [_remote(ins[0].at[pl.ds((2 * s + 1 - c) * hr, hr), :], outs[0].at[s], ss.at[s], rs.at[s], (x, y, 1 - c))
                for s in range(shards)]

    def start(*a_):
        for cp in copies(*a_):
            cp.start()

    def finish(*a_):
        for cp in copies(*a_):
            cp.wait()

    return _Ride([a], [jax.ShapeDtypeStruct((total, hr, a.shape[1]), a.dtype)], shards, start, finish)


def _ride_swap(h):
    def copy(ins, outs, ss, rs):
        x, y, c, _ = _place()
        return _remote(ins[0], outs[0], ss.at[0], rs.at[0], (x, y, 1 - c))

    return _Ride([h], [jax.ShapeDtypeStruct(h.shape, h.dtype)], 1,
                 lambda *a: copy(*a).start(), lambda *a: copy(*a).wait())


def _mesh_place(p):
    return (p // 4, (p // 2) % 2, p % 2)


def _ride_small_to_all(packed):
    def copies(ins, outs, ss, rs):
        x, y, c, _ = _place()
        me = 4 * x + 2 * y + c
        return [_remote(ins[0], outs[0].at[me], ss.at[k - 1], rs.at[k - 1], _mesh_place((me + k) % N_DEV))
                for k in range(1, N_DEV)]

    def own(ins, outs, ss, rs):
        x, y, c, _ = _place()
        return pltpu.make_async_copy(ins[0], outs[0].at[4 * x + 2 * y + c], ss.at[N_DEV - 1])

    def start(*a):
        own(*a).start()
        for cp in copies(*a):
            cp.start()

    def finish(ins, outs, ss, rs):
        x, y, c, _ = _place()
        me = 4 * x + 2 * y + c
        for k in range(1, N_DEV):
            _remote(ins[0], outs[0].at[(me + N_DEV - k) % N_DEV], ss.at[k - 1], rs.at[k - 1], (x, y, c)).wait_recv()
        for cp in copies(ins, outs, ss, rs):
            cp.wait_send()
        own(ins, outs, ss, rs).wait()

    return _Ride([packed], [jax.ShapeDtypeStruct((N_DEV,) + packed.shape, packed.dtype)], N_DEV, start, finish)


def _carrier(rides, *, name):
    _, outs = _call(lambda: None, name=name, grid=(1,), in_specs=[], out_specs=[], out_shape=[], rides=rides)()
    return outs


def _norm_bf16(a_ref, g_ref):
    xf = a_ref[...]
    r = lax.rsqrt(jnp.mean(xf * xf, axis=-1, keepdims=True) + EPS)
    return ((xf * r) * g_ref[...]).astype(BF16)


def _norm_matmul_wide(a, g, b, *, tm, tn, name, rides=()):
    T, K = a.shape
    N = b.shape[0]

    def body(a_ref, g_ref, b_ref, n_ref, o_ref):
        n = _norm_bf16(a_ref, g_ref)
        n_ref[...] = n
        o_ref[...] = _dot_nt(n, b_ref[...])

    return _call(
        body, name=name, grid=(N // tn, T // tm),
        in_specs=[pl.BlockSpec((tm, K), lambda j, i: (i, 0)), pl.BlockSpec((1, K), lambda j, i: (0, 0)),
                  pl.BlockSpec((tn, K), lambda j, i: (j, 0))],
        out_specs=[pl.BlockSpec((None, tm, K), lambda j, i: (j, i, 0)), pl.BlockSpec((tm, tn), lambda j, i: (i, j))],
        out_shape=[jax.ShapeDtypeStruct((N // tn, T, K), BF16), jax.ShapeDtypeStruct((T, N), F32)],
        sem=("arbitrary", "arbitrary"), rides=rides,
    )(a, g, b)


def _norm_matmul_sq(a, g, b, *, tm, tn, name, rides=()):
    T, K = a.shape
    per = b.shape[2] // tn
    N = b.shape[0] * b.shape[2]

    def body(a_ref, g_ref, b_ref, nt_ref, o_ref, z_ref, zt_ref, n_scr):
        @pl.when(pl.program_id(1) == 0)
        def _():
            n = _norm_bf16(a_ref, g_ref)
            n_scr[...] = n
            nt_ref[...] = n.T
        r = jnp.maximum(_dot(n_scr[...], b_ref[...]), 0.0)
        o_ref[...] = r.astype(BF16)
        z = (r * r).astype(BF16)
        z_ref[...] = z
        zt_ref[...] = z.T

    return _call(
        body, name=name, grid=(T // tm, N // tn),
        in_specs=[pl.BlockSpec((tm, K), lambda i, j: (i, 0)), pl.BlockSpec((1, K), lambda i, j: (0, 0)),
                  pl.BlockSpec((None, K, tn), lambda i, j: (j // per, 0, j % per))],
        out_specs=[pl.BlockSpec((K, tm), lambda i, j: (0, i)), pl.BlockSpec((tm, tn), lambda i, j: (i, j)),
                   pl.BlockSpec((tm, tn), lambda i, j: (i, j)), pl.BlockSpec((tn, tm), lambda i, j: (j, i))],
        out_shape=[jax.ShapeDtypeStruct((K, T), BF16), jax.ShapeDtypeStruct((T, N), BF16),
                   jax.ShapeDtypeStruct((T, N), BF16), jax.ShapeDtypeStruct((N, T), BF16)],
        scratch_shapes=[pltpu.VMEM((tm, K), BF16)],
        sem=("parallel", "arbitrary"), rides=rides,
    )(a, g, b)


def _grad_pair(at, at_sib, b, b_sib, *, cols_sharded, tmo, tk, name, shards=None, into=None, rides=()):
    S, _, hr, T = at.shape
    C = b.shape[-1] // N_CHIPS if cols_sharded else b.shape[-1]
    nk = T // tk

    def shard(s):
        if shards is None:
            return s
        x, y = lax.axis_index("x"), lax.axis_index("y")
        first, second = ((2 * (1 - x) + y, 2 * x + (1 - y)) if shards == "near" else (2 * (1 - x) + (1 - y), 2 * x + y))
        return jnp.where(s == 0, first, second)

    a_sel = (lambda s: 0) if cols_sharded else shard
    b_sel = shard if cols_sharded else (lambda s: 0)
    if b.ndim == 3:
        b_spec = pl.BlockSpec((None, tk, C), lambda s, i, k: (0, k, b_sel(s)))
    else:
        b_spec = pl.BlockSpec((tk, C), lambda s, i, k: (k, b_sel(s)))
    n_into = 0 if into is None else 1

    def body(a_ref, as_ref, b_ref, bs_ref, *rest):
        o_ref, ob_ref = rest[n_into:]
        k = pl.program_id(2)
        p = _dot(a_ref[...], b_ref[...]) + _dot(as_ref[...], bs_ref[...])

        @pl.when(k == 0)
        def _():
            o_ref[...] = p

        @pl.when(k > 0)
        def _():
            o_ref[...] += p

        @pl.when(k == nk - 1)
        def _():
            ob_ref[...] = o_ref[...].astype(BF16)

    out = pl.BlockSpec((None, tmo, C), lambda s, i, k: (shard(s), i, 0))
    held = [pl.BlockSpec(memory_space=pl.ANY)] * n_into
    return _call(
        body, name=name, grid=(N_CHIPS if shards is None else 2, hr // tmo, nk),
        in_specs=[pl.BlockSpec((None, None, tmo, tk), lambda s, i, k: (a_sel(s), lax.axis_index("c"), i, k)),
                  pl.BlockSpec((None, tmo, tk), lambda s, i, k: (a_sel(s), i, k)),
                  b_spec, pl.BlockSpec((tk, C), lambda s, i, k: (k, b_sel(s)))] + held,
        out_specs=[out, out],
        out_shape=[jax.ShapeDtypeStruct((N_CHIPS, hr, C), F32), jax.ShapeDtypeStruct((N_CHIPS, hr, C), BF16)],
        sem=("parallel", "parallel", "arbitrary"), rides=rides, aliases={4: 0} if into is not None else None,
    )(at, at_sib, b, b_sib, *([into] if into is not None else []))


def _grad_pair_merged(at, at_sib, b, b_sib, *, tk, name, rides=()):
    S, _, hr, T = at.shape
    C = b.shape[-1]
    nk = T // tk

    def body(a_ref, as_ref, b_ref, bs_ref, o_ref, ob_ref):
        k = pl.program_id(0)
        p = (_dot(a_ref[...].reshape(S * hr, tk), b_ref[...])
             + _dot(as_ref[...].reshape(S * hr, tk), bs_ref[...])).reshape(S, hr, C)

        @pl.when(k == 0)
        def _():
            o_ref[...] = p

        @pl.when(k > 0)
        def _():
            o_ref[...] += p

        @pl.when(k == nk - 1)
        def _():
            ob_ref[...] = o_ref[...].astype(BF16)

    out = pl.BlockSpec((S, hr, C), lambda k: (0, 0, 0))
    return _call(
        body, name=name, grid=(nk,),
        in_specs=[pl.BlockSpec((S, None, hr, tk), lambda k: (0, lax.axis_index("c"), 0, k)),
                  pl.BlockSpec((S, hr, tk), lambda k: (0, 0, k)),
                  pl.BlockSpec((tk, C), lambda k: (k, 0)), pl.BlockSpec((tk, C), lambda k: (k, 0))],
        out_specs=[out, out],
        out_shape=[jax.ShapeDtypeStruct((S, hr, C), F32), jax.ShapeDtypeStruct((S, hr, C), BF16)],
        sem=("arbitrary",), rides=rides,
    )(at, at_sib, b, b_sib)


def _matmul_parts(parts, b, *, tm, tn, name, rides=()):
    T = parts[0].shape[0]
    N = b.shape[1]
    offs = [sum(p.shape[1] for p in parts[:i]) for i in range(len(parts))]
    assert all(o % p.shape[1] == 0 for o, p in zip(offs, parts))

    def body(*refs):
        n = len(parts)
        acc = _dot(refs[0][...], refs[n][...])
        for i in range(1, n):
            acc = acc + _dot(refs[i][...], refs[n + i][...])
        refs[-1][...] = acc

    a_specs = [pl.BlockSpec((tm, p.shape[1]), lambda i, j: (i, 0)) for p in parts]
    b_specs = [pl.BlockSpec((p.shape[1], tn), lambda i, j, r=o // p.shape[1]: (r, j)) for o, p in zip(offs, parts)]
    return _call(
        body, name=name, grid=(T // tm, N // tn), in_specs=a_specs + b_specs,
        out_specs=pl.BlockSpec((tm, tn), lambda i, j: (i, j)), out_shape=jax.ShapeDtypeStruct((T, N), F32),
        sem=("parallel", "parallel"), rides=rides,
    )(*parts, *([b] * len(parts)))


def _to_bf16(v):
    return v.astype(BF16)


def _matmul_res(a, b, res, *, tm, tn, tk, prologue, name, rides=()):
    T, K = a.shape
    N = b.shape[1]

    def body(a_ref, b_ref, res_ref, o_ref):
        k = pl.program_id(2)
        p = _dot(prologue(a_ref[...]), b_ref[...])

        @pl.when(k == 0)
        def _():
            o_ref[...] = res_ref[...] + p

        @pl.when(k > 0)
        def _():
            o_ref[...] += p

    return _call(
        body, name=name, grid=(T // tm, N // tn, K // tk),
        in_specs=[pl.BlockSpec((tm, tk), lambda i, j, k: (i, k)), pl.BlockSpec((tk, tn), lambda i, j, k: (k, j)),
                  pl.BlockSpec((tm, tn), lambda i, j, k: (i, j))],
        out_specs=pl.BlockSpec((tm, tn), lambda i, j, k: (i, j)),
        out_shape=jax.ShapeDtypeStruct((T, N), F32),
        sem=("parallel", "parallel", "arbitrary"), rides=rides,
    )(a, b, res)


def _matmul_nt(a, b, *, tm, tn, tk, name, extra=None, epilogue=None, out_dtype=F32, rides=()):
    T, K = a.shape
    two = b.ndim == 3 and tk == 2 * b.shape[2]
    if two:
        N, ks = b.shape[1], b.shape[2]
        b_specs = [pl.BlockSpec((None, tn, ks), lambda i, j, k: (2 * k, j, 0)),
                   pl.BlockSpec((None, tn, ks), lambda i, j, k: (2 * k + 1, j, 0))]
    elif b.ndim == 3:
        per = b.shape[2] // tk
        N = b.shape[1]
        b_specs = [pl.BlockSpec((None, tn, tk), lambda i, j, k: (k // per, j, k % per))]
    else:
        N = b.shape[0]
        b_specs = [pl.BlockSpec((tn, tk), lambda i, j, k: (j, k))]
    nb = len(b_specs)
    nk = K // tk
    assert out_dtype == F32 or nk == 1
    in_specs = [pl.BlockSpec((tm, tk), lambda i, j, k: (i, k))] + b_specs
    args = [a] + [b] * nb
    if extra is not None:
        in_specs.append(pl.BlockSpec((tm, tn), lambda i, j, k: (i, j)))
        args.append(extra)

    def body(*refs):
        a_ref, b_ref = refs[0], refs[1]
        o_ref = refs[-1]
        if two:
            p = (_dot_nt(a_ref[:, :tk // 2].astype(BF16), refs[1][...])
                 + _dot_nt(a_ref[:, tk // 2:].astype(BF16), refs[2][...]))
        else:
            p = _dot_nt(a_ref[...].astype(BF16), b_ref[...])
        if nk == 1:
            if epilogue is not None:
                p = epilogue(p, refs[1 + nb][...])
            o_ref[...] = p.astype(out_dtype)
        else:
            k = pl.program_id(2)

            @pl.when(k == 0)
            def _():
                o_ref[...] = p

            @pl.when(k > 0)
            def _():
                o_ref[...] += p

    return _call(
        body, name=name, grid=(T // tm, N // tn, nk),
        in_specs=in_specs,
        out_specs=pl.BlockSpec((tm, tn), lambda i, j, k: (i, j)),
        out_shape=jax.ShapeDtypeStruct((T, N), out_dtype),
        sem=("parallel", "parallel", "arbitrary"), rides=rides,
    )(*args)


def _loss_bwd(h2, tgt, g, *, tm):
    T, D = h2.shape

    def body(h_ref, t_ref, g_ref, dh_ref, dhb_ref, dg_ref, loss_ref):
        @pl.when(pl.program_id(0) == 0)
        def _():
            dg_ref[...] = jnp.zeros_like(dg_ref)
            loss_ref[...] = jnp.zeros_like(loss_ref)
        h = h_ref[...]
        gg = g_ref[...]
        r = lax.rsqrt(jnp.mean(h * h, axis=-1, keepdims=True) + EPS)
        hn = h * r
        err = hn * gg - t_ref[...]
        loss_ref[...] += 0.5 * jnp.sum(jnp.mean(err * err, axis=-1, keepdims=True), axis=0, keepdims=True)
        dy = err * (1.0 / D)
        dg_ref[...] += jnp.sum(dy * hn, axis=0, keepdims=True)
        w = dy * gg
        dh = r * w - h * ((r * r * r) * jnp.mean(w * h, axis=-1, keepdims=True))
        dh_ref[...] = dh
        dhb_ref[...] = dh.astype(BF16)

    tile = pl.BlockSpec((tm, D), lambda i: (i, 0))
    return pl.pallas_call(
        body, name="loss_bwd", grid=(T // tm,),
        in_specs=[tile, tile, pl.BlockSpec((1, D), lambda i: (0, 0))],
        out_specs=[tile, tile, pl.BlockSpec((1, D), lambda i: (0, 0)), pl.BlockSpec((1, 1), lambda i: (0, 0))],
        out_shape=[jax.ShapeDtypeStruct((T, D), F32), jax.ShapeDtypeStruct((T, D), BF16),
                   jax.ShapeDtypeStruct((1, D), F32), jax.ShapeDtypeStruct((1, 1), F32)],
        compiler_params=_params(("arbitrary",)),
    )(h2, tgt, g)


def _rms_bwd_res(dn, h, g, dres, *, tm, name, bf16_copy=True, rides=()):
    T, D = h.shape

    def body(dn_ref, h_ref, g_ref, dres_ref, dh_ref, *rest):
        dg_ref = rest[-1]

        @pl.when(pl.program_id(0) == 0)
        def _():
            dg_ref[...] = jnp.zeros_like(dg_ref)
        h_ = h_ref[...]
        dn_ = dn_ref[...]
        dh, r = _rms_bwd(dn_, h_, g_ref[...])
        dg_ref[...] += jnp.sum(dn_ * (h_ * r), axis=0, keepdims=True)
        dh = dres_ref[...] + dh
        dh_ref[...] = dh
        if bf16_copy:
            rest[0][...] = dh.astype(BF16)

    tile = pl.BlockSpec((tm, D), lambda i: (i, 0))
    row = pl.BlockSpec((1, D), lambda i: (0, 0))
    copy_spec = [tile] if bf16_copy else []
    copy_shape = [jax.ShapeDtypeStruct((T, D), BF16)] if bf16_copy else []
    return _call(
        body, name=name, grid=(T // tm,),
        in_specs=[tile, tile, row, tile], out_specs=[tile] + copy_spec + [row],
        out_shape=[jax.ShapeDtypeStruct((T, D), F32)] + copy_shape + [jax.ShapeDtypeStruct((1, D), F32)],
        sem=("arbitrary",), rides=rides,
    )(dn, h, g, dres)


def _rel_distance():
    i = lax.broadcasted_iota(jnp.int32, (CHUNK, 2 * CHUNK), 0)
    j = lax.broadcasted_iota(jnp.int32, (CHUNK, 2 * CHUNK), 1)
    return i + CHUNK - j


def _bias_build(table):
    def body(tab_ref, o_ref):
        rel = _rel_distance()
        j = lax.broadcasted_iota(jnp.int32, (CHUNK, 2 * CHUNK), 1)
        band = (rel >= 0) & (rel < CHUNK)
        ge = [rel >= t for t in BUCKET_THR]
        for h in range(B_HEADS):
            cur = jnp.full((CHUNK, 2 * CHUNK), tab_ref[0, h], F32)
            for b in range(1, N_BUCKETS):
                cur = jnp.where(ge[b - 1], tab_ref[b, h], cur)
            o_ref[0, h] = jnp.where(band & (j >= CHUNK), cur, NEG)
            o_ref[1, h] = jnp.where(band, cur, NEG)

    return pl.pallas_call(
        body, name="bias_build",
        in_specs=[pl.BlockSpec(memory_space=pltpu.SMEM)],
        out_specs=pl.BlockSpec(memory_space=pltpu.VMEM),
        out_shape=jax.ShapeDtypeStruct((2, B_HEADS, CHUNK, 2 * CHUNK), F32),
    )(table)


def _bias_grad(dbias):
    def body(db_ref, o_ref, acc_ref):
        rel = _rel_distance()
        lo = [0] + BUCKET_THR
        hi = BUCKET_THR + [CHUNK]
        for b in range(N_BUCKETS):
            m = (rel >= lo[b]) & (rel < hi[b])
            for h in range(B_HEADS):
                row = b * B_HEADS + h
                acc_ref[row:row + 1, :] = jnp.sum(jnp.where(m, db_ref[h], 0.0), axis=0, keepdims=True)
        o_ref[...] = jnp.sum(acc_ref[...], axis=1, keepdims=True)

    return pl.pallas_call(
        body, name="bias_grad",
        in_specs=[pl.BlockSpec(memory_space=pltpu.VMEM)],
        out_specs=pl.BlockSpec(memory_space=pltpu.VMEM),
        out_shape=jax.ShapeDtypeStruct((N_BUCKETS * B_HEADS, 1), F32),
        scratch_shapes=[pltpu.VMEM((N_BUCKETS * B_HEADS, 2 * CHUNK), F32)],
    )(dbias)


def _causal_mask():
    t = lax.broadcasted_iota(jnp.int32, (CHUNK, CHUNK), 0)
    s = lax.broadcasted_iota(jnp.int32, (CHUNK, CHUNK), 1)
    return s <= t


def _gate_forward(u, v, lg, lb, wc, bs):
    ug = _gelu(u)
    vg = _gelu(v)
    mu = jnp.mean(vg, axis=-1, keepdims=True)
    xc = vg - mu
    rstd = lax.rsqrt(jnp.mean(xc * xc, axis=-1, keepdims=True) + EPS)
    xhat = xc * rstd
    vl = (xhat * lg + lb).astype(BF16)
    mixed = _dot(wc, vl) + bs
    return ug, xhat, rstd, vl, mixed


def _softmax_scores(qk, bias, sink):
    s = qk + bias
    m = jnp.maximum(jnp.max(s, axis=-1, keepdims=True), sink)
    p = jnp.exp(s - m)
    e_sink = jnp.exp(sink - m)
    inv = 1.0 / (jnp.sum(p, axis=-1, keepdims=True) + e_sink)
    return p * inv, e_sink * inv


PAIRS = Q_PER_KV // 2


def _head(g, pr, e):
    return g * Q_PER_KV + 2 * pr + e


def _stack_pairs(ref, g, col0=0):
    w = 2 * HEAD_DIM
    return jnp.concatenate([ref[:, col0 + (g * PAIRS + pr) * w:col0 + (g * PAIRS + pr + 1) * w] for pr in range(PAIRS)],
                           axis=0)


def _low_lanes():
    return lax.broadcasted_iota(jnp.int32, (2 * CHUNK, 2 * HEAD_DIM), 1) < HEAD_DIM


def _band_operands(kv_prev, kv_cur):
    band = jnp.concatenate([kv_prev, kv_cur], axis=0)
    low = _low_lanes()
    ops = []
    for cat in (band[:, :KV_WIDTH], band[:, KV_WIDTH:]):
        rol = pltpu.roll(cat, HEAD_DIM, 1)
        ops.append([[jnp.where(low if e == 0 else ~low, cat if g == e else rol, 0.0).astype(BF16) for e in range(2)]
                    for g in range(2)])
    return ops


def _mixer_fwd(proj, lg, lb, wsp, bs_col, sinks, bias, ga, gb, rides=()):
    T = proj.shape[0]
    nb = T // CHUNK

    def body(u_ref, v_ref, q_ref, kvc_ref, kvp_ref, lg_ref, lb_ref, w_ref, bs_ref, sink_ref, bias_ref,
             ga_ref, gb_ref, mixed_ref, mixed_t_ref, ab_ref):
        causal = _causal_mask()
        ssq = jnp.zeros((CHUNK, 1), F32)
        for g in range(A_GROUPS):
            cols = slice(g * CHUNK, (g + 1) * CHUNK)
            wc = jnp.where(causal, w_ref[g], 0.0).astype(BF16)
            ug, _, _, _, mixed = _gate_forward(u_ref[:, cols], v_ref[:, cols], lg_ref[g:g + 1, :], lb_ref[g:g + 1, :],
                                               wc, bs_ref[g])
            a = ug * mixed
            ab_ref[:, cols] = a
            ssq = ssq + jnp.sum(a * a, axis=-1, keepdims=True)
        ra = lax.rsqrt(ssq * (1.0 / A_WIDTH) + EPS)
        mixed_ref[:, :A_WIDTH] = ((ab_ref[:, :A_WIDTH] * ra) * ga_ref[...]).astype(BF16)

        kops, vops = _band_operands(kvp_ref[...], kvc_ref[...])
        ssq = jnp.zeros((CHUNK, 1), F32)
        for g in range(B_HEADS // Q_PER_KV):
            qst = (_stack_pairs(q_ref, g) * SCALE).astype(BF16)
            o_st = jnp.zeros((PAIRS * CHUNK, 2 * HEAD_DIM), F32)
            for e in range(2):
                s_all = _dot_nt(qst, kops[g][e])
                ps = []
                for pr in range(PAIRS):
                    h = _head(g, pr, e)
                    p, _ = _softmax_scores(s_all[pr * CHUNK:(pr + 1) * CHUNK], bias_ref[h], sink_ref[0, h])
                    ps.append(p.astype(BF16))
                o_st = o_st + _dot(jnp.concatenate(ps, axis=0), vops[g][e])
            for pr in range(PAIRS):
                o = o_st[pr * CHUNK:(pr + 1) * CHUNK]
                c0 = A_WIDTH + (g * PAIRS + pr) * 2 * HEAD_DIM
                ab_ref[:, c0:c0 + 2 * HEAD_DIM] = o
                ssq = ssq + jnp.sum(o * o, axis=-1, keepdims=True)
        rb = lax.rsqrt(ssq * (1.0 / B_WIDTH) + EPS)
        mixed_ref[:, A_WIDTH:] = ((ab_ref[:, A_WIDTH:] * rb) * gb_ref[...]).astype(BF16)
        mixed_t_ref[...] = mixed_ref[...].T

    full = lambda *shape: pl.BlockSpec(shape, lambda n: (0,) * len(shape))
    return _call(
        body, name="mixer_fwd", grid=(nb,),
        in_specs=[pl.BlockSpec((CHUNK, A_WIDTH), lambda n: (n, 0)),
                  pl.BlockSpec((CHUNK, A_WIDTH), lambda n: (n, 1)),
                  pl.BlockSpec((CHUNK, B_WIDTH), lambda n: (n, 2)),
                  pl.BlockSpec((CHUNK, 2 * KV_WIDTH), lambda n: (n, 12)),
                  pl.BlockSpec((CHUNK, 2 * KV_WIDTH), lambda n: (jnp.maximum(n - 1, 0), 12)),
                  full(A_GROUPS, CHUNK), full(A_GROUPS, CHUNK), full(A_GROUPS, CHUNK, CHUNK), full(A_GROUPS, CHUNK, 1),
                  pl.BlockSpec(memory_space=pltpu.SMEM),
                  pl.BlockSpec((None, B_HEADS, CHUNK, 2 * CHUNK), lambda n: (jnp.minimum(n, 1), 0, 0, 0)),
                  full(1, A_WIDTH), full(1, B_WIDTH)],
        out_specs=[pl.BlockSpec((CHUNK, D_MODEL), lambda n: (n, 0)), pl.BlockSpec((D_MODEL, CHUNK), lambda n: (0, n)),
                   pl.BlockSpec((CHUNK, D_MODEL), lambda n: (n, 0))],
        out_shape=[jax.ShapeDtypeStruct((T, D_MODEL), BF16), jax.ShapeDtypeStruct((D_MODEL, T), BF16),
                   jax.ShapeDtypeStruct((T, D_MODEL), F32)],
        sem=("parallel",), rides=rides,
    )(proj, proj, proj, proj, proj, lg, lb, wsp, bs_col, sinks, bias, ga, gb)


def _gmlp_bwd(proj, ab, dmixed, ga, lg, lb, wsp, bs_col, rides=()):
    T = proj.shape[0]
    nb = T // CHUNK

    def body(u_ref, v_ref, a_ref, dna_ref, ga_ref, lg_ref, lb_ref, w_ref, bs_ref,
             dp_ref, dpt_ref, dga_ref, dw_ref, dbs_ref, dlg_ref, dlb_ref):
        @pl.when(pl.program_id(0) == 0)
        def _():
            for r in (dga_ref, dw_ref, dbs_ref, dlg_ref, dlb_ref):
                r[...] = jnp.zeros_like(r)
        causal = _causal_mask()
        a_all = a_ref[...]
        dna = dna_ref[...]
        da_all, ra = _rms_bwd(dna, a_all, ga_ref[...])
        dga_ref[...] += jnp.sum(dna * (a_all * ra), axis=0, keepdims=True)
        for g in range(A_GROUPS):
            cols = slice(g * CHUNK, (g + 1) * CHUNK)
            wc = jnp.where(causal, w_ref[g], 0.0).astype(BF16)
            lgg = lg_ref[g:g + 1, :]
            u = u_ref[:, cols]
            v = v_ref[:, cols]
            ug, xhat, rstd, vl, mixed = _gate_forward(u, v, lgg, lb_ref[g:g + 1, :], wc, bs_ref[g])
            da = da_all[:, cols]
            dug = da * mixed
            dmg = da * ug
            dmg_b = dmg.astype(BF16)
            dbs_ref[g] += jnp.sum(dmg, axis=-1, keepdims=True)
            dw_ref[g] += jnp.where(causal, _dot_nt(dmg_b, vl), 0.0)
            dvl = _dot_tn(wc, dmg_b)
            dlg_ref[g:g + 1, :] += jnp.sum(dvl * xhat, axis=0, keepdims=True)
            dlb_ref[g:g + 1, :] += jnp.sum(dvl, axis=0, keepdims=True)
            dxh = dvl * lgg
            dvg = rstd * (dxh - jnp.mean(dxh, axis=-1, keepdims=True)
                          - xhat * jnp.mean(dxh * xhat, axis=-1, keepdims=True))
            _, gu = _gelu_and_grad(u)
            _, gv = _gelu_and_grad(v)
            dp_ref[:, cols] = (dug * gu).astype(BF16)
            dp_ref[:, A_WIDTH + g * CHUNK:A_WIDTH + (g + 1) * CHUNK] = (dvg * gv).astype(BF16)
        dpt_ref[...] = dp_ref[...].T

    full = lambda *shape: pl.BlockSpec(shape, lambda n: (0,) * len(shape))
    return _call(
        body, name="gmlp_bwd", grid=(nb,),
        in_specs=[pl.BlockSpec((CHUNK, A_WIDTH), lambda n: (n, 0)),
                  pl.BlockSpec((CHUNK, A_WIDTH), lambda n: (n, 1)),
                  pl.BlockSpec((CHUNK, A_WIDTH), lambda n: (n, 0)),
                  pl.BlockSpec((CHUNK, A_WIDTH), lambda n: (n, 0)),
                  full(1, A_WIDTH), full(A_GROUPS, CHUNK), full(A_GROUPS, CHUNK), full(A_GROUPS, CHUNK, CHUNK),
                  full(A_GROUPS, CHUNK, 1)],
        out_specs=[pl.BlockSpec((CHUNK, 2 * A_WIDTH), lambda n: (n, 0)), pl.BlockSpec((2 * A_WIDTH, CHUNK), lambda n: (0, n)),
                   full(1, A_WIDTH), full(A_GROUPS, CHUNK, CHUNK), full(A_GROUPS, CHUNK, 1),
                   full(A_GROUPS, CHUNK), full(A_GROUPS, CHUNK)],
        out_shape=[jax.ShapeDtypeStruct((T, 2 * A_WIDTH), BF16), jax.ShapeDtypeStruct((2 * A_WIDTH, T), BF16),
                   jax.ShapeDtypeStruct((1, A_WIDTH), F32), jax.ShapeDtypeStruct((A_GROUPS, CHUNK, CHUNK), F32),
                   jax.ShapeDtypeStruct((A_GROUPS, CHUNK, 1), F32), jax.ShapeDtypeStruct((A_GROUPS, CHUNK), F32),
                   jax.ShapeDtypeStruct((A_GROUPS, CHUNK), F32)],
        sem=("arbitrary",), rides=rides,
    )(proj, proj, ab, dmixed, ga, lg, lb, wsp, bs_col)


def _attn_bwd(proj, ab, dmixed, gb, sinks, bias, rides=()):
    T = proj.shape[0]
    nb = T // CHUNK
    qn = lambda n: jnp.minimum(n, nb - 1)

    def body(q_ref, kvc_ref, kvp_ref, o_ref, dnb_ref, gb_ref, sink_ref, bias_ref,
             dq_ref, dkv_ref, dqt_ref, dkvt_ref, dgb_ref, dsink_ref, dbias_ref, carry_ref, sacc_ref):
        n = pl.program_id(0)

        @pl.when(n == 0)
        def _():
            carry_ref[...] = jnp.zeros_like(carry_ref)
            sacc_ref[...] = jnp.zeros_like(sacc_ref)
            dgb_ref[...] = jnp.zeros_like(dgb_ref)
            dbias_ref[...] = jnp.zeros_like(dbias_ref)

        @pl.when(n < nb)
        def _():
            o_all = o_ref[...]
            dnb = dnb_ref[...]
            do_all, rb = _rms_bwd(dnb, o_all, gb_ref[...])
            dgb_ref[...] += jnp.sum(dnb * (o_all * rb), axis=0, keepdims=True)
            kops, vops = _band_operands(kvp_ref[...], kvc_ref[...])
            low = _low_lanes()
            halves = []
            for g in range(B_HEADS // Q_PER_KV):
                qst = (_stack_pairs(q_ref, g) * SCALE).astype(BF16)
                dost = _stack_pairs(do_all, g).astype(BF16)
                dq_st = jnp.zeros((PAIRS * CHUNK, 2 * HEAD_DIM), F32)
                dk_e, dv_e = [], []
                for e in range(2):
                    s_all = _dot_nt(qst, kops[g][e])
                    dp_all = _dot_nt(dost, vops[g][e])
                    ps, dsrs = [], []
                    for pr in range(PAIRS):
                        h = _head(g, pr, e)
                        rows = slice(pr * CHUNK, (pr + 1) * CHUNK)
                        p, p_sink = _softmax_scores(s_all[rows], bias_ref[h], sink_ref[0, h])
                        dp = dp_all[rows]
                        delta = jnp.sum(p * dp, axis=-1, keepdims=True)
                        ds = p * (dp - delta)
                        sacc_ref[:, h:h + 1] += -(p_sink * delta)
                        dbias_ref[h] += ds
                        ps.append(p.astype(BF16))
                        dsrs.append(ds.astype(BF16))
                    dsr_all = jnp.concatenate(dsrs, axis=0)
                    dq_st = dq_st + _dot(dsr_all, kops[g][e])
                    dk_e.append(_dot_tn(dsr_all, qst))
                    dv_e.append(_dot_tn(jnp.concatenate(ps, axis=0), dost))
                for pr in range(PAIRS):
                    c0 = (g * PAIRS + pr) * 2 * HEAD_DIM
                    dq_ref[:, c0:c0 + 2 * HEAD_DIM] = (dq_st[pr * CHUNK:(pr + 1) * CHUNK] * SCALE).astype(BF16)
                halves.append((dk_e, dv_e))
            tiles = []
            for t in range(2):
                g0, g1 = halves[0][t], halves[1][t]
                tiles.append(jnp.where(low, g0[0] + pltpu.roll(g0[1], HEAD_DIM, 1), pltpu.roll(g1[0], HEAD_DIM, 1) + g1[1]))
            dband = jnp.concatenate(tiles, axis=1)
            dkv = (carry_ref[...] + dband[:CHUNK]).astype(BF16)
            dkv_ref[...] = dkv
            dkvt_ref[...] = dkv.T
            dqt_ref[...] = dq_ref[...].T
            carry_ref[...] = dband[CHUNK:]

        @pl.when(n == nb)
        def _():
            dkv = carry_ref[...].astype(BF16)
            dkv_ref[...] = dkv
            dkvt_ref[...] = dkv.T
            dsink_ref[...] = jnp.sum(sacc_ref[...], axis=0, keepdims=True)

    full = lambda *shape: pl.BlockSpec(shape, lambda n: (0,) * len(shape))
    return _call(
        body, name="attn_bwd", grid=(nb + 1,),
        in_specs=[pl.BlockSpec((CHUNK, B_WIDTH), lambda n: (qn(n), 2)),
                  pl.BlockSpec((CHUNK, 2 * KV_WIDTH), lambda n: (qn(n), 12)),
                  pl.BlockSpec((CHUNK, 2 * KV_WIDTH), lambda n: (jnp.maximum(qn(n) - 1, 0), 12)),
                  pl.BlockSpec((CHUNK, B_WIDTH), lambda n: (qn(n), 1)),
                  pl.BlockSpec((CHUNK, B_WIDTH), lambda n: (qn(n), 1)),
                  full(1, B_WIDTH), pl.BlockSpec(memory_space=pltpu.SMEM),
                  pl.BlockSpec((None, B_HEADS, CHUNK, 2 * CHUNK), lambda n: (jnp.minimum(n, 1), 0, 0, 0))],
        out_specs=[pl.BlockSpec((CHUNK, B_WIDTH), lambda n: (qn(n), 0)),
                   pl.BlockSpec((CHUNK, 2 * KV_WIDTH), lambda n: (jnp.maximum(n - 1, 0), 0)),
                   pl.BlockSpec((B_WIDTH, CHUNK), lambda n: (0, qn(n))),
                   pl.BlockSpec((2 * KV_WIDTH, CHUNK), lambda n: (0, jnp.maximum(n - 1, 0))),
                   full(1, B_WIDTH), full(1, B_HEADS), full(B_HEADS, CHUNK, 2 * CHUNK)],
        out_shape=[jax.ShapeDtypeStruct((T, B_WIDTH), BF16), jax.ShapeDtypeStruct((T, 2 * KV_WIDTH), BF16),
                   jax.ShapeDtypeStruct((B_WIDTH, T), BF16), jax.ShapeDtypeStruct((2 * KV_WIDTH, T), BF16),
                   jax.ShapeDtypeStruct((1, B_WIDTH), F32), jax.ShapeDtypeStruct((1, B_HEADS), F32),
                   jax.ShapeDtypeStruct((B_HEADS, CHUNK, 2 * CHUNK), F32)],
        scratch_shapes=[pltpu.VMEM((CHUNK, 2 * KV_WIDTH), F32), pltpu.VMEM((CHUNK, B_HEADS), F32)],
        sem=("arbitrary",), rides=rides,
    )(proj, proj, proj, ab, dmixed, gb, sinks, bias)


def _sq_relu_grad(acc, r):
    return acc * (2.0 * r.astype(F32))


def _chip_index():
    return (2 * lax.axis_index("x") + lax.axis_index("y")).astype(jnp.int32).reshape(1)


def _cast_into_slot(w, *, tm, name):
    _, R, C = w.shape

    def body(me_ref, w_ref, o_ref):
        del me_ref
        o_ref[...] = w_ref[...].astype(BF16)

    return pl.pallas_call(
        body, name=name,
        grid_spec=pltpu.PrefetchScalarGridSpec(
            num_scalar_prefetch=1, grid=(R // tm,),
            in_specs=[pl.BlockSpec((None, tm, C), lambda i, me: (0, i, 0))],
            out_specs=pl.BlockSpec((None, tm, C), lambda i, me: (me[0], i, 0))),
        out_shape=jax.ShapeDtypeStruct((N_CHIPS, R, C), BF16), compiler_params=_params(("parallel",)),
    )(_chip_index(), w)


def _cast_into_slots_carrying(ws, *, steps, name, rides):
    n = len(ws)

    def body(*refs):
        for w_ref, o_ref in zip(refs[:n], refs[n:]):
            o_ref[...] = w_ref[...].astype(BF16)

    me = lambda: 2 * lax.axis_index("x") + lax.axis_index("y")
    return _call(
        body, name=name, grid=(steps,),
        in_specs=[pl.BlockSpec((None, w.shape[1] // steps, w.shape[2]), lambda i: (0, i, 0)) for w in ws],
        out_specs=[pl.BlockSpec((None, w.shape[1] // steps, w.shape[2]), lambda i: (me(), i, 0)) for w in ws],
        out_shape=[jax.ShapeDtypeStruct((N_CHIPS,) + w.shape[1:], BF16) for w in ws], sem=("arbitrary",), rides=rides,
    )(*ws)


def _owner_total(gh, others, *, tm, name):
    _, hr, C = gh.shape

    def body(me_ref, g_ref, o_ref_in, out_ref):
        del me_ref
        acc = g_ref[...]
        for j in range(3):
            acc = acc + o_ref_in[j].astype(F32)
        out_ref[...] = acc

    return pl.pallas_call(
        body, name=name,
        grid_spec=pltpu.PrefetchScalarGridSpec(
            num_scalar_prefetch=1, grid=(hr // tm,),
            in_specs=[pl.BlockSpec((None, tm, C), lambda i, me: (me[0], i, 0)),
                      pl.BlockSpec((3, tm, C), lambda i, me: (0, i, 0))],
            out_specs=pl.BlockSpec((tm, C), lambda i, me: (i, 0))),
        out_shape=jax.ShapeDtypeStruct((hr, C), F32),
        compiler_params=_params(("parallel",)),
    )(_chip_index(), gh, others)


def _adamw_math(w, g, m, v):
    m = ADAM_B1 * m + (1.0 - ADAM_B1) * g
    v = ADAM_B2 * v + (1.0 - ADAM_B2) * (g * g)
    m_hat = m / (1.0 - ADAM_B1 ** ADAM_STEP)
    v_hat = v / (1.0 - ADAM_B2 ** ADAM_STEP)
    delta = -ADAM_LR * (m_hat / (jnp.sqrt(v_hat) + ADAM_EPS) + ADAM_WD * w)
    return delta, m, v


def _adamw_halves(w, own, got, m, v, *, tm, name, rides=()):
    _, R, C = w.shape
    nt = (R // 2) // tm

    def body(w_ref, own_ref, got_ref, m_ref, v_ref, g_ref, d_ref, nm_ref, nv_ref):
        g = jnp.where(pl.program_id(0) == lax.axis_index("c"), own_ref[...], got_ref[...])
        g_ref[...] = g
        d_ref[...], nm_ref[...], nv_ref[...] = _adamw_math(w_ref[...], g, m_ref[...], v_ref[...])

    whole = pl.BlockSpec((None, tm, C), lambda h, i: (0, h * nt + i, 0))
    half = pl.BlockSpec((tm, C), lambda h, i: (i, 0))
    return _call(
        body, name=name, grid=(2, nt), in_specs=[whole, half, half, whole, whole], out_specs=[whole] * 4,
        out_shape=[jax.ShapeDtypeStruct((1, R, C), F32)] * 4, sem=("parallel", "parallel"), rides=rides,
    )(w, own, got, m, v)


def _adamw_small(w, slots, m, v, *, name):
    def body(w_ref, slots_ref, m_ref, v_ref, g_ref, d_ref, nm_ref, nv_ref):
        g = slots_ref[0]
        for d in range(1, N_DEV):
            g = g + slots_ref[d]
        g_ref[...] = g
        d_ref[...], nm_ref[...], nv_ref[...] = _adamw_math(w_ref[...], g, m_ref[...], v_ref[...])

    vmem = pl.BlockSpec(memory_space=pltpu.VMEM)
    return pl.pallas_call(
        body, name=name, in_specs=[vmem] * 4, out_specs=[vmem] * 4,
        out_shape=[jax.ShapeDtypeStruct(w.shape, F32)] * 4, compiler_params=_params(),
    )(w, slots, m, v)


SMALL = ["rel_bias_table", "mix_norm_g", "gate_norm_g", "gate_norm_b", "w_spatial", "b_spatial", "attn_sinks",
         "out_norm_a_g", "out_norm_b_g", "ffn_norm_g", "final_norm_g"]
SMALL_A = ["gate_norm_g", "gate_norm_b", "w_spatial", "b_spatial", "out_norm_a_g"]
SMALL_B = ["rel_bias_table", "mix_norm_g", "attn_sinks", "out_norm_b_g", "ffn_norm_g", "final_norm_g"]
LARGE = ["w_in", "w_out", "w_up", "w_down"]
ROW_TILE = {"w_in": 208, "w_out": 256, "w_up": 256, "w_down": 256}
WEIGHTS = ["rel_bias_table", "mix_norm_g", "w_in", "gate_norm_g", "gate_norm_b", "w_spatial", "b_spatial", "attn_sinks",
           "out_norm_a_g", "out_norm_b_g", "w_out", "ffn_norm_g", "w_up", "w_down", "final_norm_g"]
PACK_UNIT = 8 * 128


def _pack(parts):
    rows = []
    for p in parts:
        flat = p.reshape(-1)
        pad = (-flat.shape[0]) % PACK_UNIT
        rows.append(jnp.pad(flat, (0, pad)).reshape(-1, 128))
    return jnp.concatenate(rows, axis=0)


def _unpack(packed, like):
    out, row = [], 0
    for p in like:
        n = math.prod(p.shape)
        nrows = (n + PACK_UNIT - 1) // PACK_UNIT * 8
        out.append(packed[row:row + nrows].reshape(-1)[:n].reshape(p.shape))
        row += nrows
    return out


def kernel(x, rel_bias_table, mix_norm_g, w_in, gate_norm_g, gate_norm_b, w_spatial, b_spatial, attn_sinks, out_norm_a_g, out_norm_b_g, w_out, ffn_norm_g, w_up, w_down, final_norm_g, loss_target, m_rel_bias_table, m_mix_norm_g, m_w_in, m_gate_norm_g, m_gate_norm_b, m_w_spatial, m_b_spatial, m_attn_sinks, m_out_norm_a_g, m_out_norm_b_g, m_w_out, m_ffn_norm_g, m_w_up, m_w_down, m_final_norm_g, v_rel_bias_table, v_mix_norm_g, v_w_in, v_gate_norm_g, v_gate_norm_b, v_w_spatial, v_b_spatial, v_attn_sinks, v_out_norm_a_g, v_out_norm_b_g, v_w_out, v_ffn_norm_g, v_w_up, v_w_down, v_final_norm_g):
    args = dict(locals())
    wts = {n: args[n] for n in WEIGHTS}
    mom = {n: args["m_" + n] for n in WEIGHTS}
    var = {n: args["v_" + n] for n in WEIGHTS}
    sp = {n: wts[n] for n in SMALL}
    x2, tgt = x[0], loss_target[0]
    T = x2.shape[0]
    tm = min(512, T)
    tl = min(1024, T)
    lg = sp["gate_norm_g"].reshape(A_GROUPS, CHUNK)
    lb = sp["gate_norm_b"].reshape(A_GROUPS, CHUNK)
    wsp = sp["w_spatial"].reshape(A_GROUPS, CHUNK, CHUNK)
    bs_col = sp["b_spatial"].reshape(A_GROUPS, CHUNK, 1)
    sinks = sp["attn_sinks"].reshape(1, B_HEADS)
    ga = sp["out_norm_a_g"].reshape(1, A_WIDTH)
    gb = sp["out_norm_b_g"].reshape(1, B_WIDTH)
    g1 = sp["mix_norm_g"].reshape(1, D_MODEL)
    g2 = sp["ffn_norm_g"].reshape(1, D_MODEL)
    gf = sp["final_norm_g"].reshape(1, D_MODEL)

    def owner_total(n, gh, others):
        return _owner_total(gh, others, tm=ROW_TILE[n], name="rs_owner_total_" + n)

    def halves_view(at, shards):
        return at.reshape(shards, 2, at.shape[0] // shards // 2, at.shape[1])

    for d in (wts, mom, var):
        d["w_in"] = jnp.swapaxes(d["w_in"], 1, 2)

    s_in = _cast_into_slot(wts["w_in"], tm=ROW_TILE["w_in"], name="cast_w_in")
    (s_out, s_up, s_down), ((g_in,),) = _cast_into_slots_carrying(
        [wts["w_out"], wts["w_up"], wts["w_down"]], steps=8, name="cast_w_rest",
        rides=[_ride_gather(s_in, chain=(0, 1, 1), chain_fracs=(0.3, 0.6))])
    win_t = g_in.reshape(PROJ_WIDTH, D_MODEL)
    bias = _bias_build(sp["rel_bias_table"])
    (n1, proj), ((g_out,), (s_up,)) = _norm_matmul_wide(
        x2, g1, win_t, tm=tm, tn=PROJ_WIDTH // 2, name="in_proj",
        rides=[_ride_gather(s_out, chain=(0, 1, 1), chain_fracs=(0.65, 0.85)), _ride_gather(s_up, s1=(0, 3, 8))])
    wo = g_out.reshape(A_WIDTH + B_WIDTH, D_MODEL)
    (mixed, mixed_t, ab), ((s_up,), (s_down,), (n1_sib,)) = _mixer_fwd(
        proj, lg, lb, wsp, bs_col, sinks, bias, ga, gb,
        rides=[_ride_gather(s_up, s2=(0, 3, 8), s1=(3, 8, 8)), _ride_gather(s_down, s1=(0, 3, 8)),
               _ride_to_sibling(n1, first=True)])
    mixed_t = halves_view(mixed_t, N_CHIPS)
    h1, ((wu,), (s_down,), (mixed_t_sib,)) = _matmul_res(
        mixed, wo, x2, tm=tl, tn=1024, tk=D_MODEL, prologue=_to_bf16, name="out_proj",
        rides=[_ride_gather(s_up, s3=(0, 3, 8), tail=(3, 8, 8), mid_frac=0.75), _ride_gather(s_down, s2=(0, 3, 8)),
               _ride_to_sibling(mixed_t, halves=True)])
    (n2t, zp, z2, z2t), ((g_down,),) = _norm_matmul_sq(
        h1, g2, wu, tm=tl, tn=1024, name="up_proj", rides=[_ride_gather(s_down, s3=(0, 3, 8), chain=(3, 8, 8), chain_fracs=(0.45, 0.75))])
    wd = g_down.reshape(D_FF, D_MODEL)
    n2t, z2t = halves_view(n2t, 1), halves_view(z2t, N_CHIPS)
    h2, ((n2t_sib,), (z2t_sib,)) = _matmul_res(
        z2, wd, h1, tm=tl, tn=1024, tk=4096, prologue=_to_bf16, name="down_proj",
        rides=[_ride_to_sibling(n2t, halves=True), _ride_to_sibling(z2t, halves=True)])

    dh2, dh2b, dgf, loss = _loss_bwd(h2, tgt, gf, tm=tm)
    dzp, ((dh2b_sib,),) = _matmul_nt(dh2b, wd, tm=tl, tn=1024, tk=D_MODEL, name="bwd_dz", extra=zp,
                                     epilogue=_sq_relu_grad, out_dtype=BF16, rides=[_ride_to_sibling(dh2b)])
    (gd, gdb), ((dzp_sib,),) = _grad_pair(z2t, z2t_sib, dh2b, dh2b_sib, cols_sharded=False, tmo=1024, tk=tl,
                                          name="grad_w_down", rides=[_ride_to_sibling(dzp)])
    (gu, gub), ((o_d,),) = _grad_pair(n2t, n2t_sib, dzp, dzp_sib, cols_sharded=True, tmo=1024, tk=tl,
                                      name="grad_w_up", rides=[_ride_scatter(gdb, None, (0, 7, 8))])
    dn2, ((o_d,), (o_u,)) = _matmul_nt(dzp, wu, tm=tl, tn=1024, tk=4096, name="bwd_dn2",
                                       rides=[_ride_scatter(gdb, o_d, (7, 8, 8)), _ride_scatter(gub, None, (0, 6, 8))])
    h_d = owner_total("w_down", gd, o_d)
    (dh1, dh1b, dg2), ((o_u,),) = _rms_bwd_res(dn2, h1, g2, dh2, tm=tm, name="ffn_norm_bwd",
                                               rides=[_ride_scatter(gub, o_u, (6, 7, 8))])
    dmixed, ((o_u,), (dh1b_sib,), (w_d,)) = _matmul_nt(
        dh1b, wo, tm=tl, tn=1024, tk=D_MODEL, name="bwd_dmixed",
        rides=[_ride_scatter(gub, o_u, (7, 8, 8)), _ride_to_sibling(dh1b), _ride_swap(h_d)])
    h_u = owner_total("w_up", gu, o_u)
    (go, gob), ((w_u,),) = _grad_pair_merged(mixed_t, mixed_t_sib, dh1b, dh1b_sib, tk=tl, name="grad_w_out",
                                             rides=[_ride_swap(h_u)])
    (duv, duv_t, dga, dwsp, dbs, dlg, dlb), ((o_o,),) = _gmlp_bwd(proj, ab, dmixed, ga, lg, lb, wsp, bs_col,
                                                                  rides=[_ride_scatter(gob)])
    h_o = owner_total("w_out", go, o_o)
    small = {"gate_norm_g": dlg, "gate_norm_b": dlb, "w_spatial": dwsp, "b_spatial": dbs, "out_norm_a_g": dga}
    hr_in = PROJ_WIDTH // N_CHIPS // 2
    (dq, dkv, dq_t, dkv_t, dgb, dsinks, dbias), ((w_o,), (dproj_t_sib,)) = _attn_bwd(
        proj, ab, dmixed, gb, sinks, bias, rides=[_ride_swap(h_o), _ride_rows_to_sibling(duv_t, hr_in, 2, N_CHIPS)])
    dtable = _bias_grad(dbias)
    dproj_t = halves_view(jnp.concatenate([duv_t, dq_t, dkv_t], axis=0), N_CHIPS)
    ((dproj_t_sib,),) = _carrier([_ride_to_sibling(dproj_t, halves=True, shards=(2, N_CHIPS), land=dproj_t_sib)],
                                 name="trade_dproj_t")
    (gi, gib_near), ((slots_a,),) = _grad_pair(
        dproj_t, dproj_t_sib, n1, n1_sib, cols_sharded=False, tmo=hr_in, tk=tl, name="grad_w_in_near", shards="near",
        rides=[_ride_small_to_all(_pack([small[n] for n in SMALL_A]))])
    (gi, gib_far), ((o_i,),) = _grad_pair(
        dproj_t, dproj_t_sib, n1, n1_sib, cols_sharded=False, tmo=hr_in, tk=tl, name="grad_w_in_far", shards="far",
        into=gi, rides=[_ride_scatter(gib_near, None, to=(0, 1))])
    dn1, ((o_i,),) = _matmul_parts([duv, dq, dkv], win_t, tm=tl, tn=1024, name="bwd_dn1",
                                   rides=[_ride_scatter(gib_far, o_i, to=(2,))])
    h_i = owner_total("w_in", gi, o_i)
    dx, dg1 = _rms_bwd_res(dn1, x2, g1, dh1, tm=tm, name="mix_norm_bwd", bf16_copy=False)
    small.update({"rel_bias_table": dtable.reshape(N_BUCKETS, B_HEADS), "mix_norm_g": dg1, "attn_sinks": dsinks,
                  "out_norm_b_g": dgb, "ffn_norm_g": dg2, "final_norm_g": dgf})
    (w_i,), (slots_b,) = _carrier([_ride_swap(h_i), _ride_small_to_all(_pack([small[n] for n in SMALL_B] + [loss]))],
                                  name="swap_w_in")

    out_g, out_d, out_m, out_v = {}, {}, {}, {}
    for n, h, s in zip(LARGE, [h_i, h_o, h_u, h_d], [w_i, w_o, w_u, w_d]):
        res = _adamw_halves(wts[n], h, s, mom[n], var[n], tm=ROW_TILE[n], name="adamw_" + n)
        if n == "w_in":
            res = [jnp.swapaxes(r, 1, 2) for r in res]
        out_g[n], out_d[n], out_m[n], out_v[n] = res
    for names, slots, tag in ((SMALL_A, slots_a, "a"), (SMALL_B, slots_b, "b")):
        extra = [jnp.zeros((1, 1), F32)] if tag == "b" else []
        like = [wts[n] for n in names] + extra
        res = _adamw_small(_pack(like), slots, _pack([mom[n] for n in names] + extra),
                           _pack([var[n] for n in names] + extra), name="adamw_small_" + tag)
        for store, packed in zip((out_g, out_d, out_m, out_v), res):
            for n, val in zip(names + ["loss"], _unpack(packed, like)):
                store[n] = val

    total = out_g["loss"][0, 0]
    return (total, dx[None], *[out_g[n] for n in WEIGHTS], *[out_d[n] for n in WEIGHTS],
            *[out_m[n] for n in WEIGHTS], *[out_v[n] for n in WEIGHTS])
```

```python
import math

import numpy as np
import jax
import jax.numpy as jnp
from jax import lax
from jax.experimental import pallas as pl
from jax.experimental.pallas import tpu as pltpu

F32 = jnp.float32
BF16 = jnp.bfloat16

D_MODEL = 2048
CHUNK = 128
A_GROUPS = 8
A_WIDTH = 1024
HEAD_DIM = 64
B_HEADS = 16
Q_PER_KV = 8
B_WIDTH = 1024
KV_WIDTH = 128
PROJ_WIDTH = 3328
D_FF = 8192
N_BUCKETS = 32
EPS = 1e-5
NEG = -1e30
SCALE = HEAD_DIM ** -0.5
N_CHIPS = 4
N_DEV = 8

ADAM_LR = 0.001
ADAM_B1 = 0.9
ADAM_B2 = 0.999
ADAM_EPS = 1e-08
ADAM_WD = 0.01
ADAM_STEP = 10

VMEM_LIMIT = 60 * 1024 * 1024
MESH = pl.DeviceIdType.MESH


def _bucket_thresholds():
    d = np.arange(CHUNK)
    n_exact = N_BUCKETS // 2
    relf = np.maximum(d, n_exact).astype(np.float64)
    large = n_exact + (np.log(relf / n_exact) / math.log(CHUNK / n_exact) * (N_BUCKETS - n_exact)).astype(np.int32)
    bucket = np.where(d < n_exact, d, np.minimum(large, N_BUCKETS - 1))
    return [int(np.min(d[bucket >= b])) for b in range(1, N_BUCKETS)]


BUCKET_THR = _bucket_thresholds()


def _params(sem=None):
    return pltpu.CompilerParams(dimension_semantics=sem, vmem_limit_bytes=VMEM_LIMIT)


def _gelu(x):
    c = math.sqrt(2.0 / math.pi)
    return 0.5 * x * (1.0 + jnp.tanh(c * (x + 0.044715 * (x * x * x))))


def _gelu_and_grad(x):
    c = math.sqrt(2.0 / math.pi)
    x2 = x * x
    t = jnp.tanh(c * (x + 0.044715 * (x2 * x)))
    g = 0.5 * x * (1.0 + t)
    dg = 0.5 * (1.0 + t) + 0.5 * x * (1.0 - t * t) * (c * (1.0 + 3.0 * 0.044715 * x2))
    return g, dg


def _dot(a, b):
    return jnp.dot(a, b, preferred_element_type=F32)


def _dot_nt(a, b):
    return lax.dot_general(a, b, (((1,), (1,)), ((), ())), preferred_element_type=F32)


def _dot_tn(a, b):
    return lax.dot_general(a, b, (((0,), (0,)), ((), ())), preferred_element_type=F32)


def _rms_bwd(dn, h, g):
    r = lax.rsqrt(jnp.mean(h * h, axis=-1, keepdims=True) + EPS)
    w = dn * g
    dh = r * w - h * ((r * r * r) * jnp.mean(w * h, axis=-1, keepdims=True))
    return dh, r


def _place():
    x, y, c = lax.axis_index("x"), lax.axis_index("y"), lax.axis_index("c")
    chips = [(1 - x, y), (x, 1 - y), (1 - x, 1 - y)]
    return x, y, c, chips


def _remote(src, dst, send_sem, recv_sem, to):
    return pltpu.make_async_remote_copy(src_ref=src, dst_ref=dst, send_sem=send_sem, recv_sem=recv_sem,
                                        device_id=to, device_id_type=MESH)


class _Ride:
    def __init__(self, args, out_shape, n_sem, start, finish, mids=(), aliases=None):
        self.args, self.out_shape, self.n_sem = list(args), list(out_shape), n_sem
        self.start, self.mids, self.finish = start, list(mids), finish
        self.aliases = dict(aliases or {})


def _call(body, *, name, grid, in_specs, out_specs, out_shape, scratch_shapes=(), sem=None, rides=(), aliases=None):
    single = not isinstance(out_shape, (list, tuple))
    out_specs = [out_specs] if single else list(out_specs)
    out_shape = [out_shape] if single else list(out_shape)
    n_in, n_out, n_scr = len(in_specs), len(out_shape), len(scratch_shapes)
    r_in = [len(r.args) for r in rides]
    r_out = [len(r.out_shape) for r in rides]
    any_spec = pl.BlockSpec(memory_space=pl.ANY)
    aliases, off_i, off_o = dict(aliases or {}), n_in, n_out
    for r in rides:
        for i, o in r.aliases.items():
            aliases[off_i + i] = off_o + o
        off_i += len(r.args)
        off_o += len(r.out_shape)
    steps = math.prod(grid)

    def wrapped(*refs):
        p = 0
        ins = refs[p:p + n_in]; p += n_in
        rins = refs[p:p + sum(r_in)]; p += sum(r_in)
        outs = refs[p:p + n_out]; p += n_out
        routs = refs[p:p + sum(r_out)]; p += sum(r_out)
        scr = refs[p:p + n_scr]; p += n_scr
        sems = refs[p:]
        parts, pi, po = [], 0, 0
        for k, r in enumerate(rides):
            parts.append((rins[pi:pi + r_in[k]], routs[po:po + r_out[k]], sems[2 * k], sems[2 * k + 1]))
            pi += r_in[k]
            po += r_out[k]
        lin = 0
        for d in range(len(grid)):
            lin = lin * grid[d] + pl.program_id(d)
        if rides:
            @pl.when(lin == 0)
            def _():
                for r, part in zip(rides, parts):
                    r.start(*part)
        body(*ins, *outs, *scr)
        for r, part in zip(rides, parts):
            for frac, fn in r.mids:
                @pl.when(lin == min(steps - 1, int(frac * steps)))
                def _(fn=fn, part=part):
                    fn(*part)
        if rides:
            @pl.when(lin == steps - 1)
            def _():
                for r, part in zip(rides, parts):
                    r.finish(*part)

    scratch = list(scratch_shapes)
    for r in rides:
        scratch += [pltpu.SemaphoreType.DMA((r.n_sem,)), pltpu.SemaphoreType.DMA((r.n_sem,))]
    if rides:
        sem = ("arbitrary",) * len(grid)
    res = pl.pallas_call(
        wrapped, name=name, grid=grid,
        in_specs=list(in_specs) + [any_spec] * sum(r_in),
        out_specs=out_specs + [any_spec] * sum(r_out),
        out_shape=out_shape + [s for r in rides for s in r.out_shape],
        scratch_shapes=scratch, input_output_aliases=aliases,
        compiler_params=_params(sem),
    )

    def run(*args):
        got = res(*args, *[a for r in rides for a in r.args])
        mine = got[0] if single else list(got[:n_out])
        if not rides:
            return mine
        rest, out = list(got[n_out:]), []
        for k in range(len(rides)):
            out.append(rest[:r_out[k]])
            rest = rest[r_out[k]:]
        return mine, out

    return run


def _ride_gather(slot, s1=None, s2=None, s3=None, tail=None, chain=None, mid_frac=0.6, chain_fracs=(0.35, 0.7)):
    half = slot.shape[1] // 2

    def rows(part, c, which=None):
        k0, k1, n = part
        count, first = (k1 - k0) * (half // n), c * half + k0 * (half // n)
        return pl.ds(first, count) if which is None else pl.ds(first + which * (count // 2), count // 2)

    def ids():
        x, y, c, _ = _place()
        return x, y, c, 2 * x + y, 2 * (1 - x) + y, 2 * x + (1 - y), 2 * (1 - x) + (1 - y)

    def copy(full, chip, r, ss, rs, k, to):
        piece = full.at[chip, r, :]
        return _remote(piece, piece, ss.at[k], rs.at[k], to)

    def to_neighbours(full, ss, rs, part, base):
        x, y, c, me, _, _, _ = ids()
        return [copy(full, me, rows(part, c), ss, rs, base, (1 - x, y, c)),
                copy(full, me, rows(part, c), ss, rs, base + 1, (x, 1 - y, c))]

    def from_neighbours(full, ss, rs, part, base):
        x, y, c, _, cx, cy, _ = ids()
        return [copy(full, cx, rows(part, c), ss, rs, base, (x, y, c)), copy(full, cy, rows(part, c), ss, rs, base + 1, (x, y, c))]

    def onward(full, ss, rs, part, base):
        x, y, c, _, cx, cy, _ = ids()
        return [copy(full, cx, rows(part, c, 0), ss, rs, base, (x, 1 - y, c)),
                copy(full, cy, rows(part, c, 1), ss, rs, base + 1, (1 - x, y, c))]

    def from_onward(full, ss, rs, part, base):
        x, y, c, _, _, _, cd = ids()
        return [copy(full, cd, rows(part, c, 0), ss, rs, base, (x, y, c)), copy(full, cd, rows(part, c, 1), ss, rs, base + 1, (x, y, c))]

    def to_sibling(full, ss, rs, part, base, diagonal):
        x, y, c, _, cx, cy, cd = ids()
        return [copy(full, chip, rows(part, c), ss, rs, base + j, (x, y, 1 - c))
                for j, chip in enumerate([cd] if diagonal else [cx, cy])]

    def from_sibling(full, ss, rs, part, base, diagonal):
        x, y, c, _, cx, cy, cd = ids()
        return [copy(full, chip, rows(part, 1 - c), ss, rs, base + j, (x, y, c))
                for j, chip in enumerate([cd] if diagonal else [cx, cy])]

    def start(ins, outs, ss, rs):
        full, cps = outs[0], []
        for part, base in ((s1, 0), (chain, 12)):
            if part is not None:
                cps += to_neighbours(full, ss, rs, part, base)
        for part, b_ici, b_sib in ((s2, 2, 4), (tail, 7, 9)):
            if part is not None:
                cps += onward(full, ss, rs, part, b_ici) + to_sibling(full, ss, rs, part, b_sib, False)
        if s3 is not None:
            cps += to_sibling(full, ss, rs, s3, 6, True)
        for cp in cps:
            cp.start()

    def second(part, b_in, b_ici, b_sib):
        def fn(ins, outs, ss, rs):
            for cp in from_neighbours(outs[0], ss, rs, part, b_in):
                cp.wait_recv()
            for cp in onward(outs[0], ss, rs, part, b_ici) + to_sibling(outs[0], ss, rs, part, b_sib, False):
                cp.start()
        return fn

    def third(part, b_ici, b_sib):
        def fn(ins, outs, ss, rs):
            for cp in from_onward(outs[0], ss, rs, part, b_ici):
                cp.wait_recv()
            for cp in to_sibling(outs[0], ss, rs, part, b_sib, True):
                cp.start()
        return fn

    mids = []
    if tail is not None:
        mids.append((mid_frac, third(tail, 7, 11)))
    if chain is not None:
        mids += [(chain_fracs[0], second(chain, 12, 14, 16)), (chain_fracs[1], third(chain, 14, 18))]

    def finish(ins, outs, ss, rs):
        full, got, sent = outs[0], [], []
        if s1 is not None:
            got += from_neighbours(full, ss, rs, s1, 0)
            sent += to_neighbours(full, ss, rs, s1, 0)
        if s2 is not None:
            got += from_onward(full, ss, rs, s2, 2) + from_sibling(full, ss, rs, s2, 4, False)
            sent += onward(full, ss, rs, s2, 2) + to_sibling(full, ss, rs, s2, 4, False)
        if s3 is not None:
            got += from_sibling(full, ss, rs, s3, 6, True)
            sent += to_sibling(full, ss, rs, s3, 6, True)
        if tail is not None:
            got += from_sibling(full, ss, rs, tail, 9, False) + from_sibling(full, ss, rs, tail, 11, True)
            sent += onward(full, ss, rs, tail, 7) + to_sibling(full, ss, rs, tail, 9, False) + to_sibling(full, ss, rs, tail, 11, True)
        if chain is not None:
            got += from_sibling(full, ss, rs, chain, 16, False) + from_sibling(full, ss, rs, chain, 18, True)
            sent += (to_neighbours(full, ss, rs, chain, 12) + onward(full, ss, rs, chain, 14)
                     + to_sibling(full, ss, rs, chain, 16, False) + to_sibling(full, ss, rs, chain, 18, True))
        for cp in got:
            cp.wait_recv()
        for cp in sent:
            cp.wait_send()

    return _Ride([slot], [jax.ShapeDtypeStruct(slot.shape, slot.dtype)], 19, start, finish, mids=mids, aliases={0: 0})


def _ride_scatter(q, land=None, part=(0, 1), to=(0, 1, 2)):
    k0, k1, n = part if len(part) == 3 else (part[0], part[0] + 1, part[1])
    rows_n = q.shape[1] // n
    rows = pl.ds(k0 * rows_n, (k1 - k0) * rows_n)

    def copies(ins, outs, ss, rs):
        x, y, c, chips = _place()
        return [_remote(ins[0].at[2 * chip[0] + chip[1], rows, :], outs[0].at[j, rows, :], ss.at[j], rs.at[j], (*chip, c))
                for j, chip in enumerate(chips) if j in to]

    def start(*a):
        for cp in copies(*a):
            cp.start()

    def finish(*a):
        for cp in copies(*a):
            cp.wait()

    shape = jax.ShapeDtypeStruct((3,) + q.shape[1:], q.dtype)
    if land is None:
        return _Ride([q], [shape], 3, start, finish)
    return _Ride([q, land], [shape], 3, start, finish, aliases={1: 0})


def _ride_to_sibling(a, halves=False, first=False, shards=None, land=None):
    s0, s1 = shards or (0, a.shape[0])

    def copy(ins, outs, ss, rs):
        x, y, c, _ = _place()
        if halves:
            src, dst = ins[0].at[s0:s1, 1 - c], outs[0].at[s0:s1]
        else:
            src, dst = (ins[0].at[0] if first else ins[0]), outs[0]
        return _remote(src, dst, ss.at[0], rs.at[0], (x, y, 1 - c))

    shape = (a.shape[0],) + a.shape[2:] if halves else (a.shape[1:] if first else a.shape)
    return _Ride([a] if land is None else [a, land], [jax.ShapeDtypeStruct(shape, a.dtype)], 1,
                 lambda *a_: copy(*a_).start(), lambda *a_: copy(*a_).wait(), aliases=None if land is None else {1: 0})


def _ride_rows_to_sibling(a, hr, shards, total):
    def copies(ins, outs, ss, rs):
        x, y, c, _ = _place()
        return [_remote(ins[0].at[pl.ds((2 * s + 1 - c) * hr, hr), :], outs[0].at[s], ss.at[s], rs.at[s], (x, y, 1 - c))
                for s in range(shards)]

    def start(*a_):
        for cp in copies(*a_):
            cp.start()

    def finish(*a_):
        for cp in copies(*a_):
            cp.wait()

    return _Ride([a], [jax.ShapeDtypeStruct((total, hr, a.shape[1]), a.dtype)], shards, start, finish)


def _ride_swap(h):
    def copy(ins, outs, ss, rs):
        x, y, c, _ = _place()
        return _remote(ins[0], outs[0], ss.at[0], rs.at[0], (x, y, 1 - c))

    return _Ride([h], [jax.ShapeDtypeStruct(h.shape, h.dtype)], 1,
                 lambda *a: copy(*a).start(), lambda *a: copy(*a).wait())


def _mesh_place(p):
    return (p // 4, (p // 2) % 2, p % 2)


def _ride_small_to_all(packed):
    def copies(ins, outs, ss, rs):
        x, y, c, _ = _place()
        me = 4 * x + 2 * y + c
        return [_remote(ins[0], outs[0].at[me], ss.at[k - 1], rs.at[k - 1], _mesh_place((me + k) % N_DEV))
                for k in range(1, N_DEV)]

    def own(ins, outs, ss, rs):
        x, y, c, _ = _place()
        return pltpu.make_async_copy(ins[0], outs[0].at[4 * x + 2 * y + c], ss.at[N_DEV - 1])

    def start(*a):
        own(*a).start()
        for cp in copies(*a):
            cp.start()

    def finish(ins, outs, ss, rs):
        x, y, c, _ = _place()
        me = 4 * x + 2 * y + c
        for k in range(1, N_DEV):
            _remote(ins[0], outs[0].at[(me + N_DEV - k) % N_DEV], ss.at[k - 1], rs.at[k - 1], (x, y, c)).wait_recv()
        for cp in copies(ins, outs, ss, rs):
            cp.wait_send()
        own(ins, outs, ss, rs).wait()

    return _Ride([packed], [jax.ShapeDtypeStruct((N_DEV,) + packed.shape, packed.dtype)], N_DEV, start, finish)


def _carrier(rides, *, name):
    _, outs = _call(lambda: None, name=name, grid=(1,), in_specs=[], out_specs=[], out_shape=[], rides=rides)()
    return outs


def _norm_bf16(a_ref, g_ref):
    xf = a_ref[...]
    r = lax.rsqrt(jnp.mean(xf * xf, axis=-1, keepdims=True) + EPS)
    return ((xf * r) * g_ref[...]).astype(BF16)


def _norm_matmul_wide(a, g, b, *, tm, tn, name, rides=()):
    T, K = a.shape
    N = b.shape[0]

    def body(a_ref, g_ref, b_ref, n_ref, o_ref):
        n = _norm_bf16(a_ref, g_ref)
        n_ref[...] = n
        o_ref[...] = _dot_nt(n, b_ref[...])

    return _call(
        body, name=name, grid=(N // tn, T // tm),
        in_specs=[pl.BlockSpec((tm, K), lambda j, i: (i, 0)), pl.BlockSpec((1, K), lambda j, i: (0, 0)),
                  pl.BlockSpec((tn, K), lambda j, i: (j, 0))],
        out_specs=[pl.BlockSpec((None, tm, K), lambda j, i: (j, i, 0)), pl.BlockSpec((tm, tn), lambda j, i: (i, j))],
        out_shape=[jax.ShapeDtypeStruct((N // tn, T, K), BF16), jax.ShapeDtypeStruct((T, N), F32)],
        sem=("arbitrary", "arbitrary"), rides=rides,
    )(a, g, b)


def _norm_matmul_sq(a, g, b, *, tm, tn, name, rides=()):
    T, K = a.shape
    per = b.shape[2] // tn
    N = b.shape[0] * b.shape[2]

    def body(a_ref, g_ref, b_ref, nt_ref, o_ref, z_ref, zt_ref, n_scr):
        @pl.when(pl.program_id(1) == 0)
        def _():
            n = _norm_bf16(a_ref, g_ref)
            n_scr[...] = n
            nt_ref[...] = n.T
        r = jnp.maximum(_dot(n_scr[...], b_ref[...]), 0.0)
        o_ref[...] = r.astype(BF16)
        z = (r * r).astype(BF16)
        z_ref[...] = z
        zt_ref[...] = z.T

    return _call(
        body, name=name, grid=(T // tm, N // tn),
        in_specs=[pl.BlockSpec((tm, K), lambda i, j: (i, 0)), pl.BlockSpec((1, K), lambda i, j: (0, 0)),
                  pl.BlockSpec((None, K, tn), lambda i, j: (j // per, 0, j % per))],
        out_specs=[pl.BlockSpec((K, tm), lambda i, j: (0, i)), pl.BlockSpec((tm, tn), lambda i, j: (i, j)),
                   pl.BlockSpec((tm, tn), lambda i, j: (i, j)), pl.BlockSpec((tn, tm), lambda i, j: (j, i))],
        out_shape=[jax.ShapeDtypeStruct((K, T), BF16), jax.ShapeDtypeStruct((T, N), BF16),
                   jax.ShapeDtypeStruct((T, N), BF16), jax.ShapeDtypeStruct((N, T), BF16)],
        scratch_shapes=[pltpu.VMEM((tm, K), BF16)],
        sem=("parallel", "arbitrary"), rides=rides,
    )(a, g, b)


def _grad_pair(at, at_sib, b, b_sib, *, cols_sharded, tmo, tk, name, shards=None, into=None, rides=()):
    S, _, hr, T = at.shape
    C = b.shape[-1] // N_CHIPS if cols_sharded else b.shape[-1]
    nk = T // tk

    def shard(s):
        if shards is None:
            return s
        x, y = lax.axis_index("x"), lax.axis_index("y")
        first, second = ((2 * (1 - x) + y, 2 * x + (1 - y)) if shards == "near" else (2 * (1 - x) + (1 - y), 2 * x + y))
        return jnp.where(s == 0, first, second)

    a_sel = (lambda s: 0) if cols_sharded else shard
    b_sel = shard if cols_sharded else (lambda s: 0)
    if b.ndim == 3:
        b_spec = pl.BlockSpec((None, tk, C), lambda s, i, k: (0, k, b_sel(s)))
    else:
        b_spec = pl.BlockSpec((tk, C), lambda s, i, k: (k, b_sel(s)))
    n_into = 0 if into is None else 1

    def body(a_ref, as_ref, b_ref, bs_ref, *rest):
        o_ref, ob_ref = rest[n_into:]
        k = pl.program_id(2)
        p = _dot(a_ref[...], b_ref[...]) + _dot(as_ref[...], bs_ref[...])

        @pl.when(k == 0)
        def _():
            o_ref[...] = p

        @pl.when(k > 0)
        def _():
            o_ref[...] += p

        @pl.when(k == nk - 1)
        def _():
            ob_ref[...] = o_ref[...].astype(BF16)

    out = pl.BlockSpec((None, tmo, C), lambda s, i, k: (shard(s), i, 0))
    held = [pl.BlockSpec(memory_space=pl.ANY)] * n_into
    return _call(
        body, name=name, grid=(N_CHIPS if shards is None else 2, hr // tmo, nk),
        in_specs=[pl.BlockSpec((None, None, tmo, tk), lambda s, i, k: (a_sel(s), lax.axis_index("c"), i, k)),
                  pl.BlockSpec((None, tmo, tk), lambda s, i, k: (a_sel(s), i, k)),
                  b_spec, pl.BlockSpec((tk, C), lambda s, i, k: (k, b_sel(s)))] + held,
        out_specs=[out, out],
        out_shape=[jax.ShapeDtypeStruct((N_CHIPS, hr, C), F32), jax.ShapeDtypeStruct((N_CHIPS, hr, C), BF16)],
        sem=("parallel", "parallel", "arbitrary"), rides=rides, aliases={4: 0} if into is not None else None,
    )(at, at_sib, b, b_sib, *([into] if into is not None else []))


def _grad_pair_merged(at, at_sib, b, b_sib, *, tk, name, rides=()):
    S, _, hr, T = at.shape
    C = b.shape[-1]
    nk = T // tk

    def body(a_ref, as_ref, b_ref, bs_ref, o_ref, ob_ref):
        k = pl.program_id(0)
        p = (_dot(a_ref[...].reshape(S * hr, tk), b_ref[...])
             + _dot(as_ref[...].reshape(S * hr, tk), bs_ref[...])).reshape(S, hr, C)

        @pl.when(k == 0)
        def _():
            o_ref[...] = p

        @pl.when(k > 0)
        def _():
            o_ref[...] += p

        @pl.when(k == nk - 1)
        def _():
            ob_ref[...] = o_ref[...].astype(BF16)

    out = pl.BlockSpec((S, hr, C), lambda k: (0, 0, 0))
    return _call(
        body, name=name, grid=(nk,),
        in_specs=[pl.BlockSpec((S, None, hr, tk), lambda k: (0, lax.axis_index("c"), 0, k)),
                  pl.BlockSpec((S, hr, tk), lambda k: (0, 0, k)),
                  pl.BlockSpec((tk, C), lambda k: (k, 0)), pl.BlockSpec((tk, C), lambda k: (k, 0))],
        out_specs=[out, out],
        out_shape=[jax.ShapeDtypeStruct((S, hr, C), F32), jax.ShapeDtypeStruct((S, hr, C), BF16)],
        sem=("arbitrary",), rides=rides,
    )(at, at_sib, b, b_sib)


def _matmul_parts(parts, b, *, tm, tn, name, rides=()):
    T = parts[0].shape[0]
    N = b.shape[1]
    offs = [sum(p.shape[1] for p in parts[:i]) for i in range(len(parts))]
    assert all(o % p.shape[1] == 0 for o, p in zip(offs, parts))

    def body(*refs):
        n = len(parts)
        acc = _dot(refs[0][...], refs[n][...])
        for i in range(1, n):
            acc = acc + _dot(refs[i][...], refs[n + i][...])
        refs[-1][...] = acc

    a_specs = [pl.BlockSpec((tm, p.shape[1]), lambda i, j: (i, 0)) for p in parts]
    b_specs = [pl.BlockSpec((p.shape[1], tn), lambda i, j, r=o // p.shape[1]: (r, j)) for o, p in zip(offs, parts)]
    return _call(
        body, name=name, grid=(T // tm, N // tn), in_specs=a_specs + b_specs,
        out_specs=pl.BlockSpec((tm, tn), lambda i, j: (i, j)), out_shape=jax.ShapeDtypeStruct((T, N), F32),
        sem=("parallel", "parallel"), rides=rides,
    )(*parts, *([b] * len(parts)))


def _to_bf16(v):
    return v.astype(BF16)


def _matmul_res(a, b, res, *, tm, tn, tk, prologue, name, rides=()):
    T, K = a.shape
    N = b.shape[1]

    def body(a_ref, b_ref, res_ref, o_ref):
        k = pl.program_id(2)
        p = _dot(prologue(a_ref[...]), b_ref[...])

        @pl.when(k == 0)
        def _():
            o_ref[...] = res_ref[...] + p

        @pl.when(k > 0)
        def _():
            o_ref[...] += p

    return _call(
        body, name=name, grid=(T // tm, N // tn, K // tk),
        in_specs=[pl.BlockSpec((tm, tk), lambda i, j, k: (i, k)), pl.BlockSpec((tk, tn), lambda i, j, k: (k, j)),
                  pl.BlockSpec((tm, tn), lambda i, j, k: (i, j))],
        out_specs=pl.BlockSpec((tm, tn), lambda i, j, k: (i, j)),
        out_shape=jax.ShapeDtypeStruct((T, N), F32),
        sem=("parallel", "parallel", "arbitrary"), rides=rides,
    )(a, b, res)


def _matmul_nt(a, b, *, tm, tn, tk, name, extra=None, epilogue=None, out_dtype=F32, rides=()):
    T, K = a.shape
    two = b.ndim == 3 and tk == 2 * b.shape[2]
    if two:
        N, ks = b.shape[1], b.shape[2]
        b_specs = [pl.BlockSpec((None, tn, ks), lambda i, j, k: (2 * k, j, 0)),
                   pl.BlockSpec((None, tn, ks), lambda i, j, k: (2 * k + 1, j, 0))]
    elif b.ndim == 3:
        per = b.shape[2] // tk
        N = b.shape[1]
        b_specs = [pl.BlockSpec((None, tn, tk), lambda i, j, k: (k // per, j, k % per))]
    else:
        N = b.shape[0]
        b_specs = [pl.BlockSpec((tn, tk), lambda i, j, k: (j, k))]
    nb = len(b_specs)
    nk = K // tk
    assert out_dtype == F32 or nk == 1
    in_specs = [pl.BlockSpec((tm, tk), lambda i, j, k: (i, k))] + b_specs
    args = [a] + [b] * nb
    if extra is not None:
        in_specs.append(pl.BlockSpec((tm, tn), lambda i, j, k: (i, j)))
        args.append(extra)

    def body(*refs):
        a_ref, b_ref = refs[0], refs[1]
        o_ref = refs[-1]
        if two:
            p = (_dot_nt(a_ref[:, :tk // 2].astype(BF16), refs[1][...])
                 + _dot_nt(a_ref[:, tk // 2:].astype(BF16), refs[2][...]))
        else:
            p = _dot_nt(a_ref[...].astype(BF16), b_ref[...])
        if nk == 1:
            if epilogue is not None:
                p = epilogue(p, refs[1 + nb][...])
            o_ref[...] = p.astype(out_dtype)
        else:
            k = pl.program_id(2)

            @pl.when(k == 0)
            def _():
                o_ref[...] = p

            @pl.when(k > 0)
            def _():
                o_ref[...] += p

    return _call(
        body, name=name, grid=(T // tm, N // tn, nk),
        in_specs=in_specs,
        out_specs=pl.BlockSpec((tm, tn), lambda i, j, k: (i, j)),
        out_shape=jax.ShapeDtypeStruct((T, N), out_dtype),
        sem=("parallel", "parallel", "arbitrary"), rides=rides,
    )(*args)


def _loss_bwd(h2, tgt, g, *, tm):
    T, D = h2.shape

    def body(h_ref, t_ref, g_ref, dh_ref, dhb_ref, dg_ref, loss_ref):
        @pl.when(pl.program_id(0) == 0)
        def _():
            dg_ref[...] = jnp.zeros_like(dg_ref)
            loss_ref[...] = jnp.zeros_like(loss_ref)
        h = h_ref[...]
        gg = g_ref[...]
        r = lax.rsqrt(jnp.mean(h * h, axis=-1, keepdims=True) + EPS)
        hn = h * r
        err = hn * gg - t_ref[...]
        loss_ref[...] += 0.5 * jnp.sum(jnp.mean(err * err, axis=-1, keepdims=True), axis=0, keepdims=True)
        dy = err * (1.0 / D)
        dg_ref[...] += jnp.sum(dy * hn, axis=0, keepdims=True)
        w = dy * gg
        dh = r * w - h * ((r * r * r) * jnp.mean(w * h, axis=-1, keepdims=True))
        dh_ref[...] = dh
        dhb_ref[...] = dh.astype(BF16)

    tile = pl.BlockSpec((tm, D), lambda i: (i, 0))
    return pl.pallas_call(
        body, name="loss_bwd", grid=(T // tm,),
        in_specs=[tile, tile, pl.BlockSpec((1, D), lambda i: (0, 0))],
        out_specs=[tile, tile, pl.BlockSpec((1, D), lambda i: (0, 0)), pl.BlockSpec((1, 1), lambda i: (0, 0))],
        out_shape=[jax.ShapeDtypeStruct((T, D), F32), jax.ShapeDtypeStruct((T, D), BF16),
                   jax.ShapeDtypeStruct((1, D), F32), jax.ShapeDtypeStruct((1, 1), F32)],
        compiler_params=_params(("arbitrary",)),
    )(h2, tgt, g)


def _rms_bwd_res(dn, h, g, dres, *, tm, name, bf16_copy=True, rides=()):
    T, D = h.shape

    def body(dn_ref, h_ref, g_ref, dres_ref, dh_ref, *rest):
        dg_ref = rest[-1]

        @pl.when(pl.program_id(0) == 0)
        def _():
            dg_ref[...] = jnp.zeros_like(dg_ref)
        h_ = h_ref[...]
        dn_ = dn_ref[...]
        dh, r = _rms_bwd(dn_, h_, g_ref[...])
        dg_ref[...] += jnp.sum(dn_ * (h_ * r), axis=0, keepdims=True)
        dh = dres_ref[...] + dh
        dh_ref[...] = dh
        if bf16_copy:
            rest[0][...] = dh.astype(BF16)

    tile = pl.BlockSpec((tm, D), lambda i: (i, 0))
    row = pl.BlockSpec((1, D), lambda i: (0, 0))
    copy_spec = [tile] if bf16_copy else []
    copy_shape = [jax.ShapeDtypeStruct((T, D), BF16)] if bf16_copy else []
    return _call(
        body, name=name, grid=(T // tm,),
        in_specs=[tile, tile, row, tile], out_specs=[tile] + copy_spec + [row],
        out_shape=[jax.ShapeDtypeStruct((T, D), F32)] + copy_shape + [jax.ShapeDtypeStruct((1, D), F32)],
        sem=("arbitrary",), rides=rides,
    )(dn, h, g, dres)


def _rel_distance():
    i = lax.broadcasted_iota(jnp.int32, (CHUNK, 2 * CHUNK), 0)
    j = lax.broadcasted_iota(jnp.int32, (CHUNK, 2 * CHUNK), 1)
    return i + CHUNK - j


def _bias_build(table):
    def body(tab_ref, o_ref):
        rel = _rel_distance()
        j = lax.broadcasted_iota(jnp.int32, (CHUNK, 2 * CHUNK), 1)
        band = (rel >= 0) & (rel < CHUNK)
        ge = [rel >= t for t in BUCKET_THR]
        for h in range(B_HEADS):
            cur = jnp.full((CHUNK, 2 * CHUNK), tab_ref[0, h], F32)
            for b in range(1, N_BUCKETS):
                cur = jnp.where(ge[b - 1], tab_ref[b, h], cur)
            o_ref[0, h] = jnp.where(band & (j >= CHUNK), cur, NEG)
            o_ref[1, h] = jnp.where(band, cur, NEG)

    return pl.pallas_call(
        body, name="bias_build",
        in_specs=[pl.BlockSpec(memory_space=pltpu.SMEM)],
        out_specs=pl.BlockSpec(memory_space=pltpu.VMEM),
        out_shape=jax.ShapeDtypeStruct((2, B_HEADS, CHUNK, 2 * CHUNK), F32),
    )(table)


def _bias_grad(dbias):
    def body(db_ref, o_ref, acc_ref):
        rel = _rel_distance()
        lo = [0] + BUCKET_THR
        hi = BUCKET_THR + [CHUNK]
        for b in range(N_BUCKETS):
            m = (rel >= lo[b]) & (rel < hi[b])
            for h in range(B_HEADS):
                row = b * B_HEADS + h
                acc_ref[row:row + 1, :] = jnp.sum(jnp.where(m, db_ref[h], 0.0), axis=0, keepdims=True)
        o_ref[...] = jnp.sum(acc_ref[...], axis=1, keepdims=True)

    return pl.pallas_call(
        body, name="bias_grad",
        in_specs=[pl.BlockSpec(memory_space=pltpu.VMEM)],
        out_specs=pl.BlockSpec(memory_space=pltpu.VMEM),
        out_shape=jax.ShapeDtypeStruct((N_BUCKETS * B_HEADS, 1), F32),
        scratch_shapes=[pltpu.VMEM((N_BUCKETS * B_HEADS, 2 * CHUNK), F32)],
    )(dbias)


def _causal_mask():
    t = lax.broadcasted_iota(jnp.int32, (CHUNK, CHUNK), 0)
    s = lax.broadcasted_iota(jnp.int32, (CHUNK, CHUNK), 1)
    return s <= t


def _gate_forward(u, v, lg, lb, wc, bs):
    ug = _gelu(u)
    vg = _gelu(v)
    mu = jnp.mean(vg, axis=-1, keepdims=True)
    xc = vg - mu
    rstd = lax.rsqrt(jnp.mean(xc * xc, axis=-1, keepdims=True) + EPS)
    xhat = xc * rstd
    vl = (xhat * lg + lb).astype(BF16)
    mixed = _dot(wc, vl) + bs
    return ug, xhat, rstd, vl, mixed


def _softmax_scores(qk, bias, sink):
    s = qk + bias
    m = jnp.maximum(jnp.max(s, axis=-1, keepdims=True), sink)
    p = jnp.exp(s - m)
    e_sink = jnp.exp(sink - m)
    inv = 1.0 / (jnp.sum(p, axis=-1, keepdims=True) + e_sink)
    return p * inv, e_sink * inv


PAIRS = Q_PER_KV // 2


def _head(g, pr, e):
    return g * Q_PER_KV + 2 * pr + e


def _stack_pairs(ref, g, col0=0):
    w = 2 * HEAD_DIM
    return jnp.concatenate([ref[:, col0 + (g * PAIRS + pr) * w:col0 + (g * PAIRS + pr + 1) * w] for pr in range(PAIRS)],
                           axis=0)


def _low_lanes():
    return lax.broadcasted_iota(jnp.int32, (2 * CHUNK, 2 * HEAD_DIM), 1) < HEAD_DIM


def _band_operands(kv_prev, kv_cur):
    band = jnp.concatenate([kv_prev, kv_cur], axis=0)
    low = _low_lanes()
    ops = []
    for cat in (band[:, :KV_WIDTH], band[:, KV_WIDTH:]):
        rol = pltpu.roll(cat, HEAD_DIM, 1)
        ops.append([[jnp.where(low if e == 0 else ~low, cat if g == e else rol, 0.0).astype(BF16) for e in range(2)]
                    for g in range(2)])
    return ops


def _mixer_fwd(proj, lg, lb, wsp, bs_col, sinks, bias, ga, gb, rides=()):
    T = proj.shape[0]
    nb = T // CHUNK

    def body(u_ref, v_ref, q_ref, kvc_ref, kvp_ref, lg_ref, lb_ref, w_ref, bs_ref, sink_ref, bias_ref,
             ga_ref, gb_ref, mixed_ref, mixed_t_ref, ab_ref):
        causal = _causal_mask()
        ssq = jnp.zeros((CHUNK, 1), F32)
        for g in range(A_GROUPS):
            cols = slice(g * CHUNK, (g + 1) * CHUNK)
            wc = jnp.where(causal, w_ref[g], 0.0).astype(BF16)
            ug, _, _, _, mixed = _gate_forward(u_ref[:, cols], v_ref[:, cols], lg_ref[g:g + 1, :], lb_ref[g:g + 1, :],
                                               wc, bs_ref[g])
            a = ug * mixed
            ab_ref[:, cols] = a
            ssq = ssq + jnp.sum(a * a, axis=-1, keepdims=True)
        ra = lax.rsqrt(ssq * (1.0 / A_WIDTH) + EPS)
        mixed_ref[:, :A_WIDTH] = ((ab_ref[:, :A_WIDTH] * ra) * ga_ref[...]).astype(BF16)

        kops, vops = _band_operands(kvp_ref[...], kvc_ref[...])
        ssq = jnp.zeros((CHUNK, 1), F32)
        for g in range(B_HEADS // Q_PER_KV):
            qst = (_stack_pairs(q_ref, g) * SCALE).astype(BF16)
            o_st = jnp.zeros((PAIRS * CHUNK, 2 * HEAD_DIM), F32)
            for e in range(2):
                s_all = _dot_nt(qst, kops[g][e])
                ps = []
                for pr in range(PAIRS):
                    h = _head(g, pr, e)
                    p, _ = _softmax_scores(s_all[pr * CHUNK:(pr + 1) * CHUNK], bias_ref[h], sink_ref[0, h])
                    ps.append(p.astype(BF16))
                o_st = o_st + _dot(jnp.concatenate(ps, axis=0), vops[g][e])
            for pr in range(PAIRS):
                o = o_st[pr * CHUNK:(pr + 1) * CHUNK]
                c0 = A_WIDTH + (g * PAIRS + pr) * 2 * HEAD_DIM
                ab_ref[:, c0:c0 + 2 * HEAD_DIM] = o
                ssq = ssq + jnp.sum(o * o, axis=-1, keepdims=True)
        rb = lax.rsqrt(ssq * (1.0 / B_WIDTH) + EPS)
        mixed_ref[:, A_WIDTH:] = ((ab_ref[:, A_WIDTH:] * rb) * gb_ref[...]).astype(BF16)
        mixed_t_ref[...] = mixed_ref[...].T

    full = lambda *shape: pl.BlockSpec(shape, lambda n: (0,) * len(shape))
    return _call(
        body, name="mixer_fwd", grid=(nb,),
        in_specs=[pl.BlockSpec((CHUNK, A_WIDTH), lambda n: (n, 0)),
                  pl.BlockSpec((CHUNK, A_WIDTH), lambda n: (n, 1)),
                  pl.BlockSpec((CHUNK, B_WIDTH), lambda n: (n, 2)),
                  pl.BlockSpec((CHUNK, 2 * KV_WIDTH), lambda n: (n, 12)),
                  pl.BlockSpec((CHUNK, 2 * KV_WIDTH), lambda n: (jnp.maximum(n - 1, 0), 12)),
                  full(A_GROUPS, CHUNK), full(A_GROUPS, CHUNK), full(A_GROUPS, CHUNK, CHUNK), full(A_GROUPS, CHUNK, 1),
                  pl.BlockSpec(memory_space=pltpu.SMEM),
                  pl.BlockSpec((None, B_HEADS, CHUNK, 2 * CHUNK), lambda n: (jnp.minimum(n, 1), 0, 0, 0)),
                  full(1, A_WIDTH), full(1, B_WIDTH)],
        out_specs=[pl.BlockSpec((CHUNK, D_MODEL), lambda n: (n, 0)), pl.BlockSpec((D_MODEL, CHUNK), lambda n: (0, n)),
                   pl.BlockSpec((CHUNK, D_MODEL), lambda n: (n, 0))],
        out_shape=[jax.ShapeDtypeStruct((T, D_MODEL), BF16), jax.ShapeDtypeStruct((D_MODEL, T), BF16),
                   jax.ShapeDtypeStruct((T, D_MODEL), F32)],
        sem=("parallel",), rides=rides,
    )(proj, proj, proj, proj, proj, lg, lb, wsp, bs_col, sinks, bias, ga, gb)


def _gmlp_bwd(proj, ab, dmixed, ga, lg, lb, wsp, bs_col, rides=()):
    T = proj.shape[0]
    nb = T // CHUNK

    def body(u_ref, v_ref, a_ref, dna_ref, ga_ref, lg_ref, lb_ref, w_ref, bs_ref,
             dp_ref, dpt_ref, dga_ref, dw_ref, dbs_ref, dlg_ref, dlb_ref):
        @pl.when(pl.program_id(0) == 0)
        def _():
            for r in (dga_ref, dw_ref, dbs_ref, dlg_ref, dlb_ref):
                r[...] = jnp.zeros_like(r)
        causal = _causal_mask()
        a_all = a_ref[...]
        dna = dna_ref[...]
        da_all, ra = _rms_bwd(dna, a_all, ga_ref[...])
        dga_ref[...] += jnp.sum(dna * (a_all * ra), axis=0, keepdims=True)
        for g in range(A_GROUPS):
            cols = slice(g * CHUNK, (g + 1) * CHUNK)
            wc = jnp.where(causal, w_ref[g], 0.0).astype(BF16)
            lgg = lg_ref[g:g + 1, :]
            u = u_ref[:, cols]
            v = v_ref[:, cols]
            ug, xhat, rstd, vl, mixed = _gate_forward(u, v, lgg, lb_ref[g:g + 1, :], wc, bs_ref[g])
            da = da_all[:, cols]
            dug = da * mixed
            dmg = da * ug
            dmg_b = dmg.astype(BF16)
            dbs_ref[g] += jnp.sum(dmg, axis=-1, keepdims=True)
            dw_ref[g] += jnp.where(causal, _dot_nt(dmg_b, vl), 0.0)
            dvl = _dot_tn(wc, dmg_b)
            dlg_ref[g:g + 1, :] += jnp.sum(dvl * xhat, axis=0, keepdims=True)
            dlb_ref[g:g + 1, :] += jnp.sum(dvl, axis=0, keepdims=True)
            dxh = dvl * lgg
            dvg = rstd * (dxh - jnp.mean(dxh, axis=-1, keepdims=True)
                          - xhat * jnp.mean(dxh * xhat, axis=-1, keepdims=True))
            _, gu = _gelu_and_grad(u)
            _, gv = _gelu_and_grad(v)
            dp_ref[:, cols] = (dug * gu).astype(BF16)
            dp_ref[:, A_WIDTH + g * CHUNK:A_WIDTH + (g + 1) * CHUNK] = (dvg * gv).astype(BF16)
        dpt_ref[...] = dp_ref[...].T

    full = lambda *shape: pl.BlockSpec(shape, lambda n: (0,) * len(shape))
    return _call(
        body, name="gmlp_bwd", grid=(nb,),
        in_specs=[pl.BlockSpec((CHUNK, A_WIDTH), lambda n: (n, 0)),
                  pl.BlockSpec((CHUNK, A_WIDTH), lambda n: (n, 1)),
                  pl.BlockSpec((CHUNK, A_WIDTH), lambda n: (n, 0)),
                  pl.BlockSpec((CHUNK, A_WIDTH), lambda n: (n, 0)),
                  full(1, A_WIDTH), full(A_GROUPS, CHUNK), full(A_GROUPS, CHUNK), full(A_GROUPS, CHUNK, CHUNK),
                  full(A_GROUPS, CHUNK, 1)],
        out_specs=[pl.BlockSpec((CHUNK, 2 * A_WIDTH), lambda n: (n, 0)), pl.BlockSpec((2 * A_WIDTH, CHUNK), lambda n: (0, n)),
                   full(1, A_WIDTH), full(A_GROUPS, CHUNK, CHUNK), full(A_GROUPS, CHUNK, 1),
                   full(A_GROUPS, CHUNK), full(A_GROUPS, CHUNK)],
        out_shape=[jax.ShapeDtypeStruct((T, 2 * A_WIDTH), BF16), jax.ShapeDtypeStruct((2 * A_WIDTH, T), BF16),
                   jax.ShapeDtypeStruct((1, A_WIDTH), F32), jax.ShapeDtypeStruct((A_GROUPS, CHUNK, CHUNK), F32),
                   jax.ShapeDtypeStruct((A_GROUPS, CHUNK, 1), F32), jax.ShapeDtypeStruct((A_GROUPS, CHUNK), F32),
                   jax.ShapeDtypeStruct((A_GROUPS, CHUNK), F32)],
        sem=("arbitrary",), rides=rides,
    )(proj, proj, ab, dmixed, ga, lg, lb, wsp, bs_col)


def _attn_bwd(proj, ab, dmixed, gb, sinks, bias, rides=()):
    T = proj.shape[0]
    nb = T // CHUNK
    qn = lambda n: jnp.minimum(n, nb - 1)

    def body(q_ref, kvc_ref, kvp_ref, o_ref, dnb_ref, gb_ref, sink_ref, bias_ref,
             dq_ref, dkv_ref, dqt_ref, dkvt_ref, dgb_ref, dsink_ref, dbias_ref, carry_ref, sacc_ref):
        n = pl.program_id(0)

        @pl.when(n == 0)
        def _():
            carry_ref[...] = jnp.zeros_like(carry_ref)
            sacc_ref[...] = jnp.zeros_like(sacc_ref)
            dgb_ref[...] = jnp.zeros_like(dgb_ref)
            dbias_ref[...] = jnp.zeros_like(dbias_ref)

        @pl.when(n < nb)
        def _():
            o_all = o_ref[...]
            dnb = dnb_ref[...]
            do_all, rb = _rms_bwd(dnb, o_all, gb_ref[...])
            dgb_ref[...] += jnp.sum(dnb * (o_all * rb), axis=0, keepdims=True)
            kops, vops = _band_operands(kvp_ref[...], kvc_ref[...])
            low = _low_lanes()
            halves = []
            for g in range(B_HEADS // Q_PER_KV):
                qst = (_stack_pairs(q_ref, g) * SCALE).astype(BF16)
                dost = _stack_pairs(do_all, g).astype(BF16)
                dq_st = jnp.zeros((PAIRS * CHUNK, 2 * HEAD_DIM), F32)
                dk_e, dv_e = [], []
                for e in range(2):
                    s_all = _dot_nt(qst, kops[g][e])
                    dp_all = _dot_nt(dost, vops[g][e])
                    ps, dsrs = [], []
                    for pr in range(PAIRS):
                        h = _head(g, pr, e)
                        rows = slice(pr * CHUNK, (pr + 1) * CHUNK)
                        p, p_sink = _softmax_scores(s_all[rows], bias_ref[h], sink_ref[0, h])
                        dp = dp_all[rows]
                        delta = jnp.sum(p * dp, axis=-1, keepdims=True)
                        ds = p * (dp - delta)
                        sacc_ref[:, h:h + 1] += -(p_sink * delta)
                        dbias_ref[h] += ds
                        ps.append(p.astype(BF16))
                        dsrs.append(ds.astype(BF16))
                    dsr_all = jnp.concatenate(dsrs, axis=0)
                    dq_st = dq_st + _dot(dsr_all, kops[g][e])
                    dk_e.append(_dot_tn(dsr_all, qst))
                    dv_e.append(_dot_tn(jnp.concatenate(ps, axis=0), dost))
                for pr in range(PAIRS):
                    c0 = (g * PAIRS + pr) * 2 * HEAD_DIM
                    dq_ref[:, c0:c0 + 2 * HEAD_DIM] = (dq_st[pr * CHUNK:(pr + 1) * CHUNK] * SCALE).astype(BF16)
                halves.append((dk_e, dv_e))
            tiles = []
            for t in range(2):
                g0, g1 = halves[0][t], halves[1][t]
                tiles.append(jnp.where(low, g0[0] + pltpu.roll(g0[1], HEAD_DIM, 1), pltpu.roll(g1[0], HEAD_DIM, 1) + g1[1]))
            dband = jnp.concatenate(tiles, axis=1)
            dkv = (carry_ref[...] + dband[:CHUNK]).astype(BF16)
            dkv_ref[...] = dkv
            dkvt_ref[...] = dkv.T
            dqt_ref[...] = dq_ref[...].T
            carry_ref[...] = dband[CHUNK:]

        @pl.when(n == nb)
        def _():
            dkv = carry_ref[...].astype(BF16)
            dkv_ref[...] = dkv
            dkvt_ref[...] = dkv.T
            dsink_ref[...] = jnp.sum(sacc_ref[...], axis=0, keepdims=True)

    full = lambda *shape: pl.BlockSpec(shape, lambda n: (0,) * len(shape))
    return _call(
        body, name="attn_bwd", grid=(nb + 1,),
        in_specs=[pl.BlockSpec((CHUNK, B_WIDTH), lambda n: (qn(n), 2)),
                  pl.BlockSpec((CHUNK, 2 * KV_WIDTH), lambda n: (qn(n), 12)),
                  pl.BlockSpec((CHUNK, 2 * KV_WIDTH), lambda n: (jnp.maximum(qn(n) - 1, 0), 12)),
                  pl.BlockSpec((CHUNK, B_WIDTH), lambda n: (qn(n), 1)),
                  pl.BlockSpec((CHUNK, B_WIDTH), lambda n: (qn(n), 1)),
                  full(1, B_WIDTH), pl.BlockSpec(memory_space=pltpu.SMEM),
                  pl.BlockSpec((None, B_HEADS, CHUNK, 2 * CHUNK), lambda n: (jnp.minimum(n, 1), 0, 0, 0))],
        out_specs=[pl.BlockSpec((CHUNK, B_WIDTH), lambda n: (qn(n), 0)),
                   pl.BlockSpec((CHUNK, 2 * KV_WIDTH), lambda n: (jnp.maximum(n - 1, 0), 0)),
                   pl.BlockSpec((B_WIDTH, CHUNK), lambda n: (0, qn(n))),
                   pl.BlockSpec((2 * KV_WIDTH, CHUNK), lambda n: (0, jnp.maximum(n - 1, 0))),
                   full(1, B_WIDTH), full(1, B_HEADS), full(B_HEADS, CHUNK, 2 * CHUNK)],
        out_shape=[jax.ShapeDtypeStruct((T, B_WIDTH), BF16), jax.ShapeDtypeStruct((T, 2 * KV_WIDTH), BF16),
                   jax.ShapeDtypeStruct((B_WIDTH, T), BF16), jax.ShapeDtypeStruct((2 * KV_WIDTH, T), BF16),
                   jax.ShapeDtypeStruct((1, B_WIDTH), F32), jax.ShapeDtypeStruct((1, B_HEADS), F32),
                   jax.ShapeDtypeStruct((B_HEADS, CHUNK, 2 * CHUNK), F32)],
        scratch_shapes=[pltpu.VMEM((CHUNK, 2 * KV_WIDTH), F32), pltpu.VMEM((CHUNK, B_HEADS), F32)],
        sem=("arbitrary",), rides=rides,
    )(proj, proj, proj, ab, dmixed, gb, sinks, bias)


def _sq_relu_grad(acc, r):
    return acc * (2.0 * r.astype(F32))


def _chip_index():
    return (2 * lax.axis_index("x") + lax.axis_index("y")).astype(jnp.int32).reshape(1)


def _cast_into_slot(w, *, tm, name):
    _, R, C = w.shape

    def body(me_ref, w_ref, o_ref):
        del me_ref
        o_ref[...] = w_ref[...].astype(BF16)

    return pl.pallas_call(
        body, name=name,
        grid_spec=pltpu.PrefetchScalarGridSpec(
            num_scalar_prefetch=1, grid=(R // tm,),
            in_specs=[pl.BlockSpec((None, tm, C), lambda i, me: (0, i, 0))],
            out_specs=pl.BlockSpec((None, tm, C), lambda i, me: (me[0], i, 0))),
        out_shape=jax.ShapeDtypeStruct((N_CHIPS, R, C), BF16), compiler_params=_params(("parallel",)),
    )(_chip_index(), w)


def _cast_into_slots_carrying(ws, *, steps, name, rides):
    n = len(ws)

    def body(*refs):
        for w_ref, o_ref in zip(refs[:n], refs[n:]):
            o_ref[...] = w_ref[...].astype(BF16)

    me = lambda: 2 * lax.axis_index("x") + lax.axis_index("y")
    return _call(
        body, name=name, grid=(steps,),
        in_specs=[pl.BlockSpec((None, w.shape[1] // steps, w.shape[2]), lambda i: (0, i, 0)) for w in ws],
        out_specs=[pl.BlockSpec((None, w.shape[1] // steps, w.shape[2]), lambda i: (me(), i, 0)) for w in ws],
        out_shape=[jax.ShapeDtypeStruct((N_CHIPS,) + w.shape[1:], BF16) for w in ws], sem=("arbitrary",), rides=rides,
    )(*ws)


def _owner_total(gh, others, *, tm, name):
    _, hr, C = gh.shape

    def body(me_ref, g_ref, o_ref_in, out_ref):
        del me_ref
        acc = g_ref[...]
        for j in range(3):
            acc = acc + o_ref_in[j].astype(F32)
        out_ref[...] = acc

    return pl.pallas_call(
        body, name=name,
        grid_spec=pltpu.PrefetchScalarGridSpec(
            num_scalar_prefetch=1, grid=(hr // tm,),
            in_specs=[pl.BlockSpec((None, tm, C), lambda i, me: (me[0], i, 0)),
                      pl.BlockSpec((3, tm, C), lambda i, me: (0, i, 0))],
            out_specs=pl.BlockSpec((tm, C), lambda i, me: (i, 0))),
        out_shape=jax.ShapeDtypeStruct((hr, C), F32),
        compiler_params=_params(("parallel",)),
    )(_chip_index(), gh, others)


def _adamw_math(w, g, m, v):
    m = ADAM_B1 * m + (1.0 - ADAM_B1) * g
    v = ADAM_B2 * v + (1.0 - ADAM_B2) * (g * g)
    m_hat = m / (1.0 - ADAM_B1 ** ADAM_STEP)
    v_hat = v / (1.0 - ADAM_B2 ** ADAM_STEP)
    delta = -ADAM_LR * (m_hat / (jnp.sqrt(v_hat) + ADAM_EPS) + ADAM_WD * w)
    return delta, m, v


def _adamw_halves(w, own, got, m, v, *, tm, name, rides=()):
    _, R, C = w.shape
    nt = (R // 2) // tm

    def body(w_ref, own_ref, got_ref, m_ref, v_ref, g_ref, d_ref, nm_ref, nv_ref):
        g = jnp.where(pl.program_id(0) == lax.axis_index("c"), own_ref[...], got_ref[...])
        g_ref[...] = g
        d_ref[...], nm_ref[...], nv_ref[...] = _adamw_math(w_ref[...], g, m_ref[...], v_ref[...])

    whole = pl.BlockSpec((None, tm, C), lambda h, i: (0, h * nt + i, 0))
    half = pl.BlockSpec((tm, C), lambda h, i: (i, 0))
    return _call(
        body, name=name, grid=(2, nt), in_specs=[whole, half, half, whole, whole], out_specs=[whole] * 4,
        out_shape=[jax.ShapeDtypeStruct((1, R, C), F32)] * 4, sem=("parallel", "parallel"), rides=rides,
    )(w, own, got, m, v)


def _adamw_small(w, slots, m, v, *, name):
    def body(w_ref, slots_ref, m_ref, v_ref, g_ref, d_ref, nm_ref, nv_ref):
        g = slots_ref[0]
        for d in range(1, N_DEV):
            g = g + slots_ref[d]
        g_ref[...] = g
        d_ref[...], nm_ref[...], nv_ref[...] = _adamw_math(w_ref[...], g, m_ref[...], v_ref[...])

    vmem = pl.BlockSpec(memory_space=pltpu.VMEM)
    return pl.pallas_call(
        body, name=name, in_specs=[vmem] * 4, out_specs=[vmem] * 4,
        out_shape=[jax.ShapeDtypeStruct(w.shape, F32)] * 4, compiler_params=_params(),
    )(w, slots, m, v)


SMALL = ["rel_bias_table", "mix_norm_g", "gate_norm_g", "gate_norm_b", "w_spatial", "b_spatial", "attn_sinks",
         "out_norm_a_g", "out_norm_b_g", "ffn_norm_g", "final_norm_g"]
SMALL_A = ["gate_norm_g", "gate_norm_b", "w_spatial", "b_spatial", "out_norm_a_g"]
SMALL_B = ["rel_bias_table", "mix_norm_g", "attn_sinks", "out_norm_b_g", "ffn_norm_g", "final_norm_g"]
LARGE = ["w_in", "w_out", "w_up", "w_down"]
ROW_TILE = {"w_in": 208, "w_out": 256, "w_up": 256, "w_down": 256}
WEIGHTS = ["rel_bias_table", "mix_norm_g", "w_in", "gate_norm_g", "gate_norm_b", "w_spatial", "b_spatial", "attn_sinks",
           "out_norm_a_g", "out_norm_b_g", "w_out", "ffn_norm_g", "w_up", "w_down", "final_norm_g"]
PACK_UNIT = 8 * 128


def _pack(parts):
    rows = []
    for p in parts:
        flat = p.reshape(-1)
        pad = (-flat.shape[0]) % PACK_UNIT
        rows.append(jnp.pad(flat, (0, pad)).reshape(-1, 128))
    return jnp.concatenate(rows, axis=0)


def _unpack(packed, like):
    out, row = [], 0
    for p in like:
        n = math.prod(p.shape)
        nrows = (n + PACK_UNIT - 1) // PACK_UNIT * 8
        out.append(packed[row:row + nrows].reshape(-1)[:n].reshape(p.shape))
        row += nrows
    return out


def kernel(x, rel_bias_table, mix_norm_g, w_in, gate_norm_g, gate_norm_b, w_spatial, b_spatial, attn_sinks, out_norm_a_g, out_norm_b_g, w_out, ffn_norm_g, w_up, w_down, final_norm_g, loss_target, m_rel_bias_table, m_mix_norm_g, m_w_in, m_gate_norm_g, m_gate_norm_b, m_w_spatial, m_b_spatial, m_attn_sinks, m_out_norm_a_g, m_out_norm_b_g, m_w_out, m_ffn_norm_g, m_w_up, m_w_down, m_final_norm_g, v_rel_bias_table, v_mix_norm_g, v_w_in, v_gate_norm_g, v_gate_norm_b, v_w_spatial, v_b_spatial, v_attn_sinks, v_out_norm_a_g, v_out_norm_b_g, v_w_out, v_ffn_norm_g, v_w_up, v_w_down, v_final_norm_g):
    args = dict(locals())
    wts = {n: args[n] for n in WEIGHTS}
    mom = {n: args["m_" + n] for n in WEIGHTS}
    var = {n: args["v_" + n] for n in WEIGHTS}
    sp = {n: wts[n] for n in SMALL}
    x2, tgt = x[0], loss_target[0]
    T = x2.shape[0]
    tm = min(512, T)
    tl = min(1024, T)
    lg = sp["gate_norm_g"].reshape(A_GROUPS, CHUNK)
    lb = sp["gate_norm_b"].reshape(A_GROUPS, CHUNK)
    wsp = sp["w_spatial"].reshape(A_GROUPS, CHUNK, CHUNK)
    bs_col = sp["b_spatial"].reshape(A_GROUPS, CHUNK, 1)
    sinks = sp["attn_sinks"].reshape(1, B_HEADS)
    ga = sp["out_norm_a_g"].reshape(1, A_WIDTH)
    gb = sp["out_norm_b_g"].reshape(1, B_WIDTH)
    g1 = sp["mix_norm_g"].reshape(1, D_MODEL)
    g2 = sp["ffn_norm_g"].reshape(1, D_MODEL)
    gf = sp["final_norm_g"].reshape(1, D_MODEL)

    def owner_total(n, gh, others):
        return _owner_total(gh, others, tm=ROW_TILE[n], name="rs_owner_total_" + n)

    def halves_view(at, shards):
        return at.reshape(shards, 2, at.shape[0] // shards // 2, at.shape[1])

    for d in (wts, mom, var):
        d["w_in"] = jnp.swapaxes(d["w_in"], 1, 2)

    s_in = _cast_into_slot(wts["w_in"], tm=ROW_TILE["w_in"], name="cast_w_in")
    (s_out, s_up, s_down), ((g_in,),) = _cast_into_slots_carrying(
        [wts["w_out"], wts["w_up"], wts["w_down"]], steps=8, name="cast_w_rest",
        rides=[_ride_gather(s_in, chain=(0, 1, 1), chain_fracs=(0.3, 0.6))])
    win_t = g_in.reshape(PROJ_WIDTH, D_MODEL)
    bias = _bias_build(sp["rel_bias_table"])
    (n1, proj), ((g_out,), (s_up,)) = _norm_matmul_wide(
        x2, g1, win_t, tm=tm, tn=PROJ_WIDTH // 2, name="in_proj",
        rides=[_ride_gather(s_out, chain=(0, 1, 1), chain_fracs=(0.65, 0.85)), _ride_gather(s_up, s1=(0, 3, 8))])
    wo = g_out.reshape(A_WIDTH + B_WIDTH, D_MODEL)
    (mixed, mixed_t, ab), ((s_up,), (s_down,), (n1_sib,)) = _mixer_fwd(
        proj, lg, lb, wsp, bs_col, sinks, bias, ga, gb,
        rides=[_ride_gather(s_up, s2=(0, 3, 8), s1=(3, 8, 8)), _ride_gather(s_down, s1=(0, 3, 8)),
               _ride_to_sibling(n1, first=True)])
    mixed_t = halves_view(mixed_t, N_CHIPS)
    h1, ((wu,), (s_down,), (mixed_t_sib,)) = _matmul_res(
        mixed, wo, x2, tm=tl, tn=1024, tk=D_MODEL, prologue=_to_bf16, name="out_proj",
        rides=[_ride_gather(s_up, s3=(0, 3, 8), tail=(3, 8, 8), mid_frac=0.75), _ride_gather(s_down, s2=(0, 3, 8)),
               _ride_to_sibling(mixed_t, halves=True)])
    (n2t, zp, z2, z2t), ((g_down,),) = _norm_matmul_sq(
        h1, g2, wu, tm=tl, tn=1024, name="up_proj", rides=[_ride_gather(s_down, s3=(0, 3, 8), chain=(3, 8, 8), chain_fracs=(0.45, 0.75))])
    wd = g_down.reshape(D_FF, D_MODEL)
    n2t, z2t = halves_view(n2t, 1), halves_view(z2t, N_CHIPS)
    h2, ((n2t_sib,), (z2t_sib,)) = _matmul_res(
        z2, wd, h1, tm=tl, tn=1024, tk=4096, prologue=_to_bf16, name="down_proj",
        rides=[_ride_to_sibling(n2t, halves=True), _ride_to_sibling(z2t, halves=True)])

    dh2, dh2b, dgf, loss = _loss_bwd(h2, tgt, gf, tm=tm)
    dzp, ((dh2b_sib,),) = _matmul_nt(dh2b, wd, tm=tl, tn=1024, tk=D_MODEL, name="bwd_dz", extra=zp,
                                     epilogue=_sq_relu_grad, out_dtype=BF16, rides=[_ride_to_sibling(dh2b)])
    (gd, gdb), ((dzp_sib,),) = _grad_pair(z2t, z2t_sib, dh2b, dh2b_sib, cols_sharded=False, tmo=1024, tk=tl,
                                          name="grad_w_down", rides=[_ride_to_sibling(dzp)])
    (gu, gub), ((o_d,),) = _grad_pair(n2t, n2t_sib, dzp, dzp_sib, cols_sharded=True, tmo=1024, tk=tl,
                                      name="grad_w_up", rides=[_ride_scatter(gdb, None, (0, 7, 8))])
    dn2, ((o_d,), (o_u,)) = _matmul_nt(dzp, wu, tm=tl, tn=1024, tk=4096, name="bwd_dn2",
                                       rides=[_ride_scatter(gdb, o_d, (7, 8, 8)), _ride_scatter(gub, None, (0, 6, 8))])
    h_d = owner_total("w_down", gd, o_d)
    (dh1, dh1b, dg2), ((o_u,),) = _rms_bwd_res(dn2, h1, g2, dh2, tm=tm, name="ffn_norm_bwd",
                                               rides=[_ride_scatter(gub, o_u, (6, 7, 8))])
    dmixed, ((o_u,), (dh1b_sib,), (w_d,)) = _matmul_nt(
        dh1b, wo, tm=tl, tn=1024, tk=D_MODEL, name="bwd_dmixed",
        rides=[_ride_scatter(gub, o_u, (7, 8, 8)), _ride_to_sibling(dh1b), _ride_swap(h_d)])
    h_u = owner_total("w_up", gu, o_u)
    (go, gob), ((w_u,),) = _grad_pair_merged(mixed_t, mixed_t_sib, dh1b, dh1b_sib, tk=tl, name="grad_w_out",
                                             rides=[_ride_swap(h_u)])
    (duv, duv_t, dga, dwsp, dbs, dlg, dlb), ((o_o,),) = _gmlp_bwd(proj, ab, dmixed, ga, lg, lb, wsp, bs_col,
                                                                  rides=[_ride_scatter(gob)])
    h_o = owner_total("w_out", go, o_o)
    small = {"gate_norm_g": dlg, "gate_norm_b": dlb, "w_spatial": dwsp, "b_spatial": dbs, "out_norm_a_g": dga}
    hr_in = PROJ_WIDTH // N_CHIPS // 2
    (dq, dkv, dq_t, dkv_t, dgb, dsinks, dbias), ((slots_a,), (dproj_t_sib,)) = _attn_bwd(
        proj, ab, dmixed, gb, sinks, bias,
        rides=[_ride_small_to_all(_pack([small[n] for n in SMALL_A])), _ride_rows_to_sibling(duv_t, hr_in, 2, N_CHIPS)])
    dtable = _bias_grad(dbias)
    dproj_t = halves_view(jnp.concatenate([duv_t, dq_t, dkv_t], axis=0), N_CHIPS)
    ((dproj_t_sib,),) = _carrier([_ride_to_sibling(dproj_t, halves=True, shards=(2, N_CHIPS), land=dproj_t_sib)],
                                 name="trade_dproj_t")
    (gi, gib_near), ((w_o,),) = _grad_pair(
        dproj_t, dproj_t_sib, n1, n1_sib, cols_sharded=False, tmo=hr_in, tk=tl, name="grad_w_in_near", shards="near",
        rides=[_ride_swap(h_o)])
    (gi, gib_far), ((o_i,),) = _grad_pair(
        dproj_t, dproj_t_sib, n1, n1_sib, cols_sharded=False, tmo=hr_in, tk=tl, name="grad_w_in_far", shards="far",
        into=gi, rides=[_ride_scatter(gib_near, None, to=(0, 1))])
    dn1, ((o_i,),) = _matmul_parts([duv, dq, dkv], win_t, tm=tl, tn=1024, name="bwd_dn1",
                                   rides=[_ride_scatter(gib_far, o_i, to=(2,))])
    h_i = owner_total("w_in", gi, o_i)
    dx, dg1 = _rms_bwd_res(dn1, x2, g1, dh1, tm=tm, name="mix_norm_bwd", bf16_copy=False)
    small.update({"rel_bias_table": dtable.reshape(N_BUCKETS, B_HEADS), "mix_norm_g": dg1, "attn_sinks": dsinks,
                  "out_norm_b_g": dgb, "ffn_norm_g": dg2, "final_norm_g": dgf})
    (w_i,), (slots_b,) = _carrier([_ride_swap(h_i), _ride_small_to_all(_pack([small[n] for n in SMALL_B] + [loss]))],
                                  name="swap_w_in")

    out_g, out_d, out_m, out_v = {}, {}, {}, {}
    for n, h, s in zip(LARGE, [h_i, h_o, h_u, h_d], [w_i, w_o, w_u, w_d]):
        res = _adamw_halves(wts[n], h, s, mom[n], var[n], tm=ROW_TILE[n], name="adamw_" + n)
        if n == "w_in":
            res = [jnp.swapaxes(r, 1, 2) for r in res]
        out_g[n], out_d[n], out_m[n], out_v[n] = res
    for names, slots, tag in ((SMALL_A, slots_a, "a"), (SMALL_B, slots_b, "b")):
        extra = [jnp.zeros((1, 1), F32)] if tag == "b" else []
        like = [wts[n] for n in names] + extra
        res = _adamw_small(_pack(like), slots, _pack([mom[n] for n in names] + extra),
                           _pack([var[n] for n in names] + extra), name="adamw_small_" + tag)
        for store, packed in zip((out_g, out_d, out_m, out_v), res):
            for n, val in zip(names + ["loss"], _unpack(packed, like)):
                store[n] = val

    total = out_g["loss"][0, 0]
    return (total, dx[None], *[out_g[n] for n in WEIGHTS], *[out_d[n] for n in WEIGHTS],
            *[out_m[n] for n in WEIGHTS], *[out_v[n] for n in WEIGHTS])
```

```python
import math

import numpy as np
import jax
import jax.numpy as jnp
from jax import lax
from jax.experimental import pallas as pl
from jax.experimental.pallas import tpu as pltpu

F32 = jnp.float32
BF16 = jnp.bfloat16

D_MODEL = 2048
CHUNK = 128
A_GROUPS = 8
A_WIDTH = 1024
HEAD_DIM = 64
B_HEADS = 16
Q_PER_KV = 8
B_WIDTH = 1024
KV_WIDTH = 128
PROJ_WIDTH = 3328
D_FF = 8192
N_BUCKETS = 32
EPS = 1e-5
NEG = -1e30
SCALE = HEAD_DIM ** -0.5
N_CHIPS = 4
N_DEV = 8

ADAM_LR = 0.001
ADAM_B1 = 0.9
ADAM_B2 = 0.999
ADAM_EPS = 1e-08
ADAM_WD = 0.01
ADAM_STEP = 10

VMEM_LIMIT = 60 * 1024 * 1024
MESH = pl.DeviceIdType.MESH


def _bucket_thresholds():
    d = np.arange(CHUNK)
    n_exact = N_BUCKETS // 2
    relf = np.maximum(d, n_exact).astype(np.float64)
    large = n_exact + (np.log(relf / n_exact) / math.log(CHUNK / n_exact) * (N_BUCKETS - n_exact)).astype(np.int32)
    bucket = np.where(d < n_exact, d, np.minimum(large, N_BUCKETS - 1))
    return [int(np.min(d[bucket >= b])) for b in range(1, N_BUCKETS)]


BUCKET_THR = _bucket_thresholds()


def _params(sem=None):
    return pltpu.CompilerParams(dimension_semantics=sem, vmem_limit_bytes=VMEM_LIMIT)


def _gelu(x):
    c = math.sqrt(2.0 / math.pi)
    return 0.5 * x * (1.0 + jnp.tanh(c * (x + 0.044715 * (x * x * x))))


def _gelu_and_grad(x):
    c = math.sqrt(2.0 / math.pi)
    x2 = x * x
    t = jnp.tanh(c * (x + 0.044715 * (x2 * x)))
    g = 0.5 * x * (1.0 + t)
    dg = 0.5 * (1.0 + t) + 0.5 * x * (1.0 - t * t) * (c * (1.0 + 3.0 * 0.044715 * x2))
    return g, dg


def _dot(a, b):
    return jnp.dot(a, b, preferred_element_type=F32)


def _dot_nt(a, b):
    return lax.dot_general(a, b, (((1,), (1,)), ((), ())), preferred_element_type=F32)


def _dot_tn(a, b):
    return lax.dot_general(a, b, (((0,), (0,)), ((), ())), preferred_element_type=F32)


def _rms_bwd(dn, h, g):
    r = lax.rsqrt(jnp.mean(h * h, axis=-1, keepdims=True) + EPS)
    w = dn * g
    dh = r * w - h * ((r * r * r) * jnp.mean(w * h, axis=-1, keepdims=True))
    return dh, r


def _place():
    x, y, c = lax.axis_index("x"), lax.axis_index("y"), lax.axis_index("c")
    chips = [(1 - x, y), (x, 1 - y), (1 - x, 1 - y)]
    return x, y, c, chips


def _remote(src, dst, send_sem, recv_sem, to):
    return pltpu.make_async_remote_copy(src_ref=src, dst_ref=dst, send_sem=send_sem, recv_sem=recv_sem,
                                        device_id=to, device_id_type=MESH)


class _Ride:
    def __init__(self, args, out_shape, n_sem, start, finish, mids=(), aliases=None):
        self.args, self.out_shape, self.n_sem = list(args), list(out_shape), n_sem
        self.start, self.mids, self.finish = start, list(mids), finish
        self.aliases = dict(aliases or {})


def _call(body, *, name, grid, in_specs, out_specs, out_shape, scratch_shapes=(), sem=None, rides=(), aliases=None):
    single = not isinstance(out_shape, (list, tuple))
    out_specs = [out_specs] if single else list(out_specs)
    out_shape = [out_shape] if single else list(out_shape)
    n_in, n_out, n_scr = len(in_specs), len(out_shape), len(scratch_shapes)
    r_in = [len(r.args) for r in rides]
    r_out = [len(r.out_shape) for r in rides]
    any_spec = pl.BlockSpec(memory_space=pl.ANY)
    aliases, off_i, off_o = dict(aliases or {}), n_in, n_out
    for r in rides:
        for i, o in r.aliases.items():
            aliases[off_i + i] = off_o + o
        off_i += len(r.args)
        off_o += len(r.out_shape)
    steps = math.prod(grid)

    def wrapped(*refs):
        p = 0
        ins = refs[p:p + n_in]; p += n_in
        rins = refs[p:p + sum(r_in)]; p += sum(r_in)
        outs = refs[p:p + n_out]; p += n_out
        routs = refs[p:p + sum(r_out)]; p += sum(r_out)
        scr = refs[p:p + n_scr]; p += n_scr
        sems = refs[p:]
        parts, pi, po = [], 0, 0
        for k, r in enumerate(rides):
            parts.append((rins[pi:pi + r_in[k]], routs[po:po + r_out[k]], sems[2 * k], sems[2 * k + 1]))
            pi += r_in[k]
            po += r_out[k]
        lin = 0
        for d in range(len(grid)):
            lin = lin * grid[d] + pl.program_id(d)
        if rides:
            @pl.when(lin == 0)
            def _():
                for r, part in zip(rides, parts):
                    r.start(*part)
        body(*ins, *outs, *scr)
        for r, part in zip(rides, parts):
            for frac, fn in r.mids:
                @pl.when(lin == min(steps - 1, int(frac * steps)))
                def _(fn=fn, part=part):
                    fn(*part)
        if rides:
            @pl.when(lin == steps - 1)
            def _():
                for r, part in zip(rides, parts):
                    r.finish(*part)

    scratch = list(scratch_shapes)
    for r in rides:
        scratch += [pltpu.SemaphoreType.DMA((r.n_sem,)), pltpu.SemaphoreType.DMA((r.n_sem,))]
    if rides:
        sem = ("arbitrary",) * len(grid)
    res = pl.pallas_call(
        wrapped, name=name, grid=grid,
        in_specs=list(in_specs) + [any_spec] * sum(r_in),
        out_specs=out_specs + [any_spec] * sum(r_out),
        out_shape=out_shape + [s for r in rides for s in r.out_shape],
        scratch_shapes=scratch, input_output_aliases=aliases,
        compiler_params=_params(sem),
    )

    def run(*args):
        got = res(*args, *[a for r in rides for a in r.args])
        mine = got[0] if single else list(got[:n_out])
        if not rides:
            return mine
        rest, out = list(got[n_out:]), []
        for k in range(len(rides)):
            out.append(rest[:r_out[k]])
            rest = rest[r_out[k]:]
        return mine, out

    return run


def _ride_gather(slot, s1=None, s2=None, s3=None, tail=None, chain=None, mid_frac=0.6, chain_fracs=(0.35, 0.7)):
    half = slot.shape[1] // 2

    def rows(part, c, which=None):
        k0, k1, n = part
        count, first = (k1 - k0) * (half // n), c * half + k0 * (half // n)
        return pl.ds(first, count) if which is None else pl.ds(first + which * (count // 2), count // 2)

    def ids():
        x, y, c, _ = _place()
        return x, y, c, 2 * x + y, 2 * (1 - x) + y, 2 * x + (1 - y), 2 * (1 - x) + (1 - y)

    def copy(full, chip, r, ss, rs, k, to):
        piece = full.at[chip, r, :]
        return _remote(piece, piece, ss.at[k], rs.at[k], to)

    def to_neighbours(full, ss, rs, part, base):
        x, y, c, me, _, _, _ = ids()
        return [copy(full, me, rows(part, c), ss, rs, base, (1 - x, y, c)),
                copy(full, me, rows(part, c), ss, rs, base + 1, (x, 1 - y, c))]

    def from_neighbours(full, ss, rs, part, base):
        x, y, c, _, cx, cy, _ = ids()
        return [copy(full, cx, rows(part, c), ss, rs, base, (x, y, c)), copy(full, cy, rows(part, c), ss, rs, base + 1, (x, y, c))]

    def onward(full, ss, rs, part, base):
        x, y, c, _, cx, cy, _ = ids()
        return [copy(full, cx, rows(part, c, 0), ss, rs, base, (x, 1 - y, c)),
                copy(full, cy, rows(part, c, 1), ss, rs, base + 1, (1 - x, y, c))]

    def from_onward(full, ss, rs, part, base):
        x, y, c, _, _, _, cd = ids()
        return [copy(full, cd, rows(part, c, 0), ss, rs, base, (x, y, c)), copy(full, cd, rows(part, c, 1), ss, rs, base + 1, (x, y, c))]

    def to_sibling(full, ss, rs, part, base, diagonal):
        x, y, c, _, cx, cy, cd = ids()
        return [copy(full, chip, rows(part, c), ss, rs, base + j, (x, y, 1 - c))
                for j, chip in enumerate([cd] if diagonal else [cx, cy])]

    def from_sibling(full, ss, rs, part, base, diagonal):
        x, y, c, _, cx, cy, cd = ids()
        return [copy(full, chip, rows(part, 1 - c), ss, rs, base + j, (x, y, c))
                for j, chip in enumerate([cd] if diagonal else [cx, cy])]

    def start(ins, outs, ss, rs):
        full, cps = outs[0], []
        for part, base in ((s1, 0), (chain, 12)):
            if part is not None:
                cps += to_neighbours(full, ss, rs, part, base)
        for part, b_ici, b_sib in ((s2, 2, 4), (tail, 7, 9)):
            if part is not None:
                cps += onward(full, ss, rs, part, b_ici) + to_sibling(full, ss, rs, part, b_sib, False)
        if s3 is not None:
            cps += to_sibling(full, ss, rs, s3, 6, True)
        for cp in cps:
            cp.start()

    def second(part, b_in, b_ici, b_sib):
        def fn(ins, outs, ss, rs):
            for cp in from_neighbours(outs[0], ss, rs, part, b_in):
                cp.wait_recv()
            for cp in onward(outs[0], ss, rs, part, b_ici) + to_sibling(outs[0], ss, rs, part, b_sib, False):
                cp.start()
        return fn

    def third(part, b_ici, b_sib):
        def fn(ins, outs, ss, rs):
            for cp in from_onward(outs[0], ss, rs, part, b_ici):
                cp.wait_recv()
            for cp in to_sibling(outs[0], ss, rs, part, b_sib, True):
                cp.start()
        return fn

    mids = []
    if tail is not None:
        mids.append((mid_frac, third(tail, 7, 11)))
    if chain is not None:
        mids += [(chain_fracs[0], second(chain, 12, 14, 16)), (chain_fracs[1], third(chain, 14, 18))]

    def finish(ins, outs, ss, rs):
        full, got, sent = outs[0], [], []
        if s1 is not None:
            got += from_neighbours(full, ss, rs, s1, 0)
            sent += to_neighbours(full, ss, rs, s1, 0)
        if s2 is not None:
            got += from_onward(full, ss, rs, s2, 2) + from_sibling(full, ss, rs, s2, 4, False)
            sent += onward(full, ss, rs, s2, 2) + to_sibling(full, ss, rs, s2, 4, False)
        if s3 is not None:
            got += from_sibling(full, ss, rs, s3, 6, True)
            sent += to_sibling(full, ss, rs, s3, 6, True)
        if tail is not None:
            got += from_sibling(full, ss, rs, tail, 9, False) + from_sibling(full, ss, rs, tail, 11, True)
            sent += onward(full, ss, rs, tail, 7) + to_sibling(full, ss, rs, tail, 9, False) + to_sibling(full, ss, rs, tail, 11, True)
        if chain is not None:
            got += from_sibling(full, ss, rs, chain, 16, False) + from_sibling(full, ss, rs, chain, 18, True)
            sent += (to_neighbours(full, ss, rs, chain, 12) + onward(full, ss, rs, chain, 14)
                     + to_sibling(full, ss, rs, chain, 16, False) + to_sibling(full, ss, rs, chain, 18, True))
        for cp in got:
            cp.wait_recv()
        for cp in sent:
            cp.wait_send()

    return _Ride([slot], [jax.ShapeDtypeStruct(slot.shape, slot.dtype)], 19, start, finish, mids=mids, aliases={0: 0})


def _ride_scatter(q, land=None, part=(0, 1), to=(0, 1, 2)):
    k0, k1, n = part if len(part) == 3 else (part[0], part[0] + 1, part[1])
    rows_n = q.shape[1] // n
    rows = pl.ds(k0 * rows_n, (k1 - k0) * rows_n)

    def copies(ins, outs, ss, rs):
        x, y, c, chips = _place()
        return [_remote(ins[0].at[2 * chip[0] + chip[1], rows, :], outs[0].at[j, rows, :], ss.at[j], rs.at[j], (*chip, c))
                for j, chip in enumerate(chips) if j in to]

    def start(*a):
        for cp in copies(*a):
            cp.start()

    def finish(*a):
        for cp in copies(*a):
            cp.wait()

    shape = jax.ShapeDtypeStruct((3,) + q.shape[1:], q.dtype)
    if land is None:
        return _Ride([q], [shape], 3, start, finish)
    return _Ride([q, land], [shape], 3, start, finish, aliases={1: 0})


def _ride_to_sibling(a, halves=False, first=False, shards=None, land=None):
    s0, s1 = shards or (0, a.shape[0])

    def copy(ins, outs, ss, rs):
        x, y, c, _ = _place()
        if halves:
            src, dst = ins[0].at[s0:s1, 1 - c], outs[0].at[s0:s1]
        else:
            src, dst = (ins[0].at[0] if first else ins[0]), outs[0]
        return _remote(src, dst, ss.at[0], rs.at[0], (x, y, 1 - c))

    shape = (a.shape[0],) + a.shape[2:] if halves else (a.shape[1:] if first else a.shape)
    return _Ride([a] if land is None else [a, land], [jax.ShapeDtypeStruct(shape, a.dtype)], 1,
                 lambda *a_: copy(*a_).start(), lambda *a_: copy(*a_).wait(), aliases=None if land is None else {1: 0})


def _ride_rows_to_sibling(a, hr, shards, total):
    def copies(ins, outs, ss, rs):
        x, y, c, _ = _place()
        return [_remote(ins[0].at[pl.ds((2 * s + 1 - c) * hr, hr), :], outs[0].at[s], ss.at[s], rs.at[s], (x, y, 1 - c))
                for s in range(shards)]

    def start(*a_):
        for cp in copies(*a_):
            cp.start()

    def finish(*a_):
        for cp in copies(*a_):
            cp.wait()

    return _Ride([a], [jax.ShapeDtypeStruct((total, hr, a.shape[1]), a.dtype)], shards, start, finish)


def _ride_swap(h):
    def copy(ins, outs, ss, rs):
        x, y, c, _ = _place()
        return _remote(ins[0], outs[0], ss.at[0], rs.at[0], (x, y, 1 - c))

    return _Ride([h], [jax.ShapeDtypeStruct(h.shape, h.dtype)], 1,
                 lambda *a: copy(*a).start(), lambda *a: copy(*a).wait())


def _mesh_place(p):
    return (p // 4, (p // 2) % 2, p % 2)


def _ride_small_to_all(packed):
    def copies(ins, outs, ss, rs):
        x, y, c, _ = _place()
        me = 4 * x + 2 * y + c
        return [_remote(ins[0], outs[0].at[me], ss.at[k - 1], rs.at[k - 1], _mesh_place((me + k) % N_DEV))
                for k in range(1, N_DEV)]

    def own(ins, outs, ss, rs):
        x, y, c, _ = _place()
        return pltpu.make_async_copy(ins[0], outs[0].at[4 * x + 2 * y + c], ss.at[N_DEV - 1])

    def start(*a):
        own(*a).start()
        for cp in copies(*a):
            cp.start()

    def finish(ins, outs, ss, rs):
        x, y, c, _ = _place()
        me = 4 * x + 2 * y + c
        for k in range(1, N_DEV):
            _remote(ins[0], outs[0].at[(me + N_DEV - k) % N_DEV], ss.at[k - 1], rs.at[k - 1], (x, y, c)).wait_recv()
        for cp in copies(ins, outs, ss, rs):
            cp.wait_send()
        own(ins, outs, ss, rs).wait()

    return _Ride([packed], [jax.ShapeDtypeStruct((N_DEV,) + packed.shape, packed.dtype)], N_DEV, start, finish)


def _carrier(rides, *, name):
    _, outs = _call(lambda: None, name=name, grid=(1,), in_specs=[], out_specs=[], out_shape=[], rides=rides)()
    return outs


def _norm_bf16(a_ref, g_ref):
    xf = a_ref[...]
    r = lax.rsqrt(jnp.mean(xf * xf, axis=-1, keepdims=True) + EPS)
    return ((xf * r) * g_ref[...]).astype(BF16)


def _norm_matmul_wide(a, g, b, *, tm, tn, name, rides=()):
    T, K = a.shape
    N = b.shape[0]

    def body(a_ref, g_ref, b_ref, n_ref, o_ref):
        n = _norm_bf16(a_ref, g_ref)
        n_ref[...] = n
        o_ref[...] = _dot_nt(n, b_ref[...])

    return _call(
        body, name=name, grid=(N // tn, T // tm),
        in_specs=[pl.BlockSpec((tm, K), lambda j, i: (i, 0)), pl.BlockSpec((1, K), lambda j, i: (0, 0)),
                  pl.BlockSpec((tn, K), lambda j, i: (j, 0))],
        out_specs=[pl.BlockSpec((None, tm, K), lambda j, i: (j, i, 0)), pl.BlockSpec((tm, tn), lambda j, i: (i, j))],
        out_shape=[jax.ShapeDtypeStruct((N // tn, T, K), BF16), jax.ShapeDtypeStruct((T, N), F32)],
        sem=("arbitrary", "arbitrary"), rides=rides,
    )(a, g, b)


def _norm_matmul_sq(a, g, b, *, tm, tn, name, rides=()):
    T, K = a.shape
    per = b.shape[2] // tn
    N = b.shape[0] * b.shape[2]

    def body(a_ref, g_ref, b_ref, nt_ref, o_ref, z_ref, zt_ref, n_scr):
        @pl.when(pl.program_id(1) == 0)
        def _():
            n = _norm_bf16(a_ref, g_ref)
            n_scr[...] = n
            nt_ref[...] = n.T
        r = jnp.maximum(_dot(n_scr[...], b_ref[...]), 0.0)
        o_ref[...] = r.astype(BF16)
        z = (r * r).astype(BF16)
        z_ref[...] = z
        zt_ref[...] = z.T

    return _call(
        body, name=name, grid=(T // tm, N // tn),
        in_specs=[pl.BlockSpec((tm, K), lambda i, j: (i, 0)), pl.BlockSpec((1, K), lambda i, j: (0, 0)),
                  pl.BlockSpec((None, K, tn), lambda i, j: (j // per, 0, j % per))],
        out_specs=[pl.BlockSpec((K, tm), lambda i, j: (0, i)), pl.BlockSpec((tm, tn), lambda i, j: (i, j)),
                   pl.BlockSpec((tm, tn), lambda i, j: (i, j)), pl.BlockSpec((tn, tm), lambda i, j: (j, i))],
        out_shape=[jax.ShapeDtypeStruct((K, T), BF16), jax.ShapeDtypeStruct((T, N), BF16),
                   jax.ShapeDtypeStruct((T, N), BF16), jax.ShapeDtypeStruct((N, T), BF16)],
        scratch_shapes=[pltpu.VMEM((tm, K), BF16)],
        sem=("parallel", "arbitrary"), rides=rides,
    )(a, g, b)


def _grad_pair(at, at_sib, b, b_sib, *, cols_sharded, tmo, tk, name, shards=None, into=None, rides=()):
    S, _, hr, T = at.shape
    C = b.shape[-1] // N_CHIPS if cols_sharded else b.shape[-1]
    nk = T // tk

    def shard(s):
        if shards is None:
            return s
        x, y = lax.axis_index("x"), lax.axis_index("y")
        first, second = ((2 * (1 - x) + y, 2 * x + (1 - y)) if shards == "near" else (2 * (1 - x) + (1 - y), 2 * x + y))
        return jnp.where(s == 0, first, second)

    a_sel = (lambda s: 0) if cols_sharded else shard
    b_sel = shard if cols_sharded else (lambda s: 0)
    if b.ndim == 3:
        b_spec = pl.BlockSpec((None, tk, C), lambda s, i, k: (0, k, b_sel(s)))
    else:
        b_spec = pl.BlockSpec((tk, C), lambda s, i, k: (k, b_sel(s)))
    n_into = 0 if into is None else 1

    def body(a_ref, as_ref, b_ref, bs_ref, *rest):
        o_ref, ob_ref = rest[n_into:]
        k = pl.program_id(2)
        p = _dot(a_ref[...], b_ref[...]) + _dot(as_ref[...], bs_ref[...])

        @pl.when(k == 0)
        def _():
            o_ref[...] = p

        @pl.when(k > 0)
        def _():
            o_ref[...] += p

        @pl.when(k == nk - 1)
        def _():
            ob_ref[...] = o_ref[...].astype(BF16)

    out = pl.BlockSpec((None, tmo, C), lambda s, i, k: (shard(s), i, 0))
    held = [pl.BlockSpec(memory_space=pl.ANY)] * n_into
    return _call(
        body, name=name, grid=(N_CHIPS if shards is None else 2, hr // tmo, nk),
        in_specs=[pl.BlockSpec((None, None, tmo, tk), lambda s, i, k: (a_sel(s), lax.axis_index("c"), i, k)),
                  pl.BlockSpec((None, tmo, tk), lambda s, i, k: (a_sel(s), i, k)),
                  b_spec, pl.BlockSpec((tk, C), lambda s, i, k: (k, b_sel(s)))] + held,
        out_specs=[out, out],
        out_shape=[jax.ShapeDtypeStruct((N_CHIPS, hr, C), F32), jax.ShapeDtypeStruct((N_CHIPS, hr, C), BF16)],
        sem=("parallel", "parallel", "arbitrary"), rides=rides, aliases={4: 0} if into is not None else None,
    )(at, at_sib, b, b_sib, *([into] if into is not None else []))


def _grad_pair_merged(at, at_sib, b, b_sib, *, tk, name, rides=()):
    S, _, hr, T = at.shape
    C = b.shape[-1]
    nk = T // tk

    def body(a_ref, as_ref, b_ref, bs_ref, o_ref, ob_ref):
        k = pl.program_id(0)
        p = (_dot(a_ref[...].reshape(S * hr, tk), b_ref[...])
             + _dot(as_ref[...].reshape(S * hr, tk), bs_ref[...])).reshape(S, hr, C)

        @pl.when(k == 0)
        def _():
            o_ref[...] = p

        @pl.when(k > 0)
        def _():
            o_ref[...] += p

        @pl.when(k == nk - 1)
        def _():
            ob_ref[...] = o_ref[...].astype(BF16)

    out = pl.BlockSpec((S, hr, C), lambda k: (0, 0, 0))
    return _call(
        body, name=name, grid=(nk,),
        in_specs=[pl.BlockSpec((S, None, hr, tk), lambda k: (0, lax.axis_index("c"), 0, k)),
                  pl.BlockSpec((S, hr, tk), lambda k: (0, 0, k)),
                  pl.BlockSpec((tk, C), lambda k: (k, 0)), pl.BlockSpec((tk, C), lambda k: (k, 0))],
        out_specs=[out, out],
        out_shape=[jax.ShapeDtypeStruct((S, hr, C), F32), jax.ShapeDtypeStruct((S, hr, C), BF16)],
        sem=("arbitrary",), rides=rides,
    )(at, at_sib, b, b_sib)


def _matmul_parts(parts, b, *, tm, tn, name, rides=()):
    T = parts[0].shape[0]
    N = b.shape[1]
    offs = [sum(p.shape[1] for p in parts[:i]) for i in range(len(parts))]
    assert all(o % p.shape[1] == 0 for o, p in zip(offs, parts))

    def body(*refs):
        n = len(parts)
        acc = _dot(refs[0][...], refs[n][...])
        for i in range(1, n):
            acc = acc + _dot(refs[i][...], refs[n + i][...])
        refs[-1][...] = acc

    a_specs = [pl.BlockSpec((tm, p.shape[1]), lambda i, j: (i, 0)) for p in parts]
    b_specs = [pl.BlockSpec((p.shape[1], tn), lambda i, j, r=o // p.shape[1]: (r, j)) for o, p in zip(offs, parts)]
    return _call(
        body, name=name, grid=(T // tm, N // tn), in_specs=a_specs + b_specs,
        out_specs=pl.BlockSpec((tm, tn), lambda i, j: (i, j)), out_shape=jax.ShapeDtypeStruct((T, N), F32),
        sem=("parallel", "parallel"), rides=rides,
    )(*parts, *([b] * len(parts)))


def _to_bf16(v):
    return v.astype(BF16)


def _matmul_res(a, b, res, *, tm, tn, tk, prologue, name, rides=()):
    T, K = a.shape
    N = b.shape[1]

    def body(a_ref, b_ref, res_ref, o_ref):
        k = pl.program_id(2)
        p = _dot(prologue(a_ref[...]), b_ref[...])

        @pl.when(k == 0)
        def _():
            o_ref[...] = res_ref[...] + p

        @pl.when(k > 0)
        def _():
            o_ref[...] += p

    return _call(
        body, name=name, grid=(T // tm, N // tn, K // tk),
        in_specs=[pl.BlockSpec((tm, tk), lambda i, j, k: (i, k)), pl.BlockSpec((tk, tn), lambda i, j, k: (k, j)),
                  pl.BlockSpec((tm, tn), lambda i, j, k: (i, j))],
        out_specs=pl.BlockSpec((tm, tn), lambda i, j, k: (i, j)),
        out_shape=jax.ShapeDtypeStruct((T, N), F32),
        sem=("parallel", "parallel", "arbitrary"), rides=rides,
    )(a, b, res)


def _matmul_nt(a, b, *, tm, tn, tk, name, extra=None, epilogue=None, out_dtype=F32, rides=()):
    T, K = a.shape
    two = b.ndim == 3 and tk == 2 * b.shape[2]
    if two:
        N, ks = b.shape[1], b.shape[2]
        b_specs = [pl.BlockSpec((None, tn, ks), lambda i, j, k: (2 * k, j, 0)),
                   pl.BlockSpec((None, tn, ks), lambda i, j, k: (2 * k + 1, j, 0))]
    elif b.ndim == 3:
        per = b.shape[2] // tk
        N = b.shape[1]
        b_specs = [pl.BlockSpec((None, tn, tk), lambda i, j, k: (k // per, j, k % per))]
    else:
        N = b.shape[0]
        b_specs = [pl.BlockSpec((tn, tk), lambda i, j, k: (j, k))]
    nb = len(b_specs)
    nk = K // tk
    assert out_dtype == F32 or nk == 1
    in_specs = [pl.BlockSpec((tm, tk), lambda i, j, k: (i, k))] + b_specs
    args = [a] + [b] * nb
    if extra is not None:
        in_specs.append(pl.BlockSpec((tm, tn), lambda i, j, k: (i, j)))
        args.append(extra)

    def body(*refs):
        a_ref, b_ref = refs[0], refs[1]
        o_ref = refs[-1]
        if two:
            p = (_dot_nt(a_ref[:, :tk // 2].astype(BF16), refs[1][...])
                 + _dot_nt(a_ref[:, tk // 2:].astype(BF16), refs[2][...]))
        else:
            p = _dot_nt(a_ref[...].astype(BF16), b_ref[...])
        if nk == 1:
            if epilogue is not None:
                p = epilogue(p, refs[1 + nb][...])
            o_ref[...] = p.astype(out_dtype)
        else:
            k = pl.program_id(2)

            @pl.when(k == 0)
            def _():
                o_ref[...] = p

            @pl.when(k > 0)
            def _():
                o_ref[...] += p

    return _call(
        body, name=name, grid=(T // tm, N // tn, nk),
        in_specs=in_specs,
        out_specs=pl.BlockSpec((tm, tn), lambda i, j, k: (i, j)),
        out_shape=jax.ShapeDtypeStruct((T, N), out_dtype),
        sem=("parallel", "parallel", "arbitrary"), rides=rides,
    )(*args)


def _loss_bwd(h2, tgt, g, *, tm):
    T, D = h2.shape

    def body(h_ref, t_ref, g_ref, dh_ref, dhb_ref, dg_ref, loss_ref):
        @pl.when(pl.program_id(0) == 0)
        def _():
            dg_ref[...] = jnp.zeros_like(dg_ref)
            loss_ref[...] = jnp.zeros_like(loss_ref)
        h = h_ref[...]
        gg = g_ref[...]
        r = lax.rsqrt(jnp.mean(h * h, axis=-1, keepdims=True) + EPS)
        hn = h * r
        err = hn * gg - t_ref[...]
        loss_ref[...] += 0.5 * jnp.sum(jnp.mean(err * err, axis=-1, keepdims=True), axis=0, keepdims=True)
        dy = err * (1.0 / D)
        dg_ref[...] += jnp.sum(dy * hn, axis=0, keepdims=True)
        w = dy * gg
        dh = r * w - h * ((r * r * r) * jnp.mean(w * h, axis=-1, keepdims=True))
        dh_ref[...] = dh
        dhb_ref[...] = dh.astype(BF16)

    tile = pl.BlockSpec((tm, D), lambda i: (i, 0))
    return pl.pallas_call(
        body, name="loss_bwd", grid=(T // tm,),
        in_specs=[tile, tile, pl.BlockSpec((1, D), lambda i: (0, 0))],
        out_specs=[tile, tile, pl.BlockSpec((1, D), lambda i: (0, 0)), pl.BlockSpec((1, 1), lambda i: (0, 0))],
        out_shape=[jax.ShapeDtypeStruct((T, D), F32), jax.ShapeDtypeStruct((T, D), BF16),
                   jax.ShapeDtypeStruct((1, D), F32), jax.ShapeDtypeStruct((1, 1), F32)],
        compiler_params=_params(("arbitrary",)),
    )(h2, tgt, g)


def _rms_bwd_res(dn, h, g, dres, *, tm, name, bf16_copy=True, rides=()):
    T, D = h.shape

    def body(dn_ref, h_ref, g_ref, dres_ref, dh_ref, *rest):
        dg_ref = rest[-1]

        @pl.when(pl.program_id(0) == 0)
        def _():
            dg_ref[...] = jnp.zeros_like(dg_ref)
        h_ = h_ref[...]
        dn_ = dn_ref[...]
        dh, r = _rms_bwd(dn_, h_, g_ref[...])
        dg_ref[...] += jnp.sum(dn_ * (h_ * r), axis=0, keepdims=True)
        dh = dres_ref[...] + dh
        dh_ref[...] = dh
        if bf16_copy:
            rest[0][...] = dh.astype(BF16)

    tile = pl.BlockSpec((tm, D), lambda i: (i, 0))
    row = pl.BlockSpec((1, D), lambda i: (0, 0))
    copy_spec = [tile] if bf16_copy else []
    copy_shape = [jax.ShapeDtypeStruct((T, D), BF16)] if bf16_copy else []
    return _call(
        body, name=name, grid=(T // tm,),
        in_specs=[tile, tile, row, tile], out_specs=[tile] + copy_spec + [row],
        out_shape=[jax.ShapeDtypeStruct((T, D), F32)] + copy_shape + [jax.ShapeDtypeStruct((1, D), F32)],
        sem=("arbitrary",), rides=rides,
    )(dn, h, g, dres)


def _rel_distance():
    i = lax.broadcasted_iota(jnp.int32, (CHUNK, 2 * CHUNK), 0)
    j = lax.broadcasted_iota(jnp.int32, (CHUNK, 2 * CHUNK), 1)
    return i + CHUNK - j


def _bias_build(table):
    def body(tab_ref, o_ref):
        rel = _rel_distance()
        j = lax.broadcasted_iota(jnp.int32, (CHUNK, 2 * CHUNK), 1)
        band = (rel >= 0) & (rel < CHUNK)
        ge = [rel >= t for t in BUCKET_THR]
        for h in range(B_HEADS):
            cur = jnp.full((CHUNK, 2 * CHUNK), tab_ref[0, h], F32)
            for b in range(1, N_BUCKETS):
                cur = jnp.where(ge[b - 1], tab_ref[b, h], cur)
            o_ref[0, h] = jnp.where(band & (j >= CHUNK), cur, NEG)
            o_ref[1, h] = jnp.where(band, cur, NEG)

    return pl.pallas_call(
        body, name="bias_build",
        in_specs=[pl.BlockSpec(memory_space=pltpu.SMEM)],
        out_specs=pl.BlockSpec(memory_space=pltpu.VMEM),
        out_shape=jax.ShapeDtypeStruct((2, B_HEADS, CHUNK, 2 * CHUNK), F32),
    )(table)


def _bias_grad(dbias):
    def body(db_ref, o_ref, acc_ref):
        rel = _rel_distance()
        lo = [0] + BUCKET_THR
        hi = BUCKET_THR + [CHUNK]
        for b in range(N_BUCKETS):
            m = (rel >= lo[b]) & (rel < hi[b])
            for h in range(B_HEADS):
                row = b * B_HEADS + h
                acc_ref[row:row + 1, :] = jnp.sum(jnp.where(m, db_ref[h], 0.0), axis=0, keepdims=True)
        o_ref[...] = jnp.sum(acc_ref[...], axis=1, keepdims=True)

    return pl.pallas_call(
        body, name="bias_grad",
        in_specs=[pl.BlockSpec(memory_space=pltpu.VMEM)],
        out_specs=pl.BlockSpec(memory_space=pltpu.VMEM),
        out_shape=jax.ShapeDtypeStruct((N_BUCKETS * B_HEADS, 1), F32),
        scratch_shapes=[pltpu.VMEM((N_BUCKETS * B_HEADS, 2 * CHUNK), F32)],
    )(dbias)


def _causal_mask():
    t = lax.broadcasted_iota(jnp.int32, (CHUNK, CHUNK), 0)
    s = lax.broadcasted_iota(jnp.int32, (CHUNK, CHUNK), 1)
    return s <= t


def _gate_forward(u, v, lg, lb, wc, bs):
    ug = _gelu(u)
    vg = _gelu(v)
    mu = jnp.mean(vg, axis=-1, keepdims=True)
    xc = vg - mu
    rstd = lax.rsqrt(jnp.mean(xc * xc, axis=-1, keepdims=True) + EPS)
    xhat = xc * rstd
    vl = (xhat * lg + lb).astype(BF16)
    mixed = _dot(wc, vl) + bs
    return ug, xhat, rstd, vl, mixed


def _softmax_scores(qk, bias, sink):
    s = qk + bias
    m = jnp.maximum(jnp.max(s, axis=-1, keepdims=True), sink)
    p = jnp.exp(s - m)
    e_sink = jnp.exp(sink - m)
    inv = 1.0 / (jnp.sum(p, axis=-1, keepdims=True) + e_sink)
    return p * inv, e_sink * inv


PAIRS = Q_PER_KV // 2


def _head(g, pr, e):
    return g * Q_PER_KV + 2 * pr + e


def _stack_pairs(ref, g, col0=0):
    w = 2 * HEAD_DIM
    return jnp.concatenate([ref[:, col0 + (g * PAIRS + pr) * w:col0 + (g * PAIRS + pr + 1) * w] for pr in range(PAIRS)],
                           axis=0)


def _low_lanes():
    return lax.broadcasted_iota(jnp.int32, (2 * CHUNK, 2 * HEAD_DIM), 1) < HEAD_DIM


def _band_operands(kv_prev, kv_cur):
    band = jnp.concatenate([kv_prev, kv_cur], axis=0)
    low = _low_lanes()
    ops = []
    for cat in (band[:, :KV_WIDTH], band[:, KV_WIDTH:]):
        rol = pltpu.roll(cat, HEAD_DIM, 1)
        ops.append([[jnp.where(low if e == 0 else ~low, cat if g == e else rol, 0.0).astype(BF16) for e in range(2)]
                    for g in range(2)])
    return ops


def _mixer_fwd(proj, lg, lb, wsp, bs_col, sinks, bias, ga, gb, rides=()):
    T = proj.shape[0]
    nb = T // CHUNK

    def body(u_ref, v_ref, q_ref, kvc_ref, kvp_ref, lg_ref, lb_ref, w_ref, bs_ref, sink_ref, bias_ref,
             ga_ref, gb_ref, mixed_ref, mixed_t_ref, ab_ref):
        causal = _causal_mask()
        ssq = jnp.zeros((CHUNK, 1), F32)
        for g in range(A_GROUPS):
            cols = slice(g * CHUNK, (g + 1) * CHUNK)
            wc = jnp.where(causal, w_ref[g], 0.0).astype(BF16)
            ug, _, _, _, mixed = _gate_forward(u_ref[:, cols], v_ref[:, cols], lg_ref[g:g + 1, :], lb_ref[g:g + 1, :],
                                               wc, bs_ref[g])
            a = ug * mixed
            ab_ref[:, cols] = a
            ssq = ssq + jnp.sum(a * a, axis=-1, keepdims=True)
        ra = lax.rsqrt(ssq * (1.0 / A_WIDTH) + EPS)
        mixed_ref[:, :A_WIDTH] = ((ab_ref[:, :A_WIDTH] * ra) * ga_ref[...]).astype(BF16)

        kops, vops = _band_operands(kvp_ref[...], kvc_ref[...])
        ssq = jnp.zeros((CHUNK, 1), F32)
        for g in range(B_HEADS // Q_PER_KV):
            qst = (_stack_pairs(q_ref, g) * SCALE).astype(BF16)
            o_st = jnp.zeros((PAIRS * CHUNK, 2 * HEAD_DIM), F32)
            for e in range(2):
                s_all = _dot_nt(qst, kops[g][e])
                ps = []
                for pr in range(PAIRS):
                    h = _head(g, pr, e)
                    p, _ = _softmax_scores(s_all[pr * CHUNK:(pr + 1) * CHUNK], bias_ref[h], sink_ref[0, h])
                    ps.append(p.astype(BF16))
                o_st = o_st + _dot(jnp.concatenate(ps, axis=0), vops[g][e])
            for pr in range(PAIRS):
                o = o_st[pr * CHUNK:(pr + 1) * CHUNK]
                c0 = A_WIDTH + (g * PAIRS + pr) * 2 * HEAD_DIM
                ab_ref[:, c0:c0 + 2 * HEAD_DIM] = o
                ssq = ssq + jnp.sum(o * o, axis=-1, keepdims=True)
        rb = lax.rsqrt(ssq * (1.0 / B_WIDTH) + EPS)
        mixed_ref[:, A_WIDTH:] = ((ab_ref[:, A_WIDTH:] * rb) * gb_ref[...]).astype(BF16)
        mixed_t_ref[...] = mixed_ref[...].T

    full = lambda *shape: pl.BlockSpec(shape, lambda n: (0,) * len(shape))
    return _call(
        body, name="mixer_fwd", grid=(nb,),
        in_specs=[pl.BlockSpec((CHUNK, A_WIDTH), lambda n: (n, 0)),
                  pl.BlockSpec((CHUNK, A_WIDTH), lambda n: (n, 1)),
                  pl.BlockSpec((CHUNK, B_WIDTH), lambda n: (n, 2)),
                  pl.BlockSpec((CHUNK, 2 * KV_WIDTH), lambda n: (n, 12)),
                  pl.BlockSpec((CHUNK, 2 * KV_WIDTH), lambda n: (jnp.maximum(n - 1, 0), 12)),
                  full(A_GROUPS, CHUNK), full(A_GROUPS, CHUNK), full(A_GROUPS, CHUNK, CHUNK), full(A_GROUPS, CHUNK, 1),
                  pl.BlockSpec(memory_space=pltpu.SMEM),
                  pl.BlockSpec((None, B_HEADS, CHUNK, 2 * CHUNK), lambda n: (jnp.minimum(n, 1), 0, 0, 0)),
                  full(1, A_WIDTH), full(1, B_WIDTH)],
        out_specs=[pl.BlockSpec((CHUNK, D_MODEL), lambda n: (n, 0)), pl.BlockSpec((D_MODEL, CHUNK), lambda n: (0, n)),
                   pl.BlockSpec((CHUNK, D_MODEL), lambda n: (n, 0))],
        out_shape=[jax.ShapeDtypeStruct((T, D_MODEL), BF16), jax.ShapeDtypeStruct((D_MODEL, T), BF16),
                   jax.ShapeDtypeStruct((T, D_MODEL), F32)],
        sem=("parallel",), rides=rides,
    )(proj, proj, proj, proj, proj, lg, lb, wsp, bs_col, sinks, bias, ga, gb)


def _gmlp_bwd(proj, ab, dmixed, ga, lg, lb, wsp, bs_col, rides=()):
    T = proj.shape[0]
    nb = T // CHUNK

    def body(u_ref, v_ref, a_ref, dna_ref, ga_ref, lg_ref, lb_ref, w_ref, bs_ref,
             dp_ref, dpt_ref, dga_ref, dw_ref, dbs_ref, dlg_ref, dlb_ref):
        @pl.when(pl.program_id(0) == 0)
        def _():
            for r in (dga_ref, dw_ref, dbs_ref, dlg_ref, dlb_ref):
                r[...] = jnp.zeros_like(r)
        causal = _causal_mask()
        a_all = a_ref[...]
        dna = dna_ref[...]
        da_all, ra = _rms_bwd(dna, a_all, ga_ref[...])
        dga_ref[...] += jnp.sum(dna * (a_all * ra), axis=0, keepdims=True)
        for g in range(A_GROUPS):
            cols = slice(g * CHUNK, (g + 1) * CHUNK)
            wc = jnp.where(causal, w_ref[g], 0.0).astype(BF16)
            lgg = lg_ref[g:g + 1, :]
            u = u_ref[:, cols]
            v = v_ref[:, cols]
            ug, xhat, rstd, vl, mixed = _gate_forward(u, v, lgg, lb_ref[g:g + 1, :], wc, bs_ref[g])
            da = da_all[:, cols]
            dug = da * mixed
            dmg = da * ug
            dmg_b = dmg.astype(BF16)
            dbs_ref[g] += jnp.sum(dmg, axis=-1, keepdims=True)
            dw_ref[g] += jnp.where(causal, _dot_nt(dmg_b, vl), 0.0)
            dvl = _dot_tn(wc, dmg_b)
            dlg_ref[g:g + 1, :] += jnp.sum(dvl * xhat, axis=0, keepdims=True)
            dlb_ref[g:g + 1, :] += jnp.sum(dvl, axis=0, keepdims=True)
            dxh = dvl * lgg
            dvg = rstd * (dxh - jnp.mean(dxh, axis=-1, keepdims=True)
                          - xhat * jnp.mean(dxh * xhat, axis=-1, keepdims=True))
            _, gu = _gelu_and_grad(u)
            _, gv = _gelu_and_grad(v)
            dp_ref[:, cols] = (dug * gu).astype(BF16)
            dp_ref[:, A_WIDTH + g * CHUNK:A_WIDTH + (g + 1) * CHUNK] = (dvg * gv).astype(BF16)
        dpt_ref[...] = dp_ref[...].T

    full = lambda *shape: pl.BlockSpec(shape, lambda n: (0,) * len(shape))
    return _call(
        body, name="gmlp_bwd", grid=(nb,),
        in_specs=[pl.BlockSpec((CHUNK, A_WIDTH), lambda n: (n, 0)),
                  pl.BlockSpec((CHUNK, A_WIDTH), lambda n: (n, 1)),
                  pl.BlockSpec((CHUNK, A_WIDTH), lambda n: (n, 0)),
                  pl.BlockSpec((CHUNK, A_WIDTH), lambda n: (n, 0)),
                  full(1, A_WIDTH), full(A_GROUPS, CHUNK), full(A_GROUPS, CHUNK), full(A_GROUPS, CHUNK, CHUNK),
                  full(A_GROUPS, CHUNK, 1)],
        out_specs=[pl.BlockSpec((CHUNK, 2 * A_WIDTH), lambda n: (n, 0)), pl.BlockSpec((2 * A_WIDTH, CHUNK), lambda n: (0, n)),
                   full(1, A_WIDTH), full(A_GROUPS, CHUNK, CHUNK), full(A_GROUPS, CHUNK, 1),
                   full(A_GROUPS, CHUNK), full(A_GROUPS, CHUNK)],
        out_shape=[jax.ShapeDtypeStruct((T, 2 * A_WIDTH), BF16), jax.ShapeDtypeStruct((2 * A_WIDTH, T), BF16),
                   jax.ShapeDtypeStruct((1, A_WIDTH), F32), jax.ShapeDtypeStruct((A_GROUPS, CHUNK, CHUNK), F32),
                   jax.ShapeDtypeStruct((A_GROUPS, CHUNK, 1), F32), jax.ShapeDtypeStruct((A_GROUPS, CHUNK), F32),
                   jax.ShapeDtypeStruct((A_GROUPS, CHUNK), F32)],
        sem=("arbitrary",), rides=rides,
    )(proj, proj, ab, dmixed, ga, lg, lb, wsp, bs_col)


def _attn_bwd(proj, ab, dmixed, gb, sinks, bias, rides=()):
    T = proj.shape[0]
    nb = T // CHUNK
    qn = lambda n: jnp.minimum(n, nb - 1)

    def body(q_ref, kvc_ref, kvp_ref, o_ref, dnb_ref, gb_ref, sink_ref, bias_ref,
             dq_ref, dkv_ref, dqt_ref, dkvt_ref, dgb_ref, dsink_ref, dbias_ref, carry_ref, sacc_ref):
        n = pl.program_id(0)

        @pl.when(n == 0)
        def _():
            carry_ref[...] = jnp.zeros_like(carry_ref)
            sacc_ref[...] = jnp.zeros_like(sacc_ref)
            dgb_ref[...] = jnp.zeros_like(dgb_ref)
            dbias_ref[...] = jnp.zeros_like(dbias_ref)

        @pl.when(n < nb)
        def _():
            o_all = o_ref[...]
            dnb = dnb_ref[...]
            do_all, rb = _rms_bwd(dnb, o_all, gb_ref[...])
            dgb_ref[...] += jnp.sum(dnb * (o_all * rb), axis=0, keepdims=True)
            kops, vops = _band_operands(kvp_ref[...], kvc_ref[...])
            low = _low_lanes()
            halves = []
            for g in range(B_HEADS // Q_PER_KV):
                qst = (_stack_pairs(q_ref, g) * SCALE).astype(BF16)
                dost = _stack_pairs(do_all, g).astype(BF16)
                dq_st = jnp.zeros((PAIRS * CHUNK, 2 * HEAD_DIM), F32)
                dk_e, dv_e = [], []
                for e in range(2):
                    s_all = _dot_nt(qst, kops[g][e])
                    dp_all = _dot_nt(dost, vops[g][e])
                    ps, dsrs = [], []
                    for pr in range(PAIRS):
                        h = _head(g, pr, e)
                        rows = slice(pr * CHUNK, (pr + 1) * CHUNK)
                        p, p_sink = _softmax_scores(s_all[rows], bias_ref[h], sink_ref[0, h])
                        dp = dp_all[rows]
                        delta = jnp.sum(p * dp, axis=-1, keepdims=True)
                        ds = p * (dp - delta)
                        sacc_ref[:, h:h + 1] += -(p_sink * delta)
                        dbias_ref[h] += ds
                        ps.append(p.astype(BF16))
                        dsrs.append(ds.astype(BF16))
                    dsr_all = jnp.concatenate(dsrs, axis=0)
                    dq_st = dq_st + _dot(dsr_all, kops[g][e])
                    dk_e.append(_dot_tn(dsr_all, qst))
                    dv_e.append(_dot_tn(jnp.concatenate(ps, axis=0), dost))
                for pr in range(PAIRS):
                    c0 = (g * PAIRS + pr) * 2 * HEAD_DIM
                    dq_ref[:, c0:c0 + 2 * HEAD_DIM] = (dq_st[pr * CHUNK:(pr + 1) * CHUNK] * SCALE).astype(BF16)
                halves.append((dk_e, dv_e))
            tiles = []
            for t in range(2):
                g0, g1 = halves[0][t], halves[1][t]
                tiles.append(jnp.where(low, g0[0] + pltpu.roll(g0[1], HEAD_DIM, 1), pltpu.roll(g1[0], HEAD_DIM, 1) + g1[1]))
            dband = jnp.concatenate(tiles, axis=1)
            dkv = (carry_ref[...] + dband[:CHUNK]).astype(BF16)
            dkv_ref[...] = dkv
            dkvt_ref[...] = dkv.T
            dqt_ref[...] = dq_ref[...].T
            carry_ref[...] = dband[CHUNK:]

        @pl.when(n == nb)
        def _():
            dkv = carry_ref[...].astype(BF16)
            dkv_ref[...] = dkv
            dkvt_ref[...] = dkv.T
            dsink_ref[...] = jnp.sum(sacc_ref[...], axis=0, keepdims=True)

    full = lambda *shape: pl.BlockSpec(shape, lambda n: (0,) * len(shape))
    return _call(
        body, name="attn_bwd", grid=(nb + 1,),
        in_specs=[pl.BlockSpec((CHUNK, B_WIDTH), lambda n: (qn(n), 2)),
                  pl.BlockSpec((CHUNK, 2 * KV_WIDTH), lambda n: (qn(n), 12)),
                  pl.BlockSpec((CHUNK, 2 * KV_WIDTH), lambda n: (jnp.maximum(qn(n) - 1, 0), 12)),
                  pl.BlockSpec((CHUNK, B_WIDTH), lambda n: (qn(n), 1)),
                  pl.BlockSpec((CHUNK, B_WIDTH), lambda n: (qn(n), 1)),
                  full(1, B_WIDTH), pl.BlockSpec(memory_space=pltpu.SMEM),
                  pl.BlockSpec((None, B_HEADS, CHUNK, 2 * CHUNK), lambda n: (jnp.minimum(n, 1), 0, 0, 0))],
        out_specs=[pl.BlockSpec((CHUNK, B_WIDTH), lambda n: (qn(n), 0)),
                   pl.BlockSpec((CHUNK, 2 * KV_WIDTH), lambda n: (jnp.maximum(n - 1, 0), 0)),
                   pl.BlockSpec((B_WIDTH, CHUNK), lambda n: (0, qn(n))),
                   pl.BlockSpec((2 * KV_WIDTH, CHUNK), lambda n: (0, jnp.maximum(n - 1, 0))),
                   full(1, B_WIDTH), full(1, B_HEADS), full(B_HEADS, CHUNK, 2 * CHUNK)],
        out_shape=[jax.ShapeDtypeStruct((T, B_WIDTH), BF16), jax.ShapeDtypeStruct((T, 2 * KV_WIDTH), BF16),
                   jax.ShapeDtypeStruct((B_WIDTH, T), BF16), jax.ShapeDtypeStruct((2 * KV_WIDTH, T), BF16),
                   jax.ShapeDtypeStruct((1, B_WIDTH), F32), jax.ShapeDtypeStruct((1, B_HEADS), F32),
                   jax.ShapeDtypeStruct((B_HEADS, CHUNK, 2 * CHUNK), F32)],
        scratch_shapes=[pltpu.VMEM((CHUNK, 2 * KV_WIDTH), F32), pltpu.VMEM((CHUNK, B_HEADS), F32)],
        sem=("arbitrary",), rides=rides,
    )(proj, proj, proj, ab, dmixed, gb, sinks, bias)


def _sq_relu_grad(acc, r):
    return acc * (2.0 * r.astype(F32))


def _chip_index():
    return (2 * lax.axis_index("x") + lax.axis_index("y")).astype(jnp.int32).reshape(1)


def _cast_into_slot(w, *, tm, name):
    _, R, C = w.shape

    def body(me_ref, w_ref, o_ref):
        del me_ref
        o_ref[...] = w_ref[...].astype(BF16)

    return pl.pallas_call(
        body, name=name,
        grid_spec=pltpu.PrefetchScalarGridSpec(
            num_scalar_prefetch=1, grid=(R // tm,),
            in_specs=[pl.BlockSpec((None, tm, C), lambda i, me: (0, i, 0))],
            out_specs=pl.BlockSpec((None, tm, C), lambda i, me: (me[0], i, 0))),
        out_shape=jax.ShapeDtypeStruct((N_CHIPS, R, C), BF16), compiler_params=_params(("parallel",)),
    )(_chip_index(), w)


def _cast_into_slots_carrying(ws, *, steps, name, rides):
    n = len(ws)

    def body(*refs):
        for w_ref, o_ref in zip(refs[:n], refs[n:]):
            o_ref[...] = w_ref[...].astype(BF16)

    me = lambda: 2 * lax.axis_index("x") + lax.axis_index("y")
    return _call(
        body, name=name, grid=(steps,),
        in_specs=[pl.BlockSpec((None, w.shape[1] // steps, w.shape[2]), lambda i: (0, i, 0)) for w in ws],
        out_specs=[pl.BlockSpec((None, w.shape[1] // steps, w.shape[2]), lambda i: (me(), i, 0)) for w in ws],
        out_shape=[jax.ShapeDtypeStruct((N_CHIPS,) + w.shape[1:], BF16) for w in ws], sem=("arbitrary",), rides=rides,
    )(*ws)


def _owner_total(gh, others, *, tm, name):
    _, hr, C = gh.shape

    def body(me_ref, g_ref, o_ref_in, out_ref):
        del me_ref
        acc = g_ref[...]
        for j in range(3):
            acc = acc + o_ref_in[j].astype(F32)
        out_ref[...] = acc

    return pl.pallas_call(
        body, name=name,
        grid_spec=pltpu.PrefetchScalarGridSpec(
            num_scalar_prefetch=1, grid=(hr // tm,),
            in_specs=[pl.BlockSpec((None, tm, C), lambda i, me: (me[0], i, 0)),
                      pl.BlockSpec((3, tm, C), lambda i, me: (0, i, 0))],
            out_specs=pl.BlockSpec((tm, C), lambda i, me: (i, 0))),
        out_shape=jax.ShapeDtypeStruct((hr, C), F32),
        compiler_params=_params(("parallel",)),
    )(_chip_index(), gh, others)


def _adamw_math(w, g, m, v):
    m = ADAM_B1 * m + (1.0 - ADAM_B1) * g
    v = ADAM_B2 * v + (1.0 - ADAM_B2) * (g * g)
    m_hat = m / (1.0 - ADAM_B1 ** ADAM_STEP)
    v_hat = v / (1.0 - ADAM_B2 ** ADAM_STEP)
    delta = -ADAM_LR * (m_hat / (jnp.sqrt(v_hat) + ADAM_EPS) + ADAM_WD * w)
    return delta, m, v


def _adamw_halves(w, own, got, m, v, *, tm, name, rides=()):
    _, R, C = w.shape
    nt = (R // 2) // tm

    def body(w_ref, own_ref, got_ref, m_ref, v_ref, g_ref, d_ref, nm_ref, nv_ref):
        g = jnp.where(pl.program_id(0) == lax.axis_index("c"), own_ref[...], got_ref[...])
        g_ref[...] = g
        d_ref[...], nm_ref[...], nv_ref[...] = _adamw_math(w_ref[...], g, m_ref[...], v_ref[...])

    whole = pl.BlockSpec((None, tm, C), lambda h, i: (0, h * nt + i, 0))
    half = pl.BlockSpec((tm, C), lambda h, i: (i, 0))
    return _call(
        body, name=name, grid=(2, nt), in_specs=[whole, half, half, whole, whole], out_specs=[whole] * 4,
        out_shape=[jax.ShapeDtypeStruct((1, R, C), F32)] * 4, sem=("parallel", "parallel"), rides=rides,
    )(w, own, got, m, v)


def _adamw_small(w, slots, m, v, *, name):
    def body(w_ref, slots_ref, m_ref, v_ref, g_ref, d_ref, nm_ref, nv_ref):
        g = slots_ref[0]
        for d in range(1, N_DEV):
            g = g + slots_ref[d]
        g_ref[...] = g
        d_ref[...], nm_ref[...], nv_ref[...] = _adamw_math(w_ref[...], g, m_ref[...], v_ref[...])

    vmem = pl.BlockSpec(memory_space=pltpu.VMEM)
    return pl.pallas_call(
        body, name=name, in_specs=[vmem] * 4, out_specs=[vmem] * 4,
        out_shape=[jax.ShapeDtypeStruct(w.shape, F32)] * 4, compiler_params=_params(),
    )(w, slots, m, v)


SMALL = ["rel_bias_table", "mix_norm_g", "gate_norm_g", "gate_norm_b", "w_spatial", "b_spatial", "attn_sinks",
         "out_norm_a_g", "out_norm_b_g", "ffn_norm_g", "final_norm_g"]
SMALL_A = ["gate_norm_g", "gate_norm_b", "w_spatial", "b_spatial", "out_norm_a_g"]
SMALL_B = ["rel_bias_table", "mix_norm_g", "attn_sinks", "out_norm_b_g", "ffn_norm_g", "final_norm_g"]
LARGE = ["w_in", "w_out", "w_up", "w_down"]
ROW_TILE = {"w_in": 208, "w_out": 256, "w_up": 256, "w_down": 256}
WEIGHTS = ["rel_bias_table", "mix_norm_g", "w_in", "gate_norm_g", "gate_norm_b", "w_spatial", "b_spatial", "attn_sinks",
           "out_norm_a_g", "out_norm_b_g", "w_out", "ffn_norm_g", "w_up", "w_down", "final_norm_g"]
PACK_UNIT = 8 * 128


def _pack(parts):
    rows = []
    for p in parts:
        flat = p.reshape(-1)
        pad = (-flat.shape[0]) % PACK_UNIT
        rows.append(jnp.pad(flat, (0, pad)).reshape(-1, 128))
    return jnp.concatenate(rows, axis=0)


def _unpack(packed, like):
    out, row = [], 0
    for p in like:
        n = math.prod(p.shape)
        nrows = (n + PACK_UNIT - 1) // PACK_UNIT * 8
        out.append(packed[row:row + nrows].reshape(-1)[:n].reshape(p.shape))
        row += nrows
    return out


def kernel(x, rel_bias_table, mix_norm_g, w_in, gate_norm_g, gate_norm_b, w_spatial, b_spatial, attn_sinks, out_norm_a_g, out_norm_b_g, w_out, ffn_norm_g, w_up, w_down, final_norm_g, loss_target, m_rel_bias_table, m_mix_norm_g, m_w_in, m_gate_norm_g, m_gate_norm_b, m_w_spatial, m_b_spatial, m_attn_sinks, m_out_norm_a_g, m_out_norm_b_g, m_w_out, m_ffn_norm_g, m_w_up, m_w_down, m_final_norm_g, v_rel_bias_table, v_mix_norm_g, v_w_in, v_gate_norm_g, v_gate_norm_b, v_w_spatial, v_b_spatial, v_attn_sinks, v_out_norm_a_g, v_out_norm_b_g, v_w_out, v_ffn_norm_g, v_w_up, v_w_down, v_final_norm_g):
    args = dict(locals())
    wts = {n: args[n] for n in WEIGHTS}
    mom = {n: args["m_" + n] for n in WEIGHTS}
    var = {n: args["v_" + n] for n in WEIGHTS}
    sp = {n: wts[n] for n in SMALL}
    x2, tgt = x[0], loss_target[0]
    T = x2.shape[0]
    tm = min(512, T)
    tl = min(1024, T)
    lg = sp["gate_norm_g"].reshape(A_GROUPS, CHUNK)
    lb = sp["gate_norm_b"].reshape(A_GROUPS, CHUNK)
    wsp = sp["w_spatial"].reshape(A_GROUPS, CHUNK, CHUNK)
    bs_col = sp["b_spatial"].reshape(A_GROUPS, CHUNK, 1)
    sinks = sp["attn_sinks"].reshape(1, B_HEADS)
    ga = sp["out_norm_a_g"].reshape(1, A_WIDTH)
    gb = sp["out_norm_b_g"].reshape(1, B_WIDTH)
    g1 = sp["mix_norm_g"].reshape(1, D_MODEL)
    g2 = sp["ffn_norm_g"].reshape(1, D_MODEL)
    gf = sp["final_norm_g"].reshape(1, D_MODEL)

    def owner_total(n, gh, others):
        return _owner_total(gh, others, tm=ROW_TILE[n], name="rs_owner_total_" + n)

    def halves_view(at, shards):
        return at.reshape(shards, 2, at.shape[0] // shards // 2, at.shape[1])

    for d in (wts, mom, var):
        d["w_in"] = jnp.swapaxes(d["w_in"], 1, 2)

    s_in = _cast_into_slot(wts["w_in"], tm=ROW_TILE["w_in"], name="cast_w_in")
    (s_out, s_up, s_down), ((g_in,),) = _cast_into_slots_carrying(
        [wts["w_out"], wts["w_up"], wts["w_down"]], steps=8, name="cast_w_rest",
        rides=[_ride_gather(s_in, chain=(0, 1, 1), chain_fracs=(0.3, 0.6))])
    win_t = g_in.reshape(PROJ_WIDTH, D_MODEL)
    bias = _bias_build(sp["rel_bias_table"])
    (n1, proj), ((g_out,), (s_up,)) = _norm_matmul_wide(
        x2, g1, win_t, tm=tm, tn=PROJ_WIDTH // 2, name="in_proj",
        rides=[_ride_gather(s_out, chain=(0, 1, 1), chain_fracs=(0.65, 0.85)), _ride_gather(s_up, s1=(0, 3, 8))])
    wo = g_out.reshape(A_WIDTH + B_WIDTH, D_MODEL)
    (mixed, mixed_t, ab), ((s_up,), (s_down,), (n1_sib,)) = _mixer_fwd(
        proj, lg, lb, wsp, bs_col, sinks, bias, ga, gb,
        rides=[_ride_gather(s_up, s2=(0, 3, 8), s1=(3, 8, 8)), _ride_gather(s_down, s1=(0, 2, 8)),
               _ride_to_sibling(n1, first=True)])
    mixed_t = halves_view(mixed_t, N_CHIPS)
    h1, ((wu,), (s_down,), (mixed_t_sib,)) = _matmul_res(
        mixed, wo, x2, tm=tl, tn=1024, tk=D_MODEL, prologue=_to_bf16, name="out_proj",
        rides=[_ride_gather(s_up, s3=(0, 3, 8), tail=(3, 8, 8), mid_frac=0.75), _ride_gather(s_down, s2=(0, 2, 8)),
               _ride_to_sibling(mixed_t, halves=True)])
    (n2t, zp, z2, z2t), ((g_down,),) = _norm_matmul_sq(
        h1, g2, wu, tm=tl, tn=1024, name="up_proj", rides=[_ride_gather(s_down, s3=(0, 2, 8), chain=(2, 8, 8), chain_fracs=(0.5, 0.8))])
    wd = g_down.reshape(D_FF, D_MODEL)
    n2t, z2t = halves_view(n2t, 1), halves_view(z2t, N_CHIPS)
    h2, ((n2t_sib,), (z2t_sib,)) = _matmul_res(
        z2, wd, h1, tm=tl, tn=1024, tk=4096, prologue=_to_bf16, name="down_proj",
        rides=[_ride_to_sibling(n2t, halves=True), _ride_to_sibling(z2t, halves=True)])

    dh2, dh2b, dgf, loss = _loss_bwd(h2, tgt, gf, tm=tm)
    dzp, ((dh2b_sib,),) = _matmul_nt(dh2b, wd, tm=tl, tn=1024, tk=D_MODEL, name="bwd_dz", extra=zp,
                                     epilogue=_sq_relu_grad, out_dtype=BF16, rides=[_ride_to_sibling(dh2b)])
    (gd, gdb), ((dzp_sib,),) = _grad_pair(z2t, z2t_sib, dh2b, dh2b_sib, cols_sharded=False, tmo=1024, tk=tl,
                                          name="grad_w_down", rides=[_ride_to_sibling(dzp)])
    (gu, gub), ((o_d,),) = _grad_pair(n2t, n2t_sib, dzp, dzp_sib, cols_sharded=True, tmo=1024, tk=tl,
                                      name="grad_w_up", rides=[_ride_scatter(gdb, None, (0, 7, 8))])
    dn2, ((o_d,), (o_u,)) = _matmul_nt(dzp, wu, tm=tl, tn=1024, tk=4096, name="bwd_dn2",
                                       rides=[_ride_scatter(gdb, o_d, (7, 8, 8)), _ride_scatter(gub, None, (0, 6, 8))])
    h_d = owner_total("w_down", gd, o_d)
    (dh1, dh1b, dg2), ((o_u,),) = _rms_bwd_res(dn2, h1, g2, dh2, tm=tm, name="ffn_norm_bwd",
                                               rides=[_ride_scatter(gub, o_u, (6, 7, 8))])
    dmixed, ((o_u,), (dh1b_sib,), (w_d,)) = _matmul_nt(
        dh1b, wo, tm=tl, tn=1024, tk=D_MODEL, name="bwd_dmixed",
        rides=[_ride_scatter(gub, o_u, (7, 8, 8)), _ride_to_sibling(dh1b), _ride_swap(h_d)])
    h_u = owner_total("w_up", gu, o_u)
    (go, gob), ((w_u,),) = _grad_pair_merged(mixed_t, mixed_t_sib, dh1b, dh1b_sib, tk=tl, name="grad_w_out",
                                             rides=[_ride_swap(h_u)])
    (duv, duv_t, dga, dwsp, dbs, dlg, dlb), ((o_o,),) = _gmlp_bwd(proj, ab, dmixed, ga, lg, lb, wsp, bs_col,
                                                                  rides=[_ride_scatter(gob)])
    h_o = owner_total("w_out", go, o_o)
    small = {"gate_norm_g": dlg, "gate_norm_b": dlb, "w_spatial": dwsp, "b_spatial": dbs, "out_norm_a_g": dga}
    hr_in = PROJ_WIDTH // N_CHIPS // 2
    (dq, dkv, dq_t, dkv_t, dgb, dsinks, dbias), ((slots_a,), (dproj_t_sib,)) = _attn_bwd(
        proj, ab, dmixed, gb, sinks, bias,
        rides=[_ride_small_to_all(_pack([small[n] for n in SMALL_A])), _ride_rows_to_sibling(duv_t, hr_in, 2, N_CHIPS)])
    dtable = _bias_grad(dbias)
    dproj_t = halves_view(jnp.concatenate([duv_t, dq_t, dkv_t], axis=0), N_CHIPS)
    ((dproj_t_sib,),) = _carrier([_ride_to_sibling(dproj_t, halves=True, shards=(2, N_CHIPS), land=dproj_t_sib)],
                                 name="trade_dproj_t")
    (gi, gib_near), ((w_o,),) = _grad_pair(
        dproj_t, dproj_t_sib, n1, n1_sib, cols_sharded=False, tmo=hr_in, tk=tl, name="grad_w_in_near", shards="near",
        rides=[_ride_swap(h_o)])
    (gi, gib_far), ((o_i,),) = _grad_pair(
        dproj_t, dproj_t_sib, n1, n1_sib, cols_sharded=False, tmo=hr_in, tk=tl, name="grad_w_in_far", shards="far",
        into=gi, rides=[_ride_scatter(gib_near, None, to=(0, 1))])
    dn1, ((o_i,),) = _matmul_parts([duv, dq, dkv], win_t, tm=tl, tn=1024, name="bwd_dn1",
                                   rides=[_ride_scatter(gib_far, o_i, to=(2,))])
    h_i = owner_total("w_in", gi, o_i)
    dx, dg1 = _rms_bwd_res(dn1, x2, g1, dh1, tm=tm, name="mix_norm_bwd", bf16_copy=False)
    small.update({"rel_bias_table": dtable.reshape(N_BUCKETS, B_HEADS), "mix_norm_g": dg1, "attn_sinks": dsinks,
                  "out_norm_b_g": dgb, "ffn_norm_g": dg2, "final_norm_g": dgf})
    (w_i,), (slots_b,) = _carrier([_ride_swap(h_i), _ride_small_to_all(_pack([small[n] for n in SMALL_B] + [loss]))],
                                  name="swap_w_in")

    out_g, out_d, out_m, out_v = {}, {}, {}, {}
    for n, h, s in zip(LARGE, [h_i, h_o, h_u, h_d], [w_i, w_o, w_u, w_d]):
        res = _adamw_halves(wts[n], h, s, mom[n], var[n], tm=ROW_TILE[n], name="adamw_" + n)
        if n == "w_in":
            res = [jnp.swapaxes(r, 1, 2) for r in res]
        out_g[n], out_d[n], out_m[n], out_v[n] = res
    for names, slots, tag in ((SMALL_A, slots_a, "a"), (SMALL_B, slots_b, "b")):
        extra = [jnp.zeros((1, 1), F32)] if tag == "b" else []
        like = [wts[n] for n in names] + extra
        res = _adamw_small(_pack(like), slots, _pack([mom[n] for n in names] + extra),
                           _pack([var[n] for n in names] + extra), name="adamw_small_" + tag)
        for store, packed in zip((out_g, out_d, out_m, out_v), res):
            for n, val in zip(names + ["loss"], _unpack(packed, like)):
                store[n] = val

    total = out_g["loss"][0, 0]
    return (total, dx[None], *[out_g[n] for n in WEIGHTS], *[out_d[n] for n in WEIGHTS],
            *[out_m[n] for n in WEIGHTS], *[out_v[n] for n in WEIGHTS])
```

```python
import math

import numpy as np
import jax
import jax.numpy as jnp
from jax import lax
from jax.experimental import pallas as pl
from jax.experimental.pallas import tpu as pltpu

F32 = jnp.float32
BF16 = jnp.bfloat16

D_MODEL = 2048
CHUNK = 128
A_GROUPS = 8
A_WIDTH = 1024
HEAD_DIM = 64
B_HEADS = 16
Q_PER_KV = 8
B_WIDTH = 1024
KV_WIDTH = 128
PROJ_WIDTH = 3328
D_FF = 8192
N_BUCKETS = 32
EPS = 1e-5
NEG = -1e30
SCALE = HEAD_DIM ** -0.5
N_CHIPS = 4
N_DEV = 8

ADAM_LR = 0.001
ADAM_B1 = 0.9
ADAM_B2 = 0.999
ADAM_EPS = 1e-08
ADAM_WD = 0.01
ADAM_STEP = 10

VMEM_LIMIT = 60 * 1024 * 1024
MESH = pl.DeviceIdType.MESH


def _bucket_thresholds():
    d = np.arange(CHUNK)
    n_exact = N_BUCKETS // 2
    relf = np.maximum(d, n_exact).astype(np.float64)
    large = n_exact + (np.log(relf / n_exact) / math.log(CHUNK / n_exact) * (N_BUCKETS - n_exact)).astype(np.int32)
    bucket = np.where(d < n_exact, d, np.minimum(large, N_BUCKETS - 1))
    return [int(np.min(d[bucket >= b])) for b in range(1, N_BUCKETS)]


BUCKET_THR = _bucket_thresholds()


def _params(sem=None):
    return pltpu.CompilerParams(dimension_semantics=sem, vmem_limit_bytes=VMEM_LIMIT)


def _gelu(x):
    c = math.sqrt(2.0 / math.pi)
    return 0.5 * x * (1.0 + jnp.tanh(c * (x + 0.044715 * (x * x * x))))


def _gelu_and_grad(x):
    c = math.sqrt(2.0 / math.pi)
    x2 = x * x
    t = jnp.tanh(c * (x + 0.044715 * (x2 * x)))
    g = 0.5 * x * (1.0 + t)
    dg = 0.5 * (1.0 + t) + 0.5 * x * (1.0 - t * t) * (c * (1.0 + 3.0 * 0.044715 * x2))
    return g, dg


def _dot(a, b):
    return jnp.dot(a, b, preferred_element_type=F32)


def _dot_nt(a, b):
    return lax.dot_general(a, b, (((1,), (1,)), ((), ())), preferred_element_type=F32)


def _dot_tn(a, b):
    return lax.dot_general(a, b, (((0,), (0,)), ((), ())), preferred_element_type=F32)


def _rms_bwd(dn, h, g):
    r = lax.rsqrt(jnp.mean(h * h, axis=-1, keepdims=True) + EPS)
    w = dn * g
    dh = r * w - h * ((r * r * r) * jnp.mean(w * h, axis=-1, keepdims=True))
    return dh, r


def _place():
    x, y, c = lax.axis_index("x"), lax.axis_index("y"), lax.axis_index("c")
    chips = [(1 - x, y), (x, 1 - y), (1 - x, 1 - y)]
    return x, y, c, chips


def _remote(src, dst, send_sem, recv_sem, to):
    return pltpu.make_async_remote_copy(src_ref=src, dst_ref=dst, send_sem=send_sem, recv_sem=recv_sem,
                                        device_id=to, device_id_type=MESH)


class _Ride:
    def __init__(self, args, out_shape, n_sem, start, finish, mids=(), aliases=None):
        self.args, self.out_shape, self.n_sem = list(args), list(out_shape), n_sem
        self.start, self.mids, self.finish = start, list(mids), finish
        self.aliases = dict(aliases or {})


def _call(body, *, name, grid, in_specs, out_specs, out_shape, scratch_shapes=(), sem=None, rides=(), aliases=None):
    single = not isinstance(out_shape, (list, tuple))
    out_specs = [out_specs] if single else list(out_specs)
    out_shape = [out_shape] if single else list(out_shape)
    n_in, n_out, n_scr = len(in_specs), len(out_shape), len(scratch_shapes)
    r_in = [len(r.args) for r in rides]
    r_out = [len(r.out_shape) for r in rides]
    any_spec = pl.BlockSpec(memory_space=pl.ANY)
    aliases, off_i, off_o = dict(aliases or {}), n_in, n_out
    for r in rides:
        for i, o in r.aliases.items():
            aliases[off_i + i] = off_o + o
        off_i += len(r.args)
        off_o += len(r.out_shape)
    steps = math.prod(grid)

    def wrapped(*refs):
        p = 0
        ins = refs[p:p + n_in]; p += n_in
        rins = refs[p:p + sum(r_in)]; p += sum(r_in)
        outs = refs[p:p + n_out]; p += n_out
        routs = refs[p:p + sum(r_out)]; p += sum(r_out)
        scr = refs[p:p + n_scr]; p += n_scr
        sems = refs[p:]
        parts, pi, po = [], 0, 0
        for k, r in enumerate(rides):
            parts.append((rins[pi:pi + r_in[k]], routs[po:po + r_out[k]], sems[2 * k], sems[2 * k + 1]))
            pi += r_in[k]
            po += r_out[k]
        lin = 0
        for d in range(len(grid)):
            lin = lin * grid[d] + pl.program_id(d)
        if rides:
            @pl.when(lin == 0)
            def _():
                for r, part in zip(rides, parts):
                    r.start(*part)
        body(*ins, *outs, *scr)
        for r, part in zip(rides, parts):
            for frac, fn in r.mids:
                @pl.when(lin == min(steps - 1, int(frac * steps)))
                def _(fn=fn, part=part):
                    fn(*part)
        if rides:
            @pl.when(lin == steps - 1)
            def _():
                for r, part in zip(rides, parts):
                    r.finish(*part)

    scratch = list(scratch_shapes)
    for r in rides:
        scratch += [pltpu.SemaphoreType.DMA((r.n_sem,)), pltpu.SemaphoreType.DMA((r.n_sem,))]
    if rides:
        sem = ("arbitrary",) * len(grid)
    res = pl.pallas_call(
        wrapped, name=name, grid=grid,
        in_specs=list(in_specs) + [any_spec] * sum(r_in),
        out_specs=out_specs + [any_spec] * sum(r_out),
        out_shape=out_shape + [s for r in rides for s in r.out_shape],
        scratch_shapes=scratch, input_output_aliases=aliases,
        compiler_params=_params(sem),
    )

    def run(*args):
        got = res(*args, *[a for r in rides for a in r.args])
        mine = got[0] if single else list(got[:n_out])
        if not rides:
            return mine
        rest, out = list(got[n_out:]), []
        for k in range(len(rides)):
            out.append(rest[:r_out[k]])
            rest = rest[r_out[k]:]
        return mine, out

    return run


def _ride_gather(slot, s1=None, s2=None, s3=None, tail=None, chain=None, mid_frac=0.6, chain_fracs=(0.35, 0.7)):
    half = slot.shape[1] // 2

    def rows(part, c, which=None):
        k0, k1, n = part
        count, first = (k1 - k0) * (half // n), c * half + k0 * (half // n)
        return pl.ds(first, count) if which is None else pl.ds(first + which * (count // 2), count // 2)

    def ids():
        x, y, c, _ = _place()
        return x, y, c, 2 * x + y, 2 * (1 - x) + y, 2 * x + (1 - y), 2 * (1 - x) + (1 - y)

    def copy(full, chip, r, ss, rs, k, to):
        piece = full.at[chip, r, :]
        return _remote(piece, piece, ss.at[k], rs.at[k], to)

    def to_neighbours(full, ss, rs, part, base):
        x, y, c, me, _, _, _ = ids()
        return [copy(full, me, rows(part, c), ss, rs, base, (1 - x, y, c)),
                copy(full, me, rows(part, c), ss, rs, base + 1, (x, 1 - y, c))]

    def from_neighbours(full, ss, rs, part, base):
        x, y, c, _, cx, cy, _ = ids()
        return [copy(full, cx, rows(part, c), ss, rs, base, (x, y, c)), copy(full, cy, rows(part, c), ss, rs, base + 1, (x, y, c))]

    def onward(full, ss, rs, part, base):
        x, y, c, _, cx, cy, _ = ids()
        return [copy(full, cx, rows(part, c, 0), ss, rs, base, (x, 1 - y, c)),
                copy(full, cy, rows(part, c, 1), ss, rs, base + 1, (1 - x, y, c))]

    def from_onward(full, ss, rs, part, base):
        x, y, c, _, _, _, cd = ids()
        return [copy(full, cd, rows(part, c, 0), ss, rs, base, (x, y, c)), copy(full, cd, rows(part, c, 1), ss, rs, base + 1, (x, y, c))]

    def to_sibling(full, ss, rs, part, base, diagonal):
        x, y, c, _, cx, cy, cd = ids()
        return [copy(full, chip, rows(part, c), ss, rs, base + j, (x, y, 1 - c))
                for j, chip in enumerate([cd] if diagonal else [cx, cy])]

    def from_sibling(full, ss, rs, part, base, diagonal):
        x, y, c, _, cx, cy, cd = ids()
        return [copy(full, chip, rows(part, 1 - c), ss, rs, base + j, (x, y, c))
                for j, chip in enumerate([cd] if diagonal else [cx, cy])]

    def start(ins, outs, ss, rs):
        full, cps = outs[0], []
        for part, base in ((s1, 0), (chain, 12)):
            if part is not None:
                cps += to_neighbours(full, ss, rs, part, base)
        for part, b_ici, b_sib in ((s2, 2, 4), (tail, 7, 9)):
            if part is not None:
                cps += onward(full, ss, rs, part, b_ici) + to_sibling(full, ss, rs, part, b_sib, False)
        if s3 is not None:
            cps += to_sibling(full, ss, rs, s3, 6, True)
        for cp in cps:
            cp.start()

    def second(part, b_in, b_ici, b_sib):
        def fn(ins, outs, ss, rs):
            for cp in from_neighbours(outs[0], ss, rs, part, b_in):
                cp.wait_recv()
            for cp in onward(outs[0], ss, rs, part, b_ici) + to_sibling(outs[0], ss, rs, part, b_sib, False):
                cp.start()
        return fn

    def third(part, b_ici, b_sib):
        def fn(ins, outs, ss, rs):
            for cp in from_onward(outs[0], ss, rs, part, b_ici):
                cp.wait_recv()
            for cp in to_sibling(outs[0], ss, rs, part, b_sib, True):
                cp.start()
        return fn

    mids = []
    if tail is not None:
        mids.append((mid_frac, third(tail, 7, 11)))
    if chain is not None:
        mids += [(chain_fracs[0], second(chain, 12, 14, 16)), (chain_fracs[1], third(chain, 14, 18))]

    def finish(ins, outs, ss, rs):
        full, got, sent = outs[0], [], []
        if s1 is not None:
            got += from_neighbours(full, ss, rs, s1, 0)
            sent += to_neighbours(full, ss, rs, s1, 0)
        if s2 is not None:
            got += from_onward(full, ss, rs, s2, 2) + from_sibling(full, ss, rs, s2, 4, False)
            sent += onward(full, ss, rs, s2, 2) + to_sibling(full, ss, rs, s2, 4, False)
        if s3 is not None:
            got += from_sibling(full, ss, rs, s3, 6, True)
            sent += to_sibling(full, ss, rs, s3, 6, True)
        if tail is not None:
            got += from_sibling(full, ss, rs, tail, 9, False) + from_sibling(full, ss, rs, tail, 11, True)
            sent += onward(full, ss, rs, tail, 7) + to_sibling(full, ss, rs, tail, 9, False) + to_sibling(full, ss, rs, tail, 11, True)
        if chain is not None:
            got += from_sibling(full, ss, rs, chain, 16, False) + from_sibling(full, ss, rs, chain, 18, True)
            sent += (to_neighbours(full, ss, rs, chain, 12) + onward(full, ss, rs, chain, 14)
                     + to_sibling(full, ss, rs, chain, 16, False) + to_sibling(full, ss, rs, chain, 18, True))
        for cp in got:
            cp.wait_recv()
        for cp in sent:
            cp.wait_send()

    return _Ride([slot], [jax.ShapeDtypeStruct(slot.shape, slot.dtype)], 19, start, finish, mids=mids, aliases={0: 0})


def _ride_scatter(q, land=None, part=(0, 1), to=(0, 1, 2)):
    k0, k1, n = part if len(part) == 3 else (part[0], part[0] + 1, part[1])
    rows_n = q.shape[1] // n
    rows = pl.ds(k0 * rows_n, (k1 - k0) * rows_n)

    def copies(ins, outs, ss, rs):
        x, y, c, chips = _place()
        return [_remote(ins[0].at[2 * chip[0] + chip[1], rows, :], outs[0].at[j, rows, :], ss.at[j], rs.at[j], (*chip, c))
                for j, chip in enumerate(chips) if j in to]

    def start(*a):
        for cp in copies(*a):
            cp.start()

    def finish(*a):
        for cp in copies(*a):
            cp.wait()

    shape = jax.ShapeDtypeStruct((3,) + q.shape[1:], q.dtype)
    if land is None:
        return _Ride([q], [shape], 3, start, finish)
    return _Ride([q, land], [shape], 3, start, finish, aliases={1: 0})


def _ride_to_sibling(a, halves=False, first=False, shards=None, land=None):
    s0, s1 = shards or (0, a.shape[0])

    def copy(ins, outs, ss, rs):
        x, y, c, _ = _place()
        if halves:
            src, dst = ins[0].at[s0:s1, 1 - c], outs[0].at[s0:s1]
        else:
            src, dst = (ins[0].at[0] if first else ins[0]), outs[0]
        return _remote(src, dst, ss.at[0], rs.at[0], (x, y, 1 - c))

    shape = (a.shape[0],) + a.shape[2:] if halves else (a.shape[1:] if first else a.shape)
    return _Ride([a] if land is None else [a, land], [jax.ShapeDtypeStruct(shape, a.dtype)], 1,
                 lambda *a_: copy(*a_).start(), lambda *a_: copy(*a_).wait(), aliases=None if land is None else {1: 0})


def _ride_rows_to_sibling(a, hr, shards, total):
    def copies(ins, outs, ss, rs):
        x, y, c, _ = _place()
        return [_remote(ins[0].at[pl.ds((2 * s + 1 - c) * hr, hr), :], outs[0].at[s], ss.at[s], rs.at[s], (x, y, 1 - c))
                for s in range(shards)]

    def start(*a_):
        for cp in copies(*a_):
            cp.start()

    def finish(*a_):
        for cp in copies(*a_):
            cp.wait()

    return _Ride([a], [jax.ShapeDtypeStruct((total, hr, a.shape[1]), a.dtype)], shards, start, finish)


def _ride_swap(h):
    def copy(ins, outs, ss, rs):
        x, y, c, _ = _place()
        return _remote(ins[0], outs[0], ss.at[0], rs.at[0], (x, y, 1 - c))

    return _Ride([h], [jax.ShapeDtypeStruct(h.shape, h.dtype)], 1,
                 lambda *a: copy(*a).start(), lambda *a: copy(*a).wait())


def _mesh_place(p):
    return (p // 4, (p // 2) % 2, p % 2)


def _ride_small_to_all(packed):
    def copies(ins, outs, ss, rs):
        x, y, c, _ = _place()
        me = 4 * x + 2 * y + c
        return [_remote(ins[0], outs[0].at[me], ss.at[k - 1], rs.at[k - 1], _mesh_place((me + k) % N_DEV))
                for k in range(1, N_DEV)]

    def own(ins, outs, ss, rs):
        x, y, c, _ = _place()
        return pltpu.make_async_copy(ins[0], outs[0].at[4 * x + 2 * y + c], ss.at[N_DEV - 1])

    def start(*a):
        own(*a).start()
        for cp in copies(*a):
            cp.start()

    def finish(ins, outs, ss, rs):
        x, y, c, _ = _place()
        me = 4 * x + 2 * y + c
        for k in range(1, N_DEV):
            _remote(ins[0], outs[0].at[(me + N_DEV - k) % N_DEV], ss.at[k - 1], rs.at[k - 1], (x, y, c)).wait_recv()
        for cp in copies(ins, outs, ss, rs):
            cp.wait_send()
        own(ins, outs, ss, rs).wait()

    return _Ride([packed], [jax.ShapeDtypeStruct((N_DEV,) + packed.shape, packed.dtype)], N_DEV, start, finish)


def _carrier(rides, *, name):
    _, outs = _call(lambda: None, name=name, grid=(1,), in_specs=[], out_specs=[], out_shape=[], rides=rides)()
    return outs


def _norm_bf16(a_ref, g_ref):
    xf = a_ref[...]
    r = lax.rsqrt(jnp.mean(xf * xf, axis=-1, keepdims=True) + EPS)
    return ((xf * r) * g_ref[...]).astype(BF16)


def _norm_matmul_wide(a, g, b, *, tm, tn, name, rides=()):
    T, K = a.shape
    N = b.shape[0]

    def body(a_ref, g_ref, b_ref, n_ref, o_ref):
        n = _norm_bf16(a_ref, g_ref)
        n_ref[...] = n
        o_ref[...] = _dot_nt(n, b_ref[...])

    return _call(
        body, name=name, grid=(N // tn, T // tm),
        in_specs=[pl.BlockSpec((tm, K), lambda j, i: (i, 0)), pl.BlockSpec((1, K), lambda j, i: (0, 0)),
                  pl.BlockSpec((tn, K), lambda j, i: (j, 0))],
        out_specs=[pl.BlockSpec((None, tm, K), lambda j, i: (j, i, 0)), pl.BlockSpec((tm, tn), lambda j, i: (i, j))],
        out_shape=[jax.ShapeDtypeStruct((N // tn, T, K), BF16), jax.ShapeDtypeStruct((T, N), F32)],
        sem=("arbitrary", "arbitrary"), rides=rides,
    )(a, g, b)


def _norm_matmul_sq(a, g, b, *, tm, tn, name, rides=()):
    T, K = a.shape
    per = b.shape[2] // tn
    N = b.shape[0] * b.shape[2]

    def body(a_ref, g_ref, b_ref, nt_ref, o_ref, z_ref, zt_ref, n_scr):
        @pl.when(pl.program_id(1) == 0)
        def _():
            n = _norm_bf16(a_ref, g_ref)
            n_scr[...] = n
            nt_ref[...] = n.T
        r = jnp.maximum(_dot(n_scr[...], b_ref[...]), 0.0)
        o_ref[...] = r.astype(BF16)
        z = (r * r).astype(BF16)
        z_ref[...] = z
        zt_ref[...] = z.T

    return _call(
        body, name=name, grid=(T // tm, N // tn),
        in_specs=[pl.BlockSpec((tm, K), lambda i, j: (i, 0)), pl.BlockSpec((1, K), lambda i, j: (0, 0)),
                  pl.BlockSpec((None, K, tn), lambda i, j: (j // per, 0, j % per))],
        out_specs=[pl.BlockSpec((K, tm), lambda i, j: (0, i)), pl.BlockSpec((tm, tn), lambda i, j: (i, j)),
                   pl.BlockSpec((tm, tn), lambda i, j: (i, j)), pl.BlockSpec((tn, tm), lambda i, j: (j, i))],
        out_shape=[jax.ShapeDtypeStruct((K, T), BF16), jax.ShapeDtypeStruct((T, N), BF16),
                   jax.ShapeDtypeStruct((T, N), BF16), jax.ShapeDtypeStruct((N, T), BF16)],
        scratch_shapes=[pltpu.VMEM((tm, K), BF16)],
        sem=("parallel", "arbitrary"), rides=rides,
    )(a, g, b)


def _grad_pair(at, at_sib, b, b_sib, *, cols_sharded, tmo, tk, name, shards=None, into=None, rides=()):
    S, _, hr, T = at.shape
    C = b.shape[-1] // N_CHIPS if cols_sharded else b.shape[-1]
    nk = T // tk

    def shard(s):
        if shards is None:
            return s
        x, y = lax.axis_index("x"), lax.axis_index("y")
        first, second = ((2 * (1 - x) + y, 2 * x + (1 - y)) if shards == "near" else (2 * (1 - x) + (1 - y), 2 * x + y))
        return jnp.where(s == 0, first, second)

    a_sel = (lambda s: 0) if cols_sharded else shard
    b_sel = shard if cols_sharded else (lambda s: 0)
    if b.ndim == 3:
        b_spec = pl.BlockSpec((None, tk, C), lambda s, i, k: (0, k, b_sel(s)))
    else:
        b_spec = pl.BlockSpec((tk, C), lambda s, i, k: (k, b_sel(s)))
    n_into = 0 if into is None else 1

    def body(a_ref, as_ref, b_ref, bs_ref, *rest):
        o_ref, ob_ref = rest[n_into:]
        k = pl.program_id(2)
        p = _dot(a_ref[...], b_ref[...]) + _dot(as_ref[...], bs_ref[...])

        @pl.when(k == 0)
        def _():
            o_ref[...] = p

        @pl.when(k > 0)
        def _():
            o_ref[...] += p

        @pl.when(k == nk - 1)
        def _():
            ob_ref[...] = o_ref[...].astype(BF16)

    out = pl.BlockSpec((None, tmo, C), lambda s, i, k: (shard(s), i, 0))
    held = [pl.BlockSpec(memory_space=pl.ANY)] * n_into
    return _call(
        body, name=name, grid=(N_CHIPS if shards is None else 2, hr // tmo, nk),
        in_specs=[pl.BlockSpec((None, None, tmo, tk), lambda s, i, k: (a_sel(s), lax.axis_index("c"), i, k)),
                  pl.BlockSpec((None, tmo, tk), lambda s, i, k: (a_sel(s), i, k)),
                  b_spec, pl.BlockSpec((tk, C), lambda s, i, k: (k, b_sel(s)))] + held,
        out_specs=[out, out],
        out_shape=[jax.ShapeDtypeStruct((N_CHIPS, hr, C), F32), jax.ShapeDtypeStruct((N_CHIPS, hr, C), BF16)],
        sem=("parallel", "parallel", "arbitrary"), rides=rides, aliases={4: 0} if into is not None else None,
    )(at, at_sib, b, b_sib, *([into] if into is not None else []))


def _grad_pair_merged(at, at_sib, b, b_sib, *, tk, name, rides=()):
    S, _, hr, T = at.shape
    C = b.shape[-1]
    nk = T // tk

    def body(a_ref, as_ref, b_ref, bs_ref, o_ref, ob_ref):
        k = pl.program_id(0)
        p = (_dot(a_ref[...].reshape(S * hr, tk), b_ref[...])
             + _dot(as_ref[...].reshape(S * hr, tk), bs_ref[...])).reshape(S, hr, C)

        @pl.when(k == 0)
        def _():
            o_ref[...] = p

        @pl.when(k > 0)
        def _():
            o_ref[...] += p

        @pl.when(k == nk - 1)
        def _():
            ob_ref[...] = o_ref[...].astype(BF16)

    out = pl.BlockSpec((S, hr, C), lambda k: (0, 0, 0))
    return _call(
        body, name=name, grid=(nk,),
        in_specs=[pl.BlockSpec((S, None, hr, tk), lambda k: (0, lax.axis_index("c"), 0, k)),
                  pl.BlockSpec((S, hr, tk), lambda k: (0, 0, k)),
                  pl.BlockSpec((tk, C), lambda k: (k, 0)), pl.BlockSpec((tk, C), lambda k: (k, 0))],
        out_specs=[out, out],
        out_shape=[jax.ShapeDtypeStruct((S, hr, C), F32), jax.ShapeDtypeStruct((S, hr, C), BF16)],
        sem=("arbitrary",), rides=rides,
    )(at, at_sib, b, b_sib)


def _matmul_parts(parts, b, *, tm, tn, name, rides=()):
    T = parts[0].shape[0]
    N = b.shape[1]
    offs = [sum(p.shape[1] for p in parts[:i]) for i in range(len(parts))]
    assert all(o % p.shape[1] == 0 for o, p in zip(offs, parts))

    def body(*refs):
        n = len(parts)
        acc = _dot(refs[0][...], refs[n][...])
        for i in range(1, n):
            acc = acc + _dot(refs[i][...], refs[n + i][...])
        refs[-1][...] = acc

    a_specs = [pl.BlockSpec((tm, p.shape[1]), lambda i, j: (i, 0)) for p in parts]
    b_specs = [pl.BlockSpec((p.shape[1], tn), lambda i, j, r=o // p.shape[1]: (r, j)) for o, p in zip(offs, parts)]
    return _call(
        body, name=name, grid=(T // tm, N // tn), in_specs=a_specs + b_specs,
        out_specs=pl.BlockSpec((tm, tn), lambda i, j: (i, j)), out_shape=jax.ShapeDtypeStruct((T, N), F32),
        sem=("parallel", "parallel"), rides=rides,
    )(*parts, *([b] * len(parts)))


def _to_bf16(v):
    return v.astype(BF16)


def _matmul_res(a, b, res, *, tm, tn, tk, prologue, name, rides=()):
    T, K = a.shape
    N = b.shape[1]

    def body(a_ref, b_ref, res_ref, o_ref):
        k = pl.program_id(2)
        p = _dot(prologue(a_ref[...]), b_ref[...])

        @pl.when(k == 0)
        def _():
            o_ref[...] = res_ref[...] + p

        @pl.when(k > 0)
        def _():
            o_ref[...] += p

    return _call(
        body, name=name, grid=(T // tm, N // tn, K // tk),
        in_specs=[pl.BlockSpec((tm, tk), lambda i, j, k: (i, k)), pl.BlockSpec((tk, tn), lambda i, j, k: (k, j)),
                  pl.BlockSpec((tm, tn), lambda i, j, k: (i, j))],
        out_specs=pl.BlockSpec((tm, tn), lambda i, j, k: (i, j)),
        out_shape=jax.ShapeDtypeStruct((T, N), F32),
        sem=("parallel", "parallel", "arbitrary"), rides=rides,
    )(a, b, res)


def _matmul_nt(a, b, *, tm, tn, tk, name, extra=None, epilogue=None, out_dtype=F32, rides=()):
    T, K = a.shape
    two = b.ndim == 3 and tk == 2 * b.shape[2]
    if two:
        N, ks = b.shape[1], b.shape[2]
        b_specs = [pl.BlockSpec((None, tn, ks), lambda i, j, k: (2 * k, j, 0)),
                   pl.BlockSpec((None, tn, ks), lambda i, j, k: (2 * k + 1, j, 0))]
    elif b.ndim == 3:
        per = b.shape[2] // tk
        N = b.shape[1]
        b_specs = [pl.BlockSpec((None, tn, tk), lambda i, j, k: (k // per, j, k % per))]
    else:
        N = b.shape[0]
        b_specs = [pl.BlockSpec((tn, tk), lambda i, j, k: (j, k))]
    nb = len(b_specs)
    nk = K // tk
    assert out_dtype == F32 or nk == 1
    in_specs = [pl.BlockSpec((tm, tk), lambda i, j, k: (i, k))] + b_specs
    args = [a] + [b] * nb
    if extra is not None:
        in_specs.append(pl.BlockSpec((tm, tn), lambda i, j, k: (i, j)))
        args.append(extra)

    def body(*refs):
        a_ref, b_ref = refs[0], refs[1]
        o_ref = refs[-1]
        if two:
            p = (_dot_nt(a_ref[:, :tk // 2].astype(BF16), refs[1][...])
                 + _dot_nt(a_ref[:, tk // 2:].astype(BF16), refs[2][...]))
        else:
            p = _dot_nt(a_ref[...].astype(BF16), b_ref[...])
        if nk == 1:
            if epilogue is not None:
                p = epilogue(p, refs[1 + nb][...])
            o_ref[...] = p.astype(out_dtype)
        else:
            k = pl.program_id(2)

            @pl.when(k == 0)
            def _():
                o_ref[...] = p

            @pl.when(k > 0)
            def _():
                o_ref[...] += p

    return _call(
        body, name=name, grid=(T // tm, N // tn, nk),
        in_specs=in_specs,
        out_specs=pl.BlockSpec((tm, tn), lambda i, j, k: (i, j)),
        out_shape=jax.ShapeDtypeStruct((T, N), out_dtype),
        sem=("parallel", "parallel", "arbitrary"), rides=rides,
    )(*args)


def _loss_bwd(h2, tgt, g, *, tm):
    T, D = h2.shape

    def body(h_ref, t_ref, g_ref, dh_ref, dhb_ref, dg_ref, loss_ref):
        @pl.when(pl.program_id(0) == 0)
        def _():
            dg_ref[...] = jnp.zeros_like(dg_ref)
            loss_ref[...] = jnp.zeros_like(loss_ref)
        h = h_ref[...]
        gg = g_ref[...]
        r = lax.rsqrt(jnp.mean(h * h, axis=-1, keepdims=True) + EPS)
        hn = h * r
        err = hn * gg - t_ref[...]
        loss_ref[...] += 0.5 * jnp.sum(jnp.mean(err * err, axis=-1, keepdims=True), axis=0, keepdims=True)
        dy = err * (1.0 / D)
        dg_ref[...] += jnp.sum(dy * hn, axis=0, keepdims=True)
        w = dy * gg
        dh = r * w - h * ((r * r * r) * jnp.mean(w * h, axis=-1, keepdims=True))
        dh_ref[...] = dh
        dhb_ref[...] = dh.astype(BF16)

    tile = pl.BlockSpec((tm, D), lambda i: (i, 0))
    return pl.pallas_call(
        body, name="loss_bwd", grid=(T // tm,),
        in_specs=[tile, tile, pl.BlockSpec((1, D), lambda i: (0, 0))],
        out_specs=[tile, tile, pl.BlockSpec((1, D), lambda i: (0, 0)), pl.BlockSpec((1, 1), lambda i: (0, 0))],
        out_shape=[jax.ShapeDtypeStruct((T, D), F32), jax.ShapeDtypeStruct((T, D), BF16),
                   jax.ShapeDtypeStruct((1, D), F32), jax.ShapeDtypeStruct((1, 1), F32)],
        compiler_params=_params(("arbitrary",)),
    )(h2, tgt, g)


def _rms_bwd_res(dn, h, g, dres, *, tm, name, bf16_copy=True, rides=()):
    T, D = h.shape

    def body(dn_ref, h_ref, g_ref, dres_ref, dh_ref, *rest):
        dg_ref = rest[-1]

        @pl.when(pl.program_id(0) == 0)
        def _():
            dg_ref[...] = jnp.zeros_like(dg_ref)
        h_ = h_ref[...]
        dn_ = dn_ref[...]
        dh, r = _rms_bwd(dn_, h_, g_ref[...])
        dg_ref[...] += jnp.sum(dn_ * (h_ * r), axis=0, keepdims=True)
        dh = dres_ref[...] + dh
        dh_ref[...] = dh
        if bf16_copy:
            rest[0][...] = dh.astype(BF16)

    tile = pl.BlockSpec((tm, D), lambda i: (i, 0))
    row = pl.BlockSpec((1, D), lambda i: (0, 0))
    copy_spec = [tile] if bf16_copy else []
    copy_shape = [jax.ShapeDtypeStruct((T, D), BF16)] if bf16_copy else []
    return _call(
        body, name=name, grid=(T // tm,),
        in_specs=[tile, tile, row, tile], out_specs=[tile] + copy_spec + [row],
        out_shape=[jax.ShapeDtypeStruct((T, D), F32)] + copy_shape + [jax.ShapeDtypeStruct((1, D), F32)],
        sem=("arbitrary",), rides=rides,
    )(dn, h, g, dres)


def _rel_distance():
    i = lax.broadcasted_iota(jnp.int32, (CHUNK, 2 * CHUNK), 0)
    j = lax.broadcasted_iota(jnp.int32, (CHUNK, 2 * CHUNK), 1)
    return i + CHUNK - j


def _bias_build(table):
    def body(tab_ref, o_ref):
        rel = _rel_distance()
        j = lax.broadcasted_iota(jnp.int32, (CHUNK, 2 * CHUNK), 1)
        band = (rel >= 0) & (rel < CHUNK)
        ge = [rel >= t for t in BUCKET_THR]
        for h in range(B_HEADS):
            cur = jnp.full((CHUNK, 2 * CHUNK), tab_ref[0, h], F32)
            for b in range(1, N_BUCKETS):
                cur = jnp.where(ge[b - 1], tab_ref[b, h], cur)
            o_ref[0, h] = jnp.where(band & (j >= CHUNK), cur, NEG)
            o_ref[1, h] = jnp.where(band, cur, NEG)

    return pl.pallas_call(
        body, name="bias_build",
        in_specs=[pl.BlockSpec(memory_space=pltpu.SMEM)],
        out_specs=pl.BlockSpec(memory_space=pltpu.VMEM),
        out_shape=jax.ShapeDtypeStruct((2, B_HEADS, CHUNK, 2 * CHUNK), F32),
    )(table)


def _bias_grad(dbias):
    def body(db_ref, o_ref, acc_ref):
        rel = _rel_distance()
        lo = [0] + BUCKET_THR
        hi = BUCKET_THR + [CHUNK]
        for b in range(N_BUCKETS):
            m = (rel >= lo[b]) & (rel < hi[b])
            for h in range(B_HEADS):
                row = b * B_HEADS + h
                acc_ref[row:row + 1, :] = jnp.sum(jnp.where(m, db_ref[h], 0.0), axis=0, keepdims=True)
        o_ref[...] = jnp.sum(acc_ref[...], axis=1, keepdims=True)

    return pl.pallas_call(
        body, name="bias_grad",
        in_specs=[pl.BlockSpec(memory_space=pltpu.VMEM)],
        out_specs=pl.BlockSpec(memory_space=pltpu.VMEM),
        out_shape=jax.ShapeDtypeStruct((N_BUCKETS * B_HEADS, 1), F32),
        scratch_shapes=[pltpu.VMEM((N_BUCKETS * B_HEADS, 2 * CHUNK), F32)],
    )(dbias)


def _causal_mask():
    t = lax.broadcasted_iota(jnp.int32, (CHUNK, CHUNK), 0)
    s = lax.broadcasted_iota(jnp.int32, (CHUNK, CHUNK), 1)
    return s <= t


def _gate_forward(u, v, lg, lb, wc, bs):
    ug = _gelu(u)
    vg = _gelu(v)
    mu = jnp.mean(vg, axis=-1, keepdims=True)
    xc = vg - mu
    rstd = lax.rsqrt(jnp.mean(xc * xc, axis=-1, keepdims=True) + EPS)
    xhat = xc * rstd
    vl = (xhat * lg + lb).astype(BF16)
    mixed = _dot(wc, vl) + bs
    return ug, xhat, rstd, vl, mixed


def _softmax_scores(qk, bias, sink):
    s = qk + bias
    m = jnp.maximum(jnp.max(s, axis=-1, keepdims=True), sink)
    p = jnp.exp(s - m)
    e_sink = jnp.exp(sink - m)
    inv = 1.0 / (jnp.sum(p, axis=-1, keepdims=True) + e_sink)
    return p * inv, e_sink * inv


PAIRS = Q_PER_KV // 2


def _head(g, pr, e):
    return g * Q_PER_KV + 2 * pr + e


def _stack_pairs(ref, g, col0=0):
    w = 2 * HEAD_DIM
    return jnp.concatenate([ref[:, col0 + (g * PAIRS + pr) * w:col0 + (g * PAIRS + pr + 1) * w] for pr in range(PAIRS)],
                           axis=0)


def _low_lanes():
    return lax.broadcasted_iota(jnp.int32, (2 * CHUNK, 2 * HEAD_DIM), 1) < HEAD_DIM


def _band_operands(kv_prev, kv_cur):
    band = jnp.concatenate([kv_prev, kv_cur], axis=0)
    low = _low_lanes()
    ops = []
    for cat in (band[:, :KV_WIDTH], band[:, KV_WIDTH:]):
        rol = pltpu.roll(cat, HEAD_DIM, 1)
        ops.append([[jnp.where(low if e == 0 else ~low, cat if g == e else rol, 0.0).astype(BF16) for e in range(2)]
                    for g in range(2)])
    return ops


def _mixer_fwd(proj, lg, lb, wsp, bs_col, sinks, bias, ga, gb, rides=()):
    T = proj.shape[0]
    nb = T // CHUNK

    def body(u_ref, v_ref, q_ref, kvc_ref, kvp_ref, lg_ref, lb_ref, w_ref, bs_ref, sink_ref, bias_ref,
             ga_ref, gb_ref, mixed_ref, mixed_t_ref, ab_ref):
        causal = _causal_mask()
        ssq = jnp.zeros((CHUNK, 1), F32)
        for g in range(A_GROUPS):
            cols = slice(g * CHUNK, (g + 1) * CHUNK)
            wc = jnp.where(causal, w_ref[g], 0.0).astype(BF16)
            ug, _, _, _, mixed = _gate_forward(u_ref[:, cols], v_ref[:, cols], lg_ref[g:g + 1, :], lb_ref[g:g + 1, :],
                                               wc, bs_ref[g])
            a = ug * mixed
            ab_ref[:, cols] = a
            ssq = ssq + jnp.sum(a * a, axis=-1, keepdims=True)
        ra = lax.rsqrt(ssq * (1.0 / A_WIDTH) + EPS)
        mixed_ref[:, :A_WIDTH] = ((ab_ref[:, :A_WIDTH] * ra) * ga_ref[...]).astype(BF16)

        kops, vops = _band_operands(kvp_ref[...], kvc_ref[...])
        ssq = jnp.zeros((CHUNK, 1), F32)
        for g in range(B_HEADS // Q_PER_KV):
            qst = (_stack_pairs(q_ref, g) * SCALE).astype(BF16)
            o_st = jnp.zeros((PAIRS * CHUNK, 2 * HEAD_DIM), F32)
            for e in range(2):
                s_all = _dot_nt(qst, kops[g][e])
                ps = []
                for pr in range(PAIRS):
                    h = _head(g, pr, e)
                    p, _ = _softmax_scores(s_all[pr * CHUNK:(pr + 1) * CHUNK], bias_ref[h], sink_ref[0, h])
                    ps.append(p.astype(BF16))
                o_st = o_st + _dot(jnp.concatenate(ps, axis=0), vops[g][e])
            for pr in range(PAIRS):
                o = o_st[pr * CHUNK:(pr + 1) * CHUNK]
                c0 = A_WIDTH + (g * PAIRS + pr) * 2 * HEAD_DIM
                ab_ref[:, c0:c0 + 2 * HEAD_DIM] = o
                ssq = ssq + jnp.sum(o * o, axis=-1, keepdims=True)
        rb = lax.rsqrt(ssq * (1.0 / B_WIDTH) + EPS)
        mixed_ref[:, A_WIDTH:] = ((ab_ref[:, A_WIDTH:] * rb) * gb_ref[...]).astype(BF16)
        mixed_t_ref[...] = mixed_ref[...].T

    full = lambda *shape: pl.BlockSpec(shape, lambda n: (0,) * len(shape))
    return _call(
        body, name="mixer_fwd", grid=(nb,),
        in_specs=[pl.BlockSpec((CHUNK, A_WIDTH), lambda n: (n, 0)),
                  pl.BlockSpec((CHUNK, A_WIDTH), lambda n: (n, 1)),
                  pl.BlockSpec((CHUNK, B_WIDTH), lambda n: (n, 2)),
                  pl.BlockSpec((CHUNK, 2 * KV_WIDTH), lambda n: (n, 12)),
                  pl.BlockSpec((CHUNK, 2 * KV_WIDTH), lambda n: (jnp.maximum(n - 1, 0), 12)),
                  full(A_GROUPS, CHUNK), full(A_GROUPS, CHUNK), full(A_GROUPS, CHUNK, CHUNK), full(A_GROUPS, CHUNK, 1),
                  pl.BlockSpec(memory_space=pltpu.SMEM),
                  pl.BlockSpec((None, B_HEADS, CHUNK, 2 * CHUNK), lambda n: (jnp.minimum(n, 1), 0, 0, 0)),
                  full(1, A_WIDTH), full(1, B_WIDTH)],
        out_specs=[pl.BlockSpec((CHUNK, D_MODEL), lambda n: (n, 0)), pl.BlockSpec((D_MODEL, CHUNK), lambda n: (0, n)),
                   pl.BlockSpec((CHUNK, D_MODEL), lambda n: (n, 0))],
        out_shape=[jax.ShapeDtypeStruct((T, D_MODEL), BF16), jax.ShapeDtypeStruct((D_MODEL, T), BF16),
                   jax.ShapeDtypeStruct((T, D_MODEL), F32)],
        sem=("parallel",), rides=rides,
    )(proj, proj, proj, proj, proj, lg, lb, wsp, bs_col, sinks, bias, ga, gb)


def _gmlp_bwd(proj, ab, dmixed, ga, lg, lb, wsp, bs_col, rides=()):
    T = proj.shape[0]
    nb = T // CHUNK

    def body(u_ref, v_ref, a_ref, dna_ref, ga_ref, lg_ref, lb_ref, w_ref, bs_ref,
             dp_ref, dpt_ref, dga_ref, dw_ref, dbs_ref, dlg_ref, dlb_ref):
        @pl.when(pl.program_id(0) == 0)
        def _():
            for r in (dga_ref, dw_ref, dbs_ref, dlg_ref, dlb_ref):
                r[...] = jnp.zeros_like(r)
        causal = _causal_mask()
        a_all = a_ref[...]
        dna = dna_ref[...]
        da_all, ra = _rms_bwd(dna, a_all, ga_ref[...])
        dga_ref[...] += jnp.sum(dna * (a_all * ra), axis=0, keepdims=True)
        for g in range(A_GROUPS):
            cols = slice(g * CHUNK, (g + 1) * CHUNK)
            wc = jnp.where(causal, w_ref[g], 0.0).astype(BF16)
            lgg = lg_ref[g:g + 1, :]
            u = u_ref[:, cols]
            v = v_ref[:, cols]
            ug, xhat, rstd, vl, mixed = _gate_forward(u, v, lgg, lb_ref[g:g + 1, :], wc, bs_ref[g])
            da = da_all[:, cols]
            dug = da * mixed
            dmg = da * ug
            dmg_b = dmg.astype(BF16)
            dbs_ref[g] += jnp.sum(dmg, axis=-1, keepdims=True)
            dw_ref[g] += jnp.where(causal, _dot_nt(dmg_b, vl), 0.0)
            dvl = _dot_tn(wc, dmg_b)
            dlg_ref[g:g + 1, :] += jnp.sum(dvl * xhat, axis=0, keepdims=True)
            dlb_ref[g:g + 1, :] += jnp.sum(dvl, axis=0, keepdims=True)
            dxh = dvl * lgg
            dvg = rstd * (dxh - jnp.mean(dxh, axis=-1, keepdims=True)
                          - xhat * jnp.mean(dxh * xhat, axis=-1, keepdims=True))
            _, gu = _gelu_and_grad(u)
            _, gv = _gelu_and_grad(v)
            dp_ref[:, cols] = (dug * gu).astype(BF16)
            dp_ref[:, A_WIDTH + g * CHUNK:A_WIDTH + (g + 1) * CHUNK] = (dvg * gv).astype(BF16)
        dpt_ref[...] = dp_ref[...].T

    full = lambda *shape: pl.BlockSpec(shape, lambda n: (0,) * len(shape))
    return _call(
        body, name="gmlp_bwd", grid=(nb,),
        in_specs=[pl.BlockSpec((CHUNK, A_WIDTH), lambda n: (n, 0)),
                  pl.BlockSpec((CHUNK, A_WIDTH), lambda n: (n, 1)),
                  pl.BlockSpec((CHUNK, A_WIDTH), lambda n: (n, 0)),
                  pl.BlockSpec((CHUNK, A_WIDTH), lambda n: (n, 0)),
                  full(1, A_WIDTH), full(A_GROUPS, CHUNK), full(A_GROUPS, CHUNK), full(A_GROUPS, CHUNK, CHUNK),
                  full(A_GROUPS, CHUNK, 1)],
        out_specs=[pl.BlockSpec((CHUNK, 2 * A_WIDTH), lambda n: (n, 0)), pl.BlockSpec((2 * A_WIDTH, CHUNK), lambda n: (0, n)),
                   full(1, A_WIDTH), full(A_GROUPS, CHUNK, CHUNK), full(A_GROUPS, CHUNK, 1),
                   full(A_GROUPS, CHUNK), full(A_GROUPS, CHUNK)],
        out_shape=[jax.ShapeDtypeStruct((T, 2 * A_WIDTH), BF16), jax.ShapeDtypeStruct((2 * A_WIDTH, T), BF16),
                   jax.ShapeDtypeStruct((1, A_WIDTH), F32), jax.ShapeDtypeStruct((A_GROUPS, CHUNK, CHUNK), F32),
                   jax.ShapeDtypeStruct((A_GROUPS, CHUNK, 1), F32), jax.ShapeDtypeStruct((A_GROUPS, CHUNK), F32),
                   jax.ShapeDtypeStruct((A_GROUPS, CHUNK), F32)],
        sem=("arbitrary",), rides=rides,
    )(proj, proj, ab, dmixed, ga, lg, lb, wsp, bs_col)


def _attn_bwd(proj, ab, dmixed, gb, sinks, bias, rides=()):
    T = proj.shape[0]
    nb = T // CHUNK
    qn = lambda n: jnp.minimum(n, nb - 1)

    def body(q_ref, kvc_ref, kvp_ref, o_ref, dnb_ref, gb_ref, sink_ref, bias_ref,
             dq_ref, dkv_ref, dqt_ref, dkvt_ref, dgb_ref, dsink_ref, dbias_ref, carry_ref, sacc_ref):
        n = pl.program_id(0)

        @pl.when(n == 0)
        def _():
            carry_ref[...] = jnp.zeros_like(carry_ref)
            sacc_ref[...] = jnp.zeros_like(sacc_ref)
            dgb_ref[...] = jnp.zeros_like(dgb_ref)
            dbias_ref[...] = jnp.zeros_like(dbias_ref)

        @pl.when(n < nb)
        def _():
            o_all = o_ref[...]
            dnb = dnb_ref[...]
            do_all, rb = _rms_bwd(dnb, o_all, gb_ref[...])
            dgb_ref[...] += jnp.sum(dnb * (o_all * rb), axis=0, keepdims=True)
            kops, vops = _band_operands(kvp_ref[...], kvc_ref[...])
            low = _low_lanes()
            halves = []
            for g in range(B_HEADS // Q_PER_KV):
                qst = (_stack_pairs(q_ref, g) * SCALE).astype(BF16)
                dost = _stack_pairs(do_all, g).astype(BF16)
                dq_st = jnp.zeros((PAIRS * CHUNK, 2 * HEAD_DIM), F32)
                dk_e, dv_e = [], []
                for e in range(2):
                    s_all = _dot_nt(qst, kops[g][e])
                    dp_all = _dot_nt(dost, vops[g][e])
                    ps, dsrs = [], []
                    for pr in range(PAIRS):
                        h = _head(g, pr, e)
                        rows = slice(pr * CHUNK, (pr + 1) * CHUNK)
                        p, p_sink = _softmax_scores(s_all[rows], bias_ref[h], sink_ref[0, h])
                        dp = dp_all[rows]
                        delta = jnp.sum(p * dp, axis=-1, keepdims=True)
                        ds = p * (dp - delta)
                        sacc_ref[:, h:h + 1] += -(p_sink * delta)
                        dbias_ref[h] += ds
                        ps.append(p.astype(BF16))
                        dsrs.append(ds.astype(BF16))
                    dsr_all = jnp.concatenate(dsrs, axis=0)
                    dq_st = dq_st + _dot(dsr_all, kops[g][e])
                    dk_e.append(_dot_tn(dsr_all, qst))
                    dv_e.append(_dot_tn(jnp.concatenate(ps, axis=0), dost))
                for pr in range(PAIRS):
                    c0 = (g * PAIRS + pr) * 2 * HEAD_DIM
                    dq_ref[:, c0:c0 + 2 * HEAD_DIM] = (dq_st[pr * CHUNK:(pr + 1) * CHUNK] * SCALE).astype(BF16)
                halves.append((dk_e, dv_e))
            tiles = []
            for t in range(2):
                g0, g1 = halves[0][t], halves[1][t]
                tiles.append(jnp.where(low, g0[0] + pltpu.roll(g0[1], HEAD_DIM, 1), pltpu.roll(g1[0], HEAD_DIM, 1) + g1[1]))
            dband = jnp.concatenate(tiles, axis=1)
            dkv = (carry_ref[...] + dband[:CHUNK]).astype(BF16)
            dkv_ref[...] = dkv
            dkvt_ref[...] = dkv.T
            dqt_ref[...] = dq_ref[...].T
            carry_ref[...] = dband[CHUNK:]

        @pl.when(n == nb)
        def _():
            dkv = carry_ref[...].astype(BF16)
            dkv_ref[...] = dkv
            dkvt_ref[...] = dkv.T
            dsink_ref[...] = jnp.sum(sacc_ref[...], axis=0, keepdims=True)

    full = lambda *shape: pl.BlockSpec(shape, lambda n: (0,) * len(shape))
    return _call(
        body, name="attn_bwd", grid=(nb + 1,),
        in_specs=[pl.BlockSpec((CHUNK, B_WIDTH), lambda n: (qn(n), 2)),
                  pl.BlockSpec((CHUNK, 2 * KV_WIDTH), lambda n: (qn(n), 12)),
                  pl.BlockSpec((CHUNK, 2 * KV_WIDTH), lambda n: (jnp.maximum(qn(n) - 1, 0), 12)),
                  pl.BlockSpec((CHUNK, B_WIDTH), lambda n: (qn(n), 1)),
                  pl.BlockSpec((CHUNK, B_WIDTH), lambda n: (qn(n), 1)),
                  full(1, B_WIDTH), pl.BlockSpec(memory_space=pltpu.SMEM),
                  pl.BlockSpec((None, B_HEADS, CHUNK, 2 * CHUNK), lambda n: (jnp.minimum(n, 1), 0, 0, 0))],
        out_specs=[pl.BlockSpec((CHUNK, B_WIDTH), lambda n: (qn(n), 0)),
                   pl.BlockSpec((CHUNK, 2 * KV_WIDTH), lambda n: (jnp.maximum(n - 1, 0), 0)),
                   pl.BlockSpec((B_WIDTH, CHUNK), lambda n: (0, qn(n))),
                   pl.BlockSpec((2 * KV_WIDTH, CHUNK), lambda n: (0, jnp.maximum(n - 1, 0))),
                   full(1, B_WIDTH), full(1, B_HEADS), full(B_HEADS, CHUNK, 2 * CHUNK)],
        out_shape=[jax.ShapeDtypeStruct((T, B_WIDTH), BF16), jax.ShapeDtypeStruct((T, 2 * KV_WIDTH), BF16),
                   jax.ShapeDtypeStruct((B_WIDTH, T), BF16), jax.ShapeDtypeStruct((2 * KV_WIDTH, T), BF16),
                   jax.ShapeDtypeStruct((1, B_WIDTH), F32), jax.ShapeDtypeStruct((1, B_HEADS), F32),
                   jax.ShapeDtypeStruct((B_HEADS, CHUNK, 2 * CHUNK), F32)],
        scratch_shapes=[pltpu.VMEM((CHUNK, 2 * KV_WIDTH), F32), pltpu.VMEM((CHUNK, B_HEADS), F32)],
        sem=("arbitrary",), rides=rides,
    )(proj, proj, proj, ab, dmixed, gb, sinks, bias)


def _sq_relu_grad(acc, r):
    return acc * (2.0 * r.astype(F32))


def _chip_index():
    return (2 * lax.axis_index("x") + lax.axis_index("y")).astype(jnp.int32).reshape(1)


def _cast_into_slot(w, *, tm, name):
    _, R, C = w.shape

    def body(me_ref, w_ref, o_ref):
        del me_ref
        o_ref[...] = w_ref[...].astype(BF16)

    return pl.pallas_call(
        body, name=name,
        grid_spec=pltpu.PrefetchScalarGridSpec(
            num_scalar_prefetch=1, grid=(R // tm,),
            in_specs=[pl.BlockSpec((None, tm, C), lambda i, me: (0, i, 0))],
            out_specs=pl.BlockSpec((None, tm, C), lambda i, me: (me[0], i, 0))),
        out_shape=jax.ShapeDtypeStruct((N_CHIPS, R, C), BF16), compiler_params=_params(("parallel",)),
    )(_chip_index(), w)


def _cast_into_slots_carrying(ws, *, steps, name, rides):
    n = len(ws)

    def body(*refs):
        for w_ref, o_ref in zip(refs[:n], refs[n:]):
            o_ref[...] = w_ref[...].astype(BF16)

    me = lambda: 2 * lax.axis_index("x") + lax.axis_index("y")
    return _call(
        body, name=name, grid=(steps,),
        in_specs=[pl.BlockSpec((None, w.shape[1] // steps, w.shape[2]), lambda i: (0, i, 0)) for w in ws],
        out_specs=[pl.BlockSpec((None, w.shape[1] // steps, w.shape[2]), lambda i: (me(), i, 0)) for w in ws],
        out_shape=[jax.ShapeDtypeStruct((N_CHIPS,) + w.shape[1:], BF16) for w in ws], sem=("arbitrary",), rides=rides,
    )(*ws)


def _owner_total(gh, others, *, tm, name):
    _, hr, C = gh.shape

    def body(me_ref, g_ref, o_ref_in, out_ref):
        del me_ref
        acc = g_ref[...]
        for j in range(3):
            acc = acc + o_ref_in[j].astype(F32)
        out_ref[...] = acc

    return pl.pallas_call(
        body, name=name,
        grid_spec=pltpu.PrefetchScalarGridSpec(
            num_scalar_prefetch=1, grid=(hr // tm,),
            in_specs=[pl.BlockSpec((None, tm, C), lambda i, me: (me[0], i, 0)),
                      pl.BlockSpec((3, tm, C), lambda i, me: (0, i, 0))],
            out_specs=pl.BlockSpec((tm, C), lambda i, me: (i, 0))),
        out_shape=jax.ShapeDtypeStruct((hr, C), F32),
        compiler_params=_params(("parallel",)),
    )(_chip_index(), gh, others)


def _adamw_math(w, g, m, v):
    m = ADAM_B1 * m + (1.0 - ADAM_B1) * g
    v = ADAM_B2 * v + (1.0 - ADAM_B2) * (g * g)
    m_hat = m / (1.0 - ADAM_B1 ** ADAM_STEP)
    v_hat = v / (1.0 - ADAM_B2 ** ADAM_STEP)
    delta = -ADAM_LR * (m_hat / (jnp.sqrt(v_hat) + ADAM_EPS) + ADAM_WD * w)
    return delta, m, v


def _adamw_halves(w, own, got, m, v, *, tm, name, rides=()):
    _, R, C = w.shape
    nt = (R // 2) // tm

    def body(w_ref, own_ref, got_ref, m_ref, v_ref, g_ref, d_ref, nm_ref, nv_ref):
        g = jnp.where(pl.program_id(0) == lax.axis_index("c"), own_ref[...], got_ref[...])
        g_ref[...] = g
        d_ref[...], nm_ref[...], nv_ref[...] = _adamw_math(w_ref[...], g, m_ref[...], v_ref[...])

    whole = pl.BlockSpec((None, tm, C), lambda h, i: (0, h * nt + i, 0))
    half = pl.BlockSpec((tm, C), lambda h, i: (i, 0))
    return _call(
        body, name=name, grid=(2, nt), in_specs=[whole, half, half, whole, whole], out_specs=[whole] * 4,
        out_shape=[jax.ShapeDtypeStruct((1, R, C), F32)] * 4, sem=("parallel", "parallel"), rides=rides,
    )(w, own, got, m, v)


def _adamw_small(w, slots, m, v, *, name):
    def body(w_ref, slots_ref, m_ref, v_ref, g_ref, d_ref, nm_ref, nv_ref):
        g = slots_ref[0]
        for d in range(1, N_DEV):
            g = g + slots_ref[d]
        g_ref[...] = g
        d_ref[...], nm_ref[...], nv_ref[...] = _adamw_math(w_ref[...], g, m_ref[...], v_ref[...])

    vmem = pl.BlockSpec(memory_space=pltpu.VMEM)
    return pl.pallas_call(
        body, name=name, in_specs=[vmem] * 4, out_specs=[vmem] * 4,
        out_shape=[jax.ShapeDtypeStruct(w.shape, F32)] * 4, compiler_params=_params(),
    )(w, slots, m, v)


SMALL = ["rel_bias_table", "mix_norm_g", "gate_norm_g", "gate_norm_b", "w_spatial", "b_spatial", "attn_sinks",
         "out_norm_a_g", "out_norm_b_g", "ffn_norm_g", "final_norm_g"]
SMALL_A = ["gate_norm_g", "gate_norm_b", "w_spatial", "b_spatial", "out_norm_a_g"]
SMALL_B = ["rel_bias_table", "mix_norm_g", "attn_sinks", "out_norm_b_g", "ffn_norm_g", "final_norm_g"]
LARGE = ["w_in", "w_out", "w_up", "w_down"]
ROW_TILE = {"w_in": 208, "w_out": 256, "w_up": 256, "w_down": 256}
WEIGHTS = ["rel_bias_table", "mix_norm_g", "w_in", "gate_norm_g", "gate_norm_b", "w_spatial", "b_spatial", "attn_sinks",
           "out_norm_a_g", "out_norm_b_g", "w_out", "ffn_norm_g", "w_up", "w_down", "final_norm_g"]
PACK_UNIT = 8 * 128


def _pack(parts):
    rows = []
    for p in parts:
        flat = p.reshape(-1)
        pad = (-flat.shape[0]) % PACK_UNIT
        rows.append(jnp.pad(flat, (0, pad)).reshape(-1, 128))
    return jnp.concatenate(rows, axis=0)


def _unpack(packed, like):
    out, row = [], 0
    for p in like:
        n = math.prod(p.shape)
        nrows = (n + PACK_UNIT - 1) // PACK_UNIT * 8
        out.append(packed[row:row + nrows].reshape(-1)[:n].reshape(p.shape))
        row += nrows
    return out


def kernel(x, rel_bias_table, mix_norm_g, w_in, gate_norm_g, gate_norm_b, w_spatial, b_spatial, attn_sinks, out_norm_a_g, out_norm_b_g, w_out, ffn_norm_g, w_up, w_down, final_norm_g, loss_target, m_rel_bias_table, m_mix_norm_g, m_w_in, m_gate_norm_g, m_gate_norm_b, m_w_spatial, m_b_spatial, m_attn_sinks, m_out_norm_a_g, m_out_norm_b_g, m_w_out, m_ffn_norm_g, m_w_up, m_w_down, m_final_norm_g, v_rel_bias_table, v_mix_norm_g, v_w_in, v_gate_norm_g, v_gate_norm_b, v_w_spatial, v_b_spatial, v_attn_sinks, v_out_norm_a_g, v_out_norm_b_g, v_w_out, v_ffn_norm_g, v_w_up, v_w_down, v_final_norm_g):
    args = dict(locals())
    wts = {n: args[n] for n in WEIGHTS}
    mom = {n: args["m_" + n] for n in WEIGHTS}
    var = {n: args["v_" + n] for n in WEIGHTS}
    sp = {n: wts[n] for n in SMALL}
    x2, tgt = x[0], loss_target[0]
    T = x2.shape[0]
    tm = min(512, T)
    tl = min(1024, T)
    lg = sp["gate_norm_g"].reshape(A_GROUPS, CHUNK)
    lb = sp["gate_norm_b"].reshape(A_GROUPS, CHUNK)
    wsp = sp["w_spatial"].reshape(A_GROUPS, CHUNK, CHUNK)
    bs_col = sp["b_spatial"].reshape(A_GROUPS, CHUNK, 1)
    sinks = sp["attn_sinks"].reshape(1, B_HEADS)
    ga = sp["out_norm_a_g"].reshape(1, A_WIDTH)
    gb = sp["out_norm_b_g"].reshape(1, B_WIDTH)
    g1 = sp["mix_norm_g"].reshape(1, D_MODEL)
    g2 = sp["ffn_norm_g"].reshape(1, D_MODEL)
    gf = sp["final_norm_g"].reshape(1, D_MODEL)

    def owner_total(n, gh, others):
        return _owner_total(gh, others, tm=ROW_TILE[n], name="rs_owner_total_" + n)

    def halves_view(at, shards):
        return at.reshape(shards, 2, at.shape[0] // shards // 2, at.shape[1])

    for d in (wts, mom, var):
        d["w_in"] = jnp.swapaxes(d["w_in"], 1, 2)

    s_in = _cast_into_slot(wts["w_in"], tm=ROW_TILE["w_in"], name="cast_w_in")
    (s_out, s_up, s_down), ((g_in,),) = _cast_into_slots_carrying(
        [wts["w_out"], wts["w_up"], wts["w_down"]], steps=8, name="cast_w_rest",
        rides=[_ride_gather(s_in, chain=(0, 1, 1), chain_fracs=(0.3, 0.6))])
    win_t = g_in.reshape(PROJ_WIDTH, D_MODEL)
    bias = _bias_build(sp["rel_bias_table"])
    (n1, proj), ((g_out,), (s_up,)) = _norm_matmul_wide(
        x2, g1, win_t, tm=tm, tn=PROJ_WIDTH // 2, name="in_proj",
        rides=[_ride_gather(s_out, chain=(0, 1, 1), chain_fracs=(0.65, 0.85)), _ride_gather(s_up, s1=(0, 3, 8))])
    wo = g_out.reshape(A_WIDTH + B_WIDTH, D_MODEL)
    (mixed, mixed_t, ab), ((s_up,), (s_down,), (n1_sib,)) = _mixer_fwd(
        proj, lg, lb, wsp, bs_col, sinks, bias, ga, gb,
        rides=[_ride_gather(s_up, s2=(0, 3, 8), s1=(3, 8, 8)), _ride_gather(s_down, s1=(0, 1, 8)),
               _ride_to_sibling(n1, first=True)])
    mixed_t = halves_view(mixed_t, N_CHIPS)
    h1, ((wu,), (s_down,), (mixed_t_sib,)) = _matmul_res(
        mixed, wo, x2, tm=tl, tn=1024, tk=D_MODEL, prologue=_to_bf16, name="out_proj",
        rides=[_ride_gather(s_up, s3=(0, 3, 8), tail=(3, 8, 8), mid_frac=0.75), _ride_gather(s_down, s2=(0, 1, 8)),
               _ride_to_sibling(mixed_t, halves=True)])
    (n2t, zp, z2, z2t), ((g_down,),) = _norm_matmul_sq(
        h1, g2, wu, tm=tl, tn=1024, name="up_proj", rides=[_ride_gather(s_down, s3=(0, 1, 8), chain=(1, 8, 8), chain_fracs=(0.55, 0.85))])
    wd = g_down.reshape(D_FF, D_MODEL)
    n2t, z2t = halves_view(n2t, 1), halves_view(z2t, N_CHIPS)
    h2, ((n2t_sib,), (z2t_sib,)) = _matmul_res(
        z2, wd, h1, tm=tl, tn=1024, tk=4096, prologue=_to_bf16, name="down_proj",
        rides=[_ride_to_sibling(n2t, halves=True), _ride_to_sibling(z2t, halves=True)])

    dh2, dh2b, dgf, loss = _loss_bwd(h2, tgt, gf, tm=tm)
    dzp, ((dh2b_sib,),) = _matmul_nt(dh2b, wd, tm=tl, tn=1024, tk=D_MODEL, name="bwd_dz", extra=zp,
                                     epilogue=_sq_relu_grad, out_dtype=BF16, rides=[_ride_to_sibling(dh2b)])
    (gd, gdb), ((dzp_sib,),) = _grad_pair(z2t, z2t_sib, dh2b, dh2b_sib, cols_sharded=False, tmo=1024, tk=tl,
                                          name="grad_w_down", rides=[_ride_to_sibling(dzp)])
    (gu, gub), ((o_d,),) = _grad_pair(n2t, n2t_sib, dzp, dzp_sib, cols_sharded=True, tmo=1024, tk=tl,
                                      name="grad_w_up", rides=[_ride_scatter(gdb, None, (0, 7, 8))])
    dn2, ((o_d,), (o_u,)) = _matmul_nt(dzp, wu, tm=tl, tn=1024, tk=4096, name="bwd_dn2",
                                       rides=[_ride_scatter(gdb, o_d, (7, 8, 8)), _ride_scatter(gub, None, (0, 6, 8))])
    h_d = owner_total("w_down", gd, o_d)
    (dh1, dh1b, dg2), ((o_u,),) = _rms_bwd_res(dn2, h1, g2, dh2, tm=tm, name="ffn_norm_bwd",
                                               rides=[_ride_scatter(gub, o_u, (6, 7, 8))])
    dmixed, ((o_u,), (dh1b_sib,), (w_d,)) = _matmul_nt(
        dh1b, wo, tm=tl, tn=1024, tk=D_MODEL, name="bwd_dmixed",
        rides=[_ride_scatter(gub, o_u, (7, 8, 8)), _ride_to_sibling(dh1b), _ride_swap(h_d)])
    h_u = owner_total("w_up", gu, o_u)
    (go, gob), ((w_u,),) = _grad_pair_merged(mixed_t, mixed_t_sib, dh1b, dh1b_sib, tk=tl, name="grad_w_out",
                                             rides=[_ride_swap(h_u)])
    (duv, duv_t, dga, dwsp, dbs, dlg, dlb), ((o_o,),) = _gmlp_bwd(proj, ab, dmixed, ga, lg, lb, wsp, bs_col,
                                                                  rides=[_ride_scatter(gob)])
    h_o = owner_total("w_out", go, o_o)
    small = {"gate_norm_g": dlg, "gate_norm_b": dlb, "w_spatial": dwsp, "b_spatial": dbs, "out_norm_a_g": dga}
    hr_in = PROJ_WIDTH // N_CHIPS // 2
    (dq, dkv, dq_t, dkv_t, dgb, dsinks, dbias), ((slots_a,), (dproj_t_sib,)) = _attn_bwd(
        proj, ab, dmixed, gb, sinks, bias,
        rides=[_ride_small_to_all(_pack([small[n] for n in SMALL_A])), _ride_rows_to_sibling(duv_t, hr_in, 2, N_CHIPS)])
    dtable = _bias_grad(dbias)
    dproj_t = halves_view(jnp.concatenate([duv_t, dq_t, dkv_t], axis=0), N_CHIPS)
    ((dproj_t_sib,),) = _carrier([_ride_to_sibling(dproj_t, halves=True, shards=(2, N_CHIPS), land=dproj_t_sib)],
                                 name="trade_dproj_t")
    (gi, gib_near), ((w_o,),) = _grad_pair(
        dproj_t, dproj_t_sib, n1, n1_sib, cols_sharded=False, tmo=hr_in, tk=tl, name="grad_w_in_near", shards="near",
        rides=[_ride_swap(h_o)])
    (gi, gib_far), ((o_i,),) = _grad_pair(
        dproj_t, dproj_t_sib, n1, n1_sib, cols_sharded=False, tmo=hr_in, tk=tl, name="grad_w_in_far", shards="far",
        into=gi, rides=[_ride_scatter(gib_near, None, to=(0, 1))])
    dn1, ((o_i,),) = _matmul_parts([duv, dq, dkv], win_t, tm=tl, tn=1024, name="bwd_dn1",
                                   rides=[_ride_scatter(gib_far, o_i, to=(2,))])
    h_i = owner_total("w_in", gi, o_i)
    dx, dg1 = _rms_bwd_res(dn1, x2, g1, dh1, tm=tm, name="mix_norm_bwd", bf16_copy=False)
    small.update({"rel_bias_table": dtable.reshape(N_BUCKETS, B_HEADS), "mix_norm_g": dg1, "attn_sinks": dsinks,
                  "out_norm_b_g": dgb, "ffn_norm_g": dg2, "final_norm_g": dgf})
    (w_i,), (slots_b,) = _carrier([_ride_swap(h_i), _ride_small_to_all(_pack([small[n] for n in SMALL_B] + [loss]))],
                                  name="swap_w_in")

    out_g, out_d, out_m, out_v = {}, {}, {}, {}
    for n, h, s in zip(LARGE, [h_i, h_o, h_u, h_d], [w_i, w_o, w_u, w_d]):
        res = _adamw_halves(wts[n], h, s, mom[n], var[n], tm=ROW_TILE[n], name="adamw_" + n)
        if n == "w_in":
            res = [jnp.swapaxes(r, 1, 2) for r in res]
        out_g[n], out_d[n], out_m[n], out_v[n] = res
    for names, slots, tag in ((SMALL_A, slots_a, "a"), (SMALL_B, slots_b, "b")):
        extra = [jnp.zeros((1, 1), F32)] if tag == "b" else []
        like = [wts[n] for n in names] + extra
        res = _adamw_small(_pack(like), slots, _pack([mom[n] for n in names] + extra),
                           _pack([var[n] for n in names] + extra), name="adamw_small_" + tag)
        for store, packed in zip((out_g, out_d, out_m, out_v), res):
            for n, val in zip(names + ["loss"], _unpack(packed, like)):
                store[n] = val

    total = out_g["loss"][0, 0]
    return (total, dx[None], *[out_g[n] for n in WEIGHTS], *[out_d[n] for n in WEIGHTS],
            *[out_m[n] for n in WEIGHTS], *[out_v[n] for n in WEIGHTS])
```

```python
import math

import numpy as np
import jax
import jax.numpy as jnp
from jax import lax
from jax.experimental import pallas as pl
from jax.experimental.pallas import tpu as pltpu

F32 = jnp.float32
BF16 = jnp.bfloat16

D_MODEL = 2048
CHUNK = 128
A_GROUPS = 8
A_WIDTH = 1024
HEAD_DIM = 64
B_HEADS = 16
Q_PER_KV = 8
B_WIDTH = 1024
KV_WIDTH = 128
PROJ_WIDTH = 3328
D_FF = 8192
N_BUCKETS = 32
EPS = 1e-5
NEG = -1e30
SCALE = HEAD_DIM ** -0.5
N_CHIPS = 4
N_DEV = 8

ADAM_LR = 0.001
ADAM_B1 = 0.9
ADAM_B2 = 0.999
ADAM_EPS = 1e-08
ADAM_WD = 0.01
ADAM_STEP = 10

VMEM_LIMIT = 60 * 1024 * 1024
MESH = pl.DeviceIdType.MESH


def _bucket_thresholds():
    d = np.arange(CHUNK)
    n_exact = N_BUCKETS // 2
    relf = np.maximum(d, n_exact).astype(np.float64)
    large = n_exact + (np.log(relf / n_exact) / math.log(CHUNK / n_exact) * (N_BUCKETS - n_exact)).astype(np.int32)
    bucket = np.where(d < n_exact, d, np.minimum(large, N_BUCKETS - 1))
    return [int(np.min(d[bucket >= b])) for b in range(1, N_BUCKETS)]


BUCKET_THR = _bucket_thresholds()


def _params(sem=None):
    return pltpu.CompilerParams(dimension_semantics=sem, vmem_limit_bytes=VMEM_LIMIT)


def _gelu(x):
    c = math.sqrt(2.0 / math.pi)
    return 0.5 * x * (1.0 + jnp.tanh(c * (x + 0.044715 * (x * x * x))))


def _gelu_and_grad(x):
    c = math.sqrt(2.0 / math.pi)
    x2 = x * x
    t = jnp.tanh(c * (x + 0.044715 * (x2 * x)))
    g = 0.5 * x * (1.0 + t)
    dg = 0.5 * (1.0 + t) + 0.5 * x * (1.0 - t * t) * (c * (1.0 + 3.0 * 0.044715 * x2))
    return g, dg


def _dot(a, b):
    return jnp.dot(a, b, preferred_element_type=F32)


def _dot_nt(a, b):
    return lax.dot_general(a, b, (((1,), (1,)), ((), ())), preferred_element_type=F32)


def _dot_tn(a, b):
    return lax.dot_general(a, b, (((0,), (0,)), ((), ())), preferred_element_type=F32)


def _rms_bwd(dn, h, g):
    r = lax.rsqrt(jnp.mean(h * h, axis=-1, keepdims=True) + EPS)
    w = dn * g
    dh = r * w - h * ((r * r * r) * jnp.mean(w * h, axis=-1, keepdims=True))
    return dh, r


def _place():
    x, y, c = lax.axis_index("x"), lax.axis_index("y"), lax.axis_index("c")
    chips = [(1 - x, y), (x, 1 - y), (1 - x, 1 - y)]
    return x, y, c, chips


def _remote(src, dst, send_sem, recv_sem, to):
    return pltpu.make_async_remote_copy(src_ref=src, dst_ref=dst, send_sem=send_sem, recv_sem=recv_sem,
                                        device_id=to, device_id_type=MESH)


class _Ride:
    def __init__(self, args, out_shape, n_sem, start, finish, mids=(), aliases=None):
        self.args, self.out_shape, self.n_sem = list(args), list(out_shape), n_sem
        self.start, self.mids, self.finish = start, list(mids), finish
        self.aliases = dict(aliases or {})


def _call(body, *, name, grid, in_specs, out_specs, out_shape, scratch_shapes=(), sem=None, rides=(), aliases=None):
    single = not isinstance(out_shape, (list, tuple))
    out_specs = [out_specs] if single else list(out_specs)
    out_shape = [out_shape] if single else list(out_shape)
    n_in, n_out, n_scr = len(in_specs), len(out_shape), len(scratch_shapes)
    r_in = [len(r.args) for r in rides]
    r_out = [len(r.out_shape) for r in rides]
    any_spec = pl.BlockSpec(memory_space=pl.ANY)
    aliases, off_i, off_o = dict(aliases or {}), n_in, n_out
    for r in rides:
        for i, o in r.aliases.items():
            aliases[off_i + i] = off_o + o
        off_i += len(r.args)
        off_o += len(r.out_shape)
    steps = math.prod(grid)

    def wrapped(*refs):
        p = 0
        ins = refs[p:p + n_in]; p += n_in
        rins = refs[p:p + sum(r_in)]; p += sum(r_in)
        outs = refs[p:p + n_out]; p += n_out
        routs = refs[p:p + sum(r_out)]; p += sum(r_out)
        scr = refs[p:p + n_scr]; p += n_scr
        sems = refs[p:]
        parts, pi, po = [], 0, 0
        for k, r in enumerate(rides):
            parts.append((rins[pi:pi + r_in[k]], routs[po:po + r_out[k]], sems[2 * k], sems[2 * k + 1]))
            pi += r_in[k]
            po += r_out[k]
        lin = 0
        for d in range(len(grid)):
            lin = lin * grid[d] + pl.program_id(d)
        if rides:
            @pl.when(lin == 0)
            def _():
                for r, part in zip(rides, parts):
                    r.start(*part)
        body(*ins, *outs, *scr)
        for r, part in zip(rides, parts):
            for frac, fn in r.mids:
                @pl.when(lin == min(steps - 1, int(frac * steps)))
                def _(fn=fn, part=part):
                    fn(*part)
        if rides:
            @pl.when(lin == steps - 1)
            def _():
                for r, part in zip(rides, parts):
                    r.finish(*part)

    scratch = list(scratch_shapes)
    for r in rides:
        scratch += [pltpu.SemaphoreType.DMA((r.n_sem,)), pltpu.SemaphoreType.DMA((r.n_sem,))]
    if rides:
        sem = ("arbitrary",) * len(grid)
    res = pl.pallas_call(
        wrapped, name=name, grid=grid,
        in_specs=list(in_specs) + [any_spec] * sum(r_in),
        out_specs=out_specs + [any_spec] * sum(r_out),
        out_shape=out_shape + [s for r in rides for s in r.out_shape],
        scratch_shapes=scratch, input_output_aliases=aliases,
        compiler_params=_params(sem),
    )

    def run(*args):
        got = res(*args, *[a for r in rides for a in r.args])
        mine = got[0] if single else list(got[:n_out])
        if not rides:
            return mine
        rest, out = list(got[n_out:]), []
        for k in range(len(rides)):
            out.append(rest[:r_out[k]])
            rest = rest[r_out[k]:]
        return mine, out

    return run


def _ride_gather(slot, s1=None, s2=None, s3=None, tail=None, chain=None, mid_frac=0.6, chain_fracs=(0.35, 0.7)):
    half = slot.shape[1] // 2

    def rows(part, c, which=None):
        k0, k1, n = part
        count, first = (k1 - k0) * (half // n), c * half + k0 * (half // n)
        return pl.ds(first, count) if which is None else pl.ds(first + which * (count // 2), count // 2)

    def ids():
        x, y, c, _ = _place()
        return x, y, c, 2 * x + y, 2 * (1 - x) + y, 2 * x + (1 - y), 2 * (1 - x) + (1 - y)

    def copy(full, chip, r, ss, rs, k, to):
        piece = full.at[chip, r, :]
        return _remote(piece, piece, ss.at[k], rs.at[k], to)

    def to_neighbours(full, ss, rs, part, base):
        x, y, c, me, _, _, _ = ids()
        return [copy(full, me, rows(part, c), ss, rs, base, (1 - x, y, c)),
                copy(full, me, rows(part, c), ss, rs, base + 1, (x, 1 - y, c))]

    def from_neighbours(full, ss, rs, part, base):
        x, y, c, _, cx, cy, _ = ids()
        return [copy(full, cx, rows(part, c), ss, rs, base, (x, y, c)), copy(full, cy, rows(part, c), ss, rs, base + 1, (x, y, c))]

    def onward(full, ss, rs, part, base):
        x, y, c, _, cx, cy, _ = ids()
        return [copy(full, cx, rows(part, c, 0), ss, rs, base, (x, 1 - y, c)),
                copy(full, cy, rows(part, c, 1), ss, rs, base + 1, (1 - x, y, c))]

    def from_onward(full, ss, rs, part, base):
        x, y, c, _, _, _, cd = ids()
        return [copy(full, cd, rows(part, c, 0), ss, rs, base, (x, y, c)), copy(full, cd, rows(part, c, 1), ss, rs, base + 1, (x, y, c))]

    def to_sibling(full, ss, rs, part, base, diagonal):
        x, y, c, _, cx, cy, cd = ids()
        return [copy(full, chip, rows(part, c), ss, rs, base + j, (x, y, 1 - c))
                for j, chip in enumerate([cd] if diagonal else [cx, cy])]

    def from_sibling(full, ss, rs, part, base, diagonal):
        x, y, c, _, cx, cy, cd = ids()
        return [copy(full, chip, rows(part, 1 - c), ss, rs, base + j, (x, y, c))
                for j, chip in enumerate([cd] if diagonal else [cx, cy])]

    def start(ins, outs, ss, rs):
        full, cps = outs[0], []
        for part, base in ((s1, 0), (chain, 12)):
            if part is not None:
                cps += to_neighbours(full, ss, rs, part, base)
        for part, b_ici, b_sib in ((s2, 2, 4), (tail, 7, 9)):
            if part is not None:
                cps += onward(full, ss, rs, part, b_ici) + to_sibling(full, ss, rs, part, b_sib, False)
        if s3 is not None:
            cps += to_sibling(full, ss, rs, s3, 6, True)
        for cp in cps:
            cp.start()

    def second(part, b_in, b_ici, b_sib):
        def fn(ins, outs, ss, rs):
            for cp in from_neighbours(outs[0], ss, rs, part, b_in):
                cp.wait_recv()
            for cp in onward(outs[0], ss, rs, part, b_ici) + to_sibling(outs[0], ss, rs, part, b_sib, False):
                cp.start()
        return fn

    def third(part, b_ici, b_sib):
        def fn(ins, outs, ss, rs):
            for cp in from_onward(outs[0], ss, rs, part, b_ici):
                cp.wait_recv()
            for cp in to_sibling(outs[0], ss, rs, part, b_sib, True):
                cp.start()
        return fn

    mids = []
    if tail is not None:
        mids.append((mid_frac, third(tail, 7, 11)))
    if chain is not None:
        mids += [(chain_fracs[0], second(chain, 12, 14, 16)), (chain_fracs[1], third(chain, 14, 18))]

    def finish(ins, outs, ss, rs):
        full, got, sent = outs[0], [], []
        if s1 is not None:
            got += from_neighbours(full, ss, rs, s1, 0)
            sent += to_neighbours(full, ss, rs, s1, 0)
        if s2 is not None:
            got += from_onward(full, ss, rs, s2, 2) + from_sibling(full, ss, rs, s2, 4, False)
            sent += onward(full, ss, rs, s2, 2) + to_sibling(full, ss, rs, s2, 4, False)
        if s3 is not None:
            got += from_sibling(full, ss, rs, s3, 6, True)
            sent += to_sibling(full, ss, rs, s3, 6, True)
        if tail is not None:
            got += from_sibling(full, ss, rs, tail, 9, False) + from_sibling(full, ss, rs, tail, 11, True)
            sent += onward(full, ss, rs, tail, 7) + to_sibling(full, ss, rs, tail, 9, False) + to_sibling(full, ss, rs, tail, 11, True)
        if chain is not None:
            got += from_sibling(full, ss, rs, chain, 16, False) + from_sibling(full, ss, rs, chain, 18, True)
            sent += (to_neighbours(full, ss, rs, chain, 12) + onward(full, ss, rs, chain, 14)
                     + to_sibling(full, ss, rs, chain, 16, False) + to_sibling(full, ss, rs, chain, 18, True))
        for cp in got:
            cp.wait_recv()
        for cp in sent:
            cp.wait_send()

    return _Ride([slot], [jax.ShapeDtypeStruct(slot.shape, slot.dtype)], 19, start, finish, mids=mids, aliases={0: 0})


def _ride_scatter(q, land=None, part=(0, 1), to=(0, 1, 2)):
    k0, k1, n = part if len(part) == 3 else (part[0], part[0] + 1, part[1])
    rows_n = q.shape[1] // n
    rows = pl.ds(k0 * rows_n, (k1 - k0) * rows_n)

    def copies(ins, outs, ss, rs):
        x, y, c, chips = _place()
        return [_remote(ins[0].at[2 * chip[0] + chip[1], rows, :], outs[0].at[j, rows, :], ss.at[j], rs.at[j], (*chip, c))
                for j, chip in enumerate(chips) if j in to]

    def start(*a):
        for cp in copies(*a):
            cp.start()

    def finish(*a):
        for cp in copies(*a):
            cp.wait()

    shape = jax.ShapeDtypeStruct((3,) + q.shape[1:], q.dtype)
    if land is None:
        return _Ride([q], [shape], 3, start, finish)
    return _Ride([q, land], [shape], 3, start, finish, aliases={1: 0})


def _ride_to_sibling(a, halves=False, first=False, shards=None, land=None):
    s0, s1 = shards or (0, a.shape[0])

    def copy(ins, outs, ss, rs):
        x, y, c, _ = _place()
        if halves:
            src, dst = ins[0].at[s0:s1, 1 - c], outs[0].at[s0:s1]
        else:
            src, dst = (ins[0].at[0] if first else ins[0]), outs[0]
        return _remote(src, dst, ss.at[0], rs.at[0], (x, y, 1 - c))

    shape = (a.shape[0],) + a.shape[2:] if halves else (a.shape[1:] if first else a.shape)
    return _Ride([a] if land is None else [a, land], [jax.ShapeDtypeStruct(shape, a.dtype)], 1,
                 lambda *a_: copy(*a_).start(), lambda *a_: copy(*a_).wait(), aliases=None if land is None else {1: 0})


def _ride_rows_to_sibling(a, hr, shards, total):
    def copies(ins, outs, ss, rs):
        x, y, c, _ = _place()
        return [_remote(ins[0].at[pl.ds((2 * s + 1 - c) * hr, hr), :], outs[0].at[s], ss.at[s], rs.at[s], (x, y, 1 - c))
                for s in range(shards)]

    def start(*a_):
        for cp in copies(*a_):
            cp.start()

    def finish(*a_):
        for cp in copies(*a_):
            cp.wait()

    return _Ride([a], [jax.ShapeDtypeStruct((total, hr, a.shape[1]), a.dtype)], shards, start, finish)


def _ride_swap(h):
    def copy(ins, outs, ss, rs):
        x, y, c, _ = _place()
        return _remote(ins[0], outs[0], ss.at[0], rs.at[0], (x, y, 1 - c))

    return _Ride([h], [jax.ShapeDtypeStruct(h.shape, h.dtype)], 1,
                 lambda *a: copy(*a).start(), lambda *a: copy(*a).wait())


def _mesh_place(p):
    return (p // 4, (p // 2) % 2, p % 2)


def _ride_small_to_all(packed):
    def copies(ins, outs, ss, rs):
        x, y, c, _ = _place()
        me = 4 * x + 2 * y + c
        return [_remote(ins[0], outs[0].at[me], ss.at[k - 1], rs.at[k - 1], _mesh_place((me + k) % N_DEV))
                for k in range(1, N_DEV)]

    def own(ins, outs, ss, rs):
        x, y, c, _ = _place()
        return pltpu.make_async_copy(ins[0], outs[0].at[4 * x + 2 * y + c], ss.at[N_DEV - 1])

    def start(*a):
        own(*a).start()
        for cp in copies(*a):
            cp.start()

    def finish(ins, outs, ss, rs):
        x, y, c, _ = _place()
        me = 4 * x + 2 * y + c
        for k in range(1, N_DEV):
            _remote(ins[0], outs[0].at[(me + N_DEV - k) % N_DEV], ss.at[k - 1], rs.at[k - 1], (x, y, c)).wait_recv()
        for cp in copies(ins, outs, ss, rs):
            cp.wait_send()
        own(ins, outs, ss, rs).wait()

    return _Ride([packed], [jax.ShapeDtypeStruct((N_DEV,) + packed.shape, packed.dtype)], N_DEV, start, finish)


def _carrier(rides, *, name):
    _, outs = _call(lambda: None, name=name, grid=(1,), in_specs=[], out_specs=[], out_shape=[], rides=rides)()
    return outs


def _norm_bf16(a_ref, g_ref):
    xf = a_ref[...]
    r = lax.rsqrt(jnp.mean(xf * xf, axis=-1, keepdims=True) + EPS)
    return ((xf * r) * g_ref[...]).astype(BF16)


def _norm_matmul_wide(a, g, b, *, tm, tn, name, rides=()):
    T, K = a.shape
    N = b.shape[0]

    def body(a_ref, g_ref, b_ref, n_ref, o_ref):
        n = _norm_bf16(a_ref, g_ref)
        n_ref[...] = n
        o_ref[...] = _dot_nt(n, b_ref[...])

    return _call(
        body, name=name, grid=(N // tn, T // tm),
        in_specs=[pl.BlockSpec((tm, K), lambda j, i: (i, 0)), pl.BlockSpec((1, K), lambda j, i: (0, 0)),
                  pl.BlockSpec((tn, K), lambda j, i: (j, 0))],
        out_specs=[pl.BlockSpec((None, tm, K), lambda j, i: (j, i, 0)), pl.BlockSpec((tm, tn), lambda j, i: (i, j))],
        out_shape=[jax.ShapeDtypeStruct((N // tn, T, K), BF16), jax.ShapeDtypeStruct((T, N), F32)],
        sem=("arbitrary", "arbitrary"), rides=rides,
    )(a, g, b)


def _norm_matmul_sq(a, g, b, *, tm, tn, name, rides=()):
    T, K = a.shape
    per = b.shape[2] // tn
    N = b.shape[0] * b.shape[2]

    def body(a_ref, g_ref, b_ref, nt_ref, o_ref, z_ref, zt_ref, n_scr):
        @pl.when(pl.program_id(1) == 0)
        def _():
            n = _norm_bf16(a_ref, g_ref)
            n_scr[...] = n
            nt_ref[...] = n.T
        r = jnp.maximum(_dot(n_scr[...], b_ref[...]), 0.0)
        o_ref[...] = r.astype(BF16)
        z = (r * r).astype(BF16)
        z_ref[...] = z
        zt_ref[...] = z.T

    return _call(
        body, name=name, grid=(T // tm, N // tn),
        in_specs=[pl.BlockSpec((tm, K), lambda i, j: (i, 0)), pl.BlockSpec((1, K), lambda i, j: (0, 0)),
                  pl.BlockSpec((None, K, tn), lambda i, j: (j // per, 0, j % per))],
        out_specs=[pl.BlockSpec((K, tm), lambda i, j: (0, i)), pl.BlockSpec((tm, tn), lambda i, j: (i, j)),
                   pl.BlockSpec((tm, tn), lambda i, j: (i, j)), pl.BlockSpec((tn, tm), lambda i, j: (j, i))],
        out_shape=[jax.ShapeDtypeStruct((K, T), BF16), jax.ShapeDtypeStruct((T, N), BF16),
                   jax.ShapeDtypeStruct((T, N), BF16), jax.ShapeDtypeStruct((N, T), BF16)],
        scratch_shapes=[pltpu.VMEM((tm, K), BF16)],
        sem=("parallel", "arbitrary"), rides=rides,
    )(a, g, b)


def _grad_pair(at, at_sib, b, b_sib, *, cols_sharded, tmo, tk, name, shards=None, into=None, rides=()):
    S, _, hr, T = at.shape
    C = b.shape[-1] // N_CHIPS if cols_sharded else b.shape[-1]
    nk = T // tk

    def shard(s):
        if shards is None:
            return s
        x, y = lax.axis_index("x"), lax.axis_index("y")
        first, second = ((2 * (1 - x) + y, 2 * x + (1 - y)) if shards == "near" else (2 * (1 - x) + (1 - y), 2 * x + y))
        return jnp.where(s == 0, first, second)

    a_sel = (lambda s: 0) if cols_sharded else shard
    b_sel = shard if cols_sharded else (lambda s: 0)
    if b.ndim == 3:
        b_spec = pl.BlockSpec((None, tk, C), lambda s, i, k: (0, k, b_sel(s)))
    else:
        b_spec = pl.BlockSpec((tk, C), lambda s, i, k: (k, b_sel(s)))
    n_into = 0 if into is None else 1

    def body(a_ref, as_ref, b_ref, bs_ref, *rest):
        o_ref, ob_ref = rest[n_into:]
        k = pl.program_id(2)
        p = _dot(a_ref[...], b_ref[...]) + _dot(as_ref[...], bs_ref[...])

        @pl.when(k == 0)
        def _():
            o_ref[...] = p

        @pl.when(k > 0)
        def _():
            o_ref[...] += p

        @pl.when(k == nk - 1)
        def _():
            ob_ref[...] = o_ref[...].astype(BF16)

    out = pl.BlockSpec((None, tmo, C), lambda s, i, k: (shard(s), i, 0))
    held = [pl.BlockSpec(memory_space=pl.ANY)] * n_into
    return _call(
        body, name=name, grid=(N_CHIPS if shards is None else 2, hr // tmo, nk),
        in_specs=[pl.BlockSpec((None, None, tmo, tk), lambda s, i, k: (a_sel(s), lax.axis_index("c"), i, k)),
                  pl.BlockSpec((None, tmo, tk), lambda s, i, k: (a_sel(s), i, k)),
                  b_spec, pl.BlockSpec((tk, C), lambda s, i, k: (k, b_sel(s)))] + held,
        out_specs=[out, out],
        out_shape=[jax.ShapeDtypeStruct((N_CHIPS, hr, C), F32), jax.ShapeDtypeStruct((N_CHIPS, hr, C), BF16)],
        sem=("parallel", "parallel", "arbitrary"), rides=rides, aliases={4: 0} if into is not None else None,
    )(at, at_sib, b, b_sib, *([into] if into is not None else []))


def _grad_pair_merged(at, at_sib, b, b_sib, *, tk, name, rides=()):
    S, _, hr, T = at.shape
    C = b.shape[-1]
    nk = T // tk

    def body(a_ref, as_ref, b_ref, bs_ref, o_ref, ob_ref):
        k = pl.program_id(0)
        p = (_dot(a_ref[...].reshape(S * hr, tk), b_ref[...])
             + _dot(as_ref[...].reshape(S * hr, tk), bs_ref[...])).reshape(S, hr, C)

        @pl.when(k == 0)
        def _():
            o_ref[...] = p

        @pl.when(k > 0)
        def _():
            o_ref[...] += p

        @pl.when(k == nk - 1)
        def _():
            ob_ref[...] = o_ref[...].astype(BF16)

    out = pl.BlockSpec((S, hr, C), lambda k: (0, 0, 0))
    return _call(
        body, name=name, grid=(nk,),
        in_specs=[pl.BlockSpec((S, None, hr, tk), lambda k: (0, lax.axis_index("c"), 0, k)),
                  pl.BlockSpec((S, hr, tk), lambda k: (0, 0, k)),
                  pl.BlockSpec((tk, C), lambda k: (k, 0)), pl.BlockSpec((tk, C), lambda k: (k, 0))],
        out_specs=[out, out],
        out_shape=[jax.ShapeDtypeStruct((S, hr, C), F32), jax.ShapeDtypeStruct((S, hr, C), BF16)],
        sem=("arbitrary",), rides=rides,
    )(at, at_sib, b, b_sib)


def _matmul_parts(parts, b, *, tm, tn, name, rides=()):
    T = parts[0].shape[0]
    N = b.shape[1]
    offs = [sum(p.shape[1] for p in parts[:i]) for i in range(len(parts))]
    assert all(o % p.shape[1] == 0 for o, p in zip(offs, parts))

    def body(*refs):
        n = len(parts)
        acc = _dot(refs[0][...], refs[n][...])
        for i in range(1, n):
            acc = acc + _dot(refs[i][...], refs[n + i][...])
        refs[-1][...] = acc

    a_specs = [pl.BlockSpec((tm, p.shape[1]), lambda i, j: (i, 0)) for p in parts]
    b_specs = [pl.BlockSpec((p.shape[1], tn), lambda i, j, r=o // p.shape[1]: (r, j)) for o, p in zip(offs, parts)]
    return _call(
        body, name=name, grid=(T // tm, N // tn), in_specs=a_specs + b_specs,
        out_specs=pl.BlockSpec((tm, tn), lambda i, j: (i, j)), out_shape=jax.ShapeDtypeStruct((T, N), F32),
        sem=("parallel", "parallel"), rides=rides,
    )(*parts, *([b] * len(parts)))


def _to_bf16(v):
    return v.astype(BF16)


def _matmul_res(a, b, res, *, tm, tn, tk, prologue, name, rides=()):
    T, K = a.shape
    N = b.shape[1]

    def body(a_ref, b_ref, res_ref, o_ref):
        k = pl.program_id(2)
        p = _dot(prologue(a_ref[...]), b_ref[...])

        @pl.when(k == 0)
        def _():
            o_ref[...] = res_ref[...] + p

        @pl.when(k > 0)
        def _():
            o_ref[...] += p

    return _call(
        body, name=name, grid=(T // tm, N // tn, K // tk),
        in_specs=[pl.BlockSpec((tm, tk), lambda i, j, k: (i, k)), pl.BlockSpec((tk, tn), lambda i, j, k: (k, j)),
                  pl.BlockSpec((tm, tn), lambda i, j, k: (i, j))],
        out_specs=pl.BlockSpec((tm, tn), lambda i, j, k: (i, j)),
        out_shape=jax.ShapeDtypeStruct((T, N), F32),
        sem=("parallel", "parallel", "arbitrary"), rides=rides,
    )(a, b, res)


def _matmul_nt(a, b, *, tm, tn, tk, name, extra=None, epilogue=None, out_dtype=F32, rides=()):
    T, K = a.shape
    two = b.ndim == 3 and tk == 2 * b.shape[2]
    if two:
        N, ks = b.shape[1], b.shape[2]
        b_specs = [pl.BlockSpec((None, tn, ks), lambda i, j, k: (2 * k, j, 0)),
                   pl.BlockSpec((None, tn, ks), lambda i, j, k: (2 * k + 1, j, 0))]
    elif b.ndim == 3:
        per = b.shape[2] // tk
        N = b.shape[1]
        b_specs = [pl.BlockSpec((None, tn, tk), lambda i, j, k: (k // per, j, k % per))]
    else:
        N = b.shape[0]
        b_specs = [pl.BlockSpec((tn, tk), lambda i, j, k: (j, k))]
    nb = len(b_specs)
    nk = K // tk
    assert out_dtype == F32 or nk == 1
    in_specs = [pl.BlockSpec((tm, tk), lambda i, j, k: (i, k))] + b_specs
    args = [a] + [b] * nb
    if extra is not None:
        in_specs.append(pl.BlockSpec((tm, tn), lambda i, j, k: (i, j)))
        args.append(extra)

    def body(*refs):
        a_ref, b_ref = refs[0], refs[1]
        o_ref = refs[-1]
        if two:
            p = (_dot_nt(a_ref[:, :tk // 2].astype(BF16), refs[1][...])
                 + _dot_nt(a_ref[:, tk // 2:].astype(BF16), refs[2][...]))
        else:
            p = _dot_nt(a_ref[...].astype(BF16), b_ref[...])
        if nk == 1:
            if epilogue is not None:
                p = epilogue(p, refs[1 + nb][...])
            o_ref[...] = p.astype(out_dtype)
        else:
            k = pl.program_id(2)

            @pl.when(k == 0)
            def _():
                o_ref[...] = p

            @pl.when(k > 0)
            def _():
                o_ref[...] += p

    return _call(
        body, name=name, grid=(T // tm, N // tn, nk),
        in_specs=in_specs,
        out_specs=pl.BlockSpec((tm, tn), lambda i, j, k: (i, j)),
        out_shape=jax.ShapeDtypeStruct((T, N), out_dtype),
        sem=("parallel", "parallel", "arbitrary"), rides=rides,
    )(*args)


def _loss_bwd(h2, tgt, g, *, tm):
    T, D = h2.shape

    def body(h_ref, t_ref, g_ref, dh_ref, dhb_ref, dg_ref, loss_ref):
        @pl.when(pl.program_id(0) == 0)
        def _():
            dg_ref[...] = jnp.zeros_like(dg_ref)
            loss_ref[...] = jnp.zeros_like(loss_ref)
        h = h_ref[...]
        gg = g_ref[...]
        r = lax.rsqrt(jnp.mean(h * h, axis=-1, keepdims=True) + EPS)
        hn = h * r
        err = hn * gg - t_ref[...]
        loss_ref[...] += 0.5 * jnp.sum(jnp.mean(err * err, axis=-1, keepdims=True), axis=0, keepdims=True)
        dy = err * (1.0 / D)
        dg_ref[...] += jnp.sum(dy * hn, axis=0, keepdims=True)
        w = dy * gg
        dh = r * w - h * ((r * r * r) * jnp.mean(w * h, axis=-1, keepdims=True))
        dh_ref[...] = dh
        dhb_ref[...] = dh.astype(BF16)

    tile = pl.BlockSpec((tm, D), lambda i: (i, 0))
    return pl.pallas_call(
        body, name="loss_bwd", grid=(T // tm,),
        in_specs=[tile, tile, pl.BlockSpec((1, D), lambda i: (0, 0))],
        out_specs=[tile, tile, pl.BlockSpec((1, D), lambda i: (0, 0)), pl.BlockSpec((1, 1), lambda i: (0, 0))],
        out_shape=[jax.ShapeDtypeStruct((T, D), F32), jax.ShapeDtypeStruct((T, D), BF16),
                   jax.ShapeDtypeStruct((1, D), F32), jax.ShapeDtypeStruct((1, 1), F32)],
        compiler_params=_params(("arbitrary",)),
    )(h2, tgt, g)


def _rms_bwd_res(dn, h, g, dres, *, tm, name, bf16_copy=True, rides=()):
    T, D = h.shape

    def body(dn_ref, h_ref, g_ref, dres_ref, dh_ref, *rest):
        dg_ref = rest[-1]

        @pl.when(pl.program_id(0) == 0)
        def _():
            dg_ref[...] = jnp.zeros_like(dg_ref)
        h_ = h_ref[...]
        dn_ = dn_ref[...]
        dh, r = _rms_bwd(dn_, h_, g_ref[...])
        dg_ref[...] += jnp.sum(dn_ * (h_ * r), axis=0, keepdims=True)
        dh = dres_ref[...] + dh
        dh_ref[...] = dh
        if bf16_copy:
            rest[0][...] = dh.astype(BF16)

    tile = pl.BlockSpec((tm, D), lambda i: (i, 0))
    row = pl.BlockSpec((1, D), lambda i: (0, 0))
    copy_spec = [tile] if bf16_copy else []
    copy_shape = [jax.ShapeDtypeStruct((T, D), BF16)] if bf16_copy else []
    return _call(
        body, name=name, grid=(T // tm,),
        in_specs=[tile, tile, row, tile], out_specs=[tile] + copy_spec + [row],
        out_shape=[jax.ShapeDtypeStruct((T, D), F32)] + copy_shape + [jax.ShapeDtypeStruct((1, D), F32)],
        sem=("arbitrary",), rides=rides,
    )(dn, h, g, dres)


def _rel_distance():
    i = lax.broadcasted_iota(jnp.int32, (CHUNK, 2 * CHUNK), 0)
    j = lax.broadcasted_iota(jnp.int32, (CHUNK, 2 * CHUNK), 1)
    return i + CHUNK - j


def _bias_build(table):
    def body(tab_ref, o_ref):
        rel = _rel_distance()
        j = lax.broadcasted_iota(jnp.int32, (CHUNK, 2 * CHUNK), 1)
        band = (rel >= 0) & (rel < CHUNK)
        ge = [rel >= t for t in BUCKET_THR]
        for h in range(B_HEADS):
            cur = jnp.full((CHUNK, 2 * CHUNK), tab_ref[0, h], F32)
            for b in range(1, N_BUCKETS):
                cur = jnp.where(ge[b - 1], tab_ref[b, h], cur)
            o_ref[0, h] = jnp.where(band & (j >= CHUNK), cur, NEG)
            o_ref[1, h] = jnp.where(band, cur, NEG)

    return pl.pallas_call(
        body, name="bias_build",
        in_specs=[pl.BlockSpec(memory_space=pltpu.SMEM)],
        out_specs=pl.BlockSpec(memory_space=pltpu.VMEM),
        out_shape=jax.ShapeDtypeStruct((2, B_HEADS, CHUNK, 2 * CHUNK), F32),
    )(table)


def _bias_grad(dbias):
    def body(db_ref, o_ref, acc_ref):
        rel = _rel_distance()
        lo = [0] + BUCKET_THR
        hi = BUCKET_THR + [CHUNK]
        for b in range(N_BUCKETS):
            m = (rel >= lo[b]) & (rel < hi[b])
            for h in range(B_HEADS):
                row = b * B_HEADS + h
                acc_ref[row:row + 1, :] = jnp.sum(jnp.where(m, db_ref[h], 0.0), axis=0, keepdims=True)
        o_ref[...] = jnp.sum(acc_ref[...], axis=1, keepdims=True)

    return pl.pallas_call(
        body, name="bias_grad",
        in_specs=[pl.BlockSpec(memory_space=pltpu.VMEM)],
        out_specs=pl.BlockSpec(memory_space=pltpu.VMEM),
        out_shape=jax.ShapeDtypeStruct((N_BUCKETS * B_HEADS, 1), F32),
        scratch_shapes=[pltpu.VMEM((N_BUCKETS * B_HEADS, 2 * CHUNK), F32)],
    )(dbias)


def _causal_mask():
    t = lax.broadcasted_iota(jnp.int32, (CHUNK, CHUNK), 0)
    s = lax.broadcasted_iota(jnp.int32, (CHUNK, CHUNK), 1)
    return s <= t


def _gate_forward(u, v, lg, lb, wc, bs):
    ug = _gelu(u)
    vg = _gelu(v)
    mu = jnp.mean(vg, axis=-1, keepdims=True)
    xc = vg - mu
    rstd = lax.rsqrt(jnp.mean(xc * xc, axis=-1, keepdims=True) + EPS)
    xhat = xc * rstd
    vl = (xhat * lg + lb).astype(BF16)
    mixed = _dot(wc, vl) + bs
    return ug, xhat, rstd, vl, mixed


def _softmax_scores(qk, bias, sink):
    s = qk + bias
    m = jnp.maximum(jnp.max(s, axis=-1, keepdims=True), sink)
    p = jnp.exp(s - m)
    e_sink = jnp.exp(sink - m)
    inv = 1.0 / (jnp.sum(p, axis=-1, keepdims=True) + e_sink)
    return p * inv, e_sink * inv


PAIRS = Q_PER_KV // 2


def _head(g, pr, e):
    return g * Q_PER_KV + 2 * pr + e


def _stack_pairs(ref, g, col0=0):
    w = 2 * HEAD_DIM
    return jnp.concatenate([ref[:, col0 + (g * PAIRS + pr) * w:col0 + (g * PAIRS + pr + 1) * w] for pr in range(PAIRS)],
                           axis=0)


def _low_lanes():
    return lax.broadcasted_iota(jnp.int32, (2 * CHUNK, 2 * HEAD_DIM), 1) < HEAD_DIM


def _band_operands(kv_prev, kv_cur):
    band = jnp.concatenate([kv_prev, kv_cur], axis=0)
    low = _low_lanes()
    ops = []
    for cat in (band[:, :KV_WIDTH], band[:, KV_WIDTH:]):
        rol = pltpu.roll(cat, HEAD_DIM, 1)
        ops.append([[jnp.where(low if e == 0 else ~low, cat if g == e else rol, 0.0).astype(BF16) for e in range(2)]
                    for g in range(2)])
    return ops


def _mixer_fwd(proj, lg, lb, wsp, bs_col, sinks, bias, ga, gb, rides=()):
    T = proj.shape[0]
    nb = T // CHUNK

    def body(u_ref, v_ref, q_ref, kvc_ref, kvp_ref, lg_ref, lb_ref, w_ref, bs_ref, sink_ref, bias_ref,
             ga_ref, gb_ref, mixed_ref, mixed_t_ref, ab_ref):
        causal = _causal_mask()
        ssq = jnp.zeros((CHUNK, 1), F32)
        for g in range(A_GROUPS):
            cols = slice(g * CHUNK, (g + 1) * CHUNK)
            wc = jnp.where(causal, w_ref[g], 0.0).astype(BF16)
            ug, _, _, _, mixed = _gate_forward(u_ref[:, cols], v_ref[:, cols], lg_ref[g:g + 1, :], lb_ref[g:g + 1, :],
                                               wc, bs_ref[g])
            a = ug * mixed
            ab_ref[:, cols] = a
            ssq = ssq + jnp.sum(a * a, axis=-1, keepdims=True)
        ra = lax.rsqrt(ssq * (1.0 / A_WIDTH) + EPS)
        mixed_ref[:, :A_WIDTH] = ((ab_ref[:, :A_WIDTH] * ra) * ga_ref[...]).astype(BF16)

        kops, vops = _band_operands(kvp_ref[...], kvc_ref[...])
        ssq = jnp.zeros((CHUNK, 1), F32)
        for g in range(B_HEADS // Q_PER_KV):
            qst = (_stack_pairs(q_ref, g) * SCALE).astype(BF16)
            o_st = jnp.zeros((PAIRS * CHUNK, 2 * HEAD_DIM), F32)
            for e in range(2):
                s_all = _dot_nt(qst, kops[g][e])
                ps = []
                for pr in range(PAIRS):
                    h = _head(g, pr, e)
                    p, _ = _softmax_scores(s_all[pr * CHUNK:(pr + 1) * CHUNK], bias_ref[h], sink_ref[0, h])
                    ps.append(p.astype(BF16))
                o_st = o_st + _dot(jnp.concatenate(ps, axis=0), vops[g][e])
            for pr in range(PAIRS):
                o = o_st[pr * CHUNK:(pr + 1) * CHUNK]
                c0 = A_WIDTH + (g * PAIRS + pr) * 2 * HEAD_DIM
                ab_ref[:, c0:c0 + 2 * HEAD_DIM] = o
                ssq = ssq + jnp.sum(o * o, axis=-1, keepdims=True)
        rb = lax.rsqrt(ssq * (1.0 / B_WIDTH) + EPS)
        mixed_ref[:, A_WIDTH:] = ((ab_ref[:, A_WIDTH:] * rb) * gb_ref[...]).astype(BF16)
        mixed_t_ref[...] = mixed_ref[...].T

    full = lambda *shape: pl.BlockSpec(shape, lambda n: (0,) * len(shape))
    return _call(
        body, name="mixer_fwd", grid=(nb,),
        in_specs=[pl.BlockSpec((CHUNK, A_WIDTH), lambda n: (n, 0)),
                  pl.BlockSpec((CHUNK, A_WIDTH), lambda n: (n, 1)),
                  pl.BlockSpec((CHUNK, B_WIDTH), lambda n: (n, 2)),
                  pl.BlockSpec((CHUNK, 2 * KV_WIDTH), lambda n: (n, 12)),
                  pl.BlockSpec((CHUNK, 2 * KV_WIDTH), lambda n: (jnp.maximum(n - 1, 0), 12)),
                  full(A_GROUPS, CHUNK), full(A_GROUPS, CHUNK), full(A_GROUPS, CHUNK, CHUNK), full(A_GROUPS, CHUNK, 1),
                  pl.BlockSpec(memory_space=pltpu.SMEM),
                  pl.BlockSpec((None, B_HEADS, CHUNK, 2 * CHUNK), lambda n: (jnp.minimum(n, 1), 0, 0, 0)),
                  full(1, A_WIDTH), full(1, B_WIDTH)],
        out_specs=[pl.BlockSpec((CHUNK, D_MODEL), lambda n: (n, 0)), pl.BlockSpec((D_MODEL, CHUNK), lambda n: (0, n)),
                   pl.BlockSpec((CHUNK, D_MODEL), lambda n: (n, 0))],
        out_shape=[jax.ShapeDtypeStruct((T, D_MODEL), BF16), jax.ShapeDtypeStruct((D_MODEL, T), BF16),
                   jax.ShapeDtypeStruct((T, D_MODEL), F32)],
        sem=("parallel",), rides=rides,
    )(proj, proj, proj, proj, proj, lg, lb, wsp, bs_col, sinks, bias, ga, gb)


def _gmlp_bwd(proj, ab, dmixed, ga, lg, lb, wsp, bs_col, rides=()):
    T = proj.shape[0]
    nb = T // CHUNK

    def body(u_ref, v_ref, a_ref, dna_ref, ga_ref, lg_ref, lb_ref, w_ref, bs_ref,
             dp_ref, dpt_ref, dga_ref, dw_ref, dbs_ref, dlg_ref, dlb_ref):
        @pl.when(pl.program_id(0) == 0)
        def _():
            for r in (dga_ref, dw_ref, dbs_ref, dlg_ref, dlb_ref):
                r[...] = jnp.zeros_like(r)
        causal = _causal_mask()
        a_all = a_ref[...]
        dna = dna_ref[...]
        da_all, ra = _rms_bwd(dna, a_all, ga_ref[...])
        dga_ref[...] += jnp.sum(dna * (a_all * ra), axis=0, keepdims=True)
        for g in range(A_GROUPS):
            cols = slice(g * CHUNK, (g + 1) * CHUNK)
            wc = jnp.where(causal, w_ref[g], 0.0).astype(BF16)
            lgg = lg_ref[g:g + 1, :]
            u = u_ref[:, cols]
            v = v_ref[:, cols]
            ug, xhat, rstd, vl, mixed = _gate_forward(u, v, lgg, lb_ref[g:g + 1, :], wc, bs_ref[g])
            da = da_all[:, cols]
            dug = da * mixed
            dmg = da * ug
            dmg_b = dmg.astype(BF16)
            dbs_ref[g] += jnp.sum(dmg, axis=-1, keepdims=True)
            dw_ref[g] += jnp.where(causal, _dot_nt(dmg_b, vl), 0.0)
            dvl = _dot_tn(wc, dmg_b)
            dlg_ref[g:g + 1, :] += jnp.sum(dvl * xhat, axis=0, keepdims=True)
            dlb_ref[g:g + 1, :] += jnp.sum(dvl, axis=0, keepdims=True)
            dxh = dvl * lgg
            dvg = rstd * (dxh - jnp.mean(dxh, axis=-1, keepdims=True)
                          - xhat * jnp.mean(dxh * xhat, axis=-1, keepdims=True))
            _, gu = _gelu_and_grad(u)
            _, gv = _gelu_and_grad(v)
            dp_ref[:, cols] = (dug * gu).astype(BF16)
            dp_ref[:, A_WIDTH + g * CHUNK:A_WIDTH + (g + 1) * CHUNK] = (dvg * gv).astype(BF16)
        dpt_ref[...] = dp_ref[...].T

    full = lambda *shape: pl.BlockSpec(shape, lambda n: (0,) * len(shape))
    return _call(
        body, name="gmlp_bwd", grid=(nb,),
        in_specs=[pl.BlockSpec((CHUNK, A_WIDTH), lambda n: (n, 0)),
                  pl.BlockSpec((CHUNK, A_WIDTH), lambda n: (n, 1)),
                  pl.BlockSpec((CHUNK, A_WIDTH), lambda n: (n, 0)),
                  pl.BlockSpec((CHUNK, A_WIDTH), lambda n: (n, 0)),
                  full(1, A_WIDTH), full(A_GROUPS, CHUNK), full(A_GROUPS, CHUNK), full(A_GROUPS, CHUNK, CHUNK),
                  full(A_GROUPS, CHUNK, 1)],
        out_specs=[pl.BlockSpec((CHUNK, 2 * A_WIDTH), lambda n: (n, 0)), pl.BlockSpec((2 * A_WIDTH, CHUNK), lambda n: (0, n)),
                   full(1, A_WIDTH), full(A_GROUPS, CHUNK, CHUNK), full(A_GROUPS, CHUNK, 1),
                   full(A_GROUPS, CHUNK), full(A_GROUPS, CHUNK)],
        out_shape=[jax.ShapeDtypeStruct((T, 2 * A_WIDTH), BF16), jax.ShapeDtypeStruct((2 * A_WIDTH, T), BF16),
                   jax.ShapeDtypeStruct((1, A_WIDTH), F32), jax.ShapeDtypeStruct((A_GROUPS, CHUNK, CHUNK), F32),
                   jax.ShapeDtypeStruct((A_GROUPS, CHUNK, 1), F32), jax.ShapeDtypeStruct((A_GROUPS, CHUNK), F32),
                   jax.ShapeDtypeStruct((A_GROUPS, CHUNK), F32)],
        sem=("arbitrary",), rides=rides,
    )(proj, proj, ab, dmixed, ga, lg, lb, wsp, bs_col)


def _attn_bwd(proj, ab, dmixed, gb, sinks, bias, rides=()):
    T = proj.shape[0]
    nb = T // CHUNK
    qn = lambda n: jnp.minimum(n, nb - 1)

    def body(q_ref, kvc_ref, kvp_ref, o_ref, dnb_ref, gb_ref, sink_ref, bias_ref,
             dq_ref, dkv_ref, dqt_ref, dkvt_ref, dgb_ref, dsink_ref, dbias_ref, carry_ref, sacc_ref):
        n = pl.program_id(0)

        @pl.when(n == 0)
        def _():
            carry_ref[...] = jnp.zeros_like(carry_ref)
            sacc_ref[...] = jnp.zeros_like(sacc_ref)
            dgb_ref[...] = jnp.zeros_like(dgb_ref)
            dbias_ref[...] = jnp.zeros_like(dbias_ref)

        @pl.when(n < nb)
        def _():
            o_all = o_ref[...]
            dnb = dnb_ref[...]
            do_all, rb = _rms_bwd(dnb, o_all, gb_ref[...])
            dgb_ref[...] += jnp.sum(dnb * (o_all * rb), axis=0, keepdims=True)
            kops, vops = _band_operands(kvp_ref[...], kvc_ref[...])
            low = _low_lanes()
            halves = []
            for g in range(B_HEADS // Q_PER_KV):
                qst = (_stack_pairs(q_ref, g) * SCALE).astype(BF16)
                dost = _stack_pairs(do_all, g).astype(BF16)
                dq_st = jnp.zeros((PAIRS * CHUNK, 2 * HEAD_DIM), F32)
                dk_e, dv_e = [], []
                for e in range(2):
                    s_all = _dot_nt(qst, kops[g][e])
                    dp_all = _dot_nt(dost, vops[g][e])
                    ps, dsrs = [], []
                    for pr in range(PAIRS):
                        h = _head(g, pr, e)
                        rows = slice(pr * CHUNK, (pr + 1) * CHUNK)
                        p, p_sink = _softmax_scores(s_all[rows], bias_ref[h], sink_ref[0, h])
                        dp = dp_all[rows]
                        delta = jnp.sum(p * dp, axis=-1, keepdims=True)
                        ds = p * (dp - delta)
                        sacc_ref[:, h:h + 1] += -(p_sink * delta)
                        dbias_ref[h] += ds
                        ps.append(p.astype(BF16))
                        dsrs.append(ds.astype(BF16))
                    dsr_all = jnp.concatenate(dsrs, axis=0)
                    dq_st = dq_st + _dot(dsr_all, kops[g][e])
                    dk_e.append(_dot_tn(dsr_all, qst))
                    dv_e.append(_dot_tn(jnp.concatenate(ps, axis=0), dost))
                for pr in range(PAIRS):
                    c0 = (g * PAIRS + pr) * 2 * HEAD_DIM
                    dq_ref[:, c0:c0 + 2 * HEAD_DIM] = (dq_st[pr * CHUNK:(pr + 1) * CHUNK] * SCALE).astype(BF16)
                halves.append((dk_e, dv_e))
            tiles = []
            for t in range(2):
                g0, g1 = halves[0][t], halves[1][t]
                tiles.append(jnp.where(low, g0[0] + pltpu.roll(g0[1], HEAD_DIM, 1), pltpu.roll(g1[0], HEAD_DIM, 1) + g1[1]))
            dband = jnp.concatenate(tiles, axis=1)
            dkv = (carry_ref[...] + dband[:CHUNK]).astype(BF16)
            dkv_ref[...] = dkv
            dkvt_ref[...] = dkv.T
            dqt_ref[...] = dq_ref[...].T
            carry_ref[...] = dband[CHUNK:]

        @pl.when(n == nb)
        def _():
            dkv = carry_ref[...].astype(BF16)
            dkv_ref[...] = dkv
            dkvt_ref[...] = dkv.T
            dsink_ref[...] = jnp.sum(sacc_ref[...], axis=0, keepdims=True)

    full = lambda *shape: pl.BlockSpec(shape, lambda n: (0,) * len(shape))
    return _call(
        body, name="attn_bwd", grid=(nb + 1,),
        in_specs=[pl.BlockSpec((CHUNK, B_WIDTH), lambda n: (qn(n), 2)),
                  pl.BlockSpec((CHUNK, 2 * KV_WIDTH), lambda n: (qn(n), 12)),
                  pl.BlockSpec((CHUNK, 2 * KV_WIDTH), lambda n: (jnp.maximum(qn(n) - 1, 0), 12)),
                  pl.BlockSpec((CHUNK, B_WIDTH), lambda n: (qn(n), 1)),
                  pl.BlockSpec((CHUNK, B_WIDTH), lambda n: (qn(n), 1)),
                  full(1, B_WIDTH), pl.BlockSpec(memory_space=pltpu.SMEM),
                  pl.BlockSpec((None, B_HEADS, CHUNK, 2 * CHUNK), lambda n: (jnp.minimum(n, 1), 0, 0, 0))],
        out_specs=[pl.BlockSpec((CHUNK, B_WIDTH), lambda n: (qn(n), 0)),
                   pl.BlockSpec((CHUNK, 2 * KV_WIDTH), lambda n: (jnp.maximum(n - 1, 0), 0)),
                   pl.BlockSpec((B_WIDTH, CHUNK), lambda n: (0, qn(n))),
                   pl.BlockSpec((2 * KV_WIDTH, CHUNK), lambda n: (0, jnp.maximum(n - 1, 0))),
                   full(1, B_WIDTH), full(1, B_HEADS), full(B_HEADS, CHUNK, 2 * CHUNK)],
        out_shape=[jax.ShapeDtypeStruct((T, B_WIDTH), BF16), jax.ShapeDtypeStruct((T, 2 * KV_WIDTH), BF16),
                   jax.ShapeDtypeStruct((B_WIDTH, T), BF16), jax.ShapeDtypeStruct((2 * KV_WIDTH, T), BF16),
                   jax.ShapeDtypeStruct((1, B_WIDTH), F32), jax.ShapeDtypeStruct((1, B_HEADS), F32),
                   jax.ShapeDtypeStruct((B_HEADS, CHUNK, 2 * CHUNK), F32)],
        scratch_shapes=[pltpu.VMEM((CHUNK, 2 * KV_WIDTH), F32), pltpu.VMEM((CHUNK, B_HEADS), F32)],
        sem=("arbitrary",), rides=rides,
    )(proj, proj, proj, ab, dmixed, gb, sinks, bias)


def _sq_relu_grad(acc, r):
    return acc * (2.0 * r.astype(F32))


def _chip_index():
    return (2 * lax.axis_index("x") + lax.axis_index("y")).astype(jnp.int32).reshape(1)


def _cast_into_slot(w, *, tm, name):
    _, R, C = w.shape

    def body(me_ref, w_ref, o_ref):
        del me_ref
        o_ref[...] = w_ref[...].astype(BF16)

    return pl.pallas_call(
        body, name=name,
        grid_spec=pltpu.PrefetchScalarGridSpec(
            num_scalar_prefetch=1, grid=(R // tm,),
            in_specs=[pl.BlockSpec((None, tm, C), lambda i, me: (0, i, 0))],
            out_specs=pl.BlockSpec((None, tm, C), lambda i, me: (me[0], i, 0))),
        out_shape=jax.ShapeDtypeStruct((N_CHIPS, R, C), BF16), compiler_params=_params(("parallel",)),
    )(_chip_index(), w)


def _cast_into_slots_carrying(ws, *, steps, name, rides):
    n = len(ws)

    def body(*refs):
        for w_ref, o_ref in zip(refs[:n], refs[n:]):
            o_ref[...] = w_ref[...].astype(BF16)

    me = lambda: 2 * lax.axis_index("x") + lax.axis_index("y")
    return _call(
        body, name=name, grid=(steps,),
        in_specs=[pl.BlockSpec((None, w.shape[1] // steps, w.shape[2]), lambda i: (0, i, 0)) for w in ws],
        out_specs=[pl.BlockSpec((None, w.shape[1] // steps, w.shape[2]), lambda i: (me(), i, 0)) for w in ws],
        out_shape=[jax.ShapeDtypeStruct((N_CHIPS,) + w.shape[1:], BF16) for w in ws], sem=("arbitrary",), rides=rides,
    )(*ws)


def _owner_total(gh, others, *, tm, name):
    _, hr, C = gh.shape

    def body(me_ref, g_ref, o_ref_in, out_ref):
        del me_ref
        acc = g_ref[...]
        for j in range(3):
            acc = acc + o_ref_in[j].astype(F32)
        out_ref[...] = acc

    return pl.pallas_call(
        body, name=name,
        grid_spec=pltpu.PrefetchScalarGridSpec(
            num_scalar_prefetch=1, grid=(hr // tm,),
            in_specs=[pl.BlockSpec((None, tm, C), lambda i, me: (me[0], i, 0)),
                      pl.BlockSpec((3, tm, C), lambda i, me: (0, i, 0))],
            out_specs=pl.BlockSpec((tm, C), lambda i, me: (i, 0))),
        out_shape=jax.ShapeDtypeStruct((hr, C), F32),
        compiler_params=_params(("parallel",)),
    )(_chip_index(), gh, others)


def _adamw_math(w, g, m, v):
    m = ADAM_B1 * m + (1.0 - ADAM_B1) * g
    v = ADAM_B2 * v + (1.0 - ADAM_B2) * (g * g)
    m_hat = m / (1.0 - ADAM_B1 ** ADAM_STEP)
    v_hat = v / (1.0 - ADAM_B2 ** ADAM_STEP)
    delta = -ADAM_LR * (m_hat / (jnp.sqrt(v_hat) + ADAM_EPS) + ADAM_WD * w)
    return delta, m, v


def _adamw_halves(w, own, got, m, v, *, tm, name, rides=()):
    _, R, C = w.shape
    nt = (R // 2) // tm

    def body(w_ref, own_ref, got_ref, m_ref, v_ref, g_ref, d_ref, nm_ref, nv_ref):
        g = jnp.where(pl.program_id(0) == lax.axis_index("c"), own_ref[...], got_ref[...])
        g_ref[...] = g
        d_ref[...], nm_ref[...], nv_ref[...] = _adamw_math(w_ref[...], g, m_ref[...], v_ref[...])

    whole = pl.BlockSpec((None, tm, C), lambda h, i: (0, h * nt + i, 0))
    half = pl.BlockSpec((tm, C), lambda h, i: (i, 0))
    return _call(
        body, name=name, grid=(2, nt), in_specs=[whole, half, half, whole, whole], out_specs=[whole] * 4,
        out_shape=[jax.ShapeDtypeStruct((1, R, C), F32)] * 4, sem=("parallel", "parallel"), rides=rides,
    )(w, own, got, m, v)


def _adamw_small(w, slots, m, v, *, name):
    def body(w_ref, slots_ref, m_ref, v_ref, g_ref, d_ref, nm_ref, nv_ref):
        g = slots_ref[0]
        for d in range(1, N_DEV):
            g = g + slots_ref[d]
        g_ref[...] = g
        d_ref[...], nm_ref[...], nv_ref[...] = _adamw_math(w_ref[...], g, m_ref[...], v_ref[...])

    vmem = pl.BlockSpec(memory_space=pltpu.VMEM)
    return pl.pallas_call(
        body, name=name, in_specs=[vmem] * 4, out_specs=[vmem] * 4,
        out_shape=[jax.ShapeDtypeStruct(w.shape, F32)] * 4, compiler_params=_params(),
    )(w, slots, m, v)


SMALL = ["rel_bias_table", "mix_norm_g", "gate_norm_g", "gate_norm_b", "w_spatial", "b_spatial", "attn_sinks",
         "out_norm_a_g", "out_norm_b_g", "ffn_norm_g", "final_norm_g"]
SMALL_A = ["gate_norm_g", "gate_norm_b", "w_spatial", "b_spatial", "out_norm_a_g"]
SMALL_B = ["rel_bias_table", "mix_norm_g", "attn_sinks", "out_norm_b_g", "ffn_norm_g", "final_norm_g"]
LARGE = ["w_in", "w_out", "w_up", "w_down"]
ROW_TILE = {"w_in": 208, "w_out": 256, "w_up": 256, "w_down": 256}
WEIGHTS = ["rel_bias_table", "mix_norm_g", "w_in", "gate_norm_g", "gate_norm_b", "w_spatial", "b_spatial", "attn_sinks",
           "out_norm_a_g", "out_norm_b_g", "w_out", "ffn_norm_g", "w_up", "w_down", "final_norm_g"]
PACK_UNIT = 8 * 128


def _pack(parts):
    rows = []
    for p in parts:
        flat = p.reshape(-1)
        pad = (-flat.shape[0]) % PACK_UNIT
        rows.append(jnp.pad(flat, (0, pad)).reshape(-1, 128))
    return jnp.concatenate(rows, axis=0)


def _unpack(packed, like):
    out, row = [], 0
    for p in like:
        n = math.prod(p.shape)
        nrows = (n + PACK_UNIT - 1) // PACK_UNIT * 8
        out.append(packed[row:row + nrows].reshape(-1)[:n].reshape(p.shape))
        row += nrows
    return out


def kernel(x, rel_bias_table, mix_norm_g, w_in, gate_norm_g, gate_norm_b, w_spatial, b_spatial, attn_sinks, out_norm_a_g, out_norm_b_g, w_out, ffn_norm_g, w_up, w_down, final_norm_g, loss_target, m_rel_bias_table, m_mix_norm_g, m_w_in, m_gate_norm_g, m_gate_norm_b, m_w_spatial, m_b_spatial, m_attn_sinks, m_out_norm_a_g, m_out_norm_b_g, m_w_out, m_ffn_norm_g, m_w_up, m_w_down, m_final_norm_g, v_rel_bias_table, v_mix_norm_g, v_w_in, v_gate_norm_g, v_gate_norm_b, v_w_spatial, v_b_spatial, v_attn_sinks, v_out_norm_a_g, v_out_norm_b_g, v_w_out, v_ffn_norm_g, v_w_up, v_w_down, v_final_norm_g):
    args = dict(locals())
    wts = {n: args[n] for n in WEIGHTS}
    mom = {n: args["m_" + n] for n in WEIGHTS}
    var = {n: args["v_" + n] for n in WEIGHTS}
    sp = {n: wts[n] for n in SMALL}
    x2, tgt = x[0], loss_target[0]
    T = x2.shape[0]
    tm = min(512, T)
    tl = min(1024, T)
    lg = sp["gate_norm_g"].reshape(A_GROUPS, CHUNK)
    lb = sp["gate_norm_b"].reshape(A_GROUPS, CHUNK)
    wsp = sp["w_spatial"].reshape(A_GROUPS, CHUNK, CHUNK)
    bs_col = sp["b_spatial"].reshape(A_GROUPS, CHUNK, 1)
    sinks = sp["attn_sinks"].reshape(1, B_HEADS)
    ga = sp["out_norm_a_g"].reshape(1, A_WIDTH)
    gb = sp["out_norm_b_g"].reshape(1, B_WIDTH)
    g1 = sp["mix_norm_g"].reshape(1, D_MODEL)
    g2 = sp["ffn_norm_g"].reshape(1, D_MODEL)
    gf = sp["final_norm_g"].reshape(1, D_MODEL)

    def owner_total(n, gh, others):
        return _owner_total(gh, others, tm=ROW_TILE[n], name="rs_owner_total_" + n)

    def halves_view(at, shards):
        return at.reshape(shards, 2, at.shape[0] // shards // 2, at.shape[1])

    for d in (wts, mom, var):
        d["w_in"] = jnp.swapaxes(d["w_in"], 1, 2)

    s_in = _cast_into_slot(wts["w_in"], tm=ROW_TILE["w_in"], name="cast_w_in")
    (s_out, s_up, s_down), ((g_in,),) = _cast_into_slots_carrying(
        [wts["w_out"], wts["w_up"], wts["w_down"]], steps=8, name="cast_w_rest",
        rides=[_ride_gather(s_in, chain=(0, 1, 1), chain_fracs=(0.3, 0.6))])
    win_t = g_in.reshape(PROJ_WIDTH, D_MODEL)
    bias = _bias_build(sp["rel_bias_table"])
    (n1, proj), ((s_out,), (s_up,)) = _norm_matmul_wide(
        x2, g1, win_t, tm=tm, tn=PROJ_WIDTH // 2, name="in_proj",
        rides=[_ride_gather(s_out, s1=(0, 1, 1)), _ride_gather(s_up, s1=(0, 3, 8))])
    (mixed, mixed_t, ab), ((g_out,), (s_up,), (s_down,), (n1_sib,)) = _mixer_fwd(
        proj, lg, lb, wsp, bs_col, sinks, bias, ga, gb,
        rides=[_ride_gather(s_out, tail=(0, 1, 1), mid_frac=0.5), _ride_gather(s_up, s2=(0, 3, 8), s1=(3, 8, 8)),
               _ride_gather(s_down, s1=(0, 2, 8)), _ride_to_sibling(n1, first=True)])
    wo = g_out.reshape(A_WIDTH + B_WIDTH, D_MODEL)
    mixed_t = halves_view(mixed_t, N_CHIPS)
    h1, ((wu,), (s_down,), (mixed_t_sib,)) = _matmul_res(
        mixed, wo, x2, tm=tl, tn=1024, tk=D_MODEL, prologue=_to_bf16, name="out_proj",
        rides=[_ride_gather(s_up, s3=(0, 3, 8), tail=(3, 8, 8), mid_frac=0.75), _ride_gather(s_down, s2=(0, 2, 8)),
               _ride_to_sibling(mixed_t, halves=True)])
    (n2t, zp, z2, z2t), ((g_down,),) = _norm_matmul_sq(
        h1, g2, wu, tm=tl, tn=1024, name="up_proj", rides=[_ride_gather(s_down, s3=(0, 2, 8), chain=(2, 8, 8), chain_fracs=(0.5, 0.8))])
    wd = g_down.reshape(D_FF, D_MODEL)
    n2t, z2t = halves_view(n2t, 1), halves_view(z2t, N_CHIPS)
    h2, ((n2t_sib,), (z2t_sib,)) = _matmul_res(
        z2, wd, h1, tm=tl, tn=1024, tk=4096, prologue=_to_bf16, name="down_proj",
        rides=[_ride_to_sibling(n2t, halves=True), _ride_to_sibling(z2t, halves=True)])

    dh2, dh2b, dgf, loss = _loss_bwd(h2, tgt, gf, tm=tm)
    dzp, ((dh2b_sib,),) = _matmul_nt(dh2b, wd, tm=tl, tn=1024, tk=D_MODEL, name="bwd_dz", extra=zp,
                                     epilogue=_sq_relu_grad, out_dtype=BF16, rides=[_ride_to_sibling(dh2b)])
    (gd, gdb), ((dzp_sib,),) = _grad_pair(z2t, z2t_sib, dh2b, dh2b_sib, cols_sharded=False, tmo=1024, tk=tl,
                                          name="grad_w_down", rides=[_ride_to_sibling(dzp)])
    (gu, gub), ((o_d,),) = _grad_pair(n2t, n2t_sib, dzp, dzp_sib, cols_sharded=True, tmo=1024, tk=tl,
                                      name="grad_w_up", rides=[_ride_scatter(gdb, None, (0, 7, 8))])
    dn2, ((o_d,), (o_u,)) = _matmul_nt(dzp, wu, tm=tl, tn=1024, tk=4096, name="bwd_dn2",
                                       rides=[_ride_scatter(gdb, o_d, (7, 8, 8)), _ride_scatter(gub, None, (0, 6, 8))])
    h_d = owner_total("w_down", gd, o_d)
    (dh1, dh1b, dg2), ((o_u,),) = _rms_bwd_res(dn2, h1, g2, dh2, tm=tm, name="ffn_norm_bwd",
                                               rides=[_ride_scatter(gub, o_u, (6, 7, 8))])
    dmixed, ((o_u,), (dh1b_sib,), (w_d,)) = _matmul_nt(
        dh1b, wo, tm=tl, tn=1024, tk=D_MODEL, name="bwd_dmixed",
        rides=[_ride_scatter(gub, o_u, (7, 8, 8)), _ride_to_sibling(dh1b), _ride_swap(h_d)])
    h_u = owner_total("w_up", gu, o_u)
    (go, gob), ((w_u,),) = _grad_pair_merged(mixed_t, mixed_t_sib, dh1b, dh1b_sib, tk=tl, name="grad_w_out",
                                             rides=[_ride_swap(h_u)])
    (duv, duv_t, dga, dwsp, dbs, dlg, dlb), ((o_o,),) = _gmlp_bwd(proj, ab, dmixed, ga, lg, lb, wsp, bs_col,
                                                                  rides=[_ride_scatter(gob)])
    h_o = owner_total("w_out", go, o_o)
    small = {"gate_norm_g": dlg, "gate_norm_b": dlb, "w_spatial": dwsp, "b_spatial": dbs, "out_norm_a_g": dga}
    hr_in = PROJ_WIDTH // N_CHIPS // 2
    (dq, dkv, dq_t, dkv_t, dgb, dsinks, dbias), ((slots_a,), (dproj_t_sib,)) = _attn_bwd(
        proj, ab, dmixed, gb, sinks, bias,
        rides=[_ride_small_to_all(_pack([small[n] for n in SMALL_A])), _ride_rows_to_sibling(duv_t, hr_in, 2, N_CHIPS)])
    dtable = _bias_grad(dbias)
    dproj_t = halves_view(jnp.concatenate([duv_t, dq_t, dkv_t], axis=0), N_CHIPS)
    ((dproj_t_sib,),) = _carrier([_ride_to_sibling(dproj_t, halves=True, shards=(2, N_CHIPS), land=dproj_t_sib)],
                                 name="trade_dproj_t")
    (gi, gib_near), ((w_o,),) = _grad_pair(
        dproj_t, dproj_t_sib, n1, n1_sib, cols_sharded=False, tmo=hr_in, tk=tl, name="grad_w_in_near", shards="near",
        rides=[_ride_swap(h_o)])
    (gi, gib_far), ((o_i,),) = _grad_pair(
        dproj_t, dproj_t_sib, n1, n1_sib, cols_sharded=False, tmo=hr_in, tk=tl, name="grad_w_in_far", shards="far",
        into=gi, rides=[_ride_scatter(gib_near, None, to=(0, 1))])
    dn1, ((o_i,),) = _matmul_parts([duv, dq, dkv], win_t, tm=tl, tn=1024, name="bwd_dn1",
                                   rides=[_ride_scatter(gib_far, o_i, to=(2,))])
    h_i = owner_total("w_in", gi, o_i)
    dx, dg1 = _rms_bwd_res(dn1, x2, g1, dh1, tm=tm, name="mix_norm_bwd", bf16_copy=False)
    small.update({"rel_bias_table": dtable.reshape(N_BUCKETS, B_HEADS), "mix_norm_g": dg1, "attn_sinks": dsinks,
                  "out_norm_b_g": dgb, "ffn_norm_g": dg2, "final_norm_g": dgf})
    (w_i,), (slots_b,) = _carrier([_ride_swap(h_i), _ride_small_to_all(_pack([small[n] for n in SMALL_B] + [loss]))],
                                  name="swap_w_in")

    out_g, out_d, out_m, out_v = {}, {}, {}, {}
    for n, h, s in zip(LARGE, [h_i, h_o, h_u, h_d], [w_i, w_o, w_u, w_d]):
        res = _adamw_halves(wts[n], h, s, mom[n], var[n], tm=ROW_TILE[n], name="adamw_" + n)
        if n == "w_in":
            res = [jnp.swapaxes(r, 1, 2) for r in res]
        out_g[n], out_d[n], out_m[n], out_v[n] = res
    for names, slots, tag in ((SMALL_A, slots_a, "a"), (SMALL_B, slots_b, "b")):
        extra = [jnp.zeros((1, 1), F32)] if tag == "b" else []
        like = [wts[n] for n in names] + extra
        res = _adamw_small(_pack(like), slots, _pack([mom[n] for n in names] + extra),
                           _pack([var[n] for n in names] + extra), name="adamw_small_" + tag)
        for store, packed in zip((out_g, out_d, out_m, out_v), res):
            for n, val in zip(names + ["loss"], _unpack(packed, like)):
                store[n] = val

    total = out_g["loss"][0, 0]
    return (total, dx[None], *[out_g[n] for n in WEIGHTS], *[out_d[n] for n in WEIGHTS],
            *[out_m[n] for n in WEIGHTS], *[out_v[n] for n in WEIGHTS])
```

```python
import math

import numpy as np
import jax
import jax.numpy as jnp
from jax import lax
from jax.experimental import pallas as pl
from jax.experimental.pallas import tpu as pltpu

F32 = jnp.float32
BF16 = jnp.bfloat16

D_MODEL = 2048
CHUNK = 128
A_GROUPS = 8
A_WIDTH = 1024
HEAD_DIM = 64
B_HEADS = 16
Q_PER_KV = 8
B_WIDTH = 1024
KV_WIDTH = 128
PROJ_WIDTH = 3328
D_FF = 8192
N_BUCKETS = 32
EPS = 1e-5
NEG = -1e30
SCALE = HEAD_DIM ** -0.5
N_CHIPS = 4
N_DEV = 8

ADAM_LR = 0.001
ADAM_B1 = 0.9
ADAM_B2 = 0.999
ADAM_EPS = 1e-08
ADAM_WD = 0.01
ADAM_STEP = 10

VMEM_LIMIT = 60 * 1024 * 1024
MESH = pl.DeviceIdType.MESH


def _bucket_thresholds():
    d = np.arange(CHUNK)
    n_exact = N_BUCKETS // 2
    relf = np.maximum(d, n_exact).astype(np.float64)
    large = n_exact + (np.log(relf / n_exact) / math.log(CHUNK / n_exact) * (N_BUCKETS - n_exact)).astype(np.int32)
    bucket = np.where(d < n_exact, d, np.minimum(large, N_BUCKETS - 1))
    return [int(np.min(d[bucket >= b])) for b in range(1, N_BUCKETS)]


BUCKET_THR = _bucket_thresholds()


def _params(sem=None):
    return pltpu.CompilerParams(dimension_semantics=sem, vmem_limit_bytes=VMEM_LIMIT)


def _gelu(x):
    c = math.sqrt(2.0 / math.pi)
    return 0.5 * x * (1.0 + jnp.tanh(c * (x + 0.044715 * (x * x * x))))


def _gelu_and_grad(x):
    c = math.sqrt(2.0 / math.pi)
    x2 = x * x
    t = jnp.tanh(c * (x + 0.044715 * (x2 * x)))
    g = 0.5 * x * (1.0 + t)
    dg = 0.5 * (1.0 + t) + 0.5 * x * (1.0 - t * t) * (c * (1.0 + 3.0 * 0.044715 * x2))
    return g, dg


def _dot(a, b):
    return jnp.dot(a, b, preferred_element_type=F32)


def _dot_nt(a, b):
    return lax.dot_general(a, b, (((1,), (1,)), ((), ())), preferred_element_type=F32)


def _dot_tn(a, b):
    return lax.dot_general(a, b, (((0,), (0,)), ((), ())), preferred_element_type=F32)


def _rms_bwd(dn, h, g):
    r = lax.rsqrt(jnp.mean(h * h, axis=-1, keepdims=True) + EPS)
    w = dn * g
    dh = r * w - h * ((r * r * r) * jnp.mean(w * h, axis=-1, keepdims=True))
    return dh, r


def _place():
    x, y, c = lax.axis_index("x"), lax.axis_index("y"), lax.axis_index("c")
    chips = [(1 - x, y), (x, 1 - y), (1 - x, 1 - y)]
    return x, y, c, chips


def _remote(src, dst, send_sem, recv_sem, to):
    return pltpu.make_async_remote_copy(src_ref=src, dst_ref=dst, send_sem=send_sem, recv_sem=recv_sem,
                                        device_id=to, device_id_type=MESH)


class _Ride:
    def __init__(self, args, out_shape, n_sem, start, finish, mids=(), aliases=None):
        self.args, self.out_shape, self.n_sem = list(args), list(out_shape), n_sem
        self.start, self.mids, self.finish = start, list(mids), finish
        self.aliases = dict(aliases or {})


def _call(body, *, name, grid, in_specs, out_specs, out_shape, scratch_shapes=(), sem=None, rides=(), aliases=None):
    single = not isinstance(out_shape, (list, tuple))
    out_specs = [out_specs] if single else list(out_specs)
    out_shape = [out_shape] if single else list(out_shape)
    n_in, n_out, n_scr = len(in_specs), len(out_shape), len(scratch_shapes)
    r_in = [len(r.args) for r in rides]
    r_out = [len(r.out_shape) for r in rides]
    any_spec = pl.BlockSpec(memory_space=pl.ANY)
    aliases, off_i, off_o = dict(aliases or {}), n_in, n_out
    for r in rides:
        for i, o in r.aliases.items():
            aliases[off_i + i] = off_o + o
        off_i += len(r.args)
        off_o += len(r.out_shape)
    steps = math.prod(grid)

    def wrapped(*refs):
        p = 0
        ins = refs[p:p + n_in]; p += n_in
        rins = refs[p:p + sum(r_in)]; p += sum(r_in)
        outs = refs[p:p + n_out]; p += n_out
        routs = refs[p:p + sum(r_out)]; p += sum(r_out)
        scr = refs[p:p + n_scr]; p += n_scr
        sems = refs[p:]
        parts, pi, po = [], 0, 0
        for k, r in enumerate(rides):
            parts.append((rins[pi:pi + r_in[k]], routs[po:po + r_out[k]], sems[2 * k], sems[2 * k + 1]))
            pi += r_in[k]
            po += r_out[k]
        lin = 0
        for d in range(len(grid)):
            lin = lin * grid[d] + pl.program_id(d)
        if rides:
            @pl.when(lin == 0)
            def _():
                for r, part in zip(rides, parts):
                    r.start(*part)
        body(*ins, *outs, *scr)
        for r, part in zip(rides, parts):
            for frac, fn in r.mids:
                @pl.when(lin == min(steps - 1, int(frac * steps)))
                def _(fn=fn, part=part):
                    fn(*part)
        if rides:
            @pl.when(lin == steps - 1)
            def _():
                for r, part in zip(rides, parts):
                    r.finish(*part)

    scratch = list(scratch_shapes)
    for r in rides:
        scratch += [pltpu.SemaphoreType.DMA((r.n_sem,)), pltpu.SemaphoreType.DMA((r.n_sem,))]
    if rides:
        sem = ("arbitrary",) * len(grid)
    res = pl.pallas_call(
        wrapped, name=name, grid=grid,
        in_specs=list(in_specs) + [any_spec] * sum(r_in),
        out_specs=out_specs + [any_spec] * sum(r_out),
        out_shape=out_shape + [s for r in rides for s in r.out_shape],
        scratch_shapes=scratch, input_output_aliases=aliases,
        compiler_params=_params(sem),
    )

    def run(*args):
        got = res(*args, *[a for r in rides for a in r.args])
        mine = got[0] if single else list(got[:n_out])
        if not rides:
            return mine
        rest, out = list(got[n_out:]), []
        for k in range(len(rides)):
            out.append(rest[:r_out[k]])
            rest = rest[r_out[k]:]
        return mine, out

    return run


def _ride_gather(slot, s1=None, s2=None, s3=None, tail=None, chain=None, mid_frac=0.6, chain_fracs=(0.35, 0.7)):
    half = slot.shape[1] // 2

    def rows(part, c, which=None):
        k0, k1, n = part
        count, first = (k1 - k0) * (half // n), c * half + k0 * (half // n)
        return pl.ds(first, count) if which is None else pl.ds(first + which * (count // 2), count // 2)

    def ids():
        x, y, c, _ = _place()
        return x, y, c, 2 * x + y, 2 * (1 - x) + y, 2 * x + (1 - y), 2 * (1 - x) + (1 - y)

    def copy(full, chip, r, ss, rs, k, to):
        piece = full.at[chip, r, :]
        return _remote(piece, piece, ss.at[k], rs.at[k], to)

    def to_neighbours(full, ss, rs, part, base):
        x, y, c, me, _, _, _ = ids()
        return [copy(full, me, rows(part, c), ss, rs, base, (1 - x, y, c)),
                copy(full, me, rows(part, c), ss, rs, base + 1, (x, 1 - y, c))]

    def from_neighbours(full, ss, rs, part, base):
        x, y, c, _, cx, cy, _ = ids()
        return [copy(full, cx, rows(part, c), ss, rs, base, (x, y, c)), copy(full, cy, rows(part, c), ss, rs, base + 1, (x, y, c))]

    def onward(full, ss, rs, part, base):
        x, y, c, _, cx, cy, _ = ids()
        return [copy(full, cx, rows(part, c, 0), ss, rs, base, (x, 1 - y, c)),
                copy(full, cy, rows(part, c, 1), ss, rs, base + 1, (1 - x, y, c))]

    def from_onward(full, ss, rs, part, base):
        x, y, c, _, _, _, cd = ids()
        return [copy(full, cd, rows(part, c, 0), ss, rs, base, (x, y, c)), copy(full, cd, rows(part, c, 1), ss, rs, base + 1, (x, y, c))]

    def to_sibling(full, ss, rs, part, base, diagonal):
        x, y, c, _, cx, cy, cd = ids()
        return [copy(full, chip, rows(part, c), ss, rs, base + j, (x, y, 1 - c))
                for j, chip in enumerate([cd] if diagonal else [cx, cy])]

    def from_sibling(full, ss, rs, part, base, diagonal):
        x, y, c, _, cx, cy, cd = ids()
        return [copy(full, chip, rows(part, 1 - c), ss, rs, base + j, (x, y, c))
                for j, chip in enumerate([cd] if diagonal else [cx, cy])]

    def start(ins, outs, ss, rs):
        full, cps = outs[0], []
        for part, base in ((s1, 0), (chain, 12)):
            if part is not None:
                cps += to_neighbours(full, ss, rs, part, base)
        for part, b_ici, b_sib in ((s2, 2, 4), (tail, 7, 9)):
            if part is not None:
                cps += onward(full, ss, rs, part, b_ici) + to_sibling(full, ss, rs, part, b_sib, False)
        if s3 is not None:
            cps += to_sibling(full, ss, rs, s3, 6, True)
        for cp in cps:
            cp.start()

    def second(part, b_in, b_ici, b_sib):
        def fn(ins, outs, ss, rs):
            for cp in from_neighbours(outs[0], ss, rs, part, b_in):
                cp.wait_recv()
            for cp in onward(outs[0], ss, rs, part, b_ici) + to_sibling(outs[0], ss, rs, part, b_sib, False):
                cp.start()
        return fn

    def third(part, b_ici, b_sib):
        def fn(ins, outs, ss, rs):
            for cp in from_onward(outs[0], ss, rs, part, b_ici):
                cp.wait_recv()
            for cp in to_sibling(outs[0], ss, rs, part, b_sib, True):
                cp.start()
        return fn

    mids = []
    if tail is not None:
        mids.append((mid_frac, third(tail, 7, 11)))
    if chain is not None:
        mids += [(chain_fracs[0], second(chain, 12, 14, 16)), (chain_fracs[1], third(chain, 14, 18))]

    def finish(ins, outs, ss, rs):
        full, got, sent = outs[0], [], []
        if s1 is not None:
            got += from_neighbours(full, ss, rs, s1, 0)
            sent += to_neighbours(full, ss, rs, s1, 0)
        if s2 is not None:
            got += from_onward(full, ss, rs, s2, 2) + from_sibling(full, ss, rs, s2, 4, False)
            sent += onward(full, ss, rs, s2, 2) + to_sibling(full, ss, rs, s2, 4, False)
        if s3 is not None:
            got += from_sibling(full, ss, rs, s3, 6, True)
            sent += to_sibling(full, ss, rs, s3, 6, True)
        if tail is not None:
            got += from_sibling(full, ss, rs, tail, 9, False) + from_sibling(full, ss, rs, tail, 11, True)
            sent += onward(full, ss, rs, tail, 7) + to_sibling(full, ss, rs, tail, 9, False) + to_sibling(full, ss, rs, tail, 11, True)
        if chain is not None:
            got += from_sibling(full, ss, rs, chain, 16, False) + from_sibling(full, ss, rs, chain, 18, True)
            sent += (to_neighbours(full, ss, rs, chain, 12) + onward(full, ss, rs, chain, 14)
                     + to_sibling(full, ss, rs, chain, 16, False) + to_sibling(full, ss, rs, chain, 18, True))
        for cp in got:
            cp.wait_recv()
        for cp in sent:
            cp.wait_send()

    return _Ride([slot], [jax.ShapeDtypeStruct(slot.shape, slot.dtype)], 19, start, finish, mids=mids, aliases={0: 0})


def _ride_scatter(q, land=None, part=(0, 1), to=(0, 1, 2)):
    k0, k1, n = part if len(part) == 3 else (part[0], part[0] + 1, part[1])
    rows_n = q.shape[1] // n
    rows = pl.ds(k0 * rows_n, (k1 - k0) * rows_n)

    def copies(ins, outs, ss, rs):
        x, y, c, chips = _place()
        return [_remote(ins[0].at[2 * chip[0] + chip[1], rows, :], outs[0].at[j, rows, :], ss.at[j], rs.at[j], (*chip, c))
                for j, chip in enumerate(chips) if j in to]

    def start(*a):
        for cp in copies(*a):
            cp.start()

    def finish(*a):
        for cp in copies(*a):
            cp.wait()

    shape = jax.ShapeDtypeStruct((3,) + q.shape[1:], q.dtype)
    if land is None:
        return _Ride([q], [shape], 3, start, finish)
    return _Ride([q, land], [shape], 3, start, finish, aliases={1: 0})


def _ride_to_sibling(a, halves=False, first=False, shards=None, land=None):
    s0, s1 = shards or (0, a.shape[0])

    def copy(ins, outs, ss, rs):
        x, y, c, _ = _place()
        if halves:
            src, dst = ins[0].at[s0:s1, 1 - c], outs[0].at[s0:s1]
        else:
            src, dst = (ins[0].at[0] if first else ins[0]), outs[0]
        return _remote(src, dst, ss.at[0], rs.at[0], (x, y, 1 - c))

    shape = (a.shape[0],) + a.shape[2:] if halves else (a.shape[1:] if first else a.shape)
    return _Ride([a] if land is None else [a, land], [jax.ShapeDtypeStruct(shape, a.dtype)], 1,
                 lambda *a_: copy(*a_).start(), lambda *a_: copy(*a_).wait(), aliases=None if land is None else {1: 0})


def _ride_rows_to_sibling(a, hr, shards, total):
    def copies(ins, outs, ss, rs):
        x, y, c, _ = _place()
        return [_remote(ins[0].at[pl.ds((2 * s + 1 - c) * hr, hr), :], outs[0].at[s], ss.at[s], rs.at[s], (x, y, 1 - c))
                for s in range(shards)]

    def start(*a_):
        for cp in copies(*a_):
            cp.start()

    def finish(*a_):
        for cp in copies(*a_):
            cp.wait()

    return _Ride([a], [jax.ShapeDtypeStruct((total, hr, a.shape[1]), a.dtype)], shards, start, finish)


def _ride_swap(h):
    def copy(ins, outs, ss, rs):
        x, y, c, _ = _place()
        return _remote(ins[0], outs[0], ss.at[0], rs.at[0], (x, y, 1 - c))

    return _Ride([h], [jax.ShapeDtypeStruct(h.shape, h.dtype)], 1,
                 lambda *a: copy(*a).start(), lambda *a: copy(*a).wait())


def _mesh_place(p):
    return (p // 4, (p // 2) % 2, p % 2)


def _ride_small_to_all(packed):
    def copies(ins, outs, ss, rs):
        x, y, c, _ = _place()
        me = 4 * x + 2 * y + c
        return [_remote(ins[0], outs[0].at[me], ss.at[k - 1], rs.at[k - 1], _mesh_place((me + k) % N_DEV))
                for k in range(1, N_DEV)]

    def own(ins, outs, ss, rs):
        x, y, c, _ = _place()
        return pltpu.make_async_copy(ins[0], outs[0].at[4 * x + 2 * y + c], ss.at[N_DEV - 1])

    def start(*a):
        own(*a).start()
        for cp in copies(*a):
            cp.start()

    def finish(ins, outs, ss, rs):
        x, y, c, _ = _place()
        me = 4 * x + 2 * y + c
        for k in range(1, N_DEV):
            _remote(ins[0], outs[0].at[(me + N_DEV - k) % N_DEV], ss.at[k - 1], rs.at[k - 1], (x, y, c)).wait_recv()
        for cp in copies(ins, outs, ss, rs):
            cp.wait_send()
        own(ins, outs, ss, rs).wait()

    return _Ride([packed], [jax.ShapeDtypeStruct((N_DEV,) + packed.shape, packed.dtype)], N_DEV, start, finish)


def _carrier(rides, *, name):
    _, outs = _call(lambda: None, name=name, grid=(1,), in_specs=[], out_specs=[], out_shape=[], rides=rides)()
    return outs


def _norm_bf16(a_ref, g_ref):
    xf = a_ref[...]
    r = lax.rsqrt(jnp.mean(xf * xf, axis=-1, keepdims=True) + EPS)
    return ((xf * r) * g_ref[...]).astype(BF16)


def _norm_matmul_wide(a, g, b, *, tm, tn, name, rides=()):
    T, K = a.shape
    N = b.shape[0]

    def body(a_ref, g_ref, b_ref, n_ref, o_ref):
        n = _norm_bf16(a_ref, g_ref)
        n_ref[...] = n
        o_ref[...] = _dot_nt(n, b_ref[...])

    return _call(
        body, name=name, grid=(N // tn, T // tm),
        in_specs=[pl.BlockSpec((tm, K), lambda j, i: (i, 0)), pl.BlockSpec((1, K), lambda j, i: (0, 0)),
                  pl.BlockSpec((tn, K), lambda j, i: (j, 0))],
        out_specs=[pl.BlockSpec((None, tm, K), lambda j, i: (j, i, 0)), pl.BlockSpec((tm, tn), lambda j, i: (i, j))],
        out_shape=[jax.ShapeDtypeStruct((N // tn, T, K), BF16), jax.ShapeDtypeStruct((T, N), F32)],
        sem=("arbitrary", "arbitrary"), rides=rides,
    )(a, g, b)


def _norm_matmul_sq(a, g, b, *, tm, tn, name, rides=()):
    T, K = a.shape
    per = b.shape[2] // tn
    N = b.shape[0] * b.shape[2]

    def body(a_ref, g_ref, b_ref, nt_ref, o_ref, z_ref, zt_ref, n_scr):
        @pl.when(pl.program_id(1) == 0)
        def _():
            n = _norm_bf16(a_ref, g_ref)
            n_scr[...] = n
            nt_ref[...] = n.T
        r = jnp.maximum(_dot(n_scr[...], b_ref[...]), 0.0)
        o_ref[...] = r.astype(BF16)
        z = (r * r).astype(BF16)
        z_ref[...] = z
        zt_ref[...] = z.T

    return _call(
        body, name=name, grid=(T // tm, N // tn),
        in_specs=[pl.BlockSpec((tm, K), lambda i, j: (i, 0)), pl.BlockSpec((1, K), lambda i, j: (0, 0)),
                  pl.BlockSpec((None, K, tn), lambda i, j: (j // per, 0, j % per))],
        out_specs=[pl.BlockSpec((K, tm), lambda i, j: (0, i)), pl.BlockSpec((tm, tn), lambda i, j: (i, j)),
                   pl.BlockSpec((tm, tn), lambda i, j: (i, j)), pl.BlockSpec((tn, tm), lambda i, j: (j, i))],
        out_shape=[jax.ShapeDtypeStruct((K, T), BF16), jax.ShapeDtypeStruct((T, N), BF16),
                   jax.ShapeDtypeStruct((T, N), BF16), jax.ShapeDtypeStruct((N, T), BF16)],
        scratch_shapes=[pltpu.VMEM((tm, K), BF16)],
        sem=("parallel", "arbitrary"), rides=rides,
    )(a, g, b)


def _grad_pair(at, at_sib, b, b_sib, *, cols_sharded, tmo, tk, name, shards=None, into=None, rides=()):
    S, _, hr, T = at.shape
    C = b.shape[-1] // N_CHIPS if cols_sharded else b.shape[-1]
    nk = T // tk

    def shard(s):
        if shards is None:
            return s
        x, y = lax.axis_index("x"), lax.axis_index("y")
        first, second = ((2 * (1 - x) + y, 2 * x + (1 - y)) if shards == "near" else (2 * (1 - x) + (1 - y), 2 * x + y))
        return jnp.where(s == 0, first, second)

    a_sel = (lambda s: 0) if cols_sharded else shard
    b_sel = shard if cols_sharded else (lambda s: 0)
    if b.ndim == 3:
        b_spec = pl.BlockSpec((None, tk, C), lambda s, i, k: (0, k, b_sel(s)))
    else:
        b_spec = pl.BlockSpec((tk, C), lambda s, i, k: (k, b_sel(s)))
    n_into = 0 if into is None else 1

    def body(a_ref, as_ref, b_ref, bs_ref, *rest):
        o_ref, ob_ref = rest[n_into:]
        k = pl.program_id(2)
        p = _dot(a_ref[...], b_ref[...]) + _dot(as_ref[...], bs_ref[...])

        @pl.when(k == 0)
        def _():
            o_ref[...] = p

        @pl.when(k > 0)
        def _():
            o_ref[...] += p

        @pl.when(k == nk - 1)
        def _():
            ob_ref[...] = o_ref[...].astype(BF16)

    out = pl.BlockSpec((None, tmo, C), lambda s, i, k: (shard(s), i, 0))
    held = [pl.BlockSpec(memory_space=pl.ANY)] * n_into
    return _call(
        body, name=name, grid=(N_CHIPS if shards is None else 2, hr // tmo, nk),
        in_specs=[pl.BlockSpec((None, None, tmo, tk), lambda s, i, k: (a_sel(s), lax.axis_index("c"), i, k)),
                  pl.BlockSpec((None, tmo, tk), lambda s, i, k: (a_sel(s), i, k)),
                  b_spec, pl.BlockSpec((tk, C), lambda s, i, k: (k, b_sel(s)))] + held,
        out_specs=[out, out],
        out_shape=[jax.ShapeDtypeStruct((N_CHIPS, hr, C), F32), jax.ShapeDtypeStruct((N_CHIPS, hr, C), BF16)],
        sem=("parallel", "parallel", "arbitrary"), rides=rides, aliases={4: 0} if into is not None else None,
    )(at, at_sib, b, b_sib, *([into] if into is not None else []))


def _grad_pair_merged(at, at_sib, b, b_sib, *, tk, name, rides=()):
    S, _, hr, T = at.shape
    C = b.shape[-1]
    nk = T // tk

    def body(a_ref, as_ref, b_ref, bs_ref, o_ref, ob_ref):
        k = pl.program_id(0)
        p = (_dot(a_ref[...].reshape(S * hr, tk), b_ref[...])
             + _dot(as_ref[...].reshape(S * hr, tk), bs_ref[...])).reshape(S, hr, C)

        @pl.when(k == 0)
        def _():
            o_ref[...] = p

        @pl.when(k > 0)
        def _():
            o_ref[...] += p

        @pl.when(k == nk - 1)
        def _():
            ob_ref[...] = o_ref[...].astype(BF16)

    out = pl.BlockSpec((S, hr, C), lambda k: (0, 0, 0))
    return _call(
        body, name=name, grid=(nk,),
        in_specs=[pl.BlockSpec((S, None, hr, tk), lambda k: (0, lax.axis_index("c"), 0, k)),
                  pl.BlockSpec((S, hr, tk), lambda k: (0, 0, k)),
                  pl.BlockSpec((tk, C), lambda k: (k, 0)), pl.BlockSpec((tk, C), lambda k: (k, 0))],
        out_specs=[out, out],
        out_shape=[jax.ShapeDtypeStruct((S, hr, C), F32), jax.ShapeDtypeStruct((S, hr, C), BF16)],
        sem=("arbitrary",), rides=rides,
    )(at, at_sib, b, b_sib)


def _matmul_parts(parts, b, *, tm, tn, name, rides=()):
    T = parts[0].shape[0]
    N = b.shape[1]
    offs = [sum(p.shape[1] for p in parts[:i]) for i in range(len(parts))]
    assert all(o % p.shape[1] == 0 for o, p in zip(offs, parts))

    def body(*refs):
        n = len(parts)
        acc = _dot(refs[0][...], refs[n][...])
        for i in range(1, n):
            acc = acc + _dot(refs[i][...], refs[n + i][...])
        refs[-1][...] = acc

    a_specs = [pl.BlockSpec((tm, p.shape[1]), lambda i, j: (i, 0)) for p in parts]
    b_specs = [pl.BlockSpec((p.shape[1], tn), lambda i, j, r=o // p.shape[1]: (r, j)) for o, p in zip(offs, parts)]
    return _call(
        body, name=name, grid=(T // tm, N // tn), in_specs=a_specs + b_specs,
        out_specs=pl.BlockSpec((tm, tn), lambda i, j: (i, j)), out_shape=jax.ShapeDtypeStruct((T, N), F32),
        sem=("parallel", "parallel"), rides=rides,
    )(*parts, *([b] * len(parts)))


def _to_bf16(v):
    return v.astype(BF16)


def _matmul_res(a, b, res, *, tm, tn, tk, prologue, name, rides=()):
    T, K = a.shape
    N = b.shape[1]

    def body(a_ref, b_ref, res_ref, o_ref):
        k = pl.program_id(2)
        p = _dot(prologue(a_ref[...]), b_ref[...])

        @pl.when(k == 0)
        def _():
            o_ref[...] = res_ref[...] + p

        @pl.when(k > 0)
        def _():
            o_ref[...] += p

    return _call(
        body, name=name, grid=(T // tm, N // tn, K // tk),
        in_specs=[pl.BlockSpec((tm, tk), lambda i, j, k: (i, k)), pl.BlockSpec((tk, tn), lambda i, j, k: (k, j)),
                  pl.BlockSpec((tm, tn), lambda i, j, k: (i, j))],
        out_specs=pl.BlockSpec((tm, tn), lambda i, j, k: (i, j)),
        out_shape=jax.ShapeDtypeStruct((T, N), F32),
        sem=("parallel", "parallel", "arbitrary"), rides=rides,
    )(a, b, res)


def _matmul_nt(a, b, *, tm, tn, tk, name, extra=None, epilogue=None, out_dtype=F32, rides=()):
    T, K = a.shape
    two = b.ndim == 3 and tk == 2 * b.shape[2]
    if two:
        N, ks = b.shape[1], b.shape[2]
        b_specs = [pl.BlockSpec((None, tn, ks), lambda i, j, k: (2 * k, j, 0)),
                   pl.BlockSpec((None, tn, ks), lambda i, j, k: (2 * k + 1, j, 0))]
    elif b.ndim == 3:
        per = b.shape[2] // tk
        N = b.shape[1]
        b_specs = [pl.BlockSpec((None, tn, tk), lambda i, j, k: (k // per, j, k % per))]
    else:
        N = b.shape[0]
        b_specs = [pl.BlockSpec((tn, tk), lambda i, j, k: (j, k))]
    nb = len(b_specs)
    nk = K // tk
    assert out_dtype == F32 or nk == 1
    in_specs = [pl.BlockSpec((tm, tk), lambda i, j, k: (i, k))] + b_specs
    args = [a] + [b] * nb
    if extra is not None:
        in_specs.append(pl.BlockSpec((tm, tn), lambda i, j, k: (i, j)))
        args.append(extra)

    def body(*refs):
        a_ref, b_ref = refs[0], refs[1]
        o_ref = refs[-1]
        if two:
            p = (_dot_nt(a_ref[:, :tk // 2].astype(BF16), refs[1][...])
                 + _dot_nt(a_ref[:, tk // 2:].astype(BF16), refs[2][...]))
        else:
            p = _dot_nt(a_ref[...].astype(BF16), b_ref[...])
        if nk == 1:
            if epilogue is not None:
                p = epilogue(p, refs[1 + nb][...])
            o_ref[...] = p.astype(out_dtype)
        else:
            k = pl.program_id(2)

            @pl.when(k == 0)
            def _():
                o_ref[...] = p

            @pl.when(k > 0)
            def _():
                o_ref[...] += p

    return _call(
        body, name=name, grid=(T // tm, N // tn, nk),
        in_specs=in_specs,
        out_specs=pl.BlockSpec((tm, tn), lambda i, j, k: (i, j)),
        out_shape=jax.ShapeDtypeStruct((T, N), out_dtype),
        sem=("parallel", "parallel", "arbitrary"), rides=rides,
    )(*args)


def _loss_bwd(h2, tgt, g, *, tm):
    T, D = h2.shape

    def body(h_ref, t_ref, g_ref, dh_ref, dhb_ref, dg_ref, loss_ref):
        @pl.when(pl.program_id(0) == 0)
        def _():
            dg_ref[...] = jnp.zeros_like(dg_ref)
            loss_ref[...] = jnp.zeros_like(loss_ref)
        h = h_ref[...]
        gg = g_ref[...]
        r = lax.rsqrt(jnp.mean(h * h, axis=-1, keepdims=True) + EPS)
        hn = h * r
        err = hn * gg - t_ref[...]
        loss_ref[...] += 0.5 * jnp.sum(jnp.mean(err * err, axis=-1, keepdims=True), axis=0, keepdims=True)
        dy = err * (1.0 / D)
        dg_ref[...] += jnp.sum(dy * hn, axis=0, keepdims=True)
        w = dy * gg
        dh = r * w - h * ((r * r * r) * jnp.mean(w * h, axis=-1, keepdims=True))
        dh_ref[...] = dh
        dhb_ref[...] = dh.astype(BF16)

    tile = pl.BlockSpec((tm, D), lambda i: (i, 0))
    return pl.pallas_call(
        body, name="loss_bwd", grid=(T // tm,),
        in_specs=[tile, tile, pl.BlockSpec((1, D), lambda i: (0, 0))],
        out_specs=[tile, tile, pl.BlockSpec((1, D), lambda i: (0, 0)), pl.BlockSpec((1, 1), lambda i: (0, 0))],
        out_shape=[jax.ShapeDtypeStruct((T, D), F32), jax.ShapeDtypeStruct((T, D), BF16),
                   jax.ShapeDtypeStruct((1, D), F32), jax.ShapeDtypeStruct((1, 1), F32)],
        compiler_params=_params(("arbitrary",)),
    )(h2, tgt, g)


def _rms_bwd_res(dn, h, g, dres, *, tm, name, bf16_copy=True, rides=()):
    T, D = h.shape

    def body(dn_ref, h_ref, g_ref, dres_ref, dh_ref, *rest):
        dg_ref = rest[-1]

        @pl.when(pl.program_id(0) == 0)
        def _():
            dg_ref[...] = jnp.zeros_like(dg_ref)
        h_ = h_ref[...]
        dn_ = dn_ref[...]
        dh, r = _rms_bwd(dn_, h_, g_ref[...])
        dg_ref[...] += jnp.sum(dn_ * (h_ * r), axis=0, keepdims=True)
        dh = dres_ref[...] + dh
        dh_ref[...] = dh
        if bf16_copy:
            rest[0][...] = dh.astype(BF16)

    tile = pl.BlockSpec((tm, D), lambda i: (i, 0))
    row = pl.BlockSpec((1, D), lambda i: (0, 0))
    copy_spec = [tile] if bf16_copy else []
    copy_shape = [jax.ShapeDtypeStruct((T, D), BF16)] if bf16_copy else []
    return _call(
        body, name=name, grid=(T // tm,),
        in_specs=[tile, tile, row, tile], out_specs=[tile] + copy_spec + [row],
        out_shape=[jax.ShapeDtypeStruct((T, D), F32)] + copy_shape + [jax.ShapeDtypeStruct((1, D), F32)],
        sem=("arbitrary",), rides=rides,
    )(dn, h, g, dres)


def _rel_distance():
    i = lax.broadcasted_iota(jnp.int32, (CHUNK, 2 * CHUNK), 0)
    j = lax.broadcasted_iota(jnp.int32, (CHUNK, 2 * CHUNK), 1)
    return i + CHUNK - j


def _bias_build(table):
    def body(tab_ref, o_ref):
        rel = _rel_distance()
        j = lax.broadcasted_iota(jnp.int32, (CHUNK, 2 * CHUNK), 1)
        band = (rel >= 0) & (rel < CHUNK)
        ge = [rel >= t for t in BUCKET_THR]
        for h in range(B_HEADS):
            cur = jnp.full((CHUNK, 2 * CHUNK), tab_ref[0, h], F32)
            for b in range(1, N_BUCKETS):
                cur = jnp.where(ge[b - 1], tab_ref[b, h], cur)
            o_ref[0, h] = jnp.where(band & (j >= CHUNK), cur, NEG)
            o_ref[1, h] = jnp.where(band, cur, NEG)

    return pl.pallas_call(
        body, name="bias_build",
        in_specs=[pl.BlockSpec(memory_space=pltpu.SMEM)],
        out_specs=pl.BlockSpec(memory_space=pltpu.VMEM),
        out_shape=jax.ShapeDtypeStruct((2, B_HEADS, CHUNK, 2 * CHUNK), F32),
    )(table)


def _bias_grad(dbias):
    def body(db_ref, o_ref, acc_ref):
        rel = _rel_distance()
        lo = [0] + BUCKET_THR
        hi = BUCKET_THR + [CHUNK]
        for b in range(N_BUCKETS):
            m = (rel >= lo[b]) & (rel < hi[b])
            for h in range(B_HEADS):
                row = b * B_HEADS + h
                acc_ref[row:row + 1, :] = jnp.sum(jnp.where(m, db_ref[h], 0.0), axis=0, keepdims=True)
        o_ref[...] = jnp.sum(acc_ref[...], axis=1, keepdims=True)

    return pl.pallas_call(
        body, name="bias_grad",
        in_specs=[pl.BlockSpec(memory_space=pltpu.VMEM)],
        out_specs=pl.BlockSpec(memory_space=pltpu.VMEM),
        out_shape=jax.ShapeDtypeStruct((N_BUCKETS * B_HEADS, 1), F32),
        scratch_shapes=[pltpu.VMEM((N_BUCKETS * B_HEADS, 2 * CHUNK), F32)],
    )(dbias)


def _causal_mask():
    t = lax.broadcasted_iota(jnp.int32, (CHUNK, CHUNK), 0)
    s = lax.broadcasted_iota(jnp.int32, (CHUNK, CHUNK), 1)
    return s <= t


def _gate_forward(u, v, lg, lb, wc, bs):
    ug = _gelu(u)
    vg = _gelu(v)
    mu = jnp.mean(vg, axis=-1, keepdims=True)
    xc = vg - mu
    rstd = lax.rsqrt(jnp.mean(xc * xc, axis=-1, keepdims=True) + EPS)
    xhat = xc * rstd
    vl = (xhat * lg + lb).astype(BF16)
    mixed = _dot(wc, vl) + bs
    return ug, xhat, rstd, vl, mixed


def _softmax_scores(qk, bias, sink):
    s = qk + bias
    m = jnp.maximum(jnp.max(s, axis=-1, keepdims=True), sink)
    p = jnp.exp(s - m)
    e_sink = jnp.exp(sink - m)
    inv = 1.0 / (jnp.sum(p, axis=-1, keepdims=True) + e_sink)
    return p * inv, e_sink * inv


PAIRS = Q_PER_KV // 2


def _head(g, pr, e):
    return g * Q_PER_KV + 2 * pr + e


def _stack_pairs(ref, g, col0=0):
    w = 2 * HEAD_DIM
    return jnp.concatenate([ref[:, col0 + (g * PAIRS + pr) * w:col0 + (g * PAIRS + pr + 1) * w] for pr in range(PAIRS)],
                           axis=0)


def _low_lanes():
    return lax.broadcasted_iota(jnp.int32, (2 * CHUNK, 2 * HEAD_DIM), 1) < HEAD_DIM


def _band_operands(kv_prev, kv_cur):
    band = jnp.concatenate([kv_prev, kv_cur], axis=0)
    low = _low_lanes()
    ops = []
    for cat in (band[:, :KV_WIDTH], band[:, KV_WIDTH:]):
        rol = pltpu.roll(cat, HEAD_DIM, 1)
        ops.append([[jnp.where(low if e == 0 else ~low, cat if g == e else rol, 0.0).astype(BF16) for e in range(2)]
                    for g in range(2)])
    return ops


def _mixer_fwd(proj, lg, lb, wsp, bs_col, sinks, bias, ga, gb, rides=()):
    T = proj.shape[0]
    nb = T // CHUNK

    def body(u_ref, v_ref, q_ref, kvc_ref, kvp_ref, lg_ref, lb_ref, w_ref, bs_ref, sink_ref, bias_ref,
             ga_ref, gb_ref, mixed_ref, mixed_t_ref, ab_ref):
        causal = _causal_mask()
        ssq = jnp.zeros((CHUNK, 1), F32)
        for g in range(A_GROUPS):
            cols = slice(g * CHUNK, (g + 1) * CHUNK)
            wc = jnp.where(causal, w_ref[g], 0.0).astype(BF16)
            ug, _, _, _, mixed = _gate_forward(u_ref[:, cols], v_ref[:, cols], lg_ref[g:g + 1, :], lb_ref[g:g + 1, :],
                                               wc, bs_ref[g])
            a = ug * mixed
            ab_ref[:, cols] = a
            ssq = ssq + jnp.sum(a * a, axis=-1, keepdims=True)
        ra = lax.rsqrt(ssq * (1.0 / A_WIDTH) + EPS)
        mixed_ref[:, :A_WIDTH] = ((ab_ref[:, :A_WIDTH] * ra) * ga_ref[...]).astype(BF16)

        kops, vops = _band_operands(kvp_ref[...], kvc_ref[...])
        ssq = jnp.zeros((CHUNK, 1), F32)
        for g in range(B_HEADS // Q_PER_KV):
            qst = (_stack_pairs(q_ref, g) * SCALE).astype(BF16)
            o_st = jnp.zeros((PAIRS * CHUNK, 2 * HEAD_DIM), F32)
            for e in range(2):
                s_all = _dot_nt(qst, kops[g][e])
                ps = []
                for pr in range(PAIRS):
                    h = _head(g, pr, e)
                    p, _ = _softmax_scores(s_all[pr * CHUNK:(pr + 1) * CHUNK], bias_ref[h], sink_ref[0, h])
                    ps.append(p.astype(BF16))
                o_st = o_st + _dot(jnp.concatenate(ps, axis=0), vops[g][e])
            for pr in range(PAIRS):
                o = o_st[pr * CHUNK:(pr + 1) * CHUNK]
                c0 = A_WIDTH + (g * PAIRS + pr) * 2 * HEAD_DIM
                ab_ref[:, c0:c0 + 2 * HEAD_DIM] = o
                ssq = ssq + jnp.sum(o * o, axis=-1, keepdims=True)
        rb = lax.rsqrt(ssq * (1.0 / B_WIDTH) + EPS)
        mixed_ref[:, A_WIDTH:] = ((ab_ref[:, A_WIDTH:] * rb) * gb_ref[...]).astype(BF16)
        mixed_t_ref[...] = mixed_ref[...].T

    full = lambda *shape: pl.BlockSpec(shape, lambda n: (0,) * len(shape))
    return _call(
        body, name="mixer_fwd", grid=(nb,),
        in_specs=[pl.BlockSpec((CHUNK, A_WIDTH), lambda n: (n, 0)),
                  pl.BlockSpec((CHUNK, A_WIDTH), lambda n: (n, 1)),
                  pl.BlockSpec((CHUNK, B_WIDTH), lambda n: (n, 2)),
                  pl.BlockSpec((CHUNK, 2 * KV_WIDTH), lambda n: (n, 12)),
                  pl.BlockSpec((CHUNK, 2 * KV_WIDTH), lambda n: (jnp.maximum(n - 1, 0), 12)),
                  full(A_GROUPS, CHUNK), full(A_GROUPS, CHUNK), full(A_GROUPS, CHUNK, CHUNK), full(A_GROUPS, CHUNK, 1),
                  pl.BlockSpec(memory_space=pltpu.SMEM),
                  pl.BlockSpec((None, B_HEADS, CHUNK, 2 * CHUNK), lambda n: (jnp.minimum(n, 1), 0, 0, 0)),
                  full(1, A_WIDTH), full(1, B_WIDTH)],
        out_specs=[pl.BlockSpec((CHUNK, D_MODEL), lambda n: (n, 0)), pl.BlockSpec((D_MODEL, CHUNK), lambda n: (0, n)),
                   pl.BlockSpec((CHUNK, D_MODEL), lambda n: (n, 0))],
        out_shape=[jax.ShapeDtypeStruct((T, D_MODEL), BF16), jax.ShapeDtypeStruct((D_MODEL, T), BF16),
                   jax.ShapeDtypeStruct((T, D_MODEL), F32)],
        sem=("parallel",), rides=rides,
    )(proj, proj, proj, proj, proj, lg, lb, wsp, bs_col, sinks, bias, ga, gb)


def _gmlp_bwd(proj, ab, dmixed, ga, lg, lb, wsp, bs_col, rides=()):
    T = proj.shape[0]
    nb = T // CHUNK

    def body(u_ref, v_ref, a_ref, dna_ref, ga_ref, lg_ref, lb_ref, w_ref, bs_ref,
             dp_ref, dpt_ref, dga_ref, dw_ref, dbs_ref, dlg_ref, dlb_ref):
        @pl.when(pl.program_id(0) == 0)
        def _():
            for r in (dga_ref, dw_ref, dbs_ref, dlg_ref, dlb_ref):
                r[...] = jnp.zeros_like(r)
        causal = _causal_mask()
        a_all = a_ref[...]
        dna = dna_ref[...]
        da_all, ra = _rms_bwd(dna, a_all, ga_ref[...])
        dga_ref[...] += jnp.sum(dna * (a_all * ra), axis=0, keepdims=True)
        for g in range(A_GROUPS):
            cols = slice(g * CHUNK, (g + 1) * CHUNK)
            wc = jnp.where(causal, w_ref[g], 0.0).astype(BF16)
            lgg = lg_ref[g:g + 1, :]
            u = u_ref[:, cols]
            v = v_ref[:, cols]
            ug, xhat, rstd, vl, mixed = _gate_forward(u, v, lgg, lb_ref[g:g + 1, :], wc, bs_ref[g])
            da = da_all[:, cols]
            dug = da * mixed
            dmg = da * ug
            dmg_b = dmg.astype(BF16)
            dbs_ref[g] += jnp.sum(dmg, axis=-1, keepdims=True)
            dw_ref[g] += jnp.where(causal, _dot_nt(dmg_b, vl), 0.0)
            dvl = _dot_tn(wc, dmg_b)
            dlg_ref[g:g + 1, :] += jnp.sum(dvl * xhat, axis=0, keepdims=True)
            dlb_ref[g:g + 1, :] += jnp.sum(dvl, axis=0, keepdims=True)
            dxh = dvl * lgg
            dvg = rstd * (dxh - jnp.mean(dxh, axis=-1, keepdims=True)
                          - xhat * jnp.mean(dxh * xhat, axis=-1, keepdims=True))
            _, gu = _gelu_and_grad(u)
            _, gv = _gelu_and_grad(v)
            dp_ref[:, cols] = (dug * gu).astype(BF16)
            dp_ref[:, A_WIDTH + g * CHUNK:A_WIDTH + (g + 1) * CHUNK] = (dvg * gv).astype(BF16)
        dpt_ref[...] = dp_ref[...].T

    full = lambda *shape: pl.BlockSpec(shape, lambda n: (0,) * len(shape))
    return _call(
        body, name="gmlp_bwd", grid=(nb,),
        in_specs=[pl.BlockSpec((CHUNK, A_WIDTH), lambda n: (n, 0)),
                  pl.BlockSpec((CHUNK, A_WIDTH), lambda n: (n, 1)),
                  pl.BlockSpec((CHUNK, A_WIDTH), lambda n: (n, 0)),
                  pl.BlockSpec((CHUNK, A_WIDTH), lambda n: (n, 0)),
                  full(1, A_WIDTH), full(A_GROUPS, CHUNK), full(A_GROUPS, CHUNK), full(A_GROUPS, CHUNK, CHUNK),
                  full(A_GROUPS, CHUNK, 1)],
        out_specs=[pl.BlockSpec((CHUNK, 2 * A_WIDTH), lambda n: (n, 0)), pl.BlockSpec((2 * A_WIDTH, CHUNK), lambda n: (0, n)),
                   full(1, A_WIDTH), full(A_GROUPS, CHUNK, CHUNK), full(A_GROUPS, CHUNK, 1),
                   full(A_GROUPS, CHUNK), full(A_GROUPS, CHUNK)],
        out_shape=[jax.ShapeDtypeStruct((T, 2 * A_WIDTH), BF16), jax.ShapeDtypeStruct((2 * A_WIDTH, T), BF16),
                   jax.ShapeDtypeStruct((1, A_WIDTH), F32), jax.ShapeDtypeStruct((A_GROUPS, CHUNK, CHUNK), F32),
                   jax.ShapeDtypeStruct((A_GROUPS, CHUNK, 1), F32), jax.ShapeDtypeStruct((A_GROUPS, CHUNK), F32),
                   jax.ShapeDtypeStruct((A_GROUPS, CHUNK), F32)],
        sem=("arbitrary",), rides=rides,
    )(proj, proj, ab, dmixed, ga, lg, lb, wsp, bs_col)


def _attn_bwd(proj, ab, dmixed, gb, sinks, bias, rides=()):
    T = proj.shape[0]
    nb = T // CHUNK
    qn = lambda n: jnp.minimum(n, nb - 1)

    def body(q_ref, kvc_ref, kvp_ref, o_ref, dnb_ref, gb_ref, sink_ref, bias_ref,
             dq_ref, dkv_ref, dqt_ref, dkvt_ref, dgb_ref, dsink_ref, dbias_ref, carry_ref, sacc_ref):
        n = pl.program_id(0)

        @pl.when(n == 0)
        def _():
            carry_ref[...] = jnp.zeros_like(carry_ref)
            sacc_ref[...] = jnp.zeros_like(sacc_ref)
            dgb_ref[...] = jnp.zeros_like(dgb_ref)
            dbias_ref[...] = jnp.zeros_like(dbias_ref)

        @pl.when(n < nb)
        def _():
            o_all = o_ref[...]
            dnb = dnb_ref[...]
            do_all, rb = _rms_bwd(dnb, o_all, gb_ref[...])
            dgb_ref[...] += jnp.sum(dnb * (o_all * rb), axis=0, keepdims=True)
            kops, vops = _band_operands(kvp_ref[...], kvc_ref[...])
            low = _low_lanes()
            halves = []
            for g in range(B_HEADS // Q_PER_KV):
                qst = (_stack_pairs(q_ref, g) * SCALE).astype(BF16)
                dost = _stack_pairs(do_all, g).astype(BF16)
                dq_st = jnp.zeros((PAIRS * CHUNK, 2 * HEAD_DIM), F32)
                dk_e, dv_e = [], []
                for e in range(2):
                    s_all = _dot_nt(qst, kops[g][e])
                    dp_all = _dot_nt(dost, vops[g][e])
                    ps, dsrs = [], []
                    for pr in range(PAIRS):
                        h = _head(g, pr, e)
                        rows = slice(pr * CHUNK, (pr + 1) * CHUNK)
                        p, p_sink = _softmax_scores(s_all[rows], bias_ref[h], sink_ref[0, h])
                        dp = dp_all[rows]
                        delta = jnp.sum(p * dp, axis=-1, keepdims=True)
                        ds = p * (dp - delta)
                        sacc_ref[:, h:h + 1] += -(p_sink * delta)
                        dbias_ref[h] += ds
                        ps.append(p.astype(BF16))
                        dsrs.append(ds.astype(BF16))
                    dsr_all = jnp.concatenate(dsrs, axis=0)
                    dq_st = dq_st + _dot(dsr_all, kops[g][e])
                    dk_e.append(_dot_tn(dsr_all, qst))
                    dv_e.append(_dot_tn(jnp.concatenate(ps, axis=0), dost))
                for pr in range(PAIRS):
                    c0 = (g * PAIRS + pr) * 2 * HEAD_DIM
                    dq_ref[:, c0:c0 + 2 * HEAD_DIM] = (dq_st[pr * CHUNK:(pr + 1) * CHUNK] * SCALE).astype(BF16)
                halves.append((dk_e, dv_e))
            tiles = []
            for t in range(2):
                g0, g1 = halves[0][t], halves[1][t]
                tiles.append(jnp.where(low, g0[0] + pltpu.roll(g0[1], HEAD_DIM, 1), pltpu.roll(g1[0], HEAD_DIM, 1) + g1[1]))
            dband = jnp.concatenate(tiles, axis=1)
            dkv = (carry_ref[...] + dband[:CHUNK]).astype(BF16)
            dkv_ref[...] = dkv
            dkvt_ref[...] = dkv.T
            dqt_ref[...] = dq_ref[...].T
            carry_ref[...] = dband[CHUNK:]

        @pl.when(n == nb)
        def _():
            dkv = carry_ref[...].astype(BF16)
            dkv_ref[...] = dkv
            dkvt_ref[...] = dkv.T
            dsink_ref[...] = jnp.sum(sacc_ref[...], axis=0, keepdims=True)

    full = lambda *shape: pl.BlockSpec(shape, lambda n: (0,) * len(shape))
    return _call(
        body, name="attn_bwd", grid=(nb + 1,),
        in_specs=[pl.BlockSpec((CHUNK, B_WIDTH), lambda n: (qn(n), 2)),
                  pl.BlockSpec((CHUNK, 2 * KV_WIDTH), lambda n: (qn(n), 12)),
                  pl.BlockSpec((CHUNK, 2 * KV_WIDTH), lambda n: (jnp.maximum(qn(n) - 1, 0), 12)),
                  pl.BlockSpec((CHUNK, B_WIDTH), lambda n: (qn(n), 1)),
                  pl.BlockSpec((CHUNK, B_WIDTH), lambda n: (qn(n), 1)),
                  full(1, B_WIDTH), pl.BlockSpec(memory_space=pltpu.SMEM),
                  pl.BlockSpec((None, B_HEADS, CHUNK, 2 * CHUNK), lambda n: (jnp.minimum(n, 1), 0, 0, 0))],
        out_specs=[pl.BlockSpec((CHUNK, B_WIDTH), lambda n: (qn(n), 0)),
                   pl.BlockSpec((CHUNK, 2 * KV_WIDTH), lambda n: (jnp.maximum(n - 1, 0), 0)),
                   pl.BlockSpec((B_WIDTH, CHUNK), lambda n: (0, qn(n))),
                   pl.BlockSpec((2 * KV_WIDTH, CHUNK), lambda n: (0, jnp.maximum(n - 1, 0))),
                   full(1, B_WIDTH), full(1, B_HEADS), full(B_HEADS, CHUNK, 2 * CHUNK)],
        out_shape=[jax.ShapeDtypeStruct((T, B_WIDTH), BF16), jax.ShapeDtypeStruct((T, 2 * KV_WIDTH), BF16),
                   jax.ShapeDtypeStruct((B_WIDTH, T), BF16), jax.ShapeDtypeStruct((2 * KV_WIDTH, T), BF16),
                   jax.ShapeDtypeStruct((1, B_WIDTH), F32), jax.ShapeDtypeStruct((1, B_HEADS), F32),
                   jax.ShapeDtypeStruct((B_HEADS, CHUNK, 2 * CHUNK), F32)],
        scratch_shapes=[pltpu.VMEM((CHUNK, 2 * KV_WIDTH), F32), pltpu.VMEM((CHUNK, B_HEADS), F32)],
        sem=("arbitrary",), rides=rides,
    )(proj, proj, proj, ab, dmixed, gb, sinks, bias)


def _sq_relu_grad(acc, r):
    return acc * (2.0 * r.astype(F32))


def _chip_index():
    return (2 * lax.axis_index("x") + lax.axis_index("y")).astype(jnp.int32).reshape(1)


def _cast_into_slot(w, *, tm, name):
    _, R, C = w.shape

    def body(me_ref, w_ref, o_ref):
        del me_ref
        o_ref[...] = w_ref[...].astype(BF16)

    return pl.pallas_call(
        body, name=name,
        grid_spec=pltpu.PrefetchScalarGridSpec(
            num_scalar_prefetch=1, grid=(R // tm,),
            in_specs=[pl.BlockSpec((None, tm, C), lambda i, me: (0, i, 0))],
            out_specs=pl.BlockSpec((None, tm, C), lambda i, me: (me[0], i, 0))),
        out_shape=jax.ShapeDtypeStruct((N_CHIPS, R, C), BF16), compiler_params=_params(("parallel",)),
    )(_chip_index(), w)


def _cast_into_slots_carrying(ws, *, steps, name, rides):
    n = len(ws)

    def body(*refs):
        for w_ref, o_ref in zip(refs[:n], refs[n:]):
            o_ref[...] = w_ref[...].astype(BF16)

    me = lambda: 2 * lax.axis_index("x") + lax.axis_index("y")
    return _call(
        body, name=name, grid=(steps,),
        in_specs=[pl.BlockSpec((None, w.shape[1] // steps, w.shape[2]), lambda i: (0, i, 0)) for w in ws],
        out_specs=[pl.BlockSpec((None, w.shape[1] // steps, w.shape[2]), lambda i: (me(), i, 0)) for w in ws],
        out_shape=[jax.ShapeDtypeStruct((N_CHIPS,) + w.shape[1:], BF16) for w in ws], sem=("arbitrary",), rides=rides,
    )(*ws)


def _owner_total(gh, others, *, tm, name):
    _, hr, C = gh.shape

    def body(me_ref, g_ref, o_ref_in, out_ref):
        del me_ref
        acc = g_ref[...]
        for j in range(3):
            acc = acc + o_ref_in[j].astype(F32)
        out_ref[...] = acc

    return pl.pallas_call(
        body, name=name,
        grid_spec=pltpu.PrefetchScalarGridSpec(
            num_scalar_prefetch=1, grid=(hr // tm,),
            in_specs=[pl.BlockSpec((None, tm, C), lambda i, me: (me[0], i, 0)),
                      pl.BlockSpec((3, tm, C), lambda i, me: (0, i, 0))],
            out_specs=pl.BlockSpec((tm, C), lambda i, me: (i, 0))),
        out_shape=jax.ShapeDtypeStruct((hr, C), F32),
        compiler_params=_params(("parallel",)),
    )(_chip_index(), gh, others)


def _adamw_math(w, g, m, v):
    m = ADAM_B1 * m + (1.0 - ADAM_B1) * g
    v = ADAM_B2 * v + (1.0 - ADAM_B2) * (g * g)
    m_hat = m / (1.0 - ADAM_B1 ** ADAM_STEP)
    v_hat = v / (1.0 - ADAM_B2 ** ADAM_STEP)
    delta = -ADAM_LR * (m_hat / (jnp.sqrt(v_hat) + ADAM_EPS) + ADAM_WD * w)
    return delta, m, v


def _adamw_halves(w, own, got, m, v, *, tm, name, rides=()):
    _, R, C = w.shape
    nt = (R // 2) // tm

    def body(w_ref, own_ref, got_ref, m_ref, v_ref, g_ref, d_ref, nm_ref, nv_ref):
        g = jnp.where(pl.program_id(0) == lax.axis_index("c"), own_ref[...], got_ref[...])
        g_ref[...] = g
        d_ref[...], nm_ref[...], nv_ref[...] = _adamw_math(w_ref[...], g, m_ref[...], v_ref[...])

    whole = pl.BlockSpec((None, tm, C), lambda h, i: (0, h * nt + i, 0))
    half = pl.BlockSpec((tm, C), lambda h, i: (i, 0))
    return _call(
        body, name=name, grid=(2, nt), in_specs=[whole, half, half, whole, whole], out_specs=[whole] * 4,
        out_shape=[jax.ShapeDtypeStruct((1, R, C), F32)] * 4, sem=("parallel", "parallel"), rides=rides,
    )(w, own, got, m, v)


def _adamw_small(w, slots, m, v, *, name, rides=()):
    def body(w_ref, slots_ref, m_ref, v_ref, g_ref, d_ref, nm_ref, nv_ref):
        g = slots_ref[0]
        for d in range(1, N_DEV):
            g = g + slots_ref[d]
        g_ref[...] = g
        d_ref[...], nm_ref[...], nv_ref[...] = _adamw_math(w_ref[...], g, m_ref[...], v_ref[...])

    flat = pl.BlockSpec(w.shape, lambda i: (0, 0))
    return _call(
        body, name=name, grid=(1,), in_specs=[flat, pl.BlockSpec(slots.shape, lambda i: (0, 0, 0)), flat, flat],
        out_specs=[flat] * 4, out_shape=[jax.ShapeDtypeStruct(w.shape, F32)] * 4, sem=("arbitrary",), rides=rides,
    )(w, slots, m, v)


SMALL = ["rel_bias_table", "mix_norm_g", "gate_norm_g", "gate_norm_b", "w_spatial", "b_spatial", "attn_sinks",
         "out_norm_a_g", "out_norm_b_g", "ffn_norm_g", "final_norm_g"]
SMALL_A = ["gate_norm_g", "gate_norm_b", "w_spatial", "b_spatial", "out_norm_a_g"]
SMALL_B = ["rel_bias_table", "mix_norm_g", "attn_sinks", "out_norm_b_g", "ffn_norm_g", "final_norm_g"]
LARGE = ["w_in", "w_out", "w_up", "w_down"]
ROW_TILE = {"w_in": 208, "w_out": 256, "w_up": 256, "w_down": 256}
WEIGHTS = ["rel_bias_table", "mix_norm_g", "w_in", "gate_norm_g", "gate_norm_b", "w_spatial", "b_spatial", "attn_sinks",
           "out_norm_a_g", "out_norm_b_g", "w_out", "ffn_norm_g", "w_up", "w_down", "final_norm_g"]
PACK_UNIT = 8 * 128


def _pack(parts):
    rows = []
    for p in parts:
        flat = p.reshape(-1)
        pad = (-flat.shape[0]) % PACK_UNIT
        rows.append(jnp.pad(flat, (0, pad)).reshape(-1, 128))
    return jnp.concatenate(rows, axis=0)


def _unpack(packed, like):
    out, row = [], 0
    for p in like:
        n = math.prod(p.shape)
        nrows = (n + PACK_UNIT - 1) // PACK_UNIT * 8
        out.append(packed[row:row + nrows].reshape(-1)[:n].reshape(p.shape))
        row += nrows
    return out


def kernel(x, rel_bias_table, mix_norm_g, w_in, gate_norm_g, gate_norm_b, w_spatial, b_spatial, attn_sinks, out_norm_a_g, out_norm_b_g, w_out, ffn_norm_g, w_up, w_down, final_norm_g, loss_target, m_rel_bias_table, m_mix_norm_g, m_w_in, m_gate_norm_g, m_gate_norm_b, m_w_spatial, m_b_spatial, m_attn_sinks, m_out_norm_a_g, m_out_norm_b_g, m_w_out, m_ffn_norm_g, m_w_up, m_w_down, m_final_norm_g, v_rel_bias_table, v_mix_norm_g, v_w_in, v_gate_norm_g, v_gate_norm_b, v_w_spatial, v_b_spatial, v_attn_sinks, v_out_norm_a_g, v_out_norm_b_g, v_w_out, v_ffn_norm_g, v_w_up, v_w_down, v_final_norm_g):
    args = dict(locals())
    wts = {n: args[n] for n in WEIGHTS}
    mom = {n: args["m_" + n] for n in WEIGHTS}
    var = {n: args["v_" + n] for n in WEIGHTS}
    sp = {n: wts[n] for n in SMALL}
    x2, tgt = x[0], loss_target[0]
    T = x2.shape[0]
    tm = min(512, T)
    tl = min(1024, T)
    lg = sp["gate_norm_g"].reshape(A_GROUPS, CHUNK)
    lb = sp["gate_norm_b"].reshape(A_GROUPS, CHUNK)
    wsp = sp["w_spatial"].reshape(A_GROUPS, CHUNK, CHUNK)
    bs_col = sp["b_spatial"].reshape(A_GROUPS, CHUNK, 1)
    sinks = sp["attn_sinks"].reshape(1, B_HEADS)
    ga = sp["out_norm_a_g"].reshape(1, A_WIDTH)
    gb = sp["out_norm_b_g"].reshape(1, B_WIDTH)
    g1 = sp["mix_norm_g"].reshape(1, D_MODEL)
    g2 = sp["ffn_norm_g"].reshape(1, D_MODEL)
    gf = sp["final_norm_g"].reshape(1, D_MODEL)

    def owner_total(n, gh, others):
        return _owner_total(gh, others, tm=ROW_TILE[n], name="rs_owner_total_" + n)

    def halves_view(at, shards):
        return at.reshape(shards, 2, at.shape[0] // shards // 2, at.shape[1])

    for d in (wts, mom, var):
        d["w_in"] = jnp.swapaxes(d["w_in"], 1, 2)

    s_in = _cast_into_slot(wts["w_in"], tm=ROW_TILE["w_in"], name="cast_w_in")
    (s_out, s_up, s_down), ((g_in,),) = _cast_into_slots_carrying(
        [wts["w_out"], wts["w_up"], wts["w_down"]], steps=8, name="cast_w_rest",
        rides=[_ride_gather(s_in, chain=(0, 1, 1), chain_fracs=(0.3, 0.6))])
    win_t = g_in.reshape(PROJ_WIDTH, D_MODEL)
    bias = _bias_build(sp["rel_bias_table"])
    (n1, proj), ((g_out,), (s_up,)) = _norm_matmul_wide(
        x2, g1, win_t, tm=tm, tn=PROJ_WIDTH // 2, name="in_proj",
        rides=[_ride_gather(s_out, chain=(0, 1, 1), chain_fracs=(0.65, 0.85)), _ride_gather(s_up, s1=(0, 3, 8))])
    wo = g_out.reshape(A_WIDTH + B_WIDTH, D_MODEL)
    (mixed, mixed_t, ab), ((s_up,), (s_down,), (n1_sib,)) = _mixer_fwd(
        proj, lg, lb, wsp, bs_col, sinks, bias, ga, gb,
        rides=[_ride_gather(s_up, s2=(0, 3, 8), s1=(3, 8, 8)), _ride_gather(s_down, s1=(0, 2, 8)),
               _ride_to_sibling(n1, first=True)])
    mixed_t = halves_view(mixed_t, N_CHIPS)
    h1, ((wu,), (s_down,), (mixed_t_sib,)) = _matmul_res(
        mixed, wo, x2, tm=tl, tn=1024, tk=D_MODEL, prologue=_to_bf16, name="out_proj",
        rides=[_ride_gather(s_up, s3=(0, 3, 8), tail=(3, 8, 8), mid_frac=0.75), _ride_gather(s_down, s2=(0, 2, 8)),
               _ride_to_sibling(mixed_t, halves=True)])
    (n2t, zp, z2, z2t), ((g_down,),) = _norm_matmul_sq(
        h1, g2, wu, tm=tl, tn=1024, name="up_proj", rides=[_ride_gather(s_down, s3=(0, 2, 8), chain=(2, 8, 8), chain_fracs=(0.5, 0.8))])
    wd = g_down.reshape(D_FF, D_MODEL)
    n2t, z2t = halves_view(n2t, 1), halves_view(z2t, N_CHIPS)
    h2, ((n2t_sib,), (z2t_sib,)) = _matmul_res(
        z2, wd, h1, tm=tl, tn=1024, tk=4096, prologue=_to_bf16, name="down_proj",
        rides=[_ride_to_sibling(n2t, halves=True), _ride_to_sibling(z2t, halves=True)])

    dh2, dh2b, dgf, loss = _loss_bwd(h2, tgt, gf, tm=tm)
    dzp, ((dh2b_sib,),) = _matmul_nt(dh2b, wd, tm=tl, tn=1024, tk=D_MODEL, name="bwd_dz", extra=zp,
                                     epilogue=_sq_relu_grad, out_dtype=BF16, rides=[_ride_to_sibling(dh2b)])
    (gd, gdb), ((dzp_sib,),) = _grad_pair(z2t, z2t_sib, dh2b, dh2b_sib, cols_sharded=False, tmo=1024, tk=tl,
                                          name="grad_w_down", rides=[_ride_to_sibling(dzp)])
    (gu, gub), ((o_d,),) = _grad_pair(n2t, n2t_sib, dzp, dzp_sib, cols_sharded=True, tmo=1024, tk=tl,
                                      name="grad_w_up", rides=[_ride_scatter(gdb, None, (0, 7, 8))])
    dn2, ((o_d,), (o_u,)) = _matmul_nt(dzp, wu, tm=tl, tn=1024, tk=4096, name="bwd_dn2",
                                       rides=[_ride_scatter(gdb, o_d, (7, 8, 8)), _ride_scatter(gub, None, (0, 6, 8))])
    h_d = owner_total("w_down", gd, o_d)
    (dh1, dh1b, dg2), ((o_u,),) = _rms_bwd_res(dn2, h1, g2, dh2, tm=tm, name="ffn_norm_bwd",
                                               rides=[_ride_scatter(gub, o_u, (6, 7, 8))])
    dmixed, ((o_u,), (dh1b_sib,), (w_d,)) = _matmul_nt(
        dh1b, wo, tm=tl, tn=1024, tk=D_MODEL, name="bwd_dmixed",
        rides=[_ride_scatter(gub, o_u, (7, 8, 8)), _ride_to_sibling(dh1b), _ride_swap(h_d)])
    h_u = owner_total("w_up", gu, o_u)
    (go, gob), ((w_u,),) = _grad_pair_merged(mixed_t, mixed_t_sib, dh1b, dh1b_sib, tk=tl, name="grad_w_out",
                                             rides=[_ride_swap(h_u)])
    (duv, duv_t, dga, dwsp, dbs, dlg, dlb), ((o_o,),) = _gmlp_bwd(proj, ab, dmixed, ga, lg, lb, wsp, bs_col,
                                                                  rides=[_ride_scatter(gob)])
    h_o = owner_total("w_out", go, o_o)
    small = {"gate_norm_g": dlg, "gate_norm_b": dlb, "w_spatial": dwsp, "b_spatial": dbs, "out_norm_a_g": dga}
    hr_in = PROJ_WIDTH // N_CHIPS // 2
    (dq, dkv, dq_t, dkv_t, dgb, dsinks, dbias), ((slots_a,), (dproj_t_sib,)) = _attn_bwd(
        proj, ab, dmixed, gb, sinks, bias,
        rides=[_ride_small_to_all(_pack([small[n] for n in SMALL_A])), _ride_rows_to_sibling(duv_t, hr_in, 2, N_CHIPS)])
    dtable = _bias_grad(dbias)
    dproj_t = halves_view(jnp.concatenate([duv_t, dq_t, dkv_t], axis=0), N_CHIPS)
    ((dproj_t_sib,),) = _carrier([_ride_to_sibling(dproj_t, halves=True, shards=(2, N_CHIPS), land=dproj_t_sib)],
                                 name="trade_dproj_t")
    (gi, gib_near), ((w_o,),) = _grad_pair(
        dproj_t, dproj_t_sib, n1, n1_sib, cols_sharded=False, tmo=hr_in, tk=tl, name="grad_w_in_near", shards="near",
        rides=[_ride_swap(h_o)])
    (gi, gib_far), ((o_i,),) = _grad_pair(
        dproj_t, dproj_t_sib, n1, n1_sib, cols_sharded=False, tmo=hr_in, tk=tl, name="grad_w_in_far", shards="far",
        into=gi, rides=[_ride_scatter(gib_near, None, to=(0, 1))])
    dn1, ((o_i,),) = _matmul_parts([duv, dq, dkv], win_t, tm=tl, tn=1024, name="bwd_dn1",
                                   rides=[_ride_scatter(gib_far, o_i, to=(2,))])
    h_i = owner_total("w_in", gi, o_i)
    dx, dg1 = _rms_bwd_res(dn1, x2, g1, dh1, tm=tm, name="mix_norm_bwd", bf16_copy=False)
    small.update({"rel_bias_table": dtable.reshape(N_BUCKETS, B_HEADS), "mix_norm_g": dg1, "attn_sinks": dsinks,
                  "out_norm_b_g": dgb, "ffn_norm_g": dg2, "final_norm_g": dgf})
    out_g, out_d, out_m, out_v = {}, {}, {}, {}

    def adamw_small(names, slots, tag, rides=()):
        extra = [jnp.zeros((1, 1), F32)] if tag == "b" else []
        like = [wts[n] for n in names] + extra
        res = _adamw_small(_pack(like), slots, _pack([mom[n] for n in names] + extra),
                           _pack([var[n] for n in names] + extra), name="adamw_small_" + tag, rides=rides)
        res, carried = res if rides else (res, None)
        for store, packed in zip((out_g, out_d, out_m, out_v), res):
            for n, val in zip(names + ["loss"], _unpack(packed, like)):
                store[n] = val
        return carried

    (w_i,), (slots_b,) = adamw_small(
        SMALL_A, slots_a, "a", rides=[_ride_swap(h_i), _ride_small_to_all(_pack([small[n] for n in SMALL_B] + [loss]))])
    for n, h, s in zip(LARGE, [h_i, h_o, h_u, h_d], [w_i, w_o, w_u, w_d]):
        res = _adamw_halves(wts[n], h, s, mom[n], var[n], tm=ROW_TILE[n], name="adamw_" + n)
        if n == "w_in":
            res = [jnp.swapaxes(r, 1, 2) for r in res]
        out_g[n], out_d[n], out_m[n], out_v[n] = res
    adamw_small(SMALL_B, slots_b, "b")

    total = out_g["loss"][0, 0]
    return (total, dx[None], *[out_g[n] for n in WEIGHTS], *[out_d[n] for n in WEIGHTS],
            *[out_m[n] for n in WEIGHTS], *[out_v[n] for n in WEIGHTS])
```

```python
import math

import numpy as np
import jax
import jax.numpy as jnp
from jax import lax
from jax.experimental import pallas as pl
from jax.experimental.pallas import tpu as pltpu

F32 = jnp.float32
BF16 = jnp.bfloat16

D_MODEL = 2048
CHUNK = 128
A_GROUPS = 8
A_WIDTH = 1024
HEAD_DIM = 64
B_HEADS = 16
Q_PER_KV = 8
B_WIDTH = 1024
KV_WIDTH = 128
PROJ_WIDTH = 3328
D_FF = 8192
N_BUCKETS = 32
EPS = 1e-5
NEG = -1e30
SCALE = HEAD_DIM ** -0.5
N_CHIPS = 4
N_DEV = 8

ADAM_LR = 0.001
ADAM_B1 = 0.9
ADAM_B2 = 0.999
ADAM_EPS = 1e-08
ADAM_WD = 0.01
ADAM_STEP = 10

VMEM_LIMIT = 60 * 1024 * 1024
MESH = pl.DeviceIdType.MESH


def _bucket_thresholds():
    d = np.arange(CHUNK)
    n_exact = N_BUCKETS // 2
    relf = np.maximum(d, n_exact).astype(np.float64)
    large = n_exact + (np.log(relf / n_exact) / math.log(CHUNK / n_exact) * (N_BUCKETS - n_exact)).astype(np.int32)
    bucket = np.where(d < n_exact, d, np.minimum(large, N_BUCKETS - 1))
    return [int(np.min(d[bucket >= b])) for b in range(1, N_BUCKETS)]


BUCKET_THR = _bucket_thresholds()


def _params(sem=None):
    return pltpu.CompilerParams(dimension_semantics=sem, vmem_limit_bytes=VMEM_LIMIT)


def _gelu(x):
    c = math.sqrt(2.0 / math.pi)
    return 0.5 * x * (1.0 + jnp.tanh(c * (x + 0.044715 * (x * x * x))))


def _gelu_and_grad(x):
    c = math.sqrt(2.0 / math.pi)
    x2 = x * x
    t = jnp.tanh(c * (x + 0.044715 * (x2 * x)))
    g = 0.5 * x * (1.0 + t)
    dg = 0.5 * (1.0 + t) + 0.5 * x * (1.0 - t * t) * (c * (1.0 + 3.0 * 0.044715 * x2))
    return g, dg


def _dot(a, b):
    return jnp.dot(a, b, preferred_element_type=F32)


def _dot_nt(a, b):
    return lax.dot_general(a, b, (((1,), (1,)), ((), ())), preferred_element_type=F32)


def _dot_tn(a, b):
    return lax.dot_general(a, b, (((0,), (0,)), ((), ())), preferred_element_type=F32)


def _rms_bwd(dn, h, g):
    r = lax.rsqrt(jnp.mean(h * h, axis=-1, keepdims=True) + EPS)
    w = dn * g
    dh = r * w - h * ((r * r * r) * jnp.mean(w * h, axis=-1, keepdims=True))
    return dh, r


def _place():
    x, y, c = lax.axis_index("x"), lax.axis_index("y"), lax.axis_index("c")
    chips = [(1 - x, y), (x, 1 - y), (1 - x, 1 - y)]
    return x, y, c, chips


def _remote(src, dst, send_sem, recv_sem, to):
    return pltpu.make_async_remote_copy(src_ref=src, dst_ref=dst, send_sem=send_sem, recv_sem=recv_sem,
                                        device_id=to, device_id_type=MESH)


class _Ride:
    def __init__(self, args, out_shape, n_sem, start, finish, mids=(), aliases=None):
        self.args, self.out_shape, self.n_sem = list(args), list(out_shape), n_sem
        self.start, self.mids, self.finish = start, list(mids), finish
        self.aliases = dict(aliases or {})


def _call(body, *, name, grid, in_specs, out_specs, out_shape, scratch_shapes=(), sem=None, rides=(), aliases=None):
    single = not isinstance(out_shape, (list, tuple))
    out_specs = [out_specs] if single else list(out_specs)
    out_shape = [out_shape] if single else list(out_shape)
    n_in, n_out, n_scr = len(in_specs), len(out_shape), len(scratch_shapes)
    r_in = [len(r.args) for r in rides]
    r_out = [len(r.out_shape) for r in rides]
    any_spec = pl.BlockSpec(memory_space=pl.ANY)
    aliases, off_i, off_o = dict(aliases or {}), n_in, n_out
    for r in rides:
        for i, o in r.aliases.items():
            aliases[off_i + i] = off_o + o
        off_i += len(r.args)
        off_o += len(r.out_shape)
    steps = math.prod(grid)

    def wrapped(*refs):
        p = 0
        ins = refs[p:p + n_in]; p += n_in
        rins = refs[p:p + sum(r_in)]; p += sum(r_in)
        outs = refs[p:p + n_out]; p += n_out
        routs = refs[p:p + sum(r_out)]; p += sum(r_out)
        scr = refs[p:p + n_scr]; p += n_scr
        sems = refs[p:]
        parts, pi, po = [], 0, 0
        for k, r in enumerate(rides):
            parts.append((rins[pi:pi + r_in[k]], routs[po:po + r_out[k]], sems[2 * k], sems[2 * k + 1]))
            pi += r_in[k]
            po += r_out[k]
        lin = 0
        for d in range(len(grid)):
            lin = lin * grid[d] + pl.program_id(d)
        if rides:
            @pl.when(lin == 0)
            def _():
                for r, part in zip(rides, parts):
                    r.start(*part)
        body(*ins, *outs, *scr)
        for r, part in zip(rides, parts):
            for frac, fn in r.mids:
                @pl.when(lin == min(steps - 1, int(frac * steps)))
                def _(fn=fn, part=part):
                    fn(*part)
        if rides:
            @pl.when(lin == steps - 1)
            def _():
                for r, part in zip(rides, parts):
                    r.finish(*part)

    scratch = list(scratch_shapes)
    for r in rides:
        scratch += [pltpu.SemaphoreType.DMA((r.n_sem,)), pltpu.SemaphoreType.DMA((r.n_sem,))]
    if rides:
        sem = ("arbitrary",) * len(grid)
    res = pl.pallas_call(
        wrapped, name=name, grid=grid,
        in_specs=list(in_specs) + [any_spec] * sum(r_in),
        out_specs=out_specs + [any_spec] * sum(r_out),
        out_shape=out_shape + [s for r in rides for s in r.out_shape],
        scratch_shapes=scratch, input_output_aliases=aliases,
        compiler_params=_params(sem),
    )

    def run(*args):
        got = res(*args, *[a for r in rides for a in r.args])
        mine = got[0] if single else list(got[:n_out])
        if not rides:
            return mine
        rest, out = list(got[n_out:]), []
        for k in range(len(rides)):
            out.append(rest[:r_out[k]])
            rest = rest[r_out[k]:]
        return mine, out

    return run


def _ride_gather(slot, s1=None, s2=None, s3=None, tail=None, chain=None, mid_frac=0.6, chain_fracs=(0.35, 0.7)):
    half = slot.shape[1] // 2

    def rows(part, c, which=None):
        k0, k1, n = part
        count, first = (k1 - k0) * (half // n), c * half + k0 * (half // n)
        return pl.ds(first, count) if which is None else pl.ds(first + which * (count // 2), count // 2)

    def ids():
        x, y, c, _ = _place()
        return x, y, c, 2 * x + y, 2 * (1 - x) + y, 2 * x + (1 - y), 2 * (1 - x) + (1 - y)

    def copy(full, chip, r, ss, rs, k, to):
        piece = full.at[chip, r, :]
        return _remote(piece, piece, ss.at[k], rs.at[k], to)

    def to_neighbours(full, ss, rs, part, base):
        x, y, c, me, _, _, _ = ids()
        return [copy(full, me, rows(part, c), ss, rs, base, (1 - x, y, c)),
                copy(full, me, rows(part, c), ss, rs, base + 1, (x, 1 - y, c))]

    def from_neighbours(full, ss, rs, part, base):
        x, y, c, _, cx, cy, _ = ids()
        return [copy(full, cx, rows(part, c), ss, rs, base, (x, y, c)), copy(full, cy, rows(part, c), ss, rs, base + 1, (x, y, c))]

    def onward(full, ss, rs, part, base):
        x, y, c, _, cx, cy, _ = ids()
        return [copy(full, cx, rows(part, c, 0), ss, rs, base, (x, 1 - y, c)),
                copy(full, cy, rows(part, c, 1), ss, rs, base + 1, (1 - x, y, c))]

    def from_onward(full, ss, rs, part, base):
        x, y, c, _, _, _, cd = ids()
        return [copy(full, cd, rows(part, c, 0), ss, rs, base, (x, y, c)), copy(full, cd, rows(part, c, 1), ss, rs, base + 1, (x, y, c))]

    def to_sibling(full, ss, rs, part, base, diagonal):
        x, y, c, _, cx, cy, cd = ids()
        return [copy(full, chip, rows(part, c), ss, rs, base + j, (x, y, 1 - c))
                for j, chip in enumerate([cd] if diagonal else [cx, cy])]

    def from_sibling(full, ss, rs, part, base, diagonal):
        x, y, c, _, cx, cy, cd = ids()
        return [copy(full, chip, rows(part, 1 - c), ss, rs, base + j, (x, y, c))
                for j, chip in enumerate([cd] if diagonal else [cx, cy])]

    def start(ins, outs, ss, rs):
        full, cps = outs[0], []
        for part, base in ((s1, 0), (chain, 12)):
            if part is not None:
                cps += to_neighbours(full, ss, rs, part, base)
        for part, b_ici, b_sib in ((s2, 2, 4), (tail, 7, 9)):
            if part is not None:
                cps += onward(full, ss, rs, part, b_ici) + to_sibling(full, ss, rs, part, b_sib, False)
        if s3 is not None:
            cps += to_sibling(full, ss, rs, s3, 6, True)
        for cp in cps:
            cp.start()

    def second(part, b_in, b_ici, b_sib):
        def fn(ins, outs, ss, rs):
            for cp in from_neighbours(outs[0], ss, rs, part, b_in):
                cp.wait_recv()
            for cp in onward(outs[0], ss, rs, part, b_ici) + to_sibling(outs[0], ss, rs, part, b_sib, False):
                cp.start()
        return fn

    def third(part, b_ici, b_sib):
        def fn(ins, outs, ss, rs):
            for cp in from_onward(outs[0], ss, rs, part, b_ici):
                cp.wait_recv()
            for cp in to_sibling(outs[0], ss, rs, part, b_sib, True):
                cp.start()
        return fn

    mids = []
    if tail is not None:
        mids.append((mid_frac, third(tail, 7, 11)))
    if chain is not None:
        mids += [(chain_fracs[0], second(chain, 12, 14, 16)), (chain_fracs[1], third(chain, 14, 18))]

    def finish(ins, outs, ss, rs):
        full, got, sent = outs[0], [], []
        if s1 is not None:
            got += from_neighbours(full, ss, rs, s1, 0)
            sent += to_neighbours(full, ss, rs, s1, 0)
        if s2 is not None:
            got += from_onward(full, ss, rs, s2, 2) + from_sibling(full, ss, rs, s2, 4, False)
            sent += onward(full, ss, rs, s2, 2) + to_sibling(full, ss, rs, s2, 4, False)
        if s3 is not None:
            got += from_sibling(full, ss, rs, s3, 6, True)
            sent += to_sibling(full, ss, rs, s3, 6, True)
        if tail is not None:
            got += from_sibling(full, ss, rs, tail, 9, False) + from_sibling(full, ss, rs, tail, 11, True)
            sent += onward(full, ss, rs, tail, 7) + to_sibling(full, ss, rs, tail, 9, False) + to_sibling(full, ss, rs, tail, 11, True)
        if chain is not None:
            got += from_sibling(full, ss, rs, chain, 16, False) + from_sibling(full, ss, rs, chain, 18, True)
            sent += (to_neighbours(full, ss, rs, chain, 12) + onward(full, ss, rs, chain, 14)
                     + to_sibling(full, ss, rs, chain, 16, False) + to_sibling(full, ss, rs, chain, 18, True))
        for cp in got:
            cp.wait_recv()
        for cp in sent:
            cp.wait_send()

    return _Ride([slot], [jax.ShapeDtypeStruct(slot.shape, slot.dtype)], 19, start, finish, mids=mids, aliases={0: 0})


def _ride_scatter(q, land=None, part=(0, 1), to=(0, 1, 2)):
    k0, k1, n = part if len(part) == 3 else (part[0], part[0] + 1, part[1])
    rows_n = q.shape[1] // n
    rows = pl.ds(k0 * rows_n, (k1 - k0) * rows_n)

    def copies(ins, outs, ss, rs):
        x, y, c, chips = _place()
        return [_remote(ins[0].at[2 * chip[0] + chip[1], rows, :], outs[0].at[j, rows, :], ss.at[j], rs.at[j], (*chip, c))
                for j, chip in enumerate(chips) if j in to]

    def start(*a):
        for cp in copies(*a):
            cp.start()

    def finish(*a):
        for cp in copies(*a):
            cp.wait()

    shape = jax.ShapeDtypeStruct((3,) + q.shape[1:], q.dtype)
    if land is None:
        return _Ride([q], [shape], 3, start, finish)
    return _Ride([q, land], [shape], 3, start, finish, aliases={1: 0})


def _ride_to_sibling(a, halves=False, first=False, shards=None, land=None):
    s0, s1 = shards or (0, a.shape[0])

    def copy(ins, outs, ss, rs):
        x, y, c, _ = _place()
        if halves:
            src, dst = ins[0].at[s0:s1, 1 - c], outs[0].at[s0:s1]
        else:
            src, dst = (ins[0].at[0] if first else ins[0]), outs[0]
        return _remote(src, dst, ss.at[0], rs.at[0], (x, y, 1 - c))

    shape = (a.shape[0],) + a.shape[2:] if halves else (a.shape[1:] if first else a.shape)
    return _Ride([a] if land is None else [a, land], [jax.ShapeDtypeStruct(shape, a.dtype)], 1,
                 lambda *a_: copy(*a_).start(), lambda *a_: copy(*a_).wait(), aliases=None if land is None else {1: 0})


def _ride_rows_to_sibling(a, hr, shards, total):
    def copies(ins, outs, ss, rs):
        x, y, c, _ = _place()
        return [_remote(ins[0].at[pl.ds((2 * s + 1 - c) * hr, hr), :], outs[0].at[s], ss.at[s], rs.at[s], (x, y, 1 - c))
                for s in range(shards)]

    def start(*a_):
        for cp in copies(*a_):
            cp.start()

    def finish(*a_):
        for cp in copies(*a_):
            cp.wait()

    return _Ride([a], [jax.ShapeDtypeStruct((total, hr, a.shape[1]), a.dtype)], shards, start, finish)


def _ride_swap(h):
    def copy(ins, outs, ss, rs):
        x, y, c, _ = _place()
        return _remote(ins[0], outs[0], ss.at[0], rs.at[0], (x, y, 1 - c))

    return _Ride([h], [jax.ShapeDtypeStruct(h.shape, h.dtype)], 1,
                 lambda *a: copy(*a).start(), lambda *a: copy(*a).wait())


def _mesh_place(p):
    return (p // 4, (p // 2) % 2, p % 2)


def _ride_small_to_all(packed):
    def copies(ins, outs, ss, rs):
        x, y, c, _ = _place()
        me = 4 * x + 2 * y + c
        return [_remote(ins[0], outs[0].at[me], ss.at[k - 1], rs.at[k - 1], _mesh_place((me + k) % N_DEV))
                for k in range(1, N_DEV)]

    def own(ins, outs, ss, rs):
        x, y, c, _ = _place()
        return pltpu.make_async_copy(ins[0], outs[0].at[4 * x + 2 * y + c], ss.at[N_DEV - 1])

    def start(*a):
        own(*a).start()
        for cp in copies(*a):
            cp.start()

    def finish(ins, outs, ss, rs):
        x, y, c, _ = _place()
        me = 4 * x + 2 * y + c
        for k in range(1, N_DEV):
            _remote(ins[0], outs[0].at[(me + N_DEV - k) % N_DEV], ss.at[k - 1], rs.at[k - 1], (x, y, c)).wait_recv()
        for cp in copies(ins, outs, ss, rs):
            cp.wait_send()
        own(ins, outs, ss, rs).wait()

    return _Ride([packed], [jax.ShapeDtypeStruct((N_DEV,) + packed.shape, packed.dtype)], N_DEV, start, finish)


def _norm_bf16(a_ref, g_ref):
    xf = a_ref[...]
    r = lax.rsqrt(jnp.mean(xf * xf, axis=-1, keepdims=True) + EPS)
    return ((xf * r) * g_ref[...]).astype(BF16)


def _norm_matmul_wide(a, g, b, *, tm, tn, name, rides=()):
    T, K = a.shape
    N = b.shape[0]

    def body(a_ref, g_ref, b_ref, n_ref, o_ref):
        n = _norm_bf16(a_ref, g_ref)
        n_ref[...] = n
        o_ref[...] = _dot_nt(n, b_ref[...])

    return _call(
        body, name=name, grid=(N // tn, T // tm),
        in_specs=[pl.BlockSpec((tm, K), lambda j, i: (i, 0)), pl.BlockSpec((1, K), lambda j, i: (0, 0)),
                  pl.BlockSpec((tn, K), lambda j, i: (j, 0))],
        out_specs=[pl.BlockSpec((None, tm, K), lambda j, i: (j, i, 0)), pl.BlockSpec((tm, tn), lambda j, i: (i, j))],
        out_shape=[jax.ShapeDtypeStruct((N // tn, T, K), BF16), jax.ShapeDtypeStruct((T, N), F32)],
        sem=("arbitrary", "arbitrary"), rides=rides,
    )(a, g, b)


def _norm_matmul_sq(a, g, b, *, tm, tn, name, rides=()):
    T, K = a.shape
    per = b.shape[2] // tn
    N = b.shape[0] * b.shape[2]

    def body(a_ref, g_ref, b_ref, nt_ref, o_ref, z_ref, zt_ref, n_scr):
        @pl.when(pl.program_id(1) == 0)
        def _():
            n = _norm_bf16(a_ref, g_ref)
            n_scr[...] = n
            nt_ref[...] = n.T
        r = jnp.maximum(_dot(n_scr[...], b_ref[...]), 0.0)
        o_ref[...] = r.astype(BF16)
        z = (r * r).astype(BF16)
        z_ref[...] = z
        zt_ref[...] = z.T

    return _call(
        body, name=name, grid=(T // tm, N // tn),
        in_specs=[pl.BlockSpec((tm, K), lambda i, j: (i, 0)), pl.BlockSpec((1, K), lambda i, j: (0, 0)),
                  pl.BlockSpec((None, K, tn), lambda i, j: (j // per, 0, j % per))],
        out_specs=[pl.BlockSpec((K, tm), lambda i, j: (0, i)), pl.BlockSpec((tm, tn), lambda i, j: (i, j)),
                   pl.BlockSpec((tm, tn), lambda i, j: (i, j)), pl.BlockSpec((tn, tm), lambda i, j: (j, i))],
        out_shape=[jax.ShapeDtypeStruct((K, T), BF16), jax.ShapeDtypeStruct((T, N), BF16),
                   jax.ShapeDtypeStruct((T, N), BF16), jax.ShapeDtypeStruct((N, T), BF16)],
        scratch_shapes=[pltpu.VMEM((tm, K), BF16)],
        sem=("parallel", "arbitrary"), rides=rides,
    )(a, g, b)


def _grad_pair(at, at_sib, b, b_sib, *, cols_sharded, tmo, tk, name, shards=None, into=None, rides=()):
    S, _, hr, T = at.shape
    C = b.shape[-1] // N_CHIPS if cols_sharded else b.shape[-1]
    nk = T // tk

    def shard(s):
        if shards is None:
            return s
        x, y = lax.axis_index("x"), lax.axis_index("y")
        first, second = ((2 * (1 - x) + y, 2 * x + (1 - y)) if shards == "near" else (2 * (1 - x) + (1 - y), 2 * x + y))
        return jnp.where(s == 0, first, second)

    a_sel = (lambda s: 0) if cols_sharded else shard
    b_sel = shard if cols_sharded else (lambda s: 0)
    if b.ndim == 3:
        b_spec = pl.BlockSpec((None, tk, C), lambda s, i, k: (0, k, b_sel(s)))
    else:
        b_spec = pl.BlockSpec((tk, C), lambda s, i, k: (k, b_sel(s)))
    n_into = 0 if into is None else 1

    def body(a_ref, as_ref, b_ref, bs_ref, *rest):
        o_ref, ob_ref = rest[n_into:]
        k = pl.program_id(2)
        p = _dot(a_ref[...], b_ref[...]) + _dot(as_ref[...], bs_ref[...])

        @pl.when(k == 0)
        def _():
            o_ref[...] = p

        @pl.when(k > 0)
        def _():
            o_ref[...] += p

        @pl.when(k == nk - 1)
        def _():
            ob_ref[...] = o_ref[...].astype(BF16)

    out = pl.BlockSpec((None, tmo, C), lambda s, i, k: (shard(s), i, 0))
    held = [pl.BlockSpec(memory_space=pl.ANY)] * n_into
    return _call(
        body, name=name, grid=(N_CHIPS if shards is None else 2, hr // tmo, nk),
        in_specs=[pl.BlockSpec((None, None, tmo, tk), lambda s, i, k: (a_sel(s), lax.axis_index("c"), i, k)),
                  pl.BlockSpec((None, tmo, tk), lambda s, i, k: (a_sel(s), i, k)),
                  b_spec, pl.BlockSpec((tk, C), lambda s, i, k: (k, b_sel(s)))] + held,
        out_specs=[out, out],
        out_shape=[jax.ShapeDtypeStruct((N_CHIPS, hr, C), F32), jax.ShapeDtypeStruct((N_CHIPS, hr, C), BF16)],
        sem=("parallel", "parallel", "arbitrary"), rides=rides, aliases={4: 0} if into is not None else None,
    )(at, at_sib, b, b_sib, *([into] if into is not None else []))


def _grad_pair_merged(at, at_sib, b, b_sib, *, tk, name, rides=()):
    S, _, hr, T = at.shape
    C = b.shape[-1]
    nk = T // tk

    def body(a_ref, as_ref, b_ref, bs_ref, o_ref, ob_ref):
        k = pl.program_id(0)
        p = (_dot(a_ref[...].reshape(S * hr, tk), b_ref[...])
             + _dot(as_ref[...].reshape(S * hr, tk), bs_ref[...])).reshape(S, hr, C)

        @pl.when(k == 0)
        def _():
            o_ref[...] = p

        @pl.when(k > 0)
        def _():
            o_ref[...] += p

        @pl.when(k == nk - 1)
        def _():
            ob_ref[...] = o_ref[...].astype(BF16)

    out = pl.BlockSpec((S, hr, C), lambda k: (0, 0, 0))
    return _call(
        body, name=name, grid=(nk,),
        in_specs=[pl.BlockSpec((S, None, hr, tk), lambda k: (0, lax.axis_index("c"), 0, k)),
                  pl.BlockSpec((S, hr, tk), lambda k: (0, 0, k)),
                  pl.BlockSpec((tk, C), lambda k: (k, 0)), pl.BlockSpec((tk, C), lambda k: (k, 0))],
        out_specs=[out, out],
        out_shape=[jax.ShapeDtypeStruct((S, hr, C), F32), jax.ShapeDtypeStruct((S, hr, C), BF16)],
        sem=("arbitrary",), rides=rides,
    )(at, at_sib, b, b_sib)


def _matmul_parts(parts, b, *, tm, tn, name, rides=()):
    T = parts[0].shape[0]
    N = b.shape[1]
    offs = [sum(p.shape[1] for p in parts[:i]) for i in range(len(parts))]
    assert all(o % p.shape[1] == 0 for o, p in zip(offs, parts))

    def body(*refs):
        n = len(parts)
        acc = _dot(refs[0][...], refs[n][...])
        for i in range(1, n):
            acc = acc + _dot(refs[i][...], refs[n + i][...])
        refs[-1][...] = acc

    a_specs = [pl.BlockSpec((tm, p.shape[1]), lambda i, j: (i, 0)) for p in parts]
    b_specs = [pl.BlockSpec((p.shape[1], tn), lambda i, j, r=o // p.shape[1]: (r, j)) for o, p in zip(offs, parts)]
    return _call(
        body, name=name, grid=(T // tm, N // tn), in_specs=a_specs + b_specs,
        out_specs=pl.BlockSpec((tm, tn), lambda i, j: (i, j)), out_shape=jax.ShapeDtypeStruct((T, N), F32),
        sem=("parallel", "parallel"), rides=rides,
    )(*parts, *([b] * len(parts)))


def _to_bf16(v):
    return v.astype(BF16)


def _matmul_res(a, b, res, *, tm, tn, tk, prologue, name, rides=()):
    T, K = a.shape
    N = b.shape[1]

    def body(a_ref, b_ref, res_ref, o_ref):
        k = pl.program_id(2)
        p = _dot(prologue(a_ref[...]), b_ref[...])

        @pl.when(k == 0)
        def _():
            o_ref[...] = res_ref[...] + p

        @pl.when(k > 0)
        def _():
            o_ref[...] += p

    return _call(
        body, name=name, grid=(T // tm, N // tn, K // tk),
        in_specs=[pl.BlockSpec((tm, tk), lambda i, j, k: (i, k)), pl.BlockSpec((tk, tn), lambda i, j, k: (k, j)),
                  pl.BlockSpec((tm, tn), lambda i, j, k: (i, j))],
        out_specs=pl.BlockSpec((tm, tn), lambda i, j, k: (i, j)),
        out_shape=jax.ShapeDtypeStruct((T, N), F32),
        sem=("parallel", "parallel", "arbitrary"), rides=rides,
    )(a, b, res)


def _matmul_nt(a, b, *, tm, tn, tk, name, extra=None, epilogue=None, out_dtype=F32, rides=()):
    T, K = a.shape
    two = b.ndim == 3 and tk == 2 * b.shape[2]
    if two:
        N, ks = b.shape[1], b.shape[2]
        b_specs = [pl.BlockSpec((None, tn, ks), lambda i, j, k: (2 * k, j, 0)),
                   pl.BlockSpec((None, tn, ks), lambda i, j, k: (2 * k + 1, j, 0))]
    elif b.ndim == 3:
        per = b.shape[2] // tk
        N = b.shape[1]
        b_specs = [pl.BlockSpec((None, tn, tk), lambda i, j, k: (k // per, j, k % per))]
    else:
        N = b.shape[0]
        b_specs = [pl.BlockSpec((tn, tk), lambda i, j, k: (j, k))]
    nb = len(b_specs)
    nk = K // tk
    assert out_dtype == F32 or nk == 1
    in_specs = [pl.BlockSpec((tm, tk), lambda i, j, k: (i, k))] + b_specs
    args = [a] + [b] * nb
    if extra is not None:
        in_specs.append(pl.BlockSpec((tm, tn), lambda i, j, k: (i, j)))
        args.append(extra)

    def body(*refs):
        a_ref, b_ref = refs[0], refs[1]
        o_ref = refs[-1]
        if two:
            p = (_dot_nt(a_ref[:, :tk // 2].astype(BF16), refs[1][...])
                 + _dot_nt(a_ref[:, tk // 2:].astype(BF16), refs[2][...]))
        else:
            p = _dot_nt(a_ref[...].astype(BF16), b_ref[...])
        if nk == 1:
            if epilogue is not None:
                p = epilogue(p, refs[1 + nb][...])
            o_ref[...] = p.astype(out_dtype)
        else:
            k = pl.program_id(2)

            @pl.when(k == 0)
            def _():
                o_ref[...] = p

            @pl.when(k > 0)
            def _():
                o_ref[...] += p

    return _call(
        body, name=name, grid=(T // tm, N // tn, nk),
        in_specs=in_specs,
        out_specs=pl.BlockSpec((tm, tn), lambda i, j, k: (i, j)),
        out_shape=jax.ShapeDtypeStruct((T, N), out_dtype),
        sem=("parallel", "parallel", "arbitrary"), rides=rides,
    )(*args)


def _loss_bwd(h2, tgt, g, *, tm):
    T, D = h2.shape

    def body(h_ref, t_ref, g_ref, dh_ref, dhb_ref, dg_ref, loss_ref):
        @pl.when(pl.program_id(0) == 0)
        def _():
            dg_ref[...] = jnp.zeros_like(dg_ref)
            loss_ref[...] = jnp.zeros_like(loss_ref)
        h = h_ref[...]
        gg = g_ref[...]
        r = lax.rsqrt(jnp.mean(h * h, axis=-1, keepdims=True) + EPS)
        hn = h * r
        err = hn * gg - t_ref[...]
        loss_ref[...] += 0.5 * jnp.sum(jnp.mean(err * err, axis=-1, keepdims=True), axis=0, keepdims=True)
        dy = err * (1.0 / D)
        dg_ref[...] += jnp.sum(dy * hn, axis=0, keepdims=True)
        w = dy * gg
        dh = r * w - h * ((r * r * r) * jnp.mean(w * h, axis=-1, keepdims=True))
        dh_ref[...] = dh
        dhb_ref[...] = dh.astype(BF16)

    tile = pl.BlockSpec((tm, D), lambda i: (i, 0))
    return pl.pallas_call(
        body, name="loss_bwd", grid=(T // tm,),
        in_specs=[tile, tile, pl.BlockSpec((1, D), lambda i: (0, 0))],
        out_specs=[tile, tile, pl.BlockSpec((1, D), lambda i: (0, 0)), pl.BlockSpec((1, 1), lambda i: (0, 0))],
        out_shape=[jax.ShapeDtypeStruct((T, D), F32), jax.ShapeDtypeStruct((T, D), BF16),
                   jax.ShapeDtypeStruct((1, D), F32), jax.ShapeDtypeStruct((1, 1), F32)],
        compiler_params=_params(("arbitrary",)),
    )(h2, tgt, g)


def _rms_bwd_res(dn, h, g, dres, *, tm, name, bf16_copy=True, rides=()):
    T, D = h.shape

    def body(dn_ref, h_ref, g_ref, dres_ref, dh_ref, *rest):
        dg_ref = rest[-1]

        @pl.when(pl.program_id(0) == 0)
        def _():
            dg_ref[...] = jnp.zeros_like(dg_ref)
        h_ = h_ref[...]
        dn_ = dn_ref[...]
        dh, r = _rms_bwd(dn_, h_, g_ref[...])
        dg_ref[...] += jnp.sum(dn_ * (h_ * r), axis=0, keepdims=True)
        dh = dres_ref[...] + dh
        dh_ref[...] = dh
        if bf16_copy:
            rest[0][...] = dh.astype(BF16)

    tile = pl.BlockSpec((tm, D), lambda i: (i, 0))
    row = pl.BlockSpec((1, D), lambda i: (0, 0))
    copy_spec = [tile] if bf16_copy else []
    copy_shape = [jax.ShapeDtypeStruct((T, D), BF16)] if bf16_copy else []
    return _call(
        body, name=name, grid=(T // tm,),
        in_specs=[tile, tile, row, tile], out_specs=[tile] + copy_spec + [row],
        out_shape=[jax.ShapeDtypeStruct((T, D), F32)] + copy_shape + [jax.ShapeDtypeStruct((1, D), F32)],
        sem=("arbitrary",), rides=rides,
    )(dn, h, g, dres)


def _rel_distance():
    i = lax.broadcasted_iota(jnp.int32, (CHUNK, 2 * CHUNK), 0)
    j = lax.broadcasted_iota(jnp.int32, (CHUNK, 2 * CHUNK), 1)
    return i + CHUNK - j


def _bias_build(table):
    def body(tab_ref, o_ref):
        rel = _rel_distance()
        j = lax.broadcasted_iota(jnp.int32, (CHUNK, 2 * CHUNK), 1)
        band = (rel >= 0) & (rel < CHUNK)
        ge = [rel >= t for t in BUCKET_THR]
        for h in range(B_HEADS):
            cur = jnp.full((CHUNK, 2 * CHUNK), tab_ref[0, h], F32)
            for b in range(1, N_BUCKETS):
                cur = jnp.where(ge[b - 1], tab_ref[b, h], cur)
            o_ref[0, h] = jnp.where(band & (j >= CHUNK), cur, NEG)
            o_ref[1, h] = jnp.where(band, cur, NEG)

    return pl.pallas_call(
        body, name="bias_build",
        in_specs=[pl.BlockSpec(memory_space=pltpu.SMEM)],
        out_specs=pl.BlockSpec(memory_space=pltpu.VMEM),
        out_shape=jax.ShapeDtypeStruct((2, B_HEADS, CHUNK, 2 * CHUNK), F32),
    )(table)


def _bias_grad(dbias, rides=()):
    def body(db_ref, o_ref, acc_ref):
        rel = _rel_distance()
        lo = [0] + BUCKET_THR
        hi = BUCKET_THR + [CHUNK]
        for b in range(N_BUCKETS):
            m = (rel >= lo[b]) & (rel < hi[b])
            for h in range(B_HEADS):
                row = b * B_HEADS + h
                acc_ref[row:row + 1, :] = jnp.sum(jnp.where(m, db_ref[h], 0.0), axis=0, keepdims=True)
        o_ref[...] = jnp.sum(acc_ref[...], axis=1, keepdims=True)

    return _call(
        body, name="bias_grad", grid=(1,),
        in_specs=[pl.BlockSpec(dbias.shape, lambda i: (0, 0, 0))],
        out_specs=pl.BlockSpec((N_BUCKETS * B_HEADS, 1), lambda i: (0, 0)),
        out_shape=jax.ShapeDtypeStruct((N_BUCKETS * B_HEADS, 1), F32),
        scratch_shapes=[pltpu.VMEM((N_BUCKETS * B_HEADS, 2 * CHUNK), F32)],
        sem=("arbitrary",), rides=rides,
    )(dbias)


def _causal_mask():
    t = lax.broadcasted_iota(jnp.int32, (CHUNK, CHUNK), 0)
    s = lax.broadcasted_iota(jnp.int32, (CHUNK, CHUNK), 1)
    return s <= t


def _gate_forward(u, v, lg, lb, wc, bs):
    ug = _gelu(u)
    vg = _gelu(v)
    mu = jnp.mean(vg, axis=-1, keepdims=True)
    xc = vg - mu
    rstd = lax.rsqrt(jnp.mean(xc * xc, axis=-1, keepdims=True) + EPS)
    xhat = xc * rstd
    vl = (xhat * lg + lb).astype(BF16)
    mixed = _dot(wc, vl) + bs
    return ug, xhat, rstd, vl, mixed


def _softmax_scores(qk, bias, sink):
    s = qk + bias
    m = jnp.maximum(jnp.max(s, axis=-1, keepdims=True), sink)
    p = jnp.exp(s - m)
    e_sink = jnp.exp(sink - m)
    inv = 1.0 / (jnp.sum(p, axis=-1, keepdims=True) + e_sink)
    return p * inv, e_sink * inv


PAIRS = Q_PER_KV // 2


def _head(g, pr, e):
    return g * Q_PER_KV + 2 * pr + e


def _stack_pairs(ref, g, col0=0):
    w = 2 * HEAD_DIM
    return jnp.concatenate([ref[:, col0 + (g * PAIRS + pr) * w:col0 + (g * PAIRS + pr + 1) * w] for pr in range(PAIRS)],
                           axis=0)


def _low_lanes():
    return lax.broadcasted_iota(jnp.int32, (2 * CHUNK, 2 * HEAD_DIM), 1) < HEAD_DIM


def _band_operands(kv_prev, kv_cur):
    band = jnp.concatenate([kv_prev, kv_cur], axis=0)
    low = _low_lanes()
    ops = []
    for cat in (band[:, :KV_WIDTH], band[:, KV_WIDTH:]):
        rol = pltpu.roll(cat, HEAD_DIM, 1)
        ops.append([[jnp.where(low if e == 0 else ~low, cat if g == e else rol, 0.0).astype(BF16) for e in range(2)]
                    for g in range(2)])
    return ops


def _mixer_fwd(proj, lg, lb, wsp, bs_col, sinks, bias, ga, gb, rides=()):
    T = proj.shape[0]
    nb = T // CHUNK

    def body(u_ref, v_ref, q_ref, kvc_ref, kvp_ref, lg_ref, lb_ref, w_ref, bs_ref, sink_ref, bias_ref,
             ga_ref, gb_ref, mixed_ref, mixed_t_ref, ab_ref):
        causal = _causal_mask()
        ssq = jnp.zeros((CHUNK, 1), F32)
        for g in range(A_GROUPS):
            cols = slice(g * CHUNK, (g + 1) * CHUNK)
            wc = jnp.where(causal, w_ref[g], 0.0).astype(BF16)
            ug, _, _, _, mixed = _gate_forward(u_ref[:, cols], v_ref[:, cols], lg_ref[g:g + 1, :], lb_ref[g:g + 1, :],
                                               wc, bs_ref[g])
            a = ug * mixed
            ab_ref[:, cols] = a
            ssq = ssq + jnp.sum(a * a, axis=-1, keepdims=True)
        ra = lax.rsqrt(ssq * (1.0 / A_WIDTH) + EPS)
        mixed_ref[:, :A_WIDTH] = ((ab_ref[:, :A_WIDTH] * ra) * ga_ref[...]).astype(BF16)

        kops, vops = _band_operands(kvp_ref[...], kvc_ref[...])
        ssq = jnp.zeros((CHUNK, 1), F32)
        for g in range(B_HEADS // Q_PER_KV):
            qst = (_stack_pairs(q_ref, g) * SCALE).astype(BF16)
            o_st = jnp.zeros((PAIRS * CHUNK, 2 * HEAD_DIM), F32)
            for e in range(2):
                s_all = _dot_nt(qst, kops[g][e])
                ps = []
                for pr in range(PAIRS):
                    h = _head(g, pr, e)
                    p, _ = _softmax_scores(s_all[pr * CHUNK:(pr + 1) * CHUNK], bias_ref[h], sink_ref[0, h])
                    ps.append(p.astype(BF16))
                o_st = o_st + _dot(jnp.concatenate(ps, axis=0), vops[g][e])
            for pr in range(PAIRS):
                o = o_st[pr * CHUNK:(pr + 1) * CHUNK]
                c0 = A_WIDTH + (g * PAIRS + pr) * 2 * HEAD_DIM
                ab_ref[:, c0:c0 + 2 * HEAD_DIM] = o
                ssq = ssq + jnp.sum(o * o, axis=-1, keepdims=True)
        rb = lax.rsqrt(ssq * (1.0 / B_WIDTH) + EPS)
        mixed_ref[:, A_WIDTH:] = ((ab_ref[:, A_WIDTH:] * rb) * gb_ref[...]).astype(BF16)
        mixed_t_ref[...] = mixed_ref[...].T

    full = lambda *shape: pl.BlockSpec(shape, lambda n: (0,) * len(shape))
    return _call(
        body, name="mixer_fwd", grid=(nb,),
        in_specs=[pl.BlockSpec((CHUNK, A_WIDTH), lambda n: (n, 0)),
                  pl.BlockSpec((CHUNK, A_WIDTH), lambda n: (n, 1)),
                  pl.BlockSpec((CHUNK, B_WIDTH), lambda n: (n, 2)),
                  pl.BlockSpec((CHUNK, 2 * KV_WIDTH), lambda n: (n, 12)),
                  pl.BlockSpec((CHUNK, 2 * KV_WIDTH), lambda n: (jnp.maximum(n - 1, 0), 12)),
                  full(A_GROUPS, CHUNK), full(A_GROUPS, CHUNK), full(A_GROUPS, CHUNK, CHUNK), full(A_GROUPS, CHUNK, 1),
                  pl.BlockSpec(memory_space=pltpu.SMEM),
                  pl.BlockSpec((None, B_HEADS, CHUNK, 2 * CHUNK), lambda n: (jnp.minimum(n, 1), 0, 0, 0)),
                  full(1, A_WIDTH), full(1, B_WIDTH)],
        out_specs=[pl.BlockSpec((CHUNK, D_MODEL), lambda n: (n, 0)), pl.BlockSpec((D_MODEL, CHUNK), lambda n: (0, n)),
                   pl.BlockSpec((CHUNK, D_MODEL), lambda n: (n, 0))],
        out_shape=[jax.ShapeDtypeStruct((T, D_MODEL), BF16), jax.ShapeDtypeStruct((D_MODEL, T), BF16),
                   jax.ShapeDtypeStruct((T, D_MODEL), F32)],
        sem=("parallel",), rides=rides,
    )(proj, proj, proj, proj, proj, lg, lb, wsp, bs_col, sinks, bias, ga, gb)


def _gmlp_bwd(proj, ab, dmixed, ga, lg, lb, wsp, bs_col, rides=()):
    T = proj.shape[0]
    nb = T // CHUNK

    def body(u_ref, v_ref, a_ref, dna_ref, ga_ref, lg_ref, lb_ref, w_ref, bs_ref,
             dp_ref, dpt_ref, dga_ref, dw_ref, dbs_ref, dlg_ref, dlb_ref):
        @pl.when(pl.program_id(0) == 0)
        def _():
            for r in (dga_ref, dw_ref, dbs_ref, dlg_ref, dlb_ref):
                r[...] = jnp.zeros_like(r)
        causal = _causal_mask()
        a_all = a_ref[...]
        dna = dna_ref[...]
        da_all, ra = _rms_bwd(dna, a_all, ga_ref[...])
        dga_ref[...] += jnp.sum(dna * (a_all * ra), axis=0, keepdims=True)
        for g in range(A_GROUPS):
            cols = slice(g * CHUNK, (g + 1) * CHUNK)
            wc = jnp.where(causal, w_ref[g], 0.0).astype(BF16)
            lgg = lg_ref[g:g + 1, :]
            u = u_ref[:, cols]
            v = v_ref[:, cols]
            ug, xhat, rstd, vl, mixed = _gate_forward(u, v, lgg, lb_ref[g:g + 1, :], wc, bs_ref[g])
            da = da_all[:, cols]
            dug = da * mixed
            dmg = da * ug
            dmg_b = dmg.astype(BF16)
            dbs_ref[g] += jnp.sum(dmg, axis=-1, keepdims=True)
            dw_ref[g] += jnp.where(causal, _dot_nt(dmg_b, vl), 0.0)
            dvl = _dot_tn(wc, dmg_b)
            dlg_ref[g:g + 1, :] += jnp.sum(dvl * xhat, axis=0, keepdims=True)
            dlb_ref[g:g + 1, :] += jnp.sum(dvl, axis=0, keepdims=True)
            dxh = dvl * lgg
            dvg = rstd * (dxh - jnp.mean(dxh, axis=-1, keepdims=True)
                          - xhat * jnp.mean(dxh * xhat, axis=-1, keepdims=True))
            _, gu = _gelu_and_grad(u)
            _, gv = _gelu_and_grad(v)
            dp_ref[:, cols] = (dug * gu).astype(BF16)
            dp_ref[:, A_WIDTH + g * CHUNK:A_WIDTH + (g + 1) * CHUNK] = (dvg * gv).astype(BF16)
        dpt_ref[...] = dp_ref[...].T

    full = lambda *shape: pl.BlockSpec(shape, lambda n: (0,) * len(shape))
    return _call(
        body, name="gmlp_bwd", grid=(nb,),
        in_specs=[pl.BlockSpec((CHUNK, A_WIDTH), lambda n: (n, 0)),
                  pl.BlockSpec((CHUNK, A_WIDTH), lambda n: (n, 1)),
                  pl.BlockSpec((CHUNK, A_WIDTH), lambda n: (n, 0)),
                  pl.BlockSpec((CHUNK, A_WIDTH), lambda n: (n, 0)),
                  full(1, A_WIDTH), full(A_GROUPS, CHUNK), full(A_GROUPS, CHUNK), full(A_GROUPS, CHUNK, CHUNK),
                  full(A_GROUPS, CHUNK, 1)],
        out_specs=[pl.BlockSpec((CHUNK, 2 * A_WIDTH), lambda n: (n, 0)), pl.BlockSpec((2 * A_WIDTH, CHUNK), lambda n: (0, n)),
                   full(1, A_WIDTH), full(A_GROUPS, CHUNK, CHUNK), full(A_GROUPS, CHUNK, 1),
                   full(A_GROUPS, CHUNK), full(A_GROUPS, CHUNK)],
        out_shape=[jax.ShapeDtypeStruct((T, 2 * A_WIDTH), BF16), jax.ShapeDtypeStruct((2 * A_WIDTH, T), BF16),
                   jax.ShapeDtypeStruct((1, A_WIDTH), F32), jax.ShapeDtypeStruct((A_GROUPS, CHUNK, CHUNK), F32),
                   jax.ShapeDtypeStruct((A_GROUPS, CHUNK, 1), F32), jax.ShapeDtypeStruct((A_GROUPS, CHUNK), F32),
                   jax.ShapeDtypeStruct((A_GROUPS, CHUNK), F32)],
        sem=("arbitrary",), rides=rides,
    )(proj, proj, ab, dmixed, ga, lg, lb, wsp, bs_col)


def _attn_bwd(proj, ab, dmixed, gb, sinks, bias, rides=()):
    T = proj.shape[0]
    nb = T // CHUNK
    qn = lambda n: jnp.minimum(n, nb - 1)

    def body(q_ref, kvc_ref, kvp_ref, o_ref, dnb_ref, gb_ref, sink_ref, bias_ref,
             dq_ref, dkv_ref, dqt_ref, dkvt_ref, dgb_ref, dsink_ref, dbias_ref, carry_ref, sacc_ref):
        n = pl.program_id(0)

        @pl.when(n == 0)
        def _():
            carry_ref[...] = jnp.zeros_like(carry_ref)
            sacc_ref[...] = jnp.zeros_like(sacc_ref)
            dgb_ref[...] = jnp.zeros_like(dgb_ref)
            dbias_ref[...] = jnp.zeros_like(dbias_ref)

        @pl.when(n < nb)
        def _():
            o_all = o_ref[...]
            dnb = dnb_ref[...]
            do_all, rb = _rms_bwd(dnb, o_all, gb_ref[...])
            dgb_ref[...] += jnp.sum(dnb * (o_all * rb), axis=0, keepdims=True)
            kops, vops = _band_operands(kvp_ref[...], kvc_ref[...])
            low = _low_lanes()
            halves = []
            for g in range(B_HEADS // Q_PER_KV):
                qst = (_stack_pairs(q_ref, g) * SCALE).astype(BF16)
                dost = _stack_pairs(do_all, g).astype(BF16)
                dq_st = jnp.zeros((PAIRS * CHUNK, 2 * HEAD_DIM), F32)
                dk_e, dv_e = [], []
                for e in range(2):
                    s_all = _dot_nt(qst, kops[g][e])
                    dp_all = _dot_nt(dost, vops[g][e])
                    ps, dsrs = [], []
                    for pr in range(PAIRS):
                        h = _head(g, pr, e)
                        rows = slice(pr * CHUNK, (pr + 1) * CHUNK)
                        p, p_sink = _softmax_scores(s_all[rows], bias_ref[h], sink_ref[0, h])
                        dp = dp_all[rows]
                        delta = jnp.sum(p * dp, axis=-1, keepdims=True)
                        ds = p * (dp - delta)
                        sacc_ref[:, h:h + 1] += -(p_sink * delta)
                        dbias_ref[h] += ds
                        ps.append(p.astype(BF16))
                        dsrs.append(ds.astype(BF16))
                    dsr_all = jnp.concatenate(dsrs, axis=0)
                    dq_st = dq_st + _dot(dsr_all, kops[g][e])
                    dk_e.append(_dot_tn(dsr_all, qst))
                    dv_e.append(_dot_tn(jnp.concatenate(ps, axis=0), dost))
                for pr in range(PAIRS):
                    c0 = (g * PAIRS + pr) * 2 * HEAD_DIM
                    dq_ref[:, c0:c0 + 2 * HEAD_DIM] = (dq_st[pr * CHUNK:(pr + 1) * CHUNK] * SCALE).astype(BF16)
                halves.append((dk_e, dv_e))
            tiles = []
            for t in range(2):
                g0, g1 = halves[0][t], halves[1][t]
                tiles.append(jnp.where(low, g0[0] + pltpu.roll(g0[1], HEAD_DIM, 1), pltpu.roll(g1[0], HEAD_DIM, 1) + g1[1]))
            dband = jnp.concatenate(tiles, axis=1)
            dkv = (carry_ref[...] + dband[:CHUNK]).astype(BF16)
            dkv_ref[...] = dkv
            dkvt_ref[...] = dkv.T
            dqt_ref[...] = dq_ref[...].T
            carry_ref[...] = dband[CHUNK:]

        @pl.when(n == nb)
        def _():
            dkv = carry_ref[...].astype(BF16)
            dkv_ref[...] = dkv
            dkvt_ref[...] = dkv.T
            dsink_ref[...] = jnp.sum(sacc_ref[...], axis=0, keepdims=True)

    full = lambda *shape: pl.BlockSpec(shape, lambda n: (0,) * len(shape))
    return _call(
        body, name="attn_bwd", grid=(nb + 1,),
        in_specs=[pl.BlockSpec((CHUNK, B_WIDTH), lambda n: (qn(n), 2)),
                  pl.BlockSpec((CHUNK, 2 * KV_WIDTH), lambda n: (qn(n), 12)),
                  pl.BlockSpec((CHUNK, 2 * KV_WIDTH), lambda n: (jnp.maximum(qn(n) - 1, 0), 12)),
                  pl.BlockSpec((CHUNK, B_WIDTH), lambda n: (qn(n), 1)),
                  pl.BlockSpec((CHUNK, B_WIDTH), lambda n: (qn(n), 1)),
                  full(1, B_WIDTH), pl.BlockSpec(memory_space=pltpu.SMEM),
                  pl.BlockSpec((None, B_HEADS, CHUNK, 2 * CHUNK), lambda n: (jnp.minimum(n, 1), 0, 0, 0))],
        out_specs=[pl.BlockSpec((CHUNK, B_WIDTH), lambda n: (qn(n), 0)),
                   pl.BlockSpec((CHUNK, 2 * KV_WIDTH), lambda n: (jnp.maximum(n - 1, 0), 0)),
                   pl.BlockSpec((B_WIDTH, CHUNK), lambda n: (0, qn(n))),
                   pl.BlockSpec((2 * KV_WIDTH, CHUNK), lambda n: (0, jnp.maximum(n - 1, 0))),
                   full(1, B_WIDTH), full(1, B_HEADS), full(B_HEADS, CHUNK, 2 * CHUNK)],
        out_shape=[jax.ShapeDtypeStruct((T, B_WIDTH), BF16), jax.ShapeDtypeStruct((T, 2 * KV_WIDTH), BF16),
                   jax.ShapeDtypeStruct((B_WIDTH, T), BF16), jax.ShapeDtypeStruct((2 * KV_WIDTH, T), BF16),
                   jax.ShapeDtypeStruct((1, B_WIDTH), F32), jax.ShapeDtypeStruct((1, B_HEADS), F32),
                   jax.ShapeDtypeStruct((B_HEADS, CHUNK, 2 * CHUNK), F32)],
        scratch_shapes=[pltpu.VMEM((CHUNK, 2 * KV_WIDTH), F32), pltpu.VMEM((CHUNK, B_HEADS), F32)],
        sem=("arbitrary",), rides=rides,
    )(proj, proj, proj, ab, dmixed, gb, sinks, bias)


def _sq_relu_grad(acc, r):
    return acc * (2.0 * r.astype(F32))


def _chip_index():
    return (2 * lax.axis_index("x") + lax.axis_index("y")).astype(jnp.int32).reshape(1)


def _cast_into_slot(w, *, tm, name):
    _, R, C = w.shape

    def body(me_ref, w_ref, o_ref):
        del me_ref
        o_ref[...] = w_ref[...].astype(BF16)

    return pl.pallas_call(
        body, name=name,
        grid_spec=pltpu.PrefetchScalarGridSpec(
            num_scalar_prefetch=1, grid=(R // tm,),
            in_specs=[pl.BlockSpec((None, tm, C), lambda i, me: (0, i, 0))],
            out_specs=pl.BlockSpec((None, tm, C), lambda i, me: (me[0], i, 0))),
        out_shape=jax.ShapeDtypeStruct((N_CHIPS, R, C), BF16), compiler_params=_params(("parallel",)),
    )(_chip_index(), w)


def _cast_into_slots_carrying(ws, *, steps, name, rides):
    n = len(ws)

    def body(*refs):
        for w_ref, o_ref in zip(refs[:n], refs[n:]):
            o_ref[...] = w_ref[...].astype(BF16)

    me = lambda: 2 * lax.axis_index("x") + lax.axis_index("y")
    return _call(
        body, name=name, grid=(steps,),
        in_specs=[pl.BlockSpec((None, w.shape[1] // steps, w.shape[2]), lambda i: (0, i, 0)) for w in ws],
        out_specs=[pl.BlockSpec((None, w.shape[1] // steps, w.shape[2]), lambda i: (me(), i, 0)) for w in ws],
        out_shape=[jax.ShapeDtypeStruct((N_CHIPS,) + w.shape[1:], BF16) for w in ws], sem=("arbitrary",), rides=rides,
    )(*ws)


def _owner_total(gh, others, *, tm, name):
    _, hr, C = gh.shape

    def body(me_ref, g_ref, o_ref_in, out_ref):
        del me_ref
        acc = g_ref[...]
        for j in range(3):
            acc = acc + o_ref_in[j].astype(F32)
        out_ref[...] = acc

    return pl.pallas_call(
        body, name=name,
        grid_spec=pltpu.PrefetchScalarGridSpec(
            num_scalar_prefetch=1, grid=(hr // tm,),
            in_specs=[pl.BlockSpec((None, tm, C), lambda i, me: (me[0], i, 0)),
                      pl.BlockSpec((3, tm, C), lambda i, me: (0, i, 0))],
            out_specs=pl.BlockSpec((tm, C), lambda i, me: (i, 0))),
        out_shape=jax.ShapeDtypeStruct((hr, C), F32),
        compiler_params=_params(("parallel",)),
    )(_chip_index(), gh, others)


def _adamw_math(w, g, m, v):
    m = ADAM_B1 * m + (1.0 - ADAM_B1) * g
    v = ADAM_B2 * v + (1.0 - ADAM_B2) * (g * g)
    m_hat = m / (1.0 - ADAM_B1 ** ADAM_STEP)
    v_hat = v / (1.0 - ADAM_B2 ** ADAM_STEP)
    delta = -ADAM_LR * (m_hat / (jnp.sqrt(v_hat) + ADAM_EPS) + ADAM_WD * w)
    return delta, m, v


def _adamw_halves(w, own, got, m, v, *, tm, name, rides=()):
    _, R, C = w.shape
    nt = (R // 2) // tm

    def body(w_ref, own_ref, got_ref, m_ref, v_ref, g_ref, d_ref, nm_ref, nv_ref):
        g = jnp.where(pl.program_id(0) == lax.axis_index("c"), own_ref[...], got_ref[...])
        g_ref[...] = g
        d_ref[...], nm_ref[...], nv_ref[...] = _adamw_math(w_ref[...], g, m_ref[...], v_ref[...])

    whole = pl.BlockSpec((None, tm, C), lambda h, i: (0, h * nt + i, 0))
    half = pl.BlockSpec((tm, C), lambda h, i: (i, 0))
    return _call(
        body, name=name, grid=(2, nt), in_specs=[whole, half, half, whole, whole], out_specs=[whole] * 4,
        out_shape=[jax.ShapeDtypeStruct((1, R, C), F32)] * 4, sem=("parallel", "parallel"), rides=rides,
    )(w, own, got, m, v)


def _adamw_small(w, slots, m, v, *, name, rides=()):
    def body(w_ref, slots_ref, m_ref, v_ref, g_ref, d_ref, nm_ref, nv_ref):
        g = slots_ref[0]
        for d in range(1, N_DEV):
            g = g + slots_ref[d]
        g_ref[...] = g
        d_ref[...], nm_ref[...], nv_ref[...] = _adamw_math(w_ref[...], g, m_ref[...], v_ref[...])

    flat = pl.BlockSpec(w.shape, lambda i: (0, 0))
    return _call(
        body, name=name, grid=(1,), in_specs=[flat, pl.BlockSpec(slots.shape, lambda i: (0, 0, 0)), flat, flat],
        out_specs=[flat] * 4, out_shape=[jax.ShapeDtypeStruct(w.shape, F32)] * 4, sem=("arbitrary",), rides=rides,
    )(w, slots, m, v)


SMALL = ["rel_bias_table", "mix_norm_g", "gate_norm_g", "gate_norm_b", "w_spatial", "b_spatial", "attn_sinks",
         "out_norm_a_g", "out_norm_b_g", "ffn_norm_g", "final_norm_g"]
SMALL_A = ["gate_norm_g", "gate_norm_b", "w_spatial", "b_spatial", "out_norm_a_g"]
SMALL_B = ["rel_bias_table", "mix_norm_g", "attn_sinks", "out_norm_b_g", "ffn_norm_g", "final_norm_g"]
LARGE = ["w_in", "w_out", "w_up", "w_down"]
ROW_TILE = {"w_in": 208, "w_out": 256, "w_up": 256, "w_down": 256}
WEIGHTS = ["rel_bias_table", "mix_norm_g", "w_in", "gate_norm_g", "gate_norm_b", "w_spatial", "b_spatial", "attn_sinks",
           "out_norm_a_g", "out_norm_b_g", "w_out", "ffn_norm_g", "w_up", "w_down", "final_norm_g"]
PACK_UNIT = 8 * 128


def _pack(parts):
    rows = []
    for p in parts:
        flat = p.reshape(-1)
        pad = (-flat.shape[0]) % PACK_UNIT
        rows.append(jnp.pad(flat, (0, pad)).reshape(-1, 128))
    return jnp.concatenate(rows, axis=0)


def _unpack(packed, like):
    out, row = [], 0
    for p in like:
        n = math.prod(p.shape)
        nrows = (n + PACK_UNIT - 1) // PACK_UNIT * 8
        out.append(packed[row:row + nrows].reshape(-1)[:n].reshape(p.shape))
        row += nrows
    return out


def kernel(x, rel_bias_table, mix_norm_g, w_in, gate_norm_g, gate_norm_b, w_spatial, b_spatial, attn_sinks, out_norm_a_g, out_norm_b_g, w_out, ffn_norm_g, w_up, w_down, final_norm_g, loss_target, m_rel_bias_table, m_mix_norm_g, m_w_in, m_gate_norm_g, m_gate_norm_b, m_w_spatial, m_b_spatial, m_attn_sinks, m_out_norm_a_g, m_out_norm_b_g, m_w_out, m_ffn_norm_g, m_w_up, m_w_down, m_final_norm_g, v_rel_bias_table, v_mix_norm_g, v_w_in, v_gate_norm_g, v_gate_norm_b, v_w_spatial, v_b_spatial, v_attn_sinks, v_out_norm_a_g, v_out_norm_b_g, v_w_out, v_ffn_norm_g, v_w_up, v_w_down, v_final_norm_g):
    args = dict(locals())
    wts = {n: args[n] for n in WEIGHTS}
    mom = {n: args["m_" + n] for n in WEIGHTS}
    var = {n: args["v_" + n] for n in WEIGHTS}
    sp = {n: wts[n] for n in SMALL}
    x2, tgt = x[0], loss_target[0]
    T = x2.shape[0]
    tm = min(512, T)
    tl = min(1024, T)
    lg = sp["gate_norm_g"].reshape(A_GROUPS, CHUNK)
    lb = sp["gate_norm_b"].reshape(A_GROUPS, CHUNK)
    wsp = sp["w_spatial"].reshape(A_GROUPS, CHUNK, CHUNK)
    bs_col = sp["b_spatial"].reshape(A_GROUPS, CHUNK, 1)
    sinks = sp["attn_sinks"].reshape(1, B_HEADS)
    ga = sp["out_norm_a_g"].reshape(1, A_WIDTH)
    gb = sp["out_norm_b_g"].reshape(1, B_WIDTH)
    g1 = sp["mix_norm_g"].reshape(1, D_MODEL)
    g2 = sp["ffn_norm_g"].reshape(1, D_MODEL)
    gf = sp["final_norm_g"].reshape(1, D_MODEL)

    def owner_total(n, gh, others):
        return _owner_total(gh, others, tm=ROW_TILE[n], name="rs_owner_total_" + n)

    def halves_view(at, shards):
        return at.reshape(shards, 2, at.shape[0] // shards // 2, at.shape[1])

    for d in (wts, mom, var):
        d["w_in"] = jnp.swapaxes(d["w_in"], 1, 2)

    s_in = _cast_into_slot(wts["w_in"], tm=ROW_TILE["w_in"], name="cast_w_in")
    (s_out, s_up, s_down), ((g_in,),) = _cast_into_slots_carrying(
        [wts["w_out"], wts["w_up"], wts["w_down"]], steps=8, name="cast_w_rest",
        rides=[_ride_gather(s_in, chain=(0, 1, 1), chain_fracs=(0.3, 0.6))])
    win_t = g_in.reshape(PROJ_WIDTH, D_MODEL)
    bias = _bias_build(sp["rel_bias_table"])
    (n1, proj), ((g_out,), (s_up,)) = _norm_matmul_wide(
        x2, g1, win_t, tm=tm, tn=PROJ_WIDTH // 2, name="in_proj",
        rides=[_ride_gather(s_out, chain=(0, 1, 1), chain_fracs=(0.65, 0.85)), _ride_gather(s_up, s1=(0, 3, 8))])
    wo = g_out.reshape(A_WIDTH + B_WIDTH, D_MODEL)
    (mixed, mixed_t, ab), ((s_up,), (s_down,), (n1_sib,)) = _mixer_fwd(
        proj, lg, lb, wsp, bs_col, sinks, bias, ga, gb,
        rides=[_ride_gather(s_up, s2=(0, 3, 8), s1=(3, 8, 8)), _ride_gather(s_down, s1=(0, 2, 8)),
               _ride_to_sibling(n1, first=True)])
    mixed_t = halves_view(mixed_t, N_CHIPS)
    h1, ((wu,), (s_down,), (mixed_t_sib,)) = _matmul_res(
        mixed, wo, x2, tm=tl, tn=1024, tk=D_MODEL, prologue=_to_bf16, name="out_proj",
        rides=[_ride_gather(s_up, s3=(0, 3, 8), tail=(3, 8, 8), mid_frac=0.75), _ride_gather(s_down, s2=(0, 2, 8)),
               _ride_to_sibling(mixed_t, halves=True)])
    (n2t, zp, z2, z2t), ((g_down,),) = _norm_matmul_sq(
        h1, g2, wu, tm=tl, tn=1024, name="up_proj", rides=[_ride_gather(s_down, s3=(0, 2, 8), chain=(2, 8, 8), chain_fracs=(0.5, 0.8))])
    wd = g_down.reshape(D_FF, D_MODEL)
    n2t, z2t = halves_view(n2t, 1), halves_view(z2t, N_CHIPS)
    h2, ((n2t_sib,), (z2t_sib,)) = _matmul_res(
        z2, wd, h1, tm=tl, tn=1024, tk=4096, prologue=_to_bf16, name="down_proj",
        rides=[_ride_to_sibling(n2t, halves=True), _ride_to_sibling(z2t, halves=True)])

    dh2, dh2b, dgf, loss = _loss_bwd(h2, tgt, gf, tm=tm)
    dzp, ((dh2b_sib,),) = _matmul_nt(dh2b, wd, tm=tl, tn=1024, tk=D_MODEL, name="bwd_dz", extra=zp,
                                     epilogue=_sq_relu_grad, out_dtype=BF16, rides=[_ride_to_sibling(dh2b)])
    (gd, gdb), ((dzp_sib,),) = _grad_pair(z2t, z2t_sib, dh2b, dh2b_sib, cols_sharded=False, tmo=1024, tk=tl,
                                          name="grad_w_down", rides=[_ride_to_sibling(dzp)])
    (gu, gub), ((o_d,),) = _grad_pair(n2t, n2t_sib, dzp, dzp_sib, cols_sharded=True, tmo=1024, tk=tl,
                                      name="grad_w_up", rides=[_ride_scatter(gdb, None, (0, 7, 8))])
    dn2, ((o_d,), (o_u,)) = _matmul_nt(dzp, wu, tm=tl, tn=1024, tk=4096, name="bwd_dn2",
                                       rides=[_ride_scatter(gdb, o_d, (7, 8, 8)), _ride_scatter(gub, None, (0, 6, 8))])
    h_d = owner_total("w_down", gd, o_d)
    (dh1, dh1b, dg2), ((o_u,),) = _rms_bwd_res(dn2, h1, g2, dh2, tm=tm, name="ffn_norm_bwd",
                                               rides=[_ride_scatter(gub, o_u, (6, 7, 8))])
    dmixed, ((o_u,), (dh1b_sib,), (w_d,)) = _matmul_nt(
        dh1b, wo, tm=tl, tn=1024, tk=D_MODEL, name="bwd_dmixed",
        rides=[_ride_scatter(gub, o_u, (7, 8, 8)), _ride_to_sibling(dh1b), _ride_swap(h_d)])
    h_u = owner_total("w_up", gu, o_u)
    (go, gob), ((w_u,),) = _grad_pair_merged(mixed_t, mixed_t_sib, dh1b, dh1b_sib, tk=tl, name="grad_w_out",
                                             rides=[_ride_swap(h_u)])
    (duv, duv_t, dga, dwsp, dbs, dlg, dlb), ((o_o,),) = _gmlp_bwd(proj, ab, dmixed, ga, lg, lb, wsp, bs_col,
                                                                  rides=[_ride_scatter(gob)])
    h_o = owner_total("w_out", go, o_o)
    small = {"gate_norm_g": dlg, "gate_norm_b": dlb, "w_spatial": dwsp, "b_spatial": dbs, "out_norm_a_g": dga}
    hr_in = PROJ_WIDTH // N_CHIPS // 2
    (dq, dkv, dq_t, dkv_t, dgb, dsinks, dbias), ((slots_a,), (dproj_t_sib,)) = _attn_bwd(
        proj, ab, dmixed, gb, sinks, bias,
        rides=[_ride_small_to_all(_pack([small[n] for n in SMALL_A])), _ride_rows_to_sibling(duv_t, hr_in, 2, N_CHIPS)])
    dproj_t = halves_view(jnp.concatenate([duv_t, dq_t, dkv_t], axis=0), N_CHIPS)
    dtable, ((dproj_t_sib,),) = _bias_grad(
        dbias, rides=[_ride_to_sibling(dproj_t, halves=True, shards=(2, N_CHIPS), land=dproj_t_sib)])
    (gi, gib_near), ((w_o,),) = _grad_pair(
        dproj_t, dproj_t_sib, n1, n1_sib, cols_sharded=False, tmo=hr_in, tk=tl, name="grad_w_in_near", shards="near",
        rides=[_ride_swap(h_o)])
    (gi, gib_far), ((o_i,),) = _grad_pair(
        dproj_t, dproj_t_sib, n1, n1_sib, cols_sharded=False, tmo=hr_in, tk=tl, name="grad_w_in_far", shards="far",
        into=gi, rides=[_ride_scatter(gib_near, None, to=(0, 1))])
    dn1, ((o_i,),) = _matmul_parts([duv, dq, dkv], win_t, tm=tl, tn=1024, name="bwd_dn1",
                                   rides=[_ride_scatter(gib_far, o_i, to=(2,))])
    h_i = owner_total("w_in", gi, o_i)
    dx, dg1 = _rms_bwd_res(dn1, x2, g1, dh1, tm=tm, name="mix_norm_bwd", bf16_copy=False)
    small.update({"rel_bias_table": dtable.reshape(N_BUCKETS, B_HEADS), "mix_norm_g": dg1, "attn_sinks": dsinks,
                  "out_norm_b_g": dgb, "ffn_norm_g": dg2, "final_norm_g": dgf})
    out_g, out_d, out_m, out_v = {}, {}, {}, {}

    def adamw_small(names, slots, tag, rides=()):
        extra = [jnp.zeros((1, 1), F32)] if tag == "b" else []
        like = [wts[n] for n in names] + extra
        res = _adamw_small(_pack(like), slots, _pack([mom[n] for n in names] + extra),
                           _pack([var[n] for n in names] + extra), name="adamw_small_" + tag, rides=rides)
        res, carried = res if rides else (res, None)
        for store, packed in zip((out_g, out_d, out_m, out_v), res):
            for n, val in zip(names + ["loss"], _unpack(packed, like)):
                store[n] = val
        return carried

    (w_i,), (slots_b,) = adamw_small(
        SMALL_A, slots_a, "a", rides=[_ride_swap(h_i), _ride_small_to_all(_pack([small[n] for n in SMALL_B] + [loss]))])
    for n, h, s in zip(LARGE, [h_i, h_o, h_u, h_d], [w_i, w_o, w_u, w_d]):
        res = _adamw_halves(wts[n], h, s, mom[n], var[n], tm=ROW_TILE[n], name="adamw_" + n)
        if n == "w_in":
            res = [jnp.swapaxes(r, 1, 2) for r in res]
        out_g[n], out_d[n], out_m[n], out_v[n] = res
    adamw_small(SMALL_B, slots_b, "b")

    total = out_g["loss"][0, 0]
    return (total, dx[None], *[out_g[n] for n in WEIGHTS], *[out_d[n] for n in WEIGHTS],
            *[out_m[n] for n in WEIGHTS], *[out_v[n] for n in WEIGHTS])
```

```python
import math

import numpy as np
import jax
import jax.numpy as jnp
from jax import lax
from jax.experimental import pallas as pl
from jax.experimental.pallas import tpu as pltpu

F32 = jnp.float32
BF16 = jnp.bfloat16

D_MODEL = 2048
CHUNK = 128
A_GROUPS = 8
A_WIDTH = 1024
HEAD_DIM = 64
B_HEADS = 16
Q_PER_KV = 8
B_WIDTH = 1024
KV_WIDTH = 128
PROJ_WIDTH = 3328
D_FF = 8192
N_BUCKETS = 32
EPS = 1e-5
NEG = -1e30
SCALE = HEAD_DIM ** -0.5
N_CHIPS = 4
N_DEV = 8

ADAM_LR = 0.001
ADAM_B1 = 0.9
ADAM_B2 = 0.999
ADAM_EPS = 1e-08
ADAM_WD = 0.01
ADAM_STEP = 10

VMEM_LIMIT = 60 * 1024 * 1024
MESH = pl.DeviceIdType.MESH


def _bucket_thresholds():
    d = np.arange(CHUNK)
    n_exact = N_BUCKETS // 2
    relf = np.maximum(d, n_exact).astype(np.float64)
    large = n_exact + (np.log(relf / n_exact) / math.log(CHUNK / n_exact) * (N_BUCKETS - n_exact)).astype(np.int32)
    bucket = np.where(d < n_exact, d, np.minimum(large, N_BUCKETS - 1))
    return [int(np.min(d[bucket >= b])) for b in range(1, N_BUCKETS)]


BUCKET_THR = _bucket_thresholds()


def _params(sem=None):
    return pltpu.CompilerParams(dimension_semantics=sem, vmem_limit_bytes=VMEM_LIMIT)


def _gelu(x):
    c = math.sqrt(2.0 / math.pi)
    return 0.5 * x * (1.0 + jnp.tanh(c * (x + 0.044715 * (x * x * x))))


def _gelu_and_grad(x):
    c = math.sqrt(2.0 / math.pi)
    x2 = x * x
    t = jnp.tanh(c * (x + 0.044715 * (x2 * x)))
    g = 0.5 * x * (1.0 + t)
    dg = 0.5 * (1.0 + t) + 0.5 * x * (1.0 - t * t) * (c * (1.0 + 3.0 * 0.044715 * x2))
    return g, dg


def _dot(a, b):
    return jnp.dot(a, b, preferred_element_type=F32)


def _dot_nt(a, b):
    return lax.dot_general(a, b, (((1,), (1,)), ((), ())), preferred_element_type=F32)


def _dot_tn(a, b):
    return lax.dot_general(a, b, (((0,), (0,)), ((), ())), preferred_element_type=F32)


def _rms_bwd(dn, h, g):
    r = lax.rsqrt(jnp.mean(h * h, axis=-1, keepdims=True) + EPS)
    w = dn * g
    dh = r * w - h * ((r * r * r) * jnp.mean(w * h, axis=-1, keepdims=True))
    return dh, r


def _place():
    x, y, c = lax.axis_index("x"), lax.axis_index("y"), lax.axis_index("c")
    chips = [(1 - x, y), (x, 1 - y), (1 - x, 1 - y)]
    return x, y, c, chips


def _remote(src, dst, send_sem, recv_sem, to):
    return pltpu.make_async_remote_copy(src_ref=src, dst_ref=dst, send_sem=send_sem, recv_sem=recv_sem,
                                        device_id=to, device_id_type=MESH)


class _Ride:
    def __init__(self, args, out_shape, n_sem, start, finish, mids=(), aliases=None):
        self.args, self.out_shape, self.n_sem = list(args), list(out_shape), n_sem
        self.start, self.mids, self.finish = start, list(mids), finish
        self.aliases = dict(aliases or {})


def _call(body, *, name, grid, in_specs, out_specs, out_shape, scratch_shapes=(), sem=None, rides=(), aliases=None):
    single = not isinstance(out_shape, (list, tuple))
    out_specs = [out_specs] if single else list(out_specs)
    out_shape = [out_shape] if single else list(out_shape)
    n_in, n_out, n_scr = len(in_specs), len(out_shape), len(scratch_shapes)
    r_in = [len(r.args) for r in rides]
    r_out = [len(r.out_shape) for r in rides]
    any_spec = pl.BlockSpec(memory_space=pl.ANY)
    aliases, off_i, off_o = dict(aliases or {}), n_in, n_out
    for r in rides:
        for i, o in r.aliases.items():
            aliases[off_i + i] = off_o + o
        off_i += len(r.args)
        off_o += len(r.out_shape)
    steps = math.prod(grid)

    def wrapped(*refs):
        p = 0
        ins = refs[p:p + n_in]; p += n_in
        rins = refs[p:p + sum(r_in)]; p += sum(r_in)
        outs = refs[p:p + n_out]; p += n_out
        routs = refs[p:p + sum(r_out)]; p += sum(r_out)
        scr = refs[p:p + n_scr]; p += n_scr
        sems = refs[p:]
        parts, pi, po = [], 0, 0
        for k, r in enumerate(rides):
            parts.append((rins[pi:pi + r_in[k]], routs[po:po + r_out[k]], sems[2 * k], sems[2 * k + 1]))
            pi += r_in[k]
            po += r_out[k]
        lin = 0
        for d in range(len(grid)):
            lin = lin * grid[d] + pl.program_id(d)
        if rides:
            @pl.when(lin == 0)
            def _():
                for r, part in zip(rides, parts):
                    r.start(*part)
        body(*ins, *outs, *scr)
        for r, part in zip(rides, parts):
            for frac, fn in r.mids:
                @pl.when(lin == min(steps - 1, int(frac * steps)))
                def _(fn=fn, part=part):
                    fn(*part)
        if rides:
            @pl.when(lin == steps - 1)
            def _():
                for r, part in zip(rides, parts):
                    r.finish(*part)

    scratch = list(scratch_shapes)
    for r in rides:
        scratch += [pltpu.SemaphoreType.DMA((r.n_sem,)), pltpu.SemaphoreType.DMA((r.n_sem,))]
    if rides:
        sem = ("arbitrary",) * len(grid)
    res = pl.pallas_call(
        wrapped, name=name, grid=grid,
        in_specs=list(in_specs) + [any_spec] * sum(r_in),
        out_specs=out_specs + [any_spec] * sum(r_out),
        out_shape=out_shape + [s for r in rides for s in r.out_shape],
        scratch_shapes=scratch, input_output_aliases=aliases,
        compiler_params=_params(sem),
    )

    def run(*args):
        got = res(*args, *[a for r in rides for a in r.args])
        mine = got[0] if single else list(got[:n_out])
        if not rides:
            return mine
        rest, out = list(got[n_out:]), []
        for k in range(len(rides)):
            out.append(rest[:r_out[k]])
            rest = rest[r_out[k]:]
        return mine, out

    return run


def _ride_gather(slot, s1=None, s2=None, s3=None, tail=None, chain=None, mid_frac=0.6, chain_fracs=(0.35, 0.7)):
    half = slot.shape[1] // 2

    def rows(part, c, which=None):
        k0, k1, n = part
        count, first = (k1 - k0) * (half // n), c * half + k0 * (half // n)
        return pl.ds(first, count) if which is None else pl.ds(first + which * (count // 2), count // 2)

    def ids():
        x, y, c, _ = _place()
        return x, y, c, 2 * x + y, 2 * (1 - x) + y, 2 * x + (1 - y), 2 * (1 - x) + (1 - y)

    def copy(full, chip, r, ss, rs, k, to):
        piece = full.at[chip, r, :]
        return _remote(piece, piece, ss.at[k], rs.at[k], to)

    def to_neighbours(full, ss, rs, part, base):
        x, y, c, me, _, _, _ = ids()
        return [copy(full, me, rows(part, c), ss, rs, base, (1 - x, y, c)),
                copy(full, me, rows(part, c), ss, rs, base + 1, (x, 1 - y, c))]

    def from_neighbours(full, ss, rs, part, base):
        x, y, c, _, cx, cy, _ = ids()
        return [copy(full, cx, rows(part, c), ss, rs, base, (x, y, c)), copy(full, cy, rows(part, c), ss, rs, base + 1, (x, y, c))]

    def onward(full, ss, rs, part, base):
        x, y, c, _, cx, cy, _ = ids()
        return [copy(full, cx, rows(part, c, 0), ss, rs, base, (x, 1 - y, c)),
                copy(full, cy, rows(part, c, 1), ss, rs, base + 1, (1 - x, y, c))]

    def from_onward(full, ss, rs, part, base):
        x, y, c, _, _, _, cd = ids()
        return [copy(full, cd, rows(part, c, 0), ss, rs, base, (x, y, c)), copy(full, cd, rows(part, c, 1), ss, rs, base + 1, (x, y, c))]

    def to_sibling(full, ss, rs, part, base, diagonal):
        x, y, c, _, cx, cy, cd = ids()
        return [copy(full, chip, rows(part, c), ss, rs, base + j, (x, y, 1 - c))
                for j, chip in enumerate([cd] if diagonal else [cx, cy])]

    def from_sibling(full, ss, rs, part, base, diagonal):
        x, y, c, _, cx, cy, cd = ids()
        return [copy(full, chip, rows(part, 1 - c), ss, rs, base + j, (x, y, c))
                for j, chip in enumerate([cd] if diagonal else [cx, cy])]

    def start(ins, outs, ss, rs):
        full, cps = outs[0], []
        for part, base in ((s1, 0), (chain, 12)):
            if part is not None:
                cps += to_neighbours(full, ss, rs, part, base)
        for part, b_ici, b_sib in ((s2, 2, 4), (tail, 7, 9)):
            if part is not None:
                cps += onward(full, ss, rs, part, b_ici) + to_sibling(full, ss, rs, part, b_sib, False)
        if s3 is not None:
            cps += to_sibling(full, ss, rs, s3, 6, True)
        for cp in cps:
            cp.start()

    def second(part, b_in, b_ici, b_sib):
        def fn(ins, outs, ss, rs):
            for cp in from_neighbours(outs[0], ss, rs, part, b_in):
                cp.wait_recv()
            for cp in onward(outs[0], ss, rs, part, b_ici) + to_sibling(outs[0], ss, rs, part, b_sib, False):
                cp.start()
        return fn

    def third(part, b_ici, b_sib):
        def fn(ins, outs, ss, rs):
            for cp in from_onward(outs[0], ss, rs, part, b_ici):
                cp.wait_recv()
            for cp in to_sibling(outs[0], ss, rs, part, b_sib, True):
                cp.start()
        return fn

    mids = []
    if tail is not None:
        mids.append((mid_frac, third(tail, 7, 11)))
    if chain is not None:
        mids += [(chain_fracs[0], second(chain, 12, 14, 16)), (chain_fracs[1], third(chain, 14, 18))]

    def finish(ins, outs, ss, rs):
        full, got, sent = outs[0], [], []
        if s1 is not None:
            got += from_neighbours(full, ss, rs, s1, 0)
            sent += to_neighbours(full, ss, rs, s1, 0)
        if s2 is not None:
            got += from_onward(full, ss, rs, s2, 2) + from_sibling(full, ss, rs, s2, 4, False)
            sent += onward(full, ss, rs, s2, 2) + to_sibling(full, ss, rs, s2, 4, False)
        if s3 is not None:
            got += from_sibling(full, ss, rs, s3, 6, True)
            sent += to_sibling(full, ss, rs, s3, 6, True)
        if tail is not None:
            got += from_sibling(full, ss, rs, tail, 9, False) + from_sibling(full, ss, rs, tail, 11, True)
            sent += onward(full, ss, rs, tail, 7) + to_sibling(full, ss, rs, tail, 9, False) + to_sibling(full, ss, rs, tail, 11, True)
        if chain is not None:
            got += from_sibling(full, ss, rs, chain, 16, False) + from_sibling(full, ss, rs, chain, 18, True)
            sent += (to_neighbours(full, ss, rs, chain, 12) + onward(full, ss, rs, chain, 14)
                     + to_sibling(full, ss, rs, chain, 16, False) + to_sibling(full, ss, rs, chain, 18, True))
        for cp in got:
            cp.wait_recv()
        for cp in sent:
            cp.wait_send()

    return _Ride([slot], [jax.ShapeDtypeStruct(slot.shape, slot.dtype)], 19, start, finish, mids=mids, aliases={0: 0})


def _ride_scatter(q, land=None, part=(0, 1), to=(0, 1, 2)):
    k0, k1, n = part if len(part) == 3 else (part[0], part[0] + 1, part[1])
    rows_n = q.shape[1] // n
    rows = pl.ds(k0 * rows_n, (k1 - k0) * rows_n)

    def copies(ins, outs, ss, rs):
        x, y, c, chips = _place()
        return [_remote(ins[0].at[2 * chip[0] + chip[1], rows, :], outs[0].at[j, rows, :], ss.at[j], rs.at[j], (*chip, c))
                for j, chip in enumerate(chips) if j in to]

    def start(*a):
        for cp in copies(*a):
            cp.start()

    def finish(*a):
        for cp in copies(*a):
            cp.wait()

    shape = jax.ShapeDtypeStruct((3,) + q.shape[1:], q.dtype)
    if land is None:
        return _Ride([q], [shape], 3, start, finish)
    return _Ride([q, land], [shape], 3, start, finish, aliases={1: 0})


def _ride_to_sibling(a, halves=False, first=False, shards=None, land=None):
    s0, s1 = shards or (0, a.shape[0])

    def copy(ins, outs, ss, rs):
        x, y, c, _ = _place()
        if halves:
            src, dst = ins[0].at[s0:s1, 1 - c], outs[0].at[s0:s1]
        else:
            src, dst = (ins[0].at[0] if first else ins[0]), outs[0]
        return _remote(src, dst, ss.at[0], rs.at[0], (x, y, 1 - c))

    shape = (a.shape[0],) + a.shape[2:] if halves else (a.shape[1:] if first else a.shape)
    return _Ride([a] if land is None else [a, land], [jax.ShapeDtypeStruct(shape, a.dtype)], 1,
                 lambda *a_: copy(*a_).start(), lambda *a_: copy(*a_).wait(), aliases=None if land is None else {1: 0})


def _ride_rows_to_sibling(a, hr, shards, total):
    def copies(ins, outs, ss, rs):
        x, y, c, _ = _place()
        return [_remote(ins[0].at[pl.ds((2 * s + 1 - c) * hr, hr), :], outs[0].at[s], ss.at[s], rs.at[s], (x, y, 1 - c))
                for s in range(shards)]

    def start(*a_):
        for cp in copies(*a_):
            cp.start()

    def finish(*a_):
        for cp in copies(*a_):
            cp.wait()

    return _Ride([a], [jax.ShapeDtypeStruct((total, hr, a.shape[1]), a.dtype)], shards, start, finish)


def _ride_swap(h):
    def copy(ins, outs, ss, rs):
        x, y, c, _ = _place()
        return _remote(ins[0], outs[0], ss.at[0], rs.at[0], (x, y, 1 - c))

    return _Ride([h], [jax.ShapeDtypeStruct(h.shape, h.dtype)], 1,
                 lambda *a: copy(*a).start(), lambda *a: copy(*a).wait())


def _mesh_place(p):
    return (p // 4, (p // 2) % 2, p % 2)


def _ride_small_to_all(packed):
    def copies(ins, outs, ss, rs):
        x, y, c, _ = _place()
        me = 4 * x + 2 * y + c
        return [_remote(ins[0], outs[0].at[me], ss.at[k - 1], rs.at[k - 1], _mesh_place((me + k) % N_DEV))
                for k in range(1, N_DEV)]

    def own(ins, outs, ss, rs):
        x, y, c, _ = _place()
        return pltpu.make_async_copy(ins[0], outs[0].at[4 * x + 2 * y + c], ss.at[N_DEV - 1])

    def start(*a):
        own(*a).start()
        for cp in copies(*a):
            cp.start()

    def finish(ins, outs, ss, rs):
        x, y, c, _ = _place()
        me = 4 * x + 2 * y + c
        for k in range(1, N_DEV):
            _remote(ins[0], outs[0].at[(me + N_DEV - k) % N_DEV], ss.at[k - 1], rs.at[k - 1], (x, y, c)).wait_recv()
        for cp in copies(ins, outs, ss, rs):
            cp.wait_send()
        own(ins, outs, ss, rs).wait()

    return _Ride([packed], [jax.ShapeDtypeStruct((N_DEV,) + packed.shape, packed.dtype)], N_DEV, start, finish)


def _norm_bf16(a_ref, g_ref):
    xf = a_ref[...]
    r = lax.rsqrt(jnp.mean(xf * xf, axis=-1, keepdims=True) + EPS)
    return ((xf * r) * g_ref[...]).astype(BF16)


def _norm_matmul_wide(a, g, b, *, tm, tn, name, rides=()):
    T, K = a.shape
    N = b.shape[0]

    def body(a_ref, g_ref, b_ref, n_ref, o_ref):
        n = _norm_bf16(a_ref, g_ref)
        n_ref[...] = n
        o_ref[...] = _dot_nt(n, b_ref[...])

    return _call(
        body, name=name, grid=(N // tn, T // tm),
        in_specs=[pl.BlockSpec((tm, K), lambda j, i: (i, 0)), pl.BlockSpec((1, K), lambda j, i: (0, 0)),
                  pl.BlockSpec((tn, K), lambda j, i: (j, 0))],
        out_specs=[pl.BlockSpec((None, tm, K), lambda j, i: (j, i, 0)), pl.BlockSpec((tm, tn), lambda j, i: (i, j))],
        out_shape=[jax.ShapeDtypeStruct((N // tn, T, K), BF16), jax.ShapeDtypeStruct((T, N), F32)],
        sem=("arbitrary", "arbitrary"), rides=rides,
    )(a, g, b)


def _norm_matmul_sq(a, g, b, *, tm, tn, name, rides=()):
    T, K = a.shape
    per = b.shape[2] // tn
    N = b.shape[0] * b.shape[2]

    def body(a_ref, g_ref, b_ref, nt_ref, o_ref, z_ref, zt_ref, n_scr):
        @pl.when(pl.program_id(1) == 0)
        def _():
            n = _norm_bf16(a_ref, g_ref)
            n_scr[...] = n
            nt_ref[...] = n.T
        r = jnp.maximum(_dot(n_scr[...], b_ref[...]), 0.0)
        o_ref[...] = r.astype(BF16)
        z = (r * r).astype(BF16)
        z_ref[...] = z
        zt_ref[...] = z.T

    return _call(
        body, name=name, grid=(T // tm, N // tn),
        in_specs=[pl.BlockSpec((tm, K), lambda i, j: (i, 0)), pl.BlockSpec((1, K), lambda i, j: (0, 0)),
                  pl.BlockSpec((None, K, tn), lambda i, j: (j // per, 0, j % per))],
        out_specs=[pl.BlockSpec((K, tm), lambda i, j: (0, i)), pl.BlockSpec((tm, tn), lambda i, j: (i, j)),
                   pl.BlockSpec((tm, tn), lambda i, j: (i, j)), pl.BlockSpec((tn, tm), lambda i, j: (j, i))],
        out_shape=[jax.ShapeDtypeStruct((K, T), BF16), jax.ShapeDtypeStruct((T, N), BF16),
                   jax.ShapeDtypeStruct((T, N), BF16), jax.ShapeDtypeStruct((N, T), BF16)],
        scratch_shapes=[pltpu.VMEM((tm, K), BF16)],
        sem=("parallel", "arbitrary"), rides=rides,
    )(a, g, b)


def _grad_pair(at, at_sib, b, b_sib, *, cols_sharded, tmo, tk, name, shards=None, into=None, rides=()):
    S, _, hr, T = at.shape
    C = b.shape[-1] // N_CHIPS if cols_sharded else b.shape[-1]
    nk = T // tk

    def shard(s):
        if shards is None:
            return s
        x, y = lax.axis_index("x"), lax.axis_index("y")
        first, second = ((2 * (1 - x) + y, 2 * x + (1 - y)) if shards == "near" else (2 * (1 - x) + (1 - y), 2 * x + y))
        return jnp.where(s == 0, first, second)

    a_sel = (lambda s: 0) if cols_sharded else shard
    b_sel = shard if cols_sharded else (lambda s: 0)
    if b.ndim == 3:
        b_spec = pl.BlockSpec((None, tk, C), lambda s, i, k: (0, k, b_sel(s)))
    else:
        b_spec = pl.BlockSpec((tk, C), lambda s, i, k: (k, b_sel(s)))
    n_into = 0 if into is None else 1

    def body(a_ref, as_ref, b_ref, bs_ref, *rest):
        o_ref, ob_ref = rest[n_into:]
        k = pl.program_id(2)
        p = _dot(a_ref[...], b_ref[...]) + _dot(as_ref[...], bs_ref[...])

        @pl.when(k == 0)
        def _():
            o_ref[...] = p

        @pl.when(k > 0)
        def _():
            o_ref[...] += p

        @pl.when(k == nk - 1)
        def _():
            ob_ref[...] = o_ref[...].astype(BF16)

    out = pl.BlockSpec((None, tmo, C), lambda s, i, k: (shard(s), i, 0))
    held = [pl.BlockSpec(memory_space=pl.ANY)] * n_into
    return _call(
        body, name=name, grid=(N_CHIPS if shards is None else 2, hr // tmo, nk),
        in_specs=[pl.BlockSpec((None, None, tmo, tk), lambda s, i, k: (a_sel(s), lax.axis_index("c"), i, k)),
                  pl.BlockSpec((None, tmo, tk), lambda s, i, k: (a_sel(s), i, k)),
                  b_spec, pl.BlockSpec((tk, C), lambda s, i, k: (k, b_sel(s)))] + held,
        out_specs=[out, out],
        out_shape=[jax.ShapeDtypeStruct((N_CHIPS, hr, C), F32), jax.ShapeDtypeStruct((N_CHIPS, hr, C), BF16)],
        sem=("parallel", "parallel", "arbitrary"), rides=rides, aliases={4: 0} if into is not None else None,
    )(at, at_sib, b, b_sib, *([into] if into is not None else []))


def _grad_pair_merged(at, at_sib, b, b_sib, *, tk, name, rides=()):
    S, _, hr, T = at.shape
    C = b.shape[-1]
    nk = T // tk

    def body(a_ref, as_ref, b_ref, bs_ref, o_ref, ob_ref):
        k = pl.program_id(0)
        p = (_dot(a_ref[...].reshape(S * hr, tk), b_ref[...])
             + _dot(as_ref[...].reshape(S * hr, tk), bs_ref[...])).reshape(S, hr, C)

        @pl.when(k == 0)
        def _():
            o_ref[...] = p

        @pl.when(k > 0)
        def _():
            o_ref[...] += p

        @pl.when(k == nk - 1)
        def _():
            ob_ref[...] = o_ref[...].astype(BF16)

    out = pl.BlockSpec((S, hr, C), lambda k: (0, 0, 0))
    return _call(
        body, name=name, grid=(nk,),
        in_specs=[pl.BlockSpec((S, None, hr, tk), lambda k: (0, lax.axis_index("c"), 0, k)),
                  pl.BlockSpec((S, hr, tk), lambda k: (0, 0, k)),
                  pl.BlockSpec((tk, C), lambda k: (k, 0)), pl.BlockSpec((tk, C), lambda k: (k, 0))],
        out_specs=[out, out],
        out_shape=[jax.ShapeDtypeStruct((S, hr, C), F32), jax.ShapeDtypeStruct((S, hr, C), BF16)],
        sem=("arbitrary",), rides=rides,
    )(at, at_sib, b, b_sib)


def _matmul_parts(parts, b, *, tm, tn, name, rides=()):
    T = parts[0].shape[0]
    N = b.shape[1]
    offs = [sum(p.shape[1] for p in parts[:i]) for i in range(len(parts))]
    assert all(o % p.shape[1] == 0 for o, p in zip(offs, parts))

    def body(*refs):
        n = len(parts)
        acc = _dot(refs[0][...], refs[n][...])
        for i in range(1, n):
            acc = acc + _dot(refs[i][...], refs[n + i][...])
        refs[-1][...] = acc

    a_specs = [pl.BlockSpec((tm, p.shape[1]), lambda i, j: (i, 0)) for p in parts]
    b_specs = [pl.BlockSpec((p.shape[1], tn), lambda i, j, r=o // p.shape[1]: (r, j)) for o, p in zip(offs, parts)]
    return _call(
        body, name=name, grid=(T // tm, N // tn), in_specs=a_specs + b_specs,
        out_specs=pl.BlockSpec((tm, tn), lambda i, j: (i, j)), out_shape=jax.ShapeDtypeStruct((T, N), F32),
        sem=("parallel", "parallel"), rides=rides,
    )(*parts, *([b] * len(parts)))


def _to_bf16(v):
    return v.astype(BF16)


def _matmul_res(a, b, res, *, tm, tn, tk, prologue, name, rides=()):
    T, K = a.shape
    N = b.shape[1]

    def body(a_ref, b_ref, res_ref, o_ref):
        k = pl.program_id(2)
        p = _dot(prologue(a_ref[...]), b_ref[...])

        @pl.when(k == 0)
        def _():
            o_ref[...] = res_ref[...] + p

        @pl.when(k > 0)
        def _():
            o_ref[...] += p

    return _call(
        body, name=name, grid=(T // tm, N // tn, K // tk),
        in_specs=[pl.BlockSpec((tm, tk), lambda i, j, k: (i, k)), pl.BlockSpec((tk, tn), lambda i, j, k: (k, j)),
                  pl.BlockSpec((tm, tn), lambda i, j, k: (i, j))],
        out_specs=pl.BlockSpec((tm, tn), lambda i, j, k: (i, j)),
        out_shape=jax.ShapeDtypeStruct((T, N), F32),
        sem=("parallel", "parallel", "arbitrary"), rides=rides,
    )(a, b, res)


def _matmul_nt(a, b, *, tm, tn, tk, name, extra=None, epilogue=None, out_dtype=F32, rides=()):
    T, K = a.shape
    two = b.ndim == 3 and tk == 2 * b.shape[2]
    if two:
        N, ks = b.shape[1], b.shape[2]
        b_specs = [pl.BlockSpec((None, tn, ks), lambda i, j, k: (2 * k, j, 0)),
                   pl.BlockSpec((None, tn, ks), lambda i, j, k: (2 * k + 1, j, 0))]
    elif b.ndim == 3:
        per = b.shape[2] // tk
        N = b.shape[1]
        b_specs = [pl.BlockSpec((None, tn, tk), lambda i, j, k: (k // per, j, k % per))]
    else:
        N = b.shape[0]
        b_specs = [pl.BlockSpec((tn, tk), lambda i, j, k: (j, k))]
    nb = len(b_specs)
    nk = K // tk
    assert out_dtype == F32 or nk == 1
    in_specs = [pl.BlockSpec((tm, tk), lambda i, j, k: (i, k))] + b_specs
    args = [a] + [b] * nb
    if extra is not None:
        in_specs.append(pl.BlockSpec((tm, tn), lambda i, j, k: (i, j)))
        args.append(extra)

    def body(*refs):
        a_ref, b_ref = refs[0], refs[1]
        o_ref = refs[-1]
        if two:
            p = (_dot_nt(a_ref[:, :tk // 2].astype(BF16), refs[1][...])
                 + _dot_nt(a_ref[:, tk // 2:].astype(BF16), refs[2][...]))
        else:
            p = _dot_nt(a_ref[...].astype(BF16), b_ref[...])
        if nk == 1:
            if epilogue is not None:
                p = epilogue(p, refs[1 + nb][...])
            o_ref[...] = p.astype(out_dtype)
        else:
            k = pl.program_id(2)

            @pl.when(k == 0)
            def _():
                o_ref[...] = p

            @pl.when(k > 0)
            def _():
                o_ref[...] += p

    return _call(
        body, name=name, grid=(T // tm, N // tn, nk),
        in_specs=in_specs,
        out_specs=pl.BlockSpec((tm, tn), lambda i, j, k: (i, j)),
        out_shape=jax.ShapeDtypeStruct((T, N), out_dtype),
        sem=("parallel", "parallel", "arbitrary"), rides=rides,
    )(*args)


def _loss_bwd(h2, tgt, g, *, tm):
    T, D = h2.shape

    def body(h_ref, t_ref, g_ref, dh_ref, dhb_ref, dg_ref, loss_ref):
        @pl.when(pl.program_id(0) == 0)
        def _():
            dg_ref[...] = jnp.zeros_like(dg_ref)
            loss_ref[...] = jnp.zeros_like(loss_ref)
        h = h_ref[...]
        gg = g_ref[...]
        r = lax.rsqrt(jnp.mean(h * h, axis=-1, keepdims=True) + EPS)
        hn = h * r
        err = hn * gg - t_ref[...]
        loss_ref[...] += 0.5 * jnp.sum(jnp.mean(err * err, axis=-1, keepdims=True), axis=0, keepdims=True)
        dy = err * (1.0 / D)
        dg_ref[...] += jnp.sum(dy * hn, axis=0, keepdims=True)
        w = dy * gg
        dh = r * w - h * ((r * r * r) * jnp.mean(w * h, axis=-1, keepdims=True))
        dh_ref[...] = dh
        dhb_ref[...] = dh.astype(BF16)

    tile = pl.BlockSpec((tm, D), lambda i: (i, 0))
    return pl.pallas_call(
        body, name="loss_bwd", grid=(T // tm,),
        in_specs=[tile, tile, pl.BlockSpec((1, D), lambda i: (0, 0))],
        out_specs=[tile, tile, pl.BlockSpec((1, D), lambda i: (0, 0)), pl.BlockSpec((1, 1), lambda i: (0, 0))],
        out_shape=[jax.ShapeDtypeStruct((T, D), F32), jax.ShapeDtypeStruct((T, D), BF16),
                   jax.ShapeDtypeStruct((1, D), F32), jax.ShapeDtypeStruct((1, 1), F32)],
        compiler_params=_params(("arbitrary",)),
    )(h2, tgt, g)


def _rms_bwd_res(dn, h, g, dres, *, tm, name, bf16_copy=True, rides=()):
    T, D = h.shape

    def body(dn_ref, h_ref, g_ref, dres_ref, dh_ref, *rest):
        dg_ref = rest[-1]

        @pl.when(pl.program_id(0) == 0)
        def _():
            dg_ref[...] = jnp.zeros_like(dg_ref)
        h_ = h_ref[...]
        dn_ = dn_ref[...]
        dh, r = _rms_bwd(dn_, h_, g_ref[...])
        dg_ref[...] += jnp.sum(dn_ * (h_ * r), axis=0, keepdims=True)
        dh = dres_ref[...] + dh
        dh_ref[...] = dh
        if bf16_copy:
            rest[0][...] = dh.astype(BF16)

    tile = pl.BlockSpec((tm, D), lambda i: (i, 0))
    row = pl.BlockSpec((1, D), lambda i: (0, 0))
    copy_spec = [tile] if bf16_copy else []
    copy_shape = [jax.ShapeDtypeStruct((T, D), BF16)] if bf16_copy else []
    return _call(
        body, name=name, grid=(T // tm,),
        in_specs=[tile, tile, row, tile], out_specs=[tile] + copy_spec + [row],
        out_shape=[jax.ShapeDtypeStruct((T, D), F32)] + copy_shape + [jax.ShapeDtypeStruct((1, D), F32)],
        sem=("arbitrary",), rides=rides,
    )(dn, h, g, dres)


def _rel_distance():
    i = lax.broadcasted_iota(jnp.int32, (CHUNK, 2 * CHUNK), 0)
    j = lax.broadcasted_iota(jnp.int32, (CHUNK, 2 * CHUNK), 1)
    return i + CHUNK - j


def _bias_build(table):
    def body(tab_ref, o_ref):
        rel = _rel_distance()
        j = lax.broadcasted_iota(jnp.int32, (CHUNK, 2 * CHUNK), 1)
        band = (rel >= 0) & (rel < CHUNK)
        ge = [rel >= t for t in BUCKET_THR]
        for h in range(B_HEADS):
            cur = jnp.full((CHUNK, 2 * CHUNK), tab_ref[0, h], F32)
            for b in range(1, N_BUCKETS):
                cur = jnp.where(ge[b - 1], tab_ref[b, h], cur)
            o_ref[0, h] = jnp.where(band & (j >= CHUNK), cur, NEG)
            o_ref[1, h] = jnp.where(band, cur, NEG)

    return pl.pallas_call(
        body, name="bias_build",
        in_specs=[pl.BlockSpec(memory_space=pltpu.SMEM)],
        out_specs=pl.BlockSpec(memory_space=pltpu.VMEM),
        out_shape=jax.ShapeDtypeStruct((2, B_HEADS, CHUNK, 2 * CHUNK), F32),
    )(table)


def _bias_grad(dbias, rides=()):
    def body(db_ref, o_ref, acc_ref):
        rel = _rel_distance()
        lo = [0] + BUCKET_THR
        hi = BUCKET_THR + [CHUNK]
        for b in range(N_BUCKETS):
            m = (rel >= lo[b]) & (rel < hi[b])
            for h in range(B_HEADS):
                row = b * B_HEADS + h
                acc_ref[row:row + 1, :] = jnp.sum(jnp.where(m, db_ref[h], 0.0), axis=0, keepdims=True)
        o_ref[...] = jnp.sum(acc_ref[...], axis=1, keepdims=True)

    return _call(
        body, name="bias_grad", grid=(1,),
        in_specs=[pl.BlockSpec(dbias.shape, lambda i: (0, 0, 0))],
        out_specs=pl.BlockSpec((N_BUCKETS * B_HEADS, 1), lambda i: (0, 0)),
        out_shape=jax.ShapeDtypeStruct((N_BUCKETS * B_HEADS, 1), F32),
        scratch_shapes=[pltpu.VMEM((N_BUCKETS * B_HEADS, 2 * CHUNK), F32)],
        sem=("arbitrary",), rides=rides,
    )(dbias)


def _causal_mask():
    t = lax.broadcasted_iota(jnp.int32, (CHUNK, CHUNK), 0)
    s = lax.broadcasted_iota(jnp.int32, (CHUNK, CHUNK), 1)
    return s <= t


def _gate_forward(u, v, lg, lb, wc, bs):
    ug = _gelu(u)
    vg = _gelu(v)
    mu = jnp.mean(vg, axis=-1, keepdims=True)
    xc = vg - mu
    rstd = lax.rsqrt(jnp.mean(xc * xc, axis=-1, keepdims=True) + EPS)
    xhat = xc * rstd
    vl = (xhat * lg + lb).astype(BF16)
    mixed = _dot(wc, vl) + bs
    return ug, xhat, rstd, vl, mixed


def _softmax_scores(qk, bias, sink):
    s = qk + bias
    m = jnp.maximum(jnp.max(s, axis=-1, keepdims=True), sink)
    p = jnp.exp(s - m)
    e_sink = jnp.exp(sink - m)
    inv = 1.0 / (jnp.sum(p, axis=-1, keepdims=True) + e_sink)
    return p * inv, e_sink * inv


PAIRS = Q_PER_KV // 2


def _head(g, pr, e):
    return g * Q_PER_KV + 2 * pr + e


def _stack_pairs(ref, g, col0=0):
    w = 2 * HEAD_DIM
    return jnp.concatenate([ref[:, col0 + (g * PAIRS + pr) * w:col0 + (g * PAIRS + pr + 1) * w] for pr in range(PAIRS)],
                           axis=0)


def _low_lanes():
    return lax.broadcasted_iota(jnp.int32, (2 * CHUNK, 2 * HEAD_DIM), 1) < HEAD_DIM


def _band_operands(kv_prev, kv_cur):
    band = jnp.concatenate([kv_prev, kv_cur], axis=0)
    low = _low_lanes()
    ops = []
    for cat in (band[:, :KV_WIDTH], band[:, KV_WIDTH:]):
        rol = pltpu.roll(cat, HEAD_DIM, 1)
        ops.append([[jnp.where(low if e == 0 else ~low, cat if g == e else rol, 0.0).astype(BF16) for e in range(2)]
                    for g in range(2)])
    return ops


def _mixer_fwd(proj, lg, lb, wsp, bs_col, sinks, bias, ga, gb, rides=()):
    T = proj.shape[0]
    nb = T // CHUNK

    def body(u_ref, v_ref, q_ref, kvc_ref, kvp_ref, lg_ref, lb_ref, w_ref, bs_ref, sink_ref, bias_ref,
             ga_ref, gb_ref, mixed_ref, mixed_t_ref, ab_ref):
        causal = _causal_mask()
        ssq = jnp.zeros((CHUNK, 1), F32)
        for g in range(A_GROUPS):
            cols = slice(g * CHUNK, (g + 1) * CHUNK)
            wc = jnp.where(causal, w_ref[g], 0.0).astype(BF16)
            ug, _, _, _, mixed = _gate_forward(u_ref[:, cols], v_ref[:, cols], lg_ref[g:g + 1, :], lb_ref[g:g + 1, :],
                                               wc, bs_ref[g])
            a = ug * mixed
            ab_ref[:, cols] = a
            ssq = ssq + jnp.sum(a * a, axis=-1, keepdims=True)
        ra = lax.rsqrt(ssq * (1.0 / A_WIDTH) + EPS)
        mixed_ref[:, :A_WIDTH] = ((ab_ref[:, :A_WIDTH] * ra) * ga_ref[...]).astype(BF16)

        kops, vops = _band_operands(kvp_ref[...], kvc_ref[...])
        ssq = jnp.zeros((CHUNK, 1), F32)
        for g in range(B_HEADS // Q_PER_KV):
            qst = (_stack_pairs(q_ref, g) * SCALE).astype(BF16)
            o_st = jnp.zeros((PAIRS * CHUNK, 2 * HEAD_DIM), F32)
            for e in range(2):
                s_all = _dot_nt(qst, kops[g][e])
                ps = []
                for pr in range(PAIRS):
                    h = _head(g, pr, e)
                    p, _ = _softmax_scores(s_all[pr * CHUNK:(pr + 1) * CHUNK], bias_ref[h], sink_ref[0, h])
                    ps.append(p.astype(BF16))
                o_st = o_st + _dot(jnp.concatenate(ps, axis=0), vops[g][e])
            for pr in range(PAIRS):
                o = o_st[pr * CHUNK:(pr + 1) * CHUNK]
                c0 = A_WIDTH + (g * PAIRS + pr) * 2 * HEAD_DIM
                ab_ref[:, c0:c0 + 2 * HEAD_DIM] = o
                ssq = ssq + jnp.sum(o * o, axis=-1, keepdims=True)
        rb = lax.rsqrt(ssq * (1.0 / B_WIDTH) + EPS)
        mixed_ref[:, A_WIDTH:] = ((ab_ref[:, A_WIDTH:] * rb) * gb_ref[...]).astype(BF16)
        mixed_t_ref[...] = mixed_ref[...].T

    full = lambda *shape: pl.BlockSpec(shape, lambda n: (0,) * len(shape))
    return _call(
        body, name="mixer_fwd", grid=(nb,),
        in_specs=[pl.BlockSpec((CHUNK, A_WIDTH), lambda n: (n, 0)),
                  pl.BlockSpec((CHUNK, A_WIDTH), lambda n: (n, 1)),
                  pl.BlockSpec((CHUNK, B_WIDTH), lambda n: (n, 2)),
                  pl.BlockSpec((CHUNK, 2 * KV_WIDTH), lambda n: (n, 12)),
                  pl.BlockSpec((CHUNK, 2 * KV_WIDTH), lambda n: (jnp.maximum(n - 1, 0), 12)),
                  full(A_GROUPS, CHUNK), full(A_GROUPS, CHUNK), full(A_GROUPS, CHUNK, CHUNK), full(A_GROUPS, CHUNK, 1),
                  pl.BlockSpec(memory_space=pltpu.SMEM),
                  pl.BlockSpec((None, B_HEADS, CHUNK, 2 * CHUNK), lambda n: (jnp.minimum(n, 1), 0, 0, 0)),
                  full(1, A_WIDTH), full(1, B_WIDTH)],
        out_specs=[pl.BlockSpec((CHUNK, D_MODEL), lambda n: (n, 0)), pl.BlockSpec((D_MODEL, CHUNK), lambda n: (0, n)),
                   pl.BlockSpec((CHUNK, D_MODEL), lambda n: (n, 0))],
        out_shape=[jax.ShapeDtypeStruct((T, D_MODEL), BF16), jax.ShapeDtypeStruct((D_MODEL, T), BF16),
                   jax.ShapeDtypeStruct((T, D_MODEL), F32)],
        sem=("parallel",), rides=rides,
    )(proj, proj, proj, proj, proj, lg, lb, wsp, bs_col, sinks, bias, ga, gb)


def _gmlp_bwd(proj, ab, dmixed, ga, lg, lb, wsp, bs_col, rides=()):
    T = proj.shape[0]
    nb = T // CHUNK

    def body(u_ref, v_ref, a_ref, dna_ref, ga_ref, lg_ref, lb_ref, w_ref, bs_ref,
             dp_ref, dpt_ref, dga_ref, dw_ref, dbs_ref, dlg_ref, dlb_ref):
        @pl.when(pl.program_id(0) == 0)
        def _():
            for r in (dga_ref, dw_ref, dbs_ref, dlg_ref, dlb_ref):
                r[...] = jnp.zeros_like(r)
        causal = _causal_mask()
        a_all = a_ref[...]
        dna = dna_ref[...]
        da_all, ra = _rms_bwd(dna, a_all, ga_ref[...])
        dga_ref[...] += jnp.sum(dna * (a_all * ra), axis=0, keepdims=True)
        for g in range(A_GROUPS):
            cols = slice(g * CHUNK, (g + 1) * CHUNK)
            wc = jnp.where(causal, w_ref[g], 0.0).astype(BF16)
            lgg = lg_ref[g:g + 1, :]
            u = u_ref[:, cols]
            v = v_ref[:, cols]
            ug, xhat, rstd, vl, mixed = _gate_forward(u, v, lgg, lb_ref[g:g + 1, :], wc, bs_ref[g])
            da = da_all[:, cols]
            dug = da * mixed
            dmg = da * ug
            dmg_b = dmg.astype(BF16)
            dbs_ref[g] += jnp.sum(dmg, axis=-1, keepdims=True)
            dw_ref[g] += jnp.where(causal, _dot_nt(dmg_b, vl), 0.0)
            dvl = _dot_tn(wc, dmg_b)
            dlg_ref[g:g + 1, :] += jnp.sum(dvl * xhat, axis=0, keepdims=True)
            dlb_ref[g:g + 1, :] += jnp.sum(dvl, axis=0, keepdims=True)
            dxh = dvl * lgg
            dvg = rstd * (dxh - jnp.mean(dxh, axis=-1, keepdims=True)
                          - xhat * jnp.mean(dxh * xhat, axis=-1, keepdims=True))
            _, gu = _gelu_and_grad(u)
            _, gv = _gelu_and_grad(v)
            dp_ref[:, cols] = (dug * gu).astype(BF16)
            dp_ref[:, A_WIDTH + g * CHUNK:A_WIDTH + (g + 1) * CHUNK] = (dvg * gv).astype(BF16)
        dpt_ref[...] = dp_ref[...].T

    full = lambda *shape: pl.BlockSpec(shape, lambda n: (0,) * len(shape))
    return _call(
        body, name="gmlp_bwd", grid=(nb,),
        in_specs=[pl.BlockSpec((CHUNK, A_WIDTH), lambda n: (n, 0)),
                  pl.BlockSpec((CHUNK, A_WIDTH), lambda n: (n, 1)),
                  pl.BlockSpec((CHUNK, A_WIDTH), lambda n: (n, 0)),
                  pl.BlockSpec((CHUNK, A_WIDTH), lambda n: (n, 0)),
                  full(1, A_WIDTH), full(A_GROUPS, CHUNK), full(A_GROUPS, CHUNK), full(A_GROUPS, CHUNK, CHUNK),
                  full(A_GROUPS, CHUNK, 1)],
        out_specs=[pl.BlockSpec((CHUNK, 2 * A_WIDTH), lambda n: (n, 0)), pl.BlockSpec((2 * A_WIDTH, CHUNK), lambda n: (0, n)),
                   full(1, A_WIDTH), full(A_GROUPS, CHUNK, CHUNK), full(A_GROUPS, CHUNK, 1),
                   full(A_GROUPS, CHUNK), full(A_GROUPS, CHUNK)],
        out_shape=[jax.ShapeDtypeStruct((T, 2 * A_WIDTH), BF16), jax.ShapeDtypeStruct((2 * A_WIDTH, T), BF16),
                   jax.ShapeDtypeStruct((1, A_WIDTH), F32), jax.ShapeDtypeStruct((A_GROUPS, CHUNK, CHUNK), F32),
                   jax.ShapeDtypeStruct((A_GROUPS, CHUNK, 1), F32), jax.ShapeDtypeStruct((A_GROUPS, CHUNK), F32),
                   jax.ShapeDtypeStruct((A_GROUPS, CHUNK), F32)],
        sem=("arbitrary",), rides=rides,
    )(proj, proj, ab, dmixed, ga, lg, lb, wsp, bs_col)


def _attn_bwd(proj, ab, dmixed, gb, sinks, bias, rides=()):
    T = proj.shape[0]
    nb = T // CHUNK
    qn = lambda n: jnp.minimum(n, nb - 1)

    def body(q_ref, kvc_ref, kvp_ref, o_ref, dnb_ref, gb_ref, sink_ref, bias_ref,
             dq_ref, dkv_ref, dqt_ref, dkvt_ref, dgb_ref, dsink_ref, dbias_ref, carry_ref, sacc_ref):
        n = pl.program_id(0)

        @pl.when(n == 0)
        def _():
            carry_ref[...] = jnp.zeros_like(carry_ref)
            sacc_ref[...] = jnp.zeros_like(sacc_ref)
            dgb_ref[...] = jnp.zeros_like(dgb_ref)
            dbias_ref[...] = jnp.zeros_like(dbias_ref)

        @pl.when(n < nb)
        def _():
            o_all = o_ref[...]
            dnb = dnb_ref[...]
            do_all, rb = _rms_bwd(dnb, o_all, gb_ref[...])
            dgb_ref[...] += jnp.sum(dnb * (o_all * rb), axis=0, keepdims=True)
            kops, vops = _band_operands(kvp_ref[...], kvc_ref[...])
            low = _low_lanes()
            halves = []
            for g in range(B_HEADS // Q_PER_KV):
                qst = (_stack_pairs(q_ref, g) * SCALE).astype(BF16)
                dost = _stack_pairs(do_all, g).astype(BF16)
                dq_st = jnp.zeros((PAIRS * CHUNK, 2 * HEAD_DIM), F32)
                dk_e, dv_e = [], []
                for e in range(2):
                    s_all = _dot_nt(qst, kops[g][e])
                    dp_all = _dot_nt(dost, vops[g][e])
                    ps, dsrs = [], []
                    for pr in range(PAIRS):
                        h = _head(g, pr, e)
                        rows = slice(pr * CHUNK, (pr + 1) * CHUNK)
                        p, p_sink = _softmax_scores(s_all[rows], bias_ref[h], sink_ref[0, h])
                        dp = dp_all[rows]
                        delta = jnp.sum(p * dp, axis=-1, keepdims=True)
                        ds = p * (dp - delta)
                        sacc_ref[:, h:h + 1] += -(p_sink * delta)
                        dbias_ref[h] += ds
                        ps.append(p.astype(BF16))
                        dsrs.append(ds.astype(BF16))
                    dsr_all = jnp.concatenate(dsrs, axis=0)
                    dq_st = dq_st + _dot(dsr_all, kops[g][e])
                    dk_e.append(_dot_tn(dsr_all, qst))
                    dv_e.append(_dot_tn(jnp.concatenate(ps, axis=0), dost))
                for pr in range(PAIRS):
                    c0 = (g * PAIRS + pr) * 2 * HEAD_DIM
                    dq_ref[:, c0:c0 + 2 * HEAD_DIM] = (dq_st[pr * CHUNK:(pr + 1) * CHUNK] * SCALE).astype(BF16)
                halves.append((dk_e, dv_e))
            tiles = []
            for t in range(2):
                g0, g1 = halves[0][t], halves[1][t]
                tiles.append(jnp.where(low, g0[0] + pltpu.roll(g0[1], HEAD_DIM, 1), pltpu.roll(g1[0], HEAD_DIM, 1) + g1[1]))
            dband = jnp.concatenate(tiles, axis=1)
            dkv = (carry_ref[...] + dband[:CHUNK]).astype(BF16)
            dkv_ref[...] = dkv
            dkvt_ref[...] = dkv.T
            dqt_ref[...] = dq_ref[...].T
            carry_ref[...] = dband[CHUNK:]

        @pl.when(n == nb)
        def _():
            dkv = carry_ref[...].astype(BF16)
            dkv_ref[...] = dkv
            dkvt_ref[...] = dkv.T
            dsink_ref[...] = jnp.sum(sacc_ref[...], axis=0, keepdims=True)

    full = lambda *shape: pl.BlockSpec(shape, lambda n: (0,) * len(shape))
    return _call(
        body, name="attn_bwd", grid=(nb + 1,),
        in_specs=[pl.BlockSpec((CHUNK, B_WIDTH), lambda n: (qn(n), 2)),
                  pl.BlockSpec((CHUNK, 2 * KV_WIDTH), lambda n: (qn(n), 12)),
                  pl.BlockSpec((CHUNK, 2 * KV_WIDTH), lambda n: (jnp.maximum(qn(n) - 1, 0), 12)),
                  pl.BlockSpec((CHUNK, B_WIDTH), lambda n: (qn(n), 1)),
                  pl.BlockSpec((CHUNK, B_WIDTH), lambda n: (qn(n), 1)),
                  full(1, B_WIDTH), pl.BlockSpec(memory_space=pltpu.SMEM),
                  pl.BlockSpec((None, B_HEADS, CHUNK, 2 * CHUNK), lambda n: (jnp.minimum(n, 1), 0, 0, 0))],
        out_specs=[pl.BlockSpec((CHUNK, B_WIDTH), lambda n: (qn(n), 0)),
                   pl.BlockSpec((CHUNK, 2 * KV_WIDTH), lambda n: (jnp.maximum(n - 1, 0), 0)),
                   pl.BlockSpec((B_WIDTH, CHUNK), lambda n: (0, qn(n))),
                   pl.BlockSpec((2 * KV_WIDTH, CHUNK), lambda n: (0, jnp.maximum(n - 1, 0))),
                   full(1, B_WIDTH), full(1, B_HEADS), full(B_HEADS, CHUNK, 2 * CHUNK)],
        out_shape=[jax.ShapeDtypeStruct((T, B_WIDTH), BF16), jax.ShapeDtypeStruct((T, 2 * KV_WIDTH), BF16),
                   jax.ShapeDtypeStruct((B_WIDTH, T), BF16), jax.ShapeDtypeStruct((2 * KV_WIDTH, T), BF16),
                   jax.ShapeDtypeStruct((1, B_WIDTH), F32), jax.ShapeDtypeStruct((1, B_HEADS), F32),
                   jax.ShapeDtypeStruct((B_HEADS, CHUNK, 2 * CHUNK), F32)],
        scratch_shapes=[pltpu.VMEM((CHUNK, 2 * KV_WIDTH), F32), pltpu.VMEM((CHUNK, B_HEADS), F32)],
        sem=("arbitrary",), rides=rides,
    )(proj, proj, proj, ab, dmixed, gb, sinks, bias)


def _sq_relu_grad(acc, r):
    return acc * (2.0 * r.astype(F32))


def _chip_index():
    return (2 * lax.axis_index("x") + lax.axis_index("y")).astype(jnp.int32).reshape(1)


def _cast_into_slot(w, *, tm, name):
    _, R, C = w.shape

    def body(me_ref, w_ref, o_ref):
        del me_ref
        o_ref[...] = w_ref[...].astype(BF16)

    return pl.pallas_call(
        body, name=name,
        grid_spec=pltpu.PrefetchScalarGridSpec(
            num_scalar_prefetch=1, grid=(R // tm,),
            in_specs=[pl.BlockSpec((None, tm, C), lambda i, me: (0, i, 0))],
            out_specs=pl.BlockSpec((None, tm, C), lambda i, me: (me[0], i, 0))),
        out_shape=jax.ShapeDtypeStruct((N_CHIPS, R, C), BF16), compiler_params=_params(("parallel",)),
    )(_chip_index(), w)


def _cast_into_slots_carrying(ws, *, steps, name, rides):
    n = len(ws)

    def body(*refs):
        for w_ref, o_ref in zip(refs[:n], refs[n:]):
            o_ref[...] = w_ref[...].astype(BF16)

    me = lambda: 2 * lax.axis_index("x") + lax.axis_index("y")
    return _call(
        body, name=name, grid=(steps,),
        in_specs=[pl.BlockSpec((None, w.shape[1] // steps, w.shape[2]), lambda i: (0, i, 0)) for w in ws],
        out_specs=[pl.BlockSpec((None, w.shape[1] // steps, w.shape[2]), lambda i: (me(), i, 0)) for w in ws],
        out_shape=[jax.ShapeDtypeStruct((N_CHIPS,) + w.shape[1:], BF16) for w in ws], sem=("arbitrary",), rides=rides,
    )(*ws)


def _owner_total(gh, others, *, tm, name):
    _, hr, C = gh.shape

    def body(me_ref, g_ref, o_ref_in, out_ref):
        del me_ref
        acc = g_ref[...]
        for j in range(3):
            acc = acc + o_ref_in[j].astype(F32)
        out_ref[...] = acc

    return pl.pallas_call(
        body, name=name,
        grid_spec=pltpu.PrefetchScalarGridSpec(
            num_scalar_prefetch=1, grid=(hr // tm,),
            in_specs=[pl.BlockSpec((None, tm, C), lambda i, me: (me[0], i, 0)),
                      pl.BlockSpec((3, tm, C), lambda i, me: (0, i, 0))],
            out_specs=pl.BlockSpec((tm, C), lambda i, me: (i, 0))),
        out_shape=jax.ShapeDtypeStruct((hr, C), F32),
        compiler_params=_params(("parallel",)),
    )(_chip_index(), gh, others)


def _adamw_math(w, g, m, v):
    m = ADAM_B1 * m + (1.0 - ADAM_B1) * g
    v = ADAM_B2 * v + (1.0 - ADAM_B2) * (g * g)
    m_hat = m / (1.0 - ADAM_B1 ** ADAM_STEP)
    v_hat = v / (1.0 - ADAM_B2 ** ADAM_STEP)
    delta = -ADAM_LR * (m_hat / (jnp.sqrt(v_hat) + ADAM_EPS) + ADAM_WD * w)
    return delta, m, v


def _adamw_halves(w, own, got, m, v, *, tm, name, rides=()):
    _, R, C = w.shape
    nt = (R // 2) // tm

    n_steps, ring = 2 * nt, 3

    def body(w_hbm, own_ref, got_ref, m_hbm, v_hbm, g_ref, d_ref, nm_ref, nv_ref, wbuf, mbuf, vbuf, sems):
        s = pl.program_id(0) * nt + pl.program_id(1)

        def copies(step, slot):
            rows = pl.ds(pl.multiple_of(step * tm, 8), tm)
            return [pltpu.make_async_copy(src.at[0, rows, :], buf.at[slot], sems.at[k, slot])
                    for k, (src, buf) in enumerate(((w_hbm, wbuf), (m_hbm, mbuf), (v_hbm, vbuf)))]

        @pl.when(s == 0)
        def _():
            for first in range(min(ring - 1, n_steps)):
                for c in copies(first, first):
                    c.start()

        @pl.when(s + ring - 1 < n_steps)
        def _():
            for c in copies(s + ring - 1, (s + ring - 1) % ring):
                c.start()

        slot = s % ring
        for c in copies(s, slot):
            c.wait()
        g = jnp.where(pl.program_id(0) == lax.axis_index("c"), own_ref[...], got_ref[...])
        g_ref[...] = g
        d_ref[...], nm_ref[...], nv_ref[...] = _adamw_math(wbuf[slot], g, mbuf[slot], vbuf[slot])

    whole = pl.BlockSpec((None, tm, C), lambda h, i: (0, h * nt + i, 0))
    half = pl.BlockSpec((tm, C), lambda h, i: (i, 0))
    any_spec = pl.BlockSpec(memory_space=pl.ANY)
    return _call(
        body, name=name, grid=(2, nt), in_specs=[any_spec, half, half, any_spec, any_spec], out_specs=[whole] * 4,
        out_shape=[jax.ShapeDtypeStruct((1, R, C), F32)] * 4, sem=("arbitrary", "arbitrary"), rides=rides,
        scratch_shapes=[pltpu.VMEM((ring, tm, C), F32)] * 3 + [pltpu.SemaphoreType.DMA((3, ring))],
    )(w, own, got, m, v)


def _adamw_small(w, slots, m, v, *, name, rides=()):
    def body(w_ref, slots_ref, m_ref, v_ref, g_ref, d_ref, nm_ref, nv_ref):
        g = slots_ref[0]
        for d in range(1, N_DEV):
            g = g + slots_ref[d]
        g_ref[...] = g
        d_ref[...], nm_ref[...], nv_ref[...] = _adamw_math(w_ref[...], g, m_ref[...], v_ref[...])

    flat = pl.BlockSpec(w.shape, lambda i: (0, 0))
    return _call(
        body, name=name, grid=(1,), in_specs=[flat, pl.BlockSpec(slots.shape, lambda i: (0, 0, 0)), flat, flat],
        out_specs=[flat] * 4, out_shape=[jax.ShapeDtypeStruct(w.shape, F32)] * 4, sem=("arbitrary",), rides=rides,
    )(w, slots, m, v)


SMALL = ["rel_bias_table", "mix_norm_g", "gate_norm_g", "gate_norm_b", "w_spatial", "b_spatial", "attn_sinks",
         "out_norm_a_g", "out_norm_b_g", "ffn_norm_g", "final_norm_g"]
SMALL_A = ["gate_norm_g", "gate_norm_b", "w_spatial", "b_spatial", "out_norm_a_g"]
SMALL_B = ["rel_bias_table", "mix_norm_g", "attn_sinks", "out_norm_b_g", "ffn_norm_g", "final_norm_g"]
LARGE = ["w_in", "w_out", "w_up", "w_down"]
ROW_TILE = {"w_in": 208, "w_out": 256, "w_up": 256, "w_down": 256}
WEIGHTS = ["rel_bias_table", "mix_norm_g", "w_in", "gate_norm_g", "gate_norm_b", "w_spatial", "b_spatial", "attn_sinks",
           "out_norm_a_g", "out_norm_b_g", "w_out", "ffn_norm_g", "w_up", "w_down", "final_norm_g"]
PACK_UNIT = 8 * 128


def _pack(parts):
    rows = []
    for p in parts:
        flat = p.reshape(-1)
        pad = (-flat.shape[0]) % PACK_UNIT
        rows.append(jnp.pad(flat, (0, pad)).reshape(-1, 128))
    return jnp.concatenate(rows, axis=0)


def _unpack(packed, like):
    out, row = [], 0
    for p in like:
        n = math.prod(p.shape)
        nrows = (n + PACK_UNIT - 1) // PACK_UNIT * 8
        out.append(packed[row:row + nrows].reshape(-1)[:n].reshape(p.shape))
        row += nrows
    return out


def kernel(x, rel_bias_table, mix_norm_g, w_in, gate_norm_g, gate_norm_b, w_spatial, b_spatial, attn_sinks, out_norm_a_g, out_norm_b_g, w_out, ffn_norm_g, w_up, w_down, final_norm_g, loss_target, m_rel_bias_table, m_mix_norm_g, m_w_in, m_gate_norm_g, m_gate_norm_b, m_w_spatial, m_b_spatial, m_attn_sinks, m_out_norm_a_g, m_out_norm_b_g, m_w_out, m_ffn_norm_g, m_w_up, m_w_down, m_final_norm_g, v_rel_bias_table, v_mix_norm_g, v_w_in, v_gate_norm_g, v_gate_norm_b, v_w_spatial, v_b_spatial, v_attn_sinks, v_out_norm_a_g, v_out_norm_b_g, v_w_out, v_ffn_norm_g, v_w_up, v_w_down, v_final_norm_g):
    args = dict(locals())
    wts = {n: args[n] for n in WEIGHTS}
    mom = {n: args["m_" + n] for n in WEIGHTS}
    var = {n: args["v_" + n] for n in WEIGHTS}
    sp = {n: wts[n] for n in SMALL}
    x2, tgt = x[0], loss_target[0]
    T = x2.shape[0]
    tm = min(512, T)
    tl = min(1024, T)
    lg = sp["gate_norm_g"].reshape(A_GROUPS, CHUNK)
    lb = sp["gate_norm_b"].reshape(A_GROUPS, CHUNK)
    wsp = sp["w_spatial"].reshape(A_GROUPS, CHUNK, CHUNK)
    bs_col = sp["b_spatial"].reshape(A_GROUPS, CHUNK, 1)
    sinks = sp["attn_sinks"].reshape(1, B_HEADS)
    ga = sp["out_norm_a_g"].reshape(1, A_WIDTH)
    gb = sp["out_norm_b_g"].reshape(1, B_WIDTH)
    g1 = sp["mix_norm_g"].reshape(1, D_MODEL)
    g2 = sp["ffn_norm_g"].reshape(1, D_MODEL)
    gf = sp["final_norm_g"].reshape(1, D_MODEL)

    def owner_total(n, gh, others):
        return _owner_total(gh, others, tm=ROW_TILE[n], name="rs_owner_total_" + n)

    def halves_view(at, shards):
        return at.reshape(shards, 2, at.shape[0] // shards // 2, at.shape[1])

    for d in (wts, mom, var):
        d["w_in"] = jnp.swapaxes(d["w_in"], 1, 2)

    s_in = _cast_into_slot(wts["w_in"], tm=ROW_TILE["w_in"], name="cast_w_in")
    (s_out, s_up, s_down), ((g_in,),) = _cast_into_slots_carrying(
        [wts["w_out"], wts["w_up"], wts["w_down"]], steps=8, name="cast_w_rest",
        rides=[_ride_gather(s_in, chain=(0, 1, 1), chain_fracs=(0.3, 0.6))])
    win_t = g_in.reshape(PROJ_WIDTH, D_MODEL)
    bias = _bias_build(sp["rel_bias_table"])
    (n1, proj), ((g_out,), (s_up,)) = _norm_matmul_wide(
        x2, g1, win_t, tm=tm, tn=PROJ_WIDTH // 2, name="in_proj",
        rides=[_ride_gather(s_out, chain=(0, 1, 1), chain_fracs=(0.65, 0.85)), _ride_gather(s_up, s1=(0, 3, 8))])
    wo = g_out.reshape(A_WIDTH + B_WIDTH, D_MODEL)
    (mixed, mixed_t, ab), ((s_up,), (s_down,), (n1_sib,)) = _mixer_fwd(
        proj, lg, lb, wsp, bs_col, sinks, bias, ga, gb,
        rides=[_ride_gather(s_up, s2=(0, 3, 8), s1=(3, 8, 8)), _ride_gather(s_down, s1=(0, 2, 8)),
               _ride_to_sibling(n1, first=True)])
    mixed_t = halves_view(mixed_t, N_CHIPS)
    h1, ((wu,), (s_down,), (mixed_t_sib,)) = _matmul_res(
        mixed, wo, x2, tm=tl, tn=1024, tk=D_MODEL, prologue=_to_bf16, name="out_proj",
        rides=[_ride_gather(s_up, s3=(0, 3, 8), tail=(3, 8, 8), mid_frac=0.75), _ride_gather(s_down, s2=(0, 2, 8)),
               _ride_to_sibling(mixed_t, halves=True)])
    (n2t, zp, z2, z2t), ((g_down,),) = _norm_matmul_sq(
        h1, g2, wu, tm=tl, tn=1024, name="up_proj", rides=[_ride_gather(s_down, s3=(0, 2, 8), chain=(2, 8, 8), chain_fracs=(0.5, 0.8))])
    wd = g_down.reshape(D_FF, D_MODEL)
    n2t, z2t = halves_view(n2t, 1), halves_view(z2t, N_CHIPS)
    h2, ((n2t_sib,), (z2t_sib,)) = _matmul_res(
        z2, wd, h1, tm=tl, tn=1024, tk=4096, prologue=_to_bf16, name="down_proj",
        rides=[_ride_to_sibling(n2t, halves=True), _ride_to_sibling(z2t, halves=True)])

    dh2, dh2b, dgf, loss = _loss_bwd(h2, tgt, gf, tm=tm)
    dzp, ((dh2b_sib,),) = _matmul_nt(dh2b, wd, tm=tl, tn=1024, tk=D_MODEL, name="bwd_dz", extra=zp,
                                     epilogue=_sq_relu_grad, out_dtype=BF16, rides=[_ride_to_sibling(dh2b)])
    (gd, gdb), ((dzp_sib,),) = _grad_pair(z2t, z2t_sib, dh2b, dh2b_sib, cols_sharded=False, tmo=1024, tk=tl,
                                          name="grad_w_down", rides=[_ride_to_sibling(dzp)])
    (gu, gub), ((o_d,),) = _grad_pair(n2t, n2t_sib, dzp, dzp_sib, cols_sharded=True, tmo=1024, tk=tl,
                                      name="grad_w_up", rides=[_ride_scatter(gdb, None, (0, 7, 8))])
    dn2, ((o_d,), (o_u,)) = _matmul_nt(dzp, wu, tm=tl, tn=1024, tk=4096, name="bwd_dn2",
                                       rides=[_ride_scatter(gdb, o_d, (7, 8, 8)), _ride_scatter(gub, None, (0, 6, 8))])
    h_d = owner_total("w_down", gd, o_d)
    (dh1, dh1b, dg2), ((o_u,),) = _rms_bwd_res(dn2, h1, g2, dh2, tm=tm, name="ffn_norm_bwd",
                                               rides=[_ride_scatter(gub, o_u, (6, 7, 8))])
    dmixed, ((o_u,), (dh1b_sib,), (w_d,)) = _matmul_nt(
        dh1b, wo, tm=tl, tn=1024, tk=D_MODEL, name="bwd_dmixed",
        rides=[_ride_scatter(gub, o_u, (7, 8, 8)), _ride_to_sibling(dh1b), _ride_swap(h_d)])
    h_u = owner_total("w_up", gu, o_u)
    (go, gob), ((w_u,),) = _grad_pair_merged(mixed_t, mixed_t_sib, dh1b, dh1b_sib, tk=tl, name="grad_w_out",
                                             rides=[_ride_swap(h_u)])
    (duv, duv_t, dga, dwsp, dbs, dlg, dlb), ((o_o,),) = _gmlp_bwd(proj, ab, dmixed, ga, lg, lb, wsp, bs_col,
                                                                  rides=[_ride_scatter(gob)])
    h_o = owner_total("w_out", go, o_o)
    small = {"gate_norm_g": dlg, "gate_norm_b": dlb, "w_spatial": dwsp, "b_spatial": dbs, "out_norm_a_g": dga}
    hr_in = PROJ_WIDTH // N_CHIPS // 2
    (dq, dkv, dq_t, dkv_t, dgb, dsinks, dbias), ((slots_a,), (dproj_t_sib,)) = _attn_bwd(
        proj, ab, dmixed, gb, sinks, bias,
        rides=[_ride_small_to_all(_pack([small[n] for n in SMALL_A])), _ride_rows_to_sibling(duv_t, hr_in, 2, N_CHIPS)])
    dproj_t = halves_view(jnp.concatenate([duv_t, dq_t, dkv_t], axis=0), N_CHIPS)
    dtable, ((dproj_t_sib,),) = _bias_grad(
        dbias, rides=[_ride_to_sibling(dproj_t, halves=True, shards=(2, N_CHIPS), land=dproj_t_sib)])
    (gi, gib_near), ((w_o,),) = _grad_pair(
        dproj_t, dproj_t_sib, n1, n1_sib, cols_sharded=False, tmo=hr_in, tk=tl, name="grad_w_in_near", shards="near",
        rides=[_ride_swap(h_o)])
    (gi, gib_far), ((o_i,),) = _grad_pair(
        dproj_t, dproj_t_sib, n1, n1_sib, cols_sharded=False, tmo=hr_in, tk=tl, name="grad_w_in_far", shards="far",
        into=gi, rides=[_ride_scatter(gib_near, None, to=(0, 1))])
    dn1, ((o_i,),) = _matmul_parts([duv, dq, dkv], win_t, tm=tl, tn=1024, name="bwd_dn1",
                                   rides=[_ride_scatter(gib_far, o_i, to=(2,))])
    h_i = owner_total("w_in", gi, o_i)
    dx, dg1 = _rms_bwd_res(dn1, x2, g1, dh1, tm=tm, name="mix_norm_bwd", bf16_copy=False)
    small.update({"rel_bias_table": dtable.reshape(N_BUCKETS, B_HEADS), "mix_norm_g": dg1, "attn_sinks": dsinks,
                  "out_norm_b_g": dgb, "ffn_norm_g": dg2, "final_norm_g": dgf})
    out_g, out_d, out_m, out_v = {}, {}, {}, {}

    def adamw_small(names, slots, tag, rides=()):
        extra = [jnp.zeros((1, 1), F32)] if tag == "b" else []
        like = [wts[n] for n in names] + extra
        res = _adamw_small(_pack(like), slots, _pack([mom[n] for n in names] + extra),
                           _pack([var[n] for n in names] + extra), name="adamw_small_" + tag, rides=rides)
        res, carried = res if rides else (res, None)
        for store, packed in zip((out_g, out_d, out_m, out_v), res):
            for n, val in zip(names + ["loss"], _unpack(packed, like)):
                store[n] = val
        return carried

    (w_i,), (slots_b,) = adamw_small(
        SMALL_A, slots_a, "a", rides=[_ride_swap(h_i), _ride_small_to_all(_pack([small[n] for n in SMALL_B] + [loss]))])
    for n, h, s in zip(LARGE, [h_i, h_o, h_u, h_d], [w_i, w_o, w_u, w_d]):
        res = _adamw_halves(wts[n], h, s, mom[n], var[n], tm=ROW_TILE[n], name="adamw_" + n)
        if n == "w_in":
            res = [jnp.swapaxes(r, 1, 2) for r in res]
        out_g[n], out_d[n], out_m[n], out_v[n] = res
    adamw_small(SMALL_B, slots_b, "b")

    total = out_g["loss"][0, 0]
    return (total, dx[None], *[out_g[n] for n in WEIGHTS], *[out_d[n] for n in WEIGHTS],
            *[out_m[n] for n in WEIGHTS], *[out_v[n] for n in WEIGHTS])
```

```python
import math

import numpy as np
import jax
import jax.numpy as jnp
from jax import lax
from jax.experimental import pallas as pl
from jax.experimental.pallas import tpu as pltpu

F32 = jnp.float32
BF16 = jnp.bfloat16

D_MODEL = 2048
CHUNK = 128
A_GROUPS = 8
A_WIDTH = 1024
HEAD_DIM = 64
B_HEADS = 16
Q_PER_KV = 8
B_WIDTH = 1024
KV_WIDTH = 128
PROJ_WIDTH = 3328
D_FF = 8192
N_BUCKETS = 32
EPS = 1e-5
NEG = -1e30
SCALE = HEAD_DIM ** -0.5
N_CHIPS = 4
N_DEV = 8

ADAM_LR = 0.001
ADAM_B1 = 0.9
ADAM_B2 = 0.999
ADAM_EPS = 1e-08
ADAM_WD = 0.01
ADAM_STEP = 10

VMEM_LIMIT = 60 * 1024 * 1024
MESH = pl.DeviceIdType.MESH


def _bucket_thresholds():
    d = np.arange(CHUNK)
    n_exact = N_BUCKETS // 2
    relf = np.maximum(d, n_exact).astype(np.float64)
    large = n_exact + (np.log(relf / n_exact) / math.log(CHUNK / n_exact) * (N_BUCKETS - n_exact)).astype(np.int32)
    bucket = np.where(d < n_exact, d, np.minimum(large, N_BUCKETS - 1))
    return [int(np.min(d[bucket >= b])) for b in range(1, N_BUCKETS)]


BUCKET_THR = _bucket_thresholds()


def _params(sem=None):
    return pltpu.CompilerParams(dimension_semantics=sem, vmem_limit_bytes=VMEM_LIMIT)


def _gelu(x):
    c = math.sqrt(2.0 / math.pi)
    return 0.5 * x * (1.0 + jnp.tanh(c * (x + 0.044715 * (x * x * x))))


def _gelu_and_grad(x):
    c = math.sqrt(2.0 / math.pi)
    x2 = x * x
    t = jnp.tanh(c * (x + 0.044715 * (x2 * x)))
    g = 0.5 * x * (1.0 + t)
    dg = 0.5 * (1.0 + t) + 0.5 * x * (1.0 - t * t) * (c * (1.0 + 3.0 * 0.044715 * x2))
    return g, dg


def _dot(a, b):
    return jnp.dot(a, b, preferred_element_type=F32)


def _dot_nt(a, b):
    return lax.dot_general(a, b, (((1,), (1,)), ((), ())), preferred_element_type=F32)


def _dot_tn(a, b):
    return lax.dot_general(a, b, (((0,), (0,)), ((), ())), preferred_element_type=F32)


def _rms_bwd(dn, h, g):
    r = lax.rsqrt(jnp.mean(h * h, axis=-1, keepdims=True) + EPS)
    w = dn * g
    dh = r * w - h * ((r * r * r) * jnp.mean(w * h, axis=-1, keepdims=True))
    return dh, r


def _place():
    x, y, c = lax.axis_index("x"), lax.axis_index("y"), lax.axis_index("c")
    chips = [(1 - x, y), (x, 1 - y), (1 - x, 1 - y)]
    return x, y, c, chips


def _remote(src, dst, send_sem, recv_sem, to):
    return pltpu.make_async_remote_copy(src_ref=src, dst_ref=dst, send_sem=send_sem, recv_sem=recv_sem,
                                        device_id=to, device_id_type=MESH)


class _Ride:
    def __init__(self, args, out_shape, n_sem, start, finish, mids=(), aliases=None):
        self.args, self.out_shape, self.n_sem = list(args), list(out_shape), n_sem
        self.start, self.mids, self.finish = start, list(mids), finish
        self.aliases = dict(aliases or {})


def _call(body, *, name, grid, in_specs, out_specs, out_shape, scratch_shapes=(), sem=None, rides=(), aliases=None):
    single = not isinstance(out_shape, (list, tuple))
    out_specs = [out_specs] if single else list(out_specs)
    out_shape = [out_shape] if single else list(out_shape)
    n_in, n_out, n_scr = len(in_specs), len(out_shape), len(scratch_shapes)
    r_in = [len(r.args) for r in rides]
    r_out = [len(r.out_shape) for r in rides]
    any_spec = pl.BlockSpec(memory_space=pl.ANY)
    aliases, off_i, off_o = dict(aliases or {}), n_in, n_out
    for r in rides:
        for i, o in r.aliases.items():
            aliases[off_i + i] = off_o + o
        off_i += len(r.args)
        off_o += len(r.out_shape)
    steps = math.prod(grid)

    def wrapped(*refs):
        p = 0
        ins = refs[p:p + n_in]; p += n_in
        rins = refs[p:p + sum(r_in)]; p += sum(r_in)
        outs = refs[p:p + n_out]; p += n_out
        routs = refs[p:p + sum(r_out)]; p += sum(r_out)
        scr = refs[p:p + n_scr]; p += n_scr
        sems = refs[p:]
        parts, pi, po = [], 0, 0
        for k, r in enumerate(rides):
            parts.append((rins[pi:pi + r_in[k]], routs[po:po + r_out[k]], sems[2 * k], sems[2 * k + 1]))
            pi += r_in[k]
            po += r_out[k]
        lin = 0
        for d in range(len(grid)):
            lin = lin * grid[d] + pl.program_id(d)
        if rides:
            @pl.when(lin == 0)
            def _():
                for r, part in zip(rides, parts):
                    r.start(*part)
        body(*ins, *outs, *scr)
        for r, part in zip(rides, parts):
            for frac, fn in r.mids:
                @pl.when(lin == min(steps - 1, int(frac * steps)))
                def _(fn=fn, part=part):
                    fn(*part)
        if rides:
            @pl.when(lin == steps - 1)
            def _():
                for r, part in zip(rides, parts):
                    r.finish(*part)

    scratch = list(scratch_shapes)
    for r in rides:
        scratch += [pltpu.SemaphoreType.DMA((r.n_sem,)), pltpu.SemaphoreType.DMA((r.n_sem,))]
    if rides:
        sem = ("arbitrary",) * len(grid)
    res = pl.pallas_call(
        wrapped, name=name, grid=grid,
        in_specs=list(in_specs) + [any_spec] * sum(r_in),
        out_specs=out_specs + [any_spec] * sum(r_out),
        out_shape=out_shape + [s for r in rides for s in r.out_shape],
        scratch_shapes=scratch, input_output_aliases=aliases,
        compiler_params=_params(sem),
    )

    def run(*args):
        got = res(*args, *[a for r in rides for a in r.args])
        mine = got[0] if single else list(got[:n_out])
        if not rides:
            return mine
        rest, out = list(got[n_out:]), []
        for k in range(len(rides)):
            out.append(rest[:r_out[k]])
            rest = rest[r_out[k]:]
        return mine, out

    return run


def _ride_gather(slot, s1=None, s2=None, s3=None, tail=None, chain=None, mid_frac=0.6, chain_fracs=(0.35, 0.7)):
    half = slot.shape[1] // 2

    def rows(part, c, which=None):
        k0, k1, n = part
        count, first = (k1 - k0) * (half // n), c * half + k0 * (half // n)
        return pl.ds(first, count) if which is None else pl.ds(first + which * (count // 2), count // 2)

    def ids():
        x, y, c, _ = _place()
        return x, y, c, 2 * x + y, 2 * (1 - x) + y, 2 * x + (1 - y), 2 * (1 - x) + (1 - y)

    def copy(full, chip, r, ss, rs, k, to):
        piece = full.at[chip, r, :]
        return _remote(piece, piece, ss.at[k], rs.at[k], to)

    def to_neighbours(full, ss, rs, part, base):
        x, y, c, me, _, _, _ = ids()
        return [copy(full, me, rows(part, c), ss, rs, base, (1 - x, y, c)),
                copy(full, me, rows(part, c), ss, rs, base + 1, (x, 1 - y, c))]

    def from_neighbours(full, ss, rs, part, base):
        x, y, c, _, cx, cy, _ = ids()
        return [copy(full, cx, rows(part, c), ss, rs, base, (x, y, c)), copy(full, cy, rows(part, c), ss, rs, base + 1, (x, y, c))]

    def onward(full, ss, rs, part, base):
        x, y, c, _, cx, cy, _ = ids()
        return [copy(full, cx, rows(part, c, 0), ss, rs, base, (x, 1 - y, c)),
                copy(full, cy, rows(part, c, 1), ss, rs, base + 1, (1 - x, y, c))]

    def from_onward(full, ss, rs, part, base):
        x, y, c, _, _, _, cd = ids()
        return [copy(full, cd, rows(part, c, 0), ss, rs, base, (x, y, c)), copy(full, cd, rows(part, c, 1), ss, rs, base + 1, (x, y, c))]

    def to_sibling(full, ss, rs, part, base, diagonal):
        x, y, c, _, cx, cy, cd = ids()
        return [copy(full, chip, rows(part, c), ss, rs, base + j, (x, y, 1 - c))
                for j, chip in enumerate([cd] if diagonal else [cx, cy])]

    def from_sibling(full, ss, rs, part, base, diagonal):
        x, y, c, _, cx, cy, cd = ids()
        return [copy(full, chip, rows(part, 1 - c), ss, rs, base + j, (x, y, c))
                for j, chip in enumerate([cd] if diagonal else [cx, cy])]

    def start(ins, outs, ss, rs):
        full, cps = outs[0], []
        for part, base in ((s1, 0), (chain, 12)):
            if part is not None:
                cps += to_neighbours(full, ss, rs, part, base)
        for part, b_ici, b_sib in ((s2, 2, 4), (tail, 7, 9)):
            if part is not None:
                cps += onward(full, ss, rs, part, b_ici) + to_sibling(full, ss, rs, part, b_sib, False)
        if s3 is not None:
            cps += to_sibling(full, ss, rs, s3, 6, True)
        for cp in cps:
            cp.start()

    def second(part, b_in, b_ici, b_sib):
        def fn(ins, outs, ss, rs):
            for cp in from_neighbours(outs[0], ss, rs, part, b_in):
                cp.wait_recv()
            for cp in onward(outs[0], ss, rs, part, b_ici) + to_sibling(outs[0], ss, rs, part, b_sib, False):
                cp.start()
        return fn

    def third(part, b_ici, b_sib):
        def fn(ins, outs, ss, rs):
            for cp in from_onward(outs[0], ss, rs, part, b_ici):
                cp.wait_recv()
            for cp in to_sibling(outs[0], ss, rs, part, b_sib, True):
                cp.start()
        return fn

    mids = []
    if tail is not None:
        mids.append((mid_frac, third(tail, 7, 11)))
    if chain is not None:
        mids += [(chain_fracs[0], second(chain, 12, 14, 16)), (chain_fracs[1], third(chain, 14, 18))]

    def finish(ins, outs, ss, rs):
        full, got, sent = outs[0], [], []
        if s1 is not None:
            got += from_neighbours(full, ss, rs, s1, 0)
            sent += to_neighbours(full, ss, rs, s1, 0)
        if s2 is not None:
            got += from_onward(full, ss, rs, s2, 2) + from_sibling(full, ss, rs, s2, 4, False)
            sent += onward(full, ss, rs, s2, 2) + to_sibling(full, ss, rs, s2, 4, False)
        if s3 is not None:
            got += from_sibling(full, ss, rs, s3, 6, True)
            sent += to_sibling(full, ss, rs, s3, 6, True)
        if tail is not None:
            got += from_sibling(full, ss, rs, tail, 9, False) + from_sibling(full, ss, rs, tail, 11, True)
            sent += onward(full, ss, rs, tail, 7) + to_sibling(full, ss, rs, tail, 9, False) + to_sibling(full, ss, rs, tail, 11, True)
        if chain is not None:
            got += from_sibling(full, ss, rs, chain, 16, False) + from_sibling(full, ss, rs, chain, 18, True)
            sent += (to_neighbours(full, ss, rs, chain, 12) + onward(full, ss, rs, chain, 14)
                     + to_sibling(full, ss, rs, chain, 16, False) + to_sibling(full, ss, rs, chain, 18, True))
        for cp in got:
            cp.wait_recv()
        for cp in sent:
            cp.wait_send()

    return _Ride([slot], [jax.ShapeDtypeStruct(slot.shape, slot.dtype)], 19, start, finish, mids=mids, aliases={0: 0})


def _ride_scatter(q, land=None, part=(0, 1), to=(0, 1, 2)):
    k0, k1, n = part if len(part) == 3 else (part[0], part[0] + 1, part[1])
    rows_n = q.shape[1] // n
    rows = pl.ds(k0 * rows_n, (k1 - k0) * rows_n)

    def copies(ins, outs, ss, rs):
        x, y, c, chips = _place()
        return [_remote(ins[0].at[2 * chip[0] + chip[1], rows, :], outs[0].at[j, rows, :], ss.at[j], rs.at[j], (*chip, c))
                for j, chip in enumerate(chips) if j in to]

    def start(*a):
        for cp in copies(*a):
            cp.start()

    def finish(*a):
        for cp in copies(*a):
            cp.wait()

    shape = jax.ShapeDtypeStruct((3,) + q.shape[1:], q.dtype)
    if land is None:
        return _Ride([q], [shape], 3, start, finish)
    return _Ride([q, land], [shape], 3, start, finish, aliases={1: 0})


def _ride_to_sibling(a, halves=False, first=False, shards=None, land=None):
    s0, s1 = shards or (0, a.shape[0])

    def copy(ins, outs, ss, rs):
        x, y, c, _ = _place()
        if halves:
            src, dst = ins[0].at[s0:s1, 1 - c], outs[0].at[s0:s1]
        else:
            src, dst = (ins[0].at[0] if first else ins[0]), outs[0]
        return _remote(src, dst, ss.at[0], rs.at[0], (x, y, 1 - c))

    shape = (a.shape[0],) + a.shape[2:] if halves else (a.shape[1:] if first else a.shape)
    return _Ride([a] if land is None else [a, land], [jax.ShapeDtypeStruct(shape, a.dtype)], 1,
                 lambda *a_: copy(*a_).start(), lambda *a_: copy(*a_).wait(), aliases=None if land is None else {1: 0})


def _ride_rows_to_sibling(a, hr, shards, total):
    def copies(ins, outs, ss, rs):
        x, y, c, _ = _place()
        return [_remote(ins[0].at[pl.ds((2 * s + 1 - c) * hr, hr), :], outs[0].at[s], ss.at[s], rs.at[s], (x, y, 1 - c))
                for s in range(shards)]

    def start(*a_):
        for cp in copies(*a_):
            cp.start()

    def finish(*a_):
        for cp in copies(*a_):
            cp.wait()

    return _Ride([a], [jax.ShapeDtypeStruct((total, hr, a.shape[1]), a.dtype)], shards, start, finish)


def _ride_swap(h):
    def copy(ins, outs, ss, rs):
        x, y, c, _ = _place()
        return _remote(ins[0], outs[0], ss.at[0], rs.at[0], (x, y, 1 - c))

    return _Ride([h], [jax.ShapeDtypeStruct(h.shape, h.dtype)], 1,
                 lambda *a: copy(*a).start(), lambda *a: copy(*a).wait())


def _mesh_place(p):
    return (p // 4, (p // 2) % 2, p % 2)


def _ride_small_to_all(packed):
    def copies(ins, outs, ss, rs):
        x, y, c, _ = _place()
        me = 4 * x + 2 * y + c
        return [_remote(ins[0], outs[0].at[me], ss.at[k - 1], rs.at[k - 1], _mesh_place((me + k) % N_DEV))
                for k in range(1, N_DEV)]

    def own(ins, outs, ss, rs):
        x, y, c, _ = _place()
        return pltpu.make_async_copy(ins[0], outs[0].at[4 * x + 2 * y + c], ss.at[N_DEV - 1])

    def start(*a):
        own(*a).start()
        for cp in copies(*a):
            cp.start()

    def finish(ins, outs, ss, rs):
        x, y, c, _ = _place()
        me = 4 * x + 2 * y + c
        for k in range(1, N_DEV):
            _remote(ins[0], outs[0].at[(me + N_DEV - k) % N_DEV], ss.at[k - 1], rs.at[k - 1], (x, y, c)).wait_recv()
        for cp in copies(ins, outs, ss, rs):
            cp.wait_send()
        own(ins, outs, ss, rs).wait()

    return _Ride([packed], [jax.ShapeDtypeStruct((N_DEV,) + packed.shape, packed.dtype)], N_DEV, start, finish)


def _norm_bf16(a_ref, g_ref):
    xf = a_ref[...]
    r = lax.rsqrt(jnp.mean(xf * xf, axis=-1, keepdims=True) + EPS)
    return ((xf * r) * g_ref[...]).astype(BF16)


def _norm_matmul_wide(a, g, b, *, tm, tn, name, rides=()):
    T, K = a.shape
    N = b.shape[0]

    def body(a_ref, g_ref, b_ref, n_ref, o_ref):
        n = _norm_bf16(a_ref, g_ref)
        n_ref[...] = n
        o_ref[...] = _dot_nt(n, b_ref[...])

    return _call(
        body, name=name, grid=(N // tn, T // tm),
        in_specs=[pl.BlockSpec((tm, K), lambda j, i: (i, 0)), pl.BlockSpec((1, K), lambda j, i: (0, 0)),
                  pl.BlockSpec((tn, K), lambda j, i: (j, 0))],
        out_specs=[pl.BlockSpec((None, tm, K), lambda j, i: (j, i, 0)), pl.BlockSpec((tm, tn), lambda j, i: (i, j))],
        out_shape=[jax.ShapeDtypeStruct((N // tn, T, K), BF16), jax.ShapeDtypeStruct((T, N), F32)],
        sem=("arbitrary", "arbitrary"), rides=rides,
    )(a, g, b)


def _norm_matmul_sq(a, g, b, *, tm, tn, name, rides=()):
    T, K = a.shape
    per = b.shape[2] // tn
    N = b.shape[0] * b.shape[2]

    def body(a_ref, g_ref, b_ref, nt_ref, o_ref, z_ref, zt_ref, n_scr):
        @pl.when(pl.program_id(1) == 0)
        def _():
            n = _norm_bf16(a_ref, g_ref)
            n_scr[...] = n
            nt_ref[...] = n.T
        r = jnp.maximum(_dot(n_scr[...], b_ref[...]), 0.0)
        o_ref[...] = r.astype(BF16)
        z = (r * r).astype(BF16)
        z_ref[...] = z
        zt_ref[...] = z.T

    return _call(
        body, name=name, grid=(T // tm, N // tn),
        in_specs=[pl.BlockSpec((tm, K), lambda i, j: (i, 0)), pl.BlockSpec((1, K), lambda i, j: (0, 0)),
                  pl.BlockSpec((None, K, tn), lambda i, j: (j // per, 0, j % per))],
        out_specs=[pl.BlockSpec((K, tm), lambda i, j: (0, i)), pl.BlockSpec((tm, tn), lambda i, j: (i, j)),
                   pl.BlockSpec((tm, tn), lambda i, j: (i, j)), pl.BlockSpec((tn, tm), lambda i, j: (j, i))],
        out_shape=[jax.ShapeDtypeStruct((K, T), BF16), jax.ShapeDtypeStruct((T, N), BF16),
                   jax.ShapeDtypeStruct((T, N), BF16), jax.ShapeDtypeStruct((N, T), BF16)],
        scratch_shapes=[pltpu.VMEM((tm, K), BF16)],
        sem=("parallel", "arbitrary"), rides=rides,
    )(a, g, b)


def _grad_pair(at, at_sib, b, b_sib, *, cols_sharded, tmo, tk, name, shards=None, into=None, rides=()):
    S, _, hr, T = at.shape
    C = b.shape[-1] // N_CHIPS if cols_sharded else b.shape[-1]
    nk = T // tk

    def shard(s):
        if shards is None:
            return s
        x, y = lax.axis_index("x"), lax.axis_index("y")
        first, second = ((2 * (1 - x) + y, 2 * x + (1 - y)) if shards == "near" else (2 * (1 - x) + (1 - y), 2 * x + y))
        return jnp.where(s == 0, first, second)

    a_sel = (lambda s: 0) if cols_sharded else shard
    b_sel = shard if cols_sharded else (lambda s: 0)
    if b.ndim == 3:
        b_spec = pl.BlockSpec((None, tk, C), lambda s, i, k: (0, k, b_sel(s)))
    else:
        b_spec = pl.BlockSpec((tk, C), lambda s, i, k: (k, b_sel(s)))
    n_into = 0 if into is None else 1

    def body(a_ref, as_ref, b_ref, bs_ref, *rest):
        o_ref, ob_ref = rest[n_into:]
        k = pl.program_id(2)
        p = _dot(a_ref[...], b_ref[...]) + _dot(as_ref[...], bs_ref[...])

        @pl.when(k == 0)
        def _():
            o_ref[...] = p

        @pl.when(k > 0)
        def _():
            o_ref[...] += p

        @pl.when(k == nk - 1)
        def _():
            ob_ref[...] = o_ref[...].astype(BF16)

    out = pl.BlockSpec((None, tmo, C), lambda s, i, k: (shard(s), i, 0))
    held = [pl.BlockSpec(memory_space=pl.ANY)] * n_into
    return _call(
        body, name=name, grid=(N_CHIPS if shards is None else 2, hr // tmo, nk),
        in_specs=[pl.BlockSpec((None, None, tmo, tk), lambda s, i, k: (a_sel(s), lax.axis_index("c"), i, k)),
                  pl.BlockSpec((None, tmo, tk), lambda s, i, k: (a_sel(s), i, k)),
                  b_spec, pl.BlockSpec((tk, C), lambda s, i, k: (k, b_sel(s)))] + held,
        out_specs=[out, out],
        out_shape=[jax.ShapeDtypeStruct((N_CHIPS, hr, C), F32), jax.ShapeDtypeStruct((N_CHIPS, hr, C), BF16)],
        sem=("parallel", "parallel", "arbitrary"), rides=rides, aliases={4: 0} if into is not None else None,
    )(at, at_sib, b, b_sib, *([into] if into is not None else []))


def _grad_pair_merged(at, at_sib, b, b_sib, *, tk, name, rides=()):
    S, _, hr, T = at.shape
    C = b.shape[-1]
    nk = T // tk

    def body(a_ref, as_ref, b_ref, bs_ref, o_ref, ob_ref):
        k = pl.program_id(0)
        p = (_dot(a_ref[...].reshape(S * hr, tk), b_ref[...])
             + _dot(as_ref[...].reshape(S * hr, tk), bs_ref[...])).reshape(S, hr, C)

        @pl.when(k == 0)
        def _():
            o_ref[...] = p

        @pl.when(k > 0)
        def _():
            o_ref[...] += p

        @pl.when(k == nk - 1)
        def _():
            ob_ref[...] = o_ref[...].astype(BF16)

    out = pl.BlockSpec((S, hr, C), lambda k: (0, 0, 0))
    return _call(
        body, name=name, grid=(nk,),
        in_specs=[pl.BlockSpec((S, None, hr, tk), lambda k: (0, lax.axis_index("c"), 0, k)),
                  pl.BlockSpec((S, hr, tk), lambda k: (0, 0, k)),
                  pl.BlockSpec((tk, C), lambda k: (k, 0)), pl.BlockSpec((tk, C), lambda k: (k, 0))],
        out_specs=[out, out],
        out_shape=[jax.ShapeDtypeStruct((S, hr, C), F32), jax.ShapeDtypeStruct((S, hr, C), BF16)],
        sem=("arbitrary",), rides=rides,
    )(at, at_sib, b, b_sib)


def _matmul_parts(parts, b, *, tm, tn, name, rides=()):
    T = parts[0].shape[0]
    N = b.shape[1]
    offs = [sum(p.shape[1] for p in parts[:i]) for i in range(len(parts))]
    assert all(o % p.shape[1] == 0 for o, p in zip(offs, parts))

    def body(*refs):
        n = len(parts)
        acc = _dot(refs[0][...], refs[n][...])
        for i in range(1, n):
            acc = acc + _dot(refs[i][...], refs[n + i][...])
        refs[-1][...] = acc

    a_specs = [pl.BlockSpec((tm, p.shape[1]), lambda i, j: (i, 0)) for p in parts]
    b_specs = [pl.BlockSpec((p.shape[1], tn), lambda i, j, r=o // p.shape[1]: (r, j)) for o, p in zip(offs, parts)]
    return _call(
        body, name=name, grid=(T // tm, N // tn), in_specs=a_specs + b_specs,
        out_specs=pl.BlockSpec((tm, tn), lambda i, j: (i, j)), out_shape=jax.ShapeDtypeStruct((T, N), F32),
        sem=("parallel", "parallel"), rides=rides,
    )(*parts, *([b] * len(parts)))


def _to_bf16(v):
    return v.astype(BF16)


def _matmul_res(a, b, res, *, tm, tn, tk, prologue, name, rides=()):
    T, K = a.shape
    N = b.shape[1]

    def body(a_ref, b_ref, res_ref, o_ref):
        k = pl.program_id(2)
        p = _dot(prologue(a_ref[...]), b_ref[...])

        @pl.when(k == 0)
        def _():
            o_ref[...] = res_ref[...] + p

        @pl.when(k > 0)
        def _():
            o_ref[...] += p

    return _call(
        body, name=name, grid=(T // tm, N // tn, K // tk),
        in_specs=[pl.BlockSpec((tm, tk), lambda i, j, k: (i, k)), pl.BlockSpec((tk, tn), lambda i, j, k: (k, j)),
                  pl.BlockSpec((tm, tn), lambda i, j, k: (i, j))],
        out_specs=pl.BlockSpec((tm, tn), lambda i, j, k: (i, j)),
        out_shape=jax.ShapeDtypeStruct((T, N), F32),
        sem=("parallel", "parallel", "arbitrary"), rides=rides,
    )(a, b, res)


def _matmul_nt(a, b, *, tm, tn, tk, name, extra=None, epilogue=None, out_dtype=F32, rides=()):
    T, K = a.shape
    two = b.ndim == 3 and tk == 2 * b.shape[2]
    if two:
        N, ks = b.shape[1], b.shape[2]
        b_specs = [pl.BlockSpec((None, tn, ks), lambda i, j, k: (2 * k, j, 0)),
                   pl.BlockSpec((None, tn, ks), lambda i, j, k: (2 * k + 1, j, 0))]
    elif b.ndim == 3:
        per = b.shape[2] // tk
        N = b.shape[1]
        b_specs = [pl.BlockSpec((None, tn, tk), lambda i, j, k: (k // per, j, k % per))]
    else:
        N = b.shape[0]
        b_specs = [pl.BlockSpec((tn, tk), lambda i, j, k: (j, k))]
    nb = len(b_specs)
    nk = K // tk
    assert out_dtype == F32 or nk == 1
    in_specs = [pl.BlockSpec((tm, tk), lambda i, j, k: (i, k))] + b_specs
    args = [a] + [b] * nb
    if extra is not None:
        in_specs.append(pl.BlockSpec((tm, tn), lambda i, j, k: (i, j)))
        args.append(extra)

    def body(*refs):
        a_ref, b_ref = refs[0], refs[1]
        o_ref = refs[-1]
        if two:
            p = (_dot_nt(a_ref[:, :tk // 2].astype(BF16), refs[1][...])
                 + _dot_nt(a_ref[:, tk // 2:].astype(BF16), refs[2][...]))
        else:
            p = _dot_nt(a_ref[...].astype(BF16), b_ref[...])
        if nk == 1:
            if epilogue is not None:
                p = epilogue(p, refs[1 + nb][...])
            o_ref[...] = p.astype(out_dtype)
        else:
            k = pl.program_id(2)

            @pl.when(k == 0)
            def _():
                o_ref[...] = p

            @pl.when(k > 0)
            def _():
                o_ref[...] += p

    return _call(
        body, name=name, grid=(T // tm, N // tn, nk),
        in_specs=in_specs,
        out_specs=pl.BlockSpec((tm, tn), lambda i, j, k: (i, j)),
        out_shape=jax.ShapeDtypeStruct((T, N), out_dtype),
        sem=("parallel", "parallel", "arbitrary"), rides=rides,
    )(*args)


def _loss_bwd(h2, tgt, g, *, tm):
    T, D = h2.shape

    def body(h_hbm, t_hbm, g_ref, dh_ref, dhb_ref, dg_ref, loss_ref, hbuf, tbuf, sems):
        slot = _ring_fetch((h_hbm, t_hbm), (hbuf, tbuf), sems, pl.program_id(0), T // tm, tm)

        @pl.when(pl.program_id(0) == 0)
        def _():
            dg_ref[...] = jnp.zeros_like(dg_ref)
            loss_ref[...] = jnp.zeros_like(loss_ref)
        h = hbuf[slot]
        gg = g_ref[...]
        r = lax.rsqrt(jnp.mean(h * h, axis=-1, keepdims=True) + EPS)
        hn = h * r
        err = hn * gg - tbuf[slot]
        loss_ref[...] += 0.5 * jnp.sum(jnp.mean(err * err, axis=-1, keepdims=True), axis=0, keepdims=True)
        dy = err * (1.0 / D)
        dg_ref[...] += jnp.sum(dy * hn, axis=0, keepdims=True)
        w = dy * gg
        dh = r * w - h * ((r * r * r) * jnp.mean(w * h, axis=-1, keepdims=True))
        dh_ref[...] = dh
        dhb_ref[...] = dh.astype(BF16)

    tile = pl.BlockSpec((tm, D), lambda i: (i, 0))
    any_spec = pl.BlockSpec(memory_space=pl.ANY)
    return pl.pallas_call(
        body, name="loss_bwd", grid=(T // tm,),
        in_specs=[any_spec, any_spec, pl.BlockSpec((1, D), lambda i: (0, 0))],
        out_specs=[tile, tile, pl.BlockSpec((1, D), lambda i: (0, 0)), pl.BlockSpec((1, 1), lambda i: (0, 0))],
        out_shape=[jax.ShapeDtypeStruct((T, D), F32), jax.ShapeDtypeStruct((T, D), BF16),
                   jax.ShapeDtypeStruct((1, D), F32), jax.ShapeDtypeStruct((1, 1), F32)],
        scratch_shapes=[pltpu.VMEM((RING, tm, D), F32)] * 2 + [pltpu.SemaphoreType.DMA((2, RING))],
        compiler_params=_params(("arbitrary",)),
    )(h2, tgt, g)


RING = 3


def _ring_fetch(srcs, bufs, sems, step, n_steps, tm):
    def copies(s, slot):
        rows = pl.ds(pl.multiple_of(s * tm, 8), tm)
        return [pltpu.make_async_copy(src.at[rows, :], buf.at[slot], sems.at[k, slot])
                for k, (src, buf) in enumerate(zip(srcs, bufs))]

    @pl.when(step == 0)
    def _():
        for first in range(min(RING - 1, n_steps)):
            for c in copies(first, first):
                c.start()

    @pl.when(step + RING - 1 < n_steps)
    def _():
        for c in copies(step + RING - 1, (step + RING - 1) % RING):
            c.start()

    slot = step % RING
    for c in copies(step, slot):
        c.wait()
    return slot


def _rms_bwd_res(dn, h, g, dres, *, tm, name, bf16_copy=True, rides=()):
    T, D = h.shape

    def body(dn_hbm, h_hbm, g_ref, dres_hbm, dh_ref, *rest):
        dnbuf, hbuf, rbuf, sems = rest[-4:]
        dg_ref = rest[-5]
        slot = _ring_fetch((dn_hbm, h_hbm, dres_hbm), (dnbuf, hbuf, rbuf), sems, pl.program_id(0), T // tm, tm)

        @pl.when(pl.program_id(0) == 0)
        def _():
            dg_ref[...] = jnp.zeros_like(dg_ref)
        h_ = hbuf[slot]
        dn_ = dnbuf[slot]
        dh, r = _rms_bwd(dn_, h_, g_ref[...])
        dg_ref[...] += jnp.sum(dn_ * (h_ * r), axis=0, keepdims=True)
        dh = rbuf[slot] + dh
        dh_ref[...] = dh
        if bf16_copy:
            rest[0][...] = dh.astype(BF16)

    tile = pl.BlockSpec((tm, D), lambda i: (i, 0))
    row = pl.BlockSpec((1, D), lambda i: (0, 0))
    any_spec = pl.BlockSpec(memory_space=pl.ANY)
    copy_spec = [tile] if bf16_copy else []
    copy_shape = [jax.ShapeDtypeStruct((T, D), BF16)] if bf16_copy else []
    return _call(
        body, name=name, grid=(T // tm,),
        in_specs=[any_spec, any_spec, row, any_spec], out_specs=[tile] + copy_spec + [row],
        out_shape=[jax.ShapeDtypeStruct((T, D), F32)] + copy_shape + [jax.ShapeDtypeStruct((1, D), F32)],
        scratch_shapes=[pltpu.VMEM((RING, tm, D), F32)] * 3 + [pltpu.SemaphoreType.DMA((3, RING))],
        sem=("arbitrary",), rides=rides,
    )(dn, h, g, dres)


def _rel_distance():
    i = lax.broadcasted_iota(jnp.int32, (CHUNK, 2 * CHUNK), 0)
    j = lax.broadcasted_iota(jnp.int32, (CHUNK, 2 * CHUNK), 1)
    return i + CHUNK - j


def _bias_build(table):
    def body(tab_ref, o_ref):
        rel = _rel_distance()
        j = lax.broadcasted_iota(jnp.int32, (CHUNK, 2 * CHUNK), 1)
        band = (rel >= 0) & (rel < CHUNK)
        ge = [rel >= t for t in BUCKET_THR]
        for h in range(B_HEADS):
            cur = jnp.full((CHUNK, 2 * CHUNK), tab_ref[0, h], F32)
            for b in range(1, N_BUCKETS):
                cur = jnp.where(ge[b - 1], tab_ref[b, h], cur)
            o_ref[0, h] = jnp.where(band & (j >= CHUNK), cur, NEG)
            o_ref[1, h] = jnp.where(band, cur, NEG)

    return pl.pallas_call(
        body, name="bias_build",
        in_specs=[pl.BlockSpec(memory_space=pltpu.SMEM)],
        out_specs=pl.BlockSpec(memory_space=pltpu.VMEM),
        out_shape=jax.ShapeDtypeStruct((2, B_HEADS, CHUNK, 2 * CHUNK), F32),
    )(table)


def _bias_grad(dbias, rides=()):
    def body(db_ref, o_ref, acc_ref):
        rel = _rel_distance()
        lo = [0] + BUCKET_THR
        hi = BUCKET_THR + [CHUNK]
        for b in range(N_BUCKETS):
            m = (rel >= lo[b]) & (rel < hi[b])
            for h in range(B_HEADS):
                row = b * B_HEADS + h
                acc_ref[row:row + 1, :] = jnp.sum(jnp.where(m, db_ref[h], 0.0), axis=0, keepdims=True)
        o_ref[...] = jnp.sum(acc_ref[...], axis=1, keepdims=True)

    return _call(
        body, name="bias_grad", grid=(1,),
        in_specs=[pl.BlockSpec(dbias.shape, lambda i: (0, 0, 0))],
        out_specs=pl.BlockSpec((N_BUCKETS * B_HEADS, 1), lambda i: (0, 0)),
        out_shape=jax.ShapeDtypeStruct((N_BUCKETS * B_HEADS, 1), F32),
        scratch_shapes=[pltpu.VMEM((N_BUCKETS * B_HEADS, 2 * CHUNK), F32)],
        sem=("arbitrary",), rides=rides,
    )(dbias)


def _causal_mask():
    t = lax.broadcasted_iota(jnp.int32, (CHUNK, CHUNK), 0)
    s = lax.broadcasted_iota(jnp.int32, (CHUNK, CHUNK), 1)
    return s <= t


def _gate_forward(u, v, lg, lb, wc, bs):
    ug = _gelu(u)
    vg = _gelu(v)
    mu = jnp.mean(vg, axis=-1, keepdims=True)
    xc = vg - mu
    rstd = lax.rsqrt(jnp.mean(xc * xc, axis=-1, keepdims=True) + EPS)
    xhat = xc * rstd
    vl = (xhat * lg + lb).astype(BF16)
    mixed = _dot(wc, vl) + bs
    return ug, xhat, rstd, vl, mixed


def _softmax_scores(qk, bias, sink):
    s = qk + bias
    m = jnp.maximum(jnp.max(s, axis=-1, keepdims=True), sink)
    p = jnp.exp(s - m)
    e_sink = jnp.exp(sink - m)
    inv = 1.0 / (jnp.sum(p, axis=-1, keepdims=True) + e_sink)
    return p * inv, e_sink * inv


PAIRS = Q_PER_KV // 2


def _head(g, pr, e):
    return g * Q_PER_KV + 2 * pr + e


def _stack_pairs(ref, g, col0=0):
    w = 2 * HEAD_DIM
    return jnp.concatenate([ref[:, col0 + (g * PAIRS + pr) * w:col0 + (g * PAIRS + pr + 1) * w] for pr in range(PAIRS)],
                           axis=0)


def _low_lanes():
    return lax.broadcasted_iota(jnp.int32, (2 * CHUNK, 2 * HEAD_DIM), 1) < HEAD_DIM


def _band_operands(kv_prev, kv_cur):
    band = jnp.concatenate([kv_prev, kv_cur], axis=0)
    low = _low_lanes()
    ops = []
    for cat in (band[:, :KV_WIDTH], band[:, KV_WIDTH:]):
        rol = pltpu.roll(cat, HEAD_DIM, 1)
        ops.append([[jnp.where(low if e == 0 else ~low, cat if g == e else rol, 0.0).astype(BF16) for e in range(2)]
                    for g in range(2)])
    return ops


def _mixer_fwd(proj, lg, lb, wsp, bs_col, sinks, bias, ga, gb, rides=()):
    T = proj.shape[0]
    nb = T // CHUNK

    def body(u_ref, v_ref, q_ref, kvc_ref, kvp_ref, lg_ref, lb_ref, w_ref, bs_ref, sink_ref, bias_ref,
             ga_ref, gb_ref, mixed_ref, mixed_t_ref, ab_ref):
        causal = _causal_mask()
        ssq = jnp.zeros((CHUNK, 1), F32)
        for g in range(A_GROUPS):
            cols = slice(g * CHUNK, (g + 1) * CHUNK)
            wc = jnp.where(causal, w_ref[g], 0.0).astype(BF16)
            ug, _, _, _, mixed = _gate_forward(u_ref[:, cols], v_ref[:, cols], lg_ref[g:g + 1, :], lb_ref[g:g + 1, :],
                                               wc, bs_ref[g])
            a = ug * mixed
            ab_ref[:, cols] = a
            ssq = ssq + jnp.sum(a * a, axis=-1, keepdims=True)
        ra = lax.rsqrt(ssq * (1.0 / A_WIDTH) + EPS)
        mixed_ref[:, :A_WIDTH] = ((ab_ref[:, :A_WIDTH] * ra) * ga_ref[...]).astype(BF16)

        kops, vops = _band_operands(kvp_ref[...], kvc_ref[...])
        ssq = jnp.zeros((CHUNK, 1), F32)
        for g in range(B_HEADS // Q_PER_KV):
            qst = (_stack_pairs(q_ref, g) * SCALE).astype(BF16)
            o_st = jnp.zeros((PAIRS * CHUNK, 2 * HEAD_DIM), F32)
            for e in range(2):
                s_all = _dot_nt(qst, kops[g][e])
                ps = []
                for pr in range(PAIRS):
                    h = _head(g, pr, e)
                    p, _ = _softmax_scores(s_all[pr * CHUNK:(pr + 1) * CHUNK], bias_ref[h], sink_ref[0, h])
                    ps.append(p.astype(BF16))
                o_st = o_st + _dot(jnp.concatenate(ps, axis=0), vops[g][e])
            for pr in range(PAIRS):
                o = o_st[pr * CHUNK:(pr + 1) * CHUNK]
                c0 = A_WIDTH + (g * PAIRS + pr) * 2 * HEAD_DIM
                ab_ref[:, c0:c0 + 2 * HEAD_DIM] = o
                ssq = ssq + jnp.sum(o * o, axis=-1, keepdims=True)
        rb = lax.rsqrt(ssq * (1.0 / B_WIDTH) + EPS)
        mixed_ref[:, A_WIDTH:] = ((ab_ref[:, A_WIDTH:] * rb) * gb_ref[...]).astype(BF16)
        mixed_t_ref[...] = mixed_ref[...].T

    full = lambda *shape: pl.BlockSpec(shape, lambda n: (0,) * len(shape))
    return _call(
        body, name="mixer_fwd", grid=(nb,),
        in_specs=[pl.BlockSpec((CHUNK, A_WIDTH), lambda n: (n, 0)),
                  pl.BlockSpec((CHUNK, A_WIDTH), lambda n: (n, 1)),
                  pl.BlockSpec((CHUNK, B_WIDTH), lambda n: (n, 2)),
                  pl.BlockSpec((CHUNK, 2 * KV_WIDTH), lambda n: (n, 12)),
                  pl.BlockSpec((CHUNK, 2 * KV_WIDTH), lambda n: (jnp.maximum(n - 1, 0), 12)),
                  full(A_GROUPS, CHUNK), full(A_GROUPS, CHUNK), full(A_GROUPS, CHUNK, CHUNK), full(A_GROUPS, CHUNK, 1),
                  pl.BlockSpec(memory_space=pltpu.SMEM),
                  pl.BlockSpec((None, B_HEADS, CHUNK, 2 * CHUNK), lambda n: (jnp.minimum(n, 1), 0, 0, 0)),
                  full(1, A_WIDTH), full(1, B_WIDTH)],
        out_specs=[pl.BlockSpec((CHUNK, D_MODEL), lambda n: (n, 0)), pl.BlockSpec((D_MODEL, CHUNK), lambda n: (0, n)),
                   pl.BlockSpec((CHUNK, D_MODEL), lambda n: (n, 0))],
        out_shape=[jax.ShapeDtypeStruct((T, D_MODEL), BF16), jax.ShapeDtypeStruct((D_MODEL, T), BF16),
                   jax.ShapeDtypeStruct((T, D_MODEL), F32)],
        sem=("parallel",), rides=rides,
    )(proj, proj, proj, proj, proj, lg, lb, wsp, bs_col, sinks, bias, ga, gb)


def _gmlp_bwd(proj, ab, dmixed, ga, lg, lb, wsp, bs_col, rides=()):
    T = proj.shape[0]
    nb = T // CHUNK

    def body(u_ref, v_ref, a_ref, dna_ref, ga_ref, lg_ref, lb_ref, w_ref, bs_ref,
             dp_ref, dpt_ref, dga_ref, dw_ref, dbs_ref, dlg_ref, dlb_ref):
        @pl.when(pl.program_id(0) == 0)
        def _():
            for r in (dga_ref, dw_ref, dbs_ref, dlg_ref, dlb_ref):
                r[...] = jnp.zeros_like(r)
        causal = _causal_mask()
        a_all = a_ref[...]
        dna = dna_ref[...]
        da_all, ra = _rms_bwd(dna, a_all, ga_ref[...])
        dga_ref[...] += jnp.sum(dna * (a_all * ra), axis=0, keepdims=True)
        for g in range(A_GROUPS):
            cols = slice(g * CHUNK, (g + 1) * CHUNK)
            wc = jnp.where(causal, w_ref[g], 0.0).astype(BF16)
            lgg = lg_ref[g:g + 1, :]
            u = u_ref[:, cols]
            v = v_ref[:, cols]
            ug, xhat, rstd, vl, mixed = _gate_forward(u, v, lgg, lb_ref[g:g + 1, :], wc, bs_ref[g])
            da = da_all[:, cols]
            dug = da * mixed
            dmg = da * ug
            dmg_b = dmg.astype(BF16)
            dbs_ref[g] += jnp.sum(dmg, axis=-1, keepdims=True)
            dw_ref[g] += jnp.where(causal, _dot_nt(dmg_b, vl), 0.0)
            dvl = _dot_tn(wc, dmg_b)
            dlg_ref[g:g + 1, :] += jnp.sum(dvl * xhat, axis=0, keepdims=True)
            dlb_ref[g:g + 1, :] += jnp.sum(dvl, axis=0, keepdims=True)
            dxh = dvl * lgg
            dvg = rstd * (dxh - jnp.mean(dxh, axis=-1, keepdims=True)
                          - xhat * jnp.mean(dxh * xhat, axis=-1, keepdims=True))
            _, gu = _gelu_and_grad(u)
            _, gv = _gelu_and_grad(v)
            dp_ref[:, cols] = (dug * gu).astype(BF16)
            dp_ref[:, A_WIDTH + g * CHUNK:A_WIDTH + (g + 1) * CHUNK] = (dvg * gv).astype(BF16)
        dpt_ref[...] = dp_ref[...].T

    full = lambda *shape: pl.BlockSpec(shape, lambda n: (0,) * len(shape))
    return _call(
        body, name="gmlp_bwd", grid=(nb,),
        in_specs=[pl.BlockSpec((CHUNK, A_WIDTH), lambda n: (n, 0)),
                  pl.BlockSpec((CHUNK, A_WIDTH), lambda n: (n, 1)),
                  pl.BlockSpec((CHUNK, A_WIDTH), lambda n: (n, 0)),
                  pl.BlockSpec((CHUNK, A_WIDTH), lambda n: (n, 0)),
                  full(1, A_WIDTH), full(A_GROUPS, CHUNK), full(A_GROUPS, CHUNK), full(A_GROUPS, CHUNK, CHUNK),
                  full(A_GROUPS, CHUNK, 1)],
        out_specs=[pl.BlockSpec((CHUNK, 2 * A_WIDTH), lambda n: (n, 0)), pl.BlockSpec((2 * A_WIDTH, CHUNK), lambda n: (0, n)),
                   full(1, A_WIDTH), full(A_GROUPS, CHUNK, CHUNK), full(A_GROUPS, CHUNK, 1),
                   full(A_GROUPS, CHUNK), full(A_GROUPS, CHUNK)],
        out_shape=[jax.ShapeDtypeStruct((T, 2 * A_WIDTH), BF16), jax.ShapeDtypeStruct((2 * A_WIDTH, T), BF16),
                   jax.ShapeDtypeStruct((1, A_WIDTH), F32), jax.ShapeDtypeStruct((A_GROUPS, CHUNK, CHUNK), F32),
                   jax.ShapeDtypeStruct((A_GROUPS, CHUNK, 1), F32), jax.ShapeDtypeStruct((A_GROUPS, CHUNK), F32),
                   jax.ShapeDtypeStruct((A_GROUPS, CHUNK), F32)],
        sem=("arbitrary",), rides=rides,
    )(proj, proj, ab, dmixed, ga, lg, lb, wsp, bs_col)


def _attn_bwd(proj, ab, dmixed, gb, sinks, bias, rides=()):
    T = proj.shape[0]
    nb = T // CHUNK
    qn = lambda n: jnp.minimum(n, nb - 1)

    def body(q_ref, kvc_ref, kvp_ref, o_ref, dnb_ref, gb_ref, sink_ref, bias_ref,
             dq_ref, dkv_ref, dqt_ref, dkvt_ref, dgb_ref, dsink_ref, dbias_ref, carry_ref, sacc_ref):
        n = pl.program_id(0)

        @pl.when(n == 0)
        def _():
            carry_ref[...] = jnp.zeros_like(carry_ref)
            sacc_ref[...] = jnp.zeros_like(sacc_ref)
            dgb_ref[...] = jnp.zeros_like(dgb_ref)
            dbias_ref[...] = jnp.zeros_like(dbias_ref)

        @pl.when(n < nb)
        def _():
            o_all = o_ref[...]
            dnb = dnb_ref[...]
            do_all, rb = _rms_bwd(dnb, o_all, gb_ref[...])
            dgb_ref[...] += jnp.sum(dnb * (o_all * rb), axis=0, keepdims=True)
            kops, vops = _band_operands(kvp_ref[...], kvc_ref[...])
            low = _low_lanes()
            halves = []
            for g in range(B_HEADS // Q_PER_KV):
                qst = (_stack_pairs(q_ref, g) * SCALE).astype(BF16)
                dost = _stack_pairs(do_all, g).astype(BF16)
                dq_st = jnp.zeros((PAIRS * CHUNK, 2 * HEAD_DIM), F32)
                dk_e, dv_e = [], []
                for e in range(2):
                    s_all = _dot_nt(qst, kops[g][e])
                    dp_all = _dot_nt(dost, vops[g][e])
                    ps, dsrs = [], []
                    for pr in range(PAIRS):
                        h = _head(g, pr, e)
                        rows = slice(pr * CHUNK, (pr + 1) * CHUNK)
                        p, p_sink = _softmax_scores(s_all[rows], bias_ref[h], sink_ref[0, h])
                        dp = dp_all[rows]
                        delta = jnp.sum(p * dp, axis=-1, keepdims=True)
                        ds = p * (dp - delta)
                        sacc_ref[:, h:h + 1] += -(p_sink * delta)
                        dbias_ref[h] += ds
                        ps.append(p.astype(BF16))
                        dsrs.append(ds.astype(BF16))
                    dsr_all = jnp.concatenate(dsrs, axis=0)
                    dq_st = dq_st + _dot(dsr_all, kops[g][e])
                    dk_e.append(_dot_tn(dsr_all, qst))
                    dv_e.append(_dot_tn(jnp.concatenate(ps, axis=0), dost))
                for pr in range(PAIRS):
                    c0 = (g * PAIRS + pr) * 2 * HEAD_DIM
                    dq_ref[:, c0:c0 + 2 * HEAD_DIM] = (dq_st[pr * CHUNK:(pr + 1) * CHUNK] * SCALE).astype(BF16)
                halves.append((dk_e, dv_e))
            tiles = []
            for t in range(2):
                g0, g1 = halves[0][t], halves[1][t]
                tiles.append(jnp.where(low, g0[0] + pltpu.roll(g0[1], HEAD_DIM, 1), pltpu.roll(g1[0], HEAD_DIM, 1) + g1[1]))
            dband = jnp.concatenate(tiles, axis=1)
            dkv = (carry_ref[...] + dband[:CHUNK]).astype(BF16)
            dkv_ref[...] = dkv
            dkvt_ref[...] = dkv.T
            dqt_ref[...] = dq_ref[...].T
            carry_ref[...] = dband[CHUNK:]

        @pl.when(n == nb)
        def _():
            dkv = carry_ref[...].astype(BF16)
            dkv_ref[...] = dkv
            dkvt_ref[...] = dkv.T
            dsink_ref[...] = jnp.sum(sacc_ref[...], axis=0, keepdims=True)

    full = lambda *shape: pl.BlockSpec(shape, lambda n: (0,) * len(shape))
    return _call(
        body, name="attn_bwd", grid=(nb + 1,),
        in_specs=[pl.BlockSpec((CHUNK, B_WIDTH), lambda n: (qn(n), 2)),
                  pl.BlockSpec((CHUNK, 2 * KV_WIDTH), lambda n: (qn(n), 12)),
                  pl.BlockSpec((CHUNK, 2 * KV_WIDTH), lambda n: (jnp.maximum(qn(n) - 1, 0), 12)),
                  pl.BlockSpec((CHUNK, B_WIDTH), lambda n: (qn(n), 1)),
                  pl.BlockSpec((CHUNK, B_WIDTH), lambda n: (qn(n), 1)),
                  full(1, B_WIDTH), pl.BlockSpec(memory_space=pltpu.SMEM),
                  pl.BlockSpec((None, B_HEADS, CHUNK, 2 * CHUNK), lambda n: (jnp.minimum(n, 1), 0, 0, 0))],
        out_specs=[pl.BlockSpec((CHUNK, B_WIDTH), lambda n: (qn(n), 0)),
                   pl.BlockSpec((CHUNK, 2 * KV_WIDTH), lambda n: (jnp.maximum(n - 1, 0), 0)),
                   pl.BlockSpec((B_WIDTH, CHUNK), lambda n: (0, qn(n))),
                   pl.BlockSpec((2 * KV_WIDTH, CHUNK), lambda n: (0, jnp.maximum(n - 1, 0))),
                   full(1, B_WIDTH), full(1, B_HEADS), full(B_HEADS, CHUNK, 2 * CHUNK)],
        out_shape=[jax.ShapeDtypeStruct((T, B_WIDTH), BF16), jax.ShapeDtypeStruct((T, 2 * KV_WIDTH), BF16),
                   jax.ShapeDtypeStruct((B_WIDTH, T), BF16), jax.ShapeDtypeStruct((2 * KV_WIDTH, T), BF16),
                   jax.ShapeDtypeStruct((1, B_WIDTH), F32), jax.ShapeDtypeStruct((1, B_HEADS), F32),
                   jax.ShapeDtypeStruct((B_HEADS, CHUNK, 2 * CHUNK), F32)],
        scratch_shapes=[pltpu.VMEM((CHUNK, 2 * KV_WIDTH), F32), pltpu.VMEM((CHUNK, B_HEADS), F32)],
        sem=("arbitrary",), rides=rides,
    )(proj, proj, proj, ab, dmixed, gb, sinks, bias)


def _sq_relu_grad(acc, r):
    return acc * (2.0 * r.astype(F32))


def _chip_index():
    return (2 * lax.axis_index("x") + lax.axis_index("y")).astype(jnp.int32).reshape(1)


def _cast_into_slot(w, *, tm, name):
    _, R, C = w.shape

    def body(me_ref, w_ref, o_ref):
        del me_ref
        o_ref[...] = w_ref[...].astype(BF16)

    return pl.pallas_call(
        body, name=name,
        grid_spec=pltpu.PrefetchScalarGridSpec(
            num_scalar_prefetch=1, grid=(R // tm,),
            in_specs=[pl.BlockSpec((None, tm, C), lambda i, me: (0, i, 0))],
            out_specs=pl.BlockSpec((None, tm, C), lambda i, me: (me[0], i, 0))),
        out_shape=jax.ShapeDtypeStruct((N_CHIPS, R, C), BF16), compiler_params=_params(("parallel",)),
    )(_chip_index(), w)


def _cast_into_slots_carrying(ws, *, steps, name, rides):
    n = len(ws)

    def body(*refs):
        for w_ref, o_ref in zip(refs[:n], refs[n:]):
            o_ref[...] = w_ref[...].astype(BF16)

    me = lambda: 2 * lax.axis_index("x") + lax.axis_index("y")
    return _call(
        body, name=name, grid=(steps,),
        in_specs=[pl.BlockSpec((None, w.shape[1] // steps, w.shape[2]), lambda i: (0, i, 0)) for w in ws],
        out_specs=[pl.BlockSpec((None, w.shape[1] // steps, w.shape[2]), lambda i: (me(), i, 0)) for w in ws],
        out_shape=[jax.ShapeDtypeStruct((N_CHIPS,) + w.shape[1:], BF16) for w in ws], sem=("arbitrary",), rides=rides,
    )(*ws)


def _owner_total(gh, others, *, tm, name):
    _, hr, C = gh.shape

    def body(me_ref, g_ref, o_ref_in, out_ref):
        del me_ref
        acc = g_ref[...]
        for j in range(3):
            acc = acc + o_ref_in[j].astype(F32)
        out_ref[...] = acc

    return pl.pallas_call(
        body, name=name,
        grid_spec=pltpu.PrefetchScalarGridSpec(
            num_scalar_prefetch=1, grid=(hr // tm,),
            in_specs=[pl.BlockSpec((None, tm, C), lambda i, me: (me[0], i, 0)),
                      pl.BlockSpec((3, tm, C), lambda i, me: (0, i, 0))],
            out_specs=pl.BlockSpec((tm, C), lambda i, me: (i, 0))),
        out_shape=jax.ShapeDtypeStruct((hr, C), F32),
        compiler_params=_params(("parallel",)),
    )(_chip_index(), gh, others)


def _adamw_math(w, g, m, v):
    m = ADAM_B1 * m + (1.0 - ADAM_B1) * g
    v = ADAM_B2 * v + (1.0 - ADAM_B2) * (g * g)
    m_hat = m / (1.0 - ADAM_B1 ** ADAM_STEP)
    v_hat = v / (1.0 - ADAM_B2 ** ADAM_STEP)
    delta = -ADAM_LR * (m_hat / (jnp.sqrt(v_hat) + ADAM_EPS) + ADAM_WD * w)
    return delta, m, v


def _adamw_halves(w, own, got, m, v, *, tm, name, rides=()):
    _, R, C = w.shape
    nt = (R // 2) // tm

    n_steps, ring = 2 * nt, 3

    def body(w_hbm, own_ref, got_ref, m_hbm, v_hbm, g_ref, d_ref, nm_ref, nv_ref, wbuf, mbuf, vbuf, sems):
        s = pl.program_id(0) * nt + pl.program_id(1)

        def copies(step, slot):
            rows = pl.ds(pl.multiple_of(step * tm, 8), tm)
            return [pltpu.make_async_copy(src.at[0, rows, :], buf.at[slot], sems.at[k, slot])
                    for k, (src, buf) in enumerate(((w_hbm, wbuf), (m_hbm, mbuf), (v_hbm, vbuf)))]

        @pl.when(s == 0)
        def _():
            for first in range(min(ring - 1, n_steps)):
                for c in copies(first, first):
                    c.start()

        @pl.when(s + ring - 1 < n_steps)
        def _():
            for c in copies(s + ring - 1, (s + ring - 1) % ring):
                c.start()

        slot = s % ring
        for c in copies(s, slot):
            c.wait()
        g = jnp.where(pl.program_id(0) == lax.axis_index("c"), own_ref[...], got_ref[...])
        g_ref[...] = g
        d_ref[...], nm_ref[...], nv_ref[...] = _adamw_math(wbuf[slot], g, mbuf[slot], vbuf[slot])

    whole = pl.BlockSpec((None, tm, C), lambda h, i: (0, h * nt + i, 0))
    half = pl.BlockSpec((tm, C), lambda h, i: (i, 0))
    any_spec = pl.BlockSpec(memory_space=pl.ANY)
    return _call(
        body, name=name, grid=(2, nt), in_specs=[any_spec, half, half, any_spec, any_spec], out_specs=[whole] * 4,
        out_shape=[jax.ShapeDtypeStruct((1, R, C), F32)] * 4, sem=("arbitrary", "arbitrary"), rides=rides,
        scratch_shapes=[pltpu.VMEM((ring, tm, C), F32)] * 3 + [pltpu.SemaphoreType.DMA((3, ring))],
    )(w, own, got, m, v)


def _adamw_small(w, slots, m, v, *, name, rides=()):
    def body(w_ref, slots_ref, m_ref, v_ref, g_ref, d_ref, nm_ref, nv_ref):
        g = slots_ref[0]
        for d in range(1, N_DEV):
            g = g + slots_ref[d]
        g_ref[...] = g
        d_ref[...], nm_ref[...], nv_ref[...] = _adamw_math(w_ref[...], g, m_ref[...], v_ref[...])

    flat = pl.BlockSpec(w.shape, lambda i: (0, 0))
    return _call(
        body, name=name, grid=(1,), in_specs=[flat, pl.BlockSpec(slots.shape, lambda i: (0, 0, 0)), flat, flat],
        out_specs=[flat] * 4, out_shape=[jax.ShapeDtypeStruct(w.shape, F32)] * 4, sem=("arbitrary",), rides=rides,
    )(w, slots, m, v)


SMALL = ["rel_bias_table", "mix_norm_g", "gate_norm_g", "gate_norm_b", "w_spatial", "b_spatial", "attn_sinks",
         "out_norm_a_g", "out_norm_b_g", "ffn_norm_g", "final_norm_g"]
SMALL_A = ["gate_norm_g", "gate_norm_b", "w_spatial", "b_spatial", "out_norm_a_g"]
SMALL_B = ["rel_bias_table", "mix_norm_g", "attn_sinks", "out_norm_b_g", "ffn_norm_g", "final_norm_g"]
LARGE = ["w_in", "w_out", "w_up", "w_down"]
ROW_TILE = {"w_in": 208, "w_out": 256, "w_up": 256, "w_down": 256}
WEIGHTS = ["rel_bias_table", "mix_norm_g", "w_in", "gate_norm_g", "gate_norm_b", "w_spatial", "b_spatial", "attn_sinks",
           "out_norm_a_g", "out_norm_b_g", "w_out", "ffn_norm_g", "w_up", "w_down", "final_norm_g"]
PACK_UNIT = 8 * 128


def _pack(parts):
    rows = []
    for p in parts:
        flat = p.reshape(-1)
        pad = (-flat.shape[0]) % PACK_UNIT
        rows.append(jnp.pad(flat, (0, pad)).reshape(-1, 128))
    return jnp.concatenate(rows, axis=0)


def _unpack(packed, like):
    out, row = [], 0
    for p in like:
        n = math.prod(p.shape)
        nrows = (n + PACK_UNIT - 1) // PACK_UNIT * 8
        out.append(packed[row:row + nrows].reshape(-1)[:n].reshape(p.shape))
        row += nrows
    return out


def kernel(x, rel_bias_table, mix_norm_g, w_in, gate_norm_g, gate_norm_b, w_spatial, b_spatial, attn_sinks, out_norm_a_g, out_norm_b_g, w_out, ffn_norm_g, w_up, w_down, final_norm_g, loss_target, m_rel_bias_table, m_mix_norm_g, m_w_in, m_gate_norm_g, m_gate_norm_b, m_w_spatial, m_b_spatial, m_attn_sinks, m_out_norm_a_g, m_out_norm_b_g, m_w_out, m_ffn_norm_g, m_w_up, m_w_down, m_final_norm_g, v_rel_bias_table, v_mix_norm_g, v_w_in, v_gate_norm_g, v_gate_norm_b, v_w_spatial, v_b_spatial, v_attn_sinks, v_out_norm_a_g, v_out_norm_b_g, v_w_out, v_ffn_norm_g, v_w_up, v_w_down, v_final_norm_g):
    args = dict(locals())
    wts = {n: args[n] for n in WEIGHTS}
    mom = {n: args["m_" + n] for n in WEIGHTS}
    var = {n: args["v_" + n] for n in WEIGHTS}
    sp = {n: wts[n] for n in SMALL}
    x2, tgt = x[0], loss_target[0]
    T = x2.shape[0]
    tm = min(512, T)
    tl = min(1024, T)
    lg = sp["gate_norm_g"].reshape(A_GROUPS, CHUNK)
    lb = sp["gate_norm_b"].reshape(A_GROUPS, CHUNK)
    wsp = sp["w_spatial"].reshape(A_GROUPS, CHUNK, CHUNK)
    bs_col = sp["b_spatial"].reshape(A_GROUPS, CHUNK, 1)
    sinks = sp["attn_sinks"].reshape(1, B_HEADS)
    ga = sp["out_norm_a_g"].reshape(1, A_WIDTH)
    gb = sp["out_norm_b_g"].reshape(1, B_WIDTH)
    g1 = sp["mix_norm_g"].reshape(1, D_MODEL)
    g2 = sp["ffn_norm_g"].reshape(1, D_MODEL)
    gf = sp["final_norm_g"].reshape(1, D_MODEL)

    def owner_total(n, gh, others):
        return _owner_total(gh, others, tm=ROW_TILE[n], name="rs_owner_total_" + n)

    def halves_view(at, shards):
        return at.reshape(shards, 2, at.shape[0] // shards // 2, at.shape[1])

    for d in (wts, mom, var):
        d["w_in"] = jnp.swapaxes(d["w_in"], 1, 2)

    s_in = _cast_into_slot(wts["w_in"], tm=ROW_TILE["w_in"], name="cast_w_in")
    (s_out, s_up, s_down), ((g_in,),) = _cast_into_slots_carrying(
        [wts["w_out"], wts["w_up"], wts["w_down"]], steps=8, name="cast_w_rest",
        rides=[_ride_gather(s_in, chain=(0, 1, 1), chain_fracs=(0.3, 0.6))])
    win_t = g_in.reshape(PROJ_WIDTH, D_MODEL)
    bias = _bias_build(sp["rel_bias_table"])
    (n1, proj), ((g_out,), (s_up,)) = _norm_matmul_wide(
        x2, g1, win_t, tm=tm, tn=PROJ_WIDTH // 2, name="in_proj",
        rides=[_ride_gather(s_out, chain=(0, 1, 1), chain_fracs=(0.65, 0.85)), _ride_gather(s_up, s1=(0, 3, 8))])
    wo = g_out.reshape(A_WIDTH + B_WIDTH, D_MODEL)
    (mixed, mixed_t, ab), ((s_up,), (s_down,), (n1_sib,)) = _mixer_fwd(
        proj, lg, lb, wsp, bs_col, sinks, bias, ga, gb,
        rides=[_ride_gather(s_up, s2=(0, 3, 8), s1=(3, 8, 8)), _ride_gather(s_down, s1=(0, 2, 8)),
               _ride_to_sibling(n1, first=True)])
    mixed_t = halves_view(mixed_t, N_CHIPS)
    h1, ((wu,), (s_down,), (mixed_t_sib,)) = _matmul_res(
        mixed, wo, x2, tm=tl, tn=1024, tk=D_MODEL, prologue=_to_bf16, name="out_proj",
        rides=[_ride_gather(s_up, s3=(0, 3, 8), tail=(3, 8, 8), mid_frac=0.75), _ride_gather(s_down, s2=(0, 2, 8)),
               _ride_to_sibling(mixed_t, halves=True)])
    (n2t, zp, z2, z2t), ((g_down,),) = _norm_matmul_sq(
        h1, g2, wu, tm=tl, tn=1024, name="up_proj", rides=[_ride_gather(s_down, s3=(0, 2, 8), chain=(2, 8, 8), chain_fracs=(0.5, 0.8))])
    wd = g_down.reshape(D_FF, D_MODEL)
    n2t, z2t = halves_view(n2t, 1), halves_view(z2t, N_CHIPS)
    h2, ((n2t_sib,), (z2t_sib,)) = _matmul_res(
        z2, wd, h1, tm=tl, tn=1024, tk=4096, prologue=_to_bf16, name="down_proj",
        rides=[_ride_to_sibling(n2t, halves=True), _ride_to_sibling(z2t, halves=True)])

    dh2, dh2b, dgf, loss = _loss_bwd(h2, tgt, gf, tm=tm)
    dzp, ((dh2b_sib,),) = _matmul_nt(dh2b, wd, tm=tl, tn=1024, tk=D_MODEL, name="bwd_dz", extra=zp,
                                     epilogue=_sq_relu_grad, out_dtype=BF16, rides=[_ride_to_sibling(dh2b)])
    (gd, gdb), ((dzp_sib,),) = _grad_pair(z2t, z2t_sib, dh2b, dh2b_sib, cols_sharded=False, tmo=1024, tk=tl,
                                          name="grad_w_down", rides=[_ride_to_sibling(dzp)])
    (gu, gub), ((o_d,),) = _grad_pair(n2t, n2t_sib, dzp, dzp_sib, cols_sharded=True, tmo=1024, tk=tl,
                                      name="grad_w_up", rides=[_ride_scatter(gdb, None, (0, 7, 8))])
    dn2, ((o_d,), (o_u,)) = _matmul_nt(dzp, wu, tm=tl, tn=1024, tk=4096, name="bwd_dn2",
                                       rides=[_ride_scatter(gdb, o_d, (7, 8, 8)), _ride_scatter(gub, None, (0, 6, 8))])
    h_d = owner_total("w_down", gd, o_d)
    (dh1, dh1b, dg2), ((o_u,),) = _rms_bwd_res(dn2, h1, g2, dh2, tm=tm, name="ffn_norm_bwd",
                                               rides=[_ride_scatter(gub, o_u, (6, 7, 8))])
    dmixed, ((o_u,), (dh1b_sib,), (w_d,)) = _matmul_nt(
        dh1b, wo, tm=tl, tn=1024, tk=D_MODEL, name="bwd_dmixed",
        rides=[_ride_scatter(gub, o_u, (7, 8, 8)), _ride_to_sibling(dh1b), _ride_swap(h_d)])
    h_u = owner_total("w_up", gu, o_u)
    (go, gob), ((w_u,),) = _grad_pair_merged(mixed_t, mixed_t_sib, dh1b, dh1b_sib, tk=tl, name="grad_w_out",
                                             rides=[_ride_swap(h_u)])
    (duv, duv_t, dga, dwsp, dbs, dlg, dlb), ((o_o,),) = _gmlp_bwd(proj, ab, dmixed, ga, lg, lb, wsp, bs_col,
                                                                  rides=[_ride_scatter(gob)])
    h_o = owner_total("w_out", go, o_o)
    small = {"gate_norm_g": dlg, "gate_norm_b": dlb, "w_spatial": dwsp, "b_spatial": dbs, "out_norm_a_g": dga}
    hr_in = PROJ_WIDTH // N_CHIPS // 2
    (dq, dkv, dq_t, dkv_t, dgb, dsinks, dbias), ((slots_a,), (dproj_t_sib,)) = _attn_bwd(
        proj, ab, dmixed, gb, sinks, bias,
        rides=[_ride_small_to_all(_pack([small[n] for n in SMALL_A])), _ride_rows_to_sibling(duv_t, hr_in, 2, N_CHIPS)])
    dproj_t = halves_view(jnp.concatenate([duv_t, dq_t, dkv_t], axis=0), N_CHIPS)
    dtable, ((dproj_t_sib,),) = _bias_grad(
        dbias, rides=[_ride_to_sibling(dproj_t, halves=True, shards=(2, N_CHIPS), land=dproj_t_sib)])
    (gi, gib_near), ((w_o,),) = _grad_pair(
        dproj_t, dproj_t_sib, n1, n1_sib, cols_sharded=False, tmo=hr_in, tk=tl, name="grad_w_in_near", shards="near",
        rides=[_ride_swap(h_o)])
    (gi, gib_far), ((o_i,),) = _grad_pair(
        dproj_t, dproj_t_sib, n1, n1_sib, cols_sharded=False, tmo=hr_in, tk=tl, name="grad_w_in_far", shards="far",
        into=gi, rides=[_ride_scatter(gib_near, None, to=(0, 1))])
    dn1, ((o_i,),) = _matmul_parts([duv, dq, dkv], win_t, tm=tl, tn=1024, name="bwd_dn1",
                                   rides=[_ride_scatter(gib_far, o_i, to=(2,))])
    h_i = owner_total("w_in", gi, o_i)
    dx, dg1 = _rms_bwd_res(dn1, x2, g1, dh1, tm=tm, name="mix_norm_bwd", bf16_copy=False)
    small.update({"rel_bias_table": dtable.reshape(N_BUCKETS, B_HEADS), "mix_norm_g": dg1, "attn_sinks": dsinks,
                  "out_norm_b_g": dgb, "ffn_norm_g": dg2, "final_norm_g": dgf})
    out_g, out_d, out_m, out_v = {}, {}, {}, {}

    def adamw_small(names, slots, tag, rides=()):
        extra = [jnp.zeros((1, 1), F32)] if tag == "b" else []
        like = [wts[n] for n in names] + extra
        res = _adamw_small(_pack(like), slots, _pack([mom[n] for n in names] + extra),
                           _pack([var[n] for n in names] + extra), name="adamw_small_" + tag, rides=rides)
        res, carried = res if rides else (res, None)
        for store, packed in zip((out_g, out_d, out_m, out_v), res):
            for n, val in zip(names + ["loss"], _unpack(packed, like)):
                store[n] = val
        return carried

    (w_i,), (slots_b,) = adamw_small(
        SMALL_A, slots_a, "a", rides=[_ride_swap(h_i), _ride_small_to_all(_pack([small[n] for n in SMALL_B] + [loss]))])
    for n, h, s in zip(LARGE, [h_i, h_o, h_u, h_d], [w_i, w_o, w_u, w_d]):
        res = _adamw_halves(wts[n], h, s, mom[n], var[n], tm=ROW_TILE[n], name="adamw_" + n)
        if n == "w_in":
            res = [jnp.swapaxes(r, 1, 2) for r in res]
        out_g[n], out_d[n], out_m[n], out_v[n] = res
    adamw_small(SMALL_B, slots_b, "b")

    total = out_g["loss"][0, 0]
    return (total, dx[None], *[out_g[n] for n in WEIGHTS], *[out_d[n] for n in WEIGHTS],
            *[out_m[n] for n in WEIGHTS], *[out_v[n] for n in WEIGHTS])
```

```python
import math

import numpy as np
import jax
import jax.numpy as jnp
from jax import lax
from jax.experimental import pallas as pl
from jax.experimental.pallas import tpu as pltpu

F32 = jnp.float32
BF16 = jnp.bfloat16

D_MODEL = 2048
CHUNK = 128
A_GROUPS = 8
A_WIDTH = 1024
HEAD_DIM = 64
B_HEADS = 16
Q_PER_KV = 8
B_WIDTH = 1024
KV_WIDTH = 128
PROJ_WIDTH = 3328
D_FF = 8192
N_BUCKETS = 32
EPS = 1e-5
NEG = -1e30
SCALE = HEAD_DIM ** -0.5
N_CHIPS = 4
N_DEV = 8

ADAM_LR = 0.001
ADAM_B1 = 0.9
ADAM_B2 = 0.999
ADAM_EPS = 1e-08
ADAM_WD = 0.01
ADAM_STEP = 10

VMEM_LIMIT = 60 * 1024 * 1024
MESH = pl.DeviceIdType.MESH


def _bucket_thresholds():
    d = np.arange(CHUNK)
    n_exact = N_BUCKETS // 2
    relf = np.maximum(d, n_exact).astype(np.float64)
    large = n_exact + (np.log(relf / n_exact) / math.log(CHUNK / n_exact) * (N_BUCKETS - n_exact)).astype(np.int32)
    bucket = np.where(d < n_exact, d, np.minimum(large, N_BUCKETS - 1))
    return [int(np.min(d[bucket >= b])) for b in range(1, N_BUCKETS)]


BUCKET_THR = _bucket_thresholds()


def _params(sem=None):
    return pltpu.CompilerParams(dimension_semantics=sem, vmem_limit_bytes=VMEM_LIMIT)


def _gelu(x):
    c = math.sqrt(2.0 / math.pi)
    return 0.5 * x * (1.0 + jnp.tanh(c * (x + 0.044715 * (x * x * x))))


def _gelu_and_grad(x):
    c = math.sqrt(2.0 / math.pi)
    x2 = x * x
    t = jnp.tanh(c * (x + 0.044715 * (x2 * x)))
    g = 0.5 * x * (1.0 + t)
    dg = 0.5 * (1.0 + t) + 0.5 * x * (1.0 - t * t) * (c * (1.0 + 3.0 * 0.044715 * x2))
    return g, dg


def _dot(a, b):
    return jnp.dot(a, b, preferred_element_type=F32)


def _dot_nt(a, b):
    return lax.dot_general(a, b, (((1,), (1,)), ((), ())), preferred_element_type=F32)


def _dot_tn(a, b):
    return lax.dot_general(a, b, (((0,), (0,)), ((), ())), preferred_element_type=F32)


def _rms_bwd(dn, h, g):
    r = lax.rsqrt(jnp.mean(h * h, axis=-1, keepdims=True) + EPS)
    w = dn * g
    dh = r * w - h * ((r * r * r) * jnp.mean(w * h, axis=-1, keepdims=True))
    return dh, r


def _place():
    x, y, c = lax.axis_index("x"), lax.axis_index("y"), lax.axis_index("c")
    chips = [(1 - x, y), (x, 1 - y), (1 - x, 1 - y)]
    return x, y, c, chips


def _remote(src, dst, send_sem, recv_sem, to):
    return pltpu.make_async_remote_copy(src_ref=src, dst_ref=dst, send_sem=send_sem, recv_sem=recv_sem,
                                        device_id=to, device_id_type=MESH)


class _Ride:
    def __init__(self, args, out_shape, n_sem, start, finish, mids=(), aliases=None):
        self.args, self.out_shape, self.n_sem = list(args), list(out_shape), n_sem
        self.start, self.mids, self.finish = start, list(mids), finish
        self.aliases = dict(aliases or {})


def _call(body, *, name, grid, in_specs, out_specs, out_shape, scratch_shapes=(), sem=None, rides=(), aliases=None):
    single = not isinstance(out_shape, (list, tuple))
    out_specs = [out_specs] if single else list(out_specs)
    out_shape = [out_shape] if single else list(out_shape)
    n_in, n_out, n_scr = len(in_specs), len(out_shape), len(scratch_shapes)
    r_in = [len(r.args) for r in rides]
    r_out = [len(r.out_shape) for r in rides]
    any_spec = pl.BlockSpec(memory_space=pl.ANY)
    aliases, off_i, off_o = dict(aliases or {}), n_in, n_out
    for r in rides:
        for i, o in r.aliases.items():
            aliases[off_i + i] = off_o + o
        off_i += len(r.args)
        off_o += len(r.out_shape)
    steps = math.prod(grid)

    def wrapped(*refs):
        p = 0
        ins = refs[p:p + n_in]; p += n_in
        rins = refs[p:p + sum(r_in)]; p += sum(r_in)
        outs = refs[p:p + n_out]; p += n_out
        routs = refs[p:p + sum(r_out)]; p += sum(r_out)
        scr = refs[p:p + n_scr]; p += n_scr
        sems = refs[p:]
        parts, pi, po = [], 0, 0
        for k, r in enumerate(rides):
            parts.append((rins[pi:pi + r_in[k]], routs[po:po + r_out[k]], sems[2 * k], sems[2 * k + 1]))
            pi += r_in[k]
            po += r_out[k]
        lin = 0
        for d in range(len(grid)):
            lin = lin * grid[d] + pl.program_id(d)
        if rides:
            @pl.when(lin == 0)
            def _():
                for r, part in zip(rides, parts):
                    r.start(*part)
        body(*ins, *outs, *scr)
        for r, part in zip(rides, parts):
            for frac, fn in r.mids:
                @pl.when(lin == min(steps - 1, int(frac * steps)))
                def _(fn=fn, part=part):
                    fn(*part)
        if rides:
            @pl.when(lin == steps - 1)
            def _():
                for r, part in zip(rides, parts):
                    r.finish(*part)

    scratch = list(scratch_shapes)
    for r in rides:
        scratch += [pltpu.SemaphoreType.DMA((r.n_sem,)), pltpu.SemaphoreType.DMA((r.n_sem,))]
    if rides:
        sem = ("arbitrary",) * len(grid)
    res = pl.pallas_call(
        wrapped, name=name, grid=grid,
        in_specs=list(in_specs) + [any_spec] * sum(r_in),
        out_specs=out_specs + [any_spec] * sum(r_out),
        out_shape=out_shape + [s for r in rides for s in r.out_shape],
        scratch_shapes=scratch, input_output_aliases=aliases,
        compiler_params=_params(sem),
    )

    def run(*args):
        got = res(*args, *[a for r in rides for a in r.args])
        mine = got[0] if single else list(got[:n_out])
        if not rides:
            return mine
        rest, out = list(got[n_out:]), []
        for k in range(len(rides)):
            out.append(rest[:r_out[k]])
            rest = rest[r_out[k]:]
        return mine, out

    return run


def _ride_gather(slot, s1=None, s2=None, s3=None, tail=None, chain=None, mid_frac=0.6, chain_fracs=(0.35, 0.7)):
    half = slot.shape[1] // 2

    def rows(part, c, which=None):
        k0, k1, n = part
        count, first = (k1 - k0) * (half // n), c * half + k0 * (half // n)
        return pl.ds(first, count) if which is None else pl.ds(first + which * (count // 2), count // 2)

    def ids():
        x, y, c, _ = _place()
        return x, y, c, 2 * x + y, 2 * (1 - x) + y, 2 * x + (1 - y), 2 * (1 - x) + (1 - y)

    def copy(full, chip, r, ss, rs, k, to):
        piece = full.at[chip, r, :]
        return _remote(piece, piece, ss.at[k], rs.at[k], to)

    def to_neighbours(full, ss, rs, part, base):
        x, y, c, me, _, _, _ = ids()
        return [copy(full, me, rows(part, c), ss, rs, base, (1 - x, y, c)),
                copy(full, me, rows(part, c), ss, rs, base + 1, (x, 1 - y, c))]

    def from_neighbours(full, ss, rs, part, base):
        x, y, c, _, cx, cy, _ = ids()
        return [copy(full, cx, rows(part, c), ss, rs, base, (x, y, c)), copy(full, cy, rows(part, c), ss, rs, base + 1, (x, y, c))]

    def onward(full, ss, rs, part, base):
        x, y, c, _, cx, cy, _ = ids()
        return [copy(full, cx, rows(part, c, 0), ss, rs, base, (x, 1 - y, c)),
                copy(full, cy, rows(part, c, 1), ss, rs, base + 1, (1 - x, y, c))]

    def from_onward(full, ss, rs, part, base):
        x, y, c, _, _, _, cd = ids()
        return [copy(full, cd, rows(part, c, 0), ss, rs, base, (x, y, c)), copy(full, cd, rows(part, c, 1), ss, rs, base + 1, (x, y, c))]

    def to_sibling(full, ss, rs, part, base, diagonal):
        x, y, c, _, cx, cy, cd = ids()
        return [copy(full, chip, rows(part, c), ss, rs, base + j, (x, y, 1 - c))
                for j, chip in enumerate([cd] if diagonal else [cx, cy])]

    def from_sibling(full, ss, rs, part, base, diagonal):
        x, y, c, _, cx, cy, cd = ids()
        return [copy(full, chip, rows(part, 1 - c), ss, rs, base + j, (x, y, c))
                for j, chip in enumerate([cd] if diagonal else [cx, cy])]

    def start(ins, outs, ss, rs):
        full, cps = outs[0], []
        for part, base in ((s1, 0), (chain, 12)):
            if part is not None:
                cps += to_neighbours(full, ss, rs, part, base)
        for part, b_ici, b_sib in ((s2, 2, 4), (tail, 7, 9)):
            if part is not None:
                cps += onward(full, ss, rs, part, b_ici) + to_sibling(full, ss, rs, part, b_sib, False)
        if s3 is not None:
            cps += to_sibling(full, ss, rs, s3, 6, True)
        for cp in cps:
            cp.start()

    def second(part, b_in, b_ici, b_sib):
        def fn(ins, outs, ss, rs):
            for cp in from_neighbours(outs[0], ss, rs, part, b_in):
                cp.wait_recv()
            for cp in onward(outs[0], ss, rs, part, b_ici) + to_sibling(outs[0], ss, rs, part, b_sib, False):
                cp.start()
        return fn

    def third(part, b_ici, b_sib):
        def fn(ins, outs, ss, rs):
            for cp in from_onward(outs[0], ss, rs, part, b_ici):
                cp.wait_recv()
            for cp in to_sibling(outs[0], ss, rs, part, b_sib, True):
                cp.start()
        return fn

    mids = []
    if tail is not None:
        mids.append((mid_frac, third(tail, 7, 11)))
    if chain is not None:
        mids += [(chain_fracs[0], second(chain, 12, 14, 16)), (chain_fracs[1], third(chain, 14, 18))]

    def finish(ins, outs, ss, rs):
        full, got, sent = outs[0], [], []
        if s1 is not None:
            got += from_neighbours(full, ss, rs, s1, 0)
            sent += to_neighbours(full, ss, rs, s1, 0)
        if s2 is not None:
            got += from_onward(full, ss, rs, s2, 2) + from_sibling(full, ss, rs, s2, 4, False)
            sent += onward(full, ss, rs, s2, 2) + to_sibling(full, ss, rs, s2, 4, False)
        if s3 is not None:
            got += from_sibling(full, ss, rs, s3, 6, True)
            sent += to_sibling(full, ss, rs, s3, 6, True)
        if tail is not None:
            got += from_sibling(full, ss, rs, tail, 9, False) + from_sibling(full, ss, rs, tail, 11, True)
            sent += onward(full, ss, rs, tail, 7) + to_sibling(full, ss, rs, tail, 9, False) + to_sibling(full, ss, rs, tail, 11, True)
        if chain is not None:
            got += from_sibling(full, ss, rs, chain, 16, False) + from_sibling(full, ss, rs, chain, 18, True)
            sent += (to_neighbours(full, ss, rs, chain, 12) + onward(full, ss, rs, chain, 14)
                     + to_sibling(full, ss, rs, chain, 16, False) + to_sibling(full, ss, rs, chain, 18, True))
        for cp in got:
            cp.wait_recv()
        for cp in sent:
            cp.wait_send()

    return _Ride([slot], [jax.ShapeDtypeStruct(slot.shape, slot.dtype)], 19, start, finish, mids=mids, aliases={0: 0})


def _ride_scatter(q, land=None, part=(0, 1), to=(0, 1, 2)):
    k0, k1, n = part if len(part) == 3 else (part[0], part[0] + 1, part[1])
    rows_n = q.shape[1] // n
    rows = pl.ds(k0 * rows_n, (k1 - k0) * rows_n)

    def copies(ins, outs, ss, rs):
        x, y, c, chips = _place()
        return [_remote(ins[0].at[2 * chip[0] + chip[1], rows, :], outs[0].at[j, rows, :], ss.at[j], rs.at[j], (*chip, c))
                for j, chip in enumerate(chips) if j in to]

    def start(*a):
        for cp in copies(*a):
            cp.start()

    def finish(*a):
        for cp in copies(*a):
            cp.wait()

    shape = jax.ShapeDtypeStruct((3,) + q.shape[1:], q.dtype)
    if land is None:
        return _Ride([q], [shape], 3, start, finish)
    return _Ride([q, land], [shape], 3, start, finish, aliases={1: 0})


def _ride_to_sibling(a, halves=False, first=False, shards=None, land=None):
    s0, s1 = shards or (0, a.shape[0])

    def copy(ins, outs, ss, rs):
        x, y, c, _ = _place()
        if halves:
            src, dst = ins[0].at[s0:s1, 1 - c], outs[0].at[s0:s1]
        else:
            src, dst = (ins[0].at[0] if first else ins[0]), outs[0]
        return _remote(src, dst, ss.at[0], rs.at[0], (x, y, 1 - c))

    shape = (a.shape[0],) + a.shape[2:] if halves else (a.shape[1:] if first else a.shape)
    return _Ride([a] if land is None else [a, land], [jax.ShapeDtypeStruct(shape, a.dtype)], 1,
                 lambda *a_: copy(*a_).start(), lambda *a_: copy(*a_).wait(), aliases=None if land is None else {1: 0})


def _ride_rows_to_sibling(a, hr, shards, total):
    def copies(ins, outs, ss, rs):
        x, y, c, _ = _place()
        return [_remote(ins[0].at[pl.ds((2 * s + 1 - c) * hr, hr), :], outs[0].at[s], ss.at[s], rs.at[s], (x, y, 1 - c))
                for s in range(shards)]

    def start(*a_):
        for cp in copies(*a_):
            cp.start()

    def finish(*a_):
        for cp in copies(*a_):
            cp.wait()

    return _Ride([a], [jax.ShapeDtypeStruct((total, hr, a.shape[1]), a.dtype)], shards, start, finish)


def _ride_swap(h):
    def copy(ins, outs, ss, rs):
        x, y, c, _ = _place()
        return _remote(ins[0], outs[0], ss.at[0], rs.at[0], (x, y, 1 - c))

    return _Ride([h], [jax.ShapeDtypeStruct(h.shape, h.dtype)], 1,
                 lambda *a: copy(*a).start(), lambda *a: copy(*a).wait())


def _mesh_place(p):
    return (p // 4, (p // 2) % 2, p % 2)


def _ride_small_to_all(packed):
    def copies(ins, outs, ss, rs):
        x, y, c, _ = _place()
        me = 4 * x + 2 * y + c
        return [_remote(ins[0], outs[0].at[me], ss.at[k - 1], rs.at[k - 1], _mesh_place((me + k) % N_DEV))
                for k in range(1, N_DEV)]

    def own(ins, outs, ss, rs):
        x, y, c, _ = _place()
        return pltpu.make_async_copy(ins[0], outs[0].at[4 * x + 2 * y + c], ss.at[N_DEV - 1])

    def start(*a):
        own(*a).start()
        for cp in copies(*a):
            cp.start()

    def finish(ins, outs, ss, rs):
        x, y, c, _ = _place()
        me = 4 * x + 2 * y + c
        for k in range(1, N_DEV):
            _remote(ins[0], outs[0].at[(me + N_DEV - k) % N_DEV], ss.at[k - 1], rs.at[k - 1], (x, y, c)).wait_recv()
        for cp in copies(ins, outs, ss, rs):
            cp.wait_send()
        own(ins, outs, ss, rs).wait()

    return _Ride([packed], [jax.ShapeDtypeStruct((N_DEV,) + packed.shape, packed.dtype)], N_DEV, start, finish)


def _norm_bf16(a_ref, g_ref):
    xf = a_ref[...]
    r = lax.rsqrt(jnp.mean(xf * xf, axis=-1, keepdims=True) + EPS)
    return ((xf * r) * g_ref[...]).astype(BF16)


def _norm_matmul_wide(a, g, b, *, tm, tn, name, rides=()):
    T, K = a.shape
    N = b.shape[0]

    def body(a_ref, g_ref, b_ref, n_ref, o_ref):
        n = _norm_bf16(a_ref, g_ref)
        n_ref[...] = n
        o_ref[...] = _dot_nt(n, b_ref[...])

    return _call(
        body, name=name, grid=(N // tn, T // tm),
        in_specs=[pl.BlockSpec((tm, K), lambda j, i: (i, 0)), pl.BlockSpec((1, K), lambda j, i: (0, 0)),
                  pl.BlockSpec((tn, K), lambda j, i: (j, 0))],
        out_specs=[pl.BlockSpec((None, tm, K), lambda j, i: (j, i, 0)), pl.BlockSpec((tm, tn), lambda j, i: (i, j))],
        out_shape=[jax.ShapeDtypeStruct((N // tn, T, K), BF16), jax.ShapeDtypeStruct((T, N), F32)],
        sem=("arbitrary", "arbitrary"), rides=rides,
    )(a, g, b)


def _norm_matmul_sq(a, g, b, *, tm, tn, name, rides=()):
    T, K = a.shape
    per = b.shape[2] // tn
    N = b.shape[0] * b.shape[2]

    def body(a_ref, g_ref, b_ref, nt_ref, o_ref, z_ref, zt_ref, n_scr):
        @pl.when(pl.program_id(1) == 0)
        def _():
            n = _norm_bf16(a_ref, g_ref)
            n_scr[...] = n
            nt_ref[...] = n.T
        r = jnp.maximum(_dot(n_scr[...], b_ref[...]), 0.0)
        o_ref[...] = r.astype(BF16)
        z = (r * r).astype(BF16)
        z_ref[...] = z
        zt_ref[...] = z.T

    return _call(
        body, name=name, grid=(T // tm, N // tn),
        in_specs=[pl.BlockSpec((tm, K), lambda i, j: (i, 0)), pl.BlockSpec((1, K), lambda i, j: (0, 0)),
                  pl.BlockSpec((None, K, tn), lambda i, j: (j // per, 0, j % per))],
        out_specs=[pl.BlockSpec((K, tm), lambda i, j: (0, i)), pl.BlockSpec((tm, tn), lambda i, j: (i, j)),
                   pl.BlockSpec((tm, tn), lambda i, j: (i, j)), pl.BlockSpec((tn, tm), lambda i, j: (j, i))],
        out_shape=[jax.ShapeDtypeStruct((K, T), BF16), jax.ShapeDtypeStruct((T, N), BF16),
                   jax.ShapeDtypeStruct((T, N), BF16), jax.ShapeDtypeStruct((N, T), BF16)],
        scratch_shapes=[pltpu.VMEM((tm, K), BF16)],
        sem=("parallel", "arbitrary"), rides=rides,
    )(a, g, b)


def _grad_pair(at, at_sib, b, b_sib, *, cols_sharded, tmo, tk, name, shards=None, into=None, rides=()):
    S, _, hr, T = at.shape
    C = b.shape[-1] // N_CHIPS if cols_sharded else b.shape[-1]
    nk = T // tk

    def shard(s):
        if shards is None:
            return s
        x, y = lax.axis_index("x"), lax.axis_index("y")
        first, second = ((2 * (1 - x) + y, 2 * x + (1 - y)) if shards == "near" else (2 * (1 - x) + (1 - y), 2 * x + y))
        return jnp.where(s == 0, first, second)

    a_sel = (lambda s: 0) if cols_sharded else shard
    b_sel = shard if cols_sharded else (lambda s: 0)
    if b.ndim == 3:
        b_spec = pl.BlockSpec((None, tk, C), lambda s, i, k: (0, k, b_sel(s)))
    else:
        b_spec = pl.BlockSpec((tk, C), lambda s, i, k: (k, b_sel(s)))
    n_into = 0 if into is None else 1

    def body(a_ref, as_ref, b_ref, bs_ref, *rest):
        o_ref, ob_ref = rest[n_into:]
        k = pl.program_id(2)
        p = _dot(a_ref[...], b_ref[...]) + _dot(as_ref[...], bs_ref[...])

        @pl.when(k == 0)
        def _():
            o_ref[...] = p

        @pl.when(k > 0)
        def _():
            o_ref[...] += p

        @pl.when(k == nk - 1)
        def _():
            ob_ref[...] = o_ref[...].astype(BF16)

    out = pl.BlockSpec((None, tmo, C), lambda s, i, k: (shard(s), i, 0))
    held = [pl.BlockSpec(memory_space=pl.ANY)] * n_into
    return _call(
        body, name=name, grid=(N_CHIPS if shards is None else 2, hr // tmo, nk),
        in_specs=[pl.BlockSpec((None, None, tmo, tk), lambda s, i, k: (a_sel(s), lax.axis_index("c"), i, k)),
                  pl.BlockSpec((None, tmo, tk), lambda s, i, k: (a_sel(s), i, k)),
                  b_spec, pl.BlockSpec((tk, C), lambda s, i, k: (k, b_sel(s)))] + held,
        out_specs=[out, out],
        out_shape=[jax.ShapeDtypeStruct((N_CHIPS, hr, C), F32), jax.ShapeDtypeStruct((N_CHIPS, hr, C), BF16)],
        sem=("parallel", "parallel", "arbitrary"), rides=rides, aliases={4: 0} if into is not None else None,
    )(at, at_sib, b, b_sib, *([into] if into is not None else []))


def _grad_pair_merged(at, at_sib, b, b_sib, *, tk, name, rides=()):
    S, _, hr, T = at.shape
    C = b.shape[-1]
    nk = T // tk

    def body(a_ref, as_ref, b_ref, bs_ref, o_ref, ob_ref):
        k = pl.program_id(0)
        p = (_dot(a_ref[...].reshape(S * hr, tk), b_ref[...])
             + _dot(as_ref[...].reshape(S * hr, tk), bs_ref[...])).reshape(S, hr, C)

        @pl.when(k == 0)
        def _():
            o_ref[...] = p

        @pl.when(k > 0)
        def _():
            o_ref[...] += p

        @pl.when(k == nk - 1)
        def _():
            ob_ref[...] = o_ref[...].astype(BF16)

    out = pl.BlockSpec((S, hr, C), lambda k: (0, 0, 0))
    return _call(
        body, name=name, grid=(nk,),
        in_specs=[pl.BlockSpec((S, None, hr, tk), lambda k: (0, lax.axis_index("c"), 0, k)),
                  pl.BlockSpec((S, hr, tk), lambda k: (0, 0, k)),
                  pl.BlockSpec((tk, C), lambda k: (k, 0)), pl.BlockSpec((tk, C), lambda k: (k, 0))],
        out_specs=[out, out],
        out_shape=[jax.ShapeDtypeStruct((S, hr, C), F32), jax.ShapeDtypeStruct((S, hr, C), BF16)],
        sem=("arbitrary",), rides=rides,
    )(at, at_sib, b, b_sib)


def _matmul_parts(parts, b, *, tm, tn, name, rides=()):
    T = parts[0].shape[0]
    N = b.shape[1]
    offs = [sum(p.shape[1] for p in parts[:i]) for i in range(len(parts))]
    assert all(o % p.shape[1] == 0 for o, p in zip(offs, parts))

    def body(*refs):
        n = len(parts)
        acc = _dot(refs[0][...], refs[n][...])
        for i in range(1, n):
            acc = acc + _dot(refs[i][...], refs[n + i][...])
        refs[-1][...] = acc

    a_specs = [pl.BlockSpec((tm, p.shape[1]), lambda i, j: (i, 0)) for p in parts]
    b_specs = [pl.BlockSpec((p.shape[1], tn), lambda i, j, r=o // p.shape[1]: (r, j)) for o, p in zip(offs, parts)]
    return _call(
        body, name=name, grid=(T // tm, N // tn), in_specs=a_specs + b_specs,
        out_specs=pl.BlockSpec((tm, tn), lambda i, j: (i, j)), out_shape=jax.ShapeDtypeStruct((T, N), F32),
        sem=("parallel", "parallel"), rides=rides,
    )(*parts, *([b] * len(parts)))


def _to_bf16(v):
    return v.astype(BF16)


def _matmul_res(a, b, res, *, tm, tn, tk, prologue, name, rides=()):
    T, K = a.shape
    N = b.shape[1]

    def body(a_ref, b_ref, res_ref, o_ref):
        k = pl.program_id(2)
        p = _dot(prologue(a_ref[...]), b_ref[...])

        @pl.when(k == 0)
        def _():
            o_ref[...] = res_ref[...] + p

        @pl.when(k > 0)
        def _():
            o_ref[...] += p

    return _call(
        body, name=name, grid=(T // tm, N // tn, K // tk),
        in_specs=[pl.BlockSpec((tm, tk), lambda i, j, k: (i, k)), pl.BlockSpec((tk, tn), lambda i, j, k: (k, j)),
                  pl.BlockSpec((tm, tn), lambda i, j, k: (i, j))],
        out_specs=pl.BlockSpec((tm, tn), lambda i, j, k: (i, j)),
        out_shape=jax.ShapeDtypeStruct((T, N), F32),
        sem=("parallel", "parallel", "arbitrary"), rides=rides,
    )(a, b, res)


def _matmul_nt(a, b, *, tm, tn, tk, name, extra=None, epilogue=None, out_dtype=F32, rides=()):
    T, K = a.shape
    two = b.ndim == 3 and tk == 2 * b.shape[2]
    if two:
        N, ks = b.shape[1], b.shape[2]
        b_specs = [pl.BlockSpec((None, tn, ks), lambda i, j, k: (2 * k, j, 0)),
                   pl.BlockSpec((None, tn, ks), lambda i, j, k: (2 * k + 1, j, 0))]
    elif b.ndim == 3:
        per = b.shape[2] // tk
        N = b.shape[1]
        b_specs = [pl.BlockSpec((None, tn, tk), lambda i, j, k: (k // per, j, k % per))]
    else:
        N = b.shape[0]
        b_specs = [pl.BlockSpec((tn, tk), lambda i, j, k: (j, k))]
    nb = len(b_specs)
    nk = K // tk
    assert out_dtype == F32 or nk == 1
    in_specs = [pl.BlockSpec((tm, tk), lambda i, j, k: (i, k))] + b_specs
    args = [a] + [b] * nb
    if extra is not None:
        in_specs.append(pl.BlockSpec((tm, tn), lambda i, j, k: (i, j)))
        args.append(extra)

    def body(*refs):
        a_ref, b_ref = refs[0], refs[1]
        o_ref = refs[-1]
        if two:
            p = (_dot_nt(a_ref[:, :tk // 2].astype(BF16), refs[1][...])
                 + _dot_nt(a_ref[:, tk // 2:].astype(BF16), refs[2][...]))
        else:
            p = _dot_nt(a_ref[...].astype(BF16), b_ref[...])
        if nk == 1:
            if epilogue is not None:
                p = epilogue(p, refs[1 + nb][...])
            o_ref[...] = p.astype(out_dtype)
        else:
            k = pl.program_id(2)

            @pl.when(k == 0)
            def _():
                o_ref[...] = p

            @pl.when(k > 0)
            def _():
                o_ref[...] += p

    return _call(
        body, name=name, grid=(T // tm, N // tn, nk),
        in_specs=in_specs,
        out_specs=pl.BlockSpec((tm, tn), lambda i, j, k: (i, j)),
        out_shape=jax.ShapeDtypeStruct((T, N), out_dtype),
        sem=("parallel", "parallel", "arbitrary"), rides=rides,
    )(*args)


def _loss_bwd(h2, tgt, g, *, tm):
    T, D = h2.shape

    def body(h_hbm, t_hbm, g_ref, dh_ref, dhb_ref, dg_ref, loss_ref, hbuf, tbuf, sems):
        slot = _ring_fetch((h_hbm, t_hbm), (hbuf, tbuf), sems, pl.program_id(0), T // tm, tm)

        @pl.when(pl.program_id(0) == 0)
        def _():
            dg_ref[...] = jnp.zeros_like(dg_ref)
            loss_ref[...] = jnp.zeros_like(loss_ref)
        h = hbuf[slot]
        gg = g_ref[...]
        r = lax.rsqrt(jnp.mean(h * h, axis=-1, keepdims=True) + EPS)
        hn = h * r
        err = hn * gg - tbuf[slot]
        loss_ref[...] += 0.5 * jnp.sum(jnp.mean(err * err, axis=-1, keepdims=True), axis=0, keepdims=True)
        dy = err * (1.0 / D)
        dg_ref[...] += jnp.sum(dy * hn, axis=0, keepdims=True)
        w = dy * gg
        dh = r * w - h * ((r * r * r) * jnp.mean(w * h, axis=-1, keepdims=True))
        dh_ref[...] = dh
        dhb_ref[...] = dh.astype(BF16)

    tile = pl.BlockSpec((tm, D), lambda i: (i, 0))
    any_spec = pl.BlockSpec(memory_space=pl.ANY)
    return pl.pallas_call(
        body, name="loss_bwd", grid=(T // tm,),
        in_specs=[any_spec, any_spec, pl.BlockSpec((1, D), lambda i: (0, 0))],
        out_specs=[tile, tile, pl.BlockSpec((1, D), lambda i: (0, 0)), pl.BlockSpec((1, 1), lambda i: (0, 0))],
        out_shape=[jax.ShapeDtypeStruct((T, D), F32), jax.ShapeDtypeStruct((T, D), BF16),
                   jax.ShapeDtypeStruct((1, D), F32), jax.ShapeDtypeStruct((1, 1), F32)],
        scratch_shapes=[pltpu.VMEM((RING, tm, D), F32)] * 2 + [pltpu.SemaphoreType.DMA((2, RING))],
        compiler_params=_params(("arbitrary",)),
    )(h2, tgt, g)


RING = 3


def _ring_fetch(srcs, bufs, sems, step, n_steps, tm):
    def copies(s, slot):
        rows = pl.ds(pl.multiple_of(s * tm, 8), tm)
        return [pltpu.make_async_copy(src.at[rows, :], buf.at[slot], sems.at[k, slot])
                for k, (src, buf) in enumerate(zip(srcs, bufs))]

    @pl.when(step == 0)
    def _():
        for first in range(min(RING - 1, n_steps)):
            for c in copies(first, first):
                c.start()

    @pl.when(step + RING - 1 < n_steps)
    def _():
        for c in copies(step + RING - 1, (step + RING - 1) % RING):
            c.start()

    slot = step % RING
    for c in copies(step, slot):
        c.wait()
    return slot


def _rms_bwd_res(dn, h, g, dres, *, tm, name, bf16_copy=True, rides=()):
    T, D = h.shape

    def body(dn_hbm, h_hbm, g_ref, dres_hbm, dh_ref, *rest):
        dnbuf, hbuf, rbuf, sems = rest[-4:]
        dg_ref = rest[-5]
        slot = _ring_fetch((dn_hbm, h_hbm, dres_hbm), (dnbuf, hbuf, rbuf), sems, pl.program_id(0), T // tm, tm)

        @pl.when(pl.program_id(0) == 0)
        def _():
            dg_ref[...] = jnp.zeros_like(dg_ref)
        h_ = hbuf[slot]
        dn_ = dnbuf[slot]
        dh, r = _rms_bwd(dn_, h_, g_ref[...])
        dg_ref[...] += jnp.sum(dn_ * (h_ * r), axis=0, keepdims=True)
        dh = rbuf[slot] + dh
        dh_ref[...] = dh
        if bf16_copy:
            rest[0][...] = dh.astype(BF16)

    tile = pl.BlockSpec((tm, D), lambda i: (i, 0))
    row = pl.BlockSpec((1, D), lambda i: (0, 0))
    any_spec = pl.BlockSpec(memory_space=pl.ANY)
    copy_spec = [tile] if bf16_copy else []
    copy_shape = [jax.ShapeDtypeStruct((T, D), BF16)] if bf16_copy else []
    return _call(
        body, name=name, grid=(T // tm,),
        in_specs=[any_spec, any_spec, row, any_spec], out_specs=[tile] + copy_spec + [row],
        out_shape=[jax.ShapeDtypeStruct((T, D), F32)] + copy_shape + [jax.ShapeDtypeStruct((1, D), F32)],
        scratch_shapes=[pltpu.VMEM((RING, tm, D), F32)] * 3 + [pltpu.SemaphoreType.DMA((3, RING))],
        sem=("arbitrary",), rides=rides,
    )(dn, h, g, dres)


def _rel_distance():
    i = lax.broadcasted_iota(jnp.int32, (CHUNK, 2 * CHUNK), 0)
    j = lax.broadcasted_iota(jnp.int32, (CHUNK, 2 * CHUNK), 1)
    return i + CHUNK - j


def _bias_build(table):
    def body(tab_ref, o_ref):
        rel = _rel_distance()
        j = lax.broadcasted_iota(jnp.int32, (CHUNK, 2 * CHUNK), 1)
        band = (rel >= 0) & (rel < CHUNK)
        ge = [rel >= t for t in BUCKET_THR]
        for h in range(B_HEADS):
            cur = jnp.full((CHUNK, 2 * CHUNK), tab_ref[0, h], F32)
            for b in range(1, N_BUCKETS):
                cur = jnp.where(ge[b - 1], tab_ref[b, h], cur)
            o_ref[0, h] = jnp.where(band & (j >= CHUNK), cur, NEG)
            o_ref[1, h] = jnp.where(band, cur, NEG)

    return pl.pallas_call(
        body, name="bias_build",
        in_specs=[pl.BlockSpec(memory_space=pltpu.SMEM)],
        out_specs=pl.BlockSpec(memory_space=pltpu.VMEM),
        out_shape=jax.ShapeDtypeStruct((2, B_HEADS, CHUNK, 2 * CHUNK), F32),
    )(table)


def _bias_grad(dbias, rides=()):
    def body(db_ref, o_ref, acc_ref):
        rel = _rel_distance()
        lo = [0] + BUCKET_THR
        hi = BUCKET_THR + [CHUNK]
        for b in range(N_BUCKETS):
            m = (rel >= lo[b]) & (rel < hi[b])
            for h in range(B_HEADS):
                row = b * B_HEADS + h
                acc_ref[row:row + 1, :] = jnp.sum(jnp.where(m, db_ref[h], 0.0), axis=0, keepdims=True)
        o_ref[...] = jnp.sum(acc_ref[...], axis=1, keepdims=True)

    return _call(
        body, name="bias_grad", grid=(1,),
        in_specs=[pl.BlockSpec(dbias.shape, lambda i: (0, 0, 0))],
        out_specs=pl.BlockSpec((N_BUCKETS * B_HEADS, 1), lambda i: (0, 0)),
        out_shape=jax.ShapeDtypeStruct((N_BUCKETS * B_HEADS, 1), F32),
        scratch_shapes=[pltpu.VMEM((N_BUCKETS * B_HEADS, 2 * CHUNK), F32)],
        sem=("arbitrary",), rides=rides,
    )(dbias)


def _causal_mask():
    t = lax.broadcasted_iota(jnp.int32, (CHUNK, CHUNK), 0)
    s = lax.broadcasted_iota(jnp.int32, (CHUNK, CHUNK), 1)
    return s <= t


def _gate_forward(u, v, lg, lb, wc, bs):
    ug = _gelu(u)
    vg = _gelu(v)
    mu = jnp.mean(vg, axis=-1, keepdims=True)
    xc = vg - mu
    rstd = lax.rsqrt(jnp.mean(xc * xc, axis=-1, keepdims=True) + EPS)
    xhat = xc * rstd
    vl = (xhat * lg + lb).astype(BF16)
    mixed = _dot(wc, vl) + bs
    return ug, xhat, rstd, vl, mixed


def _softmax_scores(qk, bias, sink):
    s = qk + bias
    m = jnp.maximum(jnp.max(s, axis=-1, keepdims=True), sink)
    p = jnp.exp(s - m)
    e_sink = jnp.exp(sink - m)
    inv = 1.0 / (jnp.sum(p, axis=-1, keepdims=True) + e_sink)
    return p * inv, e_sink * inv


PAIRS = Q_PER_KV // 2


def _head(g, pr, e):
    return g * Q_PER_KV + 2 * pr + e


def _stack_pairs(ref, g, col0=0):
    w = 2 * HEAD_DIM
    return jnp.concatenate([ref[:, col0 + (g * PAIRS + pr) * w:col0 + (g * PAIRS + pr + 1) * w] for pr in range(PAIRS)],
                           axis=0)


def _low_lanes():
    return lax.broadcasted_iota(jnp.int32, (2 * CHUNK, 2 * HEAD_DIM), 1) < HEAD_DIM


def _band_operands(kv_prev, kv_cur):
    band = jnp.concatenate([kv_prev, kv_cur], axis=0)
    low = _low_lanes()
    ops = []
    for cat in (band[:, :KV_WIDTH], band[:, KV_WIDTH:]):
        rol = pltpu.roll(cat, HEAD_DIM, 1)
        ops.append([[jnp.where(low if e == 0 else ~low, cat if g == e else rol, 0.0).astype(BF16) for e in range(2)]
                    for g in range(2)])
    return ops


def _mixer_fwd(proj, lg, lb, wsp, bs_col, sinks, bias, ga, gb, rides=()):
    T = proj.shape[0]
    nb = T // CHUNK

    def body(u_ref, v_ref, q_ref, kvc_ref, kvp_ref, lg_ref, lb_ref, w_ref, bs_ref, sink_ref, bias_ref,
             ga_ref, gb_ref, mixed_ref, mixed_t_ref, ab_ref):
        causal = _causal_mask()
        ssq = jnp.zeros((CHUNK, 1), F32)
        for g in range(A_GROUPS):
            cols = slice(g * CHUNK, (g + 1) * CHUNK)
            wc = jnp.where(causal, w_ref[g], 0.0).astype(BF16)
            ug, _, _, _, mixed = _gate_forward(u_ref[:, cols], v_ref[:, cols], lg_ref[g:g + 1, :], lb_ref[g:g + 1, :],
                                               wc, bs_ref[g])
            a = ug * mixed
            ab_ref[:, cols] = a
            ssq = ssq + jnp.sum(a * a, axis=-1, keepdims=True)
        ra = lax.rsqrt(ssq * (1.0 / A_WIDTH) + EPS)
        mixed_ref[:, :A_WIDTH] = ((ab_ref[:, :A_WIDTH] * ra) * ga_ref[...]).astype(BF16)

        kops, vops = _band_operands(kvp_ref[...], kvc_ref[...])
        ssq = jnp.zeros((CHUNK, 1), F32)
        for g in range(B_HEADS // Q_PER_KV):
            qst = (_stack_pairs(q_ref, g) * SCALE).astype(BF16)
            o_st = jnp.zeros((PAIRS * CHUNK, 2 * HEAD_DIM), F32)
            for e in range(2):
                s_all = _dot_nt(qst, kops[g][e])
                ps = []
                for pr in range(PAIRS):
                    h = _head(g, pr, e)
                    p, _ = _softmax_scores(s_all[pr * CHUNK:(pr + 1) * CHUNK], bias_ref[h], sink_ref[0, h])
                    ps.append(p.astype(BF16))
                o_st = o_st + _dot(jnp.concatenate(ps, axis=0), vops[g][e])
            for pr in range(PAIRS):
                o = o_st[pr * CHUNK:(pr + 1) * CHUNK]
                c0 = A_WIDTH + (g * PAIRS + pr) * 2 * HEAD_DIM
                ab_ref[:, c0:c0 + 2 * HEAD_DIM] = o
                ssq = ssq + jnp.sum(o * o, axis=-1, keepdims=True)
        rb = lax.rsqrt(ssq * (1.0 / B_WIDTH) + EPS)
        mixed_ref[:, A_WIDTH:] = ((ab_ref[:, A_WIDTH:] * rb) * gb_ref[...]).astype(BF16)
        mixed_t_ref[...] = mixed_ref[...].T

    full = lambda *shape: pl.BlockSpec(shape, lambda n: (0,) * len(shape))
    return _call(
        body, name="mixer_fwd", grid=(nb,),
        in_specs=[pl.BlockSpec((CHUNK, A_WIDTH), lambda n: (n, 0)),
                  pl.BlockSpec((CHUNK, A_WIDTH), lambda n: (n, 1)),
                  pl.BlockSpec((CHUNK, B_WIDTH), lambda n: (n, 2)),
                  pl.BlockSpec((CHUNK, 2 * KV_WIDTH), lambda n: (n, 12)),
                  pl.BlockSpec((CHUNK, 2 * KV_WIDTH), lambda n: (jnp.maximum(n - 1, 0), 12)),
                  full(A_GROUPS, CHUNK), full(A_GROUPS, CHUNK), full(A_GROUPS, CHUNK, CHUNK), full(A_GROUPS, CHUNK, 1),
                  pl.BlockSpec(memory_space=pltpu.SMEM),
                  pl.BlockSpec((None, B_HEADS, CHUNK, 2 * CHUNK), lambda n: (jnp.minimum(n, 1), 0, 0, 0)),
                  full(1, A_WIDTH), full(1, B_WIDTH)],
        out_specs=[pl.BlockSpec((CHUNK, D_MODEL), lambda n: (n, 0)), pl.BlockSpec((D_MODEL, CHUNK), lambda n: (0, n)),
                   pl.BlockSpec((CHUNK, D_MODEL), lambda n: (n, 0))],
        out_shape=[jax.ShapeDtypeStruct((T, D_MODEL), BF16), jax.ShapeDtypeStruct((D_MODEL, T), BF16),
                   jax.ShapeDtypeStruct((T, D_MODEL), F32)],
        sem=("parallel",), rides=rides,
    )(proj, proj, proj, proj, proj, lg, lb, wsp, bs_col, sinks, bias, ga, gb)


def _gmlp_bwd(proj, ab, dmixed, ga, lg, lb, wsp, bs_col, rides=()):
    T = proj.shape[0]
    nb = T // CHUNK

    def body(u_ref, v_ref, a_ref, dna_ref, ga_ref, lg_ref, lb_ref, w_ref, bs_ref,
             dp_ref, dpt_ref, dga_ref, dw_ref, dbs_ref, dlg_ref, dlb_ref):
        @pl.when(pl.program_id(0) == 0)
        def _():
            for r in (dga_ref, dw_ref, dbs_ref, dlg_ref, dlb_ref):
                r[...] = jnp.zeros_like(r)
        causal = _causal_mask()
        a_all = a_ref[...]
        dna = dna_ref[...]
        da_all, ra = _rms_bwd(dna, a_all, ga_ref[...])
        dga_ref[...] += jnp.sum(dna * (a_all * ra), axis=0, keepdims=True)
        for g in range(A_GROUPS):
            cols = slice(g * CHUNK, (g + 1) * CHUNK)
            wc = jnp.where(causal, w_ref[g], 0.0).astype(BF16)
            lgg = lg_ref[g:g + 1, :]
            u = u_ref[:, cols]
            v = v_ref[:, cols]
            ug, xhat, rstd, vl, mixed = _gate_forward(u, v, lgg, lb_ref[g:g + 1, :], wc, bs_ref[g])
            da = da_all[:, cols]
            dug = da * mixed
            dmg = da * ug
            dmg_b = dmg.astype(BF16)
            dbs_ref[g] += jnp.sum(dmg, axis=-1, keepdims=True)
            dw_ref[g] += jnp.where(causal, _dot_nt(dmg_b, vl), 0.0)
            dvl = _dot_tn(wc, dmg_b)
            dlg_ref[g:g + 1, :] += jnp.sum(dvl * xhat, axis=0, keepdims=True)
            dlb_ref[g:g + 1, :] += jnp.sum(dvl, axis=0, keepdims=True)
            dxh = dvl * lgg
            dvg = rstd * (dxh - jnp.mean(dxh, axis=-1, keepdims=True)
                          - xhat * jnp.mean(dxh * xhat, axis=-1, keepdims=True))
            _, gu = _gelu_and_grad(u)
            _, gv = _gelu_and_grad(v)
            dp_ref[:, cols] = (dug * gu).astype(BF16)
            dp_ref[:, A_WIDTH + g * CHUNK:A_WIDTH + (g + 1) * CHUNK] = (dvg * gv).astype(BF16)
        dpt_ref[...] = dp_ref[...].T

    full = lambda *shape: pl.BlockSpec(shape, lambda n: (0,) * len(shape))
    return _call(
        body, name="gmlp_bwd", grid=(nb,),
        in_specs=[pl.BlockSpec((CHUNK, A_WIDTH), lambda n: (n, 0)),
                  pl.BlockSpec((CHUNK, A_WIDTH), lambda n: (n, 1)),
                  pl.BlockSpec((CHUNK, A_WIDTH), lambda n: (n, 0)),
                  pl.BlockSpec((CHUNK, A_WIDTH), lambda n: (n, 0)),
                  full(1, A_WIDTH), full(A_GROUPS, CHUNK), full(A_GROUPS, CHUNK), full(A_GROUPS, CHUNK, CHUNK),
                  full(A_GROUPS, CHUNK, 1)],
        out_specs=[pl.BlockSpec((CHUNK, 2 * A_WIDTH), lambda n: (n, 0)), pl.BlockSpec((2 * A_WIDTH, CHUNK), lambda n: (0, n)),
                   full(1, A_WIDTH), full(A_GROUPS, CHUNK, CHUNK), full(A_GROUPS, CHUNK, 1),
                   full(A_GROUPS, CHUNK), full(A_GROUPS, CHUNK)],
        out_shape=[jax.ShapeDtypeStruct((T, 2 * A_WIDTH), BF16), jax.ShapeDtypeStruct((2 * A_WIDTH, T), BF16),
                   jax.ShapeDtypeStruct((1, A_WIDTH), F32), jax.ShapeDtypeStruct((A_GROUPS, CHUNK, CHUNK), F32),
                   jax.ShapeDtypeStruct((A_GROUPS, CHUNK, 1), F32), jax.ShapeDtypeStruct((A_GROUPS, CHUNK), F32),
                   jax.ShapeDtypeStruct((A_GROUPS, CHUNK), F32)],
        sem=("arbitrary",), rides=rides,
    )(proj, proj, ab, dmixed, ga, lg, lb, wsp, bs_col)


def _attn_bwd(proj, ab, dmixed, gb, sinks, bias, rides=()):
    T = proj.shape[0]
    nb = T // CHUNK
    qn = lambda n: jnp.minimum(n, nb - 1)

    def body(q_ref, kvc_ref, kvp_ref, o_ref, dnb_ref, gb_ref, sink_ref, bias_ref,
             dq_ref, dkv_ref, dqt_ref, dkvt_ref, dgb_ref, dsink_ref, dbias_ref, carry_ref, sacc_ref):
        n = pl.program_id(0)

        @pl.when(n == 0)
        def _():
            carry_ref[...] = jnp.zeros_like(carry_ref)
            sacc_ref[...] = jnp.zeros_like(sacc_ref)
            dgb_ref[...] = jnp.zeros_like(dgb_ref)
            dbias_ref[...] = jnp.zeros_like(dbias_ref)

        @pl.when(n < nb)
        def _():
            o_all = o_ref[...]
            dnb = dnb_ref[...]
            do_all, rb = _rms_bwd(dnb, o_all, gb_ref[...])
            dgb_ref[...] += jnp.sum(dnb * (o_all * rb), axis=0, keepdims=True)
            kops, vops = _band_operands(kvp_ref[...], kvc_ref[...])
            low = _low_lanes()
            halves = []
            for g in range(B_HEADS // Q_PER_KV):
                qst = (_stack_pairs(q_ref, g) * SCALE).astype(BF16)
                dost = _stack_pairs(do_all, g).astype(BF16)
                dq_st = jnp.zeros((PAIRS * CHUNK, 2 * HEAD_DIM), F32)
                dk_e, dv_e = [], []
                for e in range(2):
                    s_all = _dot_nt(qst, kops[g][e])
                    dp_all = _dot_nt(dost, vops[g][e])
                    ps, dsrs = [], []
                    for pr in range(PAIRS):
                        h = _head(g, pr, e)
                        rows = slice(pr * CHUNK, (pr + 1) * CHUNK)
                        p, p_sink = _softmax_scores(s_all[rows], bias_ref[h], sink_ref[0, h])
                        dp = dp_all[rows]
                        delta = jnp.sum(p * dp, axis=-1, keepdims=True)
                        ds = p * (dp - delta)
                        sacc_ref[:, h:h + 1] += -(p_sink * delta)
                        dbias_ref[h] += ds
                        ps.append(p.astype(BF16))
                        dsrs.append(ds.astype(BF16))
                    dsr_all = jnp.concatenate(dsrs, axis=0)
                    dq_st = dq_st + _dot(dsr_all, kops[g][e])
                    dk_e.append(_dot_tn(dsr_all, qst))
                    dv_e.append(_dot_tn(jnp.concatenate(ps, axis=0), dost))
                for pr in range(PAIRS):
                    c0 = (g * PAIRS + pr) * 2 * HEAD_DIM
                    dq_ref[:, c0:c0 + 2 * HEAD_DIM] = (dq_st[pr * CHUNK:(pr + 1) * CHUNK] * SCALE).astype(BF16)
                halves.append((dk_e, dv_e))
            tiles = []
            for t in range(2):
                g0, g1 = halves[0][t], halves[1][t]
                tiles.append(jnp.where(low, g0[0] + pltpu.roll(g0[1], HEAD_DIM, 1), pltpu.roll(g1[0], HEAD_DIM, 1) + g1[1]))
            dband = jnp.concatenate(tiles, axis=1)
            dkv = (carry_ref[...] + dband[:CHUNK]).astype(BF16)
            dkv_ref[...] = dkv
            dkvt_ref[...] = dkv.T
            dqt_ref[...] = dq_ref[...].T
            carry_ref[...] = dband[CHUNK:]

        @pl.when(n == nb)
        def _():
            dkv = carry_ref[...].astype(BF16)
            dkv_ref[...] = dkv
            dkvt_ref[...] = dkv.T
            dsink_ref[...] = jnp.sum(sacc_ref[...], axis=0, keepdims=True)

    full = lambda *shape: pl.BlockSpec(shape, lambda n: (0,) * len(shape))
    return _call(
        body, name="attn_bwd", grid=(nb + 1,),
        in_specs=[pl.BlockSpec((CHUNK, B_WIDTH), lambda n: (qn(n), 2)),
                  pl.BlockSpec((CHUNK, 2 * KV_WIDTH), lambda n: (qn(n), 12)),
                  pl.BlockSpec((CHUNK, 2 * KV_WIDTH), lambda n: (jnp.maximum(qn(n) - 1, 0), 12)),
                  pl.BlockSpec((CHUNK, B_WIDTH), lambda n: (qn(n), 1)),
                  pl.BlockSpec((CHUNK, B_WIDTH), lambda n: (qn(n), 1)),
                  full(1, B_WIDTH), pl.BlockSpec(memory_space=pltpu.SMEM),
                  pl.BlockSpec((None, B_HEADS, CHUNK, 2 * CHUNK), lambda n: (jnp.minimum(n, 1), 0, 0, 0))],
        out_specs=[pl.BlockSpec((CHUNK, B_WIDTH), lambda n: (qn(n), 0)),
                   pl.BlockSpec((CHUNK, 2 * KV_WIDTH), lambda n: (jnp.maximum(n - 1, 0), 0)),
                   pl.BlockSpec((B_WIDTH, CHUNK), lambda n: (0, qn(n))),
                   pl.BlockSpec((2 * KV_WIDTH, CHUNK), lambda n: (0, jnp.maximum(n - 1, 0))),
                   full(1, B_WIDTH), full(1, B_HEADS), full(B_HEADS, CHUNK, 2 * CHUNK)],
        out_shape=[jax.ShapeDtypeStruct((T, B_WIDTH), BF16), jax.ShapeDtypeStruct((T, 2 * KV_WIDTH), BF16),
                   jax.ShapeDtypeStruct((B_WIDTH, T), BF16), jax.ShapeDtypeStruct((2 * KV_WIDTH, T), BF16),
                   jax.ShapeDtypeStruct((1, B_WIDTH), F32), jax.ShapeDtypeStruct((1, B_HEADS), F32),
                   jax.ShapeDtypeStruct((B_HEADS, CHUNK, 2 * CHUNK), F32)],
        scratch_shapes=[pltpu.VMEM((CHUNK, 2 * KV_WIDTH), F32), pltpu.VMEM((CHUNK, B_HEADS), F32)],
        sem=("arbitrary",), rides=rides,
    )(proj, proj, proj, ab, dmixed, gb, sinks, bias)


def _sq_relu_grad(acc, r):
    return acc * (2.0 * r.astype(F32))


def _chip_index():
    return (2 * lax.axis_index("x") + lax.axis_index("y")).astype(jnp.int32).reshape(1)


def _cast_into_slot(w, *, tm, name):
    _, R, C = w.shape

    def body(me_ref, w_ref, o_ref):
        del me_ref
        o_ref[...] = w_ref[...].astype(BF16)

    return pl.pallas_call(
        body, name=name,
        grid_spec=pltpu.PrefetchScalarGridSpec(
            num_scalar_prefetch=1, grid=(R // tm,),
            in_specs=[pl.BlockSpec((None, tm, C), lambda i, me: (0, i, 0))],
            out_specs=pl.BlockSpec((None, tm, C), lambda i, me: (me[0], i, 0))),
        out_shape=jax.ShapeDtypeStruct((N_CHIPS, R, C), BF16), compiler_params=_params(("parallel",)),
    )(_chip_index(), w)


def _cast_into_slots_carrying(ws, *, steps, name, rides):
    n = len(ws)

    def body(*refs):
        for w_ref, o_ref in zip(refs[:n], refs[n:]):
            o_ref[...] = w_ref[...].astype(BF16)

    me = lambda: 2 * lax.axis_index("x") + lax.axis_index("y")
    return _call(
        body, name=name, grid=(steps,),
        in_specs=[pl.BlockSpec((None, w.shape[1] // steps, w.shape[2]), lambda i: (0, i, 0)) for w in ws],
        out_specs=[pl.BlockSpec((None, w.shape[1] // steps, w.shape[2]), lambda i: (me(), i, 0)) for w in ws],
        out_shape=[jax.ShapeDtypeStruct((N_CHIPS,) + w.shape[1:], BF16) for w in ws], sem=("arbitrary",), rides=rides,
    )(*ws)


def _owner_total(gh, others, *, tm, name):
    _, hr, C = gh.shape

    def body(me_ref, g_ref, o_ref_in, out_ref):
        del me_ref
        acc = g_ref[...]
        for j in range(3):
            acc = acc + o_ref_in[j].astype(F32)
        out_ref[...] = acc

    return pl.pallas_call(
        body, name=name,
        grid_spec=pltpu.PrefetchScalarGridSpec(
            num_scalar_prefetch=1, grid=(hr // tm,),
            in_specs=[pl.BlockSpec((None, tm, C), lambda i, me: (me[0], i, 0)),
                      pl.BlockSpec((3, tm, C), lambda i, me: (0, i, 0))],
            out_specs=pl.BlockSpec((tm, C), lambda i, me: (i, 0))),
        out_shape=jax.ShapeDtypeStruct((hr, C), F32),
        compiler_params=_params(("parallel",)),
    )(_chip_index(), gh, others)


def _adamw_math(w, g, m, v):
    m = ADAM_B1 * m + (1.0 - ADAM_B1) * g
    v = ADAM_B2 * v + (1.0 - ADAM_B2) * (g * g)
    m_hat = m / (1.0 - ADAM_B1 ** ADAM_STEP)
    v_hat = v / (1.0 - ADAM_B2 ** ADAM_STEP)
    delta = -ADAM_LR * (m_hat / (jnp.sqrt(v_hat) + ADAM_EPS) + ADAM_WD * w)
    return delta, m, v


def _adamw_halves(w, own, got, m, v, *, tm, name, rides=()):
    _, R, C = w.shape
    nt = (R // 2) // tm

    n_steps, ring = 2 * nt, 3

    def body(w_hbm, own_hbm, got_hbm, m_hbm, v_hbm, g_ref, d_ref, nm_ref, nv_ref, wbuf, mbuf, vbuf, gbuf, sems):
        s = pl.program_id(0) * nt + pl.program_id(1)

        def copies(step, slot):
            rows = pl.ds(pl.multiple_of(step * tm, 8), tm)
            return [pltpu.make_async_copy(src.at[0, rows, :], buf.at[slot], sems.at[k, slot])
                    for k, (src, buf) in enumerate(((w_hbm, wbuf), (m_hbm, mbuf), (v_hbm, vbuf)))]

        def g_copy(step, slot, src):
            rows = pl.ds(pl.multiple_of((step % nt) * tm, 8), tm)
            return pltpu.make_async_copy(src.at[rows, :], gbuf.at[slot], sems.at[3, slot])

        def start(step, slot):
            for c in copies(step, slot):
                c.start()
            mine = (step // nt) == lax.axis_index("c")

            @pl.when(mine)
            def _():
                g_copy(step, slot, own_hbm).start()

            @pl.when(jnp.logical_not(mine))
            def _():
                g_copy(step, slot, got_hbm).start()

        @pl.when(s == 0)
        def _():
            for first in range(min(ring - 1, n_steps)):
                start(first, first)

        @pl.when(s + ring - 1 < n_steps)
        def _():
            start(s + ring - 1, (s + ring - 1) % ring)

        slot = s % ring
        for c in copies(s, slot):
            c.wait()
        g_copy(s, slot, own_hbm).wait()
        g = gbuf[slot]
        g_ref[...] = g
        d_ref[...], nm_ref[...], nv_ref[...] = _adamw_math(wbuf[slot], g, mbuf[slot], vbuf[slot])

    whole = pl.BlockSpec((None, tm, C), lambda h, i: (0, h * nt + i, 0))
    any_spec = pl.BlockSpec(memory_space=pl.ANY)
    return _call(
        body, name=name, grid=(2, nt), in_specs=[any_spec] * 5, out_specs=[whole] * 4,
        out_shape=[jax.ShapeDtypeStruct((1, R, C), F32)] * 4, sem=("arbitrary", "arbitrary"), rides=rides,
        scratch_shapes=[pltpu.VMEM((ring, tm, C), F32)] * 4 + [pltpu.SemaphoreType.DMA((4, ring))],
    )(w, own, got, m, v)


def _adamw_small(w, slots, m, v, *, name, rides=()):
    def body(w_ref, slots_ref, m_ref, v_ref, g_ref, d_ref, nm_ref, nv_ref):
        g = slots_ref[0]
        for d in range(1, N_DEV):
            g = g + slots_ref[d]
        g_ref[...] = g
        d_ref[...], nm_ref[...], nv_ref[...] = _adamw_math(w_ref[...], g, m_ref[...], v_ref[...])

    flat = pl.BlockSpec(w.shape, lambda i: (0, 0))
    return _call(
        body, name=name, grid=(1,), in_specs=[flat, pl.BlockSpec(slots.shape, lambda i: (0, 0, 0)), flat, flat],
        out_specs=[flat] * 4, out_shape=[jax.ShapeDtypeStruct(w.shape, F32)] * 4, sem=("arbitrary",), rides=rides,
    )(w, slots, m, v)


SMALL = ["rel_bias_table", "mix_norm_g", "gate_norm_g", "gate_norm_b", "w_spatial", "b_spatial", "attn_sinks",
         "out_norm_a_g", "out_norm_b_g", "ffn_norm_g", "final_norm_g"]
SMALL_A = ["gate_norm_g", "gate_norm_b", "w_spatial", "b_spatial", "out_norm_a_g"]
SMALL_B = ["rel_bias_table", "mix_norm_g", "attn_sinks", "out_norm_b_g", "ffn_norm_g", "final_norm_g"]
LARGE = ["w_in", "w_out", "w_up", "w_down"]
ROW_TILE = {"w_in": 208, "w_out": 256, "w_up": 256, "w_down": 256}
WEIGHTS = ["rel_bias_table", "mix_norm_g", "w_in", "gate_norm_g", "gate_norm_b", "w_spatial", "b_spatial", "attn_sinks",
           "out_norm_a_g", "out_norm_b_g", "w_out", "ffn_norm_g", "w_up", "w_down", "final_norm_g"]
PACK_UNIT = 8 * 128


def _pack(parts):
    rows = []
    for p in parts:
        flat = p.reshape(-1)
        pad = (-flat.shape[0]) % PACK_UNIT
        rows.append(jnp.pad(flat, (0, pad)).reshape(-1, 128))
    return jnp.concatenate(rows, axis=0)


def _unpack(packed, like):
    out, row = [], 0
    for p in like:
        n = math.prod(p.shape)
        nrows = (n + PACK_UNIT - 1) // PACK_UNIT * 8
        out.append(packed[row:row + nrows].reshape(-1)[:n].reshape(p.shape))
        row += nrows
    return out


def kernel(x, rel_bias_table, mix_norm_g, w_in, gate_norm_g, gate_norm_b, w_spatial, b_spatial, attn_sinks, out_norm_a_g, out_norm_b_g, w_out, ffn_norm_g, w_up, w_down, final_norm_g, loss_target, m_rel_bias_table, m_mix_norm_g, m_w_in, m_gate_norm_g, m_gate_norm_b, m_w_spatial, m_b_spatial, m_attn_sinks, m_out_norm_a_g, m_out_norm_b_g, m_w_out, m_ffn_norm_g, m_w_up, m_w_down, m_final_norm_g, v_rel_bias_table, v_mix_norm_g, v_w_in, v_gate_norm_g, v_gate_norm_b, v_w_spatial, v_b_spatial, v_attn_sinks, v_out_norm_a_g, v_out_norm_b_g, v_w_out, v_ffn_norm_g, v_w_up, v_w_down, v_final_norm_g):
    args = dict(locals())
    wts = {n: args[n] for n in WEIGHTS}
    mom = {n: args["m_" + n] for n in WEIGHTS}
    var = {n: args["v_" + n] for n in WEIGHTS}
    sp = {n: wts[n] for n in SMALL}
    x2, tgt = x[0], loss_target[0]
    T = x2.shape[0]
    tm = min(512, T)
    tl = min(1024, T)
    lg = sp["gate_norm_g"].reshape(A_GROUPS, CHUNK)
    lb = sp["gate_norm_b"].reshape(A_GROUPS, CHUNK)
    wsp = sp["w_spatial"].reshape(A_GROUPS, CHUNK, CHUNK)
    bs_col = sp["b_spatial"].reshape(A_GROUPS, CHUNK, 1)
    sinks = sp["attn_sinks"].reshape(1, B_HEADS)
    ga = sp["out_norm_a_g"].reshape(1, A_WIDTH)
    gb = sp["out_norm_b_g"].reshape(1, B_WIDTH)
    g1 = sp["mix_norm_g"].reshape(1, D_MODEL)
    g2 = sp["ffn_norm_g"].reshape(1, D_MODEL)
    gf = sp["final_norm_g"].reshape(1, D_MODEL)

    def owner_total(n, gh, others):
        return _owner_total(gh, others, tm=ROW_TILE[n], name="rs_owner_total_" + n)

    def halves_view(at, shards):
        return at.reshape(shards, 2, at.shape[0] // shards // 2, at.shape[1])

    for d in (wts, mom, var):
        d["w_in"] = jnp.swapaxes(d["w_in"], 1, 2)

    s_in = _cast_into_slot(wts["w_in"], tm=ROW_TILE["w_in"], name="cast_w_in")
    (s_out, s_up, s_down), ((g_in,),) = _cast_into_slots_carrying(
        [wts["w_out"], wts["w_up"], wts["w_down"]], steps=8, name="cast_w_rest",
        rides=[_ride_gather(s_in, chain=(0, 1, 1), chain_fracs=(0.3, 0.6))])
    win_t = g_in.reshape(PROJ_WIDTH, D_MODEL)
    bias = _bias_build(sp["rel_bias_table"])
    (n1, proj), ((g_out,), (s_up,)) = _norm_matmul_wide(
        x2, g1, win_t, tm=tm, tn=PROJ_WIDTH // 2, name="in_proj",
        rides=[_ride_gather(s_out, chain=(0, 1, 1), chain_fracs=(0.65, 0.85)), _ride_gather(s_up, s1=(0, 3, 8))])
    wo = g_out.reshape(A_WIDTH + B_WIDTH, D_MODEL)
    (mixed, mixed_t, ab), ((s_up,), (s_down,), (n1_sib,)) = _mixer_fwd(
        proj, lg, lb, wsp, bs_col, sinks, bias, ga, gb,
        rides=[_ride_gather(s_up, s2=(0, 3, 8), s1=(3, 8, 8)), _ride_gather(s_down, s1=(0, 2, 8)),
               _ride_to_sibling(n1, first=True)])
    mixed_t = halves_view(mixed_t, N_CHIPS)
    h1, ((wu,), (s_down,), (mixed_t_sib,)) = _matmul_res(
        mixed, wo, x2, tm=tl, tn=1024, tk=D_MODEL, prologue=_to_bf16, name="out_proj",
        rides=[_ride_gather(s_up, s3=(0, 3, 8), tail=(3, 8, 8), mid_frac=0.75), _ride_gather(s_down, s2=(0, 2, 8)),
               _ride_to_sibling(mixed_t, halves=True)])
    (n2t, zp, z2, z2t), ((g_down,),) = _norm_matmul_sq(
        h1, g2, wu, tm=tl, tn=1024, name="up_proj", rides=[_ride_gather(s_down, s3=(0, 2, 8), chain=(2, 8, 8), chain_fracs=(0.5, 0.8))])
    wd = g_down.reshape(D_FF, D_MODEL)
    n2t, z2t = halves_view(n2t, 1), halves_view(z2t, N_CHIPS)
    h2, ((n2t_sib,), (z2t_sib,)) = _matmul_res(
        z2, wd, h1, tm=tl, tn=1024, tk=4096, prologue=_to_bf16, name="down_proj",
        rides=[_ride_to_sibling(n2t, halves=True), _ride_to_sibling(z2t, halves=True)])

    dh2, dh2b, dgf, loss = _loss_bwd(h2, tgt, gf, tm=tm)
    dzp, ((dh2b_sib,),) = _matmul_nt(dh2b, wd, tm=tl, tn=1024, tk=D_MODEL, name="bwd_dz", extra=zp,
                                     epilogue=_sq_relu_grad, out_dtype=BF16, rides=[_ride_to_sibling(dh2b)])
    (gd, gdb), ((dzp_sib,),) = _grad_pair(z2t, z2t_sib, dh2b, dh2b_sib, cols_sharded=False, tmo=1024, tk=tl,
                                          name="grad_w_down", rides=[_ride_to_sibling(dzp)])
    (gu, gub), ((o_d,),) = _grad_pair(n2t, n2t_sib, dzp, dzp_sib, cols_sharded=True, tmo=1024, tk=tl,
                                      name="grad_w_up", rides=[_ride_scatter(gdb, None, (0, 7, 8))])
    dn2, ((o_d,), (o_u,)) = _matmul_nt(dzp, wu, tm=tl, tn=1024, tk=4096, name="bwd_dn2",
                                       rides=[_ride_scatter(gdb, o_d, (7, 8, 8)), _ride_scatter(gub, None, (0, 6, 8))])
    h_d = owner_total("w_down", gd, o_d)
    (dh1, dh1b, dg2), ((o_u,),) = _rms_bwd_res(dn2, h1, g2, dh2, tm=tm, name="ffn_norm_bwd",
                                               rides=[_ride_scatter(gub, o_u, (6, 7, 8))])
    dmixed, ((o_u,), (dh1b_sib,), (w_d,)) = _matmul_nt(
        dh1b, wo, tm=tl, tn=1024, tk=D_MODEL, name="bwd_dmixed",
        rides=[_ride_scatter(gub, o_u, (7, 8, 8)), _ride_to_sibling(dh1b), _ride_swap(h_d)])
    h_u = owner_total("w_up", gu, o_u)
    (go, gob), ((w_u,),) = _grad_pair_merged(mixed_t, mixed_t_sib, dh1b, dh1b_sib, tk=tl, name="grad_w_out",
                                             rides=[_ride_swap(h_u)])
    (duv, duv_t, dga, dwsp, dbs, dlg, dlb), ((o_o,),) = _gmlp_bwd(proj, ab, dmixed, ga, lg, lb, wsp, bs_col,
                                                                  rides=[_ride_scatter(gob)])
    h_o = owner_total("w_out", go, o_o)
    small = {"gate_norm_g": dlg, "gate_norm_b": dlb, "w_spatial": dwsp, "b_spatial": dbs, "out_norm_a_g": dga}
    hr_in = PROJ_WIDTH // N_CHIPS // 2
    (dq, dkv, dq_t, dkv_t, dgb, dsinks, dbias), ((slots_a,), (dproj_t_sib,)) = _attn_bwd(
        proj, ab, dmixed, gb, sinks, bias,
        rides=[_ride_small_to_all(_pack([small[n] for n in SMALL_A])), _ride_rows_to_sibling(duv_t, hr_in, 2, N_CHIPS)])
    dproj_t = halves_view(jnp.concatenate([duv_t, dq_t, dkv_t], axis=0), N_CHIPS)
    dtable, ((dproj_t_sib,),) = _bias_grad(
        dbias, rides=[_ride_to_sibling(dproj_t, halves=True, shards=(2, N_CHIPS), land=dproj_t_sib)])
    (gi, gib_near), ((w_o,),) = _grad_pair(
        dproj_t, dproj_t_sib, n1, n1_sib, cols_sharded=False, tmo=hr_in, tk=tl, name="grad_w_in_near", shards="near",
        rides=[_ride_swap(h_o)])
    (gi, gib_far), ((o_i,),) = _grad_pair(
        dproj_t, dproj_t_sib, n1, n1_sib, cols_sharded=False, tmo=hr_in, tk=tl, name="grad_w_in_far", shards="far",
        into=gi, rides=[_ride_scatter(gib_near, None, to=(0, 1))])
    dn1, ((o_i,),) = _matmul_parts([duv, dq, dkv], win_t, tm=tl, tn=1024, name="bwd_dn1",
                                   rides=[_ride_scatter(gib_far, o_i, to=(2,))])
    h_i = owner_total("w_in", gi, o_i)
    dx, dg1 = _rms_bwd_res(dn1, x2, g1, dh1, tm=tm, name="mix_norm_bwd", bf16_copy=False)
    small.update({"rel_bias_table": dtable.reshape(N_BUCKETS, B_HEADS), "mix_norm_g": dg1, "attn_sinks": dsinks,
                  "out_norm_b_g": dgb, "ffn_norm_g": dg2, "final_norm_g": dgf})
    out_g, out_d, out_m, out_v = {}, {}, {}, {}

    def adamw_small(names, slots, tag, rides=()):
        extra = [jnp.zeros((1, 1), F32)] if tag == "b" else []
        like = [wts[n] for n in names] + extra
        res = _adamw_small(_pack(like), slots, _pack([mom[n] for n in names] + extra),
                           _pack([var[n] for n in names] + extra), name="adamw_small_" + tag, rides=rides)
        res, carried = res if rides else (res, None)
        for store, packed in zip((out_g, out_d, out_m, out_v), res):
            for n, val in zip(names + ["loss"], _unpack(packed, like)):
                store[n] = val
        return carried

    (w_i,), (slots_b,) = adamw_small(
        SMALL_A, slots_a, "a", rides=[_ride_swap(h_i), _ride_small_to_all(_pack([small[n] for n in SMALL_B] + [loss]))])
    for n, h, s in zip(LARGE, [h_i, h_o, h_u, h_d], [w_i, w_o, w_u, w_d]):
        res = _adamw_halves(wts[n], h, s, mom[n], var[n], tm=ROW_TILE[n], name="adamw_" + n)
        if n == "w_in":
            res = [jnp.swapaxes(r, 1, 2) for r in res]
        out_g[n], out_d[n], out_m[n], out_v[n] = res
    adamw_small(SMALL_B, slots_b, "b")

    total = out_g["loss"][0, 0]
    return (total, dx[None], *[out_g[n] for n in WEIGHTS], *[out_d[n] for n in WEIGHTS],
            *[out_m[n] for n in WEIGHTS], *[out_v[n] for n in WEIGHTS])
```

```python
import math

import numpy as np
import jax
import jax.numpy as jnp
from jax import lax
from jax.experimental import pallas as pl
from jax.experimental.pallas import tpu as pltpu

F32 = jnp.float32
BF16 = jnp.bfloat16

D_MODEL = 2048
CHUNK = 128
A_GROUPS = 8
A_WIDTH = 1024
HEAD_DIM = 64
B_HEADS = 16
Q_PER_KV = 8
B_WIDTH = 1024
KV_WIDTH = 128
PROJ_WIDTH = 3328
D_FF = 8192
N_BUCKETS = 32
EPS = 1e-5
NEG = -1e30
SCALE = HEAD_DIM ** -0.5
N_CHIPS = 4
N_DEV = 8

ADAM_LR = 0.001
ADAM_B1 = 0.9
ADAM_B2 = 0.999
ADAM_EPS = 1e-08
ADAM_WD = 0.01
ADAM_STEP = 10

VMEM_LIMIT = 60 * 1024 * 1024
MESH = pl.DeviceIdType.MESH


def _bucket_thresholds():
    d = np.arange(CHUNK)
    n_exact = N_BUCKETS // 2
    relf = np.maximum(d, n_exact).astype(np.float64)
    large = n_exact + (np.log(relf / n_exact) / math.log(CHUNK / n_exact) * (N_BUCKETS - n_exact)).astype(np.int32)
    bucket = np.where(d < n_exact, d, np.minimum(large, N_BUCKETS - 1))
    return [int(np.min(d[bucket >= b])) for b in range(1, N_BUCKETS)]


BUCKET_THR = _bucket_thresholds()


def _params(sem=None):
    return pltpu.CompilerParams(dimension_semantics=sem, vmem_limit_bytes=VMEM_LIMIT)


def _gelu(x):
    c = math.sqrt(2.0 / math.pi)
    return 0.5 * x * (1.0 + jnp.tanh(c * (x + 0.044715 * (x * x * x))))


def _gelu_and_grad(x):
    c = math.sqrt(2.0 / math.pi)
    x2 = x * x
    t = jnp.tanh(c * (x + 0.044715 * (x2 * x)))
    g = 0.5 * x * (1.0 + t)
    dg = 0.5 * (1.0 + t) + 0.5 * x * (1.0 - t * t) * (c * (1.0 + 3.0 * 0.044715 * x2))
    return g, dg


def _dot(a, b):
    return jnp.dot(a, b, preferred_element_type=F32)


def _dot_nt(a, b):
    return lax.dot_general(a, b, (((1,), (1,)), ((), ())), preferred_element_type=F32)


def _dot_tn(a, b):
    return lax.dot_general(a, b, (((0,), (0,)), ((), ())), preferred_element_type=F32)


def _rms_bwd(dn, h, g):
    r = lax.rsqrt(jnp.mean(h * h, axis=-1, keepdims=True) + EPS)
    w = dn * g
    dh = r * w - h * ((r * r * r) * jnp.mean(w * h, axis=-1, keepdims=True))
    return dh, r


def _place():
    x, y, c = lax.axis_index("x"), lax.axis_index("y"), lax.axis_index("c")
    chips = [(1 - x, y), (x, 1 - y), (1 - x, 1 - y)]
    return x, y, c, chips


def _remote(src, dst, send_sem, recv_sem, to):
    return pltpu.make_async_remote_copy(src_ref=src, dst_ref=dst, send_sem=send_sem, recv_sem=recv_sem,
                                        device_id=to, device_id_type=MESH)


class _Ride:
    def __init__(self, args, out_shape, n_sem, start, finish, mids=(), aliases=None):
        self.args, self.out_shape, self.n_sem = list(args), list(out_shape), n_sem
        self.start, self.mids, self.finish = start, list(mids), finish
        self.aliases = dict(aliases or {})


def _call(body, *, name, grid, in_specs, out_specs, out_shape, scratch_shapes=(), sem=None, rides=(), aliases=None):
    single = not isinstance(out_shape, (list, tuple))
    out_specs = [out_specs] if single else list(out_specs)
    out_shape = [out_shape] if single else list(out_shape)
    n_in, n_out, n_scr = len(in_specs), len(out_shape), len(scratch_shapes)
    r_in = [len(r.args) for r in rides]
    r_out = [len(r.out_shape) for r in rides]
    any_spec = pl.BlockSpec(memory_space=pl.ANY)
    aliases, off_i, off_o = dict(aliases or {}), n_in, n_out
    for r in rides:
        for i, o in r.aliases.items():
            aliases[off_i + i] = off_o + o
        off_i += len(r.args)
        off_o += len(r.out_shape)
    steps = math.prod(grid)

    def wrapped(*refs):
        p = 0
        ins = refs[p:p + n_in]; p += n_in
        rins = refs[p:p + sum(r_in)]; p += sum(r_in)
        outs = refs[p:p + n_out]; p += n_out
        routs = refs[p:p + sum(r_out)]; p += sum(r_out)
        scr = refs[p:p + n_scr]; p += n_scr
        sems = refs[p:]
        parts, pi, po = [], 0, 0
        for k, r in enumerate(rides):
            parts.append((rins[pi:pi + r_in[k]], routs[po:po + r_out[k]], sems[2 * k], sems[2 * k + 1]))
            pi += r_in[k]
            po += r_out[k]
        lin = 0
        for d in range(len(grid)):
            lin = lin * grid[d] + pl.program_id(d)
        if rides:
            @pl.when(lin == 0)
            def _():
                for r, part in zip(rides, parts):
                    r.start(*part)
        body(*ins, *outs, *scr)
        for r, part in zip(rides, parts):
            for frac, fn in r.mids:
                @pl.when(lin == min(steps - 1, int(frac * steps)))
                def _(fn=fn, part=part):
                    fn(*part)
        if rides:
            @pl.when(lin == steps - 1)
            def _():
                for r, part in zip(rides, parts):
                    r.finish(*part)

    scratch = list(scratch_shapes)
    for r in rides:
        scratch += [pltpu.SemaphoreType.DMA((r.n_sem,)), pltpu.SemaphoreType.DMA((r.n_sem,))]
    if rides:
        sem = ("arbitrary",) * len(grid)
    res = pl.pallas_call(
        wrapped, name=name, grid=grid,
        in_specs=list(in_specs) + [any_spec] * sum(r_in),
        out_specs=out_specs + [any_spec] * sum(r_out),
        out_shape=out_shape + [s for r in rides for s in r.out_shape],
        scratch_shapes=scratch, input_output_aliases=aliases,
        compiler_params=_params(sem),
    )

    def run(*args):
        got = res(*args, *[a for r in rides for a in r.args])
        mine = got[0] if single else list(got[:n_out])
        if not rides:
            return mine
        rest, out = list(got[n_out:]), []
        for k in range(len(rides)):
            out.append(rest[:r_out[k]])
            rest = rest[r_out[k]:]
        return mine, out

    return run


def _ride_gather(slot, s1=None, s2=None, s3=None, tail=None, chain=None, mid_frac=0.6, chain_fracs=(0.35, 0.7)):
    half = slot.shape[1] // 2

    def rows(part, c, which=None):
        k0, k1, n = part
        count, first = (k1 - k0) * (half // n), c * half + k0 * (half // n)
        return pl.ds(first, count) if which is None else pl.ds(first + which * (count // 2), count // 2)

    def ids():
        x, y, c, _ = _place()
        return x, y, c, 2 * x + y, 2 * (1 - x) + y, 2 * x + (1 - y), 2 * (1 - x) + (1 - y)

    def copy(full, chip, r, ss, rs, k, to):
        piece = full.at[chip, r, :]
        return _remote(piece, piece, ss.at[k], rs.at[k], to)

    def to_neighbours(full, ss, rs, part, base):
        x, y, c, me, _, _, _ = ids()
        return [copy(full, me, rows(part, c), ss, rs, base, (1 - x, y, c)),
                copy(full, me, rows(part, c), ss, rs, base + 1, (x, 1 - y, c))]

    def from_neighbours(full, ss, rs, part, base):
        x, y, c, _, cx, cy, _ = ids()
        return [copy(full, cx, rows(part, c), ss, rs, base, (x, y, c)), copy(full, cy, rows(part, c), ss, rs, base + 1, (x, y, c))]

    def onward(full, ss, rs, part, base):
        x, y, c, _, cx, cy, _ = ids()
        return [copy(full, cx, rows(part, c, 0), ss, rs, base, (x, 1 - y, c)),
                copy(full, cy, rows(part, c, 1), ss, rs, base + 1, (1 - x, y, c))]

    def from_onward(full, ss, rs, part, base):
        x, y, c, _, _, _, cd = ids()
        return [copy(full, cd, rows(part, c, 0), ss, rs, base, (x, y, c)), copy(full, cd, rows(part, c, 1), ss, rs, base + 1, (x, y, c))]

    def to_sibling(full, ss, rs, part, base, diagonal):
        x, y, c, _, cx, cy, cd = ids()
        return [copy(full, chip, rows(part, c), ss, rs, base + j, (x, y, 1 - c))
                for j, chip in enumerate([cd] if diagonal else [cx, cy])]

    def from_sibling(full, ss, rs, part, base, diagonal):
        x, y, c, _, cx, cy, cd = ids()
        return [copy(full, chip, rows(part, 1 - c), ss, rs, base + j, (x, y, c))
                for j, chip in enumerate([cd] if diagonal else [cx, cy])]

    def start(ins, outs, ss, rs):
        full, cps = outs[0], []
        for part, base in ((s1, 0), (chain, 12)):
            if part is not None:
                cps += to_neighbours(full, ss, rs, part, base)
        for part, b_ici, b_sib in ((s2, 2, 4), (tail, 7, 9)):
            if part is not None:
                cps += onward(full, ss, rs, part, b_ici) + to_sibling(full, ss, rs, part, b_sib, False)
        if s3 is not None:
            cps += to_sibling(full, ss, rs, s3, 6, True)
        for cp in cps:
            cp.start()

    def second(part, b_in, b_ici, b_sib):
        def fn(ins, outs, ss, rs):
            for cp in from_neighbours(outs[0], ss, rs, part, b_in):
                cp.wait_recv()
            for cp in onward(outs[0], ss, rs, part, b_ici) + to_sibling(outs[0], ss, rs, part, b_sib, False):
                cp.start()
        return fn

    def third(part, b_ici, b_sib):
        def fn(ins, outs, ss, rs):
            for cp in from_onward(outs[0], ss, rs, part, b_ici):
                cp.wait_recv()
            for cp in to_sibling(outs[0], ss, rs, part, b_sib, True):
                cp.start()
        return fn

    mids = []
    if tail is not None:
        mids.append((mid_frac, third(tail, 7, 11)))
    if chain is not None:
        mids += [(chain_fracs[0], second(chain, 12, 14, 16)), (chain_fracs[1], third(chain, 14, 18))]

    def finish(ins, outs, ss, rs):
        full, got, sent = outs[0], [], []
        if s1 is not None:
            got += from_neighbours(full, ss, rs, s1, 0)
            sent += to_neighbours(full, ss, rs, s1, 0)
        if s2 is not None:
            got += from_onward(full, ss, rs, s2, 2) + from_sibling(full, ss, rs, s2, 4, False)
            sent += onward(full, ss, rs, s2, 2) + to_sibling(full, ss, rs, s2, 4, False)
        if s3 is not None:
            got += from_sibling(full, ss, rs, s3, 6, True)
            sent += to_sibling(full, ss, rs, s3, 6, True)
        if tail is not None:
            got += from_sibling(full, ss, rs, tail, 9, False) + from_sibling(full, ss, rs, tail, 11, True)
            sent += onward(full, ss, rs, tail, 7) + to_sibling(full, ss, rs, tail, 9, False) + to_sibling(full, ss, rs, tail, 11, True)
        if chain is not None:
            got += from_sibling(full, ss, rs, chain, 16, False) + from_sibling(full, ss, rs, chain, 18, True)
            sent += (to_neighbours(full, ss, rs, chain, 12) + onward(full, ss, rs, chain, 14)
                     + to_sibling(full, ss, rs, chain, 16, False) + to_sibling(full, ss, rs, chain, 18, True))
        for cp in got:
            cp.wait_recv()
        for cp in sent:
            cp.wait_send()

    return _Ride([slot], [jax.ShapeDtypeStruct(slot.shape, slot.dtype)], 19, start, finish, mids=mids, aliases={0: 0})


def _ride_scatter(q, land=None, part=(0, 1), to=(0, 1, 2)):
    k0, k1, n = part if len(part) == 3 else (part[0], part[0] + 1, part[1])
    rows_n = q.shape[1] // n
    rows = pl.ds(k0 * rows_n, (k1 - k0) * rows_n)

    def copies(ins, outs, ss, rs):
        x, y, c, chips = _place()
        return [_remote(ins[0].at[2 * chip[0] + chip[1], rows, :], outs[0].at[j, rows, :], ss.at[j], rs.at[j], (*chip, c))
                for j, chip in enumerate(chips) if j in to]

    def start(*a):
        for cp in copies(*a):
            cp.start()

    def finish(*a):
        for cp in copies(*a):
            cp.wait()

    shape = jax.ShapeDtypeStruct((3,) + q.shape[1:], q.dtype)
    if land is None:
        return _Ride([q], [shape], 3, start, finish)
    return _Ride([q, land], [shape], 3, start, finish, aliases={1: 0})


def _ride_to_sibling(a, halves=False, first=False, shards=None, land=None):
    s0, s1 = shards or (0, a.shape[0])

    def copy(ins, outs, ss, rs):
        x, y, c, _ = _place()
        if halves:
            src, dst = ins[0].at[s0:s1, 1 - c], outs[0].at[s0:s1]
        else:
            src, dst = (ins[0].at[0] if first else ins[0]), outs[0]
        return _remote(src, dst, ss.at[0], rs.at[0], (x, y, 1 - c))

    shape = (a.shape[0],) + a.shape[2:] if halves else (a.shape[1:] if first else a.shape)
    return _Ride([a] if land is None else [a, land], [jax.ShapeDtypeStruct(shape, a.dtype)], 1,
                 lambda *a_: copy(*a_).start(), lambda *a_: copy(*a_).wait(), aliases=None if land is None else {1: 0})


def _ride_rows_to_sibling(a, hr, shards, total):
    def copies(ins, outs, ss, rs):
        x, y, c, _ = _place()
        return [_remote(ins[0].at[pl.ds((2 * s + 1 - c) * hr, hr), :], outs[0].at[s], ss.at[s], rs.at[s], (x, y, 1 - c))
                for s in range(shards)]

    def start(*a_):
        for cp in copies(*a_):
            cp.start()

    def finish(*a_):
        for cp in copies(*a_):
            cp.wait()

    return _Ride([a], [jax.ShapeDtypeStruct((total, hr, a.shape[1]), a.dtype)], shards, start, finish)


def _ride_swap(h):
    def copy(ins, outs, ss, rs):
        x, y, c, _ = _place()
        return _remote(ins[0], outs[0], ss.at[0], rs.at[0], (x, y, 1 - c))

    return _Ride([h], [jax.ShapeDtypeStruct(h.shape, h.dtype)], 1,
                 lambda *a: copy(*a).start(), lambda *a: copy(*a).wait())


def _mesh_place(p):
    return (p // 4, (p // 2) % 2, p % 2)


def _ride_small_to_all(packed):
    def copies(ins, outs, ss, rs):
        x, y, c, _ = _place()
        me = 4 * x + 2 * y + c
        return [_remote(ins[0], outs[0].at[me], ss.at[k - 1], rs.at[k - 1], _mesh_place((me + k) % N_DEV))
                for k in range(1, N_DEV)]

    def own(ins, outs, ss, rs):
        x, y, c, _ = _place()
        return pltpu.make_async_copy(ins[0], outs[0].at[4 * x + 2 * y + c], ss.at[N_DEV - 1])

    def start(*a):
        own(*a).start()
        for cp in copies(*a):
            cp.start()

    def finish(ins, outs, ss, rs):
        x, y, c, _ = _place()
        me = 4 * x + 2 * y + c
        for k in range(1, N_DEV):
            _remote(ins[0], outs[0].at[(me + N_DEV - k) % N_DEV], ss.at[k - 1], rs.at[k - 1], (x, y, c)).wait_recv()
        for cp in copies(ins, outs, ss, rs):
            cp.wait_send()
        own(ins, outs, ss, rs).wait()

    return _Ride([packed], [jax.ShapeDtypeStruct((N_DEV,) + packed.shape, packed.dtype)], N_DEV, start, finish)


def _norm_bf16(a_ref, g_ref):
    xf = a_ref[...]
    r = lax.rsqrt(jnp.mean(xf * xf, axis=-1, keepdims=True) + EPS)
    return ((xf * r) * g_ref[...]).astype(BF16)


def _norm_matmul_wide(a, g, b, *, tm, tn, name, rides=()):
    T, K = a.shape
    N = b.shape[0]

    def body(a_ref, g_ref, b_ref, n_ref, o_ref):
        n = _norm_bf16(a_ref, g_ref)
        n_ref[...] = n
        o_ref[...] = _dot_nt(n, b_ref[...])

    return _call(
        body, name=name, grid=(N // tn, T // tm),
        in_specs=[pl.BlockSpec((tm, K), lambda j, i: (i, 0)), pl.BlockSpec((1, K), lambda j, i: (0, 0)),
                  pl.BlockSpec((tn, K), lambda j, i: (j, 0))],
        out_specs=[pl.BlockSpec((None, tm, K), lambda j, i: (j, i, 0)), pl.BlockSpec((tm, tn), lambda j, i: (i, j))],
        out_shape=[jax.ShapeDtypeStruct((N // tn, T, K), BF16), jax.ShapeDtypeStruct((T, N), F32)],
        sem=("arbitrary", "arbitrary"), rides=rides,
    )(a, g, b)


def _norm_matmul_sq(a, g, b, *, tm, tn, name, rides=()):
    T, K = a.shape
    per = b.shape[2] // tn
    N = b.shape[0] * b.shape[2]

    def body(a_ref, g_ref, b_ref, nt_ref, o_ref, z_ref, zt_ref, n_scr):
        @pl.when(pl.program_id(1) == 0)
        def _():
            n = _norm_bf16(a_ref, g_ref)
            n_scr[...] = n
            nt_ref[...] = n.T
        r = jnp.maximum(_dot(n_scr[...], b_ref[...]), 0.0)
        o_ref[...] = r.astype(BF16)
        z = (r * r).astype(BF16)
        z_ref[...] = z
        zt_ref[...] = z.T

    return _call(
        body, name=name, grid=(T // tm, N // tn),
        in_specs=[pl.BlockSpec((tm, K), lambda i, j: (i, 0)), pl.BlockSpec((1, K), lambda i, j: (0, 0)),
                  pl.BlockSpec((None, K, tn), lambda i, j: (j // per, 0, j % per))],
        out_specs=[pl.BlockSpec((K, tm), lambda i, j: (0, i)), pl.BlockSpec((tm, tn), lambda i, j: (i, j)),
                   pl.BlockSpec((tm, tn), lambda i, j: (i, j)), pl.BlockSpec((tn, tm), lambda i, j: (j, i))],
        out_shape=[jax.ShapeDtypeStruct((K, T), BF16), jax.ShapeDtypeStruct((T, N), BF16),
                   jax.ShapeDtypeStruct((T, N), BF16), jax.ShapeDtypeStruct((N, T), BF16)],
        scratch_shapes=[pltpu.VMEM((tm, K), BF16)],
        sem=("parallel", "arbitrary"), rides=rides,
    )(a, g, b)


def _grad_pair(at, at_sib, b, b_sib, *, cols_sharded, tmo, tk, name, shards=None, into=None, rides=()):
    S, _, hr, T = at.shape
    C = b.shape[-1] // N_CHIPS if cols_sharded else b.shape[-1]
    nk = T // tk

    def shard(s):
        if shards is None:
            return s
        x, y = lax.axis_index("x"), lax.axis_index("y")
        first, second = ((2 * (1 - x) + y, 2 * x + (1 - y)) if shards == "near" else (2 * (1 - x) + (1 - y), 2 * x + y))
        return jnp.where(s == 0, first, second)

    a_sel = (lambda s: 0) if cols_sharded else shard
    b_sel = shard if cols_sharded else (lambda s: 0)
    if b.ndim == 3:
        b_spec = pl.BlockSpec((None, tk, C), lambda s, i, k: (0, k, b_sel(s)))
    else:
        b_spec = pl.BlockSpec((tk, C), lambda s, i, k: (k, b_sel(s)))
    n_into = 0 if into is None else 1

    def body(a_ref, as_ref, b_ref, bs_ref, *rest):
        o_ref, ob_ref = rest[n_into:]
        k = pl.program_id(2)
        p = _dot(a_ref[...], b_ref[...]) + _dot(as_ref[...], bs_ref[...])

        @pl.when(k == 0)
        def _():
            o_ref[...] = p

        @pl.when(k > 0)
        def _():
            o_ref[...] += p

        @pl.when(k == nk - 1)
        def _():
            ob_ref[...] = o_ref[...].astype(BF16)

    out = pl.BlockSpec((None, tmo, C), lambda s, i, k: (shard(s), i, 0))
    held = [pl.BlockSpec(memory_space=pl.ANY)] * n_into
    return _call(
        body, name=name, grid=(N_CHIPS if shards is None else 2, hr // tmo, nk),
        in_specs=[pl.BlockSpec((None, None, tmo, tk), lambda s, i, k: (a_sel(s), lax.axis_index("c"), i, k)),
                  pl.BlockSpec((None, tmo, tk), lambda s, i, k: (a_sel(s), i, k)),
                  b_spec, pl.BlockSpec((tk, C), lambda s, i, k: (k, b_sel(s)))] + held,
        out_specs=[out, out],
        out_shape=[jax.ShapeDtypeStruct((N_CHIPS, hr, C), F32), jax.ShapeDtypeStruct((N_CHIPS, hr, C), BF16)],
        sem=("parallel", "parallel", "arbitrary"), rides=rides, aliases={4: 0} if into is not None else None,
    )(at, at_sib, b, b_sib, *([into] if into is not None else []))


def _grad_pair_merged(at, at_sib, b, b_sib, *, tk, name, rides=()):
    S, _, hr, T = at.shape
    C = b.shape[-1]
    nk = T // tk

    def body(a_ref, as_ref, b_ref, bs_ref, o_ref, ob_ref):
        k = pl.program_id(0)
        p = (_dot(a_ref[...].reshape(S * hr, tk), b_ref[...])
             + _dot(as_ref[...].reshape(S * hr, tk), bs_ref[...])).reshape(S, hr, C)

        @pl.when(k == 0)
        def _():
            o_ref[...] = p

        @pl.when(k > 0)
        def _():
            o_ref[...] += p

        @pl.when(k == nk - 1)
        def _():
            ob_ref[...] = o_ref[...].astype(BF16)

    out = pl.BlockSpec((S, hr, C), lambda k: (0, 0, 0))
    return _call(
        body, name=name, grid=(nk,),
        in_specs=[pl.BlockSpec((S, None, hr, tk), lambda k: (0, lax.axis_index("c"), 0, k)),
                  pl.BlockSpec((S, hr, tk), lambda k: (0, 0, k)),
                  pl.BlockSpec((tk, C), lambda k: (k, 0)), pl.BlockSpec((tk, C), lambda k: (k, 0))],
        out_specs=[out, out],
        out_shape=[jax.ShapeDtypeStruct((S, hr, C), F32), jax.ShapeDtypeStruct((S, hr, C), BF16)],
        sem=("arbitrary",), rides=rides,
    )(at, at_sib, b, b_sib)


def _matmul_parts(parts, b, *, tm, tn, name, rides=()):
    T = parts[0].shape[0]
    N = b.shape[1]
    offs = [sum(p.shape[1] for p in parts[:i]) for i in range(len(parts))]
    assert all(o % p.shape[1] == 0 for o, p in zip(offs, parts))

    def body(*refs):
        n = len(parts)
        acc = _dot(refs[0][...], refs[n][...])
        for i in range(1, n):
            acc = acc + _dot(refs[i][...], refs[n + i][...])
        refs[-1][...] = acc

    a_specs = [pl.BlockSpec((tm, p.shape[1]), lambda i, j: (i, 0)) for p in parts]
    b_specs = [pl.BlockSpec((p.shape[1], tn), lambda i, j, r=o // p.shape[1]: (r, j)) for o, p in zip(offs, parts)]
    return _call(
        body, name=name, grid=(T // tm, N // tn), in_specs=a_specs + b_specs,
        out_specs=pl.BlockSpec((tm, tn), lambda i, j: (i, j)), out_shape=jax.ShapeDtypeStruct((T, N), F32),
        sem=("parallel", "parallel"), rides=rides,
    )(*parts, *([b] * len(parts)))


def _to_bf16(v):
    return v.astype(BF16)


def _matmul_res(a, b, res, *, tm, tn, tk, prologue, name, rides=()):
    T, K = a.shape
    N = b.shape[1]

    def body(a_ref, b_ref, res_ref, o_ref):
        k = pl.program_id(2)
        p = _dot(prologue(a_ref[...]), b_ref[...])

        @pl.when(k == 0)
        def _():
            o_ref[...] = res_ref[...] + p

        @pl.when(k > 0)
        def _():
            o_ref[...] += p

    return _call(
        body, name=name, grid=(T // tm, N // tn, K // tk),
        in_specs=[pl.BlockSpec((tm, tk), lambda i, j, k: (i, k)), pl.BlockSpec((tk, tn), lambda i, j, k: (k, j)),
                  pl.BlockSpec((tm, tn), lambda i, j, k: (i, j))],
        out_specs=pl.BlockSpec((tm, tn), lambda i, j, k: (i, j)),
        out_shape=jax.ShapeDtypeStruct((T, N), F32),
        sem=("parallel", "parallel", "arbitrary"), rides=rides,
    )(a, b, res)


def _matmul_nt(a, b, *, tm, tn, tk, name, extra=None, epilogue=None, out_dtype=F32, rides=()):
    T, K = a.shape
    two = b.ndim == 3 and tk == 2 * b.shape[2]
    if two:
        N, ks = b.shape[1], b.shape[2]
        b_specs = [pl.BlockSpec((None, tn, ks), lambda i, j, k: (2 * k, j, 0)),
                   pl.BlockSpec((None, tn, ks), lambda i, j, k: (2 * k + 1, j, 0))]
    elif b.ndim == 3:
        per = b.shape[2] // tk
        N = b.shape[1]
        b_specs = [pl.BlockSpec((None, tn, tk), lambda i, j, k: (k // per, j, k % per))]
    else:
        N = b.shape[0]
        b_specs = [pl.BlockSpec((tn, tk), lambda i, j, k: (j, k))]
    nb = len(b_specs)
    nk = K // tk
    assert out_dtype == F32 or nk == 1
    in_specs = [pl.BlockSpec((tm, tk), lambda i, j, k: (i, k))] + b_specs
    args = [a] + [b] * nb
    if extra is not None:
        in_specs.append(pl.BlockSpec((tm, tn), lambda i, j, k: (i, j)))
        args.append(extra)

    def body(*refs):
        a_ref, b_ref = refs[0], refs[1]
        o_ref = refs[-1]
        if two:
            p = (_dot_nt(a_ref[:, :tk // 2].astype(BF16), refs[1][...])
                 + _dot_nt(a_ref[:, tk // 2:].astype(BF16), refs[2][...]))
        else:
            p = _dot_nt(a_ref[...].astype(BF16), b_ref[...])
        if nk == 1:
            if epilogue is not None:
                p = epilogue(p, refs[1 + nb][...])
            o_ref[...] = p.astype(out_dtype)
        else:
            k = pl.program_id(2)

            @pl.when(k == 0)
            def _():
                o_ref[...] = p

            @pl.when(k > 0)
            def _():
                o_ref[...] += p

    return _call(
        body, name=name, grid=(T // tm, N // tn, nk),
        in_specs=in_specs,
        out_specs=pl.BlockSpec((tm, tn), lambda i, j, k: (i, j)),
        out_shape=jax.ShapeDtypeStruct((T, N), out_dtype),
        sem=("parallel", "parallel", "arbitrary"), rides=rides,
    )(*args)


def _loss_bwd(h2, tgt, g, *, tm):
    T, D = h2.shape

    def body(h_hbm, t_hbm, g_ref, dh_ref, dhb_ref, dg_ref, loss_ref, hbuf, tbuf, sems):
        slot = _ring_fetch((h_hbm, t_hbm), (hbuf, tbuf), sems, pl.program_id(0), T // tm, tm)

        @pl.when(pl.program_id(0) == 0)
        def _():
            dg_ref[...] = jnp.zeros_like(dg_ref)
            loss_ref[...] = jnp.zeros_like(loss_ref)
        h = hbuf[slot]
        gg = g_ref[...]
        r = lax.rsqrt(jnp.mean(h * h, axis=-1, keepdims=True) + EPS)
        hn = h * r
        err = hn * gg - tbuf[slot]
        loss_ref[...] += 0.5 * jnp.sum(jnp.mean(err * err, axis=-1, keepdims=True), axis=0, keepdims=True)
        dy = err * (1.0 / D)
        dg_ref[...] += jnp.sum(dy * hn, axis=0, keepdims=True)
        w = dy * gg
        dh = r * w - h * ((r * r * r) * jnp.mean(w * h, axis=-1, keepdims=True))
        dh_ref[...] = dh
        dhb_ref[...] = dh.astype(BF16)

    tile = pl.BlockSpec((tm, D), lambda i: (i, 0))
    any_spec = pl.BlockSpec(memory_space=pl.ANY)
    return pl.pallas_call(
        body, name="loss_bwd", grid=(T // tm,),
        in_specs=[any_spec, any_spec, pl.BlockSpec((1, D), lambda i: (0, 0))],
        out_specs=[tile, tile, pl.BlockSpec((1, D), lambda i: (0, 0)), pl.BlockSpec((1, 1), lambda i: (0, 0))],
        out_shape=[jax.ShapeDtypeStruct((T, D), F32), jax.ShapeDtypeStruct((T, D), BF16),
                   jax.ShapeDtypeStruct((1, D), F32), jax.ShapeDtypeStruct((1, 1), F32)],
        scratch_shapes=[pltpu.VMEM((RING, tm, D), F32)] * 2 + [pltpu.SemaphoreType.DMA((2, RING))],
        compiler_params=_params(("arbitrary",)),
    )(h2, tgt, g)


RING = 3


def _ring_fetch(srcs, bufs, sems, step, n_steps, tm):
    def copies(s, slot):
        rows = pl.ds(pl.multiple_of(s * tm, 8), tm)
        return [pltpu.make_async_copy(src.at[rows, :], buf.at[slot], sems.at[k, slot])
                for k, (src, buf) in enumerate(zip(srcs, bufs))]

    @pl.when(step == 0)
    def _():
        for first in range(min(RING - 1, n_steps)):
            for c in copies(first, first):
                c.start()

    @pl.when(step + RING - 1 < n_steps)
    def _():
        for c in copies(step + RING - 1, (step + RING - 1) % RING):
            c.start()

    slot = step % RING
    for c in copies(step, slot):
        c.wait()
    return slot


def _rms_bwd_res(dn, h, g, dres, *, tm, name, bf16_copy=True, rides=()):
    T, D = h.shape

    def body(dn_hbm, h_hbm, g_ref, dres_hbm, dh_ref, *rest):
        dnbuf, hbuf, rbuf, sems = rest[-4:]
        dg_ref = rest[-5]
        slot = _ring_fetch((dn_hbm, h_hbm, dres_hbm), (dnbuf, hbuf, rbuf), sems, pl.program_id(0), T // tm, tm)

        @pl.when(pl.program_id(0) == 0)
        def _():
            dg_ref[...] = jnp.zeros_like(dg_ref)
        h_ = hbuf[slot]
        dn_ = dnbuf[slot]
        dh, r = _rms_bwd(dn_, h_, g_ref[...])
        dg_ref[...] += jnp.sum(dn_ * (h_ * r), axis=0, keepdims=True)
        dh = rbuf[slot] + dh
        dh_ref[...] = dh
        if bf16_copy:
            rest[0][...] = dh.astype(BF16)

    tile = pl.BlockSpec((tm, D), lambda i: (i, 0))
    row = pl.BlockSpec((1, D), lambda i: (0, 0))
    any_spec = pl.BlockSpec(memory_space=pl.ANY)
    copy_spec = [tile] if bf16_copy else []
    copy_shape = [jax.ShapeDtypeStruct((T, D), BF16)] if bf16_copy else []
    return _call(
        body, name=name, grid=(T // tm,),
        in_specs=[any_spec, any_spec, row, any_spec], out_specs=[tile] + copy_spec + [row],
        out_shape=[jax.ShapeDtypeStruct((T, D), F32)] + copy_shape + [jax.ShapeDtypeStruct((1, D), F32)],
        scratch_shapes=[pltpu.VMEM((RING, tm, D), F32)] * 3 + [pltpu.SemaphoreType.DMA((3, RING))],
        sem=("arbitrary",), rides=rides,
    )(dn, h, g, dres)


def _rel_distance():
    i = lax.broadcasted_iota(jnp.int32, (CHUNK, 2 * CHUNK), 0)
    j = lax.broadcasted_iota(jnp.int32, (CHUNK, 2 * CHUNK), 1)
    return i + CHUNK - j


def _bias_build(table):
    def body(tab_ref, o_ref):
        rel = _rel_distance()
        j = lax.broadcasted_iota(jnp.int32, (CHUNK, 2 * CHUNK), 1)
        band = (rel >= 0) & (rel < CHUNK)
        ge = [rel >= t for t in BUCKET_THR]
        for h in range(B_HEADS):
            cur = jnp.full((CHUNK, 2 * CHUNK), tab_ref[0, h], F32)
            for b in range(1, N_BUCKETS):
                cur = jnp.where(ge[b - 1], tab_ref[b, h], cur)
            o_ref[0, h] = jnp.where(band & (j >= CHUNK), cur, NEG)
            o_ref[1, h] = jnp.where(band, cur, NEG)

    return pl.pallas_call(
        body, name="bias_build",
        in_specs=[pl.BlockSpec(memory_space=pltpu.SMEM)],
        out_specs=pl.BlockSpec(memory_space=pltpu.VMEM),
        out_shape=jax.ShapeDtypeStruct((2, B_HEADS, CHUNK, 2 * CHUNK), F32),
    )(table)


def _bias_grad(dbias, rides=()):
    def body(db_ref, o_ref, acc_ref):
        rel = _rel_distance()
        lo = [0] + BUCKET_THR
        hi = BUCKET_THR + [CHUNK]
        for b in range(N_BUCKETS):
            m = (rel >= lo[b]) & (rel < hi[b])
            for h in range(B_HEADS):
                row = b * B_HEADS + h
                acc_ref[row:row + 1, :] = jnp.sum(jnp.where(m, db_ref[h], 0.0), axis=0, keepdims=True)
        o_ref[...] = jnp.sum(acc_ref[...], axis=1, keepdims=True)

    return _call(
        body, name="bias_grad", grid=(1,),
        in_specs=[pl.BlockSpec(dbias.shape, lambda i: (0, 0, 0))],
        out_specs=pl.BlockSpec((N_BUCKETS * B_HEADS, 1), lambda i: (0, 0)),
        out_shape=jax.ShapeDtypeStruct((N_BUCKETS * B_HEADS, 1), F32),
        scratch_shapes=[pltpu.VMEM((N_BUCKETS * B_HEADS, 2 * CHUNK), F32)],
        sem=("arbitrary",), rides=rides,
    )(dbias)


def _causal_mask():
    t = lax.broadcasted_iota(jnp.int32, (CHUNK, CHUNK), 0)
    s = lax.broadcasted_iota(jnp.int32, (CHUNK, CHUNK), 1)
    return s <= t


def _gate_forward(u, v, lg, lb, wc, bs):
    ug = _gelu(u)
    vg = _gelu(v)
    mu = jnp.mean(vg, axis=-1, keepdims=True)
    xc = vg - mu
    rstd = lax.rsqrt(jnp.mean(xc * xc, axis=-1, keepdims=True) + EPS)
    xhat = xc * rstd
    vl = (xhat * lg + lb).astype(BF16)
    mixed = _dot(wc, vl) + bs
    return ug, xhat, rstd, vl, mixed


def _softmax_scores(qk, bias, sink):
    s = qk + bias
    m = jnp.maximum(jnp.max(s, axis=-1, keepdims=True), sink)
    p = jnp.exp(s - m)
    e_sink = jnp.exp(sink - m)
    inv = 1.0 / (jnp.sum(p, axis=-1, keepdims=True) + e_sink)
    return p * inv, e_sink * inv


PAIRS = Q_PER_KV // 2


def _head(g, pr, e):
    return g * Q_PER_KV + 2 * pr + e


def _stack_pairs(ref, g, col0=0):
    w = 2 * HEAD_DIM
    return jnp.concatenate([ref[:, col0 + (g * PAIRS + pr) * w:col0 + (g * PAIRS + pr + 1) * w] for pr in range(PAIRS)],
                           axis=0)


def _low_lanes():
    return lax.broadcasted_iota(jnp.int32, (2 * CHUNK, 2 * HEAD_DIM), 1) < HEAD_DIM


def _band_operands(kv_prev, kv_cur):
    band = jnp.concatenate([kv_prev, kv_cur], axis=0)
    low = _low_lanes()
    ops = []
    for cat in (band[:, :KV_WIDTH], band[:, KV_WIDTH:]):
        rol = pltpu.roll(cat, HEAD_DIM, 1)
        ops.append([[jnp.where(low if e == 0 else ~low, cat if g == e else rol, 0.0).astype(BF16) for e in range(2)]
                    for g in range(2)])
    return ops


def _mixer_fwd(proj, lg, lb, wsp, bs_col, sinks, bias, ga, gb, rides=()):
    T = proj.shape[0]
    nb = T // CHUNK

    def body(u_ref, v_ref, q_ref, kvc_ref, kvp_ref, lg_ref, lb_ref, w_ref, bs_ref, sink_ref, bias_ref,
             ga_ref, gb_ref, mixed_ref, mixed_t_ref, ab_ref):
        causal = _causal_mask()
        ssq = jnp.zeros((CHUNK, 1), F32)
        for g in range(A_GROUPS):
            cols = slice(g * CHUNK, (g + 1) * CHUNK)
            wc = jnp.where(causal, w_ref[g], 0.0).astype(BF16)
            ug, _, _, _, mixed = _gate_forward(u_ref[:, cols], v_ref[:, cols], lg_ref[g:g + 1, :], lb_ref[g:g + 1, :],
                                               wc, bs_ref[g])
            a = ug * mixed
            ab_ref[:, cols] = a
            ssq = ssq + jnp.sum(a * a, axis=-1, keepdims=True)
        ra = lax.rsqrt(ssq * (1.0 / A_WIDTH) + EPS)
        mixed_ref[:, :A_WIDTH] = ((ab_ref[:, :A_WIDTH] * ra) * ga_ref[...]).astype(BF16)

        kops, vops = _band_operands(kvp_ref[...], kvc_ref[...])
        ssq = jnp.zeros((CHUNK, 1), F32)
        for g in range(B_HEADS // Q_PER_KV):
            qst = (_stack_pairs(q_ref, g) * SCALE).astype(BF16)
            o_st = jnp.zeros((PAIRS * CHUNK, 2 * HEAD_DIM), F32)
            for e in range(2):
                s_all = _dot_nt(qst, kops[g][e])
                ps = []
                for pr in range(PAIRS):
                    h = _head(g, pr, e)
                    p, _ = _softmax_scores(s_all[pr * CHUNK:(pr + 1) * CHUNK], bias_ref[h], sink_ref[0, h])
                    ps.append(p.astype(BF16))
                o_st = o_st + _dot(jnp.concatenate(ps, axis=0), vops[g][e])
            for pr in range(PAIRS):
                o = o_st[pr * CHUNK:(pr + 1) * CHUNK]
                c0 = A_WIDTH + (g * PAIRS + pr) * 2 * HEAD_DIM
                ab_ref[:, c0:c0 + 2 * HEAD_DIM] = o
                ssq = ssq + jnp.sum(o * o, axis=-1, keepdims=True)
        rb = lax.rsqrt(ssq * (1.0 / B_WIDTH) + EPS)
        mixed_ref[:, A_WIDTH:] = ((ab_ref[:, A_WIDTH:] * rb) * gb_ref[...]).astype(BF16)
        mixed_t_ref[...] = mixed_ref[...].T

    full = lambda *shape: pl.BlockSpec(shape, lambda n: (0,) * len(shape))
    return _call(
        body, name="mixer_fwd", grid=(nb,),
        in_specs=[pl.BlockSpec((CHUNK, A_WIDTH), lambda n: (n, 0)),
                  pl.BlockSpec((CHUNK, A_WIDTH), lambda n: (n, 1)),
                  pl.BlockSpec((CHUNK, B_WIDTH), lambda n: (n, 2)),
                  pl.BlockSpec((CHUNK, 2 * KV_WIDTH), lambda n: (n, 12)),
                  pl.BlockSpec((CHUNK, 2 * KV_WIDTH), lambda n: (jnp.maximum(n - 1, 0), 12)),
                  full(A_GROUPS, CHUNK), full(A_GROUPS, CHUNK), full(A_GROUPS, CHUNK, CHUNK), full(A_GROUPS, CHUNK, 1),
                  pl.BlockSpec(memory_space=pltpu.SMEM),
                  pl.BlockSpec((None, B_HEADS, CHUNK, 2 * CHUNK), lambda n: (jnp.minimum(n, 1), 0, 0, 0)),
                  full(1, A_WIDTH), full(1, B_WIDTH)],
        out_specs=[pl.BlockSpec((CHUNK, D_MODEL), lambda n: (n, 0)), pl.BlockSpec((D_MODEL, CHUNK), lambda n: (0, n)),
                   pl.BlockSpec((CHUNK, D_MODEL), lambda n: (n, 0))],
        out_shape=[jax.ShapeDtypeStruct((T, D_MODEL), BF16), jax.ShapeDtypeStruct((D_MODEL, T), BF16),
                   jax.ShapeDtypeStruct((T, D_MODEL), F32)],
        sem=("parallel",), rides=rides,
    )(proj, proj, proj, proj, proj, lg, lb, wsp, bs_col, sinks, bias, ga, gb)


def _gmlp_bwd(proj, ab, dmixed, ga, lg, lb, wsp, bs_col, rides=()):
    T = proj.shape[0]
    nb = T // CHUNK

    def body(u_ref, v_ref, a_ref, dna_ref, ga_ref, lg_ref, lb_ref, w_ref, bs_ref,
             dp_ref, dpt_ref, dga_ref, dw_ref, dbs_ref, dlg_ref, dlb_ref):
        @pl.when(pl.program_id(0) == 0)
        def _():
            for r in (dga_ref, dw_ref, dbs_ref, dlg_ref, dlb_ref):
                r[...] = jnp.zeros_like(r)
        causal = _causal_mask()
        a_all = a_ref[...]
        dna = dna_ref[...]
        da_all, ra = _rms_bwd(dna, a_all, ga_ref[...])
        dga_ref[...] += jnp.sum(dna * (a_all * ra), axis=0, keepdims=True)
        for g in range(A_GROUPS):
            cols = slice(g * CHUNK, (g + 1) * CHUNK)
            wc = jnp.where(causal, w_ref[g], 0.0).astype(BF16)
            lgg = lg_ref[g:g + 1, :]
            u = u_ref[:, cols]
            v = v_ref[:, cols]
            ug, xhat, rstd, vl, mixed = _gate_forward(u, v, lgg, lb_ref[g:g + 1, :], wc, bs_ref[g])
            da = da_all[:, cols]
            dug = da * mixed
            dmg = da * ug
            dmg_b = dmg.astype(BF16)
            dbs_ref[g] += jnp.sum(dmg, axis=-1, keepdims=True)
            dw_ref[g] += jnp.where(causal, _dot_nt(dmg_b, vl), 0.0)
            dvl = _dot_tn(wc, dmg_b)
            dlg_ref[g:g + 1, :] += jnp.sum(dvl * xhat, axis=0, keepdims=True)
            dlb_ref[g:g + 1, :] += jnp.sum(dvl, axis=0, keepdims=True)
            dxh = dvl * lgg
            dvg = rstd * (dxh - jnp.mean(dxh, axis=-1, keepdims=True)
                          - xhat * jnp.mean(dxh * xhat, axis=-1, keepdims=True))
            _, gu = _gelu_and_grad(u)
            _, gv = _gelu_and_grad(v)
            dp_ref[:, cols] = (dug * gu).astype(BF16)
            dp_ref[:, A_WIDTH + g * CHUNK:A_WIDTH + (g + 1) * CHUNK] = (dvg * gv).astype(BF16)
        dpt_ref[...] = dp_ref[...].T

    full = lambda *shape: pl.BlockSpec(shape, lambda n: (0,) * len(shape))
    return _call(
        body, name="gmlp_bwd", grid=(nb,),
        in_specs=[pl.BlockSpec((CHUNK, A_WIDTH), lambda n: (n, 0)),
                  pl.BlockSpec((CHUNK, A_WIDTH), lambda n: (n, 1)),
                  pl.BlockSpec((CHUNK, A_WIDTH), lambda n: (n, 0)),
                  pl.BlockSpec((CHUNK, A_WIDTH), lambda n: (n, 0)),
                  full(1, A_WIDTH), full(A_GROUPS, CHUNK), full(A_GROUPS, CHUNK), full(A_GROUPS, CHUNK, CHUNK),
                  full(A_GROUPS, CHUNK, 1)],
        out_specs=[pl.BlockSpec((CHUNK, 2 * A_WIDTH), lambda n: (n, 0)), pl.BlockSpec((2 * A_WIDTH, CHUNK), lambda n: (0, n)),
                   full(1, A_WIDTH), full(A_GROUPS, CHUNK, CHUNK), full(A_GROUPS, CHUNK, 1),
                   full(A_GROUPS, CHUNK), full(A_GROUPS, CHUNK)],
        out_shape=[jax.ShapeDtypeStruct((T, 2 * A_WIDTH), BF16), jax.ShapeDtypeStruct((2 * A_WIDTH, T), BF16),
                   jax.ShapeDtypeStruct((1, A_WIDTH), F32), jax.ShapeDtypeStruct((A_GROUPS, CHUNK, CHUNK), F32),
                   jax.ShapeDtypeStruct((A_GROUPS, CHUNK, 1), F32), jax.ShapeDtypeStruct((A_GROUPS, CHUNK), F32),
                   jax.ShapeDtypeStruct((A_GROUPS, CHUNK), F32)],
        sem=("arbitrary",), rides=rides,
    )(proj, proj, ab, dmixed, ga, lg, lb, wsp, bs_col)


def _attn_bwd(proj, ab, dmixed, gb, sinks, bias, rides=()):
    T = proj.shape[0]
    nb = T // CHUNK
    qn = lambda n: jnp.minimum(n, nb - 1)

    def body(q_ref, kvc_ref, kvp_ref, o_ref, dnb_ref, gb_ref, sink_ref, bias_ref,
             dq_ref, dkv_ref, dqt_ref, dkvt_ref, dgb_ref, dsink_ref, dbias_ref, carry_ref, sacc_ref):
        n = pl.program_id(0)

        @pl.when(n == 0)
        def _():
            carry_ref[...] = jnp.zeros_like(carry_ref)
            sacc_ref[...] = jnp.zeros_like(sacc_ref)
            dgb_ref[...] = jnp.zeros_like(dgb_ref)
            dbias_ref[...] = jnp.zeros_like(dbias_ref)

        @pl.when(n < nb)
        def _():
            o_all = o_ref[...]
            dnb = dnb_ref[...]
            do_all, rb = _rms_bwd(dnb, o_all, gb_ref[...])
            dgb_ref[...] += jnp.sum(dnb * (o_all * rb), axis=0, keepdims=True)
            kops, vops = _band_operands(kvp_ref[...], kvc_ref[...])
            low = _low_lanes()
            halves = []
            for g in range(B_HEADS // Q_PER_KV):
                qst = (_stack_pairs(q_ref, g) * SCALE).astype(BF16)
                dost = _stack_pairs(do_all, g).astype(BF16)
                dq_st = jnp.zeros((PAIRS * CHUNK, 2 * HEAD_DIM), F32)
                dk_e, dv_e = [], []
                for e in range(2):
                    s_all = _dot_nt(qst, kops[g][e])
                    dp_all = _dot_nt(dost, vops[g][e])
                    ps, dsrs = [], []
                    for pr in range(PAIRS):
                        h = _head(g, pr, e)
                        rows = slice(pr * CHUNK, (pr + 1) * CHUNK)
                        p, p_sink = _softmax_scores(s_all[rows], bias_ref[h], sink_ref[0, h])
                        dp = dp_all[rows]
                        delta = jnp.sum(p * dp, axis=-1, keepdims=True)
                        ds = p * (dp - delta)
                        sacc_ref[:, h:h + 1] += -(p_sink * delta)
                        dbias_ref[h] += ds
                        ps.append(p.astype(BF16))
                        dsrs.append(ds.astype(BF16))
                    dsr_all = jnp.concatenate(dsrs, axis=0)
                    dq_st = dq_st + _dot(dsr_all, kops[g][e])
                    dk_e.append(_dot_tn(dsr_all, qst))
                    dv_e.append(_dot_tn(jnp.concatenate(ps, axis=0), dost))
                for pr in range(PAIRS):
                    c0 = (g * PAIRS + pr) * 2 * HEAD_DIM
                    dq_ref[:, c0:c0 + 2 * HEAD_DIM] = (dq_st[pr * CHUNK:(pr + 1) * CHUNK] * SCALE).astype(BF16)
                halves.append((dk_e, dv_e))
            tiles = []
            for t in range(2):
                g0, g1 = halves[0][t], halves[1][t]
                tiles.append(jnp.where(low, g0[0] + pltpu.roll(g0[1], HEAD_DIM, 1), pltpu.roll(g1[0], HEAD_DIM, 1) + g1[1]))
            dband = jnp.concatenate(tiles, axis=1)
            dkv = (carry_ref[...] + dband[:CHUNK]).astype(BF16)
            dkv_ref[...] = dkv
            dkvt_ref[...] = dkv.T
            dqt_ref[...] = dq_ref[...].T
            carry_ref[...] = dband[CHUNK:]

        @pl.when(n == nb)
        def _():
            dkv = carry_ref[...].astype(BF16)
            dkv_ref[...] = dkv
            dkvt_ref[...] = dkv.T
            dsink_ref[...] = jnp.sum(sacc_ref[...], axis=0, keepdims=True)

    full = lambda *shape: pl.BlockSpec(shape, lambda n: (0,) * len(shape))
    return _call(
        body, name="attn_bwd", grid=(nb + 1,),
        in_specs=[pl.BlockSpec((CHUNK, B_WIDTH), lambda n: (qn(n), 2)),
                  pl.BlockSpec((CHUNK, 2 * KV_WIDTH), lambda n: (qn(n), 12)),
                  pl.BlockSpec((CHUNK, 2 * KV_WIDTH), lambda n: (jnp.maximum(qn(n) - 1, 0), 12)),
                  pl.BlockSpec((CHUNK, B_WIDTH), lambda n: (qn(n), 1)),
                  pl.BlockSpec((CHUNK, B_WIDTH), lambda n: (qn(n), 1)),
                  full(1, B_WIDTH), pl.BlockSpec(memory_space=pltpu.SMEM),
                  pl.BlockSpec((None, B_HEADS, CHUNK, 2 * CHUNK), lambda n: (jnp.minimum(n, 1), 0, 0, 0))],
        out_specs=[pl.BlockSpec((CHUNK, B_WIDTH), lambda n: (qn(n), 0)),
                   pl.BlockSpec((CHUNK, 2 * KV_WIDTH), lambda n: (jnp.maximum(n - 1, 0), 0)),
                   pl.BlockSpec((B_WIDTH, CHUNK), lambda n: (0, qn(n))),
                   pl.BlockSpec((2 * KV_WIDTH, CHUNK), lambda n: (0, jnp.maximum(n - 1, 0))),
                   full(1, B_WIDTH), full(1, B_HEADS), full(B_HEADS, CHUNK, 2 * CHUNK)],
        out_shape=[jax.ShapeDtypeStruct((T, B_WIDTH), BF16), jax.ShapeDtypeStruct((T, 2 * KV_WIDTH), BF16),
                   jax.ShapeDtypeStruct((B_WIDTH, T), BF16), jax.ShapeDtypeStruct((2 * KV_WIDTH, T), BF16),
                   jax.ShapeDtypeStruct((1, B_WIDTH), F32), jax.ShapeDtypeStruct((1, B_HEADS), F32),
                   jax.ShapeDtypeStruct((B_HEADS, CHUNK, 2 * CHUNK), F32)],
        scratch_shapes=[pltpu.VMEM((CHUNK, 2 * KV_WIDTH), F32), pltpu.VMEM((CHUNK, B_HEADS), F32)],
        sem=("arbitrary",), rides=rides,
    )(proj, proj, proj, ab, dmixed, gb, sinks, bias)


def _sq_relu_grad(acc, r):
    return acc * (2.0 * r.astype(F32))


def _chip_index():
    return (2 * lax.axis_index("x") + lax.axis_index("y")).astype(jnp.int32).reshape(1)


def _cast_into_slot(w, *, tm, name):
    _, R, C = w.shape

    def body(me_ref, w_ref, o_ref):
        del me_ref
        o_ref[...] = w_ref[...].astype(BF16)

    return pl.pallas_call(
        body, name=name,
        grid_spec=pltpu.PrefetchScalarGridSpec(
            num_scalar_prefetch=1, grid=(R // tm,),
            in_specs=[pl.BlockSpec((None, tm, C), lambda i, me: (0, i, 0))],
            out_specs=pl.BlockSpec((None, tm, C), lambda i, me: (me[0], i, 0))),
        out_shape=jax.ShapeDtypeStruct((N_CHIPS, R, C), BF16), compiler_params=_params(("parallel",)),
    )(_chip_index(), w)


def _cast_into_slots_carrying(ws, *, steps, name, rides):
    n = len(ws)

    def body(*refs):
        for w_ref, o_ref in zip(refs[:n], refs[n:]):
            o_ref[...] = w_ref[...].astype(BF16)

    me = lambda: 2 * lax.axis_index("x") + lax.axis_index("y")
    return _call(
        body, name=name, grid=(steps,),
        in_specs=[pl.BlockSpec((None, w.shape[1] // steps, w.shape[2]), lambda i: (0, i, 0)) for w in ws],
        out_specs=[pl.BlockSpec((None, w.shape[1] // steps, w.shape[2]), lambda i: (me(), i, 0)) for w in ws],
        out_shape=[jax.ShapeDtypeStruct((N_CHIPS,) + w.shape[1:], BF16) for w in ws], sem=("arbitrary",), rides=rides,
    )(*ws)


def _owner_total(gh, others, *, tm, name):
    _, hr, C = gh.shape

    def body(me_ref, g_ref, o_ref_in, out_ref):
        del me_ref
        acc = g_ref[...]
        for j in range(3):
            acc = acc + o_ref_in[j].astype(F32)
        out_ref[...] = acc

    return pl.pallas_call(
        body, name=name,
        grid_spec=pltpu.PrefetchScalarGridSpec(
            num_scalar_prefetch=1, grid=(hr // tm,),
            in_specs=[pl.BlockSpec((None, tm, C), lambda i, me: (me[0], i, 0)),
                      pl.BlockSpec((3, tm, C), lambda i, me: (0, i, 0))],
            out_specs=pl.BlockSpec((tm, C), lambda i, me: (i, 0))),
        out_shape=jax.ShapeDtypeStruct((hr, C), F32),
        compiler_params=_params(("parallel",)),
    )(_chip_index(), gh, others)


def _adamw_math(w, g, m, v):
    m = ADAM_B1 * m + (1.0 - ADAM_B1) * g
    v = ADAM_B2 * v + (1.0 - ADAM_B2) * (g * g)
    m_hat = m / (1.0 - ADAM_B1 ** ADAM_STEP)
    v_hat = v / (1.0 - ADAM_B2 ** ADAM_STEP)
    delta = -ADAM_LR * (m_hat / (jnp.sqrt(v_hat) + ADAM_EPS) + ADAM_WD * w)
    return delta, m, v


def _adamw_halves(w, own, got, m, v, *, tm, name, rides=()):
    _, R, C = w.shape
    nt = (R // 2) // tm

    n_steps, ring = 2 * nt, 3

    def body(w_hbm, own_hbm, got_hbm, m_hbm, v_hbm, g_ref, d_ref, nm_ref, nv_ref, wbuf, mbuf, vbuf, gbuf, sems):
        s = pl.program_id(0) * nt + pl.program_id(1)

        def copies(step, slot):
            rows = pl.ds(pl.multiple_of(step * tm, 8), tm)
            return [pltpu.make_async_copy(src.at[0, rows, :], buf.at[slot], sems.at[k, slot])
                    for k, (src, buf) in enumerate(((w_hbm, wbuf), (m_hbm, mbuf), (v_hbm, vbuf)))]

        def g_copy(step, slot, src):
            rows = pl.ds(pl.multiple_of((step % nt) * tm, 8), tm)
            return pltpu.make_async_copy(src.at[rows, :], gbuf.at[slot], sems.at[3, slot])

        def start(step, slot):
            for k, c in enumerate(copies(step, slot)):
                c.start(priority=k % 2)
            mine = (step // nt) == lax.axis_index("c")

            @pl.when(mine)
            def _():
                g_copy(step, slot, own_hbm).start()

            @pl.when(jnp.logical_not(mine))
            def _():
                g_copy(step, slot, got_hbm).start()

        @pl.when(s == 0)
        def _():
            for first in range(min(ring - 1, n_steps)):
                start(first, first)

        @pl.when(s + ring - 1 < n_steps)
        def _():
            start(s + ring - 1, (s + ring - 1) % ring)

        slot = s % ring
        for c in copies(s, slot):
            c.wait()
        g_copy(s, slot, own_hbm).wait()
        g = gbuf[slot]
        g_ref[...] = g
        d_ref[...], nm_ref[...], nv_ref[...] = _adamw_math(wbuf[slot], g, mbuf[slot], vbuf[slot])

    whole = pl.BlockSpec((None, tm, C), lambda h, i: (0, h * nt + i, 0))
    any_spec = pl.BlockSpec(memory_space=pl.ANY)
    return _call(
        body, name=name, grid=(2, nt), in_specs=[any_spec] * 5, out_specs=[whole] * 4,
        out_shape=[jax.ShapeDtypeStruct((1, R, C), F32)] * 4, sem=("arbitrary", "arbitrary"), rides=rides,
        scratch_shapes=[pltpu.VMEM((ring, tm, C), F32)] * 4 + [pltpu.SemaphoreType.DMA((4, ring))],
    )(w, own, got, m, v)


def _adamw_small(w, slots, m, v, *, name, rides=()):
    def body(w_ref, slots_ref, m_ref, v_ref, g_ref, d_ref, nm_ref, nv_ref):
        g = slots_ref[0]
        for d in range(1, N_DEV):
            g = g + slots_ref[d]
        g_ref[...] = g
        d_ref[...], nm_ref[...], nv_ref[...] = _adamw_math(w_ref[...], g, m_ref[...], v_ref[...])

    flat = pl.BlockSpec(w.shape, lambda i: (0, 0))
    return _call(
        body, name=name, grid=(1,), in_specs=[flat, pl.BlockSpec(slots.shape, lambda i: (0, 0, 0)), flat, flat],
        out_specs=[flat] * 4, out_shape=[jax.ShapeDtypeStruct(w.shape, F32)] * 4, sem=("arbitrary",), rides=rides,
    )(w, slots, m, v)


SMALL = ["rel_bias_table", "mix_norm_g", "gate_norm_g", "gate_norm_b", "w_spatial", "b_spatial", "attn_sinks",
         "out_norm_a_g", "out_norm_b_g", "ffn_norm_g", "final_norm_g"]
SMALL_A = ["gate_norm_g", "gate_norm_b", "w_spatial", "b_spatial", "out_norm_a_g"]
SMALL_B = ["rel_bias_table", "mix_norm_g", "attn_sinks", "out_norm_b_g", "ffn_norm_g", "final_norm_g"]
LARGE = ["w_in", "w_out", "w_up", "w_down"]
ROW_TILE = {"w_in": 208, "w_out": 256, "w_up": 256, "w_down": 256}
WEIGHTS = ["rel_bias_table", "mix_norm_g", "w_in", "gate_norm_g", "gate_norm_b", "w_spatial", "b_spatial", "attn_sinks",
           "out_norm_a_g", "out_norm_b_g", "w_out", "ffn_norm_g", "w_up", "w_down", "final_norm_g"]
PACK_UNIT = 8 * 128


def _pack(parts):
    rows = []
    for p in parts:
        flat = p.reshape(-1)
        pad = (-flat.shape[0]) % PACK_UNIT
        rows.append(jnp.pad(flat, (0, pad)).reshape(-1, 128))
    return jnp.concatenate(rows, axis=0)


def _unpack(packed, like):
    out, row = [], 0
    for p in like:
        n = math.prod(p.shape)
        nrows = (n + PACK_UNIT - 1) // PACK_UNIT * 8
        out.append(packed[row:row + nrows].reshape(-1)[:n].reshape(p.shape))
        row += nrows
    return out


def kernel(x, rel_bias_table, mix_norm_g, w_in, gate_norm_g, gate_norm_b, w_spatial, b_spatial, attn_sinks, out_norm_a_g, out_norm_b_g, w_out, ffn_norm_g, w_up, w_down, final_norm_g, loss_target, m_rel_bias_table, m_mix_norm_g, m_w_in, m_gate_norm_g, m_gate_norm_b, m_w_spatial, m_b_spatial, m_attn_sinks, m_out_norm_a_g, m_out_norm_b_g, m_w_out, m_ffn_norm_g, m_w_up, m_w_down, m_final_norm_g, v_rel_bias_table, v_mix_norm_g, v_w_in, v_gate_norm_g, v_gate_norm_b, v_w_spatial, v_b_spatial, v_attn_sinks, v_out_norm_a_g, v_out_norm_b_g, v_w_out, v_ffn_norm_g, v_w_up, v_w_down, v_final_norm_g):
    args = dict(locals())
    wts = {n: args[n] for n in WEIGHTS}
    mom = {n: args["m_" + n] for n in WEIGHTS}
    var = {n: args["v_" + n] for n in WEIGHTS}
    sp = {n: wts[n] for n in SMALL}
    x2, tgt = x[0], loss_target[0]
    T = x2.shape[0]
    tm = min(512, T)
    tl = min(1024, T)
    lg = sp["gate_norm_g"].reshape(A_GROUPS, CHUNK)
    lb = sp["gate_norm_b"].reshape(A_GROUPS, CHUNK)
    wsp = sp["w_spatial"].reshape(A_GROUPS, CHUNK, CHUNK)
    bs_col = sp["b_spatial"].reshape(A_GROUPS, CHUNK, 1)
    sinks = sp["attn_sinks"].reshape(1, B_HEADS)
    ga = sp["out_norm_a_g"].reshape(1, A_WIDTH)
    gb = sp["out_norm_b_g"].reshape(1, B_WIDTH)
    g1 = sp["mix_norm_g"].reshape(1, D_MODEL)
    g2 = sp["ffn_norm_g"].reshape(1, D_MODEL)
    gf = sp["final_norm_g"].reshape(1, D_MODEL)

    def owner_total(n, gh, others):
        return _owner_total(gh, others, tm=ROW_TILE[n], name="rs_owner_total_" + n)

    def halves_view(at, shards):
        return at.reshape(shards, 2, at.shape[0] // shards // 2, at.shape[1])

    for d in (wts, mom, var):
        d["w_in"] = jnp.swapaxes(d["w_in"], 1, 2)

    s_in = _cast_into_slot(wts["w_in"], tm=ROW_TILE["w_in"], name="cast_w_in")
    (s_out, s_up, s_down), ((g_in,),) = _cast_into_slots_carrying(
        [wts["w_out"], wts["w_up"], wts["w_down"]], steps=8, name="cast_w_rest",
        rides=[_ride_gather(s_in, chain=(0, 1, 1), chain_fracs=(0.3, 0.6))])
    win_t = g_in.reshape(PROJ_WIDTH, D_MODEL)
    bias = _bias_build(sp["rel_bias_table"])
    (n1, proj), ((g_out,), (s_up,)) = _norm_matmul_wide(
        x2, g1, win_t, tm=tm, tn=PROJ_WIDTH // 2, name="in_proj",
        rides=[_ride_gather(s_out, chain=(0, 1, 1), chain_fracs=(0.65, 0.85)), _ride_gather(s_up, s1=(0, 3, 8))])
    wo = g_out.reshape(A_WIDTH + B_WIDTH, D_MODEL)
    (mixed, mixed_t, ab), ((s_up,), (s_down,), (n1_sib,)) = _mixer_fwd(
        proj, lg, lb, wsp, bs_col, sinks, bias, ga, gb,
        rides=[_ride_gather(s_up, s2=(0, 3, 8), s1=(3, 8, 8)), _ride_gather(s_down, s1=(0, 2, 8)),
               _ride_to_sibling(n1, first=True)])
    mixed_t = halves_view(mixed_t, N_CHIPS)
    h1, ((wu,), (s_down,), (mixed_t_sib,)) = _matmul_res(
        mixed, wo, x2, tm=tl, tn=1024, tk=D_MODEL, prologue=_to_bf16, name="out_proj",
        rides=[_ride_gather(s_up, s3=(0, 3, 8), tail=(3, 8, 8), mid_frac=0.75), _ride_gather(s_down, s2=(0, 2, 8)),
               _ride_to_sibling(mixed_t, halves=True)])
    (n2t, zp, z2, z2t), ((g_down,),) = _norm_matmul_sq(
        h1, g2, wu, tm=tl, tn=1024, name="up_proj", rides=[_ride_gather(s_down, s3=(0, 2, 8), chain=(2, 8, 8), chain_fracs=(0.5, 0.8))])
    wd = g_down.reshape(D_FF, D_MODEL)
    n2t, z2t = halves_view(n2t, 1), halves_view(z2t, N_CHIPS)
    h2, ((n2t_sib,), (z2t_sib,)) = _matmul_res(
        z2, wd, h1, tm=tl, tn=1024, tk=4096, prologue=_to_bf16, name="down_proj",
        rides=[_ride_to_sibling(n2t, halves=True), _ride_to_sibling(z2t, halves=True)])

    dh2, dh2b, dgf, loss = _loss_bwd(h2, tgt, gf, tm=tm)
    dzp, ((dh2b_sib,),) = _matmul_nt(dh2b, wd, tm=tl, tn=1024, tk=D_MODEL, name="bwd_dz", extra=zp,
                                     epilogue=_sq_relu_grad, out_dtype=BF16, rides=[_ride_to_sibling(dh2b)])
    (gd, gdb), ((dzp_sib,),) = _grad_pair(z2t, z2t_sib, dh2b, dh2b_sib, cols_sharded=False, tmo=1024, tk=tl,
                                          name="grad_w_down", rides=[_ride_to_sibling(dzp)])
    (gu, gub), ((o_d,),) = _grad_pair(n2t, n2t_sib, dzp, dzp_sib, cols_sharded=True, tmo=1024, tk=tl,
                                      name="grad_w_up", rides=[_ride_scatter(gdb, None, (0, 7, 8))])
    dn2, ((o_d,), (o_u,)) = _matmul_nt(dzp, wu, tm=tl, tn=1024, tk=4096, name="bwd_dn2",
                                       rides=[_ride_scatter(gdb, o_d, (7, 8, 8)), _ride_scatter(gub, None, (0, 6, 8))])
    h_d = owner_total("w_down", gd, o_d)
    (dh1, dh1b, dg2), ((o_u,),) = _rms_bwd_res(dn2, h1, g2, dh2, tm=tm, name="ffn_norm_bwd",
                                               rides=[_ride_scatter(gub, o_u, (6, 7, 8))])
    dmixed, ((o_u,), (dh1b_sib,), (w_d,)) = _matmul_nt(
        dh1b, wo, tm=tl, tn=1024, tk=D_MODEL, name="bwd_dmixed",
        rides=[_ride_scatter(gub, o_u, (7, 8, 8)), _ride_to_sibling(dh1b), _ride_swap(h_d)])
    h_u = owner_total("w_up", gu, o_u)
    (go, gob), ((w_u,),) = _grad_pair_merged(mixed_t, mixed_t_sib, dh1b, dh1b_sib, tk=tl, name="grad_w_out",
                                             rides=[_ride_swap(h_u)])
    (duv, duv_t, dga, dwsp, dbs, dlg, dlb), ((o_o,),) = _gmlp_bwd(proj, ab, dmixed, ga, lg, lb, wsp, bs_col,
                                                                  rides=[_ride_scatter(gob)])
    h_o = owner_total("w_out", go, o_o)
    small = {"gate_norm_g": dlg, "gate_norm_b": dlb, "w_spatial": dwsp, "b_spatial": dbs, "out_norm_a_g": dga}
    hr_in = PROJ_WIDTH // N_CHIPS // 2
    (dq, dkv, dq_t, dkv_t, dgb, dsinks, dbias), ((slots_a,), (dproj_t_sib,)) = _attn_bwd(
        proj, ab, dmixed, gb, sinks, bias,
        rides=[_ride_small_to_all(_pack([small[n] for n in SMALL_A])), _ride_rows_to_sibling(duv_t, hr_in, 2, N_CHIPS)])
    dproj_t = halves_view(jnp.concatenate([duv_t, dq_t, dkv_t], axis=0), N_CHIPS)
    dtable, ((dproj_t_sib,),) = _bias_grad(
        dbias, rides=[_ride_to_sibling(dproj_t, halves=True, shards=(2, N_CHIPS), land=dproj_t_sib)])
    (gi, gib_near), ((w_o,),) = _grad_pair(
        dproj_t, dproj_t_sib, n1, n1_sib, cols_sharded=False, tmo=hr_in, tk=tl, name="grad_w_in_near", shards="near",
        rides=[_ride_swap(h_o)])
    (gi, gib_far), ((o_i,),) = _grad_pair(
        dproj_t, dproj_t_sib, n1, n1_sib, cols_sharded=False, tmo=hr_in, tk=tl, name="grad_w_in_far", shards="far",
        into=gi, rides=[_ride_scatter(gib_near, None, to=(0, 1))])
    dn1, ((o_i,),) = _matmul_parts([duv, dq, dkv], win_t, tm=tl, tn=1024, name="bwd_dn1",
                                   rides=[_ride_scatter(gib_far, o_i, to=(2,))])
    h_i = owner_total("w_in", gi, o_i)
    dx, dg1 = _rms_bwd_res(dn1, x2, g1, dh1, tm=tm, name="mix_norm_bwd", bf16_copy=False)
    small.update({"rel_bias_table": dtable.reshape(N_BUCKETS, B_HEADS), "mix_norm_g": dg1, "attn_sinks": dsinks,
                  "out_norm_b_g": dgb, "ffn_norm_g": dg2, "final_norm_g": dgf})
    out_g, out_d, out_m, out_v = {}, {}, {}, {}

    def adamw_small(names, slots, tag, rides=()):
        extra = [jnp.zeros((1, 1), F32)] if tag == "b" else []
        like = [wts[n] for n in names] + extra
        res = _adamw_small(_pack(like), slots, _pack([mom[n] for n in names] + extra),
                           _pack([var[n] for n in names] + extra), name="adamw_small_" + tag, rides=rides)
        res, carried = res if rides else (res, None)
        for store, packed in zip((out_g, out_d, out_m, out_v), res):
            for n, val in zip(names + ["loss"], _unpack(packed, like)):
                store[n] = val
        return carried

    (w_i,), (slots_b,) = adamw_small(
        SMALL_A, slots_a, "a", rides=[_ride_swap(h_i), _ride_small_to_all(_pack([small[n] for n in SMALL_B] + [loss]))])
    for n, h, s in zip(LARGE, [h_i, h_o, h_u, h_d], [w_i, w_o, w_u, w_d]):
        res = _adamw_halves(wts[n], h, s, mom[n], var[n], tm=ROW_TILE[n], name="adamw_" + n)
        if n == "w_in":
            res = [jnp.swapaxes(r, 1, 2) for r in res]
        out_g[n], out_d[n], out_m[n], out_v[n] = res
    adamw_small(SMALL_B, slots_b, "b")

    total = out_g["loss"][0, 0]
    return (total, dx[None], *[out_g[n] for n in WEIGHTS], *[out_d[n] for n in WEIGHTS],
            *[out_m[n] for n in WEIGHTS], *[out_v[n] for n in WEIGHTS])
```
